```python
import jax, jax.numpy as jnp
from jax import lax
import numpy as np

D_MODEL = 1024
BATCH = 8
SEQ = 4096
DEPTH = 1

CONV_WIDTH = D_MODEL
CONV_GROUPS = 8
CONV_KERNEL = 31
SGU_WIDTH = D_MODEL
SGU_HEADS = 8
SGU_HEAD_DIM = SGU_WIDTH // SGU_HEADS
CHUNK = 128
EPS = 1e-6

OFF_A_VAL = 0
OFF_A_GLU = OFF_A_VAL + CONV_WIDTH
OFF_A_SILU = OFF_A_GLU + CONV_WIDTH
OFF_B_U = OFF_A_SILU + CONV_WIDTH
OFF_B_V = OFF_B_U + SGU_WIDTH
OFF_B_SILU = OFF_B_V + SGU_WIDTH
OFF_G_A = OFF_B_SILU + SGU_WIDTH
OFF_G_B = OFF_G_A + D_MODEL
IN_COLS = OFF_G_B + D_MODEL

kernel_name = "hybrid_conformer_conv_gmlp_adaln"


def rmsnorm(x, g):
    x32 = x.astype(jnp.float32)
    y = x32 * lax.rsqrt(jnp.mean(x32 * x32, axis=-1, keepdims=True) + EPS)
    return y.astype(x.dtype) * g


def layernorm(x, g, b):
    x32 = x.astype(jnp.float32)
    mu = jnp.mean(x32, axis=-1, keepdims=True)
    xc = x32 - mu
    var = jnp.mean(xc * xc, axis=-1, keepdims=True)
    return (xc * lax.rsqrt(var + EPS)).astype(x.dtype) * g + b


def conformer_conv_branch(val, glu, z, conv_w, conv_b, ln_g, ln_b, w_out):
    a = val * jax.nn.sigmoid(glu)
    kern = conv_w.reshape(CONV_KERNEL, 1, CONV_WIDTH)
    y = lax.conv_general_dilated(
        a, kern, window_strides=(1,), padding=[(CONV_KERNEL - 1, 0)],
        dimension_numbers=("NWC", "WIO", "NWC"),
        feature_group_count=CONV_WIDTH) + conv_b
    y = jax.nn.silu(layernorm(y, ln_g, ln_b))
    y = y * jax.nn.silu(z)
    return y @ w_out


def sgu_branch(u, v, z, ln_g, ln_b, w_s, b_s, w_out):
    bsz, seq, _ = u.shape
    u = jax.nn.gelu(u, approximate=False)
    v = layernorm(jax.nn.gelu(v, approximate=False), ln_g, ln_b)
    vc = v.reshape(bsz, seq // CHUNK, CHUNK, SGU_HEADS, SGU_HEAD_DIM)
    causal = jnp.tril(jnp.ones((CHUNK, CHUNK), dtype=bool))
    ws = jnp.where(causal[None], w_s, jnp.zeros((), w_s.dtype))
    s = jnp.einsum("hts,bcshd->bcthd", ws, vc) + b_s.T[:, :, None]
    s = s.reshape(bsz, seq, SGU_WIDTH)
    y = u * s * jax.nn.silu(z)
    return y @ w_out


def _fwd_setup_inputs(seed: int = 0) -> dict:
    key = jax.random.key(seed)
    ks = jax.random.split(key, 20)
    f32 = jnp.float32
    n = lambda k, shape, s: (jax.random.normal(k, shape, f32) * s)
    x = jax.random.normal(ks[0], (BATCH, SEQ, D_MODEL), f32)
    c = jax.random.normal(ks[1], (BATCH, D_MODEL), f32)
    w_ada = n(ks[2], (DEPTH, D_MODEL, 3 * D_MODEL), 0.3 * D_MODEL ** -0.5)
    b_ada = n(ks[3], (DEPTH, 3 * D_MODEL), 0.02)
    g_pre = 1.0 + n(ks[4], (DEPTH, D_MODEL), 0.02)
    w_in = n(ks[5], (DEPTH, D_MODEL, IN_COLS), D_MODEL ** -0.5)
    conv_w = n(ks[6], (DEPTH, CONV_KERNEL, CONV_WIDTH), CONV_KERNEL ** -0.5)
    conv_b = n(ks[7], (DEPTH, CONV_WIDTH), 0.02)
    conv_ln_g = 1.0 + n(ks[8], (DEPTH, CONV_WIDTH), 0.02)
    conv_ln_b = n(ks[9], (DEPTH, CONV_WIDTH), 0.02)
    w_conv_out = n(ks[10], (DEPTH, CONV_WIDTH, D_MODEL), CONV_WIDTH ** -0.5)
    sgu_ln_g = 1.0 + n(ks[11], (DEPTH, SGU_WIDTH), 0.02)
    sgu_ln_b = n(ks[12], (DEPTH, SGU_WIDTH), 0.02)
    w_sgu = n(ks[13], (DEPTH, SGU_HEADS, CHUNK, CHUNK), 0.5 * CHUNK ** -0.5)
    b_sgu = 1.0 + n(ks[14], (DEPTH, SGU_HEADS, CHUNK), 0.02)
    w_sgu_out = n(ks[15], (DEPTH, SGU_WIDTH, D_MODEL), SGU_WIDTH ** -0.5)
    w_o = n(ks[16], (DEPTH, D_MODEL, D_MODEL), D_MODEL ** -0.5)
    g_final = 1.0 + n(ks[17], (D_MODEL,), 0.02)
    return {"x": x, "c": c, "w_ada": w_ada, "b_ada": b_ada, "g_pre": g_pre, "w_in": w_in,
            "conv_w": conv_w, "conv_b": conv_b, "conv_ln_g": conv_ln_g, "conv_ln_b": conv_ln_b,
            "w_conv_out": w_conv_out, "sgu_ln_g": sgu_ln_g, "sgu_ln_b": sgu_ln_b,
            "w_sgu": w_sgu, "b_sgu": b_sgu, "w_sgu_out": w_sgu_out, "w_o": w_o,
            "g_final": g_final}


def _fwd_reference(x, c, w_ada, b_ada, g_pre, w_in, conv_w, conv_b, conv_ln_g, conv_ln_b,
              w_conv_out, sgu_ln_g, sgu_ln_b, w_sgu, b_sgu, w_sgu_out, w_o, g_final):
    for l in range(DEPTH):
        mod = c @ w_ada[l] + b_ada[l]
        shift, scale, gate = jnp.split(mod, 3, axis=-1)
        h = rmsnorm(x, g_pre[l]) * (1.0 + scale[:, None, :]) + shift[:, None, :]
        p = h @ w_in[l]
        y_a = conformer_conv_branch(
            p[..., OFF_A_VAL:OFF_A_GLU], p[..., OFF_A_GLU:OFF_A_SILU], p[..., OFF_A_SILU:OFF_B_U],
            conv_w[l], conv_b[l], conv_ln_g[l], conv_ln_b[l], w_conv_out[l])
        y_b = sgu_branch(
            p[..., OFF_B_U:OFF_B_V], p[..., OFF_B_V:OFF_B_SILU], p[..., OFF_B_SILU:OFF_G_A],
            sgu_ln_g[l], sgu_ln_b[l], w_sgu[l], b_sgu[l], w_sgu_out[l])
        merged = (jax.nn.sigmoid(p[..., OFF_G_A:OFF_G_B]) * y_a
                  + jax.nn.sigmoid(p[..., OFF_G_B:IN_COLS]) * y_b)
        x = x + gate[:, None, :] * (merged @ w_o[l])
    return rmsnorm(x, g_final)


import jax as _jax
import jax.numpy as _jnp

TWIN_FORMAT = 'train_step'
FWD_PARAMS = ['x', 'c', 'w_ada', 'b_ada', 'g_pre', 'w_in', 'conv_w', 'conv_b', 'conv_ln_g', 'conv_ln_b', 'w_conv_out', 'sgu_ln_g', 'sgu_ln_b', 'w_sgu', 'b_sgu', 'w_sgu_out', 'w_o', 'g_final']
TWIN_WEIGHTS = ['w_ada', 'b_ada', 'g_pre', 'w_in', 'conv_w', 'conv_b', 'conv_ln_g', 'conv_ln_b', 'w_conv_out', 'sgu_ln_g', 'sgu_ln_b', 'w_sgu', 'b_sgu', 'w_sgu_out', 'w_o', 'g_final']
TWIN_DIFF_INPUT = 'x'
TWIN_INPUTS = ['x', 'c', 'w_ada', 'b_ada', 'g_pre', 'w_in', 'conv_w', 'conv_b', 'conv_ln_g', 'conv_ln_b', 'w_conv_out', 'sgu_ln_g', 'sgu_ln_b', 'w_sgu', 'b_sgu', 'w_sgu_out', 'w_o', 'g_final', 'loss_target', 'm_w_ada', 'm_b_ada', 'm_g_pre', 'm_w_in', 'm_conv_w', 'm_conv_b', 'm_conv_ln_g', 'm_conv_ln_b', 'm_w_conv_out', 'm_sgu_ln_g', 'm_sgu_ln_b', 'm_w_sgu', 'm_b_sgu', 'm_w_sgu_out', 'm_w_o', 'm_g_final', 'v_w_ada', 'v_b_ada', 'v_g_pre', 'v_w_in', 'v_conv_w', 'v_conv_b', 'v_conv_ln_g', 'v_conv_ln_b', 'v_w_conv_out', 'v_sgu_ln_g', 'v_sgu_ln_b', 'v_w_sgu', 'v_b_sgu', 'v_w_sgu_out', 'v_w_o', 'v_g_final']
TWIN_OUTPUTS = ['loss', 'grad_x', 'grad_w_ada', 'grad_b_ada', 'grad_g_pre', 'grad_w_in', 'grad_conv_w', 'grad_conv_b', 'grad_conv_ln_g', 'grad_conv_ln_b', 'grad_w_conv_out', 'grad_sgu_ln_g', 'grad_sgu_ln_b', 'grad_w_sgu', 'grad_b_sgu', 'grad_w_sgu_out', 'grad_w_o', 'grad_g_final', 'delta_w_ada', 'delta_b_ada', 'delta_g_pre', 'delta_w_in', 'delta_conv_w', 'delta_conv_b', 'delta_conv_ln_g', 'delta_conv_ln_b', 'delta_w_conv_out', 'delta_sgu_ln_g', 'delta_sgu_ln_b', 'delta_w_sgu', 'delta_b_sgu', 'delta_w_sgu_out', 'delta_w_o', 'delta_g_final', 'new_m_w_ada', 'new_m_b_ada', 'new_m_g_pre', 'new_m_w_in', 'new_m_conv_w', 'new_m_conv_b', 'new_m_conv_ln_g', 'new_m_conv_ln_b', 'new_m_w_conv_out', 'new_m_sgu_ln_g', 'new_m_sgu_ln_b', 'new_m_w_sgu', 'new_m_b_sgu', 'new_m_w_sgu_out', 'new_m_w_o', 'new_m_g_final', 'new_v_w_ada', 'new_v_b_ada', 'new_v_g_pre', 'new_v_w_in', 'new_v_conv_w', 'new_v_conv_b', 'new_v_conv_ln_g', 'new_v_conv_ln_b', 'new_v_w_conv_out', 'new_v_sgu_ln_g', 'new_v_sgu_ln_b', 'new_v_w_sgu', 'new_v_b_sgu', 'new_v_w_sgu_out', 'new_v_w_o', 'new_v_g_final']
TWIN_LEAF_KINDS = {'loss': 'loss', 'grad_x': 'grad_x', 'grad_w_ada': 'grad_w', 'grad_b_ada': 'grad_w', 'grad_g_pre': 'grad_w', 'grad_w_in': 'grad_w', 'grad_conv_w': 'grad_w', 'grad_conv_b': 'grad_w', 'grad_conv_ln_g': 'grad_w', 'grad_conv_ln_b': 'grad_w', 'grad_w_conv_out': 'grad_w', 'grad_sgu_ln_g': 'grad_w', 'grad_sgu_ln_b': 'grad_w', 'grad_w_sgu': 'grad_w', 'grad_b_sgu': 'grad_w', 'grad_w_sgu_out': 'grad_w', 'grad_w_o': 'grad_w', 'grad_g_final': 'grad_w', 'delta_w_ada': 'delta_w', 'delta_b_ada': 'delta_w', 'delta_g_pre': 'delta_w', 'delta_w_in': 'delta_w', 'delta_conv_w': 'delta_w', 'delta_conv_b': 'delta_w', 'delta_conv_ln_g': 'delta_w', 'delta_conv_ln_b': 'delta_w', 'delta_w_conv_out': 'delta_w', 'delta_sgu_ln_g': 'delta_w', 'delta_sgu_ln_b': 'delta_w', 'delta_w_sgu': 'delta_w', 'delta_b_sgu': 'delta_w', 'delta_w_sgu_out': 'delta_w', 'delta_w_o': 'delta_w', 'delta_g_final': 'delta_w', 'new_m_w_ada': 'new_m', 'new_m_b_ada': 'new_m', 'new_m_g_pre': 'new_m', 'new_m_w_in': 'new_m', 'new_m_conv_w': 'new_m', 'new_m_conv_b': 'new_m', 'new_m_conv_ln_g': 'new_m', 'new_m_conv_ln_b': 'new_m', 'new_m_w_conv_out': 'new_m', 'new_m_sgu_ln_g': 'new_m', 'new_m_sgu_ln_b': 'new_m', 'new_m_w_sgu': 'new_m', 'new_m_b_sgu': 'new_m', 'new_m_w_sgu_out': 'new_m', 'new_m_w_o': 'new_m', 'new_m_g_final': 'new_m', 'new_v_w_ada': 'new_v', 'new_v_b_ada': 'new_v', 'new_v_g_pre': 'new_v', 'new_v_w_in': 'new_v', 'new_v_conv_w': 'new_v', 'new_v_conv_b': 'new_v', 'new_v_conv_ln_g': 'new_v', 'new_v_conv_ln_b': 'new_v', 'new_v_w_conv_out': 'new_v', 'new_v_sgu_ln_g': 'new_v', 'new_v_sgu_ln_b': 'new_v', 'new_v_w_sgu': 'new_v', 'new_v_b_sgu': 'new_v', 'new_v_w_sgu_out': 'new_v', 'new_v_w_o': 'new_v', 'new_v_g_final': 'new_v'}


def _forward(args):
    return _fwd_reference(*[args[k] for k in FWD_PARAMS])


def _output_shape():
    out = _jax.eval_shape(lambda: _forward(_fwd_setup_inputs(0)))
    return out.shape, out.dtype

N_MICROBATCH = 1
ADAM_LR = 0.001
ADAM_B1 = 0.9
ADAM_B2 = 0.999
ADAM_EPS = 1e-08
ADAM_WD = 0.01
ADAM_STEP = 10
PER_EXAMPLE_BATCH_AXIS = {'x': 0, 'c': 0, 'loss_target': 0}
SHARED_INPUTS = []
_WEIGHT_DTYPES = {'w_ada': _jnp.float32, 'b_ada': _jnp.float32, 'g_pre': _jnp.float32, 'w_in': _jnp.float32, 'conv_w': _jnp.float32, 'conv_b': _jnp.float32, 'conv_ln_g': _jnp.float32, 'conv_ln_b': _jnp.float32, 'w_conv_out': _jnp.float32, 'sgu_ln_g': _jnp.float32, 'sgu_ln_b': _jnp.float32, 'w_sgu': _jnp.float32, 'b_sgu': _jnp.float32, 'w_sgu_out': _jnp.float32, 'w_o': _jnp.float32, 'g_final': _jnp.float32}
MOMENT_SCALE = {'w_ada': 4.543250e-02, 'b_ada': 4.393390e-02, 'g_pre': 3.006425e-02, 'w_in': 1.082090e-02, 'conv_w': 1.261840e-02, 'conv_b': 2.405605e-02, 'conv_ln_g': 1.527892e-02, 'conv_ln_b': 1.278267e-02, 'w_conv_out': 1.202432e-02, 'sgu_ln_g': 5.500701e-03, 'sgu_ln_b': 5.058242e-03, 'w_sgu': 1.028899e-02, 'b_sgu': 1.450266e-02, 'w_sgu_out': 1.529533e-02, 'w_o': 1.945753e-02, 'g_final': 3.199963e+01}


def _to_microbatches(a, axis):
    t = _jnp.moveaxis(a, axis, 0)
    t = t.reshape((N_MICROBATCH, t.shape[0] // N_MICROBATCH) + t.shape[1:])
    return _jnp.moveaxis(t, 1, axis + 1)


def setup_inputs(seed: int = 0) -> dict:
    inp = _fwd_setup_inputs(seed)
    key = _jax.random.fold_in(_jax.random.key(seed), 7919)
    shape, _ = _output_shape()
    out = dict(inp)
    out["loss_target"] = _jax.random.normal(_jax.random.fold_in(key, 0), shape, _jnp.float32)
    for i, name in enumerate(TWIN_WEIGHTS):
        w = inp[name].astype(_jnp.float32)
        if MOMENT_SCALE is None:
            s = _jnp.sqrt(_jnp.mean(_jnp.square(w)) + 1e-30)
        else:
            s = MOMENT_SCALE[name]
        km, kv = _jax.random.split(_jax.random.fold_in(key, i + 1))
        out[name] = w
        out["m_" + name] = s * _jax.random.normal(km, w.shape, _jnp.float32)
        out["v_" + name] = (s * s) * _jax.random.uniform(kv, w.shape, _jnp.float32, 0.5, 1.5)
    if N_MICROBATCH > 1:
        for name, axis in PER_EXAMPLE_BATCH_AXIS.items():
            out[name] = _to_microbatches(out[name], axis)
    return {'x': out['x'], 'c': out['c'], 'w_ada': out['w_ada'], 'b_ada': out['b_ada'], 'g_pre': out['g_pre'], 'w_in': out['w_in'], 'conv_w': out['conv_w'], 'conv_b': out['conv_b'], 'conv_ln_g': out['conv_ln_g'], 'conv_ln_b': out['conv_ln_b'], 'w_conv_out': out['w_conv_out'], 'sgu_ln_g': out['sgu_ln_g'], 'sgu_ln_b': out['sgu_ln_b'], 'w_sgu': out['w_sgu'], 'b_sgu': out['b_sgu'], 'w_sgu_out': out['w_sgu_out'], 'w_o': out['w_o'], 'g_final': out['g_final'], 'loss_target': out['loss_target'], 'm_w_ada': out['m_w_ada'], 'm_b_ada': out['m_b_ada'], 'm_g_pre': out['m_g_pre'], 'm_w_in': out['m_w_in'], 'm_conv_w': out['m_conv_w'], 'm_conv_b': out['m_conv_b'], 'm_conv_ln_g': out['m_conv_ln_g'], 'm_conv_ln_b': out['m_conv_ln_b'], 'm_w_conv_out': out['m_w_conv_out'], 'm_sgu_ln_g': out['m_sgu_ln_g'], 'm_sgu_ln_b': out['m_sgu_ln_b'], 'm_w_sgu': out['m_w_sgu'], 'm_b_sgu': out['m_b_sgu'], 'm_w_sgu_out': out['m_w_sgu_out'], 'm_w_o': out['m_w_o'], 'm_g_final': out['m_g_final'], 'v_w_ada': out['v_w_ada'], 'v_b_ada': out['v_b_ada'], 'v_g_pre': out['v_g_pre'], 'v_w_in': out['v_w_in'], 'v_conv_w': out['v_conv_w'], 'v_conv_b': out['v_conv_b'], 'v_conv_ln_g': out['v_conv_ln_g'], 'v_conv_ln_b': out['v_conv_ln_b'], 'v_w_conv_out': out['v_w_conv_out'], 'v_sgu_ln_g': out['v_sgu_ln_g'], 'v_sgu_ln_b': out['v_sgu_ln_b'], 'v_w_sgu': out['v_w_sgu'], 'v_b_sgu': out['v_b_sgu'], 'v_w_sgu_out': out['v_w_sgu_out'], 'v_w_o': out['v_w_o'], 'v_g_final': out['v_g_final']}


def _loss(weights, diff, rest, loss_target):
    with _jax.named_scope("forward"):
        args = {**rest, TWIN_DIFF_INPUT: diff, **{k: w.astype(_WEIGHT_DTYPES[k]) for k, w in weights.items()}}
        y = _forward(args)
    with _jax.named_scope("loss_head"):
        err = _jnp.square(y.astype(_jnp.float32) - loss_target)
        return 0.5 * _jnp.sum(_jnp.mean(err, axis=-1)) if err.ndim else 0.5 * err


def _adamw(w, g, m, v):
    m = ADAM_B1 * m + (1.0 - ADAM_B1) * g
    v = ADAM_B2 * v + (1.0 - ADAM_B2) * _jnp.square(g)
    m_hat = m / (1.0 - ADAM_B1 ** ADAM_STEP)
    v_hat = v / (1.0 - ADAM_B2 ** ADAM_STEP)
    delta = -ADAM_LR * (m_hat / (_jnp.sqrt(v_hat) + ADAM_EPS) + ADAM_WD * w)
    return delta, m, v


def reference(x, c, w_ada, b_ada, g_pre, w_in, conv_w, conv_b, conv_ln_g, conv_ln_b, w_conv_out, sgu_ln_g, sgu_ln_b, w_sgu, b_sgu, w_sgu_out, w_o, g_final, loss_target, m_w_ada, m_b_ada, m_g_pre, m_w_in, m_conv_w, m_conv_b, m_conv_ln_g, m_conv_ln_b, m_w_conv_out, m_sgu_ln_g, m_sgu_ln_b, m_w_sgu, m_b_sgu, m_w_sgu_out, m_w_o, m_g_final, v_w_ada, v_b_ada, v_g_pre, v_w_in, v_conv_w, v_conv_b, v_conv_ln_g, v_conv_ln_b, v_w_conv_out, v_sgu_ln_g, v_sgu_ln_b, v_w_sgu, v_b_sgu, v_w_sgu_out, v_w_o, v_g_final):
    given = dict(x=x, c=c, w_ada=w_ada, b_ada=b_ada, g_pre=g_pre, w_in=w_in, conv_w=conv_w, conv_b=conv_b, conv_ln_g=conv_ln_g, conv_ln_b=conv_ln_b, w_conv_out=w_conv_out, sgu_ln_g=sgu_ln_g, sgu_ln_b=sgu_ln_b, w_sgu=w_sgu, b_sgu=b_sgu, w_sgu_out=w_sgu_out, w_o=w_o, g_final=g_final, loss_target=loss_target, m_w_ada=m_w_ada, m_b_ada=m_b_ada, m_g_pre=m_g_pre, m_w_in=m_w_in, m_conv_w=m_conv_w, m_conv_b=m_conv_b, m_conv_ln_g=m_conv_ln_g, m_conv_ln_b=m_conv_ln_b, m_w_conv_out=m_w_conv_out, m_sgu_ln_g=m_sgu_ln_g, m_sgu_ln_b=m_sgu_ln_b, m_w_sgu=m_w_sgu, m_b_sgu=m_b_sgu, m_w_sgu_out=m_w_sgu_out, m_w_o=m_w_o, m_g_final=m_g_final, v_w_ada=v_w_ada, v_b_ada=v_b_ada, v_g_pre=v_g_pre, v_w_in=v_w_in, v_conv_w=v_conv_w, v_conv_b=v_conv_b, v_conv_ln_g=v_conv_ln_g, v_conv_ln_b=v_conv_ln_b, v_w_conv_out=v_w_conv_out, v_sgu_ln_g=v_sgu_ln_g, v_sgu_ln_b=v_sgu_ln_b, v_w_sgu=v_w_sgu, v_b_sgu=v_b_sgu, v_w_sgu_out=v_w_sgu_out, v_w_o=v_w_o, v_g_final=v_g_final)
    weights = {n: given[n] for n in TWIN_WEIGHTS}
    shared = {n: given[n] for n in SHARED_INPUTS}
    per_example = {n: given[n] for n in ['x', 'c']}
    grad_fn = _jax.value_and_grad(_loss, argnums=(0, 1))

    def one_microbatch(ex, loss_target):
        ex = dict(ex)
        diff = ex.pop(TWIN_DIFF_INPUT)
        return grad_fn(weights, diff, {**shared, **ex}, loss_target)

    if N_MICROBATCH == 1:
        loss, (grad_w, grad_x) = one_microbatch(per_example, given["loss_target"])
    else:
        def body(carry, xs):
            loss_sum, grad_sum = carry
            l_k, (gw_k, gx_k) = one_microbatch(xs[0], xs[1])
            with _jax.named_scope("update"):
                return (loss_sum + l_k, _jax.tree.map(_jnp.add, grad_sum, gw_k)), gx_k

        init = (_jnp.zeros((), _jnp.float32), _jax.tree.map(_jnp.zeros_like, weights))
        (loss, grad_w), grad_x = _jax.lax.scan(body, init, (per_example, given["loss_target"]))
    with _jax.named_scope("update"):
        delta_w, new_m, new_v = {}, {}, {}
        for n in TWIN_WEIGHTS:
            delta_w[n], new_m[n], new_v[n] = _adamw(weights[n], grad_w[n], given["m_" + n], given["v_" + n])
    return (loss, grad_x, *[grad_w[n] for n in TWIN_WEIGHTS], *[delta_w[n] for n in TWIN_WEIGHTS],
            *[new_m[n] for n in TWIN_WEIGHTS], *[new_v[n] for n in TWIN_WEIGHTS])
```

```python
import functools

import jax
import jax.numpy as jnp
from jax import lax
from jax.experimental import pallas as pl
from jax.experimental.pallas import tpu as pltpu

F32 = jnp.float32
BF16 = jnp.bfloat16
MESH = pl.DeviceIdType.MESH
VMEM = pltpu.VMEM
ANY = pl.ANY

D = 1024
NDEV = 8
NSEG = 8
HEADS = 8
HD = 128
CHUNK = 128
KW = 31
HALO = 32
EPS = 1e-6
TM = 256
TMB = 512
RB = 32
RBC = 16
EXT = TM + HALO
NSMALL = 176
ROW_CW = 16
ROW_WS = 48

ADAM_LR = 0.001
ADAM_B1 = 0.9
ADAM_B2 = 0.999
ADAM_EPS = 1e-08
ADAM_WD = 0.01
ADAM_STEP = 10

INV_SQRT2 = 0.7071067811865476
INV_SQRT_2PI = 0.3989422804014327

NT = (((1,), (1,)), ((), ()))
TN = (((0,), (0,)), ((), ()))


def _cp(sem=None, vmem_mb=48):
    return pltpu.CompilerParams(dimension_semantics=sem, vmem_limit_bytes=vmem_mb * 1024 * 1024)


def _me():
    return lax.axis_index("x"), lax.axis_index("y"), lax.axis_index("c")


def _slot(px, py, pc):
    return 4 * px + 2 * py + pc


def _xor_peer(k):
    x, y, c = _me()
    return (1 - x if k & 4 else x, 1 - y if k & 2 else y, 1 - c if k & 1 else c)


def _allgather(srcs, dst_at, send_sems, recv_sems, loc_sems):
    x, y, c = _me()
    me = (x, y, c)
    sibling = (x, y, 1 - c)
    chips = [(1 - x, y), (x, 1 - y), (1 - x, 1 - y)]
    na = len(srcs)

    def copy(a, k, block, to, src=None):
        d = dst_at(a, _slot(*block))
        return pltpu.make_async_remote_copy(
            src_ref=d if src is None else src, dst_ref=d,
            send_sem=send_sems.at[a, k], recv_sem=recv_sems.at[a, k],
            device_id=to, device_id_type=MESH)

    local = [pltpu.make_async_copy(srcs[a], dst_at(a, _slot(*me)), loc_sems.at[a]) for a in range(na)]
    for cp in local:
        cp.start()
    first = []
    for a in range(na):
        first.append(copy(a, 0, me, sibling, src=srcs[a]))
        for j, chip in enumerate(chips):
            first.append(copy(a, 1 + j, me, (*chip, c), src=srcs[a]))
    for cp in first:
        cp.start()
    passed = []
    for j, chip in enumerate(chips):
        for a in range(na):
            copy(a, 1 + j, (*chip, c), me).wait_recv()
            p = copy(a, 4 + j, (*chip, c), sibling)
            p.start()
            passed.append(p)
    for a in range(na):
        copy(a, 0, sibling, me).wait_recv()
    for j, chip in enumerate(chips):
        for a in range(na):
            copy(a, 4 + j, (*chip, 1 - c), me).wait_recv()
    for cp in first + passed:
        cp.wait_send()
    for cp in local:
        cp.wait()


def _sig(v):
    return jax.nn.sigmoid(v)


def _gelu_parts(v):
    cdf = 0.5 * (1.0 + lax.erf(v * INV_SQRT2))
    pdf = jnp.exp(-0.5 * v * v) * INV_SQRT_2PI
    return v * cdf, cdf + v * pdf


def _ln_stats(v):
    mu = jnp.mean(v, axis=-1, keepdims=True)
    xc = v - mu
    var = jnp.mean(xc * xc, axis=-1, keepdims=True)
    rs = lax.rsqrt(var + EPS)
    return xc * rs, rs


def _ln_bwd(dn, n, rs):
    return rs * (dn - jnp.mean(dn, axis=-1, keepdims=True) - n * jnp.mean(dn * n, axis=-1, keepdims=True))


def _fold8(v):
    acc = v[0:8]
    for r in range(8, v.shape[0], 8):
        acc = acc + v[r:r + 8]
    return acc


def _rows(rb, n=RB):
    return pl.ds(pl.multiple_of(rb * n, n), n)


def _shift_copies(s_ref):
    n = EXT - 8
    for m in range(1, 8):
        for c0 in range(0, n, 56):
            s_ref[m, c0:c0 + 56, :] = s_ref[0, c0 + m:c0 + m + 56, :]


def _gather_params(c, w_in, w_co, w_so, w_o, conv_w):
    def body(c_ref, win_ref, wco_ref, wso_ref, wo_ref, cw_ref, cg_ref, wing_ref, wqg_ref, cwg_ref,
             win_st, wq_st, send_sems, recv_sems, loc_sems):
        win_st[...] = win_ref[...].astype(BF16)
        wq_st[0] = wco_ref[...].astype(BF16)
        wq_st[1] = wso_ref[...].astype(BF16)
        wq_st[2] = wo_ref[...].astype(BF16)
        srcs = [c_ref, win_st, wq_st, cw_ref]

        def dst_at(a, slot):
            if a == 2:
                return wqg_ref.at[:, slot]
            return (cg_ref, wing_ref, None, cwg_ref)[a].at[slot]

        _allgather(srcs, dst_at, send_sems, recv_sems, loc_sems)

    vm = pl.BlockSpec(memory_space=VMEM)
    hbm = pl.BlockSpec(memory_space=ANY)
    return pl.pallas_call(
        body, name="gather_params",
        out_shape=(jax.ShapeDtypeStruct((NDEV, 1, D), F32), jax.ShapeDtypeStruct((NDEV, D, D), BF16),
                   jax.ShapeDtypeStruct((3, NDEV, 128, D), BF16), jax.ShapeDtypeStruct((NDEV, KW, 128), F32)),
        in_specs=[vm] * 6, out_specs=(hbm, hbm, hbm, hbm),
        scratch_shapes=[VMEM((D, D), BF16), VMEM((3, 128, D), BF16),
                        pltpu.SemaphoreType.DMA((4, 7)), pltpu.SemaphoreType.DMA((4, 7)), pltpu.SemaphoreType.DMA((4,))],
        compiler_params=_cp(vmem_mb=32),
    )(c, w_in, w_co, w_so, w_o, conv_w)


def _ada_fwd(cg, w_ada, b_ada):
    wcols = w_ada.shape[1]

    def body(cg_ref, w_ref, b_ref, mod_ref, msrc, mdst, send_sems, recv_sems):
        x, y, c = _me()
        me = _slot(x, y, c)
        m = jnp.dot(cg_ref[...], w_ref[...], preferred_element_type=F32, precision=lax.Precision.HIGHEST)
        for b in range(NDEV):
            msrc[b] = m[b:b + 1, :]
        mdst[me] = msrc[me]
        sends = []
        for k in range(1, NDEV):
            px, py, pc = _xor_peer(k)
            cp = pltpu.make_async_remote_copy(
                src_ref=msrc.at[_slot(px, py, pc)], dst_ref=mdst.at[me],
                send_sem=send_sems.at[k - 1], recv_sem=recv_sems.at[k - 1],
                device_id=(px, py, pc), device_id_type=MESH)
            cp.start()
            sends.append(cp)
        for k in range(1, NDEV):
            px, py, pc = _xor_peer(k)
            pltpu.make_async_remote_copy(
                src_ref=msrc.at[me], dst_ref=mdst.at[_slot(px, py, pc)],
                send_sem=send_sems.at[k - 1], recv_sem=recv_sems.at[k - 1],
                device_id=(px, py, pc), device_id_type=MESH).wait_recv()
        for cp in sends:
            cp.wait_send()
        full = jnp.concatenate([mdst[k] for k in range(NDEV)], axis=1) + b_ref[...]
        for r in range(3):
            mod_ref[r:r + 1, :] = full[:, r * D:(r + 1) * D]

    vm = pl.BlockSpec(memory_space=VMEM)
    return pl.pallas_call(
        body, name="ada_fwd",
        out_shape=jax.ShapeDtypeStruct((3, D), F32),
        in_specs=[vm, vm, vm], out_specs=vm,
        scratch_shapes=[VMEM((NDEV, 1, wcols), F32), VMEM((NDEV, 1, wcols), F32),
                        pltpu.SemaphoreType.DMA((7,)), pltpu.SemaphoreType.DMA((7,))],
        compiler_params=_cp(vmem_mb=32),
    )(cg, w_ada, b_ada)


def _grad_exchange(smalls, gw_in, gw_co, gw_so, gw_o):
    def body(sm_ref, gin_ref, gco_ref, gso_ref, go_ref, smg_ref, rin_ref, rq_ref,
             ag_send, ag_recv, ag_loc, x_send, x_recv, x_loc):
        x, y, c = _me()
        me = _slot(x, y, c)
        gq = (gco_ref, gso_ref, go_ref)

        def src_of(a, slot):
            if a == 0:
                return gin_ref.at[slot]
            return gq[a - 1].at[pl.ds(pl.multiple_of(slot * 128, 128), 128), :]

        def dst_of(a, slot):
            if a == 0:
                return rin_ref.at[slot]
            return rq_ref.at[slot, a - 1]

        local = [pltpu.make_async_copy(src_of(a, me), dst_of(a, me), x_loc.at[a]) for a in range(4)]
        for cp in local:
            cp.start()
        sends = []
        for k in range(1, NDEV):
            px, py, pc = _xor_peer(k)
            for a in range(4):
                cp = pltpu.make_async_remote_copy(
                    src_ref=src_of(a, _slot(px, py, pc)), dst_ref=dst_of(a, me),
                    send_sem=x_send.at[a, k - 1], recv_sem=x_recv.at[a, k - 1],
                    device_id=(px, py, pc), device_id_type=MESH)
                cp.start()
                sends.append(cp)

        _allgather([sm_ref], lambda a, slot: smg_ref.at[slot], ag_send, ag_recv, ag_loc)

        for k in range(1, NDEV):
            px, py, pc = _xor_peer(k)
            for a in range(4):
                pltpu.make_async_remote_copy(
                    src_ref=src_of(a, me), dst_ref=dst_of(a, _slot(px, py, pc)),
                    send_sem=x_send.at[a, k - 1], recv_sem=x_recv.at[a, k - 1],
                    device_id=(px, py, pc), device_id_type=MESH).wait_recv()
        for cp in sends:
            cp.wait_send()
        for cp in local:
            cp.wait()

    hbm = pl.BlockSpec(memory_space=ANY)
    return pl.pallas_call(
        body, name="grad_exchange",
        out_shape=(jax.ShapeDtypeStruct((NDEV, NSMALL, D), F32), jax.ShapeDtypeStruct((NDEV, D, D), BF16),
                   jax.ShapeDtypeStruct((NDEV, 3, 128, D), BF16)),
        in_specs=[hbm] * 5, out_specs=(hbm, hbm, hbm),
        scratch_shapes=[pltpu.SemaphoreType.DMA((1, 7)), pltpu.SemaphoreType.DMA((1, 7)), pltpu.SemaphoreType.DMA((1,)),
                        pltpu.SemaphoreType.DMA((4, 7)), pltpu.SemaphoreType.DMA((4, 7)), pltpu.SemaphoreType.DMA((4,))],
        compiler_params=_cp(vmem_mb=32),
    )(smalls, gw_in, gw_co, gw_so, gw_o)


def _prep_h(x, mod, g_pre):
    T = x.shape[0]

    def body(x_ref, mod_ref, g_ref, h_ref):
        def blk(rb, carry):
            rows = _rows(rb)
            xv = x_ref[rows, :]
            r = lax.rsqrt(jnp.mean(xv * xv, axis=-1, keepdims=True) + EPS)
            h_ref[rows, :] = ((xv * r) * g_ref[...] * (1.0 + mod_ref[1:2, :]) + mod_ref[0:1, :]).astype(BF16)
            return carry
        lax.fori_loop(0, TMB // RB, blk, 0)

    return pl.pallas_call(
        body, name="prep_h", grid=(T // TMB,),
        out_shape=jax.ShapeDtypeStruct((T, D), BF16),
        in_specs=[pl.BlockSpec((TMB, D), lambda i: (i, 0)), pl.BlockSpec((3, D), lambda i: (0, 0)),
                  pl.BlockSpec((1, D), lambda i: (0, 0))],
        out_specs=pl.BlockSpec((TMB, D), lambda i: (i, 0)),
        compiler_params=_cp(("parallel",)),
    )(x, mod, g_pre)


def _in_proj(h, wing):
    T = h.shape[0]

    def body(h_ref, w_ref, p_ref):
        p_ref[...] = jnp.dot(h_ref[...], w_ref[...], preferred_element_type=F32)

    return pl.pallas_call(
        body, name="in_proj", grid=(NSEG, T // TMB),
        out_shape=jax.ShapeDtypeStruct((T, NSEG * D), F32),
        in_specs=[pl.BlockSpec((TMB, D), lambda j, i: (i, 0)), pl.BlockSpec((None, D, D), lambda j, i: (j, 0, 0))],
        out_specs=pl.BlockSpec((TMB, D), lambda j, i: (i, j)),
        compiler_params=_cp(("parallel", "parallel")),
    )(h, wing)


def _fill_a_ext(i, s_ref, val_ref, glu_ref, hval_ref, hglu_ref):
    ah = hval_ref[...] * _sig(hglu_ref[...])
    s_ref[0, 0:HALO, :] = jnp.where(i > 0, ah, 0.0)

    def fill(rb, carry):
        rows = _rows(rb)
        s_ref[0, pl.ds(pl.multiple_of(HALO + rb * RB, RB), RB), :] = val_ref[rows, :] * _sig(glu_ref[rows, :])
        return carry
    lax.fori_loop(0, TM // RB, fill, 0)


def _halo_prev(seg):
    hb = TM // HALO
    return pl.BlockSpec((HALO, D), lambda i: (jnp.maximum(i * hb - 1, 0), seg))


def _branch_a_fwd(p, cw, conv_b, lg, lb, wq):
    T = p.shape[0]

    def body(val_ref, glu_ref, z_ref, hval_ref, hglu_ref, cw_ref, cb_ref, lg_ref, lb_ref, w_ref,
             yc_ref, ya_ref, s_ref, yp_ref):
        i = pl.program_id(0)
        _fill_a_ext(i, s_ref, val_ref, glu_ref, hval_ref, hglu_ref)
        _shift_copies(s_ref)

        def conv(rb, carry):
            r0 = rb * RBC
            acc = jnp.broadcast_to(cb_ref[...], (RBC, D))
            for k in range(KW):
                o = 2 + k
                acc = acc + cw_ref[k:k + 1, :] * s_ref[o % 8, pl.ds(pl.multiple_of(r0 + 8 * (o // 8), 8), RBC), :]
            yc_ref[pl.ds(pl.multiple_of(r0, RBC), RBC), :] = acc
            return carry
        lax.fori_loop(0, TM // RBC, conv, 0)

        def post(rb, carry):
            rows = _rows(rb)
            n, _ = _ln_stats(yc_ref[rows, :])
            l = n * lg_ref[...] + lb_ref[...]
            z = z_ref[rows, :]
            yp_ref[rows, :] = ((l * _sig(l)) * (z * _sig(z))).astype(BF16)
            return carry
        lax.fori_loop(0, TM // RB, post, 0)
        ya_ref[...] = jnp.dot(yp_ref[...], w_ref[...], preferred_element_type=F32)

    tile = lambda seg: pl.BlockSpec((TM, D), lambda i: (i, seg))
    row = pl.BlockSpec((1, D), lambda i: (0, 0))
    return pl.pallas_call(
        body, name="branch_a_fwd", grid=(T // TM,),
        out_shape=(jax.ShapeDtypeStruct((T, D), F32), jax.ShapeDtypeStruct((T, D), F32)),
        in_specs=[tile(0), tile(1), tile(2), _halo_prev(0), _halo_prev(1),
                  pl.BlockSpec((KW, D), lambda i: (0, 0)), row, row, row,
                  pl.BlockSpec((None, D, D), lambda i: (0, 0, 0))],
        out_specs=(pl.BlockSpec((TM, D), lambda i: (i, 0)), pl.BlockSpec((TM, D), lambda i: (i, 0))),
        scratch_shapes=[VMEM((8, EXT, D), F32), VMEM((TM, D), BF16)],
        compiler_params=_cp(("parallel",)),
    )(p, p, p, p, p, cw, conv_b, lg, lb, wq)


def _masked_ws(ws_ref, wt_ref):
    tri = lax.broadcasted_iota(jnp.int32, (CHUNK, CHUNK), 0) >= lax.broadcasted_iota(jnp.int32, (CHUNK, CHUNK), 1)
    for h in range(HEADS):
        wt_ref[h] = jnp.where(tri, ws_ref[h], 0.0).astype(BF16)


def _sgu_mix(wt_ref, vl_ref, bst_ref, s_ref):
    for ck in range(TM // CHUNK):
        r = slice(ck * CHUNK, (ck + 1) * CHUNK)
        for h in range(HEADS):
            cs = slice(h * HD, (h + 1) * HD)
            s_ref[r, cs] = jnp.dot(wt_ref[h], vl_ref[r, cs], preferred_element_type=F32) + bst_ref[:, h:h + 1]


def _branch_b_fwd(p, ws, bst, sg, sb, wq):
    T = p.shape[0]

    def body(pu_ref, pv_ref, pz_ref, ws_ref, bst_ref, sg_ref, sb_ref, w_ref, yb_ref,
             wt_ref, vl_ref, t_ref, s_ref, yp_ref):
        _masked_ws(ws_ref, wt_ref)

        def pre(rb, carry):
            rows = _rows(rb)
            vg, _ = _gelu_parts(pv_ref[rows, :])
            vn, _ = _ln_stats(vg)
            vl_ref[rows, :] = (vn * sg_ref[...] + sb_ref[...]).astype(BF16)
            u, _ = _gelu_parts(pu_ref[rows, :])
            z = pz_ref[rows, :]
            t_ref[rows, :] = u * (z * _sig(z))
            return carry
        lax.fori_loop(0, TM // RB, pre, 0)
        _sgu_mix(wt_ref, vl_ref, bst_ref, s_ref)

        def post(rb, carry):
            rows = _rows(rb)
            yp_ref[rows, :] = (t_ref[rows, :] * s_ref[rows, :]).astype(BF16)
            return carry
        lax.fori_loop(0, TM // RB, post, 0)
        yb_ref[...] = jnp.dot(yp_ref[...], w_ref[...], preferred_element_type=F32)

    tile = lambda seg: pl.BlockSpec((TM, D), lambda i: (i, seg))
    row = pl.BlockSpec((1, D), lambda i: (0, 0))
    return pl.pallas_call(
        body, name="branch_b_fwd", grid=(T // TM,),
        out_shape=jax.ShapeDtypeStruct((T, D), F32),
        in_specs=[tile(3), tile(4), tile(5), pl.BlockSpec((HEADS, CHUNK, CHUNK), lambda i: (0, 0, 0)),
                  pl.BlockSpec((CHUNK, HEADS), lambda i: (0, 0)), row, row,
                  pl.BlockSpec((None, D, D), lambda i: (1, 0, 0))],
        out_specs=pl.BlockSpec((TM, D), lambda i: (i, 0)),
        scratch_shapes=[VMEM((HEADS, CHUNK, CHUNK), BF16), VMEM((TM, D), BF16), VMEM((TM, D), F32),
                        VMEM((TM, D), F32), VMEM((TM, D), BF16)],
        compiler_params=_cp(("parallel",)),
    )(p, p, p, ws, bst, sg, sb, wq)


def _merge_loss(p, ya, yb, x, tgt, mod, g_final, wq):
    T = x.shape[0]
    nt = T // TM

    def body(ga_ref, gb_ref, ya_ref, yb_ref, x_ref, t_ref, mod_ref, gf_ref, w_ref,
             dx2_ref, dya_ref, dyb_ref, dp_ref, gwo_ref, st_ref,
             mrg_s, out_s, dout_s, dm_s, gw_acc):
        i = pl.program_id(0)

        @pl.when(i == 0)
        def _():
            gw_acc[...] = jnp.zeros_like(gw_acc)
            st_ref[...] = jnp.zeros_like(st_ref)

        def merge(rb, carry):
            rows = _rows(rb)
            mrg_s[rows, :] = (_sig(ga_ref[rows, :]) * ya_ref[rows, :] + _sig(gb_ref[rows, :]) * yb_ref[rows, :]).astype(BF16)
            return carry
        lax.fori_loop(0, TM // RB, merge, 0)
        out_s[...] = jnp.dot(mrg_s[...], w_ref[...], preferred_element_type=F32)

        def head(rb, carry):
            loss, gg, dg = carry
            rows = _rows(rb)
            gate = mod_ref[2:3, :]
            gf = gf_ref[...]
            out = out_s[rows, :]
            x2 = x_ref[rows, :] + gate * out
            r2 = lax.rsqrt(jnp.mean(x2 * x2, axis=-1, keepdims=True) + EPS)
            x2n = x2 * r2
            diff = x2n * gf - t_ref[rows, :]
            dy = diff * (1.0 / D)
            dx2n = dy * gf
            dx2 = r2 * (dx2n - x2n * jnp.mean(dx2n * x2n, axis=-1, keepdims=True))
            dx2_ref[rows, :] = dx2
            dout_s[rows, :] = (dx2 * gate).astype(BF16)
            return loss + _fold8(diff * diff), gg + _fold8(dy * x2n), dg + _fold8(dx2 * out)
        zero = jnp.zeros((8, D), F32)
        loss, gg, dg = lax.fori_loop(0, TM // RB, head, (zero, zero, zero))
        st_ref[0] = st_ref[0] + loss * (0.5 / D)
        st_ref[1] = st_ref[1] + gg
        st_ref[2] = st_ref[2] + dg

        dm_s[...] = lax.dot_general(dout_s[...], w_ref[...], NT, preferred_element_type=F32)
        gw_acc[...] += lax.dot_general(mrg_s[...], dout_s[...], TN, preferred_element_type=F32)

        def split(rb, carry):
            rows = _rows(rb)
            dm = dm_s[rows, :]
            sa = _sig(ga_ref[rows, :])
            sb = _sig(gb_ref[rows, :])
            dya_ref[rows, :] = (dm * sa).astype(BF16)
            dyb_ref[rows, :] = (dm * sb).astype(BF16)
            dp_ref[0, rows, :] = (dm * ya_ref[rows, :] * (sa * (1.0 - sa))).astype(BF16)
            dp_ref[1, rows, :] = (dm * yb_ref[rows, :] * (sb * (1.0 - sb))).astype(BF16)
            return carry
        lax.fori_loop(0, TM // RB, split, 0)

        @pl.when(i == nt - 1)
        def _():
            gwo_ref[...] = gw_acc[...].astype(BF16)

    tile = pl.BlockSpec((TM, D), lambda i: (i, 0))
    return pl.pallas_call(
        body, name="merge_loss", grid=(nt,),
        out_shape=(jax.ShapeDtypeStruct((T, D), F32), jax.ShapeDtypeStruct((T, D), BF16), jax.ShapeDtypeStruct((T, D), BF16),
                   jax.ShapeDtypeStruct((NSEG, T, D), BF16), jax.ShapeDtypeStruct((D, D), BF16),
                   jax.ShapeDtypeStruct((3, 8, D), F32)),
        in_specs=[pl.BlockSpec((TM, D), lambda i: (i, 6)), pl.BlockSpec((TM, D), lambda i: (i, 7)), tile, tile, tile, tile,
                  pl.BlockSpec((3, D), lambda i: (0, 0)), pl.BlockSpec((1, D), lambda i: (0, 0)),
                  pl.BlockSpec((None, D, D), lambda i: (2, 0, 0))],
        out_specs=(tile, tile, tile, pl.BlockSpec((2, TM, D), lambda i: (3, i, 0)),
                   pl.BlockSpec((D, D), lambda i: (0, 0)), pl.BlockSpec((3, 8, D), lambda i: (0, 0, 0))),
        scratch_shapes=[VMEM((TM, D), BF16), VMEM((TM, D), F32), VMEM((TM, D), BF16), VMEM((TM, D), F32), VMEM((D, D), F32)],
        compiler_params=_cp(("arbitrary",)),
    )(p, p, ya, yb, x, tgt, mod, g_final, wq)


def _branch_b_bwd(dp, dyb, p, ws, bst, sg, sb, wq):
    T = p.shape[0]
    nt = T // TM

    def body(dp_in, dyb_ref, pu_ref, pv_ref, pz_ref, ws_ref, bst_ref, sg_ref, sb_ref, w_ref,
             dp_ref, gw_ref, gws_ref, gbs_ref, st_ref,
             wt_ref, d_s, vl_s, vn_s, gpv_s, rs_s, s_s, ds_s, ds32_s, yp_s, dvl_s, gw_acc):
        del dp_in
        i = pl.program_id(0)

        @pl.when(i == 0)
        def _():
            gw_acc[...] = jnp.zeros_like(gw_acc)
            gws_ref[...] = jnp.zeros_like(gws_ref)
            gbs_ref[...] = jnp.zeros_like(gbs_ref)
            st_ref[...] = jnp.zeros_like(st_ref)
            _masked_ws(ws_ref, wt_ref)

        d_s[...] = lax.dot_general(dyb_ref[...], w_ref[...], NT, preferred_element_type=F32)

        def pre(rb, carry):
            rows = _rows(rb)
            vg, gpv = _gelu_parts(pv_ref[rows, :])
            vn, rs = _ln_stats(vg)
            vl_s[rows, :] = (vn * sg_ref[...] + sb_ref[...]).astype(BF16)
            vn_s[rows, :] = vn
            gpv_s[rows, :] = gpv
            rs_s[rows, :] = rs
            return carry
        lax.fori_loop(0, TM // RB, pre, 0)
        _sgu_mix(wt_ref, vl_s, bst_ref, s_s)

        def mid(rb, carry):
            rows = _rows(rb)
            u, gpu = _gelu_parts(pu_ref[rows, :])
            z = pz_ref[rows, :]
            sz = _sig(z)
            siluz = z * sz
            d = d_s[rows, :]
            s = s_s[rows, :]
            t = u * siluz
            yp_s[rows, :] = (t * s).astype(BF16)
            dp_ref[0, rows, :] = (d * s * siluz * gpu).astype(BF16)
            ds = d * t
            ds32_s[rows, :] = ds
            ds_s[rows, :] = ds.astype(BF16)
            dp_ref[2, rows, :] = (d * u * s * (sz * (1.0 + z * (1.0 - sz)))).astype(BF16)
            return carry
        lax.fori_loop(0, TM // RB, mid, 0)

        for ck in range(TM // CHUNK):
            r = slice(ck * CHUNK, (ck + 1) * CHUNK)
            for h in range(HEADS):
                cs = slice(h * HD, (h + 1) * HD)
                dsh = ds_s[r, cs]
                dvl_s[r, cs] = lax.dot_general(wt_ref[h], dsh, TN, preferred_element_type=F32)
                gws_ref[h] += lax.dot_general(dsh, vl_s[r, cs], NT, preferred_element_type=F32)
                gbs_ref[h] += ds32_s[r, cs]

        def post(rb, carry):
            g_sg, g_sb = carry
            rows = _rows(rb)
            dvl = dvl_s[rows, :]
            vn = vn_s[rows, :]
            dvg = _ln_bwd(dvl * sg_ref[...], vn, rs_s[rows, :])
            dp_ref[1, rows, :] = (dvg * gpv_s[rows, :]).astype(BF16)
            return g_sg + _fold8(dvl * vn), g_sb + _fold8(dvl)
        zero = jnp.zeros((8, D), F32)
        g_sg, g_sb = lax.fori_loop(0, TM // RB, post, (zero, zero))
        st_ref[0] = st_ref[0] + g_sg
        st_ref[1] = st_ref[1] + g_sb

        gw_acc[...] += lax.dot_general(yp_s[...], dyb_ref[...], TN, preferred_element_type=F32)

        @pl.when(i == nt - 1)
        def _():
            gw_ref[...] = gw_acc[...].astype(BF16)

    tile = lambda seg: pl.BlockSpec((TM, D), lambda i: (i, seg))
    row = pl.BlockSpec((1, D), lambda i: (0, 0))
    hh = pl.BlockSpec((HEADS, CHUNK, CHUNK), lambda i: (0, 0, 0))
    return pl.pallas_call(
        body, name="branch_b_bwd", grid=(nt,),
        out_shape=(jax.ShapeDtypeStruct((NSEG, T, D), BF16), jax.ShapeDtypeStruct((D, D), BF16),
                   jax.ShapeDtypeStruct((HEADS, CHUNK, CHUNK), F32), jax.ShapeDtypeStruct((HEADS, CHUNK, HD), F32),
                   jax.ShapeDtypeStruct((2, 8, D), F32)),
        in_specs=[pl.BlockSpec(memory_space=ANY), pl.BlockSpec((TM, D), lambda i: (i, 0)), tile(3), tile(4), tile(5),
                  hh, pl.BlockSpec((CHUNK, HEADS), lambda i: (0, 0)), row, row,
                  pl.BlockSpec((None, D, D), lambda i: (1, 0, 0))],
        out_specs=(pl.BlockSpec((3, TM, D), lambda i: (1, i, 0)), pl.BlockSpec((D, D), lambda i: (0, 0)), hh, hh,
                   pl.BlockSpec((2, 8, D), lambda i: (0, 0, 0))),
        scratch_shapes=[VMEM((HEADS, CHUNK, CHUNK), BF16), VMEM((TM, D), F32), VMEM((TM, D), BF16), VMEM((TM, D), F32),
                        VMEM((TM, D), F32), VMEM((TM, 1), F32), VMEM((TM, D), F32), VMEM((TM, D), BF16), VMEM((TM, D), F32),
                        VMEM((TM, D), BF16), VMEM((TM, D), F32), VMEM((D, D), F32)],
        input_output_aliases={0: 0},
        compiler_params=_cp(("arbitrary",)),
    )(dp, dyb, p, p, p, ws, bst, sg, sb, wq)


def _branch_a_bwd1(dp, dya, yc, p, lg, lb, wq):
    T = p.shape[0]
    nt = T // TM

    def body(dp_in, dya_ref, yc_ref, z_ref, lg_ref, lb_ref, w_ref, dp_ref, dyc_ref, gw_ref, st_ref,
             d_s, yp_s, gw_acc):
        del dp_in
        i = pl.program_id(0)

        @pl.when(i == 0)
        def _():
            gw_acc[...] = jnp.zeros_like(gw_acc)
            st_ref[...] = jnp.zeros_like(st_ref)

        d_s[...] = lax.dot_general(dya_ref[...], w_ref[...], NT, preferred_element_type=F32)

        def blk(rb, carry):
            g_lg, g_lb, g_cb = carry
            rows = _rows(rb)
            n, rs = _ln_stats(yc_ref[rows, :])
            l = n * lg_ref[...] + lb_ref[...]
            sgl = _sig(l)
            sl = l * sgl
            z = z_ref[rows, :]
            sz = _sig(z)
            siluz = z * sz
            d = d_s[rows, :]
            yp_s[rows, :] = (sl * siluz).astype(BF16)
            dp_ref[rows, :] = (d * sl * (sz * (1.0 + z * (1.0 - sz)))).astype(BF16)
            dl = (d * siluz) * (sgl * (1.0 + l * (1.0 - sgl)))
            dyc = _ln_bwd(dl * lg_ref[...], n, rs)
            dyc_ref[rows, :] = dyc
            return g_lg + _fold8(dl * n), g_lb + _fold8(dl), g_cb + _fold8(dyc)
        zero = jnp.zeros((8, D), F32)
        g_lg, g_lb, g_cb = lax.fori_loop(0, TM // RB, blk, (zero, zero, zero))
        st_ref[0] = st_ref[0] + g_lg
        st_ref[1] = st_ref[1] + g_lb
        st_ref[2] = st_ref[2] + g_cb

        gw_acc[...] += lax.dot_general(yp_s[...], dya_ref[...], TN, preferred_element_type=F32)

        @pl.when(i == nt - 1)
        def _():
            gw_ref[...] = gw_acc[...].astype(BF16)

    tile = pl.BlockSpec((TM, D), lambda i: (i, 0))
    row = pl.BlockSpec((1, D), lambda i: (0, 0))
    return pl.pallas_call(
        body, name="branch_a_bwd1", grid=(nt,),
        out_shape=(jax.ShapeDtypeStruct((NSEG, T, D), BF16), jax.ShapeDtypeStruct((T, D), F32),
                   jax.ShapeDtypeStruct((D, D), BF16), jax.ShapeDtypeStruct((3, 8, D), F32)),
        in_specs=[pl.BlockSpec(memory_space=ANY), tile, tile, pl.BlockSpec((TM, D), lambda i: (i, 2)), row, row,
                  pl.BlockSpec((None, D, D), lambda i: (0, 0, 0))],
        out_specs=(pl.BlockSpec((None, TM, D), lambda i: (2, i, 0)), tile, pl.BlockSpec((D, D), lambda i: (0, 0)),
                   pl.BlockSpec((3, 8, D), lambda i: (0, 0, 0))),
        scratch_shapes=[VMEM((TM, D), F32), VMEM((TM, D), BF16), VMEM((D, D), F32)],
        input_output_aliases={0: 0},
        compiler_params=_cp(("arbitrary",)),
    )(dp, dya, yc, p, lg, lb, wq)


def _branch_a_bwd2(dp, dyc, p, cw):
    T = p.shape[0]
    nt = T // TM
    hb = TM // HALO

    def body(dp_in, dyc_ref, hdyc_ref, val_ref, glu_ref, hval_ref, hglu_ref, cw_ref, dp_ref, gcw_ref,
             sa_ref, sd_ref, da_s):
        del dp_in
        i = pl.program_id(0)

        @pl.when(i == 0)
        def _():
            gcw_ref[...] = jnp.zeros_like(gcw_ref)

        _fill_a_ext(i, sa_ref, val_ref, glu_ref, hval_ref, hglu_ref)
        _shift_copies(sa_ref)
        sd_ref[0, 0:TM, :] = dyc_ref[...]
        sd_ref[0, TM:EXT, :] = jnp.where(i < nt - 1, hdyc_ref[...], 0.0)
        _shift_copies(sd_ref)

        def conv_t(rb, carry):
            r0 = rb * RBC
            acc = jnp.zeros((RBC, D), F32)
            for k in range(KW):
                o = KW - 1 - k
                acc = acc + cw_ref[k:k + 1, :] * sd_ref[o % 8, pl.ds(pl.multiple_of(r0 + 8 * (o // 8), 8), RBC), :]
            da_s[pl.ds(pl.multiple_of(r0, RBC), RBC), :] = acc
            return carry
        lax.fori_loop(0, TM // RBC, conv_t, 0)

        for k in range(KW):
            o = 2 + k

            def tap(rb, acc, o=o):
                r0 = rb * RB
                prod = dyc_ref[pl.ds(pl.multiple_of(r0, RB), RB), :] * sa_ref[o % 8, pl.ds(pl.multiple_of(r0 + 8 * (o // 8), 8), RB), :]
                return acc + _fold8(prod)
            gcw_ref[k] = gcw_ref[k] + lax.fori_loop(0, TM // RB, tap, jnp.zeros((8, D), F32))

        def glu_b(rb, carry):
            rows = _rows(rb)
            da = da_s[rows, :]
            sg = _sig(glu_ref[rows, :])
            dp_ref[0, rows, :] = (da * sg).astype(BF16)
            dp_ref[1, rows, :] = (da * val_ref[rows, :] * (sg * (1.0 - sg))).astype(BF16)
            return carry
        lax.fori_loop(0, TM // RB, glu_b, 0)

    tile = lambda seg: pl.BlockSpec((TM, D), lambda i: (i, seg))
    return pl.pallas_call(
        body, name="branch_a_bwd2", grid=(nt,),
        out_shape=(jax.ShapeDtypeStruct((NSEG, T, D), BF16), jax.ShapeDtypeStruct((32, 8, D), F32)),
        in_specs=[pl.BlockSpec(memory_space=ANY), pl.BlockSpec((TM, D), lambda i: (i, 0)),
                  pl.BlockSpec((HALO, D), lambda i: (jnp.minimum((i + 1) * hb, nt * hb - 1), 0)),
                  tile(0), tile(1), _halo_prev(0), _halo_prev(1), pl.BlockSpec((KW, D), lambda i: (0, 0))],
        out_specs=(pl.BlockSpec((2, TM, D), lambda i: (0, i, 0)), pl.BlockSpec((32, 8, D), lambda i: (0, 0, 0))),
        scratch_shapes=[VMEM((8, EXT, D), F32), VMEM((8, EXT, D), F32), VMEM((TM, D), F32)],
        input_output_aliases={0: 0},
        compiler_params=_cp(("arbitrary",)),
    )(dp, dyc, dyc, p, p, p, p, cw)


def _in_bwd_dx(dp, wing, x, dx2, mod, g_pre):
    T = x.shape[0]
    nt = T // TM

    def body(dp_ref, w_ref, x_ref, dx2_ref, mod_ref, g_ref, gx_ref, st_ref, dh_s):
        i = pl.program_id(0)

        @pl.when(i == 0)
        def _():
            st_ref[...] = jnp.zeros_like(st_ref)

        dh_s[...] = lax.dot_general(dp_ref[0], w_ref[0], NT, preferred_element_type=F32)
        for j in range(1, NSEG):
            dh_s[...] += lax.dot_general(dp_ref[j], w_ref[j], NT, preferred_element_type=F32)

        def blk(rb, carry):
            d_sh, d_sc, g_g = carry
            rows = _rows(rb)
            xv = x_ref[rows, :]
            r = lax.rsqrt(jnp.mean(xv * xv, axis=-1, keepdims=True) + EPS)
            xn = xv * r
            g = g_ref[...]
            hpre = xn * g
            dh = dh_s[rows, :]
            dhp = dh * (1.0 + mod_ref[1:2, :])
            dxn = dhp * g
            gx_ref[rows, :] = dx2_ref[rows, :] + r * (dxn - xn * jnp.mean(dxn * xn, axis=-1, keepdims=True))
            return d_sh + _fold8(dh), d_sc + _fold8(dh * hpre), g_g + _fold8(dhp * xn)
        zero = jnp.zeros((8, D), F32)
        d_sh, d_sc, g_g = lax.fori_loop(0, TM // RB, blk, (zero, zero, zero))
        st_ref[0] = st_ref[0] + d_sh
        st_ref[1] = st_ref[1] + d_sc
        st_ref[2] = st_ref[2] + g_g

    tile = pl.BlockSpec((TM, D), lambda i: (i, 0))
    return pl.pallas_call(
        body, name="in_bwd_dx", grid=(nt,),
        out_shape=(jax.ShapeDtypeStruct((T, D), F32), jax.ShapeDtypeStruct((3, 8, D), F32)),
        in_specs=[pl.BlockSpec((NSEG, TM, D), lambda i: (0, i, 0)),
                  pl.BlockSpec((NSEG, D, D), lambda i: (0, 0, 0), pipeline_mode=pl.Buffered(1)),
                  tile, tile, pl.BlockSpec((3, D), lambda i: (0, 0)), pl.BlockSpec((1, D), lambda i: (0, 0))],
        out_specs=(tile, pl.BlockSpec((3, 8, D), lambda i: (0, 0, 0))),
        scratch_shapes=[VMEM((TM, D), F32)],
        compiler_params=_cp(("arbitrary",), vmem_mb=56),
    )(dp, wing, x, dx2, mod, g_pre)


def _in_bwd_dw(h, dp):
    T = h.shape[0]
    nb = T // TMB

    def body(h_ref, dp_ref, gw_ref, acc):
        i = pl.program_id(1)

        @pl.when(i == 0)
        def _():
            acc[...] = jnp.zeros_like(acc)
        acc[...] += lax.dot_general(h_ref[...], dp_ref[...], TN, preferred_element_type=F32)

        @pl.when(i == nb - 1)
        def _():
            gw_ref[...] = acc[...].astype(BF16)

    return pl.pallas_call(
        body, name="in_bwd_dw", grid=(NSEG, nb),
        out_shape=jax.ShapeDtypeStruct((NSEG, D, D), BF16),
        in_specs=[pl.BlockSpec((TMB, D), lambda j, i: (i, 0)), pl.BlockSpec((None, TMB, D), lambda j, i: (j, i, 0))],
        out_specs=pl.BlockSpec((None, D, D), lambda j, i: (j, 0, 0)),
        scratch_shapes=[VMEM((D, D), F32)],
        compiler_params=_cp(("parallel", "arbitrary")),
    )(h, dp)


def _pack_smalls(st_x, st_a, st_b, st_m, gcw, gws, gbs):
    def body(sx_ref, sa_ref, sb_ref, sm_ref, gcw_ref, gws_ref, gbs_ref, out_ref):
        out_ref[...] = jnp.zeros_like(out_ref)
        fold = lambda v: jnp.sum(v, axis=0, keepdims=True)
        out_ref[0:1, :] = fold(sx_ref[0])
        out_ref[1:2, :] = fold(sx_ref[1])
        out_ref[2:3, :] = fold(sm_ref[2])
        out_ref[3:4, :] = fold(sx_ref[2])
        out_ref[4:5, :] = fold(sa_ref[2])
        out_ref[5:6, :] = fold(sa_ref[0])
        out_ref[6:7, :] = fold(sa_ref[1])
        out_ref[7:8, :] = fold(sb_ref[0])
        out_ref[8:9, :] = fold(sb_ref[1])
        out_ref[10:11, :] = fold(sm_ref[1])
        out_ref[11:12, :] = fold(sm_ref[0])
        for k in range(KW):
            out_ref[ROW_CW + k:ROW_CW + k + 1, :] = fold(gcw_ref[k])
        tri = lax.broadcasted_iota(jnp.int32, (CHUNK, CHUNK), 0) >= lax.broadcasted_iota(jnp.int32, (CHUNK, CHUNK), 1)
        for h in range(HEADS):
            out_ref[ROW_WS:ROW_WS + CHUNK, h * CHUNK:(h + 1) * CHUNK] = jnp.where(tri, gws_ref[h], 0.0)
            out_ref[9:10, h * CHUNK:(h + 1) * CHUNK] = fold(gbs_ref[h].T)

    vm = pl.BlockSpec(memory_space=VMEM)
    return pl.pallas_call(
        body, name="pack_smalls", out_shape=jax.ShapeDtypeStruct((NSMALL, D), F32),
        in_specs=[vm] * 7, out_specs=vm, compiler_params=_cp(vmem_mb=32),
    )(st_x, st_a, st_b, st_m, gcw, gws, gbs)


def _adam(w, g, m, v):
    m2 = ADAM_B1 * m + (1.0 - ADAM_B1) * g
    v2 = ADAM_B2 * v + (1.0 - ADAM_B2) * (g * g)
    m_hat = m2 / (1.0 - ADAM_B1 ** ADAM_STEP)
    v_hat = v2 / (1.0 - ADAM_B2 ** ADAM_STEP)
    delta = -ADAM_LR * (m_hat / (jnp.sqrt(v_hat) + ADAM_EPS) + ADAM_WD * w)
    return delta, m2, v2


def _small_finish(smg, w, m, v):
    def body(smg_ref, w_ref, m_ref, v_ref, loss_ref, g_ref, d_ref, m2_ref, v2_ref):
        g = smg_ref[0]
        for s in range(1, NDEV):
            g = g + smg_ref[s]
        g_ref[...] = g
        loss_ref[...] = jnp.sum(g[11:12, :], axis=1, keepdims=True)
        d_ref[...], m2_ref[...], v2_ref[...] = _adam(w_ref[...], g, m_ref[...], v_ref[...])

    vm = pl.BlockSpec(memory_space=VMEM)
    sd = jax.ShapeDtypeStruct((NSMALL, D), F32)
    return pl.pallas_call(
        body, name="small_finish", out_shape=(jax.ShapeDtypeStruct((1, 1), F32), sd, sd, sd, sd),
        in_specs=[vm] * 4, out_specs=(vm,) * 5, compiler_params=_cp(vmem_mb=32),
    )(smg, w, m, v)


def _ada_grad_adam(ct, dm, w, m, v):
    def body(ct_ref, dm_ref, w_ref, m_ref, v_ref, g_ref, d_ref, m2_ref, v2_ref):
        g = ct_ref[:, 0:1] * dm_ref[0:1, :]
        for b in range(1, NDEV):
            g = g + ct_ref[:, b:b + 1] * dm_ref[b:b + 1, :]
        g_ref[...] = g
        d_ref[...], m2_ref[...], v2_ref[...] = _adam(w_ref[...], g, m_ref[...], v_ref[...])

    vm = pl.BlockSpec(memory_space=VMEM)
    sd = jax.ShapeDtypeStruct(w.shape, F32)
    return pl.pallas_call(
        body, name="ada_grad_adam", out_shape=(sd, sd, sd, sd),
        in_specs=[vm] * 5, out_specs=(vm,) * 4, compiler_params=_cp(vmem_mb=32),
    )(ct, dm, w, m, v)


def _adam_f32(name, g, w, m, v):
    def body(g_ref, w_ref, m_ref, v_ref, d_ref, m2_ref, v2_ref):
        d_ref[...], m2_ref[...], v2_ref[...] = _adam(w_ref[...], g_ref[...], m_ref[...], v_ref[...])

    vm = pl.BlockSpec(memory_space=VMEM)
    sd = jax.ShapeDtypeStruct(w.shape, F32)
    return pl.pallas_call(
        body, name=name, out_shape=(sd, sd, sd), in_specs=[vm] * 4, out_specs=(vm,) * 3,
        compiler_params=_cp(vmem_mb=32),
    )(g, w, m, v)


def _adam_reduce(name, recv, w, m, v, rows, recv_spec):
    R, C = w.shape

    def body(r_ref, w_ref, m_ref, v_ref, g_ref, d_ref, m2_ref, v2_ref):
        g = r_ref[0].astype(F32)
        for s in range(1, NDEV):
            g = g + r_ref[s].astype(F32)
        g_ref[...] = g
        d_ref[...], m2_ref[...], v2_ref[...] = _adam(w_ref[...], g, m_ref[...], v_ref[...])

    tile = pl.BlockSpec((rows, C), lambda i: (i, 0))
    sd = jax.ShapeDtypeStruct((R, C), F32)
    return pl.pallas_call(
        body, name=name, grid=(R // rows,), out_shape=(sd, sd, sd, sd),
        in_specs=[recv_spec, tile, tile, tile], out_specs=(tile,) * 4,
        compiler_params=_cp(("parallel",)),
    )(recv, w, m, v)


def kernel(x, c, w_ada, b_ada, g_pre, w_in, conv_w, conv_b, conv_ln_g, conv_ln_b, w_conv_out, sgu_ln_g, sgu_ln_b, w_sgu, b_sgu, w_sgu_out, w_o, g_final, loss_target, m_w_ada, m_b_ada, m_g_pre, m_w_in, m_conv_w, m_conv_b, m_conv_ln_g, m_conv_ln_b, m_w_conv_out, m_sgu_ln_g, m_sgu_ln_b, m_w_sgu, m_b_sgu, m_w_sgu_out, m_w_o, m_g_final, v_w_ada, v_b_ada, v_g_pre, v_w_in, v_conv_w, v_conv_b, v_conv_ln_g, v_conv_ln_b, v_w_conv_out, v_sgu_ln_g, v_sgu_ln_b, v_w_sgu, v_b_sgu, v_w_sgu_out, v_w_o, v_g_final):
    T = x.shape[1]
    assert T % TMB == 0 and x.shape[2] == D
    xs = x[0]
    tgt = loss_target[0]
    my = 4 * lax.axis_index("x") + 2 * lax.axis_index("y") + lax.axis_index("c")

    cg, wing, wqg, cwg = _gather_params(c, w_in[0], w_conv_out[0], w_sgu_out[0], w_o[0], conv_w[0])
    wq = wqg.reshape(3, D, D)
    cw = jnp.transpose(cwg, (1, 0, 2)).reshape(KW, D)
    mod = _ada_fwd(cg.reshape(NDEV, D), w_ada[0], b_ada)
    gf = g_final.reshape(1, D)
    bst = b_sgu[0].T

    h = _prep_h(xs, mod, g_pre)
    p = _in_proj(h, wing)
    yc, ya = _branch_a_fwd(p, cw, conv_b, conv_ln_g, conv_ln_b, wq)
    yb = _branch_b_fwd(p, w_sgu[0], bst, sgu_ln_g, sgu_ln_b, wq)
    dx2, dya, dyb, dp, gw_o, st_m = _merge_loss(p, ya, yb, xs, tgt, mod, gf, wq)

    dp, gw_so, gws, gbs, st_b = _branch_b_bwd(dp, dyb, p, w_sgu[0], bst, sgu_ln_g, sgu_ln_b, wq)
    dp, dyc, gw_co, st_a = _branch_a_bwd1(dp, dya, yc, p, conv_ln_g, conv_ln_b, wq)
    dp, gcw = _branch_a_bwd2(dp, dyc, p, cw)
    grad_x, st_x = _in_bwd_dx(dp, wing, xs, dx2, mod, g_pre)
    gw_in = _in_bwd_dw(h, dp)

    smalls = _pack_smalls(st_x, st_a, st_b, st_m, gcw, gws, gbs)
    smg, rin, rq = _grad_exchange(smalls, gw_in, gw_co, gw_so, gw_o)

    def pack(b_ada_, g_pre_, conv_b_, lg_, lb_, sg_, sb_, b_sgu_, gfin_, w_sgu_):
        top = jnp.concatenate([b_ada_.reshape(3, D), g_pre_, conv_b_, lg_, lb_, sg_, sb_, b_sgu_.reshape(1, D),
                               gfin_.reshape(1, D), jnp.zeros((ROW_WS - 11, D), F32)], axis=0)
        return jnp.concatenate([top, jnp.transpose(w_sgu_[0], (1, 0, 2)).reshape(CHUNK, D)], axis=0)

    pw = pack(b_ada, g_pre, conv_b, conv_ln_g, conv_ln_b, sgu_ln_g, sgu_ln_b, b_sgu, g_final, w_sgu)
    pm = pack(m_b_ada, m_g_pre, m_conv_b, m_conv_ln_g, m_conv_ln_b, m_sgu_ln_g, m_sgu_ln_b, m_b_sgu, m_g_final, m_w_sgu)
    pv = pack(v_b_ada, v_g_pre, v_conv_b, v_conv_ln_g, v_conv_ln_b, v_sgu_ln_g, v_sgu_ln_b, v_b_sgu, v_g_final, v_w_sgu)
    loss11, sg_, sd_, sm_, sv_ = _small_finish(smg, pw, pm, pv)

    def unpack(a):
        return dict(
            b_ada=a[0:3].reshape(1, 3 * D), g_pre=a[3:4], conv_b=a[4:5], conv_ln_g=a[5:6], conv_ln_b=a[6:7],
            sgu_ln_g=a[7:8], sgu_ln_b=a[8:9], b_sgu=a[9:10].reshape(1, HEADS, CHUNK), g_final=a[10],
            w_sgu=jnp.transpose(a[ROW_WS:ROW_WS + CHUNK].reshape(CHUNK, HEADS, CHUNK), (1, 0, 2))[None])
    small = [unpack(a) for a in (sg_, sd_, sm_, sv_)]

    g_cw = lax.dynamic_slice_in_dim(sg_[ROW_CW:ROW_CW + KW], my * 128, 128, axis=1)
    d_cw, m_cw, v_cw = _adam_f32("adam_conv_w", g_cw, conv_w[0], m_conv_w[0], v_conv_w[0])

    wcols = w_ada.shape[2]
    dm_all = smg[:, 0:3, :].reshape(NDEV, 3 * D)
    dm_mine = lax.dynamic_slice_in_dim(dm_all, my * wcols, wcols, axis=1)
    g_ada, d_ada, m_ada, v_ada = _ada_grad_adam(cg.reshape(NDEV, D).T, dm_mine, w_ada[0], m_w_ada[0], v_w_ada[0])

    g_in, d_in, m_in, v_in = _adam_reduce("adam_w_in", rin, w_in[0], m_w_in[0], v_w_in[0], 256,
                                          pl.BlockSpec((NDEV, 256, D), lambda i: (0, i, 0)))
    big = {}
    for j, (nm, w_, m_, v_) in enumerate((("w_conv_out", w_conv_out, m_w_conv_out, v_w_conv_out),
                                          ("w_sgu_out", w_sgu_out, m_w_sgu_out, v_w_sgu_out),
                                          ("w_o", w_o, m_w_o, v_w_o))):
        big[nm] = _adam_reduce("adam_" + nm, rq, w_[0], m_[0], v_[0], 128,
                               pl.BlockSpec((NDEV, None, 128, D), lambda i, j=j: (0, j, 0, 0)))

    per = {
        "w_ada": tuple(a[None] for a in (g_ada, d_ada, m_ada, v_ada)),
        "w_in": tuple(a[None] for a in (g_in, d_in, m_in, v_in)),
        "conv_w": tuple(a[None] for a in (g_cw, d_cw, m_cw, v_cw)),
    }
    for nm in ("w_conv_out", "w_sgu_out", "w_o"):
        per[nm] = tuple(a[None] for a in big[nm])
    for nm in ("b_ada", "g_pre", "conv_b", "conv_ln_g", "conv_ln_b", "sgu_ln_g", "sgu_ln_b", "w_sgu", "b_sgu", "g_final"):
        per[nm] = tuple(s[nm] for s in small)

    order = ["w_ada", "b_ada", "g_pre", "w_in", "conv_w", "conv_b", "conv_ln_g", "conv_ln_b", "w_conv_out",
             "sgu_ln_g", "sgu_ln_b", "w_sgu", "b_sgu", "w_sgu_out", "w_o", "g_final"]
    outs = [loss11.reshape(()), grad_x[None]]
    for part in range(4):
        outs += [per[nm][part] for nm in order]
    return tuple(outs)
```

```python
import functools

import jax
import jax.numpy as jnp
from jax import lax
from jax.experimental import pallas as pl
from jax.experimental.pallas import tpu as pltpu

F32 = jnp.float32
BF16 = jnp.bfloat16
MESH = pl.DeviceIdType.MESH
VMEM = pltpu.VMEM
ANY = pl.ANY

D = 1024
NDEV = 8
NSEG = 8
HEADS = 8
HD = 128
CHUNK = 128
KW = 31
HALO = 32
EPS = 1e-6
TM = 256
TMB = 512
RB = 32
RBC = 32
TAPG = 4
EXT = TM + HALO
NSMALL = 176
ROW_CW = 16
ROW_WS = 48

ADAM_LR = 0.001
ADAM_B1 = 0.9
ADAM_B2 = 0.999
ADAM_EPS = 1e-08
ADAM_WD = 0.01
ADAM_STEP = 10

INV_SQRT2 = 0.7071067811865476
INV_SQRT_2PI = 0.3989422804014327

NT = (((1,), (1,)), ((), ()))
TN = (((0,), (0,)), ((), ()))


def _cp(sem=None, vmem_mb=48):
    return pltpu.CompilerParams(dimension_semantics=sem, vmem_limit_bytes=vmem_mb * 1024 * 1024)


def _me():
    return lax.axis_index("x"), lax.axis_index("y"), lax.axis_index("c")


def _slot(px, py, pc):
    return 4 * px + 2 * py + pc


def _xor_peer(k):
    x, y, c = _me()
    return (1 - x if k & 4 else x, 1 - y if k & 2 else y, 1 - c if k & 1 else c)


def _allgather(srcs, dst_at, send_sems, recv_sems, loc_sems):
    x, y, c = _me()
    me = (x, y, c)
    sibling = (x, y, 1 - c)
    chips = [(1 - x, y), (x, 1 - y), (1 - x, 1 - y)]
    na = len(srcs)

    def copy(a, k, block, to, src=None):
        d = dst_at(a, _slot(*block))
        return pltpu.make_async_remote_copy(
            src_ref=d if src is None else src, dst_ref=d,
            send_sem=send_sems.at[a, k], recv_sem=recv_sems.at[a, k],
            device_id=to, device_id_type=MESH)

    local = [pltpu.make_async_copy(srcs[a], dst_at(a, _slot(*me)), loc_sems.at[a]) for a in range(na)]
    for cp in local:
        cp.start()
    first = []
    for a in range(na):
        first.append(copy(a, 0, me, sibling, src=srcs[a]))
        for j, chip in enumerate(chips):
            first.append(copy(a, 1 + j, me, (*chip, c), src=srcs[a]))
    for cp in first:
        cp.start()
    passed = []
    for j, chip in enumerate(chips):
        for a in range(na):
            copy(a, 1 + j, (*chip, c), me).wait_recv()
            p = copy(a, 4 + j, (*chip, c), sibling)
            p.start()
            passed.append(p)
    for a in range(na):
        copy(a, 0, sibling, me).wait_recv()
    for j, chip in enumerate(chips):
        for a in range(na):
            copy(a, 4 + j, (*chip, 1 - c), me).wait_recv()
    for cp in first + passed:
        cp.wait_send()
    for cp in local:
        cp.wait()


def _sig(v):
    return jax.nn.sigmoid(v)


def _gelu_parts(v):
    cdf = 0.5 * (1.0 + lax.erf(v * INV_SQRT2))
    pdf = jnp.exp(-0.5 * v * v) * INV_SQRT_2PI
    return v * cdf, cdf + v * pdf


def _ln_stats(v):
    mu = jnp.mean(v, axis=-1, keepdims=True)
    xc = v - mu
    var = jnp.mean(xc * xc, axis=-1, keepdims=True)
    rs = lax.rsqrt(var + EPS)
    return xc * rs, rs


def _ln_bwd(dn, n, rs):
    return rs * (dn - jnp.mean(dn, axis=-1, keepdims=True) - n * jnp.mean(dn * n, axis=-1, keepdims=True))


def _fold8(v):
    acc = v[0:8]
    for r in range(8, v.shape[0], 8):
        acc = acc + v[r:r + 8]
    return acc


def _rows(rb, n=RB):
    return pl.ds(pl.multiple_of(rb * n, n), n)


def _shift_copies(s_ref):
    n = EXT - 8
    for m in range(1, 8):
        for c0 in range(0, n, 56):
            s_ref[m, c0:c0 + 56, :] = s_ref[0, c0 + m:c0 + m + 56, :]


def _gather_params(c, w_co, w_so, w_o, conv_w):
    def body(c_ref, wco_ref, wso_ref, wo_ref, cw_ref, cg_ref, wqg_ref, cwg_ref,
             wq_st, send_sems, recv_sems, loc_sems):
        wq_st[0] = wco_ref[...].astype(BF16)
        wq_st[1] = wso_ref[...].astype(BF16)
        wq_st[2] = wo_ref[...].astype(BF16)
        srcs = [c_ref, wq_st, cw_ref]

        def dst_at(a, slot):
            if a == 1:
                return wqg_ref.at[:, slot]
            return (cg_ref, None, cwg_ref)[a].at[slot]

        _allgather(srcs, dst_at, send_sems, recv_sems, loc_sems)

    vm = pl.BlockSpec(memory_space=VMEM)
    hbm = pl.BlockSpec(memory_space=ANY)
    return pl.pallas_call(
        body, name="gather_params",
        out_shape=(jax.ShapeDtypeStruct((NDEV, 1, D), F32), jax.ShapeDtypeStruct((3, NDEV, 128, D), BF16),
                   jax.ShapeDtypeStruct((NDEV, KW, 128), F32)),
        in_specs=[vm] * 5, out_specs=(hbm, hbm, hbm),
        scratch_shapes=[VMEM((3, 128, D), BF16),
                        pltpu.SemaphoreType.DMA((3, 7)), pltpu.SemaphoreType.DMA((3, 7)), pltpu.SemaphoreType.DMA((3,))],
        compiler_params=_cp(vmem_mb=32),
    )(c, w_co, w_so, w_o, conv_w)


def _ada_fwd(cg, w_ada, b_ada):
    wcols = w_ada.shape[1]

    def body(cg_ref, w_ref, b_ref, mod_ref, msrc, mdst, send_sems, recv_sems):
        x, y, c = _me()
        me = _slot(x, y, c)
        m = jnp.dot(cg_ref[...], w_ref[...], preferred_element_type=F32, precision=lax.Precision.HIGHEST)
        for b in range(NDEV):
            msrc[b] = m[b:b + 1, :]
        mdst[me] = msrc[me]
        sends = []
        for k in range(1, NDEV):
            px, py, pc = _xor_peer(k)
            cp = pltpu.make_async_remote_copy(
                src_ref=msrc.at[_slot(px, py, pc)], dst_ref=mdst.at[me],
                send_sem=send_sems.at[k - 1], recv_sem=recv_sems.at[k - 1],
                device_id=(px, py, pc), device_id_type=MESH)
            cp.start()
            sends.append(cp)
        for k in range(1, NDEV):
            px, py, pc = _xor_peer(k)
            pltpu.make_async_remote_copy(
                src_ref=msrc.at[me], dst_ref=mdst.at[_slot(px, py, pc)],
                send_sem=send_sems.at[k - 1], recv_sem=recv_sems.at[k - 1],
                device_id=(px, py, pc), device_id_type=MESH).wait_recv()
        for cp in sends:
            cp.wait_send()
        full = jnp.concatenate([mdst[k] for k in range(NDEV)], axis=1) + b_ref[...]
        for r in range(3):
            mod_ref[r:r + 1, :] = full[:, r * D:(r + 1) * D]

    vm = pl.BlockSpec(memory_space=VMEM)
    return pl.pallas_call(
        body, name="ada_fwd",
        out_shape=jax.ShapeDtypeStruct((3, D), F32),
        in_specs=[vm, vm, vm], out_specs=vm,
        scratch_shapes=[VMEM((NDEV, 1, wcols), F32), VMEM((NDEV, 1, wcols), F32),
                        pltpu.SemaphoreType.DMA((7,)), pltpu.SemaphoreType.DMA((7,))],
        compiler_params=_cp(vmem_mb=32),
    )(cg, w_ada, b_ada)


def _grad_exchange(smalls, gw_in):
    def body(sm_ref, gin_ref, smg_ref, rin_ref, ag_send, ag_recv, ag_loc, x_send, x_recv, x_loc):
        x, y, c = _me()
        me = _slot(x, y, c)
        local = pltpu.make_async_copy(gin_ref.at[me], rin_ref.at[me], x_loc)
        local.start()
        sends = []
        for k in range(1, NDEV):
            px, py, pc = _xor_peer(k)
            cp = pltpu.make_async_remote_copy(
                src_ref=gin_ref.at[_slot(px, py, pc)], dst_ref=rin_ref.at[me],
                send_sem=x_send.at[k - 1], recv_sem=x_recv.at[k - 1],
                device_id=(px, py, pc), device_id_type=MESH)
            cp.start()
            sends.append(cp)

        _allgather([sm_ref], lambda a, slot: smg_ref.at[slot], ag_send, ag_recv, ag_loc)

        for k in range(1, NDEV):
            px, py, pc = _xor_peer(k)
            pltpu.make_async_remote_copy(
                src_ref=gin_ref.at[me], dst_ref=rin_ref.at[_slot(px, py, pc)],
                send_sem=x_send.at[k - 1], recv_sem=x_recv.at[k - 1],
                device_id=(px, py, pc), device_id_type=MESH).wait_recv()
        for cp in sends:
            cp.wait_send()
        local.wait()

    hbm = pl.BlockSpec(memory_space=ANY)
    return pl.pallas_call(
        body, name="grad_exchange",
        out_shape=(jax.ShapeDtypeStruct((NDEV, NSMALL, D), F32), jax.ShapeDtypeStruct((NDEV, D, D), BF16)),
        in_specs=[hbm] * 2, out_specs=(hbm, hbm),
        scratch_shapes=[pltpu.SemaphoreType.DMA((1, 7)), pltpu.SemaphoreType.DMA((1, 7)), pltpu.SemaphoreType.DMA((1,)),
                        pltpu.SemaphoreType.DMA((7,)), pltpu.SemaphoreType.DMA((7,)), pltpu.SemaphoreType.DMA],
        compiler_params=_cp(vmem_mb=32),
    )(smalls, gw_in)


def _prep_h(x, mod, g_pre):
    T = x.shape[0]

    def body(x_ref, mod_ref, g_ref, h_ref):
        def blk(rb, carry):
            rows = _rows(rb)
            xv = x_ref[rows, :]
            r = lax.rsqrt(jnp.mean(xv * xv, axis=-1, keepdims=True) + EPS)
            h_ref[rows, :] = ((xv * r) * g_ref[...] * (1.0 + mod_ref[1:2, :]) + mod_ref[0:1, :]).astype(BF16)
            return carry
        lax.fori_loop(0, TMB // RB, blk, 0)

    return pl.pallas_call(
        body, name="prep_h", grid=(T // TMB,),
        out_shape=jax.ShapeDtypeStruct((T, D), BF16),
        in_specs=[pl.BlockSpec((TMB, D), lambda i: (i, 0)), pl.BlockSpec((3, D), lambda i: (0, 0)),
                  pl.BlockSpec((1, D), lambda i: (0, 0))],
        out_specs=pl.BlockSpec((TMB, D), lambda i: (i, 0)),
        compiler_params=_cp(("parallel",)),
    )(x, mod, g_pre)


AG_ORDER = (0, 1, 4, 2, 6, 5, 3, 7)


def _in_proj(h, w_in, order):
    T = h.shape[0]
    nb = T // TMB

    def body(order_ref, h_ref, w_ref, p_ref, wing_ref, wbuf, send_sems, recv_sems, out_sems):
        j = pl.program_id(0)
        i = pl.program_id(1)
        x, y, c = _me()
        me = (x, y, c)
        sibling = (x, y, 1 - c)
        chips = [(1 - x, y), (x, 1 - y), (1 - x, 1 - y)]
        blocks = [me, sibling] + [(*chip, c) for chip in chips] + [(*chip, 1 - c) for chip in chips]

        def copy(k, block, to):
            d = wbuf.at[_slot(*block)]
            return pltpu.make_async_remote_copy(src_ref=d, dst_ref=d, send_sem=send_sems.at[k], recv_sem=recv_sems.at[k],
                                                device_id=to, device_id_type=MESH)

        def writeback(jj):
            s = _slot(*blocks[jj])
            return pltpu.make_async_copy(wbuf.at[s], wing_ref.at[s], out_sems.at[jj])

        first = [copy(0, me, sibling)] + [copy(1 + q, me, (*chip, c)) for q, chip in enumerate(chips)]
        passed = [copy(4 + q, (*chip, c), sibling) for q, chip in enumerate(chips)]

        @pl.when((j == 0) & (i == 0))
        def _():
            wbuf[_slot(*me)] = w_ref[...].astype(BF16)
            for cp in first:
                cp.start()
            writeback(0).start()

        for jj in range(1, NSEG):
            @pl.when((j == jj) & (i == 0))
            def _(jj=jj):
                copy(jj - 1, blocks[jj], me).wait_recv()
                if 2 <= jj <= 4:
                    passed[jj - 2].start()
                writeback(jj).start()

        p_ref[...] = jnp.dot(h_ref[...], wbuf[order_ref[j]], preferred_element_type=F32)

        @pl.when((j == NSEG - 1) & (i == nb - 1))
        def _():
            for cp in first + passed:
                cp.wait_send()
            for jj in range(NSEG):
                writeback(jj).wait()

    return pl.pallas_call(
        body, name="in_proj",
        out_shape=(jax.ShapeDtypeStruct((T, NSEG * D), F32), jax.ShapeDtypeStruct((NSEG, D, D), BF16)),
        grid_spec=pltpu.PrefetchScalarGridSpec(
            num_scalar_prefetch=1, grid=(NSEG, nb),
            in_specs=[pl.BlockSpec((TMB, D), lambda j, i, o: (i, 0)), pl.BlockSpec(memory_space=VMEM)],
            out_specs=(pl.BlockSpec((TMB, D), lambda j, i, o: (i, o[j])), pl.BlockSpec(memory_space=ANY)),
            scratch_shapes=[VMEM((NSEG, D, D), BF16), pltpu.SemaphoreType.DMA((7,)), pltpu.SemaphoreType.DMA((7,)),
                            pltpu.SemaphoreType.DMA((NSEG,))]),
        compiler_params=_cp(("arbitrary", "arbitrary")),
    )(order, h, w_in)


def _fill_a_ext(i, s_ref, val_ref, glu_ref, hval_ref, hglu_ref):
    ah = hval_ref[...] * _sig(hglu_ref[...])
    s_ref[0, 0:HALO, :] = jnp.where(i > 0, ah, 0.0)

    def fill(rb, carry):
        rows = _rows(rb)
        s_ref[0, pl.ds(pl.multiple_of(HALO + rb * RB, RB), RB), :] = val_ref[rows, :] * _sig(glu_ref[rows, :])
        return carry
    lax.fori_loop(0, TM // RB, fill, 0)


def _halo_prev(seg):
    hb = TM // HALO
    return pl.BlockSpec((HALO, D), lambda i: (jnp.maximum(i * hb - 1, 0), seg))


def _branch_a_fwd(p, cw, conv_b, lg, lb, wq):
    T = p.shape[0]

    def body(val_ref, glu_ref, z_ref, hval_ref, hglu_ref, cw_ref, cb_ref, lg_ref, lb_ref, w_ref,
             yc_ref, ya_ref, s_ref, yp_ref):
        i = pl.program_id(0)
        _fill_a_ext(i, s_ref, val_ref, glu_ref, hval_ref, hglu_ref)
        _shift_copies(s_ref)

        def conv(rb, carry):
            r0 = rb * RBC
            accs = [jnp.broadcast_to(cb_ref[...], (8, D))] * (RBC // 8)
            for k in range(KW):
                o = 2 + k
                w = cw_ref[k]
                accs = [acc + w * s_ref[o % 8, pl.ds(pl.multiple_of(r0 + 8 * (o // 8 + g), 8), 8), :]
                        for g, acc in enumerate(accs)]
            for g, acc in enumerate(accs):
                yc_ref[pl.ds(pl.multiple_of(r0 + 8 * g, 8), 8), :] = acc
            return carry
        lax.fori_loop(0, TM // RBC, conv, 0)

        def post(rb, carry):
            rows = _rows(rb)
            n, _ = _ln_stats(yc_ref[rows, :])
            l = n * lg_ref[...] + lb_ref[...]
            z = z_ref[rows, :]
            yp_ref[rows, :] = ((l * _sig(l)) * (z * _sig(z))).astype(BF16)
            return carry
        lax.fori_loop(0, TM // RB, post, 0)
        ya_ref[...] = jnp.dot(yp_ref[...], w_ref[...], preferred_element_type=F32)

    tile = lambda seg: pl.BlockSpec((TM, D), lambda i: (i, seg))
    row = pl.BlockSpec((1, D), lambda i: (0, 0))
    return pl.pallas_call(
        body, name="branch_a_fwd", grid=(T // TM,),
        out_shape=(jax.ShapeDtypeStruct((T, D), F32), jax.ShapeDtypeStruct((T, D), F32)),
        in_specs=[tile(0), tile(1), tile(2), _halo_prev(0), _halo_prev(1),
                  pl.BlockSpec((KW, 8, D), lambda i: (0, 0, 0)), row, row, row,
                  pl.BlockSpec((None, D, D), lambda i: (0, 0, 0))],
        out_specs=(pl.BlockSpec((TM, D), lambda i: (i, 0)), pl.BlockSpec((TM, D), lambda i: (i, 0))),
        scratch_shapes=[VMEM((8, EXT, D), F32), VMEM((TM, D), BF16)],
        compiler_params=_cp(("parallel",)),
    )(p, p, p, p, p, cw, conv_b, lg, lb, wq)


def _masked_ws(ws_ref, wt_ref):
    tri = lax.broadcasted_iota(jnp.int32, (CHUNK, CHUNK), 0) >= lax.broadcasted_iota(jnp.int32, (CHUNK, CHUNK), 1)
    for h in range(HEADS):
        wt_ref[h] = jnp.where(tri, ws_ref[h], 0.0).astype(BF16)


def _sgu_mix(wt_ref, vl_ref, bst_ref, s_ref):
    for ck in range(TM // CHUNK):
        r = slice(ck * CHUNK, (ck + 1) * CHUNK)
        for h in range(HEADS):
            cs = slice(h * HD, (h + 1) * HD)
            s_ref[r, cs] = jnp.dot(wt_ref[h], vl_ref[r, cs], preferred_element_type=F32) + bst_ref[:, h:h + 1]


def _branch_b_fwd(p, ws, bst, sg, sb, wq):
    T = p.shape[0]

    def body(pu_ref, pv_ref, pz_ref, ws_ref, bst_ref, sg_ref, sb_ref, w_ref, yb_ref,
             wt_ref, vl_ref, t_ref, s_ref, yp_ref):
        _masked_ws(ws_ref, wt_ref)

        def pre(rb, carry):
            rows = _rows(rb)
            vg, _ = _gelu_parts(pv_ref[rows, :])
            vn, _ = _ln_stats(vg)
            vl_ref[rows, :] = (vn * sg_ref[...] + sb_ref[...]).astype(BF16)
            u, _ = _gelu_parts(pu_ref[rows, :])
            z = pz_ref[rows, :]
            t_ref[rows, :] = u * (z * _sig(z))
            return carry
        lax.fori_loop(0, TM // RB, pre, 0)
        _sgu_mix(wt_ref, vl_ref, bst_ref, s_ref)

        def post(rb, carry):
            rows = _rows(rb)
            yp_ref[rows, :] = (t_ref[rows, :] * s_ref[rows, :]).astype(BF16)
            return carry
        lax.fori_loop(0, TM // RB, post, 0)
        yb_ref[...] = jnp.dot(yp_ref[...], w_ref[...], preferred_element_type=F32)

    tile = lambda seg: pl.BlockSpec((TM, D), lambda i: (i, seg))
    row = pl.BlockSpec((1, D), lambda i: (0, 0))
    return pl.pallas_call(
        body, name="branch_b_fwd", grid=(T // TM,),
        out_shape=jax.ShapeDtypeStruct((T, D), F32),
        in_specs=[tile(3), tile(4), tile(5), pl.BlockSpec((HEADS, CHUNK, CHUNK), lambda i: (0, 0, 0)),
                  pl.BlockSpec((CHUNK, HEADS), lambda i: (0, 0)), row, row,
                  pl.BlockSpec((None, D, D), lambda i: (1, 0, 0))],
        out_specs=pl.BlockSpec((TM, D), lambda i: (i, 0)),
        scratch_shapes=[VMEM((HEADS, CHUNK, CHUNK), BF16), VMEM((TM, D), BF16), VMEM((TM, D), F32),
                        VMEM((TM, D), F32), VMEM((TM, D), BF16)],
        compiler_params=_cp(("parallel",)),
    )(p, p, p, ws, bst, sg, sb, wq)


def _merge_loss(p, ya, yb, x, tgt, mod, g_final, wq):
    T = x.shape[0]
    nt = T // TM

    def body(ga_ref, gb_ref, ya_ref, yb_ref, x_ref, t_ref, mod_ref, gf_ref, w_ref,
             dx2_ref, dya_ref, dyb_ref, dp_ref, gwo_ref, st_ref,
             mrg_s, out_s, dout_s, dm_s, gw_acc):
        i = pl.program_id(0)

        @pl.when(i == 0)
        def _():
            gw_acc[...] = jnp.zeros_like(gw_acc)
            st_ref[...] = jnp.zeros_like(st_ref)

        def merge(rb, carry):
            rows = _rows(rb)
            mrg_s[rows, :] = (_sig(ga_ref[rows, :]) * ya_ref[rows, :] + _sig(gb_ref[rows, :]) * yb_ref[rows, :]).astype(BF16)
            return carry
        lax.fori_loop(0, TM // RB, merge, 0)
        out_s[...] = jnp.dot(mrg_s[...], w_ref[...], preferred_element_type=F32)

        def head(rb, carry):
            loss, gg, dg = carry
            rows = _rows(rb)
            gate = mod_ref[2:3, :]
            gf = gf_ref[...]
            out = out_s[rows, :]
            x2 = x_ref[rows, :] + gate * out
            r2 = lax.rsqrt(jnp.mean(x2 * x2, axis=-1, keepdims=True) + EPS)
            x2n = x2 * r2
            diff = x2n * gf - t_ref[rows, :]
            dy = diff * (1.0 / D)
            dx2n = dy * gf
            dx2 = r2 * (dx2n - x2n * jnp.mean(dx2n * x2n, axis=-1, keepdims=True))
            dx2_ref[rows, :] = dx2
            dout_s[rows, :] = (dx2 * gate).astype(BF16)
            return loss + _fold8(diff * diff), gg + _fold8(dy * x2n), dg + _fold8(dx2 * out)
        zero = jnp.zeros((8, D), F32)
        loss, gg, dg = lax.fori_loop(0, TM // RB, head, (zero, zero, zero))
        st_ref[0] = st_ref[0] + loss * (0.5 / D)
        st_ref[1] = st_ref[1] + gg
        st_ref[2] = st_ref[2] + dg

        dm_s[...] = lax.dot_general(dout_s[...], w_ref[...], NT, preferred_element_type=F32)
        gw_acc[...] += lax.dot_general(mrg_s[...], dout_s[...], TN, preferred_element_type=F32)

        def split(rb, carry):
            rows = _rows(rb)
            dm = dm_s[rows, :]
            sa = _sig(ga_ref[rows, :])
            sb = _sig(gb_ref[rows, :])
            dya_ref[rows, :] = (dm * sa).astype(BF16)
            dyb_ref[rows, :] = (dm * sb).astype(BF16)
            dp_ref[0, rows, :] = (dm * ya_ref[rows, :] * (sa * (1.0 - sa))).astype(BF16)
            dp_ref[1, rows, :] = (dm * yb_ref[rows, :] * (sb * (1.0 - sb))).astype(BF16)
            return carry
        lax.fori_loop(0, TM // RB, split, 0)

        @pl.when(i == nt - 1)
        def _():
            gwo_ref[...] = gw_acc[...].astype(BF16)

    tile = pl.BlockSpec((TM, D), lambda i: (i, 0))
    return pl.pallas_call(
        body, name="merge_loss", grid=(nt,),
        out_shape=(jax.ShapeDtypeStruct((T, D), F32), jax.ShapeDtypeStruct((T, D), BF16), jax.ShapeDtypeStruct((T, D), BF16),
                   jax.ShapeDtypeStruct((NSEG, T, D), BF16), jax.ShapeDtypeStruct((D, D), BF16),
                   jax.ShapeDtypeStruct((3, 8, D), F32)),
        in_specs=[pl.BlockSpec((TM, D), lambda i: (i, 6)), pl.BlockSpec((TM, D), lambda i: (i, 7)), tile, tile, tile, tile,
                  pl.BlockSpec((3, D), lambda i: (0, 0)), pl.BlockSpec((1, D), lambda i: (0, 0)),
                  pl.BlockSpec((None, D, D), lambda i: (2, 0, 0))],
        out_specs=(tile, tile, tile, pl.BlockSpec((2, TM, D), lambda i: (3, i, 0)),
                   pl.BlockSpec((D, D), lambda i: (0, 0)), pl.BlockSpec((3, 8, D), lambda i: (0, 0, 0))),
        scratch_shapes=[VMEM((TM, D), BF16), VMEM((TM, D), F32), VMEM((TM, D), BF16), VMEM((TM, D), F32), VMEM((D, D), F32)],
        compiler_params=_cp(("arbitrary",)),
    )(p, p, ya, yb, x, tgt, mod, g_final, wq)


def _branch_b_bwd(dp, dyb, p, ws, bst, sg, sb, wq):
    T = p.shape[0]
    nt = T // TM

    def body(dp_in, dyb_ref, pu_ref, pv_ref, pz_ref, ws_ref, bst_ref, sg_ref, sb_ref, w_ref,
             dp_ref, gw_ref, gws_ref, gbs_ref, st_ref,
             wt_ref, d_s, vl_s, vn_s, gpv_s, rs_s, s_s, ds_s, ds32_s, yp_s, dvl_s, gw_acc):
        del dp_in
        i = pl.program_id(0)

        @pl.when(i == 0)
        def _():
            gw_acc[...] = jnp.zeros_like(gw_acc)
            gws_ref[...] = jnp.zeros_like(gws_ref)
            gbs_ref[...] = jnp.zeros_like(gbs_ref)
            st_ref[...] = jnp.zeros_like(st_ref)
            _masked_ws(ws_ref, wt_ref)

        d_s[...] = lax.dot_general(dyb_ref[...], w_ref[...], NT, preferred_element_type=F32)

        def pre(rb, carry):
            rows = _rows(rb)
            vg, gpv = _gelu_parts(pv_ref[rows, :])
            vn, rs = _ln_stats(vg)
            vl_s[rows, :] = (vn * sg_ref[...] + sb_ref[...]).astype(BF16)
            vn_s[rows, :] = vn
            gpv_s[rows, :] = gpv
            rs_s[rows, :] = rs
            return carry
        lax.fori_loop(0, TM // RB, pre, 0)
        _sgu_mix(wt_ref, vl_s, bst_ref, s_s)

        def mid(rb, carry):
            rows = _rows(rb)
            u, gpu = _gelu_parts(pu_ref[rows, :])
            z = pz_ref[rows, :]
            sz = _sig(z)
            siluz = z * sz
            d = d_s[rows, :]
            s = s_s[rows, :]
            t = u * siluz
            yp_s[rows, :] = (t * s).astype(BF16)
            dp_ref[0, rows, :] = (d * s * siluz * gpu).astype(BF16)
            ds = d * t
            ds32_s[rows, :] = ds
            ds_s[rows, :] = ds.astype(BF16)
            dp_ref[2, rows, :] = (d * u * s * (sz * (1.0 + z * (1.0 - sz)))).astype(BF16)
            return carry
        lax.fori_loop(0, TM // RB, mid, 0)

        for ck in range(TM // CHUNK):
            r = slice(ck * CHUNK, (ck + 1) * CHUNK)
            for h in range(HEADS):
                cs = slice(h * HD, (h + 1) * HD)
                dsh = ds_s[r, cs]
                dvl_s[r, cs] = lax.dot_general(wt_ref[h], dsh, TN, preferred_element_type=F32)
                gws_ref[h] += lax.dot_general(dsh, vl_s[r, cs], NT, preferred_element_type=F32)
                gbs_ref[h] += ds32_s[r, cs]

        def post(rb, carry):
            g_sg, g_sb = carry
            rows = _rows(rb)
            dvl = dvl_s[rows, :]
            vn = vn_s[rows, :]
            dvg = _ln_bwd(dvl * sg_ref[...], vn, rs_s[rows, :])
            dp_ref[1, rows, :] = (dvg * gpv_s[rows, :]).astype(BF16)
            return g_sg + _fold8(dvl * vn), g_sb + _fold8(dvl)
        zero = jnp.zeros((8, D), F32)
        g_sg, g_sb = lax.fori_loop(0, TM // RB, post, (zero, zero))
        st_ref[0] = st_ref[0] + g_sg
        st_ref[1] = st_ref[1] + g_sb

        gw_acc[...] += lax.dot_general(yp_s[...], dyb_ref[...], TN, preferred_element_type=F32)

        @pl.when(i == nt - 1)
        def _():
            gw_ref[...] = gw_acc[...].astype(BF16)

    tile = lambda seg: pl.BlockSpec((TM, D), lambda i: (i, seg))
    row = pl.BlockSpec((1, D), lambda i: (0, 0))
    hh = pl.BlockSpec((HEADS, CHUNK, CHUNK), lambda i: (0, 0, 0))
    return pl.pallas_call(
        body, name="branch_b_bwd", grid=(nt,),
        out_shape=(jax.ShapeDtypeStruct((NSEG, T, D), BF16), jax.ShapeDtypeStruct((D, D), BF16),
                   jax.ShapeDtypeStruct((HEADS, CHUNK, CHUNK), F32), jax.ShapeDtypeStruct((HEADS, CHUNK, HD), F32),
                   jax.ShapeDtypeStruct((2, 8, D), F32)),
        in_specs=[pl.BlockSpec(memory_space=ANY), pl.BlockSpec((TM, D), lambda i: (i, 0)), tile(3), tile(4), tile(5),
                  hh, pl.BlockSpec((CHUNK, HEADS), lambda i: (0, 0)), row, row,
                  pl.BlockSpec((None, D, D), lambda i: (1, 0, 0))],
        out_specs=(pl.BlockSpec((3, TM, D), lambda i: (1, i, 0)), pl.BlockSpec((D, D), lambda i: (0, 0)), hh, hh,
                   pl.BlockSpec((2, 8, D), lambda i: (0, 0, 0))),
        scratch_shapes=[VMEM((HEADS, CHUNK, CHUNK), BF16), VMEM((TM, D), F32), VMEM((TM, D), BF16), VMEM((TM, D), F32),
                        VMEM((TM, D), F32), VMEM((TM, 1), F32), VMEM((TM, D), F32), VMEM((TM, D), BF16), VMEM((TM, D), F32),
                        VMEM((TM, D), BF16), VMEM((TM, D), F32), VMEM((D, D), F32)],
        input_output_aliases={0: 0},
        compiler_params=_cp(("arbitrary",)),
    )(dp, dyb, p, p, p, ws, bst, sg, sb, wq)


def _branch_a_bwd1(dp, dya, yc, p, lg, lb, wq):
    T = p.shape[0]
    nt = T // TM

    def body(dp_in, dya_ref, yc_ref, z_ref, lg_ref, lb_ref, w_ref, dp_ref, dyc_ref, gw_ref, st_ref,
             d_s, yp_s, gw_acc):
        del dp_in
        i = pl.program_id(0)

        @pl.when(i == 0)
        def _():
            gw_acc[...] = jnp.zeros_like(gw_acc)
            st_ref[...] = jnp.zeros_like(st_ref)

        d_s[...] = lax.dot_general(dya_ref[...], w_ref[...], NT, preferred_element_type=F32)

        def blk(rb, carry):
            g_lg, g_lb, g_cb = carry
            rows = _rows(rb)
            n, rs = _ln_stats(yc_ref[rows, :])
            l = n * lg_ref[...] + lb_ref[...]
            sgl = _sig(l)
            sl = l * sgl
            z = z_ref[rows, :]
            sz = _sig(z)
            siluz = z * sz
            d = d_s[rows, :]
            yp_s[rows, :] = (sl * siluz).astype(BF16)
            dp_ref[rows, :] = (d * sl * (sz * (1.0 + z * (1.0 - sz)))).astype(BF16)
            dl = (d * siluz) * (sgl * (1.0 + l * (1.0 - sgl)))
            dyc = _ln_bwd(dl * lg_ref[...], n, rs)
            dyc_ref[rows, :] = dyc
            return g_lg + _fold8(dl * n), g_lb + _fold8(dl), g_cb + _fold8(dyc)
        zero = jnp.zeros((8, D), F32)
        g_lg, g_lb, g_cb = lax.fori_loop(0, TM // RB, blk, (zero, zero, zero))
        st_ref[0] = st_ref[0] + g_lg
        st_ref[1] = st_ref[1] + g_lb
        st_ref[2] = st_ref[2] + g_cb

        gw_acc[...] += lax.dot_general(yp_s[...], dya_ref[...], TN, preferred_element_type=F32)

        @pl.when(i == nt - 1)
        def _():
            gw_ref[...] = gw_acc[...].astype(BF16)

    tile = pl.BlockSpec((TM, D), lambda i: (i, 0))
    row = pl.BlockSpec((1, D), lambda i: (0, 0))
    return pl.pallas_call(
        body, name="branch_a_bwd1", grid=(nt,),
        out_shape=(jax.ShapeDtypeStruct((NSEG, T, D), BF16), jax.ShapeDtypeStruct((T, D), F32),
                   jax.ShapeDtypeStruct((D, D), BF16), jax.ShapeDtypeStruct((3, 8, D), F32)),
        in_specs=[pl.BlockSpec(memory_space=ANY), tile, tile, pl.BlockSpec((TM, D), lambda i: (i, 2)), row, row,
                  pl.BlockSpec((None, D, D), lambda i: (0, 0, 0))],
        out_specs=(pl.BlockSpec((None, TM, D), lambda i: (2, i, 0)), tile, pl.BlockSpec((D, D), lambda i: (0, 0)),
                   pl.BlockSpec((3, 8, D), lambda i: (0, 0, 0))),
        scratch_shapes=[VMEM((TM, D), F32), VMEM((TM, D), BF16), VMEM((D, D), F32)],
        input_output_aliases={0: 0},
        compiler_params=_cp(("arbitrary",)),
    )(dp, dya, yc, p, lg, lb, wq)


def _branch_a_bwd2(dp, dyc, p, cw):
    T = p.shape[0]
    nt = T // TM
    hb = TM // HALO

    def body(dp_in, dyc_ref, hdyc_ref, val_ref, glu_ref, hval_ref, hglu_ref, cw_ref, dp_ref, gcw_ref,
             sa_ref, sd_ref, da_s):
        del dp_in
        i = pl.program_id(0)

        @pl.when(i == 0)
        def _():
            gcw_ref[...] = jnp.zeros_like(gcw_ref)

        _fill_a_ext(i, sa_ref, val_ref, glu_ref, hval_ref, hglu_ref)
        _shift_copies(sa_ref)
        sd_ref[0, 0:TM, :] = dyc_ref[...]
        sd_ref[0, TM:EXT, :] = jnp.where(i < nt - 1, hdyc_ref[...], 0.0)
        _shift_copies(sd_ref)

        def conv_t(rb, carry):
            r0 = rb * RBC
            accs = [jnp.zeros((8, D), F32)] * (RBC // 8)
            for k in range(KW):
                o = KW - 1 - k
                w = cw_ref[k]
                accs = [acc + w * sd_ref[o % 8, pl.ds(pl.multiple_of(r0 + 8 * (o // 8 + g), 8), 8), :]
                        for g, acc in enumerate(accs)]
            for g, acc in enumerate(accs):
                da_s[pl.ds(pl.multiple_of(r0 + 8 * g, 8), 8), :] = acc
            return carry
        lax.fori_loop(0, TM // RBC, conv_t, 0)

        for k0 in range(0, KW, TAPG):
            taps = list(range(k0, min(k0 + TAPG, KW)))

            def tap_group(rb, accs, taps=taps):
                r0 = rb * 16
                d = dyc_ref[pl.ds(pl.multiple_of(r0, 16), 16), :]
                out = []
                for k, acc in zip(taps, accs):
                    o = 2 + k
                    a = sa_ref[o % 8, pl.ds(pl.multiple_of(r0 + 8 * (o // 8), 8), 16), :]
                    out.append(acc + _fold8(d * a))
                return tuple(out)
            sums = lax.fori_loop(0, TM // 16, tap_group, tuple(jnp.zeros((8, D), F32) for _ in taps))
            for k, s in zip(taps, sums):
                gcw_ref[k] = gcw_ref[k] + s

        def glu_b(rb, carry):
            rows = _rows(rb)
            da = da_s[rows, :]
            sg = _sig(glu_ref[rows, :])
            dp_ref[0, rows, :] = (da * sg).astype(BF16)
            dp_ref[1, rows, :] = (da * val_ref[rows, :] * (sg * (1.0 - sg))).astype(BF16)
            return carry
        lax.fori_loop(0, TM // RB, glu_b, 0)

    tile = lambda seg: pl.BlockSpec((TM, D), lambda i: (i, seg))
    return pl.pallas_call(
        body, name="branch_a_bwd2", grid=(nt,),
        out_shape=(jax.ShapeDtypeStruct((NSEG, T, D), BF16), jax.ShapeDtypeStruct((32, 8, D), F32)),
        in_specs=[pl.BlockSpec(memory_space=ANY), pl.BlockSpec((TM, D), lambda i: (i, 0)),
                  pl.BlockSpec((HALO, D), lambda i: (jnp.minimum((i + 1) * hb, nt * hb - 1), 0)),
                  tile(0), tile(1), _halo_prev(0), _halo_prev(1), pl.BlockSpec((KW, 8, D), lambda i: (0, 0, 0))],
        out_specs=(pl.BlockSpec((2, TM, D), lambda i: (0, i, 0)), pl.BlockSpec((32, 8, D), lambda i: (0, 0, 0))),
        scratch_shapes=[VMEM((8, EXT, D), F32), VMEM((8, EXT, D), F32), VMEM((TM, D), F32)],
        input_output_aliases={0: 0},
        compiler_params=_cp(("arbitrary",)),
    )(dp, dyc, dyc, p, p, p, p, cw)


def _in_bwd_dx(dp, wing, x, dx2, mod, g_pre, gw_co, gw_so, gw_o):
    T = x.shape[0]
    nt = T // TM

    def body(dp_ref, w_ref, x_ref, dx2_ref, mod_ref, g_ref, gco_ref, gso_ref, go_ref,
             gx_ref, st_ref, rq_ref, dh_s, x_send, x_recv, x_loc):
        i = pl.program_id(0)
        mx, my_, mc = _me()
        me = _slot(mx, my_, mc)
        gq = (gco_ref, gso_ref, go_ref)

        def rows_of(a, slot):
            return gq[a].at[pl.ds(pl.multiple_of(slot * 128, 128), 128), :]

        def exchange(k, a, recv):
            px, py, pc = _xor_peer(k)
            peer = _slot(px, py, pc)
            return pltpu.make_async_remote_copy(
                src_ref=rows_of(a, me if recv else peer), dst_ref=rq_ref.at[peer if recv else me, a],
                send_sem=x_send.at[a, k - 1], recv_sem=x_recv.at[a, k - 1],
                device_id=(px, py, pc), device_id_type=MESH)

        local = [pltpu.make_async_copy(rows_of(a, me), rq_ref.at[me, a], x_loc.at[a]) for a in range(3)]

        @pl.when(i == 0)
        def _():
            st_ref[...] = jnp.zeros_like(st_ref)
            for cp in local:
                cp.start()
            for k in range(1, NDEV):
                for a in range(3):
                    exchange(k, a, False).start()

        dh_s[...] = lax.dot_general(dp_ref[0], w_ref[0], NT, preferred_element_type=F32)
        for j in range(1, NSEG):
            dh_s[...] += lax.dot_general(dp_ref[j], w_ref[j], NT, preferred_element_type=F32)

        def blk(rb, carry):
            d_sh, d_sc, g_g = carry
            rows = _rows(rb)
            xv = x_ref[rows, :]
            r = lax.rsqrt(jnp.mean(xv * xv, axis=-1, keepdims=True) + EPS)
            xn = xv * r
            g = g_ref[...]
            hpre = xn * g
            dh = dh_s[rows, :]
            dhp = dh * (1.0 + mod_ref[1:2, :])
            dxn = dhp * g
            gx_ref[rows, :] = dx2_ref[rows, :] + r * (dxn - xn * jnp.mean(dxn * xn, axis=-1, keepdims=True))
            return d_sh + _fold8(dh), d_sc + _fold8(dh * hpre), g_g + _fold8(dhp * xn)
        zero = jnp.zeros((8, D), F32)
        d_sh, d_sc, g_g = lax.fori_loop(0, TM // RB, blk, (zero, zero, zero))
        st_ref[0] = st_ref[0] + d_sh
        st_ref[1] = st_ref[1] + d_sc
        st_ref[2] = st_ref[2] + g_g

        @pl.when(i == nt - 1)
        def _():
            for k in range(1, NDEV):
                for a in range(3):
                    exchange(k, a, True).wait_recv()
            for k in range(1, NDEV):
                for a in range(3):
                    exchange(k, a, False).wait_send()
            for cp in local:
                cp.wait()

    tile = pl.BlockSpec((TM, D), lambda i: (i, 0))
    hbm = pl.BlockSpec(memory_space=ANY)
    return pl.pallas_call(
        body, name="in_bwd_dx", grid=(nt,),
        out_shape=(jax.ShapeDtypeStruct((T, D), F32), jax.ShapeDtypeStruct((3, 8, D), F32),
                   jax.ShapeDtypeStruct((NDEV, 3, 128, D), BF16)),
        in_specs=[pl.BlockSpec((NSEG, TM, D), lambda i: (0, i, 0)),
                  pl.BlockSpec((NSEG, D, D), lambda i: (0, 0, 0), pipeline_mode=pl.Buffered(1)),
                  tile, tile, pl.BlockSpec((3, D), lambda i: (0, 0)), pl.BlockSpec((1, D), lambda i: (0, 0)),
                  hbm, hbm, hbm],
        out_specs=(tile, pl.BlockSpec((3, 8, D), lambda i: (0, 0, 0)), hbm),
        scratch_shapes=[VMEM((TM, D), F32), pltpu.SemaphoreType.DMA((3, 7)), pltpu.SemaphoreType.DMA((3, 7)),
                        pltpu.SemaphoreType.DMA((3,))],
        compiler_params=_cp(("arbitrary",), vmem_mb=56),
    )(dp, wing, x, dx2, mod, g_pre, gw_co, gw_so, gw_o)


def _in_bwd_dw(h, dp):
    T = h.shape[0]
    nb = T // TMB

    def body(h_ref, dp_ref, gw_ref, acc):
        i = pl.program_id(1)

        @pl.when(i == 0)
        def _():
            acc[...] = jnp.zeros_like(acc)
        acc[...] += lax.dot_general(h_ref[...], dp_ref[...], TN, preferred_element_type=F32)

        @pl.when(i == nb - 1)
        def _():
            gw_ref[...] = acc[...].astype(BF16)

    return pl.pallas_call(
        body, name="in_bwd_dw", grid=(NSEG, nb),
        out_shape=jax.ShapeDtypeStruct((NSEG, D, D), BF16),
        in_specs=[pl.BlockSpec((TMB, D), lambda j, i: (i, 0)), pl.BlockSpec((None, TMB, D), lambda j, i: (j, i, 0))],
        out_specs=pl.BlockSpec((None, D, D), lambda j, i: (j, 0, 0)),
        scratch_shapes=[VMEM((D, D), F32)],
        compiler_params=_cp(("parallel", "arbitrary")),
    )(h, dp)


def _pack_smalls(st_x, st_a, st_b, st_m, gcw, gws, gbs):
    def body(sx_ref, sa_ref, sb_ref, sm_ref, gcw_ref, gws_ref, gbs_ref, out_ref):
        out_ref[...] = jnp.zeros_like(out_ref)
        fold = lambda v: jnp.sum(v, axis=0, keepdims=True)
        out_ref[0:1, :] = fold(sx_ref[0])
        out_ref[1:2, :] = fold(sx_ref[1])
        out_ref[2:3, :] = fold(sm_ref[2])
        out_ref[3:4, :] = fold(sx_ref[2])
        out_ref[4:5, :] = fold(sa_ref[2])
        out_ref[5:6, :] = fold(sa_ref[0])
        out_ref[6:7, :] = fold(sa_ref[1])
        out_ref[7:8, :] = fold(sb_ref[0])
        out_ref[8:9, :] = fold(sb_ref[1])
        out_ref[10:11, :] = fold(sm_ref[1])
        out_ref[11:12, :] = fold(sm_ref[0])
        for k in range(KW):
            out_ref[ROW_CW + k:ROW_CW + k + 1, :] = fold(gcw_ref[k])
        tri = lax.broadcasted_iota(jnp.int32, (CHUNK, CHUNK), 0) >= lax.broadcasted_iota(jnp.int32, (CHUNK, CHUNK), 1)
        for h in range(HEADS):
            out_ref[ROW_WS:ROW_WS + CHUNK, h * CHUNK:(h + 1) * CHUNK] = jnp.where(tri, gws_ref[h], 0.0)
            out_ref[9:10, h * CHUNK:(h + 1) * CHUNK] = fold(gbs_ref[h].T)

    vm = pl.BlockSpec(memory_space=VMEM)
    return pl.pallas_call(
        body, name="pack_smalls", out_shape=jax.ShapeDtypeStruct((NSMALL, D), F32),
        in_specs=[vm] * 7, out_specs=vm, compiler_params=_cp(vmem_mb=32),
    )(st_x, st_a, st_b, st_m, gcw, gws, gbs)


def _adam(w, g, m, v):
    m2 = ADAM_B1 * m + (1.0 - ADAM_B1) * g
    v2 = ADAM_B2 * v + (1.0 - ADAM_B2) * (g * g)
    m_hat = m2 / (1.0 - ADAM_B1 ** ADAM_STEP)
    v_hat = v2 / (1.0 - ADAM_B2 ** ADAM_STEP)
    delta = -ADAM_LR * (m_hat / (jnp.sqrt(v_hat) + ADAM_EPS) + ADAM_WD * w)
    return delta, m2, v2


def _small_finish(smg, w, m, v):
    def body(smg_ref, w_ref, m_ref, v_ref, loss_ref, g_ref, d_ref, m2_ref, v2_ref):
        g = smg_ref[0]
        for s in range(1, NDEV):
            g = g + smg_ref[s]
        g_ref[...] = g
        loss_ref[...] = jnp.sum(g[11:12, :], axis=1, keepdims=True)
        d_ref[...], m2_ref[...], v2_ref[...] = _adam(w_ref[...], g, m_ref[...], v_ref[...])

    vm = pl.BlockSpec(memory_space=VMEM)
    sd = jax.ShapeDtypeStruct((NSMALL, D), F32)
    return pl.pallas_call(
        body, name="small_finish", out_shape=(jax.ShapeDtypeStruct((1, 1), F32), sd, sd, sd, sd),
        in_specs=[vm] * 4, out_specs=(vm,) * 5, compiler_params=_cp(vmem_mb=32),
    )(smg, w, m, v)


def _ada_grad_adam(ct, dm, w, m, v):
    def body(ct_ref, dm_ref, w_ref, m_ref, v_ref, g_ref, d_ref, m2_ref, v2_ref):
        g = ct_ref[:, 0:1] * dm_ref[0:1, :]
        for b in range(1, NDEV):
            g = g + ct_ref[:, b:b + 1] * dm_ref[b:b + 1, :]
        g_ref[...] = g
        d_ref[...], m2_ref[...], v2_ref[...] = _adam(w_ref[...], g, m_ref[...], v_ref[...])

    vm = pl.BlockSpec(memory_space=VMEM)
    sd = jax.ShapeDtypeStruct(w.shape, F32)
    return pl.pallas_call(
        body, name="ada_grad_adam", out_shape=(sd, sd, sd, sd),
        in_specs=[vm] * 5, out_specs=(vm,) * 4, compiler_params=_cp(vmem_mb=32),
    )(ct, dm, w, m, v)


def _adam_f32(name, g, w, m, v):
    def body(g_ref, w_ref, m_ref, v_ref, d_ref, m2_ref, v2_ref):
        d_ref[...], m2_ref[...], v2_ref[...] = _adam(w_ref[...], g_ref[...], m_ref[...], v_ref[...])

    vm = pl.BlockSpec(memory_space=VMEM)
    sd = jax.ShapeDtypeStruct(w.shape, F32)
    return pl.pallas_call(
        body, name=name, out_shape=(sd, sd, sd), in_specs=[vm] * 4, out_specs=(vm,) * 3,
        compiler_params=_cp(vmem_mb=32),
    )(g, w, m, v)


def _adam_reduce(name, recv, w, m, v, rows, recv_spec):
    R, C = w.shape

    def body(r_ref, w_ref, m_ref, v_ref, g_ref, d_ref, m2_ref, v2_ref):
        g = r_ref[0].astype(F32)
        for s in range(1, NDEV):
            g = g + r_ref[s].astype(F32)
        g_ref[...] = g
        d_ref[...], m2_ref[...], v2_ref[...] = _adam(w_ref[...], g, m_ref[...], v_ref[...])

    tile = pl.BlockSpec((rows, C), lambda i: (i, 0))
    sd = jax.ShapeDtypeStruct((R, C), F32)
    return pl.pallas_call(
        body, name=name, grid=(R // rows,), out_shape=(sd, sd, sd, sd),
        in_specs=[recv_spec, tile, tile, tile], out_specs=(tile,) * 4,
        compiler_params=_cp(("parallel",)),
    )(recv, w, m, v)


def kernel(x, c, w_ada, b_ada, g_pre, w_in, conv_w, conv_b, conv_ln_g, conv_ln_b, w_conv_out, sgu_ln_g, sgu_ln_b, w_sgu, b_sgu, w_sgu_out, w_o, g_final, loss_target, m_w_ada, m_b_ada, m_g_pre, m_w_in, m_conv_w, m_conv_b, m_conv_ln_g, m_conv_ln_b, m_w_conv_out, m_sgu_ln_g, m_sgu_ln_b, m_w_sgu, m_b_sgu, m_w_sgu_out, m_w_o, m_g_final, v_w_ada, v_b_ada, v_g_pre, v_w_in, v_conv_w, v_conv_b, v_conv_ln_g, v_conv_ln_b, v_w_conv_out, v_sgu_ln_g, v_sgu_ln_b, v_w_sgu, v_b_sgu, v_w_sgu_out, v_w_o, v_g_final):
    T = x.shape[1]
    assert T % TMB == 0 and x.shape[2] == D
    xs = x[0]
    tgt = loss_target[0]
    my = 4 * lax.axis_index("x") + 2 * lax.axis_index("y") + lax.axis_index("c")

    cg, wqg, cwg = _gather_params(c, w_conv_out[0], w_sgu_out[0], w_o[0], conv_w[0])
    wq = wqg.reshape(3, D, D)
    cw = jnp.broadcast_to(jnp.transpose(cwg, (1, 0, 2)).reshape(KW, 1, D), (KW, 8, D))
    mod = _ada_fwd(cg.reshape(NDEV, D), w_ada[0], b_ada)
    gf = g_final.reshape(1, D)
    bst = b_sgu[0].T

    h = _prep_h(xs, mod, g_pre)
    p, wing = _in_proj(h, w_in[0], my ^ jnp.array(AG_ORDER, jnp.int32))
    yc, ya = _branch_a_fwd(p, cw, conv_b, conv_ln_g, conv_ln_b, wq)
    yb = _branch_b_fwd(p, w_sgu[0], bst, sgu_ln_g, sgu_ln_b, wq)
    dx2, dya, dyb, dp, gw_o, st_m = _merge_loss(p, ya, yb, xs, tgt, mod, gf, wq)

    dp, gw_so, gws, gbs, st_b = _branch_b_bwd(dp, dyb, p, w_sgu[0], bst, sgu_ln_g, sgu_ln_b, wq)
    dp, dyc, gw_co, st_a = _branch_a_bwd1(dp, dya, yc, p, conv_ln_g, conv_ln_b, wq)
    dp, gcw = _branch_a_bwd2(dp, dyc, p, cw)
    grad_x, st_x, rq = _in_bwd_dx(dp, wing, xs, dx2, mod, g_pre, gw_co, gw_so, gw_o)
    gw_in = _in_bwd_dw(h, dp)

    smalls = _pack_smalls(st_x, st_a, st_b, st_m, gcw, gws, gbs)
    smg, rin = _grad_exchange(smalls, gw_in)

    def pack(b_ada_, g_pre_, conv_b_, lg_, lb_, sg_, sb_, b_sgu_, gfin_, w_sgu_):
        top = jnp.concatenate([b_ada_.reshape(3, D), g_pre_, conv_b_, lg_, lb_, sg_, sb_, b_sgu_.reshape(1, D),
                               gfin_.reshape(1, D), jnp.zeros((ROW_WS - 11, D), F32)], axis=0)
        return jnp.concatenate([top, jnp.transpose(w_sgu_[0], (1, 0, 2)).reshape(CHUNK, D)], axis=0)

    pw = pack(b_ada, g_pre, conv_b, conv_ln_g, conv_ln_b, sgu_ln_g, sgu_ln_b, b_sgu, g_final, w_sgu)
    pm = pack(m_b_ada, m_g_pre, m_conv_b, m_conv_ln_g, m_conv_ln_b, m_sgu_ln_g, m_sgu_ln_b, m_b_sgu, m_g_final, m_w_sgu)
    pv = pack(v_b_ada, v_g_pre, v_conv_b, v_conv_ln_g, v_conv_ln_b, v_sgu_ln_g, v_sgu_ln_b, v_b_sgu, v_g_final, v_w_sgu)
    loss11, sg_, sd_, sm_, sv_ = _small_finish(smg, pw, pm, pv)

    def unpack(a):
        return dict(
            b_ada=a[0:3].reshape(1, 3 * D), g_pre=a[3:4], conv_b=a[4:5], conv_ln_g=a[5:6], conv_ln_b=a[6:7],
            sgu_ln_g=a[7:8], sgu_ln_b=a[8:9], b_sgu=a[9:10].reshape(1, HEADS, CHUNK), g_final=a[10],
            w_sgu=jnp.transpose(a[ROW_WS:ROW_WS + CHUNK].reshape(CHUNK, HEADS, CHUNK), (1, 0, 2))[None])
    small = [unpack(a) for a in (sg_, sd_, sm_, sv_)]

    g_cw = lax.dynamic_slice_in_dim(sg_[ROW_CW:ROW_CW + KW], my * 128, 128, axis=1)
    d_cw, m_cw, v_cw = _adam_f32("adam_conv_w", g_cw, conv_w[0], m_conv_w[0], v_conv_w[0])

    wcols = w_ada.shape[2]
    dm_all = smg[:, 0:3, :].reshape(NDEV, 3 * D)
    dm_mine = lax.dynamic_slice_in_dim(dm_all, my * wcols, wcols, axis=1)
    g_ada, d_ada, m_ada, v_ada = _ada_grad_adam(cg.reshape(NDEV, D).T, dm_mine, w_ada[0], m_w_ada[0], v_w_ada[0])

    g_in, d_in, m_in, v_in = _adam_reduce("adam_w_in", rin, w_in[0], m_w_in[0], v_w_in[0], 256,
                                          pl.BlockSpec((NDEV, 256, D), lambda i: (0, i, 0)))
    big = {}
    for j, (nm, w_, m_, v_) in enumerate((("w_conv_out", w_conv_out, m_w_conv_out, v_w_conv_out),
                                          ("w_sgu_out", w_sgu_out, m_w_sgu_out, v_w_sgu_out),
                                          ("w_o", w_o, m_w_o, v_w_o))):
        big[nm] = _adam_reduce("adam_" + nm, rq, w_[0], m_[0], v_[0], 128,
                               pl.BlockSpec((NDEV, None, 128, D), lambda i, j=j: (0, j, 0, 0)))

    per = {
        "w_ada": tuple(a[None] for a in (g_ada, d_ada, m_ada, v_ada)),
        "w_in": tuple(a[None] for a in (g_in, d_in, m_in, v_in)),
        "conv_w": tuple(a[None] for a in (g_cw, d_cw, m_cw, v_cw)),
    }
    for nm in ("w_conv_out", "w_sgu_out", "w_o"):
        per[nm] = tuple(a[None] for a in big[nm])
    for nm in ("b_ada", "g_pre", "conv_b", "conv_ln_g", "conv_ln_b", "sgu_ln_g", "sgu_ln_b", "w_sgu", "b_sgu", "g_final"):
        per[nm] = tuple(s[nm] for s in small)

    order = ["w_ada", "b_ada", "g_pre", "w_in", "conv_w", "conv_b", "conv_ln_g", "conv_ln_b", "w_conv_out",
             "sgu_ln_g", "sgu_ln_b", "w_sgu", "b_sgu", "w_sgu_out", "w_o", "g_final"]
    outs = [loss11.reshape(()), grad_x[None]]
    for part in range(4):
        outs += [per[nm][part] for nm in order]
    return tuple(outs)
```

```python
import functools

import jax
import jax.numpy as jnp
from jax import lax
from jax.experimental import pallas as pl
from jax.experimental.pallas import tpu as pltpu

F32 = jnp.float32
BF16 = jnp.bfloat16
MESH = pl.DeviceIdType.MESH
VMEM = pltpu.VMEM
ANY = pl.ANY

D = 1024
NDEV = 8
NSEG = 8
HEADS = 8
HD = 128
CHUNK = 128
KW = 31
HALO = 32
EPS = 1e-6
TM = 256
TMB = 512
RB = 32
RBC = 32
TAPG = 4
EXT = TM + HALO
NSMALL = 176
ROW_CW = 16
ROW_WS = 48

ADAM_LR = 0.001
ADAM_B1 = 0.9
ADAM_B2 = 0.999
ADAM_EPS = 1e-08
ADAM_WD = 0.01
ADAM_STEP = 10

INV_SQRT2 = 0.7071067811865476
INV_SQRT_2PI = 0.3989422804014327

NT = (((1,), (1,)), ((), ()))
TN = (((0,), (0,)), ((), ()))


def _cp(sem=None, vmem_mb=48):
    return pltpu.CompilerParams(dimension_semantics=sem, vmem_limit_bytes=vmem_mb * 1024 * 1024)


def _me():
    return lax.axis_index("x"), lax.axis_index("y"), lax.axis_index("c")


def _slot(px, py, pc):
    return 4 * px + 2 * py + pc


def _xor_peer(k):
    x, y, c = _me()
    return (1 - x if k & 4 else x, 1 - y if k & 2 else y, 1 - c if k & 1 else c)


def _allgather(srcs, dst_at, send_sems, recv_sems, loc_sems):
    x, y, c = _me()
    me = (x, y, c)
    sibling = (x, y, 1 - c)
    chips = [(1 - x, y), (x, 1 - y), (1 - x, 1 - y)]
    na = len(srcs)

    def copy(a, k, block, to, src=None):
        d = dst_at(a, _slot(*block))
        return pltpu.make_async_remote_copy(
            src_ref=d if src is None else src, dst_ref=d,
            send_sem=send_sems.at[a, k], recv_sem=recv_sems.at[a, k],
            device_id=to, device_id_type=MESH)

    local = [pltpu.make_async_copy(srcs[a], dst_at(a, _slot(*me)), loc_sems.at[a]) for a in range(na)]
    for cp in local:
        cp.start()
    first = []
    for a in range(na):
        first.append(copy(a, 0, me, sibling, src=srcs[a]))
        for j, chip in enumerate(chips):
            first.append(copy(a, 1 + j, me, (*chip, c), src=srcs[a]))
    for cp in first:
        cp.start()
    passed = []
    for j, chip in enumerate(chips):
        for a in range(na):
            copy(a, 1 + j, (*chip, c), me).wait_recv()
            p = copy(a, 4 + j, (*chip, c), sibling)
            p.start()
            passed.append(p)
    for a in range(na):
        copy(a, 0, sibling, me).wait_recv()
    for j, chip in enumerate(chips):
        for a in range(na):
            copy(a, 4 + j, (*chip, 1 - c), me).wait_recv()
    for cp in first + passed:
        cp.wait_send()
    for cp in local:
        cp.wait()


def _owner_exchange(gq, rq_ref, x_send, x_recv, x_loc):
    mx, my_, mc = _me()
    me = _slot(mx, my_, mc)

    def rows_of(a, slot):
        return gq[a].at[pl.ds(pl.multiple_of(slot * 128, 128), 128), :]

    def exchange(k, a, recv):
        px, py, pc = _xor_peer(k)
        peer = _slot(px, py, pc)
        return pltpu.make_async_remote_copy(
            src_ref=rows_of(a, me if recv else peer), dst_ref=rq_ref.at[peer if recv else me, a],
            send_sem=x_send.at[a, k - 1], recv_sem=x_recv.at[a, k - 1],
            device_id=(px, py, pc), device_id_type=MESH)

    local = [pltpu.make_async_copy(rows_of(a, me), rq_ref.at[me, a], x_loc.at[a]) for a in range(len(gq))]

    def start():
        for cp in local:
            cp.start()
        for k in range(1, NDEV):
            for a in range(len(gq)):
                exchange(k, a, False).start()

    def wait():
        for k in range(1, NDEV):
            for a in range(len(gq)):
                exchange(k, a, True).wait_recv()
        for k in range(1, NDEV):
            for a in range(len(gq)):
                exchange(k, a, False).wait_send()
        for cp in local:
            cp.wait()

    return start, wait


def _sig(v):
    return jax.nn.sigmoid(v)


def _gelu_parts(v):
    cdf = 0.5 * (1.0 + lax.erf(v * INV_SQRT2))
    pdf = jnp.exp(-0.5 * v * v) * INV_SQRT_2PI
    return v * cdf, cdf + v * pdf


def _ln_stats(v):
    mu = jnp.mean(v, axis=-1, keepdims=True)
    xc = v - mu
    var = jnp.mean(xc * xc, axis=-1, keepdims=True)
    rs = lax.rsqrt(var + EPS)
    return xc * rs, rs


def _ln_bwd(dn, n, rs):
    return rs * (dn - jnp.mean(dn, axis=-1, keepdims=True) - n * jnp.mean(dn * n, axis=-1, keepdims=True))


def _fold8(v):
    acc = v[0:8]
    for r in range(8, v.shape[0], 8):
        acc = acc + v[r:r + 8]
    return acc


def _rows(rb, n=RB):
    return pl.ds(pl.multiple_of(rb * n, n), n)


def _shift_copies(s_ref):
    n = EXT - 8
    for m in range(1, 8):
        for c0 in range(0, n, 56):
            s_ref[m, c0:c0 + 56, :] = s_ref[0, c0 + m:c0 + m + 56, :]


def _gather_params(c, w_co, w_so, w_o, conv_w):
    def body(c_ref, wco_ref, wso_ref, wo_ref, cw_ref, cg_ref, wqg_ref, cwg_ref,
             wq_st, send_sems, recv_sems, loc_sems):
        wq_st[0] = wco_ref[...].astype(BF16)
        wq_st[1] = wso_ref[...].astype(BF16)
        wq_st[2] = wo_ref[...].astype(BF16)
        srcs = [c_ref, wq_st, cw_ref]

        def dst_at(a, slot):
            if a == 1:
                return wqg_ref.at[:, slot]
            return (cg_ref, None, cwg_ref)[a].at[slot]

        _allgather(srcs, dst_at, send_sems, recv_sems, loc_sems)

    vm = pl.BlockSpec(memory_space=VMEM)
    hbm = pl.BlockSpec(memory_space=ANY)
    return pl.pallas_call(
        body, name="gather_params",
        out_shape=(jax.ShapeDtypeStruct((NDEV, 1, D), F32), jax.ShapeDtypeStruct((3, NDEV, 128, D), BF16),
                   jax.ShapeDtypeStruct((NDEV, KW, 128), F32)),
        in_specs=[vm] * 5, out_specs=(hbm, hbm, hbm),
        scratch_shapes=[VMEM((3, 128, D), BF16),
                        pltpu.SemaphoreType.DMA((3, 7)), pltpu.SemaphoreType.DMA((3, 7)), pltpu.SemaphoreType.DMA((3,))],
        compiler_params=_cp(vmem_mb=32),
    )(c, w_co, w_so, w_o, conv_w)


def _ada_fwd(cg, w_ada, b_ada):
    wcols = w_ada.shape[1]

    def body(cg_ref, w_ref, b_ref, mod_ref, msrc, mdst, send_sems, recv_sems):
        x, y, c = _me()
        me = _slot(x, y, c)
        m = jnp.dot(cg_ref[...], w_ref[...], preferred_element_type=F32, precision=lax.Precision.HIGHEST)
        for b in range(NDEV):
            msrc[b] = m[b:b + 1, :]
        mdst[me] = msrc[me]
        sends = []
        for k in range(1, NDEV):
            px, py, pc = _xor_peer(k)
            cp = pltpu.make_async_remote_copy(
                src_ref=msrc.at[_slot(px, py, pc)], dst_ref=mdst.at[me],
                send_sem=send_sems.at[k - 1], recv_sem=recv_sems.at[k - 1],
                device_id=(px, py, pc), device_id_type=MESH)
            cp.start()
            sends.append(cp)
        for k in range(1, NDEV):
            px, py, pc = _xor_peer(k)
            pltpu.make_async_remote_copy(
                src_ref=msrc.at[me], dst_ref=mdst.at[_slot(px, py, pc)],
                send_sem=send_sems.at[k - 1], recv_sem=recv_sems.at[k - 1],
                device_id=(px, py, pc), device_id_type=MESH).wait_recv()
        for cp in sends:
            cp.wait_send()
        full = jnp.concatenate([mdst[k] for k in range(NDEV)], axis=1) + b_ref[...]
        for r in range(3):
            mod_ref[r:r + 1, :] = full[:, r * D:(r + 1) * D]

    vm = pl.BlockSpec(memory_space=VMEM)
    return pl.pallas_call(
        body, name="ada_fwd",
        out_shape=jax.ShapeDtypeStruct((3, D), F32),
        in_specs=[vm, vm, vm], out_specs=vm,
        scratch_shapes=[VMEM((NDEV, 1, wcols), F32), VMEM((NDEV, 1, wcols), F32),
                        pltpu.SemaphoreType.DMA((7,)), pltpu.SemaphoreType.DMA((7,))],
        compiler_params=_cp(vmem_mb=32),
    )(cg, w_ada, b_ada)


def _gather_smalls(smalls):
    def body(sm_ref, smg_ref, ag_send, ag_recv, ag_loc):
        _allgather([sm_ref], lambda a, slot: smg_ref.at[slot], ag_send, ag_recv, ag_loc)

    hbm = pl.BlockSpec(memory_space=ANY)
    return pl.pallas_call(
        body, name="gather_smalls",
        out_shape=jax.ShapeDtypeStruct((NDEV, NSMALL, D), F32),
        in_specs=[hbm], out_specs=hbm,
        scratch_shapes=[pltpu.SemaphoreType.DMA((1, 7)), pltpu.SemaphoreType.DMA((1, 7)), pltpu.SemaphoreType.DMA((1,))],
        compiler_params=_cp(vmem_mb=32),
    )(smalls)


def _prep_h(x, mod, g_pre):
    T = x.shape[0]

    def body(x_ref, mod_ref, g_ref, h_ref):
        def blk(rb, carry):
            rows = _rows(rb)
            xv = x_ref[rows, :]
            r = lax.rsqrt(jnp.mean(xv * xv, axis=-1, keepdims=True) + EPS)
            h_ref[rows, :] = ((xv * r) * g_ref[...] * (1.0 + mod_ref[1:2, :]) + mod_ref[0:1, :]).astype(BF16)
            return carry
        lax.fori_loop(0, TMB // RB, blk, 0)

    return pl.pallas_call(
        body, name="prep_h", grid=(T // TMB,),
        out_shape=jax.ShapeDtypeStruct((T, D), BF16),
        in_specs=[pl.BlockSpec((TMB, D), lambda i: (i, 0)), pl.BlockSpec((3, D), lambda i: (0, 0)),
                  pl.BlockSpec((1, D), lambda i: (0, 0))],
        out_specs=pl.BlockSpec((TMB, D), lambda i: (i, 0)),
        compiler_params=_cp(("parallel",)),
    )(x, mod, g_pre)


AG_ORDER = (0, 1, 4, 2, 6, 5, 3, 7)


def _in_proj(h, w_in, order):
    T = h.shape[0]
    nb = T // TMB

    def body(order_ref, h_ref, w_ref, p_ref, wing_ref, wbuf, send_sems, recv_sems, out_sems):
        j = pl.program_id(0)
        i = pl.program_id(1)
        x, y, c = _me()
        me = (x, y, c)
        sibling = (x, y, 1 - c)
        chips = [(1 - x, y), (x, 1 - y), (1 - x, 1 - y)]
        blocks = [me, sibling] + [(*chip, c) for chip in chips] + [(*chip, 1 - c) for chip in chips]

        def copy(k, block, to):
            d = wbuf.at[_slot(*block)]
            return pltpu.make_async_remote_copy(src_ref=d, dst_ref=d, send_sem=send_sems.at[k], recv_sem=recv_sems.at[k],
                                                device_id=to, device_id_type=MESH)

        def writeback(jj):
            s = _slot(*blocks[jj])
            return pltpu.make_async_copy(wbuf.at[s], wing_ref.at[s], out_sems.at[jj])

        first = [copy(0, me, sibling)] + [copy(1 + q, me, (*chip, c)) for q, chip in enumerate(chips)]
        passed = [copy(4 + q, (*chip, c), sibling) for q, chip in enumerate(chips)]

        @pl.when((j == 0) & (i == 0))
        def _():
            wbuf[_slot(*me)] = w_ref[...].astype(BF16)
            for cp in first:
                cp.start()
            writeback(0).start()

        for jj in range(1, NSEG):
            @pl.when((j == jj) & (i == 0))
            def _(jj=jj):
                copy(jj - 1, blocks[jj], me).wait_recv()
                if 2 <= jj <= 4:
                    passed[jj - 2].start()
                writeback(jj).start()

        p_ref[...] = jnp.dot(h_ref[...], wbuf[order_ref[j]], preferred_element_type=F32)

        @pl.when((j == NSEG - 1) & (i == nb - 1))
        def _():
            for cp in first + passed:
                cp.wait_send()
            for jj in range(NSEG):
                writeback(jj).wait()

    return pl.pallas_call(
        body, name="in_proj",
        out_shape=(jax.ShapeDtypeStruct((T, NSEG * D), F32), jax.ShapeDtypeStruct((NSEG, D, D), BF16)),
        grid_spec=pltpu.PrefetchScalarGridSpec(
            num_scalar_prefetch=1, grid=(NSEG, nb),
            in_specs=[pl.BlockSpec((TMB, D), lambda j, i, o: (i, 0)), pl.BlockSpec(memory_space=VMEM)],
            out_specs=(pl.BlockSpec((TMB, D), lambda j, i, o: (i, o[j])), pl.BlockSpec(memory_space=ANY)),
            scratch_shapes=[VMEM((NSEG, D, D), BF16), pltpu.SemaphoreType.DMA((7,)), pltpu.SemaphoreType.DMA((7,)),
                            pltpu.SemaphoreType.DMA((NSEG,))]),
        compiler_params=_cp(("arbitrary", "arbitrary")),
    )(order, h, w_in)


def _fill_a_ext(i, s_ref, val_ref, glu_ref, hval_ref, hglu_ref):
    ah = hval_ref[...] * _sig(hglu_ref[...])
    s_ref[0, 0:HALO, :] = jnp.where(i > 0, ah, 0.0)

    def fill(rb, carry):
        rows = _rows(rb)
        s_ref[0, pl.ds(pl.multiple_of(HALO + rb * RB, RB), RB), :] = val_ref[rows, :] * _sig(glu_ref[rows, :])
        return carry
    lax.fori_loop(0, TM // RB, fill, 0)


def _halo_prev(seg):
    hb = TM // HALO
    return pl.BlockSpec((HALO, D), lambda i: (jnp.maximum(i * hb - 1, 0), seg))


def _branch_a_fwd(p, cw, conv_b, lg, lb, wq):
    T = p.shape[0]

    def body(val_ref, glu_ref, z_ref, hval_ref, hglu_ref, cw_ref, cb_ref, lg_ref, lb_ref, w_ref,
             yc_ref, ya_ref, s_ref, yp_ref):
        i = pl.program_id(0)
        _fill_a_ext(i, s_ref, val_ref, glu_ref, hval_ref, hglu_ref)
        _shift_copies(s_ref)

        def conv(rb, carry):
            r0 = rb * RBC
            accs = [jnp.broadcast_to(cb_ref[...], (8, D))] * (RBC // 8)
            for k in range(KW):
                o = 2 + k
                w = cw_ref[k]
                accs = [acc + w * s_ref[o % 8, pl.ds(pl.multiple_of(r0 + 8 * (o // 8 + g), 8), 8), :]
                        for g, acc in enumerate(accs)]
            for g, acc in enumerate(accs):
                yc_ref[pl.ds(pl.multiple_of(r0 + 8 * g, 8), 8), :] = acc
            return carry
        lax.fori_loop(0, TM // RBC, conv, 0)

        def post(rb, carry):
            rows = _rows(rb)
            n, _ = _ln_stats(yc_ref[rows, :])
            l = n * lg_ref[...] + lb_ref[...]
            z = z_ref[rows, :]
            yp_ref[rows, :] = ((l * _sig(l)) * (z * _sig(z))).astype(BF16)
            return carry
        lax.fori_loop(0, TM // RB, post, 0)
        ya_ref[...] = jnp.dot(yp_ref[...], w_ref[...], preferred_element_type=F32)

    tile = lambda seg: pl.BlockSpec((TM, D), lambda i: (i, seg))
    row = pl.BlockSpec((1, D), lambda i: (0, 0))
    return pl.pallas_call(
        body, name="branch_a_fwd", grid=(T // TM,),
        out_shape=(jax.ShapeDtypeStruct((T, D), F32), jax.ShapeDtypeStruct((T, D), F32)),
        in_specs=[tile(0), tile(1), tile(2), _halo_prev(0), _halo_prev(1),
                  pl.BlockSpec((KW, 8, D), lambda i: (0, 0, 0)), row, row, row,
                  pl.BlockSpec((None, D, D), lambda i: (0, 0, 0))],
        out_specs=(pl.BlockSpec((TM, D), lambda i: (i, 0)), pl.BlockSpec((TM, D), lambda i: (i, 0))),
        scratch_shapes=[VMEM((8, EXT, D), F32), VMEM((TM, D), BF16)],
        compiler_params=_cp(("parallel",)),
    )(p, p, p, p, p, cw, conv_b, lg, lb, wq)


def _masked_ws(ws_ref, wt_ref):
    tri = lax.broadcasted_iota(jnp.int32, (CHUNK, CHUNK), 0) >= lax.broadcasted_iota(jnp.int32, (CHUNK, CHUNK), 1)
    for h in range(HEADS):
        wt_ref[h] = jnp.where(tri, ws_ref[h], 0.0).astype(BF16)


def _sgu_mix(wt_ref, vl_ref, bst_ref, s_ref):
    for ck in range(TM // CHUNK):
        r = slice(ck * CHUNK, (ck + 1) * CHUNK)
        for h in range(HEADS):
            cs = slice(h * HD, (h + 1) * HD)
            s_ref[r, cs] = jnp.dot(wt_ref[h], vl_ref[r, cs], preferred_element_type=F32) + bst_ref[:, h:h + 1]


def _branch_b_fwd(p, ws, bst, sg, sb, wq):
    T = p.shape[0]

    def body(pu_ref, pv_ref, pz_ref, ws_ref, bst_ref, sg_ref, sb_ref, w_ref, yb_ref,
             wt_ref, vl_ref, t_ref, s_ref, yp_ref):
        _masked_ws(ws_ref, wt_ref)

        def pre(rb, carry):
            rows = _rows(rb)
            vg, _ = _gelu_parts(pv_ref[rows, :])
            vn, _ = _ln_stats(vg)
            vl_ref[rows, :] = (vn * sg_ref[...] + sb_ref[...]).astype(BF16)
            u, _ = _gelu_parts(pu_ref[rows, :])
            z = pz_ref[rows, :]
            t_ref[rows, :] = u * (z * _sig(z))
            return carry
        lax.fori_loop(0, TM // RB, pre, 0)
        _sgu_mix(wt_ref, vl_ref, bst_ref, s_ref)

        def post(rb, carry):
            rows = _rows(rb)
            yp_ref[rows, :] = (t_ref[rows, :] * s_ref[rows, :]).astype(BF16)
            return carry
        lax.fori_loop(0, TM // RB, post, 0)
        yb_ref[...] = jnp.dot(yp_ref[...], w_ref[...], preferred_element_type=F32)

    tile = lambda seg: pl.BlockSpec((TM, D), lambda i: (i, seg))
    row = pl.BlockSpec((1, D), lambda i: (0, 0))
    return pl.pallas_call(
        body, name="branch_b_fwd", grid=(T // TM,),
        out_shape=jax.ShapeDtypeStruct((T, D), F32),
        in_specs=[tile(3), tile(4), tile(5), pl.BlockSpec((HEADS, CHUNK, CHUNK), lambda i: (0, 0, 0)),
                  pl.BlockSpec((CHUNK, HEADS), lambda i: (0, 0)), row, row,
                  pl.BlockSpec((None, D, D), lambda i: (1, 0, 0))],
        out_specs=pl.BlockSpec((TM, D), lambda i: (i, 0)),
        scratch_shapes=[VMEM((HEADS, CHUNK, CHUNK), BF16), VMEM((TM, D), BF16), VMEM((TM, D), F32),
                        VMEM((TM, D), F32), VMEM((TM, D), BF16)],
        compiler_params=_cp(("parallel",)),
    )(p, p, p, ws, bst, sg, sb, wq)


def _merge_loss(p, ya, yb, x, tgt, mod, g_final, wq):
    T = x.shape[0]
    nt = T // TM

    def body(ga_ref, gb_ref, ya_ref, yb_ref, x_ref, t_ref, mod_ref, gf_ref, w_ref,
             dx2_ref, dya_ref, dyb_ref, dp_ref, gwo_ref, st_ref,
             mrg_s, out_s, dout_s, dm_s, gw_acc):
        i = pl.program_id(0)

        @pl.when(i == 0)
        def _():
            gw_acc[...] = jnp.zeros_like(gw_acc)
            st_ref[...] = jnp.zeros_like(st_ref)

        def merge(rb, carry):
            rows = _rows(rb)
            mrg_s[rows, :] = (_sig(ga_ref[rows, :]) * ya_ref[rows, :] + _sig(gb_ref[rows, :]) * yb_ref[rows, :]).astype(BF16)
            return carry
        lax.fori_loop(0, TM // RB, merge, 0)
        out_s[...] = jnp.dot(mrg_s[...], w_ref[...], preferred_element_type=F32)

        def head(rb, carry):
            loss, gg, dg = carry
            rows = _rows(rb)
            gate = mod_ref[2:3, :]
            gf = gf_ref[...]
            out = out_s[rows, :]
            x2 = x_ref[rows, :] + gate * out
            r2 = lax.rsqrt(jnp.mean(x2 * x2, axis=-1, keepdims=True) + EPS)
            x2n = x2 * r2
            diff = x2n * gf - t_ref[rows, :]
            dy = diff * (1.0 / D)
            dx2n = dy * gf
            dx2 = r2 * (dx2n - x2n * jnp.mean(dx2n * x2n, axis=-1, keepdims=True))
            dx2_ref[rows, :] = dx2
            dout_s[rows, :] = (dx2 * gate).astype(BF16)
            return loss + _fold8(diff * diff), gg + _fold8(dy * x2n), dg + _fold8(dx2 * out)
        zero = jnp.zeros((8, D), F32)
        loss, gg, dg = lax.fori_loop(0, TM // RB, head, (zero, zero, zero))
        st_ref[0] = st_ref[0] + loss * (0.5 / D)
        st_ref[1] = st_ref[1] + gg
        st_ref[2] = st_ref[2] + dg

        dm_s[...] = lax.dot_general(dout_s[...], w_ref[...], NT, preferred_element_type=F32)
        gw_acc[...] += lax.dot_general(mrg_s[...], dout_s[...], TN, preferred_element_type=F32)

        def split(rb, carry):
            rows = _rows(rb)
            dm = dm_s[rows, :]
            sa = _sig(ga_ref[rows, :])
            sb = _sig(gb_ref[rows, :])
            dya_ref[rows, :] = (dm * sa).astype(BF16)
            dyb_ref[rows, :] = (dm * sb).astype(BF16)
            dp_ref[0, rows, :] = (dm * ya_ref[rows, :] * (sa * (1.0 - sa))).astype(BF16)
            dp_ref[1, rows, :] = (dm * yb_ref[rows, :] * (sb * (1.0 - sb))).astype(BF16)
            return carry
        lax.fori_loop(0, TM // RB, split, 0)

        @pl.when(i == nt - 1)
        def _():
            gwo_ref[...] = gw_acc[...].astype(BF16)

    tile = pl.BlockSpec((TM, D), lambda i: (i, 0))
    return pl.pallas_call(
        body, name="merge_loss", grid=(nt,),
        out_shape=(jax.ShapeDtypeStruct((T, D), F32), jax.ShapeDtypeStruct((T, D), BF16), jax.ShapeDtypeStruct((T, D), BF16),
                   jax.ShapeDtypeStruct((NSEG, T, D), BF16), jax.ShapeDtypeStruct((D, D), BF16),
                   jax.ShapeDtypeStruct((3, 8, D), F32)),
        in_specs=[pl.BlockSpec((TM, D), lambda i: (i, 6)), pl.BlockSpec((TM, D), lambda i: (i, 7)), tile, tile, tile, tile,
                  pl.BlockSpec((3, D), lambda i: (0, 0)), pl.BlockSpec((1, D), lambda i: (0, 0)),
                  pl.BlockSpec((None, D, D), lambda i: (2, 0, 0))],
        out_specs=(tile, tile, tile, pl.BlockSpec((2, TM, D), lambda i: (3, i, 0)),
                   pl.BlockSpec((D, D), lambda i: (0, 0)), pl.BlockSpec((3, 8, D), lambda i: (0, 0, 0))),
        scratch_shapes=[VMEM((TM, D), BF16), VMEM((TM, D), F32), VMEM((TM, D), BF16), VMEM((TM, D), F32), VMEM((D, D), F32)],
        compiler_params=_cp(("arbitrary",)),
    )(p, p, ya, yb, x, tgt, mod, g_final, wq)


def _branch_b_bwd(dp, dyb, p, ws, bst, sg, sb, wq):
    T = p.shape[0]
    nt = T // TM

    def body(dp_in, dyb_ref, pu_ref, pv_ref, pz_ref, ws_ref, bst_ref, sg_ref, sb_ref, w_ref,
             dp_ref, gw_ref, gws_ref, gbs_ref, st_ref,
             wt_ref, d_s, vl_s, vn_s, gpv_s, rs_s, s_s, ds_s, ds32_s, yp_s, dvl_s, gw_acc):
        del dp_in
        i = pl.program_id(0)

        @pl.when(i == 0)
        def _():
            gw_acc[...] = jnp.zeros_like(gw_acc)
            gws_ref[...] = jnp.zeros_like(gws_ref)
            gbs_ref[...] = jnp.zeros_like(gbs_ref)
            st_ref[...] = jnp.zeros_like(st_ref)
            _masked_ws(ws_ref, wt_ref)

        d_s[...] = lax.dot_general(dyb_ref[...], w_ref[...], NT, preferred_element_type=F32)

        def pre(rb, carry):
            rows = _rows(rb)
            vg, gpv = _gelu_parts(pv_ref[rows, :])
            vn, rs = _ln_stats(vg)
            vl_s[rows, :] = (vn * sg_ref[...] + sb_ref[...]).astype(BF16)
            vn_s[rows, :] = vn
            gpv_s[rows, :] = gpv
            rs_s[rows, :] = rs
            return carry
        lax.fori_loop(0, TM // RB, pre, 0)
        _sgu_mix(wt_ref, vl_s, bst_ref, s_s)

        def mid(rb, carry):
            rows = _rows(rb)
            u, gpu = _gelu_parts(pu_ref[rows, :])
            z = pz_ref[rows, :]
            sz = _sig(z)
            siluz = z * sz
            d = d_s[rows, :]
            s = s_s[rows, :]
            t = u * siluz
            yp_s[rows, :] = (t * s).astype(BF16)
            dp_ref[0, rows, :] = (d * s * siluz * gpu).astype(BF16)
            ds = d * t
            ds32_s[rows, :] = ds
            ds_s[rows, :] = ds.astype(BF16)
            dp_ref[2, rows, :] = (d * u * s * (sz * (1.0 + z * (1.0 - sz)))).astype(BF16)
            return carry
        lax.fori_loop(0, TM // RB, mid, 0)

        for ck in range(TM // CHUNK):
            r = slice(ck * CHUNK, (ck + 1) * CHUNK)
            for h in range(HEADS):
                cs = slice(h * HD, (h + 1) * HD)
                dsh = ds_s[r, cs]
                dvl_s[r, cs] = lax.dot_general(wt_ref[h], dsh, TN, preferred_element_type=F32)
                gws_ref[h] += lax.dot_general(dsh, vl_s[r, cs], NT, preferred_element_type=F32)
                gbs_ref[h] += ds32_s[r, cs]

        def post(rb, carry):
            g_sg, g_sb = carry
            rows = _rows(rb)
            dvl = dvl_s[rows, :]
            vn = vn_s[rows, :]
            dvg = _ln_bwd(dvl * sg_ref[...], vn, rs_s[rows, :])
            dp_ref[1, rows, :] = (dvg * gpv_s[rows, :]).astype(BF16)
            return g_sg + _fold8(dvl * vn), g_sb + _fold8(dvl)
        zero = jnp.zeros((8, D), F32)
        g_sg, g_sb = lax.fori_loop(0, TM // RB, post, (zero, zero))
        st_ref[0] = st_ref[0] + g_sg
        st_ref[1] = st_ref[1] + g_sb

        gw_acc[...] += lax.dot_general(yp_s[...], dyb_ref[...], TN, preferred_element_type=F32)

        @pl.when(i == nt - 1)
        def _():
            gw_ref[...] = gw_acc[...].astype(BF16)

    tile = lambda seg: pl.BlockSpec((TM, D), lambda i: (i, seg))
    row = pl.BlockSpec((1, D), lambda i: (0, 0))
    hh = pl.BlockSpec((HEADS, CHUNK, CHUNK), lambda i: (0, 0, 0))
    return pl.pallas_call(
        body, name="branch_b_bwd", grid=(nt,),
        out_shape=(jax.ShapeDtypeStruct((NSEG, T, D), BF16), jax.ShapeDtypeStruct((D, D), BF16),
                   jax.ShapeDtypeStruct((HEADS, CHUNK, CHUNK), F32), jax.ShapeDtypeStruct((HEADS, CHUNK, HD), F32),
                   jax.ShapeDtypeStruct((2, 8, D), F32)),
        in_specs=[pl.BlockSpec(memory_space=ANY), pl.BlockSpec((TM, D), lambda i: (i, 0)), tile(3), tile(4), tile(5),
                  hh, pl.BlockSpec((CHUNK, HEADS), lambda i: (0, 0)), row, row,
                  pl.BlockSpec((None, D, D), lambda i: (1, 0, 0))],
        out_specs=(pl.BlockSpec((3, TM, D), lambda i: (1, i, 0)), pl.BlockSpec((D, D), lambda i: (0, 0)), hh, hh,
                   pl.BlockSpec((2, 8, D), lambda i: (0, 0, 0))),
        scratch_shapes=[VMEM((HEADS, CHUNK, CHUNK), BF16), VMEM((TM, D), F32), VMEM((TM, D), BF16), VMEM((TM, D), F32),
                        VMEM((TM, D), F32), VMEM((TM, 1), F32), VMEM((TM, D), F32), VMEM((TM, D), BF16), VMEM((TM, D), F32),
                        VMEM((TM, D), BF16), VMEM((TM, D), F32), VMEM((D, D), F32)],
        input_output_aliases={0: 0},
        compiler_params=_cp(("arbitrary",)),
    )(dp, dyb, p, p, p, ws, bst, sg, sb, wq)


def _branch_a_bwd1(dp, dya, yc, p, lg, lb, wq):
    T = p.shape[0]
    nt = T // TM

    def body(dp_in, dya_ref, yc_ref, z_ref, lg_ref, lb_ref, w_ref, dp_ref, dyc_ref, gw_ref, st_ref,
             d_s, yp_s, gw_acc):
        del dp_in
        i = pl.program_id(0)

        @pl.when(i == 0)
        def _():
            gw_acc[...] = jnp.zeros_like(gw_acc)
            st_ref[...] = jnp.zeros_like(st_ref)

        d_s[...] = lax.dot_general(dya_ref[...], w_ref[...], NT, preferred_element_type=F32)

        def blk(rb, carry):
            g_lg, g_lb, g_cb = carry
            rows = _rows(rb)
            n, rs = _ln_stats(yc_ref[rows, :])
            l = n * lg_ref[...] + lb_ref[...]
            sgl = _sig(l)
            sl = l * sgl
            z = z_ref[rows, :]
            sz = _sig(z)
            siluz = z * sz
            d = d_s[rows, :]
            yp_s[rows, :] = (sl * siluz).astype(BF16)
            dp_ref[rows, :] = (d * sl * (sz * (1.0 + z * (1.0 - sz)))).astype(BF16)
            dl = (d * siluz) * (sgl * (1.0 + l * (1.0 - sgl)))
            dyc = _ln_bwd(dl * lg_ref[...], n, rs)
            dyc_ref[rows, :] = dyc
            return g_lg + _fold8(dl * n), g_lb + _fold8(dl), g_cb + _fold8(dyc)
        zero = jnp.zeros((8, D), F32)
        g_lg, g_lb, g_cb = lax.fori_loop(0, TM // RB, blk, (zero, zero, zero))
        st_ref[0] = st_ref[0] + g_lg
        st_ref[1] = st_ref[1] + g_lb
        st_ref[2] = st_ref[2] + g_cb

        gw_acc[...] += lax.dot_general(yp_s[...], dya_ref[...], TN, preferred_element_type=F32)

        @pl.when(i == nt - 1)
        def _():
            gw_ref[...] = gw_acc[...].astype(BF16)

    tile = pl.BlockSpec((TM, D), lambda i: (i, 0))
    row = pl.BlockSpec((1, D), lambda i: (0, 0))
    return pl.pallas_call(
        body, name="branch_a_bwd1", grid=(nt,),
        out_shape=(jax.ShapeDtypeStruct((NSEG, T, D), BF16), jax.ShapeDtypeStruct((T, D), F32),
                   jax.ShapeDtypeStruct((D, D), BF16), jax.ShapeDtypeStruct((3, 8, D), F32)),
        in_specs=[pl.BlockSpec(memory_space=ANY), tile, tile, pl.BlockSpec((TM, D), lambda i: (i, 2)), row, row,
                  pl.BlockSpec((None, D, D), lambda i: (0, 0, 0))],
        out_specs=(pl.BlockSpec((None, TM, D), lambda i: (2, i, 0)), tile, pl.BlockSpec((D, D), lambda i: (0, 0)),
                   pl.BlockSpec((3, 8, D), lambda i: (0, 0, 0))),
        scratch_shapes=[VMEM((TM, D), F32), VMEM((TM, D), BF16), VMEM((D, D), F32)],
        input_output_aliases={0: 0},
        compiler_params=_cp(("arbitrary",)),
    )(dp, dya, yc, p, lg, lb, wq)


def _branch_a_bwd2(dp, dyc, p, cw, gw_co, gw_so, gw_o):
    T = p.shape[0]
    nt = T // TM
    hb = TM // HALO

    def body(dp_in, dyc_ref, hdyc_ref, val_ref, glu_ref, hval_ref, hglu_ref, cw_ref, gco_ref, gso_ref, go_ref,
             dp_ref, gcw_ref, rq_ref, sa_ref, sd_ref, da_s, x_send, x_recv, x_loc):
        del dp_in
        i = pl.program_id(0)
        start_exchange, wait_exchange = _owner_exchange((gco_ref, gso_ref, go_ref), rq_ref, x_send, x_recv, x_loc)

        @pl.when(i == 0)
        def _():
            gcw_ref[...] = jnp.zeros_like(gcw_ref)
            start_exchange()

        _fill_a_ext(i, sa_ref, val_ref, glu_ref, hval_ref, hglu_ref)
        _shift_copies(sa_ref)
        sd_ref[0, 0:TM, :] = dyc_ref[...]
        sd_ref[0, TM:EXT, :] = jnp.where(i < nt - 1, hdyc_ref[...], 0.0)
        _shift_copies(sd_ref)

        def conv_t(rb, carry):
            r0 = rb * RBC
            accs = [jnp.zeros((8, D), F32)] * (RBC // 8)
            for k in range(KW):
                o = KW - 1 - k
                w = cw_ref[k]
                accs = [acc + w * sd_ref[o % 8, pl.ds(pl.multiple_of(r0 + 8 * (o // 8 + g), 8), 8), :]
                        for g, acc in enumerate(accs)]
            for g, acc in enumerate(accs):
                da_s[pl.ds(pl.multiple_of(r0 + 8 * g, 8), 8), :] = acc
            return carry
        lax.fori_loop(0, TM // RBC, conv_t, 0)

        for k0 in range(0, KW, TAPG):
            taps = list(range(k0, min(k0 + TAPG, KW)))

            def tap_group(rb, accs, taps=taps):
                r0 = rb * 16
                d = dyc_ref[pl.ds(pl.multiple_of(r0, 16), 16), :]
                out = []
                for k, acc in zip(taps, accs):
                    o = 2 + k
                    a = sa_ref[o % 8, pl.ds(pl.multiple_of(r0 + 8 * (o // 8), 8), 16), :]
                    out.append(acc + _fold8(d * a))
                return tuple(out)
            sums = lax.fori_loop(0, TM // 16, tap_group, tuple(jnp.zeros((8, D), F32) for _ in taps))
            for k, s in zip(taps, sums):
                gcw_ref[k] = gcw_ref[k] + s

        def glu_b(rb, carry):
            rows = _rows(rb)
            da = da_s[rows, :]
            sg = _sig(glu_ref[rows, :])
            dp_ref[0, rows, :] = (da * sg).astype(BF16)
            dp_ref[1, rows, :] = (da * val_ref[rows, :] * (sg * (1.0 - sg))).astype(BF16)
            return carry
        lax.fori_loop(0, TM // RB, glu_b, 0)

        @pl.when(i == nt - 1)
        def _():
            wait_exchange()

    tile = lambda seg: pl.BlockSpec((TM, D), lambda i: (i, seg))
    hbm = pl.BlockSpec(memory_space=ANY)
    return pl.pallas_call(
        body, name="branch_a_bwd2", grid=(nt,),
        out_shape=(jax.ShapeDtypeStruct((NSEG, T, D), BF16), jax.ShapeDtypeStruct((32, 8, D), F32),
                   jax.ShapeDtypeStruct((NDEV, 3, 128, D), BF16)),
        in_specs=[hbm, pl.BlockSpec((TM, D), lambda i: (i, 0)),
                  pl.BlockSpec((HALO, D), lambda i: (jnp.minimum((i + 1) * hb, nt * hb - 1), 0)),
                  tile(0), tile(1), _halo_prev(0), _halo_prev(1), pl.BlockSpec((KW, 8, D), lambda i: (0, 0, 0)),
                  hbm, hbm, hbm],
        out_specs=(pl.BlockSpec((2, TM, D), lambda i: (0, i, 0)), pl.BlockSpec((32, 8, D), lambda i: (0, 0, 0)), hbm),
        scratch_shapes=[VMEM((8, EXT, D), F32), VMEM((8, EXT, D), F32), VMEM((TM, D), F32),
                        pltpu.SemaphoreType.DMA((3, 7)), pltpu.SemaphoreType.DMA((3, 7)), pltpu.SemaphoreType.DMA((3,))],
        input_output_aliases={0: 0},
        compiler_params=_cp(("arbitrary",)),
    )(dp, dyc, dyc, p, p, p, p, cw, gw_co, gw_so, gw_o)


def _in_bwd_dx(dp, wing, x, dx2, mod, g_pre):
    T = x.shape[0]
    nt = T // TM

    def body(dp_ref, w_ref, x_ref, dx2_ref, mod_ref, g_ref, gx_ref, st_ref, dh_s):
        i = pl.program_id(0)

        @pl.when(i == 0)
        def _():
            st_ref[...] = jnp.zeros_like(st_ref)

        dh_s[...] = lax.dot_general(dp_ref[0], w_ref[0], NT, preferred_element_type=F32)
        for j in range(1, NSEG):
            dh_s[...] += lax.dot_general(dp_ref[j], w_ref[j], NT, preferred_element_type=F32)

        def blk(rb, carry):
            d_sh, d_sc, g_g = carry
            rows = _rows(rb)
            xv = x_ref[rows, :]
            r = lax.rsqrt(jnp.mean(xv * xv, axis=-1, keepdims=True) + EPS)
            xn = xv * r
            g = g_ref[...]
            hpre = xn * g
            dh = dh_s[rows, :]
            dhp = dh * (1.0 + mod_ref[1:2, :])
            dxn = dhp * g
            gx_ref[rows, :] = dx2_ref[rows, :] + r * (dxn - xn * jnp.mean(dxn * xn, axis=-1, keepdims=True))
            return d_sh + _fold8(dh), d_sc + _fold8(dh * hpre), g_g + _fold8(dhp * xn)
        zero = jnp.zeros((8, D), F32)
        d_sh, d_sc, g_g = lax.fori_loop(0, TM // RB, blk, (zero, zero, zero))
        st_ref[0] = st_ref[0] + d_sh
        st_ref[1] = st_ref[1] + d_sc
        st_ref[2] = st_ref[2] + g_g

    tile = pl.BlockSpec((TM, D), lambda i: (i, 0))
    return pl.pallas_call(
        body, name="in_bwd_dx", grid=(nt,),
        out_shape=(jax.ShapeDtypeStruct((T, D), F32), jax.ShapeDtypeStruct((3, 8, D), F32)),
        in_specs=[pl.BlockSpec((NSEG, TM, D), lambda i: (0, i, 0)),
                  pl.BlockSpec((NSEG, D, D), lambda i: (0, 0, 0), pipeline_mode=pl.Buffered(1)),
                  tile, tile, pl.BlockSpec((3, D), lambda i: (0, 0)), pl.BlockSpec((1, D), lambda i: (0, 0))],
        out_specs=(tile, pl.BlockSpec((3, 8, D), lambda i: (0, 0, 0))),
        scratch_shapes=[VMEM((TM, D), F32)],
        compiler_params=_cp(("arbitrary",), vmem_mb=56),
    )(dp, wing, x, dx2, mod, g_pre)


RS_ORDER = (5, 4, 3, 2, 7, 6, 1, 0)


def _in_bwd_dw(h, dp, order):
    T = h.shape[0]
    tmw = min(2 * TMB, T)
    nb = T // tmw

    def body(order_ref, h_ref, dp_ref, g_ref, acc, sbuf, rsib, rici, send_sems, recv_sems):
        j = pl.program_id(0)
        i = pl.program_id(1)
        x, y, c = _me()
        sibling = (x, y, 1 - c)
        chips = [(1 - x, y), (x, 1 - y), (1 - x, 1 - y)]

        def to_sibling(q):
            return pltpu.make_async_remote_copy(src_ref=sbuf.at[q], dst_ref=rsib.at[q], send_sem=send_sems.at[q],
                                                recv_sem=recv_sems.at[q], device_id=sibling, device_id_type=MESH)

        def to_owner(q):
            return pltpu.make_async_remote_copy(src_ref=sbuf.at[4 + q], dst_ref=rici.at[q], send_sem=send_sems.at[4 + q],
                                                recv_sem=recv_sems.at[4 + q], device_id=(*chips[q], c), device_id_type=MESH)

        @pl.when(i == 0)
        def _():
            acc[...] = jnp.zeros_like(acc)
        acc[...] += lax.dot_general(h_ref[...], dp_ref[...], TN, preferred_element_type=F32)

        for q in range(4):
            @pl.when((j == 2 * q) & (i == nb - 1))
            def _(q=q):
                sbuf[q] = acc[...].astype(BF16)
                to_sibling(q).start()

        for q in range(3):
            @pl.when((j == 2 * q + 1) & (i == nb - 1))
            def _(q=q):
                to_sibling(q).wait_recv()
                sbuf[4 + q] = (acc[...] + rsib[q].astype(F32)).astype(BF16)
                to_owner(q).start()

        @pl.when((j == NSEG - 1) & (i == nb - 1))
        def _():
            to_sibling(3).wait_recv()
            g = acc[...] + rsib[3].astype(F32)
            for q in range(3):
                to_owner(q).wait_recv()
                g = g + rici[q].astype(F32)
            g_ref[...] = g
            for q in range(4):
                to_sibling(q).wait_send()
            for q in range(3):
                to_owner(q).wait_send()

    return pl.pallas_call(
        body, name="in_bwd_dw",
        out_shape=jax.ShapeDtypeStruct((D, D), F32),
        grid_spec=pltpu.PrefetchScalarGridSpec(
            num_scalar_prefetch=1, grid=(NSEG, nb),
            in_specs=[pl.BlockSpec((tmw, D), lambda j, i, o: (i, 0)),
                      pl.BlockSpec((None, tmw, D), lambda j, i, o: (o[j], i, 0))],
            out_specs=pl.BlockSpec((D, D), lambda j, i, o: (0, 0)),
            scratch_shapes=[VMEM((D, D), F32), VMEM((7, D, D), BF16), VMEM((4, D, D), BF16), VMEM((3, D, D), BF16),
                            pltpu.SemaphoreType.DMA((7,)), pltpu.SemaphoreType.DMA((7,))]),
        compiler_params=_cp(("arbitrary", "arbitrary"), vmem_mb=56),
    )(order, h, dp)


def _pack_smalls(st_x, st_a, st_b, st_m, gcw, gws, gbs):
    def body(sx_ref, sa_ref, sb_ref, sm_ref, gcw_ref, gws_ref, gbs_ref, out_ref):
        out_ref[...] = jnp.zeros_like(out_ref)
        fold = lambda v: jnp.sum(v, axis=0, keepdims=True)
        out_ref[0:1, :] = fold(sx_ref[0])
        out_ref[1:2, :] = fold(sx_ref[1])
        out_ref[2:3, :] = fold(sm_ref[2])
        out_ref[3:4, :] = fold(sx_ref[2])
        out_ref[4:5, :] = fold(sa_ref[2])
        out_ref[5:6, :] = fold(sa_ref[0])
        out_ref[6:7, :] = fold(sa_ref[1])
        out_ref[7:8, :] = fold(sb_ref[0])
        out_ref[8:9, :] = fold(sb_ref[1])
        out_ref[10:11, :] = fold(sm_ref[1])
        out_ref[11:12, :] = fold(sm_ref[0])
        for k in range(KW):
            out_ref[ROW_CW + k:ROW_CW + k + 1, :] = fold(gcw_ref[k])
        tri = lax.broadcasted_iota(jnp.int32, (CHUNK, CHUNK), 0) >= lax.broadcasted_iota(jnp.int32, (CHUNK, CHUNK), 1)
        for h in range(HEADS):
            out_ref[ROW_WS:ROW_WS + CHUNK, h * CHUNK:(h + 1) * CHUNK] = jnp.where(tri, gws_ref[h], 0.0)
            out_ref[9:10, h * CHUNK:(h + 1) * CHUNK] = fold(gbs_ref[h].T)

    vm = pl.BlockSpec(memory_space=VMEM)
    return pl.pallas_call(
        body, name="pack_smalls", out_shape=jax.ShapeDtypeStruct((NSMALL, D), F32),
        in_specs=[vm] * 7, out_specs=vm, compiler_params=_cp(vmem_mb=32),
    )(st_x, st_a, st_b, st_m, gcw, gws, gbs)


def _adam(w, g, m, v):
    m2 = ADAM_B1 * m + (1.0 - ADAM_B1) * g
    v2 = ADAM_B2 * v + (1.0 - ADAM_B2) * (g * g)
    m_hat = m2 / (1.0 - ADAM_B1 ** ADAM_STEP)
    v_hat = v2 / (1.0 - ADAM_B2 ** ADAM_STEP)
    delta = -ADAM_LR * (m_hat / (jnp.sqrt(v_hat) + ADAM_EPS) + ADAM_WD * w)
    return delta, m2, v2


def _small_finish(smg, w, m, v):
    def body(smg_ref, w_ref, m_ref, v_ref, loss_ref, g_ref, d_ref, m2_ref, v2_ref):
        g = smg_ref[0]
        for s in range(1, NDEV):
            g = g + smg_ref[s]
        g_ref[...] = g
        loss_ref[...] = jnp.sum(g[11:12, :], axis=1, keepdims=True)
        d_ref[...], m2_ref[...], v2_ref[...] = _adam(w_ref[...], g, m_ref[...], v_ref[...])

    vm = pl.BlockSpec(memory_space=VMEM)
    sd = jax.ShapeDtypeStruct((NSMALL, D), F32)
    return pl.pallas_call(
        body, name="small_finish", out_shape=(jax.ShapeDtypeStruct((1, 1), F32), sd, sd, sd, sd),
        in_specs=[vm] * 4, out_specs=(vm,) * 5, compiler_params=_cp(vmem_mb=32),
    )(smg, w, m, v)


def _ada_grad_adam(ct, dm, w, m, v):
    def body(ct_ref, dm_ref, w_ref, m_ref, v_ref, g_ref, d_ref, m2_ref, v2_ref):
        g = ct_ref[:, 0:1] * dm_ref[0:1, :]
        for b in range(1, NDEV):
            g = g + ct_ref[:, b:b + 1] * dm_ref[b:b + 1, :]
        g_ref[...] = g
        d_ref[...], m2_ref[...], v2_ref[...] = _adam(w_ref[...], g, m_ref[...], v_ref[...])

    vm = pl.BlockSpec(memory_space=VMEM)
    sd = jax.ShapeDtypeStruct(w.shape, F32)
    return pl.pallas_call(
        body, name="ada_grad_adam", out_shape=(sd, sd, sd, sd),
        in_specs=[vm] * 5, out_specs=(vm,) * 4, compiler_params=_cp(vmem_mb=32),
    )(ct, dm, w, m, v)


def _adam_f32(name, g, w, m, v, rows=None):
    R, C = w.shape
    rows = R if rows is None else rows

    def body(g_ref, w_ref, m_ref, v_ref, d_ref, m2_ref, v2_ref):
        d_ref[...], m2_ref[...], v2_ref[...] = _adam(w_ref[...], g_ref[...], m_ref[...], v_ref[...])

    tile = pl.BlockSpec((rows, C), lambda i: (i, 0))
    sd = jax.ShapeDtypeStruct(w.shape, F32)
    return pl.pallas_call(
        body, name=name, grid=(R // rows,), out_shape=(sd, sd, sd), in_specs=[tile] * 4, out_specs=(tile,) * 3,
        compiler_params=_cp(("parallel",), vmem_mb=32),
    )(g, w, m, v)


def _adam_reduce(name, recv, w, m, v, rows, recv_spec):
    R, C = w.shape

    def body(r_ref, w_ref, m_ref, v_ref, g_ref, d_ref, m2_ref, v2_ref):
        g = r_ref[0].astype(F32)
        for s in range(1, NDEV):
            g = g + r_ref[s].astype(F32)
        g_ref[...] = g
        d_ref[...], m2_ref[...], v2_ref[...] = _adam(w_ref[...], g, m_ref[...], v_ref[...])

    tile = pl.BlockSpec((rows, C), lambda i: (i, 0))
    sd = jax.ShapeDtypeStruct((R, C), F32)
    return pl.pallas_call(
        body, name=name, grid=(R // rows,), out_shape=(sd, sd, sd, sd),
        in_specs=[recv_spec, tile, tile, tile], out_specs=(tile,) * 4,
        compiler_params=_cp(("parallel",)),
    )(recv, w, m, v)


def kernel(x, c, w_ada, b_ada, g_pre, w_in, conv_w, conv_b, conv_ln_g, conv_ln_b, w_conv_out, sgu_ln_g, sgu_ln_b, w_sgu, b_sgu, w_sgu_out, w_o, g_final, loss_target, m_w_ada, m_b_ada, m_g_pre, m_w_in, m_conv_w, m_conv_b, m_conv_ln_g, m_conv_ln_b, m_w_conv_out, m_sgu_ln_g, m_sgu_ln_b, m_w_sgu, m_b_sgu, m_w_sgu_out, m_w_o, m_g_final, v_w_ada, v_b_ada, v_g_pre, v_w_in, v_conv_w, v_conv_b, v_conv_ln_g, v_conv_ln_b, v_w_conv_out, v_sgu_ln_g, v_sgu_ln_b, v_w_sgu, v_b_sgu, v_w_sgu_out, v_w_o, v_g_final):
    T = x.shape[1]
    assert T % TMB == 0 and x.shape[2] == D
    xs = x[0]
    tgt = loss_target[0]
    my = 4 * lax.axis_index("x") + 2 * lax.axis_index("y") + lax.axis_index("c")

    cg, wqg, cwg = _gather_params(c, w_conv_out[0], w_sgu_out[0], w_o[0], conv_w[0])
    wq = wqg.reshape(3, D, D)
    cw = jnp.broadcast_to(jnp.transpose(cwg, (1, 0, 2)).reshape(KW, 1, D), (KW, 8, D))
    mod = _ada_fwd(cg.reshape(NDEV, D), w_ada[0], b_ada)
    gf = g_final.reshape(1, D)
    bst = b_sgu[0].T

    h = _prep_h(xs, mod, g_pre)
    p, wing = _in_proj(h, w_in[0], my ^ jnp.array(AG_ORDER, jnp.int32))
    yc, ya = _branch_a_fwd(p, cw, conv_b, conv_ln_g, conv_ln_b, wq)
    yb = _branch_b_fwd(p, w_sgu[0], bst, sgu_ln_g, sgu_ln_b, wq)
    dx2, dya, dyb, dp, gw_o, st_m = _merge_loss(p, ya, yb, xs, tgt, mod, gf, wq)

    dp, gw_so, gws, gbs, st_b = _branch_b_bwd(dp, dyb, p, w_sgu[0], bst, sgu_ln_g, sgu_ln_b, wq)
    dp, dyc, gw_co, st_a = _branch_a_bwd1(dp, dya, yc, p, conv_ln_g, conv_ln_b, wq)
    dp, gcw, rq = _branch_a_bwd2(dp, dyc, p, cw, gw_co, gw_so, gw_o)
    g_in = _in_bwd_dw(h, dp, my ^ jnp.array(RS_ORDER, jnp.int32))
    grad_x, st_x = _in_bwd_dx(dp, wing, xs, dx2, mod, g_pre)

    smalls = _pack_smalls(st_x, st_a, st_b, st_m, gcw, gws, gbs)
    smg = _gather_smalls(smalls)

    def pack(b_ada_, g_pre_, conv_b_, lg_, lb_, sg_, sb_, b_sgu_, gfin_, w_sgu_):
        top = jnp.concatenate([b_ada_.reshape(3, D), g_pre_, conv_b_, lg_, lb_, sg_, sb_, b_sgu_.reshape(1, D),
                               gfin_.reshape(1, D), jnp.zeros((ROW_WS - 11, D), F32)], axis=0)
        return jnp.concatenate([top, jnp.transpose(w_sgu_[0], (1, 0, 2)).reshape(CHUNK, D)], axis=0)

    pw = pack(b_ada, g_pre, conv_b, conv_ln_g, conv_ln_b, sgu_ln_g, sgu_ln_b, b_sgu, g_final, w_sgu)
    pm = pack(m_b_ada, m_g_pre, m_conv_b, m_conv_ln_g, m_conv_ln_b, m_sgu_ln_g, m_sgu_ln_b, m_b_sgu, m_g_final, m_w_sgu)
    pv = pack(v_b_ada, v_g_pre, v_conv_b, v_conv_ln_g, v_conv_ln_b, v_sgu_ln_g, v_sgu_ln_b, v_b_sgu, v_g_final, v_w_sgu)
    loss11, sg_, sd_, sm_, sv_ = _small_finish(smg, pw, pm, pv)

    def unpack(a):
        return dict(
            b_ada=a[0:3].reshape(1, 3 * D), g_pre=a[3:4], conv_b=a[4:5], conv_ln_g=a[5:6], conv_ln_b=a[6:7],
            sgu_ln_g=a[7:8], sgu_ln_b=a[8:9], b_sgu=a[9:10].reshape(1, HEADS, CHUNK), g_final=a[10],
            w_sgu=jnp.transpose(a[ROW_WS:ROW_WS + CHUNK].reshape(CHUNK, HEADS, CHUNK), (1, 0, 2))[None])
    small = [unpack(a) for a in (sg_, sd_, sm_, sv_)]

    g_cw = lax.dynamic_slice_in_dim(sg_[ROW_CW:ROW_CW + KW], my * 128, 128, axis=1)
    d_cw, m_cw, v_cw = _adam_f32("adam_conv_w", g_cw, conv_w[0], m_conv_w[0], v_conv_w[0])

    wcols = w_ada.shape[2]
    dm_all = smg[:, 0:3, :].reshape(NDEV, 3 * D)
    dm_mine = lax.dynamic_slice_in_dim(dm_all, my * wcols, wcols, axis=1)
    g_ada, d_ada, m_ada, v_ada = _ada_grad_adam(cg.reshape(NDEV, D).T, dm_mine, w_ada[0], m_w_ada[0], v_w_ada[0])

    d_in, m_in, v_in = _adam_f32("adam_w_in", g_in, w_in[0], m_w_in[0], v_w_in[0], rows=256)
    big = {}
    for j, (nm, w_, m_, v_) in enumerate((("w_conv_out", w_conv_out, m_w_conv_out, v_w_conv_out),
                                          ("w_sgu_out", w_sgu_out, m_w_sgu_out, v_w_sgu_out),
                                          ("w_o", w_o, m_w_o, v_w_o))):
        big[nm] = _adam_reduce("adam_" + nm, rq, w_[0], m_[0], v_[0], 128,
                               pl.BlockSpec((NDEV, None, 128, D), lambda i, j=j: (0, j, 0, 0)))

    per = {
        "w_ada": tuple(a[None] for a in (g_ada, d_ada, m_ada, v_ada)),
        "w_in": tuple(a[None] for a in (g_in, d_in, m_in, v_in)),
        "conv_w": tuple(a[None] for a in (g_cw, d_cw, m_cw, v_cw)),
    }
    for nm in ("w_conv_out", "w_sgu_out", "w_o"):
        per[nm] = tuple(a[None] for a in big[nm])
    for nm in ("b_ada", "g_pre", "conv_b", "conv_ln_g", "conv_ln_b", "sgu_ln_g", "sgu_ln_b", "w_sgu", "b_sgu", "g_final"):
        per[nm] = tuple(s[nm] for s in small)

    order = ["w_ada", "b_ada", "g_pre", "w_in", "conv_w", "conv_b", "conv_ln_g", "conv_ln_b", "w_conv_out",
             "sgu_ln_g", "sgu_ln_b", "w_sgu", "b_sgu", "w_sgu_out", "w_o", "g_final"]
    outs = [loss11.reshape(()), grad_x[None]]
    for part in range(4):
        outs += [per[nm][part] for nm in order]
    return tuple(outs)
```

```python
import functools

import jax
import jax.numpy as jnp
from jax import lax
from jax.experimental import pallas as pl
from jax.experimental.pallas import tpu as pltpu

F32 = jnp.float32
BF16 = jnp.bfloat16
MESH = pl.DeviceIdType.MESH
VMEM = pltpu.VMEM
ANY = pl.ANY

D = 1024
NDEV = 8
NSEG = 8
HEADS = 8
HD = 128
CHUNK = 128
KW = 31
HALO = 32
EPS = 1e-6
TM = 256
TMB = 512
RB = 32
UNROLL = 4
RBC = 32
TAPG = 4
EXT = TM + HALO
NSMALL = 176
ROW_CW = 16
ROW_WS = 48

ADAM_LR = 0.001
ADAM_B1 = 0.9
ADAM_B2 = 0.999
ADAM_EPS = 1e-08
ADAM_WD = 0.01
ADAM_STEP = 10

INV_SQRT2 = 0.7071067811865476
INV_SQRT_2PI = 0.3989422804014327

NT = (((1,), (1,)), ((), ()))
TN = (((0,), (0,)), ((), ()))


def _cp(sem=None, vmem_mb=48):
    return pltpu.CompilerParams(dimension_semantics=sem, vmem_limit_bytes=vmem_mb * 1024 * 1024)


def _me():
    return lax.axis_index("x"), lax.axis_index("y"), lax.axis_index("c")


def _slot(px, py, pc):
    return 4 * px + 2 * py + pc


def _xor_peer(k):
    x, y, c = _me()
    return (1 - x if k & 4 else x, 1 - y if k & 2 else y, 1 - c if k & 1 else c)


def _allgather(srcs, dst_at, send_sems, recv_sems, loc_sems):
    x, y, c = _me()
    me = (x, y, c)
    sibling = (x, y, 1 - c)
    chips = [(1 - x, y), (x, 1 - y), (1 - x, 1 - y)]
    na = len(srcs)

    def copy(a, k, block, to, src=None):
        d = dst_at(a, _slot(*block))
        return pltpu.make_async_remote_copy(
            src_ref=d if src is None else src, dst_ref=d,
            send_sem=send_sems.at[a, k], recv_sem=recv_sems.at[a, k],
            device_id=to, device_id_type=MESH)

    local = [pltpu.make_async_copy(srcs[a], dst_at(a, _slot(*me)), loc_sems.at[a]) for a in range(na)]
    for cp in local:
        cp.start()
    first = []
    for a in range(na):
        first.append(copy(a, 0, me, sibling, src=srcs[a]))
        for j, chip in enumerate(chips):
            first.append(copy(a, 1 + j, me, (*chip, c), src=srcs[a]))
    for cp in first:
        cp.start()
    passed = []
    for j, chip in enumerate(chips):
        for a in range(na):
            copy(a, 1 + j, (*chip, c), me).wait_recv()
            p = copy(a, 4 + j, (*chip, c), sibling)
            p.start()
            passed.append(p)
    for a in range(na):
        copy(a, 0, sibling, me).wait_recv()
    for j, chip in enumerate(chips):
        for a in range(na):
            copy(a, 4 + j, (*chip, 1 - c), me).wait_recv()
    for cp in first + passed:
        cp.wait_send()
    for cp in local:
        cp.wait()


def _owner_exchange(gq, rq_ref, x_send, x_recv, x_loc):
    mx, my_, mc = _me()
    me = _slot(mx, my_, mc)

    def rows_of(a, slot):
        return gq[a].at[pl.ds(pl.multiple_of(slot * 128, 128), 128), :]

    def exchange(k, a, recv):
        px, py, pc = _xor_peer(k)
        peer = _slot(px, py, pc)
        return pltpu.make_async_remote_copy(
            src_ref=rows_of(a, me if recv else peer), dst_ref=rq_ref.at[peer if recv else me, a],
            send_sem=x_send.at[a, k - 1], recv_sem=x_recv.at[a, k - 1],
            device_id=(px, py, pc), device_id_type=MESH)

    local = [pltpu.make_async_copy(rows_of(a, me), rq_ref.at[me, a], x_loc.at[a]) for a in range(len(gq))]

    def start():
        for cp in local:
            cp.start()
        for k in range(1, NDEV):
            for a in range(len(gq)):
                exchange(k, a, False).start()

    def wait():
        for k in range(1, NDEV):
            for a in range(len(gq)):
                exchange(k, a, True).wait_recv()
        for k in range(1, NDEV):
            for a in range(len(gq)):
                exchange(k, a, False).wait_send()
        for cp in local:
            cp.wait()

    return start, wait


def _sig(v):
    return jax.nn.sigmoid(v)


def _gelu_parts(v):
    cdf = 0.5 * (1.0 + lax.erf(v * INV_SQRT2))
    pdf = jnp.exp(-0.5 * v * v) * INV_SQRT_2PI
    return v * cdf, cdf + v * pdf


def _ln_stats(v):
    mu = jnp.mean(v, axis=-1, keepdims=True)
    xc = v - mu
    var = jnp.mean(xc * xc, axis=-1, keepdims=True)
    rs = lax.rsqrt(var + EPS)
    return xc * rs, rs


def _ln_bwd(dn, n, rs):
    return rs * (dn - jnp.mean(dn, axis=-1, keepdims=True) - n * jnp.mean(dn * n, axis=-1, keepdims=True))


def _fold8(v):
    acc = v[0:8]
    for r in range(8, v.shape[0], 8):
        acc = acc + v[r:r + 8]
    return acc


def _rows(rb, n=RB):
    return pl.ds(pl.multiple_of(rb * n, n), n)


def _blocks(n, body, init=0, unroll=UNROLL):
    def trip(t, carry):
        for u in range(unroll):
            carry = body(t * unroll + u, carry)
        return carry
    return lax.fori_loop(0, n // unroll, trip, init)


def _shift_copies(s_ref):
    n = EXT - 8
    for m in range(1, 8):
        for c0 in range(0, n, 56):
            s_ref[m, c0:c0 + 56, :] = s_ref[0, c0 + m:c0 + m + 56, :]


def _gather_params(c, w_co, w_so, w_o, conv_w):
    def body(c_ref, wco_ref, wso_ref, wo_ref, cw_ref, cg_ref, wqg_ref, cwg_ref,
             wq_st, send_sems, recv_sems, loc_sems):
        wq_st[0] = wco_ref[...].astype(BF16)
        wq_st[1] = wso_ref[...].astype(BF16)
        wq_st[2] = wo_ref[...].astype(BF16)
        srcs = [c_ref, wq_st, cw_ref]

        def dst_at(a, slot):
            if a == 1:
                return wqg_ref.at[:, slot]
            return (cg_ref, None, cwg_ref)[a].at[slot]

        _allgather(srcs, dst_at, send_sems, recv_sems, loc_sems)

    vm = pl.BlockSpec(memory_space=VMEM)
    hbm = pl.BlockSpec(memory_space=ANY)
    return pl.pallas_call(
        body, name="gather_params",
        out_shape=(jax.ShapeDtypeStruct((NDEV, 1, D), F32), jax.ShapeDtypeStruct((3, NDEV, 128, D), BF16),
                   jax.ShapeDtypeStruct((NDEV, KW, 128), F32)),
        in_specs=[vm] * 5, out_specs=(hbm, hbm, hbm),
        scratch_shapes=[VMEM((3, 128, D), BF16),
                        pltpu.SemaphoreType.DMA((3, 7)), pltpu.SemaphoreType.DMA((3, 7)), pltpu.SemaphoreType.DMA((3,))],
        compiler_params=_cp(vmem_mb=32),
    )(c, w_co, w_so, w_o, conv_w)


def _ada_fwd(cg, w_ada, b_ada):
    wcols = w_ada.shape[1]

    def body(cg_ref, w_ref, b_ref, mod_ref, msrc, mdst, send_sems, recv_sems):
        x, y, c = _me()
        me = _slot(x, y, c)
        m = jnp.dot(cg_ref[...], w_ref[...], preferred_element_type=F32, precision=lax.Precision.HIGHEST)
        for b in range(NDEV):
            msrc[b] = m[b:b + 1, :]
        mdst[me] = msrc[me]
        sends = []
        for k in range(1, NDEV):
            px, py, pc = _xor_peer(k)
            cp = pltpu.make_async_remote_copy(
                src_ref=msrc.at[_slot(px, py, pc)], dst_ref=mdst.at[me],
                send_sem=send_sems.at[k - 1], recv_sem=recv_sems.at[k - 1],
                device_id=(px, py, pc), device_id_type=MESH)
            cp.start()
            sends.append(cp)
        for k in range(1, NDEV):
            px, py, pc = _xor_peer(k)
            pltpu.make_async_remote_copy(
                src_ref=msrc.at[me], dst_ref=mdst.at[_slot(px, py, pc)],
                send_sem=send_sems.at[k - 1], recv_sem=recv_sems.at[k - 1],
                device_id=(px, py, pc), device_id_type=MESH).wait_recv()
        for cp in sends:
            cp.wait_send()
        full = jnp.concatenate([mdst[k] for k in range(NDEV)], axis=1) + b_ref[...]
        for r in range(3):
            mod_ref[r:r + 1, :] = full[:, r * D:(r + 1) * D]

    vm = pl.BlockSpec(memory_space=VMEM)
    return pl.pallas_call(
        body, name="ada_fwd",
        out_shape=jax.ShapeDtypeStruct((3, D), F32),
        in_specs=[vm, vm, vm], out_specs=vm,
        scratch_shapes=[VMEM((NDEV, 1, wcols), F32), VMEM((NDEV, 1, wcols), F32),
                        pltpu.SemaphoreType.DMA((7,)), pltpu.SemaphoreType.DMA((7,))],
        compiler_params=_cp(vmem_mb=32),
    )(cg, w_ada, b_ada)


def _gather_smalls(smalls):
    def body(sm_ref, smg_ref, ag_send, ag_recv, ag_loc):
        _allgather([sm_ref], lambda a, slot: smg_ref.at[slot], ag_send, ag_recv, ag_loc)

    hbm = pl.BlockSpec(memory_space=ANY)
    return pl.pallas_call(
        body, name="gather_smalls",
        out_shape=jax.ShapeDtypeStruct((NDEV, NSMALL, D), F32),
        in_specs=[hbm], out_specs=hbm,
        scratch_shapes=[pltpu.SemaphoreType.DMA((1, 7)), pltpu.SemaphoreType.DMA((1, 7)), pltpu.SemaphoreType.DMA((1,))],
        compiler_params=_cp(vmem_mb=32),
    )(smalls)


def _prep_h(x, mod, g_pre):
    T = x.shape[0]

    def body(x_ref, mod_ref, g_ref, h_ref):
        def blk(rb, carry):
            rows = _rows(rb)
            xv = x_ref[rows, :]
            r = lax.rsqrt(jnp.mean(xv * xv, axis=-1, keepdims=True) + EPS)
            h_ref[rows, :] = ((xv * r) * g_ref[...] * (1.0 + mod_ref[1:2, :]) + mod_ref[0:1, :]).astype(BF16)
            return carry
        _blocks(TMB // RB, blk)

    return pl.pallas_call(
        body, name="prep_h", grid=(T // TMB,),
        out_shape=jax.ShapeDtypeStruct((T, D), BF16),
        in_specs=[pl.BlockSpec((TMB, D), lambda i: (i, 0)), pl.BlockSpec((3, D), lambda i: (0, 0)),
                  pl.BlockSpec((1, D), lambda i: (0, 0))],
        out_specs=pl.BlockSpec((TMB, D), lambda i: (i, 0)),
        compiler_params=_cp(("parallel",)),
    )(x, mod, g_pre)


AG_ORDER = (0, 1, 4, 5, 2, 3, 6, 7)


def _in_proj(h, w_in, order):
    T = h.shape[0]
    nb = T // TMB

    def body(order_ref, h_ref, w_ref, p_ref, wing_ref, wbuf, send_sems, recv_sems, out_sems):
        j = pl.program_id(0)
        i = pl.program_id(1)
        x, y, c = _me()
        me = (x, y, c)
        sibling = (x, y, 1 - c)
        chips = [(1 - x, y), (x, 1 - y), (1 - x, 1 - y)]
        blocks = [me, sibling] + [(*chip, cc) for chip in chips for cc in (c, 1 - c)]
        recv_k = [None, 0, 1, 4, 2, 5, 3, 6]

        def copy(k, block, to):
            d = wbuf.at[_slot(*block)]
            return pltpu.make_async_remote_copy(src_ref=d, dst_ref=d, send_sem=send_sems.at[k], recv_sem=recv_sems.at[k],
                                                device_id=to, device_id_type=MESH)

        def writeback(jj):
            s = _slot(*blocks[jj])
            return pltpu.make_async_copy(wbuf.at[s], wing_ref.at[s], out_sems.at[jj])

        first = [copy(0, me, sibling)] + [copy(1 + q, me, (*chip, c)) for q, chip in enumerate(chips)]
        passed = [copy(4 + q, (*chip, c), sibling) for q, chip in enumerate(chips)]

        @pl.when((j == 0) & (i == 0))
        def _():
            wbuf[_slot(*me)] = w_ref[...].astype(BF16)
            for cp in first:
                cp.start()
            writeback(0).start()

        for jj in range(1, NSEG):
            @pl.when((j == jj) & (i == 0))
            def _(jj=jj):
                copy(recv_k[jj], blocks[jj], me).wait_recv()
                if jj in (2, 4, 6):
                    passed[jj // 2 - 1].start()
                writeback(jj).start()

        p_ref[...] = jnp.dot(h_ref[...], wbuf[order_ref[j]], preferred_element_type=F32)

        @pl.when((j == NSEG - 1) & (i == nb - 1))
        def _():
            for cp in first + passed:
                cp.wait_send()
            for jj in range(NSEG):
                writeback(jj).wait()

    return pl.pallas_call(
        body, name="in_proj",
        out_shape=(jax.ShapeDtypeStruct((T, NSEG * D), F32), jax.ShapeDtypeStruct((NSEG, D, D), BF16)),
        grid_spec=pltpu.PrefetchScalarGridSpec(
            num_scalar_prefetch=1, grid=(NSEG, nb),
            in_specs=[pl.BlockSpec((TMB, D), lambda j, i, o: (i, 0)), pl.BlockSpec(memory_space=VMEM)],
            out_specs=(pl.BlockSpec((TMB, D), lambda j, i, o: (i, o[j])), pl.BlockSpec(memory_space=ANY)),
            scratch_shapes=[VMEM((NSEG, D, D), BF16), pltpu.SemaphoreType.DMA((7,)), pltpu.SemaphoreType.DMA((7,)),
                            pltpu.SemaphoreType.DMA((NSEG,))]),
        compiler_params=_cp(("arbitrary", "arbitrary")),
    )(order, h, w_in)


def _fill_a_ext(i, s_ref, val_ref, glu_ref, hval_ref, hglu_ref):
    ah = hval_ref[...] * _sig(hglu_ref[...])
    s_ref[0, 0:HALO, :] = jnp.where(i > 0, ah, 0.0)

    def fill(rb, carry):
        rows = _rows(rb)
        s_ref[0, pl.ds(pl.multiple_of(HALO + rb * RB, RB), RB), :] = val_ref[rows, :] * _sig(glu_ref[rows, :])
        return carry
    lax.fori_loop(0, TM // RB, fill, 0)


def _halo_prev(seg):
    hb = TM // HALO
    return pl.BlockSpec((HALO, D), lambda i: (jnp.maximum(i * hb - 1, 0), seg))


def _branch_a_fwd(p, cw, conv_b, lg, lb, wq):
    T = p.shape[0]

    def body(val_ref, glu_ref, z_ref, hval_ref, hglu_ref, cw_ref, cb_ref, lg_ref, lb_ref, w_ref,
             yc_ref, ya_ref, s_ref, yp_ref):
        i = pl.program_id(0)
        _fill_a_ext(i, s_ref, val_ref, glu_ref, hval_ref, hglu_ref)
        _shift_copies(s_ref)

        def conv(rb, carry):
            r0 = rb * RBC
            accs = [jnp.broadcast_to(cb_ref[...], (8, D))] * (RBC // 8)
            for k in range(KW):
                o = 2 + k
                w = cw_ref[k]
                accs = [acc + w * s_ref[o % 8, pl.ds(pl.multiple_of(r0 + 8 * (o // 8 + g), 8), 8), :]
                        for g, acc in enumerate(accs)]
            for g, acc in enumerate(accs):
                yc_ref[pl.ds(pl.multiple_of(r0 + 8 * g, 8), 8), :] = acc
            return carry
        lax.fori_loop(0, TM // RBC, conv, 0)

        def post(rb, carry):
            rows = _rows(rb)
            n, _ = _ln_stats(yc_ref[rows, :])
            l = n * lg_ref[...] + lb_ref[...]
            z = z_ref[rows, :]
            yp_ref[rows, :] = ((l * _sig(l)) * (z * _sig(z))).astype(BF16)
            return carry
        _blocks(TM // RB, post)
        ya_ref[...] = jnp.dot(yp_ref[...], w_ref[...], preferred_element_type=F32)

    tile = lambda seg: pl.BlockSpec((TM, D), lambda i: (i, seg))
    row = pl.BlockSpec((1, D), lambda i: (0, 0))
    return pl.pallas_call(
        body, name="branch_a_fwd", grid=(T // TM,),
        out_shape=(jax.ShapeDtypeStruct((T, D), F32), jax.ShapeDtypeStruct((T, D), F32)),
        in_specs=[tile(0), tile(1), tile(2), _halo_prev(0), _halo_prev(1),
                  pl.BlockSpec((KW, 8, D), lambda i: (0, 0, 0)), row, row, row,
                  pl.BlockSpec((None, D, D), lambda i: (0, 0, 0))],
        out_specs=(pl.BlockSpec((TM, D), lambda i: (i, 0)), pl.BlockSpec((TM, D), lambda i: (i, 0))),
        scratch_shapes=[VMEM((8, EXT, D), F32), VMEM((TM, D), BF16)],
        compiler_params=_cp(("parallel",)),
    )(p, p, p, p, p, cw, conv_b, lg, lb, wq)


def _masked_ws(ws_ref, wt_ref):
    tri = lax.broadcasted_iota(jnp.int32, (CHUNK, CHUNK), 0) >= lax.broadcasted_iota(jnp.int32, (CHUNK, CHUNK), 1)
    for h in range(HEADS):
        wt_ref[h] = jnp.where(tri, ws_ref[h], 0.0).astype(BF16)


def _sgu_mix(wt_ref, vl_ref, bst_ref, s_ref):
    for ck in range(TM // CHUNK):
        r = slice(ck * CHUNK, (ck + 1) * CHUNK)
        for h in range(HEADS):
            cs = slice(h * HD, (h + 1) * HD)
            s_ref[r, cs] = jnp.dot(wt_ref[h], vl_ref[r, cs], preferred_element_type=F32) + bst_ref[:, h:h + 1]


def _branch_b_fwd(p, ws, bst, sg, sb, wq):
    T = p.shape[0]

    def body(pu_ref, pv_ref, pz_ref, ws_ref, bst_ref, sg_ref, sb_ref, w_ref, yb_ref,
             wt_ref, vl_ref, t_ref, s_ref, yp_ref):
        _masked_ws(ws_ref, wt_ref)

        def pre(rb, carry):
            rows = _rows(rb)
            vg, _ = _gelu_parts(pv_ref[rows, :])
            vn, _ = _ln_stats(vg)
            vl_ref[rows, :] = (vn * sg_ref[...] + sb_ref[...]).astype(BF16)
            u, _ = _gelu_parts(pu_ref[rows, :])
            z = pz_ref[rows, :]
            t_ref[rows, :] = u * (z * _sig(z))
            return carry
        _blocks(TM // RB, pre)
        _sgu_mix(wt_ref, vl_ref, bst_ref, s_ref)

        def post(rb, carry):
            rows = _rows(rb)
            yp_ref[rows, :] = (t_ref[rows, :] * s_ref[rows, :]).astype(BF16)
            return carry
        _blocks(TM // RB, post)
        yb_ref[...] = jnp.dot(yp_ref[...], w_ref[...], preferred_element_type=F32)

    tile = lambda seg: pl.BlockSpec((TM, D), lambda i: (i, seg))
    row = pl.BlockSpec((1, D), lambda i: (0, 0))
    return pl.pallas_call(
        body, name="branch_b_fwd", grid=(T // TM,),
        out_shape=jax.ShapeDtypeStruct((T, D), F32),
        in_specs=[tile(3), tile(4), tile(5), pl.BlockSpec((HEADS, CHUNK, CHUNK), lambda i: (0, 0, 0)),
                  pl.BlockSpec((CHUNK, HEADS), lambda i: (0, 0)), row, row,
                  pl.BlockSpec((None, D, D), lambda i: (1, 0, 0))],
        out_specs=pl.BlockSpec((TM, D), lambda i: (i, 0)),
        scratch_shapes=[VMEM((HEADS, CHUNK, CHUNK), BF16), VMEM((TM, D), BF16), VMEM((TM, D), F32),
                        VMEM((TM, D), F32), VMEM((TM, D), BF16)],
        compiler_params=_cp(("parallel",)),
    )(p, p, p, ws, bst, sg, sb, wq)


def _merge_loss(p, ya, yb, x, tgt, mod, g_final, wq):
    T = x.shape[0]
    nt = T // TM

    def body(ga_ref, gb_ref, ya_ref, yb_ref, x_ref, t_ref, mod_ref, gf_ref, w_ref,
             dx2_ref, dya_ref, dyb_ref, dp_ref, gwo_ref, st_ref,
             mrg_s, out_s, dout_s, dm_s, gw_acc):
        i = pl.program_id(0)

        @pl.when(i == 0)
        def _():
            gw_acc[...] = jnp.zeros_like(gw_acc)
            st_ref[...] = jnp.zeros_like(st_ref)

        def merge(rb, carry):
            rows = _rows(rb)
            mrg_s[rows, :] = (_sig(ga_ref[rows, :]) * ya_ref[rows, :] + _sig(gb_ref[rows, :]) * yb_ref[rows, :]).astype(BF16)
            return carry
        lax.fori_loop(0, TM // RB, merge, 0)
        out_s[...] = jnp.dot(mrg_s[...], w_ref[...], preferred_element_type=F32)

        def head(rb, carry):
            loss, gg, dg = carry
            rows = _rows(rb)
            gate = mod_ref[2:3, :]
            gf = gf_ref[...]
            out = out_s[rows, :]
            x2 = x_ref[rows, :] + gate * out
            r2 = lax.rsqrt(jnp.mean(x2 * x2, axis=-1, keepdims=True) + EPS)
            x2n = x2 * r2
            diff = x2n * gf - t_ref[rows, :]
            dy = diff * (1.0 / D)
            dx2n = dy * gf
            dx2 = r2 * (dx2n - x2n * jnp.mean(dx2n * x2n, axis=-1, keepdims=True))
            dx2_ref[rows, :] = dx2
            dout_s[rows, :] = (dx2 * gate).astype(BF16)
            return loss + _fold8(diff * diff), gg + _fold8(dy * x2n), dg + _fold8(dx2 * out)
        zero = jnp.zeros((8, D), F32)
        loss, gg, dg = _blocks(TM // RB, head, (zero, zero, zero))
        st_ref[0] = st_ref[0] + loss * (0.5 / D)
        st_ref[1] = st_ref[1] + gg
        st_ref[2] = st_ref[2] + dg

        dm_s[...] = lax.dot_general(dout_s[...], w_ref[...], NT, preferred_element_type=F32)
        gw_acc[...] += lax.dot_general(mrg_s[...], dout_s[...], TN, preferred_element_type=F32)

        def split(rb, carry):
            rows = _rows(rb)
            dm = dm_s[rows, :]
            sa = _sig(ga_ref[rows, :])
            sb = _sig(gb_ref[rows, :])
            dya_ref[rows, :] = (dm * sa).astype(BF16)
            dyb_ref[rows, :] = (dm * sb).astype(BF16)
            dp_ref[0, rows, :] = (dm * ya_ref[rows, :] * (sa * (1.0 - sa))).astype(BF16)
            dp_ref[1, rows, :] = (dm * yb_ref[rows, :] * (sb * (1.0 - sb))).astype(BF16)
            return carry
        lax.fori_loop(0, TM // RB, split, 0)

        @pl.when(i == nt - 1)
        def _():
            gwo_ref[...] = gw_acc[...].astype(BF16)

    tile = pl.BlockSpec((TM, D), lambda i: (i, 0))
    return pl.pallas_call(
        body, name="merge_loss", grid=(nt,),
        out_shape=(jax.ShapeDtypeStruct((T, D), F32), jax.ShapeDtypeStruct((T, D), BF16), jax.ShapeDtypeStruct((T, D), BF16),
                   jax.ShapeDtypeStruct((NSEG, T, D), BF16), jax.ShapeDtypeStruct((D, D), BF16),
                   jax.ShapeDtypeStruct((3, 8, D), F32)),
        in_specs=[pl.BlockSpec((TM, D), lambda i: (i, 6)), pl.BlockSpec((TM, D), lambda i: (i, 7)), tile, tile, tile, tile,
                  pl.BlockSpec((3, D), lambda i: (0, 0)), pl.BlockSpec((1, D), lambda i: (0, 0)),
                  pl.BlockSpec((None, D, D), lambda i: (2, 0, 0))],
        out_specs=(tile, tile, tile, pl.BlockSpec((2, TM, D), lambda i: (3, i, 0)),
                   pl.BlockSpec((D, D), lambda i: (0, 0)), pl.BlockSpec((3, 8, D), lambda i: (0, 0, 0))),
        scratch_shapes=[VMEM((TM, D), BF16), VMEM((TM, D), F32), VMEM((TM, D), BF16), VMEM((TM, D), F32), VMEM((D, D), F32)],
        compiler_params=_cp(("arbitrary",)),
    )(p, p, ya, yb, x, tgt, mod, g_final, wq)


def _branch_b_bwd(dp, dyb, p, ws, bst, sg, sb, wq):
    T = p.shape[0]
    nt = T // TM

    def body(dp_in, dyb_ref, pu_ref, pv_ref, pz_ref, ws_ref, bst_ref, sg_ref, sb_ref, w_ref,
             dp_ref, gw_ref, gws_ref, gbs_ref, st_ref,
             wt_ref, d_s, vl_s, vn_s, gpv_s, rs_s, s_s, ds_s, ds32_s, yp_s, dvl_s, gw_acc):
        del dp_in
        i = pl.program_id(0)

        @pl.when(i == 0)
        def _():
            gw_acc[...] = jnp.zeros_like(gw_acc)
            gws_ref[...] = jnp.zeros_like(gws_ref)
            gbs_ref[...] = jnp.zeros_like(gbs_ref)
            st_ref[...] = jnp.zeros_like(st_ref)
            _masked_ws(ws_ref, wt_ref)

        d_s[...] = lax.dot_general(dyb_ref[...], w_ref[...], NT, preferred_element_type=F32)

        def pre(rb, carry):
            rows = _rows(rb)
            vg, gpv = _gelu_parts(pv_ref[rows, :])
            vn, rs = _ln_stats(vg)
            vl_s[rows, :] = (vn * sg_ref[...] + sb_ref[...]).astype(BF16)
            vn_s[rows, :] = vn
            gpv_s[rows, :] = gpv
            rs_s[rows, :] = rs
            return carry
        _blocks(TM // RB, pre)
        _sgu_mix(wt_ref, vl_s, bst_ref, s_s)

        def mid(rb, carry):
            rows = _rows(rb)
            u, gpu = _gelu_parts(pu_ref[rows, :])
            z = pz_ref[rows, :]
            sz = _sig(z)
            siluz = z * sz
            d = d_s[rows, :]
            s = s_s[rows, :]
            t = u * siluz
            yp_s[rows, :] = (t * s).astype(BF16)
            dp_ref[0, rows, :] = (d * s * siluz * gpu).astype(BF16)
            ds = d * t
            ds32_s[rows, :] = ds
            ds_s[rows, :] = ds.astype(BF16)
            dp_ref[2, rows, :] = (d * u * s * (sz * (1.0 + z * (1.0 - sz)))).astype(BF16)
            return carry
        lax.fori_loop(0, TM // RB, mid, 0)

        for ck in range(TM // CHUNK):
            r = slice(ck * CHUNK, (ck + 1) * CHUNK)
            for h in range(HEADS):
                cs = slice(h * HD, (h + 1) * HD)
                dsh = ds_s[r, cs]
                dvl_s[r, cs] = lax.dot_general(wt_ref[h], dsh, TN, preferred_element_type=F32)
                gws_ref[h] += lax.dot_general(dsh, vl_s[r, cs], NT, preferred_element_type=F32)
                gbs_ref[h] += ds32_s[r, cs]

        def post(rb, carry):
            g_sg, g_sb = carry
            rows = _rows(rb)
            dvl = dvl_s[rows, :]
            vn = vn_s[rows, :]
            dvg = _ln_bwd(dvl * sg_ref[...], vn, rs_s[rows, :])
            dp_ref[1, rows, :] = (dvg * gpv_s[rows, :]).astype(BF16)
            return g_sg + _fold8(dvl * vn), g_sb + _fold8(dvl)
        zero = jnp.zeros((8, D), F32)
        g_sg, g_sb = _blocks(TM // RB, post, (zero, zero))
        st_ref[0] = st_ref[0] + g_sg
        st_ref[1] = st_ref[1] + g_sb

        gw_acc[...] += lax.dot_general(yp_s[...], dyb_ref[...], TN, preferred_element_type=F32)

        @pl.when(i == nt - 1)
        def _():
            gw_ref[...] = gw_acc[...].astype(BF16)

    tile = lambda seg: pl.BlockSpec((TM, D), lambda i: (i, seg))
    row = pl.BlockSpec((1, D), lambda i: (0, 0))
    hh = pl.BlockSpec((HEADS, CHUNK, CHUNK), lambda i: (0, 0, 0))
    return pl.pallas_call(
        body, name="branch_b_bwd", grid=(nt,),
        out_shape=(jax.ShapeDtypeStruct((NSEG, T, D), BF16), jax.ShapeDtypeStruct((D, D), BF16),
                   jax.ShapeDtypeStruct((HEADS, CHUNK, CHUNK), F32), jax.ShapeDtypeStruct((HEADS, CHUNK, HD), F32),
                   jax.ShapeDtypeStruct((2, 8, D), F32)),
        in_specs=[pl.BlockSpec(memory_space=ANY), pl.BlockSpec((TM, D), lambda i: (i, 0)), tile(3), tile(4), tile(5),
                  hh, pl.BlockSpec((CHUNK, HEADS), lambda i: (0, 0)), row, row,
                  pl.BlockSpec((None, D, D), lambda i: (1, 0, 0))],
        out_specs=(pl.BlockSpec((3, TM, D), lambda i: (1, i, 0)), pl.BlockSpec((D, D), lambda i: (0, 0)), hh, hh,
                   pl.BlockSpec((2, 8, D), lambda i: (0, 0, 0))),
        scratch_shapes=[VMEM((HEADS, CHUNK, CHUNK), BF16), VMEM((TM, D), F32), VMEM((TM, D), BF16), VMEM((TM, D), F32),
                        VMEM((TM, D), F32), VMEM((TM, 1), F32), VMEM((TM, D), F32), VMEM((TM, D), BF16), VMEM((TM, D), F32),
                        VMEM((TM, D), BF16), VMEM((TM, D), F32), VMEM((D, D), F32)],
        input_output_aliases={0: 0},
        compiler_params=_cp(("arbitrary",)),
    )(dp, dyb, p, p, p, ws, bst, sg, sb, wq)


def _branch_a_bwd1(dp, dya, yc, p, lg, lb, wq):
    T = p.shape[0]
    nt = T // TM

    def body(dp_in, dya_ref, yc_ref, z_ref, lg_ref, lb_ref, w_ref, dp_ref, dyc_ref, gw_ref, st_ref,
             d_s, yp_s, gw_acc):
        del dp_in
        i = pl.program_id(0)

        @pl.when(i == 0)
        def _():
            gw_acc[...] = jnp.zeros_like(gw_acc)
            st_ref[...] = jnp.zeros_like(st_ref)

        d_s[...] = lax.dot_general(dya_ref[...], w_ref[...], NT, preferred_element_type=F32)

        def blk(rb, carry):
            g_lg, g_lb, g_cb = carry
            rows = _rows(rb)
            n, rs = _ln_stats(yc_ref[rows, :])
            l = n * lg_ref[...] + lb_ref[...]
            sgl = _sig(l)
            sl = l * sgl
            z = z_ref[rows, :]
            sz = _sig(z)
            siluz = z * sz
            d = d_s[rows, :]
            yp_s[rows, :] = (sl * siluz).astype(BF16)
            dp_ref[rows, :] = (d * sl * (sz * (1.0 + z * (1.0 - sz)))).astype(BF16)
            dl = (d * siluz) * (sgl * (1.0 + l * (1.0 - sgl)))
            dyc = _ln_bwd(dl * lg_ref[...], n, rs)
            dyc_ref[rows, :] = dyc
            return g_lg + _fold8(dl * n), g_lb + _fold8(dl), g_cb + _fold8(dyc)
        zero = jnp.zeros((8, D), F32)
        g_lg, g_lb, g_cb = _blocks(TM // RB, blk, (zero, zero, zero))
        st_ref[0] = st_ref[0] + g_lg
        st_ref[1] = st_ref[1] + g_lb
        st_ref[2] = st_ref[2] + g_cb

        gw_acc[...] += lax.dot_general(yp_s[...], dya_ref[...], TN, preferred_element_type=F32)

        @pl.when(i == nt - 1)
        def _():
            gw_ref[...] = gw_acc[...].astype(BF16)

    tile = pl.BlockSpec((TM, D), lambda i: (i, 0))
    row = pl.BlockSpec((1, D), lambda i: (0, 0))
    return pl.pallas_call(
        body, name="branch_a_bwd1", grid=(nt,),
        out_shape=(jax.ShapeDtypeStruct((NSEG, T, D), BF16), jax.ShapeDtypeStruct((T, D), F32),
                   jax.ShapeDtypeStruct((D, D), BF16), jax.ShapeDtypeStruct((3, 8, D), F32)),
        in_specs=[pl.BlockSpec(memory_space=ANY), tile, tile, pl.BlockSpec((TM, D), lambda i: (i, 2)), row, row,
                  pl.BlockSpec((None, D, D), lambda i: (0, 0, 0))],
        out_specs=(pl.BlockSpec((None, TM, D), lambda i: (2, i, 0)), tile, pl.BlockSpec((D, D), lambda i: (0, 0)),
                   pl.BlockSpec((3, 8, D), lambda i: (0, 0, 0))),
        scratch_shapes=[VMEM((TM, D), F32), VMEM((TM, D), BF16), VMEM((D, D), F32)],
        input_output_aliases={0: 0},
        compiler_params=_cp(("arbitrary",)),
    )(dp, dya, yc, p, lg, lb, wq)


def _branch_a_bwd2(dp, dyc, p, cw, gw_co, gw_so, gw_o):
    T = p.shape[0]
    nt = T // TM
    hb = TM // HALO

    def body(dp_in, dyc_ref, hdyc_ref, val_ref, glu_ref, hval_ref, hglu_ref, cw_ref, gco_ref, gso_ref, go_ref,
             dp_ref, gcw_ref, rq_ref, sa_ref, sd_ref, da_s, x_send, x_recv, x_loc):
        del dp_in
        i = pl.program_id(0)
        start_exchange, wait_exchange = _owner_exchange((gco_ref, gso_ref, go_ref), rq_ref, x_send, x_recv, x_loc)

        @pl.when(i == 0)
        def _():
            gcw_ref[...] = jnp.zeros_like(gcw_ref)
            start_exchange()

        _fill_a_ext(i, sa_ref, val_ref, glu_ref, hval_ref, hglu_ref)
        _shift_copies(sa_ref)
        sd_ref[0, 0:TM, :] = dyc_ref[...]
        sd_ref[0, TM:EXT, :] = jnp.where(i < nt - 1, hdyc_ref[...], 0.0)
        _shift_copies(sd_ref)

        def conv_t(rb, carry):
            r0 = rb * RBC
            accs = [jnp.zeros((8, D), F32)] * (RBC // 8)
            for k in range(KW):
                o = KW - 1 - k
                w = cw_ref[k]
                accs = [acc + w * sd_ref[o % 8, pl.ds(pl.multiple_of(r0 + 8 * (o // 8 + g), 8), 8), :]
                        for g, acc in enumerate(accs)]
            for g, acc in enumerate(accs):
                da_s[pl.ds(pl.multiple_of(r0 + 8 * g, 8), 8), :] = acc
            return carry
        lax.fori_loop(0, TM // RBC, conv_t, 0)

        for k0 in range(0, KW, TAPG):
            taps = list(range(k0, min(k0 + TAPG, KW)))

            def tap_group(rb, accs, taps=taps):
                r0 = rb * 16
                d = dyc_ref[pl.ds(pl.multiple_of(r0, 16), 16), :]
                out = []
                for k, acc in zip(taps, accs):
                    o = 2 + k
                    a = sa_ref[o % 8, pl.ds(pl.multiple_of(r0 + 8 * (o // 8), 8), 16), :]
                    out.append(acc + _fold8(d * a))
                return tuple(out)
            sums = lax.fori_loop(0, TM // 16, tap_group, tuple(jnp.zeros((8, D), F32) for _ in taps))
            for k, s in zip(taps, sums):
                gcw_ref[k] = gcw_ref[k] + s

        def glu_b(rb, carry):
            rows = _rows(rb)
            da = da_s[rows, :]
            sg = _sig(glu_ref[rows, :])
            dp_ref[0, rows, :] = (da * sg).astype(BF16)
            dp_ref[1, rows, :] = (da * val_ref[rows, :] * (sg * (1.0 - sg))).astype(BF16)
            return carry
        lax.fori_loop(0, TM // RB, glu_b, 0)

        @pl.when(i == nt - 1)
        def _():
            wait_exchange()

    tile = lambda seg: pl.BlockSpec((TM, D), lambda i: (i, seg))
    hbm = pl.BlockSpec(memory_space=ANY)
    return pl.pallas_call(
        body, name="branch_a_bwd2", grid=(nt,),
        out_shape=(jax.ShapeDtypeStruct((NSEG, T, D), BF16), jax.ShapeDtypeStruct((32, 8, D), F32),
                   jax.ShapeDtypeStruct((NDEV, 3, 128, D), BF16)),
        in_specs=[hbm, pl.BlockSpec((TM, D), lambda i: (i, 0)),
                  pl.BlockSpec((HALO, D), lambda i: (jnp.minimum((i + 1) * hb, nt * hb - 1), 0)),
                  tile(0), tile(1), _halo_prev(0), _halo_prev(1), pl.BlockSpec((KW, 8, D), lambda i: (0, 0, 0)),
                  hbm, hbm, hbm],
        out_specs=(pl.BlockSpec((2, TM, D), lambda i: (0, i, 0)), pl.BlockSpec((32, 8, D), lambda i: (0, 0, 0)), hbm),
        scratch_shapes=[VMEM((8, EXT, D), F32), VMEM((8, EXT, D), F32), VMEM((TM, D), F32),
                        pltpu.SemaphoreType.DMA((3, 7)), pltpu.SemaphoreType.DMA((3, 7)), pltpu.SemaphoreType.DMA((3,))],
        input_output_aliases={0: 0},
        compiler_params=_cp(("arbitrary",)),
    )(dp, dyc, dyc, p, p, p, p, cw, gw_co, gw_so, gw_o)


def _in_bwd_dx(dp, wing, x, dx2, mod, g_pre):
    T = x.shape[0]
    nt = T // TM

    def body(dp_ref, w_ref, x_ref, dx2_ref, mod_ref, g_ref, gx_ref, st_ref, dh_s):
        i = pl.program_id(0)

        @pl.when(i == 0)
        def _():
            st_ref[...] = jnp.zeros_like(st_ref)

        dh_s[...] = lax.dot_general(dp_ref[0], w_ref[0], NT, preferred_element_type=F32)
        for j in range(1, NSEG):
            dh_s[...] += lax.dot_general(dp_ref[j], w_ref[j], NT, preferred_element_type=F32)

        def blk(rb, carry):
            d_sh, d_sc, g_g = carry
            rows = _rows(rb)
            xv = x_ref[rows, :]
            r = lax.rsqrt(jnp.mean(xv * xv, axis=-1, keepdims=True) + EPS)
            xn = xv * r
            g = g_ref[...]
            hpre = xn * g
            dh = dh_s[rows, :]
            dhp = dh * (1.0 + mod_ref[1:2, :])
            dxn = dhp * g
            gx_ref[rows, :] = dx2_ref[rows, :] + r * (dxn - xn * jnp.mean(dxn * xn, axis=-1, keepdims=True))
            return d_sh + _fold8(dh), d_sc + _fold8(dh * hpre), g_g + _fold8(dhp * xn)
        zero = jnp.zeros((8, D), F32)
        d_sh, d_sc, g_g = _blocks(TM // RB, blk, (zero, zero, zero))
        st_ref[0] = st_ref[0] + d_sh
        st_ref[1] = st_ref[1] + d_sc
        st_ref[2] = st_ref[2] + g_g

    tile = pl.BlockSpec((TM, D), lambda i: (i, 0))
    return pl.pallas_call(
        body, name="in_bwd_dx", grid=(nt,),
        out_shape=(jax.ShapeDtypeStruct((T, D), F32), jax.ShapeDtypeStruct((3, 8, D), F32)),
        in_specs=[pl.BlockSpec((NSEG, TM, D), lambda i: (0, i, 0)),
                  pl.BlockSpec((NSEG, D, D), lambda i: (0, 0, 0), pipeline_mode=pl.Buffered(1)),
                  tile, tile, pl.BlockSpec((3, D), lambda i: (0, 0)), pl.BlockSpec((1, D), lambda i: (0, 0))],
        out_specs=(tile, pl.BlockSpec((3, 8, D), lambda i: (0, 0, 0))),
        scratch_shapes=[VMEM((TM, D), F32)],
        compiler_params=_cp(("arbitrary",), vmem_mb=56),
    )(dp, wing, x, dx2, mod, g_pre)


RS_ORDER = (5, 4, 3, 2, 7, 6, 1, 0)


def _in_bwd_dw(h, dp, order):
    T = h.shape[0]
    tmw = min(2 * TMB, T)
    nb = T // tmw

    def body(order_ref, h_ref, dp_ref, g_ref, acc, sbuf, rsib, rici, send_sems, recv_sems):
        j = pl.program_id(0)
        i = pl.program_id(1)
        x, y, c = _me()
        sibling = (x, y, 1 - c)
        chips = [(1 - x, y), (x, 1 - y), (1 - x, 1 - y)]

        def to_sibling(q):
            return pltpu.make_async_remote_copy(src_ref=sbuf.at[q], dst_ref=rsib.at[q], send_sem=send_sems.at[q],
                                                recv_sem=recv_sems.at[q], device_id=sibling, device_id_type=MESH)

        def to_owner(q):
            return pltpu.make_async_remote_copy(src_ref=sbuf.at[4 + q], dst_ref=rici.at[q], send_sem=send_sems.at[4 + q],
                                                recv_sem=recv_sems.at[4 + q], device_id=(*chips[q], c), device_id_type=MESH)

        @pl.when(i == 0)
        def _():
            acc[...] = jnp.zeros_like(acc)
        acc[...] += lax.dot_general(h_ref[...], dp_ref[...], TN, preferred_element_type=F32)

        for q in range(4):
            @pl.when((j == 2 * q) & (i == nb - 1))
            def _(q=q):
                sbuf[q] = acc[...].astype(BF16)
                to_sibling(q).start()

        for q in range(3):
            @pl.when((j == 2 * q + 1) & (i == nb - 1))
            def _(q=q):
                to_sibling(q).wait_recv()
                sbuf[4 + q] = (acc[...] + rsib[q].astype(F32)).astype(BF16)
                to_owner(q).start()

        @pl.when((j == NSEG - 1) & (i == nb - 1))
        def _():
            to_sibling(3).wait_recv()
            g = acc[...] + rsib[3].astype(F32)
            for q in range(3):
                to_owner(q).wait_recv()
                g = g + rici[q].astype(F32)
            g_ref[...] = g
            for q in range(4):
                to_sibling(q).wait_send()
            for q in range(3):
                to_owner(q).wait_send()

    return pl.pallas_call(
        body, name="in_bwd_dw",
        out_shape=jax.ShapeDtypeStruct((D, D), F32),
        grid_spec=pltpu.PrefetchScalarGridSpec(
            num_scalar_prefetch=1, grid=(NSEG, nb),
            in_specs=[pl.BlockSpec((tmw, D), lambda j, i, o: (i, 0)),
                      pl.BlockSpec((None, tmw, D), lambda j, i, o: (o[j], i, 0))],
            out_specs=pl.BlockSpec((D, D), lambda j, i, o: (0, 0)),
            scratch_shapes=[VMEM((D, D), F32), VMEM((7, D, D), BF16), VMEM((4, D, D), BF16), VMEM((3, D, D), BF16),
                            pltpu.SemaphoreType.DMA((7,)), pltpu.SemaphoreType.DMA((7,))]),
        compiler_params=_cp(("arbitrary", "arbitrary"), vmem_mb=56),
    )(order, h, dp)


def _pack_smalls(st_x, st_a, st_b, st_m, gcw, gws, gbs):
    def body(sx_ref, sa_ref, sb_ref, sm_ref, gcw_ref, gws_ref, gbs_ref, out_ref):
        out_ref[...] = jnp.zeros_like(out_ref)
        fold = lambda v: jnp.sum(v, axis=0, keepdims=True)
        out_ref[0:1, :] = fold(sx_ref[0])
        out_ref[1:2, :] = fold(sx_ref[1])
        out_ref[2:3, :] = fold(sm_ref[2])
        out_ref[3:4, :] = fold(sx_ref[2])
        out_ref[4:5, :] = fold(sa_ref[2])
        out_ref[5:6, :] = fold(sa_ref[0])
        out_ref[6:7, :] = fold(sa_ref[1])
        out_ref[7:8, :] = fold(sb_ref[0])
        out_ref[8:9, :] = fold(sb_ref[1])
        out_ref[10:11, :] = fold(sm_ref[1])
        out_ref[11:12, :] = fold(sm_ref[0])
        for k in range(KW):
            out_ref[ROW_CW + k:ROW_CW + k + 1, :] = fold(gcw_ref[k])
        tri = lax.broadcasted_iota(jnp.int32, (CHUNK, CHUNK), 0) >= lax.broadcasted_iota(jnp.int32, (CHUNK, CHUNK), 1)
        for h in range(HEADS):
            out_ref[ROW_WS:ROW_WS + CHUNK, h * CHUNK:(h + 1) * CHUNK] = jnp.where(tri, gws_ref[h], 0.0)
            out_ref[9:10, h * CHUNK:(h + 1) * CHUNK] = fold(gbs_ref[h].T)

    vm = pl.BlockSpec(memory_space=VMEM)
    return pl.pallas_call(
        body, name="pack_smalls", out_shape=jax.ShapeDtypeStruct((NSMALL, D), F32),
        in_specs=[vm] * 7, out_specs=vm, compiler_params=_cp(vmem_mb=32),
    )(st_x, st_a, st_b, st_m, gcw, gws, gbs)


def _adam(w, g, m, v):
    m2 = ADAM_B1 * m + (1.0 - ADAM_B1) * g
    v2 = ADAM_B2 * v + (1.0 - ADAM_B2) * (g * g)
    m_hat = m2 / (1.0 - ADAM_B1 ** ADAM_STEP)
    v_hat = v2 / (1.0 - ADAM_B2 ** ADAM_STEP)
    delta = -ADAM_LR * (m_hat / (jnp.sqrt(v_hat) + ADAM_EPS) + ADAM_WD * w)
    return delta, m2, v2


def _small_finish(smg, w, m, v):
    def body(smg_ref, w_ref, m_ref, v_ref, loss_ref, g_ref, d_ref, m2_ref, v2_ref):
        g = smg_ref[0]
        for s in range(1, NDEV):
            g = g + smg_ref[s]
        g_ref[...] = g
        loss_ref[...] = jnp.sum(g[11:12, :], axis=1, keepdims=True)
        d_ref[...], m2_ref[...], v2_ref[...] = _adam(w_ref[...], g, m_ref[...], v_ref[...])

    vm = pl.BlockSpec(memory_space=VMEM)
    sd = jax.ShapeDtypeStruct((NSMALL, D), F32)
    return pl.pallas_call(
        body, name="small_finish", out_shape=(jax.ShapeDtypeStruct((1, 1), F32), sd, sd, sd, sd),
        in_specs=[vm] * 4, out_specs=(vm,) * 5, compiler_params=_cp(vmem_mb=32),
    )(smg, w, m, v)


def _ada_grad_adam(ct, dm, w, m, v):
    def body(ct_ref, dm_ref, w_ref, m_ref, v_ref, g_ref, d_ref, m2_ref, v2_ref):
        g = ct_ref[:, 0:1] * dm_ref[0:1, :]
        for b in range(1, NDEV):
            g = g + ct_ref[:, b:b + 1] * dm_ref[b:b + 1, :]
        g_ref[...] = g
        d_ref[...], m2_ref[...], v2_ref[...] = _adam(w_ref[...], g, m_ref[...], v_ref[...])

    vm = pl.BlockSpec(memory_space=VMEM)
    sd = jax.ShapeDtypeStruct(w.shape, F32)
    return pl.pallas_call(
        body, name="ada_grad_adam", out_shape=(sd, sd, sd, sd),
        in_specs=[vm] * 5, out_specs=(vm,) * 4, compiler_params=_cp(vmem_mb=32),
    )(ct, dm, w, m, v)


def _adam_f32(name, g, w, m, v, rows=None):
    R, C = w.shape
    rows = R if rows is None else rows

    def body(g_ref, w_ref, m_ref, v_ref, d_ref, m2_ref, v2_ref):
        d_ref[...], m2_ref[...], v2_ref[...] = _adam(w_ref[...], g_ref[...], m_ref[...], v_ref[...])

    tile = pl.BlockSpec((rows, C), lambda i: (i, 0))
    sd = jax.ShapeDtypeStruct(w.shape, F32)
    return pl.pallas_call(
        body, name=name, grid=(R // rows,), out_shape=(sd, sd, sd), in_specs=[tile] * 4, out_specs=(tile,) * 3,
        compiler_params=_cp(("parallel",), vmem_mb=32),
    )(g, w, m, v)


def _adam_reduce(name, recv, w, m, v, rows, recv_spec):
    R, C = w.shape

    def body(r_ref, w_ref, m_ref, v_ref, g_ref, d_ref, m2_ref, v2_ref):
        g = r_ref[0].astype(F32)
        for s in range(1, NDEV):
            g = g + r_ref[s].astype(F32)
        g_ref[...] = g
        d_ref[...], m2_ref[...], v2_ref[...] = _adam(w_ref[...], g, m_ref[...], v_ref[...])

    tile = pl.BlockSpec((rows, C), lambda i: (i, 0))
    sd = jax.ShapeDtypeStruct((R, C), F32)
    return pl.pallas_call(
        body, name=name, grid=(R // rows,), out_shape=(sd, sd, sd, sd),
        in_specs=[recv_spec, tile, tile, tile], out_specs=(tile,) * 4,
        compiler_params=_cp(("parallel",)),
    )(recv, w, m, v)


def kernel(x, c, w_ada, b_ada, g_pre, w_in, conv_w, conv_b, conv_ln_g, conv_ln_b, w_conv_out, sgu_ln_g, sgu_ln_b, w_sgu, b_sgu, w_sgu_out, w_o, g_final, loss_target, m_w_ada, m_b_ada, m_g_pre, m_w_in, m_conv_w, m_conv_b, m_conv_ln_g, m_conv_ln_b, m_w_conv_out, m_sgu_ln_g, m_sgu_ln_b, m_w_sgu, m_b_sgu, m_w_sgu_out, m_w_o, m_g_final, v_w_ada, v_b_ada, v_g_pre, v_w_in, v_conv_w, v_conv_b, v_conv_ln_g, v_conv_ln_b, v_w_conv_out, v_sgu_ln_g, v_sgu_ln_b, v_w_sgu, v_b_sgu, v_w_sgu_out, v_w_o, v_g_final):
    T = x.shape[1]
    assert T % TMB == 0 and x.shape[2] == D
    xs = x[0]
    tgt = loss_target[0]
    my = 4 * lax.axis_index("x") + 2 * lax.axis_index("y") + lax.axis_index("c")

    cg, wqg, cwg = _gather_params(c, w_conv_out[0], w_sgu_out[0], w_o[0], conv_w[0])
    wq = wqg.reshape(3, D, D)
    cw = jnp.broadcast_to(jnp.transpose(cwg, (1, 0, 2)).reshape(KW, 1, D), (KW, 8, D))
    mod = _ada_fwd(cg.reshape(NDEV, D), w_ada[0], b_ada)
    gf = g_final.reshape(1, D)
    bst = b_sgu[0].T

    h = _prep_h(xs, mod, g_pre)
    p, wing = _in_proj(h, w_in[0], my ^ jnp.array(AG_ORDER, jnp.int32))
    yc, ya = _branch_a_fwd(p, cw, conv_b, conv_ln_g, conv_ln_b, wq)
    yb = _branch_b_fwd(p, w_sgu[0], bst, sgu_ln_g, sgu_ln_b, wq)
    dx2, dya, dyb, dp, gw_o, st_m = _merge_loss(p, ya, yb, xs, tgt, mod, gf, wq)

    dp, gw_so, gws, gbs, st_b = _branch_b_bwd(dp, dyb, p, w_sgu[0], bst, sgu_ln_g, sgu_ln_b, wq)
    dp, dyc, gw_co, st_a = _branch_a_bwd1(dp, dya, yc, p, conv_ln_g, conv_ln_b, wq)
    dp, gcw, rq = _branch_a_bwd2(dp, dyc, p, cw, gw_co, gw_so, gw_o)
    g_in = _in_bwd_dw(h, dp, my ^ jnp.array(RS_ORDER, jnp.int32))
    grad_x, st_x = _in_bwd_dx(dp, wing, xs, dx2, mod, g_pre)

    smalls = _pack_smalls(st_x, st_a, st_b, st_m, gcw, gws, gbs)
    smg = _gather_smalls(smalls)

    def pack(b_ada_, g_pre_, conv_b_, lg_, lb_, sg_, sb_, b_sgu_, gfin_, w_sgu_):
        top = jnp.concatenate([b_ada_.reshape(3, D), g_pre_, conv_b_, lg_, lb_, sg_, sb_, b_sgu_.reshape(1, D),
                               gfin_.reshape(1, D), jnp.zeros((ROW_WS - 11, D), F32)], axis=0)
        return jnp.concatenate([top, jnp.transpose(w_sgu_[0], (1, 0, 2)).reshape(CHUNK, D)], axis=0)

    pw = pack(b_ada, g_pre, conv_b, conv_ln_g, conv_ln_b, sgu_ln_g, sgu_ln_b, b_sgu, g_final, w_sgu)
    pm = pack(m_b_ada, m_g_pre, m_conv_b, m_conv_ln_g, m_conv_ln_b, m_sgu_ln_g, m_sgu_ln_b, m_b_sgu, m_g_final, m_w_sgu)
    pv = pack(v_b_ada, v_g_pre, v_conv_b, v_conv_ln_g, v_conv_ln_b, v_sgu_ln_g, v_sgu_ln_b, v_b_sgu, v_g_final, v_w_sgu)
    loss11, sg_, sd_, sm_, sv_ = _small_finish(smg, pw, pm, pv)

    def unpack(a):
        return dict(
            b_ada=a[0:3].reshape(1, 3 * D), g_pre=a[3:4], conv_b=a[4:5], conv_ln_g=a[5:6], conv_ln_b=a[6:7],
            sgu_ln_g=a[7:8], sgu_ln_b=a[8:9], b_sgu=a[9:10].reshape(1, HEADS, CHUNK), g_final=a[10],
            w_sgu=jnp.transpose(a[ROW_WS:ROW_WS + CHUNK].reshape(CHUNK, HEADS, CHUNK), (1, 0, 2))[None])
    small = [unpack(a) for a in (sg_, sd_, sm_, sv_)]

    g_cw = lax.dynamic_slice_in_dim(sg_[ROW_CW:ROW_CW + KW], my * 128, 128, axis=1)
    d_cw, m_cw, v_cw = _adam_f32("adam_conv_w", g_cw, conv_w[0], m_conv_w[0], v_conv_w[0])

    wcols = w_ada.shape[2]
    dm_all = smg[:, 0:3, :].reshape(NDEV, 3 * D)
    dm_mine = lax.dynamic_slice_in_dim(dm_all, my * wcols, wcols, axis=1)
    g_ada, d_ada, m_ada, v_ada = _ada_grad_adam(cg.reshape(NDEV, D).T, dm_mine, w_ada[0], m_w_ada[0], v_w_ada[0])

    d_in, m_in, v_in = _adam_f32("adam_w_in", g_in, w_in[0], m_w_in[0], v_w_in[0], rows=256)
    big = {}
    for j, (nm, w_, m_, v_) in enumerate((("w_conv_out", w_conv_out, m_w_conv_out, v_w_conv_out),
                                          ("w_sgu_out", w_sgu_out, m_w_sgu_out, v_w_sgu_out),
                                          ("w_o", w_o, m_w_o, v_w_o))):
        big[nm] = _adam_reduce("adam_" + nm, rq, w_[0], m_[0], v_[0], 128,
                               pl.BlockSpec((NDEV, None, 128, D), lambda i, j=j: (0, j, 0, 0)))

    per = {
        "w_ada": tuple(a[None] for a in (g_ada, d_ada, m_ada, v_ada)),
        "w_in": tuple(a[None] for a in (g_in, d_in, m_in, v_in)),
        "conv_w": tuple(a[None] for a in (g_cw, d_cw, m_cw, v_cw)),
    }
    for nm in ("w_conv_out", "w_sgu_out", "w_o"):
        per[nm] = tuple(a[None] for a in big[nm])
    for nm in ("b_ada", "g_pre", "conv_b", "conv_ln_g", "conv_ln_b", "sgu_ln_g", "sgu_ln_b", "w_sgu", "b_sgu", "g_final"):
        per[nm] = tuple(s[nm] for s in small)

    order = ["w_ada", "b_ada", "g_pre", "w_in", "conv_w", "conv_b", "conv_ln_g", "conv_ln_b", "w_conv_out",
             "sgu_ln_g", "sgu_ln_b", "w_sgu", "b_sgu", "w_sgu_out", "w_o", "g_final"]
    outs = [loss11.reshape(()), grad_x[None]]
    for part in range(4):
        outs += [per[nm][part] for nm in order]
    return tuple(outs)
```

```python
import jax
import jax.numpy as jnp
from jax import lax
from jax.experimental import pallas as pl
from jax.experimental.pallas import tpu as pltpu

F32 = jnp.float32
BF16 = jnp.bfloat16
MESH = pl.DeviceIdType.MESH
VMEM = pltpu.VMEM
ANY = pl.ANY

D = 1024
NDEV = 8
NSEG = 8
HEADS = 8
HD = 128
CHUNK = 128
KW = 31
HALO = 32
EPS = 1e-6
TM = 256
TMB = 512
RB = 32
UNROLL = 4
RBC = 32
TAPG = 4
EXT = TM + HALO
NSMALL = 176
ROW_CW = 16
ROW_WS = 48

ADAM_LR = 0.001
ADAM_B1 = 0.9
ADAM_B2 = 0.999
ADAM_EPS = 1e-08
ADAM_WD = 0.01
ADAM_STEP = 10

INV_SQRT2 = 0.7071067811865476
INV_SQRT_2PI = 0.3989422804014327

NT = (((1,), (1,)), ((), ()))
TN = (((0,), (0,)), ((), ()))


def _cp(sem=None, vmem_mb=48):
    return pltpu.CompilerParams(dimension_semantics=sem, vmem_limit_bytes=vmem_mb * 1024 * 1024)


def _me():
    return lax.axis_index("x"), lax.axis_index("y"), lax.axis_index("c")


def _slot(px, py, pc):
    return 4 * px + 2 * py + pc


def _xor_peer(k):
    x, y, c = _me()
    return (1 - x if k & 4 else x, 1 - y if k & 2 else y, 1 - c if k & 1 else c)


def _allgather_phases(srcs, dst_at, send_sems, recv_sems, loc_sems):
    x, y, c = _me()
    me = (x, y, c)
    sibling = (x, y, 1 - c)
    chips = [(1 - x, y), (x, 1 - y), (1 - x, 1 - y)]
    na = len(srcs)

    def copy(a, k, block, to, src=None):
        d = dst_at(a, _slot(*block))
        return pltpu.make_async_remote_copy(
            src_ref=d if src is None else src, dst_ref=d,
            send_sem=send_sems.at[a, k], recv_sem=recv_sems.at[a, k],
            device_id=to, device_id_type=MESH)

    local = [pltpu.make_async_copy(srcs[a], dst_at(a, _slot(*me)), loc_sems.at[a]) for a in range(na)]
    first = []
    for a in range(na):
        first.append(copy(a, 0, me, sibling, src=srcs[a]))
        for j, chip in enumerate(chips):
            first.append(copy(a, 1 + j, me, (*chip, c), src=srcs[a]))
    passed = [copy(a, 4 + j, (*chip, c), sibling) for j, chip in enumerate(chips) for a in range(na)]

    def start():
        for cp in local + first:
            cp.start()

    def relay():
        for j, chip in enumerate(chips):
            for a in range(na):
                copy(a, 1 + j, (*chip, c), me).wait_recv()
                passed[j * na + a].start()

    def finish():
        for a in range(na):
            copy(a, 0, sibling, me).wait_recv()
        for j, chip in enumerate(chips):
            for a in range(na):
                copy(a, 4 + j, (*chip, 1 - c), me).wait_recv()
        for cp in first + passed:
            cp.wait_send()
        for cp in local:
            cp.wait()

    return start, relay, finish


def _allgather(srcs, dst_at, send_sems, recv_sems, loc_sems):
    start, relay, finish = _allgather_phases(srcs, dst_at, send_sems, recv_sems, loc_sems)
    start()
    relay()
    finish()


def _owner_exchange(gq, rq_ref, x_send, x_recv, x_loc):
    mx, my_, mc = _me()
    me = _slot(mx, my_, mc)

    def rows_of(a, slot):
        return gq[a].at[pl.ds(pl.multiple_of(slot * 128, 128), 128), :]

    def exchange(k, a, recv):
        px, py, pc = _xor_peer(k)
        peer = _slot(px, py, pc)
        return pltpu.make_async_remote_copy(
            src_ref=rows_of(a, me if recv else peer), dst_ref=rq_ref.at[peer if recv else me, a],
            send_sem=x_send.at[a, k - 1], recv_sem=x_recv.at[a, k - 1],
            device_id=(px, py, pc), device_id_type=MESH)

    local = [pltpu.make_async_copy(rows_of(a, me), rq_ref.at[me, a], x_loc.at[a]) for a in range(len(gq))]

    def start():
        for cp in local:
            cp.start()
        for k in range(1, NDEV):
            for a in range(len(gq)):
                exchange(k, a, False).start()

    def wait():
        for k in range(1, NDEV):
            for a in range(len(gq)):
                exchange(k, a, True).wait_recv()
        for k in range(1, NDEV):
            for a in range(len(gq)):
                exchange(k, a, False).wait_send()
        for cp in local:
            cp.wait()

    return start, wait


def _sig(v):
    return jax.nn.sigmoid(v)


def _gelu_parts(v):
    cdf = 0.5 * (1.0 + lax.erf(v * INV_SQRT2))
    pdf = jnp.exp(-0.5 * v * v) * INV_SQRT_2PI
    return v * cdf, cdf + v * pdf


def _ln_stats(v):
    mu = jnp.mean(v, axis=-1, keepdims=True)
    xc = v - mu
    var = jnp.mean(xc * xc, axis=-1, keepdims=True)
    rs = lax.rsqrt(var + EPS)
    return xc * rs, rs


def _ln_bwd(dn, n, rs):
    return rs * (dn - jnp.mean(dn, axis=-1, keepdims=True) - n * jnp.mean(dn * n, axis=-1, keepdims=True))


def _fold8(v):
    acc = v[0:8]
    for r in range(8, v.shape[0], 8):
        acc = acc + v[r:r + 8]
    return acc


def _rows(rb, n=RB):
    return pl.ds(pl.multiple_of(rb * n, n), n)


def _blocks(n, body, init=0, unroll=UNROLL):
    def trip(t, carry):
        for u in range(unroll):
            carry = body(t * unroll + u, carry)
        return carry
    return lax.fori_loop(0, n // unroll, trip, init)


def _shift_copies(s_ref):
    n = EXT - 8
    for m in range(1, 8):
        for c0 in range(0, n, 56):
            s_ref[m, c0:c0 + 56, :] = s_ref[0, c0 + m:c0 + m + 56, :]


def _ada_fwd(c, w_ada, b_ada):
    wcols = w_ada.shape[1]

    def body(c_ref, w_ref, b_ref, mod_ref, call_ref, cdst, msrc, mdst, c_send, c_recv, m_send, m_recv):
        x, y, c = _me()
        me = _slot(x, y, c)

        def all_to_all(src_of, dst, ss, rs):
            def cp(k, recv):
                px, py, pc = _xor_peer(k)
                peer = _slot(px, py, pc)
                return pltpu.make_async_remote_copy(
                    src_ref=src_of(me if recv else peer), dst_ref=dst.at[peer if recv else me],
                    send_sem=ss.at[k - 1], recv_sem=rs.at[k - 1], device_id=(px, py, pc), device_id_type=MESH)
            for k in range(1, NDEV):
                cp(k, False).start()
            for k in range(1, NDEV):
                cp(k, True).wait_recv()
            for k in range(1, NDEV):
                cp(k, False).wait_send()

        cdst[me] = c_ref[...]
        all_to_all(lambda s: c_ref, cdst, c_send, c_recv)
        for b in range(NDEV):
            call_ref[b:b + 1, :] = cdst[b]
        m = jnp.dot(call_ref[...], w_ref[...], preferred_element_type=F32, precision=lax.Precision.HIGHEST)
        for b in range(NDEV):
            msrc[b] = m[b:b + 1, :]
        mdst[me] = msrc[me]
        all_to_all(lambda s: msrc.at[s], mdst, m_send, m_recv)
        full = jnp.concatenate([mdst[k] for k in range(NDEV)], axis=1) + b_ref[...]
        for r in range(3):
            mod_ref[r:r + 1, :] = full[:, r * D:(r + 1) * D]

    vm = pl.BlockSpec(memory_space=VMEM)
    return pl.pallas_call(
        body, name="ada_fwd",
        out_shape=(jax.ShapeDtypeStruct((3, D), F32), jax.ShapeDtypeStruct((NDEV, D), F32)),
        in_specs=[vm, vm, vm], out_specs=(vm, vm),
        scratch_shapes=[VMEM((NDEV, 1, D), F32), VMEM((NDEV, 1, wcols), F32), VMEM((NDEV, 1, wcols), F32),
                        pltpu.SemaphoreType.DMA((7,)), pltpu.SemaphoreType.DMA((7,)),
                        pltpu.SemaphoreType.DMA((7,)), pltpu.SemaphoreType.DMA((7,))],
        compiler_params=_cp(vmem_mb=32),
    )(c, w_ada, b_ada)


def _prep_h(x, mod, g_pre):
    T = x.shape[0]

    def body(x_ref, mod_ref, g_ref, h_ref):
        def blk(rb, carry):
            rows = _rows(rb)
            xv = x_ref[rows, :]
            r = lax.rsqrt(jnp.mean(xv * xv, axis=-1, keepdims=True) + EPS)
            h_ref[rows, :] = ((xv * r) * g_ref[...] * (1.0 + mod_ref[1:2, :]) + mod_ref[0:1, :]).astype(BF16)
            return carry
        _blocks(TMB // RB, blk)

    return pl.pallas_call(
        body, name="prep_h", grid=(T // TMB,),
        out_shape=jax.ShapeDtypeStruct((T, D), BF16),
        in_specs=[pl.BlockSpec((TMB, D), lambda i: (i, 0)), pl.BlockSpec((3, D), lambda i: (0, 0)),
                  pl.BlockSpec((1, D), lambda i: (0, 0))],
        out_specs=pl.BlockSpec((TMB, D), lambda i: (i, 0)),
        compiler_params=_cp(("parallel",)),
    )(x, mod, g_pre)


AG_ORDER = (0, 1, 4, 5, 2, 3, 6, 7)


def _in_proj(h, w_in, order, w_co, w_so, w_o, conv_w):
    T = h.shape[0]
    nb = T // TMB

    def body(order_ref, h_ref, w_ref, wco_ref, wso_ref, wo_ref, cw_ref, p_ref, wing_ref, wqg_ref, cwg_ref,
             wbuf, wq_st, send_sems, recv_sems, out_sems, ag_send, ag_recv, ag_loc):
        j = pl.program_id(0)
        i = pl.program_id(1)
        x, y, c = _me()
        me = (x, y, c)
        sibling = (x, y, 1 - c)
        chips = [(1 - x, y), (x, 1 - y), (1 - x, 1 - y)]
        blocks = [me, sibling] + [(*chip, cc) for chip in chips for cc in (c, 1 - c)]
        recv_k = [None, 0, 1, 4, 2, 5, 3, 6]
        ag_start, ag_relay, ag_finish = _allgather_phases(
            [wq_st, cw_ref], lambda a, slot: wqg_ref.at[:, slot] if a == 0 else cwg_ref.at[slot], ag_send, ag_recv, ag_loc)

        def copy(k, block, to):
            d = wbuf.at[_slot(*block)]
            return pltpu.make_async_remote_copy(src_ref=d, dst_ref=d, send_sem=send_sems.at[k], recv_sem=recv_sems.at[k],
                                                device_id=to, device_id_type=MESH)

        def writeback(jj):
            s = _slot(*blocks[jj])
            return pltpu.make_async_copy(wbuf.at[s], wing_ref.at[s], out_sems.at[jj])

        first = [copy(0, me, sibling)] + [copy(1 + q, me, (*chip, c)) for q, chip in enumerate(chips)]
        passed = [copy(4 + q, (*chip, c), sibling) for q, chip in enumerate(chips)]

        @pl.when((j == 0) & (i == 0))
        def _():
            wbuf[_slot(*me)] = w_ref[...].astype(BF16)
            for cp in first:
                cp.start()
            writeback(0).start()
            wq_st[0] = wco_ref[...].astype(BF16)
            wq_st[1] = wso_ref[...].astype(BF16)
            wq_st[2] = wo_ref[...].astype(BF16)
            ag_start()

        for jj in range(1, NSEG):
            @pl.when((j == jj) & (i == 0))
            def _(jj=jj):
                copy(recv_k[jj], blocks[jj], me).wait_recv()
                if jj in (2, 4, 6):
                    passed[jj // 2 - 1].start()
                writeback(jj).start()
                if jj == NSEG - 1:
                    ag_relay()

        p_ref[...] = jnp.dot(h_ref[...], wbuf[order_ref[j]], preferred_element_type=F32)

        @pl.when((j == NSEG - 1) & (i == nb - 1))
        def _():
            for cp in first + passed:
                cp.wait_send()
            for jj in range(NSEG):
                writeback(jj).wait()
            ag_finish()

    vm = pl.BlockSpec(memory_space=VMEM)
    hbm = pl.BlockSpec(memory_space=ANY)
    return pl.pallas_call(
        body, name="in_proj",
        out_shape=(jax.ShapeDtypeStruct((T, NSEG * D), F32), jax.ShapeDtypeStruct((NSEG, D, D), BF16),
                   jax.ShapeDtypeStruct((3, NDEV, 128, D), BF16), jax.ShapeDtypeStruct((NDEV, KW, 128), F32)),
        grid_spec=pltpu.PrefetchScalarGridSpec(
            num_scalar_prefetch=1, grid=(NSEG, nb),
            in_specs=[pl.BlockSpec((TMB, D), lambda j, i, o: (i, 0)), vm, vm, vm, vm, vm],
            out_specs=(pl.BlockSpec((TMB, D), lambda j, i, o: (i, o[j])), hbm, hbm, hbm),
            scratch_shapes=[VMEM((NSEG, D, D), BF16), VMEM((3, 128, D), BF16),
                            pltpu.SemaphoreType.DMA((7,)), pltpu.SemaphoreType.DMA((7,)), pltpu.SemaphoreType.DMA((NSEG,)),
                            pltpu.SemaphoreType.DMA((2, 7)), pltpu.SemaphoreType.DMA((2, 7)), pltpu.SemaphoreType.DMA((2,))]),
        compiler_params=_cp(("arbitrary", "arbitrary")),
    )(order, h, w_in, w_co, w_so, w_o, conv_w)


def _fill_a_ext(i, s_ref, val_ref, glu_ref, hval_ref, hglu_ref):
    ah = hval_ref[...] * _sig(hglu_ref[...])
    s_ref[0, 0:HALO, :] = jnp.where(i > 0, ah, 0.0)

    def fill(rb, carry):
        rows = _rows(rb)
        s_ref[0, pl.ds(pl.multiple_of(HALO + rb * RB, RB), RB), :] = val_ref[rows, :] * _sig(glu_ref[rows, :])
        return carry
    lax.fori_loop(0, TM // RB, fill, 0)


def _halo_prev(seg):
    hb = TM // HALO
    return pl.BlockSpec((HALO, D), lambda i: (jnp.maximum(i * hb - 1, 0), seg))


def _branch_a_fwd(p, cw, conv_b, lg, lb, wq):
    T = p.shape[0]

    def body(val_ref, glu_ref, z_ref, hval_ref, hglu_ref, cw_ref, cb_ref, lg_ref, lb_ref, w_ref,
             yc_ref, ya_ref, s_ref, yp_ref):
        i = pl.program_id(0)
        _fill_a_ext(i, s_ref, val_ref, glu_ref, hval_ref, hglu_ref)
        _shift_copies(s_ref)

        def conv(rb, carry):
            r0 = rb * RBC
            accs = [jnp.broadcast_to(cb_ref[...], (8, D))] * (RBC // 8)
            for k in range(KW):
                o = 2 + k
                w = cw_ref[k]
                accs = [acc + w * s_ref[o % 8, pl.ds(pl.multiple_of(r0 + 8 * (o // 8 + g), 8), 8), :]
                        for g, acc in enumerate(accs)]
            for g, acc in enumerate(accs):
                yc_ref[pl.ds(pl.multiple_of(r0 + 8 * g, 8), 8), :] = acc
            return carry
        lax.fori_loop(0, TM // RBC, conv, 0)

        def post(rb, carry):
            rows = _rows(rb)
            n, _ = _ln_stats(yc_ref[rows, :])
            l = n * lg_ref[...] + lb_ref[...]
            z = z_ref[rows, :]
            yp_ref[rows, :] = ((l * _sig(l)) * (z * _sig(z))).astype(BF16)
            return carry
        _blocks(TM // RB, post)
        ya_ref[...] = jnp.dot(yp_ref[...], w_ref[...], preferred_element_type=F32)

    tile = lambda seg: pl.BlockSpec((TM, D), lambda i: (i, seg))
    row = pl.BlockSpec((1, D), lambda i: (0, 0))
    return pl.pallas_call(
        body, name="branch_a_fwd", grid=(T // TM,),
        out_shape=(jax.ShapeDtypeStruct((T, D), F32), jax.ShapeDtypeStruct((T, D), F32)),
        in_specs=[tile(0), tile(1), tile(2), _halo_prev(0), _halo_prev(1),
                  pl.BlockSpec((KW, 8, D), lambda i: (0, 0, 0)), row, row, row,
                  pl.BlockSpec((None, D, D), lambda i: (0, 0, 0))],
        out_specs=(pl.BlockSpec((TM, D), lambda i: (i, 0)), pl.BlockSpec((TM, D), lambda i: (i, 0))),
        scratch_shapes=[VMEM((8, EXT, D), F32), VMEM((TM, D), BF16)],
        compiler_params=_cp(("parallel",)),
    )(p, p, p, p, p, cw, conv_b, lg, lb, wq)


def _masked_ws(ws_ref, wt_ref):
    tri = lax.broadcasted_iota(jnp.int32, (CHUNK, CHUNK), 0) >= lax.broadcasted_iota(jnp.int32, (CHUNK, CHUNK), 1)
    for h in range(HEADS):
        wt_ref[h] = jnp.where(tri, ws_ref[h], 0.0).astype(BF16)


def _sgu_mix(wt_ref, vl_ref, bst_ref, s_ref):
    for ck in range(TM // CHUNK):
        r = slice(ck * CHUNK, (ck + 1) * CHUNK)
        for h in range(HEADS):
            cs = slice(h * HD, (h + 1) * HD)
            s_ref[r, cs] = jnp.dot(wt_ref[h], vl_ref[r, cs], preferred_element_type=F32) + bst_ref[:, h:h + 1]


def _branch_b_fwd(p, ws, bst, sg, sb, wq):
    T = p.shape[0]

    def body(pu_ref, pv_ref, pz_ref, ws_ref, bst_ref, sg_ref, sb_ref, w_ref, yb_ref,
             wt_ref, vl_ref, t_ref, s_ref, yp_ref):
        _masked_ws(ws_ref, wt_ref)

        def pre(rb, carry):
            rows = _rows(rb)
            vg, _ = _gelu_parts(pv_ref[rows, :])
            vn, _ = _ln_stats(vg)
            vl_ref[rows, :] = (vn * sg_ref[...] + sb_ref[...]).astype(BF16)
            u, _ = _gelu_parts(pu_ref[rows, :])
            z = pz_ref[rows, :]
            t_ref[rows, :] = u * (z * _sig(z))
            return carry
        _blocks(TM // RB, pre)
        _sgu_mix(wt_ref, vl_ref, bst_ref, s_ref)

        def post(rb, carry):
            rows = _rows(rb)
            yp_ref[rows, :] = (t_ref[rows, :] * s_ref[rows, :]).astype(BF16)
            return carry
        _blocks(TM // RB, post)
        yb_ref[...] = jnp.dot(yp_ref[...], w_ref[...], preferred_element_type=F32)

    tile = lambda seg: pl.BlockSpec((TM, D), lambda i: (i, seg))
    row = pl.BlockSpec((1, D), lambda i: (0, 0))
    return pl.pallas_call(
        body, name="branch_b_fwd", grid=(T // TM,),
        out_shape=jax.ShapeDtypeStruct((T, D), F32),
        in_specs=[tile(3), tile(4), tile(5), pl.BlockSpec((HEADS, CHUNK, CHUNK), lambda i: (0, 0, 0)),
                  pl.BlockSpec((CHUNK, HEADS), lambda i: (0, 0)), row, row,
                  pl.BlockSpec((None, D, D), lambda i: (1, 0, 0))],
        out_specs=pl.BlockSpec((TM, D), lambda i: (i, 0)),
        scratch_shapes=[VMEM((HEADS, CHUNK, CHUNK), BF16), VMEM((TM, D), BF16), VMEM((TM, D), F32),
                        VMEM((TM, D), F32), VMEM((TM, D), BF16)],
        compiler_params=_cp(("parallel",)),
    )(p, p, p, ws, bst, sg, sb, wq)


def _merge_loss(p, ya, yb, x, tgt, mod, g_final, wq):
    T = x.shape[0]
    nt = T // TM

    def body(ga_ref, gb_ref, ya_ref, yb_ref, x_ref, t_ref, mod_ref, gf_ref, w_ref,
             dx2_ref, dya_ref, dyb_ref, dp_ref, gwo_ref, st_ref,
             mrg_s, out_s, dout_s, dm_s, gw_acc):
        i = pl.program_id(0)

        @pl.when(i == 0)
        def _():
            gw_acc[...] = jnp.zeros_like(gw_acc)
            st_ref[...] = jnp.zeros_like(st_ref)

        def merge(rb, carry):
            rows = _rows(rb)
            mrg_s[rows, :] = (_sig(ga_ref[rows, :]) * ya_ref[rows, :] + _sig(gb_ref[rows, :]) * yb_ref[rows, :]).astype(BF16)
            return carry
        lax.fori_loop(0, TM // RB, merge, 0)
        out_s[...] = jnp.dot(mrg_s[...], w_ref[...], preferred_element_type=F32)

        def head(rb, carry):
            loss, gg, dg = carry
            rows = _rows(rb)
            gate = mod_ref[2:3, :]
            gf = gf_ref[...]
            out = out_s[rows, :]
            x2 = x_ref[rows, :] + gate * out
            r2 = lax.rsqrt(jnp.mean(x2 * x2, axis=-1, keepdims=True) + EPS)
            x2n = x2 * r2
            diff = x2n * gf - t_ref[rows, :]
            dy = diff * (1.0 / D)
            dx2n = dy * gf
            dx2 = r2 * (dx2n - x2n * jnp.mean(dx2n * x2n, axis=-1, keepdims=True))
            dx2_ref[rows, :] = dx2
            dout_s[rows, :] = (dx2 * gate).astype(BF16)
            return loss + _fold8(diff * diff), gg + _fold8(dy * x2n), dg + _fold8(dx2 * out)
        zero = jnp.zeros((8, D), F32)
        loss, gg, dg = _blocks(TM // RB, head, (zero, zero, zero))
        st_ref[0] = st_ref[0] + loss * (0.5 / D)
        st_ref[1] = st_ref[1] + gg
        st_ref[2] = st_ref[2] + dg

        dm_s[...] = lax.dot_general(dout_s[...], w_ref[...], NT, preferred_element_type=F32)
        gw_acc[...] += lax.dot_general(mrg_s[...], dout_s[...], TN, preferred_element_type=F32)

        def split(rb, carry):
            rows = _rows(rb)
            dm = dm_s[rows, :]
            sa = _sig(ga_ref[rows, :])
            sb = _sig(gb_ref[rows, :])
            dya_ref[rows, :] = (dm * sa).astype(BF16)
            dyb_ref[rows, :] = (dm * sb).astype(BF16)
            dp_ref[0, rows, :] = (dm * ya_ref[rows, :] * (sa * (1.0 - sa))).astype(BF16)
            dp_ref[1, rows, :] = (dm * yb_ref[rows, :] * (sb * (1.0 - sb))).astype(BF16)
            return carry
        lax.fori_loop(0, TM // RB, split, 0)

        @pl.when(i == nt - 1)
        def _():
            gwo_ref[...] = gw_acc[...].astype(BF16)

    tile = pl.BlockSpec((TM, D), lambda i: (i, 0))
    return pl.pallas_call(
        body, name="merge_loss", grid=(nt,),
        out_shape=(jax.ShapeDtypeStruct((T, D), F32), jax.ShapeDtypeStruct((T, D), BF16), jax.ShapeDtypeStruct((T, D), BF16),
                   jax.ShapeDtypeStruct((NSEG, T, D), BF16), jax.ShapeDtypeStruct((D, D), BF16),
                   jax.ShapeDtypeStruct((3, 8, D), F32)),
        in_specs=[pl.BlockSpec((TM, D), lambda i: (i, 6)), pl.BlockSpec((TM, D), lambda i: (i, 7)), tile, tile, tile, tile,
                  pl.BlockSpec((3, D), lambda i: (0, 0)), pl.BlockSpec((1, D), lambda i: (0, 0)),
                  pl.BlockSpec((None, D, D), lambda i: (2, 0, 0))],
        out_specs=(tile, tile, tile, pl.BlockSpec((2, TM, D), lambda i: (3, i, 0)),
                   pl.BlockSpec((D, D), lambda i: (0, 0)), pl.BlockSpec((3, 8, D), lambda i: (0, 0, 0))),
        scratch_shapes=[VMEM((TM, D), BF16), VMEM((TM, D), F32), VMEM((TM, D), BF16), VMEM((TM, D), F32), VMEM((D, D), F32)],
        compiler_params=_cp(("arbitrary",)),
    )(p, p, ya, yb, x, tgt, mod, g_final, wq)


def _branch_b_bwd(dp, dyb, p, ws, bst, sg, sb, wq):
    T = p.shape[0]
    nt = T // TM

    def body(dp_in, dyb_ref, pu_ref, pv_ref, pz_ref, ws_ref, bst_ref, sg_ref, sb_ref, w_ref,
             dp_ref, gw_ref, gws_ref, gbs_ref, st_ref,
             wt_ref, d_s, vl_s, vn_s, gpv_s, rs_s, s_s, ds_s, ds32_s, yp_s, dvl_s, gw_acc):
        del dp_in
        i = pl.program_id(0)

        @pl.when(i == 0)
        def _():
            gw_acc[...] = jnp.zeros_like(gw_acc)
            gws_ref[...] = jnp.zeros_like(gws_ref)
            gbs_ref[...] = jnp.zeros_like(gbs_ref)
            st_ref[...] = jnp.zeros_like(st_ref)
            _masked_ws(ws_ref, wt_ref)

        d_s[...] = lax.dot_general(dyb_ref[...], w_ref[...], NT, preferred_element_type=F32)

        def pre(rb, carry):
            rows = _rows(rb)
            vg, gpv = _gelu_parts(pv_ref[rows, :])
            vn, rs = _ln_stats(vg)
            vl_s[rows, :] = (vn * sg_ref[...] + sb_ref[...]).astype(BF16)
            vn_s[rows, :] = vn
            gpv_s[rows, :] = gpv
            rs_s[rows, :] = rs
            return carry
        _blocks(TM // RB, pre)
        _sgu_mix(wt_ref, vl_s, bst_ref, s_s)

        def mid(rb, carry):
            rows = _rows(rb)
            u, gpu = _gelu_parts(pu_ref[rows, :])
            z = pz_ref[rows, :]
            sz = _sig(z)
            siluz = z * sz
            d = d_s[rows, :]
            s = s_s[rows, :]
            t = u * siluz
            yp_s[rows, :] = (t * s).astype(BF16)
            dp_ref[0, rows, :] = (d * s * siluz * gpu).astype(BF16)
            ds = d * t
            ds32_s[rows, :] = ds
            ds_s[rows, :] = ds.astype(BF16)
            dp_ref[2, rows, :] = (d * u * s * (sz * (1.0 + z * (1.0 - sz)))).astype(BF16)
            return carry
        lax.fori_loop(0, TM // RB, mid, 0)

        for ck in range(TM // CHUNK):
            r = slice(ck * CHUNK, (ck + 1) * CHUNK)
            for h in range(HEADS):
                cs = slice(h * HD, (h + 1) * HD)
                dsh = ds_s[r, cs]
                dvl_s[r, cs] = lax.dot_general(wt_ref[h], dsh, TN, preferred_element_type=F32)
                gws_ref[h] += lax.dot_general(dsh, vl_s[r, cs], NT, preferred_element_type=F32)
                gbs_ref[h] += ds32_s[r, cs]

        def post(rb, carry):
            g_sg, g_sb = carry
            rows = _rows(rb)
            dvl = dvl_s[rows, :]
            vn = vn_s[rows, :]
            dvg = _ln_bwd(dvl * sg_ref[...], vn, rs_s[rows, :])
            dp_ref[1, rows, :] = (dvg * gpv_s[rows, :]).astype(BF16)
            return g_sg + _fold8(dvl * vn), g_sb + _fold8(dvl)
        zero = jnp.zeros((8, D), F32)
        g_sg, g_sb = _blocks(TM // RB, post, (zero, zero))
        st_ref[0] = st_ref[0] + g_sg
        st_ref[1] = st_ref[1] + g_sb

        gw_acc[...] += lax.dot_general(yp_s[...], dyb_ref[...], TN, preferred_element_type=F32)

        @pl.when(i == nt - 1)
        def _():
            gw_ref[...] = gw_acc[...].astype(BF16)

    tile = lambda seg: pl.BlockSpec((TM, D), lambda i: (i, seg))
    row = pl.BlockSpec((1, D), lambda i: (0, 0))
    hh = pl.BlockSpec((HEADS, CHUNK, CHUNK), lambda i: (0, 0, 0))
    return pl.pallas_call(
        body, name="branch_b_bwd", grid=(nt,),
        out_shape=(jax.ShapeDtypeStruct((NSEG, T, D), BF16), jax.ShapeDtypeStruct((D, D), BF16),
                   jax.ShapeDtypeStruct((HEADS, CHUNK, CHUNK), F32), jax.ShapeDtypeStruct((HEADS, CHUNK, HD), F32),
                   jax.ShapeDtypeStruct((2, 8, D), F32)),
        in_specs=[pl.BlockSpec(memory_space=ANY), pl.BlockSpec((TM, D), lambda i: (i, 0)), tile(3), tile(4), tile(5),
                  hh, pl.BlockSpec((CHUNK, HEADS), lambda i: (0, 0)), row, row,
                  pl.BlockSpec((None, D, D), lambda i: (1, 0, 0))],
        out_specs=(pl.BlockSpec((3, TM, D), lambda i: (1, i, 0)), pl.BlockSpec((D, D), lambda i: (0, 0)), hh, hh,
                   pl.BlockSpec((2, 8, D), lambda i: (0, 0, 0))),
        scratch_shapes=[VMEM((HEADS, CHUNK, CHUNK), BF16), VMEM((TM, D), F32), VMEM((TM, D), BF16), VMEM((TM, D), F32),
                        VMEM((TM, D), F32), VMEM((TM, 1), F32), VMEM((TM, D), F32), VMEM((TM, D), BF16), VMEM((TM, D), F32),
                        VMEM((TM, D), BF16), VMEM((TM, D), F32), VMEM((D, D), F32)],
        input_output_aliases={0: 0},
        compiler_params=_cp(("arbitrary",)),
    )(dp, dyb, p, p, p, ws, bst, sg, sb, wq)


def _branch_a_bwd1(dp, dya, yc, p, lg, lb, wq):
    T = p.shape[0]
    nt = T // TM

    def body(dp_in, dya_ref, yc_ref, z_ref, lg_ref, lb_ref, w_ref, dp_ref, dyc_ref, gw_ref, st_ref,
             d_s, yp_s, gw_acc):
        del dp_in
        i = pl.program_id(0)

        @pl.when(i == 0)
        def _():
            gw_acc[...] = jnp.zeros_like(gw_acc)
            st_ref[...] = jnp.zeros_like(st_ref)

        d_s[...] = lax.dot_general(dya_ref[...], w_ref[...], NT, preferred_element_type=F32)

        def blk(rb, carry):
            g_lg, g_lb, g_cb = carry
            rows = _rows(rb)
            n, rs = _ln_stats(yc_ref[rows, :])
            l = n * lg_ref[...] + lb_ref[...]
            sgl = _sig(l)
            sl = l * sgl
            z = z_ref[rows, :]
            sz = _sig(z)
            siluz = z * sz
            d = d_s[rows, :]
            yp_s[rows, :] = (sl * siluz).astype(BF16)
            dp_ref[rows, :] = (d * sl * (sz * (1.0 + z * (1.0 - sz)))).astype(BF16)
            dl = (d * siluz) * (sgl * (1.0 + l * (1.0 - sgl)))
            dyc = _ln_bwd(dl * lg_ref[...], n, rs)
            dyc_ref[rows, :] = dyc
            return g_lg + _fold8(dl * n), g_lb + _fold8(dl), g_cb + _fold8(dyc)
        zero = jnp.zeros((8, D), F32)
        g_lg, g_lb, g_cb = _blocks(TM // RB, blk, (zero, zero, zero))
        st_ref[0] = st_ref[0] + g_lg
        st_ref[1] = st_ref[1] + g_lb
        st_ref[2] = st_ref[2] + g_cb

        gw_acc[...] += lax.dot_general(yp_s[...], dya_ref[...], TN, preferred_element_type=F32)

        @pl.when(i == nt - 1)
        def _():
            gw_ref[...] = gw_acc[...].astype(BF16)

    tile = pl.BlockSpec((TM, D), lambda i: (i, 0))
    row = pl.BlockSpec((1, D), lambda i: (0, 0))
    return pl.pallas_call(
        body, name="branch_a_bwd1", grid=(nt,),
        out_shape=(jax.ShapeDtypeStruct((NSEG, T, D), BF16), jax.ShapeDtypeStruct((T, D), F32),
                   jax.ShapeDtypeStruct((D, D), BF16), jax.ShapeDtypeStruct((3, 8, D), F32)),
        in_specs=[pl.BlockSpec(memory_space=ANY), tile, tile, pl.BlockSpec((TM, D), lambda i: (i, 2)), row, row,
                  pl.BlockSpec((None, D, D), lambda i: (0, 0, 0))],
        out_specs=(pl.BlockSpec((None, TM, D), lambda i: (2, i, 0)), tile, pl.BlockSpec((D, D), lambda i: (0, 0)),
                   pl.BlockSpec((3, 8, D), lambda i: (0, 0, 0))),
        scratch_shapes=[VMEM((TM, D), F32), VMEM((TM, D), BF16), VMEM((D, D), F32)],
        input_output_aliases={0: 0},
        compiler_params=_cp(("arbitrary",)),
    )(dp, dya, yc, p, lg, lb, wq)


def _branch_a_bwd2(dp, dyc, p, cw, gw_co, gw_so, gw_o):
    T = p.shape[0]
    nt = T // TM
    hb = TM // HALO

    def body(dp_in, dyc_ref, hdyc_ref, val_ref, glu_ref, hval_ref, hglu_ref, cw_ref, gco_ref, gso_ref, go_ref,
             dp_ref, gcw_ref, rq_ref, sa_ref, sd_ref, da_s, x_send, x_recv, x_loc):
        del dp_in
        i = pl.program_id(0)
        start_exchange, wait_exchange = _owner_exchange((gco_ref, gso_ref, go_ref), rq_ref, x_send, x_recv, x_loc)

        @pl.when(i == 0)
        def _():
            gcw_ref[...] = jnp.zeros_like(gcw_ref)
            start_exchange()

        _fill_a_ext(i, sa_ref, val_ref, glu_ref, hval_ref, hglu_ref)
        _shift_copies(sa_ref)
        sd_ref[0, 0:TM, :] = dyc_ref[...]
        sd_ref[0, TM:EXT, :] = jnp.where(i < nt - 1, hdyc_ref[...], 0.0)
        _shift_copies(sd_ref)

        def conv_t(rb, carry):
            r0 = rb * RBC
            accs = [jnp.zeros((8, D), F32)] * (RBC // 8)
            for k in range(KW):
                o = KW - 1 - k
                w = cw_ref[k]
                accs = [acc + w * sd_ref[o % 8, pl.ds(pl.multiple_of(r0 + 8 * (o // 8 + g), 8), 8), :]
                        for g, acc in enumerate(accs)]
            for g, acc in enumerate(accs):
                da_s[pl.ds(pl.multiple_of(r0 + 8 * g, 8), 8), :] = acc
            return carry
        lax.fori_loop(0, TM // RBC, conv_t, 0)

        for k0 in range(0, KW, TAPG):
            taps = list(range(k0, min(k0 + TAPG, KW)))

            def tap_group(rb, accs, taps=taps):
                r0 = rb * 16
                d = dyc_ref[pl.ds(pl.multiple_of(r0, 16), 16), :]
                out = []
                for k, acc in zip(taps, accs):
                    o = 2 + k
                    a = sa_ref[o % 8, pl.ds(pl.multiple_of(r0 + 8 * (o // 8), 8), 16), :]
                    out.append(acc + _fold8(d * a))
                return tuple(out)
            sums = lax.fori_loop(0, TM // 16, tap_group, tuple(jnp.zeros((8, D), F32) for _ in taps))
            for k, s in zip(taps, sums):
                gcw_ref[k] = gcw_ref[k] + s

        def glu_b(rb, carry):
            rows = _rows(rb)
            da = da_s[rows, :]
            sg = _sig(glu_ref[rows, :])
            dp_ref[0, rows, :] = (da * sg).astype(BF16)
            dp_ref[1, rows, :] = (da * val_ref[rows, :] * (sg * (1.0 - sg))).astype(BF16)
            return carry
        lax.fori_loop(0, TM // RB, glu_b, 0)

        @pl.when(i == nt - 1)
        def _():
            wait_exchange()

    tile = lambda seg: pl.BlockSpec((TM, D), lambda i: (i, seg))
    hbm = pl.BlockSpec(memory_space=ANY)
    return pl.pallas_call(
        body, name="branch_a_bwd2", grid=(nt,),
        out_shape=(jax.ShapeDtypeStruct((NSEG, T, D), BF16), jax.ShapeDtypeStruct((32, 8, D), F32),
                   jax.ShapeDtypeStruct((NDEV, 3, 128, D), BF16)),
        in_specs=[hbm, pl.BlockSpec((TM, D), lambda i: (i, 0)),
                  pl.BlockSpec((HALO, D), lambda i: (jnp.minimum((i + 1) * hb, nt * hb - 1), 0)),
                  tile(0), tile(1), _halo_prev(0), _halo_prev(1), pl.BlockSpec((KW, 8, D), lambda i: (0, 0, 0)),
                  hbm, hbm, hbm],
        out_specs=(pl.BlockSpec((2, TM, D), lambda i: (0, i, 0)), pl.BlockSpec((32, 8, D), lambda i: (0, 0, 0)), hbm),
        scratch_shapes=[VMEM((8, EXT, D), F32), VMEM((8, EXT, D), F32), VMEM((TM, D), F32),
                        pltpu.SemaphoreType.DMA((3, 7)), pltpu.SemaphoreType.DMA((3, 7)), pltpu.SemaphoreType.DMA((3,))],
        input_output_aliases={0: 0},
        compiler_params=_cp(("arbitrary",)),
    )(dp, dyc, dyc, p, p, p, p, cw, gw_co, gw_so, gw_o)


def _in_bwd_dx(dp, wing, x, dx2, mod, g_pre):
    T = x.shape[0]
    nt = T // TM

    def body(dp_ref, w_ref, x_ref, dx2_ref, mod_ref, g_ref, gx_ref, st_ref, dh_s):
        i = pl.program_id(0)

        @pl.when(i == 0)
        def _():
            st_ref[...] = jnp.zeros_like(st_ref)

        dh_s[...] = lax.dot_general(dp_ref[0], w_ref[0], NT, preferred_element_type=F32)
        for j in range(1, NSEG):
            dh_s[...] += lax.dot_general(dp_ref[j], w_ref[j], NT, preferred_element_type=F32)

        def blk(rb, carry):
            d_sh, d_sc, g_g = carry
            rows = _rows(rb)
            xv = x_ref[rows, :]
            r = lax.rsqrt(jnp.mean(xv * xv, axis=-1, keepdims=True) + EPS)
            xn = xv * r
            g = g_ref[...]
            hpre = xn * g
            dh = dh_s[rows, :]
            dhp = dh * (1.0 + mod_ref[1:2, :])
            dxn = dhp * g
            gx_ref[rows, :] = dx2_ref[rows, :] + r * (dxn - xn * jnp.mean(dxn * xn, axis=-1, keepdims=True))
            return d_sh + _fold8(dh), d_sc + _fold8(dh * hpre), g_g + _fold8(dhp * xn)
        zero = jnp.zeros((8, D), F32)
        d_sh, d_sc, g_g = _blocks(TM // RB, blk, (zero, zero, zero))
        st_ref[0] = st_ref[0] + d_sh
        st_ref[1] = st_ref[1] + d_sc
        st_ref[2] = st_ref[2] + g_g

    tile = pl.BlockSpec((TM, D), lambda i: (i, 0))
    return pl.pallas_call(
        body, name="in_bwd_dx", grid=(nt,),
        out_shape=(jax.ShapeDtypeStruct((T, D), F32), jax.ShapeDtypeStruct((3, 8, D), F32)),
        in_specs=[pl.BlockSpec((NSEG, TM, D), lambda i: (0, i, 0)),
                  pl.BlockSpec((NSEG, D, D), lambda i: (0, 0, 0), pipeline_mode=pl.Buffered(1)),
                  tile, tile, pl.BlockSpec((3, D), lambda i: (0, 0)), pl.BlockSpec((1, D), lambda i: (0, 0))],
        out_specs=(tile, pl.BlockSpec((3, 8, D), lambda i: (0, 0, 0))),
        scratch_shapes=[VMEM((TM, D), F32)],
        compiler_params=_cp(("arbitrary",), vmem_mb=56),
    )(dp, wing, x, dx2, mod, g_pre)


RS_ORDER = (5, 4, 3, 2, 7, 6, 1, 0)
RS_HALF = D // 2


def _in_bwd_dw(h, dp, order, smalls):
    T = h.shape[0]
    tmw = min(2 * TMB, T)
    nb = T // tmw
    nu = 2 * NSEG

    def body(order_ref, h_ref, dp_ref, sm_ref, g_ref, smg_ref, acc, sbuf, rsib, rici, send_sems, recv_sems,
             ag_send, ag_recv, ag_loc):
        j = pl.program_id(0)
        i = pl.program_id(1)
        x, y, c = _me()
        sibling = (x, y, 1 - c)
        chips = [(1 - x, y), (x, 1 - y), (1 - x, 1 - y)]
        ag_start, ag_relay, ag_finish = _allgather_phases([sm_ref], lambda a, slot: smg_ref.at[slot],
                                                          ag_send, ag_recv, ag_loc)

        def to_sibling(s):
            return pltpu.make_async_remote_copy(src_ref=sbuf.at[s], dst_ref=rsib.at[s], send_sem=send_sems.at[s],
                                                recv_sem=recv_sems.at[s], device_id=sibling, device_id_type=MESH)

        def to_owner(t):
            return pltpu.make_async_remote_copy(src_ref=sbuf.at[8 + t], dst_ref=rici.at[t], send_sem=send_sems.at[8 + t],
                                                recv_sem=recv_sems.at[8 + t], device_id=(*chips[t // 2], c),
                                                device_id_type=MESH)

        @pl.when((j == 0) & (i == 0))
        def _():
            ag_start()

        @pl.when((j == 4) & (i == 0))
        def _():
            ag_relay()

        @pl.when(i == 0)
        def _():
            acc[...] = jnp.zeros_like(acc)
        acc[...] += lax.dot_general(h_ref[...], dp_ref[...], TN, preferred_element_type=F32)

        for jj in range(nu):
            chip, half, mine = jj // 4, (jj % 4) // 2, jj % 2
            s = 2 * chip + half

            @pl.when((j == jj) & (i == nb - 1))
            def _(chip=chip, half=half, mine=mine, s=s):
                if not mine:
                    sbuf[s] = acc[...].astype(BF16)
                    to_sibling(s).start()
                elif chip < 3:
                    to_sibling(s).wait_recv()
                    sbuf[8 + s] = (acc[...] + rsib[s].astype(F32)).astype(BF16)
                    to_owner(s).start()
                else:
                    to_sibling(s).wait_recv()
                    g = acc[...] + rsib[s].astype(F32)
                    for q in range(3):
                        to_owner(2 * q + half).wait_recv()
                        g = g + rici[2 * q + half].astype(F32)
                    g_ref[:, half * RS_HALF:(half + 1) * RS_HALF] = g

        @pl.when((j == nu - 1) & (i == nb - 1))
        def _():
            for s in range(8):
                to_sibling(s).wait_send()
            for t in range(6):
                to_owner(t).wait_send()
            ag_finish()

    hbm = pl.BlockSpec(memory_space=ANY)
    return pl.pallas_call(
        body, name="in_bwd_dw",
        out_shape=(jax.ShapeDtypeStruct((D, D), F32), jax.ShapeDtypeStruct((NDEV, NSMALL, D), F32)),
        grid_spec=pltpu.PrefetchScalarGridSpec(
            num_scalar_prefetch=1, grid=(nu, nb),
            in_specs=[pl.BlockSpec((tmw, D), lambda j, i, o: (i, 0)),
                      pl.BlockSpec((None, tmw, RS_HALF), lambda j, i, o: (o[2 * (j // 4) + j % 2], i, (j % 4) // 2)),
                      hbm],
            out_specs=(pl.BlockSpec((D, D), lambda j, i, o: (0, 0)), hbm),
            scratch_shapes=[VMEM((D, RS_HALF), F32), VMEM((14, D, RS_HALF), BF16), VMEM((8, D, RS_HALF), BF16),
                            VMEM((6, D, RS_HALF), BF16), pltpu.SemaphoreType.DMA((14,)), pltpu.SemaphoreType.DMA((14,)),
                            pltpu.SemaphoreType.DMA((1, 7)), pltpu.SemaphoreType.DMA((1, 7)), pltpu.SemaphoreType.DMA((1,))]),
        compiler_params=_cp(("arbitrary", "arbitrary"), vmem_mb=56),
    )(order, h, dp, smalls)


def _pack_smalls(st_x, st_a, st_b, st_m, gcw, gws, gbs):
    def body(sx_ref, sa_ref, sb_ref, sm_ref, gcw_ref, gws_ref, gbs_ref, out_ref):
        out_ref[...] = jnp.zeros_like(out_ref)
        fold = lambda v: jnp.sum(v, axis=0, keepdims=True)
        out_ref[0:1, :] = fold(sx_ref[0])
        out_ref[1:2, :] = fold(sx_ref[1])
        out_ref[2:3, :] = fold(sm_ref[2])
        out_ref[3:4, :] = fold(sx_ref[2])
        out_ref[4:5, :] = fold(sa_ref[2])
        out_ref[5:6, :] = fold(sa_ref[0])
        out_ref[6:7, :] = fold(sa_ref[1])
        out_ref[7:8, :] = fold(sb_ref[0])
        out_ref[8:9, :] = fold(sb_ref[1])
        out_ref[10:11, :] = fold(sm_ref[1])
        out_ref[11:12, :] = fold(sm_ref[0])
        for k in range(KW):
            out_ref[ROW_CW + k:ROW_CW + k + 1, :] = fold(gcw_ref[k])
        tri = lax.broadcasted_iota(jnp.int32, (CHUNK, CHUNK), 0) >= lax.broadcasted_iota(jnp.int32, (CHUNK, CHUNK), 1)
        for h in range(HEADS):
            out_ref[ROW_WS:ROW_WS + CHUNK, h * CHUNK:(h + 1) * CHUNK] = jnp.where(tri, gws_ref[h], 0.0)
            out_ref[9:10, h * CHUNK:(h + 1) * CHUNK] = fold(gbs_ref[h].T)

    vm = pl.BlockSpec(memory_space=VMEM)
    return pl.pallas_call(
        body, name="pack_smalls", out_shape=jax.ShapeDtypeStruct((NSMALL, D), F32),
        in_specs=[vm] * 7, out_specs=vm, compiler_params=_cp(vmem_mb=32),
    )(st_x, st_a, st_b, st_m, gcw, gws, gbs)


def _adam(w, g, m, v):
    m2 = ADAM_B1 * m + (1.0 - ADAM_B1) * g
    v2 = ADAM_B2 * v + (1.0 - ADAM_B2) * (g * g)
    m_hat = m2 / (1.0 - ADAM_B1 ** ADAM_STEP)
    v_hat = v2 / (1.0 - ADAM_B2 ** ADAM_STEP)
    delta = -ADAM_LR * (m_hat / (jnp.sqrt(v_hat) + ADAM_EPS) + ADAM_WD * w)
    return delta, m2, v2


def _small_finish(smg, w, m, v):
    def body(smg_ref, w_ref, m_ref, v_ref, loss_ref, g_ref, d_ref, m2_ref, v2_ref):
        g = smg_ref[0]
        for s in range(1, NDEV):
            g = g + smg_ref[s]
        g_ref[...] = g
        loss_ref[...] = jnp.sum(g[11:12, :], axis=1, keepdims=True)
        d_ref[...], m2_ref[...], v2_ref[...] = _adam(w_ref[...], g, m_ref[...], v_ref[...])

    vm = pl.BlockSpec(memory_space=VMEM)
    sd = jax.ShapeDtypeStruct((NSMALL, D), F32)
    return pl.pallas_call(
        body, name="small_finish", out_shape=(jax.ShapeDtypeStruct((1, 1), F32), sd, sd, sd, sd),
        in_specs=[vm] * 4, out_specs=(vm,) * 5, compiler_params=_cp(vmem_mb=32),
    )(smg, w, m, v)


def _ada_grad_adam(ct, dm, w, m, v):
    def body(ct_ref, dm_ref, w_ref, m_ref, v_ref, g_ref, d_ref, m2_ref, v2_ref):
        g = ct_ref[:, 0:1] * dm_ref[0:1, :]
        for b in range(1, NDEV):
            g = g + ct_ref[:, b:b + 1] * dm_ref[b:b + 1, :]
        g_ref[...] = g
        d_ref[...], m2_ref[...], v2_ref[...] = _adam(w_ref[...], g, m_ref[...], v_ref[...])

    vm = pl.BlockSpec(memory_space=VMEM)
    sd = jax.ShapeDtypeStruct(w.shape, F32)
    return pl.pallas_call(
        body, name="ada_grad_adam", out_shape=(sd, sd, sd, sd),
        in_specs=[vm] * 5, out_specs=(vm,) * 4, compiler_params=_cp(vmem_mb=32),
    )(ct, dm, w, m, v)


def _adam_f32(name, g, w, m, v, rows=None):
    R, C = w.shape
    rows = R if rows is None else rows

    def body(g_ref, w_ref, m_ref, v_ref, d_ref, m2_ref, v2_ref):
        d_ref[...], m2_ref[...], v2_ref[...] = _adam(w_ref[...], g_ref[...], m_ref[...], v_ref[...])

    tile = pl.BlockSpec((rows, C), lambda i: (i, 0))
    sd = jax.ShapeDtypeStruct(w.shape, F32)
    return pl.pallas_call(
        body, name=name, grid=(R // rows,), out_shape=(sd, sd, sd), in_specs=[tile] * 4, out_specs=(tile,) * 3,
        compiler_params=_cp(("parallel",), vmem_mb=32),
    )(g, w, m, v)


def _adam_reduce(name, recv, w, m, v, rows, recv_spec):
    R, C = w.shape

    def body(r_ref, w_ref, m_ref, v_ref, g_ref, d_ref, m2_ref, v2_ref):
        g = r_ref[0].astype(F32)
        for s in range(1, NDEV):
            g = g + r_ref[s].astype(F32)
        g_ref[...] = g
        d_ref[...], m2_ref[...], v2_ref[...] = _adam(w_ref[...], g, m_ref[...], v_ref[...])

    tile = pl.BlockSpec((rows, C), lambda i: (i, 0))
    sd = jax.ShapeDtypeStruct((R, C), F32)
    return pl.pallas_call(
        body, name=name, grid=(R // rows,), out_shape=(sd, sd, sd, sd),
        in_specs=[recv_spec, tile, tile, tile], out_specs=(tile,) * 4,
        compiler_params=_cp(("parallel",)),
    )(recv, w, m, v)


def kernel(x, c, w_ada, b_ada, g_pre, w_in, conv_w, conv_b, conv_ln_g, conv_ln_b, w_conv_out, sgu_ln_g, sgu_ln_b, w_sgu, b_sgu, w_sgu_out, w_o, g_final, loss_target, m_w_ada, m_b_ada, m_g_pre, m_w_in, m_conv_w, m_conv_b, m_conv_ln_g, m_conv_ln_b, m_w_conv_out, m_sgu_ln_g, m_sgu_ln_b, m_w_sgu, m_b_sgu, m_w_sgu_out, m_w_o, m_g_final, v_w_ada, v_b_ada, v_g_pre, v_w_in, v_conv_w, v_conv_b, v_conv_ln_g, v_conv_ln_b, v_w_conv_out, v_sgu_ln_g, v_sgu_ln_b, v_w_sgu, v_b_sgu, v_w_sgu_out, v_w_o, v_g_final):
    T = x.shape[1]
    assert T % TMB == 0 and x.shape[2] == D
    xs = x[0]
    tgt = loss_target[0]
    my = 4 * lax.axis_index("x") + 2 * lax.axis_index("y") + lax.axis_index("c")

    mod, cg = _ada_fwd(c, w_ada[0], b_ada)
    gf = g_final.reshape(1, D)
    bst = b_sgu[0].T

    h = _prep_h(xs, mod, g_pre)
    p, wing, wqg, cwg = _in_proj(h, w_in[0], my ^ jnp.array(AG_ORDER, jnp.int32),
                                 w_conv_out[0], w_sgu_out[0], w_o[0], conv_w[0])
    wq = wqg.reshape(3, D, D)
    cw = jnp.broadcast_to(jnp.transpose(cwg, (1, 0, 2)).reshape(KW, 1, D), (KW, 8, D))
    yc, ya = _branch_a_fwd(p, cw, conv_b, conv_ln_g, conv_ln_b, wq)
    yb = _branch_b_fwd(p, w_sgu[0], bst, sgu_ln_g, sgu_ln_b, wq)
    dx2, dya, dyb, dp, gw_o, st_m = _merge_loss(p, ya, yb, xs, tgt, mod, gf, wq)

    dp, gw_so, gws, gbs, st_b = _branch_b_bwd(dp, dyb, p, w_sgu[0], bst, sgu_ln_g, sgu_ln_b, wq)
    dp, dyc, gw_co, st_a = _branch_a_bwd1(dp, dya, yc, p, conv_ln_g, conv_ln_b, wq)
    dp, gcw, rq = _branch_a_bwd2(dp, dyc, p, cw, gw_co, gw_so, gw_o)
    grad_x, st_x = _in_bwd_dx(dp, wing, xs, dx2, mod, g_pre)
    smalls = _pack_smalls(st_x, st_a, st_b, st_m, gcw, gws, gbs)
    g_in, smg = _in_bwd_dw(h, dp, my ^ jnp.array(RS_ORDER, jnp.int32), smalls)

    def pack(b_ada_, g_pre_, conv_b_, lg_, lb_, sg_, sb_, b_sgu_, gfin_, w_sgu_):
        top = jnp.concatenate([b_ada_.reshape(3, D), g_pre_, conv_b_, lg_, lb_, sg_, sb_, b_sgu_.reshape(1, D),
                               gfin_.reshape(1, D), jnp.zeros((ROW_WS - 11, D), F32)], axis=0)
        return jnp.concatenate([top, jnp.transpose(w_sgu_[0], (1, 0, 2)).reshape(CHUNK, D)], axis=0)

    pw = pack(b_ada, g_pre, conv_b, conv_ln_g, conv_ln_b, sgu_ln_g, sgu_ln_b, b_sgu, g_final, w_sgu)
    pm = pack(m_b_ada, m_g_pre, m_conv_b, m_conv_ln_g, m_conv_ln_b, m_sgu_ln_g, m_sgu_ln_b, m_b_sgu, m_g_final, m_w_sgu)
    pv = pack(v_b_ada, v_g_pre, v_conv_b, v_conv_ln_g, v_conv_ln_b, v_sgu_ln_g, v_sgu_ln_b, v_b_sgu, v_g_final, v_w_sgu)
    loss11, sg_, sd_, sm_, sv_ = _small_finish(smg, pw, pm, pv)

    def unpack(a):
        return dict(
            b_ada=a[0:3].reshape(1, 3 * D), g_pre=a[3:4], conv_b=a[4:5], conv_ln_g=a[5:6], conv_ln_b=a[6:7],
            sgu_ln_g=a[7:8], sgu_ln_b=a[8:9], b_sgu=a[9:10].reshape(1, HEADS, CHUNK), g_final=a[10],
            w_sgu=jnp.transpose(a[ROW_WS:ROW_WS + CHUNK].reshape(CHUNK, HEADS, CHUNK), (1, 0, 2))[None])
    small = [unpack(a) for a in (sg_, sd_, sm_, sv_)]

    g_cw = lax.dynamic_slice_in_dim(sg_[ROW_CW:ROW_CW + KW], my * 128, 128, axis=1)
    d_cw, m_cw, v_cw = _adam_f32("adam_conv_w", g_cw, conv_w[0], m_conv_w[0], v_conv_w[0])

    wcols = w_ada.shape[2]
    dm_all = smg[:, 0:3, :].reshape(NDEV, 3 * D)
    dm_mine = lax.dynamic_slice_in_dim(dm_all, my * wcols, wcols, axis=1)
    g_ada, d_ada, m_ada, v_ada = _ada_grad_adam(cg.T, dm_mine, w_ada[0], m_w_ada[0], v_w_ada[0])

    d_in, m_in, v_in = _adam_f32("adam_w_in", g_in, w_in[0], m_w_in[0], v_w_in[0], rows=256)
    big = {}
    for j, (nm, w_, m_, v_) in enumerate((("w_conv_out", w_conv_out, m_w_conv_out, v_w_conv_out),
                                          ("w_sgu_out", w_sgu_out, m_w_sgu_out, v_w_sgu_out),
                                          ("w_o", w_o, m_w_o, v_w_o))):
        big[nm] = _adam_reduce("adam_" + nm, rq, w_[0], m_[0], v_[0], 128,
                               pl.BlockSpec((NDEV, None, 128, D), lambda i, j=j: (0, j, 0, 0)))

    per = {
        "w_ada": tuple(a[None] for a in (g_ada, d_ada, m_ada, v_ada)),
        "w_in": tuple(a[None] for a in (g_in, d_in, m_in, v_in)),
        "conv_w": tuple(a[None] for a in (g_cw, d_cw, m_cw, v_cw)),
    }
    for nm in ("w_conv_out", "w_sgu_out", "w_o"):
        per[nm] = tuple(a[None] for a in big[nm])
    for nm in ("b_ada", "g_pre", "conv_b", "conv_ln_g", "conv_ln_b", "sgu_ln_g", "sgu_ln_b", "w_sgu", "b_sgu", "g_final"):
        per[nm] = tuple(s[nm] for s in small)

    order = ["w_ada", "b_ada", "g_pre", "w_in", "conv_w", "conv_b", "conv_ln_g", "conv_ln_b", "w_conv_out",
             "sgu_ln_g", "sgu_ln_b", "w_sgu", "b_sgu", "w_sgu_out", "w_o", "g_final"]
    outs = [loss11.reshape(()), grad_x[None]]
    for part in range(4):
        outs += [per[nm][part] for nm in order]
    return tuple(outs)
```

```python
import jax
import jax.numpy as jnp
from jax import lax
from jax.experimental import pallas as pl
from jax.experimental.pallas import tpu as pltpu

F32 = jnp.float32
BF16 = jnp.bfloat16
MESH = pl.DeviceIdType.MESH
VMEM = pltpu.VMEM
ANY = pl.ANY

D = 1024
NDEV = 8
NSEG = 8
HEADS = 8
HD = 128
CHUNK = 128
KW = 31
HALO = 32
EPS = 1e-6
TM = 256
TMB = 512
RB = 32
UNROLL = 4
RBC = 32
TAPG = 4
EXT = TM + HALO
NSMALL = 176
NEARLY = 144
NLATE = 40
ROW_CW = 16
ROW_WS = 48
W_SO, W_O = 0, 1

ADAM_LR = 0.001
ADAM_B1 = 0.9
ADAM_B2 = 0.999
ADAM_EPS = 1e-08
ADAM_WD = 0.01
ADAM_STEP = 10

INV_SQRT2 = 0.7071067811865476
INV_SQRT_2PI = 0.3989422804014327

NT = (((1,), (1,)), ((), ()))
TN = (((0,), (0,)), ((), ()))


def _cp(sem=None, vmem_mb=48):
    return pltpu.CompilerParams(dimension_semantics=sem, vmem_limit_bytes=vmem_mb * 1024 * 1024)


def _me():
    return lax.axis_index("x"), lax.axis_index("y"), lax.axis_index("c")


def _slot(px, py, pc):
    return 4 * px + 2 * py + pc


def _xor_peer(k):
    x, y, c = _me()
    return (1 - x if k & 4 else x, 1 - y if k & 2 else y, 1 - c if k & 1 else c)


def _allgather_phases(srcs, dst_at, send_sems, recv_sems, loc_sems):
    x, y, c = _me()
    me = (x, y, c)
    sibling = (x, y, 1 - c)
    chips = [(1 - x, y), (x, 1 - y), (1 - x, 1 - y)]
    na = len(srcs)

    def copy(a, k, block, to, src=None):
        d = dst_at(a, _slot(*block))
        return pltpu.make_async_remote_copy(
            src_ref=d if src is None else src, dst_ref=d,
            send_sem=send_sems.at[a, k], recv_sem=recv_sems.at[a, k],
            device_id=to, device_id_type=MESH)

    local = [pltpu.make_async_copy(srcs[a], dst_at(a, _slot(*me)), loc_sems.at[a]) for a in range(na)]
    first = []
    for a in range(na):
        first.append(copy(a, 0, me, sibling, src=srcs[a]))
        for j, chip in enumerate(chips):
            first.append(copy(a, 1 + j, me, (*chip, c), src=srcs[a]))
    passed = [copy(a, 4 + j, (*chip, c), sibling) for j, chip in enumerate(chips) for a in range(na)]

    def start():
        for cp in local + first:
            cp.start()

    def relay():
        for j, chip in enumerate(chips):
            for a in range(na):
                copy(a, 1 + j, (*chip, c), me).wait_recv()
                passed[j * na + a].start()

    def finish():
        for a in range(na):
            copy(a, 0, sibling, me).wait_recv()
        for j, chip in enumerate(chips):
            for a in range(na):
                copy(a, 4 + j, (*chip, 1 - c), me).wait_recv()
        for cp in first + passed:
            cp.wait_send()
        for cp in local:
            cp.wait()

    return start, relay, finish


def _allgather(srcs, dst_at, send_sems, recv_sems, loc_sems):
    start, relay, finish = _allgather_phases(srcs, dst_at, send_sems, recv_sems, loc_sems)
    start()
    relay()
    finish()


def _owner_exchange(gq, rq_ref, x_send, x_recv, x_loc):
    mx, my_, mc = _me()
    me = _slot(mx, my_, mc)

    def rows_of(a, slot):
        return gq[a].at[pl.ds(pl.multiple_of(slot * 128, 128), 128), :]

    def exchange(k, a, recv):
        px, py, pc = _xor_peer(k)
        peer = _slot(px, py, pc)
        return pltpu.make_async_remote_copy(
            src_ref=rows_of(a, me if recv else peer), dst_ref=rq_ref.at[peer if recv else me, a],
            send_sem=x_send.at[a, k - 1], recv_sem=x_recv.at[a, k - 1],
            device_id=(px, py, pc), device_id_type=MESH)

    local = [pltpu.make_async_copy(rows_of(a, me), rq_ref.at[me, a], x_loc.at[a]) for a in range(len(gq))]

    def start():
        for cp in local:
            cp.start()
        for k in range(1, NDEV):
            for a in range(len(gq)):
                exchange(k, a, False).start()

    def wait():
        for k in range(1, NDEV):
            for a in range(len(gq)):
                exchange(k, a, True).wait_recv()
        for k in range(1, NDEV):
            for a in range(len(gq)):
                exchange(k, a, False).wait_send()
        for cp in local:
            cp.wait()

    return start, wait


def _sig(v):
    return jax.nn.sigmoid(v)


def _gelu_parts(v):
    cdf = 0.5 * (1.0 + lax.erf(v * INV_SQRT2))
    pdf = jnp.exp(-0.5 * v * v) * INV_SQRT_2PI
    return v * cdf, cdf + v * pdf


def _ln_stats(v):
    mu = jnp.mean(v, axis=-1, keepdims=True)
    xc = v - mu
    var = jnp.mean(xc * xc, axis=-1, keepdims=True)
    rs = lax.rsqrt(var + EPS)
    return xc * rs, rs


def _ln_bwd(dn, n, rs):
    return rs * (dn - jnp.mean(dn, axis=-1, keepdims=True) - n * jnp.mean(dn * n, axis=-1, keepdims=True))


def _fold8(v):
    acc = v[0:8]
    for r in range(8, v.shape[0], 8):
        acc = acc + v[r:r + 8]
    return acc


def _rows(rb, n=RB):
    return pl.ds(pl.multiple_of(rb * n, n), n)


def _blocks(n, body, init=0, unroll=UNROLL):
    def trip(t, carry):
        for u in range(unroll):
            carry = body(t * unroll + u, carry)
        return carry
    return lax.fori_loop(0, n // unroll, trip, init)


def _shift_copies(s_ref):
    n = EXT - 8
    for m in range(1, 8):
        for c0 in range(0, n, 56):
            s_ref[m, c0:c0 + 56, :] = s_ref[0, c0 + m:c0 + m + 56, :]


def _ada_fwd(c, w_ada, b_ada):
    wcols = w_ada.shape[1]

    def body(c_ref, w_ref, b_ref, mod_ref, call_ref, cdst, msrc, mdst, c_send, c_recv, m_send, m_recv):
        x, y, c = _me()
        me = _slot(x, y, c)

        def all_to_all(src_of, dst, ss, rs):
            def cp(k, recv):
                px, py, pc = _xor_peer(k)
                peer = _slot(px, py, pc)
                return pltpu.make_async_remote_copy(
                    src_ref=src_of(me if recv else peer), dst_ref=dst.at[peer if recv else me],
                    send_sem=ss.at[k - 1], recv_sem=rs.at[k - 1], device_id=(px, py, pc), device_id_type=MESH)
            for k in range(1, NDEV):
                cp(k, False).start()
            for k in range(1, NDEV):
                cp(k, True).wait_recv()
            for k in range(1, NDEV):
                cp(k, False).wait_send()

        cdst[me] = c_ref[...]
        all_to_all(lambda s: c_ref, cdst, c_send, c_recv)
        for b in range(NDEV):
            call_ref[b:b + 1, :] = cdst[b]
        m = jnp.dot(call_ref[...], w_ref[...], preferred_element_type=F32, precision=lax.Precision.HIGHEST)
        for b in range(NDEV):
            msrc[b] = m[b:b + 1, :]
        mdst[me] = msrc[me]
        all_to_all(lambda s: msrc.at[s], mdst, m_send, m_recv)
        full = jnp.concatenate([mdst[k] for k in range(NDEV)], axis=1) + b_ref[...]
        for r in range(3):
            mod_ref[r:r + 1, :] = full[:, r * D:(r + 1) * D]

    vm = pl.BlockSpec(memory_space=VMEM)
    return pl.pallas_call(
        body, name="ada_fwd",
        out_shape=(jax.ShapeDtypeStruct((3, D), F32), jax.ShapeDtypeStruct((NDEV, D), F32)),
        in_specs=[vm, vm, vm], out_specs=(vm, vm),
        scratch_shapes=[VMEM((NDEV, 1, D), F32), VMEM((NDEV, 1, wcols), F32), VMEM((NDEV, 1, wcols), F32),
                        pltpu.SemaphoreType.DMA((7,)), pltpu.SemaphoreType.DMA((7,)),
                        pltpu.SemaphoreType.DMA((7,)), pltpu.SemaphoreType.DMA((7,))],
        compiler_params=_cp(vmem_mb=32),
    )(c, w_ada, b_ada)


def _prep_h(x, mod, g_pre):
    T = x.shape[0]

    def body(x_ref, mod_ref, g_ref, h_ref):
        def blk(rb, carry):
            rows = _rows(rb)
            xv = x_ref[rows, :]
            r = lax.rsqrt(jnp.mean(xv * xv, axis=-1, keepdims=True) + EPS)
            h_ref[rows, :] = ((xv * r) * g_ref[...] * (1.0 + mod_ref[1:2, :]) + mod_ref[0:1, :]).astype(BF16)
            return carry
        _blocks(TMB // RB, blk)

    return pl.pallas_call(
        body, name="prep_h", grid=(T // TMB,),
        out_shape=jax.ShapeDtypeStruct((T, D), BF16),
        in_specs=[pl.BlockSpec((TMB, D), lambda i: (i, 0)), pl.BlockSpec((3, D), lambda i: (0, 0)),
                  pl.BlockSpec((1, D), lambda i: (0, 0))],
        out_specs=pl.BlockSpec((TMB, D), lambda i: (i, 0)),
        compiler_params=_cp(("parallel",)),
    )(x, mod, g_pre)


AG_ORDER = (0, 1, 4, 5, 2, 3, 6, 7)


def _in_proj(h, w_in, order, w_co, conv_w):
    T = h.shape[0]
    nb = T // TMB

    def body(order_ref, h_ref, w_ref, wco_ref, cw_ref, p_ref, wing_ref, wqg_ref, cwg_ref,
             wbuf, wq_st, send_sems, recv_sems, out_sems, ag_send, ag_recv, ag_loc):
        j = pl.program_id(0)
        i = pl.program_id(1)
        x, y, c = _me()
        me = (x, y, c)
        sibling = (x, y, 1 - c)
        chips = [(1 - x, y), (x, 1 - y), (1 - x, 1 - y)]
        blocks = [me, sibling] + [(*chip, cc) for chip in chips for cc in (c, 1 - c)]
        recv_k = [None, 0, 1, 4, 2, 5, 3, 6]
        ag_start, ag_relay, ag_finish = _allgather_phases(
            [wq_st, cw_ref], lambda a, slot: wqg_ref.at[:, slot] if a == 0 else cwg_ref.at[slot], ag_send, ag_recv, ag_loc)

        def copy(k, block, to):
            d = wbuf.at[_slot(*block)]
            return pltpu.make_async_remote_copy(src_ref=d, dst_ref=d, send_sem=send_sems.at[k], recv_sem=recv_sems.at[k],
                                                device_id=to, device_id_type=MESH)

        def writeback(jj):
            s = _slot(*blocks[jj])
            return pltpu.make_async_copy(wbuf.at[s], wing_ref.at[s], out_sems.at[jj])

        first = [copy(0, me, sibling)] + [copy(1 + q, me, (*chip, c)) for q, chip in enumerate(chips)]
        passed = [copy(4 + q, (*chip, c), sibling) for q, chip in enumerate(chips)]

        @pl.when((j == 0) & (i == 0))
        def _():
            wbuf[_slot(*me)] = w_ref[...].astype(BF16)
            for cp in first:
                cp.start()
            writeback(0).start()
            wq_st[0] = wco_ref[...].astype(BF16)
            ag_start()

        for jj in range(1, NSEG):
            @pl.when((j == jj) & (i == 0))
            def _(jj=jj):
                copy(recv_k[jj], blocks[jj], me).wait_recv()
                if jj in (2, 4, 6):
                    passed[jj // 2 - 1].start()
                writeback(jj).start()
                if jj == NSEG - 1:
                    ag_relay()

        p_ref[...] = jnp.dot(h_ref[...], wbuf[order_ref[j]], preferred_element_type=F32)

        @pl.when((j == NSEG - 1) & (i == nb - 1))
        def _():
            for cp in first + passed:
                cp.wait_send()
            for jj in range(NSEG):
                writeback(jj).wait()
            ag_finish()

    vm = pl.BlockSpec(memory_space=VMEM)
    hbm = pl.BlockSpec(memory_space=ANY)
    return pl.pallas_call(
        body, name="in_proj",
        out_shape=(jax.ShapeDtypeStruct((T, NSEG * D), F32), jax.ShapeDtypeStruct((NSEG, D, D), BF16),
                   jax.ShapeDtypeStruct((1, NDEV, 128, D), BF16), jax.ShapeDtypeStruct((NDEV, KW, 128), F32)),
        grid_spec=pltpu.PrefetchScalarGridSpec(
            num_scalar_prefetch=1, grid=(NSEG, nb),
            in_specs=[pl.BlockSpec((TMB, D), lambda j, i, o: (i, 0)), vm, vm, vm],
            out_specs=(pl.BlockSpec((TMB, D), lambda j, i, o: (i, o[j])), hbm, hbm, hbm),
            scratch_shapes=[VMEM((NSEG, D, D), BF16), VMEM((1, 128, D), BF16),
                            pltpu.SemaphoreType.DMA((7,)), pltpu.SemaphoreType.DMA((7,)), pltpu.SemaphoreType.DMA((NSEG,)),
                            pltpu.SemaphoreType.DMA((2, 7)), pltpu.SemaphoreType.DMA((2, 7)), pltpu.SemaphoreType.DMA((2,))]),
        compiler_params=_cp(("arbitrary", "arbitrary")),
    )(order, h, w_in, w_co, conv_w)


def _fill_a_ext(i, s_ref, val_ref, glu_ref, hval_ref, hglu_ref):
    ah = hval_ref[...] * _sig(hglu_ref[...])
    s_ref[0, 0:HALO, :] = jnp.where(i > 0, ah, 0.0)

    def fill(rb, carry):
        rows = _rows(rb)
        s_ref[0, pl.ds(pl.multiple_of(HALO + rb * RB, RB), RB), :] = val_ref[rows, :] * _sig(glu_ref[rows, :])
        return carry
    lax.fori_loop(0, TM // RB, fill, 0)


def _halo_prev(seg):
    hb = TM // HALO
    return pl.BlockSpec((HALO, D), lambda i: (jnp.maximum(i * hb - 1, 0), seg))


def _branch_a_fwd(p, cw, conv_b, lg, lb, wq, w_so, w_o):
    T = p.shape[0]
    nt = T // TM

    def body(val_ref, glu_ref, z_ref, hval_ref, hglu_ref, cw_ref, cb_ref, lg_ref, lb_ref, w_ref, wso_ref, wo_ref,
             yc_ref, ya_ref, wqg_ref, s_ref, yp_ref, wq_st, ag_send, ag_recv, ag_loc):
        i = pl.program_id(0)
        ag_start, ag_relay, ag_finish = _allgather_phases([wq_st], lambda a, slot: wqg_ref.at[:, slot],
                                                          ag_send, ag_recv, ag_loc)

        @pl.when(i == 0)
        def _():
            wq_st[0] = wso_ref[...].astype(BF16)
            wq_st[1] = wo_ref[...].astype(BF16)
            ag_start()

        @pl.when(i == nt // 2)
        def _():
            ag_relay()

        _fill_a_ext(i, s_ref, val_ref, glu_ref, hval_ref, hglu_ref)
        _shift_copies(s_ref)

        def conv(rb, carry):
            r0 = rb * RBC
            accs = [jnp.broadcast_to(cb_ref[...], (8, D))] * (RBC // 8)
            for k in range(KW):
                o = 2 + k
                w = cw_ref[k]
                accs = [acc + w * s_ref[o % 8, pl.ds(pl.multiple_of(r0 + 8 * (o // 8 + g), 8), 8), :]
                        for g, acc in enumerate(accs)]
            for g, acc in enumerate(accs):
                yc_ref[pl.ds(pl.multiple_of(r0 + 8 * g, 8), 8), :] = acc
            return carry
        lax.fori_loop(0, TM // RBC, conv, 0)

        def post(rb, carry):
            rows = _rows(rb)
            n, _ = _ln_stats(yc_ref[rows, :])
            l = n * lg_ref[...] + lb_ref[...]
            z = z_ref[rows, :]
            yp_ref[rows, :] = ((l * _sig(l)) * (z * _sig(z))).astype(BF16)
            return carry
        _blocks(TM // RB, post)
        ya_ref[...] = jnp.dot(yp_ref[...], w_ref[...], preferred_element_type=F32)

        @pl.when(i == nt - 1)
        def _():
            ag_finish()

    tile = lambda seg: pl.BlockSpec((TM, D), lambda i: (i, seg))
    row = pl.BlockSpec((1, D), lambda i: (0, 0))
    vm = pl.BlockSpec(memory_space=VMEM)
    return pl.pallas_call(
        body, name="branch_a_fwd", grid=(nt,),
        out_shape=(jax.ShapeDtypeStruct((T, D), F32), jax.ShapeDtypeStruct((T, D), F32),
                   jax.ShapeDtypeStruct((2, NDEV, 128, D), BF16)),
        in_specs=[tile(0), tile(1), tile(2), _halo_prev(0), _halo_prev(1),
                  pl.BlockSpec((KW, 8, D), lambda i: (0, 0, 0)), row, row, row,
                  pl.BlockSpec((None, D, D), lambda i: (0, 0, 0)), vm, vm],
        out_specs=(pl.BlockSpec((TM, D), lambda i: (i, 0)), pl.BlockSpec((TM, D), lambda i: (i, 0)),
                   pl.BlockSpec(memory_space=ANY)),
        scratch_shapes=[VMEM((8, EXT, D), F32), VMEM((TM, D), BF16), VMEM((2, 128, D), BF16),
                        pltpu.SemaphoreType.DMA((1, 7)), pltpu.SemaphoreType.DMA((1, 7)), pltpu.SemaphoreType.DMA((1,))],
        compiler_params=_cp(("arbitrary",)),
    )(p, p, p, p, p, cw, conv_b, lg, lb, wq, w_so, w_o)


def _masked_ws(ws_ref, wt_ref):
    tri = lax.broadcasted_iota(jnp.int32, (CHUNK, CHUNK), 0) >= lax.broadcasted_iota(jnp.int32, (CHUNK, CHUNK), 1)
    for h in range(HEADS):
        wt_ref[h] = jnp.where(tri, ws_ref[h], 0.0).astype(BF16)


def _sgu_mix(wt_ref, vl_ref, bst_ref, s_ref):
    for ck in range(TM // CHUNK):
        r = slice(ck * CHUNK, (ck + 1) * CHUNK)
        for h in range(HEADS):
            cs = slice(h * HD, (h + 1) * HD)
            s_ref[r, cs] = jnp.dot(wt_ref[h], vl_ref[r, cs], preferred_element_type=F32) + bst_ref[:, h:h + 1]


def _branch_b_fwd(p, ws, bst, sg, sb, wq):
    T = p.shape[0]

    def body(pu_ref, pv_ref, pz_ref, ws_ref, bst_ref, sg_ref, sb_ref, w_ref, yb_ref,
             wt_ref, vl_ref, t_ref, s_ref, yp_ref):
        _masked_ws(ws_ref, wt_ref)

        def pre(rb, carry):
            rows = _rows(rb)
            vg, _ = _gelu_parts(pv_ref[rows, :])
            vn, _ = _ln_stats(vg)
            vl_ref[rows, :] = (vn * sg_ref[...] + sb_ref[...]).astype(BF16)
            u, _ = _gelu_parts(pu_ref[rows, :])
            z = pz_ref[rows, :]
            t_ref[rows, :] = u * (z * _sig(z))
            return carry
        _blocks(TM // RB, pre)
        _sgu_mix(wt_ref, vl_ref, bst_ref, s_ref)

        def post(rb, carry):
            rows = _rows(rb)
            yp_ref[rows, :] = (t_ref[rows, :] * s_ref[rows, :]).astype(BF16)
            return carry
        _blocks(TM // RB, post)
        yb_ref[...] = jnp.dot(yp_ref[...], w_ref[...], preferred_element_type=F32)

    tile = lambda seg: pl.BlockSpec((TM, D), lambda i: (i, seg))
    row = pl.BlockSpec((1, D), lambda i: (0, 0))
    return pl.pallas_call(
        body, name="branch_b_fwd", grid=(T // TM,),
        out_shape=jax.ShapeDtypeStruct((T, D), F32),
        in_specs=[tile(3), tile(4), tile(5), pl.BlockSpec((HEADS, CHUNK, CHUNK), lambda i: (0, 0, 0)),
                  pl.BlockSpec((CHUNK, HEADS), lambda i: (0, 0)), row, row,
                  pl.BlockSpec((None, D, D), lambda i: (W_SO, 0, 0))],
        out_specs=pl.BlockSpec((TM, D), lambda i: (i, 0)),
        scratch_shapes=[VMEM((HEADS, CHUNK, CHUNK), BF16), VMEM((TM, D), BF16), VMEM((TM, D), F32),
                        VMEM((TM, D), F32), VMEM((TM, D), BF16)],
        compiler_params=_cp(("parallel",)),
    )(p, p, p, ws, bst, sg, sb, wq)


def _merge_loss(p, ya, yb, x, tgt, mod, g_final, wq):
    T = x.shape[0]
    nt = T // TM

    def body(ga_ref, gb_ref, ya_ref, yb_ref, x_ref, t_ref, mod_ref, gf_ref, w_ref,
             dx2_ref, dya_ref, dyb_ref, dp_ref, gwo_ref, st_ref,
             mrg_s, out_s, dout_s, dm_s, gw_acc):
        i = pl.program_id(0)

        @pl.when(i == 0)
        def _():
            gw_acc[...] = jnp.zeros_like(gw_acc)
            st_ref[...] = jnp.zeros_like(st_ref)

        def merge(rb, carry):
            rows = _rows(rb)
            mrg_s[rows, :] = (_sig(ga_ref[rows, :]) * ya_ref[rows, :] + _sig(gb_ref[rows, :]) * yb_ref[rows, :]).astype(BF16)
            return carry
        lax.fori_loop(0, TM // RB, merge, 0)
        out_s[...] = jnp.dot(mrg_s[...], w_ref[...], preferred_element_type=F32)

        def head(rb, carry):
            loss, gg, dg = carry
            rows = _rows(rb)
            gate = mod_ref[2:3, :]
            gf = gf_ref[...]
            out = out_s[rows, :]
            x2 = x_ref[rows, :] + gate * out
            r2 = lax.rsqrt(jnp.mean(x2 * x2, axis=-1, keepdims=True) + EPS)
            x2n = x2 * r2
            diff = x2n * gf - t_ref[rows, :]
            dy = diff * (1.0 / D)
            dx2n = dy * gf
            dx2 = r2 * (dx2n - x2n * jnp.mean(dx2n * x2n, axis=-1, keepdims=True))
            dx2_ref[rows, :] = dx2
            dout_s[rows, :] = (dx2 * gate).astype(BF16)
            return loss + _fold8(diff * diff), gg + _fold8(dy * x2n), dg + _fold8(dx2 * out)
        zero = jnp.zeros((8, D), F32)
        loss, gg, dg = _blocks(TM // RB, head, (zero, zero, zero))
        st_ref[0] = st_ref[0] + loss * (0.5 / D)
        st_ref[1] = st_ref[1] + gg
        st_ref[2] = st_ref[2] + dg

        dm_s[...] = lax.dot_general(dout_s[...], w_ref[...], NT, preferred_element_type=F32)
        gw_acc[...] += lax.dot_general(mrg_s[...], dout_s[...], TN, preferred_element_type=F32)

        def split(rb, carry):
            rows = _rows(rb)
            dm = dm_s[rows, :]
            sa = _sig(ga_ref[rows, :])
            sb = _sig(gb_ref[rows, :])
            dya_ref[rows, :] = (dm * sa).astype(BF16)
            dyb_ref[rows, :] = (dm * sb).astype(BF16)
            dp_ref[0, rows, :] = (dm * ya_ref[rows, :] * (sa * (1.0 - sa))).astype(BF16)
            dp_ref[1, rows, :] = (dm * yb_ref[rows, :] * (sb * (1.0 - sb))).astype(BF16)
            return carry
        lax.fori_loop(0, TM // RB, split, 0)

        @pl.when(i == nt - 1)
        def _():
            gwo_ref[...] = gw_acc[...].astype(BF16)

    tile = pl.BlockSpec((TM, D), lambda i: (i, 0))
    return pl.pallas_call(
        body, name="merge_loss", grid=(nt,),
        out_shape=(jax.ShapeDtypeStruct((T, D), F32), jax.ShapeDtypeStruct((T, D), BF16), jax.ShapeDtypeStruct((T, D), BF16),
                   jax.ShapeDtypeStruct((NSEG, T, D), BF16), jax.ShapeDtypeStruct((D, D), BF16),
                   jax.ShapeDtypeStruct((3, 8, D), F32)),
        in_specs=[pl.BlockSpec((TM, D), lambda i: (i, 6)), pl.BlockSpec((TM, D), lambda i: (i, 7)), tile, tile, tile, tile,
                  pl.BlockSpec((3, D), lambda i: (0, 0)), pl.BlockSpec((1, D), lambda i: (0, 0)),
                  pl.BlockSpec((None, D, D), lambda i: (W_O, 0, 0))],
        out_specs=(tile, tile, tile, pl.BlockSpec((2, TM, D), lambda i: (3, i, 0)),
                   pl.BlockSpec((D, D), lambda i: (0, 0)), pl.BlockSpec((3, 8, D), lambda i: (0, 0, 0))),
        scratch_shapes=[VMEM((TM, D), BF16), VMEM((TM, D), F32), VMEM((TM, D), BF16), VMEM((TM, D), F32), VMEM((D, D), F32)],
        compiler_params=_cp(("arbitrary",)),
    )(p, p, ya, yb, x, tgt, mod, g_final, wq)


def _branch_b_bwd(dp, dyb, p, ws, bst, sg, sb, wq):
    T = p.shape[0]
    nt = T // TM

    def body(dp_in, dyb_ref, pu_ref, pv_ref, pz_ref, ws_ref, bst_ref, sg_ref, sb_ref, w_ref,
             dp_ref, gw_ref, gws_ref, gbs_ref, st_ref,
             wt_ref, d_s, vl_s, vn_s, gpv_s, rs_s, s_s, ds_s, ds32_s, yp_s, dvl_s, gw_acc):
        del dp_in
        i = pl.program_id(0)

        @pl.when(i == 0)
        def _():
            gw_acc[...] = jnp.zeros_like(gw_acc)
            gws_ref[...] = jnp.zeros_like(gws_ref)
            gbs_ref[...] = jnp.zeros_like(gbs_ref)
            st_ref[...] = jnp.zeros_like(st_ref)
            _masked_ws(ws_ref, wt_ref)

        d_s[...] = lax.dot_general(dyb_ref[...], w_ref[...], NT, preferred_element_type=F32)

        def pre(rb, carry):
            rows = _rows(rb)
            vg, gpv = _gelu_parts(pv_ref[rows, :])
            vn, rs = _ln_stats(vg)
            vl_s[rows, :] = (vn * sg_ref[...] + sb_ref[...]).astype(BF16)
            vn_s[rows, :] = vn
            gpv_s[rows, :] = gpv
            rs_s[rows, :] = rs
            return carry
        _blocks(TM // RB, pre)
        _sgu_mix(wt_ref, vl_s, bst_ref, s_s)

        def mid(rb, carry):
            rows = _rows(rb)
            u, gpu = _gelu_parts(pu_ref[rows, :])
            z = pz_ref[rows, :]
            sz = _sig(z)
            siluz = z * sz
            d = d_s[rows, :]
            s = s_s[rows, :]
            t = u * siluz
            yp_s[rows, :] = (t * s).astype(BF16)
            dp_ref[0, rows, :] = (d * s * siluz * gpu).astype(BF16)
            ds = d * t
            ds32_s[rows, :] = ds
            ds_s[rows, :] = ds.astype(BF16)
            dp_ref[2, rows, :] = (d * u * s * (sz * (1.0 + z * (1.0 - sz)))).astype(BF16)
            return carry
        lax.fori_loop(0, TM // RB, mid, 0)

        for ck in range(TM // CHUNK):
            r = slice(ck * CHUNK, (ck + 1) * CHUNK)
            for h in range(HEADS):
                cs = slice(h * HD, (h + 1) * HD)
                dsh = ds_s[r, cs]
                dvl_s[r, cs] = lax.dot_general(wt_ref[h], dsh, TN, preferred_element_type=F32)
                gws_ref[h] += lax.dot_general(dsh, vl_s[r, cs], NT, preferred_element_type=F32)
                gbs_ref[h] += ds32_s[r, cs]

        def post(rb, carry):
            g_sg, g_sb = carry
            rows = _rows(rb)
            dvl = dvl_s[rows, :]
            vn = vn_s[rows, :]
            dvg = _ln_bwd(dvl * sg_ref[...], vn, rs_s[rows, :])
            dp_ref[1, rows, :] = (dvg * gpv_s[rows, :]).astype(BF16)
            return g_sg + _fold8(dvl * vn), g_sb + _fold8(dvl)
        zero = jnp.zeros((8, D), F32)
        g_sg, g_sb = _blocks(TM // RB, post, (zero, zero))
        st_ref[0] = st_ref[0] + g_sg
        st_ref[1] = st_ref[1] + g_sb

        gw_acc[...] += lax.dot_general(yp_s[...], dyb_ref[...], TN, preferred_element_type=F32)

        @pl.when(i == nt - 1)
        def _():
            gw_ref[...] = gw_acc[...].astype(BF16)

    tile = lambda seg: pl.BlockSpec((TM, D), lambda i: (i, seg))
    row = pl.BlockSpec((1, D), lambda i: (0, 0))
    hh = pl.BlockSpec((HEADS, CHUNK, CHUNK), lambda i: (0, 0, 0))
    return pl.pallas_call(
        body, name="branch_b_bwd", grid=(nt,),
        out_shape=(jax.ShapeDtypeStruct((NSEG, T, D), BF16), jax.ShapeDtypeStruct((D, D), BF16),
                   jax.ShapeDtypeStruct((HEADS, CHUNK, CHUNK), F32), jax.ShapeDtypeStruct((HEADS, CHUNK, HD), F32),
                   jax.ShapeDtypeStruct((2, 8, D), F32)),
        in_specs=[pl.BlockSpec(memory_space=ANY), pl.BlockSpec((TM, D), lambda i: (i, 0)), tile(3), tile(4), tile(5),
                  hh, pl.BlockSpec((CHUNK, HEADS), lambda i: (0, 0)), row, row,
                  pl.BlockSpec((None, D, D), lambda i: (W_SO, 0, 0))],
        out_specs=(pl.BlockSpec((3, TM, D), lambda i: (1, i, 0)), pl.BlockSpec((D, D), lambda i: (0, 0)), hh, hh,
                   pl.BlockSpec((2, 8, D), lambda i: (0, 0, 0))),
        scratch_shapes=[VMEM((HEADS, CHUNK, CHUNK), BF16), VMEM((TM, D), F32), VMEM((TM, D), BF16), VMEM((TM, D), F32),
                        VMEM((TM, D), F32), VMEM((TM, 1), F32), VMEM((TM, D), F32), VMEM((TM, D), BF16), VMEM((TM, D), F32),
                        VMEM((TM, D), BF16), VMEM((TM, D), F32), VMEM((D, D), F32)],
        input_output_aliases={0: 0},
        compiler_params=_cp(("arbitrary",)),
    )(dp, dyb, p, p, p, ws, bst, sg, sb, wq)


def _branch_a_bwd1(dp, dya, yc, p, lg, lb, wq):
    T = p.shape[0]
    nt = T // TM

    def body(dp_in, dya_ref, yc_ref, z_ref, lg_ref, lb_ref, w_ref, dp_ref, dyc_ref, gw_ref, st_ref,
             d_s, yp_s, gw_acc):
        del dp_in
        i = pl.program_id(0)

        @pl.when(i == 0)
        def _():
            gw_acc[...] = jnp.zeros_like(gw_acc)
            st_ref[...] = jnp.zeros_like(st_ref)

        d_s[...] = lax.dot_general(dya_ref[...], w_ref[...], NT, preferred_element_type=F32)

        def blk(rb, carry):
            g_lg, g_lb, g_cb = carry
            rows = _rows(rb)
            n, rs = _ln_stats(yc_ref[rows, :])
            l = n * lg_ref[...] + lb_ref[...]
            sgl = _sig(l)
            sl = l * sgl
            z = z_ref[rows, :]
            sz = _sig(z)
            siluz = z * sz
            d = d_s[rows, :]
            yp_s[rows, :] = (sl * siluz).astype(BF16)
            dp_ref[rows, :] = (d * sl * (sz * (1.0 + z * (1.0 - sz)))).astype(BF16)
            dl = (d * siluz) * (sgl * (1.0 + l * (1.0 - sgl)))
            dyc = _ln_bwd(dl * lg_ref[...], n, rs)
            dyc_ref[rows, :] = dyc
            return g_lg + _fold8(dl * n), g_lb + _fold8(dl), g_cb + _fold8(dyc)
        zero = jnp.zeros((8, D), F32)
        g_lg, g_lb, g_cb = _blocks(TM // RB, blk, (zero, zero, zero))
        st_ref[0] = st_ref[0] + g_lg
        st_ref[1] = st_ref[1] + g_lb
        st_ref[2] = st_ref[2] + g_cb

        gw_acc[...] += lax.dot_general(yp_s[...], dya_ref[...], TN, preferred_element_type=F32)

        @pl.when(i == nt - 1)
        def _():
            gw_ref[...] = gw_acc[...].astype(BF16)

    tile = pl.BlockSpec((TM, D), lambda i: (i, 0))
    row = pl.BlockSpec((1, D), lambda i: (0, 0))
    return pl.pallas_call(
        body, name="branch_a_bwd1", grid=(nt,),
        out_shape=(jax.ShapeDtypeStruct((NSEG, T, D), BF16), jax.ShapeDtypeStruct((T, D), F32),
                   jax.ShapeDtypeStruct((D, D), BF16), jax.ShapeDtypeStruct((3, 8, D), F32)),
        in_specs=[pl.BlockSpec(memory_space=ANY), tile, tile, pl.BlockSpec((TM, D), lambda i: (i, 2)), row, row,
                  pl.BlockSpec((None, D, D), lambda i: (0, 0, 0))],
        out_specs=(pl.BlockSpec((None, TM, D), lambda i: (2, i, 0)), tile, pl.BlockSpec((D, D), lambda i: (0, 0)),
                   pl.BlockSpec((3, 8, D), lambda i: (0, 0, 0))),
        scratch_shapes=[VMEM((TM, D), F32), VMEM((TM, D), BF16), VMEM((D, D), F32)],
        input_output_aliases={0: 0},
        compiler_params=_cp(("arbitrary",)),
    )(dp, dya, yc, p, lg, lb, wq)


def _branch_a_bwd2(dp, dyc, p, cw, gw_co, gw_so, gw_o, early):
    T = p.shape[0]
    nt = T // TM
    hb = TM // HALO

    def body(dp_in, dyc_ref, hdyc_ref, val_ref, glu_ref, hval_ref, hglu_ref, cw_ref, gco_ref, gso_ref, go_ref, e_ref,
             dp_ref, gcw_ref, rq_ref, eg_ref, sa_ref, sd_ref, da_s, x_send, x_recv, x_loc, ag_send, ag_recv, ag_loc):
        del dp_in
        i = pl.program_id(0)
        start_exchange, wait_exchange = _owner_exchange((gco_ref, gso_ref, go_ref), rq_ref, x_send, x_recv, x_loc)
        ag_start, ag_relay, ag_finish = _allgather_phases([e_ref], lambda a, slot: eg_ref.at[slot],
                                                          ag_send, ag_recv, ag_loc)

        @pl.when(i == 0)
        def _():
            gcw_ref[...] = jnp.zeros_like(gcw_ref)
            ag_start()
            start_exchange()

        @pl.when(i == nt // 2)
        def _():
            ag_relay()

        _fill_a_ext(i, sa_ref, val_ref, glu_ref, hval_ref, hglu_ref)
        _shift_copies(sa_ref)
        sd_ref[0, 0:TM, :] = dyc_ref[...]
        sd_ref[0, TM:EXT, :] = jnp.where(i < nt - 1, hdyc_ref[...], 0.0)
        _shift_copies(sd_ref)

        def conv_t(rb, carry):
            r0 = rb * RBC
            accs = [jnp.zeros((8, D), F32)] * (RBC // 8)
            for k in range(KW):
                o = KW - 1 - k
                w = cw_ref[k]
                accs = [acc + w * sd_ref[o % 8, pl.ds(pl.multiple_of(r0 + 8 * (o // 8 + g), 8), 8), :]
                        for g, acc in enumerate(accs)]
            for g, acc in enumerate(accs):
                da_s[pl.ds(pl.multiple_of(r0 + 8 * g, 8), 8), :] = acc
            return carry
        lax.fori_loop(0, TM // RBC, conv_t, 0)

        for k0 in range(0, KW, TAPG):
            taps = list(range(k0, min(k0 + TAPG, KW)))

            def tap_group(rb, accs, taps=taps):
                r0 = rb * 16
                d = dyc_ref[pl.ds(pl.multiple_of(r0, 16), 16), :]
                out = []
                for k, acc in zip(taps, accs):
                    o = 2 + k
                    a = sa_ref[o % 8, pl.ds(pl.multiple_of(r0 + 8 * (o // 8), 8), 16), :]
                    out.append(acc + _fold8(d * a))
                return tuple(out)
            sums = lax.fori_loop(0, TM // 16, tap_group, tuple(jnp.zeros((8, D), F32) for _ in taps))
            for k, s in zip(taps, sums):
                gcw_ref[k] = gcw_ref[k] + s

        def glu_b(rb, carry):
            rows = _rows(rb)
            da = da_s[rows, :]
            sg = _sig(glu_ref[rows, :])
            dp_ref[0, rows, :] = (da * sg).astype(BF16)
            dp_ref[1, rows, :] = (da * val_ref[rows, :] * (sg * (1.0 - sg))).astype(BF16)
            return carry
        lax.fori_loop(0, TM // RB, glu_b, 0)

        @pl.when(i == nt - 1)
        def _():
            wait_exchange()
            ag_finish()

    tile = lambda seg: pl.BlockSpec((TM, D), lambda i: (i, seg))
    hbm = pl.BlockSpec(memory_space=ANY)
    return pl.pallas_call(
        body, name="branch_a_bwd2", grid=(nt,),
        out_shape=(jax.ShapeDtypeStruct((NSEG, T, D), BF16), jax.ShapeDtypeStruct((32, 8, D), F32),
                   jax.ShapeDtypeStruct((NDEV, 3, 128, D), BF16), jax.ShapeDtypeStruct((NDEV, NEARLY, D), F32)),
        in_specs=[hbm, pl.BlockSpec((TM, D), lambda i: (i, 0)),
                  pl.BlockSpec((HALO, D), lambda i: (jnp.minimum((i + 1) * hb, nt * hb - 1), 0)),
                  tile(0), tile(1), _halo_prev(0), _halo_prev(1), pl.BlockSpec((KW, 8, D), lambda i: (0, 0, 0)),
                  hbm, hbm, hbm, hbm],
        out_specs=(pl.BlockSpec((2, TM, D), lambda i: (0, i, 0)), pl.BlockSpec((32, 8, D), lambda i: (0, 0, 0)), hbm, hbm),
        scratch_shapes=[VMEM((8, EXT, D), F32), VMEM((8, EXT, D), F32), VMEM((TM, D), F32),
                        pltpu.SemaphoreType.DMA((3, 7)), pltpu.SemaphoreType.DMA((3, 7)), pltpu.SemaphoreType.DMA((3,)),
                        pltpu.SemaphoreType.DMA((1, 7)), pltpu.SemaphoreType.DMA((1, 7)), pltpu.SemaphoreType.DMA((1,))],
        input_output_aliases={0: 0},
        compiler_params=_cp(("arbitrary",)),
    )(dp, dyc, dyc, p, p, p, p, cw, gw_co, gw_so, gw_o, early)


def _in_bwd_dx(dp, wing, x, dx2, mod, g_pre):
    T = x.shape[0]
    nt = T // TM

    def body(dp_ref, w_ref, x_ref, dx2_ref, mod_ref, g_ref, gx_ref, st_ref, dh_s):
        i = pl.program_id(0)

        @pl.when(i == 0)
        def _():
            st_ref[...] = jnp.zeros_like(st_ref)

        dh_s[...] = lax.dot_general(dp_ref[0], w_ref[0], NT, preferred_element_type=F32)
        for j in range(1, NSEG):
            dh_s[...] += lax.dot_general(dp_ref[j], w_ref[j], NT, preferred_element_type=F32)

        def blk(rb, carry):
            d_sh, d_sc, g_g = carry
            rows = _rows(rb)
            xv = x_ref[rows, :]
            r = lax.rsqrt(jnp.mean(xv * xv, axis=-1, keepdims=True) + EPS)
            xn = xv * r
            g = g_ref[...]
            hpre = xn * g
            dh = dh_s[rows, :]
            dhp = dh * (1.0 + mod_ref[1:2, :])
            dxn = dhp * g
            gx_ref[rows, :] = dx2_ref[rows, :] + r * (dxn - xn * jnp.mean(dxn * xn, axis=-1, keepdims=True))
            return d_sh + _fold8(dh), d_sc + _fold8(dh * hpre), g_g + _fold8(dhp * xn)
        zero = jnp.zeros((8, D), F32)
        d_sh, d_sc, g_g = _blocks(TM // RB, blk, (zero, zero, zero))
        st_ref[0] = st_ref[0] + d_sh
        st_ref[1] = st_ref[1] + d_sc
        st_ref[2] = st_ref[2] + g_g

    tile = pl.BlockSpec((TM, D), lambda i: (i, 0))
    return pl.pallas_call(
        body, name="in_bwd_dx", grid=(nt,),
        out_shape=(jax.ShapeDtypeStruct((T, D), F32), jax.ShapeDtypeStruct((3, 8, D), F32)),
        in_specs=[pl.BlockSpec((NSEG, TM, D), lambda i: (0, i, 0)),
                  pl.BlockSpec((NSEG, D, D), lambda i: (0, 0, 0), pipeline_mode=pl.Buffered(1)),
                  tile, tile, pl.BlockSpec((3, D), lambda i: (0, 0)), pl.BlockSpec((1, D), lambda i: (0, 0))],
        out_specs=(tile, pl.BlockSpec((3, 8, D), lambda i: (0, 0, 0))),
        scratch_shapes=[VMEM((TM, D), F32)],
        compiler_params=_cp(("arbitrary",), vmem_mb=56),
    )(dp, wing, x, dx2, mod, g_pre)


RS_ORDER = (5, 4, 3, 2, 7, 6, 1, 0)
RS_HALF = D // 2


def _in_bwd_dw(h, dp, order, smalls):
    T = h.shape[0]
    tmw = min(2 * TMB, T)
    nb = T // tmw
    nu = 2 * NSEG

    def body(order_ref, h_ref, dp_ref, sm_ref, g_ref, smg_ref, acc, sbuf, rsib, rici, send_sems, recv_sems,
             ag_send, ag_recv, ag_loc):
        j = pl.program_id(0)
        i = pl.program_id(1)
        x, y, c = _me()
        sibling = (x, y, 1 - c)
        chips = [(1 - x, y), (x, 1 - y), (1 - x, 1 - y)]
        ag_start, ag_relay, ag_finish = _allgather_phases([sm_ref], lambda a, slot: smg_ref.at[slot],
                                                          ag_send, ag_recv, ag_loc)

        def to_sibling(s):
            return pltpu.make_async_remote_copy(src_ref=sbuf.at[s], dst_ref=rsib.at[s], send_sem=send_sems.at[s],
                                                recv_sem=recv_sems.at[s], device_id=sibling, device_id_type=MESH)

        def to_owner(t):
            return pltpu.make_async_remote_copy(src_ref=sbuf.at[8 + t], dst_ref=rici.at[t], send_sem=send_sems.at[8 + t],
                                                recv_sem=recv_sems.at[8 + t], device_id=(*chips[t // 2], c),
                                                device_id_type=MESH)

        @pl.when((j == 0) & (i == 0))
        def _():
            ag_start()

        @pl.when((j == 4) & (i == 0))
        def _():
            ag_relay()

        @pl.when(i == 0)
        def _():
            acc[...] = jnp.zeros_like(acc)
        acc[...] += lax.dot_general(h_ref[...], dp_ref[...], TN, preferred_element_type=F32)

        for jj in range(nu):
            chip, half, mine = jj // 4, (jj % 4) // 2, jj % 2
            s = 2 * chip + half

            @pl.when((j == jj) & (i == nb - 1))
            def _(chip=chip, half=half, mine=mine, s=s):
                if not mine:
                    sbuf[s] = acc[...].astype(BF16)
                    to_sibling(s).start()
                elif chip < 3:
                    to_sibling(s).wait_recv()
                    sbuf[8 + s] = (acc[...] + rsib[s].astype(F32)).astype(BF16)
                    to_owner(s).start()
                else:
                    to_sibling(s).wait_recv()
                    g = acc[...] + rsib[s].astype(F32)
                    for q in range(3):
                        to_owner(2 * q + half).wait_recv()
                        g = g + rici[2 * q + half].astype(F32)
                    g_ref[:, half * RS_HALF:(half + 1) * RS_HALF] = g

        @pl.when((j == nu - 1) & (i == nb - 1))
        def _():
            for s in range(8):
                to_sibling(s).wait_send()
            for t in range(6):
                to_owner(t).wait_send()
            ag_finish()

    hbm = pl.BlockSpec(memory_space=ANY)
    return pl.pallas_call(
        body, name="in_bwd_dw",
        out_shape=(jax.ShapeDtypeStruct((D, D), F32), jax.ShapeDtypeStruct((NDEV, NLATE, D), F32)),
        grid_spec=pltpu.PrefetchScalarGridSpec(
            num_scalar_prefetch=1, grid=(nu, nb),
            in_specs=[pl.BlockSpec((tmw, D), lambda j, i, o: (i, 0)),
                      pl.BlockSpec((None, tmw, RS_HALF), lambda j, i, o: (o[2 * (j // 4) + j % 2], i, (j % 4) // 2)),
                      hbm],
            out_specs=(pl.BlockSpec((D, D), lambda j, i, o: (0, 0)), hbm),
            scratch_shapes=[VMEM((D, RS_HALF), F32), VMEM((14, D, RS_HALF), BF16), VMEM((8, D, RS_HALF), BF16),
                            VMEM((6, D, RS_HALF), BF16), pltpu.SemaphoreType.DMA((14,)), pltpu.SemaphoreType.DMA((14,)),
                            pltpu.SemaphoreType.DMA((1, 7)), pltpu.SemaphoreType.DMA((1, 7)), pltpu.SemaphoreType.DMA((1,))]),
        compiler_params=_cp(("arbitrary", "arbitrary"), vmem_mb=56),
    )(order, h, dp, smalls)


def _pack_early(st_a, st_b, st_m, gws, gbs):
    def body(sa_ref, sb_ref, sm_ref, gws_ref, gbs_ref, out_ref):
        out_ref[...] = jnp.zeros_like(out_ref)
        fold = lambda v: jnp.sum(v, axis=0, keepdims=True)
        out_ref[0:1, :] = fold(sm_ref[2])
        out_ref[1:2, :] = fold(sa_ref[2])
        out_ref[2:3, :] = fold(sa_ref[0])
        out_ref[3:4, :] = fold(sa_ref[1])
        out_ref[4:5, :] = fold(sb_ref[0])
        out_ref[5:6, :] = fold(sb_ref[1])
        out_ref[7:8, :] = fold(sm_ref[1])
        out_ref[8:9, :] = fold(sm_ref[0])
        tri = lax.broadcasted_iota(jnp.int32, (CHUNK, CHUNK), 0) >= lax.broadcasted_iota(jnp.int32, (CHUNK, CHUNK), 1)
        for h in range(HEADS):
            out_ref[16:16 + CHUNK, h * CHUNK:(h + 1) * CHUNK] = jnp.where(tri, gws_ref[h], 0.0)
            out_ref[6:7, h * CHUNK:(h + 1) * CHUNK] = fold(gbs_ref[h].T)

    vm = pl.BlockSpec(memory_space=VMEM)
    return pl.pallas_call(
        body, name="pack_early", out_shape=jax.ShapeDtypeStruct((NEARLY, D), F32),
        in_specs=[vm] * 5, out_specs=vm, compiler_params=_cp(vmem_mb=32),
    )(st_a, st_b, st_m, gws, gbs)


def _pack_late(st_x, gcw):
    def body(sx_ref, gcw_ref, out_ref):
        out_ref[...] = jnp.zeros_like(out_ref)
        fold = lambda v: jnp.sum(v, axis=0, keepdims=True)
        for r in range(3):
            out_ref[r:r + 1, :] = fold(sx_ref[r])
        for k in range(KW):
            out_ref[8 + k:9 + k, :] = fold(gcw_ref[k])

    vm = pl.BlockSpec(memory_space=VMEM)
    return pl.pallas_call(
        body, name="pack_late", out_shape=jax.ShapeDtypeStruct((NLATE, D), F32),
        in_specs=[vm] * 2, out_specs=vm, compiler_params=_cp(vmem_mb=32),
    )(st_x, gcw)


def _pack_params(name, b_ada, g_pre, conv_b, lg, lb, sg, sb, b_sgu, g_final, w_sgu_t):
    def body(ba_ref, *refs):
        rows, ws_ref, out_ref = refs[:8], refs[8], refs[9]
        out_ref[...] = jnp.zeros_like(out_ref)
        for r in range(3):
            out_ref[r:r + 1, :] = ba_ref[:, r * D:(r + 1) * D]
        for r, ref in enumerate(rows):
            out_ref[3 + r:4 + r, :] = ref[...]
        out_ref[ROW_WS:ROW_WS + CHUNK, :] = ws_ref[...]

    vm = pl.BlockSpec(memory_space=VMEM)
    return pl.pallas_call(
        body, name=name, out_shape=jax.ShapeDtypeStruct((NSMALL, D), F32),
        in_specs=[vm] * 10, out_specs=vm, compiler_params=_cp(vmem_mb=32),
    )(b_ada, g_pre, conv_b, lg, lb, sg, sb, b_sgu, g_final, w_sgu_t)


def _adam(w, g, m, v):
    m2 = ADAM_B1 * m + (1.0 - ADAM_B1) * g
    v2 = ADAM_B2 * v + (1.0 - ADAM_B2) * (g * g)
    m_hat = m2 / (1.0 - ADAM_B1 ** ADAM_STEP)
    v_hat = v2 / (1.0 - ADAM_B2 ** ADAM_STEP)
    delta = -ADAM_LR * (m_hat / (jnp.sqrt(v_hat) + ADAM_EPS) + ADAM_WD * w)
    return delta, m2, v2


def _small_finish(eg, lg, w, m, v):
    def body(eg_ref, lg_ref, w_ref, m_ref, v_ref, loss_ref, g_ref, d_ref, m2_ref, v2_ref):
        e = eg_ref[0]
        l = lg_ref[0]
        for s in range(1, NDEV):
            e = e + eg_ref[s]
            l = l + lg_ref[s]
        g_ref[...] = jnp.zeros_like(g_ref)
        g_ref[0:2, :] = l[0:2, :]
        g_ref[2:3, :] = e[0:1, :]
        g_ref[3:4, :] = l[2:3, :]
        g_ref[4:12, :] = e[1:9, :]
        g_ref[ROW_CW:ROW_CW + KW, :] = l[8:8 + KW, :]
        g_ref[ROW_WS:ROW_WS + CHUNK, :] = e[16:16 + CHUNK, :]
        loss_ref[...] = jnp.sum(e[8:9, :], axis=1, keepdims=True)
        d_ref[...], m2_ref[...], v2_ref[...] = _adam(w_ref[...], g_ref[...], m_ref[...], v_ref[...])

    vm = pl.BlockSpec(memory_space=VMEM)
    sd = jax.ShapeDtypeStruct((NSMALL, D), F32)
    return pl.pallas_call(
        body, name="small_finish", out_shape=(jax.ShapeDtypeStruct((1, 1), F32), sd, sd, sd, sd),
        in_specs=[vm] * 5, out_specs=(vm,) * 5, compiler_params=_cp(vmem_mb=32),
    )(eg, lg, w, m, v)


def _ada_grad_adam(ct, dm, w, m, v):
    def body(ct_ref, dm_ref, w_ref, m_ref, v_ref, g_ref, d_ref, m2_ref, v2_ref):
        g = ct_ref[:, 0:1] * dm_ref[0:1, :]
        for b in range(1, NDEV):
            g = g + ct_ref[:, b:b + 1] * dm_ref[b:b + 1, :]
        g_ref[...] = g
        d_ref[...], m2_ref[...], v2_ref[...] = _adam(w_ref[...], g, m_ref[...], v_ref[...])

    vm = pl.BlockSpec(memory_space=VMEM)
    sd = jax.ShapeDtypeStruct(w.shape, F32)
    return pl.pallas_call(
        body, name="ada_grad_adam", out_shape=(sd, sd, sd, sd),
        in_specs=[vm] * 5, out_specs=(vm,) * 4, compiler_params=_cp(vmem_mb=32),
    )(ct, dm, w, m, v)


def _adam_f32(name, g, w, m, v, rows=None):
    R, C = w.shape
    rows = R if rows is None else rows

    def body(g_ref, w_ref, m_ref, v_ref, d_ref, m2_ref, v2_ref):
        d_ref[...], m2_ref[...], v2_ref[...] = _adam(w_ref[...], g_ref[...], m_ref[...], v_ref[...])

    tile = pl.BlockSpec((rows, C), lambda i: (i, 0))
    sd = jax.ShapeDtypeStruct(w.shape, F32)
    return pl.pallas_call(
        body, name=name, grid=(R // rows,), out_shape=(sd, sd, sd), in_specs=[tile] * 4, out_specs=(tile,) * 3,
        compiler_params=_cp(("parallel",), vmem_mb=32),
    )(g, w, m, v)


def _adam_reduce(name, recv, w, m, v, rows, recv_spec):
    R, C = w.shape

    def body(r_ref, w_ref, m_ref, v_ref, g_ref, d_ref, m2_ref, v2_ref):
        g = r_ref[0].astype(F32)
        for s in range(1, NDEV):
            g = g + r_ref[s].astype(F32)
        g_ref[...] = g
        d_ref[...], m2_ref[...], v2_ref[...] = _adam(w_ref[...], g, m_ref[...], v_ref[...])

    tile = pl.BlockSpec((rows, C), lambda i: (i, 0))
    sd = jax.ShapeDtypeStruct((R, C), F32)
    return pl.pallas_call(
        body, name=name, grid=(R // rows,), out_shape=(sd, sd, sd, sd),
        in_specs=[recv_spec, tile, tile, tile], out_specs=(tile,) * 4,
        compiler_params=_cp(("parallel",)),
    )(recv, w, m, v)


def kernel(x, c, w_ada, b_ada, g_pre, w_in, conv_w, conv_b, conv_ln_g, conv_ln_b, w_conv_out, sgu_ln_g, sgu_ln_b, w_sgu, b_sgu, w_sgu_out, w_o, g_final, loss_target, m_w_ada, m_b_ada, m_g_pre, m_w_in, m_conv_w, m_conv_b, m_conv_ln_g, m_conv_ln_b, m_w_conv_out, m_sgu_ln_g, m_sgu_ln_b, m_w_sgu, m_b_sgu, m_w_sgu_out, m_w_o, m_g_final, v_w_ada, v_b_ada, v_g_pre, v_w_in, v_conv_w, v_conv_b, v_conv_ln_g, v_conv_ln_b, v_w_conv_out, v_sgu_ln_g, v_sgu_ln_b, v_w_sgu, v_b_sgu, v_w_sgu_out, v_w_o, v_g_final):
    T = x.shape[1]
    assert T % TMB == 0 and x.shape[2] == D
    xs = x[0]
    tgt = loss_target[0]
    my = 4 * lax.axis_index("x") + 2 * lax.axis_index("y") + lax.axis_index("c")

    mod, cg = _ada_fwd(c, w_ada[0], b_ada)
    gf = g_final.reshape(1, D)
    bst = b_sgu[0].T

    h = _prep_h(xs, mod, g_pre)
    p, wing, wcog, cwg = _in_proj(h, w_in[0], my ^ jnp.array(AG_ORDER, jnp.int32), w_conv_out[0], conv_w[0])
    wco = wcog.reshape(1, D, D)
    cw = jnp.broadcast_to(jnp.transpose(cwg, (1, 0, 2)).reshape(KW, 1, D), (KW, 8, D))
    yc, ya, wq2g = _branch_a_fwd(p, cw, conv_b, conv_ln_g, conv_ln_b, wco, w_sgu_out[0], w_o[0])
    wq2 = wq2g.reshape(2, D, D)
    yb = _branch_b_fwd(p, w_sgu[0], bst, sgu_ln_g, sgu_ln_b, wq2)
    dx2, dya, dyb, dp, gw_o, st_m = _merge_loss(p, ya, yb, xs, tgt, mod, gf, wq2)

    dp, gw_so, gws, gbs, st_b = _branch_b_bwd(dp, dyb, p, w_sgu[0], bst, sgu_ln_g, sgu_ln_b, wq2)
    dp, dyc, gw_co, st_a = _branch_a_bwd1(dp, dya, yc, p, conv_ln_g, conv_ln_b, wco)
    early = _pack_early(st_a, st_b, st_m, gws, gbs)
    dp, gcw, rq, eg = _branch_a_bwd2(dp, dyc, p, cw, gw_co, gw_so, gw_o, early)
    grad_x, st_x = _in_bwd_dx(dp, wing, xs, dx2, mod, g_pre)
    late = _pack_late(st_x, gcw)
    g_in, lg = _in_bwd_dw(h, dp, my ^ jnp.array(RS_ORDER, jnp.int32), late)

    def pack(name, b_ada_, g_pre_, conv_b_, lg_, lb_, sg_, sb_, b_sgu_, gfin_, w_sgu_):
        return _pack_params(name, b_ada_, g_pre_, conv_b_, lg_, lb_, sg_, sb_, b_sgu_.reshape(1, D), gfin_.reshape(1, D),
                            jnp.transpose(w_sgu_[0], (1, 0, 2)).reshape(CHUNK, D))

    pw = pack("pack_w", b_ada, g_pre, conv_b, conv_ln_g, conv_ln_b, sgu_ln_g, sgu_ln_b, b_sgu, g_final, w_sgu)
    pm = pack("pack_m", m_b_ada, m_g_pre, m_conv_b, m_conv_ln_g, m_conv_ln_b, m_sgu_ln_g, m_sgu_ln_b, m_b_sgu, m_g_final,
              m_w_sgu)
    pv = pack("pack_v", v_b_ada, v_g_pre, v_conv_b, v_conv_ln_g, v_conv_ln_b, v_sgu_ln_g, v_sgu_ln_b, v_b_sgu, v_g_final,
              v_w_sgu)
    loss11, sg_, sd_, sm_, sv_ = _small_finish(eg, lg, pw, pm, pv)

    def unpack(a):
        return dict(
            b_ada=a[0:3].reshape(1, 3 * D), g_pre=a[3:4], conv_b=a[4:5], conv_ln_g=a[5:6], conv_ln_b=a[6:7],
            sgu_ln_g=a[7:8], sgu_ln_b=a[8:9], b_sgu=a[9:10].reshape(1, HEADS, CHUNK), g_final=a[10],
            w_sgu=jnp.transpose(a[ROW_WS:ROW_WS + CHUNK].reshape(CHUNK, HEADS, CHUNK), (1, 0, 2))[None])
    small = [unpack(a) for a in (sg_, sd_, sm_, sv_)]

    g_cw = lax.dynamic_slice_in_dim(sg_[ROW_CW:ROW_CW + KW], my * 128, 128, axis=1)
    d_cw, m_cw, v_cw = _adam_f32("adam_conv_w", g_cw, conv_w[0], m_conv_w[0], v_conv_w[0])

    wcols = w_ada.shape[2]
    dm_all = jnp.concatenate([lg[:, 0], lg[:, 1], eg[:, 0]], axis=1)
    dm_mine = lax.dynamic_slice_in_dim(dm_all, my * wcols, wcols, axis=1)
    g_ada, d_ada, m_ada, v_ada = _ada_grad_adam(cg.T, dm_mine, w_ada[0], m_w_ada[0], v_w_ada[0])

    d_in, m_in, v_in = _adam_f32("adam_w_in", g_in, w_in[0], m_w_in[0], v_w_in[0], rows=256)
    big = {}
    for j, (nm, w_, m_, v_) in enumerate((("w_conv_out", w_conv_out, m_w_conv_out, v_w_conv_out),
                                          ("w_sgu_out", w_sgu_out, m_w_sgu_out, v_w_sgu_out),
                                          ("w_o", w_o, m_w_o, v_w_o))):
        big[nm] = _adam_reduce("adam_" + nm, rq, w_[0], m_[0], v_[0], 128,
                               pl.BlockSpec((NDEV, None, 128, D), lambda i, j=j: (0, j, 0, 0)))

    per = {
        "w_ada": tuple(a[None] for a in (g_ada, d_ada, m_ada, v_ada)),
        "w_in": tuple(a[None] for a in (g_in, d_in, m_in, v_in)),
        "conv_w": tuple(a[None] for a in (g_cw, d_cw, m_cw, v_cw)),
    }
    for nm in ("w_conv_out", "w_sgu_out", "w_o"):
        per[nm] = tuple(a[None] for a in big[nm])
    for nm in ("b_ada", "g_pre", "conv_b", "conv_ln_g", "conv_ln_b", "sgu_ln_g", "sgu_ln_b", "w_sgu", "b_sgu", "g_final"):
        per[nm] = tuple(s[nm] for s in small)

    order = ["w_ada", "b_ada", "g_pre", "w_in", "conv_w", "conv_b", "conv_ln_g", "conv_ln_b", "w_conv_out",
             "sgu_ln_g", "sgu_ln_b", "w_sgu", "b_sgu", "w_sgu_out", "w_o", "g_final"]
    outs = [loss11.reshape(()), grad_x[None]]
    for part in range(4):
        outs += [per[nm][part] for nm in order]
    return tuple(outs)
```

```python
import jax
import jax.numpy as jnp
from jax import lax
from jax.experimental import pallas as pl
from jax.experimental.pallas import tpu as pltpu

F32 = jnp.float32
BF16 = jnp.bfloat16
MESH = pl.DeviceIdType.MESH
VMEM = pltpu.VMEM
ANY = pl.ANY

D = 1024
NDEV = 8
NSEG = 8
HEADS = 8
HD = 128
CHUNK = 128
KW = 31
HALO = 32
EPS = 1e-6
TM = 256
TMB = 512
RB = 32
UNROLL = 4
RBC = 32
TAPG = 4
EXT = TM + HALO
NSMALL = 176
NEARLY = 144
NLATE = 40
ROW_CW = 16
ROW_WS = 48
W_CO, W_SO, W_O = 0, 1, 2

ADAM_LR = 0.001
ADAM_B1 = 0.9
ADAM_B2 = 0.999
ADAM_EPS = 1e-08
ADAM_WD = 0.01
ADAM_STEP = 10

INV_SQRT2 = 0.7071067811865476
INV_SQRT_2PI = 0.3989422804014327

NT = (((1,), (1,)), ((), ()))
TN = (((0,), (0,)), ((), ()))


def _cp(sem=None, vmem_mb=48):
    return pltpu.CompilerParams(dimension_semantics=sem, vmem_limit_bytes=vmem_mb * 1024 * 1024)


def _me():
    return lax.axis_index("x"), lax.axis_index("y"), lax.axis_index("c")


def _slot(px, py, pc):
    return 4 * px + 2 * py + pc


def _xor_peer(k):
    x, y, c = _me()
    return (1 - x if k & 4 else x, 1 - y if k & 2 else y, 1 - c if k & 1 else c)


def _allgather_phases(srcs, dst_at, send_sems, recv_sems, loc_sems):
    x, y, c = _me()
    me = (x, y, c)
    sibling = (x, y, 1 - c)
    chips = [(1 - x, y), (x, 1 - y), (1 - x, 1 - y)]
    na = len(srcs)

    def copy(a, k, block, to, src=None):
        d = dst_at(a, _slot(*block))
        return pltpu.make_async_remote_copy(
            src_ref=d if src is None else src, dst_ref=d,
            send_sem=send_sems.at[a, k], recv_sem=recv_sems.at[a, k],
            device_id=to, device_id_type=MESH)

    local = [pltpu.make_async_copy(srcs[a], dst_at(a, _slot(*me)), loc_sems.at[a]) for a in range(na)]
    first = []
    for a in range(na):
        first.append(copy(a, 0, me, sibling, src=srcs[a]))
        for j, chip in enumerate(chips):
            first.append(copy(a, 1 + j, me, (*chip, c), src=srcs[a]))
    passed = [copy(a, 4 + j, (*chip, c), sibling) for j, chip in enumerate(chips) for a in range(na)]

    def start():
        for cp in local + first:
            cp.start()

    def relay():
        for j, chip in enumerate(chips):
            for a in range(na):
                copy(a, 1 + j, (*chip, c), me).wait_recv()
                passed[j * na + a].start()

    def finish():
        for a in range(na):
            copy(a, 0, sibling, me).wait_recv()
        for j, chip in enumerate(chips):
            for a in range(na):
                copy(a, 4 + j, (*chip, 1 - c), me).wait_recv()
        for cp in first + passed:
            cp.wait_send()
        for cp in local:
            cp.wait()

    return start, relay, finish


def _allgather(srcs, dst_at, send_sems, recv_sems, loc_sems):
    start, relay, finish = _allgather_phases(srcs, dst_at, send_sems, recv_sems, loc_sems)
    start()
    relay()
    finish()


def _owner_exchange(gq, rq_ref, x_send, x_recv, x_loc):
    mx, my_, mc = _me()
    me = _slot(mx, my_, mc)

    def rows_of(a, slot):
        return gq[a].at[pl.ds(pl.multiple_of(slot * 128, 128), 128), :]

    def exchange(k, a, recv):
        px, py, pc = _xor_peer(k)
        peer = _slot(px, py, pc)
        return pltpu.make_async_remote_copy(
            src_ref=rows_of(a, me if recv else peer), dst_ref=rq_ref.at[peer if recv else me, a],
            send_sem=x_send.at[a, k - 1], recv_sem=x_recv.at[a, k - 1],
            device_id=(px, py, pc), device_id_type=MESH)

    local = [pltpu.make_async_copy(rows_of(a, me), rq_ref.at[me, a], x_loc.at[a]) for a in range(len(gq))]

    def start():
        for cp in local:
            cp.start()
        for k in range(1, NDEV):
            for a in range(len(gq)):
                exchange(k, a, False).start()

    def wait():
        for k in range(1, NDEV):
            for a in range(len(gq)):
                exchange(k, a, True).wait_recv()
        for k in range(1, NDEV):
            for a in range(len(gq)):
                exchange(k, a, False).wait_send()
        for cp in local:
            cp.wait()

    return start, wait


def _sig(v):
    return jax.nn.sigmoid(v)


def _gelu_parts(v):
    cdf = 0.5 * (1.0 + lax.erf(v * INV_SQRT2))
    pdf = jnp.exp(-0.5 * v * v) * INV_SQRT_2PI
    return v * cdf, cdf + v * pdf


def _ln_stats(v):
    mu = jnp.mean(v, axis=-1, keepdims=True)
    xc = v - mu
    var = jnp.mean(xc * xc, axis=-1, keepdims=True)
    rs = lax.rsqrt(var + EPS)
    return xc * rs, rs


def _ln_bwd(dn, n, rs):
    return rs * (dn - jnp.mean(dn, axis=-1, keepdims=True) - n * jnp.mean(dn * n, axis=-1, keepdims=True))


def _fold8(v):
    acc = v[0:8]
    for r in range(8, v.shape[0], 8):
        acc = acc + v[r:r + 8]
    return acc


def _rows(rb, n=RB):
    return pl.ds(pl.multiple_of(rb * n, n), n)


def _blocks(n, body, init=0, unroll=UNROLL):
    def trip(t, carry):
        for u in range(unroll):
            carry = body(t * unroll + u, carry)
        return carry
    return lax.fori_loop(0, n // unroll, trip, init)


def _shift_copies(s_ref):
    n = EXT - 8
    for m in range(1, 8):
        for c0 in range(0, n, 56):
            s_ref[m, c0:c0 + 56, :] = s_ref[0, c0 + m:c0 + m + 56, :]


def _ada_fwd(c, w_ada, b_ada):
    wcols = w_ada.shape[1]

    def body(c_ref, w_ref, b_ref, mod_ref, call_ref, cdst, msrc, mdst, c_send, c_recv, m_send, m_recv):
        x, y, c = _me()
        me = _slot(x, y, c)

        def all_to_all(src_of, dst, ss, rs):
            def cp(k, recv):
                px, py, pc = _xor_peer(k)
                peer = _slot(px, py, pc)
                return pltpu.make_async_remote_copy(
                    src_ref=src_of(me if recv else peer), dst_ref=dst.at[peer if recv else me],
                    send_sem=ss.at[k - 1], recv_sem=rs.at[k - 1], device_id=(px, py, pc), device_id_type=MESH)
            for k in range(1, NDEV):
                cp(k, False).start()
            for k in range(1, NDEV):
                cp(k, True).wait_recv()
            for k in range(1, NDEV):
                cp(k, False).wait_send()

        cdst[me] = c_ref[...]
        all_to_all(lambda s: c_ref, cdst, c_send, c_recv)
        for b in range(NDEV):
            call_ref[b:b + 1, :] = cdst[b]
        m = jnp.dot(call_ref[...], w_ref[...], preferred_element_type=F32, precision=lax.Precision.HIGHEST)
        for b in range(NDEV):
            msrc[b] = m[b:b + 1, :]
        mdst[me] = msrc[me]
        all_to_all(lambda s: msrc.at[s], mdst, m_send, m_recv)
        full = jnp.concatenate([mdst[k] for k in range(NDEV)], axis=1) + b_ref[...]
        for r in range(3):
            mod_ref[r:r + 1, :] = full[:, r * D:(r + 1) * D]

    vm = pl.BlockSpec(memory_space=VMEM)
    return pl.pallas_call(
        body, name="ada_fwd",
        out_shape=(jax.ShapeDtypeStruct((3, D), F32), jax.ShapeDtypeStruct((NDEV, D), F32)),
        in_specs=[vm, vm, vm], out_specs=(vm, vm),
        scratch_shapes=[VMEM((NDEV, 1, D), F32), VMEM((NDEV, 1, wcols), F32), VMEM((NDEV, 1, wcols), F32),
                        pltpu.SemaphoreType.DMA((7,)), pltpu.SemaphoreType.DMA((7,)),
                        pltpu.SemaphoreType.DMA((7,)), pltpu.SemaphoreType.DMA((7,))],
        compiler_params=_cp(vmem_mb=32),
    )(c, w_ada, b_ada)


def _prep_h(x, mod, g_pre):
    T = x.shape[0]

    def body(x_ref, mod_ref, g_ref, h_ref, ht_ref):
        def blk(rb, carry):
            rows = _rows(rb)
            xv = x_ref[rows, :]
            r = lax.rsqrt(jnp.mean(xv * xv, axis=-1, keepdims=True) + EPS)
            h_ref[rows, :] = ((xv * r) * g_ref[...] * (1.0 + mod_ref[1:2, :]) + mod_ref[0:1, :]).astype(BF16)
            return carry
        _blocks(TMB // RB, blk)
        ht_ref[...] = h_ref[...].T

    return pl.pallas_call(
        body, name="prep_h", grid=(T // TMB,),
        out_shape=(jax.ShapeDtypeStruct((T, D), BF16), jax.ShapeDtypeStruct((D, T), BF16)),
        in_specs=[pl.BlockSpec((TMB, D), lambda i: (i, 0)), pl.BlockSpec((3, D), lambda i: (0, 0)),
                  pl.BlockSpec((1, D), lambda i: (0, 0))],
        out_specs=(pl.BlockSpec((TMB, D), lambda i: (i, 0)), pl.BlockSpec((D, TMB), lambda i: (0, i))),
        compiler_params=_cp(("parallel",)),
    )(x, mod, g_pre)


AG_ORDER = (0, 1, 4, 5, 2, 3, 6, 7)


def _in_proj(h, w_in, order, w_co, w_so, w_o, conv_w):
    T = h.shape[0]
    nb = T // TMB

    def body(order_ref, h_ref, w_ref, wco_ref, wso_ref, wo_ref, cw_ref, p_ref, wing_ref, wqg_ref, cwg_ref,
             wbuf, wq_st, send_sems, recv_sems, out_sems, ag_send, ag_recv, ag_loc):
        j = pl.program_id(0)
        i = pl.program_id(1)
        x, y, c = _me()
        me = (x, y, c)
        sibling = (x, y, 1 - c)
        chips = [(1 - x, y), (x, 1 - y), (1 - x, 1 - y)]
        blocks = [me, sibling] + [(*chip, cc) for chip in chips for cc in (c, 1 - c)]
        recv_k = [None, 0, 1, 4, 2, 5, 3, 6]
        ag_start, ag_relay, ag_finish = _allgather_phases(
            [wq_st, cw_ref], lambda a, slot: wqg_ref.at[:, slot] if a == 0 else cwg_ref.at[slot], ag_send, ag_recv, ag_loc)

        def copy(k, block, to):
            d = wbuf.at[_slot(*block)]
            return pltpu.make_async_remote_copy(src_ref=d, dst_ref=d, send_sem=send_sems.at[k], recv_sem=recv_sems.at[k],
                                                device_id=to, device_id_type=MESH)

        def writeback(jj):
            s = _slot(*blocks[jj])
            return pltpu.make_async_copy(wbuf.at[s], wing_ref.at[s], out_sems.at[jj])

        first = [copy(0, me, sibling)] + [copy(1 + q, me, (*chip, c)) for q, chip in enumerate(chips)]
        passed = [copy(4 + q, (*chip, c), sibling) for q, chip in enumerate(chips)]

        @pl.when((j == 0) & (i == 0))
        def _():
            wbuf[_slot(*me)] = w_ref[...].astype(BF16)
            for cp in first:
                cp.start()
            writeback(0).start()
            wq_st[W_CO] = wco_ref[...].astype(BF16)
            wq_st[W_SO] = wso_ref[...].astype(BF16)
            wq_st[W_O] = wo_ref[...].astype(BF16)
            ag_start()

        for jj in range(1, NSEG):
            @pl.when((j == jj) & (i == 0))
            def _(jj=jj):
                copy(recv_k[jj], blocks[jj], me).wait_recv()
                if jj in (2, 4, 6):
                    passed[jj // 2 - 1].start()
                writeback(jj).start()
                if jj == NSEG - 1:
                    ag_relay()

        p_ref[...] = jnp.dot(h_ref[...], wbuf[order_ref[j]], preferred_element_type=F32)

        @pl.when((j == NSEG - 1) & (i == nb - 1))
        def _():
            for cp in first + passed:
                cp.wait_send()
            for jj in range(NSEG):
                writeback(jj).wait()
            ag_finish()

    vm = pl.BlockSpec(memory_space=VMEM)
    hbm = pl.BlockSpec(memory_space=ANY)
    return pl.pallas_call(
        body, name="in_proj",
        out_shape=(jax.ShapeDtypeStruct((T, NSEG * D), F32), jax.ShapeDtypeStruct((NSEG, D, D), BF16),
                   jax.ShapeDtypeStruct((3, NDEV, 128, D), BF16), jax.ShapeDtypeStruct((NDEV, KW, 128), F32)),
        grid_spec=pltpu.PrefetchScalarGridSpec(
            num_scalar_prefetch=1, grid=(NSEG, nb),
            in_specs=[pl.BlockSpec((TMB, D), lambda j, i, o: (i, 0)), vm, vm, vm, vm, vm],
            out_specs=(pl.BlockSpec((TMB, D), lambda j, i, o: (i, o[j])), hbm, hbm, hbm),
            scratch_shapes=[VMEM((NSEG, D, D), BF16), VMEM((3, 128, D), BF16),
                            pltpu.SemaphoreType.DMA((7,)), pltpu.SemaphoreType.DMA((7,)), pltpu.SemaphoreType.DMA((NSEG,)),
                            pltpu.SemaphoreType.DMA((2, 7)), pltpu.SemaphoreType.DMA((2, 7)), pltpu.SemaphoreType.DMA((2,))]),
        compiler_params=_cp(("arbitrary", "arbitrary")),
    )(order, h, w_in, w_co, w_so, w_o, conv_w)


def _fill_a_ext(i, s_ref, val_ref, glu_ref, hval_ref, hglu_ref):
    ah = hval_ref[...] * _sig(hglu_ref[...])
    s_ref[0, 0:HALO, :] = jnp.where(i > 0, ah, 0.0)

    def fill(rb, carry):
        rows = _rows(rb)
        s_ref[0, pl.ds(pl.multiple_of(HALO + rb * RB, RB), RB), :] = val_ref[rows, :] * _sig(glu_ref[rows, :])
        return carry
    lax.fori_loop(0, TM // RB, fill, 0)


def _halo_prev(seg):
    hb = TM // HALO
    return pl.BlockSpec((HALO, D), lambda i: (jnp.maximum(i * hb - 1, 0), seg))


def _branch_a_fwd(p, cw, conv_b, lg, lb, wq):
    T = p.shape[0]

    def body(val_ref, glu_ref, z_ref, hval_ref, hglu_ref, cw_ref, cb_ref, lg_ref, lb_ref, w_ref,
             yc_ref, ya_ref, s_ref, yp_ref):
        i = pl.program_id(0)
        _fill_a_ext(i, s_ref, val_ref, glu_ref, hval_ref, hglu_ref)
        _shift_copies(s_ref)

        def conv(rb, carry):
            r0 = rb * RBC
            accs = [jnp.broadcast_to(cb_ref[...], (8, D))] * (RBC // 8)
            for k in range(KW):
                o = 2 + k
                w = cw_ref[k]
                accs = [acc + w * s_ref[o % 8, pl.ds(pl.multiple_of(r0 + 8 * (o // 8 + g), 8), 8), :]
                        for g, acc in enumerate(accs)]
            for g, acc in enumerate(accs):
                yc_ref[pl.ds(pl.multiple_of(r0 + 8 * g, 8), 8), :] = acc
            return carry
        lax.fori_loop(0, TM // RBC, conv, 0)

        def post(rb, carry):
            rows = _rows(rb)
            n, _ = _ln_stats(yc_ref[rows, :])
            l = n * lg_ref[...] + lb_ref[...]
            z = z_ref[rows, :]
            yp_ref[rows, :] = ((l * _sig(l)) * (z * _sig(z))).astype(BF16)
            return carry
        _blocks(TM // RB, post)
        ya_ref[...] = jnp.dot(yp_ref[...], w_ref[...], preferred_element_type=F32)

    tile = lambda seg: pl.BlockSpec((TM, D), lambda i: (i, seg))
    row = pl.BlockSpec((1, D), lambda i: (0, 0))
    return pl.pallas_call(
        body, name="branch_a_fwd", grid=(T // TM,),
        out_shape=(jax.ShapeDtypeStruct((T, D), F32), jax.ShapeDtypeStruct((T, D), F32)),
        in_specs=[tile(0), tile(1), tile(2), _halo_prev(0), _halo_prev(1),
                  pl.BlockSpec((KW, 8, D), lambda i: (0, 0, 0)), row, row, row,
                  pl.BlockSpec((None, D, D), lambda i: (W_CO, 0, 0))],
        out_specs=(pl.BlockSpec((TM, D), lambda i: (i, 0)), pl.BlockSpec((TM, D), lambda i: (i, 0))),
        scratch_shapes=[VMEM((8, EXT, D), F32), VMEM((TM, D), BF16)],
        compiler_params=_cp(("parallel",)),
    )(p, p, p, p, p, cw, conv_b, lg, lb, wq)


def _masked_ws(ws_ref, wt_ref):
    tri = lax.broadcasted_iota(jnp.int32, (CHUNK, CHUNK), 0) >= lax.broadcasted_iota(jnp.int32, (CHUNK, CHUNK), 1)
    for h in range(HEADS):
        wt_ref[h] = jnp.where(tri, ws_ref[h], 0.0).astype(BF16)


def _sgu_mix(wt_ref, vl_ref, bst_ref, s_ref):
    for ck in range(TM // CHUNK):
        r = slice(ck * CHUNK, (ck + 1) * CHUNK)
        for h in range(HEADS):
            cs = slice(h * HD, (h + 1) * HD)
            s_ref[r, cs] = jnp.dot(wt_ref[h], vl_ref[r, cs], preferred_element_type=F32) + bst_ref[:, h:h + 1]


def _branch_b_fwd(p, ws, bst, sg, sb, wq):
    T = p.shape[0]

    def body(pu_ref, pv_ref, pz_ref, ws_ref, bst_ref, sg_ref, sb_ref, w_ref, yb_ref,
             wt_ref, vl_ref, t_ref, s_ref, yp_ref):
        _masked_ws(ws_ref, wt_ref)

        def pre(rb, carry):
            rows = _rows(rb)
            vg, _ = _gelu_parts(pv_ref[rows, :])
            vn, _ = _ln_stats(vg)
            vl_ref[rows, :] = (vn * sg_ref[...] + sb_ref[...]).astype(BF16)
            u, _ = _gelu_parts(pu_ref[rows, :])
            z = pz_ref[rows, :]
            t_ref[rows, :] = u * (z * _sig(z))
            return carry
        _blocks(TM // RB, pre)
        _sgu_mix(wt_ref, vl_ref, bst_ref, s_ref)

        def post(rb, carry):
            rows = _rows(rb)
            yp_ref[rows, :] = (t_ref[rows, :] * s_ref[rows, :]).astype(BF16)
            return carry
        _blocks(TM // RB, post)
        yb_ref[...] = jnp.dot(yp_ref[...], w_ref[...], preferred_element_type=F32)

    tile = lambda seg: pl.BlockSpec((TM, D), lambda i: (i, seg))
    row = pl.BlockSpec((1, D), lambda i: (0, 0))
    return pl.pallas_call(
        body, name="branch_b_fwd", grid=(T // TM,),
        out_shape=jax.ShapeDtypeStruct((T, D), F32),
        in_specs=[tile(3), tile(4), tile(5), pl.BlockSpec((HEADS, CHUNK, CHUNK), lambda i: (0, 0, 0)),
                  pl.BlockSpec((CHUNK, HEADS), lambda i: (0, 0)), row, row,
                  pl.BlockSpec((None, D, D), lambda i: (W_SO, 0, 0))],
        out_specs=pl.BlockSpec((TM, D), lambda i: (i, 0)),
        scratch_shapes=[VMEM((HEADS, CHUNK, CHUNK), BF16), VMEM((TM, D), BF16), VMEM((TM, D), F32),
                        VMEM((TM, D), F32), VMEM((TM, D), BF16)],
        compiler_params=_cp(("parallel",)),
    )(p, p, p, ws, bst, sg, sb, wq)


def _merge_loss(p, ya, yb, x, tgt, mod, g_final, wq):
    T = x.shape[0]
    nt = T // TM

    def body(ga_ref, gb_ref, ya_ref, yb_ref, x_ref, t_ref, mod_ref, gf_ref, w_ref,
             dx2_ref, dya_ref, dyb_ref, dp_ref, gwo_ref, st_ref,
             mrg_s, out_s, dout_s, dm_s, gw_acc):
        i = pl.program_id(0)

        @pl.when(i == 0)
        def _():
            gw_acc[...] = jnp.zeros_like(gw_acc)
            st_ref[...] = jnp.zeros_like(st_ref)

        def merge(rb, carry):
            rows = _rows(rb)
            mrg_s[rows, :] = (_sig(ga_ref[rows, :]) * ya_ref[rows, :] + _sig(gb_ref[rows, :]) * yb_ref[rows, :]).astype(BF16)
            return carry
        lax.fori_loop(0, TM // RB, merge, 0)
        out_s[...] = jnp.dot(mrg_s[...], w_ref[...], preferred_element_type=F32)

        def head(rb, carry):
            loss, gg, dg = carry
            rows = _rows(rb)
            gate = mod_ref[2:3, :]
            gf = gf_ref[...]
            out = out_s[rows, :]
            x2 = x_ref[rows, :] + gate * out
            r2 = lax.rsqrt(jnp.mean(x2 * x2, axis=-1, keepdims=True) + EPS)
            x2n = x2 * r2
            diff = x2n * gf - t_ref[rows, :]
            dy = diff * (1.0 / D)
            dx2n = dy * gf
            dx2 = r2 * (dx2n - x2n * jnp.mean(dx2n * x2n, axis=-1, keepdims=True))
            dx2_ref[rows, :] = dx2
            dout_s[rows, :] = (dx2 * gate).astype(BF16)
            return loss + _fold8(diff * diff), gg + _fold8(dy * x2n), dg + _fold8(dx2 * out)
        zero = jnp.zeros((8, D), F32)
        loss, gg, dg = _blocks(TM // RB, head, (zero, zero, zero))
        st_ref[0] = st_ref[0] + loss * (0.5 / D)
        st_ref[1] = st_ref[1] + gg
        st_ref[2] = st_ref[2] + dg

        dm_s[...] = lax.dot_general(dout_s[...], w_ref[...], NT, preferred_element_type=F32)
        gw_acc[...] += lax.dot_general(mrg_s[...], dout_s[...], TN, preferred_element_type=F32)

        def split(rb, carry):
            rows = _rows(rb)
            dm = dm_s[rows, :]
            sa = _sig(ga_ref[rows, :])
            sb = _sig(gb_ref[rows, :])
            dya_ref[rows, :] = (dm * sa).astype(BF16)
            dyb_ref[rows, :] = (dm * sb).astype(BF16)
            dp_ref[0, rows, :] = (dm * ya_ref[rows, :] * (sa * (1.0 - sa))).astype(BF16)
            dp_ref[1, rows, :] = (dm * yb_ref[rows, :] * (sb * (1.0 - sb))).astype(BF16)
            return carry
        lax.fori_loop(0, TM // RB, split, 0)

        @pl.when(i == nt - 1)
        def _():
            gwo_ref[...] = gw_acc[...].astype(BF16)

    tile = pl.BlockSpec((TM, D), lambda i: (i, 0))
    return pl.pallas_call(
        body, name="merge_loss", grid=(nt,),
        out_shape=(jax.ShapeDtypeStruct((T, D), F32), jax.ShapeDtypeStruct((T, D), BF16), jax.ShapeDtypeStruct((T, D), BF16),
                   jax.ShapeDtypeStruct((NSEG, T, D), BF16), jax.ShapeDtypeStruct((D, D), BF16),
                   jax.ShapeDtypeStruct((3, 8, D), F32)),
        in_specs=[pl.BlockSpec((TM, D), lambda i: (i, 6)), pl.BlockSpec((TM, D), lambda i: (i, 7)), tile, tile, tile, tile,
                  pl.BlockSpec((3, D), lambda i: (0, 0)), pl.BlockSpec((1, D), lambda i: (0, 0)),
                  pl.BlockSpec((None, D, D), lambda i: (W_O, 0, 0))],
        out_specs=(tile, tile, tile, pl.BlockSpec((2, TM, D), lambda i: (3, i, 0)),
                   pl.BlockSpec((D, D), lambda i: (0, 0)), pl.BlockSpec((3, 8, D), lambda i: (0, 0, 0))),
        scratch_shapes=[VMEM((TM, D), BF16), VMEM((TM, D), F32), VMEM((TM, D), BF16), VMEM((TM, D), F32), VMEM((D, D), F32)],
        compiler_params=_cp(("arbitrary",)),
    )(p, p, ya, yb, x, tgt, mod, g_final, wq)


def _branch_b_bwd(dp, dyb, p, ws, bst, sg, sb, wq):
    T = p.shape[0]
    nt = T // TM

    def body(dp_in, dyb_ref, pu_ref, pv_ref, pz_ref, ws_ref, bst_ref, sg_ref, sb_ref, w_ref,
             dp_ref, gw_ref, gws_ref, gbs_ref, st_ref,
             wt_ref, d_s, vl_s, vn_s, gpv_s, rs_s, s_s, ds_s, ds32_s, yp_s, dvl_s, gw_acc):
        del dp_in
        i = pl.program_id(0)

        @pl.when(i == 0)
        def _():
            gw_acc[...] = jnp.zeros_like(gw_acc)
            gws_ref[...] = jnp.zeros_like(gws_ref)
            gbs_ref[...] = jnp.zeros_like(gbs_ref)
            st_ref[...] = jnp.zeros_like(st_ref)
            _masked_ws(ws_ref, wt_ref)

        d_s[...] = lax.dot_general(dyb_ref[...], w_ref[...], NT, preferred_element_type=F32)

        def pre(rb, carry):
            rows = _rows(rb)
            vg, gpv = _gelu_parts(pv_ref[rows, :])
            vn, rs = _ln_stats(vg)
            vl_s[rows, :] = (vn * sg_ref[...] + sb_ref[...]).astype(BF16)
            vn_s[rows, :] = vn
            gpv_s[rows, :] = gpv
            rs_s[rows, :] = rs
            return carry
        _blocks(TM // RB, pre)
        _sgu_mix(wt_ref, vl_s, bst_ref, s_s)

        def mid(rb, carry):
            rows = _rows(rb)
            u, gpu = _gelu_parts(pu_ref[rows, :])
            z = pz_ref[rows, :]
            sz = _sig(z)
            siluz = z * sz
            d = d_s[rows, :]
            s = s_s[rows, :]
            t = u * siluz
            yp_s[rows, :] = (t * s).astype(BF16)
            dp_ref[0, rows, :] = (d * s * siluz * gpu).astype(BF16)
            ds = d * t
            ds32_s[rows, :] = ds
            ds_s[rows, :] = ds.astype(BF16)
            dp_ref[2, rows, :] = (d * u * s * (sz * (1.0 + z * (1.0 - sz)))).astype(BF16)
            return carry
        lax.fori_loop(0, TM // RB, mid, 0)

        for ck in range(TM // CHUNK):
            r = slice(ck * CHUNK, (ck + 1) * CHUNK)
            for h in range(HEADS):
                cs = slice(h * HD, (h + 1) * HD)
                dsh = ds_s[r, cs]
                dvl_s[r, cs] = lax.dot_general(wt_ref[h], dsh, TN, preferred_element_type=F32)
                gws_ref[h] += lax.dot_general(dsh, vl_s[r, cs], NT, preferred_element_type=F32)
                gbs_ref[h] += ds32_s[r, cs]

        def post(rb, carry):
            g_sg, g_sb = carry
            rows = _rows(rb)
            dvl = dvl_s[rows, :]
            vn = vn_s[rows, :]
            dvg = _ln_bwd(dvl * sg_ref[...], vn, rs_s[rows, :])
            dp_ref[1, rows, :] = (dvg * gpv_s[rows, :]).astype(BF16)
            return g_sg + _fold8(dvl * vn), g_sb + _fold8(dvl)
        zero = jnp.zeros((8, D), F32)
        g_sg, g_sb = _blocks(TM // RB, post, (zero, zero))
        st_ref[0] = st_ref[0] + g_sg
        st_ref[1] = st_ref[1] + g_sb

        gw_acc[...] += lax.dot_general(yp_s[...], dyb_ref[...], TN, preferred_element_type=F32)

        @pl.when(i == nt - 1)
        def _():
            gw_ref[...] = gw_acc[...].astype(BF16)

    tile = lambda seg: pl.BlockSpec((TM, D), lambda i: (i, seg))
    row = pl.BlockSpec((1, D), lambda i: (0, 0))
    hh = pl.BlockSpec((HEADS, CHUNK, CHUNK), lambda i: (0, 0, 0))
    return pl.pallas_call(
        body, name="branch_b_bwd", grid=(nt,),
        out_shape=(jax.ShapeDtypeStruct((NSEG, T, D), BF16), jax.ShapeDtypeStruct((D, D), BF16),
                   jax.ShapeDtypeStruct((HEADS, CHUNK, CHUNK), F32), jax.ShapeDtypeStruct((HEADS, CHUNK, HD), F32),
                   jax.ShapeDtypeStruct((2, 8, D), F32)),
        in_specs=[pl.BlockSpec(memory_space=ANY), pl.BlockSpec((TM, D), lambda i: (i, 0)), tile(3), tile(4), tile(5),
                  hh, pl.BlockSpec((CHUNK, HEADS), lambda i: (0, 0)), row, row,
                  pl.BlockSpec((None, D, D), lambda i: (W_SO, 0, 0))],
        out_specs=(pl.BlockSpec((3, TM, D), lambda i: (1, i, 0)), pl.BlockSpec((D, D), lambda i: (0, 0)), hh, hh,
                   pl.BlockSpec((2, 8, D), lambda i: (0, 0, 0))),
        scratch_shapes=[VMEM((HEADS, CHUNK, CHUNK), BF16), VMEM((TM, D), F32), VMEM((TM, D), BF16), VMEM((TM, D), F32),
                        VMEM((TM, D), F32), VMEM((TM, 1), F32), VMEM((TM, D), F32), VMEM((TM, D), BF16), VMEM((TM, D), F32),
                        VMEM((TM, D), BF16), VMEM((TM, D), F32), VMEM((D, D), F32)],
        input_output_aliases={0: 0},
        compiler_params=_cp(("arbitrary",)),
    )(dp, dyb, p, p, p, ws, bst, sg, sb, wq)


def _branch_a_bwd1(dp, dya, yc, p, lg, lb, wq):
    T = p.shape[0]
    nt = T // TM

    def body(dp_in, dya_ref, yc_ref, z_ref, lg_ref, lb_ref, w_ref, dp_ref, dyc_ref, gw_ref, st_ref,
             d_s, yp_s, gw_acc):
        del dp_in
        i = pl.program_id(0)

        @pl.when(i == 0)
        def _():
            gw_acc[...] = jnp.zeros_like(gw_acc)
            st_ref[...] = jnp.zeros_like(st_ref)

        d_s[...] = lax.dot_general(dya_ref[...], w_ref[...], NT, preferred_element_type=F32)

        def blk(rb, carry):
            g_lg, g_lb, g_cb = carry
            rows = _rows(rb)
            n, rs = _ln_stats(yc_ref[rows, :])
            l = n * lg_ref[...] + lb_ref[...]
            sgl = _sig(l)
            sl = l * sgl
            z = z_ref[rows, :]
            sz = _sig(z)
            siluz = z * sz
            d = d_s[rows, :]
            yp_s[rows, :] = (sl * siluz).astype(BF16)
            dp_ref[rows, :] = (d * sl * (sz * (1.0 + z * (1.0 - sz)))).astype(BF16)
            dl = (d * siluz) * (sgl * (1.0 + l * (1.0 - sgl)))
            dyc = _ln_bwd(dl * lg_ref[...], n, rs)
            dyc_ref[rows, :] = dyc
            return g_lg + _fold8(dl * n), g_lb + _fold8(dl), g_cb + _fold8(dyc)
        zero = jnp.zeros((8, D), F32)
        g_lg, g_lb, g_cb = _blocks(TM // RB, blk, (zero, zero, zero))
        st_ref[0] = st_ref[0] + g_lg
        st_ref[1] = st_ref[1] + g_lb
        st_ref[2] = st_ref[2] + g_cb

        gw_acc[...] += lax.dot_general(yp_s[...], dya_ref[...], TN, preferred_element_type=F32)

        @pl.when(i == nt - 1)
        def _():
            gw_ref[...] = gw_acc[...].astype(BF16)

    tile = pl.BlockSpec((TM, D), lambda i: (i, 0))
    row = pl.BlockSpec((1, D), lambda i: (0, 0))
    return pl.pallas_call(
        body, name="branch_a_bwd1", grid=(nt,),
        out_shape=(jax.ShapeDtypeStruct((NSEG, T, D), BF16), jax.ShapeDtypeStruct((T, D), F32),
                   jax.ShapeDtypeStruct((D, D), BF16), jax.ShapeDtypeStruct((3, 8, D), F32)),
        in_specs=[pl.BlockSpec(memory_space=ANY), tile, tile, pl.BlockSpec((TM, D), lambda i: (i, 2)), row, row,
                  pl.BlockSpec((None, D, D), lambda i: (0, 0, 0))],
        out_specs=(pl.BlockSpec((None, TM, D), lambda i: (2, i, 0)), tile, pl.BlockSpec((D, D), lambda i: (0, 0)),
                   pl.BlockSpec((3, 8, D), lambda i: (0, 0, 0))),
        scratch_shapes=[VMEM((TM, D), F32), VMEM((TM, D), BF16), VMEM((D, D), F32)],
        input_output_aliases={0: 0},
        compiler_params=_cp(("arbitrary",)),
    )(dp, dya, yc, p, lg, lb, wq)


def _branch_a_bwd2(dp, dyc, p, cw, gw_co, gw_so, gw_o, early):
    T = p.shape[0]
    nt = T // TM
    hb = TM // HALO

    def body(dp_in, dyc_ref, hdyc_ref, val_ref, glu_ref, hval_ref, hglu_ref, cw_ref, gco_ref, gso_ref, go_ref, e_ref,
             dp_ref, gcw_ref, rq_ref, eg_ref, sa_ref, sd_ref, da_s, x_send, x_recv, x_loc, ag_send, ag_recv, ag_loc):
        del dp_in
        i = pl.program_id(0)
        start_exchange, wait_exchange = _owner_exchange((gco_ref, gso_ref, go_ref), rq_ref, x_send, x_recv, x_loc)
        ag_start, ag_relay, ag_finish = _allgather_phases([e_ref], lambda a, slot: eg_ref.at[slot],
                                                          ag_send, ag_recv, ag_loc)

        @pl.when(i == 0)
        def _():
            gcw_ref[...] = jnp.zeros_like(gcw_ref)
            ag_start()
            start_exchange()

        @pl.when(i == nt // 2)
        def _():
            ag_relay()

        _fill_a_ext(i, sa_ref, val_ref, glu_ref, hval_ref, hglu_ref)
        _shift_copies(sa_ref)
        sd_ref[0, 0:TM, :] = dyc_ref[...]
        sd_ref[0, TM:EXT, :] = jnp.where(i < nt - 1, hdyc_ref[...], 0.0)
        _shift_copies(sd_ref)

        def conv_t(rb, carry):
            r0 = rb * RBC
            accs = [jnp.zeros((8, D), F32)] * (RBC // 8)
            for k in range(KW):
                o = KW - 1 - k
                w = cw_ref[k]
                accs = [acc + w * sd_ref[o % 8, pl.ds(pl.multiple_of(r0 + 8 * (o // 8 + g), 8), 8), :]
                        for g, acc in enumerate(accs)]
            for g, acc in enumerate(accs):
                da_s[pl.ds(pl.multiple_of(r0 + 8 * g, 8), 8), :] = acc
            return carry
        lax.fori_loop(0, TM // RBC, conv_t, 0)

        for k0 in range(0, KW, TAPG):
            taps = list(range(k0, min(k0 + TAPG, KW)))

            def tap_group(rb, accs, taps=taps):
                r0 = rb * 16
                d = dyc_ref[pl.ds(pl.multiple_of(r0, 16), 16), :]
                out = []
                for k, acc in zip(taps, accs):
                    o = 2 + k
                    a = sa_ref[o % 8, pl.ds(pl.multiple_of(r0 + 8 * (o // 8), 8), 16), :]
                    out.append(acc + _fold8(d * a))
                return tuple(out)
            sums = lax.fori_loop(0, TM // 16, tap_group, tuple(jnp.zeros((8, D), F32) for _ in taps))
            for k, s in zip(taps, sums):
                gcw_ref[k] = gcw_ref[k] + s

        def glu_b(rb, carry):
            rows = _rows(rb)
            da = da_s[rows, :]
            sg = _sig(glu_ref[rows, :])
            dp_ref[0, rows, :] = (da * sg).astype(BF16)
            dp_ref[1, rows, :] = (da * val_ref[rows, :] * (sg * (1.0 - sg))).astype(BF16)
            return carry
        lax.fori_loop(0, TM // RB, glu_b, 0)

        @pl.when(i == nt - 1)
        def _():
            wait_exchange()
            ag_finish()

    tile = lambda seg: pl.BlockSpec((TM, D), lambda i: (i, seg))
    hbm = pl.BlockSpec(memory_space=ANY)
    return pl.pallas_call(
        body, name="branch_a_bwd2", grid=(nt,),
        out_shape=(jax.ShapeDtypeStruct((NSEG, T, D), BF16), jax.ShapeDtypeStruct((32, 8, D), F32),
                   jax.ShapeDtypeStruct((NDEV, 3, 128, D), BF16), jax.ShapeDtypeStruct((NDEV, NEARLY, D), F32)),
        in_specs=[hbm, pl.BlockSpec((TM, D), lambda i: (i, 0)),
                  pl.BlockSpec((HALO, D), lambda i: (jnp.minimum((i + 1) * hb, nt * hb - 1), 0)),
                  tile(0), tile(1), _halo_prev(0), _halo_prev(1), pl.BlockSpec((KW, 8, D), lambda i: (0, 0, 0)),
                  hbm, hbm, hbm, hbm],
        out_specs=(pl.BlockSpec((2, TM, D), lambda i: (0, i, 0)), pl.BlockSpec((32, 8, D), lambda i: (0, 0, 0)), hbm, hbm),
        scratch_shapes=[VMEM((8, EXT, D), F32), VMEM((8, EXT, D), F32), VMEM((TM, D), F32),
                        pltpu.SemaphoreType.DMA((3, 7)), pltpu.SemaphoreType.DMA((3, 7)), pltpu.SemaphoreType.DMA((3,)),
                        pltpu.SemaphoreType.DMA((1, 7)), pltpu.SemaphoreType.DMA((1, 7)), pltpu.SemaphoreType.DMA((1,))],
        input_output_aliases={0: 0},
        compiler_params=_cp(("arbitrary",)),
    )(dp, dyc, dyc, p, p, p, p, cw, gw_co, gw_so, gw_o, early)


def _in_bwd_dx(dp, wing, x, dx2, mod, g_pre):
    T = x.shape[0]
    nt = T // TM

    def body(dp_ref, w_ref, x_ref, dx2_ref, mod_ref, g_ref, gx_ref, st_ref, dh_s):
        i = pl.program_id(0)

        @pl.when(i == 0)
        def _():
            st_ref[...] = jnp.zeros_like(st_ref)

        dh_s[...] = lax.dot_general(dp_ref[0], w_ref[0], NT, preferred_element_type=F32)
        for j in range(1, NSEG):
            dh_s[...] += lax.dot_general(dp_ref[j], w_ref[j], NT, preferred_element_type=F32)

        def blk(rb, carry):
            d_sh, d_sc, g_g = carry
            rows = _rows(rb)
            xv = x_ref[rows, :]
            r = lax.rsqrt(jnp.mean(xv * xv, axis=-1, keepdims=True) + EPS)
            xn = xv * r
            g = g_ref[...]
            hpre = xn * g
            dh = dh_s[rows, :]
            dhp = dh * (1.0 + mod_ref[1:2, :])
            dxn = dhp * g
            gx_ref[rows, :] = dx2_ref[rows, :] + r * (dxn - xn * jnp.mean(dxn * xn, axis=-1, keepdims=True))
            return d_sh + _fold8(dh), d_sc + _fold8(dh * hpre), g_g + _fold8(dhp * xn)
        zero = jnp.zeros((8, D), F32)
        d_sh, d_sc, g_g = _blocks(TM // RB, blk, (zero, zero, zero))
        st_ref[0] = st_ref[0] + d_sh
        st_ref[1] = st_ref[1] + d_sc
        st_ref[2] = st_ref[2] + g_g

    tile = pl.BlockSpec((TM, D), lambda i: (i, 0))
    return pl.pallas_call(
        body, name="in_bwd_dx", grid=(nt,),
        out_shape=(jax.ShapeDtypeStruct((T, D), F32), jax.ShapeDtypeStruct((3, 8, D), F32)),
        in_specs=[pl.BlockSpec((NSEG, TM, D), lambda i: (0, i, 0)),
                  pl.BlockSpec((NSEG, D, D), lambda i: (0, 0, 0), pipeline_mode=pl.Buffered(1)),
                  tile, tile, pl.BlockSpec((3, D), lambda i: (0, 0)), pl.BlockSpec((1, D), lambda i: (0, 0))],
        out_specs=(tile, pl.BlockSpec((3, 8, D), lambda i: (0, 0, 0))),
        scratch_shapes=[VMEM((TM, D), F32)],
        compiler_params=_cp(("arbitrary",), vmem_mb=56),
    )(dp, wing, x, dx2, mod, g_pre)


RS_ORDER = (5, 4, 3, 2, 7, 6, 1, 0)
RS_HALF = D // 2


def _in_bwd_dw(ht, dp, order, smalls):
    T = ht.shape[1]
    tmw = min(2 * TMB, T)
    nb = T // tmw
    nu = 2 * NSEG

    def body(order_ref, ht_ref, dp_ref, sm_ref, g_ref, smg_ref, acc, sbuf, rsib, rici, send_sems, recv_sems,
             ag_send, ag_recv, ag_loc):
        j = pl.program_id(0)
        i = pl.program_id(1)
        x, y, c = _me()
        sibling = (x, y, 1 - c)
        chips = [(1 - x, y), (x, 1 - y), (1 - x, 1 - y)]
        ag_start, ag_relay, ag_finish = _allgather_phases([sm_ref], lambda a, slot: smg_ref.at[slot],
                                                          ag_send, ag_recv, ag_loc)

        def to_sibling(s):
            return pltpu.make_async_remote_copy(src_ref=sbuf.at[s], dst_ref=rsib.at[s], send_sem=send_sems.at[s],
                                                recv_sem=recv_sems.at[s], device_id=sibling, device_id_type=MESH)

        def to_owner(t):
            return pltpu.make_async_remote_copy(src_ref=sbuf.at[8 + t], dst_ref=rici.at[t], send_sem=send_sems.at[8 + t],
                                                recv_sem=recv_sems.at[8 + t], device_id=(*chips[t // 2], c),
                                                device_id_type=MESH)

        @pl.when((j == 0) & (i == 0))
        def _():
            ag_start()

        @pl.when((j == 4) & (i == 0))
        def _():
            ag_relay()

        @pl.when(i == 0)
        def _():
            acc[...] = jnp.zeros_like(acc)
        acc[...] += jnp.dot(ht_ref[...], dp_ref[...], preferred_element_type=F32)

        for jj in range(nu):
            chip, half, mine = jj // 4, (jj % 4) // 2, jj % 2
            s = 2 * chip + half

            @pl.when((j == jj) & (i == nb - 1))
            def _(chip=chip, half=half, mine=mine, s=s):
                if not mine:
                    sbuf[s] = acc[...].astype(BF16)
                    to_sibling(s).start()
                elif chip < 3:
                    to_sibling(s).wait_recv()
                    sbuf[8 + s] = (acc[...] + rsib[s].astype(F32)).astype(BF16)
                    to_owner(s).start()
                else:
                    to_sibling(s).wait_recv()
                    g = acc[...] + rsib[s].astype(F32)
                    for q in range(3):
                        to_owner(2 * q + half).wait_recv()
                        g = g + rici[2 * q + half].astype(F32)
                    g_ref[:, half * RS_HALF:(half + 1) * RS_HALF] = g

        @pl.when((j == nu - 1) & (i == nb - 1))
        def _():
            for s in range(8):
                to_sibling(s).wait_send()
            for t in range(6):
                to_owner(t).wait_send()
            ag_finish()

    hbm = pl.BlockSpec(memory_space=ANY)
    return pl.pallas_call(
        body, name="in_bwd_dw",
        out_shape=(jax.ShapeDtypeStruct((D, D), F32), jax.ShapeDtypeStruct((NDEV, NLATE, D), F32)),
        grid_spec=pltpu.PrefetchScalarGridSpec(
            num_scalar_prefetch=1, grid=(nu, nb),
            in_specs=[pl.BlockSpec((D, tmw), lambda j, i, o: (0, i)),
                      pl.BlockSpec((None, tmw, RS_HALF), lambda j, i, o: (o[2 * (j // 4) + j % 2], i, (j % 4) // 2)),
                      hbm],
            out_specs=(pl.BlockSpec((D, D), lambda j, i, o: (0, 0)), hbm),
            scratch_shapes=[VMEM((D, RS_HALF), F32), VMEM((14, D, RS_HALF), BF16), VMEM((8, D, RS_HALF), BF16),
                            VMEM((6, D, RS_HALF), BF16), pltpu.SemaphoreType.DMA((14,)), pltpu.SemaphoreType.DMA((14,)),
                            pltpu.SemaphoreType.DMA((1, 7)), pltpu.SemaphoreType.DMA((1, 7)), pltpu.SemaphoreType.DMA((1,))]),
        compiler_params=_cp(("arbitrary", "arbitrary"), vmem_mb=56),
    )(order, ht, dp, smalls)


def _pack_early(st_a, st_b, st_m, gws, gbs):
    def body(sa_ref, sb_ref, sm_ref, gws_ref, gbs_ref, out_ref):
        out_ref[...] = jnp.zeros_like(out_ref)
        fold = lambda v: jnp.sum(v, axis=0, keepdims=True)
        out_ref[0:1, :] = fold(sm_ref[2])
        out_ref[1:2, :] = fold(sa_ref[2])
        out_ref[2:3, :] = fold(sa_ref[0])
        out_ref[3:4, :] = fold(sa_ref[1])
        out_ref[4:5, :] = fold(sb_ref[0])
        out_ref[5:6, :] = fold(sb_ref[1])
        out_ref[7:8, :] = fold(sm_ref[1])
        out_ref[8:9, :] = fold(sm_ref[0])
        tri = lax.broadcasted_iota(jnp.int32, (CHUNK, CHUNK), 0) >= lax.broadcasted_iota(jnp.int32, (CHUNK, CHUNK), 1)
        for h in range(HEADS):
            out_ref[16:16 + CHUNK, h * CHUNK:(h + 1) * CHUNK] = jnp.where(tri, gws_ref[h], 0.0)
            out_ref[6:7, h * CHUNK:(h + 1) * CHUNK] = fold(gbs_ref[h].T)

    vm = pl.BlockSpec(memory_space=VMEM)
    return pl.pallas_call(
        body, name="pack_early", out_shape=jax.ShapeDtypeStruct((NEARLY, D), F32),
        in_specs=[vm] * 5, out_specs=vm, compiler_params=_cp(vmem_mb=32),
    )(st_a, st_b, st_m, gws, gbs)


def _pack_late(st_x, gcw):
    def body(sx_ref, gcw_ref, out_ref):
        out_ref[...] = jnp.zeros_like(out_ref)
        fold = lambda v: jnp.sum(v, axis=0, keepdims=True)
        for r in range(3):
            out_ref[r:r + 1, :] = fold(sx_ref[r])
        for k in range(KW):
            out_ref[8 + k:9 + k, :] = fold(gcw_ref[k])

    vm = pl.BlockSpec(memory_space=VMEM)
    return pl.pallas_call(
        body, name="pack_late", out_shape=jax.ShapeDtypeStruct((NLATE, D), F32),
        in_specs=[vm] * 2, out_specs=vm, compiler_params=_cp(vmem_mb=32),
    )(st_x, gcw)


def _pack_params(name, b_ada, g_pre, conv_b, lg, lb, sg, sb, b_sgu, g_final, w_sgu_t):
    def body(ba_ref, *refs):
        rows, ws_ref, out_ref = refs[:8], refs[8], refs[9]
        out_ref[...] = jnp.zeros_like(out_ref)
        for r in range(3):
            out_ref[r:r + 1, :] = ba_ref[:, r * D:(r + 1) * D]
        for r, ref in enumerate(rows):
            out_ref[3 + r:4 + r, :] = ref[...]
        out_ref[ROW_WS:ROW_WS + CHUNK, :] = ws_ref[...]

    vm = pl.BlockSpec(memory_space=VMEM)
    return pl.pallas_call(
        body, name=name, out_shape=jax.ShapeDtypeStruct((NSMALL, D), F32),
        in_specs=[vm] * 10, out_specs=vm, compiler_params=_cp(vmem_mb=32),
    )(b_ada, g_pre, conv_b, lg, lb, sg, sb, b_sgu, g_final, w_sgu_t)


def _adam(w, g, m, v):
    m2 = ADAM_B1 * m + (1.0 - ADAM_B1) * g
    v2 = ADAM_B2 * v + (1.0 - ADAM_B2) * (g * g)
    m_hat = m2 / (1.0 - ADAM_B1 ** ADAM_STEP)
    v_hat = v2 / (1.0 - ADAM_B2 ** ADAM_STEP)
    delta = -ADAM_LR * (m_hat / (jnp.sqrt(v_hat) + ADAM_EPS) + ADAM_WD * w)
    return delta, m2, v2


def _small_finish(eg, lg, w, m, v):
    def body(eg_ref, lg_ref, w_ref, m_ref, v_ref, loss_ref, g_ref, d_ref, m2_ref, v2_ref):
        e = eg_ref[0]
        l = lg_ref[0]
        for s in range(1, NDEV):
            e = e + eg_ref[s]
            l = l + lg_ref[s]
        g_ref[...] = jnp.zeros_like(g_ref)
        g_ref[0:2, :] = l[0:2, :]
        g_ref[2:3, :] = e[0:1, :]
        g_ref[3:4, :] = l[2:3, :]
        g_ref[4:12, :] = e[1:9, :]
        g_ref[ROW_CW:ROW_CW + KW, :] = l[8:8 + KW, :]
        g_ref[ROW_WS:ROW_WS + CHUNK, :] = e[16:16 + CHUNK, :]
        loss_ref[...] = jnp.sum(e[8:9, :], axis=1, keepdims=True)
        d_ref[...], m2_ref[...], v2_ref[...] = _adam(w_ref[...], g_ref[...], m_ref[...], v_ref[...])

    vm = pl.BlockSpec(memory_space=VMEM)
    sd = jax.ShapeDtypeStruct((NSMALL, D), F32)
    return pl.pallas_call(
        body, name="small_finish", out_shape=(jax.ShapeDtypeStruct((1, 1), F32), sd, sd, sd, sd),
        in_specs=[vm] * 5, out_specs=(vm,) * 5, compiler_params=_cp(vmem_mb=32),
    )(eg, lg, w, m, v)


def _ada_grad_adam(ct, dm, w, m, v):
    def body(ct_ref, dm_ref, w_ref, m_ref, v_ref, g_ref, d_ref, m2_ref, v2_ref):
        g = ct_ref[:, 0:1] * dm_ref[0:1, :]
        for b in range(1, NDEV):
            g = g + ct_ref[:, b:b + 1] * dm_ref[b:b + 1, :]
        g_ref[...] = g
        d_ref[...], m2_ref[...], v2_ref[...] = _adam(w_ref[...], g, m_ref[...], v_ref[...])

    vm = pl.BlockSpec(memory_space=VMEM)
    sd = jax.ShapeDtypeStruct(w.shape, F32)
    return pl.pallas_call(
        body, name="ada_grad_adam", out_shape=(sd, sd, sd, sd),
        in_specs=[vm] * 5, out_specs=(vm,) * 4, compiler_params=_cp(vmem_mb=32),
    )(ct, dm, w, m, v)


def _adam_f32(name, g, w, m, v, rows=None):
    R, C = w.shape
    rows = R if rows is None else rows

    def body(g_ref, w_ref, m_ref, v_ref, d_ref, m2_ref, v2_ref):
        d_ref[...], m2_ref[...], v2_ref[...] = _adam(w_ref[...], g_ref[...], m_ref[...], v_ref[...])

    tile = pl.BlockSpec((rows, C), lambda i: (i, 0))
    sd = jax.ShapeDtypeStruct(w.shape, F32)
    return pl.pallas_call(
        body, name=name, grid=(R // rows,), out_shape=(sd, sd, sd), in_specs=[tile] * 4, out_specs=(tile,) * 3,
        compiler_params=_cp(("parallel",), vmem_mb=32),
    )(g, w, m, v)


def _adam_reduce(name, recv, w, m, v, rows, recv_spec):
    R, C = w.shape

    def body(r_ref, w_ref, m_ref, v_ref, g_ref, d_ref, m2_ref, v2_ref):
        g = r_ref[0].astype(F32)
        for s in range(1, NDEV):
            g = g + r_ref[s].astype(F32)
        g_ref[...] = g
        d_ref[...], m2_ref[...], v2_ref[...] = _adam(w_ref[...], g, m_ref[...], v_ref[...])

    tile = pl.BlockSpec((rows, C), lambda i: (i, 0))
    sd = jax.ShapeDtypeStruct((R, C), F32)
    return pl.pallas_call(
        body, name=name, grid=(R // rows,), out_shape=(sd, sd, sd, sd),
        in_specs=[recv_spec, tile, tile, tile], out_specs=(tile,) * 4,
        compiler_params=_cp(("parallel",)),
    )(recv, w, m, v)


def kernel(x, c, w_ada, b_ada, g_pre, w_in, conv_w, conv_b, conv_ln_g, conv_ln_b, w_conv_out, sgu_ln_g, sgu_ln_b, w_sgu, b_sgu, w_sgu_out, w_o, g_final, loss_target, m_w_ada, m_b_ada, m_g_pre, m_w_in, m_conv_w, m_conv_b, m_conv_ln_g, m_conv_ln_b, m_w_conv_out, m_sgu_ln_g, m_sgu_ln_b, m_w_sgu, m_b_sgu, m_w_sgu_out, m_w_o, m_g_final, v_w_ada, v_b_ada, v_g_pre, v_w_in, v_conv_w, v_conv_b, v_conv_ln_g, v_conv_ln_b, v_w_conv_out, v_sgu_ln_g, v_sgu_ln_b, v_w_sgu, v_b_sgu, v_w_sgu_out, v_w_o, v_g_final):
    T = x.shape[1]
    assert T % TMB == 0 and x.shape[2] == D
    xs = x[0]
    tgt = loss_target[0]
    my = 4 * lax.axis_index("x") + 2 * lax.axis_index("y") + lax.axis_index("c")

    mod, cg = _ada_fwd(c, w_ada[0], b_ada)
    gf = g_final.reshape(1, D)
    bst = b_sgu[0].T

    h, ht = _prep_h(xs, mod, g_pre)
    p, wing, wqg, cwg = _in_proj(h, w_in[0], my ^ jnp.array(AG_ORDER, jnp.int32),
                                 w_conv_out[0], w_sgu_out[0], w_o[0], conv_w[0])
    wq = wqg.reshape(3, D, D)
    cw = jnp.broadcast_to(jnp.transpose(cwg, (1, 0, 2)).reshape(KW, 1, D), (KW, 8, D))
    yc, ya = _branch_a_fwd(p, cw, conv_b, conv_ln_g, conv_ln_b, wq)
    yb = _branch_b_fwd(p, w_sgu[0], bst, sgu_ln_g, sgu_ln_b, wq)
    dx2, dya, dyb, dp, gw_o, st_m = _merge_loss(p, ya, yb, xs, tgt, mod, gf, wq)

    dp, gw_so, gws, gbs, st_b = _branch_b_bwd(dp, dyb, p, w_sgu[0], bst, sgu_ln_g, sgu_ln_b, wq)
    dp, dyc, gw_co, st_a = _branch_a_bwd1(dp, dya, yc, p, conv_ln_g, conv_ln_b, wq)
    early = _pack_early(st_a, st_b, st_m, gws, gbs)
    dp, gcw, rq, eg = _branch_a_bwd2(dp, dyc, p, cw, gw_co, gw_so, gw_o, early)
    grad_x, st_x = _in_bwd_dx(dp, wing, xs, dx2, mod, g_pre)
    late = _pack_late(st_x, gcw)
    g_in, lg = _in_bwd_dw(ht, dp, my ^ jnp.array(RS_ORDER, jnp.int32), late)

    def pack(name, b_ada_, g_pre_, conv_b_, lg_, lb_, sg_, sb_, b_sgu_, gfin_, w_sgu_):
        return _pack_params(name, b_ada_, g_pre_, conv_b_, lg_, lb_, sg_, sb_, b_sgu_.reshape(1, D), gfin_.reshape(1, D),
                            jnp.transpose(w_sgu_[0], (1, 0, 2)).reshape(CHUNK, D))

    pw = pack("pack_w", b_ada, g_pre, conv_b, conv_ln_g, conv_ln_b, sgu_ln_g, sgu_ln_b, b_sgu, g_final, w_sgu)
    pm = pack("pack_m", m_b_ada, m_g_pre, m_conv_b, m_conv_ln_g, m_conv_ln_b, m_sgu_ln_g, m_sgu_ln_b, m_b_sgu, m_g_final,
              m_w_sgu)
    pv = pack("pack_v", v_b_ada, v_g_pre, v_conv_b, v_conv_ln_g, v_conv_ln_b, v_sgu_ln_g, v_sgu_ln_b, v_b_sgu, v_g_final,
              v_w_sgu)
    loss11, sg_, sd_, sm_, sv_ = _small_finish(eg, lg, pw, pm, pv)

    def unpack(a):
        return dict(
            b_ada=a[0:3].reshape(1, 3 * D), g_pre=a[3:4], conv_b=a[4:5], conv_ln_g=a[5:6], conv_ln_b=a[6:7],
            sgu_ln_g=a[7:8], sgu_ln_b=a[8:9], b_sgu=a[9:10].reshape(1, HEADS, CHUNK), g_final=a[10],
            w_sgu=jnp.transpose(a[ROW_WS:ROW_WS + CHUNK].reshape(CHUNK, HEADS, CHUNK), (1, 0, 2))[None])
    small = [unpack(a) for a in (sg_, sd_, sm_, sv_)]

    g_cw = lax.dynamic_slice_in_dim(sg_[ROW_CW:ROW_CW + KW], my * 128, 128, axis=1)
    d_cw, m_cw, v_cw = _adam_f32("adam_conv_w", g_cw, conv_w[0], m_conv_w[0], v_conv_w[0])

    wcols = w_ada.shape[2]
    dm_all = jnp.concatenate([lg[:, 0], lg[:, 1], eg[:, 0]], axis=1)
    dm_mine = lax.dynamic_slice_in_dim(dm_all, my * wcols, wcols, axis=1)
    g_ada, d_ada, m_ada, v_ada = _ada_grad_adam(cg.T, dm_mine, w_ada[0], m_w_ada[0], v_w_ada[0])

    d_in, m_in, v_in = _adam_f32("adam_w_in", g_in, w_in[0], m_w_in[0], v_w_in[0], rows=256)
    big = {}
    for j, (nm, w_, m_, v_) in enumerate((("w_conv_out", w_conv_out, m_w_conv_out, v_w_conv_out),
                                          ("w_sgu_out", w_sgu_out, m_w_sgu_out, v_w_sgu_out),
                                          ("w_o", w_o, m_w_o, v_w_o))):
        big[nm] = _adam_reduce("adam_" + nm, rq, w_[0], m_[0], v_[0], 128,
                               pl.BlockSpec((NDEV, None, 128, D), lambda i, j=j: (0, j, 0, 0)))

    per = {
        "w_ada": tuple(a[None] for a in (g_ada, d_ada, m_ada, v_ada)),
        "w_in": tuple(a[None] for a in (g_in, d_in, m_in, v_in)),
        "conv_w": tuple(a[None] for a in (g_cw, d_cw, m_cw, v_cw)),
    }
    for nm in ("w_conv_out", "w_sgu_out", "w_o"):
        per[nm] = tuple(a[None] for a in big[nm])
    for nm in ("b_ada", "g_pre", "conv_b", "conv_ln_g", "conv_ln_b", "sgu_ln_g", "sgu_ln_b", "w_sgu", "b_sgu", "g_final"):
        per[nm] = tuple(s[nm] for s in small)

    order = ["w_ada", "b_ada", "g_pre", "w_in", "conv_w", "conv_b", "conv_ln_g", "conv_ln_b", "w_conv_out",
             "sgu_ln_g", "sgu_ln_b", "w_sgu", "b_sgu", "w_sgu_out", "w_o", "g_final"]
    outs = [loss11.reshape(()), grad_x[None]]
    for part in range(4):
        outs += [per[nm][part] for nm in order]
    return tuple(outs)
```

```python
import jax
import jax.numpy as jnp
from jax import lax
from jax.experimental import pallas as pl
from jax.experimental.pallas import tpu as pltpu

F32 = jnp.float32
BF16 = jnp.bfloat16
MESH = pl.DeviceIdType.MESH
VMEM = pltpu.VMEM
ANY = pl.ANY

D = 1024
NDEV = 8
NSEG = 8
HEADS = 8
HD = 128
CHUNK = 128
KW = 31
HALO = 32
EPS = 1e-6
TM = 256
TMB = 512
RB = 32
UNROLL = 4
RBC = 32
TAPG = 4
EXT = TM + HALO
NSMALL = 176
NEARLY = 144
NLATE = 40
ROW_CW = 16
ROW_WS = 48
W_CO, W_SO, W_O = 0, 1, 2

ADAM_LR = 0.001
ADAM_B1 = 0.9
ADAM_B2 = 0.999
ADAM_EPS = 1e-08
ADAM_WD = 0.01
ADAM_STEP = 10

INV_SQRT2 = 0.7071067811865476
INV_SQRT_2PI = 0.3989422804014327

NT = (((1,), (1,)), ((), ()))
TN = (((0,), (0,)), ((), ()))


def _cp(sem=None, vmem_mb=48):
    return pltpu.CompilerParams(dimension_semantics=sem, vmem_limit_bytes=vmem_mb * 1024 * 1024)


def _me():
    return lax.axis_index("x"), lax.axis_index("y"), lax.axis_index("c")


def _slot(px, py, pc):
    return 4 * px + 2 * py + pc


def _xor_peer(k):
    x, y, c = _me()
    return (1 - x if k & 4 else x, 1 - y if k & 2 else y, 1 - c if k & 1 else c)


def _allgather_phases(srcs, dst_at, send_sems, recv_sems, loc_sems):
    x, y, c = _me()
    me = (x, y, c)
    sibling = (x, y, 1 - c)
    chips = [(1 - x, y), (x, 1 - y), (1 - x, 1 - y)]
    na = len(srcs)

    def copy(a, k, block, to, src=None):
        d = dst_at(a, _slot(*block))
        return pltpu.make_async_remote_copy(
            src_ref=d if src is None else src, dst_ref=d,
            send_sem=send_sems.at[a, k], recv_sem=recv_sems.at[a, k],
            device_id=to, device_id_type=MESH)

    local = [pltpu.make_async_copy(srcs[a], dst_at(a, _slot(*me)), loc_sems.at[a]) for a in range(na)]
    first = []
    for a in range(na):
        first.append(copy(a, 0, me, sibling, src=srcs[a]))
        for j, chip in enumerate(chips):
            first.append(copy(a, 1 + j, me, (*chip, c), src=srcs[a]))
    passed = [copy(a, 4 + j, (*chip, c), sibling) for j, chip in enumerate(chips) for a in range(na)]

    def start():
        for cp in local + first:
            cp.start()

    def relay():
        for j, chip in enumerate(chips):
            for a in range(na):
                copy(a, 1 + j, (*chip, c), me).wait_recv()
                passed[j * na + a].start()

    def finish():
        for a in range(na):
            copy(a, 0, sibling, me).wait_recv()
        for j, chip in enumerate(chips):
            for a in range(na):
                copy(a, 4 + j, (*chip, 1 - c), me).wait_recv()
        for cp in first + passed:
            cp.wait_send()
        for cp in local:
            cp.wait()

    return start, relay, finish


def _allgather(srcs, dst_at, send_sems, recv_sems, loc_sems):
    start, relay, finish = _allgather_phases(srcs, dst_at, send_sems, recv_sems, loc_sems)
    start()
    relay()
    finish()


def _owner_exchange(gq, rq_ref, x_send, x_recv, x_loc):
    mx, my_, mc = _me()
    me = _slot(mx, my_, mc)

    def rows_of(a, slot):
        return gq[a].at[pl.ds(pl.multiple_of(slot * 128, 128), 128), :]

    def exchange(k, a, recv):
        px, py, pc = _xor_peer(k)
        peer = _slot(px, py, pc)
        return pltpu.make_async_remote_copy(
            src_ref=rows_of(a, me if recv else peer), dst_ref=rq_ref.at[peer if recv else me, a],
            send_sem=x_send.at[a, k - 1], recv_sem=x_recv.at[a, k - 1],
            device_id=(px, py, pc), device_id_type=MESH)

    local = [pltpu.make_async_copy(rows_of(a, me), rq_ref.at[me, a], x_loc.at[a]) for a in range(len(gq))]

    def start():
        for cp in local:
            cp.start()
        for k in range(1, NDEV):
            for a in range(len(gq)):
                exchange(k, a, False).start()

    def wait():
        for k in range(1, NDEV):
            for a in range(len(gq)):
                exchange(k, a, True).wait_recv()
        for k in range(1, NDEV):
            for a in range(len(gq)):
                exchange(k, a, False).wait_send()
        for cp in local:
            cp.wait()

    return start, wait


def _sig(v):
    return jax.nn.sigmoid(v)


def _gelu_parts(v):
    cdf = 0.5 * (1.0 + lax.erf(v * INV_SQRT2))
    pdf = jnp.exp(-0.5 * v * v) * INV_SQRT_2PI
    return v * cdf, cdf + v * pdf


def _ln_stats(v):
    mu = jnp.mean(v, axis=-1, keepdims=True)
    xc = v - mu
    var = jnp.mean(xc * xc, axis=-1, keepdims=True)
    rs = lax.rsqrt(var + EPS)
    return xc * rs, rs


def _ln_bwd(dn, n, rs):
    return rs * (dn - jnp.mean(dn, axis=-1, keepdims=True) - n * jnp.mean(dn * n, axis=-1, keepdims=True))


def _fold8(v):
    acc = v[0:8]
    for r in range(8, v.shape[0], 8):
        acc = acc + v[r:r + 8]
    return acc


def _rows(rb, n=RB):
    return pl.ds(pl.multiple_of(rb * n, n), n)


def _blocks(n, body, init=0, unroll=UNROLL):
    def trip(t, carry):
        for u in range(unroll):
            carry = body(t * unroll + u, carry)
        return carry
    return lax.fori_loop(0, n // unroll, trip, init)


def _shift_copies(s_ref):
    n = EXT - 8
    for m in range(1, 8):
        for c0 in range(0, n, 56):
            s_ref[m, c0:c0 + 56, :] = s_ref[0, c0 + m:c0 + m + 56, :]


def _ada_fwd(c, w_ada, b_ada):
    wcols = w_ada.shape[1]

    def body(c_ref, w_ref, b_ref, mod_ref, call_ref, cdst, msrc, mdst, c_send, c_recv, m_send, m_recv):
        x, y, c = _me()
        me = _slot(x, y, c)

        def all_to_all(src_of, dst, ss, rs):
            def cp(k, recv):
                px, py, pc = _xor_peer(k)
                peer = _slot(px, py, pc)
                return pltpu.make_async_remote_copy(
                    src_ref=src_of(me if recv else peer), dst_ref=dst.at[peer if recv else me],
                    send_sem=ss.at[k - 1], recv_sem=rs.at[k - 1], device_id=(px, py, pc), device_id_type=MESH)
            for k in range(1, NDEV):
                cp(k, False).start()
            for k in range(1, NDEV):
                cp(k, True).wait_recv()
            for k in range(1, NDEV):
                cp(k, False).wait_send()

        cdst[me] = c_ref[...]
        all_to_all(lambda s: c_ref, cdst, c_send, c_recv)
        for b in range(NDEV):
            call_ref[b:b + 1, :] = cdst[b]
        m = jnp.dot(call_ref[...], w_ref[...], preferred_element_type=F32, precision=lax.Precision.HIGHEST)
        for b in range(NDEV):
            msrc[b] = m[b:b + 1, :]
        mdst[me] = msrc[me]
        all_to_all(lambda s: msrc.at[s], mdst, m_send, m_recv)
        full = jnp.concatenate([mdst[k] for k in range(NDEV)], axis=1) + b_ref[...]
        for r in range(3):
            mod_ref[r:r + 1, :] = full[:, r * D:(r + 1) * D]

    vm = pl.BlockSpec(memory_space=VMEM)
    return pl.pallas_call(
        body, name="ada_fwd",
        out_shape=(jax.ShapeDtypeStruct((3, D), F32), jax.ShapeDtypeStruct((NDEV, D), F32)),
        in_specs=[vm, vm, vm], out_specs=(vm, vm),
        scratch_shapes=[VMEM((NDEV, 1, D), F32), VMEM((NDEV, 1, wcols), F32), VMEM((NDEV, 1, wcols), F32),
                        pltpu.SemaphoreType.DMA((7,)), pltpu.SemaphoreType.DMA((7,)),
                        pltpu.SemaphoreType.DMA((7,)), pltpu.SemaphoreType.DMA((7,))],
        compiler_params=_cp(vmem_mb=32),
    )(c, w_ada, b_ada)


def _prep_h(x, mod, g_pre):
    T = x.shape[0]

    def body(x_ref, mod_ref, g_ref, h_ref, ht_ref):
        def blk(rb, carry):
            rows = _rows(rb)
            xv = x_ref[rows, :]
            r = lax.rsqrt(jnp.mean(xv * xv, axis=-1, keepdims=True) + EPS)
            h_ref[rows, :] = ((xv * r) * g_ref[...] * (1.0 + mod_ref[1:2, :]) + mod_ref[0:1, :]).astype(BF16)
            return carry
        _blocks(TMB // RB, blk)
        ht_ref[...] = h_ref[...].T

    return pl.pallas_call(
        body, name="prep_h", grid=(T // TMB,),
        out_shape=(jax.ShapeDtypeStruct((T, D), BF16), jax.ShapeDtypeStruct((D, T), BF16)),
        in_specs=[pl.BlockSpec((TMB, D), lambda i: (i, 0)), pl.BlockSpec((3, D), lambda i: (0, 0)),
                  pl.BlockSpec((1, D), lambda i: (0, 0))],
        out_specs=(pl.BlockSpec((TMB, D), lambda i: (i, 0)), pl.BlockSpec((D, TMB), lambda i: (0, i))),
        compiler_params=_cp(("parallel",)),
    )(x, mod, g_pre)


AG_ORDER = (0, 1, 4, 5, 2, 3, 6, 7)


def _in_proj(h, w_in, order, w_co, w_so, w_o, conv_w):
    T = h.shape[0]
    nb = T // TMB

    def body(order_ref, h_ref, w_ref, wco_ref, wso_ref, wo_ref, cw_ref, p_ref, wing_ref, wqg_ref, cwg_ref,
             wbuf, wq_st, send_sems, recv_sems, out_sems, ag_send, ag_recv, ag_loc):
        j = pl.program_id(0)
        i = pl.program_id(1)
        x, y, c = _me()
        me = (x, y, c)
        sibling = (x, y, 1 - c)
        chips = [(1 - x, y), (x, 1 - y), (1 - x, 1 - y)]
        blocks = [me, sibling] + [(*chip, cc) for chip in chips for cc in (c, 1 - c)]
        recv_k = [None, 0, 1, 4, 2, 5, 3, 6]
        ag_start, ag_relay, ag_finish = _allgather_phases(
            [wq_st, cw_ref], lambda a, slot: wqg_ref.at[:, slot] if a == 0 else cwg_ref.at[slot], ag_send, ag_recv, ag_loc)

        def copy(k, block, to):
            d = wbuf.at[_slot(*block)]
            return pltpu.make_async_remote_copy(src_ref=d, dst_ref=d, send_sem=send_sems.at[k], recv_sem=recv_sems.at[k],
                                                device_id=to, device_id_type=MESH)

        def writeback(jj):
            s = _slot(*blocks[jj])
            return pltpu.make_async_copy(wbuf.at[s], wing_ref.at[s], out_sems.at[jj])

        first = [copy(0, me, sibling)] + [copy(1 + q, me, (*chip, c)) for q, chip in enumerate(chips)]
        passed = [copy(4 + q, (*chip, c), sibling) for q, chip in enumerate(chips)]

        @pl.when((j == 0) & (i == 0))
        def _():
            wbuf[_slot(*me)] = w_ref[...].astype(BF16)
            for cp in first:
                cp.start()
            writeback(0).start()
            wq_st[W_CO] = wco_ref[...].astype(BF16)
            wq_st[W_SO] = wso_ref[...].astype(BF16)
            wq_st[W_O] = wo_ref[...].astype(BF16)
            ag_start()

        for jj in range(1, NSEG):
            @pl.when((j == jj) & (i == 0))
            def _(jj=jj):
                copy(recv_k[jj], blocks[jj], me).wait_recv()
                if jj in (2, 4, 6):
                    passed[jj // 2 - 1].start()
                writeback(jj).start()
                if jj == NSEG - 1:
                    ag_relay()

        p_ref[...] = jnp.dot(h_ref[...], wbuf[order_ref[j]], preferred_element_type=F32)

        @pl.when((j == NSEG - 1) & (i == nb - 1))
        def _():
            for cp in first + passed:
                cp.wait_send()
            for jj in range(NSEG):
                writeback(jj).wait()
            ag_finish()

    vm = pl.BlockSpec(memory_space=VMEM)
    hbm = pl.BlockSpec(memory_space=ANY)
    return pl.pallas_call(
        body, name="in_proj",
        out_shape=(jax.ShapeDtypeStruct((T, NSEG * D), F32), jax.ShapeDtypeStruct((NSEG, D, D), BF16),
                   jax.ShapeDtypeStruct((3, NDEV, 128, D), BF16), jax.ShapeDtypeStruct((NDEV, KW, 128), F32)),
        grid_spec=pltpu.PrefetchScalarGridSpec(
            num_scalar_prefetch=1, grid=(NSEG, nb),
            in_specs=[pl.BlockSpec((TMB, D), lambda j, i, o: (i, 0)), vm, vm, vm, vm, vm],
            out_specs=(pl.BlockSpec((TMB, D), lambda j, i, o: (i, o[j])), hbm, hbm, hbm),
            scratch_shapes=[VMEM((NSEG, D, D), BF16), VMEM((3, 128, D), BF16),
                            pltpu.SemaphoreType.DMA((7,)), pltpu.SemaphoreType.DMA((7,)), pltpu.SemaphoreType.DMA((NSEG,)),
                            pltpu.SemaphoreType.DMA((2, 7)), pltpu.SemaphoreType.DMA((2, 7)), pltpu.SemaphoreType.DMA((2,))]),
        compiler_params=_cp(("arbitrary", "arbitrary")),
    )(order, h, w_in, w_co, w_so, w_o, conv_w)


def _fill_a_ext(i, s_ref, val_ref, glu_ref, hval_ref, hglu_ref):
    ah = hval_ref[...] * _sig(hglu_ref[...])
    s_ref[0, 0:HALO, :] = jnp.where(i > 0, ah, 0.0)

    def fill(rb, carry):
        rows = _rows(rb)
        s_ref[0, pl.ds(pl.multiple_of(HALO + rb * RB, RB), RB), :] = val_ref[rows, :] * _sig(glu_ref[rows, :])
        return carry
    lax.fori_loop(0, TM // RB, fill, 0)


def _halo_prev(seg):
    hb = TM // HALO
    return pl.BlockSpec((HALO, D), lambda i: (jnp.maximum(i * hb - 1, 0), seg))


def _branch_a_fwd(p, cw, conv_b, lg, lb, wq):
    T = p.shape[0]

    def body(val_ref, glu_ref, z_ref, hval_ref, hglu_ref, cw_ref, cb_ref, lg_ref, lb_ref, w_ref,
             yc_ref, ya_ref, s_ref, yp_ref):
        i = pl.program_id(0)
        _fill_a_ext(i, s_ref, val_ref, glu_ref, hval_ref, hglu_ref)
        _shift_copies(s_ref)

        def conv(rb, carry):
            r0 = rb * RBC
            accs = [jnp.broadcast_to(cb_ref[...], (8, D))] * (RBC // 8)
            for k in range(KW):
                o = 2 + k
                w = cw_ref[k]
                accs = [acc + w * s_ref[o % 8, pl.ds(pl.multiple_of(r0 + 8 * (o // 8 + g), 8), 8), :]
                        for g, acc in enumerate(accs)]
            for g, acc in enumerate(accs):
                yc_ref[pl.ds(pl.multiple_of(r0 + 8 * g, 8), 8), :] = acc
            return carry
        lax.fori_loop(0, TM // RBC, conv, 0)

        def post(rb, carry):
            rows = _rows(rb)
            n, _ = _ln_stats(yc_ref[rows, :])
            l = n * lg_ref[...] + lb_ref[...]
            z = z_ref[rows, :]
            yp_ref[rows, :] = ((l * _sig(l)) * (z * _sig(z))).astype(BF16)
            return carry
        _blocks(TM // RB, post)
        ya_ref[...] = jnp.dot(yp_ref[...], w_ref[...], preferred_element_type=F32)

    tile = lambda seg: pl.BlockSpec((TM, D), lambda i: (i, seg))
    row = pl.BlockSpec((1, D), lambda i: (0, 0))
    return pl.pallas_call(
        body, name="branch_a_fwd", grid=(T // TM,),
        out_shape=(jax.ShapeDtypeStruct((T, D), F32), jax.ShapeDtypeStruct((T, D), F32)),
        in_specs=[tile(0), tile(1), tile(2), _halo_prev(0), _halo_prev(1),
                  pl.BlockSpec((KW, 8, D), lambda i: (0, 0, 0)), row, row, row,
                  pl.BlockSpec((None, D, D), lambda i: (W_CO, 0, 0))],
        out_specs=(pl.BlockSpec((TM, D), lambda i: (i, 0)), pl.BlockSpec((TM, D), lambda i: (i, 0))),
        scratch_shapes=[VMEM((8, EXT, D), F32), VMEM((TM, D), BF16)],
        compiler_params=_cp(("parallel",)),
    )(p, p, p, p, p, cw, conv_b, lg, lb, wq)


def _masked_ws(ws_ref, wt_ref):
    tri = lax.broadcasted_iota(jnp.int32, (CHUNK, CHUNK), 0) >= lax.broadcasted_iota(jnp.int32, (CHUNK, CHUNK), 1)
    for h in range(HEADS):
        wt_ref[h] = jnp.where(tri, ws_ref[h], 0.0).astype(BF16)


def _sgu_mix(wt_ref, vl_ref, bst_ref, s_ref):
    for ck in range(TM // CHUNK):
        r = slice(ck * CHUNK, (ck + 1) * CHUNK)
        for h in range(HEADS):
            cs = slice(h * HD, (h + 1) * HD)
            s_ref[r, cs] = jnp.dot(wt_ref[h], vl_ref[r, cs], preferred_element_type=F32) + bst_ref[:, h:h + 1]


def _branch_b_fwd(p, ws, bst, sg, sb, wq):
    T = p.shape[0]

    def body(pu_ref, pv_ref, pz_ref, ws_ref, bst_ref, sg_ref, sb_ref, w_ref, yb_ref,
             wt_ref, vl_ref, t_ref, s_ref, yp_ref):
        _masked_ws(ws_ref, wt_ref)

        def pre(rb, carry):
            rows = _rows(rb)
            vg, _ = _gelu_parts(pv_ref[rows, :])
            vn, _ = _ln_stats(vg)
            vl_ref[rows, :] = (vn * sg_ref[...] + sb_ref[...]).astype(BF16)
            u, _ = _gelu_parts(pu_ref[rows, :])
            z = pz_ref[rows, :]
            t_ref[rows, :] = u * (z * _sig(z))
            return carry
        _blocks(TM // RB, pre)
        _sgu_mix(wt_ref, vl_ref, bst_ref, s_ref)

        def post(rb, carry):
            rows = _rows(rb)
            yp_ref[rows, :] = (t_ref[rows, :] * s_ref[rows, :]).astype(BF16)
            return carry
        _blocks(TM // RB, post)
        yb_ref[...] = jnp.dot(yp_ref[...], w_ref[...], preferred_element_type=F32)

    tile = lambda seg: pl.BlockSpec((TM, D), lambda i: (i, seg))
    row = pl.BlockSpec((1, D), lambda i: (0, 0))
    return pl.pallas_call(
        body, name="branch_b_fwd", grid=(T // TM,),
        out_shape=jax.ShapeDtypeStruct((T, D), F32),
        in_specs=[tile(3), tile(4), tile(5), pl.BlockSpec((HEADS, CHUNK, CHUNK), lambda i: (0, 0, 0)),
                  pl.BlockSpec((CHUNK, HEADS), lambda i: (0, 0)), row, row,
                  pl.BlockSpec((None, D, D), lambda i: (W_SO, 0, 0))],
        out_specs=pl.BlockSpec((TM, D), lambda i: (i, 0)),
        scratch_shapes=[VMEM((HEADS, CHUNK, CHUNK), BF16), VMEM((TM, D), BF16), VMEM((TM, D), F32),
                        VMEM((TM, D), F32), VMEM((TM, D), BF16)],
        compiler_params=_cp(("parallel",)),
    )(p, p, p, ws, bst, sg, sb, wq)


def _merge_loss(p, ya, yb, x, tgt, mod, g_final, wq):
    T = x.shape[0]
    nt = T // TM

    def body(ga_ref, gb_ref, ya_ref, yb_ref, x_ref, t_ref, mod_ref, gf_ref, w_ref,
             dx2_ref, dya_ref, dyb_ref, dp_ref, gwo_ref, st_ref,
             mrg_s, out_s, dout_s, dm_s, gw_acc):
        i = pl.program_id(0)

        @pl.when(i == 0)
        def _():
            gw_acc[...] = jnp.zeros_like(gw_acc)
            st_ref[...] = jnp.zeros_like(st_ref)

        def merge(rb, carry):
            rows = _rows(rb)
            mrg_s[rows, :] = (_sig(ga_ref[rows, :]) * ya_ref[rows, :] + _sig(gb_ref[rows, :]) * yb_ref[rows, :]).astype(BF16)
            return carry
        lax.fori_loop(0, TM // RB, merge, 0)
        out_s[...] = jnp.dot(mrg_s[...], w_ref[...], preferred_element_type=F32)

        def head(rb, carry):
            loss, gg, dg = carry
            rows = _rows(rb)
            gate = mod_ref[2:3, :]
            gf = gf_ref[...]
            out = out_s[rows, :]
            x2 = x_ref[rows, :] + gate * out
            r2 = lax.rsqrt(jnp.mean(x2 * x2, axis=-1, keepdims=True) + EPS)
            x2n = x2 * r2
            diff = x2n * gf - t_ref[rows, :]
            dy = diff * (1.0 / D)
            dx2n = dy * gf
            dx2 = r2 * (dx2n - x2n * jnp.mean(dx2n * x2n, axis=-1, keepdims=True))
            dx2_ref[rows, :] = dx2
            dout_s[rows, :] = (dx2 * gate).astype(BF16)
            return loss + _fold8(diff * diff), gg + _fold8(dy * x2n), dg + _fold8(dx2 * out)
        zero = jnp.zeros((8, D), F32)
        loss, gg, dg = _blocks(TM // RB, head, (zero, zero, zero))
        st_ref[0] = st_ref[0] + loss * (0.5 / D)
        st_ref[1] = st_ref[1] + gg
        st_ref[2] = st_ref[2] + dg

        dm_s[...] = lax.dot_general(dout_s[...], w_ref[...], NT, preferred_element_type=F32)
        gw_acc[...] += lax.dot_general(mrg_s[...], dout_s[...], TN, preferred_element_type=F32)

        def split(rb, carry):
            rows = _rows(rb)
            dm = dm_s[rows, :]
            sa = _sig(ga_ref[rows, :])
            sb = _sig(gb_ref[rows, :])
            dya_ref[rows, :] = (dm * sa).astype(BF16)
            dyb_ref[rows, :] = (dm * sb).astype(BF16)
            dp_ref[0, rows, :] = (dm * ya_ref[rows, :] * (sa * (1.0 - sa))).astype(BF16)
            dp_ref[1, rows, :] = (dm * yb_ref[rows, :] * (sb * (1.0 - sb))).astype(BF16)
            return carry
        lax.fori_loop(0, TM // RB, split, 0)

        @pl.when(i == nt - 1)
        def _():
            gwo_ref[...] = gw_acc[...].astype(BF16)

    tile = pl.BlockSpec((TM, D), lambda i: (i, 0))
    return pl.pallas_call(
        body, name="merge_loss", grid=(nt,),
        out_shape=(jax.ShapeDtypeStruct((T, D), F32), jax.ShapeDtypeStruct((T, D), BF16), jax.ShapeDtypeStruct((T, D), BF16),
                   jax.ShapeDtypeStruct((NSEG, T, D), BF16), jax.ShapeDtypeStruct((D, D), BF16),
                   jax.ShapeDtypeStruct((3, 8, D), F32)),
        in_specs=[pl.BlockSpec((TM, D), lambda i: (i, 6)), pl.BlockSpec((TM, D), lambda i: (i, 7)), tile, tile, tile, tile,
                  pl.BlockSpec((3, D), lambda i: (0, 0)), pl.BlockSpec((1, D), lambda i: (0, 0)),
                  pl.BlockSpec((None, D, D), lambda i: (W_O, 0, 0))],
        out_specs=(tile, tile, tile, pl.BlockSpec((2, TM, D), lambda i: (3, i, 0)),
                   pl.BlockSpec((D, D), lambda i: (0, 0)), pl.BlockSpec((3, 8, D), lambda i: (0, 0, 0))),
        scratch_shapes=[VMEM((TM, D), BF16), VMEM((TM, D), F32), VMEM((TM, D), BF16), VMEM((TM, D), F32), VMEM((D, D), F32)],
        compiler_params=_cp(("arbitrary",)),
    )(p, p, ya, yb, x, tgt, mod, g_final, wq)


def _branch_b_bwd(dp, dyb, p, ws, bst, sg, sb, wq):
    T = p.shape[0]
    nt = T // TM

    def body(dp_in, dyb_ref, pu_ref, pv_ref, pz_ref, ws_ref, bst_ref, sg_ref, sb_ref, w_ref,
             dp_ref, gw_ref, gws_ref, gbs_ref, st_ref,
             wt_ref, d_s, vl_s, vn_s, gpv_s, rs_s, s_s, ds_s, ds32_s, yp_s, dvl_s, gw_acc):
        del dp_in
        i = pl.program_id(0)

        @pl.when(i == 0)
        def _():
            gw_acc[...] = jnp.zeros_like(gw_acc)
            gws_ref[...] = jnp.zeros_like(gws_ref)
            gbs_ref[...] = jnp.zeros_like(gbs_ref)
            st_ref[...] = jnp.zeros_like(st_ref)
            _masked_ws(ws_ref, wt_ref)

        d_s[...] = lax.dot_general(dyb_ref[...], w_ref[...], NT, preferred_element_type=F32)

        def pre(rb, carry):
            rows = _rows(rb)
            vg, gpv = _gelu_parts(pv_ref[rows, :])
            vn, rs = _ln_stats(vg)
            vl_s[rows, :] = (vn * sg_ref[...] + sb_ref[...]).astype(BF16)
            vn_s[rows, :] = vn
            gpv_s[rows, :] = gpv
            rs_s[rows, :] = rs
            return carry
        _blocks(TM // RB, pre)
        _sgu_mix(wt_ref, vl_s, bst_ref, s_s)

        def mid(rb, carry):
            rows = _rows(rb)
            u, gpu = _gelu_parts(pu_ref[rows, :])
            z = pz_ref[rows, :]
            sz = _sig(z)
            siluz = z * sz
            d = d_s[rows, :]
            s = s_s[rows, :]
            t = u * siluz
            yp_s[rows, :] = (t * s).astype(BF16)
            dp_ref[0, rows, :] = (d * s * siluz * gpu).astype(BF16)
            ds = d * t
            ds32_s[rows, :] = ds
            ds_s[rows, :] = ds.astype(BF16)
            dp_ref[2, rows, :] = (d * u * s * (sz * (1.0 + z * (1.0 - sz)))).astype(BF16)
            return carry
        lax.fori_loop(0, TM // RB, mid, 0)

        for ck in range(TM // CHUNK):
            r = slice(ck * CHUNK, (ck + 1) * CHUNK)
            for h in range(HEADS):
                cs = slice(h * HD, (h + 1) * HD)
                dsh = ds_s[r, cs]
                dvl_s[r, cs] = lax.dot_general(wt_ref[h], dsh, TN, preferred_element_type=F32)
                gws_ref[h] += lax.dot_general(dsh, vl_s[r, cs], NT, preferred_element_type=F32)
                gbs_ref[h] += ds32_s[r, cs]

        def post(rb, carry):
            g_sg, g_sb = carry
            rows = _rows(rb)
            dvl = dvl_s[rows, :]
            vn = vn_s[rows, :]
            dvg = _ln_bwd(dvl * sg_ref[...], vn, rs_s[rows, :])
            dp_ref[1, rows, :] = (dvg * gpv_s[rows, :]).astype(BF16)
            return g_sg + _fold8(dvl * vn), g_sb + _fold8(dvl)
        zero = jnp.zeros((8, D), F32)
        g_sg, g_sb = _blocks(TM // RB, post, (zero, zero))
        st_ref[0] = st_ref[0] + g_sg
        st_ref[1] = st_ref[1] + g_sb

        gw_acc[...] += lax.dot_general(yp_s[...], dyb_ref[...], TN, preferred_element_type=F32)

        @pl.when(i == nt - 1)
        def _():
            gw_ref[...] = gw_acc[...].astype(BF16)

    tile = lambda seg: pl.BlockSpec((TM, D), lambda i: (i, seg))
    row = pl.BlockSpec((1, D), lambda i: (0, 0))
    hh = pl.BlockSpec((HEADS, CHUNK, CHUNK), lambda i: (0, 0, 0))
    return pl.pallas_call(
        body, name="branch_b_bwd", grid=(nt,),
        out_shape=(jax.ShapeDtypeStruct((NSEG, T, D), BF16), jax.ShapeDtypeStruct((D, D), BF16),
                   jax.ShapeDtypeStruct((HEADS, CHUNK, CHUNK), F32), jax.ShapeDtypeStruct((HEADS, CHUNK, HD), F32),
                   jax.ShapeDtypeStruct((2, 8, D), F32)),
        in_specs=[pl.BlockSpec(memory_space=ANY), pl.BlockSpec((TM, D), lambda i: (i, 0)), tile(3), tile(4), tile(5),
                  hh, pl.BlockSpec((CHUNK, HEADS), lambda i: (0, 0)), row, row,
                  pl.BlockSpec((None, D, D), lambda i: (W_SO, 0, 0))],
        out_specs=(pl.BlockSpec((3, TM, D), lambda i: (1, i, 0)), pl.BlockSpec((D, D), lambda i: (0, 0)), hh, hh,
                   pl.BlockSpec((2, 8, D), lambda i: (0, 0, 0))),
        scratch_shapes=[VMEM((HEADS, CHUNK, CHUNK), BF16), VMEM((TM, D), F32), VMEM((TM, D), BF16), VMEM((TM, D), F32),
                        VMEM((TM, D), F32), VMEM((TM, 1), F32), VMEM((TM, D), F32), VMEM((TM, D), BF16), VMEM((TM, D), F32),
                        VMEM((TM, D), BF16), VMEM((TM, D), F32), VMEM((D, D), F32)],
        input_output_aliases={0: 0},
        compiler_params=_cp(("arbitrary",)),
    )(dp, dyb, p, p, p, ws, bst, sg, sb, wq)


def _branch_a_bwd1(dp, dya, yc, p, lg, lb, wq):
    T = p.shape[0]
    nt = T // TM

    def body(dp_in, dya_ref, yc_ref, z_ref, lg_ref, lb_ref, w_ref, dp_ref, dyc_ref, gw_ref, st_ref,
             d_s, yp_s, gw_acc):
        del dp_in
        i = pl.program_id(0)

        @pl.when(i == 0)
        def _():
            gw_acc[...] = jnp.zeros_like(gw_acc)
            st_ref[...] = jnp.zeros_like(st_ref)

        d_s[...] = lax.dot_general(dya_ref[...], w_ref[...], NT, preferred_element_type=F32)

        def blk(rb, carry):
            g_lg, g_lb, g_cb = carry
            rows = _rows(rb)
            n, rs = _ln_stats(yc_ref[rows, :])
            l = n * lg_ref[...] + lb_ref[...]
            sgl = _sig(l)
            sl = l * sgl
            z = z_ref[rows, :]
            sz = _sig(z)
            siluz = z * sz
            d = d_s[rows, :]
            yp_s[rows, :] = (sl * siluz).astype(BF16)
            dp_ref[rows, :] = (d * sl * (sz * (1.0 + z * (1.0 - sz)))).astype(BF16)
            dl = (d * siluz) * (sgl * (1.0 + l * (1.0 - sgl)))
            dyc = _ln_bwd(dl * lg_ref[...], n, rs)
            dyc_ref[rows, :] = dyc
            return g_lg + _fold8(dl * n), g_lb + _fold8(dl), g_cb + _fold8(dyc)
        zero = jnp.zeros((8, D), F32)
        g_lg, g_lb, g_cb = _blocks(TM // RB, blk, (zero, zero, zero))
        st_ref[0] = st_ref[0] + g_lg
        st_ref[1] = st_ref[1] + g_lb
        st_ref[2] = st_ref[2] + g_cb

        gw_acc[...] += lax.dot_general(yp_s[...], dya_ref[...], TN, preferred_element_type=F32)

        @pl.when(i == nt - 1)
        def _():
            gw_ref[...] = gw_acc[...].astype(BF16)

    tile = pl.BlockSpec((TM, D), lambda i: (i, 0))
    row = pl.BlockSpec((1, D), lambda i: (0, 0))
    return pl.pallas_call(
        body, name="branch_a_bwd1", grid=(nt,),
        out_shape=(jax.ShapeDtypeStruct((NSEG, T, D), BF16), jax.ShapeDtypeStruct((T, D), F32),
                   jax.ShapeDtypeStruct((D, D), BF16), jax.ShapeDtypeStruct((3, 8, D), F32)),
        in_specs=[pl.BlockSpec(memory_space=ANY), tile, tile, pl.BlockSpec((TM, D), lambda i: (i, 2)), row, row,
                  pl.BlockSpec((None, D, D), lambda i: (0, 0, 0))],
        out_specs=(pl.BlockSpec((None, TM, D), lambda i: (2, i, 0)), tile, pl.BlockSpec((D, D), lambda i: (0, 0)),
                   pl.BlockSpec((3, 8, D), lambda i: (0, 0, 0))),
        scratch_shapes=[VMEM((TM, D), F32), VMEM((TM, D), BF16), VMEM((D, D), F32)],
        input_output_aliases={0: 0},
        compiler_params=_cp(("arbitrary",)),
    )(dp, dya, yc, p, lg, lb, wq)


def _branch_a_bwd2(dp, dyc, p, cw, gw_co, gw_so, gw_o, early):
    T = p.shape[0]
    nt = T // TM
    hb = TM // HALO

    def body(dp_in, dyc_ref, hdyc_ref, val_ref, glu_ref, cw_ref, gco_ref, gso_ref, go_ref, e_ref,
             dp_ref, gcw_ref, rq_ref, eg_ref, a_s, sd_ref, da_s, x_send, x_recv, x_loc, ag_send, ag_recv, ag_loc):
        del dp_in
        i = pl.program_id(0)
        start_exchange, wait_exchange = _owner_exchange((gco_ref, gso_ref, go_ref), rq_ref, x_send, x_recv, x_loc)
        ag_start, ag_relay, ag_finish = _allgather_phases([e_ref], lambda a, slot: eg_ref.at[slot],
                                                          ag_send, ag_recv, ag_loc)

        @pl.when(i == 0)
        def _():
            gcw_ref[...] = jnp.zeros_like(gcw_ref)
            ag_start()
            start_exchange()

        @pl.when(i == nt // 2)
        def _():
            ag_relay()

        def fill(rb, carry):
            rows = _rows(rb)
            a_s[rows, :] = val_ref[rows, :] * _sig(glu_ref[rows, :])
            return carry
        lax.fori_loop(0, TM // RB, fill, 0)
        sd_ref[0, 0:TM, :] = dyc_ref[...]
        sd_ref[0, TM:EXT, :] = jnp.where(i < nt - 1, hdyc_ref[...], 0.0)
        _shift_copies(sd_ref)

        def conv_t(rb, carry):
            r0 = rb * RBC
            accs = [jnp.zeros((8, D), F32)] * (RBC // 8)
            for k in range(KW):
                o = KW - 1 - k
                w = cw_ref[k]
                accs = [acc + w * sd_ref[o % 8, pl.ds(pl.multiple_of(r0 + 8 * (o // 8 + g), 8), 8), :]
                        for g, acc in enumerate(accs)]
            for g, acc in enumerate(accs):
                da_s[pl.ds(pl.multiple_of(r0 + 8 * g, 8), 8), :] = acc
            return carry
        lax.fori_loop(0, TM // RBC, conv_t, 0)

        for k0 in range(0, KW, TAPG):
            taps = list(range(k0, min(k0 + TAPG, KW)))

            def tap_group(rb, accs, taps=taps):
                for u in range(2):
                    r0 = rb * 32 + 16 * u
                    a = a_s[pl.ds(pl.multiple_of(r0, 16), 16), :]
                    out = []
                    for k, acc in zip(taps, accs):
                        o = KW - 1 - k
                        d = sd_ref[o % 8, pl.ds(pl.multiple_of(r0 + 8 * (o // 8), 8), 16), :]
                        out.append(acc + _fold8(a * d))
                    accs = tuple(out)
                return accs
            sums = lax.fori_loop(0, TM // 32, tap_group, tuple(jnp.zeros((8, D), F32) for _ in taps))
            for k, s in zip(taps, sums):
                gcw_ref[k] = gcw_ref[k] + s

        def glu_b(rb, carry):
            rows = _rows(rb)
            da = da_s[rows, :]
            sg = _sig(glu_ref[rows, :])
            dp_ref[0, rows, :] = (da * sg).astype(BF16)
            dp_ref[1, rows, :] = (da * val_ref[rows, :] * (sg * (1.0 - sg))).astype(BF16)
            return carry
        lax.fori_loop(0, TM // RB, glu_b, 0)

        @pl.when(i == nt - 1)
        def _():
            wait_exchange()
            ag_finish()

    tile = lambda seg: pl.BlockSpec((TM, D), lambda i: (i, seg))
    hbm = pl.BlockSpec(memory_space=ANY)
    return pl.pallas_call(
        body, name="branch_a_bwd2", grid=(nt,),
        out_shape=(jax.ShapeDtypeStruct((NSEG, T, D), BF16), jax.ShapeDtypeStruct((32, 8, D), F32),
                   jax.ShapeDtypeStruct((NDEV, 3, 128, D), BF16), jax.ShapeDtypeStruct((NDEV, NEARLY, D), F32)),
        in_specs=[hbm, pl.BlockSpec((TM, D), lambda i: (i, 0)),
                  pl.BlockSpec((HALO, D), lambda i: (jnp.minimum((i + 1) * hb, nt * hb - 1), 0)),
                  tile(0), tile(1), pl.BlockSpec((KW, 8, D), lambda i: (0, 0, 0)),
                  hbm, hbm, hbm, hbm],
        out_specs=(pl.BlockSpec((2, TM, D), lambda i: (0, i, 0)), pl.BlockSpec((32, 8, D), lambda i: (0, 0, 0)), hbm, hbm),
        scratch_shapes=[VMEM((TM, D), F32), VMEM((8, EXT, D), F32), VMEM((TM, D), F32),
                        pltpu.SemaphoreType.DMA((3, 7)), pltpu.SemaphoreType.DMA((3, 7)), pltpu.SemaphoreType.DMA((3,)),
                        pltpu.SemaphoreType.DMA((1, 7)), pltpu.SemaphoreType.DMA((1, 7)), pltpu.SemaphoreType.DMA((1,))],
        input_output_aliases={0: 0},
        compiler_params=_cp(("arbitrary",)),
    )(dp, dyc, dyc, p, p, cw, gw_co, gw_so, gw_o, early)


def _in_bwd_dx(dp, wing, x, dx2, mod, g_pre):
    T = x.shape[0]
    nt = T // TM

    def body(dp_ref, w_ref, x_ref, dx2_ref, mod_ref, g_ref, gx_ref, st_ref, dh_s):
        i = pl.program_id(0)

        @pl.when(i == 0)
        def _():
            st_ref[...] = jnp.zeros_like(st_ref)

        dh_s[...] = lax.dot_general(dp_ref[0], w_ref[0], NT, preferred_element_type=F32)
        for j in range(1, NSEG):
            dh_s[...] += lax.dot_general(dp_ref[j], w_ref[j], NT, preferred_element_type=F32)

        def blk(rb, carry):
            d_sh, d_sc, g_g = carry
            rows = _rows(rb)
            xv = x_ref[rows, :]
            r = lax.rsqrt(jnp.mean(xv * xv, axis=-1, keepdims=True) + EPS)
            xn = xv * r
            g = g_ref[...]
            hpre = xn * g
            dh = dh_s[rows, :]
            dhp = dh * (1.0 + mod_ref[1:2, :])
            dxn = dhp * g
            gx_ref[rows, :] = dx2_ref[rows, :] + r * (dxn - xn * jnp.mean(dxn * xn, axis=-1, keepdims=True))
            return d_sh + _fold8(dh), d_sc + _fold8(dh * hpre), g_g + _fold8(dhp * xn)
        zero = jnp.zeros((8, D), F32)
        d_sh, d_sc, g_g = _blocks(TM // RB, blk, (zero, zero, zero))
        st_ref[0] = st_ref[0] + d_sh
        st_ref[1] = st_ref[1] + d_sc
        st_ref[2] = st_ref[2] + g_g

    tile = pl.BlockSpec((TM, D), lambda i: (i, 0))
    return pl.pallas_call(
        body, name="in_bwd_dx", grid=(nt,),
        out_shape=(jax.ShapeDtypeStruct((T, D), F32), jax.ShapeDtypeStruct((3, 8, D), F32)),
        in_specs=[pl.BlockSpec((NSEG, TM, D), lambda i: (0, i, 0)),
                  pl.BlockSpec((NSEG, D, D), lambda i: (0, 0, 0), pipeline_mode=pl.Buffered(1)),
                  tile, tile, pl.BlockSpec((3, D), lambda i: (0, 0)), pl.BlockSpec((1, D), lambda i: (0, 0))],
        out_specs=(tile, pl.BlockSpec((3, 8, D), lambda i: (0, 0, 0))),
        scratch_shapes=[VMEM((TM, D), F32)],
        compiler_params=_cp(("arbitrary",), vmem_mb=56),
    )(dp, wing, x, dx2, mod, g_pre)


RS_ORDER = (5, 4, 3, 2, 7, 6, 1, 0)
RS_HALF = D // 2


def _in_bwd_dw(ht, dp, order, smalls):
    T = ht.shape[1]
    tmw = min(2 * TMB, T)
    nb = T // tmw
    nu = 2 * NSEG

    def body(order_ref, ht_ref, dp_ref, sm_ref, g_ref, smg_ref, acc, sbuf, rsib, rici, send_sems, recv_sems,
             ag_send, ag_recv, ag_loc):
        j = pl.program_id(0)
        i = pl.program_id(1)
        x, y, c = _me()
        sibling = (x, y, 1 - c)
        chips = [(1 - x, y), (x, 1 - y), (1 - x, 1 - y)]
        ag_start, ag_relay, ag_finish = _allgather_phases([sm_ref], lambda a, slot: smg_ref.at[slot],
                                                          ag_send, ag_recv, ag_loc)

        def to_sibling(s):
            return pltpu.make_async_remote_copy(src_ref=sbuf.at[s], dst_ref=rsib.at[s], send_sem=send_sems.at[s],
                                                recv_sem=recv_sems.at[s], device_id=sibling, device_id_type=MESH)

        def to_owner(t):
            return pltpu.make_async_remote_copy(src_ref=sbuf.at[8 + t], dst_ref=rici.at[t], send_sem=send_sems.at[8 + t],
                                                recv_sem=recv_sems.at[8 + t], device_id=(*chips[t // 2], c),
                                                device_id_type=MESH)

        @pl.when((j == 0) & (i == 0))
        def _():
            ag_start()

        @pl.when((j == 4) & (i == 0))
        def _():
            ag_relay()

        @pl.when(i == 0)
        def _():
            acc[...] = jnp.zeros_like(acc)
        acc[...] += jnp.dot(ht_ref[...], dp_ref[...], preferred_element_type=F32)

        for jj in range(nu):
            chip, half, mine = jj // 4, (jj % 4) // 2, jj % 2
            s = 2 * chip + half

            @pl.when((j == jj) & (i == nb - 1))
            def _(chip=chip, half=half, mine=mine, s=s):
                if not mine:
                    sbuf[s] = acc[...].astype(BF16)
                    to_sibling(s).start()
                elif chip < 3:
                    to_sibling(s).wait_recv()
                    sbuf[8 + s] = (acc[...] + rsib[s].astype(F32)).astype(BF16)
                    to_owner(s).start()
                else:
                    to_sibling(s).wait_recv()
                    g = acc[...] + rsib[s].astype(F32)
                    for q in range(3):
                        to_owner(2 * q + half).wait_recv()
                        g = g + rici[2 * q + half].astype(F32)
                    g_ref[:, half * RS_HALF:(half + 1) * RS_HALF] = g

        @pl.when((j == nu - 1) & (i == nb - 1))
        def _():
            for s in range(8):
                to_sibling(s).wait_send()
            for t in range(6):
                to_owner(t).wait_send()
            ag_finish()

    hbm = pl.BlockSpec(memory_space=ANY)
    return pl.pallas_call(
        body, name="in_bwd_dw",
        out_shape=(jax.ShapeDtypeStruct((D, D), F32), jax.ShapeDtypeStruct((NDEV, NLATE, D), F32)),
        grid_spec=pltpu.PrefetchScalarGridSpec(
            num_scalar_prefetch=1, grid=(nu, nb),
            in_specs=[pl.BlockSpec((D, tmw), lambda j, i, o: (0, i)),
                      pl.BlockSpec((None, tmw, RS_HALF), lambda j, i, o: (o[2 * (j // 4) + j % 2], i, (j % 4) // 2)),
                      hbm],
            out_specs=(pl.BlockSpec((D, D), lambda j, i, o: (0, 0)), hbm),
            scratch_shapes=[VMEM((D, RS_HALF), F32), VMEM((14, D, RS_HALF), BF16), VMEM((8, D, RS_HALF), BF16),
                            VMEM((6, D, RS_HALF), BF16), pltpu.SemaphoreType.DMA((14,)), pltpu.SemaphoreType.DMA((14,)),
                            pltpu.SemaphoreType.DMA((1, 7)), pltpu.SemaphoreType.DMA((1, 7)), pltpu.SemaphoreType.DMA((1,))]),
        compiler_params=_cp(("arbitrary", "arbitrary"), vmem_mb=56),
    )(order, ht, dp, smalls)


def _pack_early(st_a, st_b, st_m, gws, gbs):
    def body(sa_ref, sb_ref, sm_ref, gws_ref, gbs_ref, out_ref):
        out_ref[...] = jnp.zeros_like(out_ref)
        fold = lambda v: jnp.sum(v, axis=0, keepdims=True)
        out_ref[0:1, :] = fold(sm_ref[2])
        out_ref[1:2, :] = fold(sa_ref[2])
        out_ref[2:3, :] = fold(sa_ref[0])
        out_ref[3:4, :] = fold(sa_ref[1])
        out_ref[4:5, :] = fold(sb_ref[0])
        out_ref[5:6, :] = fold(sb_ref[1])
        out_ref[7:8, :] = fold(sm_ref[1])
        out_ref[8:9, :] = fold(sm_ref[0])
        tri = lax.broadcasted_iota(jnp.int32, (CHUNK, CHUNK), 0) >= lax.broadcasted_iota(jnp.int32, (CHUNK, CHUNK), 1)
        for h in range(HEADS):
            out_ref[16:16 + CHUNK, h * CHUNK:(h + 1) * CHUNK] = jnp.where(tri, gws_ref[h], 0.0)
            out_ref[6:7, h * CHUNK:(h + 1) * CHUNK] = fold(gbs_ref[h].T)

    vm = pl.BlockSpec(memory_space=VMEM)
    return pl.pallas_call(
        body, name="pack_early", out_shape=jax.ShapeDtypeStruct((NEARLY, D), F32),
        in_specs=[vm] * 5, out_specs=vm, compiler_params=_cp(vmem_mb=32),
    )(st_a, st_b, st_m, gws, gbs)


def _pack_late(st_x, gcw):
    def body(sx_ref, gcw_ref, out_ref):
        out_ref[...] = jnp.zeros_like(out_ref)
        fold = lambda v: jnp.sum(v, axis=0, keepdims=True)
        for r in range(3):
            out_ref[r:r + 1, :] = fold(sx_ref[r])
        for k in range(KW):
            out_ref[8 + k:9 + k, :] = fold(gcw_ref[k])

    vm = pl.BlockSpec(memory_space=VMEM)
    return pl.pallas_call(
        body, name="pack_late", out_shape=jax.ShapeDtypeStruct((NLATE, D), F32),
        in_specs=[vm] * 2, out_specs=vm, compiler_params=_cp(vmem_mb=32),
    )(st_x, gcw)


def _pack_params(name, b_ada, g_pre, conv_b, lg, lb, sg, sb, b_sgu, g_final, w_sgu_t):
    def body(ba_ref, *refs):
        rows, ws_ref, out_ref = refs[:8], refs[8], refs[9]
        out_ref[...] = jnp.zeros_like(out_ref)
        for r in range(3):
            out_ref[r:r + 1, :] = ba_ref[:, r * D:(r + 1) * D]
        for r, ref in enumerate(rows):
            out_ref[3 + r:4 + r, :] = ref[...]
        out_ref[ROW_WS:ROW_WS + CHUNK, :] = ws_ref[...]

    vm = pl.BlockSpec(memory_space=VMEM)
    return pl.pallas_call(
        body, name=name, out_shape=jax.ShapeDtypeStruct((NSMALL, D), F32),
        in_specs=[vm] * 10, out_specs=vm, compiler_params=_cp(vmem_mb=32),
    )(b_ada, g_pre, conv_b, lg, lb, sg, sb, b_sgu, g_final, w_sgu_t)


def _adam(w, g, m, v):
    m2 = ADAM_B1 * m + (1.0 - ADAM_B1) * g
    v2 = ADAM_B2 * v + (1.0 - ADAM_B2) * (g * g)
    m_hat = m2 / (1.0 - ADAM_B1 ** ADAM_STEP)
    v_hat = v2 / (1.0 - ADAM_B2 ** ADAM_STEP)
    delta = -ADAM_LR * (m_hat / (jnp.sqrt(v_hat) + ADAM_EPS) + ADAM_WD * w)
    return delta, m2, v2


def _small_finish(eg, lg, w, m, v):
    def body(eg_ref, lg_ref, w_ref, m_ref, v_ref, loss_ref, g_ref, d_ref, m2_ref, v2_ref):
        e = eg_ref[0]
        l = lg_ref[0]
        for s in range(1, NDEV):
            e = e + eg_ref[s]
            l = l + lg_ref[s]
        g_ref[...] = jnp.zeros_like(g_ref)
        g_ref[0:2, :] = l[0:2, :]
        g_ref[2:3, :] = e[0:1, :]
        g_ref[3:4, :] = l[2:3, :]
        g_ref[4:12, :] = e[1:9, :]
        g_ref[ROW_CW:ROW_CW + KW, :] = l[8:8 + KW, :]
        g_ref[ROW_WS:ROW_WS + CHUNK, :] = e[16:16 + CHUNK, :]
        loss_ref[...] = jnp.sum(e[8:9, :], axis=1, keepdims=True)
        d_ref[...], m2_ref[...], v2_ref[...] = _adam(w_ref[...], g_ref[...], m_ref[...], v_ref[...])

    vm = pl.BlockSpec(memory_space=VMEM)
    sd = jax.ShapeDtypeStruct((NSMALL, D), F32)
    return pl.pallas_call(
        body, name="small_finish", out_shape=(jax.ShapeDtypeStruct((1, 1), F32), sd, sd, sd, sd),
        in_specs=[vm] * 5, out_specs=(vm,) * 5, compiler_params=_cp(vmem_mb=32),
    )(eg, lg, w, m, v)


def _ada_grad_adam(ct, dm, w, m, v):
    def body(ct_ref, dm_ref, w_ref, m_ref, v_ref, g_ref, d_ref, m2_ref, v2_ref):
        g = ct_ref[:, 0:1] * dm_ref[0:1, :]
        for b in range(1, NDEV):
            g = g + ct_ref[:, b:b + 1] * dm_ref[b:b + 1, :]
        g_ref[...] = g
        d_ref[...], m2_ref[...], v2_ref[...] = _adam(w_ref[...], g, m_ref[...], v_ref[...])

    vm = pl.BlockSpec(memory_space=VMEM)
    sd = jax.ShapeDtypeStruct(w.shape, F32)
    return pl.pallas_call(
        body, name="ada_grad_adam", out_shape=(sd, sd, sd, sd),
        in_specs=[vm] * 5, out_specs=(vm,) * 4, compiler_params=_cp(vmem_mb=32),
    )(ct, dm, w, m, v)


def _adam_f32(name, g, w, m, v, rows=None):
    R, C = w.shape
    rows = R if rows is None else rows

    def body(g_ref, w_ref, m_ref, v_ref, d_ref, m2_ref, v2_ref):
        d_ref[...], m2_ref[...], v2_ref[...] = _adam(w_ref[...], g_ref[...], m_ref[...], v_ref[...])

    tile = pl.BlockSpec((rows, C), lambda i: (i, 0))
    sd = jax.ShapeDtypeStruct(w.shape, F32)
    return pl.pallas_call(
        body, name=name, grid=(R // rows,), out_shape=(sd, sd, sd), in_specs=[tile] * 4, out_specs=(tile,) * 3,
        compiler_params=_cp(("parallel",), vmem_mb=32),
    )(g, w, m, v)


def _adam_reduce(name, recv, w, m, v, rows, recv_spec):
    R, C = w.shape

    def body(r_ref, w_ref, m_ref, v_ref, g_ref, d_ref, m2_ref, v2_ref):
        g = r_ref[0].astype(F32)
        for s in range(1, NDEV):
            g = g + r_ref[s].astype(F32)
        g_ref[...] = g
        d_ref[...], m2_ref[...], v2_ref[...] = _adam(w_ref[...], g, m_ref[...], v_ref[...])

    tile = pl.BlockSpec((rows, C), lambda i: (i, 0))
    sd = jax.ShapeDtypeStruct((R, C), F32)
    return pl.pallas_call(
        body, name=name, grid=(R // rows,), out_shape=(sd, sd, sd, sd),
        in_specs=[recv_spec, tile, tile, tile], out_specs=(tile,) * 4,
        compiler_params=_cp(("parallel",)),
    )(recv, w, m, v)


def kernel(x, c, w_ada, b_ada, g_pre, w_in, conv_w, conv_b, conv_ln_g, conv_ln_b, w_conv_out, sgu_ln_g, sgu_ln_b, w_sgu, b_sgu, w_sgu_out, w_o, g_final, loss_target, m_w_ada, m_b_ada, m_g_pre, m_w_in, m_conv_w, m_conv_b, m_conv_ln_g, m_conv_ln_b, m_w_conv_out, m_sgu_ln_g, m_sgu_ln_b, m_w_sgu, m_b_sgu, m_w_sgu_out, m_w_o, m_g_final, v_w_ada, v_b_ada, v_g_pre, v_w_in, v_conv_w, v_conv_b, v_conv_ln_g, v_conv_ln_b, v_w_conv_out, v_sgu_ln_g, v_sgu_ln_b, v_w_sgu, v_b_sgu, v_w_sgu_out, v_w_o, v_g_final):
    T = x.shape[1]
    assert T % TMB == 0 and x.shape[2] == D
    xs = x[0]
    tgt = loss_target[0]
    my = 4 * lax.axis_index("x") + 2 * lax.axis_index("y") + lax.axis_index("c")

    mod, cg = _ada_fwd(c, w_ada[0], b_ada)
    gf = g_final.reshape(1, D)
    bst = b_sgu[0].T

    h, ht = _prep_h(xs, mod, g_pre)
    p, wing, wqg, cwg = _in_proj(h, w_in[0], my ^ jnp.array(AG_ORDER, jnp.int32),
                                 w_conv_out[0], w_sgu_out[0], w_o[0], conv_w[0])
    wq = wqg.reshape(3, D, D)
    cw = jnp.broadcast_to(jnp.transpose(cwg, (1, 0, 2)).reshape(KW, 1, D), (KW, 8, D))
    yc, ya = _branch_a_fwd(p, cw, conv_b, conv_ln_g, conv_ln_b, wq)
    yb = _branch_b_fwd(p, w_sgu[0], bst, sgu_ln_g, sgu_ln_b, wq)
    dx2, dya, dyb, dp, gw_o, st_m = _merge_loss(p, ya, yb, xs, tgt, mod, gf, wq)

    dp, gw_so, gws, gbs, st_b = _branch_b_bwd(dp, dyb, p, w_sgu[0], bst, sgu_ln_g, sgu_ln_b, wq)
    dp, dyc, gw_co, st_a = _branch_a_bwd1(dp, dya, yc, p, conv_ln_g, conv_ln_b, wq)
    early = _pack_early(st_a, st_b, st_m, gws, gbs)
    dp, gcw, rq, eg = _branch_a_bwd2(dp, dyc, p, cw, gw_co, gw_so, gw_o, early)
    grad_x, st_x = _in_bwd_dx(dp, wing, xs, dx2, mod, g_pre)
    late = _pack_late(st_x, gcw)
    g_in, lg = _in_bwd_dw(ht, dp, my ^ jnp.array(RS_ORDER, jnp.int32), late)

    def pack(name, b_ada_, g_pre_, conv_b_, lg_, lb_, sg_, sb_, b_sgu_, gfin_, w_sgu_):
        return _pack_params(name, b_ada_, g_pre_, conv_b_, lg_, lb_, sg_, sb_, b_sgu_.reshape(1, D), gfin_.reshape(1, D),
                            jnp.transpose(w_sgu_[0], (1, 0, 2)).reshape(CHUNK, D))

    pw = pack("pack_w", b_ada, g_pre, conv_b, conv_ln_g, conv_ln_b, sgu_ln_g, sgu_ln_b, b_sgu, g_final, w_sgu)
    pm = pack("pack_m", m_b_ada, m_g_pre, m_conv_b, m_conv_ln_g, m_conv_ln_b, m_sgu_ln_g, m_sgu_ln_b, m_b_sgu, m_g_final,
              m_w_sgu)
    pv = pack("pack_v", v_b_ada, v_g_pre, v_conv_b, v_conv_ln_g, v_conv_ln_b, v_sgu_ln_g, v_sgu_ln_b, v_b_sgu, v_g_final,
              v_w_sgu)
    loss11, sg_, sd_, sm_, sv_ = _small_finish(eg, lg, pw, pm, pv)

    def unpack(a):
        return dict(
            b_ada=a[0:3].reshape(1, 3 * D), g_pre=a[3:4], conv_b=a[4:5], conv_ln_g=a[5:6], conv_ln_b=a[6:7],
            sgu_ln_g=a[7:8], sgu_ln_b=a[8:9], b_sgu=a[9:10].reshape(1, HEADS, CHUNK), g_final=a[10],
            w_sgu=jnp.transpose(a[ROW_WS:ROW_WS + CHUNK].reshape(CHUNK, HEADS, CHUNK), (1, 0, 2))[None])
    small = [unpack(a) for a in (sg_, sd_, sm_, sv_)]

    g_cw = lax.dynamic_slice_in_dim(sg_[ROW_CW:ROW_CW + KW], my * 128, 128, axis=1)
    d_cw, m_cw, v_cw = _adam_f32("adam_conv_w", g_cw, conv_w[0], m_conv_w[0], v_conv_w[0])

    wcols = w_ada.shape[2]
    dm_all = jnp.concatenate([lg[:, 0], lg[:, 1], eg[:, 0]], axis=1)
    dm_mine = lax.dynamic_slice_in_dim(dm_all, my * wcols, wcols, axis=1)
    g_ada, d_ada, m_ada, v_ada = _ada_grad_adam(cg.T, dm_mine, w_ada[0], m_w_ada[0], v_w_ada[0])

    d_in, m_in, v_in = _adam_f32("adam_w_in", g_in, w_in[0], m_w_in[0], v_w_in[0], rows=256)
    big = {}
    for j, (nm, w_, m_, v_) in enumerate((("w_conv_out", w_conv_out, m_w_conv_out, v_w_conv_out),
                                          ("w_sgu_out", w_sgu_out, m_w_sgu_out, v_w_sgu_out),
                                          ("w_o", w_o, m_w_o, v_w_o))):
        big[nm] = _adam_reduce("adam_" + nm, rq, w_[0], m_[0], v_[0], 128,
                               pl.BlockSpec((NDEV, None, 128, D), lambda i, j=j: (0, j, 0, 0)))

    per = {
        "w_ada": tuple(a[None] for a in (g_ada, d_ada, m_ada, v_ada)),
        "w_in": tuple(a[None] for a in (g_in, d_in, m_in, v_in)),
        "conv_w": tuple(a[None] for a in (g_cw, d_cw, m_cw, v_cw)),
    }
    for nm in ("w_conv_out", "w_sgu_out", "w_o"):
        per[nm] = tuple(a[None] for a in big[nm])
    for nm in ("b_ada", "g_pre", "conv_b", "conv_ln_g", "conv_ln_b", "sgu_ln_g", "sgu_ln_b", "w_sgu", "b_sgu", "g_final"):
        per[nm] = tuple(s[nm] for s in small)

    order = ["w_ada", "b_ada", "g_pre", "w_in", "conv_w", "conv_b", "conv_ln_g", "conv_ln_b", "w_conv_out",
             "sgu_ln_g", "sgu_ln_b", "w_sgu", "b_sgu", "w_sgu_out", "w_o", "g_final"]
    outs = [loss11.reshape(()), grad_x[None]]
    for part in range(4):
        outs += [per[nm][part] for nm in order]
    return tuple(outs)
```

```python
import jax
import jax.numpy as jnp
from jax import lax
from jax.experimental import pallas as pl
from jax.experimental.pallas import tpu as pltpu

F32 = jnp.float32
BF16 = jnp.bfloat16
MESH = pl.DeviceIdType.MESH
VMEM = pltpu.VMEM
ANY = pl.ANY

D = 1024
NDEV = 8
NSEG = 8
HEADS = 8
HD = 128
CHUNK = 128
KW = 31
HALO = 32
EPS = 1e-6
TM = 256
TMB = 512
RB = 32
UNROLL = 4
RBC = 32
TAPG = 4
EXT = TM + HALO
NSMALL = 176
NEARLY = 144
NLATE = 40
ROW_CW = 16
ROW_WS = 48
W_CO, W_SO, W_O = 0, 1, 2

ADAM_LR = 0.001
ADAM_B1 = 0.9
ADAM_B2 = 0.999
ADAM_EPS = 1e-08
ADAM_WD = 0.01
ADAM_STEP = 10

INV_SQRT2 = 0.7071067811865476
INV_SQRT_2PI = 0.3989422804014327

NT = (((1,), (1,)), ((), ()))
TN = (((0,), (0,)), ((), ()))


def _cp(sem=None, vmem_mb=48):
    return pltpu.CompilerParams(dimension_semantics=sem, vmem_limit_bytes=vmem_mb * 1024 * 1024)


def _me():
    return lax.axis_index("x"), lax.axis_index("y"), lax.axis_index("c")


def _slot(px, py, pc):
    return 4 * px + 2 * py + pc


def _xor_peer(k):
    x, y, c = _me()
    return (1 - x if k & 4 else x, 1 - y if k & 2 else y, 1 - c if k & 1 else c)


def _allgather_phases(srcs, dst_at, send_sems, recv_sems, loc_sems):
    x, y, c = _me()
    me = (x, y, c)
    sibling = (x, y, 1 - c)
    chips = [(1 - x, y), (x, 1 - y), (1 - x, 1 - y)]
    na = len(srcs)

    def copy(a, k, block, to, src=None):
        d = dst_at(a, _slot(*block))
        return pltpu.make_async_remote_copy(
            src_ref=d if src is None else src, dst_ref=d,
            send_sem=send_sems.at[a, k], recv_sem=recv_sems.at[a, k],
            device_id=to, device_id_type=MESH)

    local = [pltpu.make_async_copy(srcs[a], dst_at(a, _slot(*me)), loc_sems.at[a]) for a in range(na)]
    first = []
    for a in range(na):
        first.append(copy(a, 0, me, sibling, src=srcs[a]))
        for j, chip in enumerate(chips):
            first.append(copy(a, 1 + j, me, (*chip, c), src=srcs[a]))
    passed = [copy(a, 4 + j, (*chip, c), sibling) for j, chip in enumerate(chips) for a in range(na)]

    def start():
        for cp in local + first:
            cp.start()

    def relay():
        for j, chip in enumerate(chips):
            for a in range(na):
                copy(a, 1 + j, (*chip, c), me).wait_recv()
                passed[j * na + a].start()

    def finish():
        for a in range(na):
            copy(a, 0, sibling, me).wait_recv()
        for j, chip in enumerate(chips):
            for a in range(na):
                copy(a, 4 + j, (*chip, 1 - c), me).wait_recv()
        for cp in first + passed:
            cp.wait_send()
        for cp in local:
            cp.wait()

    return start, relay, finish


def _allgather(srcs, dst_at, send_sems, recv_sems, loc_sems):
    start, relay, finish = _allgather_phases(srcs, dst_at, send_sems, recv_sems, loc_sems)
    start()
    relay()
    finish()


def _owner_exchange(gq, rq_ref, x_send, x_recv, x_loc):
    mx, my_, mc = _me()
    me = _slot(mx, my_, mc)

    def rows_of(a, slot):
        return gq[a].at[pl.ds(pl.multiple_of(slot * 128, 128), 128), :]

    def exchange(k, a, recv):
        px, py, pc = _xor_peer(k)
        peer = _slot(px, py, pc)
        return pltpu.make_async_remote_copy(
            src_ref=rows_of(a, me if recv else peer), dst_ref=rq_ref.at[peer if recv else me, a],
            send_sem=x_send.at[a, k - 1], recv_sem=x_recv.at[a, k - 1],
            device_id=(px, py, pc), device_id_type=MESH)

    local = [pltpu.make_async_copy(rows_of(a, me), rq_ref.at[me, a], x_loc.at[a]) for a in range(len(gq))]

    def start():
        for cp in local:
            cp.start()
        for k in range(1, NDEV):
            for a in range(len(gq)):
                exchange(k, a, False).start()

    def wait():
        for k in range(1, NDEV):
            for a in range(len(gq)):
                exchange(k, a, True).wait_recv()
        for k in range(1, NDEV):
            for a in range(len(gq)):
                exchange(k, a, False).wait_send()
        for cp in local:
            cp.wait()

    return start, wait


def _sig(v):
    return jax.nn.sigmoid(v)


def _gelu_parts(v):
    cdf = 0.5 * (1.0 + lax.erf(v * INV_SQRT2))
    pdf = jnp.exp(-0.5 * v * v) * INV_SQRT_2PI
    return v * cdf, cdf + v * pdf


def _ln_stats(v):
    mu = jnp.mean(v, axis=-1, keepdims=True)
    xc = v - mu
    var = jnp.mean(xc * xc, axis=-1, keepdims=True)
    rs = lax.rsqrt(var + EPS)
    return xc * rs, rs


def _ln_bwd(dn, n, rs):
    return rs * (dn - jnp.mean(dn, axis=-1, keepdims=True) - n * jnp.mean(dn * n, axis=-1, keepdims=True))


def _fold8(v):
    acc = v[0:8]
    for r in range(8, v.shape[0], 8):
        acc = acc + v[r:r + 8]
    return acc


def _rows(rb, n=RB):
    return pl.ds(pl.multiple_of(rb * n, n), n)


def _blocks(n, body, init=0, unroll=UNROLL):
    def trip(t, carry):
        for u in range(unroll):
            carry = body(t * unroll + u, carry)
        return carry
    return lax.fori_loop(0, n // unroll, trip, init)


def _shift_copies(s_ref):
    n = EXT - 8
    for m in range(1, 8):
        for c0 in range(0, n, 56):
            s_ref[m, c0:c0 + 56, :] = s_ref[0, c0 + m:c0 + m + 56, :]


def _ada_fwd(c, w_ada, b_ada):
    wcols = w_ada.shape[1]

    def body(c_ref, w_ref, b_ref, mod_ref, call_ref, cdst, msrc, mdst, c_send, c_recv, m_send, m_recv):
        x, y, c = _me()
        me = _slot(x, y, c)

        def all_to_all(src_of, dst, ss, rs):
            def cp(k, recv):
                px, py, pc = _xor_peer(k)
                peer = _slot(px, py, pc)
                return pltpu.make_async_remote_copy(
                    src_ref=src_of(me if recv else peer), dst_ref=dst.at[peer if recv else me],
                    send_sem=ss.at[k - 1], recv_sem=rs.at[k - 1], device_id=(px, py, pc), device_id_type=MESH)
            for k in range(1, NDEV):
                cp(k, False).start()
            for k in range(1, NDEV):
                cp(k, True).wait_recv()
            for k in range(1, NDEV):
                cp(k, False).wait_send()

        cdst[me] = c_ref[...]
        all_to_all(lambda s: c_ref, cdst, c_send, c_recv)
        for b in range(NDEV):
            call_ref[b:b + 1, :] = cdst[b]
        m = jnp.dot(call_ref[...], w_ref[...], preferred_element_type=F32, precision=lax.Precision.HIGHEST)
        for b in range(NDEV):
            msrc[b] = m[b:b + 1, :]
        mdst[me] = msrc[me]
        all_to_all(lambda s: msrc.at[s], mdst, m_send, m_recv)
        full = jnp.concatenate([mdst[k] for k in range(NDEV)], axis=1) + b_ref[...]
        for r in range(3):
            mod_ref[r:r + 1, :] = full[:, r * D:(r + 1) * D]

    vm = pl.BlockSpec(memory_space=VMEM)
    return pl.pallas_call(
        body, name="ada_fwd",
        out_shape=(jax.ShapeDtypeStruct((3, D), F32), jax.ShapeDtypeStruct((NDEV, D), F32)),
        in_specs=[vm, vm, vm], out_specs=(vm, vm),
        scratch_shapes=[VMEM((NDEV, 1, D), F32), VMEM((NDEV, 1, wcols), F32), VMEM((NDEV, 1, wcols), F32),
                        pltpu.SemaphoreType.DMA((7,)), pltpu.SemaphoreType.DMA((7,)),
                        pltpu.SemaphoreType.DMA((7,)), pltpu.SemaphoreType.DMA((7,))],
        compiler_params=_cp(vmem_mb=32),
    )(c, w_ada, b_ada)


def _prep_h(x, mod, g_pre):
    T = x.shape[0]

    def body(x_ref, mod_ref, g_ref, h_ref, ht_ref):
        def blk(rb, carry):
            rows = _rows(rb)
            xv = x_ref[rows, :]
            r = lax.rsqrt(jnp.mean(xv * xv, axis=-1, keepdims=True) + EPS)
            h_ref[rows, :] = ((xv * r) * g_ref[...] * (1.0 + mod_ref[1:2, :]) + mod_ref[0:1, :]).astype(BF16)
            return carry
        _blocks(TMB // RB, blk)
        ht_ref[...] = h_ref[...].T

    return pl.pallas_call(
        body, name="prep_h", grid=(T // TMB,),
        out_shape=(jax.ShapeDtypeStruct((T, D), BF16), jax.ShapeDtypeStruct((D, T), BF16)),
        in_specs=[pl.BlockSpec((TMB, D), lambda i: (i, 0)), pl.BlockSpec((3, D), lambda i: (0, 0)),
                  pl.BlockSpec((1, D), lambda i: (0, 0))],
        out_specs=(pl.BlockSpec((TMB, D), lambda i: (i, 0)), pl.BlockSpec((D, TMB), lambda i: (0, i))),
        compiler_params=_cp(("parallel",)),
    )(x, mod, g_pre)


AG_ORDER = (0, 1, 4, 5, 2, 3, 6, 7)


def _in_proj(h, w_in, order, w_co, w_so, w_o, conv_w):
    T = h.shape[0]
    tmp = min(2 * TMB, T)
    nb = T // tmp

    def body(order_ref, h_ref, w_ref, wco_ref, wso_ref, wo_ref, cw_ref, p_ref, wing_ref, wqg_ref, cwg_ref,
             wbuf, wq_st, send_sems, recv_sems, out_sems, ag_send, ag_recv, ag_loc):
        j = pl.program_id(0)
        i = pl.program_id(1)
        x, y, c = _me()
        me = (x, y, c)
        sibling = (x, y, 1 - c)
        chips = [(1 - x, y), (x, 1 - y), (1 - x, 1 - y)]
        blocks = [me, sibling] + [(*chip, cc) for chip in chips for cc in (c, 1 - c)]
        recv_k = [None, 0, 1, 4, 2, 5, 3, 6]
        ag_start, ag_relay, ag_finish = _allgather_phases(
            [wq_st, cw_ref], lambda a, slot: wqg_ref.at[:, slot] if a == 0 else cwg_ref.at[slot], ag_send, ag_recv, ag_loc)

        def copy(k, block, to):
            d = wbuf.at[_slot(*block)]
            return pltpu.make_async_remote_copy(src_ref=d, dst_ref=d, send_sem=send_sems.at[k], recv_sem=recv_sems.at[k],
                                                device_id=to, device_id_type=MESH)

        def writeback(jj):
            s = _slot(*blocks[jj])
            return pltpu.make_async_copy(wbuf.at[s], wing_ref.at[s], out_sems.at[jj])

        first = [copy(0, me, sibling)] + [copy(1 + q, me, (*chip, c)) for q, chip in enumerate(chips)]
        passed = [copy(4 + q, (*chip, c), sibling) for q, chip in enumerate(chips)]

        @pl.when((j == 0) & (i == 0))
        def _():
            wbuf[_slot(*me)] = w_ref[...].astype(BF16)
            for cp in first:
                cp.start()
            writeback(0).start()
            wq_st[W_CO] = wco_ref[...].astype(BF16)
            wq_st[W_SO] = wso_ref[...].astype(BF16)
            wq_st[W_O] = wo_ref[...].astype(BF16)
            ag_start()

        for jj in range(1, NSEG):
            @pl.when((j == jj) & (i == 0))
            def _(jj=jj):
                copy(recv_k[jj], blocks[jj], me).wait_recv()
                if jj in (2, 4, 6):
                    passed[jj // 2 - 1].start()
                writeback(jj).start()
                if jj == NSEG - 1:
                    ag_relay()

        p_ref[...] = jnp.dot(h_ref[...], wbuf[order_ref[j]], preferred_element_type=F32)

        @pl.when((j == NSEG - 1) & (i == nb - 1))
        def _():
            for cp in first + passed:
                cp.wait_send()
            for jj in range(NSEG):
                writeback(jj).wait()
            ag_finish()

    vm = pl.BlockSpec(memory_space=VMEM)
    hbm = pl.BlockSpec(memory_space=ANY)
    return pl.pallas_call(
        body, name="in_proj",
        out_shape=(jax.ShapeDtypeStruct((T, NSEG * D), F32), jax.ShapeDtypeStruct((NSEG, D, D), BF16),
                   jax.ShapeDtypeStruct((3, NDEV, 128, D), BF16), jax.ShapeDtypeStruct((NDEV, KW, 128), F32)),
        grid_spec=pltpu.PrefetchScalarGridSpec(
            num_scalar_prefetch=1, grid=(NSEG, nb),
            in_specs=[pl.BlockSpec((tmp, D), lambda j, i, o: (i, 0)), vm, vm, vm, vm, vm],
            out_specs=(pl.BlockSpec((tmp, D), lambda j, i, o: (i, o[j])), hbm, hbm, hbm),
            scratch_shapes=[VMEM((NSEG, D, D), BF16), VMEM((3, 128, D), BF16),
                            pltpu.SemaphoreType.DMA((7,)), pltpu.SemaphoreType.DMA((7,)), pltpu.SemaphoreType.DMA((NSEG,)),
                            pltpu.SemaphoreType.DMA((2, 7)), pltpu.SemaphoreType.DMA((2, 7)), pltpu.SemaphoreType.DMA((2,))]),
        compiler_params=_cp(("arbitrary", "arbitrary")),
    )(order, h, w_in, w_co, w_so, w_o, conv_w)


def _fill_a_ext(i, s_ref, val_ref, glu_ref, hval_ref, hglu_ref):
    ah = hval_ref[...] * _sig(hglu_ref[...])
    s_ref[0, 0:HALO, :] = jnp.where(i > 0, ah, 0.0)

    def fill(rb, carry):
        rows = _rows(rb)
        s_ref[0, pl.ds(pl.multiple_of(HALO + rb * RB, RB), RB), :] = val_ref[rows, :] * _sig(glu_ref[rows, :])
        return carry
    lax.fori_loop(0, TM // RB, fill, 0)


def _halo_prev(seg):
    hb = TM // HALO
    return pl.BlockSpec((HALO, D), lambda i: (jnp.maximum(i * hb - 1, 0), seg))


def _branch_a_fwd(p, cw, conv_b, lg, lb, wq):
    T = p.shape[0]

    def body(val_ref, glu_ref, z_ref, hval_ref, hglu_ref, cw_ref, cb_ref, lg_ref, lb_ref, w_ref,
             yc_ref, ya_ref, s_ref, yp_ref):
        i = pl.program_id(0)
        _fill_a_ext(i, s_ref, val_ref, glu_ref, hval_ref, hglu_ref)
        _shift_copies(s_ref)

        def conv(rb, carry):
            r0 = rb * RBC
            accs = [jnp.broadcast_to(cb_ref[...], (8, D))] * (RBC // 8)
            for k in range(KW):
                o = 2 + k
                w = cw_ref[k]
                accs = [acc + w * s_ref[o % 8, pl.ds(pl.multiple_of(r0 + 8 * (o // 8 + g), 8), 8), :]
                        for g, acc in enumerate(accs)]
            for g, acc in enumerate(accs):
                yc_ref[pl.ds(pl.multiple_of(r0 + 8 * g, 8), 8), :] = acc
            return carry
        lax.fori_loop(0, TM // RBC, conv, 0)

        def post(rb, carry):
            rows = _rows(rb)
            n, _ = _ln_stats(yc_ref[rows, :])
            l = n * lg_ref[...] + lb_ref[...]
            z = z_ref[rows, :]
            yp_ref[rows, :] = ((l * _sig(l)) * (z * _sig(z))).astype(BF16)
            return carry
        _blocks(TM // RB, post)
        ya_ref[...] = jnp.dot(yp_ref[...], w_ref[...], preferred_element_type=F32)

    tile = lambda seg: pl.BlockSpec((TM, D), lambda i: (i, seg))
    row = pl.BlockSpec((1, D), lambda i: (0, 0))
    return pl.pallas_call(
        body, name="branch_a_fwd", grid=(T // TM,),
        out_shape=(jax.ShapeDtypeStruct((T, D), F32), jax.ShapeDtypeStruct((T, D), F32)),
        in_specs=[tile(0), tile(1), tile(2), _halo_prev(0), _halo_prev(1),
                  pl.BlockSpec((KW, 8, D), lambda i: (0, 0, 0)), row, row, row,
                  pl.BlockSpec((None, D, D), lambda i: (W_CO, 0, 0))],
        out_specs=(pl.BlockSpec((TM, D), lambda i: (i, 0)), pl.BlockSpec((TM, D), lambda i: (i, 0))),
        scratch_shapes=[VMEM((8, EXT, D), F32), VMEM((TM, D), BF16)],
        compiler_params=_cp(("parallel",)),
    )(p, p, p, p, p, cw, conv_b, lg, lb, wq)


def _masked_ws(ws_ref, wt_ref):
    tri = lax.broadcasted_iota(jnp.int32, (CHUNK, CHUNK), 0) >= lax.broadcasted_iota(jnp.int32, (CHUNK, CHUNK), 1)
    for h in range(HEADS):
        wt_ref[h] = jnp.where(tri, ws_ref[h], 0.0).astype(BF16)


def _sgu_mix(wt_ref, vl_ref, bst_ref, s_ref):
    for ck in range(TM // CHUNK):
        r = slice(ck * CHUNK, (ck + 1) * CHUNK)
        for h in range(HEADS):
            cs = slice(h * HD, (h + 1) * HD)
            s_ref[r, cs] = jnp.dot(wt_ref[h], vl_ref[r, cs], preferred_element_type=F32) + bst_ref[:, h:h + 1]


def _branch_b_fwd(p, ws, bst, sg, sb, wq):
    T = p.shape[0]

    def body(pu_ref, pv_ref, pz_ref, ws_ref, bst_ref, sg_ref, sb_ref, w_ref, yb_ref,
             wt_ref, vl_ref, t_ref, s_ref, yp_ref):
        _masked_ws(ws_ref, wt_ref)

        def pre(rb, carry):
            rows = _rows(rb)
            vg, _ = _gelu_parts(pv_ref[rows, :])
            vn, _ = _ln_stats(vg)
            vl_ref[rows, :] = (vn * sg_ref[...] + sb_ref[...]).astype(BF16)
            u, _ = _gelu_parts(pu_ref[rows, :])
            z = pz_ref[rows, :]
            t_ref[rows, :] = u * (z * _sig(z))
            return carry
        _blocks(TM // RB, pre)
        _sgu_mix(wt_ref, vl_ref, bst_ref, s_ref)

        def post(rb, carry):
            rows = _rows(rb)
            yp_ref[rows, :] = (t_ref[rows, :] * s_ref[rows, :]).astype(BF16)
            return carry
        _blocks(TM // RB, post)
        yb_ref[...] = jnp.dot(yp_ref[...], w_ref[...], preferred_element_type=F32)

    tile = lambda seg: pl.BlockSpec((TM, D), lambda i: (i, seg))
    row = pl.BlockSpec((1, D), lambda i: (0, 0))
    return pl.pallas_call(
        body, name="branch_b_fwd", grid=(T // TM,),
        out_shape=jax.ShapeDtypeStruct((T, D), F32),
        in_specs=[tile(3), tile(4), tile(5), pl.BlockSpec((HEADS, CHUNK, CHUNK), lambda i: (0, 0, 0)),
                  pl.BlockSpec((CHUNK, HEADS), lambda i: (0, 0)), row, row,
                  pl.BlockSpec((None, D, D), lambda i: (W_SO, 0, 0))],
        out_specs=pl.BlockSpec((TM, D), lambda i: (i, 0)),
        scratch_shapes=[VMEM((HEADS, CHUNK, CHUNK), BF16), VMEM((TM, D), BF16), VMEM((TM, D), F32),
                        VMEM((TM, D), F32), VMEM((TM, D), BF16)],
        compiler_params=_cp(("parallel",)),
    )(p, p, p, ws, bst, sg, sb, wq)


def _merge_loss(p, ya, yb, x, tgt, mod, g_final, wq):
    T = x.shape[0]
    nt = T // TM

    def body(ga_ref, gb_ref, ya_ref, yb_ref, x_ref, t_ref, mod_ref, gf_ref, w_ref,
             dx2_ref, dya_ref, dyb_ref, dp_ref, gwo_ref, st_ref,
             mrg_s, out_s, dout_s, dm_s, gw_acc):
        i = pl.program_id(0)

        @pl.when(i == 0)
        def _():
            gw_acc[...] = jnp.zeros_like(gw_acc)
            st_ref[...] = jnp.zeros_like(st_ref)

        def merge(rb, carry):
            rows = _rows(rb)
            mrg_s[rows, :] = (_sig(ga_ref[rows, :]) * ya_ref[rows, :] + _sig(gb_ref[rows, :]) * yb_ref[rows, :]).astype(BF16)
            return carry
        lax.fori_loop(0, TM // RB, merge, 0)
        out_s[...] = jnp.dot(mrg_s[...], w_ref[...], preferred_element_type=F32)

        def head(rb, carry):
            loss, gg, dg = carry
            rows = _rows(rb)
            gate = mod_ref[2:3, :]
            gf = gf_ref[...]
            out = out_s[rows, :]
            x2 = x_ref[rows, :] + gate * out
            r2 = lax.rsqrt(jnp.mean(x2 * x2, axis=-1, keepdims=True) + EPS)
            x2n = x2 * r2
            diff = x2n * gf - t_ref[rows, :]
            dy = diff * (1.0 / D)
            dx2n = dy * gf
            dx2 = r2 * (dx2n - x2n * jnp.mean(dx2n * x2n, axis=-1, keepdims=True))
            dx2_ref[rows, :] = dx2
            dout_s[rows, :] = (dx2 * gate).astype(BF16)
            return loss + _fold8(diff * diff), gg + _fold8(dy * x2n), dg + _fold8(dx2 * out)
        zero = jnp.zeros((8, D), F32)
        loss, gg, dg = _blocks(TM // RB, head, (zero, zero, zero))
        st_ref[0] = st_ref[0] + loss * (0.5 / D)
        st_ref[1] = st_ref[1] + gg
        st_ref[2] = st_ref[2] + dg

        dm_s[...] = lax.dot_general(dout_s[...], w_ref[...], NT, preferred_element_type=F32)
        gw_acc[...] += lax.dot_general(mrg_s[...], dout_s[...], TN, preferred_element_type=F32)

        def split(rb, carry):
            rows = _rows(rb)
            dm = dm_s[rows, :]
            sa = _sig(ga_ref[rows, :])
            sb = _sig(gb_ref[rows, :])
            dya_ref[rows, :] = (dm * sa).astype(BF16)
            dyb_ref[rows, :] = (dm * sb).astype(BF16)
            dp_ref[0, rows, :] = (dm * ya_ref[rows, :] * (sa * (1.0 - sa))).astype(BF16)
            dp_ref[1, rows, :] = (dm * yb_ref[rows, :] * (sb * (1.0 - sb))).astype(BF16)
            return carry
        lax.fori_loop(0, TM // RB, split, 0)

        @pl.when(i == nt - 1)
        def _():
            gwo_ref[...] = gw_acc[...].astype(BF16)

    tile = pl.BlockSpec((TM, D), lambda i: (i, 0))
    return pl.pallas_call(
        body, name="merge_loss", grid=(nt,),
        out_shape=(jax.ShapeDtypeStruct((T, D), F32), jax.ShapeDtypeStruct((T, D), BF16), jax.ShapeDtypeStruct((T, D), BF16),
                   jax.ShapeDtypeStruct((NSEG, T, D), BF16), jax.ShapeDtypeStruct((D, D), BF16),
                   jax.ShapeDtypeStruct((3, 8, D), F32)),
        in_specs=[pl.BlockSpec((TM, D), lambda i: (i, 6)), pl.BlockSpec((TM, D), lambda i: (i, 7)), tile, tile, tile, tile,
                  pl.BlockSpec((3, D), lambda i: (0, 0)), pl.BlockSpec((1, D), lambda i: (0, 0)),
                  pl.BlockSpec((None, D, D), lambda i: (W_O, 0, 0))],
        out_specs=(tile, tile, tile, pl.BlockSpec((2, TM, D), lambda i: (3, i, 0)),
                   pl.BlockSpec((D, D), lambda i: (0, 0)), pl.BlockSpec((3, 8, D), lambda i: (0, 0, 0))),
        scratch_shapes=[VMEM((TM, D), BF16), VMEM((TM, D), F32), VMEM((TM, D), BF16), VMEM((TM, D), F32), VMEM((D, D), F32)],
        compiler_params=_cp(("arbitrary",)),
    )(p, p, ya, yb, x, tgt, mod, g_final, wq)


def _branch_b_bwd(dp, dyb, p, ws, bst, sg, sb, wq):
    T = p.shape[0]
    nt = T // TM

    def body(dp_in, dyb_ref, pu_ref, pv_ref, pz_ref, ws_ref, bst_ref, sg_ref, sb_ref, w_ref,
             dp_ref, gw_ref, gws_ref, gbs_ref, st_ref,
             wt_ref, d_s, vl_s, vn_s, gpv_s, rs_s, s_s, ds_s, ds32_s, yp_s, dvl_s, gw_acc):
        del dp_in
        i = pl.program_id(0)

        @pl.when(i == 0)
        def _():
            gw_acc[...] = jnp.zeros_like(gw_acc)
            gws_ref[...] = jnp.zeros_like(gws_ref)
            gbs_ref[...] = jnp.zeros_like(gbs_ref)
            st_ref[...] = jnp.zeros_like(st_ref)
            _masked_ws(ws_ref, wt_ref)

        d_s[...] = lax.dot_general(dyb_ref[...], w_ref[...], NT, preferred_element_type=F32)

        def pre(rb, carry):
            rows = _rows(rb)
            vg, gpv = _gelu_parts(pv_ref[rows, :])
            vn, rs = _ln_stats(vg)
            vl_s[rows, :] = (vn * sg_ref[...] + sb_ref[...]).astype(BF16)
            vn_s[rows, :] = vn
            gpv_s[rows, :] = gpv
            rs_s[rows, :] = rs
            return carry
        _blocks(TM // RB, pre)
        _sgu_mix(wt_ref, vl_s, bst_ref, s_s)

        def mid(rb, carry):
            rows = _rows(rb)
            u, gpu = _gelu_parts(pu_ref[rows, :])
            z = pz_ref[rows, :]
            sz = _sig(z)
            siluz = z * sz
            d = d_s[rows, :]
            s = s_s[rows, :]
            t = u * siluz
            yp_s[rows, :] = (t * s).astype(BF16)
            dp_ref[0, rows, :] = (d * s * siluz * gpu).astype(BF16)
            ds = d * t
            ds32_s[rows, :] = ds
            ds_s[rows, :] = ds.astype(BF16)
            dp_ref[2, rows, :] = (d * u * s * (sz * (1.0 + z * (1.0 - sz)))).astype(BF16)
            return carry
        lax.fori_loop(0, TM // RB, mid, 0)

        for ck in range(TM // CHUNK):
            r = slice(ck * CHUNK, (ck + 1) * CHUNK)
            for h in range(HEADS):
                cs = slice(h * HD, (h + 1) * HD)
                dsh = ds_s[r, cs]
                dvl_s[r, cs] = lax.dot_general(wt_ref[h], dsh, TN, preferred_element_type=F32)
                gws_ref[h] += lax.dot_general(dsh, vl_s[r, cs], NT, preferred_element_type=F32)
                gbs_ref[h] += ds32_s[r, cs]

        def post(rb, carry):
            g_sg, g_sb = carry
            rows = _rows(rb)
            dvl = dvl_s[rows, :]
            vn = vn_s[rows, :]
            dvg = _ln_bwd(dvl * sg_ref[...], vn, rs_s[rows, :])
            dp_ref[1, rows, :] = (dvg * gpv_s[rows, :]).astype(BF16)
            return g_sg + _fold8(dvl * vn), g_sb + _fold8(dvl)
        zero = jnp.zeros((8, D), F32)
        g_sg, g_sb = _blocks(TM // RB, post, (zero, zero))
        st_ref[0] = st_ref[0] + g_sg
        st_ref[1] = st_ref[1] + g_sb

        gw_acc[...] += lax.dot_general(yp_s[...], dyb_ref[...], TN, preferred_element_type=F32)

        @pl.when(i == nt - 1)
        def _():
            gw_ref[...] = gw_acc[...].astype(BF16)

    tile = lambda seg: pl.BlockSpec((TM, D), lambda i: (i, seg))
    row = pl.BlockSpec((1, D), lambda i: (0, 0))
    hh = pl.BlockSpec((HEADS, CHUNK, CHUNK), lambda i: (0, 0, 0))
    return pl.pallas_call(
        body, name="branch_b_bwd", grid=(nt,),
        out_shape=(jax.ShapeDtypeStruct((NSEG, T, D), BF16), jax.ShapeDtypeStruct((D, D), BF16),
                   jax.ShapeDtypeStruct((HEADS, CHUNK, CHUNK), F32), jax.ShapeDtypeStruct((HEADS, CHUNK, HD), F32),
                   jax.ShapeDtypeStruct((2, 8, D), F32)),
        in_specs=[pl.BlockSpec(memory_space=ANY), pl.BlockSpec((TM, D), lambda i: (i, 0)), tile(3), tile(4), tile(5),
                  hh, pl.BlockSpec((CHUNK, HEADS), lambda i: (0, 0)), row, row,
                  pl.BlockSpec((None, D, D), lambda i: (W_SO, 0, 0))],
        out_specs=(pl.BlockSpec((3, TM, D), lambda i: (1, i, 0)), pl.BlockSpec((D, D), lambda i: (0, 0)), hh, hh,
                   pl.BlockSpec((2, 8, D), lambda i: (0, 0, 0))),
        scratch_shapes=[VMEM((HEADS, CHUNK, CHUNK), BF16), VMEM((TM, D), F32), VMEM((TM, D), BF16), VMEM((TM, D), F32),
                        VMEM((TM, D), F32), VMEM((TM, 1), F32), VMEM((TM, D), F32), VMEM((TM, D), BF16), VMEM((TM, D), F32),
                        VMEM((TM, D), BF16), VMEM((TM, D), F32), VMEM((D, D), F32)],
        input_output_aliases={0: 0},
        compiler_params=_cp(("arbitrary",)),
    )(dp, dyb, p, p, p, ws, bst, sg, sb, wq)


def _branch_a_bwd1(dp, dya, yc, p, lg, lb, wq):
    T = p.shape[0]
    nt = T // TM

    def body(dp_in, dya_ref, yc_ref, z_ref, lg_ref, lb_ref, w_ref, dp_ref, dyc_ref, gw_ref, st_ref,
             d_s, yp_s, gw_acc):
        del dp_in
        i = pl.program_id(0)

        @pl.when(i == 0)
        def _():
            gw_acc[...] = jnp.zeros_like(gw_acc)
            st_ref[...] = jnp.zeros_like(st_ref)

        d_s[...] = lax.dot_general(dya_ref[...], w_ref[...], NT, preferred_element_type=F32)

        def blk(rb, carry):
            g_lg, g_lb, g_cb = carry
            rows = _rows(rb)
            n, rs = _ln_stats(yc_ref[rows, :])
            l = n * lg_ref[...] + lb_ref[...]
            sgl = _sig(l)
            sl = l * sgl
            z = z_ref[rows, :]
            sz = _sig(z)
            siluz = z * sz
            d = d_s[rows, :]
            yp_s[rows, :] = (sl * siluz).astype(BF16)
            dp_ref[rows, :] = (d * sl * (sz * (1.0 + z * (1.0 - sz)))).astype(BF16)
            dl = (d * siluz) * (sgl * (1.0 + l * (1.0 - sgl)))
            dyc = _ln_bwd(dl * lg_ref[...], n, rs)
            dyc_ref[rows, :] = dyc
            return g_lg + _fold8(dl * n), g_lb + _fold8(dl), g_cb + _fold8(dyc)
        zero = jnp.zeros((8, D), F32)
        g_lg, g_lb, g_cb = _blocks(TM // RB, blk, (zero, zero, zero))
        st_ref[0] = st_ref[0] + g_lg
        st_ref[1] = st_ref[1] + g_lb
        st_ref[2] = st_ref[2] + g_cb

        gw_acc[...] += lax.dot_general(yp_s[...], dya_ref[...], TN, preferred_element_type=F32)

        @pl.when(i == nt - 1)
        def _():
            gw_ref[...] = gw_acc[...].astype(BF16)

    tile = pl.BlockSpec((TM, D), lambda i: (i, 0))
    row = pl.BlockSpec((1, D), lambda i: (0, 0))
    return pl.pallas_call(
        body, name="branch_a_bwd1", grid=(nt,),
        out_shape=(jax.ShapeDtypeStruct((NSEG, T, D), BF16), jax.ShapeDtypeStruct((T, D), F32),
                   jax.ShapeDtypeStruct((D, D), BF16), jax.ShapeDtypeStruct((3, 8, D), F32)),
        in_specs=[pl.BlockSpec(memory_space=ANY), tile, tile, pl.BlockSpec((TM, D), lambda i: (i, 2)), row, row,
                  pl.BlockSpec((None, D, D), lambda i: (0, 0, 0))],
        out_specs=(pl.BlockSpec((None, TM, D), lambda i: (2, i, 0)), tile, pl.BlockSpec((D, D), lambda i: (0, 0)),
                   pl.BlockSpec((3, 8, D), lambda i: (0, 0, 0))),
        scratch_shapes=[VMEM((TM, D), F32), VMEM((TM, D), BF16), VMEM((D, D), F32)],
        input_output_aliases={0: 0},
        compiler_params=_cp(("arbitrary",)),
    )(dp, dya, yc, p, lg, lb, wq)


def _branch_a_bwd2(dp, dyc, p, cw, gw_co, gw_so, gw_o, early):
    T = p.shape[0]
    nt = T // TM
    hb = TM // HALO

    def body(dp_in, dyc_ref, hdyc_ref, val_ref, glu_ref, cw_ref, gco_ref, gso_ref, go_ref, e_ref,
             dp_ref, gcw_ref, rq_ref, eg_ref, a_s, sd_ref, da_s, x_send, x_recv, x_loc, ag_send, ag_recv, ag_loc):
        del dp_in
        i = pl.program_id(0)
        start_exchange, wait_exchange = _owner_exchange((gco_ref, gso_ref, go_ref), rq_ref, x_send, x_recv, x_loc)
        ag_start, ag_relay, ag_finish = _allgather_phases([e_ref], lambda a, slot: eg_ref.at[slot],
                                                          ag_send, ag_recv, ag_loc)

        @pl.when(i == 0)
        def _():
            gcw_ref[...] = jnp.zeros_like(gcw_ref)
            ag_start()
            start_exchange()

        @pl.when(i == nt // 2)
        def _():
            ag_relay()

        def fill(rb, carry):
            rows = _rows(rb)
            a_s[rows, :] = val_ref[rows, :] * _sig(glu_ref[rows, :])
            return carry
        lax.fori_loop(0, TM // RB, fill, 0)
        sd_ref[0, 0:TM, :] = dyc_ref[...]
        sd_ref[0, TM:EXT, :] = jnp.where(i < nt - 1, hdyc_ref[...], 0.0)
        _shift_copies(sd_ref)

        def conv_t(rb, carry):
            r0 = rb * RBC
            accs = [jnp.zeros((8, D), F32)] * (RBC // 8)
            for k in range(KW):
                o = KW - 1 - k
                w = cw_ref[k]
                accs = [acc + w * sd_ref[o % 8, pl.ds(pl.multiple_of(r0 + 8 * (o // 8 + g), 8), 8), :]
                        for g, acc in enumerate(accs)]
            for g, acc in enumerate(accs):
                da_s[pl.ds(pl.multiple_of(r0 + 8 * g, 8), 8), :] = acc
            return carry
        lax.fori_loop(0, TM // RBC, conv_t, 0)

        for k0 in range(0, KW, TAPG):
            taps = list(range(k0, min(k0 + TAPG, KW)))

            def tap_group(rb, accs, taps=taps):
                for u in range(2):
                    r0 = rb * 32 + 16 * u
                    a = a_s[pl.ds(pl.multiple_of(r0, 16), 16), :]
                    out = []
                    for k, acc in zip(taps, accs):
                        o = KW - 1 - k
                        d = sd_ref[o % 8, pl.ds(pl.multiple_of(r0 + 8 * (o // 8), 8), 16), :]
                        out.append(acc + _fold8(a * d))
                    accs = tuple(out)
                return accs
            sums = lax.fori_loop(0, TM // 32, tap_group, tuple(jnp.zeros((8, D), F32) for _ in taps))
            for k, s in zip(taps, sums):
                gcw_ref[k] = gcw_ref[k] + s

        def glu_b(rb, carry):
            rows = _rows(rb)
            da = da_s[rows, :]
            sg = _sig(glu_ref[rows, :])
            dp_ref[0, rows, :] = (da * sg).astype(BF16)
            dp_ref[1, rows, :] = (da * val_ref[rows, :] * (sg * (1.0 - sg))).astype(BF16)
            return carry
        lax.fori_loop(0, TM // RB, glu_b, 0)

        @pl.when(i == nt - 1)
        def _():
            wait_exchange()
            ag_finish()

    tile = lambda seg: pl.BlockSpec((TM, D), lambda i: (i, seg))
    hbm = pl.BlockSpec(memory_space=ANY)
    return pl.pallas_call(
        body, name="branch_a_bwd2", grid=(nt,),
        out_shape=(jax.ShapeDtypeStruct((NSEG, T, D), BF16), jax.ShapeDtypeStruct((32, 8, D), F32),
                   jax.ShapeDtypeStruct((NDEV, 3, 128, D), BF16), jax.ShapeDtypeStruct((NDEV, NEARLY, D), F32)),
        in_specs=[hbm, pl.BlockSpec((TM, D), lambda i: (i, 0)),
                  pl.BlockSpec((HALO, D), lambda i: (jnp.minimum((i + 1) * hb, nt * hb - 1), 0)),
                  tile(0), tile(1), pl.BlockSpec((KW, 8, D), lambda i: (0, 0, 0)),
                  hbm, hbm, hbm, hbm],
        out_specs=(pl.BlockSpec((2, TM, D), lambda i: (0, i, 0)), pl.BlockSpec((32, 8, D), lambda i: (0, 0, 0)), hbm, hbm),
        scratch_shapes=[VMEM((TM, D), F32), VMEM((8, EXT, D), F32), VMEM((TM, D), F32),
                        pltpu.SemaphoreType.DMA((3, 7)), pltpu.SemaphoreType.DMA((3, 7)), pltpu.SemaphoreType.DMA((3,)),
                        pltpu.SemaphoreType.DMA((1, 7)), pltpu.SemaphoreType.DMA((1, 7)), pltpu.SemaphoreType.DMA((1,))],
        input_output_aliases={0: 0},
        compiler_params=_cp(("arbitrary",)),
    )(dp, dyc, dyc, p, p, cw, gw_co, gw_so, gw_o, early)


def _in_bwd_dx(dp, wing, x, dx2, mod, g_pre):
    T = x.shape[0]
    nt = T // TM

    def body(dp_ref, w_ref, x_ref, dx2_ref, mod_ref, g_ref, gx_ref, st_ref, dh_s):
        i = pl.program_id(0)

        @pl.when(i == 0)
        def _():
            st_ref[...] = jnp.zeros_like(st_ref)

        dh_s[...] = lax.dot_general(dp_ref[0], w_ref[0], NT, preferred_element_type=F32)
        for j in range(1, NSEG):
            dh_s[...] += lax.dot_general(dp_ref[j], w_ref[j], NT, preferred_element_type=F32)

        def blk(rb, carry):
            d_sh, d_sc, g_g = carry
            rows = _rows(rb)
            xv = x_ref[rows, :]
            r = lax.rsqrt(jnp.mean(xv * xv, axis=-1, keepdims=True) + EPS)
            xn = xv * r
            g = g_ref[...]
            hpre = xn * g
            dh = dh_s[rows, :]
            dhp = dh * (1.0 + mod_ref[1:2, :])
            dxn = dhp * g
            gx_ref[rows, :] = dx2_ref[rows, :] + r * (dxn - xn * jnp.mean(dxn * xn, axis=-1, keepdims=True))
            return d_sh + _fold8(dh), d_sc + _fold8(dh * hpre), g_g + _fold8(dhp * xn)
        zero = jnp.zeros((8, D), F32)
        d_sh, d_sc, g_g = _blocks(TM // RB, blk, (zero, zero, zero))
        st_ref[0] = st_ref[0] + d_sh
        st_ref[1] = st_ref[1] + d_sc
        st_ref[2] = st_ref[2] + g_g

    tile = pl.BlockSpec((TM, D), lambda i: (i, 0))
    return pl.pallas_call(
        body, name="in_bwd_dx", grid=(nt,),
        out_shape=(jax.ShapeDtypeStruct((T, D), F32), jax.ShapeDtypeStruct((3, 8, D), F32)),
        in_specs=[pl.BlockSpec((NSEG, TM, D), lambda i: (0, i, 0)),
                  pl.BlockSpec((NSEG, D, D), lambda i: (0, 0, 0), pipeline_mode=pl.Buffered(1)),
                  tile, tile, pl.BlockSpec((3, D), lambda i: (0, 0)), pl.BlockSpec((1, D), lambda i: (0, 0))],
        out_specs=(tile, pl.BlockSpec((3, 8, D), lambda i: (0, 0, 0))),
        scratch_shapes=[VMEM((TM, D), F32)],
        compiler_params=_cp(("arbitrary",), vmem_mb=56),
    )(dp, wing, x, dx2, mod, g_pre)


RS_ORDER = (5, 4, 3, 2, 7, 6, 1, 0)
RS_HALF = D // 2


def _in_bwd_dw(ht, dp, order, smalls):
    T = ht.shape[1]
    tmw = min(4 * TMB, T)
    nb = T // tmw
    nu = 2 * NSEG

    def body(order_ref, ht_ref, dp_ref, sm_ref, g_ref, smg_ref, acc, sbuf, rsib, rici, send_sems, recv_sems,
             ag_send, ag_recv, ag_loc):
        j = pl.program_id(0)
        i = pl.program_id(1)
        x, y, c = _me()
        sibling = (x, y, 1 - c)
        chips = [(1 - x, y), (x, 1 - y), (1 - x, 1 - y)]
        ag_start, ag_relay, ag_finish = _allgather_phases([sm_ref], lambda a, slot: smg_ref.at[slot],
                                                          ag_send, ag_recv, ag_loc)

        def to_sibling(s):
            return pltpu.make_async_remote_copy(src_ref=sbuf.at[s], dst_ref=rsib.at[s], send_sem=send_sems.at[s],
                                                recv_sem=recv_sems.at[s], device_id=sibling, device_id_type=MESH)

        def to_owner(t):
            return pltpu.make_async_remote_copy(src_ref=sbuf.at[8 + t], dst_ref=rici.at[t], send_sem=send_sems.at[8 + t],
                                                recv_sem=recv_sems.at[8 + t], device_id=(*chips[t // 2], c),
                                                device_id_type=MESH)

        @pl.when((j == 0) & (i == 0))
        def _():
            ag_start()

        @pl.when((j == 4) & (i == 0))
        def _():
            ag_relay()

        @pl.when(i == 0)
        def _():
            acc[...] = jnp.zeros_like(acc)
        acc[...] += jnp.dot(ht_ref[...], dp_ref[...], preferred_element_type=F32)

        for jj in range(nu):
            chip, half, mine = jj // 4, (jj % 4) // 2, jj % 2
            s = 2 * chip + half

            @pl.when((j == jj) & (i == nb - 1))
            def _(chip=chip, half=half, mine=mine, s=s):
                if not mine:
                    sbuf[s] = acc[...].astype(BF16)
                    to_sibling(s).start()
                elif chip < 3:
                    to_sibling(s).wait_recv()
                    sbuf[8 + s] = (acc[...] + rsib[s].astype(F32)).astype(BF16)
                    to_owner(s).start()
                else:
                    to_sibling(s).wait_recv()
                    g = acc[...] + rsib[s].astype(F32)
                    for q in range(3):
                        to_owner(2 * q + half).wait_recv()
                        g = g + rici[2 * q + half].astype(F32)
                    g_ref[:, half * RS_HALF:(half + 1) * RS_HALF] = g

        @pl.when((j == nu - 1) & (i == nb - 1))
        def _():
            for s in range(8):
                to_sibling(s).wait_send()
            for t in range(6):
                to_owner(t).wait_send()
            ag_finish()

    hbm = pl.BlockSpec(memory_space=ANY)
    return pl.pallas_call(
        body, name="in_bwd_dw",
        out_shape=(jax.ShapeDtypeStruct((D, D), F32), jax.ShapeDtypeStruct((NDEV, NLATE, D), F32)),
        grid_spec=pltpu.PrefetchScalarGridSpec(
            num_scalar_prefetch=1, grid=(nu, nb),
            in_specs=[pl.BlockSpec((D, tmw), lambda j, i, o: (0, i)),
                      pl.BlockSpec((None, tmw, RS_HALF), lambda j, i, o: (o[2 * (j // 4) + j % 2], i, (j % 4) // 2)),
                      hbm],
            out_specs=(pl.BlockSpec((D, D), lambda j, i, o: (0, 0)), hbm),
            scratch_shapes=[VMEM((D, RS_HALF), F32), VMEM((14, D, RS_HALF), BF16), VMEM((8, D, RS_HALF), BF16),
                            VMEM((6, D, RS_HALF), BF16), pltpu.SemaphoreType.DMA((14,)), pltpu.SemaphoreType.DMA((14,)),
                            pltpu.SemaphoreType.DMA((1, 7)), pltpu.SemaphoreType.DMA((1, 7)), pltpu.SemaphoreType.DMA((1,))]),
        compiler_params=_cp(("arbitrary", "arbitrary"), vmem_mb=56),
    )(order, ht, dp, smalls)


def _pack_early(st_a, st_b, st_m, gws, gbs):
    def body(sa_ref, sb_ref, sm_ref, gws_ref, gbs_ref, out_ref):
        out_ref[...] = jnp.zeros_like(out_ref)
        fold = lambda v: jnp.sum(v, axis=0, keepdims=True)
        out_ref[0:1, :] = fold(sm_ref[2])
        out_ref[1:2, :] = fold(sa_ref[2])
        out_ref[2:3, :] = fold(sa_ref[0])
        out_ref[3:4, :] = fold(sa_ref[1])
        out_ref[4:5, :] = fold(sb_ref[0])
        out_ref[5:6, :] = fold(sb_ref[1])
        out_ref[7:8, :] = fold(sm_ref[1])
        out_ref[8:9, :] = fold(sm_ref[0])
        tri = lax.broadcasted_iota(jnp.int32, (CHUNK, CHUNK), 0) >= lax.broadcasted_iota(jnp.int32, (CHUNK, CHUNK), 1)
        for h in range(HEADS):
            out_ref[16:16 + CHUNK, h * CHUNK:(h + 1) * CHUNK] = jnp.where(tri, gws_ref[h], 0.0)
            out_ref[6:7, h * CHUNK:(h + 1) * CHUNK] = fold(gbs_ref[h].T)

    vm = pl.BlockSpec(memory_space=VMEM)
    return pl.pallas_call(
        body, name="pack_early", out_shape=jax.ShapeDtypeStruct((NEARLY, D), F32),
        in_specs=[vm] * 5, out_specs=vm, compiler_params=_cp(vmem_mb=32),
    )(st_a, st_b, st_m, gws, gbs)


def _pack_late(st_x, gcw):
    def body(sx_ref, gcw_ref, out_ref):
        out_ref[...] = jnp.zeros_like(out_ref)
        fold = lambda v: jnp.sum(v, axis=0, keepdims=True)
        for r in range(3):
            out_ref[r:r + 1, :] = fold(sx_ref[r])
        for k in range(KW):
            out_ref[8 + k:9 + k, :] = fold(gcw_ref[k])

    vm = pl.BlockSpec(memory_space=VMEM)
    return pl.pallas_call(
        body, name="pack_late", out_shape=jax.ShapeDtypeStruct((NLATE, D), F32),
        in_specs=[vm] * 2, out_specs=vm, compiler_params=_cp(vmem_mb=32),
    )(st_x, gcw)


def _pack_params(name, b_ada, g_pre, conv_b, lg, lb, sg, sb, b_sgu, g_final, w_sgu_t):
    def body(ba_ref, *refs):
        rows, ws_ref, out_ref = refs[:8], refs[8], refs[9]
        out_ref[...] = jnp.zeros_like(out_ref)
        for r in range(3):
            out_ref[r:r + 1, :] = ba_ref[:, r * D:(r + 1) * D]
        for r, ref in enumerate(rows):
            out_ref[3 + r:4 + r, :] = ref[...]
        out_ref[ROW_WS:ROW_WS + CHUNK, :] = ws_ref[...]

    vm = pl.BlockSpec(memory_space=VMEM)
    return pl.pallas_call(
        body, name=name, out_shape=jax.ShapeDtypeStruct((NSMALL, D), F32),
        in_specs=[vm] * 10, out_specs=vm, compiler_params=_cp(vmem_mb=32),
    )(b_ada, g_pre, conv_b, lg, lb, sg, sb, b_sgu, g_final, w_sgu_t)


def _adam(w, g, m, v):
    m2 = ADAM_B1 * m + (1.0 - ADAM_B1) * g
    v2 = ADAM_B2 * v + (1.0 - ADAM_B2) * (g * g)
    m_hat = m2 / (1.0 - ADAM_B1 ** ADAM_STEP)
    v_hat = v2 / (1.0 - ADAM_B2 ** ADAM_STEP)
    delta = -ADAM_LR * (m_hat / (jnp.sqrt(v_hat) + ADAM_EPS) + ADAM_WD * w)
    return delta, m2, v2


def _small_finish(eg, lg, w, m, v):
    def body(eg_ref, lg_ref, w_ref, m_ref, v_ref, loss_ref, g_ref, d_ref, m2_ref, v2_ref):
        e = eg_ref[0]
        l = lg_ref[0]
        for s in range(1, NDEV):
            e = e + eg_ref[s]
            l = l + lg_ref[s]
        g_ref[...] = jnp.zeros_like(g_ref)
        g_ref[0:2, :] = l[0:2, :]
        g_ref[2:3, :] = e[0:1, :]
        g_ref[3:4, :] = l[2:3, :]
        g_ref[4:12, :] = e[1:9, :]
        g_ref[ROW_CW:ROW_CW + KW, :] = l[8:8 + KW, :]
        g_ref[ROW_WS:ROW_WS + CHUNK, :] = e[16:16 + CHUNK, :]
        loss_ref[...] = jnp.sum(e[8:9, :], axis=1, keepdims=True)
        d_ref[...], m2_ref[...], v2_ref[...] = _adam(w_ref[...], g_ref[...], m_ref[...], v_ref[...])

    vm = pl.BlockSpec(memory_space=VMEM)
    sd = jax.ShapeDtypeStruct((NSMALL, D), F32)
    return pl.pallas_call(
        body, name="small_finish", out_shape=(jax.ShapeDtypeStruct((1, 1), F32), sd, sd, sd, sd),
        in_specs=[vm] * 5, out_specs=(vm,) * 5, compiler_params=_cp(vmem_mb=32),
    )(eg, lg, w, m, v)


def _ada_grad_adam(ct, dm, w, m, v):
    def body(ct_ref, dm_ref, w_ref, m_ref, v_ref, g_ref, d_ref, m2_ref, v2_ref):
        g = ct_ref[:, 0:1] * dm_ref[0:1, :]
        for b in range(1, NDEV):
            g = g + ct_ref[:, b:b + 1] * dm_ref[b:b + 1, :]
        g_ref[...] = g
        d_ref[...], m2_ref[...], v2_ref[...] = _adam(w_ref[...], g, m_ref[...], v_ref[...])

    vm = pl.BlockSpec(memory_space=VMEM)
    sd = jax.ShapeDtypeStruct(w.shape, F32)
    return pl.pallas_call(
        body, name="ada_grad_adam", out_shape=(sd, sd, sd, sd),
        in_specs=[vm] * 5, out_specs=(vm,) * 4, compiler_params=_cp(vmem_mb=32),
    )(ct, dm, w, m, v)


def _adam_f32(name, g, w, m, v, rows=None):
    R, C = w.shape
    rows = R if rows is None else rows

    def body(g_ref, w_ref, m_ref, v_ref, d_ref, m2_ref, v2_ref):
        d_ref[...], m2_ref[...], v2_ref[...] = _adam(w_ref[...], g_ref[...], m_ref[...], v_ref[...])

    tile = pl.BlockSpec((rows, C), lambda i: (i, 0))
    sd = jax.ShapeDtypeStruct(w.shape, F32)
    return pl.pallas_call(
        body, name=name, grid=(R // rows,), out_shape=(sd, sd, sd), in_specs=[tile] * 4, out_specs=(tile,) * 3,
        compiler_params=_cp(("parallel",), vmem_mb=32),
    )(g, w, m, v)


def _adam_reduce(name, recv, w, m, v, rows, recv_spec):
    R, C = w.shape

    def body(r_ref, w_ref, m_ref, v_ref, g_ref, d_ref, m2_ref, v2_ref):
        g = r_ref[0].astype(F32)
        for s in range(1, NDEV):
            g = g + r_ref[s].astype(F32)
        g_ref[...] = g
        d_ref[...], m2_ref[...], v2_ref[...] = _adam(w_ref[...], g, m_ref[...], v_ref[...])

    tile = pl.BlockSpec((rows, C), lambda i: (i, 0))
    sd = jax.ShapeDtypeStruct((R, C), F32)
    return pl.pallas_call(
        body, name=name, grid=(R // rows,), out_shape=(sd, sd, sd, sd),
        in_specs=[recv_spec, tile, tile, tile], out_specs=(tile,) * 4,
        compiler_params=_cp(("parallel",)),
    )(recv, w, m, v)


def kernel(x, c, w_ada, b_ada, g_pre, w_in, conv_w, conv_b, conv_ln_g, conv_ln_b, w_conv_out, sgu_ln_g, sgu_ln_b, w_sgu, b_sgu, w_sgu_out, w_o, g_final, loss_target, m_w_ada, m_b_ada, m_g_pre, m_w_in, m_conv_w, m_conv_b, m_conv_ln_g, m_conv_ln_b, m_w_conv_out, m_sgu_ln_g, m_sgu_ln_b, m_w_sgu, m_b_sgu, m_w_sgu_out, m_w_o, m_g_final, v_w_ada, v_b_ada, v_g_pre, v_w_in, v_conv_w, v_conv_b, v_conv_ln_g, v_conv_ln_b, v_w_conv_out, v_sgu_ln_g, v_sgu_ln_b, v_w_sgu, v_b_sgu, v_w_sgu_out, v_w_o, v_g_final):
    T = x.shape[1]
    assert T % TMB == 0 and x.shape[2] == D
    xs = x[0]
    tgt = loss_target[0]
    my = 4 * lax.axis_index("x") + 2 * lax.axis_index("y") + lax.axis_index("c")

    mod, cg = _ada_fwd(c, w_ada[0], b_ada)
    gf = g_final.reshape(1, D)
    bst = b_sgu[0].T

    h, ht = _prep_h(xs, mod, g_pre)
    p, wing, wqg, cwg = _in_proj(h, w_in[0], my ^ jnp.array(AG_ORDER, jnp.int32),
                                 w_conv_out[0], w_sgu_out[0], w_o[0], conv_w[0])
    wq = wqg.reshape(3, D, D)
    cw = jnp.broadcast_to(jnp.transpose(cwg, (1, 0, 2)).reshape(KW, 1, D), (KW, 8, D))
    yc, ya = _branch_a_fwd(p, cw, conv_b, conv_ln_g, conv_ln_b, wq)
    yb = _branch_b_fwd(p, w_sgu[0], bst, sgu_ln_g, sgu_ln_b, wq)
    dx2, dya, dyb, dp, gw_o, st_m = _merge_loss(p, ya, yb, xs, tgt, mod, gf, wq)

    dp, gw_so, gws, gbs, st_b = _branch_b_bwd(dp, dyb, p, w_sgu[0], bst, sgu_ln_g, sgu_ln_b, wq)
    dp, dyc, gw_co, st_a = _branch_a_bwd1(dp, dya, yc, p, conv_ln_g, conv_ln_b, wq)
    early = _pack_early(st_a, st_b, st_m, gws, gbs)
    dp, gcw, rq, eg = _branch_a_bwd2(dp, dyc, p, cw, gw_co, gw_so, gw_o, early)
    grad_x, st_x = _in_bwd_dx(dp, wing, xs, dx2, mod, g_pre)
    late = _pack_late(st_x, gcw)
    g_in, lg = _in_bwd_dw(ht, dp, my ^ jnp.array(RS_ORDER, jnp.int32), late)

    def pack(name, b_ada_, g_pre_, conv_b_, lg_, lb_, sg_, sb_, b_sgu_, gfin_, w_sgu_):
        return _pack_params(name, b_ada_, g_pre_, conv_b_, lg_, lb_, sg_, sb_, b_sgu_.reshape(1, D), gfin_.reshape(1, D),
                            jnp.transpose(w_sgu_[0], (1, 0, 2)).reshape(CHUNK, D))

    pw = pack("pack_w", b_ada, g_pre, conv_b, conv_ln_g, conv_ln_b, sgu_ln_g, sgu_ln_b, b_sgu, g_final, w_sgu)
    pm = pack("pack_m", m_b_ada, m_g_pre, m_conv_b, m_conv_ln_g, m_conv_ln_b, m_sgu_ln_g, m_sgu_ln_b, m_b_sgu, m_g_final,
              m_w_sgu)
    pv = pack("pack_v", v_b_ada, v_g_pre, v_conv_b, v_conv_ln_g, v_conv_ln_b, v_sgu_ln_g, v_sgu_ln_b, v_b_sgu, v_g_final,
              v_w_sgu)
    loss11, sg_, sd_, sm_, sv_ = _small_finish(eg, lg, pw, pm, pv)

    def unpack(a):
        return dict(
            b_ada=a[0:3].reshape(1, 3 * D), g_pre=a[3:4], conv_b=a[4:5], conv_ln_g=a[5:6], conv_ln_b=a[6:7],
            sgu_ln_g=a[7:8], sgu_ln_b=a[8:9], b_sgu=a[9:10].reshape(1, HEADS, CHUNK), g_final=a[10],
            w_sgu=jnp.transpose(a[ROW_WS:ROW_WS + CHUNK].reshape(CHUNK, HEADS, CHUNK), (1, 0, 2))[None])
    small = [unpack(a) for a in (sg_, sd_, sm_, sv_)]

    g_cw = lax.dynamic_slice_in_dim(sg_[ROW_CW:ROW_CW + KW], my * 128, 128, axis=1)
    d_cw, m_cw, v_cw = _adam_f32("adam_conv_w", g_cw, conv_w[0], m_conv_w[0], v_conv_w[0])

    wcols = w_ada.shape[2]
    dm_all = jnp.concatenate([lg[:, 0], lg[:, 1], eg[:, 0]], axis=1)
    dm_mine = lax.dynamic_slice_in_dim(dm_all, my * wcols, wcols, axis=1)
    g_ada, d_ada, m_ada, v_ada = _ada_grad_adam(cg.T, dm_mine, w_ada[0], m_w_ada[0], v_w_ada[0])

    d_in, m_in, v_in = _adam_f32("adam_w_in", g_in, w_in[0], m_w_in[0], v_w_in[0], rows=256)
    big = {}
    for j, (nm, w_, m_, v_) in enumerate((("w_conv_out", w_conv_out, m_w_conv_out, v_w_conv_out),
                                          ("w_sgu_out", w_sgu_out, m_w_sgu_out, v_w_sgu_out),
                                          ("w_o", w_o, m_w_o, v_w_o))):
        big[nm] = _adam_reduce("adam_" + nm, rq, w_[0], m_[0], v_[0], 128,
                               pl.BlockSpec((NDEV, None, 128, D), lambda i, j=j: (0, j, 0, 0)))

    per = {
        "w_ada": tuple(a[None] for a in (g_ada, d_ada, m_ada, v_ada)),
        "w_in": tuple(a[None] for a in (g_in, d_in, m_in, v_in)),
        "conv_w": tuple(a[None] for a in (g_cw, d_cw, m_cw, v_cw)),
    }
    for nm in ("w_conv_out", "w_sgu_out", "w_o"):
        per[nm] = tuple(a[None] for a in big[nm])
    for nm in ("b_ada", "g_pre", "conv_b", "conv_ln_g", "conv_ln_b", "sgu_ln_g", "sgu_ln_b", "w_sgu", "b_sgu", "g_final"):
        per[nm] = tuple(s[nm] for s in small)

    order = ["w_ada", "b_ada", "g_pre", "w_in", "conv_w", "conv_b", "conv_ln_g", "conv_ln_b", "w_conv_out",
             "sgu_ln_g", "sgu_ln_b", "w_sgu", "b_sgu", "w_sgu_out", "w_o", "g_final"]
    outs = [loss11.reshape(()), grad_x[None]]
    for part in range(4):
        outs += [per[nm][part] for nm in order]
    return tuple(outs)
```

```python
import jax
import jax.numpy as jnp
from jax import lax
from jax.experimental import pallas as pl
from jax.experimental.pallas import tpu as pltpu

F32 = jnp.float32
BF16 = jnp.bfloat16
MESH = pl.DeviceIdType.MESH
VMEM = pltpu.VMEM
ANY = pl.ANY

D = 1024
NDEV = 8
NSEG = 8
HEADS = 8
HD = 128
CHUNK = 128
KW = 31
HALO = 32
EPS = 1e-6
TM = 256
TMB = 512
RB = 32
UNROLL = 4
RBC = 32
TAPG = 4
EXT = TM + HALO
NSMALL = 176
NEARLY = 144
NLATE = 40
ROW_CW = 16
ROW_WS = 48
W_CO, W_SO, W_O = 0, 1, 2

ADAM_LR = 0.001
ADAM_B1 = 0.9
ADAM_B2 = 0.999
ADAM_EPS = 1e-08
ADAM_WD = 0.01
ADAM_STEP = 10

INV_SQRT2 = 0.7071067811865476
INV_SQRT_2PI = 0.3989422804014327

NT = (((1,), (1,)), ((), ()))
TN = (((0,), (0,)), ((), ()))


def _cp(sem=None, vmem_mb=48):
    return pltpu.CompilerParams(dimension_semantics=sem, vmem_limit_bytes=vmem_mb * 1024 * 1024)


def _me():
    return lax.axis_index("x"), lax.axis_index("y"), lax.axis_index("c")


def _slot(px, py, pc):
    return 4 * px + 2 * py + pc


def _xor_peer(k):
    x, y, c = _me()
    return (1 - x if k & 4 else x, 1 - y if k & 2 else y, 1 - c if k & 1 else c)


def _allgather_phases(srcs, dst_at, send_sems, recv_sems, loc_sems):
    x, y, c = _me()
    me = (x, y, c)
    sibling = (x, y, 1 - c)
    chips = [(1 - x, y), (x, 1 - y), (1 - x, 1 - y)]
    na = len(srcs)

    def copy(a, k, block, to, src=None):
        d = dst_at(a, _slot(*block))
        return pltpu.make_async_remote_copy(
            src_ref=d if src is None else src, dst_ref=d,
            send_sem=send_sems.at[a, k], recv_sem=recv_sems.at[a, k],
            device_id=to, device_id_type=MESH)

    local = [pltpu.make_async_copy(srcs[a], dst_at(a, _slot(*me)), loc_sems.at[a]) for a in range(na)]
    first = []
    for a in range(na):
        first.append(copy(a, 0, me, sibling, src=srcs[a]))
        for j, chip in enumerate(chips):
            first.append(copy(a, 1 + j, me, (*chip, c), src=srcs[a]))
    passed = [copy(a, 4 + j, (*chip, c), sibling) for j, chip in enumerate(chips) for a in range(na)]

    def start():
        for cp in local + first:
            cp.start()

    def relay():
        for j, chip in enumerate(chips):
            for a in range(na):
                copy(a, 1 + j, (*chip, c), me).wait_recv()
                passed[j * na + a].start()

    def finish():
        for a in range(na):
            copy(a, 0, sibling, me).wait_recv()
        for j, chip in enumerate(chips):
            for a in range(na):
                copy(a, 4 + j, (*chip, 1 - c), me).wait_recv()
        for cp in first + passed:
            cp.wait_send()
        for cp in local:
            cp.wait()

    return start, relay, finish


def _allgather(srcs, dst_at, send_sems, recv_sems, loc_sems):
    start, relay, finish = _allgather_phases(srcs, dst_at, send_sems, recv_sems, loc_sems)
    start()
    relay()
    finish()


def _owner_exchange(gq, rq_ref, x_send, x_recv, x_loc):
    mx, my_, mc = _me()
    me = _slot(mx, my_, mc)

    def rows_of(a, slot):
        return gq[a].at[pl.ds(pl.multiple_of(slot * 128, 128), 128), :]

    def exchange(k, a, recv):
        px, py, pc = _xor_peer(k)
        peer = _slot(px, py, pc)
        return pltpu.make_async_remote_copy(
            src_ref=rows_of(a, me if recv else peer), dst_ref=rq_ref.at[peer if recv else me, a],
            send_sem=x_send.at[a, k - 1], recv_sem=x_recv.at[a, k - 1],
            device_id=(px, py, pc), device_id_type=MESH)

    local = [pltpu.make_async_copy(rows_of(a, me), rq_ref.at[me, a], x_loc.at[a]) for a in range(len(gq))]

    def start():
        for cp in local:
            cp.start()
        for k in range(1, NDEV):
            for a in range(len(gq)):
                exchange(k, a, False).start()

    def wait():
        for k in range(1, NDEV):
            for a in range(len(gq)):
                exchange(k, a, True).wait_recv()
        for k in range(1, NDEV):
            for a in range(len(gq)):
                exchange(k, a, False).wait_send()
        for cp in local:
            cp.wait()

    return start, wait


def _sig(v):
    return jax.nn.sigmoid(v)


def _gelu_parts(v):
    cdf = 0.5 * (1.0 + lax.erf(v * INV_SQRT2))
    pdf = jnp.exp(-0.5 * v * v) * INV_SQRT_2PI
    return v * cdf, cdf + v * pdf


def _ln_stats(v):
    mu = jnp.mean(v, axis=-1, keepdims=True)
    xc = v - mu
    var = jnp.mean(xc * xc, axis=-1, keepdims=True)
    rs = lax.rsqrt(var + EPS)
    return xc * rs, rs


def _ln_bwd(dn, n, rs):
    return rs * (dn - jnp.mean(dn, axis=-1, keepdims=True) - n * jnp.mean(dn * n, axis=-1, keepdims=True))


def _fold8(v):
    acc = v[0:8]
    for r in range(8, v.shape[0], 8):
        acc = acc + v[r:r + 8]
    return acc


def _rows(rb, n=RB):
    return pl.ds(pl.multiple_of(rb * n, n), n)


def _ld(ref, rows):
    return ref[rows, :].astype(F32)


def _blocks(n, body, init=0, unroll=UNROLL):
    def trip(t, carry):
        for u in range(unroll):
            carry = body(t * unroll + u, carry)
        return carry
    return lax.fori_loop(0, n // unroll, trip, init)


def _shift_copies(s_ref):
    n = EXT - 8
    for m in range(1, 8):
        for c0 in range(0, n, 56):
            s_ref[m, c0:c0 + 56, :] = s_ref[0, c0 + m:c0 + m + 56, :]


def _ada_fwd(c, w_ada, b_ada):
    wcols = w_ada.shape[1]

    def body(c_ref, w_ref, b_ref, mod_ref, call_ref, cdst, msrc, mdst, c_send, c_recv, m_send, m_recv):
        x, y, c = _me()
        me = _slot(x, y, c)

        def all_to_all(src_of, dst, ss, rs):
            def cp(k, recv):
                px, py, pc = _xor_peer(k)
                peer = _slot(px, py, pc)
                return pltpu.make_async_remote_copy(
                    src_ref=src_of(me if recv else peer), dst_ref=dst.at[peer if recv else me],
                    send_sem=ss.at[k - 1], recv_sem=rs.at[k - 1], device_id=(px, py, pc), device_id_type=MESH)
            for k in range(1, NDEV):
                cp(k, False).start()
            for k in range(1, NDEV):
                cp(k, True).wait_recv()
            for k in range(1, NDEV):
                cp(k, False).wait_send()

        cdst[me] = c_ref[...]
        all_to_all(lambda s: c_ref, cdst, c_send, c_recv)
        for b in range(NDEV):
            call_ref[b:b + 1, :] = cdst[b]
        m = jnp.dot(call_ref[...], w_ref[...], preferred_element_type=F32, precision=lax.Precision.HIGHEST)
        for b in range(NDEV):
            msrc[b] = m[b:b + 1, :]
        mdst[me] = msrc[me]
        all_to_all(lambda s: msrc.at[s], mdst, m_send, m_recv)
        full = jnp.concatenate([mdst[k] for k in range(NDEV)], axis=1) + b_ref[...]
        for r in range(3):
            mod_ref[r:r + 1, :] = full[:, r * D:(r + 1) * D]

    vm = pl.BlockSpec(memory_space=VMEM)
    return pl.pallas_call(
        body, name="ada_fwd",
        out_shape=(jax.ShapeDtypeStruct((3, D), F32), jax.ShapeDtypeStruct((NDEV, D), F32)),
        in_specs=[vm, vm, vm], out_specs=(vm, vm),
        scratch_shapes=[VMEM((NDEV, 1, D), F32), VMEM((NDEV, 1, wcols), F32), VMEM((NDEV, 1, wcols), F32),
                        pltpu.SemaphoreType.DMA((7,)), pltpu.SemaphoreType.DMA((7,)),
                        pltpu.SemaphoreType.DMA((7,)), pltpu.SemaphoreType.DMA((7,))],
        compiler_params=_cp(vmem_mb=32),
    )(c, w_ada, b_ada)


def _prep_h(x, mod, g_pre):
    T = x.shape[0]

    def body(x_ref, mod_ref, g_ref, h_ref, ht_ref):
        def blk(rb, carry):
            rows = _rows(rb)
            xv = x_ref[rows, :]
            r = lax.rsqrt(jnp.mean(xv * xv, axis=-1, keepdims=True) + EPS)
            h_ref[rows, :] = ((xv * r) * g_ref[...] * (1.0 + mod_ref[1:2, :]) + mod_ref[0:1, :]).astype(BF16)
            return carry
        _blocks(TMB // RB, blk)
        ht_ref[...] = h_ref[...].T

    return pl.pallas_call(
        body, name="prep_h", grid=(T // TMB,),
        out_shape=(jax.ShapeDtypeStruct((T, D), BF16), jax.ShapeDtypeStruct((D, T), BF16)),
        in_specs=[pl.BlockSpec((TMB, D), lambda i: (i, 0)), pl.BlockSpec((3, D), lambda i: (0, 0)),
                  pl.BlockSpec((1, D), lambda i: (0, 0))],
        out_specs=(pl.BlockSpec((TMB, D), lambda i: (i, 0)), pl.BlockSpec((D, TMB), lambda i: (0, i))),
        compiler_params=_cp(("parallel",)),
    )(x, mod, g_pre)


AG_ORDER = (0, 1, 4, 5, 2, 3, 6, 7)


def _in_proj(h, w_in, order, w_co, w_so, w_o, conv_w):
    T = h.shape[0]
    tmp = min(2 * TMB, T)
    nb = T // tmp

    def body(order_ref, h_ref, w_ref, wco_ref, wso_ref, wo_ref, cw_ref, p_ref, wing_ref, wqg_ref, cwg_ref,
             wbuf, wq_st, send_sems, recv_sems, out_sems, ag_send, ag_recv, ag_loc):
        j = pl.program_id(0)
        i = pl.program_id(1)
        x, y, c = _me()
        me = (x, y, c)
        sibling = (x, y, 1 - c)
        chips = [(1 - x, y), (x, 1 - y), (1 - x, 1 - y)]
        blocks = [me, sibling] + [(*chip, cc) for chip in chips for cc in (c, 1 - c)]
        recv_k = [None, 0, 1, 4, 2, 5, 3, 6]
        ag_start, ag_relay, ag_finish = _allgather_phases(
            [wq_st, cw_ref], lambda a, slot: wqg_ref.at[:, slot] if a == 0 else cwg_ref.at[slot], ag_send, ag_recv, ag_loc)

        def copy(k, block, to):
            d = wbuf.at[_slot(*block)]
            return pltpu.make_async_remote_copy(src_ref=d, dst_ref=d, send_sem=send_sems.at[k], recv_sem=recv_sems.at[k],
                                                device_id=to, device_id_type=MESH)

        def writeback(jj):
            s = _slot(*blocks[jj])
            return pltpu.make_async_copy(wbuf.at[s], wing_ref.at[s], out_sems.at[jj])

        first = [copy(0, me, sibling)] + [copy(1 + q, me, (*chip, c)) for q, chip in enumerate(chips)]
        passed = [copy(4 + q, (*chip, c), sibling) for q, chip in enumerate(chips)]

        @pl.when((j == 0) & (i == 0))
        def _():
            wbuf[_slot(*me)] = w_ref[...].astype(BF16)
            for cp in first:
                cp.start()
            writeback(0).start()
            wq_st[W_CO] = wco_ref[...].astype(BF16)
            wq_st[W_SO] = wso_ref[...].astype(BF16)
            wq_st[W_O] = wo_ref[...].astype(BF16)
            ag_start()

        for jj in range(1, NSEG):
            @pl.when((j == jj) & (i == 0))
            def _(jj=jj):
                copy(recv_k[jj], blocks[jj], me).wait_recv()
                if jj in (2, 4, 6):
                    passed[jj // 2 - 1].start()
                writeback(jj).start()
                if jj == NSEG - 1:
                    ag_relay()

        p_ref[...] = jnp.dot(h_ref[...], wbuf[order_ref[j]], preferred_element_type=F32).astype(BF16)

        @pl.when((j == NSEG - 1) & (i == nb - 1))
        def _():
            for cp in first + passed:
                cp.wait_send()
            for jj in range(NSEG):
                writeback(jj).wait()
            ag_finish()

    vm = pl.BlockSpec(memory_space=VMEM)
    hbm = pl.BlockSpec(memory_space=ANY)
    return pl.pallas_call(
        body, name="in_proj",
        out_shape=(jax.ShapeDtypeStruct((T, NSEG * D), BF16), jax.ShapeDtypeStruct((NSEG, D, D), BF16),
                   jax.ShapeDtypeStruct((3, NDEV, 128, D), BF16), jax.ShapeDtypeStruct((NDEV, KW, 128), F32)),
        grid_spec=pltpu.PrefetchScalarGridSpec(
            num_scalar_prefetch=1, grid=(NSEG, nb),
            in_specs=[pl.BlockSpec((tmp, D), lambda j, i, o: (i, 0)), vm, vm, vm, vm, vm],
            out_specs=(pl.BlockSpec((tmp, D), lambda j, i, o: (i, o[j])), hbm, hbm, hbm),
            scratch_shapes=[VMEM((NSEG, D, D), BF16), VMEM((3, 128, D), BF16),
                            pltpu.SemaphoreType.DMA((7,)), pltpu.SemaphoreType.DMA((7,)), pltpu.SemaphoreType.DMA((NSEG,)),
                            pltpu.SemaphoreType.DMA((2, 7)), pltpu.SemaphoreType.DMA((2, 7)), pltpu.SemaphoreType.DMA((2,))]),
        compiler_params=_cp(("arbitrary", "arbitrary")),
    )(order, h, w_in, w_co, w_so, w_o, conv_w)


def _fill_a_ext(i, s_ref, val_ref, glu_ref, hval_ref, hglu_ref):
    ah = hval_ref[...].astype(F32) * _sig(hglu_ref[...].astype(F32))
    s_ref[0, 0:HALO, :] = jnp.where(i > 0, ah, 0.0)

    def fill(rb, carry):
        rows = _rows(rb)
        s_ref[0, pl.ds(pl.multiple_of(HALO + rb * RB, RB), RB), :] = _ld(val_ref, rows) * _sig(_ld(glu_ref, rows))
        return carry
    lax.fori_loop(0, TM // RB, fill, 0)


def _halo_prev(seg):
    hb = TM // HALO
    return pl.BlockSpec((HALO, D), lambda i: (jnp.maximum(i * hb - 1, 0), seg))


def _branch_a_fwd(p, cw, conv_b, lg, lb, wq):
    T = p.shape[0]

    def body(val_ref, glu_ref, z_ref, hval_ref, hglu_ref, cw_ref, cb_ref, lg_ref, lb_ref, w_ref,
             yc_ref, ya_ref, s_ref, yp_ref):
        i = pl.program_id(0)
        _fill_a_ext(i, s_ref, val_ref, glu_ref, hval_ref, hglu_ref)
        _shift_copies(s_ref)

        def conv(rb, carry):
            r0 = rb * RBC
            accs = [jnp.broadcast_to(cb_ref[...], (8, D))] * (RBC // 8)
            for k in range(KW):
                o = 2 + k
                w = cw_ref[k]
                accs = [acc + w * s_ref[o % 8, pl.ds(pl.multiple_of(r0 + 8 * (o // 8 + g), 8), 8), :]
                        for g, acc in enumerate(accs)]
            for g, acc in enumerate(accs):
                yc_ref[pl.ds(pl.multiple_of(r0 + 8 * g, 8), 8), :] = acc
            return carry
        lax.fori_loop(0, TM // RBC, conv, 0)

        def post(rb, carry):
            rows = _rows(rb)
            n, _ = _ln_stats(yc_ref[rows, :])
            l = n * lg_ref[...] + lb_ref[...]
            z = _ld(z_ref, rows)
            yp_ref[rows, :] = ((l * _sig(l)) * (z * _sig(z))).astype(BF16)
            return carry
        _blocks(TM // RB, post)
        ya_ref[...] = jnp.dot(yp_ref[...], w_ref[...], preferred_element_type=F32)

    tile = lambda seg: pl.BlockSpec((TM, D), lambda i: (i, seg))
    row = pl.BlockSpec((1, D), lambda i: (0, 0))
    return pl.pallas_call(
        body, name="branch_a_fwd", grid=(T // TM,),
        out_shape=(jax.ShapeDtypeStruct((T, D), F32), jax.ShapeDtypeStruct((T, D), F32)),
        in_specs=[tile(0), tile(1), tile(2), _halo_prev(0), _halo_prev(1),
                  pl.BlockSpec((KW, 8, D), lambda i: (0, 0, 0)), row, row, row,
                  pl.BlockSpec((None, D, D), lambda i: (W_CO, 0, 0))],
        out_specs=(pl.BlockSpec((TM, D), lambda i: (i, 0)), pl.BlockSpec((TM, D), lambda i: (i, 0))),
        scratch_shapes=[VMEM((8, EXT, D), F32), VMEM((TM, D), BF16)],
        compiler_params=_cp(("parallel",)),
    )(p, p, p, p, p, cw, conv_b, lg, lb, wq)


def _masked_ws(ws_ref, wt_ref):
    tri = lax.broadcasted_iota(jnp.int32, (CHUNK, CHUNK), 0) >= lax.broadcasted_iota(jnp.int32, (CHUNK, CHUNK), 1)
    for h in range(HEADS):
        wt_ref[h] = jnp.where(tri, ws_ref[h], 0.0).astype(BF16)


def _sgu_mix(wt_ref, vl_ref, bst_ref, s_ref):
    for ck in range(TM // CHUNK):
        r = slice(ck * CHUNK, (ck + 1) * CHUNK)
        for h in range(HEADS):
            cs = slice(h * HD, (h + 1) * HD)
            s_ref[r, cs] = jnp.dot(wt_ref[h], vl_ref[r, cs], preferred_element_type=F32) + bst_ref[:, h:h + 1]


def _branch_b_fwd(p, ws, bst, sg, sb, wq):
    T = p.shape[0]

    def body(pu_ref, pv_ref, pz_ref, ws_ref, bst_ref, sg_ref, sb_ref, w_ref, yb_ref,
             wt_ref, vl_ref, t_ref, s_ref, yp_ref):
        _masked_ws(ws_ref, wt_ref)

        def pre(rb, carry):
            rows = _rows(rb)
            vg, _ = _gelu_parts(_ld(pv_ref, rows))
            vn, _ = _ln_stats(vg)
            vl_ref[rows, :] = (vn * sg_ref[...] + sb_ref[...]).astype(BF16)
            u, _ = _gelu_parts(_ld(pu_ref, rows))
            z = _ld(pz_ref, rows)
            t_ref[rows, :] = u * (z * _sig(z))
            return carry
        _blocks(TM // RB, pre)
        _sgu_mix(wt_ref, vl_ref, bst_ref, s_ref)

        def post(rb, carry):
            rows = _rows(rb)
            yp_ref[rows, :] = (t_ref[rows, :] * s_ref[rows, :]).astype(BF16)
            return carry
        _blocks(TM // RB, post)
        yb_ref[...] = jnp.dot(yp_ref[...], w_ref[...], preferred_element_type=F32)

    tile = lambda seg: pl.BlockSpec((TM, D), lambda i: (i, seg))
    row = pl.BlockSpec((1, D), lambda i: (0, 0))
    return pl.pallas_call(
        body, name="branch_b_fwd", grid=(T // TM,),
        out_shape=jax.ShapeDtypeStruct((T, D), F32),
        in_specs=[tile(3), tile(4), tile(5), pl.BlockSpec((HEADS, CHUNK, CHUNK), lambda i: (0, 0, 0)),
                  pl.BlockSpec((CHUNK, HEADS), lambda i: (0, 0)), row, row,
                  pl.BlockSpec((None, D, D), lambda i: (W_SO, 0, 0))],
        out_specs=pl.BlockSpec((TM, D), lambda i: (i, 0)),
        scratch_shapes=[VMEM((HEADS, CHUNK, CHUNK), BF16), VMEM((TM, D), BF16), VMEM((TM, D), F32),
                        VMEM((TM, D), F32), VMEM((TM, D), BF16)],
        compiler_params=_cp(("parallel",)),
    )(p, p, p, ws, bst, sg, sb, wq)


def _merge_loss(p, ya, yb, x, tgt, mod, g_final, wq):
    T = x.shape[0]
    nt = T // TM

    def body(ga_ref, gb_ref, ya_ref, yb_ref, x_ref, t_ref, mod_ref, gf_ref, w_ref,
             dx2_ref, dya_ref, dyb_ref, dp_ref, gwo_ref, st_ref,
             mrg_s, out_s, dout_s, dm_s, gw_acc):
        i = pl.program_id(0)

        @pl.when(i == 0)
        def _():
            gw_acc[...] = jnp.zeros_like(gw_acc)
            st_ref[...] = jnp.zeros_like(st_ref)

        def merge(rb, carry):
            rows = _rows(rb)
            mrg_s[rows, :] = (_sig(_ld(ga_ref, rows)) * ya_ref[rows, :] + _sig(_ld(gb_ref, rows)) * yb_ref[rows, :]).astype(BF16)
            return carry
        lax.fori_loop(0, TM // RB, merge, 0)
        out_s[...] = jnp.dot(mrg_s[...], w_ref[...], preferred_element_type=F32)

        def head(rb, carry):
            loss, gg, dg = carry
            rows = _rows(rb)
            gate = mod_ref[2:3, :]
            gf = gf_ref[...]
            out = out_s[rows, :]
            x2 = x_ref[rows, :] + gate * out
            r2 = lax.rsqrt(jnp.mean(x2 * x2, axis=-1, keepdims=True) + EPS)
            x2n = x2 * r2
            diff = x2n * gf - t_ref[rows, :]
            dy = diff * (1.0 / D)
            dx2n = dy * gf
            dx2 = r2 * (dx2n - x2n * jnp.mean(dx2n * x2n, axis=-1, keepdims=True))
            dx2_ref[rows, :] = dx2
            dout_s[rows, :] = (dx2 * gate).astype(BF16)
            return loss + _fold8(diff * diff), gg + _fold8(dy * x2n), dg + _fold8(dx2 * out)
        zero = jnp.zeros((8, D), F32)
        loss, gg, dg = _blocks(TM // RB, head, (zero, zero, zero))
        st_ref[0] = st_ref[0] + loss * (0.5 / D)
        st_ref[1] = st_ref[1] + gg
        st_ref[2] = st_ref[2] + dg

        dm_s[...] = lax.dot_general(dout_s[...], w_ref[...], NT, preferred_element_type=F32)
        gw_acc[...] += lax.dot_general(mrg_s[...], dout_s[...], TN, preferred_element_type=F32)

        def split(rb, carry):
            rows = _rows(rb)
            dm = dm_s[rows, :]
            sa = _sig(_ld(ga_ref, rows))
            sb = _sig(_ld(gb_ref, rows))
            dya_ref[rows, :] = (dm * sa).astype(BF16)
            dyb_ref[rows, :] = (dm * sb).astype(BF16)
            dp_ref[0, rows, :] = (dm * ya_ref[rows, :] * (sa * (1.0 - sa))).astype(BF16)
            dp_ref[1, rows, :] = (dm * yb_ref[rows, :] * (sb * (1.0 - sb))).astype(BF16)
            return carry
        lax.fori_loop(0, TM // RB, split, 0)

        @pl.when(i == nt - 1)
        def _():
            gwo_ref[...] = gw_acc[...].astype(BF16)

    tile = pl.BlockSpec((TM, D), lambda i: (i, 0))
    return pl.pallas_call(
        body, name="merge_loss", grid=(nt,),
        out_shape=(jax.ShapeDtypeStruct((T, D), F32), jax.ShapeDtypeStruct((T, D), BF16), jax.ShapeDtypeStruct((T, D), BF16),
                   jax.ShapeDtypeStruct((NSEG, T, D), BF16), jax.ShapeDtypeStruct((D, D), BF16),
                   jax.ShapeDtypeStruct((3, 8, D), F32)),
        in_specs=[pl.BlockSpec((TM, D), lambda i: (i, 6)), pl.BlockSpec((TM, D), lambda i: (i, 7)), tile, tile, tile, tile,
                  pl.BlockSpec((3, D), lambda i: (0, 0)), pl.BlockSpec((1, D), lambda i: (0, 0)),
                  pl.BlockSpec((None, D, D), lambda i: (W_O, 0, 0))],
        out_specs=(tile, tile, tile, pl.BlockSpec((2, TM, D), lambda i: (3, i, 0)),
                   pl.BlockSpec((D, D), lambda i: (0, 0)), pl.BlockSpec((3, 8, D), lambda i: (0, 0, 0))),
        scratch_shapes=[VMEM((TM, D), BF16), VMEM((TM, D), F32), VMEM((TM, D), BF16), VMEM((TM, D), F32), VMEM((D, D), F32)],
        compiler_params=_cp(("arbitrary",)),
    )(p, p, ya, yb, x, tgt, mod, g_final, wq)


def _branch_b_bwd(dp, dyb, p, ws, bst, sg, sb, wq):
    T = p.shape[0]
    nt = T // TM

    def body(dp_in, dyb_ref, pu_ref, pv_ref, pz_ref, ws_ref, bst_ref, sg_ref, sb_ref, w_ref,
             dp_ref, gw_ref, gws_ref, gbs_ref, st_ref,
             wt_ref, d_s, vl_s, vn_s, gpv_s, rs_s, s_s, ds_s, ds32_s, yp_s, dvl_s, gw_acc):
        del dp_in
        i = pl.program_id(0)

        @pl.when(i == 0)
        def _():
            gw_acc[...] = jnp.zeros_like(gw_acc)
            gws_ref[...] = jnp.zeros_like(gws_ref)
            gbs_ref[...] = jnp.zeros_like(gbs_ref)
            st_ref[...] = jnp.zeros_like(st_ref)
            _masked_ws(ws_ref, wt_ref)

        d_s[...] = lax.dot_general(dyb_ref[...], w_ref[...], NT, preferred_element_type=F32)

        def pre(rb, carry):
            rows = _rows(rb)
            vg, gpv = _gelu_parts(_ld(pv_ref, rows))
            vn, rs = _ln_stats(vg)
            vl_s[rows, :] = (vn * sg_ref[...] + sb_ref[...]).astype(BF16)
            vn_s[rows, :] = vn
            gpv_s[rows, :] = gpv
            rs_s[rows, :] = rs
            return carry
        _blocks(TM // RB, pre)
        _sgu_mix(wt_ref, vl_s, bst_ref, s_s)

        def mid(rb, carry):
            rows = _rows(rb)
            u, gpu = _gelu_parts(_ld(pu_ref, rows))
            z = _ld(pz_ref, rows)
            sz = _sig(z)
            siluz = z * sz
            d = d_s[rows, :]
            s = s_s[rows, :]
            t = u * siluz
            yp_s[rows, :] = (t * s).astype(BF16)
            dp_ref[0, rows, :] = (d * s * siluz * gpu).astype(BF16)
            ds = d * t
            ds32_s[rows, :] = ds
            ds_s[rows, :] = ds.astype(BF16)
            dp_ref[2, rows, :] = (d * u * s * (sz * (1.0 + z * (1.0 - sz)))).astype(BF16)
            return carry
        lax.fori_loop(0, TM // RB, mid, 0)

        for ck in range(TM // CHUNK):
            r = slice(ck * CHUNK, (ck + 1) * CHUNK)
            for h in range(HEADS):
                cs = slice(h * HD, (h + 1) * HD)
                dsh = ds_s[r, cs]
                dvl_s[r, cs] = lax.dot_general(wt_ref[h], dsh, TN, preferred_element_type=F32)
                gws_ref[h] += lax.dot_general(dsh, vl_s[r, cs], NT, preferred_element_type=F32)
                gbs_ref[h] += ds32_s[r, cs]

        def post(rb, carry):
            g_sg, g_sb = carry
            rows = _rows(rb)
            dvl = dvl_s[rows, :]
            vn = vn_s[rows, :]
            dvg = _ln_bwd(dvl * sg_ref[...], vn, rs_s[rows, :])
            dp_ref[1, rows, :] = (dvg * gpv_s[rows, :]).astype(BF16)
            return g_sg + _fold8(dvl * vn), g_sb + _fold8(dvl)
        zero = jnp.zeros((8, D), F32)
        g_sg, g_sb = _blocks(TM // RB, post, (zero, zero))
        st_ref[0] = st_ref[0] + g_sg
        st_ref[1] = st_ref[1] + g_sb

        gw_acc[...] += lax.dot_general(yp_s[...], dyb_ref[...], TN, preferred_element_type=F32)

        @pl.when(i == nt - 1)
        def _():
            gw_ref[...] = gw_acc[...].astype(BF16)

    tile = lambda seg: pl.BlockSpec((TM, D), lambda i: (i, seg))
    row = pl.BlockSpec((1, D), lambda i: (0, 0))
    hh = pl.BlockSpec((HEADS, CHUNK, CHUNK), lambda i: (0, 0, 0))
    return pl.pallas_call(
        body, name="branch_b_bwd", grid=(nt,),
        out_shape=(jax.ShapeDtypeStruct((NSEG, T, D), BF16), jax.ShapeDtypeStruct((D, D), BF16),
                   jax.ShapeDtypeStruct((HEADS, CHUNK, CHUNK), F32), jax.ShapeDtypeStruct((HEADS, CHUNK, HD), F32),
                   jax.ShapeDtypeStruct((2, 8, D), F32)),
        in_specs=[pl.BlockSpec(memory_space=ANY), pl.BlockSpec((TM, D), lambda i: (i, 0)), tile(3), tile(4), tile(5),
                  hh, pl.BlockSpec((CHUNK, HEADS), lambda i: (0, 0)), row, row,
                  pl.BlockSpec((None, D, D), lambda i: (W_SO, 0, 0))],
        out_specs=(pl.BlockSpec((3, TM, D), lambda i: (1, i, 0)), pl.BlockSpec((D, D), lambda i: (0, 0)), hh, hh,
                   pl.BlockSpec((2, 8, D), lambda i: (0, 0, 0))),
        scratch_shapes=[VMEM((HEADS, CHUNK, CHUNK), BF16), VMEM((TM, D), F32), VMEM((TM, D), BF16), VMEM((TM, D), F32),
                        VMEM((TM, D), F32), VMEM((TM, 1), F32), VMEM((TM, D), F32), VMEM((TM, D), BF16), VMEM((TM, D), F32),
                        VMEM((TM, D), BF16), VMEM((TM, D), F32), VMEM((D, D), F32)],
        input_output_aliases={0: 0},
        compiler_params=_cp(("arbitrary",)),
    )(dp, dyb, p, p, p, ws, bst, sg, sb, wq)


def _branch_a_bwd1(dp, dya, yc, p, lg, lb, wq):
    T = p.shape[0]
    nt = T // TM

    def body(dp_in, dya_ref, yc_ref, z_ref, lg_ref, lb_ref, w_ref, dp_ref, dyc_ref, gw_ref, st_ref,
             d_s, yp_s, gw_acc):
        del dp_in
        i = pl.program_id(0)

        @pl.when(i == 0)
        def _():
            gw_acc[...] = jnp.zeros_like(gw_acc)
            st_ref[...] = jnp.zeros_like(st_ref)

        d_s[...] = lax.dot_general(dya_ref[...], w_ref[...], NT, preferred_element_type=F32)

        def blk(rb, carry):
            g_lg, g_lb, g_cb = carry
            rows = _rows(rb)
            n, rs = _ln_stats(yc_ref[rows, :])
            l = n * lg_ref[...] + lb_ref[...]
            sgl = _sig(l)
            sl = l * sgl
            z = _ld(z_ref, rows)
            sz = _sig(z)
            siluz = z * sz
            d = d_s[rows, :]
            yp_s[rows, :] = (sl * siluz).astype(BF16)
            dp_ref[rows, :] = (d * sl * (sz * (1.0 + z * (1.0 - sz)))).astype(BF16)
            dl = (d * siluz) * (sgl * (1.0 + l * (1.0 - sgl)))
            dyc = _ln_bwd(dl * lg_ref[...], n, rs)
            dyc_ref[rows, :] = dyc
            return g_lg + _fold8(dl * n), g_lb + _fold8(dl), g_cb + _fold8(dyc)
        zero = jnp.zeros((8, D), F32)
        g_lg, g_lb, g_cb = _blocks(TM // RB, blk, (zero, zero, zero))
        st_ref[0] = st_ref[0] + g_lg
        st_ref[1] = st_ref[1] + g_lb
        st_ref[2] = st_ref[2] + g_cb

        gw_acc[...] += lax.dot_general(yp_s[...], dya_ref[...], TN, preferred_element_type=F32)

        @pl.when(i == nt - 1)
        def _():
            gw_ref[...] = gw_acc[...].astype(BF16)

    tile = pl.BlockSpec((TM, D), lambda i: (i, 0))
    row = pl.BlockSpec((1, D), lambda i: (0, 0))
    return pl.pallas_call(
        body, name="branch_a_bwd1", grid=(nt,),
        out_shape=(jax.ShapeDtypeStruct((NSEG, T, D), BF16), jax.ShapeDtypeStruct((T, D), F32),
                   jax.ShapeDtypeStruct((D, D), BF16), jax.ShapeDtypeStruct((3, 8, D), F32)),
        in_specs=[pl.BlockSpec(memory_space=ANY), tile, tile, pl.BlockSpec((TM, D), lambda i: (i, 2)), row, row,
                  pl.BlockSpec((None, D, D), lambda i: (0, 0, 0))],
        out_specs=(pl.BlockSpec((None, TM, D), lambda i: (2, i, 0)), tile, pl.BlockSpec((D, D), lambda i: (0, 0)),
                   pl.BlockSpec((3, 8, D), lambda i: (0, 0, 0))),
        scratch_shapes=[VMEM((TM, D), F32), VMEM((TM, D), BF16), VMEM((D, D), F32)],
        input_output_aliases={0: 0},
        compiler_params=_cp(("arbitrary",)),
    )(dp, dya, yc, p, lg, lb, wq)


def _branch_a_bwd2(dp, dyc, p, cw, gw_co, gw_so, gw_o, early):
    T = p.shape[0]
    nt = T // TM
    hb = TM // HALO

    def body(dp_in, dyc_ref, hdyc_ref, val_ref, glu_ref, cw_ref, gco_ref, gso_ref, go_ref, e_ref,
             dp_ref, gcw_ref, rq_ref, eg_ref, a_s, sd_ref, da_s, x_send, x_recv, x_loc, ag_send, ag_recv, ag_loc):
        del dp_in
        i = pl.program_id(0)
        start_exchange, wait_exchange = _owner_exchange((gco_ref, gso_ref, go_ref), rq_ref, x_send, x_recv, x_loc)
        ag_start, ag_relay, ag_finish = _allgather_phases([e_ref], lambda a, slot: eg_ref.at[slot],
                                                          ag_send, ag_recv, ag_loc)

        @pl.when(i == 0)
        def _():
            gcw_ref[...] = jnp.zeros_like(gcw_ref)
            ag_start()
            start_exchange()

        @pl.when(i == nt // 2)
        def _():
            ag_relay()

        def fill(rb, carry):
            rows = _rows(rb)
            a_s[rows, :] = _ld(val_ref, rows) * _sig(_ld(glu_ref, rows))
            return carry
        lax.fori_loop(0, TM // RB, fill, 0)
        sd_ref[0, 0:TM, :] = dyc_ref[...]
        sd_ref[0, TM:EXT, :] = jnp.where(i < nt - 1, hdyc_ref[...], 0.0)
        _shift_copies(sd_ref)

        def conv_t(rb, carry):
            r0 = rb * RBC
            accs = [jnp.zeros((8, D), F32)] * (RBC // 8)
            for k in range(KW):
                o = KW - 1 - k
                w = cw_ref[k]
                accs = [acc + w * sd_ref[o % 8, pl.ds(pl.multiple_of(r0 + 8 * (o // 8 + g), 8), 8), :]
                        for g, acc in enumerate(accs)]
            for g, acc in enumerate(accs):
                da_s[pl.ds(pl.multiple_of(r0 + 8 * g, 8), 8), :] = acc
            return carry
        lax.fori_loop(0, TM // RBC, conv_t, 0)

        for k0 in range(0, KW, TAPG):
            taps = list(range(k0, min(k0 + TAPG, KW)))

            def tap_group(rb, accs, taps=taps):
                for u in range(2):
                    r0 = rb * 32 + 16 * u
                    a = a_s[pl.ds(pl.multiple_of(r0, 16), 16), :]
                    out = []
                    for k, acc in zip(taps, accs):
                        o = KW - 1 - k
                        d = sd_ref[o % 8, pl.ds(pl.multiple_of(r0 + 8 * (o // 8), 8), 16), :]
                        out.append(acc + _fold8(a * d))
                    accs = tuple(out)
                return accs
            sums = lax.fori_loop(0, TM // 32, tap_group, tuple(jnp.zeros((8, D), F32) for _ in taps))
            for k, s in zip(taps, sums):
                gcw_ref[k] = gcw_ref[k] + s

        def glu_b(rb, carry):
            rows = _rows(rb)
            da = da_s[rows, :]
            sg = _sig(_ld(glu_ref, rows))
            dp_ref[0, rows, :] = (da * sg).astype(BF16)
            dp_ref[1, rows, :] = (da * _ld(val_ref, rows) * (sg * (1.0 - sg))).astype(BF16)
            return carry
        lax.fori_loop(0, TM // RB, glu_b, 0)

        @pl.when(i == nt - 1)
        def _():
            wait_exchange()
            ag_finish()

    tile = lambda seg: pl.BlockSpec((TM, D), lambda i: (i, seg))
    hbm = pl.BlockSpec(memory_space=ANY)
    return pl.pallas_call(
        body, name="branch_a_bwd2", grid=(nt,),
        out_shape=(jax.ShapeDtypeStruct((NSEG, T, D), BF16), jax.ShapeDtypeStruct((32, 8, D), F32),
                   jax.ShapeDtypeStruct((NDEV, 3, 128, D), BF16), jax.ShapeDtypeStruct((NDEV, NEARLY, D), F32)),
        in_specs=[hbm, pl.BlockSpec((TM, D), lambda i: (i, 0)),
                  pl.BlockSpec((HALO, D), lambda i: (jnp.minimum((i + 1) * hb, nt * hb - 1), 0)),
                  tile(0), tile(1), pl.BlockSpec((KW, 8, D), lambda i: (0, 0, 0)),
                  hbm, hbm, hbm, hbm],
        out_specs=(pl.BlockSpec((2, TM, D), lambda i: (0, i, 0)), pl.BlockSpec((32, 8, D), lambda i: (0, 0, 0)), hbm, hbm),
        scratch_shapes=[VMEM((TM, D), F32), VMEM((8, EXT, D), F32), VMEM((TM, D), F32),
                        pltpu.SemaphoreType.DMA((3, 7)), pltpu.SemaphoreType.DMA((3, 7)), pltpu.SemaphoreType.DMA((3,)),
                        pltpu.SemaphoreType.DMA((1, 7)), pltpu.SemaphoreType.DMA((1, 7)), pltpu.SemaphoreType.DMA((1,))],
        input_output_aliases={0: 0},
        compiler_params=_cp(("arbitrary",)),
    )(dp, dyc, dyc, p, p, cw, gw_co, gw_so, gw_o, early)


def _in_bwd_dx(dp, wing, x, dx2, mod, g_pre):
    T = x.shape[0]
    nt = T // TM

    def body(dp_ref, w_ref, x_ref, dx2_ref, mod_ref, g_ref, gx_ref, st_ref, dh_s):
        i = pl.program_id(0)

        @pl.when(i == 0)
        def _():
            st_ref[...] = jnp.zeros_like(st_ref)

        dh_s[...] = lax.dot_general(dp_ref[0], w_ref[0], NT, preferred_element_type=F32)
        for j in range(1, NSEG):
            dh_s[...] += lax.dot_general(dp_ref[j], w_ref[j], NT, preferred_element_type=F32)

        def blk(rb, carry):
            d_sh, d_sc, g_g = carry
            rows = _rows(rb)
            xv = x_ref[rows, :]
            r = lax.rsqrt(jnp.mean(xv * xv, axis=-1, keepdims=True) + EPS)
            xn = xv * r
            g = g_ref[...]
            hpre = xn * g
            dh = dh_s[rows, :]
            dhp = dh * (1.0 + mod_ref[1:2, :])
            dxn = dhp * g
            gx_ref[rows, :] = dx2_ref[rows, :] + r * (dxn - xn * jnp.mean(dxn * xn, axis=-1, keepdims=True))
            return d_sh + _fold8(dh), d_sc + _fold8(dh * hpre), g_g + _fold8(dhp * xn)
        zero = jnp.zeros((8, D), F32)
        d_sh, d_sc, g_g = _blocks(TM // RB, blk, (zero, zero, zero))
        st_ref[0] = st_ref[0] + d_sh
        st_ref[1] = st_ref[1] + d_sc
        st_ref[2] = st_ref[2] + g_g

    tile = pl.BlockSpec((TM, D), lambda i: (i, 0))
    return pl.pallas_call(
        body, name="in_bwd_dx", grid=(nt,),
        out_shape=(jax.ShapeDtypeStruct((T, D), F32), jax.ShapeDtypeStruct((3, 8, D), F32)),
        in_specs=[pl.BlockSpec((NSEG, TM, D), lambda i: (0, i, 0)),
                  pl.BlockSpec((NSEG, D, D), lambda i: (0, 0, 0), pipeline_mode=pl.Buffered(1)),
                  tile, tile, pl.BlockSpec((3, D), lambda i: (0, 0)), pl.BlockSpec((1, D), lambda i: (0, 0))],
        out_specs=(tile, pl.BlockSpec((3, 8, D), lambda i: (0, 0, 0))),
        scratch_shapes=[VMEM((TM, D), F32)],
        compiler_params=_cp(("arbitrary",), vmem_mb=56),
    )(dp, wing, x, dx2, mod, g_pre)


RS_ORDER = (5, 4, 3, 2, 7, 6, 1, 0)
RS_HALF = D // 2


def _in_bwd_dw(ht, dp, order, smalls):
    T = ht.shape[1]
    tmw = min(4 * TMB, T)
    nb = T // tmw
    nu = 2 * NSEG

    def body(order_ref, ht_ref, dp_ref, sm_ref, g_ref, smg_ref, acc, sbuf, rsib, rici, send_sems, recv_sems,
             ag_send, ag_recv, ag_loc):
        j = pl.program_id(0)
        i = pl.program_id(1)
        x, y, c = _me()
        sibling = (x, y, 1 - c)
        chips = [(1 - x, y), (x, 1 - y), (1 - x, 1 - y)]
        ag_start, ag_relay, ag_finish = _allgather_phases([sm_ref], lambda a, slot: smg_ref.at[slot],
                                                          ag_send, ag_recv, ag_loc)

        def to_sibling(s):
            return pltpu.make_async_remote_copy(src_ref=sbuf.at[s], dst_ref=rsib.at[s], send_sem=send_sems.at[s],
                                                recv_sem=recv_sems.at[s], device_id=sibling, device_id_type=MESH)

        def to_owner(t):
            return pltpu.make_async_remote_copy(src_ref=sbuf.at[8 + t], dst_ref=rici.at[t], send_sem=send_sems.at[8 + t],
                                                recv_sem=recv_sems.at[8 + t], device_id=(*chips[t // 2], c),
                                                device_id_type=MESH)

        @pl.when((j == 0) & (i == 0))
        def _():
            ag_start()

        @pl.when((j == 4) & (i == 0))
        def _():
            ag_relay()

        @pl.when(i == 0)
        def _():
            acc[...] = jnp.zeros_like(acc)
        acc[...] += jnp.dot(ht_ref[...], dp_ref[...], preferred_element_type=F32)

        for jj in range(nu):
            chip, half, mine = jj // 4, (jj % 4) // 2, jj % 2
            s = 2 * chip + half

            @pl.when((j == jj) & (i == nb - 1))
            def _(chip=chip, half=half, mine=mine, s=s):
                if not mine:
                    sbuf[s] = acc[...].astype(BF16)
                    to_sibling(s).start()
                elif chip < 3:
                    to_sibling(s).wait_recv()
                    sbuf[8 + s] = (acc[...] + rsib[s].astype(F32)).astype(BF16)
                    to_owner(s).start()
                else:
                    to_sibling(s).wait_recv()
                    g = acc[...] + rsib[s].astype(F32)
                    for q in range(3):
                        to_owner(2 * q + half).wait_recv()
                        g = g + rici[2 * q + half].astype(F32)
                    g_ref[:, half * RS_HALF:(half + 1) * RS_HALF] = g

        @pl.when((j == nu - 1) & (i == nb - 1))
        def _():
            for s in range(8):
                to_sibling(s).wait_send()
            for t in range(6):
                to_owner(t).wait_send()
            ag_finish()

    hbm = pl.BlockSpec(memory_space=ANY)
    return pl.pallas_call(
        body, name="in_bwd_dw",
        out_shape=(jax.ShapeDtypeStruct((D, D), F32), jax.ShapeDtypeStruct((NDEV, NLATE, D), F32)),
        grid_spec=pltpu.PrefetchScalarGridSpec(
            num_scalar_prefetch=1, grid=(nu, nb),
            in_specs=[pl.BlockSpec((D, tmw), lambda j, i, o: (0, i)),
                      pl.BlockSpec((None, tmw, RS_HALF), lambda j, i, o: (o[2 * (j // 4) + j % 2], i, (j % 4) // 2)),
                      hbm],
            out_specs=(pl.BlockSpec((D, D), lambda j, i, o: (0, 0)), hbm),
            scratch_shapes=[VMEM((D, RS_HALF), F32), VMEM((14, D, RS_HALF), BF16), VMEM((8, D, RS_HALF), BF16),
                            VMEM((6, D, RS_HALF), BF16), pltpu.SemaphoreType.DMA((14,)), pltpu.SemaphoreType.DMA((14,)),
                            pltpu.SemaphoreType.DMA((1, 7)), pltpu.SemaphoreType.DMA((1, 7)), pltpu.SemaphoreType.DMA((1,))]),
        compiler_params=_cp(("arbitrary", "arbitrary"), vmem_mb=56),
    )(order, ht, dp, smalls)


def _pack_early(st_a, st_b, st_m, gws, gbs):
    def body(sa_ref, sb_ref, sm_ref, gws_ref, gbs_ref, out_ref):
        out_ref[...] = jnp.zeros_like(out_ref)
        fold = lambda v: jnp.sum(v, axis=0, keepdims=True)
        out_ref[0:1, :] = fold(sm_ref[2])
        out_ref[1:2, :] = fold(sa_ref[2])
        out_ref[2:3, :] = fold(sa_ref[0])
        out_ref[3:4, :] = fold(sa_ref[1])
        out_ref[4:5, :] = fold(sb_ref[0])
        out_ref[5:6, :] = fold(sb_ref[1])
        out_ref[7:8, :] = fold(sm_ref[1])
        out_ref[8:9, :] = fold(sm_ref[0])
        tri = lax.broadcasted_iota(jnp.int32, (CHUNK, CHUNK), 0) >= lax.broadcasted_iota(jnp.int32, (CHUNK, CHUNK), 1)
        for h in range(HEADS):
            out_ref[16:16 + CHUNK, h * CHUNK:(h + 1) * CHUNK] = jnp.where(tri, gws_ref[h], 0.0)
            out_ref[6:7, h * CHUNK:(h + 1) * CHUNK] = fold(gbs_ref[h].T)

    vm = pl.BlockSpec(memory_space=VMEM)
    return pl.pallas_call(
        body, name="pack_early", out_shape=jax.ShapeDtypeStruct((NEARLY, D), F32),
        in_specs=[vm] * 5, out_specs=vm, compiler_params=_cp(vmem_mb=32),
    )(st_a, st_b, st_m, gws, gbs)


def _pack_late(st_x, gcw):
    def body(sx_ref, gcw_ref, out_ref):
        out_ref[...] = jnp.zeros_like(out_ref)
        fold = lambda v: jnp.sum(v, axis=0, keepdims=True)
        for r in range(3):
            out_ref[r:r + 1, :] = fold(sx_ref[r])
        for k in range(KW):
            out_ref[8 + k:9 + k, :] = fold(gcw_ref[k])

    vm = pl.BlockSpec(memory_space=VMEM)
    return pl.pallas_call(
        body, name="pack_late", out_shape=jax.ShapeDtypeStruct((NLATE, D), F32),
        in_specs=[vm] * 2, out_specs=vm, compiler_params=_cp(vmem_mb=32),
    )(st_x, gcw)


def _pack_params(name, b_ada, g_pre, conv_b, lg, lb, sg, sb, b_sgu, g_final, w_sgu_t):
    def body(ba_ref, *refs):
        rows, ws_ref, out_ref = refs[:8], refs[8], refs[9]
        out_ref[...] = jnp.zeros_like(out_ref)
        for r in range(3):
            out_ref[r:r + 1, :] = ba_ref[:, r * D:(r + 1) * D]
        for r, ref in enumerate(rows):
            out_ref[3 + r:4 + r, :] = ref[...]
        out_ref[ROW_WS:ROW_WS + CHUNK, :] = ws_ref[...]

    vm = pl.BlockSpec(memory_space=VMEM)
    return pl.pallas_call(
        body, name=name, out_shape=jax.ShapeDtypeStruct((NSMALL, D), F32),
        in_specs=[vm] * 10, out_specs=vm, compiler_params=_cp(vmem_mb=32),
    )(b_ada, g_pre, conv_b, lg, lb, sg, sb, b_sgu, g_final, w_sgu_t)


def _adam(w, g, m, v):
    m2 = ADAM_B1 * m + (1.0 - ADAM_B1) * g
    v2 = ADAM_B2 * v + (1.0 - ADAM_B2) * (g * g)
    m_hat = m2 / (1.0 - ADAM_B1 ** ADAM_STEP)
    v_hat = v2 / (1.0 - ADAM_B2 ** ADAM_STEP)
    delta = -ADAM_LR * (m_hat / (jnp.sqrt(v_hat) + ADAM_EPS) + ADAM_WD * w)
    return delta, m2, v2


def _small_finish(eg, lg, w, m, v):
    def body(eg_ref, lg_ref, w_ref, m_ref, v_ref, loss_ref, g_ref, d_ref, m2_ref, v2_ref):
        e = eg_ref[0]
        l = lg_ref[0]
        for s in range(1, NDEV):
            e = e + eg_ref[s]
            l = l + lg_ref[s]
        g_ref[...] = jnp.zeros_like(g_ref)
        g_ref[0:2, :] = l[0:2, :]
        g_ref[2:3, :] = e[0:1, :]
        g_ref[3:4, :] = l[2:3, :]
        g_ref[4:12, :] = e[1:9, :]
        g_ref[ROW_CW:ROW_CW + KW, :] = l[8:8 + KW, :]
        g_ref[ROW_WS:ROW_WS + CHUNK, :] = e[16:16 + CHUNK, :]
        loss_ref[...] = jnp.sum(e[8:9, :], axis=1, keepdims=True)
        d_ref[...], m2_ref[...], v2_ref[...] = _adam(w_ref[...], g_ref[...], m_ref[...], v_ref[...])

    vm = pl.BlockSpec(memory_space=VMEM)
    sd = jax.ShapeDtypeStruct((NSMALL, D), F32)
    return pl.pallas_call(
        body, name="small_finish", out_shape=(jax.ShapeDtypeStruct((1, 1), F32), sd, sd, sd, sd),
        in_specs=[vm] * 5, out_specs=(vm,) * 5, compiler_params=_cp(vmem_mb=32),
    )(eg, lg, w, m, v)


def _ada_grad_adam(ct, dm, w, m, v):
    def body(ct_ref, dm_ref, w_ref, m_ref, v_ref, g_ref, d_ref, m2_ref, v2_ref):
        g = ct_ref[:, 0:1] * dm_ref[0:1, :]
        for b in range(1, NDEV):
            g = g + ct_ref[:, b:b + 1] * dm_ref[b:b + 1, :]
        g_ref[...] = g
        d_ref[...], m2_ref[...], v2_ref[...] = _adam(w_ref[...], g, m_ref[...], v_ref[...])

    vm = pl.BlockSpec(memory_space=VMEM)
    sd = jax.ShapeDtypeStruct(w.shape, F32)
    return pl.pallas_call(
        body, name="ada_grad_adam", out_shape=(sd, sd, sd, sd),
        in_specs=[vm] * 5, out_specs=(vm,) * 4, compiler_params=_cp(vmem_mb=32),
    )(ct, dm, w, m, v)


def _adam_f32(name, g, w, m, v, rows=None):
    R, C = w.shape
    rows = R if rows is None else rows

    def body(g_ref, w_ref, m_ref, v_ref, d_ref, m2_ref, v2_ref):
        d_ref[...], m2_ref[...], v2_ref[...] = _adam(w_ref[...], g_ref[...], m_ref[...], v_ref[...])

    tile = pl.BlockSpec((rows, C), lambda i: (i, 0))
    sd = jax.ShapeDtypeStruct(w.shape, F32)
    return pl.pallas_call(
        body, name=name, grid=(R // rows,), out_shape=(sd, sd, sd), in_specs=[tile] * 4, out_specs=(tile,) * 3,
        compiler_params=_cp(("parallel",), vmem_mb=32),
    )(g, w, m, v)


def _adam_reduce(name, recv, w, m, v, rows, recv_spec):
    R, C = w.shape

    def body(r_ref, w_ref, m_ref, v_ref, g_ref, d_ref, m2_ref, v2_ref):
        g = r_ref[0].astype(F32)
        for s in range(1, NDEV):
            g = g + r_ref[s].astype(F32)
        g_ref[...] = g
        d_ref[...], m2_ref[...], v2_ref[...] = _adam(w_ref[...], g, m_ref[...], v_ref[...])

    tile = pl.BlockSpec((rows, C), lambda i: (i, 0))
    sd = jax.ShapeDtypeStruct((R, C), F32)
    return pl.pallas_call(
        body, name=name, grid=(R // rows,), out_shape=(sd, sd, sd, sd),
        in_specs=[recv_spec, tile, tile, tile], out_specs=(tile,) * 4,
        compiler_params=_cp(("parallel",)),
    )(recv, w, m, v)


def kernel(x, c, w_ada, b_ada, g_pre, w_in, conv_w, conv_b, conv_ln_g, conv_ln_b, w_conv_out, sgu_ln_g, sgu_ln_b, w_sgu, b_sgu, w_sgu_out, w_o, g_final, loss_target, m_w_ada, m_b_ada, m_g_pre, m_w_in, m_conv_w, m_conv_b, m_conv_ln_g, m_conv_ln_b, m_w_conv_out, m_sgu_ln_g, m_sgu_ln_b, m_w_sgu, m_b_sgu, m_w_sgu_out, m_w_o, m_g_final, v_w_ada, v_b_ada, v_g_pre, v_w_in, v_conv_w, v_conv_b, v_conv_ln_g, v_conv_ln_b, v_w_conv_out, v_sgu_ln_g, v_sgu_ln_b, v_w_sgu, v_b_sgu, v_w_sgu_out, v_w_o, v_g_final):
    T = x.shape[1]
    assert T % TMB == 0 and x.shape[2] == D
    xs = x[0]
    tgt = loss_target[0]
    my = 4 * lax.axis_index("x") + 2 * lax.axis_index("y") + lax.axis_index("c")

    mod, cg = _ada_fwd(c, w_ada[0], b_ada)
    gf = g_final.reshape(1, D)
    bst = b_sgu[0].T

    h, ht = _prep_h(xs, mod, g_pre)
    p, wing, wqg, cwg = _in_proj(h, w_in[0], my ^ jnp.array(AG_ORDER, jnp.int32),
                                 w_conv_out[0], w_sgu_out[0], w_o[0], conv_w[0])
    wq = wqg.reshape(3, D, D)
    cw = jnp.broadcast_to(jnp.transpose(cwg, (1, 0, 2)).reshape(KW, 1, D), (KW, 8, D))
    yc, ya = _branch_a_fwd(p, cw, conv_b, conv_ln_g, conv_ln_b, wq)
    yb = _branch_b_fwd(p, w_sgu[0], bst, sgu_ln_g, sgu_ln_b, wq)
    dx2, dya, dyb, dp, gw_o, st_m = _merge_loss(p, ya, yb, xs, tgt, mod, gf, wq)

    dp, gw_so, gws, gbs, st_b = _branch_b_bwd(dp, dyb, p, w_sgu[0], bst, sgu_ln_g, sgu_ln_b, wq)
    dp, dyc, gw_co, st_a = _branch_a_bwd1(dp, dya, yc, p, conv_ln_g, conv_ln_b, wq)
    early = _pack_early(st_a, st_b, st_m, gws, gbs)
    dp, gcw, rq, eg = _branch_a_bwd2(dp, dyc, p, cw, gw_co, gw_so, gw_o, early)
    grad_x, st_x = _in_bwd_dx(dp, wing, xs, dx2, mod, g_pre)
    late = _pack_late(st_x, gcw)
    g_in, lg = _in_bwd_dw(ht, dp, my ^ jnp.array(RS_ORDER, jnp.int32), late)

    def pack(name, b_ada_, g_pre_, conv_b_, lg_, lb_, sg_, sb_, b_sgu_, gfin_, w_sgu_):
        return _pack_params(name, b_ada_, g_pre_, conv_b_, lg_, lb_, sg_, sb_, b_sgu_.reshape(1, D), gfin_.reshape(1, D),
                            jnp.transpose(w_sgu_[0], (1, 0, 2)).reshape(CHUNK, D))

    pw = pack("pack_w", b_ada, g_pre, conv_b, conv_ln_g, conv_ln_b, sgu_ln_g, sgu_ln_b, b_sgu, g_final, w_sgu)
    pm = pack("pack_m", m_b_ada, m_g_pre, m_conv_b, m_conv_ln_g, m_conv_ln_b, m_sgu_ln_g, m_sgu_ln_b, m_b_sgu, m_g_final,
              m_w_sgu)
    pv = pack("pack_v", v_b_ada, v_g_pre, v_conv_b, v_conv_ln_g, v_conv_ln_b, v_sgu_ln_g, v_sgu_ln_b, v_b_sgu, v_g_final,
              v_w_sgu)
    loss11, sg_, sd_, sm_, sv_ = _small_finish(eg, lg, pw, pm, pv)

    def unpack(a):
        return dict(
            b_ada=a[0:3].reshape(1, 3 * D), g_pre=a[3:4], conv_b=a[4:5], conv_ln_g=a[5:6], conv_ln_b=a[6:7],
            sgu_ln_g=a[7:8], sgu_ln_b=a[8:9], b_sgu=a[9:10].reshape(1, HEADS, CHUNK), g_final=a[10],
            w_sgu=jnp.transpose(a[ROW_WS:ROW_WS + CHUNK].reshape(CHUNK, HEADS, CHUNK), (1, 0, 2))[None])
    small = [unpack(a) for a in (sg_, sd_, sm_, sv_)]

    g_cw = lax.dynamic_slice_in_dim(sg_[ROW_CW:ROW_CW + KW], my * 128, 128, axis=1)
    d_cw, m_cw, v_cw = _adam_f32("adam_conv_w", g_cw, conv_w[0], m_conv_w[0], v_conv_w[0])

    wcols = w_ada.shape[2]
    dm_all = jnp.concatenate([lg[:, 0], lg[:, 1], eg[:, 0]], axis=1)
    dm_mine = lax.dynamic_slice_in_dim(dm_all, my * wcols, wcols, axis=1)
    g_ada, d_ada, m_ada, v_ada = _ada_grad_adam(cg.T, dm_mine, w_ada[0], m_w_ada[0], v_w_ada[0])

    d_in, m_in, v_in = _adam_f32("adam_w_in", g_in, w_in[0], m_w_in[0], v_w_in[0], rows=256)
    big = {}
    for j, (nm, w_, m_, v_) in enumerate((("w_conv_out", w_conv_out, m_w_conv_out, v_w_conv_out),
                                          ("w_sgu_out", w_sgu_out, m_w_sgu_out, v_w_sgu_out),
                                          ("w_o", w_o, m_w_o, v_w_o))):
        big[nm] = _adam_reduce("adam_" + nm, rq, w_[0], m_[0], v_[0], 128,
                               pl.BlockSpec((NDEV, None, 128, D), lambda i, j=j: (0, j, 0, 0)))

    per = {
        "w_ada": tuple(a[None] for a in (g_ada, d_ada, m_ada, v_ada)),
        "w_in": tuple(a[None] for a in (g_in, d_in, m_in, v_in)),
        "conv_w": tuple(a[None] for a in (g_cw, d_cw, m_cw, v_cw)),
    }
    for nm in ("w_conv_out", "w_sgu_out", "w_o"):
        per[nm] = tuple(a[None] for a in big[nm])
    for nm in ("b_ada", "g_pre", "conv_b", "conv_ln_g", "conv_ln_b", "sgu_ln_g", "sgu_ln_b", "w_sgu", "b_sgu", "g_final"):
        per[nm] = tuple(s[nm] for s in small)

    order = ["w_ada", "b_ada", "g_pre", "w_in", "conv_w", "conv_b", "conv_ln_g", "conv_ln_b", "w_conv_out",
             "sgu_ln_g", "sgu_ln_b", "w_sgu", "b_sgu", "w_sgu_out", "w_o", "g_final"]
    outs = [loss11.reshape(()), grad_x[None]]
    for part in range(4):
        outs += [per[nm][part] for nm in order]
    return tuple(outs)
```

```python
import jax
import jax.numpy as jnp
from jax import lax
from jax.experimental import pallas as pl
from jax.experimental.pallas import tpu as pltpu

F32 = jnp.float32
BF16 = jnp.bfloat16
MESH = pl.DeviceIdType.MESH
VMEM = pltpu.VMEM
ANY = pl.ANY

D = 1024
NDEV = 8
NSEG = 8
HEADS = 8
HD = 128
CHUNK = 128
KW = 31
HALO = 32
EPS = 1e-6
TM = 256
TMB = 512
RB = 32
UNROLL = 4
RBC = 32
TAPG = 4
EXT = TM + HALO
NSMALL = 176
NEARLY = 144
NLATE = 40
ROW_CW = 16
ROW_WS = 48
W_CO, W_SO, W_O = 0, 1, 2

ADAM_LR = 0.001
ADAM_B1 = 0.9
ADAM_B2 = 0.999
ADAM_EPS = 1e-08
ADAM_WD = 0.01
ADAM_STEP = 10

INV_SQRT2 = 0.7071067811865476
INV_SQRT_2PI = 0.3989422804014327

NT = (((1,), (1,)), ((), ()))
TN = (((0,), (0,)), ((), ()))


def _cp(sem=None, vmem_mb=48):
    return pltpu.CompilerParams(dimension_semantics=sem, vmem_limit_bytes=vmem_mb * 1024 * 1024)


def _me():
    return lax.axis_index("x"), lax.axis_index("y"), lax.axis_index("c")


def _slot(px, py, pc):
    return 4 * px + 2 * py + pc


def _xor_peer(k):
    x, y, c = _me()
    return (1 - x if k & 4 else x, 1 - y if k & 2 else y, 1 - c if k & 1 else c)


def _allgather_phases(srcs, dst_at, send_sems, recv_sems, loc_sems):
    x, y, c = _me()
    me = (x, y, c)
    sibling = (x, y, 1 - c)
    chips = [(1 - x, y), (x, 1 - y), (1 - x, 1 - y)]
    na = len(srcs)

    def copy(a, k, block, to, src=None):
        d = dst_at(a, _slot(*block))
        return pltpu.make_async_remote_copy(
            src_ref=d if src is None else src, dst_ref=d,
            send_sem=send_sems.at[a, k], recv_sem=recv_sems.at[a, k],
            device_id=to, device_id_type=MESH)

    local = [pltpu.make_async_copy(srcs[a], dst_at(a, _slot(*me)), loc_sems.at[a]) for a in range(na)]
    first = []
    for a in range(na):
        first.append(copy(a, 0, me, sibling, src=srcs[a]))
        for j, chip in enumerate(chips):
            first.append(copy(a, 1 + j, me, (*chip, c), src=srcs[a]))
    passed = [copy(a, 4 + j, (*chip, c), sibling) for j, chip in enumerate(chips) for a in range(na)]

    def start():
        for cp in local + first:
            cp.start()

    def relay():
        for j, chip in enumerate(chips):
            for a in range(na):
                copy(a, 1 + j, (*chip, c), me).wait_recv()
                passed[j * na + a].start()

    def finish():
        for a in range(na):
            copy(a, 0, sibling, me).wait_recv()
        for j, chip in enumerate(chips):
            for a in range(na):
                copy(a, 4 + j, (*chip, 1 - c), me).wait_recv()
        for cp in first + passed:
            cp.wait_send()
        for cp in local:
            cp.wait()

    return start, relay, finish


def _allgather(srcs, dst_at, send_sems, recv_sems, loc_sems):
    start, relay, finish = _allgather_phases(srcs, dst_at, send_sems, recv_sems, loc_sems)
    start()
    relay()
    finish()


def _owner_exchange(gq, rq_ref, x_send, x_recv, x_loc):
    mx, my_, mc = _me()
    me = _slot(mx, my_, mc)

    def rows_of(a, slot):
        return gq[a].at[pl.ds(pl.multiple_of(slot * 128, 128), 128), :]

    def exchange(k, a, recv):
        px, py, pc = _xor_peer(k)
        peer = _slot(px, py, pc)
        return pltpu.make_async_remote_copy(
            src_ref=rows_of(a, me if recv else peer), dst_ref=rq_ref.at[peer if recv else me, a],
            send_sem=x_send.at[a, k - 1], recv_sem=x_recv.at[a, k - 1],
            device_id=(px, py, pc), device_id_type=MESH)

    local = [pltpu.make_async_copy(rows_of(a, me), rq_ref.at[me, a], x_loc.at[a]) for a in range(len(gq))]

    def start():
        for cp in local:
            cp.start()
        for k in range(1, NDEV):
            for a in range(len(gq)):
                exchange(k, a, False).start()

    def wait():
        for k in range(1, NDEV):
            for a in range(len(gq)):
                exchange(k, a, True).wait_recv()
        for k in range(1, NDEV):
            for a in range(len(gq)):
                exchange(k, a, False).wait_send()
        for cp in local:
            cp.wait()

    return start, wait


def _sig(v):
    return jax.nn.sigmoid(v)


def _gelu_parts(v):
    cdf = 0.5 * (1.0 + lax.erf(v * INV_SQRT2))
    pdf = jnp.exp(-0.5 * v * v) * INV_SQRT_2PI
    return v * cdf, cdf + v * pdf


def _ln_stats(v):
    mu = jnp.mean(v, axis=-1, keepdims=True)
    xc = v - mu
    var = jnp.mean(xc * xc, axis=-1, keepdims=True)
    rs = lax.rsqrt(var + EPS)
    return xc * rs, rs


def _ln_bwd(dn, n, rs):
    return rs * (dn - jnp.mean(dn, axis=-1, keepdims=True) - n * jnp.mean(dn * n, axis=-1, keepdims=True))


def _fold8(v):
    acc = v[0:8]
    for r in range(8, v.shape[0], 8):
        acc = acc + v[r:r + 8]
    return acc


def _rows(rb, n=RB):
    return pl.ds(pl.multiple_of(rb * n, n), n)


def _ld(ref, rows):
    return ref[rows, :].astype(F32)


def _blocks(n, body, init=0, unroll=UNROLL):
    def trip(t, carry):
        for u in range(unroll):
            carry = body(t * unroll + u, carry)
        return carry
    return lax.fori_loop(0, n // unroll, trip, init)


def _shift_copies(s_ref):
    n = EXT - 8
    for m in range(1, 8):
        for c0 in range(0, n, 56):
            s_ref[m, c0:c0 + 56, :] = s_ref[0, c0 + m:c0 + m + 56, :]


def _ada_fwd(c, w_ada, b_ada):
    wcols = w_ada.shape[1]

    def body(c_ref, w_ref, b_ref, mod_ref, call_ref, cdst, msrc, mdst, c_send, c_recv, m_send, m_recv):
        x, y, c = _me()
        me = _slot(x, y, c)

        def all_to_all(src_of, dst, ss, rs):
            def cp(k, recv):
                px, py, pc = _xor_peer(k)
                peer = _slot(px, py, pc)
                return pltpu.make_async_remote_copy(
                    src_ref=src_of(me if recv else peer), dst_ref=dst.at[peer if recv else me],
                    send_sem=ss.at[k - 1], recv_sem=rs.at[k - 1], device_id=(px, py, pc), device_id_type=MESH)
            for k in range(1, NDEV):
                cp(k, False).start()
            for k in range(1, NDEV):
                cp(k, True).wait_recv()
            for k in range(1, NDEV):
                cp(k, False).wait_send()

        cdst[me] = c_ref[...]
        all_to_all(lambda s: c_ref, cdst, c_send, c_recv)
        for b in range(NDEV):
            call_ref[b:b + 1, :] = cdst[b]
        m = jnp.dot(call_ref[...], w_ref[...], preferred_element_type=F32, precision=lax.Precision.HIGHEST)
        for b in range(NDEV):
            msrc[b] = m[b:b + 1, :]
        mdst[me] = msrc[me]
        all_to_all(lambda s: msrc.at[s], mdst, m_send, m_recv)
        full = jnp.concatenate([mdst[k] for k in range(NDEV)], axis=1) + b_ref[...]
        for r in range(3):
            mod_ref[r:r + 1, :] = full[:, r * D:(r + 1) * D]

    vm = pl.BlockSpec(memory_space=VMEM)
    return pl.pallas_call(
        body, name="ada_fwd",
        out_shape=(jax.ShapeDtypeStruct((3, D), F32), jax.ShapeDtypeStruct((NDEV, D), F32)),
        in_specs=[vm, vm, vm], out_specs=(vm, vm),
        scratch_shapes=[VMEM((NDEV, 1, D), F32), VMEM((NDEV, 1, wcols), F32), VMEM((NDEV, 1, wcols), F32),
                        pltpu.SemaphoreType.DMA((7,)), pltpu.SemaphoreType.DMA((7,)),
                        pltpu.SemaphoreType.DMA((7,)), pltpu.SemaphoreType.DMA((7,))],
        compiler_params=_cp(vmem_mb=32),
    )(c, w_ada, b_ada)


def _prep_h(x, mod, g_pre):
    T = x.shape[0]

    def body(x_ref, mod_ref, g_ref, h_ref, ht_ref):
        def blk(rb, carry):
            rows = _rows(rb)
            xv = x_ref[rows, :]
            r = lax.rsqrt(jnp.mean(xv * xv, axis=-1, keepdims=True) + EPS)
            h_ref[rows, :] = ((xv * r) * g_ref[...] * (1.0 + mod_ref[1:2, :]) + mod_ref[0:1, :]).astype(BF16)
            return carry
        _blocks(TMB // RB, blk)
        ht_ref[...] = h_ref[...].T

    return pl.pallas_call(
        body, name="prep_h", grid=(T // TMB,),
        out_shape=(jax.ShapeDtypeStruct((T, D), BF16), jax.ShapeDtypeStruct((D, T), BF16)),
        in_specs=[pl.BlockSpec((TMB, D), lambda i: (i, 0)), pl.BlockSpec((3, D), lambda i: (0, 0)),
                  pl.BlockSpec((1, D), lambda i: (0, 0))],
        out_specs=(pl.BlockSpec((TMB, D), lambda i: (i, 0)), pl.BlockSpec((D, TMB), lambda i: (0, i))),
        compiler_params=_cp(("parallel",)),
    )(x, mod, g_pre)


def _ag_blocks(x, y, c):
    a = 1 - c
    n1 = (x ^ a, y ^ c)
    n2 = (x ^ c, y ^ a)
    dg = (1 - x, 1 - y)
    return [(x, y, c), (x, y, a), (*n1, c), (*n2, c), (*n1, a), (*n2, a), (*dg, c), (*dg, a)]


def _in_proj(h, w_in, order, w_co, w_so, w_o, conv_w):
    T = h.shape[0]
    tmp = min(2 * TMB, T)
    nb = T // tmp

    def body(order_ref, h_ref, w_ref, wco_ref, wso_ref, wo_ref, cw_ref, p_ref, wing_ref, wqg_ref, cwg_ref,
             wbuf, wq_st, send_sems, recv_sems, out_sems, ag_send, ag_recv, ag_loc):
        j = pl.program_id(0)
        i = pl.program_id(1)
        blocks = _ag_blocks(*_me())
        me, sibling, n1, n2, n1o, n2o, dg, dgo = blocks
        ag_start, ag_relay, ag_finish = _allgather_phases(
            [wq_st, cw_ref], lambda a, slot: wqg_ref.at[:, slot] if a == 0 else cwg_ref.at[slot], ag_send, ag_recv, ag_loc)

        def copy(k, block, to):
            d = wbuf.at[_slot(*block)]
            return pltpu.make_async_remote_copy(src_ref=d, dst_ref=d, send_sem=send_sems.at[k], recv_sem=recv_sems.at[k],
                                                device_id=to, device_id_type=MESH)

        def writeback(jj):
            s = _slot(*blocks[jj])
            return pltpu.make_async_copy(wbuf.at[s], wing_ref.at[s], out_sems.at[jj])

        sends = [copy(0, me, sibling), copy(1, me, n1), copy(2, me, n2)]
        relay = copy(3, n1, n2)
        passed = [copy(4, n1, sibling), copy(5, n2, sibling), copy(6, dg, sibling)]
        arrivals = {1: [(0, sibling)], 2: [(1, n1), (2, n2)], 4: [(5, n1o)], 5: [(4, n2o)], 6: [(3, dg)], 7: [(6, dgo)]}

        @pl.when((j == 0) & (i == 0))
        def _():
            wbuf[_slot(*me)] = w_ref[...].astype(BF16)
            for cp in sends:
                cp.start()
            writeback(0).start()
            wq_st[W_CO] = wco_ref[...].astype(BF16)
            wq_st[W_SO] = wso_ref[...].astype(BF16)
            wq_st[W_O] = wo_ref[...].astype(BF16)
            ag_start()

        for jj in range(1, NSEG):
            @pl.when((j == jj) & (i == 0))
            def _(jj=jj):
                for k, block in arrivals.get(jj, []):
                    copy(k, block, me).wait_recv()
                if jj == 2:
                    relay.start()
                    passed[0].start()
                    passed[1].start()
                    writeback(3).start()
                if jj == 6:
                    passed[2].start()
                if jj != 3:
                    writeback(jj).start()
                if jj == NSEG - 1:
                    ag_relay()

        p_ref[...] = jnp.dot(h_ref[...], wbuf[order_ref[j]], preferred_element_type=F32).astype(BF16)

        @pl.when((j == NSEG - 1) & (i == nb - 1))
        def _():
            for cp in sends + [relay] + passed:
                cp.wait_send()
            for jj in range(NSEG):
                writeback(jj).wait()
            ag_finish()

    vm = pl.BlockSpec(memory_space=VMEM)
    hbm = pl.BlockSpec(memory_space=ANY)
    return pl.pallas_call(
        body, name="in_proj",
        out_shape=(jax.ShapeDtypeStruct((T, NSEG * D), BF16), jax.ShapeDtypeStruct((NSEG, D, D), BF16),
                   jax.ShapeDtypeStruct((3, NDEV, 128, D), BF16), jax.ShapeDtypeStruct((NDEV, KW, 128), F32)),
        grid_spec=pltpu.PrefetchScalarGridSpec(
            num_scalar_prefetch=1, grid=(NSEG, nb),
            in_specs=[pl.BlockSpec((tmp, D), lambda j, i, o: (i, 0)), vm, vm, vm, vm, vm],
            out_specs=(pl.BlockSpec((tmp, D), lambda j, i, o: (i, o[j])), hbm, hbm, hbm),
            scratch_shapes=[VMEM((NSEG, D, D), BF16), VMEM((3, 128, D), BF16),
                            pltpu.SemaphoreType.DMA((7,)), pltpu.SemaphoreType.DMA((7,)), pltpu.SemaphoreType.DMA((NSEG,)),
                            pltpu.SemaphoreType.DMA((2, 7)), pltpu.SemaphoreType.DMA((2, 7)), pltpu.SemaphoreType.DMA((2,))]),
        compiler_params=_cp(("arbitrary", "arbitrary")),
    )(order, h, w_in, w_co, w_so, w_o, conv_w)


def _fill_a_ext(i, s_ref, val_ref, glu_ref, hval_ref, hglu_ref):
    ah = hval_ref[...].astype(F32) * _sig(hglu_ref[...].astype(F32))
    s_ref[0, 0:HALO, :] = jnp.where(i > 0, ah, 0.0)

    def fill(rb, carry):
        rows = _rows(rb)
        s_ref[0, pl.ds(pl.multiple_of(HALO + rb * RB, RB), RB), :] = _ld(val_ref, rows) * _sig(_ld(glu_ref, rows))
        return carry
    lax.fori_loop(0, TM // RB, fill, 0)


def _halo_prev(seg):
    hb = TM // HALO
    return pl.BlockSpec((HALO, D), lambda i: (jnp.maximum(i * hb - 1, 0), seg))


def _branch_a_fwd(p, cw, conv_b, lg, lb, wq):
    T = p.shape[0]

    def body(val_ref, glu_ref, z_ref, hval_ref, hglu_ref, cw_ref, cb_ref, lg_ref, lb_ref, w_ref,
             yc_ref, ya_ref, s_ref, yp_ref):
        i = pl.program_id(0)
        _fill_a_ext(i, s_ref, val_ref, glu_ref, hval_ref, hglu_ref)
        _shift_copies(s_ref)

        def conv(rb, carry):
            r0 = rb * RBC
            accs = [jnp.broadcast_to(cb_ref[...], (8, D))] * (RBC // 8)
            for k in range(KW):
                o = 2 + k
                w = cw_ref[k]
                accs = [acc + w * s_ref[o % 8, pl.ds(pl.multiple_of(r0 + 8 * (o // 8 + g), 8), 8), :]
                        for g, acc in enumerate(accs)]
            for g, acc in enumerate(accs):
                yc_ref[pl.ds(pl.multiple_of(r0 + 8 * g, 8), 8), :] = acc
            return carry
        lax.fori_loop(0, TM // RBC, conv, 0)

        def post(rb, carry):
            rows = _rows(rb)
            n, _ = _ln_stats(yc_ref[rows, :])
            l = n * lg_ref[...] + lb_ref[...]
            z = _ld(z_ref, rows)
            yp_ref[rows, :] = ((l * _sig(l)) * (z * _sig(z))).astype(BF16)
            return carry
        _blocks(TM // RB, post)
        ya_ref[...] = jnp.dot(yp_ref[...], w_ref[...], preferred_element_type=F32)

    tile = lambda seg: pl.BlockSpec((TM, D), lambda i: (i, seg))
    row = pl.BlockSpec((1, D), lambda i: (0, 0))
    return pl.pallas_call(
        body, name="branch_a_fwd", grid=(T // TM,),
        out_shape=(jax.ShapeDtypeStruct((T, D), F32), jax.ShapeDtypeStruct((T, D), F32)),
        in_specs=[tile(0), tile(1), tile(2), _halo_prev(0), _halo_prev(1),
                  pl.BlockSpec((KW, 8, D), lambda i: (0, 0, 0)), row, row, row,
                  pl.BlockSpec((None, D, D), lambda i: (W_CO, 0, 0))],
        out_specs=(pl.BlockSpec((TM, D), lambda i: (i, 0)), pl.BlockSpec((TM, D), lambda i: (i, 0))),
        scratch_shapes=[VMEM((8, EXT, D), F32), VMEM((TM, D), BF16)],
        compiler_params=_cp(("parallel",)),
    )(p, p, p, p, p, cw, conv_b, lg, lb, wq)


def _masked_ws(ws_ref, wt_ref):
    tri = lax.broadcasted_iota(jnp.int32, (CHUNK, CHUNK), 0) >= lax.broadcasted_iota(jnp.int32, (CHUNK, CHUNK), 1)
    for h in range(HEADS):
        wt_ref[h] = jnp.where(tri, ws_ref[h], 0.0).astype(BF16)


def _sgu_mix(wt_ref, vl_ref, bst_ref, s_ref):
    for ck in range(TM // CHUNK):
        r = slice(ck * CHUNK, (ck + 1) * CHUNK)
        for h in range(HEADS):
            cs = slice(h * HD, (h + 1) * HD)
            s_ref[r, cs] = jnp.dot(wt_ref[h], vl_ref[r, cs], preferred_element_type=F32) + bst_ref[:, h:h + 1]


def _branch_b_fwd(p, ws, bst, sg, sb, wq):
    T = p.shape[0]

    def body(pu_ref, pv_ref, pz_ref, ws_ref, bst_ref, sg_ref, sb_ref, w_ref, yb_ref,
             wt_ref, vl_ref, t_ref, s_ref, yp_ref):
        _masked_ws(ws_ref, wt_ref)

        def pre(rb, carry):
            rows = _rows(rb)
            vg, _ = _gelu_parts(_ld(pv_ref, rows))
            vn, _ = _ln_stats(vg)
            vl_ref[rows, :] = (vn * sg_ref[...] + sb_ref[...]).astype(BF16)
            u, _ = _gelu_parts(_ld(pu_ref, rows))
            z = _ld(pz_ref, rows)
            t_ref[rows, :] = u * (z * _sig(z))
            return carry
        _blocks(TM // RB, pre)
        _sgu_mix(wt_ref, vl_ref, bst_ref, s_ref)

        def post(rb, carry):
            rows = _rows(rb)
            yp_ref[rows, :] = (t_ref[rows, :] * s_ref[rows, :]).astype(BF16)
            return carry
        _blocks(TM // RB, post)
        yb_ref[...] = jnp.dot(yp_ref[...], w_ref[...], preferred_element_type=F32)

    tile = lambda seg: pl.BlockSpec((TM, D), lambda i: (i, seg))
    row = pl.BlockSpec((1, D), lambda i: (0, 0))
    return pl.pallas_call(
        body, name="branch_b_fwd", grid=(T // TM,),
        out_shape=jax.ShapeDtypeStruct((T, D), F32),
        in_specs=[tile(3), tile(4), tile(5), pl.BlockSpec((HEADS, CHUNK, CHUNK), lambda i: (0, 0, 0)),
                  pl.BlockSpec((CHUNK, HEADS), lambda i: (0, 0)), row, row,
                  pl.BlockSpec((None, D, D), lambda i: (W_SO, 0, 0))],
        out_specs=pl.BlockSpec((TM, D), lambda i: (i, 0)),
        scratch_shapes=[VMEM((HEADS, CHUNK, CHUNK), BF16), VMEM((TM, D), BF16), VMEM((TM, D), F32),
                        VMEM((TM, D), F32), VMEM((TM, D), BF16)],
        compiler_params=_cp(("parallel",)),
    )(p, p, p, ws, bst, sg, sb, wq)


def _merge_loss(p, ya, yb, x, tgt, mod, g_final, wq):
    T = x.shape[0]
    nt = T // TM

    def body(ga_ref, gb_ref, ya_ref, yb_ref, x_ref, t_ref, mod_ref, gf_ref, w_ref,
             dx2_ref, dya_ref, dyb_ref, dp_ref, gwo_ref, st_ref,
             mrg_s, out_s, dout_s, dm_s, gw_acc):
        i = pl.program_id(0)

        @pl.when(i == 0)
        def _():
            gw_acc[...] = jnp.zeros_like(gw_acc)
            st_ref[...] = jnp.zeros_like(st_ref)

        def merge(rb, carry):
            rows = _rows(rb)
            mrg_s[rows, :] = (_sig(_ld(ga_ref, rows)) * ya_ref[rows, :] + _sig(_ld(gb_ref, rows)) * yb_ref[rows, :]).astype(BF16)
            return carry
        lax.fori_loop(0, TM // RB, merge, 0)
        out_s[...] = jnp.dot(mrg_s[...], w_ref[...], preferred_element_type=F32)

        def head(rb, carry):
            loss, gg, dg = carry
            rows = _rows(rb)
            gate = mod_ref[2:3, :]
            gf = gf_ref[...]
            out = out_s[rows, :]
            x2 = x_ref[rows, :] + gate * out
            r2 = lax.rsqrt(jnp.mean(x2 * x2, axis=-1, keepdims=True) + EPS)
            x2n = x2 * r2
            diff = x2n * gf - t_ref[rows, :]
            dy = diff * (1.0 / D)
            dx2n = dy * gf
            dx2 = r2 * (dx2n - x2n * jnp.mean(dx2n * x2n, axis=-1, keepdims=True))
            dx2_ref[rows, :] = dx2
            dout_s[rows, :] = (dx2 * gate).astype(BF16)
            return loss + _fold8(diff * diff), gg + _fold8(dy * x2n), dg + _fold8(dx2 * out)
        zero = jnp.zeros((8, D), F32)
        loss, gg, dg = _blocks(TM // RB, head, (zero, zero, zero))
        st_ref[0] = st_ref[0] + loss * (0.5 / D)
        st_ref[1] = st_ref[1] + gg
        st_ref[2] = st_ref[2] + dg

        dm_s[...] = lax.dot_general(dout_s[...], w_ref[...], NT, preferred_element_type=F32)
        gw_acc[...] += lax.dot_general(mrg_s[...], dout_s[...], TN, preferred_element_type=F32)

        def split(rb, carry):
            rows = _rows(rb)
            dm = dm_s[rows, :]
            sa = _sig(_ld(ga_ref, rows))
            sb = _sig(_ld(gb_ref, rows))
            dya_ref[rows, :] = (dm * sa).astype(BF16)
            dyb_ref[rows, :] = (dm * sb).astype(BF16)
            dp_ref[0, rows, :] = (dm * ya_ref[rows, :] * (sa * (1.0 - sa))).astype(BF16)
            dp_ref[1, rows, :] = (dm * yb_ref[rows, :] * (sb * (1.0 - sb))).astype(BF16)
            return carry
        lax.fori_loop(0, TM // RB, split, 0)

        @pl.when(i == nt - 1)
        def _():
            gwo_ref[...] = gw_acc[...].astype(BF16)

    tile = pl.BlockSpec((TM, D), lambda i: (i, 0))
    return pl.pallas_call(
        body, name="merge_loss", grid=(nt,),
        out_shape=(jax.ShapeDtypeStruct((T, D), F32), jax.ShapeDtypeStruct((T, D), BF16), jax.ShapeDtypeStruct((T, D), BF16),
                   jax.ShapeDtypeStruct((NSEG, T, D), BF16), jax.ShapeDtypeStruct((D, D), BF16),
                   jax.ShapeDtypeStruct((3, 8, D), F32)),
        in_specs=[pl.BlockSpec((TM, D), lambda i: (i, 6)), pl.BlockSpec((TM, D), lambda i: (i, 7)), tile, tile, tile, tile,
                  pl.BlockSpec((3, D), lambda i: (0, 0)), pl.BlockSpec((1, D), lambda i: (0, 0)),
                  pl.BlockSpec((None, D, D), lambda i: (W_O, 0, 0))],
        out_specs=(tile, tile, tile, pl.BlockSpec((2, TM, D), lambda i: (3, i, 0)),
                   pl.BlockSpec((D, D), lambda i: (0, 0)), pl.BlockSpec((3, 8, D), lambda i: (0, 0, 0))),
        scratch_shapes=[VMEM((TM, D), BF16), VMEM((TM, D), F32), VMEM((TM, D), BF16), VMEM((TM, D), F32), VMEM((D, D), F32)],
        compiler_params=_cp(("arbitrary",)),
    )(p, p, ya, yb, x, tgt, mod, g_final, wq)


def _branch_b_bwd(dp, dyb, p, ws, bst, sg, sb, wq):
    T = p.shape[0]
    nt = T // TM

    def body(dp_in, dyb_ref, pu_ref, pv_ref, pz_ref, ws_ref, bst_ref, sg_ref, sb_ref, w_ref,
             dp_ref, gw_ref, gws_ref, gbs_ref, st_ref,
             wt_ref, d_s, vl_s, vn_s, gpv_s, rs_s, s_s, ds_s, ds32_s, yp_s, dvl_s, gw_acc):
        del dp_in
        i = pl.program_id(0)

        @pl.when(i == 0)
        def _():
            gw_acc[...] = jnp.zeros_like(gw_acc)
            gws_ref[...] = jnp.zeros_like(gws_ref)
            gbs_ref[...] = jnp.zeros_like(gbs_ref)
            st_ref[...] = jnp.zeros_like(st_ref)
            _masked_ws(ws_ref, wt_ref)

        d_s[...] = lax.dot_general(dyb_ref[...], w_ref[...], NT, preferred_element_type=F32)

        def pre(rb, carry):
            rows = _rows(rb)
            vg, gpv = _gelu_parts(_ld(pv_ref, rows))
            vn, rs = _ln_stats(vg)
            vl_s[rows, :] = (vn * sg_ref[...] + sb_ref[...]).astype(BF16)
            vn_s[rows, :] = vn
            gpv_s[rows, :] = gpv
            rs_s[rows, :] = rs
            return carry
        _blocks(TM // RB, pre)
        _sgu_mix(wt_ref, vl_s, bst_ref, s_s)

        def mid(rb, carry):
            rows = _rows(rb)
            u, gpu = _gelu_parts(_ld(pu_ref, rows))
            z = _ld(pz_ref, rows)
            sz = _sig(z)
            siluz = z * sz
            d = d_s[rows, :]
            s = s_s[rows, :]
            t = u * siluz
            yp_s[rows, :] = (t * s).astype(BF16)
            dp_ref[0, rows, :] = (d * s * siluz * gpu).astype(BF16)
            ds = d * t
            ds32_s[rows, :] = ds
            ds_s[rows, :] = ds.astype(BF16)
            dp_ref[2, rows, :] = (d * u * s * (sz * (1.0 + z * (1.0 - sz)))).astype(BF16)
            return carry
        lax.fori_loop(0, TM // RB, mid, 0)

        for ck in range(TM // CHUNK):
            r = slice(ck * CHUNK, (ck + 1) * CHUNK)
            for h in range(HEADS):
                cs = slice(h * HD, (h + 1) * HD)
                dsh = ds_s[r, cs]
                dvl_s[r, cs] = lax.dot_general(wt_ref[h], dsh, TN, preferred_element_type=F32)
                gws_ref[h] += lax.dot_general(dsh, vl_s[r, cs], NT, preferred_element_type=F32)
                gbs_ref[h] += ds32_s[r, cs]

        def post(rb, carry):
            g_sg, g_sb = carry
            rows = _rows(rb)
            dvl = dvl_s[rows, :]
            vn = vn_s[rows, :]
            dvg = _ln_bwd(dvl * sg_ref[...], vn, rs_s[rows, :])
            dp_ref[1, rows, :] = (dvg * gpv_s[rows, :]).astype(BF16)
            return g_sg + _fold8(dvl * vn), g_sb + _fold8(dvl)
        zero = jnp.zeros((8, D), F32)
        g_sg, g_sb = _blocks(TM // RB, post, (zero, zero))
        st_ref[0] = st_ref[0] + g_sg
        st_ref[1] = st_ref[1] + g_sb

        gw_acc[...] += lax.dot_general(yp_s[...], dyb_ref[...], TN, preferred_element_type=F32)

        @pl.when(i == nt - 1)
        def _():
            gw_ref[...] = gw_acc[...].astype(BF16)

    tile = lambda seg: pl.BlockSpec((TM, D), lambda i: (i, seg))
    row = pl.BlockSpec((1, D), lambda i: (0, 0))
    hh = pl.BlockSpec((HEADS, CHUNK, CHUNK), lambda i: (0, 0, 0))
    return pl.pallas_call(
        body, name="branch_b_bwd", grid=(nt,),
        out_shape=(jax.ShapeDtypeStruct((NSEG, T, D), BF16), jax.ShapeDtypeStruct((D, D), BF16),
                   jax.ShapeDtypeStruct((HEADS, CHUNK, CHUNK), F32), jax.ShapeDtypeStruct((HEADS, CHUNK, HD), F32),
                   jax.ShapeDtypeStruct((2, 8, D), F32)),
        in_specs=[pl.BlockSpec(memory_space=ANY), pl.BlockSpec((TM, D), lambda i: (i, 0)), tile(3), tile(4), tile(5),
                  hh, pl.BlockSpec((CHUNK, HEADS), lambda i: (0, 0)), row, row,
                  pl.BlockSpec((None, D, D), lambda i: (W_SO, 0, 0))],
        out_specs=(pl.BlockSpec((3, TM, D), lambda i: (1, i, 0)), pl.BlockSpec((D, D), lambda i: (0, 0)), hh, hh,
                   pl.BlockSpec((2, 8, D), lambda i: (0, 0, 0))),
        scratch_shapes=[VMEM((HEADS, CHUNK, CHUNK), BF16), VMEM((TM, D), F32), VMEM((TM, D), BF16), VMEM((TM, D), F32),
                        VMEM((TM, D), F32), VMEM((TM, 1), F32), VMEM((TM, D), F32), VMEM((TM, D), BF16), VMEM((TM, D), F32),
                        VMEM((TM, D), BF16), VMEM((TM, D), F32), VMEM((D, D), F32)],
        input_output_aliases={0: 0},
        compiler_params=_cp(("arbitrary",)),
    )(dp, dyb, p, p, p, ws, bst, sg, sb, wq)


def _branch_a_bwd1(dp, dya, yc, p, lg, lb, wq):
    T = p.shape[0]
    nt = T // TM

    def body(dp_in, dya_ref, yc_ref, z_ref, lg_ref, lb_ref, w_ref, dp_ref, dyc_ref, gw_ref, st_ref,
             d_s, yp_s, gw_acc):
        del dp_in
        i = pl.program_id(0)

        @pl.when(i == 0)
        def _():
            gw_acc[...] = jnp.zeros_like(gw_acc)
            st_ref[...] = jnp.zeros_like(st_ref)

        d_s[...] = lax.dot_general(dya_ref[...], w_ref[...], NT, preferred_element_type=F32)

        def blk(rb, carry):
            g_lg, g_lb, g_cb = carry
            rows = _rows(rb)
            n, rs = _ln_stats(yc_ref[rows, :])
            l = n * lg_ref[...] + lb_ref[...]
            sgl = _sig(l)
            sl = l * sgl
            z = _ld(z_ref, rows)
            sz = _sig(z)
            siluz = z * sz
            d = d_s[rows, :]
            yp_s[rows, :] = (sl * siluz).astype(BF16)
            dp_ref[rows, :] = (d * sl * (sz * (1.0 + z * (1.0 - sz)))).astype(BF16)
            dl = (d * siluz) * (sgl * (1.0 + l * (1.0 - sgl)))
            dyc = _ln_bwd(dl * lg_ref[...], n, rs)
            dyc_ref[rows, :] = dyc
            return g_lg + _fold8(dl * n), g_lb + _fold8(dl), g_cb + _fold8(dyc)
        zero = jnp.zeros((8, D), F32)
        g_lg, g_lb, g_cb = _blocks(TM // RB, blk, (zero, zero, zero))
        st_ref[0] = st_ref[0] + g_lg
        st_ref[1] = st_ref[1] + g_lb
        st_ref[2] = st_ref[2] + g_cb

        gw_acc[...] += lax.dot_general(yp_s[...], dya_ref[...], TN, preferred_element_type=F32)

        @pl.when(i == nt - 1)
        def _():
            gw_ref[...] = gw_acc[...].astype(BF16)

    tile = pl.BlockSpec((TM, D), lambda i: (i, 0))
    row = pl.BlockSpec((1, D), lambda i: (0, 0))
    return pl.pallas_call(
        body, name="branch_a_bwd1", grid=(nt,),
        out_shape=(jax.ShapeDtypeStruct((NSEG, T, D), BF16), jax.ShapeDtypeStruct((T, D), F32),
                   jax.ShapeDtypeStruct((D, D), BF16), jax.ShapeDtypeStruct((3, 8, D), F32)),
        in_specs=[pl.BlockSpec(memory_space=ANY), tile, tile, pl.BlockSpec((TM, D), lambda i: (i, 2)), row, row,
                  pl.BlockSpec((None, D, D), lambda i: (0, 0, 0))],
        out_specs=(pl.BlockSpec((None, TM, D), lambda i: (2, i, 0)), tile, pl.BlockSpec((D, D), lambda i: (0, 0)),
                   pl.BlockSpec((3, 8, D), lambda i: (0, 0, 0))),
        scratch_shapes=[VMEM((TM, D), F32), VMEM((TM, D), BF16), VMEM((D, D), F32)],
        input_output_aliases={0: 0},
        compiler_params=_cp(("arbitrary",)),
    )(dp, dya, yc, p, lg, lb, wq)


def _branch_a_bwd2(dp, dyc, p, cw, gw_co, gw_so, gw_o, early):
    T = p.shape[0]
    nt = T // TM
    hb = TM // HALO

    def body(dp_in, dyc_ref, hdyc_ref, val_ref, glu_ref, cw_ref, gco_ref, gso_ref, go_ref, e_ref,
             dp_ref, gcw_ref, rq_ref, eg_ref, a_s, sd_ref, da_s, x_send, x_recv, x_loc, ag_send, ag_recv, ag_loc):
        del dp_in
        i = pl.program_id(0)
        start_exchange, wait_exchange = _owner_exchange((gco_ref, gso_ref, go_ref), rq_ref, x_send, x_recv, x_loc)
        ag_start, ag_relay, ag_finish = _allgather_phases([e_ref], lambda a, slot: eg_ref.at[slot],
                                                          ag_send, ag_recv, ag_loc)

        @pl.when(i == 0)
        def _():
            gcw_ref[...] = jnp.zeros_like(gcw_ref)
            ag_start()
            start_exchange()

        @pl.when(i == nt // 2)
        def _():
            ag_relay()

        def fill(rb, carry):
            rows = _rows(rb)
            a_s[rows, :] = _ld(val_ref, rows) * _sig(_ld(glu_ref, rows))
            return carry
        lax.fori_loop(0, TM // RB, fill, 0)
        sd_ref[0, 0:TM, :] = dyc_ref[...]
        sd_ref[0, TM:EXT, :] = jnp.where(i < nt - 1, hdyc_ref[...], 0.0)
        _shift_copies(sd_ref)

        def conv_t(rb, carry):
            r0 = rb * RBC
            accs = [jnp.zeros((8, D), F32)] * (RBC // 8)
            for k in range(KW):
                o = KW - 1 - k
                w = cw_ref[k]
                accs = [acc + w * sd_ref[o % 8, pl.ds(pl.multiple_of(r0 + 8 * (o // 8 + g), 8), 8), :]
                        for g, acc in enumerate(accs)]
            for g, acc in enumerate(accs):
                da_s[pl.ds(pl.multiple_of(r0 + 8 * g, 8), 8), :] = acc
            return carry
        lax.fori_loop(0, TM // RBC, conv_t, 0)

        for k0 in range(0, KW, TAPG):
            taps = list(range(k0, min(k0 + TAPG, KW)))

            def tap_group(rb, accs, taps=taps):
                for u in range(2):
                    r0 = rb * 32 + 16 * u
                    a = a_s[pl.ds(pl.multiple_of(r0, 16), 16), :]
                    out = []
                    for k, acc in zip(taps, accs):
                        o = KW - 1 - k
                        d = sd_ref[o % 8, pl.ds(pl.multiple_of(r0 + 8 * (o // 8), 8), 16), :]
                        out.append(acc + _fold8(a * d))
                    accs = tuple(out)
                return accs
            sums = lax.fori_loop(0, TM // 32, tap_group, tuple(jnp.zeros((8, D), F32) for _ in taps))
            for k, s in zip(taps, sums):
                gcw_ref[k] = gcw_ref[k] + s

        def glu_b(rb, carry):
            rows = _rows(rb)
            da = da_s[rows, :]
            sg = _sig(_ld(glu_ref, rows))
            dp_ref[0, rows, :] = (da * sg).astype(BF16)
            dp_ref[1, rows, :] = (da * _ld(val_ref, rows) * (sg * (1.0 - sg))).astype(BF16)
            return carry
        lax.fori_loop(0, TM // RB, glu_b, 0)

        @pl.when(i == nt - 1)
        def _():
            wait_exchange()
            ag_finish()

    tile = lambda seg: pl.BlockSpec((TM, D), lambda i: (i, seg))
    hbm = pl.BlockSpec(memory_space=ANY)
    return pl.pallas_call(
        body, name="branch_a_bwd2", grid=(nt,),
        out_shape=(jax.ShapeDtypeStruct((NSEG, T, D), BF16), jax.ShapeDtypeStruct((32, 8, D), F32),
                   jax.ShapeDtypeStruct((NDEV, 3, 128, D), BF16), jax.ShapeDtypeStruct((NDEV, NEARLY, D), F32)),
        in_specs=[hbm, pl.BlockSpec((TM, D), lambda i: (i, 0)),
                  pl.BlockSpec((HALO, D), lambda i: (jnp.minimum((i + 1) * hb, nt * hb - 1), 0)),
                  tile(0), tile(1), pl.BlockSpec((KW, 8, D), lambda i: (0, 0, 0)),
                  hbm, hbm, hbm, hbm],
        out_specs=(pl.BlockSpec((2, TM, D), lambda i: (0, i, 0)), pl.BlockSpec((32, 8, D), lambda i: (0, 0, 0)), hbm, hbm),
        scratch_shapes=[VMEM((TM, D), F32), VMEM((8, EXT, D), F32), VMEM((TM, D), F32),
                        pltpu.SemaphoreType.DMA((3, 7)), pltpu.SemaphoreType.DMA((3, 7)), pltpu.SemaphoreType.DMA((3,)),
                        pltpu.SemaphoreType.DMA((1, 7)), pltpu.SemaphoreType.DMA((1, 7)), pltpu.SemaphoreType.DMA((1,))],
        input_output_aliases={0: 0},
        compiler_params=_cp(("arbitrary",)),
    )(dp, dyc, dyc, p, p, cw, gw_co, gw_so, gw_o, early)


def _in_bwd_dx(dp, wing, x, dx2, mod, g_pre):
    T = x.shape[0]
    nt = T // TM

    def body(dp_ref, w_ref, x_ref, dx2_ref, mod_ref, g_ref, gx_ref, st_ref, dh_s):
        i = pl.program_id(0)

        @pl.when(i == 0)
        def _():
            st_ref[...] = jnp.zeros_like(st_ref)

        dh_s[...] = lax.dot_general(dp_ref[0], w_ref[0], NT, preferred_element_type=F32)
        for j in range(1, NSEG):
            dh_s[...] += lax.dot_general(dp_ref[j], w_ref[j], NT, preferred_element_type=F32)

        def blk(rb, carry):
            d_sh, d_sc, g_g = carry
            rows = _rows(rb)
            xv = x_ref[rows, :]
            r = lax.rsqrt(jnp.mean(xv * xv, axis=-1, keepdims=True) + EPS)
            xn = xv * r
            g = g_ref[...]
            hpre = xn * g
            dh = dh_s[rows, :]
            dhp = dh * (1.0 + mod_ref[1:2, :])
            dxn = dhp * g
            gx_ref[rows, :] = dx2_ref[rows, :] + r * (dxn - xn * jnp.mean(dxn * xn, axis=-1, keepdims=True))
            return d_sh + _fold8(dh), d_sc + _fold8(dh * hpre), g_g + _fold8(dhp * xn)
        zero = jnp.zeros((8, D), F32)
        d_sh, d_sc, g_g = _blocks(TM // RB, blk, (zero, zero, zero))
        st_ref[0] = st_ref[0] + d_sh
        st_ref[1] = st_ref[1] + d_sc
        st_ref[2] = st_ref[2] + g_g

    tile = pl.BlockSpec((TM, D), lambda i: (i, 0))
    return pl.pallas_call(
        body, name="in_bwd_dx", grid=(nt,),
        out_shape=(jax.ShapeDtypeStruct((T, D), F32), jax.ShapeDtypeStruct((3, 8, D), F32)),
        in_specs=[pl.BlockSpec((NSEG, TM, D), lambda i: (0, i, 0)),
                  pl.BlockSpec((NSEG, D, D), lambda i: (0, 0, 0), pipeline_mode=pl.Buffered(1)),
                  tile, tile, pl.BlockSpec((3, D), lambda i: (0, 0)), pl.BlockSpec((1, D), lambda i: (0, 0))],
        out_specs=(tile, pl.BlockSpec((3, 8, D), lambda i: (0, 0, 0))),
        scratch_shapes=[VMEM((TM, D), F32)],
        compiler_params=_cp(("arbitrary",), vmem_mb=56),
    )(dp, wing, x, dx2, mod, g_pre)


RS_ORDER = (5, 4, 3, 2, 7, 6, 1, 0)
RS_HALF = D // 2


def _in_bwd_dw(ht, dp, order, smalls):
    T = ht.shape[1]
    tmw = min(4 * TMB, T)
    nb = T // tmw
    nu = 2 * NSEG

    def body(order_ref, ht_ref, dp_ref, sm_ref, g_ref, smg_ref, acc, sbuf, rsib, rici, send_sems, recv_sems,
             ag_send, ag_recv, ag_loc):
        j = pl.program_id(0)
        i = pl.program_id(1)
        x, y, c = _me()
        sibling = (x, y, 1 - c)
        chips = [(1 - x, y), (x, 1 - y), (1 - x, 1 - y)]
        ag_start, ag_relay, ag_finish = _allgather_phases([sm_ref], lambda a, slot: smg_ref.at[slot],
                                                          ag_send, ag_recv, ag_loc)

        def to_sibling(s):
            return pltpu.make_async_remote_copy(src_ref=sbuf.at[s], dst_ref=rsib.at[s], send_sem=send_sems.at[s],
                                                recv_sem=recv_sems.at[s], device_id=sibling, device_id_type=MESH)

        def to_owner(t):
            return pltpu.make_async_remote_copy(src_ref=sbuf.at[8 + t], dst_ref=rici.at[t], send_sem=send_sems.at[8 + t],
                                                recv_sem=recv_sems.at[8 + t], device_id=(*chips[t // 2], c),
                                                device_id_type=MESH)

        @pl.when((j == 0) & (i == 0))
        def _():
            ag_start()

        @pl.when((j == 4) & (i == 0))
        def _():
            ag_relay()

        @pl.when(i == 0)
        def _():
            acc[...] = jnp.zeros_like(acc)
        acc[...] += jnp.dot(ht_ref[...], dp_ref[...], preferred_element_type=F32)

        for jj in range(nu):
            chip, half, mine = jj // 4, (jj % 4) // 2, jj % 2
            s = 2 * chip + half

            @pl.when((j == jj) & (i == nb - 1))
            def _(chip=chip, half=half, mine=mine, s=s):
                if not mine:
                    sbuf[s] = acc[...].astype(BF16)
                    to_sibling(s).start()
                elif chip < 3:
                    to_sibling(s).wait_recv()
                    sbuf[8 + s] = (acc[...] + rsib[s].astype(F32)).astype(BF16)
                    to_owner(s).start()
                else:
                    to_sibling(s).wait_recv()
                    g = acc[...] + rsib[s].astype(F32)
                    for q in range(3):
                        to_owner(2 * q + half).wait_recv()
                        g = g + rici[2 * q + half].astype(F32)
                    g_ref[:, half * RS_HALF:(half + 1) * RS_HALF] = g

        @pl.when((j == nu - 1) & (i == nb - 1))
        def _():
            for s in range(8):
                to_sibling(s).wait_send()
            for t in range(6):
                to_owner(t).wait_send()
            ag_finish()

    hbm = pl.BlockSpec(memory_space=ANY)
    return pl.pallas_call(
        body, name="in_bwd_dw",
        out_shape=(jax.ShapeDtypeStruct((D, D), F32), jax.ShapeDtypeStruct((NDEV, NLATE, D), F32)),
        grid_spec=pltpu.PrefetchScalarGridSpec(
            num_scalar_prefetch=1, grid=(nu, nb),
            in_specs=[pl.BlockSpec((D, tmw), lambda j, i, o: (0, i)),
                      pl.BlockSpec((None, tmw, RS_HALF), lambda j, i, o: (o[2 * (j // 4) + j % 2], i, (j % 4) // 2)),
                      hbm],
            out_specs=(pl.BlockSpec((D, D), lambda j, i, o: (0, 0)), hbm),
            scratch_shapes=[VMEM((D, RS_HALF), F32), VMEM((14, D, RS_HALF), BF16), VMEM((8, D, RS_HALF), BF16),
                            VMEM((6, D, RS_HALF), BF16), pltpu.SemaphoreType.DMA((14,)), pltpu.SemaphoreType.DMA((14,)),
                            pltpu.SemaphoreType.DMA((1, 7)), pltpu.SemaphoreType.DMA((1, 7)), pltpu.SemaphoreType.DMA((1,))]),
        compiler_params=_cp(("arbitrary", "arbitrary"), vmem_mb=56),
    )(order, ht, dp, smalls)


def _pack_early(st_a, st_b, st_m, gws, gbs):
    def body(sa_ref, sb_ref, sm_ref, gws_ref, gbs_ref, out_ref):
        out_ref[...] = jnp.zeros_like(out_ref)
        fold = lambda v: jnp.sum(v, axis=0, keepdims=True)
        out_ref[0:1, :] = fold(sm_ref[2])
        out_ref[1:2, :] = fold(sa_ref[2])
        out_ref[2:3, :] = fold(sa_ref[0])
        out_ref[3:4, :] = fold(sa_ref[1])
        out_ref[4:5, :] = fold(sb_ref[0])
        out_ref[5:6, :] = fold(sb_ref[1])
        out_ref[7:8, :] = fold(sm_ref[1])
        out_ref[8:9, :] = fold(sm_ref[0])
        tri = lax.broadcasted_iota(jnp.int32, (CHUNK, CHUNK), 0) >= lax.broadcasted_iota(jnp.int32, (CHUNK, CHUNK), 1)
        for h in range(HEADS):
            out_ref[16:16 + CHUNK, h * CHUNK:(h + 1) * CHUNK] = jnp.where(tri, gws_ref[h], 0.0)
            out_ref[6:7, h * CHUNK:(h + 1) * CHUNK] = fold(gbs_ref[h].T)

    vm = pl.BlockSpec(memory_space=VMEM)
    return pl.pallas_call(
        body, name="pack_early", out_shape=jax.ShapeDtypeStruct((NEARLY, D), F32),
        in_specs=[vm] * 5, out_specs=vm, compiler_params=_cp(vmem_mb=32),
    )(st_a, st_b, st_m, gws, gbs)


def _pack_late(st_x, gcw):
    def body(sx_ref, gcw_ref, out_ref):
        out_ref[...] = jnp.zeros_like(out_ref)
        fold = lambda v: jnp.sum(v, axis=0, keepdims=True)
        for r in range(3):
            out_ref[r:r + 1, :] = fold(sx_ref[r])
        for k in range(KW):
            out_ref[8 + k:9 + k, :] = fold(gcw_ref[k])

    vm = pl.BlockSpec(memory_space=VMEM)
    return pl.pallas_call(
        body, name="pack_late", out_shape=jax.ShapeDtypeStruct((NLATE, D), F32),
        in_specs=[vm] * 2, out_specs=vm, compiler_params=_cp(vmem_mb=32),
    )(st_x, gcw)


def _pack_params(name, b_ada, g_pre, conv_b, lg, lb, sg, sb, b_sgu, g_final, w_sgu_t):
    def body(ba_ref, *refs):
        rows, ws_ref, out_ref = refs[:8], refs[8], refs[9]
        out_ref[...] = jnp.zeros_like(out_ref)
        for r in range(3):
            out_ref[r:r + 1, :] = ba_ref[:, r * D:(r + 1) * D]
        for r, ref in enumerate(rows):
            out_ref[3 + r:4 + r, :] = ref[...]
        out_ref[ROW_WS:ROW_WS + CHUNK, :] = ws_ref[...]

    vm = pl.BlockSpec(memory_space=VMEM)
    return pl.pallas_call(
        body, name=name, out_shape=jax.ShapeDtypeStruct((NSMALL, D), F32),
        in_specs=[vm] * 10, out_specs=vm, compiler_params=_cp(vmem_mb=32),
    )(b_ada, g_pre, conv_b, lg, lb, sg, sb, b_sgu, g_final, w_sgu_t)


def _adam(w, g, m, v):
    m2 = ADAM_B1 * m + (1.0 - ADAM_B1) * g
    v2 = ADAM_B2 * v + (1.0 - ADAM_B2) * (g * g)
    m_hat = m2 / (1.0 - ADAM_B1 ** ADAM_STEP)
    v_hat = v2 / (1.0 - ADAM_B2 ** ADAM_STEP)
    delta = -ADAM_LR * (m_hat / (jnp.sqrt(v_hat) + ADAM_EPS) + ADAM_WD * w)
    return delta, m2, v2


def _small_finish(eg, lg, w, m, v):
    def body(eg_ref, lg_ref, w_ref, m_ref, v_ref, loss_ref, g_ref, d_ref, m2_ref, v2_ref):
        e = eg_ref[0]
        l = lg_ref[0]
        for s in range(1, NDEV):
            e = e + eg_ref[s]
            l = l + lg_ref[s]
        g_ref[...] = jnp.zeros_like(g_ref)
        g_ref[0:2, :] = l[0:2, :]
        g_ref[2:3, :] = e[0:1, :]
        g_ref[3:4, :] = l[2:3, :]
        g_ref[4:12, :] = e[1:9, :]
        g_ref[ROW_CW:ROW_CW + KW, :] = l[8:8 + KW, :]
        g_ref[ROW_WS:ROW_WS + CHUNK, :] = e[16:16 + CHUNK, :]
        loss_ref[...] = jnp.sum(e[8:9, :], axis=1, keepdims=True)
        d_ref[...], m2_ref[...], v2_ref[...] = _adam(w_ref[...], g_ref[...], m_ref[...], v_ref[...])

    vm = pl.BlockSpec(memory_space=VMEM)
    sd = jax.ShapeDtypeStruct((NSMALL, D), F32)
    return pl.pallas_call(
        body, name="small_finish", out_shape=(jax.ShapeDtypeStruct((1, 1), F32), sd, sd, sd, sd),
        in_specs=[vm] * 5, out_specs=(vm,) * 5, compiler_params=_cp(vmem_mb=32),
    )(eg, lg, w, m, v)


def _ada_grad_adam(ct, dm, w, m, v):
    def body(ct_ref, dm_ref, w_ref, m_ref, v_ref, g_ref, d_ref, m2_ref, v2_ref):
        g = ct_ref[:, 0:1] * dm_ref[0:1, :]
        for b in range(1, NDEV):
            g = g + ct_ref[:, b:b + 1] * dm_ref[b:b + 1, :]
        g_ref[...] = g
        d_ref[...], m2_ref[...], v2_ref[...] = _adam(w_ref[...], g, m_ref[...], v_ref[...])

    vm = pl.BlockSpec(memory_space=VMEM)
    sd = jax.ShapeDtypeStruct(w.shape, F32)
    return pl.pallas_call(
        body, name="ada_grad_adam", out_shape=(sd, sd, sd, sd),
        in_specs=[vm] * 5, out_specs=(vm,) * 4, compiler_params=_cp(vmem_mb=32),
    )(ct, dm, w, m, v)


def _adam_f32(name, g, w, m, v, rows=None):
    R, C = w.shape
    rows = R if rows is None else rows

    def body(g_ref, w_ref, m_ref, v_ref, d_ref, m2_ref, v2_ref):
        d_ref[...], m2_ref[...], v2_ref[...] = _adam(w_ref[...], g_ref[...], m_ref[...], v_ref[...])

    tile = pl.BlockSpec((rows, C), lambda i: (i, 0))
    sd = jax.ShapeDtypeStruct(w.shape, F32)
    return pl.pallas_call(
        body, name=name, grid=(R // rows,), out_shape=(sd, sd, sd), in_specs=[tile] * 4, out_specs=(tile,) * 3,
        compiler_params=_cp(("parallel",), vmem_mb=32),
    )(g, w, m, v)


def _adam_reduce(name, recv, w, m, v, rows, recv_spec):
    R, C = w.shape

    def body(r_ref, w_ref, m_ref, v_ref, g_ref, d_ref, m2_ref, v2_ref):
        g = r_ref[0].astype(F32)
        for s in range(1, NDEV):
            g = g + r_ref[s].astype(F32)
        g_ref[...] = g
        d_ref[...], m2_ref[...], v2_ref[...] = _adam(w_ref[...], g, m_ref[...], v_ref[...])

    tile = pl.BlockSpec((rows, C), lambda i: (i, 0))
    sd = jax.ShapeDtypeStruct((R, C), F32)
    return pl.pallas_call(
        body, name=name, grid=(R // rows,), out_shape=(sd, sd, sd, sd),
        in_specs=[recv_spec, tile, tile, tile], out_specs=(tile,) * 4,
        compiler_params=_cp(("parallel",)),
    )(recv, w, m, v)


def kernel(x, c, w_ada, b_ada, g_pre, w_in, conv_w, conv_b, conv_ln_g, conv_ln_b, w_conv_out, sgu_ln_g, sgu_ln_b, w_sgu, b_sgu, w_sgu_out, w_o, g_final, loss_target, m_w_ada, m_b_ada, m_g_pre, m_w_in, m_conv_w, m_conv_b, m_conv_ln_g, m_conv_ln_b, m_w_conv_out, m_sgu_ln_g, m_sgu_ln_b, m_w_sgu, m_b_sgu, m_w_sgu_out, m_w_o, m_g_final, v_w_ada, v_b_ada, v_g_pre, v_w_in, v_conv_w, v_conv_b, v_conv_ln_g, v_conv_ln_b, v_w_conv_out, v_sgu_ln_g, v_sgu_ln_b, v_w_sgu, v_b_sgu, v_w_sgu_out, v_w_o, v_g_final):
    T = x.shape[1]
    assert T % TMB == 0 and x.shape[2] == D
    xs = x[0]
    tgt = loss_target[0]
    my = 4 * lax.axis_index("x") + 2 * lax.axis_index("y") + lax.axis_index("c")

    mod, cg = _ada_fwd(c, w_ada[0], b_ada)
    gf = g_final.reshape(1, D)
    bst = b_sgu[0].T

    h, ht = _prep_h(xs, mod, g_pre)
    ag_order = jnp.stack([_slot(*b) for b in _ag_blocks(lax.axis_index("x"), lax.axis_index("y"), lax.axis_index("c"))])
    p, wing, wqg, cwg = _in_proj(h, w_in[0], ag_order.astype(jnp.int32),
                                 w_conv_out[0], w_sgu_out[0], w_o[0], conv_w[0])
    wq = wqg.reshape(3, D, D)
    cw = jnp.broadcast_to(jnp.transpose(cwg, (1, 0, 2)).reshape(KW, 1, D), (KW, 8, D))
    yc, ya = _branch_a_fwd(p, cw, conv_b, conv_ln_g, conv_ln_b, wq)
    yb = _branch_b_fwd(p, w_sgu[0], bst, sgu_ln_g, sgu_ln_b, wq)
    dx2, dya, dyb, dp, gw_o, st_m = _merge_loss(p, ya, yb, xs, tgt, mod, gf, wq)

    dp, gw_so, gws, gbs, st_b = _branch_b_bwd(dp, dyb, p, w_sgu[0], bst, sgu_ln_g, sgu_ln_b, wq)
    dp, dyc, gw_co, st_a = _branch_a_bwd1(dp, dya, yc, p, conv_ln_g, conv_ln_b, wq)
    early = _pack_early(st_a, st_b, st_m, gws, gbs)
    dp, gcw, rq, eg = _branch_a_bwd2(dp, dyc, p, cw, gw_co, gw_so, gw_o, early)
    grad_x, st_x = _in_bwd_dx(dp, wing, xs, dx2, mod, g_pre)
    late = _pack_late(st_x, gcw)
    g_in, lg = _in_bwd_dw(ht, dp, my ^ jnp.array(RS_ORDER, jnp.int32), late)

    def pack(name, b_ada_, g_pre_, conv_b_, lg_, lb_, sg_, sb_, b_sgu_, gfin_, w_sgu_):
        return _pack_params(name, b_ada_, g_pre_, conv_b_, lg_, lb_, sg_, sb_, b_sgu_.reshape(1, D), gfin_.reshape(1, D),
                            jnp.transpose(w_sgu_[0], (1, 0, 2)).reshape(CHUNK, D))

    pw = pack("pack_w", b_ada, g_pre, conv_b, conv_ln_g, conv_ln_b, sgu_ln_g, sgu_ln_b, b_sgu, g_final, w_sgu)
    pm = pack("pack_m", m_b_ada, m_g_pre, m_conv_b, m_conv_ln_g, m_conv_ln_b, m_sgu_ln_g, m_sgu_ln_b, m_b_sgu, m_g_final,
              m_w_sgu)
    pv = pack("pack_v", v_b_ada, v_g_pre, v_conv_b, v_conv_ln_g, v_conv_ln_b, v_sgu_ln_g, v_sgu_ln_b, v_b_sgu, v_g_final,
              v_w_sgu)
    loss11, sg_, sd_, sm_, sv_ = _small_finish(eg, lg, pw, pm, pv)

    def unpack(a):
        return dict(
            b_ada=a[0:3].reshape(1, 3 * D), g_pre=a[3:4], conv_b=a[4:5], conv_ln_g=a[5:6], conv_ln_b=a[6:7],
            sgu_ln_g=a[7:8], sgu_ln_b=a[8:9], b_sgu=a[9:10].reshape(1, HEADS, CHUNK), g_final=a[10],
            w_sgu=jnp.transpose(a[ROW_WS:ROW_WS + CHUNK].reshape(CHUNK, HEADS, CHUNK), (1, 0, 2))[None])
    small = [unpack(a) for a in (sg_, sd_, sm_, sv_)]

    g_cw = lax.dynamic_slice_in_dim(sg_[ROW_CW:ROW_CW + KW], my * 128, 128, axis=1)
    d_cw, m_cw, v_cw = _adam_f32("adam_conv_w", g_cw, conv_w[0], m_conv_w[0], v_conv_w[0])

    wcols = w_ada.shape[2]
    dm_all = jnp.concatenate([lg[:, 0], lg[:, 1], eg[:, 0]], axis=1)
    dm_mine = lax.dynamic_slice_in_dim(dm_all, my * wcols, wcols, axis=1)
    g_ada, d_ada, m_ada, v_ada = _ada_grad_adam(cg.T, dm_mine, w_ada[0], m_w_ada[0], v_w_ada[0])

    d_in, m_in, v_in = _adam_f32("adam_w_in", g_in, w_in[0], m_w_in[0], v_w_in[0], rows=256)
    big = {}
    for j, (nm, w_, m_, v_) in enumerate((("w_conv_out", w_conv_out, m_w_conv_out, v_w_conv_out),
                                          ("w_sgu_out", w_sgu_out, m_w_sgu_out, v_w_sgu_out),
                                          ("w_o", w_o, m_w_o, v_w_o))):
        big[nm] = _adam_reduce("adam_" + nm, rq, w_[0], m_[0], v_[0], 128,
                               pl.BlockSpec((NDEV, None, 128, D), lambda i, j=j: (0, j, 0, 0)))

    per = {
        "w_ada": tuple(a[None] for a in (g_ada, d_ada, m_ada, v_ada)),
        "w_in": tuple(a[None] for a in (g_in, d_in, m_in, v_in)),
        "conv_w": tuple(a[None] for a in (g_cw, d_cw, m_cw, v_cw)),
    }
    for nm in ("w_conv_out", "w_sgu_out", "w_o"):
        per[nm] = tuple(a[None] for a in big[nm])
    for nm in ("b_ada", "g_pre", "conv_b", "conv_ln_g", "conv_ln_b", "sgu_ln_g", "sgu_ln_b", "w_sgu", "b_sgu", "g_final"):
        per[nm] = tuple(s[nm] for s in small)

    order = ["w_ada", "b_ada", "g_pre", "w_in", "conv_w", "conv_b", "conv_ln_g", "conv_ln_b", "w_conv_out",
             "sgu_ln_g", "sgu_ln_b", "w_sgu", "b_sgu", "w_sgu_out", "w_o", "g_final"]
    outs = [loss11.reshape(()), grad_x[None]]
    for part in range(4):
        outs += [per[nm][part] for nm in order]
    return tuple(outs)
```

```python
import jax
import jax.numpy as jnp
from jax import lax
from jax.experimental import pallas as pl
from jax.experimental.pallas import tpu as pltpu

F32 = jnp.float32
BF16 = jnp.bfloat16
MESH = pl.DeviceIdType.MESH
VMEM = pltpu.VMEM
ANY = pl.ANY

D = 1024
NDEV = 8
NSEG = 8
HEADS = 8
HD = 128
CHUNK = 128
KW = 31
HALO = 32
EPS = 1e-6
TM = 256
TMB = 512
RB = 32
UNROLL = 4
RBC = 32
TAPG = 4
EXT = TM + HALO
NSMALL = 176
NEARLY = 144
NLATE = 40
ROW_CW = 16
ROW_WS = 48
W_CO, W_SO, W_O = 0, 1, 2

ADAM_LR = 0.001
ADAM_B1 = 0.9
ADAM_B2 = 0.999
ADAM_EPS = 1e-08
ADAM_WD = 0.01
ADAM_STEP = 10

INV_SQRT2 = 0.7071067811865476
INV_SQRT_2PI = 0.3989422804014327

NT = (((1,), (1,)), ((), ()))
TN = (((0,), (0,)), ((), ()))


def _cp(sem=None, vmem_mb=48):
    return pltpu.CompilerParams(dimension_semantics=sem, vmem_limit_bytes=vmem_mb * 1024 * 1024)


def _me():
    return lax.axis_index("x"), lax.axis_index("y"), lax.axis_index("c")


def _slot(px, py, pc):
    return 4 * px + 2 * py + pc


def _xor_peer(k):
    x, y, c = _me()
    return (1 - x if k & 4 else x, 1 - y if k & 2 else y, 1 - c if k & 1 else c)


def _allgather_phases(srcs, dst_at, send_sems, recv_sems, loc_sems):
    x, y, c = _me()
    me = (x, y, c)
    sibling = (x, y, 1 - c)
    chips = [(1 - x, y), (x, 1 - y), (1 - x, 1 - y)]
    na = len(srcs)

    def copy(a, k, block, to, src=None):
        d = dst_at(a, _slot(*block))
        return pltpu.make_async_remote_copy(
            src_ref=d if src is None else src, dst_ref=d,
            send_sem=send_sems.at[a, k], recv_sem=recv_sems.at[a, k],
            device_id=to, device_id_type=MESH)

    local = [pltpu.make_async_copy(srcs[a], dst_at(a, _slot(*me)), loc_sems.at[a]) for a in range(na)]
    first = []
    for a in range(na):
        first.append(copy(a, 0, me, sibling, src=srcs[a]))
        for j, chip in enumerate(chips):
            first.append(copy(a, 1 + j, me, (*chip, c), src=srcs[a]))
    passed = [copy(a, 4 + j, (*chip, c), sibling) for j, chip in enumerate(chips) for a in range(na)]

    def start():
        for cp in local + first:
            cp.start()

    def relay():
        for j, chip in enumerate(chips):
            for a in range(na):
                copy(a, 1 + j, (*chip, c), me).wait_recv()
                passed[j * na + a].start()

    def finish():
        for a in range(na):
            copy(a, 0, sibling, me).wait_recv()
        for j, chip in enumerate(chips):
            for a in range(na):
                copy(a, 4 + j, (*chip, 1 - c), me).wait_recv()
        for cp in first + passed:
            cp.wait_send()
        for cp in local:
            cp.wait()

    return start, relay, finish


def _allgather(srcs, dst_at, send_sems, recv_sems, loc_sems):
    start, relay, finish = _allgather_phases(srcs, dst_at, send_sems, recv_sems, loc_sems)
    start()
    relay()
    finish()


def _owner_exchange(gq, rq_ref, x_send, x_recv, x_loc):
    mx, my_, mc = _me()
    me = _slot(mx, my_, mc)

    def rows_of(a, slot):
        return gq[a].at[pl.ds(pl.multiple_of(slot * 128, 128), 128), :]

    def exchange(k, a, recv):
        px, py, pc = _xor_peer(k)
        peer = _slot(px, py, pc)
        return pltpu.make_async_remote_copy(
            src_ref=rows_of(a, me if recv else peer), dst_ref=rq_ref.at[peer if recv else me, a],
            send_sem=x_send.at[a, k - 1], recv_sem=x_recv.at[a, k - 1],
            device_id=(px, py, pc), device_id_type=MESH)

    local = [pltpu.make_async_copy(rows_of(a, me), rq_ref.at[me, a], x_loc.at[a]) for a in range(len(gq))]

    def start():
        for cp in local:
            cp.start()
        for k in range(1, NDEV):
            for a in range(len(gq)):
                exchange(k, a, False).start()

    def wait():
        for k in range(1, NDEV):
            for a in range(len(gq)):
                exchange(k, a, True).wait_recv()
        for k in range(1, NDEV):
            for a in range(len(gq)):
                exchange(k, a, False).wait_send()
        for cp in local:
            cp.wait()

    return start, wait


def _sig(v):
    return jax.nn.sigmoid(v)


def _gelu_parts(v):
    cdf = 0.5 * (1.0 + lax.erf(v * INV_SQRT2))
    pdf = jnp.exp(-0.5 * v * v) * INV_SQRT_2PI
    return v * cdf, cdf + v * pdf


def _ln_stats(v):
    mu = jnp.mean(v, axis=-1, keepdims=True)
    xc = v - mu
    var = jnp.mean(xc * xc, axis=-1, keepdims=True)
    rs = lax.rsqrt(var + EPS)
    return xc * rs, rs


def _ln_bwd(dn, n, rs):
    return rs * (dn - jnp.mean(dn, axis=-1, keepdims=True) - n * jnp.mean(dn * n, axis=-1, keepdims=True))


def _fold8(v):
    acc = v[0:8]
    for r in range(8, v.shape[0], 8):
        acc = acc + v[r:r + 8]
    return acc


def _rows(rb, n=RB):
    return pl.ds(pl.multiple_of(rb * n, n), n)


def _ring_tiles(nt):
    return 4 if nt % 4 == 0 else (2 if nt % 2 == 0 else 1)


def _ring_base(i, nt):
    return pl.multiple_of((i % _ring_tiles(nt)) * TM, TM)


def _ring_rows(base, rb, n=RB):
    return pl.ds(pl.multiple_of(base + rb * n, n), n)


def _ring_full(i, nt):
    g = _ring_tiles(nt)
    return i % g == g - 1


def _ld(ref, rows):
    return ref[rows, :].astype(F32)


def _blocks(n, body, init=0, unroll=UNROLL):
    def trip(t, carry):
        for u in range(unroll):
            carry = body(t * unroll + u, carry)
        return carry
    return lax.fori_loop(0, n // unroll, trip, init)


def _shift_copies(s_ref):
    n = EXT - 8
    for m in range(1, 8):
        for c0 in range(0, n, 56):
            s_ref[m, c0:c0 + 56, :] = s_ref[0, c0 + m:c0 + m + 56, :]


def _ada_fwd(c, w_ada, b_ada):
    wcols = w_ada.shape[1]

    def body(c_ref, w_ref, b_ref, mod_ref, call_ref, cdst, msrc, mdst, c_send, c_recv, m_send, m_recv):
        x, y, c = _me()
        me = _slot(x, y, c)

        def all_to_all(src_of, dst, ss, rs):
            def cp(k, recv):
                px, py, pc = _xor_peer(k)
                peer = _slot(px, py, pc)
                return pltpu.make_async_remote_copy(
                    src_ref=src_of(me if recv else peer), dst_ref=dst.at[peer if recv else me],
                    send_sem=ss.at[k - 1], recv_sem=rs.at[k - 1], device_id=(px, py, pc), device_id_type=MESH)
            for k in range(1, NDEV):
                cp(k, False).start()
            for k in range(1, NDEV):
                cp(k, True).wait_recv()
            for k in range(1, NDEV):
                cp(k, False).wait_send()

        cdst[me] = c_ref[...]
        all_to_all(lambda s: c_ref, cdst, c_send, c_recv)
        for b in range(NDEV):
            call_ref[b:b + 1, :] = cdst[b]
        m = jnp.dot(call_ref[...], w_ref[...], preferred_element_type=F32, precision=lax.Precision.HIGHEST)
        for b in range(NDEV):
            msrc[b] = m[b:b + 1, :]
        mdst[me] = msrc[me]
        all_to_all(lambda s: msrc.at[s], mdst, m_send, m_recv)
        full = jnp.concatenate([mdst[k] for k in range(NDEV)], axis=1) + b_ref[...]
        for r in range(3):
            mod_ref[r:r + 1, :] = full[:, r * D:(r + 1) * D]

    vm = pl.BlockSpec(memory_space=VMEM)
    return pl.pallas_call(
        body, name="ada_fwd",
        out_shape=(jax.ShapeDtypeStruct((3, D), F32), jax.ShapeDtypeStruct((NDEV, D), F32)),
        in_specs=[vm, vm, vm], out_specs=(vm, vm),
        scratch_shapes=[VMEM((NDEV, 1, D), F32), VMEM((NDEV, 1, wcols), F32), VMEM((NDEV, 1, wcols), F32),
                        pltpu.SemaphoreType.DMA((7,)), pltpu.SemaphoreType.DMA((7,)),
                        pltpu.SemaphoreType.DMA((7,)), pltpu.SemaphoreType.DMA((7,))],
        compiler_params=_cp(vmem_mb=32),
    )(c, w_ada, b_ada)


def _prep_h(x, mod, g_pre):
    T = x.shape[0]

    def body(x_ref, mod_ref, g_ref, h_ref, ht_ref):
        def blk(rb, carry):
            rows = _rows(rb)
            xv = x_ref[rows, :]
            r = lax.rsqrt(jnp.mean(xv * xv, axis=-1, keepdims=True) + EPS)
            h_ref[rows, :] = ((xv * r) * g_ref[...] * (1.0 + mod_ref[1:2, :]) + mod_ref[0:1, :]).astype(BF16)
            return carry
        _blocks(TMB // RB, blk)
        ht_ref[...] = h_ref[...].T

    return pl.pallas_call(
        body, name="prep_h", grid=(T // TMB,),
        out_shape=(jax.ShapeDtypeStruct((T, D), BF16), jax.ShapeDtypeStruct((D, T), BF16)),
        in_specs=[pl.BlockSpec((TMB, D), lambda i: (i, 0)), pl.BlockSpec((3, D), lambda i: (0, 0)),
                  pl.BlockSpec((1, D), lambda i: (0, 0))],
        out_specs=(pl.BlockSpec((TMB, D), lambda i: (i, 0)), pl.BlockSpec((D, TMB), lambda i: (0, i))),
        compiler_params=_cp(("parallel",)),
    )(x, mod, g_pre)


def _ag_blocks(x, y, c):
    a = 1 - c
    n1 = (x ^ a, y ^ c)
    n2 = (x ^ c, y ^ a)
    dg = (1 - x, 1 - y)
    return [(x, y, c), (x, y, a), (*n1, c), (*n2, c), (*n1, a), (*n2, a), (*dg, c), (*dg, a)]


def _in_proj(h, w_in, order, w_co, w_so, w_o, conv_w):
    T = h.shape[0]
    tmp = min(2 * TMB, T)
    nb = T // tmp

    def body(order_ref, h_ref, w_ref, wco_ref, wso_ref, wo_ref, cw_ref, p_ref, wing_ref, wqg_ref, cwg_ref,
             wbuf, wq_st, send_sems, recv_sems, out_sems, ag_send, ag_recv, ag_loc):
        j = pl.program_id(0)
        i = pl.program_id(1)
        blocks = _ag_blocks(*_me())
        me, sibling, n1, n2, n1o, n2o, dg, dgo = blocks
        ag_start, ag_relay, ag_finish = _allgather_phases(
            [wq_st, cw_ref], lambda a, slot: wqg_ref.at[:, slot] if a == 0 else cwg_ref.at[slot], ag_send, ag_recv, ag_loc)

        def copy(k, block, to):
            d = wbuf.at[_slot(*block)]
            return pltpu.make_async_remote_copy(src_ref=d, dst_ref=d, send_sem=send_sems.at[k], recv_sem=recv_sems.at[k],
                                                device_id=to, device_id_type=MESH)

        def writeback(jj):
            s = _slot(*blocks[jj])
            return pltpu.make_async_copy(wbuf.at[s], wing_ref.at[s], out_sems.at[jj])

        sends = [copy(0, me, sibling), copy(1, me, n1), copy(2, me, n2)]
        relay = copy(3, n1, n2)
        passed = [copy(4, n1, sibling), copy(5, n2, sibling), copy(6, dg, sibling)]
        arrivals = {1: [(0, sibling)], 2: [(1, n1), (2, n2)], 4: [(5, n1o)], 5: [(4, n2o)], 6: [(3, dg)], 7: [(6, dgo)]}

        @pl.when((j == 0) & (i == 0))
        def _():
            wbuf[_slot(*me)] = w_ref[...].astype(BF16)
            for cp in sends:
                cp.start()
            writeback(0).start()
            wq_st[W_CO] = wco_ref[...].astype(BF16)
            wq_st[W_SO] = wso_ref[...].astype(BF16)
            wq_st[W_O] = wo_ref[...].astype(BF16)
            ag_start()

        for jj in range(1, NSEG):
            @pl.when((j == jj) & (i == 0))
            def _(jj=jj):
                for k, block in arrivals.get(jj, []):
                    copy(k, block, me).wait_recv()
                if jj == 2:
                    relay.start()
                    passed[0].start()
                    passed[1].start()
                    writeback(3).start()
                if jj == 6:
                    passed[2].start()
                if jj != 3:
                    writeback(jj).start()
                if jj == NSEG - 1:
                    ag_relay()

        p_ref[...] = jnp.dot(h_ref[...], wbuf[order_ref[j]], preferred_element_type=F32).astype(BF16)

        @pl.when((j == NSEG - 1) & (i == nb - 1))
        def _():
            for cp in sends + [relay] + passed:
                cp.wait_send()
            for jj in range(NSEG):
                writeback(jj).wait()
            ag_finish()

    vm = pl.BlockSpec(memory_space=VMEM)
    hbm = pl.BlockSpec(memory_space=ANY)
    return pl.pallas_call(
        body, name="in_proj",
        out_shape=(jax.ShapeDtypeStruct((T, NSEG * D), BF16), jax.ShapeDtypeStruct((NSEG, D, D), BF16),
                   jax.ShapeDtypeStruct((3, NDEV, 128, D), BF16), jax.ShapeDtypeStruct((NDEV, KW, 128), F32)),
        grid_spec=pltpu.PrefetchScalarGridSpec(
            num_scalar_prefetch=1, grid=(NSEG, nb),
            in_specs=[pl.BlockSpec((tmp, D), lambda j, i, o: (i, 0)), vm, vm, vm, vm, vm],
            out_specs=(pl.BlockSpec((tmp, D), lambda j, i, o: (i, o[j])), hbm, hbm, hbm),
            scratch_shapes=[VMEM((NSEG, D, D), BF16), VMEM((3, 128, D), BF16),
                            pltpu.SemaphoreType.DMA((7,)), pltpu.SemaphoreType.DMA((7,)), pltpu.SemaphoreType.DMA((NSEG,)),
                            pltpu.SemaphoreType.DMA((2, 7)), pltpu.SemaphoreType.DMA((2, 7)), pltpu.SemaphoreType.DMA((2,))]),
        compiler_params=_cp(("arbitrary", "arbitrary")),
    )(order, h, w_in, w_co, w_so, w_o, conv_w)


def _fill_a_ext(i, s_ref, val_ref, glu_ref, hval_ref, hglu_ref):
    ah = hval_ref[...].astype(F32) * _sig(hglu_ref[...].astype(F32))
    s_ref[0, 0:HALO, :] = jnp.where(i > 0, ah, 0.0)

    def fill(rb, carry):
        rows = _rows(rb)
        s_ref[0, pl.ds(pl.multiple_of(HALO + rb * RB, RB), RB), :] = _ld(val_ref, rows) * _sig(_ld(glu_ref, rows))
        return carry
    lax.fori_loop(0, TM // RB, fill, 0)


def _halo_prev(seg):
    hb = TM // HALO
    return pl.BlockSpec((HALO, D), lambda i: (jnp.maximum(i * hb - 1, 0), seg))


def _branch_a_fwd(p, cw, conv_b, lg, lb, wq):
    T = p.shape[0]

    def body(val_ref, glu_ref, z_ref, hval_ref, hglu_ref, cw_ref, cb_ref, lg_ref, lb_ref, w_ref,
             yc_ref, ya_ref, s_ref, yp_ref):
        i = pl.program_id(0)
        _fill_a_ext(i, s_ref, val_ref, glu_ref, hval_ref, hglu_ref)
        _shift_copies(s_ref)

        def conv(rb, carry):
            r0 = rb * RBC
            accs = [jnp.broadcast_to(cb_ref[...], (8, D))] * (RBC // 8)
            for k in range(KW):
                o = 2 + k
                w = cw_ref[k]
                accs = [acc + w * s_ref[o % 8, pl.ds(pl.multiple_of(r0 + 8 * (o // 8 + g), 8), 8), :]
                        for g, acc in enumerate(accs)]
            for g, acc in enumerate(accs):
                yc_ref[pl.ds(pl.multiple_of(r0 + 8 * g, 8), 8), :] = acc
            return carry
        lax.fori_loop(0, TM // RBC, conv, 0)

        def post(rb, carry):
            rows = _rows(rb)
            n, _ = _ln_stats(yc_ref[rows, :])
            l = n * lg_ref[...] + lb_ref[...]
            z = _ld(z_ref, rows)
            yp_ref[rows, :] = ((l * _sig(l)) * (z * _sig(z))).astype(BF16)
            return carry
        _blocks(TM // RB, post)
        ya_ref[...] = jnp.dot(yp_ref[...], w_ref[...], preferred_element_type=F32)

    tile = lambda seg: pl.BlockSpec((TM, D), lambda i: (i, seg))
    row = pl.BlockSpec((1, D), lambda i: (0, 0))
    return pl.pallas_call(
        body, name="branch_a_fwd", grid=(T // TM,),
        out_shape=(jax.ShapeDtypeStruct((T, D), F32), jax.ShapeDtypeStruct((T, D), F32)),
        in_specs=[tile(0), tile(1), tile(2), _halo_prev(0), _halo_prev(1),
                  pl.BlockSpec((KW, 8, D), lambda i: (0, 0, 0)), row, row, row,
                  pl.BlockSpec((None, D, D), lambda i: (W_CO, 0, 0))],
        out_specs=(pl.BlockSpec((TM, D), lambda i: (i, 0)), pl.BlockSpec((TM, D), lambda i: (i, 0))),
        scratch_shapes=[VMEM((8, EXT, D), F32), VMEM((TM, D), BF16)],
        compiler_params=_cp(("parallel",)),
    )(p, p, p, p, p, cw, conv_b, lg, lb, wq)


def _masked_ws(ws_ref, wt_ref):
    tri = lax.broadcasted_iota(jnp.int32, (CHUNK, CHUNK), 0) >= lax.broadcasted_iota(jnp.int32, (CHUNK, CHUNK), 1)
    for h in range(HEADS):
        wt_ref[h] = jnp.where(tri, ws_ref[h], 0.0).astype(BF16)


def _sgu_mix(wt_ref, vl_ref, bst_ref, s_ref):
    for ck in range(TM // CHUNK):
        r = slice(ck * CHUNK, (ck + 1) * CHUNK)
        for h in range(HEADS):
            cs = slice(h * HD, (h + 1) * HD)
            s_ref[r, cs] = jnp.dot(wt_ref[h], vl_ref[r, cs], preferred_element_type=F32) + bst_ref[:, h:h + 1]


def _branch_b_fwd(p, ws, bst, sg, sb, wq):
    T = p.shape[0]

    def body(pu_ref, pv_ref, pz_ref, ws_ref, bst_ref, sg_ref, sb_ref, w_ref, yb_ref,
             wt_ref, vl_ref, t_ref, s_ref, yp_ref):
        _masked_ws(ws_ref, wt_ref)

        def pre(rb, carry):
            rows = _rows(rb)
            vg, _ = _gelu_parts(_ld(pv_ref, rows))
            vn, _ = _ln_stats(vg)
            vl_ref[rows, :] = (vn * sg_ref[...] + sb_ref[...]).astype(BF16)
            u, _ = _gelu_parts(_ld(pu_ref, rows))
            z = _ld(pz_ref, rows)
            t_ref[rows, :] = u * (z * _sig(z))
            return carry
        _blocks(TM // RB, pre)
        _sgu_mix(wt_ref, vl_ref, bst_ref, s_ref)

        def post(rb, carry):
            rows = _rows(rb)
            yp_ref[rows, :] = (t_ref[rows, :] * s_ref[rows, :]).astype(BF16)
            return carry
        _blocks(TM // RB, post)
        yb_ref[...] = jnp.dot(yp_ref[...], w_ref[...], preferred_element_type=F32)

    tile = lambda seg: pl.BlockSpec((TM, D), lambda i: (i, seg))
    row = pl.BlockSpec((1, D), lambda i: (0, 0))
    return pl.pallas_call(
        body, name="branch_b_fwd", grid=(T // TM,),
        out_shape=jax.ShapeDtypeStruct((T, D), F32),
        in_specs=[tile(3), tile(4), tile(5), pl.BlockSpec((HEADS, CHUNK, CHUNK), lambda i: (0, 0, 0)),
                  pl.BlockSpec((CHUNK, HEADS), lambda i: (0, 0)), row, row,
                  pl.BlockSpec((None, D, D), lambda i: (W_SO, 0, 0))],
        out_specs=pl.BlockSpec((TM, D), lambda i: (i, 0)),
        scratch_shapes=[VMEM((HEADS, CHUNK, CHUNK), BF16), VMEM((TM, D), BF16), VMEM((TM, D), F32),
                        VMEM((TM, D), F32), VMEM((TM, D), BF16)],
        compiler_params=_cp(("parallel",)),
    )(p, p, p, ws, bst, sg, sb, wq)


def _merge_loss(p, ya, yb, x, tgt, mod, g_final, wq):
    T = x.shape[0]
    nt = T // TM

    def body(ga_ref, gb_ref, ya_ref, yb_ref, x_ref, t_ref, mod_ref, gf_ref, w_ref,
             dx2_ref, dya_ref, dyb_ref, dp_ref, gwo_ref, st_ref,
             mrg_s, out_s, dout_s, dm_s, gw_acc):
        i = pl.program_id(0)
        base = _ring_base(i, nt)
        cur = pl.ds(base, TM)

        @pl.when(i == 0)
        def _():
            gw_acc[...] = jnp.zeros_like(gw_acc)
            st_ref[...] = jnp.zeros_like(st_ref)

        def merge(rb, carry):
            rows = _rows(rb)
            mrg_s[_ring_rows(base, rb), :] = (_sig(_ld(ga_ref, rows)) * ya_ref[rows, :]
                                              + _sig(_ld(gb_ref, rows)) * yb_ref[rows, :]).astype(BF16)
            return carry
        lax.fori_loop(0, TM // RB, merge, 0)
        out_s[...] = jnp.dot(mrg_s[cur, :], w_ref[...], preferred_element_type=F32)

        def head(rb, carry):
            loss, gg, dg = carry
            rows = _rows(rb)
            gate = mod_ref[2:3, :]
            gf = gf_ref[...]
            out = out_s[rows, :]
            x2 = x_ref[rows, :] + gate * out
            r2 = lax.rsqrt(jnp.mean(x2 * x2, axis=-1, keepdims=True) + EPS)
            x2n = x2 * r2
            diff = x2n * gf - t_ref[rows, :]
            dy = diff * (1.0 / D)
            dx2n = dy * gf
            dx2 = r2 * (dx2n - x2n * jnp.mean(dx2n * x2n, axis=-1, keepdims=True))
            dx2_ref[rows, :] = dx2
            dout_s[_ring_rows(base, rb), :] = (dx2 * gate).astype(BF16)
            return loss + _fold8(diff * diff), gg + _fold8(dy * x2n), dg + _fold8(dx2 * out)
        zero = jnp.zeros((8, D), F32)
        loss, gg, dg = _blocks(TM // RB, head, (zero, zero, zero))
        st_ref[0] = st_ref[0] + loss * (0.5 / D)
        st_ref[1] = st_ref[1] + gg
        st_ref[2] = st_ref[2] + dg

        dm_s[...] = lax.dot_general(dout_s[cur, :], w_ref[...], NT, preferred_element_type=F32)

        @pl.when(_ring_full(i, nt))
        def _():
            gw_acc[...] += lax.dot_general(mrg_s[...], dout_s[...], TN, preferred_element_type=F32)

        def split(rb, carry):
            rows = _rows(rb)
            dm = dm_s[rows, :]
            sa = _sig(_ld(ga_ref, rows))
            sb = _sig(_ld(gb_ref, rows))
            dya_ref[rows, :] = (dm * sa).astype(BF16)
            dyb_ref[rows, :] = (dm * sb).astype(BF16)
            dp_ref[0, rows, :] = (dm * ya_ref[rows, :] * (sa * (1.0 - sa))).astype(BF16)
            dp_ref[1, rows, :] = (dm * yb_ref[rows, :] * (sb * (1.0 - sb))).astype(BF16)
            return carry
        lax.fori_loop(0, TM // RB, split, 0)

        @pl.when(i == nt - 1)
        def _():
            gwo_ref[...] = gw_acc[...].astype(BF16)

    tile = pl.BlockSpec((TM, D), lambda i: (i, 0))
    ring = VMEM((_ring_tiles(nt) * TM, D), BF16)
    return pl.pallas_call(
        body, name="merge_loss", grid=(nt,),
        out_shape=(jax.ShapeDtypeStruct((T, D), F32), jax.ShapeDtypeStruct((T, D), BF16), jax.ShapeDtypeStruct((T, D), BF16),
                   jax.ShapeDtypeStruct((NSEG, T, D), BF16), jax.ShapeDtypeStruct((D, D), BF16),
                   jax.ShapeDtypeStruct((3, 8, D), F32)),
        in_specs=[pl.BlockSpec((TM, D), lambda i: (i, 6)), pl.BlockSpec((TM, D), lambda i: (i, 7)), tile, tile, tile, tile,
                  pl.BlockSpec((3, D), lambda i: (0, 0)), pl.BlockSpec((1, D), lambda i: (0, 0)),
                  pl.BlockSpec((None, D, D), lambda i: (W_O, 0, 0))],
        out_specs=(tile, tile, tile, pl.BlockSpec((2, TM, D), lambda i: (3, i, 0)),
                   pl.BlockSpec((D, D), lambda i: (0, 0)), pl.BlockSpec((3, 8, D), lambda i: (0, 0, 0))),
        scratch_shapes=[ring, VMEM((TM, D), F32), ring, VMEM((TM, D), F32), VMEM((D, D), F32)],
        compiler_params=_cp(("arbitrary",)),
    )(p, p, ya, yb, x, tgt, mod, g_final, wq)


def _branch_b_bwd(dp, dyb, p, ws, bst, sg, sb, wq):
    T = p.shape[0]
    nt = T // TM

    def body(dp_in, dyb_ref, pu_ref, pv_ref, pz_ref, ws_ref, bst_ref, sg_ref, sb_ref, w_ref,
             dp_ref, gw_ref, gws_ref, gbs_ref, st_ref,
             wt_ref, d_s, vl_s, vn_s, gpv_s, rs_s, s_s, ds_s, ds32_s, yp_s, dy_s, dvl_s, gw_acc):
        del dp_in
        i = pl.program_id(0)
        base = _ring_base(i, nt)

        @pl.when(i == 0)
        def _():
            gw_acc[...] = jnp.zeros_like(gw_acc)
            gws_ref[...] = jnp.zeros_like(gws_ref)
            gbs_ref[...] = jnp.zeros_like(gbs_ref)
            st_ref[...] = jnp.zeros_like(st_ref)
            _masked_ws(ws_ref, wt_ref)

        d_s[...] = lax.dot_general(dyb_ref[...], w_ref[...], NT, preferred_element_type=F32)
        dy_s[pl.ds(base, TM), :] = dyb_ref[...]

        def pre(rb, carry):
            rows = _rows(rb)
            vg, gpv = _gelu_parts(_ld(pv_ref, rows))
            vn, rs = _ln_stats(vg)
            vl_s[rows, :] = (vn * sg_ref[...] + sb_ref[...]).astype(BF16)
            vn_s[rows, :] = vn
            gpv_s[rows, :] = gpv
            rs_s[rows, :] = rs
            return carry
        _blocks(TM // RB, pre)
        _sgu_mix(wt_ref, vl_s, bst_ref, s_s)

        def mid(rb, carry):
            rows = _rows(rb)
            u, gpu = _gelu_parts(_ld(pu_ref, rows))
            z = _ld(pz_ref, rows)
            sz = _sig(z)
            siluz = z * sz
            d = d_s[rows, :]
            s = s_s[rows, :]
            t = u * siluz
            yp_s[_ring_rows(base, rb), :] = (t * s).astype(BF16)
            dp_ref[0, rows, :] = (d * s * siluz * gpu).astype(BF16)
            ds = d * t
            ds32_s[rows, :] = ds
            ds_s[rows, :] = ds.astype(BF16)
            dp_ref[2, rows, :] = (d * u * s * (sz * (1.0 + z * (1.0 - sz)))).astype(BF16)
            return carry
        lax.fori_loop(0, TM // RB, mid, 0)

        for ck in range(TM // CHUNK):
            r = slice(ck * CHUNK, (ck + 1) * CHUNK)
            for h in range(HEADS):
                cs = slice(h * HD, (h + 1) * HD)
                dsh = ds_s[r, cs]
                dvl_s[r, cs] = lax.dot_general(wt_ref[h], dsh, TN, preferred_element_type=F32)
                gws_ref[h] += lax.dot_general(dsh, vl_s[r, cs], NT, preferred_element_type=F32)
                gbs_ref[h] += ds32_s[r, cs]

        def post(rb, carry):
            g_sg, g_sb = carry
            rows = _rows(rb)
            dvl = dvl_s[rows, :]
            vn = vn_s[rows, :]
            dvg = _ln_bwd(dvl * sg_ref[...], vn, rs_s[rows, :])
            dp_ref[1, rows, :] = (dvg * gpv_s[rows, :]).astype(BF16)
            return g_sg + _fold8(dvl * vn), g_sb + _fold8(dvl)
        zero = jnp.zeros((8, D), F32)
        g_sg, g_sb = _blocks(TM // RB, post, (zero, zero))
        st_ref[0] = st_ref[0] + g_sg
        st_ref[1] = st_ref[1] + g_sb

        @pl.when(_ring_full(i, nt))
        def _():
            gw_acc[...] += lax.dot_general(yp_s[...], dy_s[...], TN, preferred_element_type=F32)

        @pl.when(i == nt - 1)
        def _():
            gw_ref[...] = gw_acc[...].astype(BF16)

    tile = lambda seg: pl.BlockSpec((TM, D), lambda i: (i, seg))
    row = pl.BlockSpec((1, D), lambda i: (0, 0))
    hh = pl.BlockSpec((HEADS, CHUNK, CHUNK), lambda i: (0, 0, 0))
    ring = VMEM((_ring_tiles(nt) * TM, D), BF16)
    return pl.pallas_call(
        body, name="branch_b_bwd", grid=(nt,),
        out_shape=(jax.ShapeDtypeStruct((NSEG, T, D), BF16), jax.ShapeDtypeStruct((D, D), BF16),
                   jax.ShapeDtypeStruct((HEADS, CHUNK, CHUNK), F32), jax.ShapeDtypeStruct((HEADS, CHUNK, HD), F32),
                   jax.ShapeDtypeStruct((2, 8, D), F32)),
        in_specs=[pl.BlockSpec(memory_space=ANY), pl.BlockSpec((TM, D), lambda i: (i, 0)), tile(3), tile(4), tile(5),
                  hh, pl.BlockSpec((CHUNK, HEADS), lambda i: (0, 0)), row, row,
                  pl.BlockSpec((None, D, D), lambda i: (W_SO, 0, 0))],
        out_specs=(pl.BlockSpec((3, TM, D), lambda i: (1, i, 0)), pl.BlockSpec((D, D), lambda i: (0, 0)), hh, hh,
                   pl.BlockSpec((2, 8, D), lambda i: (0, 0, 0))),
        scratch_shapes=[VMEM((HEADS, CHUNK, CHUNK), BF16), VMEM((TM, D), F32), VMEM((TM, D), BF16), VMEM((TM, D), F32),
                        VMEM((TM, D), F32), VMEM((TM, 1), F32), VMEM((TM, D), F32), VMEM((TM, D), BF16), VMEM((TM, D), F32),
                        ring, ring, VMEM((TM, D), F32), VMEM((D, D), F32)],
        input_output_aliases={0: 0},
        compiler_params=_cp(("arbitrary",)),
    )(dp, dyb, p, p, p, ws, bst, sg, sb, wq)


def _branch_a_bwd1(dp, dya, yc, p, lg, lb, wq):
    T = p.shape[0]
    nt = T // TM

    def body(dp_in, dya_ref, yc_ref, z_ref, lg_ref, lb_ref, w_ref, dp_ref, dyc_ref, gw_ref, st_ref,
             d_s, yp_s, dy_s, gw_acc):
        del dp_in
        i = pl.program_id(0)
        base = _ring_base(i, nt)

        @pl.when(i == 0)
        def _():
            gw_acc[...] = jnp.zeros_like(gw_acc)
            st_ref[...] = jnp.zeros_like(st_ref)

        d_s[...] = lax.dot_general(dya_ref[...], w_ref[...], NT, preferred_element_type=F32)
        dy_s[pl.ds(base, TM), :] = dya_ref[...]

        def blk(rb, carry):
            g_lg, g_lb, g_cb = carry
            rows = _rows(rb)
            n, rs = _ln_stats(yc_ref[rows, :])
            l = n * lg_ref[...] + lb_ref[...]
            sgl = _sig(l)
            sl = l * sgl
            z = _ld(z_ref, rows)
            sz = _sig(z)
            siluz = z * sz
            d = d_s[rows, :]
            yp_s[_ring_rows(base, rb), :] = (sl * siluz).astype(BF16)
            dp_ref[rows, :] = (d * sl * (sz * (1.0 + z * (1.0 - sz)))).astype(BF16)
            dl = (d * siluz) * (sgl * (1.0 + l * (1.0 - sgl)))
            dyc = _ln_bwd(dl * lg_ref[...], n, rs)
            dyc_ref[rows, :] = dyc
            return g_lg + _fold8(dl * n), g_lb + _fold8(dl), g_cb + _fold8(dyc)
        zero = jnp.zeros((8, D), F32)
        g_lg, g_lb, g_cb = _blocks(TM // RB, blk, (zero, zero, zero))
        st_ref[0] = st_ref[0] + g_lg
        st_ref[1] = st_ref[1] + g_lb
        st_ref[2] = st_ref[2] + g_cb

        @pl.when(_ring_full(i, nt))
        def _():
            gw_acc[...] += lax.dot_general(yp_s[...], dy_s[...], TN, preferred_element_type=F32)

        @pl.when(i == nt - 1)
        def _():
            gw_ref[...] = gw_acc[...].astype(BF16)

    tile = pl.BlockSpec((TM, D), lambda i: (i, 0))
    row = pl.BlockSpec((1, D), lambda i: (0, 0))
    ring = VMEM((_ring_tiles(nt) * TM, D), BF16)
    return pl.pallas_call(
        body, name="branch_a_bwd1", grid=(nt,),
        out_shape=(jax.ShapeDtypeStruct((NSEG, T, D), BF16), jax.ShapeDtypeStruct((T, D), F32),
                   jax.ShapeDtypeStruct((D, D), BF16), jax.ShapeDtypeStruct((3, 8, D), F32)),
        in_specs=[pl.BlockSpec(memory_space=ANY), tile, tile, pl.BlockSpec((TM, D), lambda i: (i, 2)), row, row,
                  pl.BlockSpec((None, D, D), lambda i: (0, 0, 0))],
        out_specs=(pl.BlockSpec((None, TM, D), lambda i: (2, i, 0)), tile, pl.BlockSpec((D, D), lambda i: (0, 0)),
                   pl.BlockSpec((3, 8, D), lambda i: (0, 0, 0))),
        scratch_shapes=[VMEM((TM, D), F32), ring, ring, VMEM((D, D), F32)],
        input_output_aliases={0: 0},
        compiler_params=_cp(("arbitrary",)),
    )(dp, dya, yc, p, lg, lb, wq)


def _branch_a_bwd2(dp, dyc, p, cw, gw_co, gw_so, gw_o, early):
    T = p.shape[0]
    nt = T // TM
    hb = TM // HALO

    def body(dp_in, dyc_ref, hdyc_ref, val_ref, glu_ref, cw_ref, gco_ref, gso_ref, go_ref, e_ref,
             dp_ref, gcw_ref, rq_ref, eg_ref, a_s, sd_ref, da_s, x_send, x_recv, x_loc, ag_send, ag_recv, ag_loc):
        del dp_in
        i = pl.program_id(0)
        start_exchange, wait_exchange = _owner_exchange((gco_ref, gso_ref, go_ref), rq_ref, x_send, x_recv, x_loc)
        ag_start, ag_relay, ag_finish = _allgather_phases([e_ref], lambda a, slot: eg_ref.at[slot],
                                                          ag_send, ag_recv, ag_loc)

        @pl.when(i == 0)
        def _():
            gcw_ref[...] = jnp.zeros_like(gcw_ref)
            ag_start()
            start_exchange()

        @pl.when(i == nt // 2)
        def _():
            ag_relay()

        def fill(rb, carry):
            rows = _rows(rb)
            a_s[rows, :] = _ld(val_ref, rows) * _sig(_ld(glu_ref, rows))
            return carry
        lax.fori_loop(0, TM // RB, fill, 0)
        sd_ref[0, 0:TM, :] = dyc_ref[...]
        sd_ref[0, TM:EXT, :] = jnp.where(i < nt - 1, hdyc_ref[...], 0.0)
        _shift_copies(sd_ref)

        def conv_t(rb, carry):
            r0 = rb * RBC
            accs = [jnp.zeros((8, D), F32)] * (RBC // 8)
            for k in range(KW):
                o = KW - 1 - k
                w = cw_ref[k]
                accs = [acc + w * sd_ref[o % 8, pl.ds(pl.multiple_of(r0 + 8 * (o // 8 + g), 8), 8), :]
                        for g, acc in enumerate(accs)]
            for g, acc in enumerate(accs):
                da_s[pl.ds(pl.multiple_of(r0 + 8 * g, 8), 8), :] = acc
            return carry
        lax.fori_loop(0, TM // RBC, conv_t, 0)

        for k0 in range(0, KW, TAPG):
            taps = list(range(k0, min(k0 + TAPG, KW)))

            def tap_group(rb, accs, taps=taps):
                for u in range(2):
                    r0 = rb * 32 + 16 * u
                    a = a_s[pl.ds(pl.multiple_of(r0, 16), 16), :]
                    out = []
                    for k, acc in zip(taps, accs):
                        o = KW - 1 - k
                        d = sd_ref[o % 8, pl.ds(pl.multiple_of(r0 + 8 * (o // 8), 8), 16), :]
                        out.append(acc + _fold8(a * d))
                    accs = tuple(out)
                return accs
            sums = lax.fori_loop(0, TM // 32, tap_group, tuple(jnp.zeros((8, D), F32) for _ in taps))
            for k, s in zip(taps, sums):
                gcw_ref[k] = gcw_ref[k] + s

        def glu_b(rb, carry):
            rows = _rows(rb)
            da = da_s[rows, :]
            sg = _sig(_ld(glu_ref, rows))
            dp_ref[0, rows, :] = (da * sg).astype(BF16)
            dp_ref[1, rows, :] = (da * _ld(val_ref, rows) * (sg * (1.0 - sg))).astype(BF16)
            return carry
        lax.fori_loop(0, TM // RB, glu_b, 0)

        @pl.when(i == nt - 1)
        def _():
            wait_exchange()
            ag_finish()

    tile = lambda seg: pl.BlockSpec((TM, D), lambda i: (i, seg))
    hbm = pl.BlockSpec(memory_space=ANY)
    return pl.pallas_call(
        body, name="branch_a_bwd2", grid=(nt,),
        out_shape=(jax.ShapeDtypeStruct((NSEG, T, D), BF16), jax.ShapeDtypeStruct((32, 8, D), F32),
                   jax.ShapeDtypeStruct((NDEV, 3, 128, D), BF16), jax.ShapeDtypeStruct((NDEV, NEARLY, D), F32)),
        in_specs=[hbm, pl.BlockSpec((TM, D), lambda i: (i, 0)),
                  pl.BlockSpec((HALO, D), lambda i: (jnp.minimum((i + 1) * hb, nt * hb - 1), 0)),
                  tile(0), tile(1), pl.BlockSpec((KW, 8, D), lambda i: (0, 0, 0)),
                  hbm, hbm, hbm, hbm],
        out_specs=(pl.BlockSpec((2, TM, D), lambda i: (0, i, 0)), pl.BlockSpec((32, 8, D), lambda i: (0, 0, 0)), hbm, hbm),
        scratch_shapes=[VMEM((TM, D), F32), VMEM((8, EXT, D), F32), VMEM((TM, D), F32),
                        pltpu.SemaphoreType.DMA((3, 7)), pltpu.SemaphoreType.DMA((3, 7)), pltpu.SemaphoreType.DMA((3,)),
                        pltpu.SemaphoreType.DMA((1, 7)), pltpu.SemaphoreType.DMA((1, 7)), pltpu.SemaphoreType.DMA((1,))],
        input_output_aliases={0: 0},
        compiler_params=_cp(("arbitrary",)),
    )(dp, dyc, dyc, p, p, cw, gw_co, gw_so, gw_o, early)


def _in_bwd_dx(dp, wing, x, dx2, mod, g_pre):
    T = x.shape[0]
    nt = T // TM

    def body(dp_ref, w_ref, x_ref, dx2_ref, mod_ref, g_ref, gx_ref, st_ref, dh_s):
        i = pl.program_id(0)

        @pl.when(i == 0)
        def _():
            st_ref[...] = jnp.zeros_like(st_ref)

        dh_s[...] = lax.dot_general(dp_ref[0], w_ref[0], NT, preferred_element_type=F32)
        for j in range(1, NSEG):
            dh_s[...] += lax.dot_general(dp_ref[j], w_ref[j], NT, preferred_element_type=F32)

        def blk(rb, carry):
            d_sh, d_sc, g_g = carry
            rows = _rows(rb)
            xv = x_ref[rows, :]
            r = lax.rsqrt(jnp.mean(xv * xv, axis=-1, keepdims=True) + EPS)
            xn = xv * r
            g = g_ref[...]
            hpre = xn * g
            dh = dh_s[rows, :]
            dhp = dh * (1.0 + mod_ref[1:2, :])
            dxn = dhp * g
            gx_ref[rows, :] = dx2_ref[rows, :] + r * (dxn - xn * jnp.mean(dxn * xn, axis=-1, keepdims=True))
            return d_sh + _fold8(dh), d_sc + _fold8(dh * hpre), g_g + _fold8(dhp * xn)
        zero = jnp.zeros((8, D), F32)
        d_sh, d_sc, g_g = _blocks(TM // RB, blk, (zero, zero, zero))
        st_ref[0] = st_ref[0] + d_sh
        st_ref[1] = st_ref[1] + d_sc
        st_ref[2] = st_ref[2] + g_g

    tile = pl.BlockSpec((TM, D), lambda i: (i, 0))
    return pl.pallas_call(
        body, name="in_bwd_dx", grid=(nt,),
        out_shape=(jax.ShapeDtypeStruct((T, D), F32), jax.ShapeDtypeStruct((3, 8, D), F32)),
        in_specs=[pl.BlockSpec((NSEG, TM, D), lambda i: (0, i, 0)),
                  pl.BlockSpec((NSEG, D, D), lambda i: (0, 0, 0), pipeline_mode=pl.Buffered(1)),
                  tile, tile, pl.BlockSpec((3, D), lambda i: (0, 0)), pl.BlockSpec((1, D), lambda i: (0, 0))],
        out_specs=(tile, pl.BlockSpec((3, 8, D), lambda i: (0, 0, 0))),
        scratch_shapes=[VMEM((TM, D), F32)],
        compiler_params=_cp(("arbitrary",), vmem_mb=56),
    )(dp, wing, x, dx2, mod, g_pre)


RS_ORDER = (5, 4, 3, 2, 7, 6, 1, 0)
RS_HALF = D // 2


def _in_bwd_dw(ht, dp, order, smalls):
    T = ht.shape[1]
    tmw = min(4 * TMB, T)
    nb = T // tmw
    nu = 2 * NSEG

    def body(order_ref, ht_ref, dp_ref, sm_ref, g_ref, smg_ref, acc, sbuf, rsib, rici, send_sems, recv_sems,
             ag_send, ag_recv, ag_loc):
        j = pl.program_id(0)
        i = pl.program_id(1)
        x, y, c = _me()
        sibling = (x, y, 1 - c)
        chips = [(1 - x, y), (x, 1 - y), (1 - x, 1 - y)]
        ag_start, ag_relay, ag_finish = _allgather_phases([sm_ref], lambda a, slot: smg_ref.at[slot],
                                                          ag_send, ag_recv, ag_loc)

        def to_sibling(s):
            return pltpu.make_async_remote_copy(src_ref=sbuf.at[s], dst_ref=rsib.at[s], send_sem=send_sems.at[s],
                                                recv_sem=recv_sems.at[s], device_id=sibling, device_id_type=MESH)

        def to_owner(t):
            return pltpu.make_async_remote_copy(src_ref=sbuf.at[8 + t], dst_ref=rici.at[t], send_sem=send_sems.at[8 + t],
                                                recv_sem=recv_sems.at[8 + t], device_id=(*chips[t // 2], c),
                                                device_id_type=MESH)

        @pl.when((j == 0) & (i == 0))
        def _():
            ag_start()

        @pl.when((j == 4) & (i == 0))
        def _():
            ag_relay()

        @pl.when(i == 0)
        def _():
            acc[...] = jnp.zeros_like(acc)
        acc[...] += jnp.dot(ht_ref[...], dp_ref[...], preferred_element_type=F32)

        for jj in range(nu):
            chip, half, mine = jj // 4, (jj % 4) // 2, jj % 2
            s = 2 * chip + half

            @pl.when((j == jj) & (i == nb - 1))
            def _(chip=chip, half=half, mine=mine, s=s):
                if not mine:
                    sbuf[s] = acc[...].astype(BF16)
                    to_sibling(s).start()
                elif chip < 3:
                    to_sibling(s).wait_recv()
                    sbuf[8 + s] = (acc[...] + rsib[s].astype(F32)).astype(BF16)
                    to_owner(s).start()
                else:
                    to_sibling(s).wait_recv()
                    g = acc[...] + rsib[s].astype(F32)
                    for q in range(3):
                        to_owner(2 * q + half).wait_recv()
                        g = g + rici[2 * q + half].astype(F32)
                    g_ref[:, half * RS_HALF:(half + 1) * RS_HALF] = g

        @pl.when((j == nu - 1) & (i == nb - 1))
        def _():
            for s in range(8):
                to_sibling(s).wait_send()
            for t in range(6):
                to_owner(t).wait_send()
            ag_finish()

    hbm = pl.BlockSpec(memory_space=ANY)
    return pl.pallas_call(
        body, name="in_bwd_dw",
        out_shape=(jax.ShapeDtypeStruct((D, D), F32), jax.ShapeDtypeStruct((NDEV, NLATE, D), F32)),
        grid_spec=pltpu.PrefetchScalarGridSpec(
            num_scalar_prefetch=1, grid=(nu, nb),
            in_specs=[pl.BlockSpec((D, tmw), lambda j, i, o: (0, i)),
                      pl.BlockSpec((None, tmw, RS_HALF), lambda j, i, o: (o[2 * (j // 4) + j % 2], i, (j % 4) // 2)),
                      hbm],
            out_specs=(pl.BlockSpec((D, D), lambda j, i, o: (0, 0)), hbm),
            scratch_shapes=[VMEM((D, RS_HALF), F32), VMEM((14, D, RS_HALF), BF16), VMEM((8, D, RS_HALF), BF16),
                            VMEM((6, D, RS_HALF), BF16), pltpu.SemaphoreType.DMA((14,)), pltpu.SemaphoreType.DMA((14,)),
                            pltpu.SemaphoreType.DMA((1, 7)), pltpu.SemaphoreType.DMA((1, 7)), pltpu.SemaphoreType.DMA((1,))]),
        compiler_params=_cp(("arbitrary", "arbitrary"), vmem_mb=56),
    )(order, ht, dp, smalls)


def _pack_early(st_a, st_b, st_m, gws, gbs):
    def body(sa_ref, sb_ref, sm_ref, gws_ref, gbs_ref, out_ref):
        out_ref[...] = jnp.zeros_like(out_ref)
        fold = lambda v: jnp.sum(v, axis=0, keepdims=True)
        out_ref[0:1, :] = fold(sm_ref[2])
        out_ref[1:2, :] = fold(sa_ref[2])
        out_ref[2:3, :] = fold(sa_ref[0])
        out_ref[3:4, :] = fold(sa_ref[1])
        out_ref[4:5, :] = fold(sb_ref[0])
        out_ref[5:6, :] = fold(sb_ref[1])
        out_ref[7:8, :] = fold(sm_ref[1])
        out_ref[8:9, :] = fold(sm_ref[0])
        tri = lax.broadcasted_iota(jnp.int32, (CHUNK, CHUNK), 0) >= lax.broadcasted_iota(jnp.int32, (CHUNK, CHUNK), 1)
        for h in range(HEADS):
            out_ref[16:16 + CHUNK, h * CHUNK:(h + 1) * CHUNK] = jnp.where(tri, gws_ref[h], 0.0)
            out_ref[6:7, h * CHUNK:(h + 1) * CHUNK] = fold(gbs_ref[h].T)

    vm = pl.BlockSpec(memory_space=VMEM)
    return pl.pallas_call(
        body, name="pack_early", out_shape=jax.ShapeDtypeStruct((NEARLY, D), F32),
        in_specs=[vm] * 5, out_specs=vm, compiler_params=_cp(vmem_mb=32),
    )(st_a, st_b, st_m, gws, gbs)


def _pack_late(st_x, gcw):
    def body(sx_ref, gcw_ref, out_ref):
        out_ref[...] = jnp.zeros_like(out_ref)
        fold = lambda v: jnp.sum(v, axis=0, keepdims=True)
        for r in range(3):
            out_ref[r:r + 1, :] = fold(sx_ref[r])
        for k in range(KW):
            out_ref[8 + k:9 + k, :] = fold(gcw_ref[k])

    vm = pl.BlockSpec(memory_space=VMEM)
    return pl.pallas_call(
        body, name="pack_late", out_shape=jax.ShapeDtypeStruct((NLATE, D), F32),
        in_specs=[vm] * 2, out_specs=vm, compiler_params=_cp(vmem_mb=32),
    )(st_x, gcw)


def _pack_params(name, b_ada, g_pre, conv_b, lg, lb, sg, sb, b_sgu, g_final, w_sgu_t):
    def body(ba_ref, *refs):
        rows, ws_ref, out_ref = refs[:8], refs[8], refs[9]
        out_ref[...] = jnp.zeros_like(out_ref)
        for r in range(3):
            out_ref[r:r + 1, :] = ba_ref[:, r * D:(r + 1) * D]
        for r, ref in enumerate(rows):
            out_ref[3 + r:4 + r, :] = ref[...]
        out_ref[ROW_WS:ROW_WS + CHUNK, :] = ws_ref[...]

    vm = pl.BlockSpec(memory_space=VMEM)
    return pl.pallas_call(
        body, name=name, out_shape=jax.ShapeDtypeStruct((NSMALL, D), F32),
        in_specs=[vm] * 10, out_specs=vm, compiler_params=_cp(vmem_mb=32),
    )(b_ada, g_pre, conv_b, lg, lb, sg, sb, b_sgu, g_final, w_sgu_t)


def _adam(w, g, m, v):
    m2 = ADAM_B1 * m + (1.0 - ADAM_B1) * g
    v2 = ADAM_B2 * v + (1.0 - ADAM_B2) * (g * g)
    m_hat = m2 / (1.0 - ADAM_B1 ** ADAM_STEP)
    v_hat = v2 / (1.0 - ADAM_B2 ** ADAM_STEP)
    delta = -ADAM_LR * (m_hat / (jnp.sqrt(v_hat) + ADAM_EPS) + ADAM_WD * w)
    return delta, m2, v2


def _small_finish(eg, lg, w, m, v):
    def body(eg_ref, lg_ref, w_ref, m_ref, v_ref, loss_ref, g_ref, d_ref, m2_ref, v2_ref):
        e = eg_ref[0]
        l = lg_ref[0]
        for s in range(1, NDEV):
            e = e + eg_ref[s]
            l = l + lg_ref[s]
        g_ref[...] = jnp.zeros_like(g_ref)
        g_ref[0:2, :] = l[0:2, :]
        g_ref[2:3, :] = e[0:1, :]
        g_ref[3:4, :] = l[2:3, :]
        g_ref[4:12, :] = e[1:9, :]
        g_ref[ROW_CW:ROW_CW + KW, :] = l[8:8 + KW, :]
        g_ref[ROW_WS:ROW_WS + CHUNK, :] = e[16:16 + CHUNK, :]
        loss_ref[...] = jnp.sum(e[8:9, :], axis=1, keepdims=True)
        d_ref[...], m2_ref[...], v2_ref[...] = _adam(w_ref[...], g_ref[...], m_ref[...], v_ref[...])

    vm = pl.BlockSpec(memory_space=VMEM)
    sd = jax.ShapeDtypeStruct((NSMALL, D), F32)
    return pl.pallas_call(
        body, name="small_finish", out_shape=(jax.ShapeDtypeStruct((1, 1), F32), sd, sd, sd, sd),
        in_specs=[vm] * 5, out_specs=(vm,) * 5, compiler_params=_cp(vmem_mb=32),
    )(eg, lg, w, m, v)


def _ada_grad_adam(ct, dm, w, m, v):
    def body(ct_ref, dm_ref, w_ref, m_ref, v_ref, g_ref, d_ref, m2_ref, v2_ref):
        g = ct_ref[:, 0:1] * dm_ref[0:1, :]
        for b in range(1, NDEV):
            g = g + ct_ref[:, b:b + 1] * dm_ref[b:b + 1, :]
        g_ref[...] = g
        d_ref[...], m2_ref[...], v2_ref[...] = _adam(w_ref[...], g, m_ref[...], v_ref[...])

    vm = pl.BlockSpec(memory_space=VMEM)
    sd = jax.ShapeDtypeStruct(w.shape, F32)
    return pl.pallas_call(
        body, name="ada_grad_adam", out_shape=(sd, sd, sd, sd),
        in_specs=[vm] * 5, out_specs=(vm,) * 4, compiler_params=_cp(vmem_mb=32),
    )(ct, dm, w, m, v)


def _adam_f32(name, g, w, m, v, rows=None):
    R, C = w.shape
    rows = R if rows is None else rows

    def body(g_ref, w_ref, m_ref, v_ref, d_ref, m2_ref, v2_ref):
        d_ref[...], m2_ref[...], v2_ref[...] = _adam(w_ref[...], g_ref[...], m_ref[...], v_ref[...])

    tile = pl.BlockSpec((rows, C), lambda i: (i, 0))
    sd = jax.ShapeDtypeStruct(w.shape, F32)
    return pl.pallas_call(
        body, name=name, grid=(R // rows,), out_shape=(sd, sd, sd), in_specs=[tile] * 4, out_specs=(tile,) * 3,
        compiler_params=_cp(("parallel",), vmem_mb=32),
    )(g, w, m, v)


def _adam_reduce(name, recv, w, m, v, rows, recv_spec):
    R, C = w.shape

    def body(r_ref, w_ref, m_ref, v_ref, g_ref, d_ref, m2_ref, v2_ref):
        g = r_ref[0].astype(F32)
        for s in range(1, NDEV):
            g = g + r_ref[s].astype(F32)
        g_ref[...] = g
        d_ref[...], m2_ref[...], v2_ref[...] = _adam(w_ref[...], g, m_ref[...], v_ref[...])

    tile = pl.BlockSpec((rows, C), lambda i: (i, 0))
    sd = jax.ShapeDtypeStruct((R, C), F32)
    return pl.pallas_call(
        body, name=name, grid=(R // rows,), out_shape=(sd, sd, sd, sd),
        in_specs=[recv_spec, tile, tile, tile], out_specs=(tile,) * 4,
        compiler_params=_cp(("parallel",)),
    )(recv, w, m, v)


def kernel(x, c, w_ada, b_ada, g_pre, w_in, conv_w, conv_b, conv_ln_g, conv_ln_b, w_conv_out, sgu_ln_g, sgu_ln_b, w_sgu, b_sgu, w_sgu_out, w_o, g_final, loss_target, m_w_ada, m_b_ada, m_g_pre, m_w_in, m_conv_w, m_conv_b, m_conv_ln_g, m_conv_ln_b, m_w_conv_out, m_sgu_ln_g, m_sgu_ln_b, m_w_sgu, m_b_sgu, m_w_sgu_out, m_w_o, m_g_final, v_w_ada, v_b_ada, v_g_pre, v_w_in, v_conv_w, v_conv_b, v_conv_ln_g, v_conv_ln_b, v_w_conv_out, v_sgu_ln_g, v_sgu_ln_b, v_w_sgu, v_b_sgu, v_w_sgu_out, v_w_o, v_g_final):
    T = x.shape[1]
    assert T % TMB == 0 and x.shape[2] == D
    xs = x[0]
    tgt = loss_target[0]
    my = 4 * lax.axis_index("x") + 2 * lax.axis_index("y") + lax.axis_index("c")

    mod, cg = _ada_fwd(c, w_ada[0], b_ada)
    gf = g_final.reshape(1, D)
    bst = b_sgu[0].T

    h, ht = _prep_h(xs, mod, g_pre)
    ag_order = jnp.stack([_slot(*b) for b in _ag_blocks(lax.axis_index("x"), lax.axis_index("y"), lax.axis_index("c"))])
    p, wing, wqg, cwg = _in_proj(h, w_in[0], ag_order.astype(jnp.int32),
                                 w_conv_out[0], w_sgu_out[0], w_o[0], conv_w[0])
    wq = wqg.reshape(3, D, D)
    cw = jnp.broadcast_to(jnp.transpose(cwg, (1, 0, 2)).reshape(KW, 1, D), (KW, 8, D))
    yc, ya = _branch_a_fwd(p, cw, conv_b, conv_ln_g, conv_ln_b, wq)
    yb = _branch_b_fwd(p, w_sgu[0], bst, sgu_ln_g, sgu_ln_b, wq)
    dx2, dya, dyb, dp, gw_o, st_m = _merge_loss(p, ya, yb, xs, tgt, mod, gf, wq)

    dp, gw_so, gws, gbs, st_b = _branch_b_bwd(dp, dyb, p, w_sgu[0], bst, sgu_ln_g, sgu_ln_b, wq)
    dp, dyc, gw_co, st_a = _branch_a_bwd1(dp, dya, yc, p, conv_ln_g, conv_ln_b, wq)
    early = _pack_early(st_a, st_b, st_m, gws, gbs)
    dp, gcw, rq, eg = _branch_a_bwd2(dp, dyc, p, cw, gw_co, gw_so, gw_o, early)
    grad_x, st_x = _in_bwd_dx(dp, wing, xs, dx2, mod, g_pre)
    late = _pack_late(st_x, gcw)
    g_in, lg = _in_bwd_dw(ht, dp, my ^ jnp.array(RS_ORDER, jnp.int32), late)

    def pack(name, b_ada_, g_pre_, conv_b_, lg_, lb_, sg_, sb_, b_sgu_, gfin_, w_sgu_):
        return _pack_params(name, b_ada_, g_pre_, conv_b_, lg_, lb_, sg_, sb_, b_sgu_.reshape(1, D), gfin_.reshape(1, D),
                            jnp.transpose(w_sgu_[0], (1, 0, 2)).reshape(CHUNK, D))

    pw = pack("pack_w", b_ada, g_pre, conv_b, conv_ln_g, conv_ln_b, sgu_ln_g, sgu_ln_b, b_sgu, g_final, w_sgu)
    pm = pack("pack_m", m_b_ada, m_g_pre, m_conv_b, m_conv_ln_g, m_conv_ln_b, m_sgu_ln_g, m_sgu_ln_b, m_b_sgu, m_g_final,
              m_w_sgu)
    pv = pack("pack_v", v_b_ada, v_g_pre, v_conv_b, v_conv_ln_g, v_conv_ln_b, v_sgu_ln_g, v_sgu_ln_b, v_b_sgu, v_g_final,
              v_w_sgu)
    loss11, sg_, sd_, sm_, sv_ = _small_finish(eg, lg, pw, pm, pv)

    def unpack(a):
        return dict(
            b_ada=a[0:3].reshape(1, 3 * D), g_pre=a[3:4], conv_b=a[4:5], conv_ln_g=a[5:6], conv_ln_b=a[6:7],
            sgu_ln_g=a[7:8], sgu_ln_b=a[8:9], b_sgu=a[9:10].reshape(1, HEADS, CHUNK), g_final=a[10],
            w_sgu=jnp.transpose(a[ROW_WS:ROW_WS + CHUNK].reshape(CHUNK, HEADS, CHUNK), (1, 0, 2))[None])
    small = [unpack(a) for a in (sg_, sd_, sm_, sv_)]

    g_cw = lax.dynamic_slice_in_dim(sg_[ROW_CW:ROW_CW + KW], my * 128, 128, axis=1)
    d_cw, m_cw, v_cw = _adam_f32("adam_conv_w", g_cw, conv_w[0], m_conv_w[0], v_conv_w[0])

    wcols = w_ada.shape[2]
    dm_all = jnp.concatenate([lg[:, 0], lg[:, 1], eg[:, 0]], axis=1)
    dm_mine = lax.dynamic_slice_in_dim(dm_all, my * wcols, wcols, axis=1)
    g_ada, d_ada, m_ada, v_ada = _ada_grad_adam(cg.T, dm_mine, w_ada[0], m_w_ada[0], v_w_ada[0])

    d_in, m_in, v_in = _adam_f32("adam_w_in", g_in, w_in[0], m_w_in[0], v_w_in[0], rows=256)
    big = {}
    for j, (nm, w_, m_, v_) in enumerate((("w_conv_out", w_conv_out, m_w_conv_out, v_w_conv_out),
                                          ("w_sgu_out", w_sgu_out, m_w_sgu_out, v_w_sgu_out),
                                          ("w_o", w_o, m_w_o, v_w_o))):
        big[nm] = _adam_reduce("adam_" + nm, rq, w_[0], m_[0], v_[0], 128,
                               pl.BlockSpec((NDEV, None, 128, D), lambda i, j=j: (0, j, 0, 0)))

    per = {
        "w_ada": tuple(a[None] for a in (g_ada, d_ada, m_ada, v_ada)),
        "w_in": tuple(a[None] for a in (g_in, d_in, m_in, v_in)),
        "conv_w": tuple(a[None] for a in (g_cw, d_cw, m_cw, v_cw)),
    }
    for nm in ("w_conv_out", "w_sgu_out", "w_o"):
        per[nm] = tuple(a[None] for a in big[nm])
    for nm in ("b_ada", "g_pre", "conv_b", "conv_ln_g", "conv_ln_b", "sgu_ln_g", "sgu_ln_b", "w_sgu", "b_sgu", "g_final"):
        per[nm] = tuple(s[nm] for s in small)

    order = ["w_ada", "b_ada", "g_pre", "w_in", "conv_w", "conv_b", "conv_ln_g", "conv_ln_b", "w_conv_out",
             "sgu_ln_g", "sgu_ln_b", "w_sgu", "b_sgu", "w_sgu_out", "w_o", "g_final"]
    outs = [loss11.reshape(()), grad_x[None]]
    for part in range(4):
        outs += [per[nm][part] for nm in order]
    return tuple(outs)
```

```python
import jax
import jax.numpy as jnp
from jax import lax
from jax.experimental import pallas as pl
from jax.experimental.pallas import tpu as pltpu

F32 = jnp.float32
BF16 = jnp.bfloat16
MESH = pl.DeviceIdType.MESH
VMEM = pltpu.VMEM
ANY = pl.ANY

D = 1024
NDEV = 8
NSEG = 8
HEADS = 8
HD = 128
CHUNK = 128
KW = 31
HALO = 32
EPS = 1e-6
TM = 256
TMB = 512
RB = 32
UNROLL = 4
RBC = 32
TAPG = 4
EXT = TM + HALO
NSMALL = 176
NEARLY = 144
NLATE = 40
ROW_CW = 16
ROW_WS = 48
W_CO, W_SO, W_O = 0, 1, 2

ADAM_LR = 0.001
ADAM_B1 = 0.9
ADAM_B2 = 0.999
ADAM_EPS = 1e-08
ADAM_WD = 0.01
ADAM_STEP = 10

INV_SQRT2 = 0.7071067811865476
INV_SQRT_2PI = 0.3989422804014327

NT = (((1,), (1,)), ((), ()))
TN = (((0,), (0,)), ((), ()))


def _cp(sem=None, vmem_mb=48):
    return pltpu.CompilerParams(dimension_semantics=sem, vmem_limit_bytes=vmem_mb * 1024 * 1024)


def _me():
    return lax.axis_index("x"), lax.axis_index("y"), lax.axis_index("c")


def _slot(px, py, pc):
    return 4 * px + 2 * py + pc


def _xor_peer(k):
    x, y, c = _me()
    return (1 - x if k & 4 else x, 1 - y if k & 2 else y, 1 - c if k & 1 else c)


def _allgather_phases(srcs, dst_at, send_sems, recv_sems, loc_sems):
    x, y, c = _me()
    me = (x, y, c)
    sibling = (x, y, 1 - c)
    chips = [(1 - x, y), (x, 1 - y), (1 - x, 1 - y)]
    na = len(srcs)

    def copy(a, k, block, to, src=None):
        d = dst_at(a, _slot(*block))
        return pltpu.make_async_remote_copy(
            src_ref=d if src is None else src, dst_ref=d,
            send_sem=send_sems.at[a, k], recv_sem=recv_sems.at[a, k],
            device_id=to, device_id_type=MESH)

    local = [pltpu.make_async_copy(srcs[a], dst_at(a, _slot(*me)), loc_sems.at[a]) for a in range(na)]
    first = []
    for a in range(na):
        first.append(copy(a, 0, me, sibling, src=srcs[a]))
        for j, chip in enumerate(chips):
            first.append(copy(a, 1 + j, me, (*chip, c), src=srcs[a]))
    passed = [copy(a, 4 + j, (*chip, c), sibling) for j, chip in enumerate(chips) for a in range(na)]

    def start():
        for cp in local + first:
            cp.start()

    def relay():
        for j, chip in enumerate(chips):
            for a in range(na):
                copy(a, 1 + j, (*chip, c), me).wait_recv()
                passed[j * na + a].start()

    def finish():
        for a in range(na):
            copy(a, 0, sibling, me).wait_recv()
        for j, chip in enumerate(chips):
            for a in range(na):
                copy(a, 4 + j, (*chip, 1 - c), me).wait_recv()
        for cp in first + passed:
            cp.wait_send()
        for cp in local:
            cp.wait()

    return start, relay, finish


def _allgather(srcs, dst_at, send_sems, recv_sems, loc_sems):
    start, relay, finish = _allgather_phases(srcs, dst_at, send_sems, recv_sems, loc_sems)
    start()
    relay()
    finish()


def _owner_exchange(gq, rq_ref, x_send, x_recv, x_loc):
    mx, my_, mc = _me()
    me = _slot(mx, my_, mc)

    def rows_of(a, slot):
        return gq[a].at[pl.ds(pl.multiple_of(slot * 128, 128), 128), :]

    def exchange(k, a, recv):
        px, py, pc = _xor_peer(k)
        peer = _slot(px, py, pc)
        return pltpu.make_async_remote_copy(
            src_ref=rows_of(a, me if recv else peer), dst_ref=rq_ref.at[peer if recv else me, a],
            send_sem=x_send.at[a, k - 1], recv_sem=x_recv.at[a, k - 1],
            device_id=(px, py, pc), device_id_type=MESH)

    local = [pltpu.make_async_copy(rows_of(a, me), rq_ref.at[me, a], x_loc.at[a]) for a in range(len(gq))]

    def start():
        for cp in local:
            cp.start()
        for k in range(1, NDEV):
            for a in range(len(gq)):
                exchange(k, a, False).start()

    def wait():
        for k in range(1, NDEV):
            for a in range(len(gq)):
                exchange(k, a, True).wait_recv()
        for k in range(1, NDEV):
            for a in range(len(gq)):
                exchange(k, a, False).wait_send()
        for cp in local:
            cp.wait()

    return start, wait


def _sig(v):
    return jax.nn.sigmoid(v)


def _gelu_parts(v):
    cdf = 0.5 * (1.0 + lax.erf(v * INV_SQRT2))
    pdf = jnp.exp(-0.5 * v * v) * INV_SQRT_2PI
    return v * cdf, cdf + v * pdf


def _ln_stats(v):
    mu = jnp.mean(v, axis=-1, keepdims=True)
    xc = v - mu
    var = jnp.mean(xc * xc, axis=-1, keepdims=True)
    rs = lax.rsqrt(var + EPS)
    return xc * rs, rs


def _ln_bwd(dn, n, rs):
    return rs * (dn - jnp.mean(dn, axis=-1, keepdims=True) - n * jnp.mean(dn * n, axis=-1, keepdims=True))


def _fold8(v):
    acc = v[0:8]
    for r in range(8, v.shape[0], 8):
        acc = acc + v[r:r + 8]
    return acc


def _rows(rb, n=RB):
    return pl.ds(pl.multiple_of(rb * n, n), n)


def _ring_tiles(nt):
    return 4 if nt % 4 == 0 else (2 if nt % 2 == 0 else 1)


def _ring_base(i, nt):
    return pl.multiple_of((i % _ring_tiles(nt)) * TM, TM)


def _ring_rows(base, rb, n=RB):
    return pl.ds(pl.multiple_of(base + rb * n, n), n)


def _ring_full(i, nt):
    g = _ring_tiles(nt)
    return i % g == g - 1


def _ld(ref, rows):
    return ref[rows, :].astype(F32)


def _blocks(n, body, init=0, unroll=UNROLL):
    def trip(t, carry):
        for u in range(unroll):
            carry = body(t * unroll + u, carry)
        return carry
    return lax.fori_loop(0, n // unroll, trip, init)


def _shift_copies(s_ref):
    n = EXT - 8
    for m in range(1, 8):
        for c0 in range(0, n, 56):
            s_ref[m, c0:c0 + 56, :] = s_ref[0, c0 + m:c0 + m + 56, :]


def _ada_exchange(c_ref, w_ref, b_ref, mod_ref, call_ref, cdst, msrc, mdst, c_send, c_recv, m_send, m_recv):
    x, y, c = _me()
    me = _slot(x, y, c)

    def all_to_all(src_of, dst, ss, rs):
        def cp(k, recv):
            px, py, pc = _xor_peer(k)
            peer = _slot(px, py, pc)
            return pltpu.make_async_remote_copy(
                src_ref=src_of(me if recv else peer), dst_ref=dst.at[peer if recv else me],
                send_sem=ss.at[k - 1], recv_sem=rs.at[k - 1], device_id=(px, py, pc), device_id_type=MESH)
        for k in range(1, NDEV):
            cp(k, False).start()
        for k in range(1, NDEV):
            cp(k, True).wait_recv()
        for k in range(1, NDEV):
            cp(k, False).wait_send()

    cdst[me] = c_ref[...]
    all_to_all(lambda s: c_ref, cdst, c_send, c_recv)
    for b in range(NDEV):
        call_ref[b:b + 1, :] = cdst[b]
    m = jnp.dot(call_ref[...], w_ref[...], preferred_element_type=F32, precision=lax.Precision.HIGHEST)
    for b in range(NDEV):
        msrc[b] = m[b:b + 1, :]
    mdst[me] = msrc[me]
    all_to_all(lambda s: msrc.at[s], mdst, m_send, m_recv)
    full = jnp.concatenate([mdst[k] for k in range(NDEV)], axis=1) + b_ref[...]
    for r in range(3):
        mod_ref[r:r + 1, :] = full[:, r * D:(r + 1) * D]


def _ag_blocks(x, y, c):
    a = 1 - c
    n1 = (x ^ a, y ^ c)
    n2 = (x ^ c, y ^ a)
    dg = (1 - x, 1 - y)
    return [(x, y, c), (x, y, a), (*n1, c), (*n2, c), (*n1, a), (*n2, a), (*dg, c), (*dg, a)]


def _in_proj(x, c, w_ada, b_ada, g_pre, w_in, order, w_co, w_so, w_o, conv_w):
    T = x.shape[0]
    tmp = min(2 * TMB, T)
    nb = T // tmp
    wcols = w_ada.shape[1]

    def body(order_ref, x_ref, c_ref, wada_ref, bada_ref, g_ref, w_ref, wco_ref, wso_ref, wo_ref, cw_ref,
             p_ref, wing_ref, wqg_ref, cwg_ref, ht_ref, mod_ref, call_ref,
             wbuf, wq_st, h_s, ht_s, cdst, msrc, mdst, send_sems, recv_sems, out_sems, ag_send, ag_recv, ag_loc,
             c_send, c_recv, m_send, m_recv, ht_sems):
        j = pl.program_id(0)
        i = pl.program_id(1)
        tok = pl.ds(pl.multiple_of(i * tmp, tmp), tmp)

        def ht_out():
            return pltpu.make_async_copy(ht_s.at[:, tok], ht_ref.at[:, tok], ht_sems.at[i])
        blocks = _ag_blocks(*_me())
        me, sibling, n1, n2, n1o, n2o, dg, dgo = blocks
        ag_start, ag_relay, ag_finish = _allgather_phases(
            [wq_st, cw_ref], lambda a, slot: wqg_ref.at[:, slot] if a == 0 else cwg_ref.at[slot], ag_send, ag_recv, ag_loc)

        def copy(k, block, to):
            d = wbuf.at[_slot(*block)]
            return pltpu.make_async_remote_copy(src_ref=d, dst_ref=d, send_sem=send_sems.at[k], recv_sem=recv_sems.at[k],
                                                device_id=to, device_id_type=MESH)

        def writeback(jj):
            s = _slot(*blocks[jj])
            return pltpu.make_async_copy(wbuf.at[s], wing_ref.at[s], out_sems.at[jj])

        sends = [copy(0, me, sibling), copy(1, me, n1), copy(2, me, n2)]
        relay = copy(3, n1, n2)
        passed = [copy(4, n1, sibling), copy(5, n2, sibling), copy(6, dg, sibling)]
        arrivals = {1: [(0, sibling)], 2: [(1, n1), (2, n2)], 4: [(5, n1o)], 5: [(4, n2o)], 6: [(3, dg)], 7: [(6, dgo)]}

        @pl.when((j == 0) & (i == 0))
        def _():
            wbuf[_slot(*me)] = w_ref[...].astype(BF16)
            for cp in sends:
                cp.start()
            writeback(0).start()
            wq_st[W_CO] = wco_ref[...].astype(BF16)
            wq_st[W_SO] = wso_ref[...].astype(BF16)
            wq_st[W_O] = wo_ref[...].astype(BF16)
            ag_start()
            _ada_exchange(c_ref, wada_ref, bada_ref, mod_ref, call_ref, cdst, msrc, mdst, c_send, c_recv, m_send, m_recv)

        @pl.when(j == 0)
        def _():
            def blk(rb, carry):
                rows = _rows(rb)
                xv = x_ref[rows, :]
                r = lax.rsqrt(jnp.mean(xv * xv, axis=-1, keepdims=True) + EPS)
                hv = (xv * r) * g_ref[...] * (1.0 + mod_ref[1:2, :]) + mod_ref[0:1, :]
                h_s[pl.ds(pl.multiple_of(i * tmp + rb * RB, RB), RB), :] = hv.astype(BF16)
                return carry
            _blocks(tmp // RB, blk)
            ht_s[:, tok] = h_s[tok, :].T
            ht_out().start()

        for jj in range(1, NSEG):
            @pl.when((j == jj) & (i == 0))
            def _(jj=jj):
                for k, block in arrivals.get(jj, []):
                    copy(k, block, me).wait_recv()
                if jj == 2:
                    relay.start()
                    passed[0].start()
                    passed[1].start()
                    writeback(3).start()
                if jj == 6:
                    passed[2].start()
                if jj != 3:
                    writeback(jj).start()
                if jj == NSEG - 1:
                    ag_relay()

        p_ref[...] = jnp.dot(h_s[tok, :], wbuf[order_ref[j]], preferred_element_type=F32).astype(BF16)

        @pl.when((j == NSEG - 1) & (i == nb - 1))
        def _():
            for cp in sends + [relay] + passed:
                cp.wait_send()
            for jj in range(NSEG):
                writeback(jj).wait()
            for ii in range(nb):
                t = pl.ds(ii * tmp, tmp)
                pltpu.make_async_copy(ht_s.at[:, t], ht_ref.at[:, t], ht_sems.at[ii]).wait()
            ag_finish()

    vm = pl.BlockSpec(memory_space=VMEM)
    hbm = pl.BlockSpec(memory_space=ANY)
    return pl.pallas_call(
        body, name="in_proj",
        out_shape=(jax.ShapeDtypeStruct((T, NSEG * D), BF16), jax.ShapeDtypeStruct((NSEG, D, D), BF16),
                   jax.ShapeDtypeStruct((3, NDEV, 128, D), BF16), jax.ShapeDtypeStruct((NDEV, KW, 128), F32),
                   jax.ShapeDtypeStruct((D, T), BF16), jax.ShapeDtypeStruct((3, D), F32), jax.ShapeDtypeStruct((NDEV, D), F32)),
        grid_spec=pltpu.PrefetchScalarGridSpec(
            num_scalar_prefetch=1, grid=(NSEG, nb),
            in_specs=[pl.BlockSpec((tmp, D), lambda j, i, o: (jnp.where(j == 0, i, nb - 1), 0)),
                      vm, vm, vm, vm, vm, vm, vm, vm, vm],
            out_specs=(pl.BlockSpec((tmp, D), lambda j, i, o: (i, o[j])), hbm, hbm, hbm, hbm, vm, vm),
            scratch_shapes=[VMEM((NSEG, D, D), BF16), VMEM((3, 128, D), BF16), VMEM((T, D), BF16), VMEM((D, T), BF16),
                            VMEM((NDEV, 1, D), F32), VMEM((NDEV, 1, wcols), F32), VMEM((NDEV, 1, wcols), F32),
                            pltpu.SemaphoreType.DMA((7,)), pltpu.SemaphoreType.DMA((7,)), pltpu.SemaphoreType.DMA((NSEG,)),
                            pltpu.SemaphoreType.DMA((2, 7)), pltpu.SemaphoreType.DMA((2, 7)), pltpu.SemaphoreType.DMA((2,)),
                            pltpu.SemaphoreType.DMA((7,)), pltpu.SemaphoreType.DMA((7,)),
                            pltpu.SemaphoreType.DMA((7,)), pltpu.SemaphoreType.DMA((7,)), pltpu.SemaphoreType.DMA((nb,))]),
        compiler_params=_cp(("arbitrary", "arbitrary"), vmem_mb=60),
    )(order, x, c, w_ada, b_ada, g_pre, w_in, w_co, w_so, w_o, conv_w)


def _fill_a_ext(i, s_ref, val_ref, glu_ref, hval_ref, hglu_ref):
    ah = hval_ref[...].astype(F32) * _sig(hglu_ref[...].astype(F32))
    s_ref[0, 0:HALO, :] = jnp.where(i > 0, ah, 0.0)

    def fill(rb, carry):
        rows = _rows(rb)
        s_ref[0, pl.ds(pl.multiple_of(HALO + rb * RB, RB), RB), :] = _ld(val_ref, rows) * _sig(_ld(glu_ref, rows))
        return carry
    lax.fori_loop(0, TM // RB, fill, 0)


def _halo_prev(seg):
    hb = TM // HALO
    return pl.BlockSpec((HALO, D), lambda i: (jnp.maximum(i * hb - 1, 0), seg))


def _branch_a_fwd(p, cw, conv_b, lg, lb, wq):
    T = p.shape[0]

    def body(val_ref, glu_ref, z_ref, hval_ref, hglu_ref, cw_ref, cb_ref, lg_ref, lb_ref, w_ref,
             yc_ref, ya_ref, s_ref, yp_ref):
        i = pl.program_id(0)
        _fill_a_ext(i, s_ref, val_ref, glu_ref, hval_ref, hglu_ref)
        _shift_copies(s_ref)

        def conv(rb, carry):
            r0 = rb * RBC
            accs = [jnp.broadcast_to(cb_ref[...], (8, D))] * (RBC // 8)
            for k in range(KW):
                o = 2 + k
                w = cw_ref[k]
                accs = [acc + w * s_ref[o % 8, pl.ds(pl.multiple_of(r0 + 8 * (o // 8 + g), 8), 8), :]
                        for g, acc in enumerate(accs)]
            for g, acc in enumerate(accs):
                yc_ref[pl.ds(pl.multiple_of(r0 + 8 * g, 8), 8), :] = acc
            return carry
        lax.fori_loop(0, TM // RBC, conv, 0)

        def post(rb, carry):
            rows = _rows(rb)
            n, _ = _ln_stats(yc_ref[rows, :])
            l = n * lg_ref[...] + lb_ref[...]
            z = _ld(z_ref, rows)
            yp_ref[rows, :] = ((l * _sig(l)) * (z * _sig(z))).astype(BF16)
            return carry
        _blocks(TM // RB, post)
        ya_ref[...] = jnp.dot(yp_ref[...], w_ref[...], preferred_element_type=F32)

    tile = lambda seg: pl.BlockSpec((TM, D), lambda i: (i, seg))
    row = pl.BlockSpec((1, D), lambda i: (0, 0))
    return pl.pallas_call(
        body, name="branch_a_fwd", grid=(T // TM,),
        out_shape=(jax.ShapeDtypeStruct((T, D), F32), jax.ShapeDtypeStruct((T, D), F32)),
        in_specs=[tile(0), tile(1), tile(2), _halo_prev(0), _halo_prev(1),
                  pl.BlockSpec((KW, 8, D), lambda i: (0, 0, 0)), row, row, row,
                  pl.BlockSpec((None, D, D), lambda i: (W_CO, 0, 0))],
        out_specs=(pl.BlockSpec((TM, D), lambda i: (i, 0)), pl.BlockSpec((TM, D), lambda i: (i, 0))),
        scratch_shapes=[VMEM((8, EXT, D), F32), VMEM((TM, D), BF16)],
        compiler_params=_cp(("parallel",)),
    )(p, p, p, p, p, cw, conv_b, lg, lb, wq)


def _masked_ws(ws_ref, wt_ref):
    tri = lax.broadcasted_iota(jnp.int32, (CHUNK, CHUNK), 0) >= lax.broadcasted_iota(jnp.int32, (CHUNK, CHUNK), 1)
    for h in range(HEADS):
        wt_ref[h] = jnp.where(tri, ws_ref[h], 0.0).astype(BF16)


def _sgu_mix(wt_ref, vl_ref, bst_ref, s_ref):
    for ck in range(TM // CHUNK):
        r = slice(ck * CHUNK, (ck + 1) * CHUNK)
        for h in range(HEADS):
            cs = slice(h * HD, (h + 1) * HD)
            s_ref[r, cs] = jnp.dot(wt_ref[h], vl_ref[r, cs], preferred_element_type=F32) + bst_ref[:, h:h + 1]


def _branch_b_fwd(p, ws, bst, sg, sb, wq):
    T = p.shape[0]

    def body(pu_ref, pv_ref, pz_ref, ws_ref, bst_ref, sg_ref, sb_ref, w_ref, yb_ref,
             wt_ref, vl_ref, t_ref, s_ref, yp_ref):
        _masked_ws(ws_ref, wt_ref)

        def pre(rb, carry):
            rows = _rows(rb)
            vg, _ = _gelu_parts(_ld(pv_ref, rows))
            vn, _ = _ln_stats(vg)
            vl_ref[rows, :] = (vn * sg_ref[...] + sb_ref[...]).astype(BF16)
            u, _ = _gelu_parts(_ld(pu_ref, rows))
            z = _ld(pz_ref, rows)
            t_ref[rows, :] = u * (z * _sig(z))
            return carry
        _blocks(TM // RB, pre)
        _sgu_mix(wt_ref, vl_ref, bst_ref, s_ref)

        def post(rb, carry):
            rows = _rows(rb)
            yp_ref[rows, :] = (t_ref[rows, :] * s_ref[rows, :]).astype(BF16)
            return carry
        _blocks(TM // RB, post)
        yb_ref[...] = jnp.dot(yp_ref[...], w_ref[...], preferred_element_type=F32)

    tile = lambda seg: pl.BlockSpec((TM, D), lambda i: (i, seg))
    row = pl.BlockSpec((1, D), lambda i: (0, 0))
    return pl.pallas_call(
        body, name="branch_b_fwd", grid=(T // TM,),
        out_shape=jax.ShapeDtypeStruct((T, D), F32),
        in_specs=[tile(3), tile(4), tile(5), pl.BlockSpec((HEADS, CHUNK, CHUNK), lambda i: (0, 0, 0)),
                  pl.BlockSpec((CHUNK, HEADS), lambda i: (0, 0)), row, row,
                  pl.BlockSpec((None, D, D), lambda i: (W_SO, 0, 0))],
        out_specs=pl.BlockSpec((TM, D), lambda i: (i, 0)),
        scratch_shapes=[VMEM((HEADS, CHUNK, CHUNK), BF16), VMEM((TM, D), BF16), VMEM((TM, D), F32),
                        VMEM((TM, D), F32), VMEM((TM, D), BF16)],
        compiler_params=_cp(("parallel",)),
    )(p, p, p, ws, bst, sg, sb, wq)


def _merge_loss(p, ya, yb, x, tgt, mod, g_final, wq):
    T = x.shape[0]
    nt = T // TM

    def body(ga_ref, gb_ref, ya_ref, yb_ref, x_ref, t_ref, mod_ref, gf_ref, w_ref,
             dx2_ref, dya_ref, dyb_ref, dp_ref, gwo_ref, st_ref,
             mrg_s, out_s, dout_s, dm_s, gw_acc):
        i = pl.program_id(0)
        base = _ring_base(i, nt)
        cur = pl.ds(base, TM)

        @pl.when(i == 0)
        def _():
            gw_acc[...] = jnp.zeros_like(gw_acc)
            st_ref[...] = jnp.zeros_like(st_ref)

        def merge(rb, carry):
            rows = _rows(rb)
            mrg_s[_ring_rows(base, rb), :] = (_sig(_ld(ga_ref, rows)) * ya_ref[rows, :]
                                              + _sig(_ld(gb_ref, rows)) * yb_ref[rows, :]).astype(BF16)
            return carry
        lax.fori_loop(0, TM // RB, merge, 0)
        out_s[...] = jnp.dot(mrg_s[cur, :], w_ref[...], preferred_element_type=F32)

        def head(rb, carry):
            loss, gg, dg = carry
            rows = _rows(rb)
            gate = mod_ref[2:3, :]
            gf = gf_ref[...]
            out = out_s[rows, :]
            x2 = x_ref[rows, :] + gate * out
            r2 = lax.rsqrt(jnp.mean(x2 * x2, axis=-1, keepdims=True) + EPS)
            x2n = x2 * r2
            diff = x2n * gf - t_ref[rows, :]
            dy = diff * (1.0 / D)
            dx2n = dy * gf
            dx2 = r2 * (dx2n - x2n * jnp.mean(dx2n * x2n, axis=-1, keepdims=True))
            dx2_ref[rows, :] = dx2
            dout_s[_ring_rows(base, rb), :] = (dx2 * gate).astype(BF16)
            return loss + _fold8(diff * diff), gg + _fold8(dy * x2n), dg + _fold8(dx2 * out)
        zero = jnp.zeros((8, D), F32)
        loss, gg, dg = _blocks(TM // RB, head, (zero, zero, zero))
        st_ref[0] = st_ref[0] + loss * (0.5 / D)
        st_ref[1] = st_ref[1] + gg
        st_ref[2] = st_ref[2] + dg

        dm_s[...] = lax.dot_general(dout_s[cur, :], w_ref[...], NT, preferred_element_type=F32)

        @pl.when(_ring_full(i, nt))
        def _():
            gw_acc[...] += lax.dot_general(mrg_s[...], dout_s[...], TN, preferred_element_type=F32)

        def split(rb, carry):
            rows = _rows(rb)
            dm = dm_s[rows, :]
            sa = _sig(_ld(ga_ref, rows))
            sb = _sig(_ld(gb_ref, rows))
            dya_ref[rows, :] = (dm * sa).astype(BF16)
            dyb_ref[rows, :] = (dm * sb).astype(BF16)
            dp_ref[0, rows, :] = (dm * ya_ref[rows, :] * (sa * (1.0 - sa))).astype(BF16)
            dp_ref[1, rows, :] = (dm * yb_ref[rows, :] * (sb * (1.0 - sb))).astype(BF16)
            return carry
        lax.fori_loop(0, TM // RB, split, 0)

        @pl.when(i == nt - 1)
        def _():
            gwo_ref[...] = gw_acc[...].astype(BF16)

    tile = pl.BlockSpec((TM, D), lambda i: (i, 0))
    ring = VMEM((_ring_tiles(nt) * TM, D), BF16)
    return pl.pallas_call(
        body, name="merge_loss", grid=(nt,),
        out_shape=(jax.ShapeDtypeStruct((T, D), F32), jax.ShapeDtypeStruct((T, D), BF16), jax.ShapeDtypeStruct((T, D), BF16),
                   jax.ShapeDtypeStruct((NSEG, T, D), BF16), jax.ShapeDtypeStruct((D, D), BF16),
                   jax.ShapeDtypeStruct((3, 8, D), F32)),
        in_specs=[pl.BlockSpec((TM, D), lambda i: (i, 6)), pl.BlockSpec((TM, D), lambda i: (i, 7)), tile, tile, tile, tile,
                  pl.BlockSpec((3, D), lambda i: (0, 0)), pl.BlockSpec((1, D), lambda i: (0, 0)),
                  pl.BlockSpec((None, D, D), lambda i: (W_O, 0, 0))],
        out_specs=(tile, tile, tile, pl.BlockSpec((2, TM, D), lambda i: (3, i, 0)),
                   pl.BlockSpec((D, D), lambda i: (0, 0)), pl.BlockSpec((3, 8, D), lambda i: (0, 0, 0))),
        scratch_shapes=[ring, VMEM((TM, D), F32), ring, VMEM((TM, D), F32), VMEM((D, D), F32)],
        compiler_params=_cp(("arbitrary",)),
    )(p, p, ya, yb, x, tgt, mod, g_final, wq)


def _branch_b_bwd(dp, dyb, p, ws, bst, sg, sb, wq):
    T = p.shape[0]
    nt = T // TM

    def body(dp_in, dyb_ref, pu_ref, pv_ref, pz_ref, ws_ref, bst_ref, sg_ref, sb_ref, w_ref,
             dp_ref, gw_ref, gws_ref, gbs_ref, st_ref,
             wt_ref, d_s, vl_s, vn_s, gpv_s, rs_s, s_s, ds_s, ds32_s, yp_s, dy_s, dvl_s, gw_acc):
        del dp_in
        i = pl.program_id(0)
        base = _ring_base(i, nt)

        @pl.when(i == 0)
        def _():
            gw_acc[...] = jnp.zeros_like(gw_acc)
            gws_ref[...] = jnp.zeros_like(gws_ref)
            gbs_ref[...] = jnp.zeros_like(gbs_ref)
            st_ref[...] = jnp.zeros_like(st_ref)
            _masked_ws(ws_ref, wt_ref)

        d_s[...] = lax.dot_general(dyb_ref[...], w_ref[...], NT, preferred_element_type=F32)
        dy_s[pl.ds(base, TM), :] = dyb_ref[...]

        def pre(rb, carry):
            rows = _rows(rb)
            vg, gpv = _gelu_parts(_ld(pv_ref, rows))
            vn, rs = _ln_stats(vg)
            vl_s[rows, :] = (vn * sg_ref[...] + sb_ref[...]).astype(BF16)
            vn_s[rows, :] = vn
            gpv_s[rows, :] = gpv
            rs_s[rows, :] = rs
            return carry
        _blocks(TM // RB, pre)
        _sgu_mix(wt_ref, vl_s, bst_ref, s_s)

        def mid(rb, carry):
            rows = _rows(rb)
            u, gpu = _gelu_parts(_ld(pu_ref, rows))
            z = _ld(pz_ref, rows)
            sz = _sig(z)
            siluz = z * sz
            d = d_s[rows, :]
            s = s_s[rows, :]
            t = u * siluz
            yp_s[_ring_rows(base, rb), :] = (t * s).astype(BF16)
            dp_ref[0, rows, :] = (d * s * siluz * gpu).astype(BF16)
            ds = d * t
            ds32_s[rows, :] = ds
            ds_s[rows, :] = ds.astype(BF16)
            dp_ref[2, rows, :] = (d * u * s * (sz * (1.0 + z * (1.0 - sz)))).astype(BF16)
            return carry
        lax.fori_loop(0, TM // RB, mid, 0)

        for ck in range(TM // CHUNK):
            r = slice(ck * CHUNK, (ck + 1) * CHUNK)
            for h in range(HEADS):
                cs = slice(h * HD, (h + 1) * HD)
                dsh = ds_s[r, cs]
                dvl_s[r, cs] = lax.dot_general(wt_ref[h], dsh, TN, preferred_element_type=F32)
                gws_ref[h] += lax.dot_general(dsh, vl_s[r, cs], NT, preferred_element_type=F32)
                gbs_ref[h] += ds32_s[r, cs]

        def post(rb, carry):
            g_sg, g_sb = carry
            rows = _rows(rb)
            dvl = dvl_s[rows, :]
            vn = vn_s[rows, :]
            dvg = _ln_bwd(dvl * sg_ref[...], vn, rs_s[rows, :])
            dp_ref[1, rows, :] = (dvg * gpv_s[rows, :]).astype(BF16)
            return g_sg + _fold8(dvl * vn), g_sb + _fold8(dvl)
        zero = jnp.zeros((8, D), F32)
        g_sg, g_sb = _blocks(TM // RB, post, (zero, zero))
        st_ref[0] = st_ref[0] + g_sg
        st_ref[1] = st_ref[1] + g_sb

        @pl.when(_ring_full(i, nt))
        def _():
            gw_acc[...] += lax.dot_general(yp_s[...], dy_s[...], TN, preferred_element_type=F32)

        @pl.when(i == nt - 1)
        def _():
            gw_ref[...] = gw_acc[...].astype(BF16)

    tile = lambda seg: pl.BlockSpec((TM, D), lambda i: (i, seg))
    row = pl.BlockSpec((1, D), lambda i: (0, 0))
    hh = pl.BlockSpec((HEADS, CHUNK, CHUNK), lambda i: (0, 0, 0))
    ring = VMEM((_ring_tiles(nt) * TM, D), BF16)
    return pl.pallas_call(
        body, name="branch_b_bwd", grid=(nt,),
        out_shape=(jax.ShapeDtypeStruct((NSEG, T, D), BF16), jax.ShapeDtypeStruct((D, D), BF16),
                   jax.ShapeDtypeStruct((HEADS, CHUNK, CHUNK), F32), jax.ShapeDtypeStruct((HEADS, CHUNK, HD), F32),
                   jax.ShapeDtypeStruct((2, 8, D), F32)),
        in_specs=[pl.BlockSpec(memory_space=ANY), pl.BlockSpec((TM, D), lambda i: (i, 0)), tile(3), tile(4), tile(5),
                  hh, pl.BlockSpec((CHUNK, HEADS), lambda i: (0, 0)), row, row,
                  pl.BlockSpec((None, D, D), lambda i: (W_SO, 0, 0))],
        out_specs=(pl.BlockSpec((3, TM, D), lambda i: (1, i, 0)), pl.BlockSpec((D, D), lambda i: (0, 0)), hh, hh,
                   pl.BlockSpec((2, 8, D), lambda i: (0, 0, 0))),
        scratch_shapes=[VMEM((HEADS, CHUNK, CHUNK), BF16), VMEM((TM, D), F32), VMEM((TM, D), BF16), VMEM((TM, D), F32),
                        VMEM((TM, D), F32), VMEM((TM, 1), F32), VMEM((TM, D), F32), VMEM((TM, D), BF16), VMEM((TM, D), F32),
                        ring, ring, VMEM((TM, D), F32), VMEM((D, D), F32)],
        input_output_aliases={0: 0},
        compiler_params=_cp(("arbitrary",)),
    )(dp, dyb, p, p, p, ws, bst, sg, sb, wq)


def _branch_a_bwd1(dp, dya, yc, p, lg, lb, wq):
    T = p.shape[0]
    nt = T // TM

    def body(dp_in, dya_ref, yc_ref, z_ref, lg_ref, lb_ref, w_ref, dp_ref, dyc_ref, gw_ref, st_ref,
             d_s, yp_s, dy_s, gw_acc):
        del dp_in
        i = pl.program_id(0)
        base = _ring_base(i, nt)

        @pl.when(i == 0)
        def _():
            gw_acc[...] = jnp.zeros_like(gw_acc)
            st_ref[...] = jnp.zeros_like(st_ref)

        d_s[...] = lax.dot_general(dya_ref[...], w_ref[...], NT, preferred_element_type=F32)
        dy_s[pl.ds(base, TM), :] = dya_ref[...]

        def blk(rb, carry):
            g_lg, g_lb, g_cb = carry
            rows = _rows(rb)
            n, rs = _ln_stats(yc_ref[rows, :])
            l = n * lg_ref[...] + lb_ref[...]
            sgl = _sig(l)
            sl = l * sgl
            z = _ld(z_ref, rows)
            sz = _sig(z)
            siluz = z * sz
            d = d_s[rows, :]
            yp_s[_ring_rows(base, rb), :] = (sl * siluz).astype(BF16)
            dp_ref[rows, :] = (d * sl * (sz * (1.0 + z * (1.0 - sz)))).astype(BF16)
            dl = (d * siluz) * (sgl * (1.0 + l * (1.0 - sgl)))
            dyc = _ln_bwd(dl * lg_ref[...], n, rs)
            dyc_ref[rows, :] = dyc
            return g_lg + _fold8(dl * n), g_lb + _fold8(dl), g_cb + _fold8(dyc)
        zero = jnp.zeros((8, D), F32)
        g_lg, g_lb, g_cb = _blocks(TM // RB, blk, (zero, zero, zero))
        st_ref[0] = st_ref[0] + g_lg
        st_ref[1] = st_ref[1] + g_lb
        st_ref[2] = st_ref[2] + g_cb

        @pl.when(_ring_full(i, nt))
        def _():
            gw_acc[...] += lax.dot_general(yp_s[...], dy_s[...], TN, preferred_element_type=F32)

        @pl.when(i == nt - 1)
        def _():
            gw_ref[...] = gw_acc[...].astype(BF16)

    tile = pl.BlockSpec((TM, D), lambda i: (i, 0))
    row = pl.BlockSpec((1, D), lambda i: (0, 0))
    ring = VMEM((_ring_tiles(nt) * TM, D), BF16)
    return pl.pallas_call(
        body, name="branch_a_bwd1", grid=(nt,),
        out_shape=(jax.ShapeDtypeStruct((NSEG, T, D), BF16), jax.ShapeDtypeStruct((T, D), F32),
                   jax.ShapeDtypeStruct((D, D), BF16), jax.ShapeDtypeStruct((3, 8, D), F32)),
        in_specs=[pl.BlockSpec(memory_space=ANY), tile, tile, pl.BlockSpec((TM, D), lambda i: (i, 2)), row, row,
                  pl.BlockSpec((None, D, D), lambda i: (0, 0, 0))],
        out_specs=(pl.BlockSpec((None, TM, D), lambda i: (2, i, 0)), tile, pl.BlockSpec((D, D), lambda i: (0, 0)),
                   pl.BlockSpec((3, 8, D), lambda i: (0, 0, 0))),
        scratch_shapes=[VMEM((TM, D), F32), ring, ring, VMEM((D, D), F32)],
        input_output_aliases={0: 0},
        compiler_params=_cp(("arbitrary",)),
    )(dp, dya, yc, p, lg, lb, wq)


def _branch_a_bwd2(dp, dyc, p, cw, gw_co, gw_so, gw_o, early):
    T = p.shape[0]
    nt = T // TM
    hb = TM // HALO

    def body(dp_in, dyc_ref, hdyc_ref, val_ref, glu_ref, cw_ref, gco_ref, gso_ref, go_ref, e_ref,
             dp_ref, gcw_ref, rq_ref, eg_ref, a_s, sd_ref, da_s, x_send, x_recv, x_loc, ag_send, ag_recv, ag_loc):
        del dp_in
        i = pl.program_id(0)
        start_exchange, wait_exchange = _owner_exchange((gco_ref, gso_ref, go_ref), rq_ref, x_send, x_recv, x_loc)
        ag_start, ag_relay, ag_finish = _allgather_phases([e_ref], lambda a, slot: eg_ref.at[slot],
                                                          ag_send, ag_recv, ag_loc)

        @pl.when(i == 0)
        def _():
            gcw_ref[...] = jnp.zeros_like(gcw_ref)
            ag_start()
            start_exchange()

        @pl.when(i == nt // 2)
        def _():
            ag_relay()

        def fill(rb, carry):
            rows = _rows(rb)
            a_s[rows, :] = _ld(val_ref, rows) * _sig(_ld(glu_ref, rows))
            return carry
        lax.fori_loop(0, TM // RB, fill, 0)
        sd_ref[0, 0:TM, :] = dyc_ref[...]
        sd_ref[0, TM:EXT, :] = jnp.where(i < nt - 1, hdyc_ref[...], 0.0)
        _shift_copies(sd_ref)

        def conv_t(rb, carry):
            r0 = rb * RBC
            accs = [jnp.zeros((8, D), F32)] * (RBC // 8)
            for k in range(KW):
                o = KW - 1 - k
                w = cw_ref[k]
                accs = [acc + w * sd_ref[o % 8, pl.ds(pl.multiple_of(r0 + 8 * (o // 8 + g), 8), 8), :]
                        for g, acc in enumerate(accs)]
            for g, acc in enumerate(accs):
                da_s[pl.ds(pl.multiple_of(r0 + 8 * g, 8), 8), :] = acc
            return carry
        lax.fori_loop(0, TM // RBC, conv_t, 0)

        for k0 in range(0, KW, TAPG):
            taps = list(range(k0, min(k0 + TAPG, KW)))

            def tap_group(rb, accs, taps=taps):
                for u in range(2):
                    r0 = rb * 32 + 16 * u
                    a = a_s[pl.ds(pl.multiple_of(r0, 16), 16), :]
                    out = []
                    for k, acc in zip(taps, accs):
                        o = KW - 1 - k
                        d = sd_ref[o % 8, pl.ds(pl.multiple_of(r0 + 8 * (o // 8), 8), 16), :]
                        out.append(acc + _fold8(a * d))
                    accs = tuple(out)
                return accs
            sums = lax.fori_loop(0, TM // 32, tap_group, tuple(jnp.zeros((8, D), F32) for _ in taps))
            for k, s in zip(taps, sums):
                gcw_ref[k] = gcw_ref[k] + s

        def glu_b(rb, carry):
            rows = _rows(rb)
            da = da_s[rows, :]
            sg = _sig(_ld(glu_ref, rows))
            dp_ref[0, rows, :] = (da * sg).astype(BF16)
            dp_ref[1, rows, :] = (da * _ld(val_ref, rows) * (sg * (1.0 - sg))).astype(BF16)
            return carry
        lax.fori_loop(0, TM // RB, glu_b, 0)

        @pl.when(i == nt - 1)
        def _():
            wait_exchange()
            ag_finish()

    tile = lambda seg: pl.BlockSpec((TM, D), lambda i: (i, seg))
    hbm = pl.BlockSpec(memory_space=ANY)
    return pl.pallas_call(
        body, name="branch_a_bwd2", grid=(nt,),
        out_shape=(jax.ShapeDtypeStruct((NSEG, T, D), BF16), jax.ShapeDtypeStruct((32, 8, D), F32),
                   jax.ShapeDtypeStruct((NDEV, 3, 128, D), BF16), jax.ShapeDtypeStruct((NDEV, NEARLY, D), F32)),
        in_specs=[hbm, pl.BlockSpec((TM, D), lambda i: (i, 0)),
                  pl.BlockSpec((HALO, D), lambda i: (jnp.minimum((i + 1) * hb, nt * hb - 1), 0)),
                  tile(0), tile(1), pl.BlockSpec((KW, 8, D), lambda i: (0, 0, 0)),
                  hbm, hbm, hbm, hbm],
        out_specs=(pl.BlockSpec((2, TM, D), lambda i: (0, i, 0)), pl.BlockSpec((32, 8, D), lambda i: (0, 0, 0)), hbm, hbm),
        scratch_shapes=[VMEM((TM, D), F32), VMEM((8, EXT, D), F32), VMEM((TM, D), F32),
                        pltpu.SemaphoreType.DMA((3, 7)), pltpu.SemaphoreType.DMA((3, 7)), pltpu.SemaphoreType.DMA((3,)),
                        pltpu.SemaphoreType.DMA((1, 7)), pltpu.SemaphoreType.DMA((1, 7)), pltpu.SemaphoreType.DMA((1,))],
        input_output_aliases={0: 0},
        compiler_params=_cp(("arbitrary",)),
    )(dp, dyc, dyc, p, p, cw, gw_co, gw_so, gw_o, early)


def _in_bwd_dx(dp, wing, x, dx2, mod, g_pre):
    T = x.shape[0]
    nt = T // TM

    def body(dp_ref, w_ref, x_ref, dx2_ref, mod_ref, g_ref, gx_ref, st_ref, dh_s):
        i = pl.program_id(0)

        @pl.when(i == 0)
        def _():
            st_ref[...] = jnp.zeros_like(st_ref)

        dh_s[...] = lax.dot_general(dp_ref[0], w_ref[0], NT, preferred_element_type=F32)
        for j in range(1, NSEG):
            dh_s[...] += lax.dot_general(dp_ref[j], w_ref[j], NT, preferred_element_type=F32)

        def blk(rb, carry):
            d_sh, d_sc, g_g = carry
            rows = _rows(rb)
            xv = x_ref[rows, :]
            r = lax.rsqrt(jnp.mean(xv * xv, axis=-1, keepdims=True) + EPS)
            xn = xv * r
            g = g_ref[...]
            hpre = xn * g
            dh = dh_s[rows, :]
            dhp = dh * (1.0 + mod_ref[1:2, :])
            dxn = dhp * g
            gx_ref[rows, :] = dx2_ref[rows, :] + r * (dxn - xn * jnp.mean(dxn * xn, axis=-1, keepdims=True))
            return d_sh + _fold8(dh), d_sc + _fold8(dh * hpre), g_g + _fold8(dhp * xn)
        zero = jnp.zeros((8, D), F32)
        d_sh, d_sc, g_g = _blocks(TM // RB, blk, (zero, zero, zero))
        st_ref[0] = st_ref[0] + d_sh
        st_ref[1] = st_ref[1] + d_sc
        st_ref[2] = st_ref[2] + g_g

    tile = pl.BlockSpec((TM, D), lambda i: (i, 0))
    return pl.pallas_call(
        body, name="in_bwd_dx", grid=(nt,),
        out_shape=(jax.ShapeDtypeStruct((T, D), F32), jax.ShapeDtypeStruct((3, 8, D), F32)),
        in_specs=[pl.BlockSpec((NSEG, TM, D), lambda i: (0, i, 0)),
                  pl.BlockSpec((NSEG, D, D), lambda i: (0, 0, 0), pipeline_mode=pl.Buffered(1)),
                  tile, tile, pl.BlockSpec((3, D), lambda i: (0, 0)), pl.BlockSpec((1, D), lambda i: (0, 0))],
        out_specs=(tile, pl.BlockSpec((3, 8, D), lambda i: (0, 0, 0))),
        scratch_shapes=[VMEM((TM, D), F32)],
        compiler_params=_cp(("arbitrary",), vmem_mb=56),
    )(dp, wing, x, dx2, mod, g_pre)


RS_ORDER = (5, 4, 3, 2, 7, 6, 1, 0)
RS_HALF = D // 2


def _in_bwd_dw(ht, dp, order, smalls):
    T = ht.shape[1]
    tmw = min(4 * TMB, T)
    nb = T // tmw
    nu = 2 * NSEG

    def body(order_ref, ht_ref, dp_ref, sm_ref, g_ref, smg_ref, acc, sbuf, rsib, rici, send_sems, recv_sems,
             ag_send, ag_recv, ag_loc):
        j = pl.program_id(0)
        i = pl.program_id(1)
        x, y, c = _me()
        sibling = (x, y, 1 - c)
        chips = [(1 - x, y), (x, 1 - y), (1 - x, 1 - y)]
        ag_start, ag_relay, ag_finish = _allgather_phases([sm_ref], lambda a, slot: smg_ref.at[slot],
                                                          ag_send, ag_recv, ag_loc)

        def to_sibling(s):
            return pltpu.make_async_remote_copy(src_ref=sbuf.at[s], dst_ref=rsib.at[s], send_sem=send_sems.at[s],
                                                recv_sem=recv_sems.at[s], device_id=sibling, device_id_type=MESH)

        def to_owner(t):
            return pltpu.make_async_remote_copy(src_ref=sbuf.at[8 + t], dst_ref=rici.at[t], send_sem=send_sems.at[8 + t],
                                                recv_sem=recv_sems.at[8 + t], device_id=(*chips[t // 2], c),
                                                device_id_type=MESH)

        @pl.when((j == 0) & (i == 0))
        def _():
            ag_start()

        @pl.when((j == 4) & (i == 0))
        def _():
            ag_relay()

        @pl.when(i == 0)
        def _():
            acc[...] = jnp.zeros_like(acc)
        acc[...] += jnp.dot(ht_ref[...], dp_ref[...], preferred_element_type=F32)

        for jj in range(nu):
            chip, half, mine = jj // 4, (jj % 4) // 2, jj % 2
            s = 2 * chip + half

            @pl.when((j == jj) & (i == nb - 1))
            def _(chip=chip, half=half, mine=mine, s=s):
                if not mine:
                    sbuf[s] = acc[...].astype(BF16)
                    to_sibling(s).start()
                elif chip < 3:
                    to_sibling(s).wait_recv()
                    sbuf[8 + s] = (acc[...] + rsib[s].astype(F32)).astype(BF16)
                    to_owner(s).start()
                else:
                    to_sibling(s).wait_recv()
                    g = acc[...] + rsib[s].astype(F32)
                    for q in range(3):
                        to_owner(2 * q + half).wait_recv()
                        g = g + rici[2 * q + half].astype(F32)
                    g_ref[:, half * RS_HALF:(half + 1) * RS_HALF] = g

        @pl.when((j == nu - 1) & (i == nb - 1))
        def _():
            for s in range(8):
                to_sibling(s).wait_send()
            for t in range(6):
                to_owner(t).wait_send()
            ag_finish()

    hbm = pl.BlockSpec(memory_space=ANY)
    return pl.pallas_call(
        body, name="in_bwd_dw",
        out_shape=(jax.ShapeDtypeStruct((D, D), F32), jax.ShapeDtypeStruct((NDEV, NLATE, D), F32)),
        grid_spec=pltpu.PrefetchScalarGridSpec(
            num_scalar_prefetch=1, grid=(nu, nb),
            in_specs=[pl.BlockSpec((D, tmw), lambda j, i, o: (0, i)),
                      pl.BlockSpec((None, tmw, RS_HALF), lambda j, i, o: (o[2 * (j // 4) + j % 2], i, (j % 4) // 2)),
                      hbm],
            out_specs=(pl.BlockSpec((D, D), lambda j, i, o: (0, 0)), hbm),
            scratch_shapes=[VMEM((D, RS_HALF), F32), VMEM((14, D, RS_HALF), BF16), VMEM((8, D, RS_HALF), BF16),
                            VMEM((6, D, RS_HALF), BF16), pltpu.SemaphoreType.DMA((14,)), pltpu.SemaphoreType.DMA((14,)),
                            pltpu.SemaphoreType.DMA((1, 7)), pltpu.SemaphoreType.DMA((1, 7)), pltpu.SemaphoreType.DMA((1,))]),
        compiler_params=_cp(("arbitrary", "arbitrary"), vmem_mb=56),
    )(order, ht, dp, smalls)


def _pack_early(st_a, st_b, st_m, gws, gbs):
    def body(sa_ref, sb_ref, sm_ref, gws_ref, gbs_ref, out_ref):
        out_ref[...] = jnp.zeros_like(out_ref)
        fold = lambda v: jnp.sum(v, axis=0, keepdims=True)
        out_ref[0:1, :] = fold(sm_ref[2])
        out_ref[1:2, :] = fold(sa_ref[2])
        out_ref[2:3, :] = fold(sa_ref[0])
        out_ref[3:4, :] = fold(sa_ref[1])
        out_ref[4:5, :] = fold(sb_ref[0])
        out_ref[5:6, :] = fold(sb_ref[1])
        out_ref[7:8, :] = fold(sm_ref[1])
        out_ref[8:9, :] = fold(sm_ref[0])
        tri = lax.broadcasted_iota(jnp.int32, (CHUNK, CHUNK), 0) >= lax.broadcasted_iota(jnp.int32, (CHUNK, CHUNK), 1)
        for h in range(HEADS):
            out_ref[16:16 + CHUNK, h * CHUNK:(h + 1) * CHUNK] = jnp.where(tri, gws_ref[h], 0.0)
            out_ref[6:7, h * CHUNK:(h + 1) * CHUNK] = fold(gbs_ref[h].T)

    vm = pl.BlockSpec(memory_space=VMEM)
    return pl.pallas_call(
        body, name="pack_early", out_shape=jax.ShapeDtypeStruct((NEARLY, D), F32),
        in_specs=[vm] * 5, out_specs=vm, compiler_params=_cp(vmem_mb=32),
    )(st_a, st_b, st_m, gws, gbs)


def _pack_late(st_x, gcw):
    def body(sx_ref, gcw_ref, out_ref):
        out_ref[...] = jnp.zeros_like(out_ref)
        fold = lambda v: jnp.sum(v, axis=0, keepdims=True)
        for r in range(3):
            out_ref[r:r + 1, :] = fold(sx_ref[r])
        for k in range(KW):
            out_ref[8 + k:9 + k, :] = fold(gcw_ref[k])

    vm = pl.BlockSpec(memory_space=VMEM)
    return pl.pallas_call(
        body, name="pack_late", out_shape=jax.ShapeDtypeStruct((NLATE, D), F32),
        in_specs=[vm] * 2, out_specs=vm, compiler_params=_cp(vmem_mb=32),
    )(st_x, gcw)


def _pack_params(name, b_ada, g_pre, conv_b, lg, lb, sg, sb, b_sgu, g_final, w_sgu_t):
    def body(ba_ref, *refs):
        rows, ws_ref, out_ref = refs[:8], refs[8], refs[9]
        out_ref[...] = jnp.zeros_like(out_ref)
        for r in range(3):
            out_ref[r:r + 1, :] = ba_ref[:, r * D:(r + 1) * D]
        for r, ref in enumerate(rows):
            out_ref[3 + r:4 + r, :] = ref[...]
        out_ref[ROW_WS:ROW_WS + CHUNK, :] = ws_ref[...]

    vm = pl.BlockSpec(memory_space=VMEM)
    return pl.pallas_call(
        body, name=name, out_shape=jax.ShapeDtypeStruct((NSMALL, D), F32),
        in_specs=[vm] * 10, out_specs=vm, compiler_params=_cp(vmem_mb=32),
    )(b_ada, g_pre, conv_b, lg, lb, sg, sb, b_sgu, g_final, w_sgu_t)


def _adam(w, g, m, v):
    m2 = ADAM_B1 * m + (1.0 - ADAM_B1) * g
    v2 = ADAM_B2 * v + (1.0 - ADAM_B2) * (g * g)
    m_hat = m2 / (1.0 - ADAM_B1 ** ADAM_STEP)
    v_hat = v2 / (1.0 - ADAM_B2 ** ADAM_STEP)
    delta = -ADAM_LR * (m_hat / (jnp.sqrt(v_hat) + ADAM_EPS) + ADAM_WD * w)
    return delta, m2, v2


def _small_finish(eg, lg, w, m, v):
    def body(eg_ref, lg_ref, w_ref, m_ref, v_ref, loss_ref, g_ref, d_ref, m2_ref, v2_ref):
        e = eg_ref[0]
        l = lg_ref[0]
        for s in range(1, NDEV):
            e = e + eg_ref[s]
            l = l + lg_ref[s]
        g_ref[...] = jnp.zeros_like(g_ref)
        g_ref[0:2, :] = l[0:2, :]
        g_ref[2:3, :] = e[0:1, :]
        g_ref[3:4, :] = l[2:3, :]
        g_ref[4:12, :] = e[1:9, :]
        g_ref[ROW_CW:ROW_CW + KW, :] = l[8:8 + KW, :]
        g_ref[ROW_WS:ROW_WS + CHUNK, :] = e[16:16 + CHUNK, :]
        loss_ref[...] = jnp.sum(e[8:9, :], axis=1, keepdims=True)
        d_ref[...], m2_ref[...], v2_ref[...] = _adam(w_ref[...], g_ref[...], m_ref[...], v_ref[...])

    vm = pl.BlockSpec(memory_space=VMEM)
    sd = jax.ShapeDtypeStruct((NSMALL, D), F32)
    return pl.pallas_call(
        body, name="small_finish", out_shape=(jax.ShapeDtypeStruct((1, 1), F32), sd, sd, sd, sd),
        in_specs=[vm] * 5, out_specs=(vm,) * 5, compiler_params=_cp(vmem_mb=32),
    )(eg, lg, w, m, v)


def _ada_grad_adam(ct, dm, w, m, v):
    def body(ct_ref, dm_ref, w_ref, m_ref, v_ref, g_ref, d_ref, m2_ref, v2_ref):
        g = ct_ref[:, 0:1] * dm_ref[0:1, :]
        for b in range(1, NDEV):
            g = g + ct_ref[:, b:b + 1] * dm_ref[b:b + 1, :]
        g_ref[...] = g
        d_ref[...], m2_ref[...], v2_ref[...] = _adam(w_ref[...], g, m_ref[...], v_ref[...])

    vm = pl.BlockSpec(memory_space=VMEM)
    sd = jax.ShapeDtypeStruct(w.shape, F32)
    return pl.pallas_call(
        body, name="ada_grad_adam", out_shape=(sd, sd, sd, sd),
        in_specs=[vm] * 5, out_specs=(vm,) * 4, compiler_params=_cp(vmem_mb=32),
    )(ct, dm, w, m, v)


def _adam_f32(name, g, w, m, v, rows=None):
    R, C = w.shape
    rows = R if rows is None else rows

    def body(g_ref, w_ref, m_ref, v_ref, d_ref, m2_ref, v2_ref):
        d_ref[...], m2_ref[...], v2_ref[...] = _adam(w_ref[...], g_ref[...], m_ref[...], v_ref[...])

    tile = pl.BlockSpec((rows, C), lambda i: (i, 0))
    sd = jax.ShapeDtypeStruct(w.shape, F32)
    return pl.pallas_call(
        body, name=name, grid=(R // rows,), out_shape=(sd, sd, sd), in_specs=[tile] * 4, out_specs=(tile,) * 3,
        compiler_params=_cp(("parallel",), vmem_mb=32),
    )(g, w, m, v)


def _adam_reduce(name, recv, w, m, v, rows, recv_spec):
    R, C = w.shape

    def body(r_ref, w_ref, m_ref, v_ref, g_ref, d_ref, m2_ref, v2_ref):
        g = r_ref[0].astype(F32)
        for s in range(1, NDEV):
            g = g + r_ref[s].astype(F32)
        g_ref[...] = g
        d_ref[...], m2_ref[...], v2_ref[...] = _adam(w_ref[...], g, m_ref[...], v_ref[...])

    tile = pl.BlockSpec((rows, C), lambda i: (i, 0))
    sd = jax.ShapeDtypeStruct((R, C), F32)
    return pl.pallas_call(
        body, name=name, grid=(R // rows,), out_shape=(sd, sd, sd, sd),
        in_specs=[recv_spec, tile, tile, tile], out_specs=(tile,) * 4,
        compiler_params=_cp(("parallel",)),
    )(recv, w, m, v)


def kernel(x, c, w_ada, b_ada, g_pre, w_in, conv_w, conv_b, conv_ln_g, conv_ln_b, w_conv_out, sgu_ln_g, sgu_ln_b, w_sgu, b_sgu, w_sgu_out, w_o, g_final, loss_target, m_w_ada, m_b_ada, m_g_pre, m_w_in, m_conv_w, m_conv_b, m_conv_ln_g, m_conv_ln_b, m_w_conv_out, m_sgu_ln_g, m_sgu_ln_b, m_w_sgu, m_b_sgu, m_w_sgu_out, m_w_o, m_g_final, v_w_ada, v_b_ada, v_g_pre, v_w_in, v_conv_w, v_conv_b, v_conv_ln_g, v_conv_ln_b, v_w_conv_out, v_sgu_ln_g, v_sgu_ln_b, v_w_sgu, v_b_sgu, v_w_sgu_out, v_w_o, v_g_final):
    T = x.shape[1]
    assert T % TMB == 0 and x.shape[2] == D
    xs = x[0]
    tgt = loss_target[0]
    my = 4 * lax.axis_index("x") + 2 * lax.axis_index("y") + lax.axis_index("c")

    gf = g_final.reshape(1, D)
    bst = b_sgu[0].T

    ag_order = jnp.stack([_slot(*b) for b in _ag_blocks(lax.axis_index("x"), lax.axis_index("y"), lax.axis_index("c"))])
    p, wing, wqg, cwg, ht, mod, cg = _in_proj(xs, c, w_ada[0], b_ada, g_pre, w_in[0], ag_order.astype(jnp.int32),
                                              w_conv_out[0], w_sgu_out[0], w_o[0], conv_w[0])
    wq = wqg.reshape(3, D, D)
    cw = jnp.broadcast_to(jnp.transpose(cwg, (1, 0, 2)).reshape(KW, 1, D), (KW, 8, D))
    yc, ya = _branch_a_fwd(p, cw, conv_b, conv_ln_g, conv_ln_b, wq)
    yb = _branch_b_fwd(p, w_sgu[0], bst, sgu_ln_g, sgu_ln_b, wq)
    dx2, dya, dyb, dp, gw_o, st_m = _merge_loss(p, ya, yb, xs, tgt, mod, gf, wq)

    dp, gw_so, gws, gbs, st_b = _branch_b_bwd(dp, dyb, p, w_sgu[0], bst, sgu_ln_g, sgu_ln_b, wq)
    dp, dyc, gw_co, st_a = _branch_a_bwd1(dp, dya, yc, p, conv_ln_g, conv_ln_b, wq)
    early = _pack_early(st_a, st_b, st_m, gws, gbs)
    dp, gcw, rq, eg = _branch_a_bwd2(dp, dyc, p, cw, gw_co, gw_so, gw_o, early)
    grad_x, st_x = _in_bwd_dx(dp, wing, xs, dx2, mod, g_pre)
    late = _pack_late(st_x, gcw)
    g_in, lg = _in_bwd_dw(ht, dp, my ^ jnp.array(RS_ORDER, jnp.int32), late)

    def pack(name, b_ada_, g_pre_, conv_b_, lg_, lb_, sg_, sb_, b_sgu_, gfin_, w_sgu_):
        return _pack_params(name, b_ada_, g_pre_, conv_b_, lg_, lb_, sg_, sb_, b_sgu_.reshape(1, D), gfin_.reshape(1, D),
                            jnp.transpose(w_sgu_[0], (1, 0, 2)).reshape(CHUNK, D))

    pw = pack("pack_w", b_ada, g_pre, conv_b, conv_ln_g, conv_ln_b, sgu_ln_g, sgu_ln_b, b_sgu, g_final, w_sgu)
    pm = pack("pack_m", m_b_ada, m_g_pre, m_conv_b, m_conv_ln_g, m_conv_ln_b, m_sgu_ln_g, m_sgu_ln_b, m_b_sgu, m_g_final,
              m_w_sgu)
    pv = pack("pack_v", v_b_ada, v_g_pre, v_conv_b, v_conv_ln_g, v_conv_ln_b, v_sgu_ln_g, v_sgu_ln_b, v_b_sgu, v_g_final,
              v_w_sgu)
    loss11, sg_, sd_, sm_, sv_ = _small_finish(eg, lg, pw, pm, pv)

    def unpack(a):
        return dict(
            b_ada=a[0:3].reshape(1, 3 * D), g_pre=a[3:4], conv_b=a[4:5], conv_ln_g=a[5:6], conv_ln_b=a[6:7],
            sgu_ln_g=a[7:8], sgu_ln_b=a[8:9], b_sgu=a[9:10].reshape(1, HEADS, CHUNK), g_final=a[10],
            w_sgu=jnp.transpose(a[ROW_WS:ROW_WS + CHUNK].reshape(CHUNK, HEADS, CHUNK), (1, 0, 2))[None])
    small = [unpack(a) for a in (sg_, sd_, sm_, sv_)]

    g_cw = lax.dynamic_slice_in_dim(sg_[ROW_CW:ROW_CW + KW], my * 128, 128, axis=1)
    d_cw, m_cw, v_cw = _adam_f32("adam_conv_w", g_cw, conv_w[0], m_conv_w[0], v_conv_w[0])

    wcols = w_ada.shape[2]
    dm_all = jnp.concatenate([lg[:, 0], lg[:, 1], eg[:, 0]], axis=1)
    dm_mine = lax.dynamic_slice_in_dim(dm_all, my * wcols, wcols, axis=1)
    g_ada, d_ada, m_ada, v_ada = _ada_grad_adam(cg.T, dm_mine, w_ada[0], m_w_ada[0], v_w_ada[0])

    d_in, m_in, v_in = _adam_f32("adam_w_in", g_in, w_in[0], m_w_in[0], v_w_in[0], rows=256)
    big = {}
    for j, (nm, w_, m_, v_) in enumerate((("w_conv_out", w_conv_out, m_w_conv_out, v_w_conv_out),
                                          ("w_sgu_out", w_sgu_out, m_w_sgu_out, v_w_sgu_out),
                                          ("w_o", w_o, m_w_o, v_w_o))):
        big[nm] = _adam_reduce("adam_" + nm, rq, w_[0], m_[0], v_[0], 128,
                               pl.BlockSpec((NDEV, None, 128, D), lambda i, j=j: (0, j, 0, 0)))

    per = {
        "w_ada": tuple(a[None] for a in (g_ada, d_ada, m_ada, v_ada)),
        "w_in": tuple(a[None] for a in (g_in, d_in, m_in, v_in)),
        "conv_w": tuple(a[None] for a in (g_cw, d_cw, m_cw, v_cw)),
    }
    for nm in ("w_conv_out", "w_sgu_out", "w_o"):
        per[nm] = tuple(a[None] for a in big[nm])
    for nm in ("b_ada", "g_pre", "conv_b", "conv_ln_g", "conv_ln_b", "sgu_ln_g", "sgu_ln_b", "w_sgu", "b_sgu", "g_final"):
        per[nm] = tuple(s[nm] for s in small)

    order = ["w_ada", "b_ada", "g_pre", "w_in", "conv_w", "conv_b", "conv_ln_g", "conv_ln_b", "w_conv_out",
             "sgu_ln_g", "sgu_ln_b", "w_sgu", "b_sgu", "w_sgu_out", "w_o", "g_final"]
    outs = [loss11.reshape(()), grad_x[None]]
    for part in range(4):
        outs += [per[nm][part] for nm in order]
    return tuple(outs)
```

```python
import jax
import jax.numpy as jnp
from jax import lax
from jax.experimental import pallas as pl
from jax.experimental.pallas import tpu as pltpu

F32 = jnp.float32
BF16 = jnp.bfloat16
MESH = pl.DeviceIdType.MESH
VMEM = pltpu.VMEM
ANY = pl.ANY

D = 1024
NDEV = 8
NSEG = 8
HEADS = 8
HD = 128
CHUNK = 128
KW = 31
HALO = 32
EPS = 1e-6
TM = 256
TMB = 512
RB = 32
UNROLL = 4
RBC = 32
TAPG = 4
EXT = TM + HALO
NSMALL = 176
NEARLY = 144
NLATE = 40
ROW_CW = 16
ROW_WS = 48
W_CO, W_SO, W_O = 0, 1, 2

ADAM_LR = 0.001
ADAM_B1 = 0.9
ADAM_B2 = 0.999
ADAM_EPS = 1e-08
ADAM_WD = 0.01
ADAM_STEP = 10

INV_SQRT2 = 0.7071067811865476
INV_SQRT_2PI = 0.3989422804014327

NT = (((1,), (1,)), ((), ()))
TN = (((0,), (0,)), ((), ()))


def _cp(sem=None, vmem_mb=48):
    return pltpu.CompilerParams(dimension_semantics=sem, vmem_limit_bytes=vmem_mb * 1024 * 1024)


def _me():
    return lax.axis_index("x"), lax.axis_index("y"), lax.axis_index("c")


def _slot(px, py, pc):
    return 4 * px + 2 * py + pc


def _xor_peer(k):
    x, y, c = _me()
    return (1 - x if k & 4 else x, 1 - y if k & 2 else y, 1 - c if k & 1 else c)


def _allgather_phases(srcs, dst_at, send_sems, recv_sems, loc_sems):
    x, y, c = _me()
    me = (x, y, c)
    sibling = (x, y, 1 - c)
    chips = [(1 - x, y), (x, 1 - y), (1 - x, 1 - y)]
    na = len(srcs)

    def copy(a, k, block, to, src=None):
        d = dst_at(a, _slot(*block))
        return pltpu.make_async_remote_copy(
            src_ref=d if src is None else src, dst_ref=d,
            send_sem=send_sems.at[a, k], recv_sem=recv_sems.at[a, k],
            device_id=to, device_id_type=MESH)

    local = [pltpu.make_async_copy(srcs[a], dst_at(a, _slot(*me)), loc_sems.at[a]) for a in range(na)]
    first = []
    for a in range(na):
        first.append(copy(a, 0, me, sibling, src=srcs[a]))
        for j, chip in enumerate(chips):
            first.append(copy(a, 1 + j, me, (*chip, c), src=srcs[a]))
    passed = [copy(a, 4 + j, (*chip, c), sibling) for j, chip in enumerate(chips) for a in range(na)]

    def start():
        for cp in local + first:
            cp.start()

    def relay():
        for j, chip in enumerate(chips):
            for a in range(na):
                copy(a, 1 + j, (*chip, c), me).wait_recv()
                passed[j * na + a].start()

    def finish():
        for a in range(na):
            copy(a, 0, sibling, me).wait_recv()
        for j, chip in enumerate(chips):
            for a in range(na):
                copy(a, 4 + j, (*chip, 1 - c), me).wait_recv()
        for cp in first + passed:
            cp.wait_send()
        for cp in local:
            cp.wait()

    return start, relay, finish


def _allgather(srcs, dst_at, send_sems, recv_sems, loc_sems):
    start, relay, finish = _allgather_phases(srcs, dst_at, send_sems, recv_sems, loc_sems)
    start()
    relay()
    finish()


def _owner_exchange(gq, rq_ref, x_send, x_recv, x_loc):
    mx, my_, mc = _me()
    me = _slot(mx, my_, mc)

    def rows_of(a, slot):
        return gq[a].at[pl.ds(pl.multiple_of(slot * 128, 128), 128), :]

    def exchange(k, a, recv):
        px, py, pc = _xor_peer(k)
        peer = _slot(px, py, pc)
        return pltpu.make_async_remote_copy(
            src_ref=rows_of(a, me if recv else peer), dst_ref=rq_ref.at[peer if recv else me, a],
            send_sem=x_send.at[a, k - 1], recv_sem=x_recv.at[a, k - 1],
            device_id=(px, py, pc), device_id_type=MESH)

    local = [pltpu.make_async_copy(rows_of(a, me), rq_ref.at[me, a], x_loc.at[a]) for a in range(len(gq))]

    def start():
        for cp in local:
            cp.start()
        for k in range(1, NDEV):
            for a in range(len(gq)):
                exchange(k, a, False).start()

    def wait():
        for k in range(1, NDEV):
            for a in range(len(gq)):
                exchange(k, a, True).wait_recv()
        for k in range(1, NDEV):
            for a in range(len(gq)):
                exchange(k, a, False).wait_send()
        for cp in local:
            cp.wait()

    return start, wait


def _sig(v):
    return jax.nn.sigmoid(v)


def _gelu_parts(v):
    cdf = 0.5 * (1.0 + lax.erf(v * INV_SQRT2))
    pdf = jnp.exp(-0.5 * v * v) * INV_SQRT_2PI
    return v * cdf, cdf + v * pdf


def _ln_stats(v):
    mu = jnp.mean(v, axis=-1, keepdims=True)
    xc = v - mu
    var = jnp.mean(xc * xc, axis=-1, keepdims=True)
    rs = lax.rsqrt(var + EPS)
    return xc * rs, rs


def _ln_bwd(dn, n, rs):
    return rs * (dn - jnp.mean(dn, axis=-1, keepdims=True) - n * jnp.mean(dn * n, axis=-1, keepdims=True))


def _fold8(v):
    acc = v[0:8]
    for r in range(8, v.shape[0], 8):
        acc = acc + v[r:r + 8]
    return acc


def _rows(rb, n=RB):
    return pl.ds(pl.multiple_of(rb * n, n), n)


def _ring_tiles(nt):
    return 4 if nt % 4 == 0 else (2 if nt % 2 == 0 else 1)


def _ring_base(i, nt):
    return pl.multiple_of((i % _ring_tiles(nt)) * TM, TM)


def _ring_rows(base, rb, n=RB):
    return pl.ds(pl.multiple_of(base + rb * n, n), n)


def _ring_full(i, nt):
    g = _ring_tiles(nt)
    return i % g == g - 1


def _ld(ref, rows):
    return ref[rows, :].astype(F32)


def _blocks(n, body, init=0, unroll=UNROLL):
    def trip(t, carry):
        for u in range(unroll):
            carry = body(t * unroll + u, carry)
        return carry
    return lax.fori_loop(0, n // unroll, trip, init)


def _shift_copies(s_ref):
    n = EXT - 8
    for m in range(1, 8):
        for c0 in range(0, n, 56):
            s_ref[m, c0:c0 + 56, :] = s_ref[0, c0 + m:c0 + m + 56, :]


def _ada_exchange(c_ref, w_ref, b_ref, mod_ref, call_ref, cdst, msrc, mdst, c_send, c_recv, m_send, m_recv):
    x, y, c = _me()
    me = _slot(x, y, c)

    def all_to_all(src_of, dst, ss, rs):
        def cp(k, recv):
            px, py, pc = _xor_peer(k)
            peer = _slot(px, py, pc)
            return pltpu.make_async_remote_copy(
                src_ref=src_of(me if recv else peer), dst_ref=dst.at[peer if recv else me],
                send_sem=ss.at[k - 1], recv_sem=rs.at[k - 1], device_id=(px, py, pc), device_id_type=MESH)
        for k in range(1, NDEV):
            cp(k, False).start()
        for k in range(1, NDEV):
            cp(k, True).wait_recv()
        for k in range(1, NDEV):
            cp(k, False).wait_send()

    cdst[me] = c_ref[...]
    all_to_all(lambda s: c_ref, cdst, c_send, c_recv)
    for b in range(NDEV):
        call_ref[b:b + 1, :] = cdst[b]
    m = jnp.dot(call_ref[...], w_ref[...], preferred_element_type=F32, precision=lax.Precision.HIGHEST)
    for b in range(NDEV):
        msrc[b] = m[b:b + 1, :]
    mdst[me] = msrc[me]
    all_to_all(lambda s: msrc.at[s], mdst, m_send, m_recv)
    full = jnp.concatenate([mdst[k] for k in range(NDEV)], axis=1) + b_ref[...]
    for r in range(3):
        mod_ref[r:r + 1, :] = full[:, r * D:(r + 1) * D]


def _ada_fwd(c, w_ada, b_ada):
    wcols = w_ada.shape[1]

    def body(*refs):
        _ada_exchange(*refs)

    vm = pl.BlockSpec(memory_space=VMEM)
    return pl.pallas_call(
        body, name="ada_fwd",
        out_shape=(jax.ShapeDtypeStruct((3, D), F32), jax.ShapeDtypeStruct((NDEV, D), F32)),
        in_specs=[vm, vm, vm], out_specs=(vm, vm),
        scratch_shapes=[VMEM((NDEV, 1, D), F32), VMEM((NDEV, 1, wcols), F32), VMEM((NDEV, 1, wcols), F32),
                        pltpu.SemaphoreType.DMA((7,)), pltpu.SemaphoreType.DMA((7,)),
                        pltpu.SemaphoreType.DMA((7,)), pltpu.SemaphoreType.DMA((7,))],
        compiler_params=_cp(vmem_mb=32),
    )(c, w_ada, b_ada)


def _prep_h(x, mod, g_pre):
    T = x.shape[0]

    def body(x_ref, mod_ref, g_ref, h_ref, ht_ref):
        def blk(rb, carry):
            rows = _rows(rb)
            xv = x_ref[rows, :]
            r = lax.rsqrt(jnp.mean(xv * xv, axis=-1, keepdims=True) + EPS)
            h_ref[rows, :] = ((xv * r) * g_ref[...] * (1.0 + mod_ref[1:2, :]) + mod_ref[0:1, :]).astype(BF16)
            return carry
        _blocks(TMB // RB, blk)
        ht_ref[...] = h_ref[...].T

    return pl.pallas_call(
        body, name="prep_h", grid=(T // TMB,),
        out_shape=(jax.ShapeDtypeStruct((T, D), BF16), jax.ShapeDtypeStruct((D, T), BF16)),
        in_specs=[pl.BlockSpec((TMB, D), lambda i: (i, 0)), pl.BlockSpec((3, D), lambda i: (0, 0)),
                  pl.BlockSpec((1, D), lambda i: (0, 0))],
        out_specs=(pl.BlockSpec((TMB, D), lambda i: (i, 0)), pl.BlockSpec((D, TMB), lambda i: (0, i))),
        compiler_params=_cp(("parallel",)),
    )(x, mod, g_pre)


def _ag_blocks(x, y, c):
    a = 1 - c
    n1 = (x ^ a, y ^ c)
    n2 = (x ^ c, y ^ a)
    dg = (1 - x, 1 - y)
    return [(x, y, c), (x, y, a), (*n1, c), (*n2, c), (*n1, a), (*n2, a), (*dg, c), (*dg, a)]


def _in_proj(h, w_in, order, w_co, w_so, w_o, conv_w):
    T = h.shape[0]
    tmp = min(2 * TMB, T)
    nb = T // tmp

    def body(order_ref, h_ref, w_ref, wco_ref, wso_ref, wo_ref, cw_ref, p_ref, wing_ref, wqg_ref, cwg_ref,
             wbuf, wq_st, send_sems, recv_sems, out_sems, ag_send, ag_recv, ag_loc):
        j = pl.program_id(0)
        i = pl.program_id(1)
        blocks = _ag_blocks(*_me())
        me, sibling, n1, n2, n1o, n2o, dg, dgo = blocks
        ag_start, ag_relay, ag_finish = _allgather_phases(
            [wq_st, cw_ref], lambda a, slot: wqg_ref.at[:, slot] if a == 0 else cwg_ref.at[slot], ag_send, ag_recv, ag_loc)

        def copy(k, block, to):
            d = wbuf.at[_slot(*block)]
            return pltpu.make_async_remote_copy(src_ref=d, dst_ref=d, send_sem=send_sems.at[k], recv_sem=recv_sems.at[k],
                                                device_id=to, device_id_type=MESH)

        def writeback(jj):
            s = _slot(*blocks[jj])
            return pltpu.make_async_copy(wbuf.at[s], wing_ref.at[s], out_sems.at[jj])

        sends = [copy(0, me, sibling), copy(1, me, n1), copy(2, me, n2)]
        relay = copy(3, n1, n2)
        passed = [copy(4, n1, sibling), copy(5, n2, sibling), copy(6, dg, sibling)]
        arrivals = {1: [(0, sibling)], 2: [(1, n1), (2, n2)], 4: [(5, n1o)], 5: [(4, n2o)], 6: [(3, dg)], 7: [(6, dgo)]}

        @pl.when((j == 0) & (i == 0))
        def _():
            wbuf[_slot(*me)] = w_ref[...].astype(BF16)
            for cp in sends:
                cp.start()
            writeback(0).start()
            wq_st[W_CO] = wco_ref[...].astype(BF16)
            wq_st[W_SO] = wso_ref[...].astype(BF16)
            wq_st[W_O] = wo_ref[...].astype(BF16)
            ag_start()

        for jj in range(1, NSEG):
            @pl.when((j == jj) & (i == 0))
            def _(jj=jj):
                for k, block in arrivals.get(jj, []):
                    copy(k, block, me).wait_recv()
                if jj == 2:
                    relay.start()
                    passed[0].start()
                    passed[1].start()
                    writeback(3).start()
                if jj == 6:
                    passed[2].start()
                if jj != 3:
                    writeback(jj).start()
                if jj == NSEG - 1:
                    ag_relay()

        p_ref[...] = jnp.dot(h_ref[...], wbuf[order_ref[j]], preferred_element_type=F32).astype(BF16)

        @pl.when((j == NSEG - 1) & (i == nb - 1))
        def _():
            for cp in sends + [relay] + passed:
                cp.wait_send()
            for jj in range(NSEG):
                writeback(jj).wait()
            ag_finish()

    vm = pl.BlockSpec(memory_space=VMEM)
    hbm = pl.BlockSpec(memory_space=ANY)
    return pl.pallas_call(
        body, name="in_proj",
        out_shape=(jax.ShapeDtypeStruct((T, NSEG * D), BF16), jax.ShapeDtypeStruct((NSEG, D, D), BF16),
                   jax.ShapeDtypeStruct((3, NDEV, 128, D), BF16), jax.ShapeDtypeStruct((NDEV, KW, 128), F32)),
        grid_spec=pltpu.PrefetchScalarGridSpec(
            num_scalar_prefetch=1, grid=(NSEG, nb),
            in_specs=[pl.BlockSpec((tmp, D), lambda j, i, o: (i, 0)), vm, vm, vm, vm, vm],
            out_specs=(pl.BlockSpec((tmp, D), lambda j, i, o: (i, o[j])), hbm, hbm, hbm),
            scratch_shapes=[VMEM((NSEG, D, D), BF16), VMEM((3, 128, D), BF16),
                            pltpu.SemaphoreType.DMA((7,)), pltpu.SemaphoreType.DMA((7,)), pltpu.SemaphoreType.DMA((NSEG,)),
                            pltpu.SemaphoreType.DMA((2, 7)), pltpu.SemaphoreType.DMA((2, 7)), pltpu.SemaphoreType.DMA((2,))]),
        compiler_params=_cp(("arbitrary", "arbitrary")),
    )(order, h, w_in, w_co, w_so, w_o, conv_w)


def _fill_a_ext(i, s_ref, val_ref, glu_ref, hval_ref, hglu_ref):
    ah = hval_ref[...].astype(F32) * _sig(hglu_ref[...].astype(F32))
    s_ref[0, 0:HALO, :] = jnp.where(i > 0, ah, 0.0)

    def fill(rb, carry):
        rows = _rows(rb)
        s_ref[0, pl.ds(pl.multiple_of(HALO + rb * RB, RB), RB), :] = _ld(val_ref, rows) * _sig(_ld(glu_ref, rows))
        return carry
    lax.fori_loop(0, TM // RB, fill, 0)


def _halo_prev(seg):
    hb = TM // HALO
    return pl.BlockSpec((HALO, D), lambda i: (jnp.maximum(i * hb - 1, 0), seg))


def _branch_a_fwd(p, cw, conv_b, lg, lb, wq):
    T = p.shape[0]

    def body(val_ref, glu_ref, z_ref, hval_ref, hglu_ref, cw_ref, cb_ref, lg_ref, lb_ref, w_ref,
             yc_ref, ya_ref, s_ref, yp_ref):
        i = pl.program_id(0)
        _fill_a_ext(i, s_ref, val_ref, glu_ref, hval_ref, hglu_ref)
        _shift_copies(s_ref)

        def conv(rb, carry):
            r0 = rb * RBC
            accs = [jnp.broadcast_to(cb_ref[...], (8, D))] * (RBC // 8)
            for k in range(KW):
                o = 2 + k
                w = cw_ref[k]
                accs = [acc + w * s_ref[o % 8, pl.ds(pl.multiple_of(r0 + 8 * (o // 8 + g), 8), 8), :]
                        for g, acc in enumerate(accs)]
            for g, acc in enumerate(accs):
                yc_ref[pl.ds(pl.multiple_of(r0 + 8 * g, 8), 8), :] = acc
            return carry
        lax.fori_loop(0, TM // RBC, conv, 0)

        def post(rb, carry):
            rows = _rows(rb)
            n, _ = _ln_stats(yc_ref[rows, :])
            l = n * lg_ref[...] + lb_ref[...]
            z = _ld(z_ref, rows)
            yp_ref[rows, :] = ((l * _sig(l)) * (z * _sig(z))).astype(BF16)
            return carry
        _blocks(TM // RB, post)
        ya_ref[...] = jnp.dot(yp_ref[...], w_ref[...], preferred_element_type=F32)

    tile = lambda seg: pl.BlockSpec((TM, D), lambda i: (i, seg))
    row = pl.BlockSpec((1, D), lambda i: (0, 0))
    return pl.pallas_call(
        body, name="branch_a_fwd", grid=(T // TM,),
        out_shape=(jax.ShapeDtypeStruct((T, D), F32), jax.ShapeDtypeStruct((T, D), F32)),
        in_specs=[tile(0), tile(1), tile(2), _halo_prev(0), _halo_prev(1),
                  pl.BlockSpec((KW, 8, D), lambda i: (0, 0, 0)), row, row, row,
                  pl.BlockSpec((None, D, D), lambda i: (W_CO, 0, 0))],
        out_specs=(pl.BlockSpec((TM, D), lambda i: (i, 0)), pl.BlockSpec((TM, D), lambda i: (i, 0))),
        scratch_shapes=[VMEM((8, EXT, D), F32), VMEM((TM, D), BF16)],
        compiler_params=_cp(("parallel",)),
    )(p, p, p, p, p, cw, conv_b, lg, lb, wq)


def _masked_ws(ws_ref, wt_ref):
    tri = lax.broadcasted_iota(jnp.int32, (CHUNK, CHUNK), 0) >= lax.broadcasted_iota(jnp.int32, (CHUNK, CHUNK), 1)
    for h in range(HEADS):
        wt_ref[h] = jnp.where(tri, ws_ref[h], 0.0).astype(BF16)


def _sgu_mix(wt_ref, vl_ref, bst_ref, s_ref):
    for ck in range(TM // CHUNK):
        r = slice(ck * CHUNK, (ck + 1) * CHUNK)
        for h in range(HEADS):
            cs = slice(h * HD, (h + 1) * HD)
            s_ref[r, cs] = jnp.dot(wt_ref[h], vl_ref[r, cs], preferred_element_type=F32) + bst_ref[:, h:h + 1]


def _branch_b_fwd(p, ws, bst, sg, sb, wq):
    T = p.shape[0]

    def body(pu_ref, pv_ref, pz_ref, ws_ref, bst_ref, sg_ref, sb_ref, w_ref, yb_ref,
             wt_ref, vl_ref, t_ref, s_ref, yp_ref):
        _masked_ws(ws_ref, wt_ref)

        def pre(rb, carry):
            rows = _rows(rb)
            vg, _ = _gelu_parts(_ld(pv_ref, rows))
            vn, _ = _ln_stats(vg)
            vl_ref[rows, :] = (vn * sg_ref[...] + sb_ref[...]).astype(BF16)
            u, _ = _gelu_parts(_ld(pu_ref, rows))
            z = _ld(pz_ref, rows)
            t_ref[rows, :] = u * (z * _sig(z))
            return carry
        _blocks(TM // RB, pre)
        _sgu_mix(wt_ref, vl_ref, bst_ref, s_ref)

        def post(rb, carry):
            rows = _rows(rb)
            yp_ref[rows, :] = (t_ref[rows, :] * s_ref[rows, :]).astype(BF16)
            return carry
        _blocks(TM // RB, post)
        yb_ref[...] = jnp.dot(yp_ref[...], w_ref[...], preferred_element_type=F32)

    tile = lambda seg: pl.BlockSpec((TM, D), lambda i: (i, seg))
    row = pl.BlockSpec((1, D), lambda i: (0, 0))
    return pl.pallas_call(
        body, name="branch_b_fwd", grid=(T // TM,),
        out_shape=jax.ShapeDtypeStruct((T, D), F32),
        in_specs=[tile(3), tile(4), tile(5), pl.BlockSpec((HEADS, CHUNK, CHUNK), lambda i: (0, 0, 0)),
                  pl.BlockSpec((CHUNK, HEADS), lambda i: (0, 0)), row, row,
                  pl.BlockSpec((None, D, D), lambda i: (W_SO, 0, 0))],
        out_specs=pl.BlockSpec((TM, D), lambda i: (i, 0)),
        scratch_shapes=[VMEM((HEADS, CHUNK, CHUNK), BF16), VMEM((TM, D), BF16), VMEM((TM, D), F32),
                        VMEM((TM, D), F32), VMEM((TM, D), BF16)],
        compiler_params=_cp(("parallel",)),
    )(p, p, p, ws, bst, sg, sb, wq)


def _merge_loss(p, ya, yb, x, tgt, mod, g_final, wq):
    T = x.shape[0]
    nt = T // TM

    def body(ga_ref, gb_ref, ya_ref, yb_ref, x_ref, t_ref, mod_ref, gf_ref, w_ref,
             dx2_ref, dya_ref, dyb_ref, dp_ref, gwo_ref, st_ref,
             mrg_s, out_s, dout_s, dm_s, gw_acc):
        i = pl.program_id(0)
        base = _ring_base(i, nt)
        cur = pl.ds(base, TM)

        @pl.when(i == 0)
        def _():
            gw_acc[...] = jnp.zeros_like(gw_acc)
            st_ref[...] = jnp.zeros_like(st_ref)

        def merge(rb, carry):
            rows = _rows(rb)
            mrg_s[_ring_rows(base, rb), :] = (_sig(_ld(ga_ref, rows)) * ya_ref[rows, :]
                                              + _sig(_ld(gb_ref, rows)) * yb_ref[rows, :]).astype(BF16)
            return carry
        lax.fori_loop(0, TM // RB, merge, 0)
        out_s[...] = jnp.dot(mrg_s[cur, :], w_ref[...], preferred_element_type=F32)

        def head(rb, carry):
            loss, gg, dg = carry
            rows = _rows(rb)
            gate = mod_ref[2:3, :]
            gf = gf_ref[...]
            out = out_s[rows, :]
            x2 = x_ref[rows, :] + gate * out
            r2 = lax.rsqrt(jnp.mean(x2 * x2, axis=-1, keepdims=True) + EPS)
            x2n = x2 * r2
            diff = x2n * gf - t_ref[rows, :]
            dy = diff * (1.0 / D)
            dx2n = dy * gf
            dx2 = r2 * (dx2n - x2n * jnp.mean(dx2n * x2n, axis=-1, keepdims=True))
            dx2_ref[rows, :] = dx2
            dout_s[_ring_rows(base, rb), :] = (dx2 * gate).astype(BF16)
            return loss + _fold8(diff * diff), gg + _fold8(dy * x2n), dg + _fold8(dx2 * out)
        zero = jnp.zeros((8, D), F32)
        loss, gg, dg = _blocks(TM // RB, head, (zero, zero, zero))
        st_ref[0] = st_ref[0] + loss * (0.5 / D)
        st_ref[1] = st_ref[1] + gg
        st_ref[2] = st_ref[2] + dg

        dm_s[...] = lax.dot_general(dout_s[cur, :], w_ref[...], NT, preferred_element_type=F32)

        @pl.when(_ring_full(i, nt))
        def _():
            gw_acc[...] += lax.dot_general(mrg_s[...], dout_s[...], TN, preferred_element_type=F32)

        def split(rb, carry):
            rows = _rows(rb)
            dm = dm_s[rows, :]
            sa = _sig(_ld(ga_ref, rows))
            sb = _sig(_ld(gb_ref, rows))
            dya_ref[rows, :] = (dm * sa).astype(BF16)
            dyb_ref[rows, :] = (dm * sb).astype(BF16)
            dp_ref[0, rows, :] = (dm * ya_ref[rows, :] * (sa * (1.0 - sa))).astype(BF16)
            dp_ref[1, rows, :] = (dm * yb_ref[rows, :] * (sb * (1.0 - sb))).astype(BF16)
            return carry
        lax.fori_loop(0, TM // RB, split, 0)

        @pl.when(i == nt - 1)
        def _():
            gwo_ref[...] = gw_acc[...].astype(BF16)

    tile = pl.BlockSpec((TM, D), lambda i: (i, 0))
    ring = VMEM((_ring_tiles(nt) * TM, D), BF16)
    return pl.pallas_call(
        body, name="merge_loss", grid=(nt,),
        out_shape=(jax.ShapeDtypeStruct((T, D), F32), jax.ShapeDtypeStruct((T, D), BF16), jax.ShapeDtypeStruct((T, D), BF16),
                   jax.ShapeDtypeStruct((NSEG, T, D), BF16), jax.ShapeDtypeStruct((D, D), BF16),
                   jax.ShapeDtypeStruct((3, 8, D), F32)),
        in_specs=[pl.BlockSpec((TM, D), lambda i: (i, 6)), pl.BlockSpec((TM, D), lambda i: (i, 7)), tile, tile, tile, tile,
                  pl.BlockSpec((3, D), lambda i: (0, 0)), pl.BlockSpec((1, D), lambda i: (0, 0)),
                  pl.BlockSpec((None, D, D), lambda i: (W_O, 0, 0))],
        out_specs=(tile, tile, tile, pl.BlockSpec((2, TM, D), lambda i: (3, i, 0)),
                   pl.BlockSpec((D, D), lambda i: (0, 0)), pl.BlockSpec((3, 8, D), lambda i: (0, 0, 0))),
        scratch_shapes=[ring, VMEM((TM, D), F32), ring, VMEM((TM, D), F32), VMEM((D, D), F32)],
        compiler_params=_cp(("arbitrary",)),
    )(p, p, ya, yb, x, tgt, mod, g_final, wq)


def _branch_b_bwd(dp, dyb, p, ws, bst, sg, sb, wq):
    T = p.shape[0]
    nt = T // TM

    def body(dp_in, dyb_ref, pu_ref, pv_ref, pz_ref, ws_ref, bst_ref, sg_ref, sb_ref, w_ref,
             dp_ref, gw_ref, gws_ref, gbs_ref, st_ref,
             wt_ref, d_s, vl_s, vn_s, gpv_s, rs_s, s_s, ds_s, ds32_s, yp_s, dy_s, dvl_s, gw_acc):
        del dp_in
        i = pl.program_id(0)
        base = _ring_base(i, nt)

        @pl.when(i == 0)
        def _():
            gw_acc[...] = jnp.zeros_like(gw_acc)
            gws_ref[...] = jnp.zeros_like(gws_ref)
            gbs_ref[...] = jnp.zeros_like(gbs_ref)
            st_ref[...] = jnp.zeros_like(st_ref)
            _masked_ws(ws_ref, wt_ref)

        d_s[...] = lax.dot_general(dyb_ref[...], w_ref[...], NT, preferred_element_type=F32)
        dy_s[pl.ds(base, TM), :] = dyb_ref[...]

        def pre(rb, carry):
            rows = _rows(rb)
            vg, gpv = _gelu_parts(_ld(pv_ref, rows))
            vn, rs = _ln_stats(vg)
            vl_s[rows, :] = (vn * sg_ref[...] + sb_ref[...]).astype(BF16)
            vn_s[rows, :] = vn
            gpv_s[rows, :] = gpv
            rs_s[rows, :] = rs
            return carry
        _blocks(TM // RB, pre)
        _sgu_mix(wt_ref, vl_s, bst_ref, s_s)

        def mid(rb, carry):
            rows = _rows(rb)
            u, gpu = _gelu_parts(_ld(pu_ref, rows))
            z = _ld(pz_ref, rows)
            sz = _sig(z)
            siluz = z * sz
            d = d_s[rows, :]
            s = s_s[rows, :]
            t = u * siluz
            yp_s[_ring_rows(base, rb), :] = (t * s).astype(BF16)
            dp_ref[0, rows, :] = (d * s * siluz * gpu).astype(BF16)
            ds = d * t
            ds32_s[rows, :] = ds
            ds_s[rows, :] = ds.astype(BF16)
            dp_ref[2, rows, :] = (d * u * s * (sz * (1.0 + z * (1.0 - sz)))).astype(BF16)
            return carry
        lax.fori_loop(0, TM // RB, mid, 0)

        for ck in range(TM // CHUNK):
            r = slice(ck * CHUNK, (ck + 1) * CHUNK)
            for h in range(HEADS):
                cs = slice(h * HD, (h + 1) * HD)
                dsh = ds_s[r, cs]
                dvl_s[r, cs] = lax.dot_general(wt_ref[h], dsh, TN, preferred_element_type=F32)
                gws_ref[h] += lax.dot_general(dsh, vl_s[r, cs], NT, preferred_element_type=F32)
                gbs_ref[h] += ds32_s[r, cs]

        def post(rb, carry):
            g_sg, g_sb = carry
            rows = _rows(rb)
            dvl = dvl_s[rows, :]
            vn = vn_s[rows, :]
            dvg = _ln_bwd(dvl * sg_ref[...], vn, rs_s[rows, :])
            dp_ref[1, rows, :] = (dvg * gpv_s[rows, :]).astype(BF16)
            return g_sg + _fold8(dvl * vn), g_sb + _fold8(dvl)
        zero = jnp.zeros((8, D), F32)
        g_sg, g_sb = _blocks(TM // RB, post, (zero, zero))
        st_ref[0] = st_ref[0] + g_sg
        st_ref[1] = st_ref[1] + g_sb

        @pl.when(_ring_full(i, nt))
        def _():
            gw_acc[...] += lax.dot_general(yp_s[...], dy_s[...], TN, preferred_element_type=F32)

        @pl.when(i == nt - 1)
        def _():
            gw_ref[...] = gw_acc[...].astype(BF16)

    tile = lambda seg: pl.BlockSpec((TM, D), lambda i: (i, seg))
    row = pl.BlockSpec((1, D), lambda i: (0, 0))
    hh = pl.BlockSpec((HEADS, CHUNK, CHUNK), lambda i: (0, 0, 0))
    ring = VMEM((_ring_tiles(nt) * TM, D), BF16)
    return pl.pallas_call(
        body, name="branch_b_bwd", grid=(nt,),
        out_shape=(jax.ShapeDtypeStruct((NSEG, T, D), BF16), jax.ShapeDtypeStruct((D, D), BF16),
                   jax.ShapeDtypeStruct((HEADS, CHUNK, CHUNK), F32), jax.ShapeDtypeStruct((HEADS, CHUNK, HD), F32),
                   jax.ShapeDtypeStruct((2, 8, D), F32)),
        in_specs=[pl.BlockSpec(memory_space=ANY), pl.BlockSpec((TM, D), lambda i: (i, 0)), tile(3), tile(4), tile(5),
                  hh, pl.BlockSpec((CHUNK, HEADS), lambda i: (0, 0)), row, row,
                  pl.BlockSpec((None, D, D), lambda i: (W_SO, 0, 0))],
        out_specs=(pl.BlockSpec((3, TM, D), lambda i: (1, i, 0)), pl.BlockSpec((D, D), lambda i: (0, 0)), hh, hh,
                   pl.BlockSpec((2, 8, D), lambda i: (0, 0, 0))),
        scratch_shapes=[VMEM((HEADS, CHUNK, CHUNK), BF16), VMEM((TM, D), F32), VMEM((TM, D), BF16), VMEM((TM, D), F32),
                        VMEM((TM, D), F32), VMEM((TM, 1), F32), VMEM((TM, D), F32), VMEM((TM, D), BF16), VMEM((TM, D), F32),
                        ring, ring, VMEM((TM, D), F32), VMEM((D, D), F32)],
        input_output_aliases={0: 0},
        compiler_params=_cp(("arbitrary",)),
    )(dp, dyb, p, p, p, ws, bst, sg, sb, wq)


def _branch_a_bwd1(dp, dya, yc, p, lg, lb, wq):
    T = p.shape[0]
    nt = T // TM

    def body(dp_in, dya_ref, yc_ref, z_ref, lg_ref, lb_ref, w_ref, dp_ref, dyc_ref, gw_ref, st_ref,
             d_s, yp_s, dy_s, gw_acc):
        del dp_in
        i = pl.program_id(0)
        base = _ring_base(i, nt)

        @pl.when(i == 0)
        def _():
            gw_acc[...] = jnp.zeros_like(gw_acc)
            st_ref[...] = jnp.zeros_like(st_ref)

        d_s[...] = lax.dot_general(dya_ref[...], w_ref[...], NT, preferred_element_type=F32)
        dy_s[pl.ds(base, TM), :] = dya_ref[...]

        def blk(rb, carry):
            g_lg, g_lb, g_cb = carry
            rows = _rows(rb)
            n, rs = _ln_stats(yc_ref[rows, :])
            l = n * lg_ref[...] + lb_ref[...]
            sgl = _sig(l)
            sl = l * sgl
            z = _ld(z_ref, rows)
            sz = _sig(z)
            siluz = z * sz
            d = d_s[rows, :]
            yp_s[_ring_rows(base, rb), :] = (sl * siluz).astype(BF16)
            dp_ref[rows, :] = (d * sl * (sz * (1.0 + z * (1.0 - sz)))).astype(BF16)
            dl = (d * siluz) * (sgl * (1.0 + l * (1.0 - sgl)))
            dyc = _ln_bwd(dl * lg_ref[...], n, rs)
            dyc_ref[rows, :] = dyc
            return g_lg + _fold8(dl * n), g_lb + _fold8(dl), g_cb + _fold8(dyc)
        zero = jnp.zeros((8, D), F32)
        g_lg, g_lb, g_cb = _blocks(TM // RB, blk, (zero, zero, zero))
        st_ref[0] = st_ref[0] + g_lg
        st_ref[1] = st_ref[1] + g_lb
        st_ref[2] = st_ref[2] + g_cb

        @pl.when(_ring_full(i, nt))
        def _():
            gw_acc[...] += lax.dot_general(yp_s[...], dy_s[...], TN, preferred_element_type=F32)

        @pl.when(i == nt - 1)
        def _():
            gw_ref[...] = gw_acc[...].astype(BF16)

    tile = pl.BlockSpec((TM, D), lambda i: (i, 0))
    row = pl.BlockSpec((1, D), lambda i: (0, 0))
    ring = VMEM((_ring_tiles(nt) * TM, D), BF16)
    return pl.pallas_call(
        body, name="branch_a_bwd1", grid=(nt,),
        out_shape=(jax.ShapeDtypeStruct((NSEG, T, D), BF16), jax.ShapeDtypeStruct((T, D), F32),
                   jax.ShapeDtypeStruct((D, D), BF16), jax.ShapeDtypeStruct((3, 8, D), F32)),
        in_specs=[pl.BlockSpec(memory_space=ANY), tile, tile, pl.BlockSpec((TM, D), lambda i: (i, 2)), row, row,
                  pl.BlockSpec((None, D, D), lambda i: (0, 0, 0))],
        out_specs=(pl.BlockSpec((None, TM, D), lambda i: (2, i, 0)), tile, pl.BlockSpec((D, D), lambda i: (0, 0)),
                   pl.BlockSpec((3, 8, D), lambda i: (0, 0, 0))),
        scratch_shapes=[VMEM((TM, D), F32), ring, ring, VMEM((D, D), F32)],
        input_output_aliases={0: 0},
        compiler_params=_cp(("arbitrary",)),
    )(dp, dya, yc, p, lg, lb, wq)


def _branch_a_bwd2(dp, dyc, p, cw, gw_co, gw_so, gw_o, early):
    T = p.shape[0]
    nt = T // TM
    hb = TM // HALO

    def body(dp_in, dyc_ref, hdyc_ref, val_ref, glu_ref, cw_ref, gco_ref, gso_ref, go_ref, e_ref,
             dp_ref, gcw_ref, rq_ref, eg_ref, a_s, sd_ref, da_s, x_send, x_recv, x_loc, ag_send, ag_recv, ag_loc):
        del dp_in
        i = pl.program_id(0)
        start_exchange, wait_exchange = _owner_exchange((gco_ref, gso_ref, go_ref), rq_ref, x_send, x_recv, x_loc)
        ag_start, ag_relay, ag_finish = _allgather_phases([e_ref], lambda a, slot: eg_ref.at[slot],
                                                          ag_send, ag_recv, ag_loc)

        @pl.when(i == 0)
        def _():
            gcw_ref[...] = jnp.zeros_like(gcw_ref)
            ag_start()
            start_exchange()

        @pl.when(i == nt // 2)
        def _():
            ag_relay()

        def fill(rb, carry):
            rows = _rows(rb)
            a_s[rows, :] = _ld(val_ref, rows) * _sig(_ld(glu_ref, rows))
            return carry
        lax.fori_loop(0, TM // RB, fill, 0)
        sd_ref[0, 0:TM, :] = dyc_ref[...]
        sd_ref[0, TM:EXT, :] = jnp.where(i < nt - 1, hdyc_ref[...], 0.0)
        _shift_copies(sd_ref)

        def conv_t(rb, carry):
            r0 = rb * RBC
            accs = [jnp.zeros((8, D), F32)] * (RBC // 8)
            for k in range(KW):
                o = KW - 1 - k
                w = cw_ref[k]
                accs = [acc + w * sd_ref[o % 8, pl.ds(pl.multiple_of(r0 + 8 * (o // 8 + g), 8), 8), :]
                        for g, acc in enumerate(accs)]
            for g, acc in enumerate(accs):
                da_s[pl.ds(pl.multiple_of(r0 + 8 * g, 8), 8), :] = acc
            return carry
        lax.fori_loop(0, TM // RBC, conv_t, 0)

        for k0 in range(0, KW, TAPG):
            taps = list(range(k0, min(k0 + TAPG, KW)))

            def tap_group(rb, accs, taps=taps):
                for u in range(2):
                    r0 = rb * 32 + 16 * u
                    a = a_s[pl.ds(pl.multiple_of(r0, 16), 16), :]
                    out = []
                    for k, acc in zip(taps, accs):
                        o = KW - 1 - k
                        d = sd_ref[o % 8, pl.ds(pl.multiple_of(r0 + 8 * (o // 8), 8), 16), :]
                        out.append(acc + _fold8(a * d))
                    accs = tuple(out)
                return accs
            sums = lax.fori_loop(0, TM // 32, tap_group, tuple(jnp.zeros((8, D), F32) for _ in taps))
            for k, s in zip(taps, sums):
                gcw_ref[k] = gcw_ref[k] + s

        def glu_b(rb, carry):
            rows = _rows(rb)
            da = da_s[rows, :]
            sg = _sig(_ld(glu_ref, rows))
            dp_ref[0, rows, :] = (da * sg).astype(BF16)
            dp_ref[1, rows, :] = (da * _ld(val_ref, rows) * (sg * (1.0 - sg))).astype(BF16)
            return carry
        lax.fori_loop(0, TM // RB, glu_b, 0)

        @pl.when(i == nt - 1)
        def _():
            wait_exchange()
            ag_finish()

    tile = lambda seg: pl.BlockSpec((TM, D), lambda i: (i, seg))
    hbm = pl.BlockSpec(memory_space=ANY)
    return pl.pallas_call(
        body, name="branch_a_bwd2", grid=(nt,),
        out_shape=(jax.ShapeDtypeStruct((NSEG, T, D), BF16), jax.ShapeDtypeStruct((32, 8, D), F32),
                   jax.ShapeDtypeStruct((NDEV, 3, 128, D), BF16), jax.ShapeDtypeStruct((NDEV, NEARLY, D), F32)),
        in_specs=[hbm, pl.BlockSpec((TM, D), lambda i: (i, 0)),
                  pl.BlockSpec((HALO, D), lambda i: (jnp.minimum((i + 1) * hb, nt * hb - 1), 0)),
                  tile(0), tile(1), pl.BlockSpec((KW, 8, D), lambda i: (0, 0, 0)),
                  hbm, hbm, hbm, hbm],
        out_specs=(pl.BlockSpec((2, TM, D), lambda i: (0, i, 0)), pl.BlockSpec((32, 8, D), lambda i: (0, 0, 0)), hbm, hbm),
        scratch_shapes=[VMEM((TM, D), F32), VMEM((8, EXT, D), F32), VMEM((TM, D), F32),
                        pltpu.SemaphoreType.DMA((3, 7)), pltpu.SemaphoreType.DMA((3, 7)), pltpu.SemaphoreType.DMA((3,)),
                        pltpu.SemaphoreType.DMA((1, 7)), pltpu.SemaphoreType.DMA((1, 7)), pltpu.SemaphoreType.DMA((1,))],
        input_output_aliases={0: 0},
        compiler_params=_cp(("arbitrary",)),
    )(dp, dyc, dyc, p, p, cw, gw_co, gw_so, gw_o, early)


def _in_bwd_dx(dp, wing, x, dx2, mod, g_pre, token):
    T = x.shape[0]
    nt = T // TM

    def body(dp_ref, w_ref, x_ref, dx2_ref, mod_ref, g_ref, token_ref, gx_ref, st_ref, dh_s):
        del token_ref
        i = pl.program_id(0)

        @pl.when(i == 0)
        def _():
            st_ref[...] = jnp.zeros_like(st_ref)

        dh_s[...] = lax.dot_general(dp_ref[0], w_ref[0], NT, preferred_element_type=F32)
        for j in range(1, NSEG):
            dh_s[...] += lax.dot_general(dp_ref[j], w_ref[j], NT, preferred_element_type=F32)

        def blk(rb, carry):
            d_sh, d_sc, g_g = carry
            rows = _rows(rb)
            xv = x_ref[rows, :]
            r = lax.rsqrt(jnp.mean(xv * xv, axis=-1, keepdims=True) + EPS)
            xn = xv * r
            g = g_ref[...]
            hpre = xn * g
            dh = dh_s[rows, :]
            dhp = dh * (1.0 + mod_ref[1:2, :])
            dxn = dhp * g
            gx_ref[rows, :] = dx2_ref[rows, :] + r * (dxn - xn * jnp.mean(dxn * xn, axis=-1, keepdims=True))
            return d_sh + _fold8(dh), d_sc + _fold8(dh * hpre), g_g + _fold8(dhp * xn)
        zero = jnp.zeros((8, D), F32)
        d_sh, d_sc, g_g = _blocks(TM // RB, blk, (zero, zero, zero))
        st_ref[0] = st_ref[0] + d_sh
        st_ref[1] = st_ref[1] + d_sc
        st_ref[2] = st_ref[2] + g_g

    tile = pl.BlockSpec((TM, D), lambda i: (i, 0))
    return pl.pallas_call(
        body, name="in_bwd_dx", grid=(nt,),
        out_shape=(jax.ShapeDtypeStruct((T, D), F32), jax.ShapeDtypeStruct((3, 8, D), F32)),
        in_specs=[pl.BlockSpec((NSEG, TM, D), lambda i: (0, i, 0)),
                  pl.BlockSpec((NSEG, D, D), lambda i: (0, 0, 0), pipeline_mode=pl.Buffered(1)),
                  tile, tile, pl.BlockSpec((3, D), lambda i: (0, 0)), pl.BlockSpec((1, D), lambda i: (0, 0)),
                  pl.BlockSpec((8, 128), lambda i: (0, 0))],
        out_specs=(tile, pl.BlockSpec((3, 8, D), lambda i: (0, 0, 0))),
        scratch_shapes=[VMEM((TM, D), F32)],
        compiler_params=_cp(("arbitrary",), vmem_mb=56),
    )(dp, wing, x, dx2, mod, g_pre, token)


RS_ORDER = (5, 4, 3, 2, 7, 6, 1, 0)
RS_HALF = D // 2


def _in_bwd_dw(ht, dp, order):
    T = ht.shape[1]
    tmw = min(4 * TMB, T)
    nb = T // tmw
    nu = 2 * NSEG

    def body(order_ref, ht_ref, dp_ref, csum_ref, g_ref, acc, sbuf, rsib, send_sems, recv_sems, out_sems):
        j = pl.program_id(0)
        i = pl.program_id(1)
        x, y, c = _me()
        sibling = (x, y, 1 - c)

        def to_sibling(s):
            return pltpu.make_async_remote_copy(src_ref=sbuf.at[s], dst_ref=rsib.at[s], send_sem=send_sems.at[s],
                                                recv_sem=recv_sems.at[s], device_id=sibling, device_id_type=MESH)

        def to_csum(t):
            cols = pl.ds((t % 2) * RS_HALF, RS_HALF)
            return pltpu.make_async_copy(sbuf.at[8 + t], csum_ref.at[t // 2, :, cols], out_sems.at[t])

        @pl.when(i == 0)
        def _():
            acc[...] = jnp.zeros_like(acc)
        acc[...] += jnp.dot(ht_ref[...], dp_ref[...], preferred_element_type=F32)

        for jj in range(nu):
            chip, half, mine = jj // 4, (jj % 4) // 2, jj % 2
            s = 2 * chip + half

            @pl.when((j == jj) & (i == nb - 1))
            def _(chip=chip, half=half, mine=mine, s=s):
                if not mine:
                    sbuf[s] = acc[...].astype(BF16)
                    to_sibling(s).start()
                elif chip < 3:
                    to_sibling(s).wait_recv()
                    sbuf[8 + s] = (acc[...] + rsib[s].astype(F32)).astype(BF16)
                    to_csum(s).start()
                else:
                    to_sibling(s).wait_recv()
                    g_ref[:, half * RS_HALF:(half + 1) * RS_HALF] = acc[...] + rsib[s].astype(F32)

        @pl.when((j == nu - 1) & (i == nb - 1))
        def _():
            for s in range(8):
                to_sibling(s).wait_send()
            for t in range(6):
                to_csum(t).wait()

    hbm = pl.BlockSpec(memory_space=ANY)
    return pl.pallas_call(
        body, name="in_bwd_dw",
        out_shape=(jax.ShapeDtypeStruct((3, D, D), BF16), jax.ShapeDtypeStruct((D, D), F32)),
        grid_spec=pltpu.PrefetchScalarGridSpec(
            num_scalar_prefetch=1, grid=(nu, nb),
            in_specs=[pl.BlockSpec((D, tmw), lambda j, i, o: (0, i)),
                      pl.BlockSpec((None, tmw, RS_HALF), lambda j, i, o: (o[2 * (j // 4) + j % 2], i, (j % 4) // 2))],
            out_specs=(hbm, pl.BlockSpec((D, D), lambda j, i, o: (0, 0))),
            scratch_shapes=[VMEM((D, RS_HALF), F32), VMEM((14, D, RS_HALF), BF16), VMEM((8, D, RS_HALF), BF16),
                            pltpu.SemaphoreType.DMA((8,)), pltpu.SemaphoreType.DMA((8,)), pltpu.SemaphoreType.DMA((6,))]),
        compiler_params=_cp(("arbitrary", "arbitrary"), vmem_mb=56),
    )(order, ht, dp)


def _gather_late(late):
    def body(sm_ref, smg_ref, ag_send, ag_recv, ag_loc):
        _allgather([sm_ref], lambda a, slot: smg_ref.at[slot], ag_send, ag_recv, ag_loc)

    hbm = pl.BlockSpec(memory_space=ANY)
    return pl.pallas_call(
        body, name="gather_late",
        out_shape=jax.ShapeDtypeStruct((NDEV, NLATE, D), F32),
        in_specs=[hbm], out_specs=hbm,
        scratch_shapes=[pltpu.SemaphoreType.DMA((1, 7)), pltpu.SemaphoreType.DMA((1, 7)), pltpu.SemaphoreType.DMA((1,))],
        compiler_params=_cp(vmem_mb=32),
    )(late)


RS_CHIPS = ((1, 0), (0, 1), (1, 1))


def _rs_peer(q):
    x, y, c = _me()
    fx, fy = RS_CHIPS[q]
    return (1 - x if fx else x, 1 - y if fy else y, c)


def _rs_start(csum):
    def body(csum_ref, land_ref, send_sems, recv_sems, csum_thru, land_thru, token):
        barrier = pltpu.get_barrier_semaphore()
        for q in range(3):
            pl.semaphore_signal(barrier, inc=1, device_id=_rs_peer(q), device_id_type=MESH)
        pl.semaphore_wait(barrier, 3)
        for q in range(3):
            pltpu.make_async_remote_copy(src_ref=csum_ref.at[q], dst_ref=land_ref.at[q], send_sem=send_sems.at[q],
                                         recv_sem=recv_sems.at[q], device_id=_rs_peer(q), device_id_type=MESH).start()
        token[...] = jnp.zeros_like(token)

    hbm = pl.BlockSpec(memory_space=pltpu.HBM)
    sem = pl.BlockSpec(memory_space=pltpu.SEMAPHORE)
    land = pltpu.with_memory_space_constraint(lax.empty(csum.shape, csum.dtype), pltpu.HBM)
    return pl.pallas_call(
        body, name="rs_start",
        out_shape=(pltpu.SemaphoreType.DMA((3,)), pltpu.SemaphoreType.DMA((3,)), pltpu.HBM(csum.shape, csum.dtype),
                   pltpu.HBM(csum.shape, csum.dtype), jax.ShapeDtypeStruct((8, 128), F32)),
        in_specs=(hbm, hbm), out_specs=(sem, sem, hbm, hbm, pl.BlockSpec(memory_space=VMEM)),
        input_output_aliases={0: 2, 1: 3},
        compiler_params=pltpu.CompilerParams(has_side_effects=pltpu.SideEffectType.DATAFLOW_SIDE_EFFECTING, collective_id=1),
    )(pltpu.with_memory_space_constraint(csum, pltpu.HBM), land)


def _rs_wait(send_sems, recv_sems, csum_thru, land_thru, after):
    def body(csum_ref, land_ref, send_sems, recv_sems, after_ref, csum_dead, got_ref):
        del after_ref, csum_dead, got_ref
        for q in range(3):
            cp = pltpu.make_async_remote_copy(src_ref=csum_ref.at[q], dst_ref=land_ref.at[q], send_sem=send_sems.at[q],
                                              recv_sem=recv_sems.at[q], device_id=_rs_peer(q), device_id_type=MESH)
            cp.wait_send()
            cp.wait_recv()

    hbm = pl.BlockSpec(memory_space=pltpu.HBM)
    sem = pl.BlockSpec(memory_space=pltpu.SEMAPHORE)
    return pl.pallas_call(
        body, name="rs_wait",
        out_shape=(pltpu.HBM(csum_thru.shape, csum_thru.dtype), pltpu.HBM(csum_thru.shape, csum_thru.dtype)),
        in_specs=(hbm, hbm, sem, sem, pl.BlockSpec(memory_space=ANY)), out_specs=(hbm, hbm),
        input_output_aliases={0: 0, 1: 1},
        compiler_params=pltpu.CompilerParams(has_side_effects=pltpu.SideEffectType.DATAFLOW_SIDE_EFFECTING),
    )(csum_thru, land_thru, send_sems, recv_sems, after)[1]


def _pack_early(st_a, st_b, st_m, gws, gbs):
    def body(sa_ref, sb_ref, sm_ref, gws_ref, gbs_ref, out_ref):
        out_ref[...] = jnp.zeros_like(out_ref)
        fold = lambda v: jnp.sum(v, axis=0, keepdims=True)
        out_ref[0:1, :] = fold(sm_ref[2])
        out_ref[1:2, :] = fold(sa_ref[2])
        out_ref[2:3, :] = fold(sa_ref[0])
        out_ref[3:4, :] = fold(sa_ref[1])
        out_ref[4:5, :] = fold(sb_ref[0])
        out_ref[5:6, :] = fold(sb_ref[1])
        out_ref[7:8, :] = fold(sm_ref[1])
        out_ref[8:9, :] = fold(sm_ref[0])
        tri = lax.broadcasted_iota(jnp.int32, (CHUNK, CHUNK), 0) >= lax.broadcasted_iota(jnp.int32, (CHUNK, CHUNK), 1)
        for h in range(HEADS):
            out_ref[16:16 + CHUNK, h * CHUNK:(h + 1) * CHUNK] = jnp.where(tri, gws_ref[h], 0.0)
            out_ref[6:7, h * CHUNK:(h + 1) * CHUNK] = fold(gbs_ref[h].T)

    vm = pl.BlockSpec(memory_space=VMEM)
    return pl.pallas_call(
        body, name="pack_early", out_shape=jax.ShapeDtypeStruct((NEARLY, D), F32),
        in_specs=[vm] * 5, out_specs=vm, compiler_params=_cp(vmem_mb=32),
    )(st_a, st_b, st_m, gws, gbs)


def _pack_late(st_x, gcw):
    def body(sx_ref, gcw_ref, out_ref):
        out_ref[...] = jnp.zeros_like(out_ref)
        fold = lambda v: jnp.sum(v, axis=0, keepdims=True)
        for r in range(3):
            out_ref[r:r + 1, :] = fold(sx_ref[r])
        for k in range(KW):
            out_ref[8 + k:9 + k, :] = fold(gcw_ref[k])

    vm = pl.BlockSpec(memory_space=VMEM)
    return pl.pallas_call(
        body, name="pack_late", out_shape=jax.ShapeDtypeStruct((NLATE, D), F32),
        in_specs=[vm] * 2, out_specs=vm, compiler_params=_cp(vmem_mb=32),
    )(st_x, gcw)


def _pack_params(name, b_ada, g_pre, conv_b, lg, lb, sg, sb, b_sgu, g_final, w_sgu_t):
    def body(ba_ref, *refs):
        rows, ws_ref, out_ref = refs[:8], refs[8], refs[9]
        out_ref[...] = jnp.zeros_like(out_ref)
        for r in range(3):
            out_ref[r:r + 1, :] = ba_ref[:, r * D:(r + 1) * D]
        for r, ref in enumerate(rows):
            out_ref[3 + r:4 + r, :] = ref[...]
        out_ref[ROW_WS:ROW_WS + CHUNK, :] = ws_ref[...]

    vm = pl.BlockSpec(memory_space=VMEM)
    return pl.pallas_call(
        body, name=name, out_shape=jax.ShapeDtypeStruct((NSMALL, D), F32),
        in_specs=[vm] * 10, out_specs=vm, compiler_params=_cp(vmem_mb=32),
    )(b_ada, g_pre, conv_b, lg, lb, sg, sb, b_sgu, g_final, w_sgu_t)


def _adam(w, g, m, v):
    m2 = ADAM_B1 * m + (1.0 - ADAM_B1) * g
    v2 = ADAM_B2 * v + (1.0 - ADAM_B2) * (g * g)
    m_hat = m2 / (1.0 - ADAM_B1 ** ADAM_STEP)
    v_hat = v2 / (1.0 - ADAM_B2 ** ADAM_STEP)
    delta = -ADAM_LR * (m_hat / (jnp.sqrt(v_hat) + ADAM_EPS) + ADAM_WD * w)
    return delta, m2, v2


def _small_finish(eg, lg, w, m, v):
    def body(eg_ref, lg_ref, w_ref, m_ref, v_ref, loss_ref, g_ref, d_ref, m2_ref, v2_ref):
        e = eg_ref[0]
        l = lg_ref[0]
        for s in range(1, NDEV):
            e = e + eg_ref[s]
            l = l + lg_ref[s]
        g_ref[...] = jnp.zeros_like(g_ref)
        g_ref[0:2, :] = l[0:2, :]
        g_ref[2:3, :] = e[0:1, :]
        g_ref[3:4, :] = l[2:3, :]
        g_ref[4:12, :] = e[1:9, :]
        g_ref[ROW_CW:ROW_CW + KW, :] = l[8:8 + KW, :]
        g_ref[ROW_WS:ROW_WS + CHUNK, :] = e[16:16 + CHUNK, :]
        loss_ref[...] = jnp.sum(e[8:9, :], axis=1, keepdims=True)
        d_ref[...], m2_ref[...], v2_ref[...] = _adam(w_ref[...], g_ref[...], m_ref[...], v_ref[...])

    vm = pl.BlockSpec(memory_space=VMEM)
    sd = jax.ShapeDtypeStruct((NSMALL, D), F32)
    return pl.pallas_call(
        body, name="small_finish", out_shape=(jax.ShapeDtypeStruct((1, 1), F32), sd, sd, sd, sd),
        in_specs=[vm] * 5, out_specs=(vm,) * 5, compiler_params=_cp(vmem_mb=32),
    )(eg, lg, w, m, v)


def _ada_grad_adam(ct, dm, w, m, v):
    def body(ct_ref, dm_ref, w_ref, m_ref, v_ref, g_ref, d_ref, m2_ref, v2_ref):
        g = ct_ref[:, 0:1] * dm_ref[0:1, :]
        for b in range(1, NDEV):
            g = g + ct_ref[:, b:b + 1] * dm_ref[b:b + 1, :]
        g_ref[...] = g
        d_ref[...], m2_ref[...], v2_ref[...] = _adam(w_ref[...], g, m_ref[...], v_ref[...])

    vm = pl.BlockSpec(memory_space=VMEM)
    sd = jax.ShapeDtypeStruct(w.shape, F32)
    return pl.pallas_call(
        body, name="ada_grad_adam", out_shape=(sd, sd, sd, sd),
        in_specs=[vm] * 5, out_specs=(vm,) * 4, compiler_params=_cp(vmem_mb=32),
    )(ct, dm, w, m, v)


def _adam_f32(name, g, w, m, v, rows=None):
    R, C = w.shape
    rows = R if rows is None else rows

    def body(g_ref, w_ref, m_ref, v_ref, d_ref, m2_ref, v2_ref):
        d_ref[...], m2_ref[...], v2_ref[...] = _adam(w_ref[...], g_ref[...], m_ref[...], v_ref[...])

    tile = pl.BlockSpec((rows, C), lambda i: (i, 0))
    sd = jax.ShapeDtypeStruct(w.shape, F32)
    return pl.pallas_call(
        body, name=name, grid=(R // rows,), out_shape=(sd, sd, sd), in_specs=[tile] * 4, out_specs=(tile,) * 3,
        compiler_params=_cp(("parallel",), vmem_mb=32),
    )(g, w, m, v)


def _adam_w_in(gown, got, w, m, v, rows=256):
    R, C = w.shape

    def body(go_ref, got_ref, w_ref, m_ref, v_ref, g_ref, d_ref, m2_ref, v2_ref):
        g = go_ref[...]
        for q in range(3):
            g = g + got_ref[q].astype(F32)
        g_ref[...] = g
        d_ref[...], m2_ref[...], v2_ref[...] = _adam(w_ref[...], g, m_ref[...], v_ref[...])

    tile = pl.BlockSpec((rows, C), lambda i: (i, 0))
    sd = jax.ShapeDtypeStruct((R, C), F32)
    return pl.pallas_call(
        body, name="adam_w_in", grid=(R // rows,), out_shape=(sd, sd, sd, sd),
        in_specs=[tile, pl.BlockSpec((3, rows, C), lambda i: (0, i, 0)), tile, tile, tile], out_specs=(tile,) * 4,
        compiler_params=_cp(("parallel",), vmem_mb=32),
    )(gown, got, w, m, v)


def _adam_reduce(name, recv, w, m, v, rows, recv_spec):
    R, C = w.shape

    def body(r_ref, w_ref, m_ref, v_ref, g_ref, d_ref, m2_ref, v2_ref):
        g = r_ref[0].astype(F32)
        for s in range(1, NDEV):
            g = g + r_ref[s].astype(F32)
        g_ref[...] = g
        d_ref[...], m2_ref[...], v2_ref[...] = _adam(w_ref[...], g, m_ref[...], v_ref[...])

    tile = pl.BlockSpec((rows, C), lambda i: (i, 0))
    sd = jax.ShapeDtypeStruct((R, C), F32)
    return pl.pallas_call(
        body, name=name, grid=(R // rows,), out_shape=(sd, sd, sd, sd),
        in_specs=[recv_spec, tile, tile, tile], out_specs=(tile,) * 4,
        compiler_params=_cp(("parallel",)),
    )(recv, w, m, v)


def kernel(x, c, w_ada, b_ada, g_pre, w_in, conv_w, conv_b, conv_ln_g, conv_ln_b, w_conv_out, sgu_ln_g, sgu_ln_b, w_sgu, b_sgu, w_sgu_out, w_o, g_final, loss_target, m_w_ada, m_b_ada, m_g_pre, m_w_in, m_conv_w, m_conv_b, m_conv_ln_g, m_conv_ln_b, m_w_conv_out, m_sgu_ln_g, m_sgu_ln_b, m_w_sgu, m_b_sgu, m_w_sgu_out, m_w_o, m_g_final, v_w_ada, v_b_ada, v_g_pre, v_w_in, v_conv_w, v_conv_b, v_conv_ln_g, v_conv_ln_b, v_w_conv_out, v_sgu_ln_g, v_sgu_ln_b, v_w_sgu, v_b_sgu, v_w_sgu_out, v_w_o, v_g_final):
    T = x.shape[1]
    assert T % TMB == 0 and x.shape[2] == D
    xs = x[0]
    tgt = loss_target[0]
    my = 4 * lax.axis_index("x") + 2 * lax.axis_index("y") + lax.axis_index("c")

    mod, cg = _ada_fwd(c, w_ada[0], b_ada)
    gf = g_final.reshape(1, D)
    bst = b_sgu[0].T

    h, ht = _prep_h(xs, mod, g_pre)
    ag_order = jnp.stack([_slot(*b) for b in _ag_blocks(lax.axis_index("x"), lax.axis_index("y"), lax.axis_index("c"))])
    p, wing, wqg, cwg = _in_proj(h, w_in[0], ag_order.astype(jnp.int32),
                                 w_conv_out[0], w_sgu_out[0], w_o[0], conv_w[0])
    wq = wqg.reshape(3, D, D)
    cw = jnp.broadcast_to(jnp.transpose(cwg, (1, 0, 2)).reshape(KW, 1, D), (KW, 8, D))
    yc, ya = _branch_a_fwd(p, cw, conv_b, conv_ln_g, conv_ln_b, wq)
    yb = _branch_b_fwd(p, w_sgu[0], bst, sgu_ln_g, sgu_ln_b, wq)
    dx2, dya, dyb, dp, gw_o, st_m = _merge_loss(p, ya, yb, xs, tgt, mod, gf, wq)

    dp, gw_so, gws, gbs, st_b = _branch_b_bwd(dp, dyb, p, w_sgu[0], bst, sgu_ln_g, sgu_ln_b, wq)
    dp, dyc, gw_co, st_a = _branch_a_bwd1(dp, dya, yc, p, conv_ln_g, conv_ln_b, wq)
    early = _pack_early(st_a, st_b, st_m, gws, gbs)
    dp, gcw, rq, eg = _branch_a_bwd2(dp, dyc, p, cw, gw_co, gw_so, gw_o, early)
    csum, gown = _in_bwd_dw(ht, dp, my ^ jnp.array(RS_ORDER, jnp.int32))
    rs_send, rs_recv, csum_thru, land_thru, token = _rs_start(csum)
    grad_x, st_x = _in_bwd_dx(dp, wing, xs, dx2, mod, g_pre, token)
    got = _rs_wait(rs_send, rs_recv, csum_thru, land_thru, st_x)
    lg = _gather_late(_pack_late(st_x, gcw))

    def pack(name, b_ada_, g_pre_, conv_b_, lg_, lb_, sg_, sb_, b_sgu_, gfin_, w_sgu_):
        return _pack_params(name, b_ada_, g_pre_, conv_b_, lg_, lb_, sg_, sb_, b_sgu_.reshape(1, D), gfin_.reshape(1, D),
                            jnp.transpose(w_sgu_[0], (1, 0, 2)).reshape(CHUNK, D))

    pw = pack("pack_w", b_ada, g_pre, conv_b, conv_ln_g, conv_ln_b, sgu_ln_g, sgu_ln_b, b_sgu, g_final, w_sgu)
    pm = pack("pack_m", m_b_ada, m_g_pre, m_conv_b, m_conv_ln_g, m_conv_ln_b, m_sgu_ln_g, m_sgu_ln_b, m_b_sgu, m_g_final,
              m_w_sgu)
    pv = pack("pack_v", v_b_ada, v_g_pre, v_conv_b, v_conv_ln_g, v_conv_ln_b, v_sgu_ln_g, v_sgu_ln_b, v_b_sgu, v_g_final,
              v_w_sgu)
    loss11, sg_, sd_, sm_, sv_ = _small_finish(eg, lg, pw, pm, pv)

    def unpack(a):
        return dict(
            b_ada=a[0:3].reshape(1, 3 * D), g_pre=a[3:4], conv_b=a[4:5], conv_ln_g=a[5:6], conv_ln_b=a[6:7],
            sgu_ln_g=a[7:8], sgu_ln_b=a[8:9], b_sgu=a[9:10].reshape(1, HEADS, CHUNK), g_final=a[10],
            w_sgu=jnp.transpose(a[ROW_WS:ROW_WS + CHUNK].reshape(CHUNK, HEADS, CHUNK), (1, 0, 2))[None])
    small = [unpack(a) for a in (sg_, sd_, sm_, sv_)]

    g_cw = lax.dynamic_slice_in_dim(sg_[ROW_CW:ROW_CW + KW], my * 128, 128, axis=1)
    d_cw, m_cw, v_cw = _adam_f32("adam_conv_w", g_cw, conv_w[0], m_conv_w[0], v_conv_w[0])

    wcols = w_ada.shape[2]
    dm_all = jnp.concatenate([lg[:, 0], lg[:, 1], eg[:, 0]], axis=1)
    dm_mine = lax.dynamic_slice_in_dim(dm_all, my * wcols, wcols, axis=1)
    g_ada, d_ada, m_ada, v_ada = _ada_grad_adam(cg.T, dm_mine, w_ada[0], m_w_ada[0], v_w_ada[0])

    g_in, d_in, m_in, v_in = _adam_w_in(gown, got, w_in[0], m_w_in[0], v_w_in[0])
    big = {}
    for j, (nm, w_, m_, v_) in enumerate((("w_conv_out", w_conv_out, m_w_conv_out, v_w_conv_out),
                                          ("w_sgu_out", w_sgu_out, m_w_sgu_out, v_w_sgu_out),
                                          ("w_o", w_o, m_w_o, v_w_o))):
        big[nm] = _adam_reduce("adam_" + nm, rq, w_[0], m_[0], v_[0], 128,
                               pl.BlockSpec((NDEV, None, 128, D), lambda i, j=j: (0, j, 0, 0)))

    per = {
        "w_ada": tuple(a[None] for a in (g_ada, d_ada, m_ada, v_ada)),
        "w_in": tuple(a[None] for a in (g_in, d_in, m_in, v_in)),
        "conv_w": tuple(a[None] for a in (g_cw, d_cw, m_cw, v_cw)),
    }
    for nm in ("w_conv_out", "w_sgu_out", "w_o"):
        per[nm] = tuple(a[None] for a in big[nm])
    for nm in ("b_ada", "g_pre", "conv_b", "conv_ln_g", "conv_ln_b", "sgu_ln_g", "sgu_ln_b", "w_sgu", "b_sgu", "g_final"):
        per[nm] = tuple(s[nm] for s in small)

    order = ["w_ada", "b_ada", "g_pre", "w_in", "conv_w", "conv_b", "conv_ln_g", "conv_ln_b", "w_conv_out",
             "sgu_ln_g", "sgu_ln_b", "w_sgu", "b_sgu", "w_sgu_out", "w_o", "g_final"]
    outs = [loss11.reshape(()), grad_x[None]]
    for part in range(4):
        outs += [per[nm][part] for nm in order]
    return tuple(outs)
```

```python
import jax
import jax.numpy as jnp
from jax import lax
from jax.experimental import pallas as pl
from jax.experimental.pallas import tpu as pltpu

F32 = jnp.float32
BF16 = jnp.bfloat16
MESH = pl.DeviceIdType.MESH
VMEM = pltpu.VMEM
ANY = pl.ANY

D = 1024
NDEV = 8
NSEG = 8
HEADS = 8
HD = 128
CHUNK = 128
KW = 31
HALO = 32
EPS = 1e-6
TM = 256
TMB = 512
RB = 32
UNROLL = 4
RBC = 32
TAPG = 4
EXT = TM + HALO
NSMALL = 176
NEARLY = 144
NLATE = 40
ROW_CW = 16
ROW_WS = 48
W_CO, W_SO, W_O = 0, 1, 2

ADAM_LR = 0.001
ADAM_B1 = 0.9
ADAM_B2 = 0.999
ADAM_EPS = 1e-08
ADAM_WD = 0.01
ADAM_STEP = 10

INV_SQRT2 = 0.7071067811865476
INV_SQRT_2PI = 0.3989422804014327

NT = (((1,), (1,)), ((), ()))
TN = (((0,), (0,)), ((), ()))


def _cp(sem=None, vmem_mb=48):
    return pltpu.CompilerParams(dimension_semantics=sem, vmem_limit_bytes=vmem_mb * 1024 * 1024)


def _me():
    return lax.axis_index("x"), lax.axis_index("y"), lax.axis_index("c")


def _slot(px, py, pc):
    return 4 * px + 2 * py + pc


def _xor_peer(k):
    x, y, c = _me()
    return (1 - x if k & 4 else x, 1 - y if k & 2 else y, 1 - c if k & 1 else c)


def _allgather_phases(srcs, dst_at, send_sems, recv_sems, loc_sems):
    x, y, c = _me()
    me = (x, y, c)
    sibling = (x, y, 1 - c)
    chips = [(1 - x, y), (x, 1 - y), (1 - x, 1 - y)]
    na = len(srcs)

    def copy(a, k, block, to, src=None):
        d = dst_at(a, _slot(*block))
        return pltpu.make_async_remote_copy(
            src_ref=d if src is None else src, dst_ref=d,
            send_sem=send_sems.at[a, k], recv_sem=recv_sems.at[a, k],
            device_id=to, device_id_type=MESH)

    local = [pltpu.make_async_copy(srcs[a], dst_at(a, _slot(*me)), loc_sems.at[a]) for a in range(na)]
    first = []
    for a in range(na):
        first.append(copy(a, 0, me, sibling, src=srcs[a]))
        for j, chip in enumerate(chips):
            first.append(copy(a, 1 + j, me, (*chip, c), src=srcs[a]))
    passed = [copy(a, 4 + j, (*chip, c), sibling) for j, chip in enumerate(chips) for a in range(na)]

    def start():
        for cp in local + first:
            cp.start()

    def relay():
        for j, chip in enumerate(chips):
            for a in range(na):
                copy(a, 1 + j, (*chip, c), me).wait_recv()
                passed[j * na + a].start()

    def finish():
        for a in range(na):
            copy(a, 0, sibling, me).wait_recv()
        for j, chip in enumerate(chips):
            for a in range(na):
                copy(a, 4 + j, (*chip, 1 - c), me).wait_recv()
        for cp in first + passed:
            cp.wait_send()
        for cp in local:
            cp.wait()

    return start, relay, finish


def _allgather(srcs, dst_at, send_sems, recv_sems, loc_sems):
    start, relay, finish = _allgather_phases(srcs, dst_at, send_sems, recv_sems, loc_sems)
    start()
    relay()
    finish()


def _owner_exchange(gq, rq_ref, x_send, x_recv, x_loc):
    mx, my_, mc = _me()
    me = _slot(mx, my_, mc)

    def rows_of(a, slot):
        return gq[a].at[pl.ds(pl.multiple_of(slot * 128, 128), 128), :]

    def exchange(k, a, recv):
        px, py, pc = _xor_peer(k)
        peer = _slot(px, py, pc)
        return pltpu.make_async_remote_copy(
            src_ref=rows_of(a, me if recv else peer), dst_ref=rq_ref.at[peer if recv else me, a],
            send_sem=x_send.at[a, k - 1], recv_sem=x_recv.at[a, k - 1],
            device_id=(px, py, pc), device_id_type=MESH)

    local = [pltpu.make_async_copy(rows_of(a, me), rq_ref.at[me, a], x_loc.at[a]) for a in range(len(gq))]

    def start():
        for cp in local:
            cp.start()
        for k in range(1, NDEV):
            for a in range(len(gq)):
                exchange(k, a, False).start()

    def wait():
        for k in range(1, NDEV):
            for a in range(len(gq)):
                exchange(k, a, True).wait_recv()
        for k in range(1, NDEV):
            for a in range(len(gq)):
                exchange(k, a, False).wait_send()
        for cp in local:
            cp.wait()

    return start, wait


def _sig(v):
    return jax.nn.sigmoid(v)


def _gelu_parts(v):
    cdf = 0.5 * (1.0 + lax.erf(v * INV_SQRT2))
    pdf = jnp.exp(-0.5 * v * v) * INV_SQRT_2PI
    return v * cdf, cdf + v * pdf


def _ln_stats(v):
    mu = jnp.mean(v, axis=-1, keepdims=True)
    xc = v - mu
    var = jnp.mean(xc * xc, axis=-1, keepdims=True)
    rs = lax.rsqrt(var + EPS)
    return xc * rs, rs


def _ln_bwd(dn, n, rs):
    return rs * (dn - jnp.mean(dn, axis=-1, keepdims=True) - n * jnp.mean(dn * n, axis=-1, keepdims=True))


def _fold8(v):
    acc = v[0:8]
    for r in range(8, v.shape[0], 8):
        acc = acc + v[r:r + 8]
    return acc


def _rows(rb, n=RB):
    return pl.ds(pl.multiple_of(rb * n, n), n)


def _ring_tiles(nt):
    return 4 if nt % 4 == 0 else (2 if nt % 2 == 0 else 1)


def _ring_base(i, nt):
    return pl.multiple_of((i % _ring_tiles(nt)) * TM, TM)


def _ring_rows(base, rb, n=RB):
    return pl.ds(pl.multiple_of(base + rb * n, n), n)


def _ring_full(i, nt):
    g = _ring_tiles(nt)
    return i % g == g - 1


def _ld(ref, rows):
    return ref[rows, :].astype(F32)


def _blocks(n, body, init=0, unroll=UNROLL):
    def trip(t, carry):
        for u in range(unroll):
            carry = body(t * unroll + u, carry)
        return carry
    return lax.fori_loop(0, n // unroll, trip, init)


def _shift_copies(s_ref):
    n = EXT - 8
    for m in range(1, 8):
        for c0 in range(0, n, 56):
            s_ref[m, c0:c0 + 56, :] = s_ref[0, c0 + m:c0 + m + 56, :]


def _ada_exchange(c_ref, w_ref, b_ref, mod_ref, call_ref, cdst, msrc, mdst, c_send, c_recv, m_send, m_recv):
    x, y, c = _me()
    me = _slot(x, y, c)

    def all_to_all(src_of, dst, ss, rs):
        def cp(k, recv):
            px, py, pc = _xor_peer(k)
            peer = _slot(px, py, pc)
            return pltpu.make_async_remote_copy(
                src_ref=src_of(me if recv else peer), dst_ref=dst.at[peer if recv else me],
                send_sem=ss.at[k - 1], recv_sem=rs.at[k - 1], device_id=(px, py, pc), device_id_type=MESH)
        for k in range(1, NDEV):
            cp(k, False).start()
        for k in range(1, NDEV):
            cp(k, True).wait_recv()
        for k in range(1, NDEV):
            cp(k, False).wait_send()

    cdst[me] = c_ref[...]
    all_to_all(lambda s: c_ref, cdst, c_send, c_recv)
    for b in range(NDEV):
        call_ref[b:b + 1, :] = cdst[b]
    m = jnp.dot(call_ref[...], w_ref[...], preferred_element_type=F32, precision=lax.Precision.HIGHEST)
    for b in range(NDEV):
        msrc[b] = m[b:b + 1, :]
    mdst[me] = msrc[me]
    all_to_all(lambda s: msrc.at[s], mdst, m_send, m_recv)
    full = jnp.concatenate([mdst[k] for k in range(NDEV)], axis=1) + b_ref[...]
    for r in range(3):
        mod_ref[r:r + 1, :] = full[:, r * D:(r + 1) * D]


def _ada_fwd(c, w_ada, b_ada):
    wcols = w_ada.shape[1]

    def body(*refs):
        _ada_exchange(*refs)

    vm = pl.BlockSpec(memory_space=VMEM)
    return pl.pallas_call(
        body, name="ada_fwd",
        out_shape=(jax.ShapeDtypeStruct((3, D), F32), jax.ShapeDtypeStruct((NDEV, D), F32)),
        in_specs=[vm, vm, vm], out_specs=(vm, vm),
        scratch_shapes=[VMEM((NDEV, 1, D), F32), VMEM((NDEV, 1, wcols), F32), VMEM((NDEV, 1, wcols), F32),
                        pltpu.SemaphoreType.DMA((7,)), pltpu.SemaphoreType.DMA((7,)),
                        pltpu.SemaphoreType.DMA((7,)), pltpu.SemaphoreType.DMA((7,))],
        compiler_params=_cp(vmem_mb=32),
    )(c, w_ada, b_ada)


def _prep_h(x, mod, g_pre):
    T = x.shape[0]

    def body(x_ref, mod_ref, g_ref, h_ref, ht_ref):
        def blk(rb, carry):
            rows = _rows(rb)
            xv = x_ref[rows, :]
            r = lax.rsqrt(jnp.mean(xv * xv, axis=-1, keepdims=True) + EPS)
            h_ref[rows, :] = ((xv * r) * g_ref[...] * (1.0 + mod_ref[1:2, :]) + mod_ref[0:1, :]).astype(BF16)
            return carry
        _blocks(TMB // RB, blk)
        ht_ref[...] = h_ref[...].T

    return pl.pallas_call(
        body, name="prep_h", grid=(T // TMB,),
        out_shape=(jax.ShapeDtypeStruct((T, D), BF16), jax.ShapeDtypeStruct((D, T), BF16)),
        in_specs=[pl.BlockSpec((TMB, D), lambda i: (i, 0)), pl.BlockSpec((3, D), lambda i: (0, 0)),
                  pl.BlockSpec((1, D), lambda i: (0, 0))],
        out_specs=(pl.BlockSpec((TMB, D), lambda i: (i, 0)), pl.BlockSpec((D, TMB), lambda i: (0, i))),
        compiler_params=_cp(("parallel",)),
    )(x, mod, g_pre)


def _ag_blocks(x, y, c):
    a = 1 - c
    n1 = (x ^ a, y ^ c)
    n2 = (x ^ c, y ^ a)
    dg = (1 - x, 1 - y)
    return [(x, y, c), (x, y, a), (*n1, c), (*n2, c), (*n1, a), (*n2, a), (*dg, c), (*dg, a)]


def _in_proj(h, w_in, order, w_co, w_so, w_o, conv_w):
    T = h.shape[0]
    tmp = min(2 * TMB, T)
    nb = T // tmp

    def body(order_ref, h_ref, w_ref, wco_ref, wso_ref, wo_ref, cw_ref, p_ref, wing_ref, wqg_ref, cwg_ref,
             wbuf, wq_st, send_sems, recv_sems, out_sems, ag_send, ag_recv, ag_loc):
        j = pl.program_id(0)
        i = pl.program_id(1)
        blocks = _ag_blocks(*_me())
        me, sibling, n1, n2, n1o, n2o, dg, dgo = blocks
        ag_start, ag_relay, ag_finish = _allgather_phases(
            [wq_st, cw_ref], lambda a, slot: wqg_ref.at[:, slot] if a == 0 else cwg_ref.at[slot], ag_send, ag_recv, ag_loc)

        def copy(k, block, to):
            d = wbuf.at[_slot(*block)]
            return pltpu.make_async_remote_copy(src_ref=d, dst_ref=d, send_sem=send_sems.at[k], recv_sem=recv_sems.at[k],
                                                device_id=to, device_id_type=MESH)

        def writeback(jj):
            s = _slot(*blocks[jj])
            return pltpu.make_async_copy(wbuf.at[s], wing_ref.at[s], out_sems.at[jj])

        sends = [copy(0, me, sibling), copy(1, me, n1), copy(2, me, n2)]
        relay = copy(3, n1, n2)
        passed = [copy(4, n1, sibling), copy(5, n2, sibling), copy(6, dg, sibling)]
        arrivals = {1: [(0, sibling)], 2: [(1, n1), (2, n2)], 4: [(5, n1o)], 5: [(4, n2o)], 6: [(3, dg)], 7: [(6, dgo)]}

        @pl.when((j == 0) & (i == 0))
        def _():
            wbuf[_slot(*me)] = w_ref[...].astype(BF16)
            for cp in sends:
                cp.start()
            writeback(0).start()

        for jj in range(1, NSEG):
            @pl.when((j == jj) & (i == 0))
            def _(jj=jj):
                for k, block in arrivals.get(jj, []):
                    copy(k, block, me).wait_recv()
                if jj == 2:
                    relay.start()
                    passed[0].start()
                    passed[1].start()
                    writeback(3).start()
                    wq_st[W_CO] = wco_ref[...].astype(BF16)
                    wq_st[W_SO] = wso_ref[...].astype(BF16)
                    wq_st[W_O] = wo_ref[...].astype(BF16)
                    ag_start()
                if jj == 6:
                    passed[2].start()
                if jj != 3:
                    writeback(jj).start()
                if jj == NSEG - 1:
                    ag_relay()

        p_ref[...] = jnp.dot(h_ref[...], wbuf[order_ref[j]], preferred_element_type=F32).astype(BF16)

        @pl.when((j == NSEG - 1) & (i == nb - 1))
        def _():
            for cp in sends + [relay] + passed:
                cp.wait_send()
            for jj in range(NSEG):
                writeback(jj).wait()
            ag_finish()

    vm = pl.BlockSpec(memory_space=VMEM)
    hbm = pl.BlockSpec(memory_space=ANY)
    return pl.pallas_call(
        body, name="in_proj",
        out_shape=(jax.ShapeDtypeStruct((T, NSEG * D), BF16), jax.ShapeDtypeStruct((NSEG, D, D), BF16),
                   jax.ShapeDtypeStruct((3, NDEV, 128, D), BF16), jax.ShapeDtypeStruct((NDEV, KW, 128), F32)),
        grid_spec=pltpu.PrefetchScalarGridSpec(
            num_scalar_prefetch=1, grid=(NSEG, nb),
            in_specs=[pl.BlockSpec((tmp, D), lambda j, i, o: (i, 0)), vm, vm, vm, vm, vm],
            out_specs=(pl.BlockSpec((tmp, D), lambda j, i, o: (i, o[j])), hbm, hbm, hbm),
            scratch_shapes=[VMEM((NSEG, D, D), BF16), VMEM((3, 128, D), BF16),
                            pltpu.SemaphoreType.DMA((7,)), pltpu.SemaphoreType.DMA((7,)), pltpu.SemaphoreType.DMA((NSEG,)),
                            pltpu.SemaphoreType.DMA((2, 7)), pltpu.SemaphoreType.DMA((2, 7)), pltpu.SemaphoreType.DMA((2,))]),
        compiler_params=_cp(("arbitrary", "arbitrary")),
    )(order, h, w_in, w_co, w_so, w_o, conv_w)


def _fill_a_ext(i, s_ref, val_ref, glu_ref, hval_ref, hglu_ref):
    ah = hval_ref[...].astype(F32) * _sig(hglu_ref[...].astype(F32))
    s_ref[0, 0:HALO, :] = jnp.where(i > 0, ah, 0.0)

    def fill(rb, carry):
        rows = _rows(rb)
        s_ref[0, pl.ds(pl.multiple_of(HALO + rb * RB, RB), RB), :] = _ld(val_ref, rows) * _sig(_ld(glu_ref, rows))
        return carry
    lax.fori_loop(0, TM // RB, fill, 0)


def _halo_prev(seg):
    hb = TM // HALO
    return pl.BlockSpec((HALO, D), lambda i: (jnp.maximum(i * hb - 1, 0), seg))


def _branch_a_fwd(p, cw, conv_b, lg, lb, wq):
    T = p.shape[0]

    def body(val_ref, glu_ref, z_ref, hval_ref, hglu_ref, cw_ref, cb_ref, lg_ref, lb_ref, w_ref,
             yc_ref, ya_ref, s_ref, yp_ref):
        i = pl.program_id(0)
        _fill_a_ext(i, s_ref, val_ref, glu_ref, hval_ref, hglu_ref)
        _shift_copies(s_ref)

        def conv(rb, carry):
            r0 = rb * RBC
            accs = [jnp.broadcast_to(cb_ref[...], (8, D))] * (RBC // 8)
            for k in range(KW):
                o = 2 + k
                w = cw_ref[k]
                accs = [acc + w * s_ref[o % 8, pl.ds(pl.multiple_of(r0 + 8 * (o // 8 + g), 8), 8), :]
                        for g, acc in enumerate(accs)]
            for g, acc in enumerate(accs):
                yc_ref[pl.ds(pl.multiple_of(r0 + 8 * g, 8), 8), :] = acc
            return carry
        lax.fori_loop(0, TM // RBC, conv, 0)

        def post(rb, carry):
            rows = _rows(rb)
            n, _ = _ln_stats(yc_ref[rows, :])
            l = n * lg_ref[...] + lb_ref[...]
            z = _ld(z_ref, rows)
            yp_ref[rows, :] = ((l * _sig(l)) * (z * _sig(z))).astype(BF16)
            return carry
        _blocks(TM // RB, post)
        ya_ref[...] = jnp.dot(yp_ref[...], w_ref[...], preferred_element_type=F32)

    tile = lambda seg: pl.BlockSpec((TM, D), lambda i: (i, seg))
    row = pl.BlockSpec((1, D), lambda i: (0, 0))
    return pl.pallas_call(
        body, name="branch_a_fwd", grid=(T // TM,),
        out_shape=(jax.ShapeDtypeStruct((T, D), F32), jax.ShapeDtypeStruct((T, D), F32)),
        in_specs=[tile(0), tile(1), tile(2), _halo_prev(0), _halo_prev(1),
                  pl.BlockSpec((KW, 8, D), lambda i: (0, 0, 0)), row, row, row,
                  pl.BlockSpec((None, D, D), lambda i: (W_CO, 0, 0))],
        out_specs=(pl.BlockSpec((TM, D), lambda i: (i, 0)), pl.BlockSpec((TM, D), lambda i: (i, 0))),
        scratch_shapes=[VMEM((8, EXT, D), F32), VMEM((TM, D), BF16)],
        compiler_params=_cp(("parallel",)),
    )(p, p, p, p, p, cw, conv_b, lg, lb, wq)


def _masked_ws(ws_ref, wt_ref):
    tri = lax.broadcasted_iota(jnp.int32, (CHUNK, CHUNK), 0) >= lax.broadcasted_iota(jnp.int32, (CHUNK, CHUNK), 1)
    for h in range(HEADS):
        wt_ref[h] = jnp.where(tri, ws_ref[h], 0.0).astype(BF16)


def _sgu_mix(wt_ref, vl_ref, bst_ref, s_ref):
    for ck in range(TM // CHUNK):
        r = slice(ck * CHUNK, (ck + 1) * CHUNK)
        for h in range(HEADS):
            cs = slice(h * HD, (h + 1) * HD)
            s_ref[r, cs] = jnp.dot(wt_ref[h], vl_ref[r, cs], preferred_element_type=F32) + bst_ref[:, h:h + 1]


def _branch_b_fwd(p, ws, bst, sg, sb, wq):
    T = p.shape[0]

    def body(pu_ref, pv_ref, pz_ref, ws_ref, bst_ref, sg_ref, sb_ref, w_ref, yb_ref,
             wt_ref, vl_ref, t_ref, s_ref, yp_ref):
        _masked_ws(ws_ref, wt_ref)

        def pre(rb, carry):
            rows = _rows(rb)
            vg, _ = _gelu_parts(_ld(pv_ref, rows))
            vn, _ = _ln_stats(vg)
            vl_ref[rows, :] = (vn * sg_ref[...] + sb_ref[...]).astype(BF16)
            u, _ = _gelu_parts(_ld(pu_ref, rows))
            z = _ld(pz_ref, rows)
            t_ref[rows, :] = u * (z * _sig(z))
            return carry
        _blocks(TM // RB, pre)
        _sgu_mix(wt_ref, vl_ref, bst_ref, s_ref)

        def post(rb, carry):
            rows = _rows(rb)
            yp_ref[rows, :] = (t_ref[rows, :] * s_ref[rows, :]).astype(BF16)
            return carry
        _blocks(TM // RB, post)
        yb_ref[...] = jnp.dot(yp_ref[...], w_ref[...], preferred_element_type=F32)

    tile = lambda seg: pl.BlockSpec((TM, D), lambda i: (i, seg))
    row = pl.BlockSpec((1, D), lambda i: (0, 0))
    return pl.pallas_call(
        body, name="branch_b_fwd", grid=(T // TM,),
        out_shape=jax.ShapeDtypeStruct((T, D), F32),
        in_specs=[tile(3), tile(4), tile(5), pl.BlockSpec((HEADS, CHUNK, CHUNK), lambda i: (0, 0, 0)),
                  pl.BlockSpec((CHUNK, HEADS), lambda i: (0, 0)), row, row,
                  pl.BlockSpec((None, D, D), lambda i: (W_SO, 0, 0))],
        out_specs=pl.BlockSpec((TM, D), lambda i: (i, 0)),
        scratch_shapes=[VMEM((HEADS, CHUNK, CHUNK), BF16), VMEM((TM, D), BF16), VMEM((TM, D), F32),
                        VMEM((TM, D), F32), VMEM((TM, D), BF16)],
        compiler_params=_cp(("parallel",)),
    )(p, p, p, ws, bst, sg, sb, wq)


def _merge_loss(p, ya, yb, x, tgt, mod, g_final, wq):
    T = x.shape[0]
    nt = T // TM

    def body(ga_ref, gb_ref, ya_ref, yb_ref, x_ref, t_ref, mod_ref, gf_ref, w_ref,
             dx2_ref, dya_ref, dyb_ref, dp_ref, gwo_ref, st_ref,
             mrg_s, out_s, dout_s, dm_s, gw_acc):
        i = pl.program_id(0)
        base = _ring_base(i, nt)
        cur = pl.ds(base, TM)

        @pl.when(i == 0)
        def _():
            gw_acc[...] = jnp.zeros_like(gw_acc)
            st_ref[...] = jnp.zeros_like(st_ref)

        def merge(rb, carry):
            rows = _rows(rb)
            mrg_s[_ring_rows(base, rb), :] = (_sig(_ld(ga_ref, rows)) * ya_ref[rows, :]
                                              + _sig(_ld(gb_ref, rows)) * yb_ref[rows, :]).astype(BF16)
            return carry
        lax.fori_loop(0, TM // RB, merge, 0)
        out_s[...] = jnp.dot(mrg_s[cur, :], w_ref[...], preferred_element_type=F32)

        def head(rb, carry):
            loss, gg, dg = carry
            rows = _rows(rb)
            gate = mod_ref[2:3, :]
            gf = gf_ref[...]
            out = out_s[rows, :]
            x2 = x_ref[rows, :] + gate * out
            r2 = lax.rsqrt(jnp.mean(x2 * x2, axis=-1, keepdims=True) + EPS)
            x2n = x2 * r2
            diff = x2n * gf - t_ref[rows, :]
            dy = diff * (1.0 / D)
            dx2n = dy * gf
            dx2 = r2 * (dx2n - x2n * jnp.mean(dx2n * x2n, axis=-1, keepdims=True))
            dx2_ref[rows, :] = dx2
            dout_s[_ring_rows(base, rb), :] = (dx2 * gate).astype(BF16)
            return loss + _fold8(diff * diff), gg + _fold8(dy * x2n), dg + _fold8(dx2 * out)
        zero = jnp.zeros((8, D), F32)
        loss, gg, dg = _blocks(TM // RB, head, (zero, zero, zero))
        st_ref[0] = st_ref[0] + loss * (0.5 / D)
        st_ref[1] = st_ref[1] + gg
        st_ref[2] = st_ref[2] + dg

        dm_s[...] = lax.dot_general(dout_s[cur, :], w_ref[...], NT, preferred_element_type=F32)

        @pl.when(_ring_full(i, nt))
        def _():
            gw_acc[...] += lax.dot_general(mrg_s[...], dout_s[...], TN, preferred_element_type=F32)

        def split(rb, carry):
            rows = _rows(rb)
            dm = dm_s[rows, :]
            sa = _sig(_ld(ga_ref, rows))
            sb = _sig(_ld(gb_ref, rows))
            dya_ref[rows, :] = (dm * sa).astype(BF16)
            dyb_ref[rows, :] = (dm * sb).astype(BF16)
            dp_ref[0, rows, :] = (dm * ya_ref[rows, :] * (sa * (1.0 - sa))).astype(BF16)
            dp_ref[1, rows, :] = (dm * yb_ref[rows, :] * (sb * (1.0 - sb))).astype(BF16)
            return carry
        lax.fori_loop(0, TM // RB, split, 0)

        @pl.when(i == nt - 1)
        def _():
            gwo_ref[...] = gw_acc[...].astype(BF16)

    tile = pl.BlockSpec((TM, D), lambda i: (i, 0))
    ring = VMEM((_ring_tiles(nt) * TM, D), BF16)
    return pl.pallas_call(
        body, name="merge_loss", grid=(nt,),
        out_shape=(jax.ShapeDtypeStruct((T, D), F32), jax.ShapeDtypeStruct((T, D), BF16), jax.ShapeDtypeStruct((T, D), BF16),
                   jax.ShapeDtypeStruct((NSEG, T, D), BF16), jax.ShapeDtypeStruct((D, D), BF16),
                   jax.ShapeDtypeStruct((3, 8, D), F32)),
        in_specs=[pl.BlockSpec((TM, D), lambda i: (i, 6)), pl.BlockSpec((TM, D), lambda i: (i, 7)), tile, tile, tile, tile,
                  pl.BlockSpec((3, D), lambda i: (0, 0)), pl.BlockSpec((1, D), lambda i: (0, 0)),
                  pl.BlockSpec((None, D, D), lambda i: (W_O, 0, 0))],
        out_specs=(tile, tile, tile, pl.BlockSpec((2, TM, D), lambda i: (3, i, 0)),
                   pl.BlockSpec((D, D), lambda i: (0, 0)), pl.BlockSpec((3, 8, D), lambda i: (0, 0, 0))),
        scratch_shapes=[ring, VMEM((TM, D), F32), ring, VMEM((TM, D), F32), VMEM((D, D), F32)],
        compiler_params=_cp(("arbitrary",)),
    )(p, p, ya, yb, x, tgt, mod, g_final, wq)


def _branch_b_bwd(dp, dyb, p, ws, bst, sg, sb, wq):
    T = p.shape[0]
    nt = T // TM

    def body(dp_in, dyb_ref, pu_ref, pv_ref, pz_ref, ws_ref, bst_ref, sg_ref, sb_ref, w_ref,
             dp_ref, gw_ref, gws_ref, gbs_ref, st_ref,
             wt_ref, d_s, vl_s, vn_s, gpv_s, rs_s, s_s, ds_s, ds32_s, yp_s, dy_s, dvl_s, gw_acc):
        del dp_in
        i = pl.program_id(0)
        base = _ring_base(i, nt)

        @pl.when(i == 0)
        def _():
            gw_acc[...] = jnp.zeros_like(gw_acc)
            gws_ref[...] = jnp.zeros_like(gws_ref)
            gbs_ref[...] = jnp.zeros_like(gbs_ref)
            st_ref[...] = jnp.zeros_like(st_ref)
            _masked_ws(ws_ref, wt_ref)

        d_s[...] = lax.dot_general(dyb_ref[...], w_ref[...], NT, preferred_element_type=F32)
        dy_s[pl.ds(base, TM), :] = dyb_ref[...]

        def pre(rb, carry):
            rows = _rows(rb)
            vg, gpv = _gelu_parts(_ld(pv_ref, rows))
            vn, rs = _ln_stats(vg)
            vl_s[rows, :] = (vn * sg_ref[...] + sb_ref[...]).astype(BF16)
            vn_s[rows, :] = vn
            gpv_s[rows, :] = gpv
            rs_s[rows, :] = rs
            return carry
        _blocks(TM // RB, pre)
        _sgu_mix(wt_ref, vl_s, bst_ref, s_s)

        def mid(rb, carry):
            rows = _rows(rb)
            u, gpu = _gelu_parts(_ld(pu_ref, rows))
            z = _ld(pz_ref, rows)
            sz = _sig(z)
            siluz = z * sz
            d = d_s[rows, :]
            s = s_s[rows, :]
            t = u * siluz
            yp_s[_ring_rows(base, rb), :] = (t * s).astype(BF16)
            dp_ref[0, rows, :] = (d * s * siluz * gpu).astype(BF16)
            ds = d * t
            ds32_s[rows, :] = ds
            ds_s[rows, :] = ds.astype(BF16)
            dp_ref[2, rows, :] = (d * u * s * (sz * (1.0 + z * (1.0 - sz)))).astype(BF16)
            return carry
        lax.fori_loop(0, TM // RB, mid, 0)

        for ck in range(TM // CHUNK):
            r = slice(ck * CHUNK, (ck + 1) * CHUNK)
            for h in range(HEADS):
                cs = slice(h * HD, (h + 1) * HD)
                dsh = ds_s[r, cs]
                dvl_s[r, cs] = lax.dot_general(wt_ref[h], dsh, TN, preferred_element_type=F32)
                gws_ref[h] += lax.dot_general(dsh, vl_s[r, cs], NT, preferred_element_type=F32)
                gbs_ref[h] += ds32_s[r, cs]

        def post(rb, carry):
            g_sg, g_sb = carry
            rows = _rows(rb)
            dvl = dvl_s[rows, :]
            vn = vn_s[rows, :]
            dvg = _ln_bwd(dvl * sg_ref[...], vn, rs_s[rows, :])
            dp_ref[1, rows, :] = (dvg * gpv_s[rows, :]).astype(BF16)
            return g_sg + _fold8(dvl * vn), g_sb + _fold8(dvl)
        zero = jnp.zeros((8, D), F32)
        g_sg, g_sb = _blocks(TM // RB, post, (zero, zero))
        st_ref[0] = st_ref[0] + g_sg
        st_ref[1] = st_ref[1] + g_sb

        @pl.when(_ring_full(i, nt))
        def _():
            gw_acc[...] += lax.dot_general(yp_s[...], dy_s[...], TN, preferred_element_type=F32)

        @pl.when(i == nt - 1)
        def _():
            gw_ref[...] = gw_acc[...].astype(BF16)

    tile = lambda seg: pl.BlockSpec((TM, D), lambda i: (i, seg))
    row = pl.BlockSpec((1, D), lambda i: (0, 0))
    hh = pl.BlockSpec((HEADS, CHUNK, CHUNK), lambda i: (0, 0, 0))
    ring = VMEM((_ring_tiles(nt) * TM, D), BF16)
    return pl.pallas_call(
        body, name="branch_b_bwd", grid=(nt,),
        out_shape=(jax.ShapeDtypeStruct((NSEG, T, D), BF16), jax.ShapeDtypeStruct((D, D), BF16),
                   jax.ShapeDtypeStruct((HEADS, CHUNK, CHUNK), F32), jax.ShapeDtypeStruct((HEADS, CHUNK, HD), F32),
                   jax.ShapeDtypeStruct((2, 8, D), F32)),
        in_specs=[pl.BlockSpec(memory_space=ANY), pl.BlockSpec((TM, D), lambda i: (i, 0)), tile(3), tile(4), tile(5),
                  hh, pl.BlockSpec((CHUNK, HEADS), lambda i: (0, 0)), row, row,
                  pl.BlockSpec((None, D, D), lambda i: (W_SO, 0, 0))],
        out_specs=(pl.BlockSpec((3, TM, D), lambda i: (1, i, 0)), pl.BlockSpec((D, D), lambda i: (0, 0)), hh, hh,
                   pl.BlockSpec((2, 8, D), lambda i: (0, 0, 0))),
        scratch_shapes=[VMEM((HEADS, CHUNK, CHUNK), BF16), VMEM((TM, D), F32), VMEM((TM, D), BF16), VMEM((TM, D), F32),
                        VMEM((TM, D), F32), VMEM((TM, 1), F32), VMEM((TM, D), F32), VMEM((TM, D), BF16), VMEM((TM, D), F32),
                        ring, ring, VMEM((TM, D), F32), VMEM((D, D), F32)],
        input_output_aliases={0: 0},
        compiler_params=_cp(("arbitrary",)),
    )(dp, dyb, p, p, p, ws, bst, sg, sb, wq)


def _branch_a_bwd1(dp, dya, yc, p, lg, lb, wq):
    T = p.shape[0]
    nt = T // TM

    def body(dp_in, dya_ref, yc_ref, z_ref, lg_ref, lb_ref, w_ref, dp_ref, dyc_ref, gw_ref, st_ref,
             d_s, yp_s, dy_s, gw_acc):
        del dp_in
        i = pl.program_id(0)
        base = _ring_base(i, nt)

        @pl.when(i == 0)
        def _():
            gw_acc[...] = jnp.zeros_like(gw_acc)
            st_ref[...] = jnp.zeros_like(st_ref)

        d_s[...] = lax.dot_general(dya_ref[...], w_ref[...], NT, preferred_element_type=F32)
        dy_s[pl.ds(base, TM), :] = dya_ref[...]

        def blk(rb, carry):
            g_lg, g_lb, g_cb = carry
            rows = _rows(rb)
            n, rs = _ln_stats(yc_ref[rows, :])
            l = n * lg_ref[...] + lb_ref[...]
            sgl = _sig(l)
            sl = l * sgl
            z = _ld(z_ref, rows)
            sz = _sig(z)
            siluz = z * sz
            d = d_s[rows, :]
            yp_s[_ring_rows(base, rb), :] = (sl * siluz).astype(BF16)
            dp_ref[rows, :] = (d * sl * (sz * (1.0 + z * (1.0 - sz)))).astype(BF16)
            dl = (d * siluz) * (sgl * (1.0 + l * (1.0 - sgl)))
            dyc = _ln_bwd(dl * lg_ref[...], n, rs)
            dyc_ref[rows, :] = dyc
            return g_lg + _fold8(dl * n), g_lb + _fold8(dl), g_cb + _fold8(dyc)
        zero = jnp.zeros((8, D), F32)
        g_lg, g_lb, g_cb = _blocks(TM // RB, blk, (zero, zero, zero))
        st_ref[0] = st_ref[0] + g_lg
        st_ref[1] = st_ref[1] + g_lb
        st_ref[2] = st_ref[2] + g_cb

        @pl.when(_ring_full(i, nt))
        def _():
            gw_acc[...] += lax.dot_general(yp_s[...], dy_s[...], TN, preferred_element_type=F32)

        @pl.when(i == nt - 1)
        def _():
            gw_ref[...] = gw_acc[...].astype(BF16)

    tile = pl.BlockSpec((TM, D), lambda i: (i, 0))
    row = pl.BlockSpec((1, D), lambda i: (0, 0))
    ring = VMEM((_ring_tiles(nt) * TM, D), BF16)
    return pl.pallas_call(
        body, name="branch_a_bwd1", grid=(nt,),
        out_shape=(jax.ShapeDtypeStruct((NSEG, T, D), BF16), jax.ShapeDtypeStruct((T, D), F32),
                   jax.ShapeDtypeStruct((D, D), BF16), jax.ShapeDtypeStruct((3, 8, D), F32)),
        in_specs=[pl.BlockSpec(memory_space=ANY), tile, tile, pl.BlockSpec((TM, D), lambda i: (i, 2)), row, row,
                  pl.BlockSpec((None, D, D), lambda i: (0, 0, 0))],
        out_specs=(pl.BlockSpec((None, TM, D), lambda i: (2, i, 0)), tile, pl.BlockSpec((D, D), lambda i: (0, 0)),
                   pl.BlockSpec((3, 8, D), lambda i: (0, 0, 0))),
        scratch_shapes=[VMEM((TM, D), F32), ring, ring, VMEM((D, D), F32)],
        input_output_aliases={0: 0},
        compiler_params=_cp(("arbitrary",)),
    )(dp, dya, yc, p, lg, lb, wq)


def _branch_a_bwd2(dp, dyc, p, cw, gw_co, gw_so, gw_o, early):
    T = p.shape[0]
    nt = T // TM
    hb = TM // HALO

    def body(dp_in, dyc_ref, hdyc_ref, val_ref, glu_ref, cw_ref, gco_ref, gso_ref, go_ref, e_ref,
             dp_ref, gcw_ref, rq_ref, eg_ref, a_s, sd_ref, da_s, x_send, x_recv, x_loc, ag_send, ag_recv, ag_loc):
        del dp_in
        i = pl.program_id(0)
        start_exchange, wait_exchange = _owner_exchange((gco_ref, gso_ref, go_ref), rq_ref, x_send, x_recv, x_loc)
        ag_start, ag_relay, ag_finish = _allgather_phases([e_ref], lambda a, slot: eg_ref.at[slot],
                                                          ag_send, ag_recv, ag_loc)

        @pl.when(i == 0)
        def _():
            gcw_ref[...] = jnp.zeros_like(gcw_ref)
            ag_start()
            start_exchange()

        @pl.when(i == nt // 2)
        def _():
            ag_relay()

        def fill(rb, carry):
            rows = _rows(rb)
            a_s[rows, :] = _ld(val_ref, rows) * _sig(_ld(glu_ref, rows))
            return carry
        lax.fori_loop(0, TM // RB, fill, 0)
        sd_ref[0, 0:TM, :] = dyc_ref[...]
        sd_ref[0, TM:EXT, :] = jnp.where(i < nt - 1, hdyc_ref[...], 0.0)
        _shift_copies(sd_ref)

        def conv_t(rb, carry):
            r0 = rb * RBC
            accs = [jnp.zeros((8, D), F32)] * (RBC // 8)
            for k in range(KW):
                o = KW - 1 - k
                w = cw_ref[k]
                accs = [acc + w * sd_ref[o % 8, pl.ds(pl.multiple_of(r0 + 8 * (o // 8 + g), 8), 8), :]
                        for g, acc in enumerate(accs)]
            for g, acc in enumerate(accs):
                da_s[pl.ds(pl.multiple_of(r0 + 8 * g, 8), 8), :] = acc
            return carry
        lax.fori_loop(0, TM // RBC, conv_t, 0)

        for k0 in range(0, KW, TAPG):
            taps = list(range(k0, min(k0 + TAPG, KW)))

            def tap_group(rb, accs, taps=taps):
                for u in range(2):
                    r0 = rb * 32 + 16 * u
                    a = a_s[pl.ds(pl.multiple_of(r0, 16), 16), :]
                    out = []
                    for k, acc in zip(taps, accs):
                        o = KW - 1 - k
                        d = sd_ref[o % 8, pl.ds(pl.multiple_of(r0 + 8 * (o // 8), 8), 16), :]
                        out.append(acc + _fold8(a * d))
                    accs = tuple(out)
                return accs
            sums = lax.fori_loop(0, TM // 32, tap_group, tuple(jnp.zeros((8, D), F32) for _ in taps))
            for k, s in zip(taps, sums):
                gcw_ref[k] = gcw_ref[k] + s

        def glu_b(rb, carry):
            rows = _rows(rb)
            da = da_s[rows, :]
            sg = _sig(_ld(glu_ref, rows))
            dp_ref[0, rows, :] = (da * sg).astype(BF16)
            dp_ref[1, rows, :] = (da * _ld(val_ref, rows) * (sg * (1.0 - sg))).astype(BF16)
            return carry
        lax.fori_loop(0, TM // RB, glu_b, 0)

        @pl.when(i == nt - 1)
        def _():
            wait_exchange()
            ag_finish()

    tile = lambda seg: pl.BlockSpec((TM, D), lambda i: (i, seg))
    hbm = pl.BlockSpec(memory_space=ANY)
    return pl.pallas_call(
        body, name="branch_a_bwd2", grid=(nt,),
        out_shape=(jax.ShapeDtypeStruct((NSEG, T, D), BF16), jax.ShapeDtypeStruct((32, 8, D), F32),
                   jax.ShapeDtypeStruct((NDEV, 3, 128, D), BF16), jax.ShapeDtypeStruct((NDEV, NEARLY, D), F32)),
        in_specs=[hbm, pl.BlockSpec((TM, D), lambda i: (i, 0)),
                  pl.BlockSpec((HALO, D), lambda i: (jnp.minimum((i + 1) * hb, nt * hb - 1), 0)),
                  tile(0), tile(1), pl.BlockSpec((KW, 8, D), lambda i: (0, 0, 0)),
                  hbm, hbm, hbm, hbm],
        out_specs=(pl.BlockSpec((2, TM, D), lambda i: (0, i, 0)), pl.BlockSpec((32, 8, D), lambda i: (0, 0, 0)), hbm, hbm),
        scratch_shapes=[VMEM((TM, D), F32), VMEM((8, EXT, D), F32), VMEM((TM, D), F32),
                        pltpu.SemaphoreType.DMA((3, 7)), pltpu.SemaphoreType.DMA((3, 7)), pltpu.SemaphoreType.DMA((3,)),
                        pltpu.SemaphoreType.DMA((1, 7)), pltpu.SemaphoreType.DMA((1, 7)), pltpu.SemaphoreType.DMA((1,))],
        input_output_aliases={0: 0},
        compiler_params=_cp(("arbitrary",)),
    )(dp, dyc, dyc, p, p, cw, gw_co, gw_so, gw_o, early)


def _in_bwd_dx(dp, wing, x, dx2, mod, g_pre, token):
    T = x.shape[0]
    nt = T // TM

    def body(dp_ref, w_ref, x_ref, dx2_ref, mod_ref, g_ref, token_ref, gx_ref, st_ref, dh_s):
        del token_ref
        i = pl.program_id(0)

        @pl.when(i == 0)
        def _():
            st_ref[...] = jnp.zeros_like(st_ref)

        dh_s[...] = lax.dot_general(dp_ref[0], w_ref[0], NT, preferred_element_type=F32)
        for j in range(1, NSEG):
            dh_s[...] += lax.dot_general(dp_ref[j], w_ref[j], NT, preferred_element_type=F32)

        def blk(rb, carry):
            d_sh, d_sc, g_g = carry
            rows = _rows(rb)
            xv = x_ref[rows, :]
            r = lax.rsqrt(jnp.mean(xv * xv, axis=-1, keepdims=True) + EPS)
            xn = xv * r
            g = g_ref[...]
            hpre = xn * g
            dh = dh_s[rows, :]
            dhp = dh * (1.0 + mod_ref[1:2, :])
            dxn = dhp * g
            gx_ref[rows, :] = dx2_ref[rows, :] + r * (dxn - xn * jnp.mean(dxn * xn, axis=-1, keepdims=True))
            return d_sh + _fold8(dh), d_sc + _fold8(dh * hpre), g_g + _fold8(dhp * xn)
        zero = jnp.zeros((8, D), F32)
        d_sh, d_sc, g_g = _blocks(TM // RB, blk, (zero, zero, zero))
        st_ref[0] = st_ref[0] + d_sh
        st_ref[1] = st_ref[1] + d_sc
        st_ref[2] = st_ref[2] + g_g

    tile = pl.BlockSpec((TM, D), lambda i: (i, 0))
    return pl.pallas_call(
        body, name="in_bwd_dx", grid=(nt,),
        out_shape=(jax.ShapeDtypeStruct((T, D), F32), jax.ShapeDtypeStruct((3, 8, D), F32)),
        in_specs=[pl.BlockSpec((NSEG, TM, D), lambda i: (0, i, 0)),
                  pl.BlockSpec((NSEG, D, D), lambda i: (0, 0, 0), pipeline_mode=pl.Buffered(1)),
                  tile, tile, pl.BlockSpec((3, D), lambda i: (0, 0)), pl.BlockSpec((1, D), lambda i: (0, 0)),
                  pl.BlockSpec((8, 128), lambda i: (0, 0))],
        out_specs=(tile, pl.BlockSpec((3, 8, D), lambda i: (0, 0, 0))),
        scratch_shapes=[VMEM((TM, D), F32)],
        compiler_params=_cp(("arbitrary",), vmem_mb=56),
    )(dp, wing, x, dx2, mod, g_pre, token)


RS_ORDER = (5, 4, 3, 2, 7, 6, 1, 0)
RS_HALF = D // 2


def _in_bwd_dw(ht, dp, order):
    T = ht.shape[1]
    tmw = min(4 * TMB, T)
    nb = T // tmw
    nu = 2 * NSEG

    def body(order_ref, ht_ref, dp_ref, csum_ref, g_ref, acc, sbuf, rsib, send_sems, recv_sems, out_sems):
        j = pl.program_id(0)
        i = pl.program_id(1)
        x, y, c = _me()
        sibling = (x, y, 1 - c)

        def to_sibling(s):
            return pltpu.make_async_remote_copy(src_ref=sbuf.at[s], dst_ref=rsib.at[s], send_sem=send_sems.at[s],
                                                recv_sem=recv_sems.at[s], device_id=sibling, device_id_type=MESH)

        def to_csum(t):
            cols = pl.ds((t % 2) * RS_HALF, RS_HALF)
            return pltpu.make_async_copy(sbuf.at[8 + t], csum_ref.at[t // 2, :, cols], out_sems.at[t])

        @pl.when(i == 0)
        def _():
            acc[...] = jnp.zeros_like(acc)
        acc[...] += jnp.dot(ht_ref[...], dp_ref[...], preferred_element_type=F32)

        for jj in range(nu):
            chip, half, mine = jj // 4, (jj % 4) // 2, jj % 2
            s = 2 * chip + half

            @pl.when((j == jj) & (i == nb - 1))
            def _(chip=chip, half=half, mine=mine, s=s):
                if not mine:
                    sbuf[s] = acc[...].astype(BF16)
                    to_sibling(s).start()
                elif chip < 3:
                    to_sibling(s).wait_recv()
                    sbuf[8 + s] = (acc[...] + rsib[s].astype(F32)).astype(BF16)
                    to_csum(s).start()
                else:
                    to_sibling(s).wait_recv()
                    g_ref[:, half * RS_HALF:(half + 1) * RS_HALF] = acc[...] + rsib[s].astype(F32)

        @pl.when((j == nu - 1) & (i == nb - 1))
        def _():
            for s in range(8):
                to_sibling(s).wait_send()
            for t in range(6):
                to_csum(t).wait()

    hbm = pl.BlockSpec(memory_space=ANY)
    return pl.pallas_call(
        body, name="in_bwd_dw",
        out_shape=(jax.ShapeDtypeStruct((3, D, D), BF16), jax.ShapeDtypeStruct((D, D), F32)),
        grid_spec=pltpu.PrefetchScalarGridSpec(
            num_scalar_prefetch=1, grid=(nu, nb),
            in_specs=[pl.BlockSpec((D, tmw), lambda j, i, o: (0, i)),
                      pl.BlockSpec((None, tmw, RS_HALF), lambda j, i, o: (o[2 * (j // 4) + j % 2], i, (j % 4) // 2))],
            out_specs=(hbm, pl.BlockSpec((D, D), lambda j, i, o: (0, 0))),
            scratch_shapes=[VMEM((D, RS_HALF), F32), VMEM((14, D, RS_HALF), BF16), VMEM((8, D, RS_HALF), BF16),
                            pltpu.SemaphoreType.DMA((8,)), pltpu.SemaphoreType.DMA((8,)), pltpu.SemaphoreType.DMA((6,))]),
        compiler_params=_cp(("arbitrary", "arbitrary"), vmem_mb=56),
    )(order, ht, dp)


def _gather_late(late):
    def body(sm_ref, smg_ref, ag_send, ag_recv, ag_loc):
        _allgather([sm_ref], lambda a, slot: smg_ref.at[slot], ag_send, ag_recv, ag_loc)

    hbm = pl.BlockSpec(memory_space=ANY)
    return pl.pallas_call(
        body, name="gather_late",
        out_shape=jax.ShapeDtypeStruct((NDEV, NLATE, D), F32),
        in_specs=[hbm], out_specs=hbm,
        scratch_shapes=[pltpu.SemaphoreType.DMA((1, 7)), pltpu.SemaphoreType.DMA((1, 7)), pltpu.SemaphoreType.DMA((1,))],
        compiler_params=_cp(vmem_mb=32),
    )(late)


RS_CHIPS = ((1, 0), (0, 1), (1, 1))


def _rs_peer(q):
    x, y, c = _me()
    fx, fy = RS_CHIPS[q]
    return (1 - x if fx else x, 1 - y if fy else y, c)


def _rs_start(csum):
    def body(csum_ref, land_ref, send_sems, recv_sems, csum_thru, land_thru, token):
        barrier = pltpu.get_barrier_semaphore()
        for q in range(3):
            pl.semaphore_signal(barrier, inc=1, device_id=_rs_peer(q), device_id_type=MESH)
        pl.semaphore_wait(barrier, 3)
        for q in range(3):
            pltpu.make_async_remote_copy(src_ref=csum_ref.at[q], dst_ref=land_ref.at[q], send_sem=send_sems.at[q],
                                         recv_sem=recv_sems.at[q], device_id=_rs_peer(q), device_id_type=MESH).start()
        token[...] = jnp.zeros_like(token)

    hbm = pl.BlockSpec(memory_space=pltpu.HBM)
    sem = pl.BlockSpec(memory_space=pltpu.SEMAPHORE)
    land = pltpu.with_memory_space_constraint(lax.empty(csum.shape, csum.dtype), pltpu.HBM)
    return pl.pallas_call(
        body, name="rs_start",
        out_shape=(pltpu.SemaphoreType.DMA((3,)), pltpu.SemaphoreType.DMA((3,)), pltpu.HBM(csum.shape, csum.dtype),
                   pltpu.HBM(csum.shape, csum.dtype), jax.ShapeDtypeStruct((8, 128), F32)),
        in_specs=(hbm, hbm), out_specs=(sem, sem, hbm, hbm, pl.BlockSpec(memory_space=VMEM)),
        input_output_aliases={0: 2, 1: 3},
        compiler_params=pltpu.CompilerParams(has_side_effects=pltpu.SideEffectType.DATAFLOW_SIDE_EFFECTING, collective_id=1),
    )(pltpu.with_memory_space_constraint(csum, pltpu.HBM), land)


def _rs_wait(send_sems, recv_sems, csum_thru, land_thru, after):
    def body(csum_ref, land_ref, send_sems, recv_sems, after_ref, csum_dead, got_ref):
        del after_ref, csum_dead, got_ref
        for q in range(3):
            cp = pltpu.make_async_remote_copy(src_ref=csum_ref.at[q], dst_ref=land_ref.at[q], send_sem=send_sems.at[q],
                                              recv_sem=recv_sems.at[q], device_id=_rs_peer(q), device_id_type=MESH)
            cp.wait_send()
            cp.wait_recv()

    hbm = pl.BlockSpec(memory_space=pltpu.HBM)
    sem = pl.BlockSpec(memory_space=pltpu.SEMAPHORE)
    return pl.pallas_call(
        body, name="rs_wait",
        out_shape=(pltpu.HBM(csum_thru.shape, csum_thru.dtype), pltpu.HBM(csum_thru.shape, csum_thru.dtype)),
        in_specs=(hbm, hbm, sem, sem, pl.BlockSpec(memory_space=ANY)), out_specs=(hbm, hbm),
        input_output_aliases={0: 0, 1: 1},
        compiler_params=pltpu.CompilerParams(has_side_effects=pltpu.SideEffectType.DATAFLOW_SIDE_EFFECTING),
    )(csum_thru, land_thru, send_sems, recv_sems, after)[1]


def _pack_early(st_a, st_b, st_m, gws, gbs):
    def body(sa_ref, sb_ref, sm_ref, gws_ref, gbs_ref, out_ref):
        out_ref[...] = jnp.zeros_like(out_ref)
        fold = lambda v: jnp.sum(v, axis=0, keepdims=True)
        out_ref[0:1, :] = fold(sm_ref[2])
        out_ref[1:2, :] = fold(sa_ref[2])
        out_ref[2:3, :] = fold(sa_ref[0])
        out_ref[3:4, :] = fold(sa_ref[1])
        out_ref[4:5, :] = fold(sb_ref[0])
        out_ref[5:6, :] = fold(sb_ref[1])
        out_ref[7:8, :] = fold(sm_ref[1])
        out_ref[8:9, :] = fold(sm_ref[0])
        tri = lax.broadcasted_iota(jnp.int32, (CHUNK, CHUNK), 0) >= lax.broadcasted_iota(jnp.int32, (CHUNK, CHUNK), 1)
        for h in range(HEADS):
            out_ref[16:16 + CHUNK, h * CHUNK:(h + 1) * CHUNK] = jnp.where(tri, gws_ref[h], 0.0)
            out_ref[6:7, h * CHUNK:(h + 1) * CHUNK] = fold(gbs_ref[h].T)

    vm = pl.BlockSpec(memory_space=VMEM)
    return pl.pallas_call(
        body, name="pack_early", out_shape=jax.ShapeDtypeStruct((NEARLY, D), F32),
        in_specs=[vm] * 5, out_specs=vm, compiler_params=_cp(vmem_mb=32),
    )(st_a, st_b, st_m, gws, gbs)


def _pack_late(st_x, gcw):
    def body(sx_ref, gcw_ref, out_ref):
        out_ref[...] = jnp.zeros_like(out_ref)
        fold = lambda v: jnp.sum(v, axis=0, keepdims=True)
        for r in range(3):
            out_ref[r:r + 1, :] = fold(sx_ref[r])
        for k in range(KW):
            out_ref[8 + k:9 + k, :] = fold(gcw_ref[k])

    vm = pl.BlockSpec(memory_space=VMEM)
    return pl.pallas_call(
        body, name="pack_late", out_shape=jax.ShapeDtypeStruct((NLATE, D), F32),
        in_specs=[vm] * 2, out_specs=vm, compiler_params=_cp(vmem_mb=32),
    )(st_x, gcw)


def _pack_params(name, b_ada, g_pre, conv_b, lg, lb, sg, sb, b_sgu, g_final, w_sgu_t):
    def body(ba_ref, *refs):
        rows, ws_ref, out_ref = refs[:8], refs[8], refs[9]
        out_ref[...] = jnp.zeros_like(out_ref)
        for r in range(3):
            out_ref[r:r + 1, :] = ba_ref[:, r * D:(r + 1) * D]
        for r, ref in enumerate(rows):
            out_ref[3 + r:4 + r, :] = ref[...]
        out_ref[ROW_WS:ROW_WS + CHUNK, :] = ws_ref[...]

    vm = pl.BlockSpec(memory_space=VMEM)
    return pl.pallas_call(
        body, name=name, out_shape=jax.ShapeDtypeStruct((NSMALL, D), F32),
        in_specs=[vm] * 10, out_specs=vm, compiler_params=_cp(vmem_mb=32),
    )(b_ada, g_pre, conv_b, lg, lb, sg, sb, b_sgu, g_final, w_sgu_t)


def _adam(w, g, m, v):
    m2 = ADAM_B1 * m + (1.0 - ADAM_B1) * g
    v2 = ADAM_B2 * v + (1.0 - ADAM_B2) * (g * g)
    m_hat = m2 / (1.0 - ADAM_B1 ** ADAM_STEP)
    v_hat = v2 / (1.0 - ADAM_B2 ** ADAM_STEP)
    delta = -ADAM_LR * (m_hat / (jnp.sqrt(v_hat) + ADAM_EPS) + ADAM_WD * w)
    return delta, m2, v2


def _small_finish(eg, lg, w, m, v):
    def body(eg_ref, lg_ref, w_ref, m_ref, v_ref, loss_ref, g_ref, d_ref, m2_ref, v2_ref):
        e = eg_ref[0]
        l = lg_ref[0]
        for s in range(1, NDEV):
            e = e + eg_ref[s]
            l = l + lg_ref[s]
        g_ref[...] = jnp.zeros_like(g_ref)
        g_ref[0:2, :] = l[0:2, :]
        g_ref[2:3, :] = e[0:1, :]
        g_ref[3:4, :] = l[2:3, :]
        g_ref[4:12, :] = e[1:9, :]
        g_ref[ROW_CW:ROW_CW + KW, :] = l[8:8 + KW, :]
        g_ref[ROW_WS:ROW_WS + CHUNK, :] = e[16:16 + CHUNK, :]
        loss_ref[...] = jnp.sum(e[8:9, :], axis=1, keepdims=True)
        d_ref[...], m2_ref[...], v2_ref[...] = _adam(w_ref[...], g_ref[...], m_ref[...], v_ref[...])

    vm = pl.BlockSpec(memory_space=VMEM)
    sd = jax.ShapeDtypeStruct((NSMALL, D), F32)
    return pl.pallas_call(
        body, name="small_finish", out_shape=(jax.ShapeDtypeStruct((1, 1), F32), sd, sd, sd, sd),
        in_specs=[vm] * 5, out_specs=(vm,) * 5, compiler_params=_cp(vmem_mb=32),
    )(eg, lg, w, m, v)


def _ada_grad_adam(ct, dm, w, m, v):
    def body(ct_ref, dm_ref, w_ref, m_ref, v_ref, g_ref, d_ref, m2_ref, v2_ref):
        g = ct_ref[:, 0:1] * dm_ref[0:1, :]
        for b in range(1, NDEV):
            g = g + ct_ref[:, b:b + 1] * dm_ref[b:b + 1, :]
        g_ref[...] = g
        d_ref[...], m2_ref[...], v2_ref[...] = _adam(w_ref[...], g, m_ref[...], v_ref[...])

    vm = pl.BlockSpec(memory_space=VMEM)
    sd = jax.ShapeDtypeStruct(w.shape, F32)
    return pl.pallas_call(
        body, name="ada_grad_adam", out_shape=(sd, sd, sd, sd),
        in_specs=[vm] * 5, out_specs=(vm,) * 4, compiler_params=_cp(vmem_mb=32),
    )(ct, dm, w, m, v)


def _adam_f32(name, g, w, m, v, rows=None):
    R, C = w.shape
    rows = R if rows is None else rows

    def body(g_ref, w_ref, m_ref, v_ref, d_ref, m2_ref, v2_ref):
        d_ref[...], m2_ref[...], v2_ref[...] = _adam(w_ref[...], g_ref[...], m_ref[...], v_ref[...])

    tile = pl.BlockSpec((rows, C), lambda i: (i, 0))
    sd = jax.ShapeDtypeStruct(w.shape, F32)
    return pl.pallas_call(
        body, name=name, grid=(R // rows,), out_shape=(sd, sd, sd), in_specs=[tile] * 4, out_specs=(tile,) * 3,
        compiler_params=_cp(("parallel",), vmem_mb=32),
    )(g, w, m, v)


def _adam_w_in(gown, got, w, m, v, rows=256):
    R, C = w.shape

    def body(go_ref, got_ref, w_ref, m_ref, v_ref, g_ref, d_ref, m2_ref, v2_ref):
        g = go_ref[...]
        for q in range(3):
            g = g + got_ref[q].astype(F32)
        g_ref[...] = g
        d_ref[...], m2_ref[...], v2_ref[...] = _adam(w_ref[...], g, m_ref[...], v_ref[...])

    tile = pl.BlockSpec((rows, C), lambda i: (i, 0))
    sd = jax.ShapeDtypeStruct((R, C), F32)
    return pl.pallas_call(
        body, name="adam_w_in", grid=(R // rows,), out_shape=(sd, sd, sd, sd),
        in_specs=[tile, pl.BlockSpec((3, rows, C), lambda i: (0, i, 0)), tile, tile, tile], out_specs=(tile,) * 4,
        compiler_params=_cp(("parallel",), vmem_mb=32),
    )(gown, got, w, m, v)


def _adam_reduce(name, recv, w, m, v, rows, recv_spec):
    R, C = w.shape

    def body(r_ref, w_ref, m_ref, v_ref, g_ref, d_ref, m2_ref, v2_ref):
        g = r_ref[0].astype(F32)
        for s in range(1, NDEV):
            g = g + r_ref[s].astype(F32)
        g_ref[...] = g
        d_ref[...], m2_ref[...], v2_ref[...] = _adam(w_ref[...], g, m_ref[...], v_ref[...])

    tile = pl.BlockSpec((rows, C), lambda i: (i, 0))
    sd = jax.ShapeDtypeStruct((R, C), F32)
    return pl.pallas_call(
        body, name=name, grid=(R // rows,), out_shape=(sd, sd, sd, sd),
        in_specs=[recv_spec, tile, tile, tile], out_specs=(tile,) * 4,
        compiler_params=_cp(("parallel",)),
    )(recv, w, m, v)


def kernel(x, c, w_ada, b_ada, g_pre, w_in, conv_w, conv_b, conv_ln_g, conv_ln_b, w_conv_out, sgu_ln_g, sgu_ln_b, w_sgu, b_sgu, w_sgu_out, w_o, g_final, loss_target, m_w_ada, m_b_ada, m_g_pre, m_w_in, m_conv_w, m_conv_b, m_conv_ln_g, m_conv_ln_b, m_w_conv_out, m_sgu_ln_g, m_sgu_ln_b, m_w_sgu, m_b_sgu, m_w_sgu_out, m_w_o, m_g_final, v_w_ada, v_b_ada, v_g_pre, v_w_in, v_conv_w, v_conv_b, v_conv_ln_g, v_conv_ln_b, v_w_conv_out, v_sgu_ln_g, v_sgu_ln_b, v_w_sgu, v_b_sgu, v_w_sgu_out, v_w_o, v_g_final):
    T = x.shape[1]
    assert T % TMB == 0 and x.shape[2] == D
    xs = x[0]
    tgt = loss_target[0]
    my = 4 * lax.axis_index("x") + 2 * lax.axis_index("y") + lax.axis_index("c")

    mod, cg = _ada_fwd(c, w_ada[0], b_ada)
    gf = g_final.reshape(1, D)
    bst = b_sgu[0].T

    h, ht = _prep_h(xs, mod, g_pre)
    ag_order = jnp.stack([_slot(*b) for b in _ag_blocks(lax.axis_index("x"), lax.axis_index("y"), lax.axis_index("c"))])
    p, wing, wqg, cwg = _in_proj(h, w_in[0], ag_order.astype(jnp.int32),
                                 w_conv_out[0], w_sgu_out[0], w_o[0], conv_w[0])
    wq = wqg.reshape(3, D, D)
    cw = jnp.broadcast_to(jnp.transpose(cwg, (1, 0, 2)).reshape(KW, 1, D), (KW, 8, D))
    yc, ya = _branch_a_fwd(p, cw, conv_b, conv_ln_g, conv_ln_b, wq)
    yb = _branch_b_fwd(p, w_sgu[0], bst, sgu_ln_g, sgu_ln_b, wq)
    dx2, dya, dyb, dp, gw_o, st_m = _merge_loss(p, ya, yb, xs, tgt, mod, gf, wq)

    dp, gw_so, gws, gbs, st_b = _branch_b_bwd(dp, dyb, p, w_sgu[0], bst, sgu_ln_g, sgu_ln_b, wq)
    dp, dyc, gw_co, st_a = _branch_a_bwd1(dp, dya, yc, p, conv_ln_g, conv_ln_b, wq)
    early = _pack_early(st_a, st_b, st_m, gws, gbs)
    dp, gcw, rq, eg = _branch_a_bwd2(dp, dyc, p, cw, gw_co, gw_so, gw_o, early)
    csum, gown = _in_bwd_dw(ht, dp, my ^ jnp.array(RS_ORDER, jnp.int32))
    rs_send, rs_recv, csum_thru, land_thru, token = _rs_start(csum)
    grad_x, st_x = _in_bwd_dx(dp, wing, xs, dx2, mod, g_pre, token)
    got = _rs_wait(rs_send, rs_recv, csum_thru, land_thru, st_x)
    lg = _gather_late(_pack_late(st_x, gcw))

    def pack(name, b_ada_, g_pre_, conv_b_, lg_, lb_, sg_, sb_, b_sgu_, gfin_, w_sgu_):
        return _pack_params(name, b_ada_, g_pre_, conv_b_, lg_, lb_, sg_, sb_, b_sgu_.reshape(1, D), gfin_.reshape(1, D),
                            jnp.transpose(w_sgu_[0], (1, 0, 2)).reshape(CHUNK, D))

    pw = pack("pack_w", b_ada, g_pre, conv_b, conv_ln_g, conv_ln_b, sgu_ln_g, sgu_ln_b, b_sgu, g_final, w_sgu)
    pm = pack("pack_m", m_b_ada, m_g_pre, m_conv_b, m_conv_ln_g, m_conv_ln_b, m_sgu_ln_g, m_sgu_ln_b, m_b_sgu, m_g_final,
              m_w_sgu)
    pv = pack("pack_v", v_b_ada, v_g_pre, v_conv_b, v_conv_ln_g, v_conv_ln_b, v_sgu_ln_g, v_sgu_ln_b, v_b_sgu, v_g_final,
              v_w_sgu)
    loss11, sg_, sd_, sm_, sv_ = _small_finish(eg, lg, pw, pm, pv)

    def unpack(a):
        return dict(
            b_ada=a[0:3].reshape(1, 3 * D), g_pre=a[3:4], conv_b=a[4:5], conv_ln_g=a[5:6], conv_ln_b=a[6:7],
            sgu_ln_g=a[7:8], sgu_ln_b=a[8:9], b_sgu=a[9:10].reshape(1, HEADS, CHUNK), g_final=a[10],
            w_sgu=jnp.transpose(a[ROW_WS:ROW_WS + CHUNK].reshape(CHUNK, HEADS, CHUNK), (1, 0, 2))[None])
    small = [unpack(a) for a in (sg_, sd_, sm_, sv_)]

    g_cw = lax.dynamic_slice_in_dim(sg_[ROW_CW:ROW_CW + KW], my * 128, 128, axis=1)
    d_cw, m_cw, v_cw = _adam_f32("adam_conv_w", g_cw, conv_w[0], m_conv_w[0], v_conv_w[0])

    wcols = w_ada.shape[2]
    dm_all = jnp.concatenate([lg[:, 0], lg[:, 1], eg[:, 0]], axis=1)
    dm_mine = lax.dynamic_slice_in_dim(dm_all, my * wcols, wcols, axis=1)
    g_ada, d_ada, m_ada, v_ada = _ada_grad_adam(cg.T, dm_mine, w_ada[0], m_w_ada[0], v_w_ada[0])

    g_in, d_in, m_in, v_in = _adam_w_in(gown, got, w_in[0], m_w_in[0], v_w_in[0])
    big = {}
    for j, (nm, w_, m_, v_) in enumerate((("w_conv_out", w_conv_out, m_w_conv_out, v_w_conv_out),
                                          ("w_sgu_out", w_sgu_out, m_w_sgu_out, v_w_sgu_out),
                                          ("w_o", w_o, m_w_o, v_w_o))):
        big[nm] = _adam_reduce("adam_" + nm, rq, w_[0], m_[0], v_[0], 128,
                               pl.BlockSpec((NDEV, None, 128, D), lambda i, j=j: (0, j, 0, 0)))

    per = {
        "w_ada": tuple(a[None] for a in (g_ada, d_ada, m_ada, v_ada)),
        "w_in": tuple(a[None] for a in (g_in, d_in, m_in, v_in)),
        "conv_w": tuple(a[None] for a in (g_cw, d_cw, m_cw, v_cw)),
    }
    for nm in ("w_conv_out", "w_sgu_out", "w_o"):
        per[nm] = tuple(a[None] for a in big[nm])
    for nm in ("b_ada", "g_pre", "conv_b", "conv_ln_g", "conv_ln_b", "sgu_ln_g", "sgu_ln_b", "w_sgu", "b_sgu", "g_final"):
        per[nm] = tuple(s[nm] for s in small)

    order = ["w_ada", "b_ada", "g_pre", "w_in", "conv_w", "conv_b", "conv_ln_g", "conv_ln_b", "w_conv_out",
             "sgu_ln_g", "sgu_ln_b", "w_sgu", "b_sgu", "w_sgu_out", "w_o", "g_final"]
    outs = [loss11.reshape(()), grad_x[None]]
    for part in range(4):
        outs += [per[nm][part] for nm in order]
    return tuple(outs)
```

```python
import jax
import jax.numpy as jnp
from jax import lax
from jax.experimental import pallas as pl
from jax.experimental.pallas import tpu as pltpu

F32 = jnp.float32
BF16 = jnp.bfloat16
MESH = pl.DeviceIdType.MESH
VMEM = pltpu.VMEM
ANY = pl.ANY

D = 1024
NDEV = 8
NSEG = 8
HEADS = 8
HD = 128
CHUNK = 128
KW = 31
HALO = 32
EPS = 1e-6
TM = 256
TMB = 512
RB = 32
UNROLL = 4
RBC = 32
TAPG = 4
EXT = TM + HALO
NSMALL = 176
NEARLY = 144
NLATE = 40
ROW_CW = 16
ROW_WS = 48
W_CO = 0
W_SO, W_O = 0, 1

ADAM_LR = 0.001
ADAM_B1 = 0.9
ADAM_B2 = 0.999
ADAM_EPS = 1e-08
ADAM_WD = 0.01
ADAM_STEP = 10

INV_SQRT2 = 0.7071067811865476
INV_SQRT_2PI = 0.3989422804014327

NT = (((1,), (1,)), ((), ()))
TN = (((0,), (0,)), ((), ()))


def _cp(sem=None, vmem_mb=48):
    return pltpu.CompilerParams(dimension_semantics=sem, vmem_limit_bytes=vmem_mb * 1024 * 1024)


def _me():
    return lax.axis_index("x"), lax.axis_index("y"), lax.axis_index("c")


def _slot(px, py, pc):
    return 4 * px + 2 * py + pc


def _xor_peer(k):
    x, y, c = _me()
    return (1 - x if k & 4 else x, 1 - y if k & 2 else y, 1 - c if k & 1 else c)


def _allgather_phases(srcs, dst_at, send_sems, recv_sems, loc_sems):
    x, y, c = _me()
    me = (x, y, c)
    sibling = (x, y, 1 - c)
    chips = [(1 - x, y), (x, 1 - y), (1 - x, 1 - y)]
    na = len(srcs)

    def copy(a, k, block, to, src=None):
        d = dst_at(a, _slot(*block))
        return pltpu.make_async_remote_copy(
            src_ref=d if src is None else src, dst_ref=d,
            send_sem=send_sems.at[a, k], recv_sem=recv_sems.at[a, k],
            device_id=to, device_id_type=MESH)

    local = [pltpu.make_async_copy(srcs[a], dst_at(a, _slot(*me)), loc_sems.at[a]) for a in range(na)]
    first = []
    for a in range(na):
        first.append(copy(a, 0, me, sibling, src=srcs[a]))
        for j, chip in enumerate(chips):
            first.append(copy(a, 1 + j, me, (*chip, c), src=srcs[a]))
    passed = [copy(a, 4 + j, (*chip, c), sibling) for j, chip in enumerate(chips) for a in range(na)]

    def start():
        for cp in local + first:
            cp.start()

    def relay():
        for j, chip in enumerate(chips):
            for a in range(na):
                copy(a, 1 + j, (*chip, c), me).wait_recv()
                passed[j * na + a].start()

    def finish():
        for a in range(na):
            copy(a, 0, sibling, me).wait_recv()
        for j, chip in enumerate(chips):
            for a in range(na):
                copy(a, 4 + j, (*chip, 1 - c), me).wait_recv()
        for cp in first + passed:
            cp.wait_send()
        for cp in local:
            cp.wait()

    return start, relay, finish


def _allgather(srcs, dst_at, send_sems, recv_sems, loc_sems):
    start, relay, finish = _allgather_phases(srcs, dst_at, send_sems, recv_sems, loc_sems)
    start()
    relay()
    finish()


def _owner_exchange(gq, rq_ref, x_send, x_recv, x_loc):
    mx, my_, mc = _me()
    me = _slot(mx, my_, mc)

    def rows_of(a, slot):
        return gq[a].at[pl.ds(pl.multiple_of(slot * 128, 128), 128), :]

    def exchange(k, a, recv):
        px, py, pc = _xor_peer(k)
        peer = _slot(px, py, pc)
        return pltpu.make_async_remote_copy(
            src_ref=rows_of(a, me if recv else peer), dst_ref=rq_ref.at[peer if recv else me, a],
            send_sem=x_send.at[a, k - 1], recv_sem=x_recv.at[a, k - 1],
            device_id=(px, py, pc), device_id_type=MESH)

    local = [pltpu.make_async_copy(rows_of(a, me), rq_ref.at[me, a], x_loc.at[a]) for a in range(len(gq))]

    def start():
        for cp in local:
            cp.start()
        for k in range(1, NDEV):
            for a in range(len(gq)):
                exchange(k, a, False).start()

    def wait():
        for k in range(1, NDEV):
            for a in range(len(gq)):
                exchange(k, a, True).wait_recv()
        for k in range(1, NDEV):
            for a in range(len(gq)):
                exchange(k, a, False).wait_send()
        for cp in local:
            cp.wait()

    return start, wait


def _sig(v):
    return jax.nn.sigmoid(v)


def _gelu_parts(v):
    cdf = 0.5 * (1.0 + lax.erf(v * INV_SQRT2))
    pdf = jnp.exp(-0.5 * v * v) * INV_SQRT_2PI
    return v * cdf, cdf + v * pdf


def _ln_stats(v):
    mu = jnp.mean(v, axis=-1, keepdims=True)
    xc = v - mu
    var = jnp.mean(xc * xc, axis=-1, keepdims=True)
    rs = lax.rsqrt(var + EPS)
    return xc * rs, rs


def _ln_bwd(dn, n, rs):
    return rs * (dn - jnp.mean(dn, axis=-1, keepdims=True) - n * jnp.mean(dn * n, axis=-1, keepdims=True))


def _fold8(v):
    acc = v[0:8]
    for r in range(8, v.shape[0], 8):
        acc = acc + v[r:r + 8]
    return acc


def _rows(rb, n=RB):
    return pl.ds(pl.multiple_of(rb * n, n), n)


def _ring_tiles(nt):
    return 4 if nt % 4 == 0 else (2 if nt % 2 == 0 else 1)


def _ring_base(i, nt):
    return pl.multiple_of((i % _ring_tiles(nt)) * TM, TM)


def _ring_rows(base, rb, n=RB):
    return pl.ds(pl.multiple_of(base + rb * n, n), n)


def _ring_full(i, nt):
    g = _ring_tiles(nt)
    return i % g == g - 1


def _ld(ref, rows):
    return ref[rows, :].astype(F32)


def _blocks(n, body, init=0, unroll=UNROLL):
    def trip(t, carry):
        for u in range(unroll):
            carry = body(t * unroll + u, carry)
        return carry
    return lax.fori_loop(0, n // unroll, trip, init)


def _shift_copies(s_ref):
    n = EXT - 8
    for m in range(1, 8):
        for c0 in range(0, n, 56):
            s_ref[m, c0:c0 + 56, :] = s_ref[0, c0 + m:c0 + m + 56, :]


def _ada_exchange(c_ref, w_ref, b_ref, mod_ref, call_ref, cdst, msrc, mdst, c_send, c_recv, m_send, m_recv):
    x, y, c = _me()
    me = _slot(x, y, c)

    def all_to_all(src_of, dst, ss, rs):
        def cp(k, recv):
            px, py, pc = _xor_peer(k)
            peer = _slot(px, py, pc)
            return pltpu.make_async_remote_copy(
                src_ref=src_of(me if recv else peer), dst_ref=dst.at[peer if recv else me],
                send_sem=ss.at[k - 1], recv_sem=rs.at[k - 1], device_id=(px, py, pc), device_id_type=MESH)
        for k in range(1, NDEV):
            cp(k, False).start()
        for k in range(1, NDEV):
            cp(k, True).wait_recv()
        for k in range(1, NDEV):
            cp(k, False).wait_send()

    cdst[me] = c_ref[...]
    all_to_all(lambda s: c_ref, cdst, c_send, c_recv)
    for b in range(NDEV):
        call_ref[b:b + 1, :] = cdst[b]
    m = jnp.dot(call_ref[...], w_ref[...], preferred_element_type=F32, precision=lax.Precision.HIGHEST)
    for b in range(NDEV):
        msrc[b] = m[b:b + 1, :]
    mdst[me] = msrc[me]
    all_to_all(lambda s: msrc.at[s], mdst, m_send, m_recv)
    full = jnp.concatenate([mdst[k] for k in range(NDEV)], axis=1) + b_ref[...]
    for r in range(3):
        mod_ref[r:r + 1, :] = full[:, r * D:(r + 1) * D]


def _ada_fwd(c, w_ada, b_ada):
    wcols = w_ada.shape[1]

    def body(*refs):
        _ada_exchange(*refs)

    vm = pl.BlockSpec(memory_space=VMEM)
    return pl.pallas_call(
        body, name="ada_fwd",
        out_shape=(jax.ShapeDtypeStruct((3, D), F32), jax.ShapeDtypeStruct((NDEV, D), F32)),
        in_specs=[vm, vm, vm], out_specs=(vm, vm),
        scratch_shapes=[VMEM((NDEV, 1, D), F32), VMEM((NDEV, 1, wcols), F32), VMEM((NDEV, 1, wcols), F32),
                        pltpu.SemaphoreType.DMA((7,)), pltpu.SemaphoreType.DMA((7,)),
                        pltpu.SemaphoreType.DMA((7,)), pltpu.SemaphoreType.DMA((7,))],
        compiler_params=_cp(vmem_mb=32),
    )(c, w_ada, b_ada)


def _prep_h(x, mod, g_pre):
    T = x.shape[0]

    def body(x_ref, mod_ref, g_ref, h_ref, ht_ref):
        def blk(rb, carry):
            rows = _rows(rb)
            xv = x_ref[rows, :]
            r = lax.rsqrt(jnp.mean(xv * xv, axis=-1, keepdims=True) + EPS)
            h_ref[rows, :] = ((xv * r) * g_ref[...] * (1.0 + mod_ref[1:2, :]) + mod_ref[0:1, :]).astype(BF16)
            return carry
        _blocks(TMB // RB, blk)
        ht_ref[...] = h_ref[...].T

    return pl.pallas_call(
        body, name="prep_h", grid=(T // TMB,),
        out_shape=(jax.ShapeDtypeStruct((T, D), BF16), jax.ShapeDtypeStruct((D, T), BF16)),
        in_specs=[pl.BlockSpec((TMB, D), lambda i: (i, 0)), pl.BlockSpec((3, D), lambda i: (0, 0)),
                  pl.BlockSpec((1, D), lambda i: (0, 0))],
        out_specs=(pl.BlockSpec((TMB, D), lambda i: (i, 0)), pl.BlockSpec((D, TMB), lambda i: (0, i))),
        compiler_params=_cp(("parallel",)),
    )(x, mod, g_pre)


def _ag_blocks(x, y, c):
    a = 1 - c
    n1 = (x ^ a, y ^ c)
    n2 = (x ^ c, y ^ a)
    dg = (1 - x, 1 - y)
    return [(x, y, c), (x, y, a), (*n1, c), (*n2, c), (*n1, a), (*n2, a), (*dg, c), (*dg, a)]


def _in_proj(h, w_in, order, w_co, w_so, w_o, conv_w):
    T = h.shape[0]
    tmp = min(2 * TMB, T)
    nb = T // tmp

    def body(order_ref, h_ref, w_ref, wco_ref, wso_ref, wo_ref, cw_ref, p_ref, wing_ref, wqg_ref, cwg_ref, own2_ref, land2_ref,
             wbuf, wq_st, wq2_st, send_sems, recv_sems, out_sems, ag_send, ag_recv, ag_loc, own_sems):
        j = pl.program_id(0)
        i = pl.program_id(1)
        blocks = _ag_blocks(*_me())
        me, sibling, n1, n2, n1o, n2o, dg, dgo = blocks
        own2 = [pltpu.make_async_copy(wq2_st, own2_ref, own_sems.at[0]),
                pltpu.make_async_copy(wq2_st, land2_ref.at[:, _slot(*me)], own_sems.at[1])]
        ag_start, ag_relay, ag_finish = _allgather_phases(
            [wq_st, cw_ref], lambda a, slot: wqg_ref.at[:, slot] if a == 0 else cwg_ref.at[slot], ag_send, ag_recv, ag_loc)

        def copy(k, block, to):
            d = wbuf.at[_slot(*block)]
            return pltpu.make_async_remote_copy(src_ref=d, dst_ref=d, send_sem=send_sems.at[k], recv_sem=recv_sems.at[k],
                                                device_id=to, device_id_type=MESH)

        def writeback(jj):
            s = _slot(*blocks[jj])
            return pltpu.make_async_copy(wbuf.at[s], wing_ref.at[s], out_sems.at[jj])

        sends = [copy(0, me, sibling), copy(1, me, n1), copy(2, me, n2)]
        relay = copy(3, n1, n2)
        passed = [copy(4, n1, sibling), copy(5, n2, sibling), copy(6, dg, sibling)]
        arrivals = {1: [(0, sibling)], 2: [(1, n1), (2, n2)], 4: [(5, n1o)], 5: [(4, n2o)], 6: [(3, dg)], 7: [(6, dgo)]}

        @pl.when((j == 0) & (i == 0))
        def _():
            wbuf[_slot(*me)] = w_ref[...].astype(BF16)
            for cp in sends:
                cp.start()
            writeback(0).start()
            wq2_st[W_SO] = wso_ref[...].astype(BF16)
            wq2_st[W_O] = wo_ref[...].astype(BF16)
            for cp in own2:
                cp.start()

        for jj in range(1, NSEG):
            @pl.when((j == jj) & (i == 0))
            def _(jj=jj):
                for k, block in arrivals.get(jj, []):
                    copy(k, block, me).wait_recv()
                if jj == 2:
                    relay.start()
                    passed[0].start()
                    passed[1].start()
                    writeback(3).start()
                    wq_st[0] = wco_ref[...].astype(BF16)
                    ag_start()
                if jj == 6:
                    passed[2].start()
                if jj != 3:
                    writeback(jj).start()
                if jj == NSEG - 1:
                    ag_relay()

        p_ref[...] = jnp.dot(h_ref[...], wbuf[order_ref[j]], preferred_element_type=F32).astype(BF16)

        @pl.when((j == NSEG - 1) & (i == nb - 1))
        def _():
            for cp in sends + [relay] + passed:
                cp.wait_send()
            for jj in range(NSEG):
                writeback(jj).wait()
            for cp in own2:
                cp.wait()
            ag_finish()

    vm = pl.BlockSpec(memory_space=VMEM)
    hbm = pl.BlockSpec(memory_space=ANY)
    return pl.pallas_call(
        body, name="in_proj",
        out_shape=(jax.ShapeDtypeStruct((T, NSEG * D), BF16), jax.ShapeDtypeStruct((NSEG, D, D), BF16),
                   jax.ShapeDtypeStruct((1, NDEV, 128, D), BF16), jax.ShapeDtypeStruct((NDEV, KW, 128), F32),
                   jax.ShapeDtypeStruct((2, 128, D), BF16), jax.ShapeDtypeStruct((2, NDEV, 128, D), BF16)),
        grid_spec=pltpu.PrefetchScalarGridSpec(
            num_scalar_prefetch=1, grid=(NSEG, nb),
            in_specs=[pl.BlockSpec((tmp, D), lambda j, i, o: (i, 0)), vm, vm, vm, vm, vm],
            out_specs=(pl.BlockSpec((tmp, D), lambda j, i, o: (i, o[j])), hbm, hbm, hbm, hbm, hbm),
            scratch_shapes=[VMEM((NSEG, D, D), BF16), VMEM((1, 128, D), BF16), VMEM((2, 128, D), BF16),
                            pltpu.SemaphoreType.DMA((7,)), pltpu.SemaphoreType.DMA((7,)), pltpu.SemaphoreType.DMA((NSEG,)),
                            pltpu.SemaphoreType.DMA((2, 7)), pltpu.SemaphoreType.DMA((2, 7)), pltpu.SemaphoreType.DMA((2,)),
                            pltpu.SemaphoreType.DMA((2,))]),
        compiler_params=_cp(("arbitrary", "arbitrary")),
    )(order, h, w_in, w_co, w_so, w_o, conv_w)


def _wq_start(own2, land2):
    def body(src_ref, land_ref, send_sems, recv_sems, src_thru, land_thru, token):
        barrier = pltpu.get_barrier_semaphore()
        for k in range(1, NDEV):
            pl.semaphore_signal(barrier, inc=1, device_id=_xor_peer(k), device_id_type=MESH)
        pl.semaphore_wait(barrier, NDEV - 1)
        x, y, c = _me()
        for k in range(1, NDEV):
            pltpu.make_async_remote_copy(src_ref=src_ref, dst_ref=land_ref.at[:, _slot(x, y, c)], send_sem=send_sems.at[k - 1],
                                         recv_sem=recv_sems.at[k - 1], device_id=_xor_peer(k), device_id_type=MESH).start()
        token[...] = jnp.zeros_like(token)

    hbm = pl.BlockSpec(memory_space=pltpu.HBM)
    sem = pl.BlockSpec(memory_space=pltpu.SEMAPHORE)
    return pl.pallas_call(
        body, name="wq_start",
        out_shape=(pltpu.SemaphoreType.DMA((NDEV - 1,)), pltpu.SemaphoreType.DMA((NDEV - 1,)),
                   pltpu.HBM(own2.shape, own2.dtype), pltpu.HBM(land2.shape, land2.dtype), jax.ShapeDtypeStruct((8, 128), F32)),
        in_specs=(hbm, hbm), out_specs=(sem, sem, hbm, hbm, pl.BlockSpec(memory_space=VMEM)),
        input_output_aliases={0: 2, 1: 3},
        compiler_params=pltpu.CompilerParams(has_side_effects=pltpu.SideEffectType.DATAFLOW_SIDE_EFFECTING, collective_id=2),
    )(pltpu.with_memory_space_constraint(own2, pltpu.HBM), pltpu.with_memory_space_constraint(land2, pltpu.HBM))


def _wq_wait(send_sems, recv_sems, src_thru, land_thru, after):
    def body(src_ref, land_ref, send_sems, recv_sems, after_ref, src_dead, got_ref):
        del after_ref, src_dead, got_ref
        for k in range(1, NDEV):
            px, py, pc = _xor_peer(k)
            cp = pltpu.make_async_remote_copy(src_ref=src_ref, dst_ref=land_ref.at[:, _slot(px, py, pc)],
                                              send_sem=send_sems.at[k - 1], recv_sem=recv_sems.at[k - 1],
                                              device_id=(px, py, pc), device_id_type=MESH)
            cp.wait_send()
            cp.wait_recv()

    hbm = pl.BlockSpec(memory_space=pltpu.HBM)
    sem = pl.BlockSpec(memory_space=pltpu.SEMAPHORE)
    return pl.pallas_call(
        body, name="wq_wait",
        out_shape=(pltpu.HBM(src_thru.shape, src_thru.dtype), pltpu.HBM(land_thru.shape, land_thru.dtype)),
        in_specs=(hbm, hbm, sem, sem, pl.BlockSpec(memory_space=ANY)), out_specs=(hbm, hbm),
        input_output_aliases={0: 0, 1: 1},
        compiler_params=pltpu.CompilerParams(has_side_effects=pltpu.SideEffectType.DATAFLOW_SIDE_EFFECTING),
    )(src_thru, land_thru, send_sems, recv_sems, after)[1]


def _fill_a_ext(i, s_ref, val_ref, glu_ref, hval_ref, hglu_ref):
    ah = hval_ref[...].astype(F32) * _sig(hglu_ref[...].astype(F32))
    s_ref[0, 0:HALO, :] = jnp.where(i > 0, ah, 0.0)

    def fill(rb, carry):
        rows = _rows(rb)
        s_ref[0, pl.ds(pl.multiple_of(HALO + rb * RB, RB), RB), :] = _ld(val_ref, rows) * _sig(_ld(glu_ref, rows))
        return carry
    lax.fori_loop(0, TM // RB, fill, 0)


def _halo_prev(seg):
    hb = TM // HALO
    return pl.BlockSpec((HALO, D), lambda i: (jnp.maximum(i * hb - 1, 0), seg))


def _branch_a_fwd(p, cw, conv_b, lg, lb, wq, token):
    T = p.shape[0]

    def body(val_ref, glu_ref, z_ref, hval_ref, hglu_ref, cw_ref, cb_ref, lg_ref, lb_ref, w_ref, token_ref,
             yc_ref, ya_ref, s_ref, yp_ref):
        del token_ref
        i = pl.program_id(0)
        _fill_a_ext(i, s_ref, val_ref, glu_ref, hval_ref, hglu_ref)
        _shift_copies(s_ref)

        def conv(rb, carry):
            r0 = rb * RBC
            accs = [jnp.broadcast_to(cb_ref[...], (8, D))] * (RBC // 8)
            for k in range(KW):
                o = 2 + k
                w = cw_ref[k]
                accs = [acc + w * s_ref[o % 8, pl.ds(pl.multiple_of(r0 + 8 * (o // 8 + g), 8), 8), :]
                        for g, acc in enumerate(accs)]
            for g, acc in enumerate(accs):
                yc_ref[pl.ds(pl.multiple_of(r0 + 8 * g, 8), 8), :] = acc
            return carry
        lax.fori_loop(0, TM // RBC, conv, 0)

        def post(rb, carry):
            rows = _rows(rb)
            n, _ = _ln_stats(yc_ref[rows, :])
            l = n * lg_ref[...] + lb_ref[...]
            z = _ld(z_ref, rows)
            yp_ref[rows, :] = ((l * _sig(l)) * (z * _sig(z))).astype(BF16)
            return carry
        _blocks(TM // RB, post)
        ya_ref[...] = jnp.dot(yp_ref[...], w_ref[...], preferred_element_type=F32)

    tile = lambda seg: pl.BlockSpec((TM, D), lambda i: (i, seg))
    row = pl.BlockSpec((1, D), lambda i: (0, 0))
    return pl.pallas_call(
        body, name="branch_a_fwd", grid=(T // TM,),
        out_shape=(jax.ShapeDtypeStruct((T, D), F32), jax.ShapeDtypeStruct((T, D), F32)),
        in_specs=[tile(0), tile(1), tile(2), _halo_prev(0), _halo_prev(1),
                  pl.BlockSpec((KW, 8, D), lambda i: (0, 0, 0)), row, row, row,
                  pl.BlockSpec((None, D, D), lambda i: (W_CO, 0, 0)), pl.BlockSpec((8, 128), lambda i: (0, 0))],
        out_specs=(pl.BlockSpec((TM, D), lambda i: (i, 0)), pl.BlockSpec((TM, D), lambda i: (i, 0))),
        scratch_shapes=[VMEM((8, EXT, D), F32), VMEM((TM, D), BF16)],
        compiler_params=_cp(("parallel",)),
    )(p, p, p, p, p, cw, conv_b, lg, lb, wq, token)


def _masked_ws(ws_ref, wt_ref):
    tri = lax.broadcasted_iota(jnp.int32, (CHUNK, CHUNK), 0) >= lax.broadcasted_iota(jnp.int32, (CHUNK, CHUNK), 1)
    for h in range(HEADS):
        wt_ref[h] = jnp.where(tri, ws_ref[h], 0.0).astype(BF16)


def _sgu_mix(wt_ref, vl_ref, bst_ref, s_ref):
    for ck in range(TM // CHUNK):
        r = slice(ck * CHUNK, (ck + 1) * CHUNK)
        for h in range(HEADS):
            cs = slice(h * HD, (h + 1) * HD)
            s_ref[r, cs] = jnp.dot(wt_ref[h], vl_ref[r, cs], preferred_element_type=F32) + bst_ref[:, h:h + 1]


def _branch_b_fwd(p, ws, bst, sg, sb, wq):
    T = p.shape[0]

    def body(pu_ref, pv_ref, pz_ref, ws_ref, bst_ref, sg_ref, sb_ref, w_ref, yb_ref,
             wt_ref, vl_ref, t_ref, s_ref, yp_ref):
        _masked_ws(ws_ref, wt_ref)

        def pre(rb, carry):
            rows = _rows(rb)
            vg, _ = _gelu_parts(_ld(pv_ref, rows))
            vn, _ = _ln_stats(vg)
            vl_ref[rows, :] = (vn * sg_ref[...] + sb_ref[...]).astype(BF16)
            u, _ = _gelu_parts(_ld(pu_ref, rows))
            z = _ld(pz_ref, rows)
            t_ref[rows, :] = u * (z * _sig(z))
            return carry
        _blocks(TM // RB, pre)
        _sgu_mix(wt_ref, vl_ref, bst_ref, s_ref)

        def post(rb, carry):
            rows = _rows(rb)
            yp_ref[rows, :] = (t_ref[rows, :] * s_ref[rows, :]).astype(BF16)
            return carry
        _blocks(TM // RB, post)
        yb_ref[...] = jnp.dot(yp_ref[...], w_ref[...], preferred_element_type=F32)

    tile = lambda seg: pl.BlockSpec((TM, D), lambda i: (i, seg))
    row = pl.BlockSpec((1, D), lambda i: (0, 0))
    return pl.pallas_call(
        body, name="branch_b_fwd", grid=(T // TM,),
        out_shape=jax.ShapeDtypeStruct((T, D), F32),
        in_specs=[tile(3), tile(4), tile(5), pl.BlockSpec((HEADS, CHUNK, CHUNK), lambda i: (0, 0, 0)),
                  pl.BlockSpec((CHUNK, HEADS), lambda i: (0, 0)), row, row,
                  pl.BlockSpec((None, D, D), lambda i: (W_SO, 0, 0))],
        out_specs=pl.BlockSpec((TM, D), lambda i: (i, 0)),
        scratch_shapes=[VMEM((HEADS, CHUNK, CHUNK), BF16), VMEM((TM, D), BF16), VMEM((TM, D), F32),
                        VMEM((TM, D), F32), VMEM((TM, D), BF16)],
        compiler_params=_cp(("parallel",)),
    )(p, p, p, ws, bst, sg, sb, wq)


def _merge_loss(p, ya, yb, x, tgt, mod, g_final, wq):
    T = x.shape[0]
    nt = T // TM

    def body(ga_ref, gb_ref, ya_ref, yb_ref, x_ref, t_ref, mod_ref, gf_ref, w_ref,
             dx2_ref, dya_ref, dyb_ref, dp_ref, gwo_ref, st_ref,
             mrg_s, out_s, dout_s, dm_s, gw_acc):
        i = pl.program_id(0)
        base = _ring_base(i, nt)
        cur = pl.ds(base, TM)

        @pl.when(i == 0)
        def _():
            gw_acc[...] = jnp.zeros_like(gw_acc)
            st_ref[...] = jnp.zeros_like(st_ref)

        def merge(rb, carry):
            rows = _rows(rb)
            mrg_s[_ring_rows(base, rb), :] = (_sig(_ld(ga_ref, rows)) * ya_ref[rows, :]
                                              + _sig(_ld(gb_ref, rows)) * yb_ref[rows, :]).astype(BF16)
            return carry
        lax.fori_loop(0, TM // RB, merge, 0)
        out_s[...] = jnp.dot(mrg_s[cur, :], w_ref[...], preferred_element_type=F32)

        def head(rb, carry):
            loss, gg, dg = carry
            rows = _rows(rb)
            gate = mod_ref[2:3, :]
            gf = gf_ref[...]
            out = out_s[rows, :]
            x2 = x_ref[rows, :] + gate * out
            r2 = lax.rsqrt(jnp.mean(x2 * x2, axis=-1, keepdims=True) + EPS)
            x2n = x2 * r2
            diff = x2n * gf - t_ref[rows, :]
            dy = diff * (1.0 / D)
            dx2n = dy * gf
            dx2 = r2 * (dx2n - x2n * jnp.mean(dx2n * x2n, axis=-1, keepdims=True))
            dx2_ref[rows, :] = dx2
            dout_s[_ring_rows(base, rb), :] = (dx2 * gate).astype(BF16)
            return loss + _fold8(diff * diff), gg + _fold8(dy * x2n), dg + _fold8(dx2 * out)
        zero = jnp.zeros((8, D), F32)
        loss, gg, dg = _blocks(TM // RB, head, (zero, zero, zero))
        st_ref[0] = st_ref[0] + loss * (0.5 / D)
        st_ref[1] = st_ref[1] + gg
        st_ref[2] = st_ref[2] + dg

        dm_s[...] = lax.dot_general(dout_s[cur, :], w_ref[...], NT, preferred_element_type=F32)

        @pl.when(_ring_full(i, nt))
        def _():
            gw_acc[...] += lax.dot_general(mrg_s[...], dout_s[...], TN, preferred_element_type=F32)

        def split(rb, carry):
            rows = _rows(rb)
            dm = dm_s[rows, :]
            sa = _sig(_ld(ga_ref, rows))
            sb = _sig(_ld(gb_ref, rows))
            dya_ref[rows, :] = (dm * sa).astype(BF16)
            dyb_ref[rows, :] = (dm * sb).astype(BF16)
            dp_ref[0, rows, :] = (dm * ya_ref[rows, :] * (sa * (1.0 - sa))).astype(BF16)
            dp_ref[1, rows, :] = (dm * yb_ref[rows, :] * (sb * (1.0 - sb))).astype(BF16)
            return carry
        lax.fori_loop(0, TM // RB, split, 0)

        @pl.when(i == nt - 1)
        def _():
            gwo_ref[...] = gw_acc[...].astype(BF16)

    tile = pl.BlockSpec((TM, D), lambda i: (i, 0))
    ring = VMEM((_ring_tiles(nt) * TM, D), BF16)
    return pl.pallas_call(
        body, name="merge_loss", grid=(nt,),
        out_shape=(jax.ShapeDtypeStruct((T, D), F32), jax.ShapeDtypeStruct((T, D), BF16), jax.ShapeDtypeStruct((T, D), BF16),
                   jax.ShapeDtypeStruct((NSEG, T, D), BF16), jax.ShapeDtypeStruct((D, D), BF16),
                   jax.ShapeDtypeStruct((3, 8, D), F32)),
        in_specs=[pl.BlockSpec((TM, D), lambda i: (i, 6)), pl.BlockSpec((TM, D), lambda i: (i, 7)), tile, tile, tile, tile,
                  pl.BlockSpec((3, D), lambda i: (0, 0)), pl.BlockSpec((1, D), lambda i: (0, 0)),
                  pl.BlockSpec((None, D, D), lambda i: (W_O, 0, 0))],
        out_specs=(tile, tile, tile, pl.BlockSpec((2, TM, D), lambda i: (3, i, 0)),
                   pl.BlockSpec((D, D), lambda i: (0, 0)), pl.BlockSpec((3, 8, D), lambda i: (0, 0, 0))),
        scratch_shapes=[ring, VMEM((TM, D), F32), ring, VMEM((TM, D), F32), VMEM((D, D), F32)],
        compiler_params=_cp(("arbitrary",)),
    )(p, p, ya, yb, x, tgt, mod, g_final, wq)


def _branch_b_bwd(dp, dyb, p, ws, bst, sg, sb, wq):
    T = p.shape[0]
    nt = T // TM

    def body(dp_in, dyb_ref, pu_ref, pv_ref, pz_ref, ws_ref, bst_ref, sg_ref, sb_ref, w_ref,
             dp_ref, gw_ref, gws_ref, gbs_ref, st_ref,
             wt_ref, d_s, vl_s, vn_s, gpv_s, rs_s, s_s, ds_s, ds32_s, yp_s, dy_s, dvl_s, gw_acc):
        del dp_in
        i = pl.program_id(0)
        base = _ring_base(i, nt)

        @pl.when(i == 0)
        def _():
            gw_acc[...] = jnp.zeros_like(gw_acc)
            gws_ref[...] = jnp.zeros_like(gws_ref)
            gbs_ref[...] = jnp.zeros_like(gbs_ref)
            st_ref[...] = jnp.zeros_like(st_ref)
            _masked_ws(ws_ref, wt_ref)

        d_s[...] = lax.dot_general(dyb_ref[...], w_ref[...], NT, preferred_element_type=F32)
        dy_s[pl.ds(base, TM), :] = dyb_ref[...]

        def pre(rb, carry):
            rows = _rows(rb)
            vg, gpv = _gelu_parts(_ld(pv_ref, rows))
            vn, rs = _ln_stats(vg)
            vl_s[rows, :] = (vn * sg_ref[...] + sb_ref[...]).astype(BF16)
            vn_s[rows, :] = vn
            gpv_s[rows, :] = gpv
            rs_s[rows, :] = rs
            return carry
        _blocks(TM // RB, pre)
        _sgu_mix(wt_ref, vl_s, bst_ref, s_s)

        def mid(rb, carry):
            rows = _rows(rb)
            u, gpu = _gelu_parts(_ld(pu_ref, rows))
            z = _ld(pz_ref, rows)
            sz = _sig(z)
            siluz = z * sz
            d = d_s[rows, :]
            s = s_s[rows, :]
            t = u * siluz
            yp_s[_ring_rows(base, rb), :] = (t * s).astype(BF16)
            dp_ref[0, rows, :] = (d * s * siluz * gpu).astype(BF16)
            ds = d * t
            ds32_s[rows, :] = ds
            ds_s[rows, :] = ds.astype(BF16)
            dp_ref[2, rows, :] = (d * u * s * (sz * (1.0 + z * (1.0 - sz)))).astype(BF16)
            return carry
        lax.fori_loop(0, TM // RB, mid, 0)

        for ck in range(TM // CHUNK):
            r = slice(ck * CHUNK, (ck + 1) * CHUNK)
            for h in range(HEADS):
                cs = slice(h * HD, (h + 1) * HD)
                dsh = ds_s[r, cs]
                dvl_s[r, cs] = lax.dot_general(wt_ref[h], dsh, TN, preferred_element_type=F32)
                gws_ref[h] += lax.dot_general(dsh, vl_s[r, cs], NT, preferred_element_type=F32)
                gbs_ref[h] += ds32_s[r, cs]

        def post(rb, carry):
            g_sg, g_sb = carry
            rows = _rows(rb)
            dvl = dvl_s[rows, :]
            vn = vn_s[rows, :]
            dvg = _ln_bwd(dvl * sg_ref[...], vn, rs_s[rows, :])
            dp_ref[1, rows, :] = (dvg * gpv_s[rows, :]).astype(BF16)
            return g_sg + _fold8(dvl * vn), g_sb + _fold8(dvl)
        zero = jnp.zeros((8, D), F32)
        g_sg, g_sb = _blocks(TM // RB, post, (zero, zero))
        st_ref[0] = st_ref[0] + g_sg
        st_ref[1] = st_ref[1] + g_sb

        @pl.when(_ring_full(i, nt))
        def _():
            gw_acc[...] += lax.dot_general(yp_s[...], dy_s[...], TN, preferred_element_type=F32)

        @pl.when(i == nt - 1)
        def _():
            gw_ref[...] = gw_acc[...].astype(BF16)

    tile = lambda seg: pl.BlockSpec((TM, D), lambda i: (i, seg))
    row = pl.BlockSpec((1, D), lambda i: (0, 0))
    hh = pl.BlockSpec((HEADS, CHUNK, CHUNK), lambda i: (0, 0, 0))
    ring = VMEM((_ring_tiles(nt) * TM, D), BF16)
    return pl.pallas_call(
        body, name="branch_b_bwd", grid=(nt,),
        out_shape=(jax.ShapeDtypeStruct((NSEG, T, D), BF16), jax.ShapeDtypeStruct((D, D), BF16),
                   jax.ShapeDtypeStruct((HEADS, CHUNK, CHUNK), F32), jax.ShapeDtypeStruct((HEADS, CHUNK, HD), F32),
                   jax.ShapeDtypeStruct((2, 8, D), F32)),
        in_specs=[pl.BlockSpec(memory_space=ANY), pl.BlockSpec((TM, D), lambda i: (i, 0)), tile(3), tile(4), tile(5),
                  hh, pl.BlockSpec((CHUNK, HEADS), lambda i: (0, 0)), row, row,
                  pl.BlockSpec((None, D, D), lambda i: (W_SO, 0, 0))],
        out_specs=(pl.BlockSpec((3, TM, D), lambda i: (1, i, 0)), pl.BlockSpec((D, D), lambda i: (0, 0)), hh, hh,
                   pl.BlockSpec((2, 8, D), lambda i: (0, 0, 0))),
        scratch_shapes=[VMEM((HEADS, CHUNK, CHUNK), BF16), VMEM((TM, D), F32), VMEM((TM, D), BF16), VMEM((TM, D), F32),
                        VMEM((TM, D), F32), VMEM((TM, 1), F32), VMEM((TM, D), F32), VMEM((TM, D), BF16), VMEM((TM, D), F32),
                        ring, ring, VMEM((TM, D), F32), VMEM((D, D), F32)],
        input_output_aliases={0: 0},
        compiler_params=_cp(("arbitrary",)),
    )(dp, dyb, p, p, p, ws, bst, sg, sb, wq)


def _branch_a_bwd1(dp, dya, yc, p, lg, lb, wq):
    T = p.shape[0]
    nt = T // TM

    def body(dp_in, dya_ref, yc_ref, z_ref, lg_ref, lb_ref, w_ref, dp_ref, dyc_ref, gw_ref, st_ref,
             d_s, yp_s, dy_s, gw_acc):
        del dp_in
        i = pl.program_id(0)
        base = _ring_base(i, nt)

        @pl.when(i == 0)
        def _():
            gw_acc[...] = jnp.zeros_like(gw_acc)
            st_ref[...] = jnp.zeros_like(st_ref)

        d_s[...] = lax.dot_general(dya_ref[...], w_ref[...], NT, preferred_element_type=F32)
        dy_s[pl.ds(base, TM), :] = dya_ref[...]

        def blk(rb, carry):
            g_lg, g_lb, g_cb = carry
            rows = _rows(rb)
            n, rs = _ln_stats(yc_ref[rows, :])
            l = n * lg_ref[...] + lb_ref[...]
            sgl = _sig(l)
            sl = l * sgl
            z = _ld(z_ref, rows)
            sz = _sig(z)
            siluz = z * sz
            d = d_s[rows, :]
            yp_s[_ring_rows(base, rb), :] = (sl * siluz).astype(BF16)
            dp_ref[rows, :] = (d * sl * (sz * (1.0 + z * (1.0 - sz)))).astype(BF16)
            dl = (d * siluz) * (sgl * (1.0 + l * (1.0 - sgl)))
            dyc = _ln_bwd(dl * lg_ref[...], n, rs)
            dyc_ref[rows, :] = dyc
            return g_lg + _fold8(dl * n), g_lb + _fold8(dl), g_cb + _fold8(dyc)
        zero = jnp.zeros((8, D), F32)
        g_lg, g_lb, g_cb = _blocks(TM // RB, blk, (zero, zero, zero))
        st_ref[0] = st_ref[0] + g_lg
        st_ref[1] = st_ref[1] + g_lb
        st_ref[2] = st_ref[2] + g_cb

        @pl.when(_ring_full(i, nt))
        def _():
            gw_acc[...] += lax.dot_general(yp_s[...], dy_s[...], TN, preferred_element_type=F32)

        @pl.when(i == nt - 1)
        def _():
            gw_ref[...] = gw_acc[...].astype(BF16)

    tile = pl.BlockSpec((TM, D), lambda i: (i, 0))
    row = pl.BlockSpec((1, D), lambda i: (0, 0))
    ring = VMEM((_ring_tiles(nt) * TM, D), BF16)
    return pl.pallas_call(
        body, name="branch_a_bwd1", grid=(nt,),
        out_shape=(jax.ShapeDtypeStruct((NSEG, T, D), BF16), jax.ShapeDtypeStruct((T, D), F32),
                   jax.ShapeDtypeStruct((D, D), BF16), jax.ShapeDtypeStruct((3, 8, D), F32)),
        in_specs=[pl.BlockSpec(memory_space=ANY), tile, tile, pl.BlockSpec((TM, D), lambda i: (i, 2)), row, row,
                  pl.BlockSpec((None, D, D), lambda i: (0, 0, 0))],
        out_specs=(pl.BlockSpec((None, TM, D), lambda i: (2, i, 0)), tile, pl.BlockSpec((D, D), lambda i: (0, 0)),
                   pl.BlockSpec((3, 8, D), lambda i: (0, 0, 0))),
        scratch_shapes=[VMEM((TM, D), F32), ring, ring, VMEM((D, D), F32)],
        input_output_aliases={0: 0},
        compiler_params=_cp(("arbitrary",)),
    )(dp, dya, yc, p, lg, lb, wq)


def _branch_a_bwd2(dp, dyc, p, cw, gw_co, gw_so, gw_o, early):
    T = p.shape[0]
    nt = T // TM
    hb = TM // HALO

    def body(dp_in, dyc_ref, hdyc_ref, val_ref, glu_ref, cw_ref, gco_ref, gso_ref, go_ref, e_ref,
             dp_ref, gcw_ref, rq_ref, eg_ref, a_s, sd_ref, da_s, x_send, x_recv, x_loc, ag_send, ag_recv, ag_loc):
        del dp_in
        i = pl.program_id(0)
        start_exchange, wait_exchange = _owner_exchange((gco_ref, gso_ref, go_ref), rq_ref, x_send, x_recv, x_loc)
        ag_start, ag_relay, ag_finish = _allgather_phases([e_ref], lambda a, slot: eg_ref.at[slot],
                                                          ag_send, ag_recv, ag_loc)

        @pl.when(i == 0)
        def _():
            gcw_ref[...] = jnp.zeros_like(gcw_ref)
            ag_start()
            start_exchange()

        @pl.when(i == nt // 2)
        def _():
            ag_relay()

        def fill(rb, carry):
            rows = _rows(rb)
            a_s[rows, :] = _ld(val_ref, rows) * _sig(_ld(glu_ref, rows))
            return carry
        lax.fori_loop(0, TM // RB, fill, 0)
        sd_ref[0, 0:TM, :] = dyc_ref[...]
        sd_ref[0, TM:EXT, :] = jnp.where(i < nt - 1, hdyc_ref[...], 0.0)
        _shift_copies(sd_ref)

        def conv_t(rb, carry):
            r0 = rb * RBC
            accs = [jnp.zeros((8, D), F32)] * (RBC // 8)
            for k in range(KW):
                o = KW - 1 - k
                w = cw_ref[k]
                accs = [acc + w * sd_ref[o % 8, pl.ds(pl.multiple_of(r0 + 8 * (o // 8 + g), 8), 8), :]
                        for g, acc in enumerate(accs)]
            for g, acc in enumerate(accs):
                da_s[pl.ds(pl.multiple_of(r0 + 8 * g, 8), 8), :] = acc
            return carry
        lax.fori_loop(0, TM // RBC, conv_t, 0)

        for k0 in range(0, KW, TAPG):
            taps = list(range(k0, min(k0 + TAPG, KW)))

            def tap_group(rb, accs, taps=taps):
                for u in range(2):
                    r0 = rb * 32 + 16 * u
                    a = a_s[pl.ds(pl.multiple_of(r0, 16), 16), :]
                    out = []
                    for k, acc in zip(taps, accs):
                        o = KW - 1 - k
                        d = sd_ref[o % 8, pl.ds(pl.multiple_of(r0 + 8 * (o // 8), 8), 16), :]
                        out.append(acc + _fold8(a * d))
                    accs = tuple(out)
                return accs
            sums = lax.fori_loop(0, TM // 32, tap_group, tuple(jnp.zeros((8, D), F32) for _ in taps))
            for k, s in zip(taps, sums):
                gcw_ref[k] = gcw_ref[k] + s

        def glu_b(rb, carry):
            rows = _rows(rb)
            da = da_s[rows, :]
            sg = _sig(_ld(glu_ref, rows))
            dp_ref[0, rows, :] = (da * sg).astype(BF16)
            dp_ref[1, rows, :] = (da * _ld(val_ref, rows) * (sg * (1.0 - sg))).astype(BF16)
            return carry
        lax.fori_loop(0, TM // RB, glu_b, 0)

        @pl.when(i == nt - 1)
        def _():
            wait_exchange()
            ag_finish()

    tile = lambda seg: pl.BlockSpec((TM, D), lambda i: (i, seg))
    hbm = pl.BlockSpec(memory_space=ANY)
    return pl.pallas_call(
        body, name="branch_a_bwd2", grid=(nt,),
        out_shape=(jax.ShapeDtypeStruct((NSEG, T, D), BF16), jax.ShapeDtypeStruct((32, 8, D), F32),
                   jax.ShapeDtypeStruct((NDEV, 3, 128, D), BF16), jax.ShapeDtypeStruct((NDEV, NEARLY, D), F32)),
        in_specs=[hbm, pl.BlockSpec((TM, D), lambda i: (i, 0)),
                  pl.BlockSpec((HALO, D), lambda i: (jnp.minimum((i + 1) * hb, nt * hb - 1), 0)),
                  tile(0), tile(1), pl.BlockSpec((KW, 8, D), lambda i: (0, 0, 0)),
                  hbm, hbm, hbm, hbm],
        out_specs=(pl.BlockSpec((2, TM, D), lambda i: (0, i, 0)), pl.BlockSpec((32, 8, D), lambda i: (0, 0, 0)), hbm, hbm),
        scratch_shapes=[VMEM((TM, D), F32), VMEM((8, EXT, D), F32), VMEM((TM, D), F32),
                        pltpu.SemaphoreType.DMA((3, 7)), pltpu.SemaphoreType.DMA((3, 7)), pltpu.SemaphoreType.DMA((3,)),
                        pltpu.SemaphoreType.DMA((1, 7)), pltpu.SemaphoreType.DMA((1, 7)), pltpu.SemaphoreType.DMA((1,))],
        input_output_aliases={0: 0},
        compiler_params=_cp(("arbitrary",)),
    )(dp, dyc, dyc, p, p, cw, gw_co, gw_so, gw_o, early)


def _in_bwd_dx(dp, wing, x, dx2, mod, g_pre, token):
    T = x.shape[0]
    nt = T // TM

    def body(dp_ref, w_ref, x_ref, dx2_ref, mod_ref, g_ref, token_ref, gx_ref, st_ref, dh_s):
        del token_ref
        i = pl.program_id(0)

        @pl.when(i == 0)
        def _():
            st_ref[...] = jnp.zeros_like(st_ref)

        dh_s[...] = lax.dot_general(dp_ref[0], w_ref[0], NT, preferred_element_type=F32)
        for j in range(1, NSEG):
            dh_s[...] += lax.dot_general(dp_ref[j], w_ref[j], NT, preferred_element_type=F32)

        def blk(rb, carry):
            d_sh, d_sc, g_g = carry
            rows = _rows(rb)
            xv = x_ref[rows, :]
            r = lax.rsqrt(jnp.mean(xv * xv, axis=-1, keepdims=True) + EPS)
            xn = xv * r
            g = g_ref[...]
            hpre = xn * g
            dh = dh_s[rows, :]
            dhp = dh * (1.0 + mod_ref[1:2, :])
            dxn = dhp * g
            gx_ref[rows, :] = dx2_ref[rows, :] + r * (dxn - xn * jnp.mean(dxn * xn, axis=-1, keepdims=True))
            return d_sh + _fold8(dh), d_sc + _fold8(dh * hpre), g_g + _fold8(dhp * xn)
        zero = jnp.zeros((8, D), F32)
        d_sh, d_sc, g_g = _blocks(TM // RB, blk, (zero, zero, zero))
        st_ref[0] = st_ref[0] + d_sh
        st_ref[1] = st_ref[1] + d_sc
        st_ref[2] = st_ref[2] + g_g

    tile = pl.BlockSpec((TM, D), lambda i: (i, 0))
    return pl.pallas_call(
        body, name="in_bwd_dx", grid=(nt,),
        out_shape=(jax.ShapeDtypeStruct((T, D), F32), jax.ShapeDtypeStruct((3, 8, D), F32)),
        in_specs=[pl.BlockSpec((NSEG, TM, D), lambda i: (0, i, 0)),
                  pl.BlockSpec((NSEG, D, D), lambda i: (0, 0, 0), pipeline_mode=pl.Buffered(1)),
                  tile, tile, pl.BlockSpec((3, D), lambda i: (0, 0)), pl.BlockSpec((1, D), lambda i: (0, 0)),
                  pl.BlockSpec((8, 128), lambda i: (0, 0))],
        out_specs=(tile, pl.BlockSpec((3, 8, D), lambda i: (0, 0, 0))),
        scratch_shapes=[VMEM((TM, D), F32)],
        compiler_params=_cp(("arbitrary",), vmem_mb=56),
    )(dp, wing, x, dx2, mod, g_pre, token)


RS_ORDER = (5, 4, 3, 2, 7, 6, 1, 0)
RS_HALF = D // 2


def _in_bwd_dw(ht, dp, order):
    T = ht.shape[1]
    tmw = min(4 * TMB, T)
    nb = T // tmw
    nu = 2 * NSEG

    def body(order_ref, ht_ref, dp_ref, csum_ref, g_ref, acc, sbuf, rsib, send_sems, recv_sems, out_sems):
        j = pl.program_id(0)
        i = pl.program_id(1)
        x, y, c = _me()
        sibling = (x, y, 1 - c)

        def to_sibling(s):
            return pltpu.make_async_remote_copy(src_ref=sbuf.at[s], dst_ref=rsib.at[s], send_sem=send_sems.at[s],
                                                recv_sem=recv_sems.at[s], device_id=sibling, device_id_type=MESH)

        def to_csum(t):
            cols = pl.ds((t % 2) * RS_HALF, RS_HALF)
            return pltpu.make_async_copy(sbuf.at[8 + t], csum_ref.at[t // 2, :, cols], out_sems.at[t])

        @pl.when(i == 0)
        def _():
            acc[...] = jnp.zeros_like(acc)
        acc[...] += jnp.dot(ht_ref[...], dp_ref[...], preferred_element_type=F32)

        for jj in range(nu):
            chip, half, mine = jj // 4, (jj % 4) // 2, jj % 2
            s = 2 * chip + half

            @pl.when((j == jj) & (i == nb - 1))
            def _(chip=chip, half=half, mine=mine, s=s):
                if not mine:
                    sbuf[s] = acc[...].astype(BF16)
                    to_sibling(s).start()
                elif chip < 3:
                    to_sibling(s).wait_recv()
                    sbuf[8 + s] = (acc[...] + rsib[s].astype(F32)).astype(BF16)
                    to_csum(s).start()
                else:
                    to_sibling(s).wait_recv()
                    g_ref[:, half * RS_HALF:(half + 1) * RS_HALF] = acc[...] + rsib[s].astype(F32)

        @pl.when((j == nu - 1) & (i == nb - 1))
        def _():
            for s in range(8):
                to_sibling(s).wait_send()
            for t in range(6):
                to_csum(t).wait()

    hbm = pl.BlockSpec(memory_space=ANY)
    return pl.pallas_call(
        body, name="in_bwd_dw",
        out_shape=(jax.ShapeDtypeStruct((3, D, D), BF16), jax.ShapeDtypeStruct((D, D), F32)),
        grid_spec=pltpu.PrefetchScalarGridSpec(
            num_scalar_prefetch=1, grid=(nu, nb),
            in_specs=[pl.BlockSpec((D, tmw), lambda j, i, o: (0, i)),
                      pl.BlockSpec((None, tmw, RS_HALF), lambda j, i, o: (o[2 * (j // 4) + j % 2], i, (j % 4) // 2))],
            out_specs=(hbm, pl.BlockSpec((D, D), lambda j, i, o: (0, 0))),
            scratch_shapes=[VMEM((D, RS_HALF), F32), VMEM((14, D, RS_HALF), BF16), VMEM((8, D, RS_HALF), BF16),
                            pltpu.SemaphoreType.DMA((8,)), pltpu.SemaphoreType.DMA((8,)), pltpu.SemaphoreType.DMA((6,))]),
        compiler_params=_cp(("arbitrary", "arbitrary"), vmem_mb=56),
    )(order, ht, dp)


def _gather_late(late):
    def body(sm_ref, smg_ref, ag_send, ag_recv, ag_loc):
        _allgather([sm_ref], lambda a, slot: smg_ref.at[slot], ag_send, ag_recv, ag_loc)

    hbm = pl.BlockSpec(memory_space=ANY)
    return pl.pallas_call(
        body, name="gather_late",
        out_shape=jax.ShapeDtypeStruct((NDEV, NLATE, D), F32),
        in_specs=[hbm], out_specs=hbm,
        scratch_shapes=[pltpu.SemaphoreType.DMA((1, 7)), pltpu.SemaphoreType.DMA((1, 7)), pltpu.SemaphoreType.DMA((1,))],
        compiler_params=_cp(vmem_mb=32),
    )(late)


RS_CHIPS = ((1, 0), (0, 1), (1, 1))


def _rs_peer(q):
    x, y, c = _me()
    fx, fy = RS_CHIPS[q]
    return (1 - x if fx else x, 1 - y if fy else y, c)


def _rs_start(csum):
    def body(csum_ref, land_ref, send_sems, recv_sems, csum_thru, land_thru, token):
        barrier = pltpu.get_barrier_semaphore()
        for q in range(3):
            pl.semaphore_signal(barrier, inc=1, device_id=_rs_peer(q), device_id_type=MESH)
        pl.semaphore_wait(barrier, 3)
        for q in range(3):
            pltpu.make_async_remote_copy(src_ref=csum_ref.at[q], dst_ref=land_ref.at[q], send_sem=send_sems.at[q],
                                         recv_sem=recv_sems.at[q], device_id=_rs_peer(q), device_id_type=MESH).start()
        token[...] = jnp.zeros_like(token)

    hbm = pl.BlockSpec(memory_space=pltpu.HBM)
    sem = pl.BlockSpec(memory_space=pltpu.SEMAPHORE)
    land = pltpu.with_memory_space_constraint(lax.empty(csum.shape, csum.dtype), pltpu.HBM)
    return pl.pallas_call(
        body, name="rs_start",
        out_shape=(pltpu.SemaphoreType.DMA((3,)), pltpu.SemaphoreType.DMA((3,)), pltpu.HBM(csum.shape, csum.dtype),
                   pltpu.HBM(csum.shape, csum.dtype), jax.ShapeDtypeStruct((8, 128), F32)),
        in_specs=(hbm, hbm), out_specs=(sem, sem, hbm, hbm, pl.BlockSpec(memory_space=VMEM)),
        input_output_aliases={0: 2, 1: 3},
        compiler_params=pltpu.CompilerParams(has_side_effects=pltpu.SideEffectType.DATAFLOW_SIDE_EFFECTING, collective_id=1),
    )(pltpu.with_memory_space_constraint(csum, pltpu.HBM), land)


def _rs_wait(send_sems, recv_sems, csum_thru, land_thru, after):
    def body(csum_ref, land_ref, send_sems, recv_sems, after_ref, csum_dead, got_ref):
        del after_ref, csum_dead, got_ref
        for q in range(3):
            cp = pltpu.make_async_remote_copy(src_ref=csum_ref.at[q], dst_ref=land_ref.at[q], send_sem=send_sems.at[q],
                                              recv_sem=recv_sems.at[q], device_id=_rs_peer(q), device_id_type=MESH)
            cp.wait_send()
            cp.wait_recv()

    hbm = pl.BlockSpec(memory_space=pltpu.HBM)
    sem = pl.BlockSpec(memory_space=pltpu.SEMAPHORE)
    return pl.pallas_call(
        body, name="rs_wait",
        out_shape=(pltpu.HBM(csum_thru.shape, csum_thru.dtype), pltpu.HBM(csum_thru.shape, csum_thru.dtype)),
        in_specs=(hbm, hbm, sem, sem, pl.BlockSpec(memory_space=ANY)), out_specs=(hbm, hbm),
        input_output_aliases={0: 0, 1: 1},
        compiler_params=pltpu.CompilerParams(has_side_effects=pltpu.SideEffectType.DATAFLOW_SIDE_EFFECTING),
    )(csum_thru, land_thru, send_sems, recv_sems, after)[1]


def _pack_early(st_a, st_b, st_m, gws, gbs):
    def body(sa_ref, sb_ref, sm_ref, gws_ref, gbs_ref, out_ref):
        out_ref[...] = jnp.zeros_like(out_ref)
        fold = lambda v: jnp.sum(v, axis=0, keepdims=True)
        out_ref[0:1, :] = fold(sm_ref[2])
        out_ref[1:2, :] = fold(sa_ref[2])
        out_ref[2:3, :] = fold(sa_ref[0])
        out_ref[3:4, :] = fold(sa_ref[1])
        out_ref[4:5, :] = fold(sb_ref[0])
        out_ref[5:6, :] = fold(sb_ref[1])
        out_ref[7:8, :] = fold(sm_ref[1])
        out_ref[8:9, :] = fold(sm_ref[0])
        tri = lax.broadcasted_iota(jnp.int32, (CHUNK, CHUNK), 0) >= lax.broadcasted_iota(jnp.int32, (CHUNK, CHUNK), 1)
        for h in range(HEADS):
            out_ref[16:16 + CHUNK, h * CHUNK:(h + 1) * CHUNK] = jnp.where(tri, gws_ref[h], 0.0)
            out_ref[6:7, h * CHUNK:(h + 1) * CHUNK] = fold(gbs_ref[h].T)

    vm = pl.BlockSpec(memory_space=VMEM)
    return pl.pallas_call(
        body, name="pack_early", out_shape=jax.ShapeDtypeStruct((NEARLY, D), F32),
        in_specs=[vm] * 5, out_specs=vm, compiler_params=_cp(vmem_mb=32),
    )(st_a, st_b, st_m, gws, gbs)


def _pack_late(st_x, gcw):
    def body(sx_ref, gcw_ref, out_ref):
        out_ref[...] = jnp.zeros_like(out_ref)
        fold = lambda v: jnp.sum(v, axis=0, keepdims=True)
        for r in range(3):
            out_ref[r:r + 1, :] = fold(sx_ref[r])
        for k in range(KW):
            out_ref[8 + k:9 + k, :] = fold(gcw_ref[k])

    vm = pl.BlockSpec(memory_space=VMEM)
    return pl.pallas_call(
        body, name="pack_late", out_shape=jax.ShapeDtypeStruct((NLATE, D), F32),
        in_specs=[vm] * 2, out_specs=vm, compiler_params=_cp(vmem_mb=32),
    )(st_x, gcw)


def _pack_params(name, b_ada, g_pre, conv_b, lg, lb, sg, sb, b_sgu, g_final, w_sgu_t):
    def body(ba_ref, *refs):
        rows, ws_ref, out_ref = refs[:8], refs[8], refs[9]
        out_ref[...] = jnp.zeros_like(out_ref)
        for r in range(3):
            out_ref[r:r + 1, :] = ba_ref[:, r * D:(r + 1) * D]
        for r, ref in enumerate(rows):
            out_ref[3 + r:4 + r, :] = ref[...]
        out_ref[ROW_WS:ROW_WS + CHUNK, :] = ws_ref[...]

    vm = pl.BlockSpec(memory_space=VMEM)
    return pl.pallas_call(
        body, name=name, out_shape=jax.ShapeDtypeStruct((NSMALL, D), F32),
        in_specs=[vm] * 10, out_specs=vm, compiler_params=_cp(vmem_mb=32),
    )(b_ada, g_pre, conv_b, lg, lb, sg, sb, b_sgu, g_final, w_sgu_t)


def _adam(w, g, m, v):
    m2 = ADAM_B1 * m + (1.0 - ADAM_B1) * g
    v2 = ADAM_B2 * v + (1.0 - ADAM_B2) * (g * g)
    m_hat = m2 / (1.0 - ADAM_B1 ** ADAM_STEP)
    v_hat = v2 / (1.0 - ADAM_B2 ** ADAM_STEP)
    delta = -ADAM_LR * (m_hat / (jnp.sqrt(v_hat) + ADAM_EPS) + ADAM_WD * w)
    return delta, m2, v2


def _small_finish(eg, lg, w, m, v):
    def body(eg_ref, lg_ref, w_ref, m_ref, v_ref, loss_ref, g_ref, d_ref, m2_ref, v2_ref):
        e = eg_ref[0]
        l = lg_ref[0]
        for s in range(1, NDEV):
            e = e + eg_ref[s]
            l = l + lg_ref[s]
        g_ref[...] = jnp.zeros_like(g_ref)
        g_ref[0:2, :] = l[0:2, :]
        g_ref[2:3, :] = e[0:1, :]
        g_ref[3:4, :] = l[2:3, :]
        g_ref[4:12, :] = e[1:9, :]
        g_ref[ROW_CW:ROW_CW + KW, :] = l[8:8 + KW, :]
        g_ref[ROW_WS:ROW_WS + CHUNK, :] = e[16:16 + CHUNK, :]
        loss_ref[...] = jnp.sum(e[8:9, :], axis=1, keepdims=True)
        d_ref[...], m2_ref[...], v2_ref[...] = _adam(w_ref[...], g_ref[...], m_ref[...], v_ref[...])

    vm = pl.BlockSpec(memory_space=VMEM)
    sd = jax.ShapeDtypeStruct((NSMALL, D), F32)
    return pl.pallas_call(
        body, name="small_finish", out_shape=(jax.ShapeDtypeStruct((1, 1), F32), sd, sd, sd, sd),
        in_specs=[vm] * 5, out_specs=(vm,) * 5, compiler_params=_cp(vmem_mb=32),
    )(eg, lg, w, m, v)


def _ada_grad_adam(ct, dm, w, m, v):
    def body(ct_ref, dm_ref, w_ref, m_ref, v_ref, g_ref, d_ref, m2_ref, v2_ref):
        g = ct_ref[:, 0:1] * dm_ref[0:1, :]
        for b in range(1, NDEV):
            g = g + ct_ref[:, b:b + 1] * dm_ref[b:b + 1, :]
        g_ref[...] = g
        d_ref[...], m2_ref[...], v2_ref[...] = _adam(w_ref[...], g, m_ref[...], v_ref[...])

    vm = pl.BlockSpec(memory_space=VMEM)
    sd = jax.ShapeDtypeStruct(w.shape, F32)
    return pl.pallas_call(
        body, name="ada_grad_adam", out_shape=(sd, sd, sd, sd),
        in_specs=[vm] * 5, out_specs=(vm,) * 4, compiler_params=_cp(vmem_mb=32),
    )(ct, dm, w, m, v)


def _adam_f32(name, g, w, m, v, rows=None):
    R, C = w.shape
    rows = R if rows is None else rows

    def body(g_ref, w_ref, m_ref, v_ref, d_ref, m2_ref, v2_ref):
        d_ref[...], m2_ref[...], v2_ref[...] = _adam(w_ref[...], g_ref[...], m_ref[...], v_ref[...])

    tile = pl.BlockSpec((rows, C), lambda i: (i, 0))
    sd = jax.ShapeDtypeStruct(w.shape, F32)
    return pl.pallas_call(
        body, name=name, grid=(R // rows,), out_shape=(sd, sd, sd), in_specs=[tile] * 4, out_specs=(tile,) * 3,
        compiler_params=_cp(("parallel",), vmem_mb=32),
    )(g, w, m, v)


def _adam_w_in(gown, got, w, m, v, rows=256):
    R, C = w.shape

    def body(go_ref, got_ref, w_ref, m_ref, v_ref, g_ref, d_ref, m2_ref, v2_ref):
        g = go_ref[...]
        for q in range(3):
            g = g + got_ref[q].astype(F32)
        g_ref[...] = g
        d_ref[...], m2_ref[...], v2_ref[...] = _adam(w_ref[...], g, m_ref[...], v_ref[...])

    tile = pl.BlockSpec((rows, C), lambda i: (i, 0))
    sd = jax.ShapeDtypeStruct((R, C), F32)
    return pl.pallas_call(
        body, name="adam_w_in", grid=(R // rows,), out_shape=(sd, sd, sd, sd),
        in_specs=[tile, pl.BlockSpec((3, rows, C), lambda i: (0, i, 0)), tile, tile, tile], out_specs=(tile,) * 4,
        compiler_params=_cp(("parallel",), vmem_mb=32),
    )(gown, got, w, m, v)


def _adam_reduce(name, recv, w, m, v, rows, recv_spec):
    R, C = w.shape

    def body(r_ref, w_ref, m_ref, v_ref, g_ref, d_ref, m2_ref, v2_ref):
        g = r_ref[0].astype(F32)
        for s in range(1, NDEV):
            g = g + r_ref[s].astype(F32)
        g_ref[...] = g
        d_ref[...], m2_ref[...], v2_ref[...] = _adam(w_ref[...], g, m_ref[...], v_ref[...])

    tile = pl.BlockSpec((rows, C), lambda i: (i, 0))
    sd = jax.ShapeDtypeStruct((R, C), F32)
    return pl.pallas_call(
        body, name=name, grid=(R // rows,), out_shape=(sd, sd, sd, sd),
        in_specs=[recv_spec, tile, tile, tile], out_specs=(tile,) * 4,
        compiler_params=_cp(("parallel",)),
    )(recv, w, m, v)


def kernel(x, c, w_ada, b_ada, g_pre, w_in, conv_w, conv_b, conv_ln_g, conv_ln_b, w_conv_out, sgu_ln_g, sgu_ln_b, w_sgu, b_sgu, w_sgu_out, w_o, g_final, loss_target, m_w_ada, m_b_ada, m_g_pre, m_w_in, m_conv_w, m_conv_b, m_conv_ln_g, m_conv_ln_b, m_w_conv_out, m_sgu_ln_g, m_sgu_ln_b, m_w_sgu, m_b_sgu, m_w_sgu_out, m_w_o, m_g_final, v_w_ada, v_b_ada, v_g_pre, v_w_in, v_conv_w, v_conv_b, v_conv_ln_g, v_conv_ln_b, v_w_conv_out, v_sgu_ln_g, v_sgu_ln_b, v_w_sgu, v_b_sgu, v_w_sgu_out, v_w_o, v_g_final):
    T = x.shape[1]
    assert T % TMB == 0 and x.shape[2] == D
    xs = x[0]
    tgt = loss_target[0]
    my = 4 * lax.axis_index("x") + 2 * lax.axis_index("y") + lax.axis_index("c")

    mod, cg = _ada_fwd(c, w_ada[0], b_ada)
    gf = g_final.reshape(1, D)
    bst = b_sgu[0].T

    h, ht = _prep_h(xs, mod, g_pre)
    ag_order = jnp.stack([_slot(*b) for b in _ag_blocks(lax.axis_index("x"), lax.axis_index("y"), lax.axis_index("c"))])
    p, wing, wcog, cwg, own2, land2 = _in_proj(h, w_in[0], ag_order.astype(jnp.int32),
                                               w_conv_out[0], w_sgu_out[0], w_o[0], conv_w[0])
    wco = wcog.reshape(1, D, D)
    cw = jnp.broadcast_to(jnp.transpose(cwg, (1, 0, 2)).reshape(KW, 1, D), (KW, 8, D))
    wq_send, wq_recv, own2_thru, land2_thru, wq_token = _wq_start(own2, land2)
    yc, ya = _branch_a_fwd(p, cw, conv_b, conv_ln_g, conv_ln_b, wco, wq_token)
    wq2 = _wq_wait(wq_send, wq_recv, own2_thru, land2_thru, yc).reshape(2, D, D)
    yb = _branch_b_fwd(p, w_sgu[0], bst, sgu_ln_g, sgu_ln_b, wq2)
    dx2, dya, dyb, dp, gw_o, st_m = _merge_loss(p, ya, yb, xs, tgt, mod, gf, wq2)

    dp, gw_so, gws, gbs, st_b = _branch_b_bwd(dp, dyb, p, w_sgu[0], bst, sgu_ln_g, sgu_ln_b, wq2)
    dp, dyc, gw_co, st_a = _branch_a_bwd1(dp, dya, yc, p, conv_ln_g, conv_ln_b, wco)
    early = _pack_early(st_a, st_b, st_m, gws, gbs)
    dp, gcw, rq, eg = _branch_a_bwd2(dp, dyc, p, cw, gw_co, gw_so, gw_o, early)
    csum, gown = _in_bwd_dw(ht, dp, my ^ jnp.array(RS_ORDER, jnp.int32))
    rs_send, rs_recv, csum_thru, land_thru, token = _rs_start(csum)
    grad_x, st_x = _in_bwd_dx(dp, wing, xs, dx2, mod, g_pre, token)
    got = _rs_wait(rs_send, rs_recv, csum_thru, land_thru, st_x)
    lg = _gather_late(_pack_late(st_x, gcw))

    def pack(name, b_ada_, g_pre_, conv_b_, lg_, lb_, sg_, sb_, b_sgu_, gfin_, w_sgu_):
        return _pack_params(name, b_ada_, g_pre_, conv_b_, lg_, lb_, sg_, sb_, b_sgu_.reshape(1, D), gfin_.reshape(1, D),
                            jnp.transpose(w_sgu_[0], (1, 0, 2)).reshape(CHUNK, D))

    pw = pack("pack_w", b_ada, g_pre, conv_b, conv_ln_g, conv_ln_b, sgu_ln_g, sgu_ln_b, b_sgu, g_final, w_sgu)
    pm = pack("pack_m", m_b_ada, m_g_pre, m_conv_b, m_conv_ln_g, m_conv_ln_b, m_sgu_ln_g, m_sgu_ln_b, m_b_sgu, m_g_final,
              m_w_sgu)
    pv = pack("pack_v", v_b_ada, v_g_pre, v_conv_b, v_conv_ln_g, v_conv_ln_b, v_sgu_ln_g, v_sgu_ln_b, v_b_sgu, v_g_final,
              v_w_sgu)
    loss11, sg_, sd_, sm_, sv_ = _small_finish(eg, lg, pw, pm, pv)

    def unpack(a):
        return dict(
            b_ada=a[0:3].reshape(1, 3 * D), g_pre=a[3:4], conv_b=a[4:5], conv_ln_g=a[5:6], conv_ln_b=a[6:7],
            sgu_ln_g=a[7:8], sgu_ln_b=a[8:9], b_sgu=a[9:10].reshape(1, HEADS, CHUNK), g_final=a[10],
            w_sgu=jnp.transpose(a[ROW_WS:ROW_WS + CHUNK].reshape(CHUNK, HEADS, CHUNK), (1, 0, 2))[None])
    small = [unpack(a) for a in (sg_, sd_, sm_, sv_)]

    g_cw = lax.dynamic_slice_in_dim(sg_[ROW_CW:ROW_CW + KW], my * 128, 128, axis=1)
    d_cw, m_cw, v_cw = _adam_f32("adam_conv_w", g_cw, conv_w[0], m_conv_w[0], v_conv_w[0])

    wcols = w_ada.shape[2]
    dm_all = jnp.concatenate([lg[:, 0], lg[:, 1], eg[:, 0]], axis=1)
    dm_mine = lax.dynamic_slice_in_dim(dm_all, my * wcols, wcols, axis=1)
    g_ada, d_ada, m_ada, v_ada = _ada_grad_adam(cg.T, dm_mine, w_ada[0], m_w_ada[0], v_w_ada[0])

    g_in, d_in, m_in, v_in = _adam_w_in(gown, got, w_in[0], m_w_in[0], v_w_in[0])
    big = {}
    for j, (nm, w_, m_, v_) in enumerate((("w_conv_out", w_conv_out, m_w_conv_out, v_w_conv_out),
                                          ("w_sgu_out", w_sgu_out, m_w_sgu_out, v_w_sgu_out),
                                          ("w_o", w_o, m_w_o, v_w_o))):
        big[nm] = _adam_reduce("adam_" + nm, rq, w_[0], m_[0], v_[0], 128,
                               pl.BlockSpec((NDEV, None, 128, D), lambda i, j=j: (0, j, 0, 0)))

    per = {
        "w_ada": tuple(a[None] for a in (g_ada, d_ada, m_ada, v_ada)),
        "w_in": tuple(a[None] for a in (g_in, d_in, m_in, v_in)),
        "conv_w": tuple(a[None] for a in (g_cw, d_cw, m_cw, v_cw)),
    }
    for nm in ("w_conv_out", "w_sgu_out", "w_o"):
        per[nm] = tuple(a[None] for a in big[nm])
    for nm in ("b_ada", "g_pre", "conv_b", "conv_ln_g", "conv_ln_b", "sgu_ln_g", "sgu_ln_b", "w_sgu", "b_sgu", "g_final"):
        per[nm] = tuple(s[nm] for s in small)

    order = ["w_ada", "b_ada", "g_pre", "w_in", "conv_w", "conv_b", "conv_ln_g", "conv_ln_b", "w_conv_out",
             "sgu_ln_g", "sgu_ln_b", "w_sgu", "b_sgu", "w_sgu_out", "w_o", "g_final"]
    outs = [loss11.reshape(()), grad_x[None]]
    for part in range(4):
        outs += [per[nm][part] for nm in order]
    return tuple(outs)
```

```python
import jax
import jax.numpy as jnp
from jax import lax
from jax.experimental import pallas as pl
from jax.experimental.pallas import tpu as pltpu

F32 = jnp.float32
BF16 = jnp.bfloat16
MESH = pl.DeviceIdType.MESH
VMEM = pltpu.VMEM
ANY = pl.ANY

D = 1024
NDEV = 8
NSEG = 8
HEADS = 8
HD = 128
CHUNK = 128
KW = 31
HALO = 32
EPS = 1e-6
TM = 256
TMB = 512
RB = 32
UNROLL = 4
RBC = 32
TAPG = 4
EXT = TM + HALO
NSMALL = 176
NEARLY = 144
NLATE = 40
ROW_CW = 16
ROW_WS = 48
W_CO = 0
W_SO, W_O = 0, 1

ADAM_LR = 0.001
ADAM_B1 = 0.9
ADAM_B2 = 0.999
ADAM_EPS = 1e-08
ADAM_WD = 0.01
ADAM_STEP = 10

INV_SQRT2 = 0.7071067811865476
INV_SQRT_2PI = 0.3989422804014327

NT = (((1,), (1,)), ((), ()))
TN = (((0,), (0,)), ((), ()))


def _cp(sem=None, vmem_mb=48):
    return pltpu.CompilerParams(dimension_semantics=sem, vmem_limit_bytes=vmem_mb * 1024 * 1024)


def _me():
    return lax.axis_index("x"), lax.axis_index("y"), lax.axis_index("c")


def _slot(px, py, pc):
    return 4 * px + 2 * py + pc


def _xor_peer(k):
    x, y, c = _me()
    return (1 - x if k & 4 else x, 1 - y if k & 2 else y, 1 - c if k & 1 else c)


def _allgather_phases(srcs, dst_at, send_sems, recv_sems, loc_sems):
    x, y, c = _me()
    me = (x, y, c)
    sibling = (x, y, 1 - c)
    chips = [(1 - x, y), (x, 1 - y), (1 - x, 1 - y)]
    na = len(srcs)

    def copy(a, k, block, to, src=None):
        d = dst_at(a, _slot(*block))
        return pltpu.make_async_remote_copy(
            src_ref=d if src is None else src, dst_ref=d,
            send_sem=send_sems.at[a, k], recv_sem=recv_sems.at[a, k],
            device_id=to, device_id_type=MESH)

    local = [pltpu.make_async_copy(srcs[a], dst_at(a, _slot(*me)), loc_sems.at[a]) for a in range(na)]
    first = []
    for a in range(na):
        first.append(copy(a, 0, me, sibling, src=srcs[a]))
        for j, chip in enumerate(chips):
            first.append(copy(a, 1 + j, me, (*chip, c), src=srcs[a]))
    passed = [copy(a, 4 + j, (*chip, c), sibling) for j, chip in enumerate(chips) for a in range(na)]

    def start():
        for cp in local + first:
            cp.start()

    def relay():
        for j, chip in enumerate(chips):
            for a in range(na):
                copy(a, 1 + j, (*chip, c), me).wait_recv()
                passed[j * na + a].start()

    def finish():
        for a in range(na):
            copy(a, 0, sibling, me).wait_recv()
        for j, chip in enumerate(chips):
            for a in range(na):
                copy(a, 4 + j, (*chip, 1 - c), me).wait_recv()
        for cp in first + passed:
            cp.wait_send()
        for cp in local:
            cp.wait()

    return start, relay, finish


def _allgather(srcs, dst_at, send_sems, recv_sems, loc_sems):
    start, relay, finish = _allgather_phases(srcs, dst_at, send_sems, recv_sems, loc_sems)
    start()
    relay()
    finish()


def _owner_exchange(gq, rq_ref, x_send, x_recv, x_loc):
    mx, my_, mc = _me()
    me = _slot(mx, my_, mc)

    def rows_of(a, slot):
        return gq[a].at[pl.ds(pl.multiple_of(slot * 128, 128), 128), :]

    def exchange(k, a, recv):
        px, py, pc = _xor_peer(k)
        peer = _slot(px, py, pc)
        return pltpu.make_async_remote_copy(
            src_ref=rows_of(a, me if recv else peer), dst_ref=rq_ref.at[peer if recv else me, a],
            send_sem=x_send.at[a, k - 1], recv_sem=x_recv.at[a, k - 1],
            device_id=(px, py, pc), device_id_type=MESH)

    local = [pltpu.make_async_copy(rows_of(a, me), rq_ref.at[me, a], x_loc.at[a]) for a in range(len(gq))]

    def start():
        for cp in local:
            cp.start()
        for k in range(1, NDEV):
            for a in range(len(gq)):
                exchange(k, a, False).start()

    def wait():
        for k in range(1, NDEV):
            for a in range(len(gq)):
                exchange(k, a, True).wait_recv()
        for k in range(1, NDEV):
            for a in range(len(gq)):
                exchange(k, a, False).wait_send()
        for cp in local:
            cp.wait()

    return start, wait


def _sig(v):
    return jax.nn.sigmoid(v)


def _gelu_parts(v):
    cdf = 0.5 * (1.0 + lax.erf(v * INV_SQRT2))
    pdf = jnp.exp(-0.5 * v * v) * INV_SQRT_2PI
    return v * cdf, cdf + v * pdf


def _ln_stats(v):
    mu = jnp.mean(v, axis=-1, keepdims=True)
    xc = v - mu
    var = jnp.mean(xc * xc, axis=-1, keepdims=True)
    rs = lax.rsqrt(var + EPS)
    return xc * rs, rs


def _ln_bwd(dn, n, rs):
    return rs * (dn - jnp.mean(dn, axis=-1, keepdims=True) - n * jnp.mean(dn * n, axis=-1, keepdims=True))


def _fold8(v):
    acc = v[0:8]
    for r in range(8, v.shape[0], 8):
        acc = acc + v[r:r + 8]
    return acc


def _rows(rb, n=RB):
    return pl.ds(pl.multiple_of(rb * n, n), n)


def _ring_tiles(nt):
    return 4 if nt % 4 == 0 else (2 if nt % 2 == 0 else 1)


def _ring_base(i, nt):
    return pl.multiple_of((i % _ring_tiles(nt)) * TM, TM)


def _ring_rows(base, rb, n=RB):
    return pl.ds(pl.multiple_of(base + rb * n, n), n)


def _ring_full(i, nt):
    g = _ring_tiles(nt)
    return i % g == g - 1


def _ld(ref, rows):
    return ref[rows, :].astype(F32)


def _blocks(n, body, init=0, unroll=UNROLL):
    def trip(t, carry):
        for u in range(unroll):
            carry = body(t * unroll + u, carry)
        return carry
    return lax.fori_loop(0, n // unroll, trip, init)


def _shift_copies(s_ref):
    n = EXT - 8
    for m in range(1, 8):
        for c0 in range(0, n, 56):
            s_ref[m, c0:c0 + 56, :] = s_ref[0, c0 + m:c0 + m + 56, :]


def _ada_exchange(c_ref, w_ref, b_ref, mod_ref, call_ref, cdst, msrc, mdst, c_send, c_recv, m_send, m_recv):
    x, y, c = _me()
    me = _slot(x, y, c)

    def all_to_all(src_of, dst, ss, rs):
        def cp(k, recv):
            px, py, pc = _xor_peer(k)
            peer = _slot(px, py, pc)
            return pltpu.make_async_remote_copy(
                src_ref=src_of(me if recv else peer), dst_ref=dst.at[peer if recv else me],
                send_sem=ss.at[k - 1], recv_sem=rs.at[k - 1], device_id=(px, py, pc), device_id_type=MESH)
        for k in range(1, NDEV):
            cp(k, False).start()
        for k in range(1, NDEV):
            cp(k, True).wait_recv()
        for k in range(1, NDEV):
            cp(k, False).wait_send()

    cdst[me] = c_ref[...]
    all_to_all(lambda s: c_ref, cdst, c_send, c_recv)
    for b in range(NDEV):
        call_ref[b:b + 1, :] = cdst[b]
    m = jnp.dot(call_ref[...], w_ref[...], preferred_element_type=F32, precision=lax.Precision.HIGHEST)
    for b in range(NDEV):
        msrc[b] = m[b:b + 1, :]
    mdst[me] = msrc[me]
    all_to_all(lambda s: msrc.at[s], mdst, m_send, m_recv)
    full = jnp.concatenate([mdst[k] for k in range(NDEV)], axis=1) + b_ref[...]
    for r in range(3):
        mod_ref[r:r + 1, :] = full[:, r * D:(r + 1) * D]


def _ada_fwd(c, w_ada, b_ada):
    wcols = w_ada.shape[1]

    def body(*refs):
        _ada_exchange(*refs)

    vm = pl.BlockSpec(memory_space=VMEM)
    return pl.pallas_call(
        body, name="ada_fwd",
        out_shape=(jax.ShapeDtypeStruct((3, D), F32), jax.ShapeDtypeStruct((NDEV, D), F32)),
        in_specs=[vm, vm, vm], out_specs=(vm, vm),
        scratch_shapes=[VMEM((NDEV, 1, D), F32), VMEM((NDEV, 1, wcols), F32), VMEM((NDEV, 1, wcols), F32),
                        pltpu.SemaphoreType.DMA((7,)), pltpu.SemaphoreType.DMA((7,)),
                        pltpu.SemaphoreType.DMA((7,)), pltpu.SemaphoreType.DMA((7,))],
        compiler_params=_cp(vmem_mb=32),
    )(c, w_ada, b_ada)


def _prep_h(x, mod, g_pre):
    T = x.shape[0]

    def body(x_ref, mod_ref, g_ref, h_ref, ht_ref):
        def blk(rb, carry):
            rows = _rows(rb)
            xv = x_ref[rows, :]
            r = lax.rsqrt(jnp.mean(xv * xv, axis=-1, keepdims=True) + EPS)
            h_ref[rows, :] = ((xv * r) * g_ref[...] * (1.0 + mod_ref[1:2, :]) + mod_ref[0:1, :]).astype(BF16)
            return carry
        _blocks(TMB // RB, blk)
        ht_ref[...] = h_ref[...].T

    return pl.pallas_call(
        body, name="prep_h", grid=(T // TMB,),
        out_shape=(jax.ShapeDtypeStruct((T, D), BF16), jax.ShapeDtypeStruct((D, T), BF16)),
        in_specs=[pl.BlockSpec((TMB, D), lambda i: (i, 0)), pl.BlockSpec((3, D), lambda i: (0, 0)),
                  pl.BlockSpec((1, D), lambda i: (0, 0))],
        out_specs=(pl.BlockSpec((TMB, D), lambda i: (i, 0)), pl.BlockSpec((D, TMB), lambda i: (0, i))),
        compiler_params=_cp(("parallel",)),
    )(x, mod, g_pre)


def _ag_blocks(x, y, c):
    a = 1 - c
    n1 = (x ^ a, y ^ c)
    n2 = (x ^ c, y ^ a)
    dg = (1 - x, 1 - y)
    return [(x, y, c), (x, y, a), (*n1, c), (*n2, c), (*n1, a), (*n2, a), (*dg, c), (*dg, a)]


def _in_proj(h, w_in, order, w_co, w_so, w_o, conv_w):
    T = h.shape[0]
    tmp = min(2 * TMB, T)
    nb = T // tmp

    def body(order_ref, h_ref, w_ref, wco_ref, wso_ref, wo_ref, cw_ref, p_ref, wing_ref, wqg_ref, cwg_ref, own2_ref, land2_ref,
             wbuf, wq_st, wq2_st, send_sems, recv_sems, out_sems, ag_send, ag_recv, ag_loc, own_sems):
        j = pl.program_id(0)
        i = pl.program_id(1)
        blocks = _ag_blocks(*_me())
        me, sibling, n1, n2, n1o, n2o, dg, dgo = blocks
        own2 = [pltpu.make_async_copy(wq2_st, own2_ref, own_sems.at[0]),
                pltpu.make_async_copy(wq2_st, land2_ref.at[:, _slot(*me)], own_sems.at[1])]
        ag_start, ag_relay, ag_finish = _allgather_phases(
            [wq_st, cw_ref], lambda a, slot: wqg_ref.at[:, slot] if a == 0 else cwg_ref.at[slot], ag_send, ag_recv, ag_loc)

        def copy(k, block, to):
            d = wbuf.at[_slot(*block)]
            return pltpu.make_async_remote_copy(src_ref=d, dst_ref=d, send_sem=send_sems.at[k], recv_sem=recv_sems.at[k],
                                                device_id=to, device_id_type=MESH)

        def writeback(jj):
            s = _slot(*blocks[jj])
            return pltpu.make_async_copy(wbuf.at[s], wing_ref.at[s], out_sems.at[jj])

        sends = [copy(0, me, sibling), copy(1, me, n1), copy(2, me, n2)]
        relay = copy(3, n1, n2)
        passed = [copy(4, n1, sibling), copy(5, n2, sibling), copy(6, dg, sibling)]
        arrivals = {1: [(0, sibling)], 2: [(1, n1), (2, n2)], 4: [(5, n1o)], 5: [(4, n2o)], 6: [(3, dg)], 7: [(6, dgo)]}

        @pl.when((j == 0) & (i == 0))
        def _():
            wbuf[_slot(*me)] = w_ref[...].astype(BF16)
            for cp in sends:
                cp.start()
            writeback(0).start()
            wq2_st[W_SO] = wso_ref[...].astype(BF16)
            wq2_st[W_O] = wo_ref[...].astype(BF16)
            for cp in own2:
                cp.start()

        for jj in range(1, NSEG):
            @pl.when((j == jj) & (i == 0))
            def _(jj=jj):
                for k, block in arrivals.get(jj, []):
                    copy(k, block, me).wait_recv()
                if jj == 2:
                    relay.start()
                    passed[0].start()
                    passed[1].start()
                    writeback(3).start()
                    wq_st[0] = wco_ref[...].astype(BF16)
                    ag_start()
                if jj == 6:
                    passed[2].start()
                if jj != 3:
                    writeback(jj).start()
                if jj == NSEG - 1:
                    ag_relay()

        p_ref[...] = jnp.dot(h_ref[...], wbuf[order_ref[j]], preferred_element_type=F32).astype(BF16)

        @pl.when((j == NSEG - 1) & (i == nb - 1))
        def _():
            for cp in sends + [relay] + passed:
                cp.wait_send()
            for jj in range(NSEG):
                writeback(jj).wait()
            for cp in own2:
                cp.wait()
            ag_finish()

    vm = pl.BlockSpec(memory_space=VMEM)
    hbm = pl.BlockSpec(memory_space=ANY)
    return pl.pallas_call(
        body, name="in_proj",
        out_shape=(jax.ShapeDtypeStruct((T, NSEG * D), BF16), jax.ShapeDtypeStruct((NSEG, D, D), BF16),
                   jax.ShapeDtypeStruct((1, NDEV, 128, D), BF16), jax.ShapeDtypeStruct((NDEV, KW, 128), F32),
                   jax.ShapeDtypeStruct((2, 128, D), BF16), jax.ShapeDtypeStruct((2, NDEV, 128, D), BF16)),
        grid_spec=pltpu.PrefetchScalarGridSpec(
            num_scalar_prefetch=1, grid=(NSEG, nb),
            in_specs=[pl.BlockSpec((tmp, D), lambda j, i, o: (i, 0)), vm, vm, vm, vm, vm],
            out_specs=(pl.BlockSpec((tmp, D), lambda j, i, o: (i, o[j])), hbm, hbm, hbm, hbm, hbm),
            scratch_shapes=[VMEM((NSEG, D, D), BF16), VMEM((1, 128, D), BF16), VMEM((2, 128, D), BF16),
                            pltpu.SemaphoreType.DMA((7,)), pltpu.SemaphoreType.DMA((7,)), pltpu.SemaphoreType.DMA((NSEG,)),
                            pltpu.SemaphoreType.DMA((2, 7)), pltpu.SemaphoreType.DMA((2, 7)), pltpu.SemaphoreType.DMA((2,)),
                            pltpu.SemaphoreType.DMA((2,))]),
        compiler_params=_cp(("arbitrary", "arbitrary")),
    )(order, h, w_in, w_co, w_so, w_o, conv_w)


def _wq_start(own2, land2):
    def body(src_ref, land_ref, send_sems, recv_sems, src_thru, land_thru, token):
        barrier = pltpu.get_barrier_semaphore()
        for k in range(1, NDEV):
            pl.semaphore_signal(barrier, inc=1, device_id=_xor_peer(k), device_id_type=MESH)
        pl.semaphore_wait(barrier, NDEV - 1)
        x, y, c = _me()
        for k in range(1, NDEV):
            pltpu.make_async_remote_copy(src_ref=src_ref, dst_ref=land_ref.at[:, _slot(x, y, c)], send_sem=send_sems.at[k - 1],
                                         recv_sem=recv_sems.at[k - 1], device_id=_xor_peer(k), device_id_type=MESH).start()
        token[...] = jnp.zeros_like(token)

    hbm = pl.BlockSpec(memory_space=pltpu.HBM)
    sem = pl.BlockSpec(memory_space=pltpu.SEMAPHORE)
    return pl.pallas_call(
        body, name="wq_start",
        out_shape=(pltpu.SemaphoreType.DMA((NDEV - 1,)), pltpu.SemaphoreType.DMA((NDEV - 1,)),
                   pltpu.HBM(own2.shape, own2.dtype), pltpu.HBM(land2.shape, land2.dtype), jax.ShapeDtypeStruct((8, 128), F32)),
        in_specs=(hbm, hbm), out_specs=(sem, sem, hbm, hbm, pl.BlockSpec(memory_space=VMEM)),
        input_output_aliases={0: 2, 1: 3},
        compiler_params=pltpu.CompilerParams(has_side_effects=pltpu.SideEffectType.DATAFLOW_SIDE_EFFECTING, collective_id=2),
    )(pltpu.with_memory_space_constraint(own2, pltpu.HBM), pltpu.with_memory_space_constraint(land2, pltpu.HBM))


def _wq_wait(send_sems, recv_sems, src_thru, land_thru, after):
    def body(src_ref, land_ref, send_sems, recv_sems, after_ref, src_dead, got_ref):
        del after_ref, src_dead, got_ref
        for k in range(1, NDEV):
            px, py, pc = _xor_peer(k)
            cp = pltpu.make_async_remote_copy(src_ref=src_ref, dst_ref=land_ref.at[:, _slot(px, py, pc)],
                                              send_sem=send_sems.at[k - 1], recv_sem=recv_sems.at[k - 1],
                                              device_id=(px, py, pc), device_id_type=MESH)
            cp.wait_send()
            cp.wait_recv()

    hbm = pl.BlockSpec(memory_space=pltpu.HBM)
    sem = pl.BlockSpec(memory_space=pltpu.SEMAPHORE)
    return pl.pallas_call(
        body, name="wq_wait",
        out_shape=(pltpu.HBM(src_thru.shape, src_thru.dtype), pltpu.HBM(land_thru.shape, land_thru.dtype)),
        in_specs=(hbm, hbm, sem, sem, pl.BlockSpec(memory_space=ANY)), out_specs=(hbm, hbm),
        input_output_aliases={0: 0, 1: 1},
        compiler_params=pltpu.CompilerParams(has_side_effects=pltpu.SideEffectType.DATAFLOW_SIDE_EFFECTING),
    )(src_thru, land_thru, send_sems, recv_sems, after)[1]


def _fill_a_ext(i, s_ref, val_ref, glu_ref, hval_ref, hglu_ref):
    ah = hval_ref[...].astype(F32) * _sig(hglu_ref[...].astype(F32))
    s_ref[0, 0:HALO, :] = jnp.where(i > 0, ah, 0.0)

    def fill(rb, carry):
        rows = _rows(rb)
        s_ref[0, pl.ds(pl.multiple_of(HALO + rb * RB, RB), RB), :] = _ld(val_ref, rows) * _sig(_ld(glu_ref, rows))
        return carry
    lax.fori_loop(0, TM // RB, fill, 0)


def _halo_prev(seg):
    hb = TM // HALO
    return pl.BlockSpec((HALO, D), lambda i: (jnp.maximum(i * hb - 1, 0), seg))


def _branch_a_fwd(p, cw, conv_b, lg, lb, wq, token):
    T = p.shape[0]

    def body(val_ref, glu_ref, z_ref, hval_ref, hglu_ref, cw_ref, cb_ref, lg_ref, lb_ref, w_ref, token_ref,
             yc_ref, ya_ref, s_ref, yp_ref):
        del token_ref
        i = pl.program_id(0)
        _fill_a_ext(i, s_ref, val_ref, glu_ref, hval_ref, hglu_ref)
        _shift_copies(s_ref)

        def conv(rb, carry):
            r0 = rb * RBC
            accs = [jnp.broadcast_to(cb_ref[...], (8, D))] * (RBC // 8)
            for k in range(KW):
                o = 2 + k
                w = cw_ref[k]
                accs = [acc + w * s_ref[o % 8, pl.ds(pl.multiple_of(r0 + 8 * (o // 8 + g), 8), 8), :]
                        for g, acc in enumerate(accs)]
            for g, acc in enumerate(accs):
                yc_ref[pl.ds(pl.multiple_of(r0 + 8 * g, 8), 8), :] = acc
            return carry
        lax.fori_loop(0, TM // RBC, conv, 0)

        def post(rb, carry):
            rows = _rows(rb)
            n, _ = _ln_stats(yc_ref[rows, :])
            l = n * lg_ref[...] + lb_ref[...]
            z = _ld(z_ref, rows)
            yp_ref[rows, :] = ((l * _sig(l)) * (z * _sig(z))).astype(BF16)
            return carry
        _blocks(TM // RB, post)
        ya_ref[...] = jnp.dot(yp_ref[...], w_ref[...], preferred_element_type=F32)

    tile = lambda seg: pl.BlockSpec((TM, D), lambda i: (i, seg))
    row = pl.BlockSpec((1, D), lambda i: (0, 0))
    return pl.pallas_call(
        body, name="branch_a_fwd", grid=(T // TM,),
        out_shape=(jax.ShapeDtypeStruct((T, D), F32), jax.ShapeDtypeStruct((T, D), F32)),
        in_specs=[tile(0), tile(1), tile(2), _halo_prev(0), _halo_prev(1),
                  pl.BlockSpec((KW, 8, D), lambda i: (0, 0, 0)), row, row, row,
                  pl.BlockSpec((None, D, D), lambda i: (W_CO, 0, 0)), pl.BlockSpec((8, 128), lambda i: (0, 0))],
        out_specs=(pl.BlockSpec((TM, D), lambda i: (i, 0)), pl.BlockSpec((TM, D), lambda i: (i, 0))),
        scratch_shapes=[VMEM((8, EXT, D), F32), VMEM((TM, D), BF16)],
        compiler_params=_cp(("parallel",)),
    )(p, p, p, p, p, cw, conv_b, lg, lb, wq, token)


def _masked_ws(ws_ref, wt_ref):
    tri = lax.broadcasted_iota(jnp.int32, (CHUNK, CHUNK), 0) >= lax.broadcasted_iota(jnp.int32, (CHUNK, CHUNK), 1)
    for h in range(HEADS):
        wt_ref[h] = jnp.where(tri, ws_ref[h], 0.0).astype(BF16)


def _sgu_mix(wt_ref, vl_ref, bst_ref, s_ref):
    for h in range(HEADS):
        cs = slice(h * HD, (h + 1) * HD)
        s = jnp.dot(wt_ref[h], _chunks_on_lanes(vl_ref, cs), preferred_element_type=F32) + bst_ref[:, h:h + 1]
        for ck in range(TM // CHUNK):
            s_ref[ck * CHUNK:(ck + 1) * CHUNK, cs] = s[:, ck * HD:(ck + 1) * HD]


def _chunks_on_lanes(ref, cs):
    return jnp.concatenate([ref[ck * CHUNK:(ck + 1) * CHUNK, cs] for ck in range(TM // CHUNK)], axis=1)


def _branch_b_fwd(p, ws, bst, sg, sb, wq):
    T = p.shape[0]

    def body(pu_ref, pv_ref, pz_ref, ws_ref, bst_ref, sg_ref, sb_ref, w_ref, yb_ref,
             wt_ref, vl_ref, t_ref, s_ref, yp_ref):
        _masked_ws(ws_ref, wt_ref)

        def pre(rb, carry):
            rows = _rows(rb)
            vg, _ = _gelu_parts(_ld(pv_ref, rows))
            vn, _ = _ln_stats(vg)
            vl_ref[rows, :] = (vn * sg_ref[...] + sb_ref[...]).astype(BF16)
            u, _ = _gelu_parts(_ld(pu_ref, rows))
            z = _ld(pz_ref, rows)
            t_ref[rows, :] = u * (z * _sig(z))
            return carry
        _blocks(TM // RB, pre)
        _sgu_mix(wt_ref, vl_ref, bst_ref, s_ref)

        def post(rb, carry):
            rows = _rows(rb)
            yp_ref[rows, :] = (t_ref[rows, :] * s_ref[rows, :]).astype(BF16)
            return carry
        _blocks(TM // RB, post)
        yb_ref[...] = jnp.dot(yp_ref[...], w_ref[...], preferred_element_type=F32)

    tile = lambda seg: pl.BlockSpec((TM, D), lambda i: (i, seg))
    row = pl.BlockSpec((1, D), lambda i: (0, 0))
    return pl.pallas_call(
        body, name="branch_b_fwd", grid=(T // TM,),
        out_shape=jax.ShapeDtypeStruct((T, D), F32),
        in_specs=[tile(3), tile(4), tile(5), pl.BlockSpec((HEADS, CHUNK, CHUNK), lambda i: (0, 0, 0)),
                  pl.BlockSpec((CHUNK, HEADS), lambda i: (0, 0)), row, row,
                  pl.BlockSpec((None, D, D), lambda i: (W_SO, 0, 0))],
        out_specs=pl.BlockSpec((TM, D), lambda i: (i, 0)),
        scratch_shapes=[VMEM((HEADS, CHUNK, CHUNK), BF16), VMEM((TM, D), BF16), VMEM((TM, D), F32),
                        VMEM((TM, D), F32), VMEM((TM, D), BF16)],
        compiler_params=_cp(("parallel",)),
    )(p, p, p, ws, bst, sg, sb, wq)


def _merge_loss(p, ya, yb, x, tgt, mod, g_final, wq):
    T = x.shape[0]
    nt = T // TM

    def body(ga_ref, gb_ref, ya_ref, yb_ref, x_ref, t_ref, mod_ref, gf_ref, w_ref,
             dx2_ref, dya_ref, dyb_ref, dp_ref, gwo_ref, st_ref,
             mrg_s, out_s, dout_s, dm_s, gw_acc):
        i = pl.program_id(0)
        base = _ring_base(i, nt)
        cur = pl.ds(base, TM)

        @pl.when(i == 0)
        def _():
            gw_acc[...] = jnp.zeros_like(gw_acc)
            st_ref[...] = jnp.zeros_like(st_ref)

        def merge(rb, carry):
            rows = _rows(rb)
            mrg_s[_ring_rows(base, rb), :] = (_sig(_ld(ga_ref, rows)) * ya_ref[rows, :]
                                              + _sig(_ld(gb_ref, rows)) * yb_ref[rows, :]).astype(BF16)
            return carry
        lax.fori_loop(0, TM // RB, merge, 0)
        out_s[...] = jnp.dot(mrg_s[cur, :], w_ref[...], preferred_element_type=F32)

        def head(rb, carry):
            loss, gg, dg = carry
            rows = _rows(rb)
            gate = mod_ref[2:3, :]
            gf = gf_ref[...]
            out = out_s[rows, :]
            x2 = x_ref[rows, :] + gate * out
            r2 = lax.rsqrt(jnp.mean(x2 * x2, axis=-1, keepdims=True) + EPS)
            x2n = x2 * r2
            diff = x2n * gf - t_ref[rows, :]
            dy = diff * (1.0 / D)
            dx2n = dy * gf
            dx2 = r2 * (dx2n - x2n * jnp.mean(dx2n * x2n, axis=-1, keepdims=True))
            dx2_ref[rows, :] = dx2
            dout_s[_ring_rows(base, rb), :] = (dx2 * gate).astype(BF16)
            return loss + _fold8(diff * diff), gg + _fold8(dy * x2n), dg + _fold8(dx2 * out)
        zero = jnp.zeros((8, D), F32)
        loss, gg, dg = _blocks(TM // RB, head, (zero, zero, zero))
        st_ref[0] = st_ref[0] + loss * (0.5 / D)
        st_ref[1] = st_ref[1] + gg
        st_ref[2] = st_ref[2] + dg

        dm_s[...] = lax.dot_general(dout_s[cur, :], w_ref[...], NT, preferred_element_type=F32)

        @pl.when(_ring_full(i, nt))
        def _():
            gw_acc[...] += lax.dot_general(mrg_s[...], dout_s[...], TN, preferred_element_type=F32)

        def split(rb, carry):
            rows = _rows(rb)
            dm = dm_s[rows, :]
            sa = _sig(_ld(ga_ref, rows))
            sb = _sig(_ld(gb_ref, rows))
            dya_ref[rows, :] = (dm * sa).astype(BF16)
            dyb_ref[rows, :] = (dm * sb).astype(BF16)
            dp_ref[0, rows, :] = (dm * ya_ref[rows, :] * (sa * (1.0 - sa))).astype(BF16)
            dp_ref[1, rows, :] = (dm * yb_ref[rows, :] * (sb * (1.0 - sb))).astype(BF16)
            return carry
        lax.fori_loop(0, TM // RB, split, 0)

        @pl.when(i == nt - 1)
        def _():
            gwo_ref[...] = gw_acc[...].astype(BF16)

    tile = pl.BlockSpec((TM, D), lambda i: (i, 0))
    ring = VMEM((_ring_tiles(nt) * TM, D), BF16)
    return pl.pallas_call(
        body, name="merge_loss", grid=(nt,),
        out_shape=(jax.ShapeDtypeStruct((T, D), F32), jax.ShapeDtypeStruct((T, D), BF16), jax.ShapeDtypeStruct((T, D), BF16),
                   jax.ShapeDtypeStruct((NSEG, T, D), BF16), jax.ShapeDtypeStruct((D, D), BF16),
                   jax.ShapeDtypeStruct((3, 8, D), F32)),
        in_specs=[pl.BlockSpec((TM, D), lambda i: (i, 6)), pl.BlockSpec((TM, D), lambda i: (i, 7)), tile, tile, tile, tile,
                  pl.BlockSpec((3, D), lambda i: (0, 0)), pl.BlockSpec((1, D), lambda i: (0, 0)),
                  pl.BlockSpec((None, D, D), lambda i: (W_O, 0, 0))],
        out_specs=(tile, tile, tile, pl.BlockSpec((2, TM, D), lambda i: (3, i, 0)),
                   pl.BlockSpec((D, D), lambda i: (0, 0)), pl.BlockSpec((3, 8, D), lambda i: (0, 0, 0))),
        scratch_shapes=[ring, VMEM((TM, D), F32), ring, VMEM((TM, D), F32), VMEM((D, D), F32)],
        compiler_params=_cp(("arbitrary",)),
    )(p, p, ya, yb, x, tgt, mod, g_final, wq)


def _branch_b_bwd(dp, dyb, p, ws, bst, sg, sb, wq):
    T = p.shape[0]
    nt = T // TM

    def body(dp_in, dyb_ref, pu_ref, pv_ref, pz_ref, ws_ref, bst_ref, sg_ref, sb_ref, w_ref,
             dp_ref, gw_ref, gws_ref, gbs_ref, st_ref,
             wt_ref, d_s, vl_s, vn_s, gpv_s, rs_s, s_s, ds_s, ds32_s, yp_s, dy_s, dvl_s, gw_acc):
        del dp_in
        i = pl.program_id(0)
        base = _ring_base(i, nt)

        @pl.when(i == 0)
        def _():
            gw_acc[...] = jnp.zeros_like(gw_acc)
            gws_ref[...] = jnp.zeros_like(gws_ref)
            gbs_ref[...] = jnp.zeros_like(gbs_ref)
            st_ref[...] = jnp.zeros_like(st_ref)
            _masked_ws(ws_ref, wt_ref)

        d_s[...] = lax.dot_general(dyb_ref[...], w_ref[...], NT, preferred_element_type=F32)
        dy_s[pl.ds(base, TM), :] = dyb_ref[...]

        def pre(rb, carry):
            rows = _rows(rb)
            vg, gpv = _gelu_parts(_ld(pv_ref, rows))
            vn, rs = _ln_stats(vg)
            vl_s[rows, :] = (vn * sg_ref[...] + sb_ref[...]).astype(BF16)
            vn_s[rows, :] = vn
            gpv_s[rows, :] = gpv
            rs_s[rows, :] = rs
            return carry
        _blocks(TM // RB, pre)
        _sgu_mix(wt_ref, vl_s, bst_ref, s_s)

        def mid(rb, carry):
            rows = _rows(rb)
            u, gpu = _gelu_parts(_ld(pu_ref, rows))
            z = _ld(pz_ref, rows)
            sz = _sig(z)
            siluz = z * sz
            d = d_s[rows, :]
            s = s_s[rows, :]
            t = u * siluz
            yp_s[_ring_rows(base, rb), :] = (t * s).astype(BF16)
            dp_ref[0, rows, :] = (d * s * siluz * gpu).astype(BF16)
            ds = d * t
            ds32_s[rows, :] = ds
            ds_s[rows, :] = ds.astype(BF16)
            dp_ref[2, rows, :] = (d * u * s * (sz * (1.0 + z * (1.0 - sz)))).astype(BF16)
            return carry
        lax.fori_loop(0, TM // RB, mid, 0)

        for h in range(HEADS):
            cs = slice(h * HD, (h + 1) * HD)
            dsh = _chunks_on_lanes(ds_s, cs)
            dvl = lax.dot_general(wt_ref[h], dsh, TN, preferred_element_type=F32)
            gbs = gbs_ref[h]
            for ck in range(TM // CHUNK):
                r = slice(ck * CHUNK, (ck + 1) * CHUNK)
                dvl_s[r, cs] = dvl[:, ck * HD:(ck + 1) * HD]
                gbs = gbs + ds32_s[r, cs]
            gbs_ref[h] = gbs
            gws_ref[h] += lax.dot_general(dsh, _chunks_on_lanes(vl_s, cs), NT, preferred_element_type=F32)

        def post(rb, carry):
            g_sg, g_sb = carry
            rows = _rows(rb)
            dvl = dvl_s[rows, :]
            vn = vn_s[rows, :]
            dvg = _ln_bwd(dvl * sg_ref[...], vn, rs_s[rows, :])
            dp_ref[1, rows, :] = (dvg * gpv_s[rows, :]).astype(BF16)
            return g_sg + _fold8(dvl * vn), g_sb + _fold8(dvl)
        zero = jnp.zeros((8, D), F32)
        g_sg, g_sb = _blocks(TM // RB, post, (zero, zero))
        st_ref[0] = st_ref[0] + g_sg
        st_ref[1] = st_ref[1] + g_sb

        @pl.when(_ring_full(i, nt))
        def _():
            gw_acc[...] += lax.dot_general(yp_s[...], dy_s[...], TN, preferred_element_type=F32)

        @pl.when(i == nt - 1)
        def _():
            gw_ref[...] = gw_acc[...].astype(BF16)

    tile = lambda seg: pl.BlockSpec((TM, D), lambda i: (i, seg))
    row = pl.BlockSpec((1, D), lambda i: (0, 0))
    hh = pl.BlockSpec((HEADS, CHUNK, CHUNK), lambda i: (0, 0, 0))
    ring = VMEM((_ring_tiles(nt) * TM, D), BF16)
    return pl.pallas_call(
        body, name="branch_b_bwd", grid=(nt,),
        out_shape=(jax.ShapeDtypeStruct((NSEG, T, D), BF16), jax.ShapeDtypeStruct((D, D), BF16),
                   jax.ShapeDtypeStruct((HEADS, CHUNK, CHUNK), F32), jax.ShapeDtypeStruct((HEADS, CHUNK, HD), F32),
                   jax.ShapeDtypeStruct((2, 8, D), F32)),
        in_specs=[pl.BlockSpec(memory_space=ANY), pl.BlockSpec((TM, D), lambda i: (i, 0)), tile(3), tile(4), tile(5),
                  hh, pl.BlockSpec((CHUNK, HEADS), lambda i: (0, 0)), row, row,
                  pl.BlockSpec((None, D, D), lambda i: (W_SO, 0, 0))],
        out_specs=(pl.BlockSpec((3, TM, D), lambda i: (1, i, 0)), pl.BlockSpec((D, D), lambda i: (0, 0)), hh, hh,
                   pl.BlockSpec((2, 8, D), lambda i: (0, 0, 0))),
        scratch_shapes=[VMEM((HEADS, CHUNK, CHUNK), BF16), VMEM((TM, D), F32), VMEM((TM, D), BF16), VMEM((TM, D), F32),
                        VMEM((TM, D), F32), VMEM((TM, 1), F32), VMEM((TM, D), F32), VMEM((TM, D), BF16), VMEM((TM, D), F32),
                        ring, ring, VMEM((TM, D), F32), VMEM((D, D), F32)],
        input_output_aliases={0: 0},
        compiler_params=_cp(("arbitrary",)),
    )(dp, dyb, p, p, p, ws, bst, sg, sb, wq)


def _branch_a_bwd1(dp, dya, yc, p, lg, lb, wq):
    T = p.shape[0]
    nt = T // TM

    def body(dp_in, dya_ref, yc_ref, z_ref, lg_ref, lb_ref, w_ref, dp_ref, dyc_ref, gw_ref, st_ref,
             d_s, yp_s, dy_s, gw_acc):
        del dp_in
        i = pl.program_id(0)
        base = _ring_base(i, nt)

        @pl.when(i == 0)
        def _():
            gw_acc[...] = jnp.zeros_like(gw_acc)
            st_ref[...] = jnp.zeros_like(st_ref)

        d_s[...] = lax.dot_general(dya_ref[...], w_ref[...], NT, preferred_element_type=F32)
        dy_s[pl.ds(base, TM), :] = dya_ref[...]

        def blk(rb, carry):
            g_lg, g_lb, g_cb = carry
            rows = _rows(rb)
            n, rs = _ln_stats(yc_ref[rows, :])
            l = n * lg_ref[...] + lb_ref[...]
            sgl = _sig(l)
            sl = l * sgl
            z = _ld(z_ref, rows)
            sz = _sig(z)
            siluz = z * sz
            d = d_s[rows, :]
            yp_s[_ring_rows(base, rb), :] = (sl * siluz).astype(BF16)
            dp_ref[rows, :] = (d * sl * (sz * (1.0 + z * (1.0 - sz)))).astype(BF16)
            dl = (d * siluz) * (sgl * (1.0 + l * (1.0 - sgl)))
            dyc = _ln_bwd(dl * lg_ref[...], n, rs)
            dyc_ref[rows, :] = dyc
            return g_lg + _fold8(dl * n), g_lb + _fold8(dl), g_cb + _fold8(dyc)
        zero = jnp.zeros((8, D), F32)
        g_lg, g_lb, g_cb = _blocks(TM // RB, blk, (zero, zero, zero))
        st_ref[0] = st_ref[0] + g_lg
        st_ref[1] = st_ref[1] + g_lb
        st_ref[2] = st_ref[2] + g_cb

        @pl.when(_ring_full(i, nt))
        def _():
            gw_acc[...] += lax.dot_general(yp_s[...], dy_s[...], TN, preferred_element_type=F32)

        @pl.when(i == nt - 1)
        def _():
            gw_ref[...] = gw_acc[...].astype(BF16)

    tile = pl.BlockSpec((TM, D), lambda i: (i, 0))
    row = pl.BlockSpec((1, D), lambda i: (0, 0))
    ring = VMEM((_ring_tiles(nt) * TM, D), BF16)
    return pl.pallas_call(
        body, name="branch_a_bwd1", grid=(nt,),
        out_shape=(jax.ShapeDtypeStruct((NSEG, T, D), BF16), jax.ShapeDtypeStruct((T, D), F32),
                   jax.ShapeDtypeStruct((D, D), BF16), jax.ShapeDtypeStruct((3, 8, D), F32)),
        in_specs=[pl.BlockSpec(memory_space=ANY), tile, tile, pl.BlockSpec((TM, D), lambda i: (i, 2)), row, row,
                  pl.BlockSpec((None, D, D), lambda i: (0, 0, 0))],
        out_specs=(pl.BlockSpec((None, TM, D), lambda i: (2, i, 0)), tile, pl.BlockSpec((D, D), lambda i: (0, 0)),
                   pl.BlockSpec((3, 8, D), lambda i: (0, 0, 0))),
        scratch_shapes=[VMEM((TM, D), F32), ring, ring, VMEM((D, D), F32)],
        input_output_aliases={0: 0},
        compiler_params=_cp(("arbitrary",)),
    )(dp, dya, yc, p, lg, lb, wq)


def _branch_a_bwd2(dp, dyc, p, cw, gw_co, gw_so, gw_o, early):
    T = p.shape[0]
    nt = T // TM
    hb = TM // HALO

    def body(dp_in, dyc_ref, hdyc_ref, val_ref, glu_ref, cw_ref, gco_ref, gso_ref, go_ref, e_ref,
             dp_ref, gcw_ref, rq_ref, eg_ref, a_s, sd_ref, da_s, x_send, x_recv, x_loc, ag_send, ag_recv, ag_loc):
        del dp_in
        i = pl.program_id(0)
        start_exchange, wait_exchange = _owner_exchange((gco_ref, gso_ref, go_ref), rq_ref, x_send, x_recv, x_loc)
        ag_start, ag_relay, ag_finish = _allgather_phases([e_ref], lambda a, slot: eg_ref.at[slot],
                                                          ag_send, ag_recv, ag_loc)

        @pl.when(i == 0)
        def _():
            gcw_ref[...] = jnp.zeros_like(gcw_ref)
            ag_start()
            start_exchange()

        @pl.when(i == nt // 2)
        def _():
            ag_relay()

        def fill(rb, carry):
            rows = _rows(rb)
            a_s[rows, :] = _ld(val_ref, rows) * _sig(_ld(glu_ref, rows))
            return carry
        lax.fori_loop(0, TM // RB, fill, 0)
        sd_ref[0, 0:TM, :] = dyc_ref[...]
        sd_ref[0, TM:EXT, :] = jnp.where(i < nt - 1, hdyc_ref[...], 0.0)
        _shift_copies(sd_ref)

        def conv_t(rb, carry):
            r0 = rb * RBC
            accs = [jnp.zeros((8, D), F32)] * (RBC // 8)
            for k in range(KW):
                o = KW - 1 - k
                w = cw_ref[k]
                accs = [acc + w * sd_ref[o % 8, pl.ds(pl.multiple_of(r0 + 8 * (o // 8 + g), 8), 8), :]
                        for g, acc in enumerate(accs)]
            for g, acc in enumerate(accs):
                da_s[pl.ds(pl.multiple_of(r0 + 8 * g, 8), 8), :] = acc
            return carry
        lax.fori_loop(0, TM // RBC, conv_t, 0)

        for k0 in range(0, KW, TAPG):
            taps = list(range(k0, min(k0 + TAPG, KW)))

            def tap_group(rb, accs, taps=taps):
                for u in range(2):
                    r0 = rb * 32 + 16 * u
                    a = a_s[pl.ds(pl.multiple_of(r0, 16), 16), :]
                    out = []
                    for k, acc in zip(taps, accs):
                        o = KW - 1 - k
                        d = sd_ref[o % 8, pl.ds(pl.multiple_of(r0 + 8 * (o // 8), 8), 16), :]
                        out.append(acc + _fold8(a * d))
                    accs = tuple(out)
                return accs
            sums = lax.fori_loop(0, TM // 32, tap_group, tuple(jnp.zeros((8, D), F32) for _ in taps))
            for k, s in zip(taps, sums):
                gcw_ref[k] = gcw_ref[k] + s

        def glu_b(rb, carry):
            rows = _rows(rb)
            da = da_s[rows, :]
            sg = _sig(_ld(glu_ref, rows))
            dp_ref[0, rows, :] = (da * sg).astype(BF16)
            dp_ref[1, rows, :] = (da * _ld(val_ref, rows) * (sg * (1.0 - sg))).astype(BF16)
            return carry
        lax.fori_loop(0, TM // RB, glu_b, 0)

        @pl.when(i == nt - 1)
        def _():
            wait_exchange()
            ag_finish()

    tile = lambda seg: pl.BlockSpec((TM, D), lambda i: (i, seg))
    hbm = pl.BlockSpec(memory_space=ANY)
    return pl.pallas_call(
        body, name="branch_a_bwd2", grid=(nt,),
        out_shape=(jax.ShapeDtypeStruct((NSEG, T, D), BF16), jax.ShapeDtypeStruct((32, 8, D), F32),
                   jax.ShapeDtypeStruct((NDEV, 3, 128, D), BF16), jax.ShapeDtypeStruct((NDEV, NEARLY, D), F32)),
        in_specs=[hbm, pl.BlockSpec((TM, D), lambda i: (i, 0)),
                  pl.BlockSpec((HALO, D), lambda i: (jnp.minimum((i + 1) * hb, nt * hb - 1), 0)),
                  tile(0), tile(1), pl.BlockSpec((KW, 8, D), lambda i: (0, 0, 0)),
                  hbm, hbm, hbm, hbm],
        out_specs=(pl.BlockSpec((2, TM, D), lambda i: (0, i, 0)), pl.BlockSpec((32, 8, D), lambda i: (0, 0, 0)), hbm, hbm),
        scratch_shapes=[VMEM((TM, D), F32), VMEM((8, EXT, D), F32), VMEM((TM, D), F32),
                        pltpu.SemaphoreType.DMA((3, 7)), pltpu.SemaphoreType.DMA((3, 7)), pltpu.SemaphoreType.DMA((3,)),
                        pltpu.SemaphoreType.DMA((1, 7)), pltpu.SemaphoreType.DMA((1, 7)), pltpu.SemaphoreType.DMA((1,))],
        input_output_aliases={0: 0},
        compiler_params=_cp(("arbitrary",)),
    )(dp, dyc, dyc, p, p, cw, gw_co, gw_so, gw_o, early)


def _in_bwd_dx(dp, wing, x, dx2, mod, g_pre, token):
    T = x.shape[0]
    nt = T // TM

    def body(dp_ref, w_ref, x_ref, dx2_ref, mod_ref, g_ref, token_ref, gx_ref, st_ref, dh_s):
        del token_ref
        i = pl.program_id(0)

        @pl.when(i == 0)
        def _():
            st_ref[...] = jnp.zeros_like(st_ref)

        dh_s[...] = lax.dot_general(dp_ref[0], w_ref[0], NT, preferred_element_type=F32)
        for j in range(1, NSEG):
            dh_s[...] += lax.dot_general(dp_ref[j], w_ref[j], NT, preferred_element_type=F32)

        def blk(rb, carry):
            d_sh, d_sc, g_g = carry
            rows = _rows(rb)
            xv = x_ref[rows, :]
            r = lax.rsqrt(jnp.mean(xv * xv, axis=-1, keepdims=True) + EPS)
            xn = xv * r
            g = g_ref[...]
            hpre = xn * g
            dh = dh_s[rows, :]
            dhp = dh * (1.0 + mod_ref[1:2, :])
            dxn = dhp * g
            gx_ref[rows, :] = dx2_ref[rows, :] + r * (dxn - xn * jnp.mean(dxn * xn, axis=-1, keepdims=True))
            return d_sh + _fold8(dh), d_sc + _fold8(dh * hpre), g_g + _fold8(dhp * xn)
        zero = jnp.zeros((8, D), F32)
        d_sh, d_sc, g_g = _blocks(TM // RB, blk, (zero, zero, zero))
        st_ref[0] = st_ref[0] + d_sh
        st_ref[1] = st_ref[1] + d_sc
        st_ref[2] = st_ref[2] + g_g

    tile = pl.BlockSpec((TM, D), lambda i: (i, 0))
    return pl.pallas_call(
        body, name="in_bwd_dx", grid=(nt,),
        out_shape=(jax.ShapeDtypeStruct((T, D), F32), jax.ShapeDtypeStruct((3, 8, D), F32)),
        in_specs=[pl.BlockSpec((NSEG, TM, D), lambda i: (0, i, 0)),
                  pl.BlockSpec((NSEG, D, D), lambda i: (0, 0, 0), pipeline_mode=pl.Buffered(1)),
                  tile, tile, pl.BlockSpec((3, D), lambda i: (0, 0)), pl.BlockSpec((1, D), lambda i: (0, 0)),
                  pl.BlockSpec((8, 128), lambda i: (0, 0))],
        out_specs=(tile, pl.BlockSpec((3, 8, D), lambda i: (0, 0, 0))),
        scratch_shapes=[VMEM((TM, D), F32)],
        compiler_params=_cp(("arbitrary",), vmem_mb=56),
    )(dp, wing, x, dx2, mod, g_pre, token)


RS_ORDER = (5, 4, 3, 2, 7, 6, 1, 0)
RS_HALF = D // 2


def _in_bwd_dw(ht, dp, order):
    T = ht.shape[1]
    tmw = min(4 * TMB, T)
    nb = T // tmw
    nu = 2 * NSEG

    def body(order_ref, ht_ref, dp_ref, csum_ref, g_ref, acc, sbuf, rsib, send_sems, recv_sems, out_sems):
        j = pl.program_id(0)
        i = pl.program_id(1)
        x, y, c = _me()
        sibling = (x, y, 1 - c)

        def to_sibling(s):
            return pltpu.make_async_remote_copy(src_ref=sbuf.at[s], dst_ref=rsib.at[s], send_sem=send_sems.at[s],
                                                recv_sem=recv_sems.at[s], device_id=sibling, device_id_type=MESH)

        def to_csum(t):
            cols = pl.ds((t % 2) * RS_HALF, RS_HALF)
            return pltpu.make_async_copy(sbuf.at[8 + t], csum_ref.at[t // 2, :, cols], out_sems.at[t])

        @pl.when(i == 0)
        def _():
            acc[...] = jnp.zeros_like(acc)
        acc[...] += jnp.dot(ht_ref[...], dp_ref[...], preferred_element_type=F32)

        for jj in range(nu):
            chip, half, mine = jj // 4, (jj % 4) // 2, jj % 2
            s = 2 * chip + half

            @pl.when((j == jj) & (i == nb - 1))
            def _(chip=chip, half=half, mine=mine, s=s):
                if not mine:
                    sbuf[s] = acc[...].astype(BF16)
                    to_sibling(s).start()
                elif chip < 3:
                    to_sibling(s).wait_recv()
                    sbuf[8 + s] = (acc[...] + rsib[s].astype(F32)).astype(BF16)
                    to_csum(s).start()
                else:
                    to_sibling(s).wait_recv()
                    g_ref[:, half * RS_HALF:(half + 1) * RS_HALF] = acc[...] + rsib[s].astype(F32)

        @pl.when((j == nu - 1) & (i == nb - 1))
        def _():
            for s in range(8):
                to_sibling(s).wait_send()
            for t in range(6):
                to_csum(t).wait()

    hbm = pl.BlockSpec(memory_space=ANY)
    return pl.pallas_call(
        body, name="in_bwd_dw",
        out_shape=(jax.ShapeDtypeStruct((3, D, D), BF16), jax.ShapeDtypeStruct((D, D), F32)),
        grid_spec=pltpu.PrefetchScalarGridSpec(
            num_scalar_prefetch=1, grid=(nu, nb),
            in_specs=[pl.BlockSpec((D, tmw), lambda j, i, o: (0, i)),
                      pl.BlockSpec((None, tmw, RS_HALF), lambda j, i, o: (o[2 * (j // 4) + j % 2], i, (j % 4) // 2))],
            out_specs=(hbm, pl.BlockSpec((D, D), lambda j, i, o: (0, 0))),
            scratch_shapes=[VMEM((D, RS_HALF), F32), VMEM((14, D, RS_HALF), BF16), VMEM((8, D, RS_HALF), BF16),
                            pltpu.SemaphoreType.DMA((8,)), pltpu.SemaphoreType.DMA((8,)), pltpu.SemaphoreType.DMA((6,))]),
        compiler_params=_cp(("arbitrary", "arbitrary"), vmem_mb=56),
    )(order, ht, dp)


def _gather_late(late):
    def body(sm_ref, smg_ref, ag_send, ag_recv, ag_loc):
        _allgather([sm_ref], lambda a, slot: smg_ref.at[slot], ag_send, ag_recv, ag_loc)

    hbm = pl.BlockSpec(memory_space=ANY)
    return pl.pallas_call(
        body, name="gather_late",
        out_shape=jax.ShapeDtypeStruct((NDEV, NLATE, D), F32),
        in_specs=[hbm], out_specs=hbm,
        scratch_shapes=[pltpu.SemaphoreType.DMA((1, 7)), pltpu.SemaphoreType.DMA((1, 7)), pltpu.SemaphoreType.DMA((1,))],
        compiler_params=_cp(vmem_mb=32),
    )(late)


RS_CHIPS = ((1, 0), (0, 1), (1, 1))


def _rs_peer(q):
    x, y, c = _me()
    fx, fy = RS_CHIPS[q]
    return (1 - x if fx else x, 1 - y if fy else y, c)


def _rs_start(csum):
    def body(csum_ref, land_ref, send_sems, recv_sems, csum_thru, land_thru, token):
        barrier = pltpu.get_barrier_semaphore()
        for q in range(3):
            pl.semaphore_signal(barrier, inc=1, device_id=_rs_peer(q), device_id_type=MESH)
        pl.semaphore_wait(barrier, 3)
        for q in range(3):
            pltpu.make_async_remote_copy(src_ref=csum_ref.at[q], dst_ref=land_ref.at[q], send_sem=send_sems.at[q],
                                         recv_sem=recv_sems.at[q], device_id=_rs_peer(q), device_id_type=MESH).start()
        token[...] = jnp.zeros_like(token)

    hbm = pl.BlockSpec(memory_space=pltpu.HBM)
    sem = pl.BlockSpec(memory_space=pltpu.SEMAPHORE)
    land = pltpu.with_memory_space_constraint(lax.empty(csum.shape, csum.dtype), pltpu.HBM)
    return pl.pallas_call(
        body, name="rs_start",
        out_shape=(pltpu.SemaphoreType.DMA((3,)), pltpu.SemaphoreType.DMA((3,)), pltpu.HBM(csum.shape, csum.dtype),
                   pltpu.HBM(csum.shape, csum.dtype), jax.ShapeDtypeStruct((8, 128), F32)),
        in_specs=(hbm, hbm), out_specs=(sem, sem, hbm, hbm, pl.BlockSpec(memory_space=VMEM)),
        input_output_aliases={0: 2, 1: 3},
        compiler_params=pltpu.CompilerParams(has_side_effects=pltpu.SideEffectType.DATAFLOW_SIDE_EFFECTING, collective_id=1),
    )(pltpu.with_memory_space_constraint(csum, pltpu.HBM), land)


def _rs_wait(send_sems, recv_sems, csum_thru, land_thru, after):
    def body(csum_ref, land_ref, send_sems, recv_sems, after_ref, csum_dead, got_ref):
        del after_ref, csum_dead, got_ref
        for q in range(3):
            cp = pltpu.make_async_remote_copy(src_ref=csum_ref.at[q], dst_ref=land_ref.at[q], send_sem=send_sems.at[q],
                                              recv_sem=recv_sems.at[q], device_id=_rs_peer(q), device_id_type=MESH)
            cp.wait_send()
            cp.wait_recv()

    hbm = pl.BlockSpec(memory_space=pltpu.HBM)
    sem = pl.BlockSpec(memory_space=pltpu.SEMAPHORE)
    return pl.pallas_call(
        body, name="rs_wait",
        out_shape=(pltpu.HBM(csum_thru.shape, csum_thru.dtype), pltpu.HBM(csum_thru.shape, csum_thru.dtype)),
        in_specs=(hbm, hbm, sem, sem, pl.BlockSpec(memory_space=ANY)), out_specs=(hbm, hbm),
        input_output_aliases={0: 0, 1: 1},
        compiler_params=pltpu.CompilerParams(has_side_effects=pltpu.SideEffectType.DATAFLOW_SIDE_EFFECTING),
    )(csum_thru, land_thru, send_sems, recv_sems, after)[1]


def _pack_early(st_a, st_b, st_m, gws, gbs):
    def body(sa_ref, sb_ref, sm_ref, gws_ref, gbs_ref, out_ref):
        out_ref[...] = jnp.zeros_like(out_ref)
        fold = lambda v: jnp.sum(v, axis=0, keepdims=True)
        out_ref[0:1, :] = fold(sm_ref[2])
        out_ref[1:2, :] = fold(sa_ref[2])
        out_ref[2:3, :] = fold(sa_ref[0])
        out_ref[3:4, :] = fold(sa_ref[1])
        out_ref[4:5, :] = fold(sb_ref[0])
        out_ref[5:6, :] = fold(sb_ref[1])
        out_ref[7:8, :] = fold(sm_ref[1])
        out_ref[8:9, :] = fold(sm_ref[0])
        tri = lax.broadcasted_iota(jnp.int32, (CHUNK, CHUNK), 0) >= lax.broadcasted_iota(jnp.int32, (CHUNK, CHUNK), 1)
        for h in range(HEADS):
            out_ref[16:16 + CHUNK, h * CHUNK:(h + 1) * CHUNK] = jnp.where(tri, gws_ref[h], 0.0)
            out_ref[6:7, h * CHUNK:(h + 1) * CHUNK] = fold(gbs_ref[h].T)

    vm = pl.BlockSpec(memory_space=VMEM)
    return pl.pallas_call(
        body, name="pack_early", out_shape=jax.ShapeDtypeStruct((NEARLY, D), F32),
        in_specs=[vm] * 5, out_specs=vm, compiler_params=_cp(vmem_mb=32),
    )(st_a, st_b, st_m, gws, gbs)


def _pack_late(st_x, gcw):
    def body(sx_ref, gcw_ref, out_ref):
        out_ref[...] = jnp.zeros_like(out_ref)
        fold = lambda v: jnp.sum(v, axis=0, keepdims=True)
        for r in range(3):
            out_ref[r:r + 1, :] = fold(sx_ref[r])
        for k in range(KW):
            out_ref[8 + k:9 + k, :] = fold(gcw_ref[k])

    vm = pl.BlockSpec(memory_space=VMEM)
    return pl.pallas_call(
        body, name="pack_late", out_shape=jax.ShapeDtypeStruct((NLATE, D), F32),
        in_specs=[vm] * 2, out_specs=vm, compiler_params=_cp(vmem_mb=32),
    )(st_x, gcw)


def _pack_params(name, b_ada, g_pre, conv_b, lg, lb, sg, sb, b_sgu, g_final, w_sgu_t):
    def body(ba_ref, *refs):
        rows, ws_ref, out_ref = refs[:8], refs[8], refs[9]
        out_ref[...] = jnp.zeros_like(out_ref)
        for r in range(3):
            out_ref[r:r + 1, :] = ba_ref[:, r * D:(r + 1) * D]
        for r, ref in enumerate(rows):
            out_ref[3 + r:4 + r, :] = ref[...]
        out_ref[ROW_WS:ROW_WS + CHUNK, :] = ws_ref[...]

    vm = pl.BlockSpec(memory_space=VMEM)
    return pl.pallas_call(
        body, name=name, out_shape=jax.ShapeDtypeStruct((NSMALL, D), F32),
        in_specs=[vm] * 10, out_specs=vm, compiler_params=_cp(vmem_mb=32),
    )(b_ada, g_pre, conv_b, lg, lb, sg, sb, b_sgu, g_final, w_sgu_t)


def _adam(w, g, m, v):
    m2 = ADAM_B1 * m + (1.0 - ADAM_B1) * g
    v2 = ADAM_B2 * v + (1.0 - ADAM_B2) * (g * g)
    m_hat = m2 / (1.0 - ADAM_B1 ** ADAM_STEP)
    v_hat = v2 / (1.0 - ADAM_B2 ** ADAM_STEP)
    delta = -ADAM_LR * (m_hat / (jnp.sqrt(v_hat) + ADAM_EPS) + ADAM_WD * w)
    return delta, m2, v2


def _small_finish(eg, lg, w, m, v):
    def body(eg_ref, lg_ref, w_ref, m_ref, v_ref, loss_ref, g_ref, d_ref, m2_ref, v2_ref):
        e = eg_ref[0]
        l = lg_ref[0]
        for s in range(1, NDEV):
            e = e + eg_ref[s]
            l = l + lg_ref[s]
        g_ref[...] = jnp.zeros_like(g_ref)
        g_ref[0:2, :] = l[0:2, :]
        g_ref[2:3, :] = e[0:1, :]
        g_ref[3:4, :] = l[2:3, :]
        g_ref[4:12, :] = e[1:9, :]
        g_ref[ROW_CW:ROW_CW + KW, :] = l[8:8 + KW, :]
        g_ref[ROW_WS:ROW_WS + CHUNK, :] = e[16:16 + CHUNK, :]
        loss_ref[...] = jnp.sum(e[8:9, :], axis=1, keepdims=True)
        d_ref[...], m2_ref[...], v2_ref[...] = _adam(w_ref[...], g_ref[...], m_ref[...], v_ref[...])

    vm = pl.BlockSpec(memory_space=VMEM)
    sd = jax.ShapeDtypeStruct((NSMALL, D), F32)
    return pl.pallas_call(
        body, name="small_finish", out_shape=(jax.ShapeDtypeStruct((1, 1), F32), sd, sd, sd, sd),
        in_specs=[vm] * 5, out_specs=(vm,) * 5, compiler_params=_cp(vmem_mb=32),
    )(eg, lg, w, m, v)


def _ada_grad_adam(ct, dm, w, m, v):
    def body(ct_ref, dm_ref, w_ref, m_ref, v_ref, g_ref, d_ref, m2_ref, v2_ref):
        g = ct_ref[:, 0:1] * dm_ref[0:1, :]
        for b in range(1, NDEV):
            g = g + ct_ref[:, b:b + 1] * dm_ref[b:b + 1, :]
        g_ref[...] = g
        d_ref[...], m2_ref[...], v2_ref[...] = _adam(w_ref[...], g, m_ref[...], v_ref[...])

    vm = pl.BlockSpec(memory_space=VMEM)
    sd = jax.ShapeDtypeStruct(w.shape, F32)
    return pl.pallas_call(
        body, name="ada_grad_adam", out_shape=(sd, sd, sd, sd),
        in_specs=[vm] * 5, out_specs=(vm,) * 4, compiler_params=_cp(vmem_mb=32),
    )(ct, dm, w, m, v)


def _adam_f32(name, g, w, m, v, rows=None):
    R, C = w.shape
    rows = R if rows is None else rows

    def body(g_ref, w_ref, m_ref, v_ref, d_ref, m2_ref, v2_ref):
        d_ref[...], m2_ref[...], v2_ref[...] = _adam(w_ref[...], g_ref[...], m_ref[...], v_ref[...])

    tile = pl.BlockSpec((rows, C), lambda i: (i, 0))
    sd = jax.ShapeDtypeStruct(w.shape, F32)
    return pl.pallas_call(
        body, name=name, grid=(R // rows,), out_shape=(sd, sd, sd), in_specs=[tile] * 4, out_specs=(tile,) * 3,
        compiler_params=_cp(("parallel",), vmem_mb=32),
    )(g, w, m, v)


def _adam_w_in(gown, got, w, m, v, rows=256):
    R, C = w.shape

    def body(go_ref, got_ref, w_ref, m_ref, v_ref, g_ref, d_ref, m2_ref, v2_ref):
        g = go_ref[...]
        for q in range(3):
            g = g + got_ref[q].astype(F32)
        g_ref[...] = g
        d_ref[...], m2_ref[...], v2_ref[...] = _adam(w_ref[...], g, m_ref[...], v_ref[...])

    tile = pl.BlockSpec((rows, C), lambda i: (i, 0))
    sd = jax.ShapeDtypeStruct((R, C), F32)
    return pl.pallas_call(
        body, name="adam_w_in", grid=(R // rows,), out_shape=(sd, sd, sd, sd),
        in_specs=[tile, pl.BlockSpec((3, rows, C), lambda i: (0, i, 0)), tile, tile, tile], out_specs=(tile,) * 4,
        compiler_params=_cp(("parallel",), vmem_mb=32),
    )(gown, got, w, m, v)


def _adam_reduce(name, recv, w, m, v, rows, recv_spec):
    R, C = w.shape

    def body(r_ref, w_ref, m_ref, v_ref, g_ref, d_ref, m2_ref, v2_ref):
        g = r_ref[0].astype(F32)
        for s in range(1, NDEV):
            g = g + r_ref[s].astype(F32)
        g_ref[...] = g
        d_ref[...], m2_ref[...], v2_ref[...] = _adam(w_ref[...], g, m_ref[...], v_ref[...])

    tile = pl.BlockSpec((rows, C), lambda i: (i, 0))
    sd = jax.ShapeDtypeStruct((R, C), F32)
    return pl.pallas_call(
        body, name=name, grid=(R // rows,), out_shape=(sd, sd, sd, sd),
        in_specs=[recv_spec, tile, tile, tile], out_specs=(tile,) * 4,
        compiler_params=_cp(("parallel",)),
    )(recv, w, m, v)


def kernel(x, c, w_ada, b_ada, g_pre, w_in, conv_w, conv_b, conv_ln_g, conv_ln_b, w_conv_out, sgu_ln_g, sgu_ln_b, w_sgu, b_sgu, w_sgu_out, w_o, g_final, loss_target, m_w_ada, m_b_ada, m_g_pre, m_w_in, m_conv_w, m_conv_b, m_conv_ln_g, m_conv_ln_b, m_w_conv_out, m_sgu_ln_g, m_sgu_ln_b, m_w_sgu, m_b_sgu, m_w_sgu_out, m_w_o, m_g_final, v_w_ada, v_b_ada, v_g_pre, v_w_in, v_conv_w, v_conv_b, v_conv_ln_g, v_conv_ln_b, v_w_conv_out, v_sgu_ln_g, v_sgu_ln_b, v_w_sgu, v_b_sgu, v_w_sgu_out, v_w_o, v_g_final):
    T = x.shape[1]
    assert T % TMB == 0 and x.shape[2] == D
    xs = x[0]
    tgt = loss_target[0]
    my = 4 * lax.axis_index("x") + 2 * lax.axis_index("y") + lax.axis_index("c")

    mod, cg = _ada_fwd(c, w_ada[0], b_ada)
    gf = g_final.reshape(1, D)
    bst = b_sgu[0].T

    h, ht = _prep_h(xs, mod, g_pre)
    ag_order = jnp.stack([_slot(*b) for b in _ag_blocks(lax.axis_index("x"), lax.axis_index("y"), lax.axis_index("c"))])
    p, wing, wcog, cwg, own2, land2 = _in_proj(h, w_in[0], ag_order.astype(jnp.int32),
                                               w_conv_out[0], w_sgu_out[0], w_o[0], conv_w[0])
    wco = wcog.reshape(1, D, D)
    cw = jnp.broadcast_to(jnp.transpose(cwg, (1, 0, 2)).reshape(KW, 1, D), (KW, 8, D))
    wq_send, wq_recv, own2_thru, land2_thru, wq_token = _wq_start(own2, land2)
    yc, ya = _branch_a_fwd(p, cw, conv_b, conv_ln_g, conv_ln_b, wco, wq_token)
    wq2 = _wq_wait(wq_send, wq_recv, own2_thru, land2_thru, yc).reshape(2, D, D)
    yb = _branch_b_fwd(p, w_sgu[0], bst, sgu_ln_g, sgu_ln_b, wq2)
    dx2, dya, dyb, dp, gw_o, st_m = _merge_loss(p, ya, yb, xs, tgt, mod, gf, wq2)

    dp, gw_so, gws, gbs, st_b = _branch_b_bwd(dp, dyb, p, w_sgu[0], bst, sgu_ln_g, sgu_ln_b, wq2)
    dp, dyc, gw_co, st_a = _branch_a_bwd1(dp, dya, yc, p, conv_ln_g, conv_ln_b, wco)
    early = _pack_early(st_a, st_b, st_m, gws, gbs)
    dp, gcw, rq, eg = _branch_a_bwd2(dp, dyc, p, cw, gw_co, gw_so, gw_o, early)
    csum, gown = _in_bwd_dw(ht, dp, my ^ jnp.array(RS_ORDER, jnp.int32))
    rs_send, rs_recv, csum_thru, land_thru, token = _rs_start(csum)
    grad_x, st_x = _in_bwd_dx(dp, wing, xs, dx2, mod, g_pre, token)
    got = _rs_wait(rs_send, rs_recv, csum_thru, land_thru, st_x)
    lg = _gather_late(_pack_late(st_x, gcw))

    def pack(name, b_ada_, g_pre_, conv_b_, lg_, lb_, sg_, sb_, b_sgu_, gfin_, w_sgu_):
        return _pack_params(name, b_ada_, g_pre_, conv_b_, lg_, lb_, sg_, sb_, b_sgu_.reshape(1, D), gfin_.reshape(1, D),
                            jnp.transpose(w_sgu_[0], (1, 0, 2)).reshape(CHUNK, D))

    pw = pack("pack_w", b_ada, g_pre, conv_b, conv_ln_g, conv_ln_b, sgu_ln_g, sgu_ln_b, b_sgu, g_final, w_sgu)
    pm = pack("pack_m", m_b_ada, m_g_pre, m_conv_b, m_conv_ln_g, m_conv_ln_b, m_sgu_ln_g, m_sgu_ln_b, m_b_sgu, m_g_final,
              m_w_sgu)
    pv = pack("pack_v", v_b_ada, v_g_pre, v_conv_b, v_conv_ln_g, v_conv_ln_b, v_sgu_ln_g, v_sgu_ln_b, v_b_sgu, v_g_final,
              v_w_sgu)
    loss11, sg_, sd_, sm_, sv_ = _small_finish(eg, lg, pw, pm, pv)

    def unpack(a):
        return dict(
            b_ada=a[0:3].reshape(1, 3 * D), g_pre=a[3:4], conv_b=a[4:5], conv_ln_g=a[5:6], conv_ln_b=a[6:7],
            sgu_ln_g=a[7:8], sgu_ln_b=a[8:9], b_sgu=a[9:10].reshape(1, HEADS, CHUNK), g_final=a[10],
            w_sgu=jnp.transpose(a[ROW_WS:ROW_WS + CHUNK].reshape(CHUNK, HEADS, CHUNK), (1, 0, 2))[None])
    small = [unpack(a) for a in (sg_, sd_, sm_, sv_)]

    g_cw = lax.dynamic_slice_in_dim(sg_[ROW_CW:ROW_CW + KW], my * 128, 128, axis=1)
    d_cw, m_cw, v_cw = _adam_f32("adam_conv_w", g_cw, conv_w[0], m_conv_w[0], v_conv_w[0])

    wcols = w_ada.shape[2]
    dm_all = jnp.concatenate([lg[:, 0], lg[:, 1], eg[:, 0]], axis=1)
    dm_mine = lax.dynamic_slice_in_dim(dm_all, my * wcols, wcols, axis=1)
    g_ada, d_ada, m_ada, v_ada = _ada_grad_adam(cg.T, dm_mine, w_ada[0], m_w_ada[0], v_w_ada[0])

    g_in, d_in, m_in, v_in = _adam_w_in(gown, got, w_in[0], m_w_in[0], v_w_in[0])
    big = {}
    for j, (nm, w_, m_, v_) in enumerate((("w_conv_out", w_conv_out, m_w_conv_out, v_w_conv_out),
                                          ("w_sgu_out", w_sgu_out, m_w_sgu_out, v_w_sgu_out),
                                          ("w_o", w_o, m_w_o, v_w_o))):
        big[nm] = _adam_reduce("adam_" + nm, rq, w_[0], m_[0], v_[0], 128,
                               pl.BlockSpec((NDEV, None, 128, D), lambda i, j=j: (0, j, 0, 0)))

    per = {
        "w_ada": tuple(a[None] for a in (g_ada, d_ada, m_ada, v_ada)),
        "w_in": tuple(a[None] for a in (g_in, d_in, m_in, v_in)),
        "conv_w": tuple(a[None] for a in (g_cw, d_cw, m_cw, v_cw)),
    }
    for nm in ("w_conv_out", "w_sgu_out", "w_o"):
        per[nm] = tuple(a[None] for a in big[nm])
    for nm in ("b_ada", "g_pre", "conv_b", "conv_ln_g", "conv_ln_b", "sgu_ln_g", "sgu_ln_b", "w_sgu", "b_sgu", "g_final"):
        per[nm] = tuple(s[nm] for s in small)

    order = ["w_ada", "b_ada", "g_pre", "w_in", "conv_w", "conv_b", "conv_ln_g", "conv_ln_b", "w_conv_out",
             "sgu_ln_g", "sgu_ln_b", "w_sgu", "b_sgu", "w_sgu_out", "w_o", "g_final"]
    outs = [loss11.reshape(()), grad_x[None]]
    for part in range(4):
        outs += [per[nm][part] for nm in order]
    return tuple(outs)
```

```python
import jax
import jax.numpy as jnp
from jax import lax
from jax.experimental import pallas as pl
from jax.experimental.pallas import tpu as pltpu

F32 = jnp.float32
BF16 = jnp.bfloat16
MESH = pl.DeviceIdType.MESH
VMEM = pltpu.VMEM
ANY = pl.ANY

D = 1024
NDEV = 8
NSEG = 8
HEADS = 8
HD = 128
CHUNK = 128
KW = 31
HALO = 32
EPS = 1e-6
TM = 256
TMB = 512
RB = 32
UNROLL = 4
RBC = 32
TAPG = 4
EXT = TM + HALO
NSMALL = 176
NEARLY = 144
NLATE = 40
ROW_CW = 16
ROW_WS = 48
W_CO = 0
W_SO, W_O = 0, 1

ADAM_LR = 0.001
ADAM_B1 = 0.9
ADAM_B2 = 0.999
ADAM_EPS = 1e-08
ADAM_WD = 0.01
ADAM_STEP = 10

INV_SQRT2 = 0.7071067811865476
INV_SQRT_2PI = 0.3989422804014327

NT = (((1,), (1,)), ((), ()))
TN = (((0,), (0,)), ((), ()))


def _cp(sem=None, vmem_mb=48):
    return pltpu.CompilerParams(dimension_semantics=sem, vmem_limit_bytes=vmem_mb * 1024 * 1024)


def _me():
    return lax.axis_index("x"), lax.axis_index("y"), lax.axis_index("c")


def _slot(px, py, pc):
    return 4 * px + 2 * py + pc


def _xor_peer(k):
    x, y, c = _me()
    return (1 - x if k & 4 else x, 1 - y if k & 2 else y, 1 - c if k & 1 else c)


def _allgather_phases(srcs, dst_at, send_sems, recv_sems, loc_sems):
    x, y, c = _me()
    me = (x, y, c)
    sibling = (x, y, 1 - c)
    chips = [(1 - x, y), (x, 1 - y), (1 - x, 1 - y)]
    na = len(srcs)

    def copy(a, k, block, to, src=None):
        d = dst_at(a, _slot(*block))
        return pltpu.make_async_remote_copy(
            src_ref=d if src is None else src, dst_ref=d,
            send_sem=send_sems.at[a, k], recv_sem=recv_sems.at[a, k],
            device_id=to, device_id_type=MESH)

    local = [pltpu.make_async_copy(srcs[a], dst_at(a, _slot(*me)), loc_sems.at[a]) for a in range(na)]
    first = []
    for a in range(na):
        first.append(copy(a, 0, me, sibling, src=srcs[a]))
        for j, chip in enumerate(chips):
            first.append(copy(a, 1 + j, me, (*chip, c), src=srcs[a]))
    passed = [copy(a, 4 + j, (*chip, c), sibling) for j, chip in enumerate(chips) for a in range(na)]

    def start():
        for cp in local + first:
            cp.start()

    def relay():
        for j, chip in enumerate(chips):
            for a in range(na):
                copy(a, 1 + j, (*chip, c), me).wait_recv()
                passed[j * na + a].start()

    def finish():
        for a in range(na):
            copy(a, 0, sibling, me).wait_recv()
        for j, chip in enumerate(chips):
            for a in range(na):
                copy(a, 4 + j, (*chip, 1 - c), me).wait_recv()
        for cp in first + passed:
            cp.wait_send()
        for cp in local:
            cp.wait()

    return start, relay, finish


def _allgather(srcs, dst_at, send_sems, recv_sems, loc_sems):
    start, relay, finish = _allgather_phases(srcs, dst_at, send_sems, recv_sems, loc_sems)
    start()
    relay()
    finish()


def _owner_exchange(gq, rq_ref, x_send, x_recv, x_loc):
    mx, my_, mc = _me()
    me = _slot(mx, my_, mc)

    def rows_of(a, slot):
        return gq[a].at[pl.ds(pl.multiple_of(slot * 128, 128), 128), :]

    def exchange(k, a, recv):
        px, py, pc = _xor_peer(k)
        peer = _slot(px, py, pc)
        return pltpu.make_async_remote_copy(
            src_ref=rows_of(a, me if recv else peer), dst_ref=rq_ref.at[peer if recv else me, a],
            send_sem=x_send.at[a, k - 1], recv_sem=x_recv.at[a, k - 1],
            device_id=(px, py, pc), device_id_type=MESH)

    local = [pltpu.make_async_copy(rows_of(a, me), rq_ref.at[me, a], x_loc.at[a]) for a in range(len(gq))]

    def start():
        for cp in local:
            cp.start()
        for k in range(1, NDEV):
            for a in range(len(gq)):
                exchange(k, a, False).start()

    def wait():
        for k in range(1, NDEV):
            for a in range(len(gq)):
                exchange(k, a, True).wait_recv()
        for k in range(1, NDEV):
            for a in range(len(gq)):
                exchange(k, a, False).wait_send()
        for cp in local:
            cp.wait()

    return start, wait


def _sig(v):
    return jax.nn.sigmoid(v)


def _gelu_parts(v):
    cdf = 0.5 * (1.0 + lax.erf(v * INV_SQRT2))
    pdf = jnp.exp(-0.5 * v * v) * INV_SQRT_2PI
    return v * cdf, cdf + v * pdf


def _ln_stats(v):
    mu = jnp.mean(v, axis=-1, keepdims=True)
    xc = v - mu
    var = jnp.mean(xc * xc, axis=-1, keepdims=True)
    rs = lax.rsqrt(var + EPS)
    return xc * rs, rs


def _ln_bwd(dn, n, rs):
    return rs * (dn - jnp.mean(dn, axis=-1, keepdims=True) - n * jnp.mean(dn * n, axis=-1, keepdims=True))


def _fold8(v):
    acc = v[0:8]
    for r in range(8, v.shape[0], 8):
        acc = acc + v[r:r + 8]
    return acc


def _rows(rb, n=RB):
    return pl.ds(pl.multiple_of(rb * n, n), n)


def _ring_tiles(nt):
    return 4 if nt % 4 == 0 else (2 if nt % 2 == 0 else 1)


def _ring_base(i, nt):
    return pl.multiple_of((i % _ring_tiles(nt)) * TM, TM)


def _ring_rows(base, rb, n=RB):
    return pl.ds(pl.multiple_of(base + rb * n, n), n)


def _ring_full(i, nt):
    g = _ring_tiles(nt)
    return i % g == g - 1


def _ld(ref, rows):
    return ref[rows, :].astype(F32)


def _blocks(n, body, init=0, unroll=UNROLL):
    def trip(t, carry):
        for u in range(unroll):
            carry = body(t * unroll + u, carry)
        return carry
    return lax.fori_loop(0, n // unroll, trip, init)


def _shift_copies(s_ref):
    n = EXT - 8
    for m in range(1, 8):
        for c0 in range(0, n, 56):
            s_ref[m, c0:c0 + 56, :] = s_ref[0, c0 + m:c0 + m + 56, :]


def _ada_exchange(c_ref, w_ref, b_ref, mod_ref, call_ref, cdst, msrc, mdst, c_send, c_recv, m_send, m_recv):
    x, y, c = _me()
    me = _slot(x, y, c)

    def all_to_all(src_of, dst, ss, rs):
        def cp(k, recv):
            px, py, pc = _xor_peer(k)
            peer = _slot(px, py, pc)
            return pltpu.make_async_remote_copy(
                src_ref=src_of(me if recv else peer), dst_ref=dst.at[peer if recv else me],
                send_sem=ss.at[k - 1], recv_sem=rs.at[k - 1], device_id=(px, py, pc), device_id_type=MESH)
        for k in range(1, NDEV):
            cp(k, False).start()
        for k in range(1, NDEV):
            cp(k, True).wait_recv()
        for k in range(1, NDEV):
            cp(k, False).wait_send()

    cdst[me] = c_ref[...]
    all_to_all(lambda s: c_ref, cdst, c_send, c_recv)
    for b in range(NDEV):
        call_ref[b:b + 1, :] = cdst[b]
    m = jnp.dot(call_ref[...], w_ref[...], preferred_element_type=F32, precision=lax.Precision.HIGHEST)
    for b in range(NDEV):
        msrc[b] = m[b:b + 1, :]
    mdst[me] = msrc[me]
    all_to_all(lambda s: msrc.at[s], mdst, m_send, m_recv)
    full = jnp.concatenate([mdst[k] for k in range(NDEV)], axis=1) + b_ref[...]
    for r in range(3):
        mod_ref[r:r + 1, :] = full[:, r * D:(r + 1) * D]


def _ada_fwd(c, w_ada, b_ada):
    wcols = w_ada.shape[1]

    def body(*refs):
        _ada_exchange(*refs)

    vm = pl.BlockSpec(memory_space=VMEM)
    return pl.pallas_call(
        body, name="ada_fwd",
        out_shape=(jax.ShapeDtypeStruct((3, D), F32), jax.ShapeDtypeStruct((NDEV, D), F32)),
        in_specs=[vm, vm, vm], out_specs=(vm, vm),
        scratch_shapes=[VMEM((NDEV, 1, D), F32), VMEM((NDEV, 1, wcols), F32), VMEM((NDEV, 1, wcols), F32),
                        pltpu.SemaphoreType.DMA((7,)), pltpu.SemaphoreType.DMA((7,)),
                        pltpu.SemaphoreType.DMA((7,)), pltpu.SemaphoreType.DMA((7,))],
        compiler_params=_cp(vmem_mb=32),
    )(c, w_ada, b_ada)


def _prep_h(x, mod, g_pre):
    T = x.shape[0]

    def body(x_ref, mod_ref, g_ref, h_ref, ht_ref):
        def blk(rb, carry):
            rows = _rows(rb)
            xv = x_ref[rows, :]
            r = lax.rsqrt(jnp.mean(xv * xv, axis=-1, keepdims=True) + EPS)
            h_ref[rows, :] = ((xv * r) * g_ref[...] * (1.0 + mod_ref[1:2, :]) + mod_ref[0:1, :]).astype(BF16)
            return carry
        _blocks(TMB // RB, blk)
        ht_ref[...] = h_ref[...].T

    return pl.pallas_call(
        body, name="prep_h", grid=(T // TMB,),
        out_shape=(jax.ShapeDtypeStruct((T, D), BF16), jax.ShapeDtypeStruct((D, T), BF16)),
        in_specs=[pl.BlockSpec((TMB, D), lambda i: (i, 0)), pl.BlockSpec((3, D), lambda i: (0, 0)),
                  pl.BlockSpec((1, D), lambda i: (0, 0))],
        out_specs=(pl.BlockSpec((TMB, D), lambda i: (i, 0)), pl.BlockSpec((D, TMB), lambda i: (0, i))),
        compiler_params=_cp(("parallel",)),
    )(x, mod, g_pre)


def _ag_blocks(x, y, c):
    a = 1 - c
    n1 = (x ^ a, y ^ c)
    n2 = (x ^ c, y ^ a)
    dg = (1 - x, 1 - y)
    return [(x, y, c), (x, y, a), (*n1, c), (*n2, c), (*n1, a), (*n2, a), (*dg, c), (*dg, a)]


def _in_proj(h, w_in, order, w_co, w_so, w_o, conv_w):
    T = h.shape[0]
    tmp = min(2 * TMB, T)
    nb = T // tmp

    def body(order_ref, h_ref, w_ref, wco_ref, wso_ref, wo_ref, cw_ref, p_ref, wing_ref, wqg_ref, cwg_ref, own2_ref, land2_ref,
             wbuf, wq_st, wq2_st, send_sems, recv_sems, out_sems, ag_send, ag_recv, ag_loc, own_sems):
        j = pl.program_id(0)
        i = pl.program_id(1)
        blocks = _ag_blocks(*_me())
        me, sibling, n1, n2, n1o, n2o, dg, dgo = blocks
        own2 = [pltpu.make_async_copy(wq2_st, own2_ref, own_sems.at[0]),
                pltpu.make_async_copy(wq2_st, land2_ref.at[:, _slot(*me)], own_sems.at[1])]
        ag_start, ag_relay, ag_finish = _allgather_phases(
            [wq_st, cw_ref], lambda a, slot: wqg_ref.at[:, slot] if a == 0 else cwg_ref.at[slot], ag_send, ag_recv, ag_loc)

        def copy(k, block, to):
            d = wbuf.at[_slot(*block)]
            return pltpu.make_async_remote_copy(src_ref=d, dst_ref=d, send_sem=send_sems.at[k], recv_sem=recv_sems.at[k],
                                                device_id=to, device_id_type=MESH)

        def writeback(jj):
            s = _slot(*blocks[jj])
            return pltpu.make_async_copy(wbuf.at[s], wing_ref.at[s], out_sems.at[jj])

        sends = [copy(0, me, sibling), copy(1, me, n1), copy(2, me, n2)]
        relay = copy(3, n1, n2)
        passed = [copy(4, n1, sibling), copy(5, n2, sibling), copy(6, dg, sibling)]
        arrivals = {1: [(0, sibling)], 2: [(1, n1), (2, n2)], 4: [(5, n1o)], 5: [(4, n2o)], 6: [(3, dg)], 7: [(6, dgo)]}

        @pl.when((j == 0) & (i == 0))
        def _():
            wbuf[_slot(*me)] = w_ref[...].astype(BF16)
            for cp in sends:
                cp.start()
            writeback(0).start()
            wq2_st[W_SO] = wso_ref[...].astype(BF16)
            wq2_st[W_O] = wo_ref[...].astype(BF16)
            for cp in own2:
                cp.start()

        for jj in range(1, NSEG):
            @pl.when((j == jj) & (i == 0))
            def _(jj=jj):
                for k, block in arrivals.get(jj, []):
                    copy(k, block, me).wait_recv()
                if jj == 2:
                    relay.start()
                    passed[0].start()
                    passed[1].start()
                    writeback(3).start()
                    wq_st[0] = wco_ref[...].astype(BF16)
                    ag_start()
                if jj == 6:
                    passed[2].start()
                if jj != 3:
                    writeback(jj).start()
                if jj == NSEG - 1:
                    ag_relay()

        p_ref[...] = jnp.dot(h_ref[...], wbuf[order_ref[j]], preferred_element_type=F32).astype(BF16)

        @pl.when((j == NSEG - 1) & (i == nb - 1))
        def _():
            for cp in sends + [relay] + passed:
                cp.wait_send()
            for jj in range(NSEG):
                writeback(jj).wait()
            for cp in own2:
                cp.wait()
            ag_finish()

    vm = pl.BlockSpec(memory_space=VMEM)
    hbm = pl.BlockSpec(memory_space=ANY)
    return pl.pallas_call(
        body, name="in_proj",
        out_shape=(jax.ShapeDtypeStruct((T, NSEG * D), BF16), jax.ShapeDtypeStruct((NSEG, D, D), BF16),
                   jax.ShapeDtypeStruct((1, NDEV, 128, D), BF16), jax.ShapeDtypeStruct((NDEV, KW, 128), F32),
                   jax.ShapeDtypeStruct((2, 128, D), BF16), jax.ShapeDtypeStruct((2, NDEV, 128, D), BF16)),
        grid_spec=pltpu.PrefetchScalarGridSpec(
            num_scalar_prefetch=1, grid=(NSEG, nb),
            in_specs=[pl.BlockSpec((tmp, D), lambda j, i, o: (i, 0)), vm, vm, vm, vm, vm],
            out_specs=(pl.BlockSpec((tmp, D), lambda j, i, o: (i, o[j])), hbm, hbm, hbm, hbm, hbm),
            scratch_shapes=[VMEM((NSEG, D, D), BF16), VMEM((1, 128, D), BF16), VMEM((2, 128, D), BF16),
                            pltpu.SemaphoreType.DMA((7,)), pltpu.SemaphoreType.DMA((7,)), pltpu.SemaphoreType.DMA((NSEG,)),
                            pltpu.SemaphoreType.DMA((2, 7)), pltpu.SemaphoreType.DMA((2, 7)), pltpu.SemaphoreType.DMA((2,)),
                            pltpu.SemaphoreType.DMA((2,))]),
        compiler_params=_cp(("arbitrary", "arbitrary")),
    )(order, h, w_in, w_co, w_so, w_o, conv_w)


def _wq_start(own2, land2):
    def body(src_ref, land_ref, send_sems, recv_sems, src_thru, land_thru, token):
        barrier = pltpu.get_barrier_semaphore()
        for k in range(1, NDEV):
            pl.semaphore_signal(barrier, inc=1, device_id=_xor_peer(k), device_id_type=MESH)
        pl.semaphore_wait(barrier, NDEV - 1)
        x, y, c = _me()
        for k in range(1, NDEV):
            pltpu.make_async_remote_copy(src_ref=src_ref, dst_ref=land_ref.at[:, _slot(x, y, c)], send_sem=send_sems.at[k - 1],
                                         recv_sem=recv_sems.at[k - 1], device_id=_xor_peer(k), device_id_type=MESH).start()
        token[...] = jnp.zeros_like(token)

    hbm = pl.BlockSpec(memory_space=pltpu.HBM)
    sem = pl.BlockSpec(memory_space=pltpu.SEMAPHORE)
    return pl.pallas_call(
        body, name="wq_start",
        out_shape=(pltpu.SemaphoreType.DMA((NDEV - 1,)), pltpu.SemaphoreType.DMA((NDEV - 1,)),
                   pltpu.HBM(own2.shape, own2.dtype), pltpu.HBM(land2.shape, land2.dtype), jax.ShapeDtypeStruct((8, 128), F32)),
        in_specs=(hbm, hbm), out_specs=(sem, sem, hbm, hbm, pl.BlockSpec(memory_space=VMEM)),
        input_output_aliases={0: 2, 1: 3},
        compiler_params=pltpu.CompilerParams(has_side_effects=pltpu.SideEffectType.DATAFLOW_SIDE_EFFECTING, collective_id=2),
    )(pltpu.with_memory_space_constraint(own2, pltpu.HBM), pltpu.with_memory_space_constraint(land2, pltpu.HBM))


def _wq_wait(send_sems, recv_sems, src_thru, land_thru, after):
    def body(src_ref, land_ref, send_sems, recv_sems, after_ref, src_dead, got_ref):
        del after_ref, src_dead, got_ref
        for k in range(1, NDEV):
            px, py, pc = _xor_peer(k)
            cp = pltpu.make_async_remote_copy(src_ref=src_ref, dst_ref=land_ref.at[:, _slot(px, py, pc)],
                                              send_sem=send_sems.at[k - 1], recv_sem=recv_sems.at[k - 1],
                                              device_id=(px, py, pc), device_id_type=MESH)
            cp.wait_send()
            cp.wait_recv()

    hbm = pl.BlockSpec(memory_space=pltpu.HBM)
    sem = pl.BlockSpec(memory_space=pltpu.SEMAPHORE)
    return pl.pallas_call(
        body, name="wq_wait",
        out_shape=(pltpu.HBM(src_thru.shape, src_thru.dtype), pltpu.HBM(land_thru.shape, land_thru.dtype)),
        in_specs=(hbm, hbm, sem, sem, pl.BlockSpec(memory_space=ANY)), out_specs=(hbm, hbm),
        input_output_aliases={0: 0, 1: 1},
        compiler_params=pltpu.CompilerParams(has_side_effects=pltpu.SideEffectType.DATAFLOW_SIDE_EFFECTING),
    )(src_thru, land_thru, send_sems, recv_sems, after)[1]


def _fill_a_ext(i, s_ref, val_ref, glu_ref, hval_ref, hglu_ref):
    ah = hval_ref[...].astype(F32) * _sig(hglu_ref[...].astype(F32))
    s_ref[0, 0:HALO, :] = jnp.where(i > 0, ah, 0.0)

    def fill(rb, carry):
        rows = _rows(rb)
        s_ref[0, pl.ds(pl.multiple_of(HALO + rb * RB, RB), RB), :] = _ld(val_ref, rows) * _sig(_ld(glu_ref, rows))
        return carry
    lax.fori_loop(0, TM // RB, fill, 0)


def _halo_prev(seg):
    hb = TM // HALO
    return pl.BlockSpec((HALO, D), lambda i: (jnp.maximum(i * hb - 1, 0), seg))


def _branch_a_fwd(p, cw, conv_b, lg, lb, wq, token):
    T = p.shape[0]

    def body(val_ref, glu_ref, z_ref, hval_ref, hglu_ref, cw_ref, cb_ref, lg_ref, lb_ref, w_ref, token_ref,
             yc_ref, ya_ref, s_ref, yp_ref):
        del token_ref
        i = pl.program_id(0)
        _fill_a_ext(i, s_ref, val_ref, glu_ref, hval_ref, hglu_ref)
        _shift_copies(s_ref)

        def conv(rb, carry):
            r0 = rb * RBC
            accs = [jnp.broadcast_to(cb_ref[...], (8, D))] * (RBC // 8)
            for k in range(KW):
                o = 2 + k
                w = cw_ref[k]
                accs = [acc + w * s_ref[o % 8, pl.ds(r0 + 8 * (o // 8 + g), 8), :]
                        for g, acc in enumerate(accs)]
            for g, acc in enumerate(accs):
                yc_ref[pl.ds(r0 + 8 * g, 8), :] = acc
            return carry
        for rb in range(TM // RBC):
            conv(rb, 0)

        def post(rb, carry):
            rows = _rows(rb)
            n, _ = _ln_stats(yc_ref[rows, :])
            l = n * lg_ref[...] + lb_ref[...]
            z = _ld(z_ref, rows)
            yp_ref[rows, :] = ((l * _sig(l)) * (z * _sig(z))).astype(BF16)
            return carry
        _blocks(TM // RB, post)
        ya_ref[...] = jnp.dot(yp_ref[...], w_ref[...], preferred_element_type=F32)

    tile = lambda seg: pl.BlockSpec((TM, D), lambda i: (i, seg))
    row = pl.BlockSpec((1, D), lambda i: (0, 0))
    return pl.pallas_call(
        body, name="branch_a_fwd", grid=(T // TM,),
        out_shape=(jax.ShapeDtypeStruct((T, D), F32), jax.ShapeDtypeStruct((T, D), F32)),
        in_specs=[tile(0), tile(1), tile(2), _halo_prev(0), _halo_prev(1),
                  pl.BlockSpec((KW, 8, D), lambda i: (0, 0, 0)), row, row, row,
                  pl.BlockSpec((None, D, D), lambda i: (W_CO, 0, 0)), pl.BlockSpec((8, 128), lambda i: (0, 0))],
        out_specs=(pl.BlockSpec((TM, D), lambda i: (i, 0)), pl.BlockSpec((TM, D), lambda i: (i, 0))),
        scratch_shapes=[VMEM((8, EXT, D), F32), VMEM((TM, D), BF16)],
        compiler_params=_cp(("parallel",)),
    )(p, p, p, p, p, cw, conv_b, lg, lb, wq, token)


def _masked_ws(ws_ref, wt_ref):
    tri = lax.broadcasted_iota(jnp.int32, (CHUNK, CHUNK), 0) >= lax.broadcasted_iota(jnp.int32, (CHUNK, CHUNK), 1)
    for h in range(HEADS):
        wt_ref[h] = jnp.where(tri, ws_ref[h], 0.0).astype(BF16)


def _sgu_mix(wt_ref, vl_ref, bst_ref, s_ref):
    for h in range(HEADS):
        cs = slice(h * HD, (h + 1) * HD)
        s = jnp.dot(wt_ref[h], _chunks_on_lanes(vl_ref, cs), preferred_element_type=F32) + bst_ref[:, h:h + 1]
        for ck in range(TM // CHUNK):
            s_ref[ck * CHUNK:(ck + 1) * CHUNK, cs] = s[:, ck * HD:(ck + 1) * HD]


def _chunks_on_lanes(ref, cs):
    return jnp.concatenate([ref[ck * CHUNK:(ck + 1) * CHUNK, cs] for ck in range(TM // CHUNK)], axis=1)


def _branch_b_fwd(p, ws, bst, sg, sb, wq):
    T = p.shape[0]

    def body(pu_ref, pv_ref, pz_ref, ws_ref, bst_ref, sg_ref, sb_ref, w_ref, yb_ref,
             wt_ref, vl_ref, t_ref, s_ref, yp_ref):
        _masked_ws(ws_ref, wt_ref)

        def pre(rb, carry):
            rows = _rows(rb)
            vg, _ = _gelu_parts(_ld(pv_ref, rows))
            vn, _ = _ln_stats(vg)
            vl_ref[rows, :] = (vn * sg_ref[...] + sb_ref[...]).astype(BF16)
            u, _ = _gelu_parts(_ld(pu_ref, rows))
            z = _ld(pz_ref, rows)
            t_ref[rows, :] = u * (z * _sig(z))
            return carry
        _blocks(TM // RB, pre)
        _sgu_mix(wt_ref, vl_ref, bst_ref, s_ref)

        def post(rb, carry):
            rows = _rows(rb)
            yp_ref[rows, :] = (t_ref[rows, :] * s_ref[rows, :]).astype(BF16)
            return carry
        _blocks(TM // RB, post)
        yb_ref[...] = jnp.dot(yp_ref[...], w_ref[...], preferred_element_type=F32)

    tile = lambda seg: pl.BlockSpec((TM, D), lambda i: (i, seg))
    row = pl.BlockSpec((1, D), lambda i: (0, 0))
    return pl.pallas_call(
        body, name="branch_b_fwd", grid=(T // TM,),
        out_shape=jax.ShapeDtypeStruct((T, D), F32),
        in_specs=[tile(3), tile(4), tile(5), pl.BlockSpec((HEADS, CHUNK, CHUNK), lambda i: (0, 0, 0)),
                  pl.BlockSpec((CHUNK, HEADS), lambda i: (0, 0)), row, row,
                  pl.BlockSpec((None, D, D), lambda i: (W_SO, 0, 0))],
        out_specs=pl.BlockSpec((TM, D), lambda i: (i, 0)),
        scratch_shapes=[VMEM((HEADS, CHUNK, CHUNK), BF16), VMEM((TM, D), BF16), VMEM((TM, D), F32),
                        VMEM((TM, D), F32), VMEM((TM, D), BF16)],
        compiler_params=_cp(("parallel",)),
    )(p, p, p, ws, bst, sg, sb, wq)


def _merge_loss(p, ya, yb, x, tgt, mod, g_final, wq):
    T = x.shape[0]
    nt = T // TM

    def body(ga_ref, gb_ref, ya_ref, yb_ref, x_ref, t_ref, mod_ref, gf_ref, w_ref,
             dx2_ref, dya_ref, dyb_ref, dp_ref, gwo_ref, st_ref,
             mrg_s, out_s, dout_s, dm_s, gw_acc):
        i = pl.program_id(0)
        base = _ring_base(i, nt)
        cur = pl.ds(base, TM)

        @pl.when(i == 0)
        def _():
            gw_acc[...] = jnp.zeros_like(gw_acc)
            st_ref[...] = jnp.zeros_like(st_ref)

        def merge(rb, carry):
            rows = _rows(rb)
            mrg_s[_ring_rows(base, rb), :] = (_sig(_ld(ga_ref, rows)) * ya_ref[rows, :]
                                              + _sig(_ld(gb_ref, rows)) * yb_ref[rows, :]).astype(BF16)
            return carry
        lax.fori_loop(0, TM // RB, merge, 0)
        out_s[...] = jnp.dot(mrg_s[cur, :], w_ref[...], preferred_element_type=F32)

        def head(rb, carry):
            loss, gg, dg = carry
            rows = _rows(rb)
            gate = mod_ref[2:3, :]
            gf = gf_ref[...]
            out = out_s[rows, :]
            x2 = x_ref[rows, :] + gate * out
            r2 = lax.rsqrt(jnp.mean(x2 * x2, axis=-1, keepdims=True) + EPS)
            x2n = x2 * r2
            diff = x2n * gf - t_ref[rows, :]
            dy = diff * (1.0 / D)
            dx2n = dy * gf
            dx2 = r2 * (dx2n - x2n * jnp.mean(dx2n * x2n, axis=-1, keepdims=True))
            dx2_ref[rows, :] = dx2
            dout_s[_ring_rows(base, rb), :] = (dx2 * gate).astype(BF16)
            return loss + _fold8(diff * diff), gg + _fold8(dy * x2n), dg + _fold8(dx2 * out)
        zero = jnp.zeros((8, D), F32)
        loss, gg, dg = _blocks(TM // RB, head, (zero, zero, zero))
        st_ref[0] = st_ref[0] + loss * (0.5 / D)
        st_ref[1] = st_ref[1] + gg
        st_ref[2] = st_ref[2] + dg

        dm_s[...] = lax.dot_general(dout_s[cur, :], w_ref[...], NT, preferred_element_type=F32)

        @pl.when(_ring_full(i, nt))
        def _():
            gw_acc[...] += lax.dot_general(mrg_s[...], dout_s[...], TN, preferred_element_type=F32)

        def split(rb, carry):
            rows = _rows(rb)
            dm = dm_s[rows, :]
            sa = _sig(_ld(ga_ref, rows))
            sb = _sig(_ld(gb_ref, rows))
            dya_ref[rows, :] = (dm * sa).astype(BF16)
            dyb_ref[rows, :] = (dm * sb).astype(BF16)
            dp_ref[0, rows, :] = (dm * ya_ref[rows, :] * (sa * (1.0 - sa))).astype(BF16)
            dp_ref[1, rows, :] = (dm * yb_ref[rows, :] * (sb * (1.0 - sb))).astype(BF16)
            return carry
        lax.fori_loop(0, TM // RB, split, 0)

        @pl.when(i == nt - 1)
        def _():
            gwo_ref[...] = gw_acc[...].astype(BF16)

    tile = pl.BlockSpec((TM, D), lambda i: (i, 0))
    ring = VMEM((_ring_tiles(nt) * TM, D), BF16)
    return pl.pallas_call(
        body, name="merge_loss", grid=(nt,),
        out_shape=(jax.ShapeDtypeStruct((T, D), F32), jax.ShapeDtypeStruct((T, D), BF16), jax.ShapeDtypeStruct((T, D), BF16),
                   jax.ShapeDtypeStruct((NSEG, T, D), BF16), jax.ShapeDtypeStruct((D, D), BF16),
                   jax.ShapeDtypeStruct((3, 8, D), F32)),
        in_specs=[pl.BlockSpec((TM, D), lambda i: (i, 6)), pl.BlockSpec((TM, D), lambda i: (i, 7)), tile, tile, tile, tile,
                  pl.BlockSpec((3, D), lambda i: (0, 0)), pl.BlockSpec((1, D), lambda i: (0, 0)),
                  pl.BlockSpec((None, D, D), lambda i: (W_O, 0, 0))],
        out_specs=(tile, tile, tile, pl.BlockSpec((2, TM, D), lambda i: (3, i, 0)),
                   pl.BlockSpec((D, D), lambda i: (0, 0)), pl.BlockSpec((3, 8, D), lambda i: (0, 0, 0))),
        scratch_shapes=[ring, VMEM((TM, D), F32), ring, VMEM((TM, D), F32), VMEM((D, D), F32)],
        compiler_params=_cp(("arbitrary",)),
    )(p, p, ya, yb, x, tgt, mod, g_final, wq)


def _branch_b_bwd(dp, dyb, p, ws, bst, sg, sb, wq):
    T = p.shape[0]
    nt = T // TM

    def body(dp_in, dyb_ref, pu_ref, pv_ref, pz_ref, ws_ref, bst_ref, sg_ref, sb_ref, w_ref,
             dp_ref, gw_ref, gws_ref, gbs_ref, st_ref,
             wt_ref, d_s, vl_s, vn_s, gpv_s, rs_s, s_s, ds_s, ds32_s, yp_s, dy_s, dvl_s, gw_acc):
        del dp_in
        i = pl.program_id(0)
        base = _ring_base(i, nt)

        @pl.when(i == 0)
        def _():
            gw_acc[...] = jnp.zeros_like(gw_acc)
            gws_ref[...] = jnp.zeros_like(gws_ref)
            gbs_ref[...] = jnp.zeros_like(gbs_ref)
            st_ref[...] = jnp.zeros_like(st_ref)
            _masked_ws(ws_ref, wt_ref)

        d_s[...] = lax.dot_general(dyb_ref[...], w_ref[...], NT, preferred_element_type=F32)
        dy_s[pl.ds(base, TM), :] = dyb_ref[...]

        def pre(rb, carry):
            rows = _rows(rb)
            vg, gpv = _gelu_parts(_ld(pv_ref, rows))
            vn, rs = _ln_stats(vg)
            vl_s[rows, :] = (vn * sg_ref[...] + sb_ref[...]).astype(BF16)
            vn_s[rows, :] = vn
            gpv_s[rows, :] = gpv
            rs_s[rows, :] = rs
            return carry
        _blocks(TM // RB, pre)
        _sgu_mix(wt_ref, vl_s, bst_ref, s_s)

        def mid(rb, carry):
            rows = _rows(rb)
            u, gpu = _gelu_parts(_ld(pu_ref, rows))
            z = _ld(pz_ref, rows)
            sz = _sig(z)
            siluz = z * sz
            d = d_s[rows, :]
            s = s_s[rows, :]
            t = u * siluz
            yp_s[_ring_rows(base, rb), :] = (t * s).astype(BF16)
            dp_ref[0, rows, :] = (d * s * siluz * gpu).astype(BF16)
            ds = d * t
            ds32_s[rows, :] = ds
            ds_s[rows, :] = ds.astype(BF16)
            dp_ref[2, rows, :] = (d * u * s * (sz * (1.0 + z * (1.0 - sz)))).astype(BF16)
            return carry
        lax.fori_loop(0, TM // RB, mid, 0)

        for h in range(HEADS):
            cs = slice(h * HD, (h + 1) * HD)
            dsh = _chunks_on_lanes(ds_s, cs)
            dvl = lax.dot_general(wt_ref[h], dsh, TN, preferred_element_type=F32)
            gbs = gbs_ref[h]
            for ck in range(TM // CHUNK):
                r = slice(ck * CHUNK, (ck + 1) * CHUNK)
                dvl_s[r, cs] = dvl[:, ck * HD:(ck + 1) * HD]
                gbs = gbs + ds32_s[r, cs]
            gbs_ref[h] = gbs
            gws_ref[h] += lax.dot_general(dsh, _chunks_on_lanes(vl_s, cs), NT, preferred_element_type=F32)

        def post(rb, carry):
            g_sg, g_sb = carry
            rows = _rows(rb)
            dvl = dvl_s[rows, :]
            vn = vn_s[rows, :]
            dvg = _ln_bwd(dvl * sg_ref[...], vn, rs_s[rows, :])
            dp_ref[1, rows, :] = (dvg * gpv_s[rows, :]).astype(BF16)
            return g_sg + _fold8(dvl * vn), g_sb + _fold8(dvl)
        zero = jnp.zeros((8, D), F32)
        g_sg, g_sb = _blocks(TM // RB, post, (zero, zero))
        st_ref[0] = st_ref[0] + g_sg
        st_ref[1] = st_ref[1] + g_sb

        @pl.when(_ring_full(i, nt))
        def _():
            gw_acc[...] += lax.dot_general(yp_s[...], dy_s[...], TN, preferred_element_type=F32)

        @pl.when(i == nt - 1)
        def _():
            gw_ref[...] = gw_acc[...].astype(BF16)

    tile = lambda seg: pl.BlockSpec((TM, D), lambda i: (i, seg))
    row = pl.BlockSpec((1, D), lambda i: (0, 0))
    hh = pl.BlockSpec((HEADS, CHUNK, CHUNK), lambda i: (0, 0, 0))
    ring = VMEM((_ring_tiles(nt) * TM, D), BF16)
    return pl.pallas_call(
        body, name="branch_b_bwd", grid=(nt,),
        out_shape=(jax.ShapeDtypeStruct((NSEG, T, D), BF16), jax.ShapeDtypeStruct((D, D), BF16),
                   jax.ShapeDtypeStruct((HEADS, CHUNK, CHUNK), F32), jax.ShapeDtypeStruct((HEADS, CHUNK, HD), F32),
                   jax.ShapeDtypeStruct((2, 8, D), F32)),
        in_specs=[pl.BlockSpec(memory_space=ANY), pl.BlockSpec((TM, D), lambda i: (i, 0)), tile(3), tile(4), tile(5),
                  hh, pl.BlockSpec((CHUNK, HEADS), lambda i: (0, 0)), row, row,
                  pl.BlockSpec((None, D, D), lambda i: (W_SO, 0, 0))],
        out_specs=(pl.BlockSpec((3, TM, D), lambda i: (1, i, 0)), pl.BlockSpec((D, D), lambda i: (0, 0)), hh, hh,
                   pl.BlockSpec((2, 8, D), lambda i: (0, 0, 0))),
        scratch_shapes=[VMEM((HEADS, CHUNK, CHUNK), BF16), VMEM((TM, D), F32), VMEM((TM, D), BF16), VMEM((TM, D), F32),
                        VMEM((TM, D), F32), VMEM((TM, 1), F32), VMEM((TM, D), F32), VMEM((TM, D), BF16), VMEM((TM, D), F32),
                        ring, ring, VMEM((TM, D), F32), VMEM((D, D), F32)],
        input_output_aliases={0: 0},
        compiler_params=_cp(("arbitrary",)),
    )(dp, dyb, p, p, p, ws, bst, sg, sb, wq)


def _branch_a_bwd1(dp, dya, yc, p, lg, lb, wq):
    T = p.shape[0]
    nt = T // TM

    def body(dp_in, dya_ref, yc_ref, z_ref, lg_ref, lb_ref, w_ref, dp_ref, dyc_ref, gw_ref, st_ref,
             d_s, yp_s, dy_s, gw_acc):
        del dp_in
        i = pl.program_id(0)
        base = _ring_base(i, nt)

        @pl.when(i == 0)
        def _():
            gw_acc[...] = jnp.zeros_like(gw_acc)
            st_ref[...] = jnp.zeros_like(st_ref)

        d_s[...] = lax.dot_general(dya_ref[...], w_ref[...], NT, preferred_element_type=F32)
        dy_s[pl.ds(base, TM), :] = dya_ref[...]

        def blk(rb, carry):
            g_lg, g_lb, g_cb = carry
            rows = _rows(rb)
            n, rs = _ln_stats(yc_ref[rows, :])
            l = n * lg_ref[...] + lb_ref[...]
            sgl = _sig(l)
            sl = l * sgl
            z = _ld(z_ref, rows)
            sz = _sig(z)
            siluz = z * sz
            d = d_s[rows, :]
            yp_s[_ring_rows(base, rb), :] = (sl * siluz).astype(BF16)
            dp_ref[rows, :] = (d * sl * (sz * (1.0 + z * (1.0 - sz)))).astype(BF16)
            dl = (d * siluz) * (sgl * (1.0 + l * (1.0 - sgl)))
            dyc = _ln_bwd(dl * lg_ref[...], n, rs)
            dyc_ref[rows, :] = dyc
            return g_lg + _fold8(dl * n), g_lb + _fold8(dl), g_cb + _fold8(dyc)
        zero = jnp.zeros((8, D), F32)
        g_lg, g_lb, g_cb = _blocks(TM // RB, blk, (zero, zero, zero))
        st_ref[0] = st_ref[0] + g_lg
        st_ref[1] = st_ref[1] + g_lb
        st_ref[2] = st_ref[2] + g_cb

        @pl.when(_ring_full(i, nt))
        def _():
            gw_acc[...] += lax.dot_general(yp_s[...], dy_s[...], TN, preferred_element_type=F32)

        @pl.when(i == nt - 1)
        def _():
            gw_ref[...] = gw_acc[...].astype(BF16)

    tile = pl.BlockSpec((TM, D), lambda i: (i, 0))
    row = pl.BlockSpec((1, D), lambda i: (0, 0))
    ring = VMEM((_ring_tiles(nt) * TM, D), BF16)
    return pl.pallas_call(
        body, name="branch_a_bwd1", grid=(nt,),
        out_shape=(jax.ShapeDtypeStruct((NSEG, T, D), BF16), jax.ShapeDtypeStruct((T, D), F32),
                   jax.ShapeDtypeStruct((D, D), BF16), jax.ShapeDtypeStruct((3, 8, D), F32)),
        in_specs=[pl.BlockSpec(memory_space=ANY), tile, tile, pl.BlockSpec((TM, D), lambda i: (i, 2)), row, row,
                  pl.BlockSpec((None, D, D), lambda i: (0, 0, 0))],
        out_specs=(pl.BlockSpec((None, TM, D), lambda i: (2, i, 0)), tile, pl.BlockSpec((D, D), lambda i: (0, 0)),
                   pl.BlockSpec((3, 8, D), lambda i: (0, 0, 0))),
        scratch_shapes=[VMEM((TM, D), F32), ring, ring, VMEM((D, D), F32)],
        input_output_aliases={0: 0},
        compiler_params=_cp(("arbitrary",)),
    )(dp, dya, yc, p, lg, lb, wq)


def _branch_a_bwd2(dp, dyc, p, cw, gw_co, gw_so, gw_o, early):
    T = p.shape[0]
    nt = T // TM
    hb = TM // HALO

    def body(dp_in, dyc_ref, hdyc_ref, val_ref, glu_ref, cw_ref, gco_ref, gso_ref, go_ref, e_ref,
             dp_ref, gcw_ref, rq_ref, eg_ref, a_s, sd_ref, da_s, x_send, x_recv, x_loc, ag_send, ag_recv, ag_loc):
        del dp_in
        i = pl.program_id(0)
        start_exchange, wait_exchange = _owner_exchange((gco_ref, gso_ref, go_ref), rq_ref, x_send, x_recv, x_loc)
        ag_start, ag_relay, ag_finish = _allgather_phases([e_ref], lambda a, slot: eg_ref.at[slot],
                                                          ag_send, ag_recv, ag_loc)

        @pl.when(i == 0)
        def _():
            gcw_ref[...] = jnp.zeros_like(gcw_ref)
            ag_start()
            start_exchange()

        @pl.when(i == nt // 2)
        def _():
            ag_relay()

        def fill(rb, carry):
            rows = _rows(rb)
            a_s[rows, :] = _ld(val_ref, rows) * _sig(_ld(glu_ref, rows))
            return carry
        lax.fori_loop(0, TM // RB, fill, 0)
        sd_ref[0, 0:TM, :] = dyc_ref[...]
        sd_ref[0, TM:EXT, :] = jnp.where(i < nt - 1, hdyc_ref[...], 0.0)
        _shift_copies(sd_ref)

        def conv_t(rb, carry):
            r0 = rb * RBC
            accs = [jnp.zeros((8, D), F32)] * (RBC // 8)
            for k in range(KW):
                o = KW - 1 - k
                w = cw_ref[k]
                accs = [acc + w * sd_ref[o % 8, pl.ds(r0 + 8 * (o // 8 + g), 8), :]
                        for g, acc in enumerate(accs)]
            for g, acc in enumerate(accs):
                da_s[pl.ds(r0 + 8 * g, 8), :] = acc
            return carry
        for rb in range(TM // RBC):
            conv_t(rb, 0)

        for k0 in range(0, KW, TAPG):
            taps = list(range(k0, min(k0 + TAPG, KW)))

            def tap_group(rb, accs, taps=taps):
                for u in range(2):
                    r0 = rb * 32 + 16 * u
                    a = a_s[pl.ds(r0, 16), :]
                    out = []
                    for k, acc in zip(taps, accs):
                        o = KW - 1 - k
                        d = sd_ref[o % 8, pl.ds(r0 + 8 * (o // 8), 16), :]
                        out.append(acc + _fold8(a * d))
                    accs = tuple(out)
                return accs
            sums = tuple(jnp.zeros((8, D), F32) for _ in taps)
            for rb in range(TM // 32):
                sums = tap_group(rb, sums)
            for k, s in zip(taps, sums):
                gcw_ref[k] = gcw_ref[k] + s

        def glu_b(rb, carry):
            rows = _rows(rb)
            da = da_s[rows, :]
            sg = _sig(_ld(glu_ref, rows))
            dp_ref[0, rows, :] = (da * sg).astype(BF16)
            dp_ref[1, rows, :] = (da * _ld(val_ref, rows) * (sg * (1.0 - sg))).astype(BF16)
            return carry
        lax.fori_loop(0, TM // RB, glu_b, 0)

        @pl.when(i == nt - 1)
        def _():
            wait_exchange()
            ag_finish()

    tile = lambda seg: pl.BlockSpec((TM, D), lambda i: (i, seg))
    hbm = pl.BlockSpec(memory_space=ANY)
    return pl.pallas_call(
        body, name="branch_a_bwd2", grid=(nt,),
        out_shape=(jax.ShapeDtypeStruct((NSEG, T, D), BF16), jax.ShapeDtypeStruct((32, 8, D), F32),
                   jax.ShapeDtypeStruct((NDEV, 3, 128, D), BF16), jax.ShapeDtypeStruct((NDEV, NEARLY, D), F32)),
        in_specs=[hbm, pl.BlockSpec((TM, D), lambda i: (i, 0)),
                  pl.BlockSpec((HALO, D), lambda i: (jnp.minimum((i + 1) * hb, nt * hb - 1), 0)),
                  tile(0), tile(1), pl.BlockSpec((KW, 8, D), lambda i: (0, 0, 0)),
                  hbm, hbm, hbm, hbm],
        out_specs=(pl.BlockSpec((2, TM, D), lambda i: (0, i, 0)), pl.BlockSpec((32, 8, D), lambda i: (0, 0, 0)), hbm, hbm),
        scratch_shapes=[VMEM((TM, D), F32), VMEM((8, EXT, D), F32), VMEM((TM, D), F32),
                        pltpu.SemaphoreType.DMA((3, 7)), pltpu.SemaphoreType.DMA((3, 7)), pltpu.SemaphoreType.DMA((3,)),
                        pltpu.SemaphoreType.DMA((1, 7)), pltpu.SemaphoreType.DMA((1, 7)), pltpu.SemaphoreType.DMA((1,))],
        input_output_aliases={0: 0},
        compiler_params=_cp(("arbitrary",)),
    )(dp, dyc, dyc, p, p, cw, gw_co, gw_so, gw_o, early)


def _in_bwd_dx(dp, wing, x, dx2, mod, g_pre, token):
    T = x.shape[0]
    nt = T // TM

    def body(dp_ref, w_ref, x_ref, dx2_ref, mod_ref, g_ref, token_ref, gx_ref, st_ref, dh_s):
        del token_ref
        i = pl.program_id(0)

        @pl.when(i == 0)
        def _():
            st_ref[...] = jnp.zeros_like(st_ref)

        dh_s[...] = lax.dot_general(dp_ref[0], w_ref[0], NT, preferred_element_type=F32)
        for j in range(1, NSEG):
            dh_s[...] += lax.dot_general(dp_ref[j], w_ref[j], NT, preferred_element_type=F32)

        def blk(rb, carry):
            d_sh, d_sc, g_g = carry
            rows = _rows(rb)
            xv = x_ref[rows, :]
            r = lax.rsqrt(jnp.mean(xv * xv, axis=-1, keepdims=True) + EPS)
            xn = xv * r
            g = g_ref[...]
            hpre = xn * g
            dh = dh_s[rows, :]
            dhp = dh * (1.0 + mod_ref[1:2, :])
            dxn = dhp * g
            gx_ref[rows, :] = dx2_ref[rows, :] + r * (dxn - xn * jnp.mean(dxn * xn, axis=-1, keepdims=True))
            return d_sh + _fold8(dh), d_sc + _fold8(dh * hpre), g_g + _fold8(dhp * xn)
        zero = jnp.zeros((8, D), F32)
        d_sh, d_sc, g_g = _blocks(TM // RB, blk, (zero, zero, zero))
        st_ref[0] = st_ref[0] + d_sh
        st_ref[1] = st_ref[1] + d_sc
        st_ref[2] = st_ref[2] + g_g

    tile = pl.BlockSpec((TM, D), lambda i: (i, 0))
    return pl.pallas_call(
        body, name="in_bwd_dx", grid=(nt,),
        out_shape=(jax.ShapeDtypeStruct((T, D), F32), jax.ShapeDtypeStruct((3, 8, D), F32)),
        in_specs=[pl.BlockSpec((NSEG, TM, D), lambda i: (0, i, 0)),
                  pl.BlockSpec((NSEG, D, D), lambda i: (0, 0, 0), pipeline_mode=pl.Buffered(1)),
                  tile, tile, pl.BlockSpec((3, D), lambda i: (0, 0)), pl.BlockSpec((1, D), lambda i: (0, 0)),
                  pl.BlockSpec((8, 128), lambda i: (0, 0))],
        out_specs=(tile, pl.BlockSpec((3, 8, D), lambda i: (0, 0, 0))),
        scratch_shapes=[VMEM((TM, D), F32)],
        compiler_params=_cp(("arbitrary",), vmem_mb=56),
    )(dp, wing, x, dx2, mod, g_pre, token)


RS_ORDER = (5, 4, 3, 2, 7, 6, 1, 0)
RS_HALF = D // 2


def _in_bwd_dw(ht, dp, order):
    T = ht.shape[1]
    tmw = min(4 * TMB, T)
    nb = T // tmw
    nu = 2 * NSEG

    def body(order_ref, ht_ref, dp_ref, csum_ref, g_ref, acc, sbuf, rsib, send_sems, recv_sems, out_sems):
        j = pl.program_id(0)
        i = pl.program_id(1)
        x, y, c = _me()
        sibling = (x, y, 1 - c)

        def to_sibling(s):
            return pltpu.make_async_remote_copy(src_ref=sbuf.at[s], dst_ref=rsib.at[s], send_sem=send_sems.at[s],
                                                recv_sem=recv_sems.at[s], device_id=sibling, device_id_type=MESH)

        def to_csum(t):
            cols = pl.ds((t % 2) * RS_HALF, RS_HALF)
            return pltpu.make_async_copy(sbuf.at[8 + t], csum_ref.at[t // 2, :, cols], out_sems.at[t])

        @pl.when(i == 0)
        def _():
            acc[...] = jnp.zeros_like(acc)
        acc[...] += jnp.dot(ht_ref[...], dp_ref[...], preferred_element_type=F32)

        for jj in range(nu):
            chip, half, mine = jj // 4, (jj % 4) // 2, jj % 2
            s = 2 * chip + half

            @pl.when((j == jj) & (i == nb - 1))
            def _(chip=chip, half=half, mine=mine, s=s):
                if not mine:
                    sbuf[s] = acc[...].astype(BF16)
                    to_sibling(s).start()
                elif chip < 3:
                    to_sibling(s).wait_recv()
                    sbuf[8 + s] = (acc[...] + rsib[s].astype(F32)).astype(BF16)
                    to_csum(s).start()
                else:
                    to_sibling(s).wait_recv()
                    g_ref[:, half * RS_HALF:(half + 1) * RS_HALF] = acc[...] + rsib[s].astype(F32)

        @pl.when((j == nu - 1) & (i == nb - 1))
        def _():
            for s in range(8):
                to_sibling(s).wait_send()
            for t in range(6):
                to_csum(t).wait()

    hbm = pl.BlockSpec(memory_space=ANY)
    return pl.pallas_call(
        body, name="in_bwd_dw",
        out_shape=(jax.ShapeDtypeStruct((3, D, D), BF16), jax.ShapeDtypeStruct((D, D), F32)),
        grid_spec=pltpu.PrefetchScalarGridSpec(
            num_scalar_prefetch=1, grid=(nu, nb),
            in_specs=[pl.BlockSpec((D, tmw), lambda j, i, o: (0, i)),
                      pl.BlockSpec((None, tmw, RS_HALF), lambda j, i, o: (o[2 * (j // 4) + j % 2], i, (j % 4) // 2))],
            out_specs=(hbm, pl.BlockSpec((D, D), lambda j, i, o: (0, 0))),
            scratch_shapes=[VMEM((D, RS_HALF), F32), VMEM((14, D, RS_HALF), BF16), VMEM((8, D, RS_HALF), BF16),
                            pltpu.SemaphoreType.DMA((8,)), pltpu.SemaphoreType.DMA((8,)), pltpu.SemaphoreType.DMA((6,))]),
        compiler_params=_cp(("arbitrary", "arbitrary"), vmem_mb=56),
    )(order, ht, dp)


def _gather_late(late):
    def body(sm_ref, smg_ref, ag_send, ag_recv, ag_loc):
        _allgather([sm_ref], lambda a, slot: smg_ref.at[slot], ag_send, ag_recv, ag_loc)

    hbm = pl.BlockSpec(memory_space=ANY)
    return pl.pallas_call(
        body, name="gather_late",
        out_shape=jax.ShapeDtypeStruct((NDEV, NLATE, D), F32),
        in_specs=[hbm], out_specs=hbm,
        scratch_shapes=[pltpu.SemaphoreType.DMA((1, 7)), pltpu.SemaphoreType.DMA((1, 7)), pltpu.SemaphoreType.DMA((1,))],
        compiler_params=_cp(vmem_mb=32),
    )(late)


RS_CHIPS = ((1, 0), (0, 1), (1, 1))


def _rs_peer(q):
    x, y, c = _me()
    fx, fy = RS_CHIPS[q]
    return (1 - x if fx else x, 1 - y if fy else y, c)


def _rs_start(csum):
    def body(csum_ref, land_ref, send_sems, recv_sems, csum_thru, land_thru, token):
        barrier = pltpu.get_barrier_semaphore()
        for q in range(3):
            pl.semaphore_signal(barrier, inc=1, device_id=_rs_peer(q), device_id_type=MESH)
        pl.semaphore_wait(barrier, 3)
        for q in range(3):
            pltpu.make_async_remote_copy(src_ref=csum_ref.at[q], dst_ref=land_ref.at[q], send_sem=send_sems.at[q],
                                         recv_sem=recv_sems.at[q], device_id=_rs_peer(q), device_id_type=MESH).start()
        token[...] = jnp.zeros_like(token)

    hbm = pl.BlockSpec(memory_space=pltpu.HBM)
    sem = pl.BlockSpec(memory_space=pltpu.SEMAPHORE)
    land = pltpu.with_memory_space_constraint(lax.empty(csum.shape, csum.dtype), pltpu.HBM)
    return pl.pallas_call(
        body, name="rs_start",
        out_shape=(pltpu.SemaphoreType.DMA((3,)), pltpu.SemaphoreType.DMA((3,)), pltpu.HBM(csum.shape, csum.dtype),
                   pltpu.HBM(csum.shape, csum.dtype), jax.ShapeDtypeStruct((8, 128), F32)),
        in_specs=(hbm, hbm), out_specs=(sem, sem, hbm, hbm, pl.BlockSpec(memory_space=VMEM)),
        input_output_aliases={0: 2, 1: 3},
        compiler_params=pltpu.CompilerParams(has_side_effects=pltpu.SideEffectType.DATAFLOW_SIDE_EFFECTING, collective_id=1),
    )(pltpu.with_memory_space_constraint(csum, pltpu.HBM), land)


def _rs_wait(send_sems, recv_sems, csum_thru, land_thru, after):
    def body(csum_ref, land_ref, send_sems, recv_sems, after_ref, csum_dead, got_ref):
        del after_ref, csum_dead, got_ref
        for q in range(3):
            cp = pltpu.make_async_remote_copy(src_ref=csum_ref.at[q], dst_ref=land_ref.at[q], send_sem=send_sems.at[q],
                                              recv_sem=recv_sems.at[q], device_id=_rs_peer(q), device_id_type=MESH)
            cp.wait_send()
            cp.wait_recv()

    hbm = pl.BlockSpec(memory_space=pltpu.HBM)
    sem = pl.BlockSpec(memory_space=pltpu.SEMAPHORE)
    return pl.pallas_call(
        body, name="rs_wait",
        out_shape=(pltpu.HBM(csum_thru.shape, csum_thru.dtype), pltpu.HBM(csum_thru.shape, csum_thru.dtype)),
        in_specs=(hbm, hbm, sem, sem, pl.BlockSpec(memory_space=ANY)), out_specs=(hbm, hbm),
        input_output_aliases={0: 0, 1: 1},
        compiler_params=pltpu.CompilerParams(has_side_effects=pltpu.SideEffectType.DATAFLOW_SIDE_EFFECTING),
    )(csum_thru, land_thru, send_sems, recv_sems, after)[1]


def _pack_early(st_a, st_b, st_m, gws, gbs):
    def body(sa_ref, sb_ref, sm_ref, gws_ref, gbs_ref, out_ref):
        out_ref[...] = jnp.zeros_like(out_ref)
        fold = lambda v: jnp.sum(v, axis=0, keepdims=True)
        out_ref[0:1, :] = fold(sm_ref[2])
        out_ref[1:2, :] = fold(sa_ref[2])
        out_ref[2:3, :] = fold(sa_ref[0])
        out_ref[3:4, :] = fold(sa_ref[1])
        out_ref[4:5, :] = fold(sb_ref[0])
        out_ref[5:6, :] = fold(sb_ref[1])
        out_ref[7:8, :] = fold(sm_ref[1])
        out_ref[8:9, :] = fold(sm_ref[0])
        tri = lax.broadcasted_iota(jnp.int32, (CHUNK, CHUNK), 0) >= lax.broadcasted_iota(jnp.int32, (CHUNK, CHUNK), 1)
        for h in range(HEADS):
            out_ref[16:16 + CHUNK, h * CHUNK:(h + 1) * CHUNK] = jnp.where(tri, gws_ref[h], 0.0)
            out_ref[6:7, h * CHUNK:(h + 1) * CHUNK] = fold(gbs_ref[h].T)

    vm = pl.BlockSpec(memory_space=VMEM)
    return pl.pallas_call(
        body, name="pack_early", out_shape=jax.ShapeDtypeStruct((NEARLY, D), F32),
        in_specs=[vm] * 5, out_specs=vm, compiler_params=_cp(vmem_mb=32),
    )(st_a, st_b, st_m, gws, gbs)


def _pack_late(st_x, gcw):
    def body(sx_ref, gcw_ref, out_ref):
        out_ref[...] = jnp.zeros_like(out_ref)
        fold = lambda v: jnp.sum(v, axis=0, keepdims=True)
        for r in range(3):
            out_ref[r:r + 1, :] = fold(sx_ref[r])
        for k in range(KW):
            out_ref[8 + k:9 + k, :] = fold(gcw_ref[k])

    vm = pl.BlockSpec(memory_space=VMEM)
    return pl.pallas_call(
        body, name="pack_late", out_shape=jax.ShapeDtypeStruct((NLATE, D), F32),
        in_specs=[vm] * 2, out_specs=vm, compiler_params=_cp(vmem_mb=32),
    )(st_x, gcw)


def _pack_params(name, b_ada, g_pre, conv_b, lg, lb, sg, sb, b_sgu, g_final, w_sgu_t):
    def body(ba_ref, *refs):
        rows, ws_ref, out_ref = refs[:8], refs[8], refs[9]
        out_ref[...] = jnp.zeros_like(out_ref)
        for r in range(3):
            out_ref[r:r + 1, :] = ba_ref[:, r * D:(r + 1) * D]
        for r, ref in enumerate(rows):
            out_ref[3 + r:4 + r, :] = ref[...]
        out_ref[ROW_WS:ROW_WS + CHUNK, :] = ws_ref[...]

    vm = pl.BlockSpec(memory_space=VMEM)
    return pl.pallas_call(
        body, name=name, out_shape=jax.ShapeDtypeStruct((NSMALL, D), F32),
        in_specs=[vm] * 10, out_specs=vm, compiler_params=_cp(vmem_mb=32),
    )(b_ada, g_pre, conv_b, lg, lb, sg, sb, b_sgu, g_final, w_sgu_t)


def _adam(w, g, m, v):
    m2 = ADAM_B1 * m + (1.0 - ADAM_B1) * g
    v2 = ADAM_B2 * v + (1.0 - ADAM_B2) * (g * g)
    m_hat = m2 / (1.0 - ADAM_B1 ** ADAM_STEP)
    v_hat = v2 / (1.0 - ADAM_B2 ** ADAM_STEP)
    delta = -ADAM_LR * (m_hat / (jnp.sqrt(v_hat) + ADAM_EPS) + ADAM_WD * w)
    return delta, m2, v2


def _small_finish(eg, lg, w, m, v):
    def body(eg_ref, lg_ref, w_ref, m_ref, v_ref, loss_ref, g_ref, d_ref, m2_ref, v2_ref):
        e = eg_ref[0]
        l = lg_ref[0]
        for s in range(1, NDEV):
            e = e + eg_ref[s]
            l = l + lg_ref[s]
        g_ref[...] = jnp.zeros_like(g_ref)
        g_ref[0:2, :] = l[0:2, :]
        g_ref[2:3, :] = e[0:1, :]
        g_ref[3:4, :] = l[2:3, :]
        g_ref[4:12, :] = e[1:9, :]
        g_ref[ROW_CW:ROW_CW + KW, :] = l[8:8 + KW, :]
        g_ref[ROW_WS:ROW_WS + CHUNK, :] = e[16:16 + CHUNK, :]
        loss_ref[...] = jnp.sum(e[8:9, :], axis=1, keepdims=True)
        d_ref[...], m2_ref[...], v2_ref[...] = _adam(w_ref[...], g_ref[...], m_ref[...], v_ref[...])

    vm = pl.BlockSpec(memory_space=VMEM)
    sd = jax.ShapeDtypeStruct((NSMALL, D), F32)
    return pl.pallas_call(
        body, name="small_finish", out_shape=(jax.ShapeDtypeStruct((1, 1), F32), sd, sd, sd, sd),
        in_specs=[vm] * 5, out_specs=(vm,) * 5, compiler_params=_cp(vmem_mb=32),
    )(eg, lg, w, m, v)


def _ada_grad_adam(ct, dm, w, m, v):
    def body(ct_ref, dm_ref, w_ref, m_ref, v_ref, g_ref, d_ref, m2_ref, v2_ref):
        g = ct_ref[:, 0:1] * dm_ref[0:1, :]
        for b in range(1, NDEV):
            g = g + ct_ref[:, b:b + 1] * dm_ref[b:b + 1, :]
        g_ref[...] = g
        d_ref[...], m2_ref[...], v2_ref[...] = _adam(w_ref[...], g, m_ref[...], v_ref[...])

    vm = pl.BlockSpec(memory_space=VMEM)
    sd = jax.ShapeDtypeStruct(w.shape, F32)
    return pl.pallas_call(
        body, name="ada_grad_adam", out_shape=(sd, sd, sd, sd),
        in_specs=[vm] * 5, out_specs=(vm,) * 4, compiler_params=_cp(vmem_mb=32),
    )(ct, dm, w, m, v)


def _adam_f32(name, g, w, m, v, rows=None):
    R, C = w.shape
    rows = R if rows is None else rows

    def body(g_ref, w_ref, m_ref, v_ref, d_ref, m2_ref, v2_ref):
        d_ref[...], m2_ref[...], v2_ref[...] = _adam(w_ref[...], g_ref[...], m_ref[...], v_ref[...])

    tile = pl.BlockSpec((rows, C), lambda i: (i, 0))
    sd = jax.ShapeDtypeStruct(w.shape, F32)
    return pl.pallas_call(
        body, name=name, grid=(R // rows,), out_shape=(sd, sd, sd), in_specs=[tile] * 4, out_specs=(tile,) * 3,
        compiler_params=_cp(("parallel",), vmem_mb=32),
    )(g, w, m, v)


def _adam_w_in(gown, got, w, m, v, rows=256):
    R, C = w.shape

    def body(go_ref, got_ref, w_ref, m_ref, v_ref, g_ref, d_ref, m2_ref, v2_ref):
        g = go_ref[...]
        for q in range(3):
            g = g + got_ref[q].astype(F32)
        g_ref[...] = g
        d_ref[...], m2_ref[...], v2_ref[...] = _adam(w_ref[...], g, m_ref[...], v_ref[...])

    tile = pl.BlockSpec((rows, C), lambda i: (i, 0))
    sd = jax.ShapeDtypeStruct((R, C), F32)
    return pl.pallas_call(
        body, name="adam_w_in", grid=(R // rows,), out_shape=(sd, sd, sd, sd),
        in_specs=[tile, pl.BlockSpec((3, rows, C), lambda i: (0, i, 0)), tile, tile, tile], out_specs=(tile,) * 4,
        compiler_params=_cp(("parallel",), vmem_mb=32),
    )(gown, got, w, m, v)


def _adam_reduce(name, recv, w, m, v, rows, recv_spec):
    R, C = w.shape

    def body(r_ref, w_ref, m_ref, v_ref, g_ref, d_ref, m2_ref, v2_ref):
        g = r_ref[0].astype(F32)
        for s in range(1, NDEV):
            g = g + r_ref[s].astype(F32)
        g_ref[...] = g
        d_ref[...], m2_ref[...], v2_ref[...] = _adam(w_ref[...], g, m_ref[...], v_ref[...])

    tile = pl.BlockSpec((rows, C), lambda i: (i, 0))
    sd = jax.ShapeDtypeStruct((R, C), F32)
    return pl.pallas_call(
        body, name=name, grid=(R // rows,), out_shape=(sd, sd, sd, sd),
        in_specs=[recv_spec, tile, tile, tile], out_specs=(tile,) * 4,
        compiler_params=_cp(("parallel",)),
    )(recv, w, m, v)


def kernel(x, c, w_ada, b_ada, g_pre, w_in, conv_w, conv_b, conv_ln_g, conv_ln_b, w_conv_out, sgu_ln_g, sgu_ln_b, w_sgu, b_sgu, w_sgu_out, w_o, g_final, loss_target, m_w_ada, m_b_ada, m_g_pre, m_w_in, m_conv_w, m_conv_b, m_conv_ln_g, m_conv_ln_b, m_w_conv_out, m_sgu_ln_g, m_sgu_ln_b, m_w_sgu, m_b_sgu, m_w_sgu_out, m_w_o, m_g_final, v_w_ada, v_b_ada, v_g_pre, v_w_in, v_conv_w, v_conv_b, v_conv_ln_g, v_conv_ln_b, v_w_conv_out, v_sgu_ln_g, v_sgu_ln_b, v_w_sgu, v_b_sgu, v_w_sgu_out, v_w_o, v_g_final):
    T = x.shape[1]
    assert T % TMB == 0 and x.shape[2] == D
    xs = x[0]
    tgt = loss_target[0]
    my = 4 * lax.axis_index("x") + 2 * lax.axis_index("y") + lax.axis_index("c")

    mod, cg = _ada_fwd(c, w_ada[0], b_ada)
    gf = g_final.reshape(1, D)
    bst = b_sgu[0].T

    h, ht = _prep_h(xs, mod, g_pre)
    ag_order = jnp.stack([_slot(*b) for b in _ag_blocks(lax.axis_index("x"), lax.axis_index("y"), lax.axis_index("c"))])
    p, wing, wcog, cwg, own2, land2 = _in_proj(h, w_in[0], ag_order.astype(jnp.int32),
                                               w_conv_out[0], w_sgu_out[0], w_o[0], conv_w[0])
    wco = wcog.reshape(1, D, D)
    cw = jnp.broadcast_to(jnp.transpose(cwg, (1, 0, 2)).reshape(KW, 1, D), (KW, 8, D))
    wq_send, wq_recv, own2_thru, land2_thru, wq_token = _wq_start(own2, land2)
    yc, ya = _branch_a_fwd(p, cw, conv_b, conv_ln_g, conv_ln_b, wco, wq_token)
    wq2 = _wq_wait(wq_send, wq_recv, own2_thru, land2_thru, yc).reshape(2, D, D)
    yb = _branch_b_fwd(p, w_sgu[0], bst, sgu_ln_g, sgu_ln_b, wq2)
    dx2, dya, dyb, dp, gw_o, st_m = _merge_loss(p, ya, yb, xs, tgt, mod, gf, wq2)

    dp, gw_so, gws, gbs, st_b = _branch_b_bwd(dp, dyb, p, w_sgu[0], bst, sgu_ln_g, sgu_ln_b, wq2)
    dp, dyc, gw_co, st_a = _branch_a_bwd1(dp, dya, yc, p, conv_ln_g, conv_ln_b, wco)
    early = _pack_early(st_a, st_b, st_m, gws, gbs)
    dp, gcw, rq, eg = _branch_a_bwd2(dp, dyc, p, cw, gw_co, gw_so, gw_o, early)
    csum, gown = _in_bwd_dw(ht, dp, my ^ jnp.array(RS_ORDER, jnp.int32))
    rs_send, rs_recv, csum_thru, land_thru, token = _rs_start(csum)
    grad_x, st_x = _in_bwd_dx(dp, wing, xs, dx2, mod, g_pre, token)
    got = _rs_wait(rs_send, rs_recv, csum_thru, land_thru, st_x)
    lg = _gather_late(_pack_late(st_x, gcw))

    def pack(name, b_ada_, g_pre_, conv_b_, lg_, lb_, sg_, sb_, b_sgu_, gfin_, w_sgu_):
        return _pack_params(name, b_ada_, g_pre_, conv_b_, lg_, lb_, sg_, sb_, b_sgu_.reshape(1, D), gfin_.reshape(1, D),
                            jnp.transpose(w_sgu_[0], (1, 0, 2)).reshape(CHUNK, D))

    pw = pack("pack_w", b_ada, g_pre, conv_b, conv_ln_g, conv_ln_b, sgu_ln_g, sgu_ln_b, b_sgu, g_final, w_sgu)
    pm = pack("pack_m", m_b_ada, m_g_pre, m_conv_b, m_conv_ln_g, m_conv_ln_b, m_sgu_ln_g, m_sgu_ln_b, m_b_sgu, m_g_final,
              m_w_sgu)
    pv = pack("pack_v", v_b_ada, v_g_pre, v_conv_b, v_conv_ln_g, v_conv_ln_b, v_sgu_ln_g, v_sgu_ln_b, v_b_sgu, v_g_final,
              v_w_sgu)
    loss11, sg_, sd_, sm_, sv_ = _small_finish(eg, lg, pw, pm, pv)

    def unpack(a):
        return dict(
            b_ada=a[0:3].reshape(1, 3 * D), g_pre=a[3:4], conv_b=a[4:5], conv_ln_g=a[5:6], conv_ln_b=a[6:7],
            sgu_ln_g=a[7:8], sgu_ln_b=a[8:9], b_sgu=a[9:10].reshape(1, HEADS, CHUNK), g_final=a[10],
            w_sgu=jnp.transpose(a[ROW_WS:ROW_WS + CHUNK].reshape(CHUNK, HEADS, CHUNK), (1, 0, 2))[None])
    small = [unpack(a) for a in (sg_, sd_, sm_, sv_)]

    g_cw = lax.dynamic_slice_in_dim(sg_[ROW_CW:ROW_CW + KW], my * 128, 128, axis=1)
    d_cw, m_cw, v_cw = _adam_f32("adam_conv_w", g_cw, conv_w[0], m_conv_w[0], v_conv_w[0])

    wcols = w_ada.shape[2]
    dm_all = jnp.concatenate([lg[:, 0], lg[:, 1], eg[:, 0]], axis=1)
    dm_mine = lax.dynamic_slice_in_dim(dm_all, my * wcols, wcols, axis=1)
    g_ada, d_ada, m_ada, v_ada = _ada_grad_adam(cg.T, dm_mine, w_ada[0], m_w_ada[0], v_w_ada[0])

    g_in, d_in, m_in, v_in = _adam_w_in(gown, got, w_in[0], m_w_in[0], v_w_in[0])
    big = {}
    for j, (nm, w_, m_, v_) in enumerate((("w_conv_out", w_conv_out, m_w_conv_out, v_w_conv_out),
                                          ("w_sgu_out", w_sgu_out, m_w_sgu_out, v_w_sgu_out),
                                          ("w_o", w_o, m_w_o, v_w_o))):
        big[nm] = _adam_reduce("adam_" + nm, rq, w_[0], m_[0], v_[0], 128,
                               pl.BlockSpec((NDEV, None, 128, D), lambda i, j=j: (0, j, 0, 0)))

    per = {
        "w_ada": tuple(a[None] for a in (g_ada, d_ada, m_ada, v_ada)),
        "w_in": tuple(a[None] for a in (g_in, d_in, m_in, v_in)),
        "conv_w": tuple(a[None] for a in (g_cw, d_cw, m_cw, v_cw)),
    }
    for nm in ("w_conv_out", "w_sgu_out", "w_o"):
        per[nm] = tuple(a[None] for a in big[nm])
    for nm in ("b_ada", "g_pre", "conv_b", "conv_ln_g", "conv_ln_b", "sgu_ln_g", "sgu_ln_b", "w_sgu", "b_sgu", "g_final"):
        per[nm] = tuple(s[nm] for s in small)

    order = ["w_ada", "b_ada", "g_pre", "w_in", "conv_w", "conv_b", "conv_ln_g", "conv_ln_b", "w_conv_out",
             "sgu_ln_g", "sgu_ln_b", "w_sgu", "b_sgu", "w_sgu_out", "w_o", "g_final"]
    outs = [loss11.reshape(()), grad_x[None]]
    for part in range(4):
        outs += [per[nm][part] for nm in order]
    return tuple(outs)
```

```python
import jax
import jax.numpy as jnp
from jax import lax
from jax.experimental import pallas as pl
from jax.experimental.pallas import tpu as pltpu

F32 = jnp.float32
BF16 = jnp.bfloat16
MESH = pl.DeviceIdType.MESH
VMEM = pltpu.VMEM
ANY = pl.ANY

D = 1024
NDEV = 8
NSEG = 8
HEADS = 8
HD = 128
CHUNK = 128
KW = 31
HALO = 32
EPS = 1e-6
TM = 256
TMB = 512
RB = 32
RBC = 32
TAPG = 4
EXT = TM + HALO
NSMALL = 176
NEARLY = 144
NLATE = 40
ROW_CW = 16
ROW_WS = 48
W_CO = 0
W_SO, W_O = 0, 1

ADAM_LR = 0.001
ADAM_B1 = 0.9
ADAM_B2 = 0.999
ADAM_EPS = 1e-08
ADAM_WD = 0.01
ADAM_STEP = 10

INV_SQRT2 = 0.7071067811865476
INV_SQRT_2PI = 0.3989422804014327

NT = (((1,), (1,)), ((), ()))
TN = (((0,), (0,)), ((), ()))


def _cp(sem=None, vmem_mb=48):
    return pltpu.CompilerParams(dimension_semantics=sem, vmem_limit_bytes=vmem_mb * 1024 * 1024)


def _me():
    return lax.axis_index("x"), lax.axis_index("y"), lax.axis_index("c")


def _slot(px, py, pc):
    return 4 * px + 2 * py + pc


def _xor_peer(k):
    x, y, c = _me()
    return (1 - x if k & 4 else x, 1 - y if k & 2 else y, 1 - c if k & 1 else c)


def _allgather_phases(srcs, dst_at, send_sems, recv_sems, loc_sems):
    x, y, c = _me()
    me = (x, y, c)
    sibling = (x, y, 1 - c)
    chips = [(1 - x, y), (x, 1 - y), (1 - x, 1 - y)]
    na = len(srcs)

    def copy(a, k, block, to, src=None):
        d = dst_at(a, _slot(*block))
        return pltpu.make_async_remote_copy(
            src_ref=d if src is None else src, dst_ref=d,
            send_sem=send_sems.at[a, k], recv_sem=recv_sems.at[a, k],
            device_id=to, device_id_type=MESH)

    local = [pltpu.make_async_copy(srcs[a], dst_at(a, _slot(*me)), loc_sems.at[a]) for a in range(na)]
    first = []
    for a in range(na):
        first.append(copy(a, 0, me, sibling, src=srcs[a]))
        for j, chip in enumerate(chips):
            first.append(copy(a, 1 + j, me, (*chip, c), src=srcs[a]))
    passed = [copy(a, 4 + j, (*chip, c), sibling) for j, chip in enumerate(chips) for a in range(na)]

    def start():
        for cp in local + first:
            cp.start()

    def relay():
        for j, chip in enumerate(chips):
            for a in range(na):
                copy(a, 1 + j, (*chip, c), me).wait_recv()
                passed[j * na + a].start()

    def finish():
        for a in range(na):
            copy(a, 0, sibling, me).wait_recv()
        for j, chip in enumerate(chips):
            for a in range(na):
                copy(a, 4 + j, (*chip, 1 - c), me).wait_recv()
        for cp in first + passed:
            cp.wait_send()
        for cp in local:
            cp.wait()

    return start, relay, finish


def _allgather(srcs, dst_at, send_sems, recv_sems, loc_sems):
    start, relay, finish = _allgather_phases(srcs, dst_at, send_sems, recv_sems, loc_sems)
    start()
    relay()
    finish()


def _owner_exchange(gq, rq_ref, x_send, x_recv, x_loc):
    mx, my_, mc = _me()
    me = _slot(mx, my_, mc)

    def rows_of(a, slot):
        return gq[a].at[pl.ds(pl.multiple_of(slot * 128, 128), 128), :]

    def exchange(k, a, recv):
        px, py, pc = _xor_peer(k)
        peer = _slot(px, py, pc)
        return pltpu.make_async_remote_copy(
            src_ref=rows_of(a, me if recv else peer), dst_ref=rq_ref.at[peer if recv else me, a],
            send_sem=x_send.at[a, k - 1], recv_sem=x_recv.at[a, k - 1],
            device_id=(px, py, pc), device_id_type=MESH)

    local = [pltpu.make_async_copy(rows_of(a, me), rq_ref.at[me, a], x_loc.at[a]) for a in range(len(gq))]

    def start():
        for cp in local:
            cp.start()
        for k in range(1, NDEV):
            for a in range(len(gq)):
                exchange(k, a, False).start()

    def wait():
        for k in range(1, NDEV):
            for a in range(len(gq)):
                exchange(k, a, True).wait_recv()
        for k in range(1, NDEV):
            for a in range(len(gq)):
                exchange(k, a, False).wait_send()
        for cp in local:
            cp.wait()

    return start, wait


def _sig(v):
    return jax.nn.sigmoid(v)


def _gelu_parts(v):
    cdf = 0.5 * (1.0 + lax.erf(v * INV_SQRT2))
    pdf = jnp.exp(-0.5 * v * v) * INV_SQRT_2PI
    return v * cdf, cdf + v * pdf


def _ln_stats(v):
    mu = jnp.mean(v, axis=-1, keepdims=True)
    xc = v - mu
    var = jnp.mean(xc * xc, axis=-1, keepdims=True)
    rs = lax.rsqrt(var + EPS)
    return xc * rs, rs


def _ln_bwd(dn, n, rs):
    return rs * (dn - jnp.mean(dn, axis=-1, keepdims=True) - n * jnp.mean(dn * n, axis=-1, keepdims=True))


def _fold8(v):
    acc = v[0:8]
    for r in range(8, v.shape[0], 8):
        acc = acc + v[r:r + 8]
    return acc


def _rows(rb, n=RB):
    if isinstance(rb, int):
        return pl.ds(rb * n, n)
    return pl.ds(pl.multiple_of(rb * n, n), n)


def _ring_tiles(nt):
    return 4 if nt % 4 == 0 else (2 if nt % 2 == 0 else 1)


def _ring_base(i, nt):
    return pl.multiple_of((i % _ring_tiles(nt)) * TM, TM)


def _ring_rows(base, rb, n=RB):
    return pl.ds(pl.multiple_of(base + rb * n, n), n)


def _ring_full(i, nt):
    g = _ring_tiles(nt)
    return i % g == g - 1


def _ld(ref, rows):
    return ref[rows, :].astype(F32)


def _blocks(n, body, init=0):
    carry = init
    for rb in range(n):
        carry = body(rb, carry)
    return carry


def _shift_copies(s_ref):
    n = EXT - 8
    for m in range(1, 8):
        for c0 in range(0, n, 56):
            s_ref[m, c0:c0 + 56, :] = s_ref[0, c0 + m:c0 + m + 56, :]


def _ada_exchange(c_ref, w_ref, b_ref, mod_ref, call_ref, cdst, msrc, mdst, c_send, c_recv, m_send, m_recv):
    x, y, c = _me()
    me = _slot(x, y, c)

    def all_to_all(src_of, dst, ss, rs):
        def cp(k, recv):
            px, py, pc = _xor_peer(k)
            peer = _slot(px, py, pc)
            return pltpu.make_async_remote_copy(
                src_ref=src_of(me if recv else peer), dst_ref=dst.at[peer if recv else me],
                send_sem=ss.at[k - 1], recv_sem=rs.at[k - 1], device_id=(px, py, pc), device_id_type=MESH)
        for k in range(1, NDEV):
            cp(k, False).start()
        for k in range(1, NDEV):
            cp(k, True).wait_recv()
        for k in range(1, NDEV):
            cp(k, False).wait_send()

    cdst[me] = c_ref[...]
    all_to_all(lambda s: c_ref, cdst, c_send, c_recv)
    for b in range(NDEV):
        call_ref[b:b + 1, :] = cdst[b]
    m = jnp.dot(call_ref[...], w_ref[...], preferred_element_type=F32, precision=lax.Precision.HIGHEST)
    for b in range(NDEV):
        msrc[b] = m[b:b + 1, :]
    mdst[me] = msrc[me]
    all_to_all(lambda s: msrc.at[s], mdst, m_send, m_recv)
    full = jnp.concatenate([mdst[k] for k in range(NDEV)], axis=1) + b_ref[...]
    for r in range(3):
        mod_ref[r:r + 1, :] = full[:, r * D:(r + 1) * D]


def _ada_fwd(c, w_ada, b_ada):
    wcols = w_ada.shape[1]

    def body(*refs):
        _ada_exchange(*refs)

    vm = pl.BlockSpec(memory_space=VMEM)
    return pl.pallas_call(
        body, name="ada_fwd",
        out_shape=(jax.ShapeDtypeStruct((3, D), F32), jax.ShapeDtypeStruct((NDEV, D), F32)),
        in_specs=[vm, vm, vm], out_specs=(vm, vm),
        scratch_shapes=[VMEM((NDEV, 1, D), F32), VMEM((NDEV, 1, wcols), F32), VMEM((NDEV, 1, wcols), F32),
                        pltpu.SemaphoreType.DMA((7,)), pltpu.SemaphoreType.DMA((7,)),
                        pltpu.SemaphoreType.DMA((7,)), pltpu.SemaphoreType.DMA((7,))],
        compiler_params=_cp(vmem_mb=32),
    )(c, w_ada, b_ada)


def _prep_h(x, mod, g_pre):
    T = x.shape[0]

    def body(x_ref, mod_ref, g_ref, h_ref, ht_ref):
        def blk(rb, carry):
            rows = _rows(rb)
            xv = x_ref[rows, :]
            r = lax.rsqrt(jnp.mean(xv * xv, axis=-1, keepdims=True) + EPS)
            h_ref[rows, :] = ((xv * r) * g_ref[...] * (1.0 + mod_ref[1:2, :]) + mod_ref[0:1, :]).astype(BF16)
            return carry
        _blocks(TMB // RB, blk)
        ht_ref[...] = h_ref[...].T

    return pl.pallas_call(
        body, name="prep_h", grid=(T // TMB,),
        out_shape=(jax.ShapeDtypeStruct((T, D), BF16), jax.ShapeDtypeStruct((D, T), BF16)),
        in_specs=[pl.BlockSpec((TMB, D), lambda i: (i, 0)), pl.BlockSpec((3, D), lambda i: (0, 0)),
                  pl.BlockSpec((1, D), lambda i: (0, 0))],
        out_specs=(pl.BlockSpec((TMB, D), lambda i: (i, 0)), pl.BlockSpec((D, TMB), lambda i: (0, i))),
        compiler_params=_cp(("parallel",)),
    )(x, mod, g_pre)


def _ag_blocks(x, y, c):
    a = 1 - c
    n1 = (x ^ a, y ^ c)
    n2 = (x ^ c, y ^ a)
    dg = (1 - x, 1 - y)
    return [(x, y, c), (x, y, a), (*n1, c), (*n2, c), (*n1, a), (*n2, a), (*dg, c), (*dg, a)]


def _in_proj(h, w_in, order, w_co, w_so, w_o, conv_w):
    T = h.shape[0]
    tmp = min(2 * TMB, T)
    nb = T // tmp

    def body(order_ref, h_ref, w_ref, wco_ref, wso_ref, wo_ref, cw_ref, p_ref, wing_ref, wqg_ref, cwg_ref, own2_ref, land2_ref,
             wbuf, wq_st, wq2_st, send_sems, recv_sems, out_sems, ag_send, ag_recv, ag_loc, own_sems):
        j = pl.program_id(0)
        i = pl.program_id(1)
        blocks = _ag_blocks(*_me())
        me, sibling, n1, n2, n1o, n2o, dg, dgo = blocks
        own2 = [pltpu.make_async_copy(wq2_st, own2_ref, own_sems.at[0]),
                pltpu.make_async_copy(wq2_st, land2_ref.at[:, _slot(*me)], own_sems.at[1])]
        ag_start, ag_relay, ag_finish = _allgather_phases(
            [wq_st, cw_ref], lambda a, slot: wqg_ref.at[:, slot] if a == 0 else cwg_ref.at[slot], ag_send, ag_recv, ag_loc)

        def copy(k, block, to):
            d = wbuf.at[_slot(*block)]
            return pltpu.make_async_remote_copy(src_ref=d, dst_ref=d, send_sem=send_sems.at[k], recv_sem=recv_sems.at[k],
                                                device_id=to, device_id_type=MESH)

        def writeback(jj):
            s = _slot(*blocks[jj])
            return pltpu.make_async_copy(wbuf.at[s], wing_ref.at[s], out_sems.at[jj])

        sends = [copy(0, me, sibling), copy(1, me, n1), copy(2, me, n2)]
        relay = copy(3, n1, n2)
        passed = [copy(4, n1, sibling), copy(5, n2, sibling), copy(6, dg, sibling)]
        arrivals = {1: [(0, sibling)], 2: [(1, n1), (2, n2)], 4: [(5, n1o)], 5: [(4, n2o)], 6: [(3, dg)], 7: [(6, dgo)]}

        @pl.when((j == 0) & (i == 0))
        def _():
            wbuf[_slot(*me)] = w_ref[...].astype(BF16)
            for cp in sends:
                cp.start()
            writeback(0).start()
            wq2_st[W_SO] = wso_ref[...].astype(BF16)
            wq2_st[W_O] = wo_ref[...].astype(BF16)
            for cp in own2:
                cp.start()

        for jj in range(1, NSEG):
            @pl.when((j == jj) & (i == 0))
            def _(jj=jj):
                for k, block in arrivals.get(jj, []):
                    copy(k, block, me).wait_recv()
                if jj == 2:
                    relay.start()
                    passed[0].start()
                    passed[1].start()
                    writeback(3).start()
                    wq_st[0] = wco_ref[...].astype(BF16)
                    ag_start()
                if jj == 6:
                    passed[2].start()
                if jj != 3:
                    writeback(jj).start()
                if jj == NSEG - 1:
                    ag_relay()

        p_ref[...] = jnp.dot(h_ref[...], wbuf[order_ref[j]], preferred_element_type=F32).astype(BF16)

        @pl.when((j == NSEG - 1) & (i == nb - 1))
        def _():
            for cp in sends + [relay] + passed:
                cp.wait_send()
            for jj in range(NSEG):
                writeback(jj).wait()
            for cp in own2:
                cp.wait()
            ag_finish()

    vm = pl.BlockSpec(memory_space=VMEM)
    hbm = pl.BlockSpec(memory_space=ANY)
    return pl.pallas_call(
        body, name="in_proj",
        out_shape=(jax.ShapeDtypeStruct((T, NSEG * D), BF16), jax.ShapeDtypeStruct((NSEG, D, D), BF16),
                   jax.ShapeDtypeStruct((1, NDEV, 128, D), BF16), jax.ShapeDtypeStruct((NDEV, KW, 128), F32),
                   jax.ShapeDtypeStruct((2, 128, D), BF16), jax.ShapeDtypeStruct((2, NDEV, 128, D), BF16)),
        grid_spec=pltpu.PrefetchScalarGridSpec(
            num_scalar_prefetch=1, grid=(NSEG, nb),
            in_specs=[pl.BlockSpec((tmp, D), lambda j, i, o: (i, 0)), vm, vm, vm, vm, vm],
            out_specs=(pl.BlockSpec((tmp, D), lambda j, i, o: (i, o[j])), hbm, hbm, hbm, hbm, hbm),
            scratch_shapes=[VMEM((NSEG, D, D), BF16), VMEM((1, 128, D), BF16), VMEM((2, 128, D), BF16),
                            pltpu.SemaphoreType.DMA((7,)), pltpu.SemaphoreType.DMA((7,)), pltpu.SemaphoreType.DMA((NSEG,)),
                            pltpu.SemaphoreType.DMA((2, 7)), pltpu.SemaphoreType.DMA((2, 7)), pltpu.SemaphoreType.DMA((2,)),
                            pltpu.SemaphoreType.DMA((2,))]),
        compiler_params=_cp(("arbitrary", "arbitrary")),
    )(order, h, w_in, w_co, w_so, w_o, conv_w)


def _wq_start(own2, land2):
    def body(src_ref, land_ref, send_sems, recv_sems, src_thru, land_thru, token):
        barrier = pltpu.get_barrier_semaphore()
        for k in range(1, NDEV):
            pl.semaphore_signal(barrier, inc=1, device_id=_xor_peer(k), device_id_type=MESH)
        pl.semaphore_wait(barrier, NDEV - 1)
        x, y, c = _me()
        for k in range(1, NDEV):
            pltpu.make_async_remote_copy(src_ref=src_ref, dst_ref=land_ref.at[:, _slot(x, y, c)], send_sem=send_sems.at[k - 1],
                                         recv_sem=recv_sems.at[k - 1], device_id=_xor_peer(k), device_id_type=MESH).start()
        token[...] = jnp.zeros_like(token)

    hbm = pl.BlockSpec(memory_space=pltpu.HBM)
    sem = pl.BlockSpec(memory_space=pltpu.SEMAPHORE)
    return pl.pallas_call(
        body, name="wq_start",
        out_shape=(pltpu.SemaphoreType.DMA((NDEV - 1,)), pltpu.SemaphoreType.DMA((NDEV - 1,)),
                   pltpu.HBM(own2.shape, own2.dtype), pltpu.HBM(land2.shape, land2.dtype), jax.ShapeDtypeStruct((8, 128), F32)),
        in_specs=(hbm, hbm), out_specs=(sem, sem, hbm, hbm, pl.BlockSpec(memory_space=VMEM)),
        input_output_aliases={0: 2, 1: 3},
        compiler_params=pltpu.CompilerParams(has_side_effects=pltpu.SideEffectType.DATAFLOW_SIDE_EFFECTING, collective_id=2),
    )(pltpu.with_memory_space_constraint(own2, pltpu.HBM), pltpu.with_memory_space_constraint(land2, pltpu.HBM))


def _wq_wait(send_sems, recv_sems, src_thru, land_thru, after):
    def body(src_ref, land_ref, send_sems, recv_sems, after_ref, src_dead, got_ref):
        del after_ref, src_dead, got_ref
        for k in range(1, NDEV):
            px, py, pc = _xor_peer(k)
            cp = pltpu.make_async_remote_copy(src_ref=src_ref, dst_ref=land_ref.at[:, _slot(px, py, pc)],
                                              send_sem=send_sems.at[k - 1], recv_sem=recv_sems.at[k - 1],
                                              device_id=(px, py, pc), device_id_type=MESH)
            cp.wait_send()
            cp.wait_recv()

    hbm = pl.BlockSpec(memory_space=pltpu.HBM)
    sem = pl.BlockSpec(memory_space=pltpu.SEMAPHORE)
    return pl.pallas_call(
        body, name="wq_wait",
        out_shape=(pltpu.HBM(src_thru.shape, src_thru.dtype), pltpu.HBM(land_thru.shape, land_thru.dtype)),
        in_specs=(hbm, hbm, sem, sem, pl.BlockSpec(memory_space=ANY)), out_specs=(hbm, hbm),
        input_output_aliases={0: 0, 1: 1},
        compiler_params=pltpu.CompilerParams(has_side_effects=pltpu.SideEffectType.DATAFLOW_SIDE_EFFECTING),
    )(src_thru, land_thru, send_sems, recv_sems, after)[1]


def _fill_a_ext(i, s_ref, val_ref, glu_ref, hval_ref, hglu_ref):
    ah = hval_ref[...].astype(F32) * _sig(hglu_ref[...].astype(F32))
    s_ref[0, 0:HALO, :] = jnp.where(i > 0, ah, 0.0)

    def fill(rb, carry):
        rows = _rows(rb)
        s_ref[0, pl.ds(HALO + rb * RB, RB), :] = _ld(val_ref, rows) * _sig(_ld(glu_ref, rows))
        return carry
    _blocks(TM // RB, fill)


def _halo_prev(seg):
    hb = TM // HALO
    return pl.BlockSpec((HALO, D), lambda i: (jnp.maximum(i * hb - 1, 0), seg))


def _branch_a_fwd(p, cw, conv_b, lg, lb, wq, token):
    T = p.shape[0]

    def body(val_ref, glu_ref, z_ref, hval_ref, hglu_ref, cw_ref, cb_ref, lg_ref, lb_ref, w_ref, token_ref,
             yc_ref, ya_ref, s_ref, yp_ref):
        del token_ref
        i = pl.program_id(0)
        _fill_a_ext(i, s_ref, val_ref, glu_ref, hval_ref, hglu_ref)
        _shift_copies(s_ref)

        def conv(rb, carry):
            r0 = rb * RBC
            accs = [jnp.broadcast_to(cb_ref[...], (8, D))] * (RBC // 8)
            for k in range(KW):
                o = 2 + k
                w = cw_ref[k]
                accs = [acc + w * s_ref[o % 8, pl.ds(r0 + 8 * (o // 8 + g), 8), :]
                        for g, acc in enumerate(accs)]
            for g, acc in enumerate(accs):
                yc_ref[pl.ds(r0 + 8 * g, 8), :] = acc
            return carry
        for rb in range(TM // RBC):
            conv(rb, 0)

        def post(rb, carry):
            rows = _rows(rb)
            n, _ = _ln_stats(yc_ref[rows, :])
            l = n * lg_ref[...] + lb_ref[...]
            z = _ld(z_ref, rows)
            yp_ref[rows, :] = ((l * _sig(l)) * (z * _sig(z))).astype(BF16)
            return carry
        _blocks(TM // RB, post)
        ya_ref[...] = jnp.dot(yp_ref[...], w_ref[...], preferred_element_type=F32)

    tile = lambda seg: pl.BlockSpec((TM, D), lambda i: (i, seg))
    row = pl.BlockSpec((1, D), lambda i: (0, 0))
    return pl.pallas_call(
        body, name="branch_a_fwd", grid=(T // TM,),
        out_shape=(jax.ShapeDtypeStruct((T, D), F32), jax.ShapeDtypeStruct((T, D), F32)),
        in_specs=[tile(0), tile(1), tile(2), _halo_prev(0), _halo_prev(1),
                  pl.BlockSpec((KW, 8, D), lambda i: (0, 0, 0)), row, row, row,
                  pl.BlockSpec((None, D, D), lambda i: (W_CO, 0, 0)), pl.BlockSpec((8, 128), lambda i: (0, 0))],
        out_specs=(pl.BlockSpec((TM, D), lambda i: (i, 0)), pl.BlockSpec((TM, D), lambda i: (i, 0))),
        scratch_shapes=[VMEM((8, EXT, D), F32), VMEM((TM, D), BF16)],
        compiler_params=_cp(("parallel",)),
    )(p, p, p, p, p, cw, conv_b, lg, lb, wq, token)


def _masked_ws(ws_ref, wt_ref):
    tri = lax.broadcasted_iota(jnp.int32, (CHUNK, CHUNK), 0) >= lax.broadcasted_iota(jnp.int32, (CHUNK, CHUNK), 1)
    for h in range(HEADS):
        wt_ref[h] = jnp.where(tri, ws_ref[h], 0.0).astype(BF16)


def _sgu_mix(wt_ref, vl_ref, bst_ref, s_ref):
    for h in range(HEADS):
        cs = slice(h * HD, (h + 1) * HD)
        s = jnp.dot(wt_ref[h], _chunks_on_lanes(vl_ref, cs), preferred_element_type=F32) + bst_ref[:, h:h + 1]
        for ck in range(TM // CHUNK):
            s_ref[ck * CHUNK:(ck + 1) * CHUNK, cs] = s[:, ck * HD:(ck + 1) * HD]


def _chunks_on_lanes(ref, cs):
    return jnp.concatenate([ref[ck * CHUNK:(ck + 1) * CHUNK, cs] for ck in range(TM // CHUNK)], axis=1)


def _branch_b_fwd(p, ws, bst, sg, sb, wq):
    T = p.shape[0]

    def body(pu_ref, pv_ref, pz_ref, ws_ref, bst_ref, sg_ref, sb_ref, w_ref, yb_ref,
             wt_ref, vl_ref, t_ref, s_ref, yp_ref):
        _masked_ws(ws_ref, wt_ref)

        def pre(rb, carry):
            rows = _rows(rb)
            vg, _ = _gelu_parts(_ld(pv_ref, rows))
            vn, _ = _ln_stats(vg)
            vl_ref[rows, :] = (vn * sg_ref[...] + sb_ref[...]).astype(BF16)
            u, _ = _gelu_parts(_ld(pu_ref, rows))
            z = _ld(pz_ref, rows)
            t_ref[rows, :] = u * (z * _sig(z))
            return carry
        _blocks(TM // RB, pre)
        _sgu_mix(wt_ref, vl_ref, bst_ref, s_ref)

        def post(rb, carry):
            rows = _rows(rb)
            yp_ref[rows, :] = (t_ref[rows, :] * s_ref[rows, :]).astype(BF16)
            return carry
        _blocks(TM // RB, post)
        yb_ref[...] = jnp.dot(yp_ref[...], w_ref[...], preferred_element_type=F32)

    tile = lambda seg: pl.BlockSpec((TM, D), lambda i: (i, seg))
    row = pl.BlockSpec((1, D), lambda i: (0, 0))
    return pl.pallas_call(
        body, name="branch_b_fwd", grid=(T // TM,),
        out_shape=jax.ShapeDtypeStruct((T, D), F32),
        in_specs=[tile(3), tile(4), tile(5), pl.BlockSpec((HEADS, CHUNK, CHUNK), lambda i: (0, 0, 0)),
                  pl.BlockSpec((CHUNK, HEADS), lambda i: (0, 0)), row, row,
                  pl.BlockSpec((None, D, D), lambda i: (W_SO, 0, 0))],
        out_specs=pl.BlockSpec((TM, D), lambda i: (i, 0)),
        scratch_shapes=[VMEM((HEADS, CHUNK, CHUNK), BF16), VMEM((TM, D), BF16), VMEM((TM, D), F32),
                        VMEM((TM, D), F32), VMEM((TM, D), BF16)],
        compiler_params=_cp(("parallel",)),
    )(p, p, p, ws, bst, sg, sb, wq)


def _merge_loss(p, ya, yb, x, tgt, mod, g_final, wq):
    T = x.shape[0]
    nt = T // TM

    def body(ga_ref, gb_ref, ya_ref, yb_ref, x_ref, t_ref, mod_ref, gf_ref, w_ref,
             dx2_ref, dya_ref, dyb_ref, dp_ref, gwo_ref, st_ref,
             mrg_s, out_s, dout_s, dm_s, gw_acc):
        i = pl.program_id(0)
        base = _ring_base(i, nt)
        cur = pl.ds(base, TM)

        @pl.when(i == 0)
        def _():
            gw_acc[...] = jnp.zeros_like(gw_acc)
            st_ref[...] = jnp.zeros_like(st_ref)

        def merge(rb, carry):
            rows = _rows(rb)
            mrg_s[_ring_rows(base, rb), :] = (_sig(_ld(ga_ref, rows)) * ya_ref[rows, :]
                                              + _sig(_ld(gb_ref, rows)) * yb_ref[rows, :]).astype(BF16)
            return carry
        _blocks(TM // RB, merge)
        out_s[...] = jnp.dot(mrg_s[cur, :], w_ref[...], preferred_element_type=F32)

        def head(rb, carry):
            loss, gg, dg = carry
            rows = _rows(rb)
            gate = mod_ref[2:3, :]
            gf = gf_ref[...]
            out = out_s[rows, :]
            x2 = x_ref[rows, :] + gate * out
            r2 = lax.rsqrt(jnp.mean(x2 * x2, axis=-1, keepdims=True) + EPS)
            x2n = x2 * r2
            diff = x2n * gf - t_ref[rows, :]
            dy = diff * (1.0 / D)
            dx2n = dy * gf
            dx2 = r2 * (dx2n - x2n * jnp.mean(dx2n * x2n, axis=-1, keepdims=True))
            dx2_ref[rows, :] = dx2
            dout_s[_ring_rows(base, rb), :] = (dx2 * gate).astype(BF16)
            return loss + _fold8(diff * diff), gg + _fold8(dy * x2n), dg + _fold8(dx2 * out)
        zero = jnp.zeros((8, D), F32)
        loss, gg, dg = _blocks(TM // RB, head, (zero, zero, zero))
        st_ref[0] = st_ref[0] + loss * (0.5 / D)
        st_ref[1] = st_ref[1] + gg
        st_ref[2] = st_ref[2] + dg

        dm_s[...] = lax.dot_general(dout_s[cur, :], w_ref[...], NT, preferred_element_type=F32)

        @pl.when(_ring_full(i, nt))
        def _():
            gw_acc[...] += lax.dot_general(mrg_s[...], dout_s[...], TN, preferred_element_type=F32)

        def split(rb, carry):
            rows = _rows(rb)
            dm = dm_s[rows, :]
            sa = _sig(_ld(ga_ref, rows))
            sb = _sig(_ld(gb_ref, rows))
            dya_ref[rows, :] = (dm * sa).astype(BF16)
            dyb_ref[rows, :] = (dm * sb).astype(BF16)
            dp_ref[0, rows, :] = (dm * ya_ref[rows, :] * (sa * (1.0 - sa))).astype(BF16)
            dp_ref[1, rows, :] = (dm * yb_ref[rows, :] * (sb * (1.0 - sb))).astype(BF16)
            return carry
        _blocks(TM // RB, split)

        @pl.when(i == nt - 1)
        def _():
            gwo_ref[...] = gw_acc[...].astype(BF16)

    tile = pl.BlockSpec((TM, D), lambda i: (i, 0))
    ring = VMEM((_ring_tiles(nt) * TM, D), BF16)
    return pl.pallas_call(
        body, name="merge_loss", grid=(nt,),
        out_shape=(jax.ShapeDtypeStruct((T, D), F32), jax.ShapeDtypeStruct((T, D), BF16), jax.ShapeDtypeStruct((T, D), BF16),
                   jax.ShapeDtypeStruct((NSEG, T, D), BF16), jax.ShapeDtypeStruct((D, D), BF16),
                   jax.ShapeDtypeStruct((3, 8, D), F32)),
        in_specs=[pl.BlockSpec((TM, D), lambda i: (i, 6)), pl.BlockSpec((TM, D), lambda i: (i, 7)), tile, tile, tile, tile,
                  pl.BlockSpec((3, D), lambda i: (0, 0)), pl.BlockSpec((1, D), lambda i: (0, 0)),
                  pl.BlockSpec((None, D, D), lambda i: (W_O, 0, 0))],
        out_specs=(tile, tile, tile, pl.BlockSpec((2, TM, D), lambda i: (3, i, 0)),
                   pl.BlockSpec((D, D), lambda i: (0, 0)), pl.BlockSpec((3, 8, D), lambda i: (0, 0, 0))),
        scratch_shapes=[ring, VMEM((TM, D), F32), ring, VMEM((TM, D), F32), VMEM((D, D), F32)],
        compiler_params=_cp(("arbitrary",)),
    )(p, p, ya, yb, x, tgt, mod, g_final, wq)


def _branch_b_bwd(dp, dyb, p, ws, bst, sg, sb, wq):
    T = p.shape[0]
    nt = T // TM

    def body(dp_in, dyb_ref, pu_ref, pv_ref, pz_ref, ws_ref, bst_ref, sg_ref, sb_ref, w_ref,
             dp_ref, gw_ref, gws_ref, gbs_ref, st_ref,
             wt_ref, d_s, vl_s, vn_s, gpv_s, rs_s, s_s, ds_s, ds32_s, yp_s, dy_s, dvl_s, gw_acc):
        del dp_in
        i = pl.program_id(0)
        base = _ring_base(i, nt)

        @pl.when(i == 0)
        def _():
            gw_acc[...] = jnp.zeros_like(gw_acc)
            gws_ref[...] = jnp.zeros_like(gws_ref)
            gbs_ref[...] = jnp.zeros_like(gbs_ref)
            st_ref[...] = jnp.zeros_like(st_ref)
            _masked_ws(ws_ref, wt_ref)

        d_s[...] = lax.dot_general(dyb_ref[...], w_ref[...], NT, preferred_element_type=F32)
        dy_s[pl.ds(base, TM), :] = dyb_ref[...]

        def pre(rb, carry):
            rows = _rows(rb)
            vg, gpv = _gelu_parts(_ld(pv_ref, rows))
            vn, rs = _ln_stats(vg)
            vl_s[rows, :] = (vn * sg_ref[...] + sb_ref[...]).astype(BF16)
            vn_s[rows, :] = vn
            gpv_s[rows, :] = gpv
            rs_s[rows, :] = rs
            return carry
        _blocks(TM // RB, pre)
        _sgu_mix(wt_ref, vl_s, bst_ref, s_s)

        def mid(rb, carry):
            rows = _rows(rb)
            u, gpu = _gelu_parts(_ld(pu_ref, rows))
            z = _ld(pz_ref, rows)
            sz = _sig(z)
            siluz = z * sz
            d = d_s[rows, :]
            s = s_s[rows, :]
            t = u * siluz
            yp_s[_ring_rows(base, rb), :] = (t * s).astype(BF16)
            dp_ref[0, rows, :] = (d * s * siluz * gpu).astype(BF16)
            ds = d * t
            ds32_s[rows, :] = ds
            ds_s[rows, :] = ds.astype(BF16)
            dp_ref[2, rows, :] = (d * u * s * (sz * (1.0 + z * (1.0 - sz)))).astype(BF16)
            return carry
        _blocks(TM // RB, mid)

        for h in range(HEADS):
            cs = slice(h * HD, (h + 1) * HD)
            dsh = _chunks_on_lanes(ds_s, cs)
            dvl = lax.dot_general(wt_ref[h], dsh, TN, preferred_element_type=F32)
            gbs = gbs_ref[h]
            for ck in range(TM // CHUNK):
                r = slice(ck * CHUNK, (ck + 1) * CHUNK)
                dvl_s[r, cs] = dvl[:, ck * HD:(ck + 1) * HD]
                gbs = gbs + ds32_s[r, cs]
            gbs_ref[h] = gbs
            gws_ref[h] += lax.dot_general(dsh, _chunks_on_lanes(vl_s, cs), NT, preferred_element_type=F32)

        def post(rb, carry):
            g_sg, g_sb = carry
            rows = _rows(rb)
            dvl = dvl_s[rows, :]
            vn = vn_s[rows, :]
            dvg = _ln_bwd(dvl * sg_ref[...], vn, rs_s[rows, :])
            dp_ref[1, rows, :] = (dvg * gpv_s[rows, :]).astype(BF16)
            return g_sg + _fold8(dvl * vn), g_sb + _fold8(dvl)
        zero = jnp.zeros((8, D), F32)
        g_sg, g_sb = _blocks(TM // RB, post, (zero, zero))
        st_ref[0] = st_ref[0] + g_sg
        st_ref[1] = st_ref[1] + g_sb

        @pl.when(_ring_full(i, nt))
        def _():
            gw_acc[...] += lax.dot_general(yp_s[...], dy_s[...], TN, preferred_element_type=F32)

        @pl.when(i == nt - 1)
        def _():
            gw_ref[...] = gw_acc[...].astype(BF16)

    tile = lambda seg: pl.BlockSpec((TM, D), lambda i: (i, seg))
    row = pl.BlockSpec((1, D), lambda i: (0, 0))
    hh = pl.BlockSpec((HEADS, CHUNK, CHUNK), lambda i: (0, 0, 0))
    ring = VMEM((_ring_tiles(nt) * TM, D), BF16)
    return pl.pallas_call(
        body, name="branch_b_bwd", grid=(nt,),
        out_shape=(jax.ShapeDtypeStruct((NSEG, T, D), BF16), jax.ShapeDtypeStruct((D, D), BF16),
                   jax.ShapeDtypeStruct((HEADS, CHUNK, CHUNK), F32), jax.ShapeDtypeStruct((HEADS, CHUNK, HD), F32),
                   jax.ShapeDtypeStruct((2, 8, D), F32)),
        in_specs=[pl.BlockSpec(memory_space=ANY), pl.BlockSpec((TM, D), lambda i: (i, 0)), tile(3), tile(4), tile(5),
                  hh, pl.BlockSpec((CHUNK, HEADS), lambda i: (0, 0)), row, row,
                  pl.BlockSpec((None, D, D), lambda i: (W_SO, 0, 0))],
        out_specs=(pl.BlockSpec((3, TM, D), lambda i: (1, i, 0)), pl.BlockSpec((D, D), lambda i: (0, 0)), hh, hh,
                   pl.BlockSpec((2, 8, D), lambda i: (0, 0, 0))),
        scratch_shapes=[VMEM((HEADS, CHUNK, CHUNK), BF16), VMEM((TM, D), F32), VMEM((TM, D), BF16), VMEM((TM, D), F32),
                        VMEM((TM, D), F32), VMEM((TM, 1), F32), VMEM((TM, D), F32), VMEM((TM, D), BF16), VMEM((TM, D), F32),
                        ring, ring, VMEM((TM, D), F32), VMEM((D, D), F32)],
        input_output_aliases={0: 0},
        compiler_params=_cp(("arbitrary",)),
    )(dp, dyb, p, p, p, ws, bst, sg, sb, wq)


def _branch_a_bwd1(dp, dya, yc, p, lg, lb, wq):
    T = p.shape[0]
    nt = T // TM

    def body(dp_in, dya_ref, yc_ref, z_ref, lg_ref, lb_ref, w_ref, dp_ref, dyc_ref, gw_ref, st_ref,
             d_s, yp_s, dy_s, gw_acc):
        del dp_in
        i = pl.program_id(0)
        base = _ring_base(i, nt)

        @pl.when(i == 0)
        def _():
            gw_acc[...] = jnp.zeros_like(gw_acc)
            st_ref[...] = jnp.zeros_like(st_ref)

        d_s[...] = lax.dot_general(dya_ref[...], w_ref[...], NT, preferred_element_type=F32)
        dy_s[pl.ds(base, TM), :] = dya_ref[...]

        def blk(rb, carry):
            g_lg, g_lb, g_cb = carry
            rows = _rows(rb)
            n, rs = _ln_stats(yc_ref[rows, :])
            l = n * lg_ref[...] + lb_ref[...]
            sgl = _sig(l)
            sl = l * sgl
            z = _ld(z_ref, rows)
            sz = _sig(z)
            siluz = z * sz
            d = d_s[rows, :]
            yp_s[_ring_rows(base, rb), :] = (sl * siluz).astype(BF16)
            dp_ref[rows, :] = (d * sl * (sz * (1.0 + z * (1.0 - sz)))).astype(BF16)
            dl = (d * siluz) * (sgl * (1.0 + l * (1.0 - sgl)))
            dyc = _ln_bwd(dl * lg_ref[...], n, rs)
            dyc_ref[rows, :] = dyc
            return g_lg + _fold8(dl * n), g_lb + _fold8(dl), g_cb + _fold8(dyc)
        zero = jnp.zeros((8, D), F32)
        g_lg, g_lb, g_cb = _blocks(TM // RB, blk, (zero, zero, zero))
        st_ref[0] = st_ref[0] + g_lg
        st_ref[1] = st_ref[1] + g_lb
        st_ref[2] = st_ref[2] + g_cb

        @pl.when(_ring_full(i, nt))
        def _():
            gw_acc[...] += lax.dot_general(yp_s[...], dy_s[...], TN, preferred_element_type=F32)

        @pl.when(i == nt - 1)
        def _():
            gw_ref[...] = gw_acc[...].astype(BF16)

    tile = pl.BlockSpec((TM, D), lambda i: (i, 0))
    row = pl.BlockSpec((1, D), lambda i: (0, 0))
    ring = VMEM((_ring_tiles(nt) * TM, D), BF16)
    return pl.pallas_call(
        body, name="branch_a_bwd1", grid=(nt,),
        out_shape=(jax.ShapeDtypeStruct((NSEG, T, D), BF16), jax.ShapeDtypeStruct((T, D), F32),
                   jax.ShapeDtypeStruct((D, D), BF16), jax.ShapeDtypeStruct((3, 8, D), F32)),
        in_specs=[pl.BlockSpec(memory_space=ANY), tile, tile, pl.BlockSpec((TM, D), lambda i: (i, 2)), row, row,
                  pl.BlockSpec((None, D, D), lambda i: (0, 0, 0))],
        out_specs=(pl.BlockSpec((None, TM, D), lambda i: (2, i, 0)), tile, pl.BlockSpec((D, D), lambda i: (0, 0)),
                   pl.BlockSpec((3, 8, D), lambda i: (0, 0, 0))),
        scratch_shapes=[VMEM((TM, D), F32), ring, ring, VMEM((D, D), F32)],
        input_output_aliases={0: 0},
        compiler_params=_cp(("arbitrary",)),
    )(dp, dya, yc, p, lg, lb, wq)


def _branch_a_bwd2(dp, dyc, p, cw, gw_co, gw_so, gw_o, early):
    T = p.shape[0]
    nt = T // TM
    hb = TM // HALO

    def body(dp_in, dyc_ref, hdyc_ref, val_ref, glu_ref, cw_ref, gco_ref, gso_ref, go_ref, e_ref,
             dp_ref, gcw_ref, rq_ref, eg_ref, a_s, sd_ref, da_s, x_send, x_recv, x_loc, ag_send, ag_recv, ag_loc):
        del dp_in
        i = pl.program_id(0)
        start_exchange, wait_exchange = _owner_exchange((gco_ref, gso_ref, go_ref), rq_ref, x_send, x_recv, x_loc)
        ag_start, ag_relay, ag_finish = _allgather_phases([e_ref], lambda a, slot: eg_ref.at[slot],
                                                          ag_send, ag_recv, ag_loc)

        @pl.when(i == 0)
        def _():
            gcw_ref[...] = jnp.zeros_like(gcw_ref)
            ag_start()
            start_exchange()

        @pl.when(i == nt // 2)
        def _():
            ag_relay()

        def fill(rb, carry):
            rows = _rows(rb)
            a_s[rows, :] = _ld(val_ref, rows) * _sig(_ld(glu_ref, rows))
            return carry
        _blocks(TM // RB, fill)
        sd_ref[0, 0:TM, :] = dyc_ref[...]
        sd_ref[0, TM:EXT, :] = jnp.where(i < nt - 1, hdyc_ref[...], 0.0)
        _shift_copies(sd_ref)

        def conv_t(rb, carry):
            r0 = rb * RBC
            accs = [jnp.zeros((8, D), F32)] * (RBC // 8)
            for k in range(KW):
                o = KW - 1 - k
                w = cw_ref[k]
                accs = [acc + w * sd_ref[o % 8, pl.ds(r0 + 8 * (o // 8 + g), 8), :]
                        for g, acc in enumerate(accs)]
            for g, acc in enumerate(accs):
                da_s[pl.ds(r0 + 8 * g, 8), :] = acc
            return carry
        for rb in range(TM // RBC):
            conv_t(rb, 0)

        for k0 in range(0, KW, TAPG):
            taps = list(range(k0, min(k0 + TAPG, KW)))

            def tap_group(rb, accs, taps=taps):
                for u in range(2):
                    r0 = rb * 32 + 16 * u
                    a = a_s[pl.ds(r0, 16), :]
                    out = []
                    for k, acc in zip(taps, accs):
                        o = KW - 1 - k
                        d = sd_ref[o % 8, pl.ds(r0 + 8 * (o // 8), 16), :]
                        out.append(acc + _fold8(a * d))
                    accs = tuple(out)
                return accs
            sums = tuple(jnp.zeros((8, D), F32) for _ in taps)
            for rb in range(TM // 32):
                sums = tap_group(rb, sums)
            for k, s in zip(taps, sums):
                gcw_ref[k] = gcw_ref[k] + s

        def glu_b(rb, carry):
            rows = _rows(rb)
            da = da_s[rows, :]
            sg = _sig(_ld(glu_ref, rows))
            dp_ref[0, rows, :] = (da * sg).astype(BF16)
            dp_ref[1, rows, :] = (da * _ld(val_ref, rows) * (sg * (1.0 - sg))).astype(BF16)
            return carry
        _blocks(TM // RB, glu_b)

        @pl.when(i == nt - 1)
        def _():
            wait_exchange()
            ag_finish()

    tile = lambda seg: pl.BlockSpec((TM, D), lambda i: (i, seg))
    hbm = pl.BlockSpec(memory_space=ANY)
    return pl.pallas_call(
        body, name="branch_a_bwd2", grid=(nt,),
        out_shape=(jax.ShapeDtypeStruct((NSEG, T, D), BF16), jax.ShapeDtypeStruct((32, 8, D), F32),
                   jax.ShapeDtypeStruct((NDEV, 3, 128, D), BF16), jax.ShapeDtypeStruct((NDEV, NEARLY, D), F32)),
        in_specs=[hbm, pl.BlockSpec((TM, D), lambda i: (i, 0)),
                  pl.BlockSpec((HALO, D), lambda i: (jnp.minimum((i + 1) * hb, nt * hb - 1), 0)),
                  tile(0), tile(1), pl.BlockSpec((KW, 8, D), lambda i: (0, 0, 0)),
                  hbm, hbm, hbm, hbm],
        out_specs=(pl.BlockSpec((2, TM, D), lambda i: (0, i, 0)), pl.BlockSpec((32, 8, D), lambda i: (0, 0, 0)), hbm, hbm),
        scratch_shapes=[VMEM((TM, D), F32), VMEM((8, EXT, D), F32), VMEM((TM, D), F32),
                        pltpu.SemaphoreType.DMA((3, 7)), pltpu.SemaphoreType.DMA((3, 7)), pltpu.SemaphoreType.DMA((3,)),
                        pltpu.SemaphoreType.DMA((1, 7)), pltpu.SemaphoreType.DMA((1, 7)), pltpu.SemaphoreType.DMA((1,))],
        input_output_aliases={0: 0},
        compiler_params=_cp(("arbitrary",)),
    )(dp, dyc, dyc, p, p, cw, gw_co, gw_so, gw_o, early)


def _in_bwd_dx(dp, wing, x, dx2, mod, g_pre, token):
    T = x.shape[0]
    nt = T // TM

    def body(dp_ref, w_ref, x_ref, dx2_ref, mod_ref, g_ref, token_ref, gx_ref, st_ref, dh_s):
        del token_ref
        i = pl.program_id(0)

        @pl.when(i == 0)
        def _():
            st_ref[...] = jnp.zeros_like(st_ref)

        dh_s[...] = lax.dot_general(dp_ref[0], w_ref[0], NT, preferred_element_type=F32)
        for j in range(1, NSEG):
            dh_s[...] += lax.dot_general(dp_ref[j], w_ref[j], NT, preferred_element_type=F32)

        def blk(rb, carry):
            d_sh, d_sc, g_g = carry
            rows = _rows(rb)
            xv = x_ref[rows, :]
            r = lax.rsqrt(jnp.mean(xv * xv, axis=-1, keepdims=True) + EPS)
            xn = xv * r
            g = g_ref[...]
            hpre = xn * g
            dh = dh_s[rows, :]
            dhp = dh * (1.0 + mod_ref[1:2, :])
            dxn = dhp * g
            gx_ref[rows, :] = dx2_ref[rows, :] + r * (dxn - xn * jnp.mean(dxn * xn, axis=-1, keepdims=True))
            return d_sh + _fold8(dh), d_sc + _fold8(dh * hpre), g_g + _fold8(dhp * xn)
        zero = jnp.zeros((8, D), F32)
        d_sh, d_sc, g_g = _blocks(TM // RB, blk, (zero, zero, zero))
        st_ref[0] = st_ref[0] + d_sh
        st_ref[1] = st_ref[1] + d_sc
        st_ref[2] = st_ref[2] + g_g

    tile = pl.BlockSpec((TM, D), lambda i: (i, 0))
    return pl.pallas_call(
        body, name="in_bwd_dx", grid=(nt,),
        out_shape=(jax.ShapeDtypeStruct((T, D), F32), jax.ShapeDtypeStruct((3, 8, D), F32)),
        in_specs=[pl.BlockSpec((NSEG, TM, D), lambda i: (0, i, 0)),
                  pl.BlockSpec((NSEG, D, D), lambda i: (0, 0, 0), pipeline_mode=pl.Buffered(1)),
                  tile, tile, pl.BlockSpec((3, D), lambda i: (0, 0)), pl.BlockSpec((1, D), lambda i: (0, 0)),
                  pl.BlockSpec((8, 128), lambda i: (0, 0))],
        out_specs=(tile, pl.BlockSpec((3, 8, D), lambda i: (0, 0, 0))),
        scratch_shapes=[VMEM((TM, D), F32)],
        compiler_params=_cp(("arbitrary",), vmem_mb=56),
    )(dp, wing, x, dx2, mod, g_pre, token)


RS_ORDER = (5, 4, 3, 2, 7, 6, 1, 0)
RS_HALF = D // 2


def _in_bwd_dw(ht, dp, order):
    T = ht.shape[1]
    tmw = min(4 * TMB, T)
    nb = T // tmw
    nu = 2 * NSEG

    def body(order_ref, ht_ref, dp_ref, csum_ref, g_ref, acc, sbuf, rsib, send_sems, recv_sems, out_sems):
        j = pl.program_id(0)
        i = pl.program_id(1)
        x, y, c = _me()
        sibling = (x, y, 1 - c)

        def to_sibling(s):
            return pltpu.make_async_remote_copy(src_ref=sbuf.at[s], dst_ref=rsib.at[s], send_sem=send_sems.at[s],
                                                recv_sem=recv_sems.at[s], device_id=sibling, device_id_type=MESH)

        def to_csum(t):
            cols = pl.ds((t % 2) * RS_HALF, RS_HALF)
            return pltpu.make_async_copy(sbuf.at[8 + t], csum_ref.at[t // 2, :, cols], out_sems.at[t])

        @pl.when(i == 0)
        def _():
            acc[...] = jnp.zeros_like(acc)
        acc[...] += jnp.dot(ht_ref[...], dp_ref[...], preferred_element_type=F32)

        for jj in range(nu):
            chip, half, mine = jj // 4, (jj % 4) // 2, jj % 2
            s = 2 * chip + half

            @pl.when((j == jj) & (i == nb - 1))
            def _(chip=chip, half=half, mine=mine, s=s):
                if not mine:
                    sbuf[s] = acc[...].astype(BF16)
                    to_sibling(s).start()
                elif chip < 3:
                    to_sibling(s).wait_recv()
                    sbuf[8 + s] = (acc[...] + rsib[s].astype(F32)).astype(BF16)
                    to_csum(s).start()
                else:
                    to_sibling(s).wait_recv()
                    g_ref[:, half * RS_HALF:(half + 1) * RS_HALF] = acc[...] + rsib[s].astype(F32)

        @pl.when((j == nu - 1) & (i == nb - 1))
        def _():
            for s in range(8):
                to_sibling(s).wait_send()
            for t in range(6):
                to_csum(t).wait()

    hbm = pl.BlockSpec(memory_space=ANY)
    return pl.pallas_call(
        body, name="in_bwd_dw",
        out_shape=(jax.ShapeDtypeStruct((3, D, D), BF16), jax.ShapeDtypeStruct((D, D), F32)),
        grid_spec=pltpu.PrefetchScalarGridSpec(
            num_scalar_prefetch=1, grid=(nu, nb),
            in_specs=[pl.BlockSpec((D, tmw), lambda j, i, o: (0, i)),
                      pl.BlockSpec((None, tmw, RS_HALF), lambda j, i, o: (o[2 * (j // 4) + j % 2], i, (j % 4) // 2))],
            out_specs=(hbm, pl.BlockSpec((D, D), lambda j, i, o: (0, 0))),
            scratch_shapes=[VMEM((D, RS_HALF), F32), VMEM((14, D, RS_HALF), BF16), VMEM((8, D, RS_HALF), BF16),
                            pltpu.SemaphoreType.DMA((8,)), pltpu.SemaphoreType.DMA((8,)), pltpu.SemaphoreType.DMA((6,))]),
        compiler_params=_cp(("arbitrary", "arbitrary"), vmem_mb=56),
    )(order, ht, dp)


def _gather_late(late):
    def body(sm_ref, smg_ref, ag_send, ag_recv, ag_loc):
        _allgather([sm_ref], lambda a, slot: smg_ref.at[slot], ag_send, ag_recv, ag_loc)

    hbm = pl.BlockSpec(memory_space=ANY)
    return pl.pallas_call(
        body, name="gather_late",
        out_shape=jax.ShapeDtypeStruct((NDEV, NLATE, D), F32),
        in_specs=[hbm], out_specs=hbm,
        scratch_shapes=[pltpu.SemaphoreType.DMA((1, 7)), pltpu.SemaphoreType.DMA((1, 7)), pltpu.SemaphoreType.DMA((1,))],
        compiler_params=_cp(vmem_mb=32),
    )(late)


RS_CHIPS = ((1, 0), (0, 1), (1, 1))


def _rs_peer(q):
    x, y, c = _me()
    fx, fy = RS_CHIPS[q]
    return (1 - x if fx else x, 1 - y if fy else y, c)


def _rs_start(csum):
    def body(csum_ref, land_ref, send_sems, recv_sems, csum_thru, land_thru, token):
        barrier = pltpu.get_barrier_semaphore()
        for q in range(3):
            pl.semaphore_signal(barrier, inc=1, device_id=_rs_peer(q), device_id_type=MESH)
        pl.semaphore_wait(barrier, 3)
        for q in range(3):
            pltpu.make_async_remote_copy(src_ref=csum_ref.at[q], dst_ref=land_ref.at[q], send_sem=send_sems.at[q],
                                         recv_sem=recv_sems.at[q], device_id=_rs_peer(q), device_id_type=MESH).start()
        token[...] = jnp.zeros_like(token)

    hbm = pl.BlockSpec(memory_space=pltpu.HBM)
    sem = pl.BlockSpec(memory_space=pltpu.SEMAPHORE)
    land = pltpu.with_memory_space_constraint(lax.empty(csum.shape, csum.dtype), pltpu.HBM)
    return pl.pallas_call(
        body, name="rs_start",
        out_shape=(pltpu.SemaphoreType.DMA((3,)), pltpu.SemaphoreType.DMA((3,)), pltpu.HBM(csum.shape, csum.dtype),
                   pltpu.HBM(csum.shape, csum.dtype), jax.ShapeDtypeStruct((8, 128), F32)),
        in_specs=(hbm, hbm), out_specs=(sem, sem, hbm, hbm, pl.BlockSpec(memory_space=VMEM)),
        input_output_aliases={0: 2, 1: 3},
        compiler_params=pltpu.CompilerParams(has_side_effects=pltpu.SideEffectType.DATAFLOW_SIDE_EFFECTING, collective_id=1),
    )(pltpu.with_memory_space_constraint(csum, pltpu.HBM), land)


def _rs_wait(send_sems, recv_sems, csum_thru, land_thru, after):
    def body(csum_ref, land_ref, send_sems, recv_sems, after_ref, csum_dead, got_ref):
        del after_ref, csum_dead, got_ref
        for q in range(3):
            cp = pltpu.make_async_remote_copy(src_ref=csum_ref.at[q], dst_ref=land_ref.at[q], send_sem=send_sems.at[q],
                                              recv_sem=recv_sems.at[q], device_id=_rs_peer(q), device_id_type=MESH)
            cp.wait_send()
            cp.wait_recv()

    hbm = pl.BlockSpec(memory_space=pltpu.HBM)
    sem = pl.BlockSpec(memory_space=pltpu.SEMAPHORE)
    return pl.pallas_call(
        body, name="rs_wait",
        out_shape=(pltpu.HBM(csum_thru.shape, csum_thru.dtype), pltpu.HBM(csum_thru.shape, csum_thru.dtype)),
        in_specs=(hbm, hbm, sem, sem, pl.BlockSpec(memory_space=ANY)), out_specs=(hbm, hbm),
        input_output_aliases={0: 0, 1: 1},
        compiler_params=pltpu.CompilerParams(has_side_effects=pltpu.SideEffectType.DATAFLOW_SIDE_EFFECTING),
    )(csum_thru, land_thru, send_sems, recv_sems, after)[1]


def _pack_early(st_a, st_b, st_m, gws, gbs):
    def body(sa_ref, sb_ref, sm_ref, gws_ref, gbs_ref, out_ref):
        out_ref[...] = jnp.zeros_like(out_ref)
        fold = lambda v: jnp.sum(v, axis=0, keepdims=True)
        out_ref[0:1, :] = fold(sm_ref[2])
        out_ref[1:2, :] = fold(sa_ref[2])
        out_ref[2:3, :] = fold(sa_ref[0])
        out_ref[3:4, :] = fold(sa_ref[1])
        out_ref[4:5, :] = fold(sb_ref[0])
        out_ref[5:6, :] = fold(sb_ref[1])
        out_ref[7:8, :] = fold(sm_ref[1])
        out_ref[8:9, :] = fold(sm_ref[0])
        tri = lax.broadcasted_iota(jnp.int32, (CHUNK, CHUNK), 0) >= lax.broadcasted_iota(jnp.int32, (CHUNK, CHUNK), 1)
        for h in range(HEADS):
            out_ref[16:16 + CHUNK, h * CHUNK:(h + 1) * CHUNK] = jnp.where(tri, gws_ref[h], 0.0)
            out_ref[6:7, h * CHUNK:(h + 1) * CHUNK] = fold(gbs_ref[h].T)

    vm = pl.BlockSpec(memory_space=VMEM)
    return pl.pallas_call(
        body, name="pack_early", out_shape=jax.ShapeDtypeStruct((NEARLY, D), F32),
        in_specs=[vm] * 5, out_specs=vm, compiler_params=_cp(vmem_mb=32),
    )(st_a, st_b, st_m, gws, gbs)


def _pack_late(st_x, gcw):
    def body(sx_ref, gcw_ref, out_ref):
        out_ref[...] = jnp.zeros_like(out_ref)
        fold = lambda v: jnp.sum(v, axis=0, keepdims=True)
        for r in range(3):
            out_ref[r:r + 1, :] = fold(sx_ref[r])
        for k in range(KW):
            out_ref[8 + k:9 + k, :] = fold(gcw_ref[k])

    vm = pl.BlockSpec(memory_space=VMEM)
    return pl.pallas_call(
        body, name="pack_late", out_shape=jax.ShapeDtypeStruct((NLATE, D), F32),
        in_specs=[vm] * 2, out_specs=vm, compiler_params=_cp(vmem_mb=32),
    )(st_x, gcw)


def _pack_params(name, b_ada, g_pre, conv_b, lg, lb, sg, sb, b_sgu, g_final, w_sgu_t):
    def body(ba_ref, *refs):
        rows, ws_ref, out_ref = refs[:8], refs[8], refs[9]
        out_ref[...] = jnp.zeros_like(out_ref)
        for r in range(3):
            out_ref[r:r + 1, :] = ba_ref[:, r * D:(r + 1) * D]
        for r, ref in enumerate(rows):
            out_ref[3 + r:4 + r, :] = ref[...]
        out_ref[ROW_WS:ROW_WS + CHUNK, :] = ws_ref[...]

    vm = pl.BlockSpec(memory_space=VMEM)
    return pl.pallas_call(
        body, name=name, out_shape=jax.ShapeDtypeStruct((NSMALL, D), F32),
        in_specs=[vm] * 10, out_specs=vm, compiler_params=_cp(vmem_mb=32),
    )(b_ada, g_pre, conv_b, lg, lb, sg, sb, b_sgu, g_final, w_sgu_t)


def _adam(w, g, m, v):
    m2 = ADAM_B1 * m + (1.0 - ADAM_B1) * g
    v2 = ADAM_B2 * v + (1.0 - ADAM_B2) * (g * g)
    m_hat = m2 / (1.0 - ADAM_B1 ** ADAM_STEP)
    v_hat = v2 / (1.0 - ADAM_B2 ** ADAM_STEP)
    delta = -ADAM_LR * (m_hat / (jnp.sqrt(v_hat) + ADAM_EPS) + ADAM_WD * w)
    return delta, m2, v2


def _small_finish(eg, lg, w, m, v):
    def body(eg_ref, lg_ref, w_ref, m_ref, v_ref, loss_ref, g_ref, d_ref, m2_ref, v2_ref):
        e = eg_ref[0]
        l = lg_ref[0]
        for s in range(1, NDEV):
            e = e + eg_ref[s]
            l = l + lg_ref[s]
        g_ref[...] = jnp.zeros_like(g_ref)
        g_ref[0:2, :] = l[0:2, :]
        g_ref[2:3, :] = e[0:1, :]
        g_ref[3:4, :] = l[2:3, :]
        g_ref[4:12, :] = e[1:9, :]
        g_ref[ROW_CW:ROW_CW + KW, :] = l[8:8 + KW, :]
        g_ref[ROW_WS:ROW_WS + CHUNK, :] = e[16:16 + CHUNK, :]
        loss_ref[...] = jnp.sum(e[8:9, :], axis=1, keepdims=True)
        d_ref[...], m2_ref[...], v2_ref[...] = _adam(w_ref[...], g_ref[...], m_ref[...], v_ref[...])

    vm = pl.BlockSpec(memory_space=VMEM)
    sd = jax.ShapeDtypeStruct((NSMALL, D), F32)
    return pl.pallas_call(
        body, name="small_finish", out_shape=(jax.ShapeDtypeStruct((1, 1), F32), sd, sd, sd, sd),
        in_specs=[vm] * 5, out_specs=(vm,) * 5, compiler_params=_cp(vmem_mb=32),
    )(eg, lg, w, m, v)


def _ada_grad_adam(ct, dm, w, m, v):
    def body(ct_ref, dm_ref, w_ref, m_ref, v_ref, g_ref, d_ref, m2_ref, v2_ref):
        g = ct_ref[:, 0:1] * dm_ref[0:1, :]
        for b in range(1, NDEV):
            g = g + ct_ref[:, b:b + 1] * dm_ref[b:b + 1, :]
        g_ref[...] = g
        d_ref[...], m2_ref[...], v2_ref[...] = _adam(w_ref[...], g, m_ref[...], v_ref[...])

    vm = pl.BlockSpec(memory_space=VMEM)
    sd = jax.ShapeDtypeStruct(w.shape, F32)
    return pl.pallas_call(
        body, name="ada_grad_adam", out_shape=(sd, sd, sd, sd),
        in_specs=[vm] * 5, out_specs=(vm,) * 4, compiler_params=_cp(vmem_mb=32),
    )(ct, dm, w, m, v)


def _adam_f32(name, g, w, m, v, rows=None):
    R, C = w.shape
    rows = R if rows is None else rows

    def body(g_ref, w_ref, m_ref, v_ref, d_ref, m2_ref, v2_ref):
        d_ref[...], m2_ref[...], v2_ref[...] = _adam(w_ref[...], g_ref[...], m_ref[...], v_ref[...])

    tile = pl.BlockSpec((rows, C), lambda i: (i, 0))
    sd = jax.ShapeDtypeStruct(w.shape, F32)
    return pl.pallas_call(
        body, name=name, grid=(R // rows,), out_shape=(sd, sd, sd), in_specs=[tile] * 4, out_specs=(tile,) * 3,
        compiler_params=_cp(("parallel",), vmem_mb=32),
    )(g, w, m, v)


def _adam_w_in(gown, got, w, m, v, rows=256):
    R, C = w.shape

    def body(go_ref, got_ref, w_ref, m_ref, v_ref, g_ref, d_ref, m2_ref, v2_ref):
        g = go_ref[...]
        for q in range(3):
            g = g + got_ref[q].astype(F32)
        g_ref[...] = g
        d_ref[...], m2_ref[...], v2_ref[...] = _adam(w_ref[...], g, m_ref[...], v_ref[...])

    tile = pl.BlockSpec((rows, C), lambda i: (i, 0))
    sd = jax.ShapeDtypeStruct((R, C), F32)
    return pl.pallas_call(
        body, name="adam_w_in", grid=(R // rows,), out_shape=(sd, sd, sd, sd),
        in_specs=[tile, pl.BlockSpec((3, rows, C), lambda i: (0, i, 0)), tile, tile, tile], out_specs=(tile,) * 4,
        compiler_params=_cp(("parallel",), vmem_mb=32),
    )(gown, got, w, m, v)


def _adam_reduce(name, recv, w, m, v, rows, recv_spec):
    R, C = w.shape

    def body(r_ref, w_ref, m_ref, v_ref, g_ref, d_ref, m2_ref, v2_ref):
        g = r_ref[0].astype(F32)
        for s in range(1, NDEV):
            g = g + r_ref[s].astype(F32)
        g_ref[...] = g
        d_ref[...], m2_ref[...], v2_ref[...] = _adam(w_ref[...], g, m_ref[...], v_ref[...])

    tile = pl.BlockSpec((rows, C), lambda i: (i, 0))
    sd = jax.ShapeDtypeStruct((R, C), F32)
    return pl.pallas_call(
        body, name=name, grid=(R // rows,), out_shape=(sd, sd, sd, sd),
        in_specs=[recv_spec, tile, tile, tile], out_specs=(tile,) * 4,
        compiler_params=_cp(("parallel",)),
    )(recv, w, m, v)


def kernel(x, c, w_ada, b_ada, g_pre, w_in, conv_w, conv_b, conv_ln_g, conv_ln_b, w_conv_out, sgu_ln_g, sgu_ln_b, w_sgu, b_sgu, w_sgu_out, w_o, g_final, loss_target, m_w_ada, m_b_ada, m_g_pre, m_w_in, m_conv_w, m_conv_b, m_conv_ln_g, m_conv_ln_b, m_w_conv_out, m_sgu_ln_g, m_sgu_ln_b, m_w_sgu, m_b_sgu, m_w_sgu_out, m_w_o, m_g_final, v_w_ada, v_b_ada, v_g_pre, v_w_in, v_conv_w, v_conv_b, v_conv_ln_g, v_conv_ln_b, v_w_conv_out, v_sgu_ln_g, v_sgu_ln_b, v_w_sgu, v_b_sgu, v_w_sgu_out, v_w_o, v_g_final):
    T = x.shape[1]
    assert T % TMB == 0 and x.shape[2] == D
    xs = x[0]
    tgt = loss_target[0]
    my = 4 * lax.axis_index("x") + 2 * lax.axis_index("y") + lax.axis_index("c")

    mod, cg = _ada_fwd(c, w_ada[0], b_ada)
    gf = g_final.reshape(1, D)
    bst = b_sgu[0].T

    h, ht = _prep_h(xs, mod, g_pre)
    ag_order = jnp.stack([_slot(*b) for b in _ag_blocks(lax.axis_index("x"), lax.axis_index("y"), lax.axis_index("c"))])
    p, wing, wcog, cwg, own2, land2 = _in_proj(h, w_in[0], ag_order.astype(jnp.int32),
                                               w_conv_out[0], w_sgu_out[0], w_o[0], conv_w[0])
    wco = wcog.reshape(1, D, D)
    cw = jnp.broadcast_to(jnp.transpose(cwg, (1, 0, 2)).reshape(KW, 1, D), (KW, 8, D))
    wq_send, wq_recv, own2_thru, land2_thru, wq_token = _wq_start(own2, land2)
    yc, ya = _branch_a_fwd(p, cw, conv_b, conv_ln_g, conv_ln_b, wco, wq_token)
    wq2 = _wq_wait(wq_send, wq_recv, own2_thru, land2_thru, yc).reshape(2, D, D)
    yb = _branch_b_fwd(p, w_sgu[0], bst, sgu_ln_g, sgu_ln_b, wq2)
    dx2, dya, dyb, dp, gw_o, st_m = _merge_loss(p, ya, yb, xs, tgt, mod, gf, wq2)

    dp, gw_so, gws, gbs, st_b = _branch_b_bwd(dp, dyb, p, w_sgu[0], bst, sgu_ln_g, sgu_ln_b, wq2)
    dp, dyc, gw_co, st_a = _branch_a_bwd1(dp, dya, yc, p, conv_ln_g, conv_ln_b, wco)
    early = _pack_early(st_a, st_b, st_m, gws, gbs)
    dp, gcw, rq, eg = _branch_a_bwd2(dp, dyc, p, cw, gw_co, gw_so, gw_o, early)
    csum, gown = _in_bwd_dw(ht, dp, my ^ jnp.array(RS_ORDER, jnp.int32))
    rs_send, rs_recv, csum_thru, land_thru, token = _rs_start(csum)
    grad_x, st_x = _in_bwd_dx(dp, wing, xs, dx2, mod, g_pre, token)
    got = _rs_wait(rs_send, rs_recv, csum_thru, land_thru, st_x)
    lg = _gather_late(_pack_late(st_x, gcw))

    def pack(name, b_ada_, g_pre_, conv_b_, lg_, lb_, sg_, sb_, b_sgu_, gfin_, w_sgu_):
        return _pack_params(name, b_ada_, g_pre_, conv_b_, lg_, lb_, sg_, sb_, b_sgu_.reshape(1, D), gfin_.reshape(1, D),
                            jnp.transpose(w_sgu_[0], (1, 0, 2)).reshape(CHUNK, D))

    pw = pack("pack_w", b_ada, g_pre, conv_b, conv_ln_g, conv_ln_b, sgu_ln_g, sgu_ln_b, b_sgu, g_final, w_sgu)
    pm = pack("pack_m", m_b_ada, m_g_pre, m_conv_b, m_conv_ln_g, m_conv_ln_b, m_sgu_ln_g, m_sgu_ln_b, m_b_sgu, m_g_final,
              m_w_sgu)
    pv = pack("pack_v", v_b_ada, v_g_pre, v_conv_b, v_conv_ln_g, v_conv_ln_b, v_sgu_ln_g, v_sgu_ln_b, v_b_sgu, v_g_final,
              v_w_sgu)
    loss11, sg_, sd_, sm_, sv_ = _small_finish(eg, lg, pw, pm, pv)

    def unpack(a):
        return dict(
            b_ada=a[0:3].reshape(1, 3 * D), g_pre=a[3:4], conv_b=a[4:5], conv_ln_g=a[5:6], conv_ln_b=a[6:7],
            sgu_ln_g=a[7:8], sgu_ln_b=a[8:9], b_sgu=a[9:10].reshape(1, HEADS, CHUNK), g_final=a[10],
            w_sgu=jnp.transpose(a[ROW_WS:ROW_WS + CHUNK].reshape(CHUNK, HEADS, CHUNK), (1, 0, 2))[None])
    small = [unpack(a) for a in (sg_, sd_, sm_, sv_)]

    g_cw = lax.dynamic_slice_in_dim(sg_[ROW_CW:ROW_CW + KW], my * 128, 128, axis=1)
    d_cw, m_cw, v_cw = _adam_f32("adam_conv_w", g_cw, conv_w[0], m_conv_w[0], v_conv_w[0])

    wcols = w_ada.shape[2]
    dm_all = jnp.concatenate([lg[:, 0], lg[:, 1], eg[:, 0]], axis=1)
    dm_mine = lax.dynamic_slice_in_dim(dm_all, my * wcols, wcols, axis=1)
    g_ada, d_ada, m_ada, v_ada = _ada_grad_adam(cg.T, dm_mine, w_ada[0], m_w_ada[0], v_w_ada[0])

    g_in, d_in, m_in, v_in = _adam_w_in(gown, got, w_in[0], m_w_in[0], v_w_in[0])
    big = {}
    for j, (nm, w_, m_, v_) in enumerate((("w_conv_out", w_conv_out, m_w_conv_out, v_w_conv_out),
                                          ("w_sgu_out", w_sgu_out, m_w_sgu_out, v_w_sgu_out),
                                          ("w_o", w_o, m_w_o, v_w_o))):
        big[nm] = _adam_reduce("adam_" + nm, rq, w_[0], m_[0], v_[0], 128,
                               pl.BlockSpec((NDEV, None, 128, D), lambda i, j=j: (0, j, 0, 0)))

    per = {
        "w_ada": tuple(a[None] for a in (g_ada, d_ada, m_ada, v_ada)),
        "w_in": tuple(a[None] for a in (g_in, d_in, m_in, v_in)),
        "conv_w": tuple(a[None] for a in (g_cw, d_cw, m_cw, v_cw)),
    }
    for nm in ("w_conv_out", "w_sgu_out", "w_o"):
        per[nm] = tuple(a[None] for a in big[nm])
    for nm in ("b_ada", "g_pre", "conv_b", "conv_ln_g", "conv_ln_b", "sgu_ln_g", "sgu_ln_b", "w_sgu", "b_sgu", "g_final"):
        per[nm] = tuple(s[nm] for s in small)

    order = ["w_ada", "b_ada", "g_pre", "w_in", "conv_w", "conv_b", "conv_ln_g", "conv_ln_b", "w_conv_out",
             "sgu_ln_g", "sgu_ln_b", "w_sgu", "b_sgu", "w_sgu_out", "w_o", "g_final"]
    outs = [loss11.reshape(()), grad_x[None]]
    for part in range(4):
        outs += [per[nm][part] for nm in order]
    return tuple(outs)
```

```python
import jax
import jax.numpy as jnp
from jax import lax
from jax.experimental import pallas as pl
from jax.experimental.pallas import tpu as pltpu

F32 = jnp.float32
BF16 = jnp.bfloat16
MESH = pl.DeviceIdType.MESH
VMEM = pltpu.VMEM
ANY = pl.ANY

D = 1024
NDEV = 8
NSEG = 8
HEADS = 8
HD = 128
CHUNK = 128
KW = 31
HALO = 32
EPS = 1e-6
TM = 256
TMB = 512
RB = 32
RBC = 32
TAPG = 4
EXT = TM + HALO
NSMALL = 176
NEARLY = 144
NLATE = 40
ROW_CW = 16
ROW_WS = 48
W_CO = 0
W_SO, W_O = 0, 1

ADAM_LR = 0.001
ADAM_B1 = 0.9
ADAM_B2 = 0.999
ADAM_EPS = 1e-08
ADAM_WD = 0.01
ADAM_STEP = 10

INV_SQRT2 = 0.7071067811865476
INV_SQRT_2PI = 0.3989422804014327

NT = (((1,), (1,)), ((), ()))
TN = (((0,), (0,)), ((), ()))


def _cp(sem=None, vmem_mb=48):
    return pltpu.CompilerParams(dimension_semantics=sem, vmem_limit_bytes=vmem_mb * 1024 * 1024)


def _me():
    return lax.axis_index("x"), lax.axis_index("y"), lax.axis_index("c")


def _slot(px, py, pc):
    return 4 * px + 2 * py + pc


def _xor_peer(k):
    x, y, c = _me()
    return (1 - x if k & 4 else x, 1 - y if k & 2 else y, 1 - c if k & 1 else c)


def _allgather_phases(srcs, dst_at, send_sems, recv_sems, loc_sems):
    x, y, c = _me()
    me = (x, y, c)
    sibling = (x, y, 1 - c)
    chips = [(1 - x, y), (x, 1 - y), (1 - x, 1 - y)]
    na = len(srcs)

    def copy(a, k, block, to, src=None):
        d = dst_at(a, _slot(*block))
        return pltpu.make_async_remote_copy(
            src_ref=d if src is None else src, dst_ref=d,
            send_sem=send_sems.at[a, k], recv_sem=recv_sems.at[a, k],
            device_id=to, device_id_type=MESH)

    local = [pltpu.make_async_copy(srcs[a], dst_at(a, _slot(*me)), loc_sems.at[a]) for a in range(na)]
    first = []
    for a in range(na):
        first.append(copy(a, 0, me, sibling, src=srcs[a]))
        for j, chip in enumerate(chips):
            first.append(copy(a, 1 + j, me, (*chip, c), src=srcs[a]))
    passed = [copy(a, 4 + j, (*chip, c), sibling) for j, chip in enumerate(chips) for a in range(na)]

    def start():
        for cp in local + first:
            cp.start()

    def relay():
        for j, chip in enumerate(chips):
            for a in range(na):
                copy(a, 1 + j, (*chip, c), me).wait_recv()
                passed[j * na + a].start()

    def finish():
        for a in range(na):
            copy(a, 0, sibling, me).wait_recv()
        for j, chip in enumerate(chips):
            for a in range(na):
                copy(a, 4 + j, (*chip, 1 - c), me).wait_recv()
        for cp in first + passed:
            cp.wait_send()
        for cp in local:
            cp.wait()

    return start, relay, finish


def _allgather(srcs, dst_at, send_sems, recv_sems, loc_sems):
    start, relay, finish = _allgather_phases(srcs, dst_at, send_sems, recv_sems, loc_sems)
    start()
    relay()
    finish()


def _owner_exchange(gq, rq_ref, x_send, x_recv, x_loc):
    mx, my_, mc = _me()
    me = _slot(mx, my_, mc)

    def rows_of(a, slot):
        return gq[a].at[pl.ds(pl.multiple_of(slot * 128, 128), 128), :]

    def exchange(k, a, recv):
        px, py, pc = _xor_peer(k)
        peer = _slot(px, py, pc)
        return pltpu.make_async_remote_copy(
            src_ref=rows_of(a, me if recv else peer), dst_ref=rq_ref.at[peer if recv else me, a],
            send_sem=x_send.at[a, k - 1], recv_sem=x_recv.at[a, k - 1],
            device_id=(px, py, pc), device_id_type=MESH)

    local = [pltpu.make_async_copy(rows_of(a, me), rq_ref.at[me, a], x_loc.at[a]) for a in range(len(gq))]

    def start():
        for cp in local:
            cp.start()
        for k in range(1, NDEV):
            for a in range(len(gq)):
                exchange(k, a, False).start()

    def wait():
        for k in range(1, NDEV):
            for a in range(len(gq)):
                exchange(k, a, True).wait_recv()
        for k in range(1, NDEV):
            for a in range(len(gq)):
                exchange(k, a, False).wait_send()
        for cp in local:
            cp.wait()

    return start, wait


def _sig(v):
    return jax.nn.sigmoid(v)


def _gelu_parts(v):
    cdf = 0.5 * (1.0 + lax.erf(v * INV_SQRT2))
    pdf = jnp.exp(-0.5 * v * v) * INV_SQRT_2PI
    return v * cdf, cdf + v * pdf


def _ln_stats(v):
    mu = jnp.mean(v, axis=-1, keepdims=True)
    xc = v - mu
    var = jnp.mean(xc * xc, axis=-1, keepdims=True)
    rs = lax.rsqrt(var + EPS)
    return xc * rs, rs


def _ln_bwd(dn, n, rs):
    return rs * (dn - jnp.mean(dn, axis=-1, keepdims=True) - n * jnp.mean(dn * n, axis=-1, keepdims=True))


def _fold8(v):
    acc = v[0:8]
    for r in range(8, v.shape[0], 8):
        acc = acc + v[r:r + 8]
    return acc


def _rows(rb, n=RB):
    if isinstance(rb, int):
        return pl.ds(rb * n, n)
    return pl.ds(pl.multiple_of(rb * n, n), n)


def _ring_tiles(nt):
    return 4 if nt % 4 == 0 else (2 if nt % 2 == 0 else 1)


def _ring_base(i, nt):
    if _ring_tiles(nt) == 1:
        return 0
    return pl.multiple_of((i % _ring_tiles(nt)) * TM, TM)


def _ring_rows(base, rb, n=RB):
    if isinstance(base, int) and isinstance(rb, int):
        return pl.ds(base + rb * n, n)
    return pl.ds(pl.multiple_of(base + rb * n, n), n)


def _when_ring_full(i, nt):
    g = _ring_tiles(nt)
    if g == 1:
        return lambda fn: fn()
    return pl.when(i % g == g - 1)


def _ld(ref, rows):
    return ref[rows, :].astype(F32)


def _blocks(n, body, init=0):
    carry = init
    for rb in range(n):
        carry = body(rb, carry)
    return carry


def _shift_copies(s_ref):
    n = EXT - 8
    for m in range(1, 8):
        for c0 in range(0, n, 56):
            s_ref[m, c0:c0 + 56, :] = s_ref[0, c0 + m:c0 + m + 56, :]


def _ada_exchange(c_ref, w_ref, b_ref, mod_ref, call_ref, cdst, msrc, mdst, c_send, c_recv, m_send, m_recv):
    x, y, c = _me()
    me = _slot(x, y, c)

    def all_to_all(src_of, dst, ss, rs):
        def cp(k, recv):
            px, py, pc = _xor_peer(k)
            peer = _slot(px, py, pc)
            return pltpu.make_async_remote_copy(
                src_ref=src_of(me if recv else peer), dst_ref=dst.at[peer if recv else me],
                send_sem=ss.at[k - 1], recv_sem=rs.at[k - 1], device_id=(px, py, pc), device_id_type=MESH)
        for k in range(1, NDEV):
            cp(k, False).start()
        for k in range(1, NDEV):
            cp(k, True).wait_recv()
        for k in range(1, NDEV):
            cp(k, False).wait_send()

    cdst[me] = c_ref[...]
    all_to_all(lambda s: c_ref, cdst, c_send, c_recv)
    for b in range(NDEV):
        call_ref[b:b + 1, :] = cdst[b]
    m = jnp.dot(call_ref[...], w_ref[...], preferred_element_type=F32, precision=lax.Precision.HIGHEST)
    for b in range(NDEV):
        msrc[b] = m[b:b + 1, :]
    mdst[me] = msrc[me]
    all_to_all(lambda s: msrc.at[s], mdst, m_send, m_recv)
    full = jnp.concatenate([mdst[k] for k in range(NDEV)], axis=1) + b_ref[...]
    for r in range(3):
        mod_ref[r:r + 1, :] = full[:, r * D:(r + 1) * D]


def _ada_fwd(c, w_ada, b_ada):
    wcols = w_ada.shape[1]

    def body(*refs):
        _ada_exchange(*refs)

    vm = pl.BlockSpec(memory_space=VMEM)
    return pl.pallas_call(
        body, name="ada_fwd",
        out_shape=(jax.ShapeDtypeStruct((3, D), F32), jax.ShapeDtypeStruct((NDEV, D), F32)),
        in_specs=[vm, vm, vm], out_specs=(vm, vm),
        scratch_shapes=[VMEM((NDEV, 1, D), F32), VMEM((NDEV, 1, wcols), F32), VMEM((NDEV, 1, wcols), F32),
                        pltpu.SemaphoreType.DMA((7,)), pltpu.SemaphoreType.DMA((7,)),
                        pltpu.SemaphoreType.DMA((7,)), pltpu.SemaphoreType.DMA((7,))],
        compiler_params=_cp(vmem_mb=32),
    )(c, w_ada, b_ada)


def _prep_h(x, mod, g_pre):
    T = x.shape[0]

    def body(x_ref, mod_ref, g_ref, h_ref, ht_ref):
        def blk(rb, carry):
            rows = _rows(rb)
            xv = x_ref[rows, :]
            r = lax.rsqrt(jnp.mean(xv * xv, axis=-1, keepdims=True) + EPS)
            h_ref[rows, :] = ((xv * r) * g_ref[...] * (1.0 + mod_ref[1:2, :]) + mod_ref[0:1, :]).astype(BF16)
            return carry
        _blocks(TMB // RB, blk)
        ht_ref[...] = h_ref[...].T

    return pl.pallas_call(
        body, name="prep_h", grid=(T // TMB,),
        out_shape=(jax.ShapeDtypeStruct((T, D), BF16), jax.ShapeDtypeStruct((D, T), BF16)),
        in_specs=[pl.BlockSpec((TMB, D), lambda i: (i, 0)), pl.BlockSpec((3, D), lambda i: (0, 0)),
                  pl.BlockSpec((1, D), lambda i: (0, 0))],
        out_specs=(pl.BlockSpec((TMB, D), lambda i: (i, 0)), pl.BlockSpec((D, TMB), lambda i: (0, i))),
        compiler_params=_cp(("parallel",)),
    )(x, mod, g_pre)


def _ag_blocks(x, y, c):
    a = 1 - c
    n1 = (x ^ a, y ^ c)
    n2 = (x ^ c, y ^ a)
    dg = (1 - x, 1 - y)
    return [(x, y, c), (x, y, a), (*n1, c), (*n2, c), (*n1, a), (*n2, a), (*dg, c), (*dg, a)]


def _in_proj(h, w_in, order, w_co, w_so, w_o, conv_w):
    T = h.shape[0]
    tmp = min(2 * TMB, T)
    nb = T // tmp

    def body(order_ref, h_ref, w_ref, wco_ref, wso_ref, wo_ref, cw_ref, p_ref, wing_ref, wqg_ref, cwg_ref, own2_ref, land2_ref,
             wbuf, wq_st, wq2_st, send_sems, recv_sems, out_sems, ag_send, ag_recv, ag_loc, own_sems):
        j = pl.program_id(0)
        i = pl.program_id(1)
        blocks = _ag_blocks(*_me())
        me, sibling, n1, n2, n1o, n2o, dg, dgo = blocks
        own2 = [pltpu.make_async_copy(wq2_st, own2_ref, own_sems.at[0]),
                pltpu.make_async_copy(wq2_st, land2_ref.at[:, _slot(*me)], own_sems.at[1])]
        ag_start, ag_relay, ag_finish = _allgather_phases(
            [wq_st, cw_ref], lambda a, slot: wqg_ref.at[:, slot] if a == 0 else cwg_ref.at[slot], ag_send, ag_recv, ag_loc)

        def copy(k, block, to):
            d = wbuf.at[_slot(*block)]
            return pltpu.make_async_remote_copy(src_ref=d, dst_ref=d, send_sem=send_sems.at[k], recv_sem=recv_sems.at[k],
                                                device_id=to, device_id_type=MESH)

        def writeback(jj):
            s = _slot(*blocks[jj])
            return pltpu.make_async_copy(wbuf.at[s], wing_ref.at[s], out_sems.at[jj])

        sends = [copy(0, me, sibling), copy(1, me, n1), copy(2, me, n2)]
        relay = copy(3, n1, n2)
        passed = [copy(4, n1, sibling), copy(5, n2, sibling), copy(6, dg, sibling)]
        arrivals = {1: [(0, sibling)], 2: [(1, n1), (2, n2)], 4: [(5, n1o)], 5: [(4, n2o)], 6: [(3, dg)], 7: [(6, dgo)]}

        @pl.when((j == 0) & (i == 0))
        def _():
            wbuf[_slot(*me)] = w_ref[...].astype(BF16)
            for cp in sends:
                cp.start()
            writeback(0).start()
            wq2_st[W_SO] = wso_ref[...].astype(BF16)
            wq2_st[W_O] = wo_ref[...].astype(BF16)
            for cp in own2:
                cp.start()

        for jj in range(1, NSEG):
            @pl.when((j == jj) & (i == 0))
            def _(jj=jj):
                for k, block in arrivals.get(jj, []):
                    copy(k, block, me).wait_recv()
                if jj == 2:
                    relay.start()
                    passed[0].start()
                    passed[1].start()
                    writeback(3).start()
                    wq_st[0] = wco_ref[...].astype(BF16)
                    ag_start()
                if jj == 6:
                    passed[2].start()
                if jj != 3:
                    writeback(jj).start()
                if jj == NSEG - 1:
                    ag_relay()

        p_ref[...] = jnp.dot(h_ref[...], wbuf[order_ref[j]], preferred_element_type=F32).astype(BF16)

        @pl.when((j == NSEG - 1) & (i == nb - 1))
        def _():
            for cp in sends + [relay] + passed:
                cp.wait_send()
            for jj in range(NSEG):
                writeback(jj).wait()
            for cp in own2:
                cp.wait()
            ag_finish()

    vm = pl.BlockSpec(memory_space=VMEM)
    hbm = pl.BlockSpec(memory_space=ANY)
    return pl.pallas_call(
        body, name="in_proj",
        out_shape=(jax.ShapeDtypeStruct((T, NSEG * D), BF16), jax.ShapeDtypeStruct((NSEG, D, D), BF16),
                   jax.ShapeDtypeStruct((1, NDEV, 128, D), BF16), jax.ShapeDtypeStruct((NDEV, KW, 128), F32),
                   jax.ShapeDtypeStruct((2, 128, D), BF16), jax.ShapeDtypeStruct((2, NDEV, 128, D), BF16)),
        grid_spec=pltpu.PrefetchScalarGridSpec(
            num_scalar_prefetch=1, grid=(NSEG, nb),
            in_specs=[pl.BlockSpec((tmp, D), lambda j, i, o: (i, 0)), vm, vm, vm, vm, vm],
            out_specs=(pl.BlockSpec((tmp, D), lambda j, i, o: (i, o[j])), hbm, hbm, hbm, hbm, hbm),
            scratch_shapes=[VMEM((NSEG, D, D), BF16), VMEM((1, 128, D), BF16), VMEM((2, 128, D), BF16),
                            pltpu.SemaphoreType.DMA((7,)), pltpu.SemaphoreType.DMA((7,)), pltpu.SemaphoreType.DMA((NSEG,)),
                            pltpu.SemaphoreType.DMA((2, 7)), pltpu.SemaphoreType.DMA((2, 7)), pltpu.SemaphoreType.DMA((2,)),
                            pltpu.SemaphoreType.DMA((2,))]),
        compiler_params=_cp(("arbitrary", "arbitrary")),
    )(order, h, w_in, w_co, w_so, w_o, conv_w)


def _wq_start(own2, land2):
    def body(src_ref, land_ref, send_sems, recv_sems, src_thru, land_thru, token):
        barrier = pltpu.get_barrier_semaphore()
        for k in range(1, NDEV):
            pl.semaphore_signal(barrier, inc=1, device_id=_xor_peer(k), device_id_type=MESH)
        pl.semaphore_wait(barrier, NDEV - 1)
        x, y, c = _me()
        for k in range(1, NDEV):
            pltpu.make_async_remote_copy(src_ref=src_ref, dst_ref=land_ref.at[:, _slot(x, y, c)], send_sem=send_sems.at[k - 1],
                                         recv_sem=recv_sems.at[k - 1], device_id=_xor_peer(k), device_id_type=MESH).start()
        token[...] = jnp.zeros_like(token)

    hbm = pl.BlockSpec(memory_space=pltpu.HBM)
    sem = pl.BlockSpec(memory_space=pltpu.SEMAPHORE)
    return pl.pallas_call(
        body, name="wq_start",
        out_shape=(pltpu.SemaphoreType.DMA((NDEV - 1,)), pltpu.SemaphoreType.DMA((NDEV - 1,)),
                   pltpu.HBM(own2.shape, own2.dtype), pltpu.HBM(land2.shape, land2.dtype), jax.ShapeDtypeStruct((8, 128), F32)),
        in_specs=(hbm, hbm), out_specs=(sem, sem, hbm, hbm, pl.BlockSpec(memory_space=VMEM)),
        input_output_aliases={0: 2, 1: 3},
        compiler_params=pltpu.CompilerParams(has_side_effects=pltpu.SideEffectType.DATAFLOW_SIDE_EFFECTING, collective_id=2),
    )(pltpu.with_memory_space_constraint(own2, pltpu.HBM), pltpu.with_memory_space_constraint(land2, pltpu.HBM))


def _wq_wait(send_sems, recv_sems, src_thru, land_thru, after):
    def body(src_ref, land_ref, send_sems, recv_sems, after_ref, src_dead, got_ref):
        del after_ref, src_dead, got_ref
        for k in range(1, NDEV):
            px, py, pc = _xor_peer(k)
            cp = pltpu.make_async_remote_copy(src_ref=src_ref, dst_ref=land_ref.at[:, _slot(px, py, pc)],
                                              send_sem=send_sems.at[k - 1], recv_sem=recv_sems.at[k - 1],
                                              device_id=(px, py, pc), device_id_type=MESH)
            cp.wait_send()
            cp.wait_recv()

    hbm = pl.BlockSpec(memory_space=pltpu.HBM)
    sem = pl.BlockSpec(memory_space=pltpu.SEMAPHORE)
    return pl.pallas_call(
        body, name="wq_wait",
        out_shape=(pltpu.HBM(src_thru.shape, src_thru.dtype), pltpu.HBM(land_thru.shape, land_thru.dtype)),
        in_specs=(hbm, hbm, sem, sem, pl.BlockSpec(memory_space=ANY)), out_specs=(hbm, hbm),
        input_output_aliases={0: 0, 1: 1},
        compiler_params=pltpu.CompilerParams(has_side_effects=pltpu.SideEffectType.DATAFLOW_SIDE_EFFECTING),
    )(src_thru, land_thru, send_sems, recv_sems, after)[1]


def _fill_a_ext(i, s_ref, val_ref, glu_ref, hval_ref, hglu_ref):
    ah = hval_ref[...].astype(F32) * _sig(hglu_ref[...].astype(F32))
    s_ref[0, 0:HALO, :] = jnp.where(i > 0, ah, 0.0)

    def fill(rb, carry):
        rows = _rows(rb)
        s_ref[0, pl.ds(HALO + rb * RB, RB), :] = _ld(val_ref, rows) * _sig(_ld(glu_ref, rows))
        return carry
    _blocks(TM // RB, fill)


def _halo_prev(seg):
    hb = TM // HALO
    return pl.BlockSpec((HALO, D), lambda i: (jnp.maximum(i * hb - 1, 0), seg))


def _branch_a_fwd(p, cw, conv_b, lg, lb, wq, token):
    T = p.shape[0]

    def body(val_ref, glu_ref, z_ref, hval_ref, hglu_ref, cw_ref, cb_ref, lg_ref, lb_ref, w_ref, token_ref,
             yc_ref, ya_ref, s_ref, yp_ref):
        del token_ref
        i = pl.program_id(0)
        _fill_a_ext(i, s_ref, val_ref, glu_ref, hval_ref, hglu_ref)
        _shift_copies(s_ref)

        def conv(rb, carry):
            r0 = rb * RBC
            accs = [jnp.broadcast_to(cb_ref[...], (8, D))] * (RBC // 8)
            for k in range(KW):
                o = 2 + k
                w = cw_ref[k]
                accs = [acc + w * s_ref[o % 8, pl.ds(r0 + 8 * (o // 8 + g), 8), :]
                        for g, acc in enumerate(accs)]
            for g, acc in enumerate(accs):
                yc_ref[pl.ds(r0 + 8 * g, 8), :] = acc
            return carry
        for rb in range(TM // RBC):
            conv(rb, 0)

        def post(rb, carry):
            rows = _rows(rb)
            n, _ = _ln_stats(yc_ref[rows, :])
            l = n * lg_ref[...] + lb_ref[...]
            z = _ld(z_ref, rows)
            yp_ref[rows, :] = ((l * _sig(l)) * (z * _sig(z))).astype(BF16)
            return carry
        _blocks(TM // RB, post)
        ya_ref[...] = jnp.dot(yp_ref[...], w_ref[...], preferred_element_type=F32)

    tile = lambda seg: pl.BlockSpec((TM, D), lambda i: (i, seg))
    row = pl.BlockSpec((1, D), lambda i: (0, 0))
    return pl.pallas_call(
        body, name="branch_a_fwd", grid=(T // TM,),
        out_shape=(jax.ShapeDtypeStruct((T, D), F32), jax.ShapeDtypeStruct((T, D), F32)),
        in_specs=[tile(0), tile(1), tile(2), _halo_prev(0), _halo_prev(1),
                  pl.BlockSpec((KW, 8, D), lambda i: (0, 0, 0)), row, row, row,
                  pl.BlockSpec((None, D, D), lambda i: (W_CO, 0, 0)), pl.BlockSpec((8, 128), lambda i: (0, 0))],
        out_specs=(pl.BlockSpec((TM, D), lambda i: (i, 0)), pl.BlockSpec((TM, D), lambda i: (i, 0))),
        scratch_shapes=[VMEM((8, EXT, D), F32), VMEM((TM, D), BF16)],
        compiler_params=_cp(("parallel",)),
    )(p, p, p, p, p, cw, conv_b, lg, lb, wq, token)


def _masked_ws(ws_ref, wt_ref):
    tri = lax.broadcasted_iota(jnp.int32, (CHUNK, CHUNK), 0) >= lax.broadcasted_iota(jnp.int32, (CHUNK, CHUNK), 1)
    for h in range(HEADS):
        wt_ref[h] = jnp.where(tri, ws_ref[h], 0.0).astype(BF16)


def _sgu_mix(wt_ref, vl_ref, bst_ref, s_ref):
    for h in range(HEADS):
        cs = slice(h * HD, (h + 1) * HD)
        s = jnp.dot(wt_ref[h], _chunks_on_lanes(vl_ref, cs), preferred_element_type=F32) + bst_ref[:, h:h + 1]
        for ck in range(TM // CHUNK):
            s_ref[ck * CHUNK:(ck + 1) * CHUNK, cs] = s[:, ck * HD:(ck + 1) * HD]


def _chunks_on_lanes(ref, cs):
    return jnp.concatenate([ref[ck * CHUNK:(ck + 1) * CHUNK, cs] for ck in range(TM // CHUNK)], axis=1)


def _branch_b_fwd(p, ws, bst, sg, sb, wq):
    T = p.shape[0]

    def body(pu_ref, pv_ref, pz_ref, ws_ref, bst_ref, sg_ref, sb_ref, w_ref, yb_ref,
             wt_ref, vl_ref, t_ref, s_ref, yp_ref):
        _masked_ws(ws_ref, wt_ref)

        def pre(rb, carry):
            rows = _rows(rb)
            vg, _ = _gelu_parts(_ld(pv_ref, rows))
            vn, _ = _ln_stats(vg)
            vl_ref[rows, :] = (vn * sg_ref[...] + sb_ref[...]).astype(BF16)
            u, _ = _gelu_parts(_ld(pu_ref, rows))
            z = _ld(pz_ref, rows)
            t_ref[rows, :] = u * (z * _sig(z))
            return carry
        _blocks(TM // RB, pre)
        _sgu_mix(wt_ref, vl_ref, bst_ref, s_ref)

        def post(rb, carry):
            rows = _rows(rb)
            yp_ref[rows, :] = (t_ref[rows, :] * s_ref[rows, :]).astype(BF16)
            return carry
        _blocks(TM // RB, post)
        yb_ref[...] = jnp.dot(yp_ref[...], w_ref[...], preferred_element_type=F32)

    tile = lambda seg: pl.BlockSpec((TM, D), lambda i: (i, seg))
    row = pl.BlockSpec((1, D), lambda i: (0, 0))
    return pl.pallas_call(
        body, name="branch_b_fwd", grid=(T // TM,),
        out_shape=jax.ShapeDtypeStruct((T, D), F32),
        in_specs=[tile(3), tile(4), tile(5), pl.BlockSpec((HEADS, CHUNK, CHUNK), lambda i: (0, 0, 0)),
                  pl.BlockSpec((CHUNK, HEADS), lambda i: (0, 0)), row, row,
                  pl.BlockSpec((None, D, D), lambda i: (W_SO, 0, 0))],
        out_specs=pl.BlockSpec((TM, D), lambda i: (i, 0)),
        scratch_shapes=[VMEM((HEADS, CHUNK, CHUNK), BF16), VMEM((TM, D), BF16), VMEM((TM, D), F32),
                        VMEM((TM, D), F32), VMEM((TM, D), BF16)],
        compiler_params=_cp(("parallel",)),
    )(p, p, p, ws, bst, sg, sb, wq)


def _merge_loss(p, ya, yb, x, tgt, mod, g_final, wq):
    T = x.shape[0]
    nt = T // TM

    def body(ga_ref, gb_ref, ya_ref, yb_ref, x_ref, t_ref, mod_ref, gf_ref, w_ref,
             dx2_ref, dya_ref, dyb_ref, dp_ref, gwo_ref, st_ref,
             mrg_s, out_s, dout_s, dm_s, gw_acc):
        i = pl.program_id(0)
        base = _ring_base(i, 1)
        cur = pl.ds(base, TM)

        @pl.when(i == 0)
        def _():
            gw_acc[...] = jnp.zeros_like(gw_acc)
            st_ref[...] = jnp.zeros_like(st_ref)

        def merge(rb, carry):
            rows = _rows(rb)
            mrg_s[_ring_rows(base, rb), :] = (_sig(_ld(ga_ref, rows)) * ya_ref[rows, :]
                                              + _sig(_ld(gb_ref, rows)) * yb_ref[rows, :]).astype(BF16)
            return carry
        _blocks(TM // RB, merge)
        out_s[...] = jnp.dot(mrg_s[cur, :], w_ref[...], preferred_element_type=F32)

        def head(rb, carry):
            loss, gg, dg = carry
            rows = _rows(rb)
            gate = mod_ref[2:3, :]
            gf = gf_ref[...]
            out = out_s[rows, :]
            x2 = x_ref[rows, :] + gate * out
            r2 = lax.rsqrt(jnp.mean(x2 * x2, axis=-1, keepdims=True) + EPS)
            x2n = x2 * r2
            diff = x2n * gf - t_ref[rows, :]
            dy = diff * (1.0 / D)
            dx2n = dy * gf
            dx2 = r2 * (dx2n - x2n * jnp.mean(dx2n * x2n, axis=-1, keepdims=True))
            dx2_ref[rows, :] = dx2
            dout_s[_ring_rows(base, rb), :] = (dx2 * gate).astype(BF16)
            return loss + _fold8(diff * diff), gg + _fold8(dy * x2n), dg + _fold8(dx2 * out)
        zero = jnp.zeros((8, D), F32)
        loss, gg, dg = _blocks(TM // RB, head, (zero, zero, zero))
        st_ref[0] = st_ref[0] + loss * (0.5 / D)
        st_ref[1] = st_ref[1] + gg
        st_ref[2] = st_ref[2] + dg

        dm_s[...] = lax.dot_general(dout_s[cur, :], w_ref[...], NT, preferred_element_type=F32)

        @_when_ring_full(i, 1)
        def _():
            gw_acc[...] += lax.dot_general(mrg_s[...], dout_s[...], TN, preferred_element_type=F32)

        def split(rb, carry):
            rows = _rows(rb)
            dm = dm_s[rows, :]
            sa = _sig(_ld(ga_ref, rows))
            sb = _sig(_ld(gb_ref, rows))
            dya_ref[rows, :] = (dm * sa).astype(BF16)
            dyb_ref[rows, :] = (dm * sb).astype(BF16)
            dp_ref[0, rows, :] = (dm * ya_ref[rows, :] * (sa * (1.0 - sa))).astype(BF16)
            dp_ref[1, rows, :] = (dm * yb_ref[rows, :] * (sb * (1.0 - sb))).astype(BF16)
            return carry
        _blocks(TM // RB, split)

        @pl.when(i == nt - 1)
        def _():
            gwo_ref[...] = gw_acc[...].astype(BF16)

    tile = pl.BlockSpec((TM, D), lambda i: (i, 0))
    ring = VMEM((_ring_tiles(1) * TM, D), BF16)
    return pl.pallas_call(
        body, name="merge_loss", grid=(nt,),
        out_shape=(jax.ShapeDtypeStruct((T, D), F32), jax.ShapeDtypeStruct((T, D), BF16), jax.ShapeDtypeStruct((T, D), BF16),
                   jax.ShapeDtypeStruct((NSEG, T, D), BF16), jax.ShapeDtypeStruct((D, D), BF16),
                   jax.ShapeDtypeStruct((3, 8, D), F32)),
        in_specs=[pl.BlockSpec((TM, D), lambda i: (i, 6)), pl.BlockSpec((TM, D), lambda i: (i, 7)), tile, tile, tile, tile,
                  pl.BlockSpec((3, D), lambda i: (0, 0)), pl.BlockSpec((1, D), lambda i: (0, 0)),
                  pl.BlockSpec((None, D, D), lambda i: (W_O, 0, 0))],
        out_specs=(tile, tile, tile, pl.BlockSpec((2, TM, D), lambda i: (3, i, 0)),
                   pl.BlockSpec((D, D), lambda i: (0, 0)), pl.BlockSpec((3, 8, D), lambda i: (0, 0, 0))),
        scratch_shapes=[ring, VMEM((TM, D), F32), ring, VMEM((TM, D), F32), VMEM((D, D), F32)],
        compiler_params=_cp(("arbitrary",)),
    )(p, p, ya, yb, x, tgt, mod, g_final, wq)


def _branch_b_bwd(dp, dyb, p, ws, bst, sg, sb, wq):
    T = p.shape[0]
    nt = T // TM

    def body(dp_in, dyb_ref, pu_ref, pv_ref, pz_ref, ws_ref, bst_ref, sg_ref, sb_ref, w_ref,
             dp_ref, gw_ref, gws_ref, gbs_ref, st_ref,
             wt_ref, d_s, vl_s, vn_s, gpv_s, rs_s, s_s, ds_s, ds32_s, yp_s, dy_s, dvl_s, gw_acc):
        del dp_in
        i = pl.program_id(0)
        base = _ring_base(i, nt)

        @pl.when(i == 0)
        def _():
            gw_acc[...] = jnp.zeros_like(gw_acc)
            gws_ref[...] = jnp.zeros_like(gws_ref)
            gbs_ref[...] = jnp.zeros_like(gbs_ref)
            st_ref[...] = jnp.zeros_like(st_ref)
            _masked_ws(ws_ref, wt_ref)

        d_s[...] = lax.dot_general(dyb_ref[...], w_ref[...], NT, preferred_element_type=F32)
        dy_s[pl.ds(base, TM), :] = dyb_ref[...]

        def pre(rb, carry):
            rows = _rows(rb)
            vg, gpv = _gelu_parts(_ld(pv_ref, rows))
            vn, rs = _ln_stats(vg)
            vl_s[rows, :] = (vn * sg_ref[...] + sb_ref[...]).astype(BF16)
            vn_s[rows, :] = vn
            gpv_s[rows, :] = gpv
            rs_s[rows, :] = rs
            return carry
        _blocks(TM // RB, pre)
        _sgu_mix(wt_ref, vl_s, bst_ref, s_s)

        def mid(rb, carry):
            rows = _rows(rb)
            u, gpu = _gelu_parts(_ld(pu_ref, rows))
            z = _ld(pz_ref, rows)
            sz = _sig(z)
            siluz = z * sz
            d = d_s[rows, :]
            s = s_s[rows, :]
            t = u * siluz
            yp_s[_ring_rows(base, rb), :] = (t * s).astype(BF16)
            dp_ref[0, rows, :] = (d * s * siluz * gpu).astype(BF16)
            ds = d * t
            ds32_s[rows, :] = ds
            ds_s[rows, :] = ds.astype(BF16)
            dp_ref[2, rows, :] = (d * u * s * (sz * (1.0 + z * (1.0 - sz)))).astype(BF16)
            return carry
        _blocks(TM // RB, mid)

        for h in range(HEADS):
            cs = slice(h * HD, (h + 1) * HD)
            dsh = _chunks_on_lanes(ds_s, cs)
            dvl = lax.dot_general(wt_ref[h], dsh, TN, preferred_element_type=F32)
            gbs = gbs_ref[h]
            for ck in range(TM // CHUNK):
                r = slice(ck * CHUNK, (ck + 1) * CHUNK)
                dvl_s[r, cs] = dvl[:, ck * HD:(ck + 1) * HD]
                gbs = gbs + ds32_s[r, cs]
            gbs_ref[h] = gbs
            gws_ref[h] += lax.dot_general(dsh, _chunks_on_lanes(vl_s, cs), NT, preferred_element_type=F32)

        def post(rb, carry):
            g_sg, g_sb = carry
            rows = _rows(rb)
            dvl = dvl_s[rows, :]
            vn = vn_s[rows, :]
            dvg = _ln_bwd(dvl * sg_ref[...], vn, rs_s[rows, :])
            dp_ref[1, rows, :] = (dvg * gpv_s[rows, :]).astype(BF16)
            return g_sg + _fold8(dvl * vn), g_sb + _fold8(dvl)
        zero = jnp.zeros((8, D), F32)
        g_sg, g_sb = _blocks(TM // RB, post, (zero, zero))
        st_ref[0] = st_ref[0] + g_sg
        st_ref[1] = st_ref[1] + g_sb

        @_when_ring_full(i, nt)
        def _():
            gw_acc[...] += lax.dot_general(yp_s[...], dy_s[...], TN, preferred_element_type=F32)

        @pl.when(i == nt - 1)
        def _():
            gw_ref[...] = gw_acc[...].astype(BF16)

    tile = lambda seg: pl.BlockSpec((TM, D), lambda i: (i, seg))
    row = pl.BlockSpec((1, D), lambda i: (0, 0))
    hh = pl.BlockSpec((HEADS, CHUNK, CHUNK), lambda i: (0, 0, 0))
    ring = VMEM((_ring_tiles(nt) * TM, D), BF16)
    return pl.pallas_call(
        body, name="branch_b_bwd", grid=(nt,),
        out_shape=(jax.ShapeDtypeStruct((NSEG, T, D), BF16), jax.ShapeDtypeStruct((D, D), BF16),
                   jax.ShapeDtypeStruct((HEADS, CHUNK, CHUNK), F32), jax.ShapeDtypeStruct((HEADS, CHUNK, HD), F32),
                   jax.ShapeDtypeStruct((2, 8, D), F32)),
        in_specs=[pl.BlockSpec(memory_space=ANY), pl.BlockSpec((TM, D), lambda i: (i, 0)), tile(3), tile(4), tile(5),
                  hh, pl.BlockSpec((CHUNK, HEADS), lambda i: (0, 0)), row, row,
                  pl.BlockSpec((None, D, D), lambda i: (W_SO, 0, 0))],
        out_specs=(pl.BlockSpec((3, TM, D), lambda i: (1, i, 0)), pl.BlockSpec((D, D), lambda i: (0, 0)), hh, hh,
                   pl.BlockSpec((2, 8, D), lambda i: (0, 0, 0))),
        scratch_shapes=[VMEM((HEADS, CHUNK, CHUNK), BF16), VMEM((TM, D), F32), VMEM((TM, D), BF16), VMEM((TM, D), F32),
                        VMEM((TM, D), F32), VMEM((TM, 1), F32), VMEM((TM, D), F32), VMEM((TM, D), BF16), VMEM((TM, D), F32),
                        ring, ring, VMEM((TM, D), F32), VMEM((D, D), F32)],
        input_output_aliases={0: 0},
        compiler_params=_cp(("arbitrary",)),
    )(dp, dyb, p, p, p, ws, bst, sg, sb, wq)


def _branch_a_bwd1(dp, dya, yc, p, lg, lb, wq):
    T = p.shape[0]
    nt = T // TM

    def body(dp_in, dya_ref, yc_ref, z_ref, lg_ref, lb_ref, w_ref, dp_ref, dyc_ref, gw_ref, st_ref,
             d_s, yp_s, dy_s, gw_acc):
        del dp_in
        i = pl.program_id(0)
        base = _ring_base(i, nt)

        @pl.when(i == 0)
        def _():
            gw_acc[...] = jnp.zeros_like(gw_acc)
            st_ref[...] = jnp.zeros_like(st_ref)

        d_s[...] = lax.dot_general(dya_ref[...], w_ref[...], NT, preferred_element_type=F32)
        dy_s[pl.ds(base, TM), :] = dya_ref[...]

        def blk(rb, carry):
            g_lg, g_lb, g_cb = carry
            rows = _rows(rb)
            n, rs = _ln_stats(yc_ref[rows, :])
            l = n * lg_ref[...] + lb_ref[...]
            sgl = _sig(l)
            sl = l * sgl
            z = _ld(z_ref, rows)
            sz = _sig(z)
            siluz = z * sz
            d = d_s[rows, :]
            yp_s[_ring_rows(base, rb), :] = (sl * siluz).astype(BF16)
            dp_ref[rows, :] = (d * sl * (sz * (1.0 + z * (1.0 - sz)))).astype(BF16)
            dl = (d * siluz) * (sgl * (1.0 + l * (1.0 - sgl)))
            dyc = _ln_bwd(dl * lg_ref[...], n, rs)
            dyc_ref[rows, :] = dyc
            return g_lg + _fold8(dl * n), g_lb + _fold8(dl), g_cb + _fold8(dyc)
        zero = jnp.zeros((8, D), F32)
        g_lg, g_lb, g_cb = _blocks(TM // RB, blk, (zero, zero, zero))
        st_ref[0] = st_ref[0] + g_lg
        st_ref[1] = st_ref[1] + g_lb
        st_ref[2] = st_ref[2] + g_cb

        @_when_ring_full(i, nt)
        def _():
            gw_acc[...] += lax.dot_general(yp_s[...], dy_s[...], TN, preferred_element_type=F32)

        @pl.when(i == nt - 1)
        def _():
            gw_ref[...] = gw_acc[...].astype(BF16)

    tile = pl.BlockSpec((TM, D), lambda i: (i, 0))
    row = pl.BlockSpec((1, D), lambda i: (0, 0))
    ring = VMEM((_ring_tiles(nt) * TM, D), BF16)
    return pl.pallas_call(
        body, name="branch_a_bwd1", grid=(nt,),
        out_shape=(jax.ShapeDtypeStruct((NSEG, T, D), BF16), jax.ShapeDtypeStruct((T, D), F32),
                   jax.ShapeDtypeStruct((D, D), BF16), jax.ShapeDtypeStruct((3, 8, D), F32)),
        in_specs=[pl.BlockSpec(memory_space=ANY), tile, tile, pl.BlockSpec((TM, D), lambda i: (i, 2)), row, row,
                  pl.BlockSpec((None, D, D), lambda i: (0, 0, 0))],
        out_specs=(pl.BlockSpec((None, TM, D), lambda i: (2, i, 0)), tile, pl.BlockSpec((D, D), lambda i: (0, 0)),
                   pl.BlockSpec((3, 8, D), lambda i: (0, 0, 0))),
        scratch_shapes=[VMEM((TM, D), F32), ring, ring, VMEM((D, D), F32)],
        input_output_aliases={0: 0},
        compiler_params=_cp(("arbitrary",)),
    )(dp, dya, yc, p, lg, lb, wq)


def _branch_a_bwd2(dp, dyc, p, cw, gw_co, gw_so, gw_o, early):
    T = p.shape[0]
    nt = T // TM
    hb = TM // HALO

    def body(dp_in, dyc_ref, hdyc_ref, val_ref, glu_ref, cw_ref, gco_ref, gso_ref, go_ref, e_ref,
             dp_ref, gcw_ref, rq_ref, eg_ref, a_s, sd_ref, da_s, x_send, x_recv, x_loc, ag_send, ag_recv, ag_loc):
        del dp_in
        i = pl.program_id(0)
        start_exchange, wait_exchange = _owner_exchange((gco_ref, gso_ref, go_ref), rq_ref, x_send, x_recv, x_loc)
        ag_start, ag_relay, ag_finish = _allgather_phases([e_ref], lambda a, slot: eg_ref.at[slot],
                                                          ag_send, ag_recv, ag_loc)

        @pl.when(i == 0)
        def _():
            gcw_ref[...] = jnp.zeros_like(gcw_ref)
            ag_start()
            start_exchange()

        @pl.when(i == nt // 2)
        def _():
            ag_relay()

        def fill(rb, carry):
            rows = _rows(rb)
            a_s[rows, :] = _ld(val_ref, rows) * _sig(_ld(glu_ref, rows))
            return carry
        _blocks(TM // RB, fill)
        sd_ref[0, 0:TM, :] = dyc_ref[...]
        sd_ref[0, TM:EXT, :] = jnp.where(i < nt - 1, hdyc_ref[...], 0.0)
        _shift_copies(sd_ref)

        def conv_t(rb, carry):
            r0 = rb * RBC
            accs = [jnp.zeros((8, D), F32)] * (RBC // 8)
            for k in range(KW):
                o = KW - 1 - k
                w = cw_ref[k]
                accs = [acc + w * sd_ref[o % 8, pl.ds(r0 + 8 * (o // 8 + g), 8), :]
                        for g, acc in enumerate(accs)]
            for g, acc in enumerate(accs):
                da_s[pl.ds(r0 + 8 * g, 8), :] = acc
            return carry
        for rb in range(TM // RBC):
            conv_t(rb, 0)

        for k0 in range(0, KW, TAPG):
            taps = list(range(k0, min(k0 + TAPG, KW)))

            def tap_group(rb, accs, taps=taps):
                for u in range(2):
                    r0 = rb * 32 + 16 * u
                    a = a_s[pl.ds(r0, 16), :]
                    out = []
                    for k, acc in zip(taps, accs):
                        o = KW - 1 - k
                        d = sd_ref[o % 8, pl.ds(r0 + 8 * (o // 8), 16), :]
                        out.append(acc + _fold8(a * d))
                    accs = tuple(out)
                return accs
            sums = tuple(jnp.zeros((8, D), F32) for _ in taps)
            for rb in range(TM // 32):
                sums = tap_group(rb, sums)
            for k, s in zip(taps, sums):
                gcw_ref[k] = gcw_ref[k] + s

        def glu_b(rb, carry):
            rows = _rows(rb)
            da = da_s[rows, :]
            sg = _sig(_ld(glu_ref, rows))
            dp_ref[0, rows, :] = (da * sg).astype(BF16)
            dp_ref[1, rows, :] = (da * _ld(val_ref, rows) * (sg * (1.0 - sg))).astype(BF16)
            return carry
        _blocks(TM // RB, glu_b)

        @pl.when(i == nt - 1)
        def _():
            wait_exchange()
            ag_finish()

    tile = lambda seg: pl.BlockSpec((TM, D), lambda i: (i, seg))
    hbm = pl.BlockSpec(memory_space=ANY)
    return pl.pallas_call(
        body, name="branch_a_bwd2", grid=(nt,),
        out_shape=(jax.ShapeDtypeStruct((NSEG, T, D), BF16), jax.ShapeDtypeStruct((32, 8, D), F32),
                   jax.ShapeDtypeStruct((NDEV, 3, 128, D), BF16), jax.ShapeDtypeStruct((NDEV, NEARLY, D), F32)),
        in_specs=[hbm, pl.BlockSpec((TM, D), lambda i: (i, 0)),
                  pl.BlockSpec((HALO, D), lambda i: (jnp.minimum((i + 1) * hb, nt * hb - 1), 0)),
                  tile(0), tile(1), pl.BlockSpec((KW, 8, D), lambda i: (0, 0, 0)),
                  hbm, hbm, hbm, hbm],
        out_specs=(pl.BlockSpec((2, TM, D), lambda i: (0, i, 0)), pl.BlockSpec((32, 8, D), lambda i: (0, 0, 0)), hbm, hbm),
        scratch_shapes=[VMEM((TM, D), F32), VMEM((8, EXT, D), F32), VMEM((TM, D), F32),
                        pltpu.SemaphoreType.DMA((3, 7)), pltpu.SemaphoreType.DMA((3, 7)), pltpu.SemaphoreType.DMA((3,)),
                        pltpu.SemaphoreType.DMA((1, 7)), pltpu.SemaphoreType.DMA((1, 7)), pltpu.SemaphoreType.DMA((1,))],
        input_output_aliases={0: 0},
        compiler_params=_cp(("arbitrary",)),
    )(dp, dyc, dyc, p, p, cw, gw_co, gw_so, gw_o, early)


def _in_bwd_dx(dp, wing, x, dx2, mod, g_pre, token):
    T = x.shape[0]
    nt = T // TM

    def body(dp_ref, w_ref, x_ref, dx2_ref, mod_ref, g_ref, token_ref, gx_ref, st_ref, dh_s):
        del token_ref
        i = pl.program_id(0)

        @pl.when(i == 0)
        def _():
            st_ref[...] = jnp.zeros_like(st_ref)

        dh_s[...] = lax.dot_general(dp_ref[0], w_ref[0], NT, preferred_element_type=F32)
        for j in range(1, NSEG):
            dh_s[...] += lax.dot_general(dp_ref[j], w_ref[j], NT, preferred_element_type=F32)

        def blk(rb, carry):
            d_sh, d_sc, g_g = carry
            rows = _rows(rb)
            xv = x_ref[rows, :]
            r = lax.rsqrt(jnp.mean(xv * xv, axis=-1, keepdims=True) + EPS)
            xn = xv * r
            g = g_ref[...]
            hpre = xn * g
            dh = dh_s[rows, :]
            dhp = dh * (1.0 + mod_ref[1:2, :])
            dxn = dhp * g
            gx_ref[rows, :] = dx2_ref[rows, :] + r * (dxn - xn * jnp.mean(dxn * xn, axis=-1, keepdims=True))
            return d_sh + _fold8(dh), d_sc + _fold8(dh * hpre), g_g + _fold8(dhp * xn)
        zero = jnp.zeros((8, D), F32)
        d_sh, d_sc, g_g = _blocks(TM // RB, blk, (zero, zero, zero))
        st_ref[0] = st_ref[0] + d_sh
        st_ref[1] = st_ref[1] + d_sc
        st_ref[2] = st_ref[2] + g_g

    tile = pl.BlockSpec((TM, D), lambda i: (i, 0))
    return pl.pallas_call(
        body, name="in_bwd_dx", grid=(nt,),
        out_shape=(jax.ShapeDtypeStruct((T, D), F32), jax.ShapeDtypeStruct((3, 8, D), F32)),
        in_specs=[pl.BlockSpec((NSEG, TM, D), lambda i: (0, i, 0)),
                  pl.BlockSpec((NSEG, D, D), lambda i: (0, 0, 0), pipeline_mode=pl.Buffered(1)),
                  tile, tile, pl.BlockSpec((3, D), lambda i: (0, 0)), pl.BlockSpec((1, D), lambda i: (0, 0)),
                  pl.BlockSpec((8, 128), lambda i: (0, 0))],
        out_specs=(tile, pl.BlockSpec((3, 8, D), lambda i: (0, 0, 0))),
        scratch_shapes=[VMEM((TM, D), F32)],
        compiler_params=_cp(("arbitrary",), vmem_mb=56),
    )(dp, wing, x, dx2, mod, g_pre, token)


RS_ORDER = (5, 4, 3, 2, 7, 6, 1, 0)
RS_HALF = D // 2


def _in_bwd_dw(ht, dp, order):
    T = ht.shape[1]
    tmw = min(4 * TMB, T)
    nb = T // tmw
    nu = 2 * NSEG

    def body(order_ref, ht_ref, dp_ref, csum_ref, g_ref, acc, sbuf, rsib, send_sems, recv_sems, out_sems):
        j = pl.program_id(0)
        i = pl.program_id(1)
        x, y, c = _me()
        sibling = (x, y, 1 - c)

        def to_sibling(s):
            return pltpu.make_async_remote_copy(src_ref=sbuf.at[s], dst_ref=rsib.at[s], send_sem=send_sems.at[s],
                                                recv_sem=recv_sems.at[s], device_id=sibling, device_id_type=MESH)

        def to_csum(t):
            cols = pl.ds((t % 2) * RS_HALF, RS_HALF)
            return pltpu.make_async_copy(sbuf.at[8 + t], csum_ref.at[t // 2, :, cols], out_sems.at[t])

        @pl.when(i == 0)
        def _():
            acc[...] = jnp.zeros_like(acc)
        acc[...] += jnp.dot(ht_ref[...], dp_ref[...], preferred_element_type=F32)

        for jj in range(nu):
            chip, half, mine = jj // 4, (jj % 4) // 2, jj % 2
            s = 2 * chip + half

            @pl.when((j == jj) & (i == nb - 1))
            def _(chip=chip, half=half, mine=mine, s=s):
                if not mine:
                    sbuf[s] = acc[...].astype(BF16)
                    to_sibling(s).start()
                elif chip < 3:
                    to_sibling(s).wait_recv()
                    sbuf[8 + s] = (acc[...] + rsib[s].astype(F32)).astype(BF16)
                    to_csum(s).start()
                else:
                    to_sibling(s).wait_recv()
                    g_ref[:, half * RS_HALF:(half + 1) * RS_HALF] = acc[...] + rsib[s].astype(F32)

        @pl.when((j == nu - 1) & (i == nb - 1))
        def _():
            for s in range(8):
                to_sibling(s).wait_send()
            for t in range(6):
                to_csum(t).wait()

    hbm = pl.BlockSpec(memory_space=ANY)
    return pl.pallas_call(
        body, name="in_bwd_dw",
        out_shape=(jax.ShapeDtypeStruct((3, D, D), BF16), jax.ShapeDtypeStruct((D, D), F32)),
        grid_spec=pltpu.PrefetchScalarGridSpec(
            num_scalar_prefetch=1, grid=(nu, nb),
            in_specs=[pl.BlockSpec((D, tmw), lambda j, i, o: (0, i)),
                      pl.BlockSpec((None, tmw, RS_HALF), lambda j, i, o: (o[2 * (j // 4) + j % 2], i, (j % 4) // 2))],
            out_specs=(hbm, pl.BlockSpec((D, D), lambda j, i, o: (0, 0))),
            scratch_shapes=[VMEM((D, RS_HALF), F32), VMEM((14, D, RS_HALF), BF16), VMEM((8, D, RS_HALF), BF16),
                            pltpu.SemaphoreType.DMA((8,)), pltpu.SemaphoreType.DMA((8,)), pltpu.SemaphoreType.DMA((6,))]),
        compiler_params=_cp(("arbitrary", "arbitrary"), vmem_mb=56),
    )(order, ht, dp)


def _gather_late(late):
    def body(sm_ref, smg_ref, ag_send, ag_recv, ag_loc):
        _allgather([sm_ref], lambda a, slot: smg_ref.at[slot], ag_send, ag_recv, ag_loc)

    hbm = pl.BlockSpec(memory_space=ANY)
    return pl.pallas_call(
        body, name="gather_late",
        out_shape=jax.ShapeDtypeStruct((NDEV, NLATE, D), F32),
        in_specs=[hbm], out_specs=hbm,
        scratch_shapes=[pltpu.SemaphoreType.DMA((1, 7)), pltpu.SemaphoreType.DMA((1, 7)), pltpu.SemaphoreType.DMA((1,))],
        compiler_params=_cp(vmem_mb=32),
    )(late)


RS_CHIPS = ((1, 0), (0, 1), (1, 1))


def _rs_peer(q):
    x, y, c = _me()
    fx, fy = RS_CHIPS[q]
    return (1 - x if fx else x, 1 - y if fy else y, c)


def _rs_start(csum):
    def body(csum_ref, land_ref, send_sems, recv_sems, csum_thru, land_thru, token):
        barrier = pltpu.get_barrier_semaphore()
        for q in range(3):
            pl.semaphore_signal(barrier, inc=1, device_id=_rs_peer(q), device_id_type=MESH)
        pl.semaphore_wait(barrier, 3)
        for q in range(3):
            pltpu.make_async_remote_copy(src_ref=csum_ref.at[q], dst_ref=land_ref.at[q], send_sem=send_sems.at[q],
                                         recv_sem=recv_sems.at[q], device_id=_rs_peer(q), device_id_type=MESH).start()
        token[...] = jnp.zeros_like(token)

    hbm = pl.BlockSpec(memory_space=pltpu.HBM)
    sem = pl.BlockSpec(memory_space=pltpu.SEMAPHORE)
    land = pltpu.with_memory_space_constraint(lax.empty(csum.shape, csum.dtype), pltpu.HBM)
    return pl.pallas_call(
        body, name="rs_start",
        out_shape=(pltpu.SemaphoreType.DMA((3,)), pltpu.SemaphoreType.DMA((3,)), pltpu.HBM(csum.shape, csum.dtype),
                   pltpu.HBM(csum.shape, csum.dtype), jax.ShapeDtypeStruct((8, 128), F32)),
        in_specs=(hbm, hbm), out_specs=(sem, sem, hbm, hbm, pl.BlockSpec(memory_space=VMEM)),
        input_output_aliases={0: 2, 1: 3},
        compiler_params=pltpu.CompilerParams(has_side_effects=pltpu.SideEffectType.DATAFLOW_SIDE_EFFECTING, collective_id=1),
    )(pltpu.with_memory_space_constraint(csum, pltpu.HBM), land)


def _rs_wait(send_sems, recv_sems, csum_thru, land_thru, after):
    def body(csum_ref, land_ref, send_sems, recv_sems, after_ref, csum_dead, got_ref):
        del after_ref, csum_dead, got_ref
        for q in range(3):
            cp = pltpu.make_async_remote_copy(src_ref=csum_ref.at[q], dst_ref=land_ref.at[q], send_sem=send_sems.at[q],
                                              recv_sem=recv_sems.at[q], device_id=_rs_peer(q), device_id_type=MESH)
            cp.wait_send()
            cp.wait_recv()

    hbm = pl.BlockSpec(memory_space=pltpu.HBM)
    sem = pl.BlockSpec(memory_space=pltpu.SEMAPHORE)
    return pl.pallas_call(
        body, name="rs_wait",
        out_shape=(pltpu.HBM(csum_thru.shape, csum_thru.dtype), pltpu.HBM(csum_thru.shape, csum_thru.dtype)),
        in_specs=(hbm, hbm, sem, sem, pl.BlockSpec(memory_space=ANY)), out_specs=(hbm, hbm),
        input_output_aliases={0: 0, 1: 1},
        compiler_params=pltpu.CompilerParams(has_side_effects=pltpu.SideEffectType.DATAFLOW_SIDE_EFFECTING),
    )(csum_thru, land_thru, send_sems, recv_sems, after)[1]


def _pack_early(st_a, st_b, st_m, gws, gbs):
    def body(sa_ref, sb_ref, sm_ref, gws_ref, gbs_ref, out_ref):
        out_ref[...] = jnp.zeros_like(out_ref)
        fold = lambda v: jnp.sum(v, axis=0, keepdims=True)
        out_ref[0:1, :] = fold(sm_ref[2])
        out_ref[1:2, :] = fold(sa_ref[2])
        out_ref[2:3, :] = fold(sa_ref[0])
        out_ref[3:4, :] = fold(sa_ref[1])
        out_ref[4:5, :] = fold(sb_ref[0])
        out_ref[5:6, :] = fold(sb_ref[1])
        out_ref[7:8, :] = fold(sm_ref[1])
        out_ref[8:9, :] = fold(sm_ref[0])
        tri = lax.broadcasted_iota(jnp.int32, (CHUNK, CHUNK), 0) >= lax.broadcasted_iota(jnp.int32, (CHUNK, CHUNK), 1)
        for h in range(HEADS):
            out_ref[16:16 + CHUNK, h * CHUNK:(h + 1) * CHUNK] = jnp.where(tri, gws_ref[h], 0.0)
            out_ref[6:7, h * CHUNK:(h + 1) * CHUNK] = fold(gbs_ref[h].T)

    vm = pl.BlockSpec(memory_space=VMEM)
    return pl.pallas_call(
        body, name="pack_early", out_shape=jax.ShapeDtypeStruct((NEARLY, D), F32),
        in_specs=[vm] * 5, out_specs=vm, compiler_params=_cp(vmem_mb=32),
    )(st_a, st_b, st_m, gws, gbs)


def _pack_late(st_x, gcw):
    def body(sx_ref, gcw_ref, out_ref):
        out_ref[...] = jnp.zeros_like(out_ref)
        fold = lambda v: jnp.sum(v, axis=0, keepdims=True)
        for r in range(3):
            out_ref[r:r + 1, :] = fold(sx_ref[r])
        for k in range(KW):
            out_ref[8 + k:9 + k, :] = fold(gcw_ref[k])

    vm = pl.BlockSpec(memory_space=VMEM)
    return pl.pallas_call(
        body, name="pack_late", out_shape=jax.ShapeDtypeStruct((NLATE, D), F32),
        in_specs=[vm] * 2, out_specs=vm, compiler_params=_cp(vmem_mb=32),
    )(st_x, gcw)


def _pack_params(name, b_ada, g_pre, conv_b, lg, lb, sg, sb, b_sgu, g_final, w_sgu_t):
    def body(ba_ref, *refs):
        rows, ws_ref, out_ref = refs[:8], refs[8], refs[9]
        out_ref[...] = jnp.zeros_like(out_ref)
        for r in range(3):
            out_ref[r:r + 1, :] = ba_ref[:, r * D:(r + 1) * D]
        for r, ref in enumerate(rows):
            out_ref[3 + r:4 + r, :] = ref[...]
        out_ref[ROW_WS:ROW_WS + CHUNK, :] = ws_ref[...]

    vm = pl.BlockSpec(memory_space=VMEM)
    return pl.pallas_call(
        body, name=name, out_shape=jax.ShapeDtypeStruct((NSMALL, D), F32),
        in_specs=[vm] * 10, out_specs=vm, compiler_params=_cp(vmem_mb=32),
    )(b_ada, g_pre, conv_b, lg, lb, sg, sb, b_sgu, g_final, w_sgu_t)


def _adam(w, g, m, v):
    m2 = ADAM_B1 * m + (1.0 - ADAM_B1) * g
    v2 = ADAM_B2 * v + (1.0 - ADAM_B2) * (g * g)
    m_hat = m2 / (1.0 - ADAM_B1 ** ADAM_STEP)
    v_hat = v2 / (1.0 - ADAM_B2 ** ADAM_STEP)
    delta = -ADAM_LR * (m_hat / (jnp.sqrt(v_hat) + ADAM_EPS) + ADAM_WD * w)
    return delta, m2, v2


def _small_finish(eg, lg, w, m, v):
    def body(eg_ref, lg_ref, w_ref, m_ref, v_ref, loss_ref, g_ref, d_ref, m2_ref, v2_ref):
        e = eg_ref[0]
        l = lg_ref[0]
        for s in range(1, NDEV):
            e = e + eg_ref[s]
            l = l + lg_ref[s]
        g_ref[...] = jnp.zeros_like(g_ref)
        g_ref[0:2, :] = l[0:2, :]
        g_ref[2:3, :] = e[0:1, :]
        g_ref[3:4, :] = l[2:3, :]
        g_ref[4:12, :] = e[1:9, :]
        g_ref[ROW_CW:ROW_CW + KW, :] = l[8:8 + KW, :]
        g_ref[ROW_WS:ROW_WS + CHUNK, :] = e[16:16 + CHUNK, :]
        loss_ref[...] = jnp.sum(e[8:9, :], axis=1, keepdims=True)
        d_ref[...], m2_ref[...], v2_ref[...] = _adam(w_ref[...], g_ref[...], m_ref[...], v_ref[...])

    vm = pl.BlockSpec(memory_space=VMEM)
    sd = jax.ShapeDtypeStruct((NSMALL, D), F32)
    return pl.pallas_call(
        body, name="small_finish", out_shape=(jax.ShapeDtypeStruct((1, 1), F32), sd, sd, sd, sd),
        in_specs=[vm] * 5, out_specs=(vm,) * 5, compiler_params=_cp(vmem_mb=32),
    )(eg, lg, w, m, v)


def _ada_grad_adam(ct, dm, w, m, v):
    def body(ct_ref, dm_ref, w_ref, m_ref, v_ref, g_ref, d_ref, m2_ref, v2_ref):
        g = ct_ref[:, 0:1] * dm_ref[0:1, :]
        for b in range(1, NDEV):
            g = g + ct_ref[:, b:b + 1] * dm_ref[b:b + 1, :]
        g_ref[...] = g
        d_ref[...], m2_ref[...], v2_ref[...] = _adam(w_ref[...], g, m_ref[...], v_ref[...])

    vm = pl.BlockSpec(memory_space=VMEM)
    sd = jax.ShapeDtypeStruct(w.shape, F32)
    return pl.pallas_call(
        body, name="ada_grad_adam", out_shape=(sd, sd, sd, sd),
        in_specs=[vm] * 5, out_specs=(vm,) * 4, compiler_params=_cp(vmem_mb=32),
    )(ct, dm, w, m, v)


def _adam_f32(name, g, w, m, v, rows=None):
    R, C = w.shape
    rows = R if rows is None else rows

    def body(g_ref, w_ref, m_ref, v_ref, d_ref, m2_ref, v2_ref):
        d_ref[...], m2_ref[...], v2_ref[...] = _adam(w_ref[...], g_ref[...], m_ref[...], v_ref[...])

    tile = pl.BlockSpec((rows, C), lambda i: (i, 0))
    sd = jax.ShapeDtypeStruct(w.shape, F32)
    return pl.pallas_call(
        body, name=name, grid=(R // rows,), out_shape=(sd, sd, sd), in_specs=[tile] * 4, out_specs=(tile,) * 3,
        compiler_params=_cp(("parallel",), vmem_mb=32),
    )(g, w, m, v)


def _adam_w_in(gown, got, w, m, v, rows=256):
    R, C = w.shape

    def body(go_ref, got_ref, w_ref, m_ref, v_ref, g_ref, d_ref, m2_ref, v2_ref):
        g = go_ref[...]
        for q in range(3):
            g = g + got_ref[q].astype(F32)
        g_ref[...] = g
        d_ref[...], m2_ref[...], v2_ref[...] = _adam(w_ref[...], g, m_ref[...], v_ref[...])

    tile = pl.BlockSpec((rows, C), lambda i: (i, 0))
    sd = jax.ShapeDtypeStruct((R, C), F32)
    return pl.pallas_call(
        body, name="adam_w_in", grid=(R // rows,), out_shape=(sd, sd, sd, sd),
        in_specs=[tile, pl.BlockSpec((3, rows, C), lambda i: (0, i, 0)), tile, tile, tile], out_specs=(tile,) * 4,
        compiler_params=_cp(("parallel",), vmem_mb=32),
    )(gown, got, w, m, v)


def _adam_reduce(name, recv, w, m, v, rows, recv_spec):
    R, C = w.shape

    def body(r_ref, w_ref, m_ref, v_ref, g_ref, d_ref, m2_ref, v2_ref):
        g = r_ref[0].astype(F32)
        for s in range(1, NDEV):
            g = g + r_ref[s].astype(F32)
        g_ref[...] = g
        d_ref[...], m2_ref[...], v2_ref[...] = _adam(w_ref[...], g, m_ref[...], v_ref[...])

    tile = pl.BlockSpec((rows, C), lambda i: (i, 0))
    sd = jax.ShapeDtypeStruct((R, C), F32)
    return pl.pallas_call(
        body, name=name, grid=(R // rows,), out_shape=(sd, sd, sd, sd),
        in_specs=[recv_spec, tile, tile, tile], out_specs=(tile,) * 4,
        compiler_params=_cp(("parallel",)),
    )(recv, w, m, v)


def kernel(x, c, w_ada, b_ada, g_pre, w_in, conv_w, conv_b, conv_ln_g, conv_ln_b, w_conv_out, sgu_ln_g, sgu_ln_b, w_sgu, b_sgu, w_sgu_out, w_o, g_final, loss_target, m_w_ada, m_b_ada, m_g_pre, m_w_in, m_conv_w, m_conv_b, m_conv_ln_g, m_conv_ln_b, m_w_conv_out, m_sgu_ln_g, m_sgu_ln_b, m_w_sgu, m_b_sgu, m_w_sgu_out, m_w_o, m_g_final, v_w_ada, v_b_ada, v_g_pre, v_w_in, v_conv_w, v_conv_b, v_conv_ln_g, v_conv_ln_b, v_w_conv_out, v_sgu_ln_g, v_sgu_ln_b, v_w_sgu, v_b_sgu, v_w_sgu_out, v_w_o, v_g_final):
    T = x.shape[1]
    assert T % TMB == 0 and x.shape[2] == D
    xs = x[0]
    tgt = loss_target[0]
    my = 4 * lax.axis_index("x") + 2 * lax.axis_index("y") + lax.axis_index("c")

    mod, cg = _ada_fwd(c, w_ada[0], b_ada)
    gf = g_final.reshape(1, D)
    bst = b_sgu[0].T

    h, ht = _prep_h(xs, mod, g_pre)
    ag_order = jnp.stack([_slot(*b) for b in _ag_blocks(lax.axis_index("x"), lax.axis_index("y"), lax.axis_index("c"))])
    p, wing, wcog, cwg, own2, land2 = _in_proj(h, w_in[0], ag_order.astype(jnp.int32),
                                               w_conv_out[0], w_sgu_out[0], w_o[0], conv_w[0])
    wco = wcog.reshape(1, D, D)
    cw = jnp.broadcast_to(jnp.transpose(cwg, (1, 0, 2)).reshape(KW, 1, D), (KW, 8, D))
    wq_send, wq_recv, own2_thru, land2_thru, wq_token = _wq_start(own2, land2)
    yc, ya = _branch_a_fwd(p, cw, conv_b, conv_ln_g, conv_ln_b, wco, wq_token)
    wq2 = _wq_wait(wq_send, wq_recv, own2_thru, land2_thru, yc).reshape(2, D, D)
    yb = _branch_b_fwd(p, w_sgu[0], bst, sgu_ln_g, sgu_ln_b, wq2)
    dx2, dya, dyb, dp, gw_o, st_m = _merge_loss(p, ya, yb, xs, tgt, mod, gf, wq2)

    dp, gw_so, gws, gbs, st_b = _branch_b_bwd(dp, dyb, p, w_sgu[0], bst, sgu_ln_g, sgu_ln_b, wq2)
    dp, dyc, gw_co, st_a = _branch_a_bwd1(dp, dya, yc, p, conv_ln_g, conv_ln_b, wco)
    early = _pack_early(st_a, st_b, st_m, gws, gbs)
    dp, gcw, rq, eg = _branch_a_bwd2(dp, dyc, p, cw, gw_co, gw_so, gw_o, early)
    csum, gown = _in_bwd_dw(ht, dp, my ^ jnp.array(RS_ORDER, jnp.int32))
    rs_send, rs_recv, csum_thru, land_thru, token = _rs_start(csum)
    grad_x, st_x = _in_bwd_dx(dp, wing, xs, dx2, mod, g_pre, token)
    got = _rs_wait(rs_send, rs_recv, csum_thru, land_thru, st_x)
    lg = _gather_late(_pack_late(st_x, gcw))

    def pack(name, b_ada_, g_pre_, conv_b_, lg_, lb_, sg_, sb_, b_sgu_, gfin_, w_sgu_):
        return _pack_params(name, b_ada_, g_pre_, conv_b_, lg_, lb_, sg_, sb_, b_sgu_.reshape(1, D), gfin_.reshape(1, D),
                            jnp.transpose(w_sgu_[0], (1, 0, 2)).reshape(CHUNK, D))

    pw = pack("pack_w", b_ada, g_pre, conv_b, conv_ln_g, conv_ln_b, sgu_ln_g, sgu_ln_b, b_sgu, g_final, w_sgu)
    pm = pack("pack_m", m_b_ada, m_g_pre, m_conv_b, m_conv_ln_g, m_conv_ln_b, m_sgu_ln_g, m_sgu_ln_b, m_b_sgu, m_g_final,
              m_w_sgu)
    pv = pack("pack_v", v_b_ada, v_g_pre, v_conv_b, v_conv_ln_g, v_conv_ln_b, v_sgu_ln_g, v_sgu_ln_b, v_b_sgu, v_g_final,
              v_w_sgu)
    loss11, sg_, sd_, sm_, sv_ = _small_finish(eg, lg, pw, pm, pv)

    def unpack(a):
        return dict(
            b_ada=a[0:3].reshape(1, 3 * D), g_pre=a[3:4], conv_b=a[4:5], conv_ln_g=a[5:6], conv_ln_b=a[6:7],
            sgu_ln_g=a[7:8], sgu_ln_b=a[8:9], b_sgu=a[9:10].reshape(1, HEADS, CHUNK), g_final=a[10],
            w_sgu=jnp.transpose(a[ROW_WS:ROW_WS + CHUNK].reshape(CHUNK, HEADS, CHUNK), (1, 0, 2))[None])
    small = [unpack(a) for a in (sg_, sd_, sm_, sv_)]

    g_cw = lax.dynamic_slice_in_dim(sg_[ROW_CW:ROW_CW + KW], my * 128, 128, axis=1)
    d_cw, m_cw, v_cw = _adam_f32("adam_conv_w", g_cw, conv_w[0], m_conv_w[0], v_conv_w[0])

    wcols = w_ada.shape[2]
    dm_all = jnp.concatenate([lg[:, 0], lg[:, 1], eg[:, 0]], axis=1)
    dm_mine = lax.dynamic_slice_in_dim(dm_all, my * wcols, wcols, axis=1)
    g_ada, d_ada, m_ada, v_ada = _ada_grad_adam(cg.T, dm_mine, w_ada[0], m_w_ada[0], v_w_ada[0])

    g_in, d_in, m_in, v_in = _adam_w_in(gown, got, w_in[0], m_w_in[0], v_w_in[0])
    big = {}
    for j, (nm, w_, m_, v_) in enumerate((("w_conv_out", w_conv_out, m_w_conv_out, v_w_conv_out),
                                          ("w_sgu_out", w_sgu_out, m_w_sgu_out, v_w_sgu_out),
                                          ("w_o", w_o, m_w_o, v_w_o))):
        big[nm] = _adam_reduce("adam_" + nm, rq, w_[0], m_[0], v_[0], 128,
                               pl.BlockSpec((NDEV, None, 128, D), lambda i, j=j: (0, j, 0, 0)))

    per = {
        "w_ada": tuple(a[None] for a in (g_ada, d_ada, m_ada, v_ada)),
        "w_in": tuple(a[None] for a in (g_in, d_in, m_in, v_in)),
        "conv_w": tuple(a[None] for a in (g_cw, d_cw, m_cw, v_cw)),
    }
    for nm in ("w_conv_out", "w_sgu_out", "w_o"):
        per[nm] = tuple(a[None] for a in big[nm])
    for nm in ("b_ada", "g_pre", "conv_b", "conv_ln_g", "conv_ln_b", "sgu_ln_g", "sgu_ln_b", "w_sgu", "b_sgu", "g_final"):
        per[nm] = tuple(s[nm] for s in small)

    order = ["w_ada", "b_ada", "g_pre", "w_in", "conv_w", "conv_b", "conv_ln_g", "conv_ln_b", "w_conv_out",
             "sgu_ln_g", "sgu_ln_b", "w_sgu", "b_sgu", "w_sgu_out", "w_o", "g_final"]
    outs = [loss11.reshape(()), grad_x[None]]
    for part in range(4):
        outs += [per[nm][part] for nm in order]
    return tuple(outs)
```

```python
import jax
import jax.numpy as jnp
from jax import lax
from jax.experimental import pallas as pl
from jax.experimental.pallas import tpu as pltpu

F32 = jnp.float32
BF16 = jnp.bfloat16
MESH = pl.DeviceIdType.MESH
VMEM = pltpu.VMEM
ANY = pl.ANY

D = 1024
NDEV = 8
NSEG = 8
HEADS = 8
HD = 128
CHUNK = 128
KW = 31
HALO = 32
EPS = 1e-6
TM = 256
TMB = 512
RB = 32
RBC = 32
TAPG = 4
EXT = TM + HALO
NSMALL = 176
NEARLY = 144
NLATE = 40
ROW_CW = 16
ROW_WS = 48
W_CO = 0
W_SO, W_O = 0, 1

ADAM_LR = 0.001
ADAM_B1 = 0.9
ADAM_B2 = 0.999
ADAM_EPS = 1e-08
ADAM_WD = 0.01
ADAM_STEP = 10

INV_SQRT2 = 0.7071067811865476
INV_SQRT_2PI = 0.3989422804014327

NT = (((1,), (1,)), ((), ()))
TN = (((0,), (0,)), ((), ()))


def _cp(sem=None, vmem_mb=48):
    return pltpu.CompilerParams(dimension_semantics=sem, vmem_limit_bytes=vmem_mb * 1024 * 1024)


def _me():
    return lax.axis_index("x"), lax.axis_index("y"), lax.axis_index("c")


def _slot(px, py, pc):
    return 4 * px + 2 * py + pc


def _xor_peer(k):
    x, y, c = _me()
    return (1 - x if k & 4 else x, 1 - y if k & 2 else y, 1 - c if k & 1 else c)


def _allgather_phases(srcs, dst_at, send_sems, recv_sems, loc_sems):
    x, y, c = _me()
    me = (x, y, c)
    sibling = (x, y, 1 - c)
    chips = [(1 - x, y), (x, 1 - y), (1 - x, 1 - y)]
    na = len(srcs)

    def copy(a, k, block, to, src=None):
        d = dst_at(a, _slot(*block))
        return pltpu.make_async_remote_copy(
            src_ref=d if src is None else src, dst_ref=d,
            send_sem=send_sems.at[a, k], recv_sem=recv_sems.at[a, k],
            device_id=to, device_id_type=MESH)

    local = [pltpu.make_async_copy(srcs[a], dst_at(a, _slot(*me)), loc_sems.at[a]) for a in range(na)]
    first = []
    for a in range(na):
        first.append(copy(a, 0, me, sibling, src=srcs[a]))
        for j, chip in enumerate(chips):
            first.append(copy(a, 1 + j, me, (*chip, c), src=srcs[a]))
    passed = [copy(a, 4 + j, (*chip, c), sibling) for j, chip in enumerate(chips) for a in range(na)]

    def start():
        for cp in local + first:
            cp.start()

    def relay():
        for j, chip in enumerate(chips):
            for a in range(na):
                copy(a, 1 + j, (*chip, c), me).wait_recv()
                passed[j * na + a].start()

    def finish():
        for a in range(na):
            copy(a, 0, sibling, me).wait_recv()
        for j, chip in enumerate(chips):
            for a in range(na):
                copy(a, 4 + j, (*chip, 1 - c), me).wait_recv()
        for cp in first + passed:
            cp.wait_send()
        for cp in local:
            cp.wait()

    return start, relay, finish


def _allgather(srcs, dst_at, send_sems, recv_sems, loc_sems):
    start, relay, finish = _allgather_phases(srcs, dst_at, send_sems, recv_sems, loc_sems)
    start()
    relay()
    finish()


def _owner_exchange(gq, rq_ref, x_send, x_recv, x_loc):
    mx, my_, mc = _me()
    me = _slot(mx, my_, mc)

    def rows_of(a, slot):
        return gq[a].at[pl.ds(pl.multiple_of(slot * 128, 128), 128), :]

    def exchange(k, a, recv):
        px, py, pc = _xor_peer(k)
        peer = _slot(px, py, pc)
        return pltpu.make_async_remote_copy(
            src_ref=rows_of(a, me if recv else peer), dst_ref=rq_ref.at[peer if recv else me, a],
            send_sem=x_send.at[a, k - 1], recv_sem=x_recv.at[a, k - 1],
            device_id=(px, py, pc), device_id_type=MESH)

    local = [pltpu.make_async_copy(rows_of(a, me), rq_ref.at[me, a], x_loc.at[a]) for a in range(len(gq))]

    def start():
        for cp in local:
            cp.start()
        for k in range(1, NDEV):
            for a in range(len(gq)):
                exchange(k, a, False).start()

    def wait():
        for k in range(1, NDEV):
            for a in range(len(gq)):
                exchange(k, a, True).wait_recv()
        for k in range(1, NDEV):
            for a in range(len(gq)):
                exchange(k, a, False).wait_send()
        for cp in local:
            cp.wait()

    return start, wait


def _sig(v):
    return jax.nn.sigmoid(v)


def _gelu_parts(v):
    cdf = 0.5 * (1.0 + lax.erf(v * INV_SQRT2))
    pdf = jnp.exp(-0.5 * v * v) * INV_SQRT_2PI
    return v * cdf, cdf + v * pdf


def _ln_stats(v):
    mu = jnp.mean(v, axis=-1, keepdims=True)
    xc = v - mu
    var = jnp.mean(xc * xc, axis=-1, keepdims=True)
    rs = lax.rsqrt(var + EPS)
    return xc * rs, rs


def _ln_bwd(dn, n, rs):
    return rs * (dn - jnp.mean(dn, axis=-1, keepdims=True) - n * jnp.mean(dn * n, axis=-1, keepdims=True))


def _fold8(v):
    acc = v[0:8]
    for r in range(8, v.shape[0], 8):
        acc = acc + v[r:r + 8]
    return acc


def _rows(rb, n=RB):
    if isinstance(rb, int):
        return pl.ds(rb * n, n)
    return pl.ds(pl.multiple_of(rb * n, n), n)


def _ring_tiles(nt):
    del nt
    return 1


def _ring_base(i, nt):
    if _ring_tiles(nt) == 1:
        return 0
    return pl.multiple_of((i % _ring_tiles(nt)) * TM, TM)


def _ring_rows(base, rb, n=RB):
    if isinstance(base, int) and isinstance(rb, int):
        return pl.ds(base + rb * n, n)
    return pl.ds(pl.multiple_of(base + rb * n, n), n)


def _when_ring_full(i, nt):
    g = _ring_tiles(nt)
    if g == 1:
        return lambda fn: fn()
    return pl.when(i % g == g - 1)


def _ld(ref, rows):
    return ref[rows, :].astype(F32)


def _blocks(n, body, init=0):
    carry = init
    for rb in range(n):
        carry = body(rb, carry)
    return carry


def _shift_copies(s_ref):
    n = EXT - 8
    for m in range(1, 8):
        for c0 in range(0, n, 56):
            s_ref[m, c0:c0 + 56, :] = s_ref[0, c0 + m:c0 + m + 56, :]


def _ada_exchange(c_ref, w_ref, b_ref, mod_ref, call_ref, cdst, msrc, mdst, c_send, c_recv, m_send, m_recv):
    x, y, c = _me()
    me = _slot(x, y, c)

    def all_to_all(src_of, dst, ss, rs):
        def cp(k, recv):
            px, py, pc = _xor_peer(k)
            peer = _slot(px, py, pc)
            return pltpu.make_async_remote_copy(
                src_ref=src_of(me if recv else peer), dst_ref=dst.at[peer if recv else me],
                send_sem=ss.at[k - 1], recv_sem=rs.at[k - 1], device_id=(px, py, pc), device_id_type=MESH)
        for k in range(1, NDEV):
            cp(k, False).start()
        for k in range(1, NDEV):
            cp(k, True).wait_recv()
        for k in range(1, NDEV):
            cp(k, False).wait_send()

    cdst[me] = c_ref[...]
    all_to_all(lambda s: c_ref, cdst, c_send, c_recv)
    for b in range(NDEV):
        call_ref[b:b + 1, :] = cdst[b]
    m = jnp.dot(call_ref[...], w_ref[...], preferred_element_type=F32, precision=lax.Precision.HIGHEST)
    for b in range(NDEV):
        msrc[b] = m[b:b + 1, :]
    mdst[me] = msrc[me]
    all_to_all(lambda s: msrc.at[s], mdst, m_send, m_recv)
    full = jnp.concatenate([mdst[k] for k in range(NDEV)], axis=1) + b_ref[...]
    for r in range(3):
        mod_ref[r:r + 1, :] = full[:, r * D:(r + 1) * D]


def _ada_fwd(c, w_ada, b_ada):
    wcols = w_ada.shape[1]

    def body(*refs):
        _ada_exchange(*refs)

    vm = pl.BlockSpec(memory_space=VMEM)
    return pl.pallas_call(
        body, name="ada_fwd",
        out_shape=(jax.ShapeDtypeStruct((3, D), F32), jax.ShapeDtypeStruct((NDEV, D), F32)),
        in_specs=[vm, vm, vm], out_specs=(vm, vm),
        scratch_shapes=[VMEM((NDEV, 1, D), F32), VMEM((NDEV, 1, wcols), F32), VMEM((NDEV, 1, wcols), F32),
                        pltpu.SemaphoreType.DMA((7,)), pltpu.SemaphoreType.DMA((7,)),
                        pltpu.SemaphoreType.DMA((7,)), pltpu.SemaphoreType.DMA((7,))],
        compiler_params=_cp(vmem_mb=32),
    )(c, w_ada, b_ada)


def _prep_h(x, mod, g_pre):
    T = x.shape[0]

    def body(x_ref, mod_ref, g_ref, h_ref, ht_ref):
        def blk(rb, carry):
            rows = _rows(rb)
            xv = x_ref[rows, :]
            r = lax.rsqrt(jnp.mean(xv * xv, axis=-1, keepdims=True) + EPS)
            h_ref[rows, :] = ((xv * r) * g_ref[...] * (1.0 + mod_ref[1:2, :]) + mod_ref[0:1, :]).astype(BF16)
            return carry
        _blocks(TMB // RB, blk)
        ht_ref[...] = h_ref[...].T

    return pl.pallas_call(
        body, name="prep_h", grid=(T // TMB,),
        out_shape=(jax.ShapeDtypeStruct((T, D), BF16), jax.ShapeDtypeStruct((D, T), BF16)),
        in_specs=[pl.BlockSpec((TMB, D), lambda i: (i, 0)), pl.BlockSpec((3, D), lambda i: (0, 0)),
                  pl.BlockSpec((1, D), lambda i: (0, 0))],
        out_specs=(pl.BlockSpec((TMB, D), lambda i: (i, 0)), pl.BlockSpec((D, TMB), lambda i: (0, i))),
        compiler_params=_cp(("parallel",)),
    )(x, mod, g_pre)


def _ag_blocks(x, y, c):
    a = 1 - c
    n1 = (x ^ a, y ^ c)
    n2 = (x ^ c, y ^ a)
    dg = (1 - x, 1 - y)
    return [(x, y, c), (x, y, a), (*n1, c), (*n2, c), (*n1, a), (*n2, a), (*dg, c), (*dg, a)]


def _in_proj(h, w_in, order, w_co, w_so, w_o, conv_w):
    T = h.shape[0]
    tmp = min(2 * TMB, T)
    nb = T // tmp

    def body(order_ref, h_ref, w_ref, wco_ref, wso_ref, wo_ref, cw_ref, p_ref, wing_ref, wqg_ref, cwg_ref, own2_ref, land2_ref,
             wbuf, wq_st, wq2_st, send_sems, recv_sems, out_sems, ag_send, ag_recv, ag_loc, own_sems):
        j = pl.program_id(0)
        i = pl.program_id(1)
        blocks = _ag_blocks(*_me())
        me, sibling, n1, n2, n1o, n2o, dg, dgo = blocks
        own2 = [pltpu.make_async_copy(wq2_st, own2_ref, own_sems.at[0]),
                pltpu.make_async_copy(wq2_st, land2_ref.at[:, _slot(*me)], own_sems.at[1])]
        ag_start, ag_relay, ag_finish = _allgather_phases(
            [wq_st, cw_ref], lambda a, slot: wqg_ref.at[:, slot] if a == 0 else cwg_ref.at[slot], ag_send, ag_recv, ag_loc)

        def copy(k, block, to):
            d = wbuf.at[_slot(*block)]
            return pltpu.make_async_remote_copy(src_ref=d, dst_ref=d, send_sem=send_sems.at[k], recv_sem=recv_sems.at[k],
                                                device_id=to, device_id_type=MESH)

        def writeback(jj):
            s = _slot(*blocks[jj])
            return pltpu.make_async_copy(wbuf.at[s], wing_ref.at[s], out_sems.at[jj])

        sends = [copy(0, me, sibling), copy(1, me, n1), copy(2, me, n2)]
        relay = copy(3, n1, n2)
        passed = [copy(4, n1, sibling), copy(5, n2, sibling), copy(6, dg, sibling)]
        arrivals = {1: [(0, sibling)], 2: [(1, n1), (2, n2)], 4: [(5, n1o)], 5: [(4, n2o)], 6: [(3, dg)], 7: [(6, dgo)]}

        @pl.when((j == 0) & (i == 0))
        def _():
            wbuf[_slot(*me)] = w_ref[...].astype(BF16)
            for cp in sends:
                cp.start()
            writeback(0).start()
            wq2_st[W_SO] = wso_ref[...].astype(BF16)
            wq2_st[W_O] = wo_ref[...].astype(BF16)
            for cp in own2:
                cp.start()

        for jj in range(1, NSEG):
            @pl.when((j == jj) & (i == 0))
            def _(jj=jj):
                for k, block in arrivals.get(jj, []):
                    copy(k, block, me).wait_recv()
                if jj == 2:
                    relay.start()
                    passed[0].start()
                    passed[1].start()
                    writeback(3).start()
                    wq_st[0] = wco_ref[...].astype(BF16)
                    ag_start()
                if jj == 6:
                    passed[2].start()
                if jj != 3:
                    writeback(jj).start()
                if jj == NSEG - 1:
                    ag_relay()

        p_ref[...] = jnp.dot(h_ref[...], wbuf[order_ref[j]], preferred_element_type=F32).astype(BF16)

        @pl.when((j == NSEG - 1) & (i == nb - 1))
        def _():
            for cp in sends + [relay] + passed:
                cp.wait_send()
            for jj in range(NSEG):
                writeback(jj).wait()
            for cp in own2:
                cp.wait()
            ag_finish()

    vm = pl.BlockSpec(memory_space=VMEM)
    hbm = pl.BlockSpec(memory_space=ANY)
    return pl.pallas_call(
        body, name="in_proj",
        out_shape=(jax.ShapeDtypeStruct((T, NSEG * D), BF16), jax.ShapeDtypeStruct((NSEG, D, D), BF16),
                   jax.ShapeDtypeStruct((1, NDEV, 128, D), BF16), jax.ShapeDtypeStruct((NDEV, KW, 128), F32),
                   jax.ShapeDtypeStruct((2, 128, D), BF16), jax.ShapeDtypeStruct((2, NDEV, 128, D), BF16)),
        grid_spec=pltpu.PrefetchScalarGridSpec(
            num_scalar_prefetch=1, grid=(NSEG, nb),
            in_specs=[pl.BlockSpec((tmp, D), lambda j, i, o: (i, 0)), vm, vm, vm, vm, vm],
            out_specs=(pl.BlockSpec((tmp, D), lambda j, i, o: (i, o[j])), hbm, hbm, hbm, hbm, hbm),
            scratch_shapes=[VMEM((NSEG, D, D), BF16), VMEM((1, 128, D), BF16), VMEM((2, 128, D), BF16),
                            pltpu.SemaphoreType.DMA((7,)), pltpu.SemaphoreType.DMA((7,)), pltpu.SemaphoreType.DMA((NSEG,)),
                            pltpu.SemaphoreType.DMA((2, 7)), pltpu.SemaphoreType.DMA((2, 7)), pltpu.SemaphoreType.DMA((2,)),
                            pltpu.SemaphoreType.DMA((2,))]),
        compiler_params=_cp(("arbitrary", "arbitrary")),
    )(order, h, w_in, w_co, w_so, w_o, conv_w)


def _wq_start(own2, land2):
    def body(src_ref, land_ref, send_sems, recv_sems, src_thru, land_thru, token):
        barrier = pltpu.get_barrier_semaphore()
        for k in range(1, NDEV):
            pl.semaphore_signal(barrier, inc=1, device_id=_xor_peer(k), device_id_type=MESH)
        pl.semaphore_wait(barrier, NDEV - 1)
        x, y, c = _me()
        for k in range(1, NDEV):
            pltpu.make_async_remote_copy(src_ref=src_ref, dst_ref=land_ref.at[:, _slot(x, y, c)], send_sem=send_sems.at[k - 1],
                                         recv_sem=recv_sems.at[k - 1], device_id=_xor_peer(k), device_id_type=MESH).start()
        token[...] = jnp.zeros_like(token)

    hbm = pl.BlockSpec(memory_space=pltpu.HBM)
    sem = pl.BlockSpec(memory_space=pltpu.SEMAPHORE)
    return pl.pallas_call(
        body, name="wq_start",
        out_shape=(pltpu.SemaphoreType.DMA((NDEV - 1,)), pltpu.SemaphoreType.DMA((NDEV - 1,)),
                   pltpu.HBM(own2.shape, own2.dtype), pltpu.HBM(land2.shape, land2.dtype), jax.ShapeDtypeStruct((8, 128), F32)),
        in_specs=(hbm, hbm), out_specs=(sem, sem, hbm, hbm, pl.BlockSpec(memory_space=VMEM)),
        input_output_aliases={0: 2, 1: 3},
        compiler_params=pltpu.CompilerParams(has_side_effects=pltpu.SideEffectType.DATAFLOW_SIDE_EFFECTING, collective_id=2),
    )(pltpu.with_memory_space_constraint(own2, pltpu.HBM), pltpu.with_memory_space_constraint(land2, pltpu.HBM))


def _wq_wait(send_sems, recv_sems, src_thru, land_thru, after):
    def body(src_ref, land_ref, send_sems, recv_sems, after_ref, src_dead, got_ref):
        del after_ref, src_dead, got_ref
        for k in range(1, NDEV):
            px, py, pc = _xor_peer(k)
            cp = pltpu.make_async_remote_copy(src_ref=src_ref, dst_ref=land_ref.at[:, _slot(px, py, pc)],
                                              send_sem=send_sems.at[k - 1], recv_sem=recv_sems.at[k - 1],
                                              device_id=(px, py, pc), device_id_type=MESH)
            cp.wait_send()
            cp.wait_recv()

    hbm = pl.BlockSpec(memory_space=pltpu.HBM)
    sem = pl.BlockSpec(memory_space=pltpu.SEMAPHORE)
    return pl.pallas_call(
        body, name="wq_wait",
        out_shape=(pltpu.HBM(src_thru.shape, src_thru.dtype), pltpu.HBM(land_thru.shape, land_thru.dtype)),
        in_specs=(hbm, hbm, sem, sem, pl.BlockSpec(memory_space=ANY)), out_specs=(hbm, hbm),
        input_output_aliases={0: 0, 1: 1},
        compiler_params=pltpu.CompilerParams(has_side_effects=pltpu.SideEffectType.DATAFLOW_SIDE_EFFECTING),
    )(src_thru, land_thru, send_sems, recv_sems, after)[1]


def _fill_a_ext(i, s_ref, val_ref, glu_ref, hval_ref, hglu_ref):
    ah = hval_ref[...].astype(F32) * _sig(hglu_ref[...].astype(F32))
    s_ref[0, 0:HALO, :] = jnp.where(i > 0, ah, 0.0)

    def fill(rb, carry):
        rows = _rows(rb)
        s_ref[0, pl.ds(HALO + rb * RB, RB), :] = _ld(val_ref, rows) * _sig(_ld(glu_ref, rows))
        return carry
    _blocks(TM // RB, fill)


def _halo_prev(seg):
    hb = TM // HALO
    return pl.BlockSpec((HALO, D), lambda i: (jnp.maximum(i * hb - 1, 0), seg))


def _branch_a_fwd(p, cw, conv_b, lg, lb, wq, token):
    T = p.shape[0]

    def body(val_ref, glu_ref, z_ref, hval_ref, hglu_ref, cw_ref, cb_ref, lg_ref, lb_ref, w_ref, token_ref,
             yc_ref, ya_ref, s_ref, yp_ref):
        del token_ref
        i = pl.program_id(0)
        _fill_a_ext(i, s_ref, val_ref, glu_ref, hval_ref, hglu_ref)
        _shift_copies(s_ref)

        def conv(rb, carry):
            r0 = rb * RBC
            accs = [jnp.broadcast_to(cb_ref[...], (8, D))] * (RBC // 8)
            for k in range(KW):
                o = 2 + k
                w = cw_ref[k]
                accs = [acc + w * s_ref[o % 8, pl.ds(r0 + 8 * (o // 8 + g), 8), :]
                        for g, acc in enumerate(accs)]
            for g, acc in enumerate(accs):
                yc_ref[pl.ds(r0 + 8 * g, 8), :] = acc
            return carry
        for rb in range(TM // RBC):
            conv(rb, 0)

        def post(rb, carry):
            rows = _rows(rb)
            n, _ = _ln_stats(yc_ref[rows, :])
            l = n * lg_ref[...] + lb_ref[...]
            z = _ld(z_ref, rows)
            yp_ref[rows, :] = ((l * _sig(l)) * (z * _sig(z))).astype(BF16)
            return carry
        _blocks(TM // RB, post)
        ya_ref[...] = jnp.dot(yp_ref[...], w_ref[...], preferred_element_type=F32)

    tile = lambda seg: pl.BlockSpec((TM, D), lambda i: (i, seg))
    row = pl.BlockSpec((1, D), lambda i: (0, 0))
    return pl.pallas_call(
        body, name="branch_a_fwd", grid=(T // TM,),
        out_shape=(jax.ShapeDtypeStruct((T, D), F32), jax.ShapeDtypeStruct((T, D), F32)),
        in_specs=[tile(0), tile(1), tile(2), _halo_prev(0), _halo_prev(1),
                  pl.BlockSpec((KW, 8, D), lambda i: (0, 0, 0)), row, row, row,
                  pl.BlockSpec((None, D, D), lambda i: (W_CO, 0, 0)), pl.BlockSpec((8, 128), lambda i: (0, 0))],
        out_specs=(pl.BlockSpec((TM, D), lambda i: (i, 0)), pl.BlockSpec((TM, D), lambda i: (i, 0))),
        scratch_shapes=[VMEM((8, EXT, D), F32), VMEM((TM, D), BF16)],
        compiler_params=_cp(("parallel",)),
    )(p, p, p, p, p, cw, conv_b, lg, lb, wq, token)


def _masked_ws(ws_ref, wt_ref):
    tri = lax.broadcasted_iota(jnp.int32, (CHUNK, CHUNK), 0) >= lax.broadcasted_iota(jnp.int32, (CHUNK, CHUNK), 1)
    for h in range(HEADS):
        wt_ref[h] = jnp.where(tri, ws_ref[h], 0.0).astype(BF16)


def _sgu_mix(wt_ref, vl_ref, bst_ref, s_ref):
    for h in range(HEADS):
        cs = slice(h * HD, (h + 1) * HD)
        s = jnp.dot(wt_ref[h], _chunks_on_lanes(vl_ref, cs), preferred_element_type=F32) + bst_ref[:, h:h + 1]
        for ck in range(TM // CHUNK):
            s_ref[ck * CHUNK:(ck + 1) * CHUNK, cs] = s[:, ck * HD:(ck + 1) * HD]


def _chunks_on_lanes(ref, cs):
    return jnp.concatenate([ref[ck * CHUNK:(ck + 1) * CHUNK, cs] for ck in range(TM // CHUNK)], axis=1)


def _branch_b_fwd(p, ws, bst, sg, sb, wq):
    T = p.shape[0]

    def body(pu_ref, pv_ref, pz_ref, ws_ref, bst_ref, sg_ref, sb_ref, w_ref, yb_ref,
             wt_ref, vl_ref, t_ref, s_ref, yp_ref):
        _masked_ws(ws_ref, wt_ref)

        def pre(rb, carry):
            rows = _rows(rb)
            vg, _ = _gelu_parts(_ld(pv_ref, rows))
            vn, _ = _ln_stats(vg)
            vl_ref[rows, :] = (vn * sg_ref[...] + sb_ref[...]).astype(BF16)
            u, _ = _gelu_parts(_ld(pu_ref, rows))
            z = _ld(pz_ref, rows)
            t_ref[rows, :] = u * (z * _sig(z))
            return carry
        _blocks(TM // RB, pre)
        _sgu_mix(wt_ref, vl_ref, bst_ref, s_ref)

        def post(rb, carry):
            rows = _rows(rb)
            yp_ref[rows, :] = (t_ref[rows, :] * s_ref[rows, :]).astype(BF16)
            return carry
        _blocks(TM // RB, post)
        yb_ref[...] = jnp.dot(yp_ref[...], w_ref[...], preferred_element_type=F32)

    tile = lambda seg: pl.BlockSpec((TM, D), lambda i: (i, seg))
    row = pl.BlockSpec((1, D), lambda i: (0, 0))
    return pl.pallas_call(
        body, name="branch_b_fwd", grid=(T // TM,),
        out_shape=jax.ShapeDtypeStruct((T, D), F32),
        in_specs=[tile(3), tile(4), tile(5), pl.BlockSpec((HEADS, CHUNK, CHUNK), lambda i: (0, 0, 0)),
                  pl.BlockSpec((CHUNK, HEADS), lambda i: (0, 0)), row, row,
                  pl.BlockSpec((None, D, D), lambda i: (W_SO, 0, 0))],
        out_specs=pl.BlockSpec((TM, D), lambda i: (i, 0)),
        scratch_shapes=[VMEM((HEADS, CHUNK, CHUNK), BF16), VMEM((TM, D), BF16), VMEM((TM, D), F32),
                        VMEM((TM, D), F32), VMEM((TM, D), BF16)],
        compiler_params=_cp(("parallel",)),
    )(p, p, p, ws, bst, sg, sb, wq)


def _merge_loss(p, ya, yb, x, tgt, mod, g_final, wq):
    T = x.shape[0]
    nt = T // TM

    def body(ga_ref, gb_ref, ya_ref, yb_ref, x_ref, t_ref, mod_ref, gf_ref, w_ref,
             dx2_ref, dya_ref, dyb_ref, dp_ref, gwo_ref, st_ref,
             mrg_s, out_s, dout_s, dm_s, gw_acc):
        i = pl.program_id(0)
        base = _ring_base(i, 1)
        cur = pl.ds(base, TM)

        @pl.when(i == 0)
        def _():
            gw_acc[...] = jnp.zeros_like(gw_acc)
            st_ref[...] = jnp.zeros_like(st_ref)

        def merge(rb, carry):
            rows = _rows(rb)
            mrg_s[_ring_rows(base, rb), :] = (_sig(_ld(ga_ref, rows)) * ya_ref[rows, :]
                                              + _sig(_ld(gb_ref, rows)) * yb_ref[rows, :]).astype(BF16)
            return carry
        _blocks(TM // RB, merge)
        out_s[...] = jnp.dot(mrg_s[cur, :], w_ref[...], preferred_element_type=F32)

        def head(rb, carry):
            loss, gg, dg = carry
            rows = _rows(rb)
            gate = mod_ref[2:3, :]
            gf = gf_ref[...]
            out = out_s[rows, :]
            x2 = x_ref[rows, :] + gate * out
            r2 = lax.rsqrt(jnp.mean(x2 * x2, axis=-1, keepdims=True) + EPS)
            x2n = x2 * r2
            diff = x2n * gf - t_ref[rows, :]
            dy = diff * (1.0 / D)
            dx2n = dy * gf
            dx2 = r2 * (dx2n - x2n * jnp.mean(dx2n * x2n, axis=-1, keepdims=True))
            dx2_ref[rows, :] = dx2
            dout_s[_ring_rows(base, rb), :] = (dx2 * gate).astype(BF16)
            return loss + _fold8(diff * diff), gg + _fold8(dy * x2n), dg + _fold8(dx2 * out)
        zero = jnp.zeros((8, D), F32)
        loss, gg, dg = _blocks(TM // RB, head, (zero, zero, zero))
        st_ref[0] = st_ref[0] + loss * (0.5 / D)
        st_ref[1] = st_ref[1] + gg
        st_ref[2] = st_ref[2] + dg

        dm_s[...] = lax.dot_general(dout_s[cur, :], w_ref[...], NT, preferred_element_type=F32)

        @_when_ring_full(i, 1)
        def _():
            gw_acc[...] += lax.dot_general(mrg_s[...], dout_s[...], TN, preferred_element_type=F32)

        def split(rb, carry):
            rows = _rows(rb)
            dm = dm_s[rows, :]
            sa = _sig(_ld(ga_ref, rows))
            sb = _sig(_ld(gb_ref, rows))
            dya_ref[rows, :] = (dm * sa).astype(BF16)
            dyb_ref[rows, :] = (dm * sb).astype(BF16)
            dp_ref[0, rows, :] = (dm * ya_ref[rows, :] * (sa * (1.0 - sa))).astype(BF16)
            dp_ref[1, rows, :] = (dm * yb_ref[rows, :] * (sb * (1.0 - sb))).astype(BF16)
            return carry
        _blocks(TM // RB, split)

        @pl.when(i == nt - 1)
        def _():
            gwo_ref[...] = gw_acc[...].astype(BF16)

    tile = pl.BlockSpec((TM, D), lambda i: (i, 0))
    ring = VMEM((_ring_tiles(1) * TM, D), BF16)
    return pl.pallas_call(
        body, name="merge_loss", grid=(nt,),
        out_shape=(jax.ShapeDtypeStruct((T, D), F32), jax.ShapeDtypeStruct((T, D), BF16), jax.ShapeDtypeStruct((T, D), BF16),
                   jax.ShapeDtypeStruct((NSEG, T, D), BF16), jax.ShapeDtypeStruct((D, D), BF16),
                   jax.ShapeDtypeStruct((3, 8, D), F32)),
        in_specs=[pl.BlockSpec((TM, D), lambda i: (i, 6)), pl.BlockSpec((TM, D), lambda i: (i, 7)), tile, tile, tile, tile,
                  pl.BlockSpec((3, D), lambda i: (0, 0)), pl.BlockSpec((1, D), lambda i: (0, 0)),
                  pl.BlockSpec((None, D, D), lambda i: (W_O, 0, 0))],
        out_specs=(tile, tile, tile, pl.BlockSpec((2, TM, D), lambda i: (3, i, 0)),
                   pl.BlockSpec((D, D), lambda i: (0, 0)), pl.BlockSpec((3, 8, D), lambda i: (0, 0, 0))),
        scratch_shapes=[ring, VMEM((TM, D), F32), ring, VMEM((TM, D), F32), VMEM((D, D), F32)],
        compiler_params=_cp(("arbitrary",)),
    )(p, p, ya, yb, x, tgt, mod, g_final, wq)


def _branch_b_bwd(dp, dyb, p, ws, bst, sg, sb, wq):
    T = p.shape[0]
    nt = T // TM

    def body(dp_in, dyb_ref, pu_ref, pv_ref, pz_ref, ws_ref, bst_ref, sg_ref, sb_ref, w_ref,
             dp_ref, gw_ref, gws_ref, gbs_ref, st_ref,
             wt_ref, d_s, vl_s, vn_s, gpv_s, rs_s, s_s, ds_s, ds32_s, yp_s, dy_s, dvl_s, gw_acc):
        del dp_in
        i = pl.program_id(0)
        base = _ring_base(i, nt)

        @pl.when(i == 0)
        def _():
            gw_acc[...] = jnp.zeros_like(gw_acc)
            gws_ref[...] = jnp.zeros_like(gws_ref)
            gbs_ref[...] = jnp.zeros_like(gbs_ref)
            st_ref[...] = jnp.zeros_like(st_ref)
            _masked_ws(ws_ref, wt_ref)

        d_s[...] = lax.dot_general(dyb_ref[...], w_ref[...], NT, preferred_element_type=F32)
        dy_s[pl.ds(base, TM), :] = dyb_ref[...]

        def pre(rb, carry):
            rows = _rows(rb)
            vg, gpv = _gelu_parts(_ld(pv_ref, rows))
            vn, rs = _ln_stats(vg)
            vl_s[rows, :] = (vn * sg_ref[...] + sb_ref[...]).astype(BF16)
            vn_s[rows, :] = vn
            gpv_s[rows, :] = gpv
            rs_s[rows, :] = rs
            return carry
        _blocks(TM // RB, pre)
        _sgu_mix(wt_ref, vl_s, bst_ref, s_s)

        def mid(rb, carry):
            rows = _rows(rb)
            u, gpu = _gelu_parts(_ld(pu_ref, rows))
            z = _ld(pz_ref, rows)
            sz = _sig(z)
            siluz = z * sz
            d = d_s[rows, :]
            s = s_s[rows, :]
            t = u * siluz
            yp_s[_ring_rows(base, rb), :] = (t * s).astype(BF16)
            dp_ref[0, rows, :] = (d * s * siluz * gpu).astype(BF16)
            ds = d * t
            ds32_s[rows, :] = ds
            ds_s[rows, :] = ds.astype(BF16)
            dp_ref[2, rows, :] = (d * u * s * (sz * (1.0 + z * (1.0 - sz)))).astype(BF16)
            return carry
        _blocks(TM // RB, mid)

        for h in range(HEADS):
            cs = slice(h * HD, (h + 1) * HD)
            dsh = _chunks_on_lanes(ds_s, cs)
            dvl = lax.dot_general(wt_ref[h], dsh, TN, preferred_element_type=F32)
            gbs = gbs_ref[h]
            for ck in range(TM // CHUNK):
                r = slice(ck * CHUNK, (ck + 1) * CHUNK)
                dvl_s[r, cs] = dvl[:, ck * HD:(ck + 1) * HD]
                gbs = gbs + ds32_s[r, cs]
            gbs_ref[h] = gbs
            gws_ref[h] += lax.dot_general(dsh, _chunks_on_lanes(vl_s, cs), NT, preferred_element_type=F32)

        def post(rb, carry):
            g_sg, g_sb = carry
            rows = _rows(rb)
            dvl = dvl_s[rows, :]
            vn = vn_s[rows, :]
            dvg = _ln_bwd(dvl * sg_ref[...], vn, rs_s[rows, :])
            dp_ref[1, rows, :] = (dvg * gpv_s[rows, :]).astype(BF16)
            return g_sg + _fold8(dvl * vn), g_sb + _fold8(dvl)
        zero = jnp.zeros((8, D), F32)
        g_sg, g_sb = _blocks(TM // RB, post, (zero, zero))
        st_ref[0] = st_ref[0] + g_sg
        st_ref[1] = st_ref[1] + g_sb

        @_when_ring_full(i, nt)
        def _():
            gw_acc[...] += lax.dot_general(yp_s[...], dy_s[...], TN, preferred_element_type=F32)

        @pl.when(i == nt - 1)
        def _():
            gw_ref[...] = gw_acc[...].astype(BF16)

    tile = lambda seg: pl.BlockSpec((TM, D), lambda i: (i, seg))
    row = pl.BlockSpec((1, D), lambda i: (0, 0))
    hh = pl.BlockSpec((HEADS, CHUNK, CHUNK), lambda i: (0, 0, 0))
    ring = VMEM((_ring_tiles(nt) * TM, D), BF16)
    return pl.pallas_call(
        body, name="branch_b_bwd", grid=(nt,),
        out_shape=(jax.ShapeDtypeStruct((NSEG, T, D), BF16), jax.ShapeDtypeStruct((D, D), BF16),
                   jax.ShapeDtypeStruct((HEADS, CHUNK, CHUNK), F32), jax.ShapeDtypeStruct((HEADS, CHUNK, HD), F32),
                   jax.ShapeDtypeStruct((2, 8, D), F32)),
        in_specs=[pl.BlockSpec(memory_space=ANY), pl.BlockSpec((TM, D), lambda i: (i, 0)), tile(3), tile(4), tile(5),
                  hh, pl.BlockSpec((CHUNK, HEADS), lambda i: (0, 0)), row, row,
                  pl.BlockSpec((None, D, D), lambda i: (W_SO, 0, 0))],
        out_specs=(pl.BlockSpec((3, TM, D), lambda i: (1, i, 0)), pl.BlockSpec((D, D), lambda i: (0, 0)), hh, hh,
                   pl.BlockSpec((2, 8, D), lambda i: (0, 0, 0))),
        scratch_shapes=[VMEM((HEADS, CHUNK, CHUNK), BF16), VMEM((TM, D), F32), VMEM((TM, D), BF16), VMEM((TM, D), F32),
                        VMEM((TM, D), F32), VMEM((TM, 1), F32), VMEM((TM, D), F32), VMEM((TM, D), BF16), VMEM((TM, D), F32),
                        ring, ring, VMEM((TM, D), F32), VMEM((D, D), F32)],
        input_output_aliases={0: 0},
        compiler_params=_cp(("arbitrary",)),
    )(dp, dyb, p, p, p, ws, bst, sg, sb, wq)


def _branch_a_bwd1(dp, dya, yc, p, lg, lb, wq):
    T = p.shape[0]
    nt = T // TM

    def body(dp_in, dya_ref, yc_ref, z_ref, lg_ref, lb_ref, w_ref, dp_ref, dyc_ref, gw_ref, st_ref,
             d_s, yp_s, dy_s, gw_acc):
        del dp_in
        i = pl.program_id(0)
        base = _ring_base(i, nt)

        @pl.when(i == 0)
        def _():
            gw_acc[...] = jnp.zeros_like(gw_acc)
            st_ref[...] = jnp.zeros_like(st_ref)

        d_s[...] = lax.dot_general(dya_ref[...], w_ref[...], NT, preferred_element_type=F32)
        dy_s[pl.ds(base, TM), :] = dya_ref[...]

        def blk(rb, carry):
            g_lg, g_lb, g_cb = carry
            rows = _rows(rb)
            n, rs = _ln_stats(yc_ref[rows, :])
            l = n * lg_ref[...] + lb_ref[...]
            sgl = _sig(l)
            sl = l * sgl
            z = _ld(z_ref, rows)
            sz = _sig(z)
            siluz = z * sz
            d = d_s[rows, :]
            yp_s[_ring_rows(base, rb), :] = (sl * siluz).astype(BF16)
            dp_ref[rows, :] = (d * sl * (sz * (1.0 + z * (1.0 - sz)))).astype(BF16)
            dl = (d * siluz) * (sgl * (1.0 + l * (1.0 - sgl)))
            dyc = _ln_bwd(dl * lg_ref[...], n, rs)
            dyc_ref[rows, :] = dyc
            return g_lg + _fold8(dl * n), g_lb + _fold8(dl), g_cb + _fold8(dyc)
        zero = jnp.zeros((8, D), F32)
        g_lg, g_lb, g_cb = _blocks(TM // RB, blk, (zero, zero, zero))
        st_ref[0] = st_ref[0] + g_lg
        st_ref[1] = st_ref[1] + g_lb
        st_ref[2] = st_ref[2] + g_cb

        @_when_ring_full(i, nt)
        def _():
            gw_acc[...] += lax.dot_general(yp_s[...], dy_s[...], TN, preferred_element_type=F32)

        @pl.when(i == nt - 1)
        def _():
            gw_ref[...] = gw_acc[...].astype(BF16)

    tile = pl.BlockSpec((TM, D), lambda i: (i, 0))
    row = pl.BlockSpec((1, D), lambda i: (0, 0))
    ring = VMEM((_ring_tiles(nt) * TM, D), BF16)
    return pl.pallas_call(
        body, name="branch_a_bwd1", grid=(nt,),
        out_shape=(jax.ShapeDtypeStruct((NSEG, T, D), BF16), jax.ShapeDtypeStruct((T, D), F32),
                   jax.ShapeDtypeStruct((D, D), BF16), jax.ShapeDtypeStruct((3, 8, D), F32)),
        in_specs=[pl.BlockSpec(memory_space=ANY), tile, tile, pl.BlockSpec((TM, D), lambda i: (i, 2)), row, row,
                  pl.BlockSpec((None, D, D), lambda i: (0, 0, 0))],
        out_specs=(pl.BlockSpec((None, TM, D), lambda i: (2, i, 0)), tile, pl.BlockSpec((D, D), lambda i: (0, 0)),
                   pl.BlockSpec((3, 8, D), lambda i: (0, 0, 0))),
        scratch_shapes=[VMEM((TM, D), F32), ring, ring, VMEM((D, D), F32)],
        input_output_aliases={0: 0},
        compiler_params=_cp(("arbitrary",)),
    )(dp, dya, yc, p, lg, lb, wq)


def _branch_a_bwd2(dp, dyc, p, cw, gw_co, gw_so, gw_o, early):
    T = p.shape[0]
    nt = T // TM
    hb = TM // HALO

    def body(dp_in, dyc_ref, hdyc_ref, val_ref, glu_ref, cw_ref, gco_ref, gso_ref, go_ref, e_ref,
             dp_ref, gcw_ref, rq_ref, eg_ref, a_s, sd_ref, da_s, x_send, x_recv, x_loc, ag_send, ag_recv, ag_loc):
        del dp_in
        i = pl.program_id(0)
        start_exchange, wait_exchange = _owner_exchange((gco_ref, gso_ref, go_ref), rq_ref, x_send, x_recv, x_loc)
        ag_start, ag_relay, ag_finish = _allgather_phases([e_ref], lambda a, slot: eg_ref.at[slot],
                                                          ag_send, ag_recv, ag_loc)

        @pl.when(i == 0)
        def _():
            gcw_ref[...] = jnp.zeros_like(gcw_ref)
            ag_start()
            start_exchange()

        @pl.when(i == nt // 2)
        def _():
            ag_relay()

        def fill(rb, carry):
            rows = _rows(rb)
            a_s[rows, :] = _ld(val_ref, rows) * _sig(_ld(glu_ref, rows))
            return carry
        _blocks(TM // RB, fill)
        sd_ref[0, 0:TM, :] = dyc_ref[...]
        sd_ref[0, TM:EXT, :] = jnp.where(i < nt - 1, hdyc_ref[...], 0.0)
        _shift_copies(sd_ref)

        def conv_t(rb, carry):
            r0 = rb * RBC
            accs = [jnp.zeros((8, D), F32)] * (RBC // 8)
            for k in range(KW):
                o = KW - 1 - k
                w = cw_ref[k]
                accs = [acc + w * sd_ref[o % 8, pl.ds(r0 + 8 * (o // 8 + g), 8), :]
                        for g, acc in enumerate(accs)]
            for g, acc in enumerate(accs):
                da_s[pl.ds(r0 + 8 * g, 8), :] = acc
            return carry
        for rb in range(TM // RBC):
            conv_t(rb, 0)

        for k0 in range(0, KW, TAPG):
            taps = list(range(k0, min(k0 + TAPG, KW)))

            def tap_group(rb, accs, taps=taps):
                for u in range(2):
                    r0 = rb * 32 + 16 * u
                    a = a_s[pl.ds(r0, 16), :]
                    out = []
                    for k, acc in zip(taps, accs):
                        o = KW - 1 - k
                        d = sd_ref[o % 8, pl.ds(r0 + 8 * (o // 8), 16), :]
                        out.append(acc + _fold8(a * d))
                    accs = tuple(out)
                return accs
            sums = tuple(jnp.zeros((8, D), F32) for _ in taps)
            for rb in range(TM // 32):
                sums = tap_group(rb, sums)
            for k, s in zip(taps, sums):
                gcw_ref[k] = gcw_ref[k] + s

        def glu_b(rb, carry):
            rows = _rows(rb)
            da = da_s[rows, :]
            sg = _sig(_ld(glu_ref, rows))
            dp_ref[0, rows, :] = (da * sg).astype(BF16)
            dp_ref[1, rows, :] = (da * _ld(val_ref, rows) * (sg * (1.0 - sg))).astype(BF16)
            return carry
        _blocks(TM // RB, glu_b)

        @pl.when(i == nt - 1)
        def _():
            wait_exchange()
            ag_finish()

    tile = lambda seg: pl.BlockSpec((TM, D), lambda i: (i, seg))
    hbm = pl.BlockSpec(memory_space=ANY)
    return pl.pallas_call(
        body, name="branch_a_bwd2", grid=(nt,),
        out_shape=(jax.ShapeDtypeStruct((NSEG, T, D), BF16), jax.ShapeDtypeStruct((32, 8, D), F32),
                   jax.ShapeDtypeStruct((NDEV, 3, 128, D), BF16), jax.ShapeDtypeStruct((NDEV, NEARLY, D), F32)),
        in_specs=[hbm, pl.BlockSpec((TM, D), lambda i: (i, 0)),
                  pl.BlockSpec((HALO, D), lambda i: (jnp.minimum((i + 1) * hb, nt * hb - 1), 0)),
                  tile(0), tile(1), pl.BlockSpec((KW, 8, D), lambda i: (0, 0, 0)),
                  hbm, hbm, hbm, hbm],
        out_specs=(pl.BlockSpec((2, TM, D), lambda i: (0, i, 0)), pl.BlockSpec((32, 8, D), lambda i: (0, 0, 0)), hbm, hbm),
        scratch_shapes=[VMEM((TM, D), F32), VMEM((8, EXT, D), F32), VMEM((TM, D), F32),
                        pltpu.SemaphoreType.DMA((3, 7)), pltpu.SemaphoreType.DMA((3, 7)), pltpu.SemaphoreType.DMA((3,)),
                        pltpu.SemaphoreType.DMA((1, 7)), pltpu.SemaphoreType.DMA((1, 7)), pltpu.SemaphoreType.DMA((1,))],
        input_output_aliases={0: 0},
        compiler_params=_cp(("arbitrary",)),
    )(dp, dyc, dyc, p, p, cw, gw_co, gw_so, gw_o, early)


def _in_bwd_dx(dp, wing, x, dx2, mod, g_pre, token):
    T = x.shape[0]
    nt = T // TM

    def body(dp_ref, w_ref, x_ref, dx2_ref, mod_ref, g_ref, token_ref, gx_ref, st_ref, dh_s):
        del token_ref
        i = pl.program_id(0)

        @pl.when(i == 0)
        def _():
            st_ref[...] = jnp.zeros_like(st_ref)

        dh_s[...] = lax.dot_general(dp_ref[0], w_ref[0], NT, preferred_element_type=F32)
        for j in range(1, NSEG):
            dh_s[...] += lax.dot_general(dp_ref[j], w_ref[j], NT, preferred_element_type=F32)

        def blk(rb, carry):
            d_sh, d_sc, g_g = carry
            rows = _rows(rb)
            xv = x_ref[rows, :]
            r = lax.rsqrt(jnp.mean(xv * xv, axis=-1, keepdims=True) + EPS)
            xn = xv * r
            g = g_ref[...]
            hpre = xn * g
            dh = dh_s[rows, :]
            dhp = dh * (1.0 + mod_ref[1:2, :])
            dxn = dhp * g
            gx_ref[rows, :] = dx2_ref[rows, :] + r * (dxn - xn * jnp.mean(dxn * xn, axis=-1, keepdims=True))
            return d_sh + _fold8(dh), d_sc + _fold8(dh * hpre), g_g + _fold8(dhp * xn)
        zero = jnp.zeros((8, D), F32)
        d_sh, d_sc, g_g = _blocks(TM // RB, blk, (zero, zero, zero))
        st_ref[0] = st_ref[0] + d_sh
        st_ref[1] = st_ref[1] + d_sc
        st_ref[2] = st_ref[2] + g_g

    tile = pl.BlockSpec((TM, D), lambda i: (i, 0))
    return pl.pallas_call(
        body, name="in_bwd_dx", grid=(nt,),
        out_shape=(jax.ShapeDtypeStruct((T, D), F32), jax.ShapeDtypeStruct((3, 8, D), F32)),
        in_specs=[pl.BlockSpec((NSEG, TM, D), lambda i: (0, i, 0)),
                  pl.BlockSpec((NSEG, D, D), lambda i: (0, 0, 0), pipeline_mode=pl.Buffered(1)),
                  tile, tile, pl.BlockSpec((3, D), lambda i: (0, 0)), pl.BlockSpec((1, D), lambda i: (0, 0)),
                  pl.BlockSpec((8, 128), lambda i: (0, 0))],
        out_specs=(tile, pl.BlockSpec((3, 8, D), lambda i: (0, 0, 0))),
        scratch_shapes=[VMEM((TM, D), F32)],
        compiler_params=_cp(("arbitrary",), vmem_mb=56),
    )(dp, wing, x, dx2, mod, g_pre, token)


RS_ORDER = (5, 4, 3, 2, 7, 6, 1, 0)
RS_HALF = D // 2


def _in_bwd_dw(ht, dp, order):
    T = ht.shape[1]
    tmw = min(4 * TMB, T)
    nb = T // tmw
    nu = 2 * NSEG

    def body(order_ref, ht_ref, dp_ref, csum_ref, g_ref, acc, sbuf, rsib, send_sems, recv_sems, out_sems):
        j = pl.program_id(0)
        i = pl.program_id(1)
        x, y, c = _me()
        sibling = (x, y, 1 - c)

        def to_sibling(s):
            return pltpu.make_async_remote_copy(src_ref=sbuf.at[s], dst_ref=rsib.at[s], send_sem=send_sems.at[s],
                                                recv_sem=recv_sems.at[s], device_id=sibling, device_id_type=MESH)

        def to_csum(t):
            cols = pl.ds((t % 2) * RS_HALF, RS_HALF)
            return pltpu.make_async_copy(sbuf.at[8 + t], csum_ref.at[t // 2, :, cols], out_sems.at[t])

        @pl.when(i == 0)
        def _():
            acc[...] = jnp.zeros_like(acc)
        acc[...] += jnp.dot(ht_ref[...], dp_ref[...], preferred_element_type=F32)

        for jj in range(nu):
            chip, half, mine = jj // 4, (jj % 4) // 2, jj % 2
            s = 2 * chip + half

            @pl.when((j == jj) & (i == nb - 1))
            def _(chip=chip, half=half, mine=mine, s=s):
                if not mine:
                    sbuf[s] = acc[...].astype(BF16)
                    to_sibling(s).start()
                elif chip < 3:
                    to_sibling(s).wait_recv()
                    sbuf[8 + s] = (acc[...] + rsib[s].astype(F32)).astype(BF16)
                    to_csum(s).start()
                else:
                    to_sibling(s).wait_recv()
                    g_ref[:, half * RS_HALF:(half + 1) * RS_HALF] = acc[...] + rsib[s].astype(F32)

        @pl.when((j == nu - 1) & (i == nb - 1))
        def _():
            for s in range(8):
                to_sibling(s).wait_send()
            for t in range(6):
                to_csum(t).wait()

    hbm = pl.BlockSpec(memory_space=ANY)
    return pl.pallas_call(
        body, name="in_bwd_dw",
        out_shape=(jax.ShapeDtypeStruct((3, D, D), BF16), jax.ShapeDtypeStruct((D, D), F32)),
        grid_spec=pltpu.PrefetchScalarGridSpec(
            num_scalar_prefetch=1, grid=(nu, nb),
            in_specs=[pl.BlockSpec((D, tmw), lambda j, i, o: (0, i)),
                      pl.BlockSpec((None, tmw, RS_HALF), lambda j, i, o: (o[2 * (j // 4) + j % 2], i, (j % 4) // 2))],
            out_specs=(hbm, pl.BlockSpec((D, D), lambda j, i, o: (0, 0))),
            scratch_shapes=[VMEM((D, RS_HALF), F32), VMEM((14, D, RS_HALF), BF16), VMEM((8, D, RS_HALF), BF16),
                            pltpu.SemaphoreType.DMA((8,)), pltpu.SemaphoreType.DMA((8,)), pltpu.SemaphoreType.DMA((6,))]),
        compiler_params=_cp(("arbitrary", "arbitrary"), vmem_mb=56),
    )(order, ht, dp)


def _gather_late(late):
    def body(sm_ref, smg_ref, ag_send, ag_recv, ag_loc):
        _allgather([sm_ref], lambda a, slot: smg_ref.at[slot], ag_send, ag_recv, ag_loc)

    hbm = pl.BlockSpec(memory_space=ANY)
    return pl.pallas_call(
        body, name="gather_late",
        out_shape=jax.ShapeDtypeStruct((NDEV, NLATE, D), F32),
        in_specs=[hbm], out_specs=hbm,
        scratch_shapes=[pltpu.SemaphoreType.DMA((1, 7)), pltpu.SemaphoreType.DMA((1, 7)), pltpu.SemaphoreType.DMA((1,))],
        compiler_params=_cp(vmem_mb=32),
    )(late)


RS_CHIPS = ((1, 0), (0, 1), (1, 1))


def _rs_peer(q):
    x, y, c = _me()
    fx, fy = RS_CHIPS[q]
    return (1 - x if fx else x, 1 - y if fy else y, c)


def _rs_start(csum):
    def body(csum_ref, land_ref, send_sems, recv_sems, csum_thru, land_thru, token):
        barrier = pltpu.get_barrier_semaphore()
        for q in range(3):
            pl.semaphore_signal(barrier, inc=1, device_id=_rs_peer(q), device_id_type=MESH)
        pl.semaphore_wait(barrier, 3)
        for q in range(3):
            pltpu.make_async_remote_copy(src_ref=csum_ref.at[q], dst_ref=land_ref.at[q], send_sem=send_sems.at[q],
                                         recv_sem=recv_sems.at[q], device_id=_rs_peer(q), device_id_type=MESH).start()
        token[...] = jnp.zeros_like(token)

    hbm = pl.BlockSpec(memory_space=pltpu.HBM)
    sem = pl.BlockSpec(memory_space=pltpu.SEMAPHORE)
    land = pltpu.with_memory_space_constraint(lax.empty(csum.shape, csum.dtype), pltpu.HBM)
    return pl.pallas_call(
        body, name="rs_start",
        out_shape=(pltpu.SemaphoreType.DMA((3,)), pltpu.SemaphoreType.DMA((3,)), pltpu.HBM(csum.shape, csum.dtype),
                   pltpu.HBM(csum.shape, csum.dtype), jax.ShapeDtypeStruct((8, 128), F32)),
        in_specs=(hbm, hbm), out_specs=(sem, sem, hbm, hbm, pl.BlockSpec(memory_space=VMEM)),
        input_output_aliases={0: 2, 1: 3},
        compiler_params=pltpu.CompilerParams(has_side_effects=pltpu.SideEffectType.DATAFLOW_SIDE_EFFECTING, collective_id=1),
    )(pltpu.with_memory_space_constraint(csum, pltpu.HBM), land)


def _rs_wait(send_sems, recv_sems, csum_thru, land_thru, after):
    def body(csum_ref, land_ref, send_sems, recv_sems, after_ref, csum_dead, got_ref):
        del after_ref, csum_dead, got_ref
        for q in range(3):
            cp = pltpu.make_async_remote_copy(src_ref=csum_ref.at[q], dst_ref=land_ref.at[q], send_sem=send_sems.at[q],
                                              recv_sem=recv_sems.at[q], device_id=_rs_peer(q), device_id_type=MESH)
            cp.wait_send()
            cp.wait_recv()

    hbm = pl.BlockSpec(memory_space=pltpu.HBM)
    sem = pl.BlockSpec(memory_space=pltpu.SEMAPHORE)
    return pl.pallas_call(
        body, name="rs_wait",
        out_shape=(pltpu.HBM(csum_thru.shape, csum_thru.dtype), pltpu.HBM(csum_thru.shape, csum_thru.dtype)),
        in_specs=(hbm, hbm, sem, sem, pl.BlockSpec(memory_space=ANY)), out_specs=(hbm, hbm),
        input_output_aliases={0: 0, 1: 1},
        compiler_params=pltpu.CompilerParams(has_side_effects=pltpu.SideEffectType.DATAFLOW_SIDE_EFFECTING),
    )(csum_thru, land_thru, send_sems, recv_sems, after)[1]


def _pack_early(st_a, st_b, st_m, gws, gbs):
    def body(sa_ref, sb_ref, sm_ref, gws_ref, gbs_ref, out_ref):
        out_ref[...] = jnp.zeros_like(out_ref)
        fold = lambda v: jnp.sum(v, axis=0, keepdims=True)
        out_ref[0:1, :] = fold(sm_ref[2])
        out_ref[1:2, :] = fold(sa_ref[2])
        out_ref[2:3, :] = fold(sa_ref[0])
        out_ref[3:4, :] = fold(sa_ref[1])
        out_ref[4:5, :] = fold(sb_ref[0])
        out_ref[5:6, :] = fold(sb_ref[1])
        out_ref[7:8, :] = fold(sm_ref[1])
        out_ref[8:9, :] = fold(sm_ref[0])
        tri = lax.broadcasted_iota(jnp.int32, (CHUNK, CHUNK), 0) >= lax.broadcasted_iota(jnp.int32, (CHUNK, CHUNK), 1)
        for h in range(HEADS):
            out_ref[16:16 + CHUNK, h * CHUNK:(h + 1) * CHUNK] = jnp.where(tri, gws_ref[h], 0.0)
            out_ref[6:7, h * CHUNK:(h + 1) * CHUNK] = fold(gbs_ref[h].T)

    vm = pl.BlockSpec(memory_space=VMEM)
    return pl.pallas_call(
        body, name="pack_early", out_shape=jax.ShapeDtypeStruct((NEARLY, D), F32),
        in_specs=[vm] * 5, out_specs=vm, compiler_params=_cp(vmem_mb=32),
    )(st_a, st_b, st_m, gws, gbs)


def _pack_late(st_x, gcw):
    def body(sx_ref, gcw_ref, out_ref):
        out_ref[...] = jnp.zeros_like(out_ref)
        fold = lambda v: jnp.sum(v, axis=0, keepdims=True)
        for r in range(3):
            out_ref[r:r + 1, :] = fold(sx_ref[r])
        for k in range(KW):
            out_ref[8 + k:9 + k, :] = fold(gcw_ref[k])

    vm = pl.BlockSpec(memory_space=VMEM)
    return pl.pallas_call(
        body, name="pack_late", out_shape=jax.ShapeDtypeStruct((NLATE, D), F32),
        in_specs=[vm] * 2, out_specs=vm, compiler_params=_cp(vmem_mb=32),
    )(st_x, gcw)


def _pack_params(name, b_ada, g_pre, conv_b, lg, lb, sg, sb, b_sgu, g_final, w_sgu_t):
    def body(ba_ref, *refs):
        rows, ws_ref, out_ref = refs[:8], refs[8], refs[9]
        out_ref[...] = jnp.zeros_like(out_ref)
        for r in range(3):
            out_ref[r:r + 1, :] = ba_ref[:, r * D:(r + 1) * D]
        for r, ref in enumerate(rows):
            out_ref[3 + r:4 + r, :] = ref[...]
        out_ref[ROW_WS:ROW_WS + CHUNK, :] = ws_ref[...]

    vm = pl.BlockSpec(memory_space=VMEM)
    return pl.pallas_call(
        body, name=name, out_shape=jax.ShapeDtypeStruct((NSMALL, D), F32),
        in_specs=[vm] * 10, out_specs=vm, compiler_params=_cp(vmem_mb=32),
    )(b_ada, g_pre, conv_b, lg, lb, sg, sb, b_sgu, g_final, w_sgu_t)


def _adam(w, g, m, v):
    m2 = ADAM_B1 * m + (1.0 - ADAM_B1) * g
    v2 = ADAM_B2 * v + (1.0 - ADAM_B2) * (g * g)
    m_hat = m2 / (1.0 - ADAM_B1 ** ADAM_STEP)
    v_hat = v2 / (1.0 - ADAM_B2 ** ADAM_STEP)
    delta = -ADAM_LR * (m_hat / (jnp.sqrt(v_hat) + ADAM_EPS) + ADAM_WD * w)
    return delta, m2, v2


def _small_finish(eg, lg, w, m, v):
    def body(eg_ref, lg_ref, w_ref, m_ref, v_ref, loss_ref, g_ref, d_ref, m2_ref, v2_ref):
        e = eg_ref[0]
        l = lg_ref[0]
        for s in range(1, NDEV):
            e = e + eg_ref[s]
            l = l + lg_ref[s]
        g_ref[...] = jnp.zeros_like(g_ref)
        g_ref[0:2, :] = l[0:2, :]
        g_ref[2:3, :] = e[0:1, :]
        g_ref[3:4, :] = l[2:3, :]
        g_ref[4:12, :] = e[1:9, :]
        g_ref[ROW_CW:ROW_CW + KW, :] = l[8:8 + KW, :]
        g_ref[ROW_WS:ROW_WS + CHUNK, :] = e[16:16 + CHUNK, :]
        loss_ref[...] = jnp.sum(e[8:9, :], axis=1, keepdims=True)
        d_ref[...], m2_ref[...], v2_ref[...] = _adam(w_ref[...], g_ref[...], m_ref[...], v_ref[...])

    vm = pl.BlockSpec(memory_space=VMEM)
    sd = jax.ShapeDtypeStruct((NSMALL, D), F32)
    return pl.pallas_call(
        body, name="small_finish", out_shape=(jax.ShapeDtypeStruct((1, 1), F32), sd, sd, sd, sd),
        in_specs=[vm] * 5, out_specs=(vm,) * 5, compiler_params=_cp(vmem_mb=32),
    )(eg, lg, w, m, v)


def _ada_grad_adam(ct, dm, w, m, v):
    def body(ct_ref, dm_ref, w_ref, m_ref, v_ref, g_ref, d_ref, m2_ref, v2_ref):
        g = ct_ref[:, 0:1] * dm_ref[0:1, :]
        for b in range(1, NDEV):
            g = g + ct_ref[:, b:b + 1] * dm_ref[b:b + 1, :]
        g_ref[...] = g
        d_ref[...], m2_ref[...], v2_ref[...] = _adam(w_ref[...], g, m_ref[...], v_ref[...])

    vm = pl.BlockSpec(memory_space=VMEM)
    sd = jax.ShapeDtypeStruct(w.shape, F32)
    return pl.pallas_call(
        body, name="ada_grad_adam", out_shape=(sd, sd, sd, sd),
        in_specs=[vm] * 5, out_specs=(vm,) * 4, compiler_params=_cp(vmem_mb=32),
    )(ct, dm, w, m, v)


def _adam_f32(name, g, w, m, v, rows=None):
    R, C = w.shape
    rows = R if rows is None else rows

    def body(g_ref, w_ref, m_ref, v_ref, d_ref, m2_ref, v2_ref):
        d_ref[...], m2_ref[...], v2_ref[...] = _adam(w_ref[...], g_ref[...], m_ref[...], v_ref[...])

    tile = pl.BlockSpec((rows, C), lambda i: (i, 0))
    sd = jax.ShapeDtypeStruct(w.shape, F32)
    return pl.pallas_call(
        body, name=name, grid=(R // rows,), out_shape=(sd, sd, sd), in_specs=[tile] * 4, out_specs=(tile,) * 3,
        compiler_params=_cp(("parallel",), vmem_mb=32),
    )(g, w, m, v)


def _adam_w_in(gown, got, w, m, v, rows=256):
    R, C = w.shape

    def body(go_ref, got_ref, w_ref, m_ref, v_ref, g_ref, d_ref, m2_ref, v2_ref):
        g = go_ref[...]
        for q in range(3):
            g = g + got_ref[q].astype(F32)
        g_ref[...] = g
        d_ref[...], m2_ref[...], v2_ref[...] = _adam(w_ref[...], g, m_ref[...], v_ref[...])

    tile = pl.BlockSpec((rows, C), lambda i: (i, 0))
    sd = jax.ShapeDtypeStruct((R, C), F32)
    return pl.pallas_call(
        body, name="adam_w_in", grid=(R // rows,), out_shape=(sd, sd, sd, sd),
        in_specs=[tile, pl.BlockSpec((3, rows, C), lambda i: (0, i, 0)), tile, tile, tile], out_specs=(tile,) * 4,
        compiler_params=_cp(("parallel",), vmem_mb=32),
    )(gown, got, w, m, v)


def _adam_reduce(name, recv, w, m, v, rows, recv_spec):
    R, C = w.shape

    def body(r_ref, w_ref, m_ref, v_ref, g_ref, d_ref, m2_ref, v2_ref):
        g = r_ref[0].astype(F32)
        for s in range(1, NDEV):
            g = g + r_ref[s].astype(F32)
        g_ref[...] = g
        d_ref[...], m2_ref[...], v2_ref[...] = _adam(w_ref[...], g, m_ref[...], v_ref[...])

    tile = pl.BlockSpec((rows, C), lambda i: (i, 0))
    sd = jax.ShapeDtypeStruct((R, C), F32)
    return pl.pallas_call(
        body, name=name, grid=(R // rows,), out_shape=(sd, sd, sd, sd),
        in_specs=[recv_spec, tile, tile, tile], out_specs=(tile,) * 4,
        compiler_params=_cp(("parallel",)),
    )(recv, w, m, v)


def kernel(x, c, w_ada, b_ada, g_pre, w_in, conv_w, conv_b, conv_ln_g, conv_ln_b, w_conv_out, sgu_ln_g, sgu_ln_b, w_sgu, b_sgu, w_sgu_out, w_o, g_final, loss_target, m_w_ada, m_b_ada, m_g_pre, m_w_in, m_conv_w, m_conv_b, m_conv_ln_g, m_conv_ln_b, m_w_conv_out, m_sgu_ln_g, m_sgu_ln_b, m_w_sgu, m_b_sgu, m_w_sgu_out, m_w_o, m_g_final, v_w_ada, v_b_ada, v_g_pre, v_w_in, v_conv_w, v_conv_b, v_conv_ln_g, v_conv_ln_b, v_w_conv_out, v_sgu_ln_g, v_sgu_ln_b, v_w_sgu, v_b_sgu, v_w_sgu_out, v_w_o, v_g_final):
    T = x.shape[1]
    assert T % TMB == 0 and x.shape[2] == D
    xs = x[0]
    tgt = loss_target[0]
    my = 4 * lax.axis_index("x") + 2 * lax.axis_index("y") + lax.axis_index("c")

    mod, cg = _ada_fwd(c, w_ada[0], b_ada)
    gf = g_final.reshape(1, D)
    bst = b_sgu[0].T

    h, ht = _prep_h(xs, mod, g_pre)
    ag_order = jnp.stack([_slot(*b) for b in _ag_blocks(lax.axis_index("x"), lax.axis_index("y"), lax.axis_index("c"))])
    p, wing, wcog, cwg, own2, land2 = _in_proj(h, w_in[0], ag_order.astype(jnp.int32),
                                               w_conv_out[0], w_sgu_out[0], w_o[0], conv_w[0])
    wco = wcog.reshape(1, D, D)
    cw = jnp.broadcast_to(jnp.transpose(cwg, (1, 0, 2)).reshape(KW, 1, D), (KW, 8, D))
    wq_send, wq_recv, own2_thru, land2_thru, wq_token = _wq_start(own2, land2)
    yc, ya = _branch_a_fwd(p, cw, conv_b, conv_ln_g, conv_ln_b, wco, wq_token)
    wq2 = _wq_wait(wq_send, wq_recv, own2_thru, land2_thru, yc).reshape(2, D, D)
    yb = _branch_b_fwd(p, w_sgu[0], bst, sgu_ln_g, sgu_ln_b, wq2)
    dx2, dya, dyb, dp, gw_o, st_m = _merge_loss(p, ya, yb, xs, tgt, mod, gf, wq2)

    dp, gw_so, gws, gbs, st_b = _branch_b_bwd(dp, dyb, p, w_sgu[0], bst, sgu_ln_g, sgu_ln_b, wq2)
    dp, dyc, gw_co, st_a = _branch_a_bwd1(dp, dya, yc, p, conv_ln_g, conv_ln_b, wco)
    early = _pack_early(st_a, st_b, st_m, gws, gbs)
    dp, gcw, rq, eg = _branch_a_bwd2(dp, dyc, p, cw, gw_co, gw_so, gw_o, early)
    csum, gown = _in_bwd_dw(ht, dp, my ^ jnp.array(RS_ORDER, jnp.int32))
    rs_send, rs_recv, csum_thru, land_thru, token = _rs_start(csum)
    grad_x, st_x = _in_bwd_dx(dp, wing, xs, dx2, mod, g_pre, token)
    got = _rs_wait(rs_send, rs_recv, csum_thru, land_thru, st_x)
    lg = _gather_late(_pack_late(st_x, gcw))

    def pack(name, b_ada_, g_pre_, conv_b_, lg_, lb_, sg_, sb_, b_sgu_, gfin_, w_sgu_):
        return _pack_params(name, b_ada_, g_pre_, conv_b_, lg_, lb_, sg_, sb_, b_sgu_.reshape(1, D), gfin_.reshape(1, D),
                            jnp.transpose(w_sgu_[0], (1, 0, 2)).reshape(CHUNK, D))

    pw = pack("pack_w", b_ada, g_pre, conv_b, conv_ln_g, conv_ln_b, sgu_ln_g, sgu_ln_b, b_sgu, g_final, w_sgu)
    pm = pack("pack_m", m_b_ada, m_g_pre, m_conv_b, m_conv_ln_g, m_conv_ln_b, m_sgu_ln_g, m_sgu_ln_b, m_b_sgu, m_g_final,
              m_w_sgu)
    pv = pack("pack_v", v_b_ada, v_g_pre, v_conv_b, v_conv_ln_g, v_conv_ln_b, v_sgu_ln_g, v_sgu_ln_b, v_b_sgu, v_g_final,
              v_w_sgu)
    loss11, sg_, sd_, sm_, sv_ = _small_finish(eg, lg, pw, pm, pv)

    def unpack(a):
        return dict(
            b_ada=a[0:3].reshape(1, 3 * D), g_pre=a[3:4], conv_b=a[4:5], conv_ln_g=a[5:6], conv_ln_b=a[6:7],
            sgu_ln_g=a[7:8], sgu_ln_b=a[8:9], b_sgu=a[9:10].reshape(1, HEADS, CHUNK), g_final=a[10],
            w_sgu=jnp.transpose(a[ROW_WS:ROW_WS + CHUNK].reshape(CHUNK, HEADS, CHUNK), (1, 0, 2))[None])
    small = [unpack(a) for a in (sg_, sd_, sm_, sv_)]

    g_cw = lax.dynamic_slice_in_dim(sg_[ROW_CW:ROW_CW + KW], my * 128, 128, axis=1)
    d_cw, m_cw, v_cw = _adam_f32("adam_conv_w", g_cw, conv_w[0], m_conv_w[0], v_conv_w[0])

    wcols = w_ada.shape[2]
    dm_all = jnp.concatenate([lg[:, 0], lg[:, 1], eg[:, 0]], axis=1)
    dm_mine = lax.dynamic_slice_in_dim(dm_all, my * wcols, wcols, axis=1)
    g_ada, d_ada, m_ada, v_ada = _ada_grad_adam(cg.T, dm_mine, w_ada[0], m_w_ada[0], v_w_ada[0])

    g_in, d_in, m_in, v_in = _adam_w_in(gown, got, w_in[0], m_w_in[0], v_w_in[0])
    big = {}
    for j, (nm, w_, m_, v_) in enumerate((("w_conv_out", w_conv_out, m_w_conv_out, v_w_conv_out),
                                          ("w_sgu_out", w_sgu_out, m_w_sgu_out, v_w_sgu_out),
                                          ("w_o", w_o, m_w_o, v_w_o))):
        big[nm] = _adam_reduce("adam_" + nm, rq, w_[0], m_[0], v_[0], 128,
                               pl.BlockSpec((NDEV, None, 128, D), lambda i, j=j: (0, j, 0, 0)))

    per = {
        "w_ada": tuple(a[None] for a in (g_ada, d_ada, m_ada, v_ada)),
        "w_in": tuple(a[None] for a in (g_in, d_in, m_in, v_in)),
        "conv_w": tuple(a[None] for a in (g_cw, d_cw, m_cw, v_cw)),
    }
    for nm in ("w_conv_out", "w_sgu_out", "w_o"):
        per[nm] = tuple(a[None] for a in big[nm])
    for nm in ("b_ada", "g_pre", "conv_b", "conv_ln_g", "conv_ln_b", "sgu_ln_g", "sgu_ln_b", "w_sgu", "b_sgu", "g_final"):
        per[nm] = tuple(s[nm] for s in small)

    order = ["w_ada", "b_ada", "g_pre", "w_in", "conv_w", "conv_b", "conv_ln_g", "conv_ln_b", "w_conv_out",
             "sgu_ln_g", "sgu_ln_b", "w_sgu", "b_sgu", "w_sgu_out", "w_o", "g_final"]
    outs = [loss11.reshape(()), grad_x[None]]
    for part in range(4):
        outs += [per[nm][part] for nm in order]
    return tuple(outs)
```

```python
import jax
import jax.numpy as jnp
from jax import lax
from jax.experimental import pallas as pl
from jax.experimental.pallas import tpu as pltpu

F32 = jnp.float32
BF16 = jnp.bfloat16
MESH = pl.DeviceIdType.MESH
VMEM = pltpu.VMEM
ANY = pl.ANY

D = 1024
NDEV = 8
NSEG = 8
HEADS = 8
HD = 128
CHUNK = 128
KW = 31
HALO = 32
EPS = 1e-6
TM = 256
TMB = 512
RB = 32
RBC = 32
TAPG = 4
EXT = TM + HALO
NSMALL = 176
NEARLY = 144
NLATE = 40
ROW_CW = 16
ROW_WS = 48
W_CO = 0
W_SO, W_O = 0, 1

ADAM_LR = 0.001
ADAM_B1 = 0.9
ADAM_B2 = 0.999
ADAM_EPS = 1e-08
ADAM_WD = 0.01
ADAM_STEP = 10

INV_SQRT2 = 0.7071067811865476
INV_SQRT_2PI = 0.3989422804014327

NT = (((1,), (1,)), ((), ()))
TN = (((0,), (0,)), ((), ()))


def _cp(sem=None, vmem_mb=48):
    return pltpu.CompilerParams(dimension_semantics=sem, vmem_limit_bytes=vmem_mb * 1024 * 1024)


def _me():
    return lax.axis_index("x"), lax.axis_index("y"), lax.axis_index("c")


def _slot(px, py, pc):
    return 4 * px + 2 * py + pc


def _xor_peer(k):
    x, y, c = _me()
    return (1 - x if k & 4 else x, 1 - y if k & 2 else y, 1 - c if k & 1 else c)


def _allgather_phases(srcs, dst_at, send_sems, recv_sems, loc_sems):
    x, y, c = _me()
    me = (x, y, c)
    sibling = (x, y, 1 - c)
    chips = [(1 - x, y), (x, 1 - y), (1 - x, 1 - y)]
    na = len(srcs)

    def copy(a, k, block, to, src=None):
        d = dst_at(a, _slot(*block))
        return pltpu.make_async_remote_copy(
            src_ref=d if src is None else src, dst_ref=d,
            send_sem=send_sems.at[a, k], recv_sem=recv_sems.at[a, k],
            device_id=to, device_id_type=MESH)

    local = [pltpu.make_async_copy(srcs[a], dst_at(a, _slot(*me)), loc_sems.at[a]) for a in range(na)]
    first = []
    for a in range(na):
        first.append(copy(a, 0, me, sibling, src=srcs[a]))
        for j, chip in enumerate(chips):
            first.append(copy(a, 1 + j, me, (*chip, c), src=srcs[a]))
    passed = [copy(a, 4 + j, (*chip, c), sibling) for j, chip in enumerate(chips) for a in range(na)]

    def start():
        for cp in local + first:
            cp.start()

    def relay():
        for j, chip in enumerate(chips):
            for a in range(na):
                copy(a, 1 + j, (*chip, c), me).wait_recv()
                passed[j * na + a].start()

    def finish():
        for a in range(na):
            copy(a, 0, sibling, me).wait_recv()
        for j, chip in enumerate(chips):
            for a in range(na):
                copy(a, 4 + j, (*chip, 1 - c), me).wait_recv()
        for cp in first + passed:
            cp.wait_send()
        for cp in local:
            cp.wait()

    return start, relay, finish


def _allgather(srcs, dst_at, send_sems, recv_sems, loc_sems):
    start, relay, finish = _allgather_phases(srcs, dst_at, send_sems, recv_sems, loc_sems)
    start()
    relay()
    finish()


def _owner_exchange(gq, rq_ref, x_send, x_recv, x_loc):
    mx, my_, mc = _me()
    me = _slot(mx, my_, mc)

    def rows_of(a, slot):
        return gq[a].at[pl.ds(pl.multiple_of(slot * 128, 128), 128), :]

    def exchange(k, a, recv):
        px, py, pc = _xor_peer(k)
        peer = _slot(px, py, pc)
        return pltpu.make_async_remote_copy(
            src_ref=rows_of(a, me if recv else peer), dst_ref=rq_ref.at[peer if recv else me, a],
            send_sem=x_send.at[a, k - 1], recv_sem=x_recv.at[a, k - 1],
            device_id=(px, py, pc), device_id_type=MESH)

    local = [pltpu.make_async_copy(rows_of(a, me), rq_ref.at[me, a], x_loc.at[a]) for a in range(len(gq))]

    def start():
        for cp in local:
            cp.start()
        for k in range(1, NDEV):
            for a in range(len(gq)):
                exchange(k, a, False).start()

    def wait():
        for k in range(1, NDEV):
            for a in range(len(gq)):
                exchange(k, a, True).wait_recv()
        for k in range(1, NDEV):
            for a in range(len(gq)):
                exchange(k, a, False).wait_send()
        for cp in local:
            cp.wait()

    return start, wait


def _sig(v):
    return jax.nn.sigmoid(v)


def _gelu_parts(v):
    cdf = 0.5 * (1.0 + lax.erf(v * INV_SQRT2))
    pdf = jnp.exp(-0.5 * v * v) * INV_SQRT_2PI
    return v * cdf, cdf + v * pdf


def _ln_stats(v):
    mu = jnp.mean(v, axis=-1, keepdims=True)
    xc = v - mu
    var = jnp.mean(xc * xc, axis=-1, keepdims=True)
    rs = lax.rsqrt(var + EPS)
    return xc * rs, rs


def _ln_bwd(dn, n, rs):
    return rs * (dn - jnp.mean(dn, axis=-1, keepdims=True) - n * jnp.mean(dn * n, axis=-1, keepdims=True))


def _fold8(v):
    acc = v[0:8]
    for r in range(8, v.shape[0], 8):
        acc = acc + v[r:r + 8]
    return acc


def _rows(rb, n=RB):
    return pl.ds(rb * n, n)


def _ld(ref, rows):
    return ref[rows, :].astype(F32)


def _blocks(n, body, init=0):
    carry = init
    for rb in range(n):
        carry = body(rb, carry)
    return carry


def _shift_copies(s_ref):
    n = EXT - 8
    for m in range(1, 8):
        for c0 in range(0, n, 56):
            s_ref[m, c0:c0 + 56, :] = s_ref[0, c0 + m:c0 + m + 56, :]


def _ada_exchange(c_ref, w_ref, b_ref, mod_ref, call_ref, cdst, msrc, mdst, c_send, c_recv, m_send, m_recv):
    x, y, c = _me()
    me = _slot(x, y, c)

    def all_to_all(src_of, dst, ss, rs):
        def cp(k, recv):
            px, py, pc = _xor_peer(k)
            peer = _slot(px, py, pc)
            return pltpu.make_async_remote_copy(
                src_ref=src_of(me if recv else peer), dst_ref=dst.at[peer if recv else me],
                send_sem=ss.at[k - 1], recv_sem=rs.at[k - 1], device_id=(px, py, pc), device_id_type=MESH)
        for k in range(1, NDEV):
            cp(k, False).start()
        for k in range(1, NDEV):
            cp(k, True).wait_recv()
        for k in range(1, NDEV):
            cp(k, False).wait_send()

    cdst[me] = c_ref[...]
    all_to_all(lambda s: c_ref, cdst, c_send, c_recv)
    for b in range(NDEV):
        call_ref[b:b + 1, :] = cdst[b]
    m = jnp.dot(call_ref[...], w_ref[...], preferred_element_type=F32, precision=lax.Precision.HIGHEST)
    for b in range(NDEV):
        msrc[b] = m[b:b + 1, :]
    mdst[me] = msrc[me]
    all_to_all(lambda s: msrc.at[s], mdst, m_send, m_recv)
    full = jnp.concatenate([mdst[k] for k in range(NDEV)], axis=1) + b_ref[...]
    for r in range(3):
        mod_ref[r:r + 1, :] = full[:, r * D:(r + 1) * D]


def _ada_fwd(c, w_ada, b_ada):
    wcols = w_ada.shape[1]

    def body(*refs):
        _ada_exchange(*refs)

    vm = pl.BlockSpec(memory_space=VMEM)
    return pl.pallas_call(
        body, name="ada_fwd",
        out_shape=(jax.ShapeDtypeStruct((3, D), F32), jax.ShapeDtypeStruct((NDEV, D), F32)),
        in_specs=[vm, vm, vm], out_specs=(vm, vm),
        scratch_shapes=[VMEM((NDEV, 1, D), F32), VMEM((NDEV, 1, wcols), F32), VMEM((NDEV, 1, wcols), F32),
                        pltpu.SemaphoreType.DMA((7,)), pltpu.SemaphoreType.DMA((7,)),
                        pltpu.SemaphoreType.DMA((7,)), pltpu.SemaphoreType.DMA((7,))],
        compiler_params=_cp(vmem_mb=32),
    )(c, w_ada, b_ada)


def _prep_h(x, mod, g_pre):
    T = x.shape[0]

    def body(x_ref, mod_ref, g_ref, h_ref, ht_ref):
        def blk(rb, carry):
            rows = _rows(rb)
            xv = x_ref[rows, :]
            r = lax.rsqrt(jnp.mean(xv * xv, axis=-1, keepdims=True) + EPS)
            h_ref[rows, :] = ((xv * r) * g_ref[...] * (1.0 + mod_ref[1:2, :]) + mod_ref[0:1, :]).astype(BF16)
            return carry
        _blocks(TMB // RB, blk)
        ht_ref[...] = h_ref[...].T

    return pl.pallas_call(
        body, name="prep_h", grid=(T // TMB,),
        out_shape=(jax.ShapeDtypeStruct((T, D), BF16), jax.ShapeDtypeStruct((D, T), BF16)),
        in_specs=[pl.BlockSpec((TMB, D), lambda i: (i, 0)), pl.BlockSpec((3, D), lambda i: (0, 0)),
                  pl.BlockSpec((1, D), lambda i: (0, 0))],
        out_specs=(pl.BlockSpec((TMB, D), lambda i: (i, 0)), pl.BlockSpec((D, TMB), lambda i: (0, i))),
        compiler_params=_cp(("parallel",)),
    )(x, mod, g_pre)


def _ag_blocks(x, y, c):
    a = 1 - c
    n1 = (x ^ a, y ^ c)
    n2 = (x ^ c, y ^ a)
    dg = (1 - x, 1 - y)
    return [(x, y, c), (x, y, a), (*n1, c), (*n2, c), (*n1, a), (*n2, a), (*dg, c), (*dg, a)]


def _in_proj(h, w_in, order, w_co, w_so, w_o, conv_w):
    T = h.shape[0]
    tmp = min(2 * TMB, T)
    nb = T // tmp

    def body(order_ref, h_ref, w_ref, wco_ref, wso_ref, wo_ref, cw_ref, p_ref, wing_ref, wqg_ref, cwg_ref, own2_ref, land2_ref,
             wbuf, wq_st, wq2_st, send_sems, recv_sems, out_sems, ag_send, ag_recv, ag_loc, own_sems):
        j = pl.program_id(0)
        i = pl.program_id(1)
        blocks = _ag_blocks(*_me())
        me, sibling, n1, n2, n1o, n2o, dg, dgo = blocks
        own2 = [pltpu.make_async_copy(wq2_st, own2_ref, own_sems.at[0]),
                pltpu.make_async_copy(wq2_st, land2_ref.at[:, _slot(*me)], own_sems.at[1])]
        ag_start, ag_relay, ag_finish = _allgather_phases(
            [wq_st, cw_ref], lambda a, slot: wqg_ref.at[:, slot] if a == 0 else cwg_ref.at[slot], ag_send, ag_recv, ag_loc)

        def copy(k, block, to):
            d = wbuf.at[_slot(*block)]
            return pltpu.make_async_remote_copy(src_ref=d, dst_ref=d, send_sem=send_sems.at[k], recv_sem=recv_sems.at[k],
                                                device_id=to, device_id_type=MESH)

        def writeback(jj):
            s = _slot(*blocks[jj])
            return pltpu.make_async_copy(wbuf.at[s], wing_ref.at[s], out_sems.at[jj])

        sends = [copy(0, me, sibling), copy(1, me, n1), copy(2, me, n2)]
        relay = copy(3, n1, n2)
        passed = [copy(4, n1, sibling), copy(5, n2, sibling), copy(6, dg, sibling)]
        arrivals = {1: [(0, sibling)], 2: [(1, n1), (2, n2)], 4: [(5, n1o)], 5: [(4, n2o)], 6: [(3, dg)], 7: [(6, dgo)]}

        @pl.when((j == 0) & (i == 0))
        def _():
            wbuf[_slot(*me)] = w_ref[...].astype(BF16)
            for cp in sends:
                cp.start()
            writeback(0).start()
            wq2_st[W_SO] = wso_ref[...].astype(BF16)
            wq2_st[W_O] = wo_ref[...].astype(BF16)
            for cp in own2:
                cp.start()

        for jj in range(1, NSEG):
            @pl.when((j == jj) & (i == 0))
            def _(jj=jj):
                for k, block in arrivals.get(jj, []):
                    copy(k, block, me).wait_recv()
                if jj == 2:
                    relay.start()
                    passed[0].start()
                    passed[1].start()
                    writeback(3).start()
                    wq_st[0] = wco_ref[...].astype(BF16)
                    ag_start()
                if jj == 6:
                    passed[2].start()
                if jj != 3:
                    writeback(jj).start()
                if jj == NSEG - 1:
                    ag_relay()

        p_ref[...] = jnp.dot(h_ref[...], wbuf[order_ref[j]], preferred_element_type=F32).astype(BF16)

        @pl.when((j == NSEG - 1) & (i == nb - 1))
        def _():
            for cp in sends + [relay] + passed:
                cp.wait_send()
            for jj in range(NSEG):
                writeback(jj).wait()
            for cp in own2:
                cp.wait()
            ag_finish()

    vm = pl.BlockSpec(memory_space=VMEM)
    hbm = pl.BlockSpec(memory_space=ANY)
    return pl.pallas_call(
        body, name="in_proj",
        out_shape=(jax.ShapeDtypeStruct((T, NSEG * D), BF16), jax.ShapeDtypeStruct((NSEG, D, D), BF16),
                   jax.ShapeDtypeStruct((1, NDEV, 128, D), BF16), jax.ShapeDtypeStruct((NDEV, KW, 128), F32),
                   jax.ShapeDtypeStruct((2, 128, D), BF16), jax.ShapeDtypeStruct((2, NDEV, 128, D), BF16)),
        grid_spec=pltpu.PrefetchScalarGridSpec(
            num_scalar_prefetch=1, grid=(NSEG, nb),
            in_specs=[pl.BlockSpec((tmp, D), lambda j, i, o: (i, 0)), vm, vm, vm, vm, vm],
            out_specs=(pl.BlockSpec((tmp, D), lambda j, i, o: (i, o[j])), hbm, hbm, hbm, hbm, hbm),
            scratch_shapes=[VMEM((NSEG, D, D), BF16), VMEM((1, 128, D), BF16), VMEM((2, 128, D), BF16),
                            pltpu.SemaphoreType.DMA((7,)), pltpu.SemaphoreType.DMA((7,)), pltpu.SemaphoreType.DMA((NSEG,)),
                            pltpu.SemaphoreType.DMA((2, 7)), pltpu.SemaphoreType.DMA((2, 7)), pltpu.SemaphoreType.DMA((2,)),
                            pltpu.SemaphoreType.DMA((2,))]),
        compiler_params=_cp(("arbitrary", "arbitrary")),
    )(order, h, w_in, w_co, w_so, w_o, conv_w)


def _wq_start(own2, land2):
    def body(src_ref, land_ref, send_sems, recv_sems, src_thru, land_thru, token):
        barrier = pltpu.get_barrier_semaphore()
        for k in range(1, NDEV):
            pl.semaphore_signal(barrier, inc=1, device_id=_xor_peer(k), device_id_type=MESH)
        pl.semaphore_wait(barrier, NDEV - 1)
        x, y, c = _me()
        for k in range(1, NDEV):
            pltpu.make_async_remote_copy(src_ref=src_ref, dst_ref=land_ref.at[:, _slot(x, y, c)], send_sem=send_sems.at[k - 1],
                                         recv_sem=recv_sems.at[k - 1], device_id=_xor_peer(k), device_id_type=MESH).start()
        token[...] = jnp.zeros_like(token)

    hbm = pl.BlockSpec(memory_space=pltpu.HBM)
    sem = pl.BlockSpec(memory_space=pltpu.SEMAPHORE)
    return pl.pallas_call(
        body, name="wq_start",
        out_shape=(pltpu.SemaphoreType.DMA((NDEV - 1,)), pltpu.SemaphoreType.DMA((NDEV - 1,)),
                   pltpu.HBM(own2.shape, own2.dtype), pltpu.HBM(land2.shape, land2.dtype), jax.ShapeDtypeStruct((8, 128), F32)),
        in_specs=(hbm, hbm), out_specs=(sem, sem, hbm, hbm, pl.BlockSpec(memory_space=VMEM)),
        input_output_aliases={0: 2, 1: 3},
        compiler_params=pltpu.CompilerParams(has_side_effects=pltpu.SideEffectType.DATAFLOW_SIDE_EFFECTING, collective_id=2),
    )(pltpu.with_memory_space_constraint(own2, pltpu.HBM), pltpu.with_memory_space_constraint(land2, pltpu.HBM))


def _wq_wait(send_sems, recv_sems, src_thru, land_thru, after):
    def body(src_ref, land_ref, send_sems, recv_sems, after_ref, src_dead, got_ref):
        del after_ref, src_dead, got_ref
        for k in range(1, NDEV):
            px, py, pc = _xor_peer(k)
            cp = pltpu.make_async_remote_copy(src_ref=src_ref, dst_ref=land_ref.at[:, _slot(px, py, pc)],
                                              send_sem=send_sems.at[k - 1], recv_sem=recv_sems.at[k - 1],
                                              device_id=(px, py, pc), device_id_type=MESH)
            cp.wait_send()
            cp.wait_recv()

    hbm = pl.BlockSpec(memory_space=pltpu.HBM)
    sem = pl.BlockSpec(memory_space=pltpu.SEMAPHORE)
    return pl.pallas_call(
        body, name="wq_wait",
        out_shape=(pltpu.HBM(src_thru.shape, src_thru.dtype), pltpu.HBM(land_thru.shape, land_thru.dtype)),
        in_specs=(hbm, hbm, sem, sem, pl.BlockSpec(memory_space=ANY)), out_specs=(hbm, hbm),
        input_output_aliases={0: 0, 1: 1},
        compiler_params=pltpu.CompilerParams(has_side_effects=pltpu.SideEffectType.DATAFLOW_SIDE_EFFECTING),
    )(src_thru, land_thru, send_sems, recv_sems, after)[1]


def _fill_a_ext(i, s_ref, val_ref, glu_ref, hval_ref, hglu_ref):
    ah = hval_ref[...].astype(F32) * _sig(hglu_ref[...].astype(F32))
    s_ref[0, 0:HALO, :] = jnp.where(i > 0, ah, 0.0)

    def fill(rb, carry):
        rows = _rows(rb)
        s_ref[0, pl.ds(HALO + rb * RB, RB), :] = _ld(val_ref, rows) * _sig(_ld(glu_ref, rows))
        return carry
    _blocks(TM // RB, fill)


def _halo_prev(seg):
    hb = TM // HALO
    return pl.BlockSpec((HALO, D), lambda i: (jnp.maximum(i * hb - 1, 0), seg))


def _branch_a_fwd(p, cw, conv_b, lg, lb, wq, token):
    T = p.shape[0]

    def body(val_ref, glu_ref, z_ref, hval_ref, hglu_ref, cw_ref, cb_ref, lg_ref, lb_ref, w_ref, token_ref,
             yc_ref, ya_ref, s_ref, yp_ref):
        del token_ref
        i = pl.program_id(0)
        _fill_a_ext(i, s_ref, val_ref, glu_ref, hval_ref, hglu_ref)
        _shift_copies(s_ref)

        def conv(rb, carry):
            r0 = rb * RBC
            accs = [jnp.broadcast_to(cb_ref[...], (8, D))] * (RBC // 8)
            for k in range(KW):
                o = 2 + k
                w = cw_ref[k]
                accs = [acc + w * s_ref[o % 8, pl.ds(r0 + 8 * (o // 8 + g), 8), :]
                        for g, acc in enumerate(accs)]
            for g, acc in enumerate(accs):
                yc_ref[pl.ds(r0 + 8 * g, 8), :] = acc
            return carry
        for rb in range(TM // RBC):
            conv(rb, 0)

        def post(rb, carry):
            rows = _rows(rb)
            n, _ = _ln_stats(yc_ref[rows, :])
            l = n * lg_ref[...] + lb_ref[...]
            z = _ld(z_ref, rows)
            yp_ref[rows, :] = ((l * _sig(l)) * (z * _sig(z))).astype(BF16)
            return carry
        _blocks(TM // RB, post)
        ya_ref[...] = jnp.dot(yp_ref[...], w_ref[...], preferred_element_type=F32)

    tile = lambda seg: pl.BlockSpec((TM, D), lambda i: (i, seg))
    row = pl.BlockSpec((1, D), lambda i: (0, 0))
    return pl.pallas_call(
        body, name="branch_a_fwd", grid=(T // TM,),
        out_shape=(jax.ShapeDtypeStruct((T, D), F32), jax.ShapeDtypeStruct((T, D), F32)),
        in_specs=[tile(0), tile(1), tile(2), _halo_prev(0), _halo_prev(1),
                  pl.BlockSpec((KW, 8, D), lambda i: (0, 0, 0)), row, row, row,
                  pl.BlockSpec((None, D, D), lambda i: (W_CO, 0, 0)), pl.BlockSpec((8, 128), lambda i: (0, 0))],
        out_specs=(pl.BlockSpec((TM, D), lambda i: (i, 0)), pl.BlockSpec((TM, D), lambda i: (i, 0))),
        scratch_shapes=[VMEM((8, EXT, D), F32), VMEM((TM, D), BF16)],
        compiler_params=_cp(("parallel",)),
    )(p, p, p, p, p, cw, conv_b, lg, lb, wq, token)


def _masked_ws(ws_ref, wt_ref):
    tri = lax.broadcasted_iota(jnp.int32, (CHUNK, CHUNK), 0) >= lax.broadcasted_iota(jnp.int32, (CHUNK, CHUNK), 1)
    for h in range(HEADS):
        wt_ref[h] = jnp.where(tri, ws_ref[h], 0.0).astype(BF16)


def _sgu_mix(wt_ref, vl_ref, bst_ref, s_ref):
    for h in range(HEADS):
        cs = slice(h * HD, (h + 1) * HD)
        s = jnp.dot(wt_ref[h], _chunks_on_lanes(vl_ref, cs), preferred_element_type=F32) + bst_ref[:, h:h + 1]
        for ck in range(TM // CHUNK):
            s_ref[ck * CHUNK:(ck + 1) * CHUNK, cs] = s[:, ck * HD:(ck + 1) * HD]


def _chunks_on_lanes(ref, cs):
    return jnp.concatenate([ref[ck * CHUNK:(ck + 1) * CHUNK, cs] for ck in range(TM // CHUNK)], axis=1)


def _branch_b_fwd(p, ws, bst, sg, sb, wq):
    T = p.shape[0]

    def body(pu_ref, pv_ref, pz_ref, ws_ref, bst_ref, sg_ref, sb_ref, w_ref, yb_ref,
             wt_ref, vl_ref, t_ref, s_ref, yp_ref):
        _masked_ws(ws_ref, wt_ref)

        def pre(rb, carry):
            rows = _rows(rb)
            vg, _ = _gelu_parts(_ld(pv_ref, rows))
            vn, _ = _ln_stats(vg)
            vl_ref[rows, :] = (vn * sg_ref[...] + sb_ref[...]).astype(BF16)
            u, _ = _gelu_parts(_ld(pu_ref, rows))
            z = _ld(pz_ref, rows)
            t_ref[rows, :] = u * (z * _sig(z))
            return carry
        _blocks(TM // RB, pre)
        _sgu_mix(wt_ref, vl_ref, bst_ref, s_ref)

        def post(rb, carry):
            rows = _rows(rb)
            yp_ref[rows, :] = (t_ref[rows, :] * s_ref[rows, :]).astype(BF16)
            return carry
        _blocks(TM // RB, post)
        yb_ref[...] = jnp.dot(yp_ref[...], w_ref[...], preferred_element_type=F32)

    tile = lambda seg: pl.BlockSpec((TM, D), lambda i: (i, seg))
    row = pl.BlockSpec((1, D), lambda i: (0, 0))
    return pl.pallas_call(
        body, name="branch_b_fwd", grid=(T // TM,),
        out_shape=jax.ShapeDtypeStruct((T, D), F32),
        in_specs=[tile(3), tile(4), tile(5), pl.BlockSpec((HEADS, CHUNK, CHUNK), lambda i: (0, 0, 0)),
                  pl.BlockSpec((CHUNK, HEADS), lambda i: (0, 0)), row, row,
                  pl.BlockSpec((None, D, D), lambda i: (W_SO, 0, 0))],
        out_specs=pl.BlockSpec((TM, D), lambda i: (i, 0)),
        scratch_shapes=[VMEM((HEADS, CHUNK, CHUNK), BF16), VMEM((TM, D), BF16), VMEM((TM, D), F32),
                        VMEM((TM, D), F32), VMEM((TM, D), BF16)],
        compiler_params=_cp(("parallel",)),
    )(p, p, p, ws, bst, sg, sb, wq)


def _merge_loss(p, ya, yb, x, tgt, mod, g_final, wq):
    T = x.shape[0]
    nt = T // TM

    def body(ga_ref, gb_ref, ya_ref, yb_ref, x_ref, t_ref, mod_ref, gf_ref, w_ref,
             dx2_ref, dya_ref, dyb_ref, dp_ref, gwo_ref, st_ref,
             mrg_s, out_s, dout_s, dm_s, gw_acc):
        i = pl.program_id(0)

        @pl.when(i == 0)
        def _():
            gw_acc[...] = jnp.zeros_like(gw_acc)
            st_ref[...] = jnp.zeros_like(st_ref)

        def merge(rb, carry):
            rows = _rows(rb)
            mrg_s[rows, :] = (_sig(_ld(ga_ref, rows)) * ya_ref[rows, :] + _sig(_ld(gb_ref, rows)) * yb_ref[rows, :]).astype(BF16)
            return carry
        _blocks(TM // RB, merge)
        out_s[...] = jnp.dot(mrg_s[...], w_ref[...], preferred_element_type=F32)

        def head(rb, carry):
            loss, gg, dg = carry
            rows = _rows(rb)
            gate = mod_ref[2:3, :]
            gf = gf_ref[...]
            out = out_s[rows, :]
            x2 = x_ref[rows, :] + gate * out
            r2 = lax.rsqrt(jnp.mean(x2 * x2, axis=-1, keepdims=True) + EPS)
            x2n = x2 * r2
            diff = x2n * gf - t_ref[rows, :]
            dy = diff * (1.0 / D)
            dx2n = dy * gf
            dx2 = r2 * (dx2n - x2n * jnp.mean(dx2n * x2n, axis=-1, keepdims=True))
            dx2_ref[rows, :] = dx2
            dout_s[rows, :] = (dx2 * gate).astype(BF16)
            return loss + _fold8(diff * diff), gg + _fold8(dy * x2n), dg + _fold8(dx2 * out)
        zero = jnp.zeros((8, D), F32)
        loss, gg, dg = _blocks(TM // RB, head, (zero, zero, zero))
        st_ref[0] = st_ref[0] + loss * (0.5 / D)
        st_ref[1] = st_ref[1] + gg
        st_ref[2] = st_ref[2] + dg

        dm_s[...] = lax.dot_general(dout_s[...], w_ref[...], NT, preferred_element_type=F32)
        gw_acc[...] += lax.dot_general(mrg_s[...], dout_s[...], TN, preferred_element_type=F32)

        def split(rb, carry):
            rows = _rows(rb)
            dm = dm_s[rows, :]
            sa = _sig(_ld(ga_ref, rows))
            sb = _sig(_ld(gb_ref, rows))
            dya_ref[rows, :] = (dm * sa).astype(BF16)
            dyb_ref[rows, :] = (dm * sb).astype(BF16)
            dp_ref[0, rows, :] = (dm * ya_ref[rows, :] * (sa * (1.0 - sa))).astype(BF16)
            dp_ref[1, rows, :] = (dm * yb_ref[rows, :] * (sb * (1.0 - sb))).astype(BF16)
            return carry
        _blocks(TM // RB, split)

        @pl.when(i == nt - 1)
        def _():
            gwo_ref[...] = gw_acc[...].astype(BF16)

    tile = pl.BlockSpec((TM, D), lambda i: (i, 0))
    return pl.pallas_call(
        body, name="merge_loss", grid=(nt,),
        out_shape=(jax.ShapeDtypeStruct((T, D), F32), jax.ShapeDtypeStruct((T, D), BF16), jax.ShapeDtypeStruct((T, D), BF16),
                   jax.ShapeDtypeStruct((NSEG, T, D), BF16), jax.ShapeDtypeStruct((D, D), BF16),
                   jax.ShapeDtypeStruct((3, 8, D), F32)),
        in_specs=[pl.BlockSpec((TM, D), lambda i: (i, 6)), pl.BlockSpec((TM, D), lambda i: (i, 7)), tile, tile, tile, tile,
                  pl.BlockSpec((3, D), lambda i: (0, 0)), pl.BlockSpec((1, D), lambda i: (0, 0)),
                  pl.BlockSpec((None, D, D), lambda i: (W_O, 0, 0))],
        out_specs=(tile, tile, tile, pl.BlockSpec((2, TM, D), lambda i: (3, i, 0)),
                   pl.BlockSpec((D, D), lambda i: (0, 0)), pl.BlockSpec((3, 8, D), lambda i: (0, 0, 0))),
        scratch_shapes=[VMEM((TM, D), BF16), VMEM((TM, D), F32), VMEM((TM, D), BF16), VMEM((TM, D), F32), VMEM((D, D), F32)],
        compiler_params=_cp(("arbitrary",)),
    )(p, p, ya, yb, x, tgt, mod, g_final, wq)


def _branch_b_bwd(dp, dyb, p, ws, bst, sg, sb, wq):
    T = p.shape[0]
    nt = T // TM

    def body(dp_in, dyb_ref, pu_ref, pv_ref, pz_ref, ws_ref, bst_ref, sg_ref, sb_ref, w_ref,
             dp_ref, gw_ref, gws_ref, gbs_ref, st_ref,
             wt_ref, d_s, vl_s, vn_s, gpv_s, rs_s, s_s, ds_s, ds32_s, yp_s, dvl_s, gw_acc):
        del dp_in
        i = pl.program_id(0)

        @pl.when(i == 0)
        def _():
            gw_acc[...] = jnp.zeros_like(gw_acc)
            gws_ref[...] = jnp.zeros_like(gws_ref)
            gbs_ref[...] = jnp.zeros_like(gbs_ref)
            st_ref[...] = jnp.zeros_like(st_ref)
            _masked_ws(ws_ref, wt_ref)

        d_s[...] = lax.dot_general(dyb_ref[...], w_ref[...], NT, preferred_element_type=F32)

        def pre(rb, carry):
            rows = _rows(rb)
            vg, gpv = _gelu_parts(_ld(pv_ref, rows))
            vn, rs = _ln_stats(vg)
            vl_s[rows, :] = (vn * sg_ref[...] + sb_ref[...]).astype(BF16)
            vn_s[rows, :] = vn
            gpv_s[rows, :] = gpv
            rs_s[rows, :] = rs
            return carry
        _blocks(TM // RB, pre)
        _sgu_mix(wt_ref, vl_s, bst_ref, s_s)

        def mid(rb, carry):
            rows = _rows(rb)
            u, gpu = _gelu_parts(_ld(pu_ref, rows))
            z = _ld(pz_ref, rows)
            sz = _sig(z)
            siluz = z * sz
            d = d_s[rows, :]
            s = s_s[rows, :]
            t = u * siluz
            yp_s[rows, :] = (t * s).astype(BF16)
            dp_ref[0, rows, :] = (d * s * siluz * gpu).astype(BF16)
            ds = d * t
            ds32_s[rows, :] = ds
            ds_s[rows, :] = ds.astype(BF16)
            dp_ref[2, rows, :] = (d * u * s * (sz * (1.0 + z * (1.0 - sz)))).astype(BF16)
            return carry
        _blocks(TM // RB, mid)

        for h in range(HEADS):
            cs = slice(h * HD, (h + 1) * HD)
            dsh = _chunks_on_lanes(ds_s, cs)
            dvl = lax.dot_general(wt_ref[h], dsh, TN, preferred_element_type=F32)
            gbs = gbs_ref[h]
            for ck in range(TM // CHUNK):
                r = slice(ck * CHUNK, (ck + 1) * CHUNK)
                dvl_s[r, cs] = dvl[:, ck * HD:(ck + 1) * HD]
                gbs = gbs + ds32_s[r, cs]
            gbs_ref[h] = gbs
            gws_ref[h] += lax.dot_general(dsh, _chunks_on_lanes(vl_s, cs), NT, preferred_element_type=F32)

        def post(rb, carry):
            g_sg, g_sb = carry
            rows = _rows(rb)
            dvl = dvl_s[rows, :]
            vn = vn_s[rows, :]
            dvg = _ln_bwd(dvl * sg_ref[...], vn, rs_s[rows, :])
            dp_ref[1, rows, :] = (dvg * gpv_s[rows, :]).astype(BF16)
            return g_sg + _fold8(dvl * vn), g_sb + _fold8(dvl)
        zero = jnp.zeros((8, D), F32)
        g_sg, g_sb = _blocks(TM // RB, post, (zero, zero))
        st_ref[0] = st_ref[0] + g_sg
        st_ref[1] = st_ref[1] + g_sb

        gw_acc[...] += lax.dot_general(yp_s[...], dyb_ref[...], TN, preferred_element_type=F32)

        @pl.when(i == nt - 1)
        def _():
            gw_ref[...] = gw_acc[...].astype(BF16)

    tile = lambda seg: pl.BlockSpec((TM, D), lambda i: (i, seg))
    row = pl.BlockSpec((1, D), lambda i: (0, 0))
    hh = pl.BlockSpec((HEADS, CHUNK, CHUNK), lambda i: (0, 0, 0))
    return pl.pallas_call(
        body, name="branch_b_bwd", grid=(nt,),
        out_shape=(jax.ShapeDtypeStruct((NSEG, T, D), BF16), jax.ShapeDtypeStruct((D, D), BF16),
                   jax.ShapeDtypeStruct((HEADS, CHUNK, CHUNK), F32), jax.ShapeDtypeStruct((HEADS, CHUNK, HD), F32),
                   jax.ShapeDtypeStruct((2, 8, D), F32)),
        in_specs=[pl.BlockSpec(memory_space=ANY), pl.BlockSpec((TM, D), lambda i: (i, 0)), tile(3), tile(4), tile(5),
                  hh, pl.BlockSpec((CHUNK, HEADS), lambda i: (0, 0)), row, row,
                  pl.BlockSpec((None, D, D), lambda i: (W_SO, 0, 0))],
        out_specs=(pl.BlockSpec((3, TM, D), lambda i: (1, i, 0)), pl.BlockSpec((D, D), lambda i: (0, 0)), hh, hh,
                   pl.BlockSpec((2, 8, D), lambda i: (0, 0, 0))),
        scratch_shapes=[VMEM((HEADS, CHUNK, CHUNK), BF16), VMEM((TM, D), F32), VMEM((TM, D), BF16), VMEM((TM, D), F32),
                        VMEM((TM, D), F32), VMEM((TM, 1), F32), VMEM((TM, D), F32), VMEM((TM, D), BF16), VMEM((TM, D), F32),
                        VMEM((TM, D), BF16), VMEM((TM, D), F32), VMEM((D, D), F32)],
        input_output_aliases={0: 0},
        compiler_params=_cp(("arbitrary",)),
    )(dp, dyb, p, p, p, ws, bst, sg, sb, wq)


def _branch_a_bwd1(dp, dya, yc, p, lg, lb, wq):
    T = p.shape[0]
    nt = T // TM

    def body(dp_in, dya_ref, yc_ref, z_ref, lg_ref, lb_ref, w_ref, dp_ref, dyc_ref, gw_ref, st_ref,
             d_s, yp_s, gw_acc):
        del dp_in
        i = pl.program_id(0)

        @pl.when(i == 0)
        def _():
            gw_acc[...] = jnp.zeros_like(gw_acc)
            st_ref[...] = jnp.zeros_like(st_ref)

        d_s[...] = lax.dot_general(dya_ref[...], w_ref[...], NT, preferred_element_type=F32)

        def blk(rb, carry):
            g_lg, g_lb, g_cb = carry
            rows = _rows(rb)
            n, rs = _ln_stats(yc_ref[rows, :])
            l = n * lg_ref[...] + lb_ref[...]
            sgl = _sig(l)
            sl = l * sgl
            z = _ld(z_ref, rows)
            sz = _sig(z)
            siluz = z * sz
            d = d_s[rows, :]
            yp_s[rows, :] = (sl * siluz).astype(BF16)
            dp_ref[rows, :] = (d * sl * (sz * (1.0 + z * (1.0 - sz)))).astype(BF16)
            dl = (d * siluz) * (sgl * (1.0 + l * (1.0 - sgl)))
            dyc = _ln_bwd(dl * lg_ref[...], n, rs)
            dyc_ref[rows, :] = dyc
            return g_lg + _fold8(dl * n), g_lb + _fold8(dl), g_cb + _fold8(dyc)
        zero = jnp.zeros((8, D), F32)
        g_lg, g_lb, g_cb = _blocks(TM // RB, blk, (zero, zero, zero))
        st_ref[0] = st_ref[0] + g_lg
        st_ref[1] = st_ref[1] + g_lb
        st_ref[2] = st_ref[2] + g_cb

        gw_acc[...] += lax.dot_general(yp_s[...], dya_ref[...], TN, preferred_element_type=F32)

        @pl.when(i == nt - 1)
        def _():
            gw_ref[...] = gw_acc[...].astype(BF16)

    tile = pl.BlockSpec((TM, D), lambda i: (i, 0))
    row = pl.BlockSpec((1, D), lambda i: (0, 0))
    return pl.pallas_call(
        body, name="branch_a_bwd1", grid=(nt,),
        out_shape=(jax.ShapeDtypeStruct((NSEG, T, D), BF16), jax.ShapeDtypeStruct((T, D), F32),
                   jax.ShapeDtypeStruct((D, D), BF16), jax.ShapeDtypeStruct((3, 8, D), F32)),
        in_specs=[pl.BlockSpec(memory_space=ANY), tile, tile, pl.BlockSpec((TM, D), lambda i: (i, 2)), row, row,
                  pl.BlockSpec((None, D, D), lambda i: (0, 0, 0))],
        out_specs=(pl.BlockSpec((None, TM, D), lambda i: (2, i, 0)), tile, pl.BlockSpec((D, D), lambda i: (0, 0)),
                   pl.BlockSpec((3, 8, D), lambda i: (0, 0, 0))),
        scratch_shapes=[VMEM((TM, D), F32), VMEM((TM, D), BF16), VMEM((D, D), F32)],
        input_output_aliases={0: 0},
        compiler_params=_cp(("arbitrary",)),
    )(dp, dya, yc, p, lg, lb, wq)


def _branch_a_bwd2(dp, dyc, p, cw, gw_co, gw_so, gw_o, early):
    T = p.shape[0]
    nt = T // TM
    hb = TM // HALO

    def body(dp_in, dyc_ref, hdyc_ref, val_ref, glu_ref, cw_ref, gco_ref, gso_ref, go_ref, e_ref,
             dp_ref, gcw_ref, rq_ref, eg_ref, a_s, sd_ref, da_s, x_send, x_recv, x_loc, ag_send, ag_recv, ag_loc):
        del dp_in
        i = pl.program_id(0)
        start_exchange, wait_exchange = _owner_exchange((gco_ref, gso_ref, go_ref), rq_ref, x_send, x_recv, x_loc)
        ag_start, ag_relay, ag_finish = _allgather_phases([e_ref], lambda a, slot: eg_ref.at[slot],
                                                          ag_send, ag_recv, ag_loc)

        @pl.when(i == 0)
        def _():
            gcw_ref[...] = jnp.zeros_like(gcw_ref)
            ag_start()
            start_exchange()

        @pl.when(i == nt // 2)
        def _():
            ag_relay()

        def fill(rb, carry):
            rows = _rows(rb)
            a_s[rows, :] = _ld(val_ref, rows) * _sig(_ld(glu_ref, rows))
            return carry
        _blocks(TM // RB, fill)
        sd_ref[0, 0:TM, :] = dyc_ref[...]
        sd_ref[0, TM:EXT, :] = jnp.where(i < nt - 1, hdyc_ref[...], 0.0)
        _shift_copies(sd_ref)

        def conv_t(rb, carry):
            r0 = rb * RBC
            accs = [jnp.zeros((8, D), F32)] * (RBC // 8)
            for k in range(KW):
                o = KW - 1 - k
                w = cw_ref[k]
                accs = [acc + w * sd_ref[o % 8, pl.ds(r0 + 8 * (o // 8 + g), 8), :]
                        for g, acc in enumerate(accs)]
            for g, acc in enumerate(accs):
                da_s[pl.ds(r0 + 8 * g, 8), :] = acc
            return carry
        for rb in range(TM // RBC):
            conv_t(rb, 0)

        for k0 in range(0, KW, TAPG):
            taps = list(range(k0, min(k0 + TAPG, KW)))

            def tap_group(rb, accs, taps=taps):
                for u in range(2):
                    r0 = rb * 32 + 16 * u
                    a = a_s[pl.ds(r0, 16), :]
                    out = []
                    for k, acc in zip(taps, accs):
                        o = KW - 1 - k
                        d = sd_ref[o % 8, pl.ds(r0 + 8 * (o // 8), 16), :]
                        out.append(acc + _fold8(a * d))
                    accs = tuple(out)
                return accs
            sums = tuple(jnp.zeros((8, D), F32) for _ in taps)
            for rb in range(TM // 32):
                sums = tap_group(rb, sums)
            for k, s in zip(taps, sums):
                gcw_ref[k] = gcw_ref[k] + s

        def glu_b(rb, carry):
            rows = _rows(rb)
            da = da_s[rows, :]
            sg = _sig(_ld(glu_ref, rows))
            dp_ref[0, rows, :] = (da * sg).astype(BF16)
            dp_ref[1, rows, :] = (da * _ld(val_ref, rows) * (sg * (1.0 - sg))).astype(BF16)
            return carry
        _blocks(TM // RB, glu_b)

        @pl.when(i == nt - 1)
        def _():
            wait_exchange()
            ag_finish()

    tile = lambda seg: pl.BlockSpec((TM, D), lambda i: (i, seg))
    hbm = pl.BlockSpec(memory_space=ANY)
    return pl.pallas_call(
        body, name="branch_a_bwd2", grid=(nt,),
        out_shape=(jax.ShapeDtypeStruct((NSEG, T, D), BF16), jax.ShapeDtypeStruct((32, 8, D), F32),
                   jax.ShapeDtypeStruct((NDEV, 3, 128, D), BF16), jax.ShapeDtypeStruct((NDEV, NEARLY, D), F32)),
        in_specs=[hbm, pl.BlockSpec((TM, D), lambda i: (i, 0)),
                  pl.BlockSpec((HALO, D), lambda i: (jnp.minimum((i + 1) * hb, nt * hb - 1), 0)),
                  tile(0), tile(1), pl.BlockSpec((KW, 8, D), lambda i: (0, 0, 0)),
                  hbm, hbm, hbm, hbm],
        out_specs=(pl.BlockSpec((2, TM, D), lambda i: (0, i, 0)), pl.BlockSpec((32, 8, D), lambda i: (0, 0, 0)), hbm, hbm),
        scratch_shapes=[VMEM((TM, D), F32), VMEM((8, EXT, D), F32), VMEM((TM, D), F32),
                        pltpu.SemaphoreType.DMA((3, 7)), pltpu.SemaphoreType.DMA((3, 7)), pltpu.SemaphoreType.DMA((3,)),
                        pltpu.SemaphoreType.DMA((1, 7)), pltpu.SemaphoreType.DMA((1, 7)), pltpu.SemaphoreType.DMA((1,))],
        input_output_aliases={0: 0},
        compiler_params=_cp(("arbitrary",)),
    )(dp, dyc, dyc, p, p, cw, gw_co, gw_so, gw_o, early)


def _in_bwd_dx(dp, wing, x, dx2, mod, g_pre, token):
    T = x.shape[0]
    nt = T // TM

    def body(dp_ref, w_ref, x_ref, dx2_ref, mod_ref, g_ref, token_ref, gx_ref, st_ref, dh_s):
        del token_ref
        i = pl.program_id(0)

        @pl.when(i == 0)
        def _():
            st_ref[...] = jnp.zeros_like(st_ref)

        dh_s[...] = lax.dot_general(dp_ref[0], w_ref[0], NT, preferred_element_type=F32)
        for j in range(1, NSEG):
            dh_s[...] += lax.dot_general(dp_ref[j], w_ref[j], NT, preferred_element_type=F32)

        def blk(rb, carry):
            d_sh, d_sc, g_g = carry
            rows = _rows(rb)
            xv = x_ref[rows, :]
            r = lax.rsqrt(jnp.mean(xv * xv, axis=-1, keepdims=True) + EPS)
            xn = xv * r
            g = g_ref[...]
            hpre = xn * g
            dh = dh_s[rows, :]
            dhp = dh * (1.0 + mod_ref[1:2, :])
            dxn = dhp * g
            gx_ref[rows, :] = dx2_ref[rows, :] + r * (dxn - xn * jnp.mean(dxn * xn, axis=-1, keepdims=True))
            return d_sh + _fold8(dh), d_sc + _fold8(dh * hpre), g_g + _fold8(dhp * xn)
        zero = jnp.zeros((8, D), F32)
        d_sh, d_sc, g_g = _blocks(TM // RB, blk, (zero, zero, zero))
        st_ref[0] = st_ref[0] + d_sh
        st_ref[1] = st_ref[1] + d_sc
        st_ref[2] = st_ref[2] + g_g

    tile = pl.BlockSpec((TM, D), lambda i: (i, 0))
    return pl.pallas_call(
        body, name="in_bwd_dx", grid=(nt,),
        out_shape=(jax.ShapeDtypeStruct((T, D), F32), jax.ShapeDtypeStruct((3, 8, D), F32)),
        in_specs=[pl.BlockSpec((NSEG, TM, D), lambda i: (0, i, 0)),
                  pl.BlockSpec((NSEG, D, D), lambda i: (0, 0, 0), pipeline_mode=pl.Buffered(1)),
                  tile, tile, pl.BlockSpec((3, D), lambda i: (0, 0)), pl.BlockSpec((1, D), lambda i: (0, 0)),
                  pl.BlockSpec((8, 128), lambda i: (0, 0))],
        out_specs=(tile, pl.BlockSpec((3, 8, D), lambda i: (0, 0, 0))),
        scratch_shapes=[VMEM((TM, D), F32)],
        compiler_params=_cp(("arbitrary",), vmem_mb=56),
    )(dp, wing, x, dx2, mod, g_pre, token)


RS_ORDER = (5, 4, 3, 2, 7, 6, 1, 0)
RS_HALF = D // 2


def _in_bwd_dw(ht, dp, order):
    T = ht.shape[1]
    tmw = min(4 * TMB, T)
    nb = T // tmw
    nu = 2 * NSEG

    def body(order_ref, ht_ref, dp_ref, csum_ref, g_ref, acc, sbuf, rsib, send_sems, recv_sems, out_sems):
        j = pl.program_id(0)
        i = pl.program_id(1)
        x, y, c = _me()
        sibling = (x, y, 1 - c)

        def to_sibling(s):
            return pltpu.make_async_remote_copy(src_ref=sbuf.at[s], dst_ref=rsib.at[s], send_sem=send_sems.at[s],
                                                recv_sem=recv_sems.at[s], device_id=sibling, device_id_type=MESH)

        def to_csum(t):
            cols = pl.ds((t % 2) * RS_HALF, RS_HALF)
            return pltpu.make_async_copy(sbuf.at[8 + t], csum_ref.at[t // 2, :, cols], out_sems.at[t])

        @pl.when(i == 0)
        def _():
            acc[...] = jnp.zeros_like(acc)
        acc[...] += jnp.dot(ht_ref[...], dp_ref[...], preferred_element_type=F32)

        for jj in range(nu):
            chip, half, mine = jj // 4, (jj % 4) // 2, jj % 2
            s = 2 * chip + half

            @pl.when((j == jj) & (i == nb - 1))
            def _(chip=chip, half=half, mine=mine, s=s):
                if not mine:
                    sbuf[s] = acc[...].astype(BF16)
                    to_sibling(s).start()
                elif chip < 3:
                    to_sibling(s).wait_recv()
                    sbuf[8 + s] = (acc[...] + rsib[s].astype(F32)).astype(BF16)
                    to_csum(s).start()
                else:
                    to_sibling(s).wait_recv()
                    g_ref[:, half * RS_HALF:(half + 1) * RS_HALF] = acc[...] + rsib[s].astype(F32)

        @pl.when((j == nu - 1) & (i == nb - 1))
        def _():
            for s in range(8):
                to_sibling(s).wait_send()
            for t in range(6):
                to_csum(t).wait()

    hbm = pl.BlockSpec(memory_space=ANY)
    return pl.pallas_call(
        body, name="in_bwd_dw",
        out_shape=(jax.ShapeDtypeStruct((3, D, D), BF16), jax.ShapeDtypeStruct((D, D), F32)),
        grid_spec=pltpu.PrefetchScalarGridSpec(
            num_scalar_prefetch=1, grid=(nu, nb),
            in_specs=[pl.BlockSpec((D, tmw), lambda j, i, o: (0, i)),
                      pl.BlockSpec((None, tmw, RS_HALF), lambda j, i, o: (o[2 * (j // 4) + j % 2], i, (j % 4) // 2))],
            out_specs=(hbm, pl.BlockSpec((D, D), lambda j, i, o: (0, 0))),
            scratch_shapes=[VMEM((D, RS_HALF), F32), VMEM((14, D, RS_HALF), BF16), VMEM((8, D, RS_HALF), BF16),
                            pltpu.SemaphoreType.DMA((8,)), pltpu.SemaphoreType.DMA((8,)), pltpu.SemaphoreType.DMA((6,))]),
        compiler_params=_cp(("arbitrary", "arbitrary"), vmem_mb=56),
    )(order, ht, dp)


def _gather_late(late):
    def body(sm_ref, smg_ref, ag_send, ag_recv, ag_loc):
        _allgather([sm_ref], lambda a, slot: smg_ref.at[slot], ag_send, ag_recv, ag_loc)

    hbm = pl.BlockSpec(memory_space=ANY)
    return pl.pallas_call(
        body, name="gather_late",
        out_shape=jax.ShapeDtypeStruct((NDEV, NLATE, D), F32),
        in_specs=[hbm], out_specs=hbm,
        scratch_shapes=[pltpu.SemaphoreType.DMA((1, 7)), pltpu.SemaphoreType.DMA((1, 7)), pltpu.SemaphoreType.DMA((1,))],
        compiler_params=_cp(vmem_mb=32),
    )(late)


RS_CHIPS = ((1, 0), (0, 1), (1, 1))


def _rs_peer(q):
    x, y, c = _me()
    fx, fy = RS_CHIPS[q]
    return (1 - x if fx else x, 1 - y if fy else y, c)


def _rs_start(csum):
    def body(csum_ref, land_ref, send_sems, recv_sems, csum_thru, land_thru, token):
        barrier = pltpu.get_barrier_semaphore()
        for q in range(3):
            pl.semaphore_signal(barrier, inc=1, device_id=_rs_peer(q), device_id_type=MESH)
        pl.semaphore_wait(barrier, 3)
        for q in range(3):
            pltpu.make_async_remote_copy(src_ref=csum_ref.at[q], dst_ref=land_ref.at[q], send_sem=send_sems.at[q],
                                         recv_sem=recv_sems.at[q], device_id=_rs_peer(q), device_id_type=MESH).start()
        token[...] = jnp.zeros_like(token)

    hbm = pl.BlockSpec(memory_space=pltpu.HBM)
    sem = pl.BlockSpec(memory_space=pltpu.SEMAPHORE)
    land = pltpu.with_memory_space_constraint(lax.empty(csum.shape, csum.dtype), pltpu.HBM)
    return pl.pallas_call(
        body, name="rs_start",
        out_shape=(pltpu.SemaphoreType.DMA((3,)), pltpu.SemaphoreType.DMA((3,)), pltpu.HBM(csum.shape, csum.dtype),
                   pltpu.HBM(csum.shape, csum.dtype), jax.ShapeDtypeStruct((8, 128), F32)),
        in_specs=(hbm, hbm), out_specs=(sem, sem, hbm, hbm, pl.BlockSpec(memory_space=VMEM)),
        input_output_aliases={0: 2, 1: 3},
        compiler_params=pltpu.CompilerParams(has_side_effects=pltpu.SideEffectType.DATAFLOW_SIDE_EFFECTING, collective_id=1),
    )(pltpu.with_memory_space_constraint(csum, pltpu.HBM), land)


def _rs_wait(send_sems, recv_sems, csum_thru, land_thru, after):
    def body(csum_ref, land_ref, send_sems, recv_sems, after_ref, csum_dead, got_ref):
        del after_ref, csum_dead, got_ref
        for q in range(3):
            cp = pltpu.make_async_remote_copy(src_ref=csum_ref.at[q], dst_ref=land_ref.at[q], send_sem=send_sems.at[q],
                                              recv_sem=recv_sems.at[q], device_id=_rs_peer(q), device_id_type=MESH)
            cp.wait_send()
            cp.wait_recv()

    hbm = pl.BlockSpec(memory_space=pltpu.HBM)
    sem = pl.BlockSpec(memory_space=pltpu.SEMAPHORE)
    return pl.pallas_call(
        body, name="rs_wait",
        out_shape=(pltpu.HBM(csum_thru.shape, csum_thru.dtype), pltpu.HBM(csum_thru.shape, csum_thru.dtype)),
        in_specs=(hbm, hbm, sem, sem, pl.BlockSpec(memory_space=ANY)), out_specs=(hbm, hbm),
        input_output_aliases={0: 0, 1: 1},
        compiler_params=pltpu.CompilerParams(has_side_effects=pltpu.SideEffectType.DATAFLOW_SIDE_EFFECTING),
    )(csum_thru, land_thru, send_sems, recv_sems, after)[1]


def _pack_early(st_a, st_b, st_m, gws, gbs):
    def body(sa_ref, sb_ref, sm_ref, gws_ref, gbs_ref, out_ref):
        out_ref[...] = jnp.zeros_like(out_ref)
        fold = lambda v: jnp.sum(v, axis=0, keepdims=True)
        out_ref[0:1, :] = fold(sm_ref[2])
        out_ref[1:2, :] = fold(sa_ref[2])
        out_ref[2:3, :] = fold(sa_ref[0])
        out_ref[3:4, :] = fold(sa_ref[1])
        out_ref[4:5, :] = fold(sb_ref[0])
        out_ref[5:6, :] = fold(sb_ref[1])
        out_ref[7:8, :] = fold(sm_ref[1])
        out_ref[8:9, :] = fold(sm_ref[0])
        tri = lax.broadcasted_iota(jnp.int32, (CHUNK, CHUNK), 0) >= lax.broadcasted_iota(jnp.int32, (CHUNK, CHUNK), 1)
        for h in range(HEADS):
            out_ref[16:16 + CHUNK, h * CHUNK:(h + 1) * CHUNK] = jnp.where(tri, gws_ref[h], 0.0)
            out_ref[6:7, h * CHUNK:(h + 1) * CHUNK] = fold(gbs_ref[h].T)

    vm = pl.BlockSpec(memory_space=VMEM)
    return pl.pallas_call(
        body, name="pack_early", out_shape=jax.ShapeDtypeStruct((NEARLY, D), F32),
        in_specs=[vm] * 5, out_specs=vm, compiler_params=_cp(vmem_mb=32),
    )(st_a, st_b, st_m, gws, gbs)


def _pack_late(st_x, gcw):
    def body(sx_ref, gcw_ref, out_ref):
        out_ref[...] = jnp.zeros_like(out_ref)
        fold = lambda v: jnp.sum(v, axis=0, keepdims=True)
        for r in range(3):
            out_ref[r:r + 1, :] = fold(sx_ref[r])
        for k in range(KW):
            out_ref[8 + k:9 + k, :] = fold(gcw_ref[k])

    vm = pl.BlockSpec(memory_space=VMEM)
    return pl.pallas_call(
        body, name="pack_late", out_shape=jax.ShapeDtypeStruct((NLATE, D), F32),
        in_specs=[vm] * 2, out_specs=vm, compiler_params=_cp(vmem_mb=32),
    )(st_x, gcw)


def _pack_params(name, b_ada, g_pre, conv_b, lg, lb, sg, sb, b_sgu, g_final, w_sgu_t):
    def body(ba_ref, *refs):
        rows, ws_ref, out_ref = refs[:8], refs[8], refs[9]
        out_ref[...] = jnp.zeros_like(out_ref)
        for r in range(3):
            out_ref[r:r + 1, :] = ba_ref[:, r * D:(r + 1) * D]
        for r, ref in enumerate(rows):
            out_ref[3 + r:4 + r, :] = ref[...]
        out_ref[ROW_WS:ROW_WS + CHUNK, :] = ws_ref[...]

    vm = pl.BlockSpec(memory_space=VMEM)
    return pl.pallas_call(
        body, name=name, out_shape=jax.ShapeDtypeStruct((NSMALL, D), F32),
        in_specs=[vm] * 10, out_specs=vm, compiler_params=_cp(vmem_mb=32),
    )(b_ada, g_pre, conv_b, lg, lb, sg, sb, b_sgu, g_final, w_sgu_t)


def _adam(w, g, m, v):
    m2 = ADAM_B1 * m + (1.0 - ADAM_B1) * g
    v2 = ADAM_B2 * v + (1.0 - ADAM_B2) * (g * g)
    m_hat = m2 / (1.0 - ADAM_B1 ** ADAM_STEP)
    v_hat = v2 / (1.0 - ADAM_B2 ** ADAM_STEP)
    delta = -ADAM_LR * (m_hat / (jnp.sqrt(v_hat) + ADAM_EPS) + ADAM_WD * w)
    return delta, m2, v2


def _small_finish(eg, lg, w, m, v):
    def body(eg_ref, lg_ref, w_ref, m_ref, v_ref, loss_ref, g_ref, d_ref, m2_ref, v2_ref):
        e = eg_ref[0]
        l = lg_ref[0]
        for s in range(1, NDEV):
            e = e + eg_ref[s]
            l = l + lg_ref[s]
        g_ref[...] = jnp.zeros_like(g_ref)
        g_ref[0:2, :] = l[0:2, :]
        g_ref[2:3, :] = e[0:1, :]
        g_ref[3:4, :] = l[2:3, :]
        g_ref[4:12, :] = e[1:9, :]
        g_ref[ROW_CW:ROW_CW + KW, :] = l[8:8 + KW, :]
        g_ref[ROW_WS:ROW_WS + CHUNK, :] = e[16:16 + CHUNK, :]
        loss_ref[...] = jnp.sum(e[8:9, :], axis=1, keepdims=True)
        d_ref[...], m2_ref[...], v2_ref[...] = _adam(w_ref[...], g_ref[...], m_ref[...], v_ref[...])

    vm = pl.BlockSpec(memory_space=VMEM)
    sd = jax.ShapeDtypeStruct((NSMALL, D), F32)
    return pl.pallas_call(
        body, name="small_finish", out_shape=(jax.ShapeDtypeStruct((1, 1), F32), sd, sd, sd, sd),
        in_specs=[vm] * 5, out_specs=(vm,) * 5, compiler_params=_cp(vmem_mb=32),
    )(eg, lg, w, m, v)


def _ada_grad_adam(ct, dm, w, m, v):
    def body(ct_ref, dm_ref, w_ref, m_ref, v_ref, g_ref, d_ref, m2_ref, v2_ref):
        g = ct_ref[:, 0:1] * dm_ref[0:1, :]
        for b in range(1, NDEV):
            g = g + ct_ref[:, b:b + 1] * dm_ref[b:b + 1, :]
        g_ref[...] = g
        d_ref[...], m2_ref[...], v2_ref[...] = _adam(w_ref[...], g, m_ref[...], v_ref[...])

    vm = pl.BlockSpec(memory_space=VMEM)
    sd = jax.ShapeDtypeStruct(w.shape, F32)
    return pl.pallas_call(
        body, name="ada_grad_adam", out_shape=(sd, sd, sd, sd),
        in_specs=[vm] * 5, out_specs=(vm,) * 4, compiler_params=_cp(vmem_mb=32),
    )(ct, dm, w, m, v)


def _adam_f32(name, g, w, m, v, rows=None):
    R, C = w.shape
    rows = R if rows is None else rows

    def body(g_ref, w_ref, m_ref, v_ref, d_ref, m2_ref, v2_ref):
        d_ref[...], m2_ref[...], v2_ref[...] = _adam(w_ref[...], g_ref[...], m_ref[...], v_ref[...])

    tile = pl.BlockSpec((rows, C), lambda i: (i, 0))
    sd = jax.ShapeDtypeStruct(w.shape, F32)
    return pl.pallas_call(
        body, name=name, grid=(R // rows,), out_shape=(sd, sd, sd), in_specs=[tile] * 4, out_specs=(tile,) * 3,
        compiler_params=_cp(("parallel",), vmem_mb=32),
    )(g, w, m, v)


def _adam_w_in(gown, got, w, m, v, rows=256):
    R, C = w.shape

    def body(go_ref, got_ref, w_ref, m_ref, v_ref, g_ref, d_ref, m2_ref, v2_ref):
        g = go_ref[...]
        for q in range(3):
            g = g + got_ref[q].astype(F32)
        g_ref[...] = g
        d_ref[...], m2_ref[...], v2_ref[...] = _adam(w_ref[...], g, m_ref[...], v_ref[...])

    tile = pl.BlockSpec((rows, C), lambda i: (i, 0))
    sd = jax.ShapeDtypeStruct((R, C), F32)
    return pl.pallas_call(
        body, name="adam_w_in", grid=(R // rows,), out_shape=(sd, sd, sd, sd),
        in_specs=[tile, pl.BlockSpec((3, rows, C), lambda i: (0, i, 0)), tile, tile, tile], out_specs=(tile,) * 4,
        compiler_params=_cp(("parallel",), vmem_mb=32),
    )(gown, got, w, m, v)


def _adam_reduce(name, recv, w, m, v, rows, recv_spec):
    R, C = w.shape

    def body(r_ref, w_ref, m_ref, v_ref, g_ref, d_ref, m2_ref, v2_ref):
        g = r_ref[0].astype(F32)
        for s in range(1, NDEV):
            g = g + r_ref[s].astype(F32)
        g_ref[...] = g
        d_ref[...], m2_ref[...], v2_ref[...] = _adam(w_ref[...], g, m_ref[...], v_ref[...])

    tile = pl.BlockSpec((rows, C), lambda i: (i, 0))
    sd = jax.ShapeDtypeStruct((R, C), F32)
    return pl.pallas_call(
        body, name=name, grid=(R // rows,), out_shape=(sd, sd, sd, sd),
        in_specs=[recv_spec, tile, tile, tile], out_specs=(tile,) * 4,
        compiler_params=_cp(("parallel",)),
    )(recv, w, m, v)


def kernel(x, c, w_ada, b_ada, g_pre, w_in, conv_w, conv_b, conv_ln_g, conv_ln_b, w_conv_out, sgu_ln_g, sgu_ln_b, w_sgu, b_sgu, w_sgu_out, w_o, g_final, loss_target, m_w_ada, m_b_ada, m_g_pre, m_w_in, m_conv_w, m_conv_b, m_conv_ln_g, m_conv_ln_b, m_w_conv_out, m_sgu_ln_g, m_sgu_ln_b, m_w_sgu, m_b_sgu, m_w_sgu_out, m_w_o, m_g_final, v_w_ada, v_b_ada, v_g_pre, v_w_in, v_conv_w, v_conv_b, v_conv_ln_g, v_conv_ln_b, v_w_conv_out, v_sgu_ln_g, v_sgu_ln_b, v_w_sgu, v_b_sgu, v_w_sgu_out, v_w_o, v_g_final):
    T = x.shape[1]
    assert T % TMB == 0 and x.shape[2] == D
    xs = x[0]
    tgt = loss_target[0]
    my = 4 * lax.axis_index("x") + 2 * lax.axis_index("y") + lax.axis_index("c")

    mod, cg = _ada_fwd(c, w_ada[0], b_ada)
    gf = g_final.reshape(1, D)
    bst = b_sgu[0].T

    h, ht = _prep_h(xs, mod, g_pre)
    ag_order = jnp.stack([_slot(*b) for b in _ag_blocks(lax.axis_index("x"), lax.axis_index("y"), lax.axis_index("c"))])
    p, wing, wcog, cwg, own2, land2 = _in_proj(h, w_in[0], ag_order.astype(jnp.int32),
                                               w_conv_out[0], w_sgu_out[0], w_o[0], conv_w[0])
    wco = wcog.reshape(1, D, D)
    cw = jnp.broadcast_to(jnp.transpose(cwg, (1, 0, 2)).reshape(KW, 1, D), (KW, 8, D))
    wq_send, wq_recv, own2_thru, land2_thru, wq_token = _wq_start(own2, land2)
    yc, ya = _branch_a_fwd(p, cw, conv_b, conv_ln_g, conv_ln_b, wco, wq_token)
    wq2 = _wq_wait(wq_send, wq_recv, own2_thru, land2_thru, yc).reshape(2, D, D)
    yb = _branch_b_fwd(p, w_sgu[0], bst, sgu_ln_g, sgu_ln_b, wq2)
    dx2, dya, dyb, dp, gw_o, st_m = _merge_loss(p, ya, yb, xs, tgt, mod, gf, wq2)

    dp, gw_so, gws, gbs, st_b = _branch_b_bwd(dp, dyb, p, w_sgu[0], bst, sgu_ln_g, sgu_ln_b, wq2)
    dp, dyc, gw_co, st_a = _branch_a_bwd1(dp, dya, yc, p, conv_ln_g, conv_ln_b, wco)
    early = _pack_early(st_a, st_b, st_m, gws, gbs)
    dp, gcw, rq, eg = _branch_a_bwd2(dp, dyc, p, cw, gw_co, gw_so, gw_o, early)
    csum, gown = _in_bwd_dw(ht, dp, my ^ jnp.array(RS_ORDER, jnp.int32))
    rs_send, rs_recv, csum_thru, land_thru, token = _rs_start(csum)
    grad_x, st_x = _in_bwd_dx(dp, wing, xs, dx2, mod, g_pre, token)
    got = _rs_wait(rs_send, rs_recv, csum_thru, land_thru, st_x)
    lg = _gather_late(_pack_late(st_x, gcw))

    def pack(name, b_ada_, g_pre_, conv_b_, lg_, lb_, sg_, sb_, b_sgu_, gfin_, w_sgu_):
        return _pack_params(name, b_ada_, g_pre_, conv_b_, lg_, lb_, sg_, sb_, b_sgu_.reshape(1, D), gfin_.reshape(1, D),
                            jnp.transpose(w_sgu_[0], (1, 0, 2)).reshape(CHUNK, D))

    pw = pack("pack_w", b_ada, g_pre, conv_b, conv_ln_g, conv_ln_b, sgu_ln_g, sgu_ln_b, b_sgu, g_final, w_sgu)
    pm = pack("pack_m", m_b_ada, m_g_pre, m_conv_b, m_conv_ln_g, m_conv_ln_b, m_sgu_ln_g, m_sgu_ln_b, m_b_sgu, m_g_final,
              m_w_sgu)
    pv = pack("pack_v", v_b_ada, v_g_pre, v_conv_b, v_conv_ln_g, v_conv_ln_b, v_sgu_ln_g, v_sgu_ln_b, v_b_sgu, v_g_final,
              v_w_sgu)
    loss11, sg_, sd_, sm_, sv_ = _small_finish(eg, lg, pw, pm, pv)

    def unpack(a):
        return dict(
            b_ada=a[0:3].reshape(1, 3 * D), g_pre=a[3:4], conv_b=a[4:5], conv_ln_g=a[5:6], conv_ln_b=a[6:7],
            sgu_ln_g=a[7:8], sgu_ln_b=a[8:9], b_sgu=a[9:10].reshape(1, HEADS, CHUNK), g_final=a[10],
            w_sgu=jnp.transpose(a[ROW_WS:ROW_WS + CHUNK].reshape(CHUNK, HEADS, CHUNK), (1, 0, 2))[None])
    small = [unpack(a) for a in (sg_, sd_, sm_, sv_)]

    g_cw = lax.dynamic_slice_in_dim(sg_[ROW_CW:ROW_CW + KW], my * 128, 128, axis=1)
    d_cw, m_cw, v_cw = _adam_f32("adam_conv_w", g_cw, conv_w[0], m_conv_w[0], v_conv_w[0])

    wcols = w_ada.shape[2]
    dm_all = jnp.concatenate([lg[:, 0], lg[:, 1], eg[:, 0]], axis=1)
    dm_mine = lax.dynamic_slice_in_dim(dm_all, my * wcols, wcols, axis=1)
    g_ada, d_ada, m_ada, v_ada = _ada_grad_adam(cg.T, dm_mine, w_ada[0], m_w_ada[0], v_w_ada[0])

    g_in, d_in, m_in, v_in = _adam_w_in(gown, got, w_in[0], m_w_in[0], v_w_in[0])
    big = {}
    for j, (nm, w_, m_, v_) in enumerate((("w_conv_out", w_conv_out, m_w_conv_out, v_w_conv_out),
                                          ("w_sgu_out", w_sgu_out, m_w_sgu_out, v_w_sgu_out),
                                          ("w_o", w_o, m_w_o, v_w_o))):
        big[nm] = _adam_reduce("adam_" + nm, rq, w_[0], m_[0], v_[0], 128,
                               pl.BlockSpec((NDEV, None, 128, D), lambda i, j=j: (0, j, 0, 0)))

    per = {
        "w_ada": tuple(a[None] for a in (g_ada, d_ada, m_ada, v_ada)),
        "w_in": tuple(a[None] for a in (g_in, d_in, m_in, v_in)),
        "conv_w": tuple(a[None] for a in (g_cw, d_cw, m_cw, v_cw)),
    }
    for nm in ("w_conv_out", "w_sgu_out", "w_o"):
        per[nm] = tuple(a[None] for a in big[nm])
    for nm in ("b_ada", "g_pre", "conv_b", "conv_ln_g", "conv_ln_b", "sgu_ln_g", "sgu_ln_b", "w_sgu", "b_sgu", "g_final"):
        per[nm] = tuple(s[nm] for s in small)

    order = ["w_ada", "b_ada", "g_pre", "w_in", "conv_w", "conv_b", "conv_ln_g", "conv_ln_b", "w_conv_out",
             "sgu_ln_g", "sgu_ln_b", "w_sgu", "b_sgu", "w_sgu_out", "w_o", "g_final"]
    outs = [loss11.reshape(()), grad_x[None]]
    for part in range(4):
        outs += [per[nm][part] for nm in order]
    return tuple(outs)
```

```python
import jax
import jax.numpy as jnp
from jax import lax
from jax.experimental import pallas as pl
from jax.experimental.pallas import tpu as pltpu

F32 = jnp.float32
BF16 = jnp.bfloat16
MESH = pl.DeviceIdType.MESH
VMEM = pltpu.VMEM
ANY = pl.ANY

D = 1024
NDEV = 8
NSEG = 8
HEADS = 8
HD = 128
CHUNK = 128
KW = 31
HALO = 32
EPS = 1e-6
TM = 256
TMB = 512
RB = 32
RBC = 32
TAPG = 4
EXT = TM + HALO
NSMALL = 176
NEARLY = 144
NLATE = 40
ROW_CW = 16
ROW_WS = 48
W_CO = 0
W_SO, W_O = 0, 1

ADAM_LR = 0.001
ADAM_B1 = 0.9
ADAM_B2 = 0.999
ADAM_EPS = 1e-08
ADAM_WD = 0.01
ADAM_STEP = 10

INV_SQRT2 = 0.7071067811865476
INV_SQRT_2PI = 0.3989422804014327

NT = (((1,), (1,)), ((), ()))
TN = (((0,), (0,)), ((), ()))


def _cp(sem=None, vmem_mb=48):
    return pltpu.CompilerParams(dimension_semantics=sem, vmem_limit_bytes=vmem_mb * 1024 * 1024)


def _me():
    return lax.axis_index("x"), lax.axis_index("y"), lax.axis_index("c")


def _slot(px, py, pc):
    return 4 * px + 2 * py + pc


def _xor_peer(k):
    x, y, c = _me()
    return (1 - x if k & 4 else x, 1 - y if k & 2 else y, 1 - c if k & 1 else c)


def _allgather_phases(srcs, dst_at, send_sems, recv_sems, loc_sems):
    x, y, c = _me()
    me = (x, y, c)
    sibling = (x, y, 1 - c)
    chips = [(1 - x, y), (x, 1 - y), (1 - x, 1 - y)]
    na = len(srcs)

    def copy(a, k, block, to, src=None):
        d = dst_at(a, _slot(*block))
        return pltpu.make_async_remote_copy(
            src_ref=d if src is None else src, dst_ref=d,
            send_sem=send_sems.at[a, k], recv_sem=recv_sems.at[a, k],
            device_id=to, device_id_type=MESH)

    local = [pltpu.make_async_copy(srcs[a], dst_at(a, _slot(*me)), loc_sems.at[a]) for a in range(na)]
    first = []
    for a in range(na):
        first.append(copy(a, 0, me, sibling, src=srcs[a]))
        for j, chip in enumerate(chips):
            first.append(copy(a, 1 + j, me, (*chip, c), src=srcs[a]))
    passed = [copy(a, 4 + j, (*chip, c), sibling) for j, chip in enumerate(chips) for a in range(na)]

    def start():
        for cp in local + first:
            cp.start()

    def relay():
        for j, chip in enumerate(chips):
            for a in range(na):
                copy(a, 1 + j, (*chip, c), me).wait_recv()
                passed[j * na + a].start()

    def finish():
        for a in range(na):
            copy(a, 0, sibling, me).wait_recv()
        for j, chip in enumerate(chips):
            for a in range(na):
                copy(a, 4 + j, (*chip, 1 - c), me).wait_recv()
        for cp in first + passed:
            cp.wait_send()
        for cp in local:
            cp.wait()

    return start, relay, finish


def _allgather(srcs, dst_at, send_sems, recv_sems, loc_sems):
    start, relay, finish = _allgather_phases(srcs, dst_at, send_sems, recv_sems, loc_sems)
    start()
    relay()
    finish()


def _owner_exchange(gq, rq_ref, x_send, x_recv, x_loc):
    mx, my_, mc = _me()
    me = _slot(mx, my_, mc)

    def rows_of(a, slot):
        return gq[a].at[pl.ds(pl.multiple_of(slot * 128, 128), 128), :]

    def exchange(k, a, recv):
        px, py, pc = _xor_peer(k)
        peer = _slot(px, py, pc)
        return pltpu.make_async_remote_copy(
            src_ref=rows_of(a, me if recv else peer), dst_ref=rq_ref.at[peer if recv else me, a],
            send_sem=x_send.at[a, k - 1], recv_sem=x_recv.at[a, k - 1],
            device_id=(px, py, pc), device_id_type=MESH)

    local = [pltpu.make_async_copy(rows_of(a, me), rq_ref.at[me, a], x_loc.at[a]) for a in range(len(gq))]

    def start():
        for cp in local:
            cp.start()
        for k in range(1, NDEV):
            for a in range(len(gq)):
                exchange(k, a, False).start()

    def wait():
        for k in range(1, NDEV):
            for a in range(len(gq)):
                exchange(k, a, True).wait_recv()
        for k in range(1, NDEV):
            for a in range(len(gq)):
                exchange(k, a, False).wait_send()
        for cp in local:
            cp.wait()

    return start, wait


def _sig(v):
    return jax.nn.sigmoid(v)


def _gelu_parts(v):
    cdf = 0.5 * (1.0 + lax.erf(v * INV_SQRT2))
    pdf = jnp.exp(-0.5 * v * v) * INV_SQRT_2PI
    return v * cdf, cdf + v * pdf


def _ln_stats(v):
    mu = jnp.mean(v, axis=-1, keepdims=True)
    xc = v - mu
    var = jnp.mean(xc * xc, axis=-1, keepdims=True)
    rs = lax.rsqrt(var + EPS)
    return xc * rs, rs


def _ln_bwd(dn, n, rs):
    return rs * (dn - jnp.mean(dn, axis=-1, keepdims=True) - n * jnp.mean(dn * n, axis=-1, keepdims=True))


def _fold8(v):
    acc = v[0:8]
    for r in range(8, v.shape[0], 8):
        acc = acc + v[r:r + 8]
    return acc


def _rows(rb, n=RB):
    return pl.ds(rb * n, n)


def _ld(ref, rows):
    return ref[rows, :].astype(F32)


def _blocks(n, body, init=0):
    carry = init
    for rb in range(n):
        carry = body(rb, carry)
    return carry


def _shift_copies(s_ref):
    n = EXT - 8
    for m in range(1, 8):
        for c0 in range(0, n, 56):
            s_ref[m, c0:c0 + 56, :] = s_ref[0, c0 + m:c0 + m + 56, :]


def _ada_exchange(c_ref, w_ref, b_ref, mod_ref, call_ref, cdst, msrc, mdst, c_send, c_recv, m_send, m_recv):
    x, y, c = _me()
    me = _slot(x, y, c)

    def all_to_all(src_of, dst, ss, rs):
        def cp(k, recv):
            px, py, pc = _xor_peer(k)
            peer = _slot(px, py, pc)
            return pltpu.make_async_remote_copy(
                src_ref=src_of(me if recv else peer), dst_ref=dst.at[peer if recv else me],
                send_sem=ss.at[k - 1], recv_sem=rs.at[k - 1], device_id=(px, py, pc), device_id_type=MESH)
        for k in range(1, NDEV):
            cp(k, False).start()
        for k in range(1, NDEV):
            cp(k, True).wait_recv()
        for k in range(1, NDEV):
            cp(k, False).wait_send()

    cdst[me] = c_ref[...]
    all_to_all(lambda s: c_ref, cdst, c_send, c_recv)
    for b in range(NDEV):
        call_ref[b:b + 1, :] = cdst[b]
    m = jnp.dot(call_ref[...], w_ref[...], preferred_element_type=F32, precision=lax.Precision.HIGHEST)
    for b in range(NDEV):
        msrc[b] = m[b:b + 1, :]
    mdst[me] = msrc[me]
    all_to_all(lambda s: msrc.at[s], mdst, m_send, m_recv)
    full = jnp.concatenate([mdst[k] for k in range(NDEV)], axis=1) + b_ref[...]
    for r in range(3):
        mod_ref[r:r + 1, :] = full[:, r * D:(r + 1) * D]


def _ada_fwd(c, w_ada, b_ada):
    wcols = w_ada.shape[1]

    def body(*refs):
        _ada_exchange(*refs)

    vm = pl.BlockSpec(memory_space=VMEM)
    return pl.pallas_call(
        body, name="ada_fwd",
        out_shape=(jax.ShapeDtypeStruct((3, D), F32), jax.ShapeDtypeStruct((NDEV, D), F32)),
        in_specs=[vm, vm, vm], out_specs=(vm, vm),
        scratch_shapes=[VMEM((NDEV, 1, D), F32), VMEM((NDEV, 1, wcols), F32), VMEM((NDEV, 1, wcols), F32),
                        pltpu.SemaphoreType.DMA((7,)), pltpu.SemaphoreType.DMA((7,)),
                        pltpu.SemaphoreType.DMA((7,)), pltpu.SemaphoreType.DMA((7,))],
        compiler_params=_cp(vmem_mb=32),
    )(c, w_ada, b_ada)


def _prep_h(x, mod, g_pre):
    T = x.shape[0]

    def body(x_ref, mod_ref, g_ref, h_ref, ht_ref):
        def blk(rb, carry):
            rows = _rows(rb)
            xv = x_ref[rows, :]
            r = lax.rsqrt(jnp.mean(xv * xv, axis=-1, keepdims=True) + EPS)
            h_ref[rows, :] = ((xv * r) * g_ref[...] * (1.0 + mod_ref[1:2, :]) + mod_ref[0:1, :]).astype(BF16)
            return carry
        _blocks(TMB // RB, blk)
        ht_ref[...] = h_ref[...].T

    return pl.pallas_call(
        body, name="prep_h", grid=(T // TMB,),
        out_shape=(jax.ShapeDtypeStruct((T, D), BF16), jax.ShapeDtypeStruct((D, T), BF16)),
        in_specs=[pl.BlockSpec((TMB, D), lambda i: (i, 0)), pl.BlockSpec((3, D), lambda i: (0, 0)),
                  pl.BlockSpec((1, D), lambda i: (0, 0))],
        out_specs=(pl.BlockSpec((TMB, D), lambda i: (i, 0)), pl.BlockSpec((D, TMB), lambda i: (0, i))),
        compiler_params=_cp(("parallel",)),
    )(x, mod, g_pre)


def _ag_blocks(x, y, c):
    a = 1 - c
    n1 = (x ^ a, y ^ c)
    n2 = (x ^ c, y ^ a)
    dg = (1 - x, 1 - y)
    return [(x, y, c), (x, y, a), (*n1, c), (*n2, c), (*n1, a), (*n2, a), (*dg, c), (*dg, a)]


def _in_proj(h, w_in, order, w_co, w_so, w_o, conv_w):
    T = h.shape[0]
    tmp = min(2 * TMB, T)
    nb = T // tmp

    def body(order_ref, h_ref, w_ref, wco_ref, wso_ref, wo_ref, cw_ref, p_ref, wing_ref, wqg_ref, cwg_ref, own2_ref, land2_ref,
             wbuf, wq_st, wq2_st, send_sems, recv_sems, out_sems, ag_send, ag_recv, ag_loc, own_sems):
        j = pl.program_id(0)
        i = pl.program_id(1)
        blocks = _ag_blocks(*_me())
        me, sibling, n1, n2, n1o, n2o, dg, dgo = blocks
        own2 = [pltpu.make_async_copy(wq2_st, own2_ref, own_sems.at[0]),
                pltpu.make_async_copy(wq2_st, land2_ref.at[:, _slot(*me)], own_sems.at[1])]
        ag_start, ag_relay, ag_finish = _allgather_phases(
            [wq_st, cw_ref], lambda a, slot: wqg_ref.at[:, slot] if a == 0 else cwg_ref.at[slot], ag_send, ag_recv, ag_loc)

        def copy(k, block, to):
            d = wbuf.at[_slot(*block)]
            return pltpu.make_async_remote_copy(src_ref=d, dst_ref=d, send_sem=send_sems.at[k], recv_sem=recv_sems.at[k],
                                                device_id=to, device_id_type=MESH)

        def writeback(jj):
            s = _slot(*blocks[jj])
            return pltpu.make_async_copy(wbuf.at[s], wing_ref.at[s], out_sems.at[jj])

        sends = [copy(0, me, sibling), copy(1, me, n1), copy(2, me, n2)]
        relay = copy(3, n1, n2)
        passed = [copy(4, n1, sibling), copy(5, n2, sibling), copy(6, dg, sibling)]
        arrivals = {1: [(0, sibling)], 2: [(1, n1), (2, n2)], 4: [(5, n1o)], 5: [(4, n2o)], 6: [(3, dg)], 7: [(6, dgo)]}

        @pl.when((j == 0) & (i == 0))
        def _():
            wbuf[_slot(*me)] = w_ref[...].astype(BF16)
            for cp in sends:
                cp.start()
            writeback(0).start()
            wq2_st[W_SO] = wso_ref[...].astype(BF16)
            wq2_st[W_O] = wo_ref[...].astype(BF16)
            for cp in own2:
                cp.start()

        for jj in range(1, NSEG):
            @pl.when((j == jj) & (i == 0))
            def _(jj=jj):
                for k, block in arrivals.get(jj, []):
                    copy(k, block, me).wait_recv()
                if jj == 2:
                    relay.start()
                    passed[0].start()
                    passed[1].start()
                    writeback(3).start()
                    wq_st[0] = wco_ref[...].astype(BF16)
                    ag_start()
                if jj == 6:
                    passed[2].start()
                if jj != 3:
                    writeback(jj).start()
                if jj == NSEG - 1:
                    ag_relay()

        p_ref[...] = jnp.dot(h_ref[...], wbuf[order_ref[j]], preferred_element_type=F32).astype(BF16)

        @pl.when((j == NSEG - 1) & (i == nb - 1))
        def _():
            for cp in sends + [relay] + passed:
                cp.wait_send()
            for jj in range(NSEG):
                writeback(jj).wait()
            for cp in own2:
                cp.wait()
            ag_finish()

    vm = pl.BlockSpec(memory_space=VMEM)
    hbm = pl.BlockSpec(memory_space=ANY)
    return pl.pallas_call(
        body, name="in_proj",
        out_shape=(jax.ShapeDtypeStruct((T, NSEG * D), BF16), jax.ShapeDtypeStruct((NSEG, D, D), BF16),
                   jax.ShapeDtypeStruct((1, NDEV, 128, D), BF16), jax.ShapeDtypeStruct((NDEV, KW, 128), F32),
                   jax.ShapeDtypeStruct((2, 128, D), BF16), jax.ShapeDtypeStruct((2, NDEV, 128, D), BF16)),
        grid_spec=pltpu.PrefetchScalarGridSpec(
            num_scalar_prefetch=1, grid=(NSEG, nb),
            in_specs=[pl.BlockSpec((tmp, D), lambda j, i, o: (i, 0)), vm, vm, vm, vm, vm],
            out_specs=(pl.BlockSpec((tmp, D), lambda j, i, o: (i, o[j])), hbm, hbm, hbm, hbm, hbm),
            scratch_shapes=[VMEM((NSEG, D, D), BF16), VMEM((1, 128, D), BF16), VMEM((2, 128, D), BF16),
                            pltpu.SemaphoreType.DMA((7,)), pltpu.SemaphoreType.DMA((7,)), pltpu.SemaphoreType.DMA((NSEG,)),
                            pltpu.SemaphoreType.DMA((2, 7)), pltpu.SemaphoreType.DMA((2, 7)), pltpu.SemaphoreType.DMA((2,)),
                            pltpu.SemaphoreType.DMA((2,))]),
        compiler_params=_cp(("arbitrary", "arbitrary")),
    )(order, h, w_in, w_co, w_so, w_o, conv_w)


def _wq_start(own2, land2):
    def body(src_ref, land_ref, send_sems, recv_sems, src_thru, land_thru, token):
        barrier = pltpu.get_barrier_semaphore()
        for k in range(1, NDEV):
            pl.semaphore_signal(barrier, inc=1, device_id=_xor_peer(k), device_id_type=MESH)
        pl.semaphore_wait(barrier, NDEV - 1)
        x, y, c = _me()
        for k in range(1, NDEV):
            pltpu.make_async_remote_copy(src_ref=src_ref, dst_ref=land_ref.at[:, _slot(x, y, c)], send_sem=send_sems.at[k - 1],
                                         recv_sem=recv_sems.at[k - 1], device_id=_xor_peer(k), device_id_type=MESH).start()
        token[...] = jnp.zeros_like(token)

    hbm = pl.BlockSpec(memory_space=pltpu.HBM)
    sem = pl.BlockSpec(memory_space=pltpu.SEMAPHORE)
    return pl.pallas_call(
        body, name="wq_start",
        out_shape=(pltpu.SemaphoreType.DMA((NDEV - 1,)), pltpu.SemaphoreType.DMA((NDEV - 1,)),
                   pltpu.HBM(own2.shape, own2.dtype), pltpu.HBM(land2.shape, land2.dtype), jax.ShapeDtypeStruct((8, 128), F32)),
        in_specs=(hbm, hbm), out_specs=(sem, sem, hbm, hbm, pl.BlockSpec(memory_space=VMEM)),
        input_output_aliases={0: 2, 1: 3},
        compiler_params=pltpu.CompilerParams(has_side_effects=pltpu.SideEffectType.DATAFLOW_SIDE_EFFECTING, collective_id=2),
    )(pltpu.with_memory_space_constraint(own2, pltpu.HBM), pltpu.with_memory_space_constraint(land2, pltpu.HBM))


def _wq_wait(send_sems, recv_sems, src_thru, land_thru, after):
    def body(src_ref, land_ref, send_sems, recv_sems, after_ref, src_dead, got_ref):
        del after_ref, src_dead, got_ref
        for k in range(1, NDEV):
            px, py, pc = _xor_peer(k)
            cp = pltpu.make_async_remote_copy(src_ref=src_ref, dst_ref=land_ref.at[:, _slot(px, py, pc)],
                                              send_sem=send_sems.at[k - 1], recv_sem=recv_sems.at[k - 1],
                                              device_id=(px, py, pc), device_id_type=MESH)
            cp.wait_send()
            cp.wait_recv()

    hbm = pl.BlockSpec(memory_space=pltpu.HBM)
    sem = pl.BlockSpec(memory_space=pltpu.SEMAPHORE)
    return pl.pallas_call(
        body, name="wq_wait",
        out_shape=(pltpu.HBM(src_thru.shape, src_thru.dtype), pltpu.HBM(land_thru.shape, land_thru.dtype)),
        in_specs=(hbm, hbm, sem, sem, pl.BlockSpec(memory_space=ANY)), out_specs=(hbm, hbm),
        input_output_aliases={0: 0, 1: 1},
        compiler_params=pltpu.CompilerParams(has_side_effects=pltpu.SideEffectType.DATAFLOW_SIDE_EFFECTING),
    )(src_thru, land_thru, send_sems, recv_sems, after)[1]


def _fill_a_ext(i, s_ref, val_ref, glu_ref, hval_ref, hglu_ref):
    ah = hval_ref[...].astype(F32) * _sig(hglu_ref[...].astype(F32))
    s_ref[0, 0:HALO, :] = jnp.where(i > 0, ah, 0.0)

    def fill(rb, carry):
        rows = _rows(rb)
        s_ref[0, pl.ds(HALO + rb * RB, RB), :] = _ld(val_ref, rows) * _sig(_ld(glu_ref, rows))
        return carry
    _blocks(TM // RB, fill)


def _halo_prev(seg):
    hb = TM // HALO
    return pl.BlockSpec((HALO, D), lambda i: (jnp.maximum(i * hb - 1, 0), seg))


def _branch_a_fwd(p, cw, conv_b, lg, lb, wq, token):
    T = p.shape[0]

    def body(val_ref, glu_ref, z_ref, hval_ref, hglu_ref, cw_ref, cb_ref, lg_ref, lb_ref, w_ref, token_ref,
             yc_ref, ya_ref, s_ref, yp_ref):
        del token_ref
        i = pl.program_id(0)
        _fill_a_ext(i, s_ref, val_ref, glu_ref, hval_ref, hglu_ref)
        _shift_copies(s_ref)

        def conv(rb, carry):
            r0 = rb * RBC
            accs = [jnp.broadcast_to(cb_ref[...], (8, D))] * (RBC // 8)
            for k in range(KW):
                o = 2 + k
                w = cw_ref[k]
                accs = [acc + w * s_ref[o % 8, pl.ds(r0 + 8 * (o // 8 + g), 8), :]
                        for g, acc in enumerate(accs)]
            for g, acc in enumerate(accs):
                yc_ref[pl.ds(r0 + 8 * g, 8), :] = acc
            return carry
        for rb in range(TM // RBC):
            conv(rb, 0)

        def post(rb, carry):
            rows = _rows(rb)
            n, _ = _ln_stats(yc_ref[rows, :])
            l = n * lg_ref[...] + lb_ref[...]
            z = _ld(z_ref, rows)
            yp_ref[rows, :] = ((l * _sig(l)) * (z * _sig(z))).astype(BF16)
            return carry
        _blocks(TM // RB, post)
        ya_ref[...] = jnp.dot(yp_ref[...], w_ref[...], preferred_element_type=F32)

    tile = lambda seg: pl.BlockSpec((TM, D), lambda i: (i, seg))
    row = pl.BlockSpec((1, D), lambda i: (0, 0))
    return pl.pallas_call(
        body, name="branch_a_fwd", grid=(T // TM,),
        out_shape=(jax.ShapeDtypeStruct((T, D), F32), jax.ShapeDtypeStruct((T, D), F32)),
        in_specs=[tile(0), tile(1), tile(2), _halo_prev(0), _halo_prev(1),
                  pl.BlockSpec((KW, 8, D), lambda i: (0, 0, 0)), row, row, row,
                  pl.BlockSpec((None, D, D), lambda i: (W_CO, 0, 0)), pl.BlockSpec((8, 128), lambda i: (0, 0))],
        out_specs=(pl.BlockSpec((TM, D), lambda i: (i, 0)), pl.BlockSpec((TM, D), lambda i: (i, 0))),
        scratch_shapes=[VMEM((8, EXT, D), F32), VMEM((TM, D), BF16)],
        compiler_params=_cp(("parallel",)),
    )(p, p, p, p, p, cw, conv_b, lg, lb, wq, token)


def _masked_ws(ws_ref, wt_ref):
    tri = lax.broadcasted_iota(jnp.int32, (CHUNK, CHUNK), 0) >= lax.broadcasted_iota(jnp.int32, (CHUNK, CHUNK), 1)
    for h in range(HEADS):
        wt_ref[h] = jnp.where(tri, ws_ref[h], 0.0).astype(BF16)


def _sgu_mix(wt_ref, vl_ref, bst_ref, s_ref):
    for h in range(HEADS):
        cs = slice(h * HD, (h + 1) * HD)
        s = jnp.dot(wt_ref[h], _chunks_on_lanes(vl_ref, cs), preferred_element_type=F32) + bst_ref[:, h:h + 1]
        for ck in range(TM // CHUNK):
            s_ref[ck * CHUNK:(ck + 1) * CHUNK, cs] = s[:, ck * HD:(ck + 1) * HD]


def _chunks_on_lanes(ref, cs):
    return jnp.concatenate([ref[ck * CHUNK:(ck + 1) * CHUNK, cs] for ck in range(TM // CHUNK)], axis=1)


def _branch_b_fwd(p, ws, bst, sg, sb, wq):
    T = p.shape[0]

    def body(pu_ref, pv_ref, pz_ref, ws_ref, bst_ref, sg_ref, sb_ref, w_ref, yb_ref,
             wt_ref, vl_ref, t_ref, s_ref, yp_ref):
        _masked_ws(ws_ref, wt_ref)

        def pre(rb, carry):
            rows = _rows(rb)
            vg, _ = _gelu_parts(_ld(pv_ref, rows))
            vn, _ = _ln_stats(vg)
            vl_ref[rows, :] = (vn * sg_ref[...] + sb_ref[...]).astype(BF16)
            u, _ = _gelu_parts(_ld(pu_ref, rows))
            z = _ld(pz_ref, rows)
            t_ref[rows, :] = u * (z * _sig(z))
            return carry
        _blocks(TM // RB, pre)
        _sgu_mix(wt_ref, vl_ref, bst_ref, s_ref)

        def post(rb, carry):
            rows = _rows(rb)
            yp_ref[rows, :] = (t_ref[rows, :] * s_ref[rows, :]).astype(BF16)
            return carry
        _blocks(TM // RB, post)
        yb_ref[...] = jnp.dot(yp_ref[...], w_ref[...], preferred_element_type=F32)

    tile = lambda seg: pl.BlockSpec((TM, D), lambda i: (i, seg))
    row = pl.BlockSpec((1, D), lambda i: (0, 0))
    return pl.pallas_call(
        body, name="branch_b_fwd", grid=(T // TM,),
        out_shape=jax.ShapeDtypeStruct((T, D), F32),
        in_specs=[tile(3), tile(4), tile(5), pl.BlockSpec((HEADS, CHUNK, CHUNK), lambda i: (0, 0, 0)),
                  pl.BlockSpec((CHUNK, HEADS), lambda i: (0, 0)), row, row,
                  pl.BlockSpec((None, D, D), lambda i: (W_SO, 0, 0))],
        out_specs=pl.BlockSpec((TM, D), lambda i: (i, 0)),
        scratch_shapes=[VMEM((HEADS, CHUNK, CHUNK), BF16), VMEM((TM, D), BF16), VMEM((TM, D), F32),
                        VMEM((TM, D), F32), VMEM((TM, D), BF16)],
        compiler_params=_cp(("parallel",)),
    )(p, p, p, ws, bst, sg, sb, wq)


def _merge_loss(p, ya, yb, x, tgt, mod, g_final, wq):
    T = x.shape[0]
    nh = 2 if T % (2 * TM) == 0 else 1
    tm2 = nh * TM
    nt = T // tm2
    nbh = TM // RB

    def body(ga_ref, gb_ref, ya_ref, yb_ref, x_ref, t_ref, mod_ref, gf_ref, w_ref,
             dx2_ref, dya_ref, dyb_ref, dp_ref, gwo_ref, st_ref, *scratch):
        mrg, out, dout, dm = (scratch[k * nh:(k + 1) * nh] for k in range(4))
        gw_acc = scratch[4 * nh]
        i = pl.program_id(0)

        @pl.when(i == 0)
        def _():
            gw_acc[...] = jnp.zeros_like(gw_acc)
            st_ref[...] = jnp.zeros_like(st_ref)

        def merge(h):
            def f(rb, carry):
                rows, loc = _rows(h * nbh + rb), _rows(rb)
                mrg[h][loc, :] = (_sig(_ld(ga_ref, rows)) * ya_ref[rows, :]
                                  + _sig(_ld(gb_ref, rows)) * yb_ref[rows, :]).astype(BF16)
                return carry
            _blocks(nbh, f)

        def head(h, carry):
            def f(rb, carry):
                loss, gg, dg = carry
                rows, loc = _rows(h * nbh + rb), _rows(rb)
                gate = mod_ref[2:3, :]
                gf = gf_ref[...]
                o = out[h][loc, :]
                x2 = x_ref[rows, :] + gate * o
                r2 = lax.rsqrt(jnp.mean(x2 * x2, axis=-1, keepdims=True) + EPS)
                x2n = x2 * r2
                diff = x2n * gf - t_ref[rows, :]
                dy = diff * (1.0 / D)
                dx2n = dy * gf
                dx2 = r2 * (dx2n - x2n * jnp.mean(dx2n * x2n, axis=-1, keepdims=True))
                dx2_ref[rows, :] = dx2
                dout[h][loc, :] = (dx2 * gate).astype(BF16)
                return loss + _fold8(diff * diff), gg + _fold8(dy * x2n), dg + _fold8(dx2 * o)
            return _blocks(nbh, f, carry)

        def split(h):
            def f(rb, carry):
                rows, loc = _rows(h * nbh + rb), _rows(rb)
                d = dm[h][loc, :]
                sa = _sig(_ld(ga_ref, rows))
                sb = _sig(_ld(gb_ref, rows))
                dya_ref[rows, :] = (d * sa).astype(BF16)
                dyb_ref[rows, :] = (d * sb).astype(BF16)
                dp_ref[0, rows, :] = (d * ya_ref[rows, :] * (sa * (1.0 - sa))).astype(BF16)
                dp_ref[1, rows, :] = (d * yb_ref[rows, :] * (sb * (1.0 - sb))).astype(BF16)
                return carry
            _blocks(nbh, f)

        for h in range(nh):
            merge(h)
        for h in range(nh):
            out[h][...] = jnp.dot(mrg[h][...], w_ref[...], preferred_element_type=F32)
        zero = jnp.zeros((8, D), F32)
        carry = (zero, zero, zero)
        for h in range(nh):
            carry = head(h, carry)
        loss, gg, dg = carry
        st_ref[0] = st_ref[0] + loss * (0.5 / D)
        st_ref[1] = st_ref[1] + gg
        st_ref[2] = st_ref[2] + dg
        for h in range(nh):
            dm[h][...] = lax.dot_general(dout[h][...], w_ref[...], NT, preferred_element_type=F32)
            gw_acc[...] += lax.dot_general(mrg[h][...], dout[h][...], TN, preferred_element_type=F32)
        for h in range(nh):
            split(h)

        @pl.when(i == nt - 1)
        def _():
            gwo_ref[...] = gw_acc[...].astype(BF16)

    tile = pl.BlockSpec((tm2, D), lambda i: (i, 0))
    halves = lambda dt: [VMEM((TM, D), dt)] * nh
    return pl.pallas_call(
        body, name="merge_loss", grid=(nt,),
        out_shape=(jax.ShapeDtypeStruct((T, D), F32), jax.ShapeDtypeStruct((T, D), BF16), jax.ShapeDtypeStruct((T, D), BF16),
                   jax.ShapeDtypeStruct((NSEG, T, D), BF16), jax.ShapeDtypeStruct((D, D), BF16),
                   jax.ShapeDtypeStruct((3, 8, D), F32)),
        in_specs=[pl.BlockSpec((tm2, D), lambda i: (i, 6)), pl.BlockSpec((tm2, D), lambda i: (i, 7)), tile, tile, tile, tile,
                  pl.BlockSpec((3, D), lambda i: (0, 0)), pl.BlockSpec((1, D), lambda i: (0, 0)),
                  pl.BlockSpec((None, D, D), lambda i: (W_O, 0, 0))],
        out_specs=(tile, tile, tile, pl.BlockSpec((2, tm2, D), lambda i: (3, i, 0)),
                   pl.BlockSpec((D, D), lambda i: (0, 0)), pl.BlockSpec((3, 8, D), lambda i: (0, 0, 0))),
        scratch_shapes=halves(BF16) + halves(F32) + halves(BF16) + halves(F32) + [VMEM((D, D), F32)],
        compiler_params=_cp(("arbitrary",)),
    )(p, p, ya, yb, x, tgt, mod, g_final, wq)


def _branch_b_bwd(dp, dyb, p, ws, bst, sg, sb, wq):
    T = p.shape[0]
    nt = T // TM

    def body(dp_in, dyb_ref, pu_ref, pv_ref, pz_ref, ws_ref, bst_ref, sg_ref, sb_ref, w_ref,
             dp_ref, gw_ref, gws_ref, gbs_ref, st_ref,
             wt_ref, d_s, vl_s, vn_s, gpv_s, rs_s, s_s, ds_s, ds32_s, yp_s, dvl_s, gw_acc):
        del dp_in
        i = pl.program_id(0)

        @pl.when(i == 0)
        def _():
            gw_acc[...] = jnp.zeros_like(gw_acc)
            gws_ref[...] = jnp.zeros_like(gws_ref)
            gbs_ref[...] = jnp.zeros_like(gbs_ref)
            st_ref[...] = jnp.zeros_like(st_ref)
            _masked_ws(ws_ref, wt_ref)

        d_s[...] = lax.dot_general(dyb_ref[...], w_ref[...], NT, preferred_element_type=F32)

        def pre(rb, carry):
            rows = _rows(rb)
            vg, gpv = _gelu_parts(_ld(pv_ref, rows))
            vn, rs = _ln_stats(vg)
            vl_s[rows, :] = (vn * sg_ref[...] + sb_ref[...]).astype(BF16)
            vn_s[rows, :] = vn
            gpv_s[rows, :] = gpv
            rs_s[rows, :] = rs
            return carry
        _blocks(TM // RB, pre)
        _sgu_mix(wt_ref, vl_s, bst_ref, s_s)

        def mid(rb, carry):
            rows = _rows(rb)
            u, gpu = _gelu_parts(_ld(pu_ref, rows))
            z = _ld(pz_ref, rows)
            sz = _sig(z)
            siluz = z * sz
            d = d_s[rows, :]
            s = s_s[rows, :]
            t = u * siluz
            yp_s[rows, :] = (t * s).astype(BF16)
            dp_ref[0, rows, :] = (d * s * siluz * gpu).astype(BF16)
            ds = d * t
            ds32_s[rows, :] = ds
            ds_s[rows, :] = ds.astype(BF16)
            dp_ref[2, rows, :] = (d * u * s * (sz * (1.0 + z * (1.0 - sz)))).astype(BF16)
            return carry
        _blocks(TM // RB, mid)

        for h in range(HEADS):
            cs = slice(h * HD, (h + 1) * HD)
            dsh = _chunks_on_lanes(ds_s, cs)
            dvl = lax.dot_general(wt_ref[h], dsh, TN, preferred_element_type=F32)
            gbs = gbs_ref[h]
            for ck in range(TM // CHUNK):
                r = slice(ck * CHUNK, (ck + 1) * CHUNK)
                dvl_s[r, cs] = dvl[:, ck * HD:(ck + 1) * HD]
                gbs = gbs + ds32_s[r, cs]
            gbs_ref[h] = gbs
            gws_ref[h] += lax.dot_general(dsh, _chunks_on_lanes(vl_s, cs), NT, preferred_element_type=F32)

        def post(rb, carry):
            g_sg, g_sb = carry
            rows = _rows(rb)
            dvl = dvl_s[rows, :]
            vn = vn_s[rows, :]
            dvg = _ln_bwd(dvl * sg_ref[...], vn, rs_s[rows, :])
            dp_ref[1, rows, :] = (dvg * gpv_s[rows, :]).astype(BF16)
            return g_sg + _fold8(dvl * vn), g_sb + _fold8(dvl)
        zero = jnp.zeros((8, D), F32)
        g_sg, g_sb = _blocks(TM // RB, post, (zero, zero))
        st_ref[0] = st_ref[0] + g_sg
        st_ref[1] = st_ref[1] + g_sb

        gw_acc[...] += lax.dot_general(yp_s[...], dyb_ref[...], TN, preferred_element_type=F32)

        @pl.when(i == nt - 1)
        def _():
            gw_ref[...] = gw_acc[...].astype(BF16)

    tile = lambda seg: pl.BlockSpec((TM, D), lambda i: (i, seg))
    row = pl.BlockSpec((1, D), lambda i: (0, 0))
    hh = pl.BlockSpec((HEADS, CHUNK, CHUNK), lambda i: (0, 0, 0))
    return pl.pallas_call(
        body, name="branch_b_bwd", grid=(nt,),
        out_shape=(jax.ShapeDtypeStruct((NSEG, T, D), BF16), jax.ShapeDtypeStruct((D, D), BF16),
                   jax.ShapeDtypeStruct((HEADS, CHUNK, CHUNK), F32), jax.ShapeDtypeStruct((HEADS, CHUNK, HD), F32),
                   jax.ShapeDtypeStruct((2, 8, D), F32)),
        in_specs=[pl.BlockSpec(memory_space=ANY), pl.BlockSpec((TM, D), lambda i: (i, 0)), tile(3), tile(4), tile(5),
                  hh, pl.BlockSpec((CHUNK, HEADS), lambda i: (0, 0)), row, row,
                  pl.BlockSpec((None, D, D), lambda i: (W_SO, 0, 0))],
        out_specs=(pl.BlockSpec((3, TM, D), lambda i: (1, i, 0)), pl.BlockSpec((D, D), lambda i: (0, 0)), hh, hh,
                   pl.BlockSpec((2, 8, D), lambda i: (0, 0, 0))),
        scratch_shapes=[VMEM((HEADS, CHUNK, CHUNK), BF16), VMEM((TM, D), F32), VMEM((TM, D), BF16), VMEM((TM, D), F32),
                        VMEM((TM, D), F32), VMEM((TM, 1), F32), VMEM((TM, D), F32), VMEM((TM, D), BF16), VMEM((TM, D), F32),
                        VMEM((TM, D), BF16), VMEM((TM, D), F32), VMEM((D, D), F32)],
        input_output_aliases={0: 0},
        compiler_params=_cp(("arbitrary",)),
    )(dp, dyb, p, p, p, ws, bst, sg, sb, wq)


def _branch_a_bwd1(dp, dya, yc, p, lg, lb, wq):
    T = p.shape[0]
    nt = T // TM

    def body(dp_in, dya_ref, yc_ref, z_ref, lg_ref, lb_ref, w_ref, dp_ref, dyc_ref, gw_ref, st_ref,
             d_s, yp_s, gw_acc):
        del dp_in
        i = pl.program_id(0)

        @pl.when(i == 0)
        def _():
            gw_acc[...] = jnp.zeros_like(gw_acc)
            st_ref[...] = jnp.zeros_like(st_ref)

        d_s[...] = lax.dot_general(dya_ref[...], w_ref[...], NT, preferred_element_type=F32)

        def blk(rb, carry):
            g_lg, g_lb, g_cb = carry
            rows = _rows(rb)
            n, rs = _ln_stats(yc_ref[rows, :])
            l = n * lg_ref[...] + lb_ref[...]
            sgl = _sig(l)
            sl = l * sgl
            z = _ld(z_ref, rows)
            sz = _sig(z)
            siluz = z * sz
            d = d_s[rows, :]
            yp_s[rows, :] = (sl * siluz).astype(BF16)
            dp_ref[rows, :] = (d * sl * (sz * (1.0 + z * (1.0 - sz)))).astype(BF16)
            dl = (d * siluz) * (sgl * (1.0 + l * (1.0 - sgl)))
            dyc = _ln_bwd(dl * lg_ref[...], n, rs)
            dyc_ref[rows, :] = dyc
            return g_lg + _fold8(dl * n), g_lb + _fold8(dl), g_cb + _fold8(dyc)
        zero = jnp.zeros((8, D), F32)
        g_lg, g_lb, g_cb = _blocks(TM // RB, blk, (zero, zero, zero))
        st_ref[0] = st_ref[0] + g_lg
        st_ref[1] = st_ref[1] + g_lb
        st_ref[2] = st_ref[2] + g_cb

        gw_acc[...] += lax.dot_general(yp_s[...], dya_ref[...], TN, preferred_element_type=F32)

        @pl.when(i == nt - 1)
        def _():
            gw_ref[...] = gw_acc[...].astype(BF16)

    tile = pl.BlockSpec((TM, D), lambda i: (i, 0))
    row = pl.BlockSpec((1, D), lambda i: (0, 0))
    return pl.pallas_call(
        body, name="branch_a_bwd1", grid=(nt,),
        out_shape=(jax.ShapeDtypeStruct((NSEG, T, D), BF16), jax.ShapeDtypeStruct((T, D), F32),
                   jax.ShapeDtypeStruct((D, D), BF16), jax.ShapeDtypeStruct((3, 8, D), F32)),
        in_specs=[pl.BlockSpec(memory_space=ANY), tile, tile, pl.BlockSpec((TM, D), lambda i: (i, 2)), row, row,
                  pl.BlockSpec((None, D, D), lambda i: (0, 0, 0))],
        out_specs=(pl.BlockSpec((None, TM, D), lambda i: (2, i, 0)), tile, pl.BlockSpec((D, D), lambda i: (0, 0)),
                   pl.BlockSpec((3, 8, D), lambda i: (0, 0, 0))),
        scratch_shapes=[VMEM((TM, D), F32), VMEM((TM, D), BF16), VMEM((D, D), F32)],
        input_output_aliases={0: 0},
        compiler_params=_cp(("arbitrary",)),
    )(dp, dya, yc, p, lg, lb, wq)


def _branch_a_bwd2(dp, dyc, p, cw, gw_co, gw_so, gw_o, early):
    T = p.shape[0]
    nt = T // TM
    hb = TM // HALO

    def body(dp_in, dyc_ref, hdyc_ref, val_ref, glu_ref, cw_ref, gco_ref, gso_ref, go_ref, e_ref,
             dp_ref, gcw_ref, rq_ref, eg_ref, a_s, sd_ref, da_s, x_send, x_recv, x_loc, ag_send, ag_recv, ag_loc):
        del dp_in
        i = pl.program_id(0)
        start_exchange, wait_exchange = _owner_exchange((gco_ref, gso_ref, go_ref), rq_ref, x_send, x_recv, x_loc)
        ag_start, ag_relay, ag_finish = _allgather_phases([e_ref], lambda a, slot: eg_ref.at[slot],
                                                          ag_send, ag_recv, ag_loc)

        @pl.when(i == 0)
        def _():
            gcw_ref[...] = jnp.zeros_like(gcw_ref)
            ag_start()
            start_exchange()

        @pl.when(i == nt // 2)
        def _():
            ag_relay()

        def fill(rb, carry):
            rows = _rows(rb)
            a_s[rows, :] = _ld(val_ref, rows) * _sig(_ld(glu_ref, rows))
            return carry
        _blocks(TM // RB, fill)
        sd_ref[0, 0:TM, :] = dyc_ref[...]
        sd_ref[0, TM:EXT, :] = jnp.where(i < nt - 1, hdyc_ref[...], 0.0)
        _shift_copies(sd_ref)

        def conv_t(rb, carry):
            r0 = rb * RBC
            accs = [jnp.zeros((8, D), F32)] * (RBC // 8)
            for k in range(KW):
                o = KW - 1 - k
                w = cw_ref[k]
                accs = [acc + w * sd_ref[o % 8, pl.ds(r0 + 8 * (o // 8 + g), 8), :]
                        for g, acc in enumerate(accs)]
            for g, acc in enumerate(accs):
                da_s[pl.ds(r0 + 8 * g, 8), :] = acc
            return carry
        for rb in range(TM // RBC):
            conv_t(rb, 0)

        for k0 in range(0, KW, TAPG):
            taps = list(range(k0, min(k0 + TAPG, KW)))

            def tap_group(rb, accs, taps=taps):
                for u in range(2):
                    r0 = rb * 32 + 16 * u
                    a = a_s[pl.ds(r0, 16), :]
                    out = []
                    for k, acc in zip(taps, accs):
                        o = KW - 1 - k
                        d = sd_ref[o % 8, pl.ds(r0 + 8 * (o // 8), 16), :]
                        out.append(acc + _fold8(a * d))
                    accs = tuple(out)
                return accs
            sums = tuple(jnp.zeros((8, D), F32) for _ in taps)
            for rb in range(TM // 32):
                sums = tap_group(rb, sums)
            for k, s in zip(taps, sums):
                gcw_ref[k] = gcw_ref[k] + s

        def glu_b(rb, carry):
            rows = _rows(rb)
            da = da_s[rows, :]
            sg = _sig(_ld(glu_ref, rows))
            dp_ref[0, rows, :] = (da * sg).astype(BF16)
            dp_ref[1, rows, :] = (da * _ld(val_ref, rows) * (sg * (1.0 - sg))).astype(BF16)
            return carry
        _blocks(TM // RB, glu_b)

        @pl.when(i == nt - 1)
        def _():
            wait_exchange()
            ag_finish()

    tile = lambda seg: pl.BlockSpec((TM, D), lambda i: (i, seg))
    hbm = pl.BlockSpec(memory_space=ANY)
    return pl.pallas_call(
        body, name="branch_a_bwd2", grid=(nt,),
        out_shape=(jax.ShapeDtypeStruct((NSEG, T, D), BF16), jax.ShapeDtypeStruct((32, 8, D), F32),
                   jax.ShapeDtypeStruct((NDEV, 3, 128, D), BF16), jax.ShapeDtypeStruct((NDEV, NEARLY, D), F32)),
        in_specs=[hbm, pl.BlockSpec((TM, D), lambda i: (i, 0)),
                  pl.BlockSpec((HALO, D), lambda i: (jnp.minimum((i + 1) * hb, nt * hb - 1), 0)),
                  tile(0), tile(1), pl.BlockSpec((KW, 8, D), lambda i: (0, 0, 0)),
                  hbm, hbm, hbm, hbm],
        out_specs=(pl.BlockSpec((2, TM, D), lambda i: (0, i, 0)), pl.BlockSpec((32, 8, D), lambda i: (0, 0, 0)), hbm, hbm),
        scratch_shapes=[VMEM((TM, D), F32), VMEM((8, EXT, D), F32), VMEM((TM, D), F32),
                        pltpu.SemaphoreType.DMA((3, 7)), pltpu.SemaphoreType.DMA((3, 7)), pltpu.SemaphoreType.DMA((3,)),
                        pltpu.SemaphoreType.DMA((1, 7)), pltpu.SemaphoreType.DMA((1, 7)), pltpu.SemaphoreType.DMA((1,))],
        input_output_aliases={0: 0},
        compiler_params=_cp(("arbitrary",)),
    )(dp, dyc, dyc, p, p, cw, gw_co, gw_so, gw_o, early)


def _in_bwd_dx(dp, wing, x, dx2, mod, g_pre, token):
    T = x.shape[0]
    nt = T // TM

    def body(dp_ref, w_ref, x_ref, dx2_ref, mod_ref, g_ref, token_ref, gx_ref, st_ref, dh_s):
        del token_ref
        i = pl.program_id(0)

        @pl.when(i == 0)
        def _():
            st_ref[...] = jnp.zeros_like(st_ref)

        dh_s[...] = lax.dot_general(dp_ref[0], w_ref[0], NT, preferred_element_type=F32)
        for j in range(1, NSEG):
            dh_s[...] += lax.dot_general(dp_ref[j], w_ref[j], NT, preferred_element_type=F32)

        def blk(rb, carry):
            d_sh, d_sc, g_g = carry
            rows = _rows(rb)
            xv = x_ref[rows, :]
            r = lax.rsqrt(jnp.mean(xv * xv, axis=-1, keepdims=True) + EPS)
            xn = xv * r
            g = g_ref[...]
            hpre = xn * g
            dh = dh_s[rows, :]
            dhp = dh * (1.0 + mod_ref[1:2, :])
            dxn = dhp * g
            gx_ref[rows, :] = dx2_ref[rows, :] + r * (dxn - xn * jnp.mean(dxn * xn, axis=-1, keepdims=True))
            return d_sh + _fold8(dh), d_sc + _fold8(dh * hpre), g_g + _fold8(dhp * xn)
        zero = jnp.zeros((8, D), F32)
        d_sh, d_sc, g_g = _blocks(TM // RB, blk, (zero, zero, zero))
        st_ref[0] = st_ref[0] + d_sh
        st_ref[1] = st_ref[1] + d_sc
        st_ref[2] = st_ref[2] + g_g

    tile = pl.BlockSpec((TM, D), lambda i: (i, 0))
    return pl.pallas_call(
        body, name="in_bwd_dx", grid=(nt,),
        out_shape=(jax.ShapeDtypeStruct((T, D), F32), jax.ShapeDtypeStruct((3, 8, D), F32)),
        in_specs=[pl.BlockSpec((NSEG, TM, D), lambda i: (0, i, 0)),
                  pl.BlockSpec((NSEG, D, D), lambda i: (0, 0, 0), pipeline_mode=pl.Buffered(1)),
                  tile, tile, pl.BlockSpec((3, D), lambda i: (0, 0)), pl.BlockSpec((1, D), lambda i: (0, 0)),
                  pl.BlockSpec((8, 128), lambda i: (0, 0))],
        out_specs=(tile, pl.BlockSpec((3, 8, D), lambda i: (0, 0, 0))),
        scratch_shapes=[VMEM((TM, D), F32)],
        compiler_params=_cp(("arbitrary",), vmem_mb=56),
    )(dp, wing, x, dx2, mod, g_pre, token)


RS_ORDER = (5, 4, 3, 2, 7, 6, 1, 0)
RS_HALF = D // 2


def _in_bwd_dw(ht, dp, order):
    T = ht.shape[1]
    tmw = min(4 * TMB, T)
    nb = T // tmw
    nu = 2 * NSEG

    def body(order_ref, ht_ref, dp_ref, csum_ref, g_ref, acc, sbuf, rsib, send_sems, recv_sems, out_sems):
        j = pl.program_id(0)
        i = pl.program_id(1)
        x, y, c = _me()
        sibling = (x, y, 1 - c)

        def to_sibling(s):
            return pltpu.make_async_remote_copy(src_ref=sbuf.at[s], dst_ref=rsib.at[s], send_sem=send_sems.at[s],
                                                recv_sem=recv_sems.at[s], device_id=sibling, device_id_type=MESH)

        def to_csum(t):
            cols = pl.ds((t % 2) * RS_HALF, RS_HALF)
            return pltpu.make_async_copy(sbuf.at[8 + t], csum_ref.at[t // 2, :, cols], out_sems.at[t])

        @pl.when(i == 0)
        def _():
            acc[...] = jnp.zeros_like(acc)
        acc[...] += jnp.dot(ht_ref[...], dp_ref[...], preferred_element_type=F32)

        for jj in range(nu):
            chip, half, mine = jj // 4, (jj % 4) // 2, jj % 2
            s = 2 * chip + half

            @pl.when((j == jj) & (i == nb - 1))
            def _(chip=chip, half=half, mine=mine, s=s):
                if not mine:
                    sbuf[s] = acc[...].astype(BF16)
                    to_sibling(s).start()
                elif chip < 3:
                    to_sibling(s).wait_recv()
                    sbuf[8 + s] = (acc[...] + rsib[s].astype(F32)).astype(BF16)
                    to_csum(s).start()
                else:
                    to_sibling(s).wait_recv()
                    g_ref[:, half * RS_HALF:(half + 1) * RS_HALF] = acc[...] + rsib[s].astype(F32)

        @pl.when((j == nu - 1) & (i == nb - 1))
        def _():
            for s in range(8):
                to_sibling(s).wait_send()
            for t in range(6):
                to_csum(t).wait()

    hbm = pl.BlockSpec(memory_space=ANY)
    return pl.pallas_call(
        body, name="in_bwd_dw",
        out_shape=(jax.ShapeDtypeStruct((3, D, D), BF16), jax.ShapeDtypeStruct((D, D), F32)),
        grid_spec=pltpu.PrefetchScalarGridSpec(
            num_scalar_prefetch=1, grid=(nu, nb),
            in_specs=[pl.BlockSpec((D, tmw), lambda j, i, o: (0, i)),
                      pl.BlockSpec((None, tmw, RS_HALF), lambda j, i, o: (o[2 * (j // 4) + j % 2], i, (j % 4) // 2))],
            out_specs=(hbm, pl.BlockSpec((D, D), lambda j, i, o: (0, 0))),
            scratch_shapes=[VMEM((D, RS_HALF), F32), VMEM((14, D, RS_HALF), BF16), VMEM((8, D, RS_HALF), BF16),
                            pltpu.SemaphoreType.DMA((8,)), pltpu.SemaphoreType.DMA((8,)), pltpu.SemaphoreType.DMA((6,))]),
        compiler_params=_cp(("arbitrary", "arbitrary"), vmem_mb=56),
    )(order, ht, dp)


def _gather_late(late):
    def body(sm_ref, smg_ref, ag_send, ag_recv, ag_loc):
        _allgather([sm_ref], lambda a, slot: smg_ref.at[slot], ag_send, ag_recv, ag_loc)

    hbm = pl.BlockSpec(memory_space=ANY)
    return pl.pallas_call(
        body, name="gather_late",
        out_shape=jax.ShapeDtypeStruct((NDEV, NLATE, D), F32),
        in_specs=[hbm], out_specs=hbm,
        scratch_shapes=[pltpu.SemaphoreType.DMA((1, 7)), pltpu.SemaphoreType.DMA((1, 7)), pltpu.SemaphoreType.DMA((1,))],
        compiler_params=_cp(vmem_mb=32),
    )(late)


RS_CHIPS = ((1, 0), (0, 1), (1, 1))


def _rs_peer(q):
    x, y, c = _me()
    fx, fy = RS_CHIPS[q]
    return (1 - x if fx else x, 1 - y if fy else y, c)


def _rs_start(csum):
    def body(csum_ref, land_ref, send_sems, recv_sems, csum_thru, land_thru, token):
        barrier = pltpu.get_barrier_semaphore()
        for q in range(3):
            pl.semaphore_signal(barrier, inc=1, device_id=_rs_peer(q), device_id_type=MESH)
        pl.semaphore_wait(barrier, 3)
        for q in range(3):
            pltpu.make_async_remote_copy(src_ref=csum_ref.at[q], dst_ref=land_ref.at[q], send_sem=send_sems.at[q],
                                         recv_sem=recv_sems.at[q], device_id=_rs_peer(q), device_id_type=MESH).start()
        token[...] = jnp.zeros_like(token)

    hbm = pl.BlockSpec(memory_space=pltpu.HBM)
    sem = pl.BlockSpec(memory_space=pltpu.SEMAPHORE)
    land = pltpu.with_memory_space_constraint(lax.empty(csum.shape, csum.dtype), pltpu.HBM)
    return pl.pallas_call(
        body, name="rs_start",
        out_shape=(pltpu.SemaphoreType.DMA((3,)), pltpu.SemaphoreType.DMA((3,)), pltpu.HBM(csum.shape, csum.dtype),
                   pltpu.HBM(csum.shape, csum.dtype), jax.ShapeDtypeStruct((8, 128), F32)),
        in_specs=(hbm, hbm), out_specs=(sem, sem, hbm, hbm, pl.BlockSpec(memory_space=VMEM)),
        input_output_aliases={0: 2, 1: 3},
        compiler_params=pltpu.CompilerParams(has_side_effects=pltpu.SideEffectType.DATAFLOW_SIDE_EFFECTING, collective_id=1),
    )(pltpu.with_memory_space_constraint(csum, pltpu.HBM), land)


def _rs_wait(send_sems, recv_sems, csum_thru, land_thru, after):
    def body(csum_ref, land_ref, send_sems, recv_sems, after_ref, csum_dead, got_ref):
        del after_ref, csum_dead, got_ref
        for q in range(3):
            cp = pltpu.make_async_remote_copy(src_ref=csum_ref.at[q], dst_ref=land_ref.at[q], send_sem=send_sems.at[q],
                                              recv_sem=recv_sems.at[q], device_id=_rs_peer(q), device_id_type=MESH)
            cp.wait_send()
            cp.wait_recv()

    hbm = pl.BlockSpec(memory_space=pltpu.HBM)
    sem = pl.BlockSpec(memory_space=pltpu.SEMAPHORE)
    return pl.pallas_call(
        body, name="rs_wait",
        out_shape=(pltpu.HBM(csum_thru.shape, csum_thru.dtype), pltpu.HBM(csum_thru.shape, csum_thru.dtype)),
        in_specs=(hbm, hbm, sem, sem, pl.BlockSpec(memory_space=ANY)), out_specs=(hbm, hbm),
        input_output_aliases={0: 0, 1: 1},
        compiler_params=pltpu.CompilerParams(has_side_effects=pltpu.SideEffectType.DATAFLOW_SIDE_EFFECTING),
    )(csum_thru, land_thru, send_sems, recv_sems, after)[1]


def _pack_early(st_a, st_b, st_m, gws, gbs):
    def body(sa_ref, sb_ref, sm_ref, gws_ref, gbs_ref, out_ref):
        out_ref[...] = jnp.zeros_like(out_ref)
        fold = lambda v: jnp.sum(v, axis=0, keepdims=True)
        out_ref[0:1, :] = fold(sm_ref[2])
        out_ref[1:2, :] = fold(sa_ref[2])
        out_ref[2:3, :] = fold(sa_ref[0])
        out_ref[3:4, :] = fold(sa_ref[1])
        out_ref[4:5, :] = fold(sb_ref[0])
        out_ref[5:6, :] = fold(sb_ref[1])
        out_ref[7:8, :] = fold(sm_ref[1])
        out_ref[8:9, :] = fold(sm_ref[0])
        tri = lax.broadcasted_iota(jnp.int32, (CHUNK, CHUNK), 0) >= lax.broadcasted_iota(jnp.int32, (CHUNK, CHUNK), 1)
        for h in range(HEADS):
            out_ref[16:16 + CHUNK, h * CHUNK:(h + 1) * CHUNK] = jnp.where(tri, gws_ref[h], 0.0)
            out_ref[6:7, h * CHUNK:(h + 1) * CHUNK] = fold(gbs_ref[h].T)

    vm = pl.BlockSpec(memory_space=VMEM)
    return pl.pallas_call(
        body, name="pack_early", out_shape=jax.ShapeDtypeStruct((NEARLY, D), F32),
        in_specs=[vm] * 5, out_specs=vm, compiler_params=_cp(vmem_mb=32),
    )(st_a, st_b, st_m, gws, gbs)


def _pack_late(st_x, gcw):
    def body(sx_ref, gcw_ref, out_ref):
        out_ref[...] = jnp.zeros_like(out_ref)
        fold = lambda v: jnp.sum(v, axis=0, keepdims=True)
        for r in range(3):
            out_ref[r:r + 1, :] = fold(sx_ref[r])
        for k in range(KW):
            out_ref[8 + k:9 + k, :] = fold(gcw_ref[k])

    vm = pl.BlockSpec(memory_space=VMEM)
    return pl.pallas_call(
        body, name="pack_late", out_shape=jax.ShapeDtypeStruct((NLATE, D), F32),
        in_specs=[vm] * 2, out_specs=vm, compiler_params=_cp(vmem_mb=32),
    )(st_x, gcw)


def _pack_params(name, b_ada, g_pre, conv_b, lg, lb, sg, sb, b_sgu, g_final, w_sgu_t):
    def body(ba_ref, *refs):
        rows, ws_ref, out_ref = refs[:8], refs[8], refs[9]
        out_ref[...] = jnp.zeros_like(out_ref)
        for r in range(3):
            out_ref[r:r + 1, :] = ba_ref[:, r * D:(r + 1) * D]
        for r, ref in enumerate(rows):
            out_ref[3 + r:4 + r, :] = ref[...]
        out_ref[ROW_WS:ROW_WS + CHUNK, :] = ws_ref[...]

    vm = pl.BlockSpec(memory_space=VMEM)
    return pl.pallas_call(
        body, name=name, out_shape=jax.ShapeDtypeStruct((NSMALL, D), F32),
        in_specs=[vm] * 10, out_specs=vm, compiler_params=_cp(vmem_mb=32),
    )(b_ada, g_pre, conv_b, lg, lb, sg, sb, b_sgu, g_final, w_sgu_t)


def _adam(w, g, m, v):
    m2 = ADAM_B1 * m + (1.0 - ADAM_B1) * g
    v2 = ADAM_B2 * v + (1.0 - ADAM_B2) * (g * g)
    m_hat = m2 / (1.0 - ADAM_B1 ** ADAM_STEP)
    v_hat = v2 / (1.0 - ADAM_B2 ** ADAM_STEP)
    delta = -ADAM_LR * (m_hat / (jnp.sqrt(v_hat) + ADAM_EPS) + ADAM_WD * w)
    return delta, m2, v2


def _small_finish(eg, lg, w, m, v):
    def body(eg_ref, lg_ref, w_ref, m_ref, v_ref, loss_ref, g_ref, d_ref, m2_ref, v2_ref):
        e = eg_ref[0]
        l = lg_ref[0]
        for s in range(1, NDEV):
            e = e + eg_ref[s]
            l = l + lg_ref[s]
        g_ref[...] = jnp.zeros_like(g_ref)
        g_ref[0:2, :] = l[0:2, :]
        g_ref[2:3, :] = e[0:1, :]
        g_ref[3:4, :] = l[2:3, :]
        g_ref[4:12, :] = e[1:9, :]
        g_ref[ROW_CW:ROW_CW + KW, :] = l[8:8 + KW, :]
        g_ref[ROW_WS:ROW_WS + CHUNK, :] = e[16:16 + CHUNK, :]
        loss_ref[...] = jnp.sum(e[8:9, :], axis=1, keepdims=True)
        d_ref[...], m2_ref[...], v2_ref[...] = _adam(w_ref[...], g_ref[...], m_ref[...], v_ref[...])

    vm = pl.BlockSpec(memory_space=VMEM)
    sd = jax.ShapeDtypeStruct((NSMALL, D), F32)
    return pl.pallas_call(
        body, name="small_finish", out_shape=(jax.ShapeDtypeStruct((1, 1), F32), sd, sd, sd, sd),
        in_specs=[vm] * 5, out_specs=(vm,) * 5, compiler_params=_cp(vmem_mb=32),
    )(eg, lg, w, m, v)


def _ada_grad_adam(ct, dm, w, m, v):
    def body(ct_ref, dm_ref, w_ref, m_ref, v_ref, g_ref, d_ref, m2_ref, v2_ref):
        g = ct_ref[:, 0:1] * dm_ref[0:1, :]
        for b in range(1, NDEV):
            g = g + ct_ref[:, b:b + 1] * dm_ref[b:b + 1, :]
        g_ref[...] = g
        d_ref[...], m2_ref[...], v2_ref[...] = _adam(w_ref[...], g, m_ref[...], v_ref[...])

    vm = pl.BlockSpec(memory_space=VMEM)
    sd = jax.ShapeDtypeStruct(w.shape, F32)
    return pl.pallas_call(
        body, name="ada_grad_adam", out_shape=(sd, sd, sd, sd),
        in_specs=[vm] * 5, out_specs=(vm,) * 4, compiler_params=_cp(vmem_mb=32),
    )(ct, dm, w, m, v)


def _adam_f32(name, g, w, m, v, rows=None):
    R, C = w.shape
    rows = R if rows is None else rows

    def body(g_ref, w_ref, m_ref, v_ref, d_ref, m2_ref, v2_ref):
        d_ref[...], m2_ref[...], v2_ref[...] = _adam(w_ref[...], g_ref[...], m_ref[...], v_ref[...])

    tile = pl.BlockSpec((rows, C), lambda i: (i, 0))
    sd = jax.ShapeDtypeStruct(w.shape, F32)
    return pl.pallas_call(
        body, name=name, grid=(R // rows,), out_shape=(sd, sd, sd), in_specs=[tile] * 4, out_specs=(tile,) * 3,
        compiler_params=_cp(("parallel",), vmem_mb=32),
    )(g, w, m, v)


def _adam_w_in(gown, got, w, m, v, rows=256):
    R, C = w.shape

    def body(go_ref, got_ref, w_ref, m_ref, v_ref, g_ref, d_ref, m2_ref, v2_ref):
        g = go_ref[...]
        for q in range(3):
            g = g + got_ref[q].astype(F32)
        g_ref[...] = g
        d_ref[...], m2_ref[...], v2_ref[...] = _adam(w_ref[...], g, m_ref[...], v_ref[...])

    tile = pl.BlockSpec((rows, C), lambda i: (i, 0))
    sd = jax.ShapeDtypeStruct((R, C), F32)
    return pl.pallas_call(
        body, name="adam_w_in", grid=(R // rows,), out_shape=(sd, sd, sd, sd),
        in_specs=[tile, pl.BlockSpec((3, rows, C), lambda i: (0, i, 0)), tile, tile, tile], out_specs=(tile,) * 4,
        compiler_params=_cp(("parallel",), vmem_mb=32),
    )(gown, got, w, m, v)


def _adam_reduce(name, recv, w, m, v, rows, recv_spec):
    R, C = w.shape

    def body(r_ref, w_ref, m_ref, v_ref, g_ref, d_ref, m2_ref, v2_ref):
        g = r_ref[0].astype(F32)
        for s in range(1, NDEV):
            g = g + r_ref[s].astype(F32)
        g_ref[...] = g
        d_ref[...], m2_ref[...], v2_ref[...] = _adam(w_ref[...], g, m_ref[...], v_ref[...])

    tile = pl.BlockSpec((rows, C), lambda i: (i, 0))
    sd = jax.ShapeDtypeStruct((R, C), F32)
    return pl.pallas_call(
        body, name=name, grid=(R // rows,), out_shape=(sd, sd, sd, sd),
        in_specs=[recv_spec, tile, tile, tile], out_specs=(tile,) * 4,
        compiler_params=_cp(("parallel",)),
    )(recv, w, m, v)


def kernel(x, c, w_ada, b_ada, g_pre, w_in, conv_w, conv_b, conv_ln_g, conv_ln_b, w_conv_out, sgu_ln_g, sgu_ln_b, w_sgu, b_sgu, w_sgu_out, w_o, g_final, loss_target, m_w_ada, m_b_ada, m_g_pre, m_w_in, m_conv_w, m_conv_b, m_conv_ln_g, m_conv_ln_b, m_w_conv_out, m_sgu_ln_g, m_sgu_ln_b, m_w_sgu, m_b_sgu, m_w_sgu_out, m_w_o, m_g_final, v_w_ada, v_b_ada, v_g_pre, v_w_in, v_conv_w, v_conv_b, v_conv_ln_g, v_conv_ln_b, v_w_conv_out, v_sgu_ln_g, v_sgu_ln_b, v_w_sgu, v_b_sgu, v_w_sgu_out, v_w_o, v_g_final):
    T = x.shape[1]
    assert T % TMB == 0 and x.shape[2] == D
    xs = x[0]
    tgt = loss_target[0]
    my = 4 * lax.axis_index("x") + 2 * lax.axis_index("y") + lax.axis_index("c")

    mod, cg = _ada_fwd(c, w_ada[0], b_ada)
    gf = g_final.reshape(1, D)
    bst = b_sgu[0].T

    h, ht = _prep_h(xs, mod, g_pre)
    ag_order = jnp.stack([_slot(*b) for b in _ag_blocks(lax.axis_index("x"), lax.axis_index("y"), lax.axis_index("c"))])
    p, wing, wcog, cwg, own2, land2 = _in_proj(h, w_in[0], ag_order.astype(jnp.int32),
                                               w_conv_out[0], w_sgu_out[0], w_o[0], conv_w[0])
    wco = wcog.reshape(1, D, D)
    cw = jnp.broadcast_to(jnp.transpose(cwg, (1, 0, 2)).reshape(KW, 1, D), (KW, 8, D))
    wq_send, wq_recv, own2_thru, land2_thru, wq_token = _wq_start(own2, land2)
    yc, ya = _branch_a_fwd(p, cw, conv_b, conv_ln_g, conv_ln_b, wco, wq_token)
    wq2 = _wq_wait(wq_send, wq_recv, own2_thru, land2_thru, yc).reshape(2, D, D)
    yb = _branch_b_fwd(p, w_sgu[0], bst, sgu_ln_g, sgu_ln_b, wq2)
    dx2, dya, dyb, dp, gw_o, st_m = _merge_loss(p, ya, yb, xs, tgt, mod, gf, wq2)

    dp, gw_so, gws, gbs, st_b = _branch_b_bwd(dp, dyb, p, w_sgu[0], bst, sgu_ln_g, sgu_ln_b, wq2)
    dp, dyc, gw_co, st_a = _branch_a_bwd1(dp, dya, yc, p, conv_ln_g, conv_ln_b, wco)
    early = _pack_early(st_a, st_b, st_m, gws, gbs)
    dp, gcw, rq, eg = _branch_a_bwd2(dp, dyc, p, cw, gw_co, gw_so, gw_o, early)
    csum, gown = _in_bwd_dw(ht, dp, my ^ jnp.array(RS_ORDER, jnp.int32))
    rs_send, rs_recv, csum_thru, land_thru, token = _rs_start(csum)
    grad_x, st_x = _in_bwd_dx(dp, wing, xs, dx2, mod, g_pre, token)
    got = _rs_wait(rs_send, rs_recv, csum_thru, land_thru, st_x)
    lg = _gather_late(_pack_late(st_x, gcw))

    def pack(name, b_ada_, g_pre_, conv_b_, lg_, lb_, sg_, sb_, b_sgu_, gfin_, w_sgu_):
        return _pack_params(name, b_ada_, g_pre_, conv_b_, lg_, lb_, sg_, sb_, b_sgu_.reshape(1, D), gfin_.reshape(1, D),
                            jnp.transpose(w_sgu_[0], (1, 0, 2)).reshape(CHUNK, D))

    pw = pack("pack_w", b_ada, g_pre, conv_b, conv_ln_g, conv_ln_b, sgu_ln_g, sgu_ln_b, b_sgu, g_final, w_sgu)
    pm = pack("pack_m", m_b_ada, m_g_pre, m_conv_b, m_conv_ln_g, m_conv_ln_b, m_sgu_ln_g, m_sgu_ln_b, m_b_sgu, m_g_final,
              m_w_sgu)
    pv = pack("pack_v", v_b_ada, v_g_pre, v_conv_b, v_conv_ln_g, v_conv_ln_b, v_sgu_ln_g, v_sgu_ln_b, v_b_sgu, v_g_final,
              v_w_sgu)
    loss11, sg_, sd_, sm_, sv_ = _small_finish(eg, lg, pw, pm, pv)

    def unpack(a):
        return dict(
            b_ada=a[0:3].reshape(1, 3 * D), g_pre=a[3:4], conv_b=a[4:5], conv_ln_g=a[5:6], conv_ln_b=a[6:7],
            sgu_ln_g=a[7:8], sgu_ln_b=a[8:9], b_sgu=a[9:10].reshape(1, HEADS, CHUNK), g_final=a[10],
            w_sgu=jnp.transpose(a[ROW_WS:ROW_WS + CHUNK].reshape(CHUNK, HEADS, CHUNK), (1, 0, 2))[None])
    small = [unpack(a) for a in (sg_, sd_, sm_, sv_)]

    g_cw = lax.dynamic_slice_in_dim(sg_[ROW_CW:ROW_CW + KW], my * 128, 128, axis=1)
    d_cw, m_cw, v_cw = _adam_f32("adam_conv_w", g_cw, conv_w[0], m_conv_w[0], v_conv_w[0])

    wcols = w_ada.shape[2]
    dm_all = jnp.concatenate([lg[:, 0], lg[:, 1], eg[:, 0]], axis=1)
    dm_mine = lax.dynamic_slice_in_dim(dm_all, my * wcols, wcols, axis=1)
    g_ada, d_ada, m_ada, v_ada = _ada_grad_adam(cg.T, dm_mine, w_ada[0], m_w_ada[0], v_w_ada[0])

    g_in, d_in, m_in, v_in = _adam_w_in(gown, got, w_in[0], m_w_in[0], v_w_in[0])
    big = {}
    for j, (nm, w_, m_, v_) in enumerate((("w_conv_out", w_conv_out, m_w_conv_out, v_w_conv_out),
                                          ("w_sgu_out", w_sgu_out, m_w_sgu_out, v_w_sgu_out),
                                          ("w_o", w_o, m_w_o, v_w_o))):
        big[nm] = _adam_reduce("adam_" + nm, rq, w_[0], m_[0], v_[0], 128,
                               pl.BlockSpec((NDEV, None, 128, D), lambda i, j=j: (0, j, 0, 0)))

    per = {
        "w_ada": tuple(a[None] for a in (g_ada, d_ada, m_ada, v_ada)),
        "w_in": tuple(a[None] for a in (g_in, d_in, m_in, v_in)),
        "conv_w": tuple(a[None] for a in (g_cw, d_cw, m_cw, v_cw)),
    }
    for nm in ("w_conv_out", "w_sgu_out", "w_o"):
        per[nm] = tuple(a[None] for a in big[nm])
    for nm in ("b_ada", "g_pre", "conv_b", "conv_ln_g", "conv_ln_b", "sgu_ln_g", "sgu_ln_b", "w_sgu", "b_sgu", "g_final"):
        per[nm] = tuple(s[nm] for s in small)

    order = ["w_ada", "b_ada", "g_pre", "w_in", "conv_w", "conv_b", "conv_ln_g", "conv_ln_b", "w_conv_out",
             "sgu_ln_g", "sgu_ln_b", "w_sgu", "b_sgu", "w_sgu_out", "w_o", "g_final"]
    outs = [loss11.reshape(()), grad_x[None]]
    for part in range(4):
        outs += [per[nm][part] for nm in order]
    return tuple(outs)
```

```python
import jax
import jax.numpy as jnp
from jax import lax
from jax.experimental import pallas as pl
from jax.experimental.pallas import tpu as pltpu

F32 = jnp.float32
BF16 = jnp.bfloat16
MESH = pl.DeviceIdType.MESH
VMEM = pltpu.VMEM
ANY = pl.ANY

D = 1024
NDEV = 8
NSEG = 8
HEADS = 8
HD = 128
CHUNK = 128
KW = 31
HALO = 32
EPS = 1e-6
TM = 256
TMB = 512
RB = 32
RBC = 32
TAPG = 4
EXT = TM + HALO
NSMALL = 176
NEARLY = 144
NLATE = 40
ROW_CW = 16
ROW_WS = 48
W_CO = 0
W_SO, W_O = 0, 1

ADAM_LR = 0.001
ADAM_B1 = 0.9
ADAM_B2 = 0.999
ADAM_EPS = 1e-08
ADAM_WD = 0.01
ADAM_STEP = 10

INV_SQRT2 = 0.7071067811865476
INV_SQRT_2PI = 0.3989422804014327

NT = (((1,), (1,)), ((), ()))
TN = (((0,), (0,)), ((), ()))


def _cp(sem=None, vmem_mb=48):
    return pltpu.CompilerParams(dimension_semantics=sem, vmem_limit_bytes=vmem_mb * 1024 * 1024)


def _me():
    return lax.axis_index("x"), lax.axis_index("y"), lax.axis_index("c")


def _slot(px, py, pc):
    return 4 * px + 2 * py + pc


def _xor_peer(k):
    x, y, c = _me()
    return (1 - x if k & 4 else x, 1 - y if k & 2 else y, 1 - c if k & 1 else c)


def _allgather_phases(srcs, dst_at, send_sems, recv_sems, loc_sems):
    x, y, c = _me()
    me = (x, y, c)
    sibling = (x, y, 1 - c)
    chips = [(1 - x, y), (x, 1 - y), (1 - x, 1 - y)]
    na = len(srcs)

    def copy(a, k, block, to, src=None):
        d = dst_at(a, _slot(*block))
        return pltpu.make_async_remote_copy(
            src_ref=d if src is None else src, dst_ref=d,
            send_sem=send_sems.at[a, k], recv_sem=recv_sems.at[a, k],
            device_id=to, device_id_type=MESH)

    local = [pltpu.make_async_copy(srcs[a], dst_at(a, _slot(*me)), loc_sems.at[a]) for a in range(na)]
    first = []
    for a in range(na):
        first.append(copy(a, 0, me, sibling, src=srcs[a]))
        for j, chip in enumerate(chips):
            first.append(copy(a, 1 + j, me, (*chip, c), src=srcs[a]))
    passed = [copy(a, 4 + j, (*chip, c), sibling) for j, chip in enumerate(chips) for a in range(na)]

    def start():
        for cp in local + first:
            cp.start()

    def relay():
        for j, chip in enumerate(chips):
            for a in range(na):
                copy(a, 1 + j, (*chip, c), me).wait_recv()
                passed[j * na + a].start()

    def finish():
        for a in range(na):
            copy(a, 0, sibling, me).wait_recv()
        for j, chip in enumerate(chips):
            for a in range(na):
                copy(a, 4 + j, (*chip, 1 - c), me).wait_recv()
        for cp in first + passed:
            cp.wait_send()
        for cp in local:
            cp.wait()

    return start, relay, finish


def _allgather(srcs, dst_at, send_sems, recv_sems, loc_sems):
    start, relay, finish = _allgather_phases(srcs, dst_at, send_sems, recv_sems, loc_sems)
    start()
    relay()
    finish()


def _owner_exchange(gq, rq_ref, x_send, x_recv, x_loc):
    mx, my_, mc = _me()
    me = _slot(mx, my_, mc)

    def rows_of(a, slot):
        return gq[a].at[pl.ds(pl.multiple_of(slot * 128, 128), 128), :]

    def exchange(k, a, recv):
        px, py, pc = _xor_peer(k)
        peer = _slot(px, py, pc)
        return pltpu.make_async_remote_copy(
            src_ref=rows_of(a, me if recv else peer), dst_ref=rq_ref.at[peer if recv else me, a],
            send_sem=x_send.at[a, k - 1], recv_sem=x_recv.at[a, k - 1],
            device_id=(px, py, pc), device_id_type=MESH)

    local = [pltpu.make_async_copy(rows_of(a, me), rq_ref.at[me, a], x_loc.at[a]) for a in range(len(gq))]

    def start():
        for cp in local:
            cp.start()
        for k in range(1, NDEV):
            for a in range(len(gq)):
                exchange(k, a, False).start()

    def wait():
        for k in range(1, NDEV):
            for a in range(len(gq)):
                exchange(k, a, True).wait_recv()
        for k in range(1, NDEV):
            for a in range(len(gq)):
                exchange(k, a, False).wait_send()
        for cp in local:
            cp.wait()

    return start, wait


def _sig(v):
    return jax.nn.sigmoid(v)


def _gelu_parts(v):
    cdf = 0.5 * (1.0 + lax.erf(v * INV_SQRT2))
    pdf = jnp.exp(-0.5 * v * v) * INV_SQRT_2PI
    return v * cdf, cdf + v * pdf


def _ln_stats(v):
    mu = jnp.mean(v, axis=-1, keepdims=True)
    xc = v - mu
    var = jnp.mean(xc * xc, axis=-1, keepdims=True)
    rs = lax.rsqrt(var + EPS)
    return xc * rs, rs


def _ln_bwd(dn, n, rs):
    return rs * (dn - jnp.mean(dn, axis=-1, keepdims=True) - n * jnp.mean(dn * n, axis=-1, keepdims=True))


def _fold8(v):
    acc = v[0:8]
    for r in range(8, v.shape[0], 8):
        acc = acc + v[r:r + 8]
    return acc


def _rows(rb, n=RB):
    return pl.ds(rb * n, n)


def _ld(ref, rows):
    return ref[rows, :].astype(F32)


def _blocks(n, body, init=0):
    carry = init
    for rb in range(n):
        carry = body(rb, carry)
    return carry


def _shift_copies(s_ref):
    n = EXT - 8
    for m in range(1, 8):
        for c0 in range(0, n, 56):
            s_ref[m, c0:c0 + 56, :] = s_ref[0, c0 + m:c0 + m + 56, :]


def _ada_exchange(c_ref, w_ref, b_ref, mod_ref, call_ref, cdst, msrc, mdst, c_send, c_recv, m_send, m_recv):
    x, y, c = _me()
    me = _slot(x, y, c)

    def all_to_all(src_of, dst, ss, rs):
        def cp(k, recv):
            px, py, pc = _xor_peer(k)
            peer = _slot(px, py, pc)
            return pltpu.make_async_remote_copy(
                src_ref=src_of(me if recv else peer), dst_ref=dst.at[peer if recv else me],
                send_sem=ss.at[k - 1], recv_sem=rs.at[k - 1], device_id=(px, py, pc), device_id_type=MESH)
        for k in range(1, NDEV):
            cp(k, False).start()
        for k in range(1, NDEV):
            cp(k, True).wait_recv()
        for k in range(1, NDEV):
            cp(k, False).wait_send()

    cdst[me] = c_ref[...]
    all_to_all(lambda s: c_ref, cdst, c_send, c_recv)
    for b in range(NDEV):
        call_ref[b:b + 1, :] = cdst[b]
    m = jnp.dot(call_ref[...], w_ref[...], preferred_element_type=F32, precision=lax.Precision.HIGHEST)
    for b in range(NDEV):
        msrc[b] = m[b:b + 1, :]
    mdst[me] = msrc[me]
    all_to_all(lambda s: msrc.at[s], mdst, m_send, m_recv)
    full = jnp.concatenate([mdst[k] for k in range(NDEV)], axis=1) + b_ref[...]
    for r in range(3):
        mod_ref[r:r + 1, :] = full[:, r * D:(r + 1) * D]


def _ada_fwd(c, w_ada, b_ada):
    wcols = w_ada.shape[1]

    def body(*refs):
        _ada_exchange(*refs)

    vm = pl.BlockSpec(memory_space=VMEM)
    return pl.pallas_call(
        body, name="ada_fwd",
        out_shape=(jax.ShapeDtypeStruct((3, D), F32), jax.ShapeDtypeStruct((NDEV, D), F32)),
        in_specs=[vm, vm, vm], out_specs=(vm, vm),
        scratch_shapes=[VMEM((NDEV, 1, D), F32), VMEM((NDEV, 1, wcols), F32), VMEM((NDEV, 1, wcols), F32),
                        pltpu.SemaphoreType.DMA((7,)), pltpu.SemaphoreType.DMA((7,)),
                        pltpu.SemaphoreType.DMA((7,)), pltpu.SemaphoreType.DMA((7,))],
        compiler_params=_cp(vmem_mb=32),
    )(c, w_ada, b_ada)


def _prep_h(x, mod, g_pre):
    T = x.shape[0]

    def body(x_ref, mod_ref, g_ref, h_ref, ht_ref):
        def blk(rb, carry):
            rows = _rows(rb)
            xv = x_ref[rows, :]
            r = lax.rsqrt(jnp.mean(xv * xv, axis=-1, keepdims=True) + EPS)
            h_ref[rows, :] = ((xv * r) * g_ref[...] * (1.0 + mod_ref[1:2, :]) + mod_ref[0:1, :]).astype(BF16)
            return carry
        _blocks(TMB // RB, blk)
        ht_ref[...] = h_ref[...].T

    return pl.pallas_call(
        body, name="prep_h", grid=(T // TMB,),
        out_shape=(jax.ShapeDtypeStruct((T, D), BF16), jax.ShapeDtypeStruct((D, T), BF16)),
        in_specs=[pl.BlockSpec((TMB, D), lambda i: (i, 0)), pl.BlockSpec((3, D), lambda i: (0, 0)),
                  pl.BlockSpec((1, D), lambda i: (0, 0))],
        out_specs=(pl.BlockSpec((TMB, D), lambda i: (i, 0)), pl.BlockSpec((D, TMB), lambda i: (0, i))),
        compiler_params=_cp(("parallel",)),
    )(x, mod, g_pre)


def _ag_blocks(x, y, c):
    a = 1 - c
    n1 = (x ^ a, y ^ c)
    n2 = (x ^ c, y ^ a)
    dg = (1 - x, 1 - y)
    return [(x, y, c), (x, y, a), (*n1, c), (*n2, c), (*n1, a), (*n2, a), (*dg, c), (*dg, a)]


def _in_proj(h, w_in, order, w_co, w_so, w_o, conv_w):
    T = h.shape[0]
    tmp = min(2 * TMB, T)
    nb = T // tmp

    def body(order_ref, h_ref, w_ref, wco_ref, wso_ref, wo_ref, cw_ref, p_ref, wing_ref, wqg_ref, cwg_ref, own2_ref, land2_ref,
             wbuf, wq_st, wq2_st, send_sems, recv_sems, out_sems, ag_send, ag_recv, ag_loc, own_sems):
        j = pl.program_id(0)
        i = pl.program_id(1)
        blocks = _ag_blocks(*_me())
        me, sibling, n1, n2, n1o, n2o, dg, dgo = blocks
        own2 = [pltpu.make_async_copy(wq2_st, own2_ref, own_sems.at[0]),
                pltpu.make_async_copy(wq2_st, land2_ref.at[:, _slot(*me)], own_sems.at[1])]
        ag_start, ag_relay, ag_finish = _allgather_phases(
            [wq_st, cw_ref], lambda a, slot: wqg_ref.at[:, slot] if a == 0 else cwg_ref.at[slot], ag_send, ag_recv, ag_loc)

        def copy(k, block, to):
            d = wbuf.at[_slot(*block)]
            return pltpu.make_async_remote_copy(src_ref=d, dst_ref=d, send_sem=send_sems.at[k], recv_sem=recv_sems.at[k],
                                                device_id=to, device_id_type=MESH)

        def writeback(jj):
            s = _slot(*blocks[jj])
            return pltpu.make_async_copy(wbuf.at[s], wing_ref.at[s], out_sems.at[jj])

        sends = [copy(0, me, sibling), copy(1, me, n1), copy(2, me, n2)]
        relay = copy(3, n1, n2)
        passed = [copy(4, n1, sibling), copy(5, n2, sibling), copy(6, dg, sibling)]
        arrivals = {1: [(0, sibling)], 2: [(1, n1), (2, n2)], 4: [(5, n1o)], 5: [(4, n2o)], 6: [(3, dg)], 7: [(6, dgo)]}

        @pl.when((j == 0) & (i == 0))
        def _():
            wbuf[_slot(*me)] = w_ref[...].astype(BF16)
            for cp in sends:
                cp.start()
            writeback(0).start()
            wq2_st[W_SO] = wso_ref[...].astype(BF16)
            wq2_st[W_O] = wo_ref[...].astype(BF16)
            for cp in own2:
                cp.start()

        for jj in range(1, NSEG):
            @pl.when((j == jj) & (i == 0))
            def _(jj=jj):
                for k, block in arrivals.get(jj, []):
                    copy(k, block, me).wait_recv()
                if jj == 2:
                    relay.start()
                    passed[0].start()
                    passed[1].start()
                    writeback(3).start()
                    wq_st[0] = wco_ref[...].astype(BF16)
                    ag_start()
                if jj == 6:
                    passed[2].start()
                if jj != 3:
                    writeback(jj).start()
                if jj == NSEG - 1:
                    ag_relay()

        p_ref[...] = jnp.dot(h_ref[...], wbuf[order_ref[j]], preferred_element_type=F32).astype(BF16)

        @pl.when((j == NSEG - 1) & (i == nb - 1))
        def _():
            for cp in sends + [relay] + passed:
                cp.wait_send()
            for jj in range(NSEG):
                writeback(jj).wait()
            for cp in own2:
                cp.wait()
            ag_finish()

    vm = pl.BlockSpec(memory_space=VMEM)
    hbm = pl.BlockSpec(memory_space=ANY)
    return pl.pallas_call(
        body, name="in_proj",
        out_shape=(jax.ShapeDtypeStruct((T, NSEG * D), BF16), jax.ShapeDtypeStruct((NSEG, D, D), BF16),
                   jax.ShapeDtypeStruct((1, NDEV, 128, D), BF16), jax.ShapeDtypeStruct((NDEV, KW, 128), F32),
                   jax.ShapeDtypeStruct((2, 128, D), BF16), jax.ShapeDtypeStruct((2, NDEV, 128, D), BF16)),
        grid_spec=pltpu.PrefetchScalarGridSpec(
            num_scalar_prefetch=1, grid=(NSEG, nb),
            in_specs=[pl.BlockSpec((tmp, D), lambda j, i, o: (i, 0)), vm, vm, vm, vm, vm],
            out_specs=(pl.BlockSpec((tmp, D), lambda j, i, o: (i, o[j])), hbm, hbm, hbm, hbm, hbm),
            scratch_shapes=[VMEM((NSEG, D, D), BF16), VMEM((1, 128, D), BF16), VMEM((2, 128, D), BF16),
                            pltpu.SemaphoreType.DMA((7,)), pltpu.SemaphoreType.DMA((7,)), pltpu.SemaphoreType.DMA((NSEG,)),
                            pltpu.SemaphoreType.DMA((2, 7)), pltpu.SemaphoreType.DMA((2, 7)), pltpu.SemaphoreType.DMA((2,)),
                            pltpu.SemaphoreType.DMA((2,))]),
        compiler_params=_cp(("arbitrary", "arbitrary")),
    )(order, h, w_in, w_co, w_so, w_o, conv_w)


def _wq_start(own2, land2):
    def body(src_ref, land_ref, send_sems, recv_sems, src_thru, land_thru, token):
        barrier = pltpu.get_barrier_semaphore()
        for k in range(1, NDEV):
            pl.semaphore_signal(barrier, inc=1, device_id=_xor_peer(k), device_id_type=MESH)
        pl.semaphore_wait(barrier, NDEV - 1)
        x, y, c = _me()
        for k in range(1, NDEV):
            pltpu.make_async_remote_copy(src_ref=src_ref, dst_ref=land_ref.at[:, _slot(x, y, c)], send_sem=send_sems.at[k - 1],
                                         recv_sem=recv_sems.at[k - 1], device_id=_xor_peer(k), device_id_type=MESH).start()
        token[...] = jnp.zeros_like(token)

    hbm = pl.BlockSpec(memory_space=pltpu.HBM)
    sem = pl.BlockSpec(memory_space=pltpu.SEMAPHORE)
    return pl.pallas_call(
        body, name="wq_start",
        out_shape=(pltpu.SemaphoreType.DMA((NDEV - 1,)), pltpu.SemaphoreType.DMA((NDEV - 1,)),
                   pltpu.HBM(own2.shape, own2.dtype), pltpu.HBM(land2.shape, land2.dtype), jax.ShapeDtypeStruct((8, 128), F32)),
        in_specs=(hbm, hbm), out_specs=(sem, sem, hbm, hbm, pl.BlockSpec(memory_space=VMEM)),
        input_output_aliases={0: 2, 1: 3},
        compiler_params=pltpu.CompilerParams(has_side_effects=pltpu.SideEffectType.DATAFLOW_SIDE_EFFECTING, collective_id=2),
    )(pltpu.with_memory_space_constraint(own2, pltpu.HBM), pltpu.with_memory_space_constraint(land2, pltpu.HBM))


def _wq_wait(send_sems, recv_sems, src_thru, land_thru, after):
    def body(src_ref, land_ref, send_sems, recv_sems, after_ref, src_dead, got_ref):
        del after_ref, src_dead, got_ref
        for k in range(1, NDEV):
            px, py, pc = _xor_peer(k)
            cp = pltpu.make_async_remote_copy(src_ref=src_ref, dst_ref=land_ref.at[:, _slot(px, py, pc)],
                                              send_sem=send_sems.at[k - 1], recv_sem=recv_sems.at[k - 1],
                                              device_id=(px, py, pc), device_id_type=MESH)
            cp.wait_send()
            cp.wait_recv()

    hbm = pl.BlockSpec(memory_space=pltpu.HBM)
    sem = pl.BlockSpec(memory_space=pltpu.SEMAPHORE)
    return pl.pallas_call(
        body, name="wq_wait",
        out_shape=(pltpu.HBM(src_thru.shape, src_thru.dtype), pltpu.HBM(land_thru.shape, land_thru.dtype)),
        in_specs=(hbm, hbm, sem, sem, pl.BlockSpec(memory_space=ANY)), out_specs=(hbm, hbm),
        input_output_aliases={0: 0, 1: 1},
        compiler_params=pltpu.CompilerParams(has_side_effects=pltpu.SideEffectType.DATAFLOW_SIDE_EFFECTING),
    )(src_thru, land_thru, send_sems, recv_sems, after)[1]


def _fill_a_ext(i, s_ref, val_ref, glu_ref, hval_ref, hglu_ref):
    ah = hval_ref[...].astype(F32) * _sig(hglu_ref[...].astype(F32))
    s_ref[0, 0:HALO, :] = jnp.where(i > 0, ah, 0.0)

    def fill(rb, carry):
        rows = _rows(rb)
        s_ref[0, pl.ds(HALO + rb * RB, RB), :] = _ld(val_ref, rows) * _sig(_ld(glu_ref, rows))
        return carry
    _blocks(TM // RB, fill)


def _halo_prev(seg):
    hb = TM // HALO
    return pl.BlockSpec((HALO, D), lambda i: (jnp.maximum(i * hb - 1, 0), seg))


def _branch_a_fwd(p, cw, conv_b, lg, lb, wq, token):
    T = p.shape[0]

    def body(val_ref, glu_ref, z_ref, hval_ref, hglu_ref, cw_ref, cb_ref, lg_ref, lb_ref, w_ref, token_ref,
             yc_ref, ya_ref, s_ref, yp_ref):
        del token_ref
        i = pl.program_id(0)
        _fill_a_ext(i, s_ref, val_ref, glu_ref, hval_ref, hglu_ref)
        _shift_copies(s_ref)

        def conv(rb, carry):
            r0 = rb * RBC
            accs = [jnp.broadcast_to(cb_ref[...], (8, D))] * (RBC // 8)
            for k in range(KW):
                o = 2 + k
                w = cw_ref[k]
                accs = [acc + w * s_ref[o % 8, pl.ds(r0 + 8 * (o // 8 + g), 8), :]
                        for g, acc in enumerate(accs)]
            for g, acc in enumerate(accs):
                yc_ref[pl.ds(r0 + 8 * g, 8), :] = acc
            return carry
        for rb in range(TM // RBC):
            conv(rb, 0)

        def post(rb, carry):
            rows = _rows(rb)
            n, _ = _ln_stats(yc_ref[rows, :])
            l = n * lg_ref[...] + lb_ref[...]
            z = _ld(z_ref, rows)
            yp_ref[rows, :] = ((l * _sig(l)) * (z * _sig(z))).astype(BF16)
            return carry
        _blocks(TM // RB, post)
        ya_ref[...] = jnp.dot(yp_ref[...], w_ref[...], preferred_element_type=F32)

    tile = lambda seg: pl.BlockSpec((TM, D), lambda i: (i, seg))
    row = pl.BlockSpec((1, D), lambda i: (0, 0))
    return pl.pallas_call(
        body, name="branch_a_fwd", grid=(T // TM,),
        out_shape=(jax.ShapeDtypeStruct((T, D), F32), jax.ShapeDtypeStruct((T, D), F32)),
        in_specs=[tile(0), tile(1), tile(2), _halo_prev(0), _halo_prev(1),
                  pl.BlockSpec((KW, 8, D), lambda i: (0, 0, 0)), row, row, row,
                  pl.BlockSpec((None, D, D), lambda i: (W_CO, 0, 0)), pl.BlockSpec((8, 128), lambda i: (0, 0))],
        out_specs=(pl.BlockSpec((TM, D), lambda i: (i, 0)), pl.BlockSpec((TM, D), lambda i: (i, 0))),
        scratch_shapes=[VMEM((8, EXT, D), F32), VMEM((TM, D), BF16)],
        compiler_params=_cp(("parallel",)),
    )(p, p, p, p, p, cw, conv_b, lg, lb, wq, token)


def _masked_ws(ws_ref, wt_ref):
    tri = lax.broadcasted_iota(jnp.int32, (CHUNK, CHUNK), 0) >= lax.broadcasted_iota(jnp.int32, (CHUNK, CHUNK), 1)
    for h in range(HEADS):
        wt_ref[h] = jnp.where(tri, ws_ref[h], 0.0).astype(BF16)


def _sgu_mix(wt_ref, vl_ref, bst_ref, s_ref):
    for h in range(HEADS):
        cs = slice(h * HD, (h + 1) * HD)
        s = jnp.dot(wt_ref[h], _chunks_on_lanes(vl_ref, cs), preferred_element_type=F32) + bst_ref[:, h:h + 1]
        for ck in range(TM // CHUNK):
            s_ref[ck * CHUNK:(ck + 1) * CHUNK, cs] = s[:, ck * HD:(ck + 1) * HD]


def _chunks_on_lanes(ref, cs):
    return jnp.concatenate([ref[ck * CHUNK:(ck + 1) * CHUNK, cs] for ck in range(TM // CHUNK)], axis=1)


def _branch_b_fwd(p, ws, bst, sg, sb, wq):
    T = p.shape[0]

    def body(pu_ref, pv_ref, pz_ref, ws_ref, bst_ref, sg_ref, sb_ref, w_ref, yb_ref,
             wt_ref, vl_ref, t_ref, s_ref, yp_ref):
        _masked_ws(ws_ref, wt_ref)

        def pre(rb, carry):
            rows = _rows(rb)
            vg, _ = _gelu_parts(_ld(pv_ref, rows))
            vn, _ = _ln_stats(vg)
            vl_ref[rows, :] = (vn * sg_ref[...] + sb_ref[...]).astype(BF16)
            u, _ = _gelu_parts(_ld(pu_ref, rows))
            z = _ld(pz_ref, rows)
            t_ref[rows, :] = u * (z * _sig(z))
            return carry
        _blocks(TM // RB, pre)
        _sgu_mix(wt_ref, vl_ref, bst_ref, s_ref)

        def post(rb, carry):
            rows = _rows(rb)
            yp_ref[rows, :] = (t_ref[rows, :] * s_ref[rows, :]).astype(BF16)
            return carry
        _blocks(TM // RB, post)
        yb_ref[...] = jnp.dot(yp_ref[...], w_ref[...], preferred_element_type=F32)

    tile = lambda seg: pl.BlockSpec((TM, D), lambda i: (i, seg))
    row = pl.BlockSpec((1, D), lambda i: (0, 0))
    return pl.pallas_call(
        body, name="branch_b_fwd", grid=(T // TM,),
        out_shape=jax.ShapeDtypeStruct((T, D), F32),
        in_specs=[tile(3), tile(4), tile(5), pl.BlockSpec((HEADS, CHUNK, CHUNK), lambda i: (0, 0, 0)),
                  pl.BlockSpec((CHUNK, HEADS), lambda i: (0, 0)), row, row,
                  pl.BlockSpec((None, D, D), lambda i: (W_SO, 0, 0))],
        out_specs=pl.BlockSpec((TM, D), lambda i: (i, 0)),
        scratch_shapes=[VMEM((HEADS, CHUNK, CHUNK), BF16), VMEM((TM, D), BF16), VMEM((TM, D), F32),
                        VMEM((TM, D), F32), VMEM((TM, D), BF16)],
        compiler_params=_cp(("parallel",)),
    )(p, p, p, ws, bst, sg, sb, wq)


def _merge_loss(p, ya, yb, x, tgt, mod, g_final, wq):
    T = x.shape[0]
    nh = 2 if T % (2 * TM) == 0 else 1
    tm2 = nh * TM
    nt = T // tm2
    nbh = TM // RB

    def body(ga_ref, gb_ref, ya_ref, yb_ref, x_ref, t_ref, mod_ref, gf_ref, w_ref,
             dx2_ref, dya_ref, dyb_ref, dp_ref, gwo_ref, st_ref, *scratch):
        mrg, out, dout, dm = (scratch[k * nh:(k + 1) * nh] for k in range(4))
        gw_acc = scratch[4 * nh]
        i = pl.program_id(0)

        @pl.when(i == 0)
        def _():
            gw_acc[...] = jnp.zeros_like(gw_acc)
            st_ref[...] = jnp.zeros_like(st_ref)

        def merge(h):
            def f(rb, carry):
                rows, loc = _rows(h * nbh + rb), _rows(rb)
                mrg[h][loc, :] = (_sig(_ld(ga_ref, rows)) * ya_ref[rows, :]
                                  + _sig(_ld(gb_ref, rows)) * yb_ref[rows, :]).astype(BF16)
                return carry
            _blocks(nbh, f)

        def head(h, carry):
            def f(rb, carry):
                loss, gg, dg = carry
                rows, loc = _rows(h * nbh + rb), _rows(rb)
                gate = mod_ref[2:3, :]
                gf = gf_ref[...]
                o = out[h][loc, :]
                x2 = x_ref[rows, :] + gate * o
                r2 = lax.rsqrt(jnp.mean(x2 * x2, axis=-1, keepdims=True) + EPS)
                x2n = x2 * r2
                diff = x2n * gf - t_ref[rows, :]
                dy = diff * (1.0 / D)
                dx2n = dy * gf
                dx2 = r2 * (dx2n - x2n * jnp.mean(dx2n * x2n, axis=-1, keepdims=True))
                dx2_ref[rows, :] = dx2
                dout[h][loc, :] = (dx2 * gate).astype(BF16)
                return loss + _fold8(diff * diff), gg + _fold8(dy * x2n), dg + _fold8(dx2 * o)
            return _blocks(nbh, f, carry)

        def split(h):
            def f(rb, carry):
                rows, loc = _rows(h * nbh + rb), _rows(rb)
                d = dm[h][loc, :]
                sa = _sig(_ld(ga_ref, rows))
                sb = _sig(_ld(gb_ref, rows))
                dya_ref[rows, :] = (d * sa).astype(BF16)
                dyb_ref[rows, :] = (d * sb).astype(BF16)
                dp_ref[0, rows, :] = (d * ya_ref[rows, :] * (sa * (1.0 - sa))).astype(BF16)
                dp_ref[1, rows, :] = (d * yb_ref[rows, :] * (sb * (1.0 - sb))).astype(BF16)
                return carry
            _blocks(nbh, f)

        for h in range(nh):
            merge(h)
        for h in range(nh):
            out[h][...] = jnp.dot(mrg[h][...], w_ref[...], preferred_element_type=F32)
        zero = jnp.zeros((8, D), F32)
        carry = (zero, zero, zero)
        for h in range(nh):
            carry = head(h, carry)
        loss, gg, dg = carry
        st_ref[0] = st_ref[0] + loss * (0.5 / D)
        st_ref[1] = st_ref[1] + gg
        st_ref[2] = st_ref[2] + dg
        for h in range(nh):
            dm[h][...] = lax.dot_general(dout[h][...], w_ref[...], NT, preferred_element_type=F32)
            gw_acc[...] += lax.dot_general(mrg[h][...], dout[h][...], TN, preferred_element_type=F32)
        for h in range(nh):
            split(h)

        @pl.when(i == nt - 1)
        def _():
            gwo_ref[...] = gw_acc[...].astype(BF16)

    tile = pl.BlockSpec((tm2, D), lambda i: (i, 0))
    halves = lambda dt: [VMEM((TM, D), dt)] * nh
    return pl.pallas_call(
        body, name="merge_loss", grid=(nt,),
        out_shape=(jax.ShapeDtypeStruct((T, D), F32), jax.ShapeDtypeStruct((T, D), BF16), jax.ShapeDtypeStruct((T, D), BF16),
                   jax.ShapeDtypeStruct((NSEG, T, D), BF16), jax.ShapeDtypeStruct((D, D), BF16),
                   jax.ShapeDtypeStruct((3, 8, D), F32)),
        in_specs=[pl.BlockSpec((tm2, D), lambda i: (i, 6)), pl.BlockSpec((tm2, D), lambda i: (i, 7)), tile, tile, tile, tile,
                  pl.BlockSpec((3, D), lambda i: (0, 0)), pl.BlockSpec((1, D), lambda i: (0, 0)),
                  pl.BlockSpec((None, D, D), lambda i: (W_O, 0, 0))],
        out_specs=(tile, tile, tile, pl.BlockSpec((2, tm2, D), lambda i: (3, i, 0)),
                   pl.BlockSpec((D, D), lambda i: (0, 0)), pl.BlockSpec((3, 8, D), lambda i: (0, 0, 0))),
        scratch_shapes=halves(BF16) + halves(F32) + halves(BF16) + halves(F32) + [VMEM((D, D), F32)],
        compiler_params=_cp(("arbitrary",)),
    )(p, p, ya, yb, x, tgt, mod, g_final, wq)


def _branch_b_bwd(dp, dyb, p, ws, bst, sg, sb, wq):
    T = p.shape[0]
    nt = T // TM

    def body(dp_in, dyb_ref, pu_ref, pv_ref, pz_ref, ws_ref, bst_ref, sg_ref, sb_ref, w_ref,
             dp_ref, gw_ref, gws_ref, gbs_ref, st_ref,
             wt_ref, d_s, vl_s, vn_s, gpv_s, rs_s, s_s, ds_s, ds32_s, yp_s, dvl_s, gw_acc):
        del dp_in
        i = pl.program_id(0)

        @pl.when(i == 0)
        def _():
            gw_acc[...] = jnp.zeros_like(gw_acc)
            gws_ref[...] = jnp.zeros_like(gws_ref)
            gbs_ref[...] = jnp.zeros_like(gbs_ref)
            st_ref[...] = jnp.zeros_like(st_ref)
            _masked_ws(ws_ref, wt_ref)

        d_s[...] = lax.dot_general(dyb_ref[...], w_ref[...], NT, preferred_element_type=F32)

        def pre(rb, carry):
            rows = _rows(rb)
            vg, gpv = _gelu_parts(_ld(pv_ref, rows))
            vn, rs = _ln_stats(vg)
            vl_s[rows, :] = (vn * sg_ref[...] + sb_ref[...]).astype(BF16)
            vn_s[rows, :] = vn
            gpv_s[rows, :] = gpv
            rs_s[rows, :] = rs
            return carry
        _blocks(TM // RB, pre)
        _sgu_mix(wt_ref, vl_s, bst_ref, s_s)

        def mid(rb, carry):
            rows = _rows(rb)
            u, gpu = _gelu_parts(_ld(pu_ref, rows))
            z = _ld(pz_ref, rows)
            sz = _sig(z)
            siluz = z * sz
            d = d_s[rows, :]
            s = s_s[rows, :]
            t = u * siluz
            yp_s[rows, :] = (t * s).astype(BF16)
            dp_ref[0, rows, :] = (d * s * siluz * gpu).astype(BF16)
            ds = d * t
            ds32_s[rows, :] = ds
            ds_s[rows, :] = ds.astype(BF16)
            dp_ref[2, rows, :] = (d * u * s * (sz * (1.0 + z * (1.0 - sz)))).astype(BF16)
            return carry
        _blocks(TM // RB, mid)

        for h in range(HEADS):
            cs = slice(h * HD, (h + 1) * HD)
            dsh = _chunks_on_lanes(ds_s, cs)
            dvl = lax.dot_general(wt_ref[h], dsh, TN, preferred_element_type=F32)
            gbs = gbs_ref[h]
            for ck in range(TM // CHUNK):
                r = slice(ck * CHUNK, (ck + 1) * CHUNK)
                dvl_s[r, cs] = dvl[:, ck * HD:(ck + 1) * HD]
                gbs = gbs + ds32_s[r, cs]
            gbs_ref[h] = gbs
            gws_ref[h] += lax.dot_general(dsh, _chunks_on_lanes(vl_s, cs), NT, preferred_element_type=F32)

        def post(rb, carry):
            g_sg, g_sb = carry
            rows = _rows(rb)
            dvl = dvl_s[rows, :]
            vn = vn_s[rows, :]
            dvg = _ln_bwd(dvl * sg_ref[...], vn, rs_s[rows, :])
            dp_ref[1, rows, :] = (dvg * gpv_s[rows, :]).astype(BF16)
            return g_sg + _fold8(dvl * vn), g_sb + _fold8(dvl)
        zero = jnp.zeros((8, D), F32)
        g_sg, g_sb = _blocks(TM // RB, post, (zero, zero))
        st_ref[0] = st_ref[0] + g_sg
        st_ref[1] = st_ref[1] + g_sb

        gw_acc[...] += lax.dot_general(yp_s[...], dyb_ref[...], TN, preferred_element_type=F32)

        @pl.when(i == nt - 1)
        def _():
            gw_ref[...] = gw_acc[...].astype(BF16)

    tile = lambda seg: pl.BlockSpec((TM, D), lambda i: (i, seg))
    row = pl.BlockSpec((1, D), lambda i: (0, 0))
    hh = pl.BlockSpec((HEADS, CHUNK, CHUNK), lambda i: (0, 0, 0))
    return pl.pallas_call(
        body, name="branch_b_bwd", grid=(nt,),
        out_shape=(jax.ShapeDtypeStruct((NSEG, T, D), BF16), jax.ShapeDtypeStruct((D, D), BF16),
                   jax.ShapeDtypeStruct((HEADS, CHUNK, CHUNK), F32), jax.ShapeDtypeStruct((HEADS, CHUNK, HD), F32),
                   jax.ShapeDtypeStruct((2, 8, D), F32)),
        in_specs=[pl.BlockSpec(memory_space=ANY), pl.BlockSpec((TM, D), lambda i: (i, 0)), tile(3), tile(4), tile(5),
                  hh, pl.BlockSpec((CHUNK, HEADS), lambda i: (0, 0)), row, row,
                  pl.BlockSpec((None, D, D), lambda i: (W_SO, 0, 0))],
        out_specs=(pl.BlockSpec((3, TM, D), lambda i: (1, i, 0)), pl.BlockSpec((D, D), lambda i: (0, 0)), hh, hh,
                   pl.BlockSpec((2, 8, D), lambda i: (0, 0, 0))),
        scratch_shapes=[VMEM((HEADS, CHUNK, CHUNK), BF16), VMEM((TM, D), F32), VMEM((TM, D), BF16), VMEM((TM, D), F32),
                        VMEM((TM, D), F32), VMEM((TM, 1), F32), VMEM((TM, D), F32), VMEM((TM, D), BF16), VMEM((TM, D), F32),
                        VMEM((TM, D), BF16), VMEM((TM, D), F32), VMEM((D, D), F32)],
        input_output_aliases={0: 0},
        compiler_params=_cp(("arbitrary",)),
    )(dp, dyb, p, p, p, ws, bst, sg, sb, wq)


def _branch_a_bwd1(dp, dya, yc, p, lg, lb, wq):
    T = p.shape[0]
    nh = 2 if T % (2 * TM) == 0 else 1
    tm2 = nh * TM
    nt = T // tm2
    nbh = TM // RB

    def body(dp_in, dya_ref, yc_ref, z_ref, lg_ref, lb_ref, w_ref, dp_ref, dyc_ref, gw_ref, st_ref, *scratch):
        del dp_in
        d_s, yp_s = scratch[0:nh], scratch[nh:2 * nh]
        gw_acc = scratch[2 * nh]
        i = pl.program_id(0)
        half = lambda h: slice(h * TM, (h + 1) * TM)

        @pl.when(i == 0)
        def _():
            gw_acc[...] = jnp.zeros_like(gw_acc)
            st_ref[...] = jnp.zeros_like(st_ref)

        for h in range(nh):
            d_s[h][...] = lax.dot_general(dya_ref[half(h), :], w_ref[...], NT, preferred_element_type=F32)

        def blk(h):
            def f(rb, carry):
                g_lg, g_lb, g_cb = carry
                rows, loc = _rows(h * nbh + rb), _rows(rb)
                n, rs = _ln_stats(yc_ref[rows, :])
                l = n * lg_ref[...] + lb_ref[...]
                sgl = _sig(l)
                sl = l * sgl
                z = _ld(z_ref, rows)
                sz = _sig(z)
                siluz = z * sz
                d = d_s[h][loc, :]
                yp_s[h][loc, :] = (sl * siluz).astype(BF16)
                dp_ref[rows, :] = (d * sl * (sz * (1.0 + z * (1.0 - sz)))).astype(BF16)
                dl = (d * siluz) * (sgl * (1.0 + l * (1.0 - sgl)))
                dyc = _ln_bwd(dl * lg_ref[...], n, rs)
                dyc_ref[rows, :] = dyc
                return g_lg + _fold8(dl * n), g_lb + _fold8(dl), g_cb + _fold8(dyc)
            return f
        zero = jnp.zeros((8, D), F32)
        carry = (zero, zero, zero)
        for h in range(nh):
            carry = _blocks(nbh, blk(h), carry)
        g_lg, g_lb, g_cb = carry
        st_ref[0] = st_ref[0] + g_lg
        st_ref[1] = st_ref[1] + g_lb
        st_ref[2] = st_ref[2] + g_cb

        for h in range(nh):
            gw_acc[...] += lax.dot_general(yp_s[h][...], dya_ref[half(h), :], TN, preferred_element_type=F32)

        @pl.when(i == nt - 1)
        def _():
            gw_ref[...] = gw_acc[...].astype(BF16)

    tile = pl.BlockSpec((tm2, D), lambda i: (i, 0))
    row = pl.BlockSpec((1, D), lambda i: (0, 0))
    return pl.pallas_call(
        body, name="branch_a_bwd1", grid=(nt,),
        out_shape=(jax.ShapeDtypeStruct((NSEG, T, D), BF16), jax.ShapeDtypeStruct((T, D), F32),
                   jax.ShapeDtypeStruct((D, D), BF16), jax.ShapeDtypeStruct((3, 8, D), F32)),
        in_specs=[pl.BlockSpec(memory_space=ANY), tile, tile, pl.BlockSpec((tm2, D), lambda i: (i, 2)), row, row,
                  pl.BlockSpec((None, D, D), lambda i: (0, 0, 0))],
        out_specs=(pl.BlockSpec((None, tm2, D), lambda i: (2, i, 0)), tile, pl.BlockSpec((D, D), lambda i: (0, 0)),
                   pl.BlockSpec((3, 8, D), lambda i: (0, 0, 0))),
        scratch_shapes=[VMEM((TM, D), F32)] * nh + [VMEM((TM, D), BF16)] * nh + [VMEM((D, D), F32)],
        input_output_aliases={0: 0},
        compiler_params=_cp(("arbitrary",)),
    )(dp, dya, yc, p, lg, lb, wq)


def _branch_a_bwd2(dp, dyc, p, cw, gw_co, gw_so, gw_o, early):
    T = p.shape[0]
    nt = T // TM
    hb = TM // HALO

    def body(dp_in, dyc_ref, hdyc_ref, val_ref, glu_ref, cw_ref, gco_ref, gso_ref, go_ref, e_ref,
             dp_ref, gcw_ref, rq_ref, eg_ref, a_s, sd_ref, da_s, x_send, x_recv, x_loc, ag_send, ag_recv, ag_loc):
        del dp_in
        i = pl.program_id(0)
        start_exchange, wait_exchange = _owner_exchange((gco_ref, gso_ref, go_ref), rq_ref, x_send, x_recv, x_loc)
        ag_start, ag_relay, ag_finish = _allgather_phases([e_ref], lambda a, slot: eg_ref.at[slot],
                                                          ag_send, ag_recv, ag_loc)

        @pl.when(i == 0)
        def _():
            gcw_ref[...] = jnp.zeros_like(gcw_ref)
            ag_start()
            start_exchange()

        @pl.when(i == nt // 2)
        def _():
            ag_relay()

        def fill(rb, carry):
            rows = _rows(rb)
            a_s[rows, :] = _ld(val_ref, rows) * _sig(_ld(glu_ref, rows))
            return carry
        _blocks(TM // RB, fill)
        sd_ref[0, 0:TM, :] = dyc_ref[...]
        sd_ref[0, TM:EXT, :] = jnp.where(i < nt - 1, hdyc_ref[...], 0.0)
        _shift_copies(sd_ref)

        def conv_t(rb, carry):
            r0 = rb * RBC
            accs = [jnp.zeros((8, D), F32)] * (RBC // 8)
            for k in range(KW):
                o = KW - 1 - k
                w = cw_ref[k]
                accs = [acc + w * sd_ref[o % 8, pl.ds(r0 + 8 * (o // 8 + g), 8), :]
                        for g, acc in enumerate(accs)]
            for g, acc in enumerate(accs):
                da_s[pl.ds(r0 + 8 * g, 8), :] = acc
            return carry
        for rb in range(TM // RBC):
            conv_t(rb, 0)

        for k0 in range(0, KW, TAPG):
            taps = list(range(k0, min(k0 + TAPG, KW)))

            def tap_group(rb, accs, taps=taps):
                for u in range(2):
                    r0 = rb * 32 + 16 * u
                    a = a_s[pl.ds(r0, 16), :]
                    out = []
                    for k, acc in zip(taps, accs):
                        o = KW - 1 - k
                        d = sd_ref[o % 8, pl.ds(r0 + 8 * (o // 8), 16), :]
                        out.append(acc + _fold8(a * d))
                    accs = tuple(out)
                return accs
            sums = tuple(jnp.zeros((8, D), F32) for _ in taps)
            for rb in range(TM // 32):
                sums = tap_group(rb, sums)
            for k, s in zip(taps, sums):
                gcw_ref[k] = gcw_ref[k] + s

        def glu_b(rb, carry):
            rows = _rows(rb)
            da = da_s[rows, :]
            sg = _sig(_ld(glu_ref, rows))
            dp_ref[0, rows, :] = (da * sg).astype(BF16)
            dp_ref[1, rows, :] = (da * _ld(val_ref, rows) * (sg * (1.0 - sg))).astype(BF16)
            return carry
        _blocks(TM // RB, glu_b)

        @pl.when(i == nt - 1)
        def _():
            wait_exchange()
            ag_finish()

    tile = lambda seg: pl.BlockSpec((TM, D), lambda i: (i, seg))
    hbm = pl.BlockSpec(memory_space=ANY)
    return pl.pallas_call(
        body, name="branch_a_bwd2", grid=(nt,),
        out_shape=(jax.ShapeDtypeStruct((NSEG, T, D), BF16), jax.ShapeDtypeStruct((32, 8, D), F32),
                   jax.ShapeDtypeStruct((NDEV, 3, 128, D), BF16), jax.ShapeDtypeStruct((NDEV, NEARLY, D), F32)),
        in_specs=[hbm, pl.BlockSpec((TM, D), lambda i: (i, 0)),
                  pl.BlockSpec((HALO, D), lambda i: (jnp.minimum((i + 1) * hb, nt * hb - 1), 0)),
                  tile(0), tile(1), pl.BlockSpec((KW, 8, D), lambda i: (0, 0, 0)),
                  hbm, hbm, hbm, hbm],
        out_specs=(pl.BlockSpec((2, TM, D), lambda i: (0, i, 0)), pl.BlockSpec((32, 8, D), lambda i: (0, 0, 0)), hbm, hbm),
        scratch_shapes=[VMEM((TM, D), F32), VMEM((8, EXT, D), F32), VMEM((TM, D), F32),
                        pltpu.SemaphoreType.DMA((3, 7)), pltpu.SemaphoreType.DMA((3, 7)), pltpu.SemaphoreType.DMA((3,)),
                        pltpu.SemaphoreType.DMA((1, 7)), pltpu.SemaphoreType.DMA((1, 7)), pltpu.SemaphoreType.DMA((1,))],
        input_output_aliases={0: 0},
        compiler_params=_cp(("arbitrary",)),
    )(dp, dyc, dyc, p, p, cw, gw_co, gw_so, gw_o, early)


def _in_bwd_dx(dp, wing, x, dx2, mod, g_pre, token):
    T = x.shape[0]
    nt = T // TM

    def body(dp_ref, w_ref, x_ref, dx2_ref, mod_ref, g_ref, token_ref, gx_ref, st_ref, dh_s):
        del token_ref
        i = pl.program_id(0)

        @pl.when(i == 0)
        def _():
            st_ref[...] = jnp.zeros_like(st_ref)

        dh_s[...] = lax.dot_general(dp_ref[0], w_ref[0], NT, preferred_element_type=F32)
        for j in range(1, NSEG):
            dh_s[...] += lax.dot_general(dp_ref[j], w_ref[j], NT, preferred_element_type=F32)

        def blk(rb, carry):
            d_sh, d_sc, g_g = carry
            rows = _rows(rb)
            xv = x_ref[rows, :]
            r = lax.rsqrt(jnp.mean(xv * xv, axis=-1, keepdims=True) + EPS)
            xn = xv * r
            g = g_ref[...]
            hpre = xn * g
            dh = dh_s[rows, :]
            dhp = dh * (1.0 + mod_ref[1:2, :])
            dxn = dhp * g
            gx_ref[rows, :] = dx2_ref[rows, :] + r * (dxn - xn * jnp.mean(dxn * xn, axis=-1, keepdims=True))
            return d_sh + _fold8(dh), d_sc + _fold8(dh * hpre), g_g + _fold8(dhp * xn)
        zero = jnp.zeros((8, D), F32)
        d_sh, d_sc, g_g = _blocks(TM // RB, blk, (zero, zero, zero))
        st_ref[0] = st_ref[0] + d_sh
        st_ref[1] = st_ref[1] + d_sc
        st_ref[2] = st_ref[2] + g_g

    tile = pl.BlockSpec((TM, D), lambda i: (i, 0))
    return pl.pallas_call(
        body, name="in_bwd_dx", grid=(nt,),
        out_shape=(jax.ShapeDtypeStruct((T, D), F32), jax.ShapeDtypeStruct((3, 8, D), F32)),
        in_specs=[pl.BlockSpec((NSEG, TM, D), lambda i: (0, i, 0)),
                  pl.BlockSpec((NSEG, D, D), lambda i: (0, 0, 0), pipeline_mode=pl.Buffered(1)),
                  tile, tile, pl.BlockSpec((3, D), lambda i: (0, 0)), pl.BlockSpec((1, D), lambda i: (0, 0)),
                  pl.BlockSpec((8, 128), lambda i: (0, 0))],
        out_specs=(tile, pl.BlockSpec((3, 8, D), lambda i: (0, 0, 0))),
        scratch_shapes=[VMEM((TM, D), F32)],
        compiler_params=_cp(("arbitrary",), vmem_mb=56),
    )(dp, wing, x, dx2, mod, g_pre, token)


RS_ORDER = (5, 4, 3, 2, 7, 6, 1, 0)
RS_HALF = D // 2


def _in_bwd_dw(ht, dp, order):
    T = ht.shape[1]
    tmw = min(4 * TMB, T)
    nb = T // tmw
    nu = 2 * NSEG

    def body(order_ref, ht_ref, dp_ref, csum_ref, g_ref, acc, sbuf, rsib, send_sems, recv_sems, out_sems):
        j = pl.program_id(0)
        i = pl.program_id(1)
        x, y, c = _me()
        sibling = (x, y, 1 - c)

        def to_sibling(s):
            return pltpu.make_async_remote_copy(src_ref=sbuf.at[s], dst_ref=rsib.at[s], send_sem=send_sems.at[s],
                                                recv_sem=recv_sems.at[s], device_id=sibling, device_id_type=MESH)

        def to_csum(t):
            cols = pl.ds((t % 2) * RS_HALF, RS_HALF)
            return pltpu.make_async_copy(sbuf.at[8 + t], csum_ref.at[t // 2, :, cols], out_sems.at[t])

        @pl.when(i == 0)
        def _():
            acc[...] = jnp.zeros_like(acc)
        acc[...] += jnp.dot(ht_ref[...], dp_ref[...], preferred_element_type=F32)

        for jj in range(nu):
            chip, half, mine = jj // 4, (jj % 4) // 2, jj % 2
            s = 2 * chip + half

            @pl.when((j == jj) & (i == nb - 1))
            def _(chip=chip, half=half, mine=mine, s=s):
                if not mine:
                    sbuf[s] = acc[...].astype(BF16)
                    to_sibling(s).start()
                elif chip < 3:
                    to_sibling(s).wait_recv()
                    sbuf[8 + s] = (acc[...] + rsib[s].astype(F32)).astype(BF16)
                    to_csum(s).start()
                else:
                    to_sibling(s).wait_recv()
                    g_ref[:, half * RS_HALF:(half + 1) * RS_HALF] = acc[...] + rsib[s].astype(F32)

        @pl.when((j == nu - 1) & (i == nb - 1))
        def _():
            for s in range(8):
                to_sibling(s).wait_send()
            for t in range(6):
                to_csum(t).wait()

    hbm = pl.BlockSpec(memory_space=ANY)
    return pl.pallas_call(
        body, name="in_bwd_dw",
        out_shape=(jax.ShapeDtypeStruct((3, D, D), BF16), jax.ShapeDtypeStruct((D, D), F32)),
        grid_spec=pltpu.PrefetchScalarGridSpec(
            num_scalar_prefetch=1, grid=(nu, nb),
            in_specs=[pl.BlockSpec((D, tmw), lambda j, i, o: (0, i)),
                      pl.BlockSpec((None, tmw, RS_HALF), lambda j, i, o: (o[2 * (j // 4) + j % 2], i, (j % 4) // 2))],
            out_specs=(hbm, pl.BlockSpec((D, D), lambda j, i, o: (0, 0))),
            scratch_shapes=[VMEM((D, RS_HALF), F32), VMEM((14, D, RS_HALF), BF16), VMEM((8, D, RS_HALF), BF16),
                            pltpu.SemaphoreType.DMA((8,)), pltpu.SemaphoreType.DMA((8,)), pltpu.SemaphoreType.DMA((6,))]),
        compiler_params=_cp(("arbitrary", "arbitrary"), vmem_mb=56),
    )(order, ht, dp)


def _gather_late(late):
    def body(sm_ref, smg_ref, ag_send, ag_recv, ag_loc):
        _allgather([sm_ref], lambda a, slot: smg_ref.at[slot], ag_send, ag_recv, ag_loc)

    hbm = pl.BlockSpec(memory_space=ANY)
    return pl.pallas_call(
        body, name="gather_late",
        out_shape=jax.ShapeDtypeStruct((NDEV, NLATE, D), F32),
        in_specs=[hbm], out_specs=hbm,
        scratch_shapes=[pltpu.SemaphoreType.DMA((1, 7)), pltpu.SemaphoreType.DMA((1, 7)), pltpu.SemaphoreType.DMA((1,))],
        compiler_params=_cp(vmem_mb=32),
    )(late)


RS_CHIPS = ((1, 0), (0, 1), (1, 1))


def _rs_peer(q):
    x, y, c = _me()
    fx, fy = RS_CHIPS[q]
    return (1 - x if fx else x, 1 - y if fy else y, c)


def _rs_start(csum):
    def body(csum_ref, land_ref, send_sems, recv_sems, csum_thru, land_thru, token):
        barrier = pltpu.get_barrier_semaphore()
        for q in range(3):
            pl.semaphore_signal(barrier, inc=1, device_id=_rs_peer(q), device_id_type=MESH)
        pl.semaphore_wait(barrier, 3)
        for q in range(3):
            pltpu.make_async_remote_copy(src_ref=csum_ref.at[q], dst_ref=land_ref.at[q], send_sem=send_sems.at[q],
                                         recv_sem=recv_sems.at[q], device_id=_rs_peer(q), device_id_type=MESH).start()
        token[...] = jnp.zeros_like(token)

    hbm = pl.BlockSpec(memory_space=pltpu.HBM)
    sem = pl.BlockSpec(memory_space=pltpu.SEMAPHORE)
    land = pltpu.with_memory_space_constraint(lax.empty(csum.shape, csum.dtype), pltpu.HBM)
    return pl.pallas_call(
        body, name="rs_start",
        out_shape=(pltpu.SemaphoreType.DMA((3,)), pltpu.SemaphoreType.DMA((3,)), pltpu.HBM(csum.shape, csum.dtype),
                   pltpu.HBM(csum.shape, csum.dtype), jax.ShapeDtypeStruct((8, 128), F32)),
        in_specs=(hbm, hbm), out_specs=(sem, sem, hbm, hbm, pl.BlockSpec(memory_space=VMEM)),
        input_output_aliases={0: 2, 1: 3},
        compiler_params=pltpu.CompilerParams(has_side_effects=pltpu.SideEffectType.DATAFLOW_SIDE_EFFECTING, collective_id=1),
    )(pltpu.with_memory_space_constraint(csum, pltpu.HBM), land)


def _rs_wait(send_sems, recv_sems, csum_thru, land_thru, after):
    def body(csum_ref, land_ref, send_sems, recv_sems, after_ref, csum_dead, got_ref):
        del after_ref, csum_dead, got_ref
        for q in range(3):
            cp = pltpu.make_async_remote_copy(src_ref=csum_ref.at[q], dst_ref=land_ref.at[q], send_sem=send_sems.at[q],
                                              recv_sem=recv_sems.at[q], device_id=_rs_peer(q), device_id_type=MESH)
            cp.wait_send()
            cp.wait_recv()

    hbm = pl.BlockSpec(memory_space=pltpu.HBM)
    sem = pl.BlockSpec(memory_space=pltpu.SEMAPHORE)
    return pl.pallas_call(
        body, name="rs_wait",
        out_shape=(pltpu.HBM(csum_thru.shape, csum_thru.dtype), pltpu.HBM(csum_thru.shape, csum_thru.dtype)),
        in_specs=(hbm, hbm, sem, sem, pl.BlockSpec(memory_space=ANY)), out_specs=(hbm, hbm),
        input_output_aliases={0: 0, 1: 1},
        compiler_params=pltpu.CompilerParams(has_side_effects=pltpu.SideEffectType.DATAFLOW_SIDE_EFFECTING),
    )(csum_thru, land_thru, send_sems, recv_sems, after)[1]


def _pack_early(st_a, st_b, st_m, gws, gbs):
    def body(sa_ref, sb_ref, sm_ref, gws_ref, gbs_ref, out_ref):
        out_ref[...] = jnp.zeros_like(out_ref)
        fold = lambda v: jnp.sum(v, axis=0, keepdims=True)
        out_ref[0:1, :] = fold(sm_ref[2])
        out_ref[1:2, :] = fold(sa_ref[2])
        out_ref[2:3, :] = fold(sa_ref[0])
        out_ref[3:4, :] = fold(sa_ref[1])
        out_ref[4:5, :] = fold(sb_ref[0])
        out_ref[5:6, :] = fold(sb_ref[1])
        out_ref[7:8, :] = fold(sm_ref[1])
        out_ref[8:9, :] = fold(sm_ref[0])
        tri = lax.broadcasted_iota(jnp.int32, (CHUNK, CHUNK), 0) >= lax.broadcasted_iota(jnp.int32, (CHUNK, CHUNK), 1)
        for h in range(HEADS):
            out_ref[16:16 + CHUNK, h * CHUNK:(h + 1) * CHUNK] = jnp.where(tri, gws_ref[h], 0.0)
            out_ref[6:7, h * CHUNK:(h + 1) * CHUNK] = fold(gbs_ref[h].T)

    vm = pl.BlockSpec(memory_space=VMEM)
    return pl.pallas_call(
        body, name="pack_early", out_shape=jax.ShapeDtypeStruct((NEARLY, D), F32),
        in_specs=[vm] * 5, out_specs=vm, compiler_params=_cp(vmem_mb=32),
    )(st_a, st_b, st_m, gws, gbs)


def _pack_late(st_x, gcw):
    def body(sx_ref, gcw_ref, out_ref):
        out_ref[...] = jnp.zeros_like(out_ref)
        fold = lambda v: jnp.sum(v, axis=0, keepdims=True)
        for r in range(3):
            out_ref[r:r + 1, :] = fold(sx_ref[r])
        for k in range(KW):
            out_ref[8 + k:9 + k, :] = fold(gcw_ref[k])

    vm = pl.BlockSpec(memory_space=VMEM)
    return pl.pallas_call(
        body, name="pack_late", out_shape=jax.ShapeDtypeStruct((NLATE, D), F32),
        in_specs=[vm] * 2, out_specs=vm, compiler_params=_cp(vmem_mb=32),
    )(st_x, gcw)


def _pack_params(name, b_ada, g_pre, conv_b, lg, lb, sg, sb, b_sgu, g_final, w_sgu_t):
    def body(ba_ref, *refs):
        rows, ws_ref, out_ref = refs[:8], refs[8], refs[9]
        out_ref[...] = jnp.zeros_like(out_ref)
        for r in range(3):
            out_ref[r:r + 1, :] = ba_ref[:, r * D:(r + 1) * D]
        for r, ref in enumerate(rows):
            out_ref[3 + r:4 + r, :] = ref[...]
        out_ref[ROW_WS:ROW_WS + CHUNK, :] = ws_ref[...]

    vm = pl.BlockSpec(memory_space=VMEM)
    return pl.pallas_call(
        body, name=name, out_shape=jax.ShapeDtypeStruct((NSMALL, D), F32),
        in_specs=[vm] * 10, out_specs=vm, compiler_params=_cp(vmem_mb=32),
    )(b_ada, g_pre, conv_b, lg, lb, sg, sb, b_sgu, g_final, w_sgu_t)


def _adam(w, g, m, v):
    m2 = ADAM_B1 * m + (1.0 - ADAM_B1) * g
    v2 = ADAM_B2 * v + (1.0 - ADAM_B2) * (g * g)
    m_hat = m2 / (1.0 - ADAM_B1 ** ADAM_STEP)
    v_hat = v2 / (1.0 - ADAM_B2 ** ADAM_STEP)
    delta = -ADAM_LR * (m_hat / (jnp.sqrt(v_hat) + ADAM_EPS) + ADAM_WD * w)
    return delta, m2, v2


def _small_finish(eg, lg, w, m, v):
    def body(eg_ref, lg_ref, w_ref, m_ref, v_ref, loss_ref, g_ref, d_ref, m2_ref, v2_ref):
        e = eg_ref[0]
        l = lg_ref[0]
        for s in range(1, NDEV):
            e = e + eg_ref[s]
            l = l + lg_ref[s]
        g_ref[...] = jnp.zeros_like(g_ref)
        g_ref[0:2, :] = l[0:2, :]
        g_ref[2:3, :] = e[0:1, :]
        g_ref[3:4, :] = l[2:3, :]
        g_ref[4:12, :] = e[1:9, :]
        g_ref[ROW_CW:ROW_CW + KW, :] = l[8:8 + KW, :]
        g_ref[ROW_WS:ROW_WS + CHUNK, :] = e[16:16 + CHUNK, :]
        loss_ref[...] = jnp.sum(e[8:9, :], axis=1, keepdims=True)
        d_ref[...], m2_ref[...], v2_ref[...] = _adam(w_ref[...], g_ref[...], m_ref[...], v_ref[...])

    vm = pl.BlockSpec(memory_space=VMEM)
    sd = jax.ShapeDtypeStruct((NSMALL, D), F32)
    return pl.pallas_call(
        body, name="small_finish", out_shape=(jax.ShapeDtypeStruct((1, 1), F32), sd, sd, sd, sd),
        in_specs=[vm] * 5, out_specs=(vm,) * 5, compiler_params=_cp(vmem_mb=32),
    )(eg, lg, w, m, v)


def _ada_grad_adam(ct, dm, w, m, v):
    def body(ct_ref, dm_ref, w_ref, m_ref, v_ref, g_ref, d_ref, m2_ref, v2_ref):
        g = ct_ref[:, 0:1] * dm_ref[0:1, :]
        for b in range(1, NDEV):
            g = g + ct_ref[:, b:b + 1] * dm_ref[b:b + 1, :]
        g_ref[...] = g
        d_ref[...], m2_ref[...], v2_ref[...] = _adam(w_ref[...], g, m_ref[...], v_ref[...])

    vm = pl.BlockSpec(memory_space=VMEM)
    sd = jax.ShapeDtypeStruct(w.shape, F32)
    return pl.pallas_call(
        body, name="ada_grad_adam", out_shape=(sd, sd, sd, sd),
        in_specs=[vm] * 5, out_specs=(vm,) * 4, compiler_params=_cp(vmem_mb=32),
    )(ct, dm, w, m, v)


def _adam_f32(name, g, w, m, v, rows=None):
    R, C = w.shape
    rows = R if rows is None else rows

    def body(g_ref, w_ref, m_ref, v_ref, d_ref, m2_ref, v2_ref):
        d_ref[...], m2_ref[...], v2_ref[...] = _adam(w_ref[...], g_ref[...], m_ref[...], v_ref[...])

    tile = pl.BlockSpec((rows, C), lambda i: (i, 0))
    sd = jax.ShapeDtypeStruct(w.shape, F32)
    return pl.pallas_call(
        body, name=name, grid=(R // rows,), out_shape=(sd, sd, sd), in_specs=[tile] * 4, out_specs=(tile,) * 3,
        compiler_params=_cp(("parallel",), vmem_mb=32),
    )(g, w, m, v)


def _adam_w_in(gown, got, w, m, v, rows=256):
    R, C = w.shape

    def body(go_ref, got_ref, w_ref, m_ref, v_ref, g_ref, d_ref, m2_ref, v2_ref):
        g = go_ref[...]
        for q in range(3):
            g = g + got_ref[q].astype(F32)
        g_ref[...] = g
        d_ref[...], m2_ref[...], v2_ref[...] = _adam(w_ref[...], g, m_ref[...], v_ref[...])

    tile = pl.BlockSpec((rows, C), lambda i: (i, 0))
    sd = jax.ShapeDtypeStruct((R, C), F32)
    return pl.pallas_call(
        body, name="adam_w_in", grid=(R // rows,), out_shape=(sd, sd, sd, sd),
        in_specs=[tile, pl.BlockSpec((3, rows, C), lambda i: (0, i, 0)), tile, tile, tile], out_specs=(tile,) * 4,
        compiler_params=_cp(("parallel",), vmem_mb=32),
    )(gown, got, w, m, v)


def _adam_reduce(name, recv, w, m, v, rows, recv_spec):
    R, C = w.shape

    def body(r_ref, w_ref, m_ref, v_ref, g_ref, d_ref, m2_ref, v2_ref):
        g = r_ref[0].astype(F32)
        for s in range(1, NDEV):
            g = g + r_ref[s].astype(F32)
        g_ref[...] = g
        d_ref[...], m2_ref[...], v2_ref[...] = _adam(w_ref[...], g, m_ref[...], v_ref[...])

    tile = pl.BlockSpec((rows, C), lambda i: (i, 0))
    sd = jax.ShapeDtypeStruct((R, C), F32)
    return pl.pallas_call(
        body, name=name, grid=(R // rows,), out_shape=(sd, sd, sd, sd),
        in_specs=[recv_spec, tile, tile, tile], out_specs=(tile,) * 4,
        compiler_params=_cp(("parallel",)),
    )(recv, w, m, v)


def kernel(x, c, w_ada, b_ada, g_pre, w_in, conv_w, conv_b, conv_ln_g, conv_ln_b, w_conv_out, sgu_ln_g, sgu_ln_b, w_sgu, b_sgu, w_sgu_out, w_o, g_final, loss_target, m_w_ada, m_b_ada, m_g_pre, m_w_in, m_conv_w, m_conv_b, m_conv_ln_g, m_conv_ln_b, m_w_conv_out, m_sgu_ln_g, m_sgu_ln_b, m_w_sgu, m_b_sgu, m_w_sgu_out, m_w_o, m_g_final, v_w_ada, v_b_ada, v_g_pre, v_w_in, v_conv_w, v_conv_b, v_conv_ln_g, v_conv_ln_b, v_w_conv_out, v_sgu_ln_g, v_sgu_ln_b, v_w_sgu, v_b_sgu, v_w_sgu_out, v_w_o, v_g_final):
    T = x.shape[1]
    assert T % TMB == 0 and x.shape[2] == D
    xs = x[0]
    tgt = loss_target[0]
    my = 4 * lax.axis_index("x") + 2 * lax.axis_index("y") + lax.axis_index("c")

    mod, cg = _ada_fwd(c, w_ada[0], b_ada)
    gf = g_final.reshape(1, D)
    bst = b_sgu[0].T

    h, ht = _prep_h(xs, mod, g_pre)
    ag_order = jnp.stack([_slot(*b) for b in _ag_blocks(lax.axis_index("x"), lax.axis_index("y"), lax.axis_index("c"))])
    p, wing, wcog, cwg, own2, land2 = _in_proj(h, w_in[0], ag_order.astype(jnp.int32),
                                               w_conv_out[0], w_sgu_out[0], w_o[0], conv_w[0])
    wco = wcog.reshape(1, D, D)
    cw = jnp.broadcast_to(jnp.transpose(cwg, (1, 0, 2)).reshape(KW, 1, D), (KW, 8, D))
    wq_send, wq_recv, own2_thru, land2_thru, wq_token = _wq_start(own2, land2)
    yc, ya = _branch_a_fwd(p, cw, conv_b, conv_ln_g, conv_ln_b, wco, wq_token)
    wq2 = _wq_wait(wq_send, wq_recv, own2_thru, land2_thru, yc).reshape(2, D, D)
    yb = _branch_b_fwd(p, w_sgu[0], bst, sgu_ln_g, sgu_ln_b, wq2)
    dx2, dya, dyb, dp, gw_o, st_m = _merge_loss(p, ya, yb, xs, tgt, mod, gf, wq2)

    dp, gw_so, gws, gbs, st_b = _branch_b_bwd(dp, dyb, p, w_sgu[0], bst, sgu_ln_g, sgu_ln_b, wq2)
    dp, dyc, gw_co, st_a = _branch_a_bwd1(dp, dya, yc, p, conv_ln_g, conv_ln_b, wco)
    early = _pack_early(st_a, st_b, st_m, gws, gbs)
    dp, gcw, rq, eg = _branch_a_bwd2(dp, dyc, p, cw, gw_co, gw_so, gw_o, early)
    csum, gown = _in_bwd_dw(ht, dp, my ^ jnp.array(RS_ORDER, jnp.int32))
    rs_send, rs_recv, csum_thru, land_thru, token = _rs_start(csum)
    grad_x, st_x = _in_bwd_dx(dp, wing, xs, dx2, mod, g_pre, token)
    got = _rs_wait(rs_send, rs_recv, csum_thru, land_thru, st_x)
    lg = _gather_late(_pack_late(st_x, gcw))

    def pack(name, b_ada_, g_pre_, conv_b_, lg_, lb_, sg_, sb_, b_sgu_, gfin_, w_sgu_):
        return _pack_params(name, b_ada_, g_pre_, conv_b_, lg_, lb_, sg_, sb_, b_sgu_.reshape(1, D), gfin_.reshape(1, D),
                            jnp.transpose(w_sgu_[0], (1, 0, 2)).reshape(CHUNK, D))

    pw = pack("pack_w", b_ada, g_pre, conv_b, conv_ln_g, conv_ln_b, sgu_ln_g, sgu_ln_b, b_sgu, g_final, w_sgu)
    pm = pack("pack_m", m_b_ada, m_g_pre, m_conv_b, m_conv_ln_g, m_conv_ln_b, m_sgu_ln_g, m_sgu_ln_b, m_b_sgu, m_g_final,
              m_w_sgu)
    pv = pack("pack_v", v_b_ada, v_g_pre, v_conv_b, v_conv_ln_g, v_conv_ln_b, v_sgu_ln_g, v_sgu_ln_b, v_b_sgu, v_g_final,
              v_w_sgu)
    loss11, sg_, sd_, sm_, sv_ = _small_finish(eg, lg, pw, pm, pv)

    def unpack(a):
        return dict(
            b_ada=a[0:3].reshape(1, 3 * D), g_pre=a[3:4], conv_b=a[4:5], conv_ln_g=a[5:6], conv_ln_b=a[6:7],
            sgu_ln_g=a[7:8], sgu_ln_b=a[8:9], b_sgu=a[9:10].reshape(1, HEADS, CHUNK), g_final=a[10],
            w_sgu=jnp.transpose(a[ROW_WS:ROW_WS + CHUNK].reshape(CHUNK, HEADS, CHUNK), (1, 0, 2))[None])
    small = [unpack(a) for a in (sg_, sd_, sm_, sv_)]

    g_cw = lax.dynamic_slice_in_dim(sg_[ROW_CW:ROW_CW + KW], my * 128, 128, axis=1)
    d_cw, m_cw, v_cw = _adam_f32("adam_conv_w", g_cw, conv_w[0], m_conv_w[0], v_conv_w[0])

    wcols = w_ada.shape[2]
    dm_all = jnp.concatenate([lg[:, 0], lg[:, 1], eg[:, 0]], axis=1)
    dm_mine = lax.dynamic_slice_in_dim(dm_all, my * wcols, wcols, axis=1)
    g_ada, d_ada, m_ada, v_ada = _ada_grad_adam(cg.T, dm_mine, w_ada[0], m_w_ada[0], v_w_ada[0])

    g_in, d_in, m_in, v_in = _adam_w_in(gown, got, w_in[0], m_w_in[0], v_w_in[0])
    big = {}
    for j, (nm, w_, m_, v_) in enumerate((("w_conv_out", w_conv_out, m_w_conv_out, v_w_conv_out),
                                          ("w_sgu_out", w_sgu_out, m_w_sgu_out, v_w_sgu_out),
                                          ("w_o", w_o, m_w_o, v_w_o))):
        big[nm] = _adam_reduce("adam_" + nm, rq, w_[0], m_[0], v_[0], 128,
                               pl.BlockSpec((NDEV, None, 128, D), lambda i, j=j: (0, j, 0, 0)))

    per = {
        "w_ada": tuple(a[None] for a in (g_ada, d_ada, m_ada, v_ada)),
        "w_in": tuple(a[None] for a in (g_in, d_in, m_in, v_in)),
        "conv_w": tuple(a[None] for a in (g_cw, d_cw, m_cw, v_cw)),
    }
    for nm in ("w_conv_out", "w_sgu_out", "w_o"):
        per[nm] = tuple(a[None] for a in big[nm])
    for nm in ("b_ada", "g_pre", "conv_b", "conv_ln_g", "conv_ln_b", "sgu_ln_g", "sgu_ln_b", "w_sgu", "b_sgu", "g_final"):
        per[nm] = tuple(s[nm] for s in small)

    order = ["w_ada", "b_ada", "g_pre", "w_in", "conv_w", "conv_b", "conv_ln_g", "conv_ln_b", "w_conv_out",
             "sgu_ln_g", "sgu_ln_b", "w_sgu", "b_sgu", "w_sgu_out", "w_o", "g_final"]
    outs = [loss11.reshape(()), grad_x[None]]
    for part in range(4):
        outs += [per[nm][part] for nm in order]
    return tuple(outs)
```

```python
import jax
import jax.numpy as jnp
from jax import lax
from jax.experimental import pallas as pl
from jax.experimental.pallas import tpu as pltpu

F32 = jnp.float32
BF16 = jnp.bfloat16
MESH = pl.DeviceIdType.MESH
VMEM = pltpu.VMEM
ANY = pl.ANY

D = 1024
NDEV = 8
NSEG = 8
HEADS = 8
HD = 128
CHUNK = 128
KW = 31
HALO = 32
EPS = 1e-6
TM = 256
TMB = 512
RB = 32
RBC = 32
TAPG = 4
EXT = TM + HALO
LANE_HALVES = (slice(0, D // 2), slice(D // 2, D))
NSMALL = 176
NEARLY = 144
NLATE = 40
ROW_CW = 16
ROW_WS = 48
W_CO = 0
W_SO, W_O = 0, 1

ADAM_LR = 0.001
ADAM_B1 = 0.9
ADAM_B2 = 0.999
ADAM_EPS = 1e-08
ADAM_WD = 0.01
ADAM_STEP = 10

INV_SQRT2 = 0.7071067811865476
INV_SQRT_2PI = 0.3989422804014327

NT = (((1,), (1,)), ((), ()))
TN = (((0,), (0,)), ((), ()))


def _cp(sem=None, vmem_mb=48):
    return pltpu.CompilerParams(dimension_semantics=sem, vmem_limit_bytes=vmem_mb * 1024 * 1024)


def _me():
    return lax.axis_index("x"), lax.axis_index("y"), lax.axis_index("c")


def _slot(px, py, pc):
    return 4 * px + 2 * py + pc


def _xor_peer(k):
    x, y, c = _me()
    return (1 - x if k & 4 else x, 1 - y if k & 2 else y, 1 - c if k & 1 else c)


def _allgather_phases(srcs, dst_at, send_sems, recv_sems, loc_sems):
    x, y, c = _me()
    me = (x, y, c)
    sibling = (x, y, 1 - c)
    chips = [(1 - x, y), (x, 1 - y), (1 - x, 1 - y)]
    na = len(srcs)

    def copy(a, k, block, to, src=None):
        d = dst_at(a, _slot(*block))
        return pltpu.make_async_remote_copy(
            src_ref=d if src is None else src, dst_ref=d,
            send_sem=send_sems.at[a, k], recv_sem=recv_sems.at[a, k],
            device_id=to, device_id_type=MESH)

    local = [pltpu.make_async_copy(srcs[a], dst_at(a, _slot(*me)), loc_sems.at[a]) for a in range(na)]
    first = []
    for a in range(na):
        first.append(copy(a, 0, me, sibling, src=srcs[a]))
        for j, chip in enumerate(chips):
            first.append(copy(a, 1 + j, me, (*chip, c), src=srcs[a]))
    passed = [copy(a, 4 + j, (*chip, c), sibling) for j, chip in enumerate(chips) for a in range(na)]

    def start():
        for cp in local + first:
            cp.start()

    def relay():
        for j, chip in enumerate(chips):
            for a in range(na):
                copy(a, 1 + j, (*chip, c), me).wait_recv()
                passed[j * na + a].start()

    def finish():
        for a in range(na):
            copy(a, 0, sibling, me).wait_recv()
        for j, chip in enumerate(chips):
            for a in range(na):
                copy(a, 4 + j, (*chip, 1 - c), me).wait_recv()
        for cp in first + passed:
            cp.wait_send()
        for cp in local:
            cp.wait()

    return start, relay, finish


def _allgather(srcs, dst_at, send_sems, recv_sems, loc_sems):
    start, relay, finish = _allgather_phases(srcs, dst_at, send_sems, recv_sems, loc_sems)
    start()
    relay()
    finish()


def _owner_exchange(gq, rq_ref, x_send, x_recv, x_loc):
    mx, my_, mc = _me()
    me = _slot(mx, my_, mc)

    def rows_of(a, slot):
        return gq[a].at[pl.ds(pl.multiple_of(slot * 128, 128), 128), :]

    def exchange(k, a, recv):
        px, py, pc = _xor_peer(k)
        peer = _slot(px, py, pc)
        return pltpu.make_async_remote_copy(
            src_ref=rows_of(a, me if recv else peer), dst_ref=rq_ref.at[peer if recv else me, a],
            send_sem=x_send.at[a, k - 1], recv_sem=x_recv.at[a, k - 1],
            device_id=(px, py, pc), device_id_type=MESH)

    local = [pltpu.make_async_copy(rows_of(a, me), rq_ref.at[me, a], x_loc.at[a]) for a in range(len(gq))]

    def start():
        for cp in local:
            cp.start()
        for k in range(1, NDEV):
            for a in range(len(gq)):
                exchange(k, a, False).start()

    def wait():
        for k in range(1, NDEV):
            for a in range(len(gq)):
                exchange(k, a, True).wait_recv()
        for k in range(1, NDEV):
            for a in range(len(gq)):
                exchange(k, a, False).wait_send()
        for cp in local:
            cp.wait()

    return start, wait


def _sig(v):
    return jax.nn.sigmoid(v)


def _gelu_parts(v):
    cdf = 0.5 * (1.0 + lax.erf(v * INV_SQRT2))
    pdf = jnp.exp(-0.5 * v * v) * INV_SQRT_2PI
    return v * cdf, cdf + v * pdf


def _ln_stats(v):
    mu = jnp.mean(v, axis=-1, keepdims=True)
    xc = v - mu
    var = jnp.mean(xc * xc, axis=-1, keepdims=True)
    rs = lax.rsqrt(var + EPS)
    return xc * rs, rs


def _ln_bwd(dn, n, rs):
    return rs * (dn - jnp.mean(dn, axis=-1, keepdims=True) - n * jnp.mean(dn * n, axis=-1, keepdims=True))


def _fold8(v):
    acc = v[0:8]
    for r in range(8, v.shape[0], 8):
        acc = acc + v[r:r + 8]
    return acc


def _rows(rb, n=RB):
    return pl.ds(rb * n, n)


def _ld(ref, rows):
    return ref[rows, :].astype(F32)


def _blocks(n, body, init=0):
    carry = init
    for rb in range(n):
        carry = body(rb, carry)
    return carry


def _shift_copies(s_ref):
    n = EXT - 8
    for m in range(1, 8):
        for c0 in range(0, n, 56):
            s_ref[m, c0:c0 + 56, :] = s_ref[0, c0 + m:c0 + m + 56, :]


def _ada_exchange(c_ref, w_ref, b_ref, mod_ref, call_ref, cdst, msrc, mdst, c_send, c_recv, m_send, m_recv):
    x, y, c = _me()
    me = _slot(x, y, c)

    def all_to_all(src_of, dst, ss, rs):
        def cp(k, recv):
            px, py, pc = _xor_peer(k)
            peer = _slot(px, py, pc)
            return pltpu.make_async_remote_copy(
                src_ref=src_of(me if recv else peer), dst_ref=dst.at[peer if recv else me],
                send_sem=ss.at[k - 1], recv_sem=rs.at[k - 1], device_id=(px, py, pc), device_id_type=MESH)
        for k in range(1, NDEV):
            cp(k, False).start()
        for k in range(1, NDEV):
            cp(k, True).wait_recv()
        for k in range(1, NDEV):
            cp(k, False).wait_send()

    cdst[me] = c_ref[...]
    all_to_all(lambda s: c_ref, cdst, c_send, c_recv)
    for b in range(NDEV):
        call_ref[b:b + 1, :] = cdst[b]
    m = jnp.dot(call_ref[...], w_ref[...], preferred_element_type=F32, precision=lax.Precision.HIGHEST)
    for b in range(NDEV):
        msrc[b] = m[b:b + 1, :]
    mdst[me] = msrc[me]
    all_to_all(lambda s: msrc.at[s], mdst, m_send, m_recv)
    full = jnp.concatenate([mdst[k] for k in range(NDEV)], axis=1) + b_ref[...]
    for r in range(3):
        mod_ref[r:r + 1, :] = full[:, r * D:(r + 1) * D]


def _ada_fwd(c, w_ada, b_ada):
    wcols = w_ada.shape[1]

    def body(*refs):
        _ada_exchange(*refs)

    vm = pl.BlockSpec(memory_space=VMEM)
    return pl.pallas_call(
        body, name="ada_fwd",
        out_shape=(jax.ShapeDtypeStruct((3, D), F32), jax.ShapeDtypeStruct((NDEV, D), F32)),
        in_specs=[vm, vm, vm], out_specs=(vm, vm),
        scratch_shapes=[VMEM((NDEV, 1, D), F32), VMEM((NDEV, 1, wcols), F32), VMEM((NDEV, 1, wcols), F32),
                        pltpu.SemaphoreType.DMA((7,)), pltpu.SemaphoreType.DMA((7,)),
                        pltpu.SemaphoreType.DMA((7,)), pltpu.SemaphoreType.DMA((7,))],
        compiler_params=_cp(vmem_mb=32),
    )(c, w_ada, b_ada)


def _prep_h(x, mod, g_pre):
    T = x.shape[0]

    def body(x_ref, mod_ref, g_ref, h_ref, ht_ref):
        def blk(rb, carry):
            rows = _rows(rb)
            xv = x_ref[rows, :]
            r = lax.rsqrt(jnp.mean(xv * xv, axis=-1, keepdims=True) + EPS)
            h_ref[rows, :] = ((xv * r) * g_ref[...] * (1.0 + mod_ref[1:2, :]) + mod_ref[0:1, :]).astype(BF16)
            return carry
        _blocks(TMB // RB, blk)
        ht_ref[...] = h_ref[...].T

    return pl.pallas_call(
        body, name="prep_h", grid=(T // TMB,),
        out_shape=(jax.ShapeDtypeStruct((T, D), BF16), jax.ShapeDtypeStruct((D, T), BF16)),
        in_specs=[pl.BlockSpec((TMB, D), lambda i: (i, 0)), pl.BlockSpec((3, D), lambda i: (0, 0)),
                  pl.BlockSpec((1, D), lambda i: (0, 0))],
        out_specs=(pl.BlockSpec((TMB, D), lambda i: (i, 0)), pl.BlockSpec((D, TMB), lambda i: (0, i))),
        compiler_params=_cp(("parallel",)),
    )(x, mod, g_pre)


def _ag_blocks(x, y, c):
    a = 1 - c
    n1 = (x ^ a, y ^ c)
    n2 = (x ^ c, y ^ a)
    dg = (1 - x, 1 - y)
    return [(x, y, c), (x, y, a), (*n1, c), (*n2, c), (*n1, a), (*n2, a), (*dg, c), (*dg, a)]


def _in_proj(h, w_in, order, w_co, w_so, w_o, conv_w):
    T = h.shape[0]
    tmp = min(2 * TMB, T)
    nb = T // tmp

    def body(order_ref, h_ref, w_ref, wco_ref, wso_ref, wo_ref, cw_ref, p_ref, wing_ref, wqg_ref, cwg_ref, own2_ref, land2_ref,
             wbuf, wq_st, wq2_st, send_sems, recv_sems, out_sems, ag_send, ag_recv, ag_loc, own_sems):
        j = pl.program_id(0)
        i = pl.program_id(1)
        blocks = _ag_blocks(*_me())
        me, sibling, n1, n2, n1o, n2o, dg, dgo = blocks
        own2 = [pltpu.make_async_copy(wq2_st, own2_ref, own_sems.at[0]),
                pltpu.make_async_copy(wq2_st, land2_ref.at[:, _slot(*me)], own_sems.at[1])]
        ag_start, ag_relay, ag_finish = _allgather_phases(
            [wq_st, cw_ref], lambda a, slot: wqg_ref.at[:, slot] if a == 0 else cwg_ref.at[slot], ag_send, ag_recv, ag_loc)

        def copy(k, block, to):
            d = wbuf.at[_slot(*block)]
            return pltpu.make_async_remote_copy(src_ref=d, dst_ref=d, send_sem=send_sems.at[k], recv_sem=recv_sems.at[k],
                                                device_id=to, device_id_type=MESH)

        def writeback(jj):
            s = _slot(*blocks[jj])
            return pltpu.make_async_copy(wbuf.at[s], wing_ref.at[s], out_sems.at[jj])

        sends = [copy(0, me, sibling), copy(1, me, n1), copy(2, me, n2)]
        relay = copy(3, n1, n2)
        passed = [copy(4, n1, sibling), copy(5, n2, sibling), copy(6, dg, sibling)]
        arrivals = {1: [(0, sibling)], 2: [(1, n1), (2, n2)], 4: [(5, n1o)], 5: [(4, n2o)], 6: [(3, dg)], 7: [(6, dgo)]}

        @pl.when((j == 0) & (i == 0))
        def _():
            wbuf[_slot(*me)] = w_ref[...].astype(BF16)
            for cp in sends:
                cp.start()
            writeback(0).start()
            wq2_st[W_SO] = wso_ref[...].astype(BF16)
            wq2_st[W_O] = wo_ref[...].astype(BF16)
            for cp in own2:
                cp.start()

        for jj in range(1, NSEG):
            @pl.when((j == jj) & (i == 0))
            def _(jj=jj):
                for k, block in arrivals.get(jj, []):
                    copy(k, block, me).wait_recv()
                if jj == 2:
                    relay.start()
                    passed[0].start()
                    passed[1].start()
                    writeback(3).start()
                    wq_st[0] = wco_ref[...].astype(BF16)
                    ag_start()
                if jj == 6:
                    passed[2].start()
                if jj != 3:
                    writeback(jj).start()
                if jj == NSEG - 1:
                    ag_relay()

        p_ref[...] = jnp.dot(h_ref[...], wbuf[order_ref[j]], preferred_element_type=F32).astype(BF16)

        @pl.when((j == NSEG - 1) & (i == nb - 1))
        def _():
            for cp in sends + [relay] + passed:
                cp.wait_send()
            for jj in range(NSEG):
                writeback(jj).wait()
            for cp in own2:
                cp.wait()
            ag_finish()

    vm = pl.BlockSpec(memory_space=VMEM)
    hbm = pl.BlockSpec(memory_space=ANY)
    return pl.pallas_call(
        body, name="in_proj",
        out_shape=(jax.ShapeDtypeStruct((T, NSEG * D), BF16), jax.ShapeDtypeStruct((NSEG, D, D), BF16),
                   jax.ShapeDtypeStruct((1, NDEV, 128, D), BF16), jax.ShapeDtypeStruct((NDEV, KW, 128), F32),
                   jax.ShapeDtypeStruct((2, 128, D), BF16), jax.ShapeDtypeStruct((2, NDEV, 128, D), BF16)),
        grid_spec=pltpu.PrefetchScalarGridSpec(
            num_scalar_prefetch=1, grid=(NSEG, nb),
            in_specs=[pl.BlockSpec((tmp, D), lambda j, i, o: (i, 0)), vm, vm, vm, vm, vm],
            out_specs=(pl.BlockSpec((tmp, D), lambda j, i, o: (i, o[j])), hbm, hbm, hbm, hbm, hbm),
            scratch_shapes=[VMEM((NSEG, D, D), BF16), VMEM((1, 128, D), BF16), VMEM((2, 128, D), BF16),
                            pltpu.SemaphoreType.DMA((7,)), pltpu.SemaphoreType.DMA((7,)), pltpu.SemaphoreType.DMA((NSEG,)),
                            pltpu.SemaphoreType.DMA((2, 7)), pltpu.SemaphoreType.DMA((2, 7)), pltpu.SemaphoreType.DMA((2,)),
                            pltpu.SemaphoreType.DMA((2,))]),
        compiler_params=_cp(("arbitrary", "arbitrary")),
    )(order, h, w_in, w_co, w_so, w_o, conv_w)


def _wq_start(own2, land2):
    def body(src_ref, land_ref, send_sems, recv_sems, src_thru, land_thru, token):
        barrier = pltpu.get_barrier_semaphore()
        for k in range(1, NDEV):
            pl.semaphore_signal(barrier, inc=1, device_id=_xor_peer(k), device_id_type=MESH)
        pl.semaphore_wait(barrier, NDEV - 1)
        x, y, c = _me()
        for k in range(1, NDEV):
            pltpu.make_async_remote_copy(src_ref=src_ref, dst_ref=land_ref.at[:, _slot(x, y, c)], send_sem=send_sems.at[k - 1],
                                         recv_sem=recv_sems.at[k - 1], device_id=_xor_peer(k), device_id_type=MESH).start()
        token[...] = jnp.zeros_like(token)

    hbm = pl.BlockSpec(memory_space=pltpu.HBM)
    sem = pl.BlockSpec(memory_space=pltpu.SEMAPHORE)
    return pl.pallas_call(
        body, name="wq_start",
        out_shape=(pltpu.SemaphoreType.DMA((NDEV - 1,)), pltpu.SemaphoreType.DMA((NDEV - 1,)),
                   pltpu.HBM(own2.shape, own2.dtype), pltpu.HBM(land2.shape, land2.dtype), jax.ShapeDtypeStruct((8, 128), F32)),
        in_specs=(hbm, hbm), out_specs=(sem, sem, hbm, hbm, pl.BlockSpec(memory_space=VMEM)),
        input_output_aliases={0: 2, 1: 3},
        compiler_params=pltpu.CompilerParams(has_side_effects=pltpu.SideEffectType.DATAFLOW_SIDE_EFFECTING, collective_id=2),
    )(pltpu.with_memory_space_constraint(own2, pltpu.HBM), pltpu.with_memory_space_constraint(land2, pltpu.HBM))


def _wq_wait(send_sems, recv_sems, src_thru, land_thru, after):
    def body(src_ref, land_ref, send_sems, recv_sems, after_ref, src_dead, got_ref):
        del after_ref, src_dead, got_ref
        for k in range(1, NDEV):
            px, py, pc = _xor_peer(k)
            cp = pltpu.make_async_remote_copy(src_ref=src_ref, dst_ref=land_ref.at[:, _slot(px, py, pc)],
                                              send_sem=send_sems.at[k - 1], recv_sem=recv_sems.at[k - 1],
                                              device_id=(px, py, pc), device_id_type=MESH)
            cp.wait_send()
            cp.wait_recv()

    hbm = pl.BlockSpec(memory_space=pltpu.HBM)
    sem = pl.BlockSpec(memory_space=pltpu.SEMAPHORE)
    return pl.pallas_call(
        body, name="wq_wait",
        out_shape=(pltpu.HBM(src_thru.shape, src_thru.dtype), pltpu.HBM(land_thru.shape, land_thru.dtype)),
        in_specs=(hbm, hbm, sem, sem, pl.BlockSpec(memory_space=ANY)), out_specs=(hbm, hbm),
        input_output_aliases={0: 0, 1: 1},
        compiler_params=pltpu.CompilerParams(has_side_effects=pltpu.SideEffectType.DATAFLOW_SIDE_EFFECTING),
    )(src_thru, land_thru, send_sems, recv_sems, after)[1]


def _fill_a_ext(i, s_ref, val_ref, glu_ref, hval_ref, hglu_ref):
    ah = hval_ref[...].astype(F32) * _sig(hglu_ref[...].astype(F32))
    s_ref[0, 0:HALO, :] = jnp.where(i > 0, ah, 0.0)

    def fill(rb, carry):
        rows = _rows(rb)
        s_ref[0, pl.ds(HALO + rb * RB, RB), :] = _ld(val_ref, rows) * _sig(_ld(glu_ref, rows))
        return carry
    _blocks(TM // RB, fill)


def _halo_prev(seg):
    hb = TM // HALO
    return pl.BlockSpec((HALO, D), lambda i: (jnp.maximum(i * hb - 1, 0), seg))


def _branch_a_fwd(p, cw, conv_b, lg, lb, wq, token):
    T = p.shape[0]

    def body(val_ref, glu_ref, z_ref, hval_ref, hglu_ref, cw_ref, cb_ref, lg_ref, lb_ref, w_ref, token_ref,
             yc_ref, ya_ref, s_ref, yp_ref):
        del token_ref
        i = pl.program_id(0)
        _fill_a_ext(i, s_ref, val_ref, glu_ref, hval_ref, hglu_ref)
        _shift_copies(s_ref)

        def conv(rb, carry):
            r0 = rb * RBC
            for lanes in LANE_HALVES:
                accs = [jnp.broadcast_to(cb_ref[:, lanes], (8, D // 2))] * (RBC // 8)
                for k in range(KW):
                    o = 2 + k
                    w = cw_ref[k, :, lanes]
                    accs = [acc + w * s_ref[o % 8, pl.ds(r0 + 8 * (o // 8 + g), 8), lanes]
                            for g, acc in enumerate(accs)]
                for g, acc in enumerate(accs):
                    yc_ref[pl.ds(r0 + 8 * g, 8), lanes] = acc
            return carry
        for rb in range(TM // RBC):
            conv(rb, 0)

        def post(rb, carry):
            rows = _rows(rb)
            n, _ = _ln_stats(yc_ref[rows, :])
            l = n * lg_ref[...] + lb_ref[...]
            z = _ld(z_ref, rows)
            yp_ref[rows, :] = ((l * _sig(l)) * (z * _sig(z))).astype(BF16)
            return carry
        _blocks(TM // RB, post)
        ya_ref[...] = jnp.dot(yp_ref[...], w_ref[...], preferred_element_type=F32)

    tile = lambda seg: pl.BlockSpec((TM, D), lambda i: (i, seg))
    row = pl.BlockSpec((1, D), lambda i: (0, 0))
    return pl.pallas_call(
        body, name="branch_a_fwd", grid=(T // TM,),
        out_shape=(jax.ShapeDtypeStruct((T, D), F32), jax.ShapeDtypeStruct((T, D), F32)),
        in_specs=[tile(0), tile(1), tile(2), _halo_prev(0), _halo_prev(1),
                  pl.BlockSpec((KW, 8, D), lambda i: (0, 0, 0)), row, row, row,
                  pl.BlockSpec((None, D, D), lambda i: (W_CO, 0, 0)), pl.BlockSpec((8, 128), lambda i: (0, 0))],
        out_specs=(pl.BlockSpec((TM, D), lambda i: (i, 0)), pl.BlockSpec((TM, D), lambda i: (i, 0))),
        scratch_shapes=[VMEM((8, EXT, D), F32), VMEM((TM, D), BF16)],
        compiler_params=_cp(("parallel",)),
    )(p, p, p, p, p, cw, conv_b, lg, lb, wq, token)


def _masked_ws(ws_ref, wt_ref):
    tri = lax.broadcasted_iota(jnp.int32, (CHUNK, CHUNK), 0) >= lax.broadcasted_iota(jnp.int32, (CHUNK, CHUNK), 1)
    for h in range(HEADS):
        wt_ref[h] = jnp.where(tri, ws_ref[h], 0.0).astype(BF16)


def _sgu_mix(wt_ref, vl_ref, bst_ref, s_ref):
    for h in range(HEADS):
        cs = slice(h * HD, (h + 1) * HD)
        s = jnp.dot(wt_ref[h], _chunks_on_lanes(vl_ref, cs), preferred_element_type=F32) + bst_ref[:, h:h + 1]
        for ck in range(TM // CHUNK):
            s_ref[ck * CHUNK:(ck + 1) * CHUNK, cs] = s[:, ck * HD:(ck + 1) * HD]


def _chunks_on_lanes(ref, cs):
    return jnp.concatenate([ref[ck * CHUNK:(ck + 1) * CHUNK, cs] for ck in range(TM // CHUNK)], axis=1)


def _branch_b_fwd(p, ws, bst, sg, sb, wq):
    T = p.shape[0]

    def body(pu_ref, pv_ref, pz_ref, ws_ref, bst_ref, sg_ref, sb_ref, w_ref, yb_ref,
             wt_ref, vl_ref, t_ref, s_ref, yp_ref):
        _masked_ws(ws_ref, wt_ref)

        def pre(rb, carry):
            rows = _rows(rb)
            vg, _ = _gelu_parts(_ld(pv_ref, rows))
            vn, _ = _ln_stats(vg)
            vl_ref[rows, :] = (vn * sg_ref[...] + sb_ref[...]).astype(BF16)
            u, _ = _gelu_parts(_ld(pu_ref, rows))
            z = _ld(pz_ref, rows)
            t_ref[rows, :] = u * (z * _sig(z))
            return carry
        _blocks(TM // RB, pre)
        _sgu_mix(wt_ref, vl_ref, bst_ref, s_ref)

        def post(rb, carry):
            rows = _rows(rb)
            yp_ref[rows, :] = (t_ref[rows, :] * s_ref[rows, :]).astype(BF16)
            return carry
        _blocks(TM // RB, post)
        yb_ref[...] = jnp.dot(yp_ref[...], w_ref[...], preferred_element_type=F32)

    tile = lambda seg: pl.BlockSpec((TM, D), lambda i: (i, seg))
    row = pl.BlockSpec((1, D), lambda i: (0, 0))
    return pl.pallas_call(
        body, name="branch_b_fwd", grid=(T // TM,),
        out_shape=jax.ShapeDtypeStruct((T, D), F32),
        in_specs=[tile(3), tile(4), tile(5), pl.BlockSpec((HEADS, CHUNK, CHUNK), lambda i: (0, 0, 0)),
                  pl.BlockSpec((CHUNK, HEADS), lambda i: (0, 0)), row, row,
                  pl.BlockSpec((None, D, D), lambda i: (W_SO, 0, 0))],
        out_specs=pl.BlockSpec((TM, D), lambda i: (i, 0)),
        scratch_shapes=[VMEM((HEADS, CHUNK, CHUNK), BF16), VMEM((TM, D), BF16), VMEM((TM, D), F32),
                        VMEM((TM, D), F32), VMEM((TM, D), BF16)],
        compiler_params=_cp(("parallel",)),
    )(p, p, p, ws, bst, sg, sb, wq)


def _merge_loss(p, ya, yb, x, tgt, mod, g_final, wq):
    T = x.shape[0]
    nh = 2 if T % (2 * TM) == 0 else 1
    tm2 = nh * TM
    nt = T // tm2
    nbh = TM // RB

    def body(ga_ref, gb_ref, ya_ref, yb_ref, x_ref, t_ref, mod_ref, gf_ref, w_ref,
             dx2_ref, dya_ref, dyb_ref, dp_ref, gwo_ref, st_ref, *scratch):
        mrg, out, dout, dm = (scratch[k * nh:(k + 1) * nh] for k in range(4))
        gw_acc = scratch[4 * nh]
        i = pl.program_id(0)

        @pl.when(i == 0)
        def _():
            gw_acc[...] = jnp.zeros_like(gw_acc)
            st_ref[...] = jnp.zeros_like(st_ref)

        def merge(h):
            def f(rb, carry):
                rows, loc = _rows(h * nbh + rb), _rows(rb)
                mrg[h][loc, :] = (_sig(_ld(ga_ref, rows)) * ya_ref[rows, :]
                                  + _sig(_ld(gb_ref, rows)) * yb_ref[rows, :]).astype(BF16)
                return carry
            _blocks(nbh, f)

        def head(h, carry):
            def f(rb, carry):
                loss, gg, dg = carry
                rows, loc = _rows(h * nbh + rb), _rows(rb)
                gate = mod_ref[2:3, :]
                gf = gf_ref[...]
                o = out[h][loc, :]
                x2 = x_ref[rows, :] + gate * o
                r2 = lax.rsqrt(jnp.mean(x2 * x2, axis=-1, keepdims=True) + EPS)
                x2n = x2 * r2
                diff = x2n * gf - t_ref[rows, :]
                dy = diff * (1.0 / D)
                dx2n = dy * gf
                dx2 = r2 * (dx2n - x2n * jnp.mean(dx2n * x2n, axis=-1, keepdims=True))
                dx2_ref[rows, :] = dx2
                dout[h][loc, :] = (dx2 * gate).astype(BF16)
                return loss + _fold8(diff * diff), gg + _fold8(dy * x2n), dg + _fold8(dx2 * o)
            return _blocks(nbh, f, carry)

        def split(h):
            def f(rb, carry):
                rows, loc = _rows(h * nbh + rb), _rows(rb)
                d = dm[h][loc, :]
                sa = _sig(_ld(ga_ref, rows))
                sb = _sig(_ld(gb_ref, rows))
                dya_ref[rows, :] = (d * sa).astype(BF16)
                dyb_ref[rows, :] = (d * sb).astype(BF16)
                dp_ref[0, rows, :] = (d * ya_ref[rows, :] * (sa * (1.0 - sa))).astype(BF16)
                dp_ref[1, rows, :] = (d * yb_ref[rows, :] * (sb * (1.0 - sb))).astype(BF16)
                return carry
            _blocks(nbh, f)

        for h in range(nh):
            merge(h)
        for h in range(nh):
            out[h][...] = jnp.dot(mrg[h][...], w_ref[...], preferred_element_type=F32)
        zero = jnp.zeros((8, D), F32)
        carry = (zero, zero, zero)
        for h in range(nh):
            carry = head(h, carry)
        loss, gg, dg = carry
        st_ref[0] = st_ref[0] + loss * (0.5 / D)
        st_ref[1] = st_ref[1] + gg
        st_ref[2] = st_ref[2] + dg
        for h in range(nh):
            dm[h][...] = lax.dot_general(dout[h][...], w_ref[...], NT, preferred_element_type=F32)
            gw_acc[...] += lax.dot_general(mrg[h][...], dout[h][...], TN, preferred_element_type=F32)
        for h in range(nh):
            split(h)

        @pl.when(i == nt - 1)
        def _():
            gwo_ref[...] = gw_acc[...].astype(BF16)

    tile = pl.BlockSpec((tm2, D), lambda i: (i, 0))
    halves = lambda dt: [VMEM((TM, D), dt)] * nh
    return pl.pallas_call(
        body, name="merge_loss", grid=(nt,),
        out_shape=(jax.ShapeDtypeStruct((T, D), F32), jax.ShapeDtypeStruct((T, D), BF16), jax.ShapeDtypeStruct((T, D), BF16),
                   jax.ShapeDtypeStruct((NSEG, T, D), BF16), jax.ShapeDtypeStruct((D, D), BF16),
                   jax.ShapeDtypeStruct((3, 8, D), F32)),
        in_specs=[pl.BlockSpec((tm2, D), lambda i: (i, 6)), pl.BlockSpec((tm2, D), lambda i: (i, 7)), tile, tile, tile, tile,
                  pl.BlockSpec((3, D), lambda i: (0, 0)), pl.BlockSpec((1, D), lambda i: (0, 0)),
                  pl.BlockSpec((None, D, D), lambda i: (W_O, 0, 0))],
        out_specs=(tile, tile, tile, pl.BlockSpec((2, tm2, D), lambda i: (3, i, 0)),
                   pl.BlockSpec((D, D), lambda i: (0, 0)), pl.BlockSpec((3, 8, D), lambda i: (0, 0, 0))),
        scratch_shapes=halves(BF16) + halves(F32) + halves(BF16) + halves(F32) + [VMEM((D, D), F32)],
        compiler_params=_cp(("arbitrary",)),
    )(p, p, ya, yb, x, tgt, mod, g_final, wq)


def _branch_b_bwd(dp, dyb, p, ws, bst, sg, sb, wq):
    T = p.shape[0]
    nt = T // TM

    def body(dp_in, dyb_ref, pu_ref, pv_ref, pz_ref, ws_ref, bst_ref, sg_ref, sb_ref, w_ref,
             dp_ref, gw_ref, gws_ref, gbs_ref, st_ref,
             wt_ref, d_s, vl_s, vn_s, gpv_s, rs_s, s_s, ds_s, ds32_s, yp_s, dvl_s, gw_acc):
        del dp_in
        i = pl.program_id(0)

        @pl.when(i == 0)
        def _():
            gw_acc[...] = jnp.zeros_like(gw_acc)
            gws_ref[...] = jnp.zeros_like(gws_ref)
            gbs_ref[...] = jnp.zeros_like(gbs_ref)
            st_ref[...] = jnp.zeros_like(st_ref)
            _masked_ws(ws_ref, wt_ref)

        d_s[...] = lax.dot_general(dyb_ref[...], w_ref[...], NT, preferred_element_type=F32)

        def pre(rb, carry):
            rows = _rows(rb)
            vg, gpv = _gelu_parts(_ld(pv_ref, rows))
            vn, rs = _ln_stats(vg)
            vl_s[rows, :] = (vn * sg_ref[...] + sb_ref[...]).astype(BF16)
            vn_s[rows, :] = vn
            gpv_s[rows, :] = gpv
            rs_s[rows, :] = rs
            return carry
        _blocks(TM // RB, pre)
        _sgu_mix(wt_ref, vl_s, bst_ref, s_s)

        def mid(rb, carry):
            rows = _rows(rb)
            u, gpu = _gelu_parts(_ld(pu_ref, rows))
            z = _ld(pz_ref, rows)
            sz = _sig(z)
            siluz = z * sz
            d = d_s[rows, :]
            s = s_s[rows, :]
            t = u * siluz
            yp_s[rows, :] = (t * s).astype(BF16)
            dp_ref[0, rows, :] = (d * s * siluz * gpu).astype(BF16)
            ds = d * t
            ds32_s[rows, :] = ds
            ds_s[rows, :] = ds.astype(BF16)
            dp_ref[2, rows, :] = (d * u * s * (sz * (1.0 + z * (1.0 - sz)))).astype(BF16)
            return carry
        _blocks(TM // RB, mid)

        for h in range(HEADS):
            cs = slice(h * HD, (h + 1) * HD)
            dsh = _chunks_on_lanes(ds_s, cs)
            dvl = lax.dot_general(wt_ref[h], dsh, TN, preferred_element_type=F32)
            gbs = gbs_ref[h]
            for ck in range(TM // CHUNK):
                r = slice(ck * CHUNK, (ck + 1) * CHUNK)
                dvl_s[r, cs] = dvl[:, ck * HD:(ck + 1) * HD]
                gbs = gbs + ds32_s[r, cs]
            gbs_ref[h] = gbs
            gws_ref[h] += lax.dot_general(dsh, _chunks_on_lanes(vl_s, cs), NT, preferred_element_type=F32)

        def post(rb, carry):
            g_sg, g_sb = carry
            rows = _rows(rb)
            dvl = dvl_s[rows, :]
            vn = vn_s[rows, :]
            dvg = _ln_bwd(dvl * sg_ref[...], vn, rs_s[rows, :])
            dp_ref[1, rows, :] = (dvg * gpv_s[rows, :]).astype(BF16)
            return g_sg + _fold8(dvl * vn), g_sb + _fold8(dvl)
        zero = jnp.zeros((8, D), F32)
        g_sg, g_sb = _blocks(TM // RB, post, (zero, zero))
        st_ref[0] = st_ref[0] + g_sg
        st_ref[1] = st_ref[1] + g_sb

        gw_acc[...] += lax.dot_general(yp_s[...], dyb_ref[...], TN, preferred_element_type=F32)

        @pl.when(i == nt - 1)
        def _():
            gw_ref[...] = gw_acc[...].astype(BF16)

    tile = lambda seg: pl.BlockSpec((TM, D), lambda i: (i, seg))
    row = pl.BlockSpec((1, D), lambda i: (0, 0))
    hh = pl.BlockSpec((HEADS, CHUNK, CHUNK), lambda i: (0, 0, 0))
    return pl.pallas_call(
        body, name="branch_b_bwd", grid=(nt,),
        out_shape=(jax.ShapeDtypeStruct((NSEG, T, D), BF16), jax.ShapeDtypeStruct((D, D), BF16),
                   jax.ShapeDtypeStruct((HEADS, CHUNK, CHUNK), F32), jax.ShapeDtypeStruct((HEADS, CHUNK, HD), F32),
                   jax.ShapeDtypeStruct((2, 8, D), F32)),
        in_specs=[pl.BlockSpec(memory_space=ANY), pl.BlockSpec((TM, D), lambda i: (i, 0)), tile(3), tile(4), tile(5),
                  hh, pl.BlockSpec((CHUNK, HEADS), lambda i: (0, 0)), row, row,
                  pl.BlockSpec((None, D, D), lambda i: (W_SO, 0, 0))],
        out_specs=(pl.BlockSpec((3, TM, D), lambda i: (1, i, 0)), pl.BlockSpec((D, D), lambda i: (0, 0)), hh, hh,
                   pl.BlockSpec((2, 8, D), lambda i: (0, 0, 0))),
        scratch_shapes=[VMEM((HEADS, CHUNK, CHUNK), BF16), VMEM((TM, D), F32), VMEM((TM, D), BF16), VMEM((TM, D), F32),
                        VMEM((TM, D), F32), VMEM((TM, 1), F32), VMEM((TM, D), F32), VMEM((TM, D), BF16), VMEM((TM, D), F32),
                        VMEM((TM, D), BF16), VMEM((TM, D), F32), VMEM((D, D), F32)],
        input_output_aliases={0: 0},
        compiler_params=_cp(("arbitrary",)),
    )(dp, dyb, p, p, p, ws, bst, sg, sb, wq)


def _branch_a_bwd1(dp, dya, yc, p, lg, lb, wq):
    T = p.shape[0]
    nh = 2 if T % (2 * TM) == 0 else 1
    tm2 = nh * TM
    nt = T // tm2
    nbh = TM // RB

    def body(dp_in, dya_ref, yc_ref, z_ref, lg_ref, lb_ref, w_ref, dp_ref, dyc_ref, gw_ref, st_ref, *scratch):
        del dp_in
        d_s, yp_s = scratch[0:nh], scratch[nh:2 * nh]
        gw_acc = scratch[2 * nh]
        i = pl.program_id(0)
        half = lambda h: slice(h * TM, (h + 1) * TM)

        @pl.when(i == 0)
        def _():
            gw_acc[...] = jnp.zeros_like(gw_acc)
            st_ref[...] = jnp.zeros_like(st_ref)

        for h in range(nh):
            d_s[h][...] = lax.dot_general(dya_ref[half(h), :], w_ref[...], NT, preferred_element_type=F32)

        def blk(h):
            def f(rb, carry):
                g_lg, g_lb, g_cb = carry
                rows, loc = _rows(h * nbh + rb), _rows(rb)
                n, rs = _ln_stats(yc_ref[rows, :])
                l = n * lg_ref[...] + lb_ref[...]
                sgl = _sig(l)
                sl = l * sgl
                z = _ld(z_ref, rows)
                sz = _sig(z)
                siluz = z * sz
                d = d_s[h][loc, :]
                yp_s[h][loc, :] = (sl * siluz).astype(BF16)
                dp_ref[rows, :] = (d * sl * (sz * (1.0 + z * (1.0 - sz)))).astype(BF16)
                dl = (d * siluz) * (sgl * (1.0 + l * (1.0 - sgl)))
                dyc = _ln_bwd(dl * lg_ref[...], n, rs)
                dyc_ref[rows, :] = dyc
                return g_lg + _fold8(dl * n), g_lb + _fold8(dl), g_cb + _fold8(dyc)
            return f
        zero = jnp.zeros((8, D), F32)
        carry = (zero, zero, zero)
        for h in range(nh):
            carry = _blocks(nbh, blk(h), carry)
        g_lg, g_lb, g_cb = carry
        st_ref[0] = st_ref[0] + g_lg
        st_ref[1] = st_ref[1] + g_lb
        st_ref[2] = st_ref[2] + g_cb

        for h in range(nh):
            gw_acc[...] += lax.dot_general(yp_s[h][...], dya_ref[half(h), :], TN, preferred_element_type=F32)

        @pl.when(i == nt - 1)
        def _():
            gw_ref[...] = gw_acc[...].astype(BF16)

    tile = pl.BlockSpec((tm2, D), lambda i: (i, 0))
    row = pl.BlockSpec((1, D), lambda i: (0, 0))
    return pl.pallas_call(
        body, name="branch_a_bwd1", grid=(nt,),
        out_shape=(jax.ShapeDtypeStruct((NSEG, T, D), BF16), jax.ShapeDtypeStruct((T, D), F32),
                   jax.ShapeDtypeStruct((D, D), BF16), jax.ShapeDtypeStruct((3, 8, D), F32)),
        in_specs=[pl.BlockSpec(memory_space=ANY), tile, tile, pl.BlockSpec((tm2, D), lambda i: (i, 2)), row, row,
                  pl.BlockSpec((None, D, D), lambda i: (0, 0, 0))],
        out_specs=(pl.BlockSpec((None, tm2, D), lambda i: (2, i, 0)), tile, pl.BlockSpec((D, D), lambda i: (0, 0)),
                   pl.BlockSpec((3, 8, D), lambda i: (0, 0, 0))),
        scratch_shapes=[VMEM((TM, D), F32)] * nh + [VMEM((TM, D), BF16)] * nh + [VMEM((D, D), F32)],
        input_output_aliases={0: 0},
        compiler_params=_cp(("arbitrary",)),
    )(dp, dya, yc, p, lg, lb, wq)


def _branch_a_bwd2(dp, dyc, p, cw, gw_co, gw_so, gw_o, early):
    T = p.shape[0]
    nt = T // TM
    hb = TM // HALO

    def body(dp_in, dyc_ref, hdyc_ref, val_ref, glu_ref, cw_ref, gco_ref, gso_ref, go_ref, e_ref,
             dp_ref, gcw_ref, rq_ref, eg_ref, a_s, sd_ref, da_s, x_send, x_recv, x_loc, ag_send, ag_recv, ag_loc):
        del dp_in
        i = pl.program_id(0)
        start_exchange, wait_exchange = _owner_exchange((gco_ref, gso_ref, go_ref), rq_ref, x_send, x_recv, x_loc)
        ag_start, ag_relay, ag_finish = _allgather_phases([e_ref], lambda a, slot: eg_ref.at[slot],
                                                          ag_send, ag_recv, ag_loc)

        @pl.when(i == 0)
        def _():
            gcw_ref[...] = jnp.zeros_like(gcw_ref)
            ag_start()
            start_exchange()

        @pl.when(i == nt // 2)
        def _():
            ag_relay()

        def fill(rb, carry):
            rows = _rows(rb)
            a_s[rows, :] = _ld(val_ref, rows) * _sig(_ld(glu_ref, rows))
            return carry
        _blocks(TM // RB, fill)
        sd_ref[0, 0:TM, :] = dyc_ref[...]
        sd_ref[0, TM:EXT, :] = jnp.where(i < nt - 1, hdyc_ref[...], 0.0)
        _shift_copies(sd_ref)

        def conv_t(rb, carry):
            r0 = rb * RBC
            for lanes in LANE_HALVES:
                accs = [jnp.zeros((8, D // 2), F32)] * (RBC // 8)
                for k in range(KW):
                    o = KW - 1 - k
                    w = cw_ref[k, :, lanes]
                    accs = [acc + w * sd_ref[o % 8, pl.ds(r0 + 8 * (o // 8 + g), 8), lanes]
                            for g, acc in enumerate(accs)]
                for g, acc in enumerate(accs):
                    da_s[pl.ds(r0 + 8 * g, 8), lanes] = acc
            return carry
        for rb in range(TM // RBC):
            conv_t(rb, 0)

        for k0 in range(0, KW, TAPG):
            taps = list(range(k0, min(k0 + TAPG, KW)))

            def tap_group(rb, accs, lanes, taps=taps):
                for u in range(2):
                    r0 = rb * 32 + 16 * u
                    a = a_s[pl.ds(r0, 16), lanes]
                    out = []
                    for k, acc in zip(taps, accs):
                        o = KW - 1 - k
                        d = sd_ref[o % 8, pl.ds(r0 + 8 * (o // 8), 16), lanes]
                        out.append(acc + _fold8(a * d))
                    accs = tuple(out)
                return accs
            for lanes in LANE_HALVES:
                sums = tuple(jnp.zeros((8, D // 2), F32) for _ in taps)
                for rb in range(TM // 32):
                    sums = tap_group(rb, sums, lanes)
                for k, s in zip(taps, sums):
                    gcw_ref[k, :, lanes] = gcw_ref[k, :, lanes] + s

        def glu_b(rb, carry):
            rows = _rows(rb)
            da = da_s[rows, :]
            sg = _sig(_ld(glu_ref, rows))
            dp_ref[0, rows, :] = (da * sg).astype(BF16)
            dp_ref[1, rows, :] = (da * _ld(val_ref, rows) * (sg * (1.0 - sg))).astype(BF16)
            return carry
        _blocks(TM // RB, glu_b)

        @pl.when(i == nt - 1)
        def _():
            wait_exchange()
            ag_finish()

    tile = lambda seg: pl.BlockSpec((TM, D), lambda i: (i, seg))
    hbm = pl.BlockSpec(memory_space=ANY)
    return pl.pallas_call(
        body, name="branch_a_bwd2", grid=(nt,),
        out_shape=(jax.ShapeDtypeStruct((NSEG, T, D), BF16), jax.ShapeDtypeStruct((32, 8, D), F32),
                   jax.ShapeDtypeStruct((NDEV, 3, 128, D), BF16), jax.ShapeDtypeStruct((NDEV, NEARLY, D), F32)),
        in_specs=[hbm, pl.BlockSpec((TM, D), lambda i: (i, 0)),
                  pl.BlockSpec((HALO, D), lambda i: (jnp.minimum((i + 1) * hb, nt * hb - 1), 0)),
                  tile(0), tile(1), pl.BlockSpec((KW, 8, D), lambda i: (0, 0, 0)),
                  hbm, hbm, hbm, hbm],
        out_specs=(pl.BlockSpec((2, TM, D), lambda i: (0, i, 0)), pl.BlockSpec((32, 8, D), lambda i: (0, 0, 0)), hbm, hbm),
        scratch_shapes=[VMEM((TM, D), F32), VMEM((8, EXT, D), F32), VMEM((TM, D), F32),
                        pltpu.SemaphoreType.DMA((3, 7)), pltpu.SemaphoreType.DMA((3, 7)), pltpu.SemaphoreType.DMA((3,)),
                        pltpu.SemaphoreType.DMA((1, 7)), pltpu.SemaphoreType.DMA((1, 7)), pltpu.SemaphoreType.DMA((1,))],
        input_output_aliases={0: 0},
        compiler_params=_cp(("arbitrary",)),
    )(dp, dyc, dyc, p, p, cw, gw_co, gw_so, gw_o, early)


def _in_bwd_dx(dp, wing, x, dx2, mod, g_pre, token):
    T = x.shape[0]
    nt = T // TM

    def body(dp_ref, w_ref, x_ref, dx2_ref, mod_ref, g_ref, token_ref, gx_ref, st_ref, dh_s):
        del token_ref
        i = pl.program_id(0)

        @pl.when(i == 0)
        def _():
            st_ref[...] = jnp.zeros_like(st_ref)

        dh_s[...] = lax.dot_general(dp_ref[0], w_ref[0], NT, preferred_element_type=F32)
        for j in range(1, NSEG):
            dh_s[...] += lax.dot_general(dp_ref[j], w_ref[j], NT, preferred_element_type=F32)

        def blk(rb, carry):
            d_sh, d_sc, g_g = carry
            rows = _rows(rb)
            xv = x_ref[rows, :]
            r = lax.rsqrt(jnp.mean(xv * xv, axis=-1, keepdims=True) + EPS)
            xn = xv * r
            g = g_ref[...]
            hpre = xn * g
            dh = dh_s[rows, :]
            dhp = dh * (1.0 + mod_ref[1:2, :])
            dxn = dhp * g
            gx_ref[rows, :] = dx2_ref[rows, :] + r * (dxn - xn * jnp.mean(dxn * xn, axis=-1, keepdims=True))
            return d_sh + _fold8(dh), d_sc + _fold8(dh * hpre), g_g + _fold8(dhp * xn)
        zero = jnp.zeros((8, D), F32)
        d_sh, d_sc, g_g = _blocks(TM // RB, blk, (zero, zero, zero))
        st_ref[0] = st_ref[0] + d_sh
        st_ref[1] = st_ref[1] + d_sc
        st_ref[2] = st_ref[2] + g_g

    tile = pl.BlockSpec((TM, D), lambda i: (i, 0))
    return pl.pallas_call(
        body, name="in_bwd_dx", grid=(nt,),
        out_shape=(jax.ShapeDtypeStruct((T, D), F32), jax.ShapeDtypeStruct((3, 8, D), F32)),
        in_specs=[pl.BlockSpec((NSEG, TM, D), lambda i: (0, i, 0)),
                  pl.BlockSpec((NSEG, D, D), lambda i: (0, 0, 0), pipeline_mode=pl.Buffered(1)),
                  tile, tile, pl.BlockSpec((3, D), lambda i: (0, 0)), pl.BlockSpec((1, D), lambda i: (0, 0)),
                  pl.BlockSpec((8, 128), lambda i: (0, 0))],
        out_specs=(tile, pl.BlockSpec((3, 8, D), lambda i: (0, 0, 0))),
        scratch_shapes=[VMEM((TM, D), F32)],
        compiler_params=_cp(("arbitrary",), vmem_mb=56),
    )(dp, wing, x, dx2, mod, g_pre, token)


RS_ORDER = (5, 4, 3, 2, 7, 6, 1, 0)
RS_HALF = D // 2


def _in_bwd_dw(ht, dp, order):
    T = ht.shape[1]
    tmw = min(4 * TMB, T)
    nb = T // tmw
    nu = 2 * NSEG

    def body(order_ref, ht_ref, dp_ref, csum_ref, g_ref, acc, sbuf, rsib, send_sems, recv_sems, out_sems):
        j = pl.program_id(0)
        i = pl.program_id(1)
        x, y, c = _me()
        sibling = (x, y, 1 - c)

        def to_sibling(s):
            return pltpu.make_async_remote_copy(src_ref=sbuf.at[s], dst_ref=rsib.at[s], send_sem=send_sems.at[s],
                                                recv_sem=recv_sems.at[s], device_id=sibling, device_id_type=MESH)

        def to_csum(t):
            cols = pl.ds((t % 2) * RS_HALF, RS_HALF)
            return pltpu.make_async_copy(sbuf.at[8 + t], csum_ref.at[t // 2, :, cols], out_sems.at[t])

        @pl.when(i == 0)
        def _():
            acc[...] = jnp.zeros_like(acc)
        acc[...] += jnp.dot(ht_ref[...], dp_ref[...], preferred_element_type=F32)

        for jj in range(nu):
            chip, half, mine = jj // 4, (jj % 4) // 2, jj % 2
            s = 2 * chip + half

            @pl.when((j == jj) & (i == nb - 1))
            def _(chip=chip, half=half, mine=mine, s=s):
                if not mine:
                    sbuf[s] = acc[...].astype(BF16)
                    to_sibling(s).start()
                elif chip < 3:
                    to_sibling(s).wait_recv()
                    sbuf[8 + s] = (acc[...] + rsib[s].astype(F32)).astype(BF16)
                    to_csum(s).start()
                else:
                    to_sibling(s).wait_recv()
                    g_ref[:, half * RS_HALF:(half + 1) * RS_HALF] = acc[...] + rsib[s].astype(F32)

        @pl.when((j == nu - 1) & (i == nb - 1))
        def _():
            for s in range(8):
                to_sibling(s).wait_send()
            for t in range(6):
                to_csum(t).wait()

    hbm = pl.BlockSpec(memory_space=ANY)
    return pl.pallas_call(
        body, name="in_bwd_dw",
        out_shape=(jax.ShapeDtypeStruct((3, D, D), BF16), jax.ShapeDtypeStruct((D, D), F32)),
        grid_spec=pltpu.PrefetchScalarGridSpec(
            num_scalar_prefetch=1, grid=(nu, nb),
            in_specs=[pl.BlockSpec((D, tmw), lambda j, i, o: (0, i)),
                      pl.BlockSpec((None, tmw, RS_HALF), lambda j, i, o: (o[2 * (j // 4) + j % 2], i, (j % 4) // 2))],
            out_specs=(hbm, pl.BlockSpec((D, D), lambda j, i, o: (0, 0))),
            scratch_shapes=[VMEM((D, RS_HALF), F32), VMEM((14, D, RS_HALF), BF16), VMEM((8, D, RS_HALF), BF16),
                            pltpu.SemaphoreType.DMA((8,)), pltpu.SemaphoreType.DMA((8,)), pltpu.SemaphoreType.DMA((6,))]),
        compiler_params=_cp(("arbitrary", "arbitrary"), vmem_mb=56),
    )(order, ht, dp)


def _gather_late(late):
    def body(sm_ref, smg_ref, ag_send, ag_recv, ag_loc):
        _allgather([sm_ref], lambda a, slot: smg_ref.at[slot], ag_send, ag_recv, ag_loc)

    hbm = pl.BlockSpec(memory_space=ANY)
    return pl.pallas_call(
        body, name="gather_late",
        out_shape=jax.ShapeDtypeStruct((NDEV, NLATE, D), F32),
        in_specs=[hbm], out_specs=hbm,
        scratch_shapes=[pltpu.SemaphoreType.DMA((1, 7)), pltpu.SemaphoreType.DMA((1, 7)), pltpu.SemaphoreType.DMA((1,))],
        compiler_params=_cp(vmem_mb=32),
    )(late)


RS_CHIPS = ((1, 0), (0, 1), (1, 1))


def _rs_peer(q):
    x, y, c = _me()
    fx, fy = RS_CHIPS[q]
    return (1 - x if fx else x, 1 - y if fy else y, c)


def _rs_start(csum):
    def body(csum_ref, land_ref, send_sems, recv_sems, csum_thru, land_thru, token):
        barrier = pltpu.get_barrier_semaphore()
        for q in range(3):
            pl.semaphore_signal(barrier, inc=1, device_id=_rs_peer(q), device_id_type=MESH)
        pl.semaphore_wait(barrier, 3)
        for q in range(3):
            pltpu.make_async_remote_copy(src_ref=csum_ref.at[q], dst_ref=land_ref.at[q], send_sem=send_sems.at[q],
                                         recv_sem=recv_sems.at[q], device_id=_rs_peer(q), device_id_type=MESH).start()
        token[...] = jnp.zeros_like(token)

    hbm = pl.BlockSpec(memory_space=pltpu.HBM)
    sem = pl.BlockSpec(memory_space=pltpu.SEMAPHORE)
    land = pltpu.with_memory_space_constraint(lax.empty(csum.shape, csum.dtype), pltpu.HBM)
    return pl.pallas_call(
        body, name="rs_start",
        out_shape=(pltpu.SemaphoreType.DMA((3,)), pltpu.SemaphoreType.DMA((3,)), pltpu.HBM(csum.shape, csum.dtype),
                   pltpu.HBM(csum.shape, csum.dtype), jax.ShapeDtypeStruct((8, 128), F32)),
        in_specs=(hbm, hbm), out_specs=(sem, sem, hbm, hbm, pl.BlockSpec(memory_space=VMEM)),
        input_output_aliases={0: 2, 1: 3},
        compiler_params=pltpu.CompilerParams(has_side_effects=pltpu.SideEffectType.DATAFLOW_SIDE_EFFECTING, collective_id=1),
    )(pltpu.with_memory_space_constraint(csum, pltpu.HBM), land)


def _rs_wait(send_sems, recv_sems, csum_thru, land_thru, after):
    def body(csum_ref, land_ref, send_sems, recv_sems, after_ref, csum_dead, got_ref):
        del after_ref, csum_dead, got_ref
        for q in range(3):
            cp = pltpu.make_async_remote_copy(src_ref=csum_ref.at[q], dst_ref=land_ref.at[q], send_sem=send_sems.at[q],
                                              recv_sem=recv_sems.at[q], device_id=_rs_peer(q), device_id_type=MESH)
            cp.wait_send()
            cp.wait_recv()

    hbm = pl.BlockSpec(memory_space=pltpu.HBM)
    sem = pl.BlockSpec(memory_space=pltpu.SEMAPHORE)
    return pl.pallas_call(
        body, name="rs_wait",
        out_shape=(pltpu.HBM(csum_thru.shape, csum_thru.dtype), pltpu.HBM(csum_thru.shape, csum_thru.dtype)),
        in_specs=(hbm, hbm, sem, sem, pl.BlockSpec(memory_space=ANY)), out_specs=(hbm, hbm),
        input_output_aliases={0: 0, 1: 1},
        compiler_params=pltpu.CompilerParams(has_side_effects=pltpu.SideEffectType.DATAFLOW_SIDE_EFFECTING),
    )(csum_thru, land_thru, send_sems, recv_sems, after)[1]


def _pack_early(st_a, st_b, st_m, gws, gbs):
    def body(sa_ref, sb_ref, sm_ref, gws_ref, gbs_ref, out_ref):
        out_ref[...] = jnp.zeros_like(out_ref)
        fold = lambda v: jnp.sum(v, axis=0, keepdims=True)
        out_ref[0:1, :] = fold(sm_ref[2])
        out_ref[1:2, :] = fold(sa_ref[2])
        out_ref[2:3, :] = fold(sa_ref[0])
        out_ref[3:4, :] = fold(sa_ref[1])
        out_ref[4:5, :] = fold(sb_ref[0])
        out_ref[5:6, :] = fold(sb_ref[1])
        out_ref[7:8, :] = fold(sm_ref[1])
        out_ref[8:9, :] = fold(sm_ref[0])
        tri = lax.broadcasted_iota(jnp.int32, (CHUNK, CHUNK), 0) >= lax.broadcasted_iota(jnp.int32, (CHUNK, CHUNK), 1)
        for h in range(HEADS):
            out_ref[16:16 + CHUNK, h * CHUNK:(h + 1) * CHUNK] = jnp.where(tri, gws_ref[h], 0.0)
            out_ref[6:7, h * CHUNK:(h + 1) * CHUNK] = fold(gbs_ref[h].T)

    vm = pl.BlockSpec(memory_space=VMEM)
    return pl.pallas_call(
        body, name="pack_early", out_shape=jax.ShapeDtypeStruct((NEARLY, D), F32),
        in_specs=[vm] * 5, out_specs=vm, compiler_params=_cp(vmem_mb=32),
    )(st_a, st_b, st_m, gws, gbs)


def _pack_late(st_x, gcw):
    def body(sx_ref, gcw_ref, out_ref):
        out_ref[...] = jnp.zeros_like(out_ref)
        fold = lambda v: jnp.sum(v, axis=0, keepdims=True)
        for r in range(3):
            out_ref[r:r + 1, :] = fold(sx_ref[r])
        for k in range(KW):
            out_ref[8 + k:9 + k, :] = fold(gcw_ref[k])

    vm = pl.BlockSpec(memory_space=VMEM)
    return pl.pallas_call(
        body, name="pack_late", out_shape=jax.ShapeDtypeStruct((NLATE, D), F32),
        in_specs=[vm] * 2, out_specs=vm, compiler_params=_cp(vmem_mb=32),
    )(st_x, gcw)


def _pack_params(name, b_ada, g_pre, conv_b, lg, lb, sg, sb, b_sgu, g_final, w_sgu_t):
    def body(ba_ref, *refs):
        rows, ws_ref, out_ref = refs[:8], refs[8], refs[9]
        out_ref[...] = jnp.zeros_like(out_ref)
        for r in range(3):
            out_ref[r:r + 1, :] = ba_ref[:, r * D:(r + 1) * D]
        for r, ref in enumerate(rows):
            out_ref[3 + r:4 + r, :] = ref[...]
        out_ref[ROW_WS:ROW_WS + CHUNK, :] = ws_ref[...]

    vm = pl.BlockSpec(memory_space=VMEM)
    return pl.pallas_call(
        body, name=name, out_shape=jax.ShapeDtypeStruct((NSMALL, D), F32),
        in_specs=[vm] * 10, out_specs=vm, compiler_params=_cp(vmem_mb=32),
    )(b_ada, g_pre, conv_b, lg, lb, sg, sb, b_sgu, g_final, w_sgu_t)


def _adam(w, g, m, v):
    m2 = ADAM_B1 * m + (1.0 - ADAM_B1) * g
    v2 = ADAM_B2 * v + (1.0 - ADAM_B2) * (g * g)
    m_hat = m2 / (1.0 - ADAM_B1 ** ADAM_STEP)
    v_hat = v2 / (1.0 - ADAM_B2 ** ADAM_STEP)
    delta = -ADAM_LR * (m_hat / (jnp.sqrt(v_hat) + ADAM_EPS) + ADAM_WD * w)
    return delta, m2, v2


def _small_finish(eg, lg, w, m, v):
    def body(eg_ref, lg_ref, w_ref, m_ref, v_ref, loss_ref, g_ref, d_ref, m2_ref, v2_ref):
        e = eg_ref[0]
        l = lg_ref[0]
        for s in range(1, NDEV):
            e = e + eg_ref[s]
            l = l + lg_ref[s]
        g_ref[...] = jnp.zeros_like(g_ref)
        g_ref[0:2, :] = l[0:2, :]
        g_ref[2:3, :] = e[0:1, :]
        g_ref[3:4, :] = l[2:3, :]
        g_ref[4:12, :] = e[1:9, :]
        g_ref[ROW_CW:ROW_CW + KW, :] = l[8:8 + KW, :]
        g_ref[ROW_WS:ROW_WS + CHUNK, :] = e[16:16 + CHUNK, :]
        loss_ref[...] = jnp.sum(e[8:9, :], axis=1, keepdims=True)
        d_ref[...], m2_ref[...], v2_ref[...] = _adam(w_ref[...], g_ref[...], m_ref[...], v_ref[...])

    vm = pl.BlockSpec(memory_space=VMEM)
    sd = jax.ShapeDtypeStruct((NSMALL, D), F32)
    return pl.pallas_call(
        body, name="small_finish", out_shape=(jax.ShapeDtypeStruct((1, 1), F32), sd, sd, sd, sd),
        in_specs=[vm] * 5, out_specs=(vm,) * 5, compiler_params=_cp(vmem_mb=32),
    )(eg, lg, w, m, v)


def _ada_grad_adam(ct, dm, w, m, v):
    def body(ct_ref, dm_ref, w_ref, m_ref, v_ref, g_ref, d_ref, m2_ref, v2_ref):
        g = ct_ref[:, 0:1] * dm_ref[0:1, :]
        for b in range(1, NDEV):
            g = g + ct_ref[:, b:b + 1] * dm_ref[b:b + 1, :]
        g_ref[...] = g
        d_ref[...], m2_ref[...], v2_ref[...] = _adam(w_ref[...], g, m_ref[...], v_ref[...])

    vm = pl.BlockSpec(memory_space=VMEM)
    sd = jax.ShapeDtypeStruct(w.shape, F32)
    return pl.pallas_call(
        body, name="ada_grad_adam", out_shape=(sd, sd, sd, sd),
        in_specs=[vm] * 5, out_specs=(vm,) * 4, compiler_params=_cp(vmem_mb=32),
    )(ct, dm, w, m, v)


def _adam_f32(name, g, w, m, v, rows=None):
    R, C = w.shape
    rows = R if rows is None else rows

    def body(g_ref, w_ref, m_ref, v_ref, d_ref, m2_ref, v2_ref):
        d_ref[...], m2_ref[...], v2_ref[...] = _adam(w_ref[...], g_ref[...], m_ref[...], v_ref[...])

    tile = pl.BlockSpec((rows, C), lambda i: (i, 0))
    sd = jax.ShapeDtypeStruct(w.shape, F32)
    return pl.pallas_call(
        body, name=name, grid=(R // rows,), out_shape=(sd, sd, sd), in_specs=[tile] * 4, out_specs=(tile,) * 3,
        compiler_params=_cp(("parallel",), vmem_mb=32),
    )(g, w, m, v)


def _adam_w_in(gown, got, w, m, v, rows=256):
    R, C = w.shape

    def body(go_ref, got_ref, w_ref, m_ref, v_ref, g_ref, d_ref, m2_ref, v2_ref):
        g = go_ref[...]
        for q in range(3):
            g = g + got_ref[q].astype(F32)
        g_ref[...] = g
        d_ref[...], m2_ref[...], v2_ref[...] = _adam(w_ref[...], g, m_ref[...], v_ref[...])

    tile = pl.BlockSpec((rows, C), lambda i: (i, 0))
    sd = jax.ShapeDtypeStruct((R, C), F32)
    return pl.pallas_call(
        body, name="adam_w_in", grid=(R // rows,), out_shape=(sd, sd, sd, sd),
        in_specs=[tile, pl.BlockSpec((3, rows, C), lambda i: (0, i, 0)), tile, tile, tile], out_specs=(tile,) * 4,
        compiler_params=_cp(("parallel",), vmem_mb=32),
    )(gown, got, w, m, v)


def _adam_reduce(name, recv, w, m, v, rows, recv_spec):
    R, C = w.shape

    def body(r_ref, w_ref, m_ref, v_ref, g_ref, d_ref, m2_ref, v2_ref):
        g = r_ref[0].astype(F32)
        for s in range(1, NDEV):
            g = g + r_ref[s].astype(F32)
        g_ref[...] = g
        d_ref[...], m2_ref[...], v2_ref[...] = _adam(w_ref[...], g, m_ref[...], v_ref[...])

    tile = pl.BlockSpec((rows, C), lambda i: (i, 0))
    sd = jax.ShapeDtypeStruct((R, C), F32)
    return pl.pallas_call(
        body, name=name, grid=(R // rows,), out_shape=(sd, sd, sd, sd),
        in_specs=[recv_spec, tile, tile, tile], out_specs=(tile,) * 4,
        compiler_params=_cp(("parallel",)),
    )(recv, w, m, v)


def kernel(x, c, w_ada, b_ada, g_pre, w_in, conv_w, conv_b, conv_ln_g, conv_ln_b, w_conv_out, sgu_ln_g, sgu_ln_b, w_sgu, b_sgu, w_sgu_out, w_o, g_final, loss_target, m_w_ada, m_b_ada, m_g_pre, m_w_in, m_conv_w, m_conv_b, m_conv_ln_g, m_conv_ln_b, m_w_conv_out, m_sgu_ln_g, m_sgu_ln_b, m_w_sgu, m_b_sgu, m_w_sgu_out, m_w_o, m_g_final, v_w_ada, v_b_ada, v_g_pre, v_w_in, v_conv_w, v_conv_b, v_conv_ln_g, v_conv_ln_b, v_w_conv_out, v_sgu_ln_g, v_sgu_ln_b, v_w_sgu, v_b_sgu, v_w_sgu_out, v_w_o, v_g_final):
    T = x.shape[1]
    assert T % TMB == 0 and x.shape[2] == D
    xs = x[0]
    tgt = loss_target[0]
    my = 4 * lax.axis_index("x") + 2 * lax.axis_index("y") + lax.axis_index("c")

    mod, cg = _ada_fwd(c, w_ada[0], b_ada)
    gf = g_final.reshape(1, D)
    bst = b_sgu[0].T

    h, ht = _prep_h(xs, mod, g_pre)
    ag_order = jnp.stack([_slot(*b) for b in _ag_blocks(lax.axis_index("x"), lax.axis_index("y"), lax.axis_index("c"))])
    p, wing, wcog, cwg, own2, land2 = _in_proj(h, w_in[0], ag_order.astype(jnp.int32),
                                               w_conv_out[0], w_sgu_out[0], w_o[0], conv_w[0])
    wco = wcog.reshape(1, D, D)
    cw = jnp.broadcast_to(jnp.transpose(cwg, (1, 0, 2)).reshape(KW, 1, D), (KW, 8, D))
    wq_send, wq_recv, own2_thru, land2_thru, wq_token = _wq_start(own2, land2)
    yc, ya = _branch_a_fwd(p, cw, conv_b, conv_ln_g, conv_ln_b, wco, wq_token)
    wq2 = _wq_wait(wq_send, wq_recv, own2_thru, land2_thru, yc).reshape(2, D, D)
    yb = _branch_b_fwd(p, w_sgu[0], bst, sgu_ln_g, sgu_ln_b, wq2)
    dx2, dya, dyb, dp, gw_o, st_m = _merge_loss(p, ya, yb, xs, tgt, mod, gf, wq2)

    dp, gw_so, gws, gbs, st_b = _branch_b_bwd(dp, dyb, p, w_sgu[0], bst, sgu_ln_g, sgu_ln_b, wq2)
    dp, dyc, gw_co, st_a = _branch_a_bwd1(dp, dya, yc, p, conv_ln_g, conv_ln_b, wco)
    early = _pack_early(st_a, st_b, st_m, gws, gbs)
    dp, gcw, rq, eg = _branch_a_bwd2(dp, dyc, p, cw, gw_co, gw_so, gw_o, early)
    csum, gown = _in_bwd_dw(ht, dp, my ^ jnp.array(RS_ORDER, jnp.int32))
    rs_send, rs_recv, csum_thru, land_thru, token = _rs_start(csum)
    grad_x, st_x = _in_bwd_dx(dp, wing, xs, dx2, mod, g_pre, token)
    got = _rs_wait(rs_send, rs_recv, csum_thru, land_thru, st_x)
    lg = _gather_late(_pack_late(st_x, gcw))

    def pack(name, b_ada_, g_pre_, conv_b_, lg_, lb_, sg_, sb_, b_sgu_, gfin_, w_sgu_):
        return _pack_params(name, b_ada_, g_pre_, conv_b_, lg_, lb_, sg_, sb_, b_sgu_.reshape(1, D), gfin_.reshape(1, D),
                            jnp.transpose(w_sgu_[0], (1, 0, 2)).reshape(CHUNK, D))

    pw = pack("pack_w", b_ada, g_pre, conv_b, conv_ln_g, conv_ln_b, sgu_ln_g, sgu_ln_b, b_sgu, g_final, w_sgu)
    pm = pack("pack_m", m_b_ada, m_g_pre, m_conv_b, m_conv_ln_g, m_conv_ln_b, m_sgu_ln_g, m_sgu_ln_b, m_b_sgu, m_g_final,
              m_w_sgu)
    pv = pack("pack_v", v_b_ada, v_g_pre, v_conv_b, v_conv_ln_g, v_conv_ln_b, v_sgu_ln_g, v_sgu_ln_b, v_b_sgu, v_g_final,
              v_w_sgu)
    loss11, sg_, sd_, sm_, sv_ = _small_finish(eg, lg, pw, pm, pv)

    def unpack(a):
        return dict(
            b_ada=a[0:3].reshape(1, 3 * D), g_pre=a[3:4], conv_b=a[4:5], conv_ln_g=a[5:6], conv_ln_b=a[6:7],
            sgu_ln_g=a[7:8], sgu_ln_b=a[8:9], b_sgu=a[9:10].reshape(1, HEADS, CHUNK), g_final=a[10],
            w_sgu=jnp.transpose(a[ROW_WS:ROW_WS + CHUNK].reshape(CHUNK, HEADS, CHUNK), (1, 0, 2))[None])
    small = [unpack(a) for a in (sg_, sd_, sm_, sv_)]

    g_cw = lax.dynamic_slice_in_dim(sg_[ROW_CW:ROW_CW + KW], my * 128, 128, axis=1)
    d_cw, m_cw, v_cw = _adam_f32("adam_conv_w", g_cw, conv_w[0], m_conv_w[0], v_conv_w[0])

    wcols = w_ada.shape[2]
    dm_all = jnp.concatenate([lg[:, 0], lg[:, 1], eg[:, 0]], axis=1)
    dm_mine = lax.dynamic_slice_in_dim(dm_all, my * wcols, wcols, axis=1)
    g_ada, d_ada, m_ada, v_ada = _ada_grad_adam(cg.T, dm_mine, w_ada[0], m_w_ada[0], v_w_ada[0])

    g_in, d_in, m_in, v_in = _adam_w_in(gown, got, w_in[0], m_w_in[0], v_w_in[0])
    big = {}
    for j, (nm, w_, m_, v_) in enumerate((("w_conv_out", w_conv_out, m_w_conv_out, v_w_conv_out),
                                          ("w_sgu_out", w_sgu_out, m_w_sgu_out, v_w_sgu_out),
                                          ("w_o", w_o, m_w_o, v_w_o))):
        big[nm] = _adam_reduce("adam_" + nm, rq, w_[0], m_[0], v_[0], 128,
                               pl.BlockSpec((NDEV, None, 128, D), lambda i, j=j: (0, j, 0, 0)))

    per = {
        "w_ada": tuple(a[None] for a in (g_ada, d_ada, m_ada, v_ada)),
        "w_in": tuple(a[None] for a in (g_in, d_in, m_in, v_in)),
        "conv_w": tuple(a[None] for a in (g_cw, d_cw, m_cw, v_cw)),
    }
    for nm in ("w_conv_out", "w_sgu_out", "w_o"):
        per[nm] = tuple(a[None] for a in big[nm])
    for nm in ("b_ada", "g_pre", "conv_b", "conv_ln_g", "conv_ln_b", "sgu_ln_g", "sgu_ln_b", "w_sgu", "b_sgu", "g_final"):
        per[nm] = tuple(s[nm] for s in small)

    order = ["w_ada", "b_ada", "g_pre", "w_in", "conv_w", "conv_b", "conv_ln_g", "conv_ln_b", "w_conv_out",
             "sgu_ln_g", "sgu_ln_b", "w_sgu", "b_sgu", "w_sgu_out", "w_o", "g_final"]
    outs = [loss11.reshape(()), grad_x[None]]
    for part in range(4):
        outs += [per[nm][part] for nm in order]
    return tuple(outs)
```

```python
import jax
import jax.numpy as jnp
from jax import lax
from jax.experimental import pallas as pl
from jax.experimental.pallas import tpu as pltpu

F32 = jnp.float32
BF16 = jnp.bfloat16
MESH = pl.DeviceIdType.MESH
VMEM = pltpu.VMEM
ANY = pl.ANY

D = 1024
NDEV = 8
NSEG = 8
HEADS = 8
HD = 128
CHUNK = 128
KW = 31
HALO = 32
EPS = 1e-6
TM = 256
TMB = 512
RB = 32
RBC = 32
TAPG = 4
EXT = TM + HALO
LANE_W = D // 4
LANE_HALVES = tuple(slice(k * LANE_W, (k + 1) * LANE_W) for k in range(D // LANE_W))
NSMALL = 176
NEARLY = 144
NLATE = 40
ROW_CW = 16
ROW_WS = 48
W_CO = 0
W_SO, W_O = 0, 1

ADAM_LR = 0.001
ADAM_B1 = 0.9
ADAM_B2 = 0.999
ADAM_EPS = 1e-08
ADAM_WD = 0.01
ADAM_STEP = 10

INV_SQRT2 = 0.7071067811865476
INV_SQRT_2PI = 0.3989422804014327

NT = (((1,), (1,)), ((), ()))
TN = (((0,), (0,)), ((), ()))


def _cp(sem=None, vmem_mb=48):
    return pltpu.CompilerParams(dimension_semantics=sem, vmem_limit_bytes=vmem_mb * 1024 * 1024)


def _me():
    return lax.axis_index("x"), lax.axis_index("y"), lax.axis_index("c")


def _slot(px, py, pc):
    return 4 * px + 2 * py + pc


def _xor_peer(k):
    x, y, c = _me()
    return (1 - x if k & 4 else x, 1 - y if k & 2 else y, 1 - c if k & 1 else c)


def _allgather_phases(srcs, dst_at, send_sems, recv_sems, loc_sems):
    x, y, c = _me()
    me = (x, y, c)
    sibling = (x, y, 1 - c)
    chips = [(1 - x, y), (x, 1 - y), (1 - x, 1 - y)]
    na = len(srcs)

    def copy(a, k, block, to, src=None):
        d = dst_at(a, _slot(*block))
        return pltpu.make_async_remote_copy(
            src_ref=d if src is None else src, dst_ref=d,
            send_sem=send_sems.at[a, k], recv_sem=recv_sems.at[a, k],
            device_id=to, device_id_type=MESH)

    local = [pltpu.make_async_copy(srcs[a], dst_at(a, _slot(*me)), loc_sems.at[a]) for a in range(na)]
    first = []
    for a in range(na):
        first.append(copy(a, 0, me, sibling, src=srcs[a]))
        for j, chip in enumerate(chips):
            first.append(copy(a, 1 + j, me, (*chip, c), src=srcs[a]))
    passed = [copy(a, 4 + j, (*chip, c), sibling) for j, chip in enumerate(chips) for a in range(na)]

    def start():
        for cp in local + first:
            cp.start()

    def relay():
        for j, chip in enumerate(chips):
            for a in range(na):
                copy(a, 1 + j, (*chip, c), me).wait_recv()
                passed[j * na + a].start()

    def finish():
        for a in range(na):
            copy(a, 0, sibling, me).wait_recv()
        for j, chip in enumerate(chips):
            for a in range(na):
                copy(a, 4 + j, (*chip, 1 - c), me).wait_recv()
        for cp in first + passed:
            cp.wait_send()
        for cp in local:
            cp.wait()

    return start, relay, finish


def _allgather(srcs, dst_at, send_sems, recv_sems, loc_sems):
    start, relay, finish = _allgather_phases(srcs, dst_at, send_sems, recv_sems, loc_sems)
    start()
    relay()
    finish()


def _owner_exchange(gq, rq_ref, x_send, x_recv, x_loc):
    mx, my_, mc = _me()
    me = _slot(mx, my_, mc)

    def rows_of(a, slot):
        return gq[a].at[pl.ds(pl.multiple_of(slot * 128, 128), 128), :]

    def exchange(k, a, recv):
        px, py, pc = _xor_peer(k)
        peer = _slot(px, py, pc)
        return pltpu.make_async_remote_copy(
            src_ref=rows_of(a, me if recv else peer), dst_ref=rq_ref.at[peer if recv else me, a],
            send_sem=x_send.at[a, k - 1], recv_sem=x_recv.at[a, k - 1],
            device_id=(px, py, pc), device_id_type=MESH)

    local = [pltpu.make_async_copy(rows_of(a, me), rq_ref.at[me, a], x_loc.at[a]) for a in range(len(gq))]

    def start():
        for cp in local:
            cp.start()
        for k in range(1, NDEV):
            for a in range(len(gq)):
                exchange(k, a, False).start()

    def wait():
        for k in range(1, NDEV):
            for a in range(len(gq)):
                exchange(k, a, True).wait_recv()
        for k in range(1, NDEV):
            for a in range(len(gq)):
                exchange(k, a, False).wait_send()
        for cp in local:
            cp.wait()

    return start, wait


def _sig(v):
    return jax.nn.sigmoid(v)


def _gelu_parts(v):
    cdf = 0.5 * (1.0 + lax.erf(v * INV_SQRT2))
    pdf = jnp.exp(-0.5 * v * v) * INV_SQRT_2PI
    return v * cdf, cdf + v * pdf


def _ln_stats(v):
    mu = jnp.mean(v, axis=-1, keepdims=True)
    xc = v - mu
    var = jnp.mean(xc * xc, axis=-1, keepdims=True)
    rs = lax.rsqrt(var + EPS)
    return xc * rs, rs


def _ln_bwd(dn, n, rs):
    return rs * (dn - jnp.mean(dn, axis=-1, keepdims=True) - n * jnp.mean(dn * n, axis=-1, keepdims=True))


def _fold8(v):
    acc = v[0:8]
    for r in range(8, v.shape[0], 8):
        acc = acc + v[r:r + 8]
    return acc


def _rows(rb, n=RB):
    return pl.ds(rb * n, n)


def _ld(ref, rows):
    return ref[rows, :].astype(F32)


def _blocks(n, body, init=0):
    carry = init
    for rb in range(n):
        carry = body(rb, carry)
    return carry


def _shift_copies(s_ref):
    n = EXT - 8
    for m in range(1, 8):
        for c0 in range(0, n, 56):
            s_ref[m, c0:c0 + 56, :] = s_ref[0, c0 + m:c0 + m + 56, :]


def _ada_exchange(c_ref, w_ref, b_ref, mod_ref, call_ref, cdst, msrc, mdst, c_send, c_recv, m_send, m_recv):
    x, y, c = _me()
    me = _slot(x, y, c)

    def all_to_all(src_of, dst, ss, rs):
        def cp(k, recv):
            px, py, pc = _xor_peer(k)
            peer = _slot(px, py, pc)
            return pltpu.make_async_remote_copy(
                src_ref=src_of(me if recv else peer), dst_ref=dst.at[peer if recv else me],
                send_sem=ss.at[k - 1], recv_sem=rs.at[k - 1], device_id=(px, py, pc), device_id_type=MESH)
        for k in range(1, NDEV):
            cp(k, False).start()
        for k in range(1, NDEV):
            cp(k, True).wait_recv()
        for k in range(1, NDEV):
            cp(k, False).wait_send()

    cdst[me] = c_ref[...]
    all_to_all(lambda s: c_ref, cdst, c_send, c_recv)
    for b in range(NDEV):
        call_ref[b:b + 1, :] = cdst[b]
    m = jnp.dot(call_ref[...], w_ref[...], preferred_element_type=F32, precision=lax.Precision.HIGHEST)
    for b in range(NDEV):
        msrc[b] = m[b:b + 1, :]
    mdst[me] = msrc[me]
    all_to_all(lambda s: msrc.at[s], mdst, m_send, m_recv)
    full = jnp.concatenate([mdst[k] for k in range(NDEV)], axis=1) + b_ref[...]
    for r in range(3):
        mod_ref[r:r + 1, :] = full[:, r * D:(r + 1) * D]


def _ada_fwd(c, w_ada, b_ada):
    wcols = w_ada.shape[1]

    def body(*refs):
        _ada_exchange(*refs)

    vm = pl.BlockSpec(memory_space=VMEM)
    return pl.pallas_call(
        body, name="ada_fwd",
        out_shape=(jax.ShapeDtypeStruct((3, D), F32), jax.ShapeDtypeStruct((NDEV, D), F32)),
        in_specs=[vm, vm, vm], out_specs=(vm, vm),
        scratch_shapes=[VMEM((NDEV, 1, D), F32), VMEM((NDEV, 1, wcols), F32), VMEM((NDEV, 1, wcols), F32),
                        pltpu.SemaphoreType.DMA((7,)), pltpu.SemaphoreType.DMA((7,)),
                        pltpu.SemaphoreType.DMA((7,)), pltpu.SemaphoreType.DMA((7,))],
        compiler_params=_cp(vmem_mb=32),
    )(c, w_ada, b_ada)


def _prep_h(x, mod, g_pre):
    T = x.shape[0]

    def body(x_ref, mod_ref, g_ref, h_ref, ht_ref):
        def blk(rb, carry):
            rows = _rows(rb)
            xv = x_ref[rows, :]
            r = lax.rsqrt(jnp.mean(xv * xv, axis=-1, keepdims=True) + EPS)
            h_ref[rows, :] = ((xv * r) * g_ref[...] * (1.0 + mod_ref[1:2, :]) + mod_ref[0:1, :]).astype(BF16)
            return carry
        _blocks(TMB // RB, blk)
        ht_ref[...] = h_ref[...].T

    return pl.pallas_call(
        body, name="prep_h", grid=(T // TMB,),
        out_shape=(jax.ShapeDtypeStruct((T, D), BF16), jax.ShapeDtypeStruct((D, T), BF16)),
        in_specs=[pl.BlockSpec((TMB, D), lambda i: (i, 0)), pl.BlockSpec((3, D), lambda i: (0, 0)),
                  pl.BlockSpec((1, D), lambda i: (0, 0))],
        out_specs=(pl.BlockSpec((TMB, D), lambda i: (i, 0)), pl.BlockSpec((D, TMB), lambda i: (0, i))),
        compiler_params=_cp(("parallel",)),
    )(x, mod, g_pre)


def _ag_blocks(x, y, c):
    a = 1 - c
    n1 = (x ^ a, y ^ c)
    n2 = (x ^ c, y ^ a)
    dg = (1 - x, 1 - y)
    return [(x, y, c), (x, y, a), (*n1, c), (*n2, c), (*n1, a), (*n2, a), (*dg, c), (*dg, a)]


def _in_proj(h, w_in, order, w_co, w_so, w_o, conv_w):
    T = h.shape[0]
    tmp = min(2 * TMB, T)
    nb = T // tmp

    def body(order_ref, h_ref, w_ref, wco_ref, wso_ref, wo_ref, cw_ref, p_ref, wing_ref, wqg_ref, cwg_ref, own2_ref, land2_ref,
             wbuf, wq_st, wq2_st, send_sems, recv_sems, out_sems, ag_send, ag_recv, ag_loc, own_sems):
        j = pl.program_id(0)
        i = pl.program_id(1)
        blocks = _ag_blocks(*_me())
        me, sibling, n1, n2, n1o, n2o, dg, dgo = blocks
        own2 = [pltpu.make_async_copy(wq2_st, own2_ref, own_sems.at[0]),
                pltpu.make_async_copy(wq2_st, land2_ref.at[:, _slot(*me)], own_sems.at[1])]
        ag_start, ag_relay, ag_finish = _allgather_phases(
            [wq_st, cw_ref], lambda a, slot: wqg_ref.at[:, slot] if a == 0 else cwg_ref.at[slot], ag_send, ag_recv, ag_loc)

        def copy(k, block, to):
            d = wbuf.at[_slot(*block)]
            return pltpu.make_async_remote_copy(src_ref=d, dst_ref=d, send_sem=send_sems.at[k], recv_sem=recv_sems.at[k],
                                                device_id=to, device_id_type=MESH)

        def writeback(jj):
            s = _slot(*blocks[jj])
            return pltpu.make_async_copy(wbuf.at[s], wing_ref.at[s], out_sems.at[jj])

        sends = [copy(0, me, sibling), copy(1, me, n1), copy(2, me, n2)]
        relay = copy(3, n1, n2)
        passed = [copy(4, n1, sibling), copy(5, n2, sibling), copy(6, dg, sibling)]
        arrivals = {1: [(0, sibling)], 2: [(1, n1), (2, n2)], 4: [(5, n1o)], 5: [(4, n2o)], 6: [(3, dg)], 7: [(6, dgo)]}

        @pl.when((j == 0) & (i == 0))
        def _():
            wbuf[_slot(*me)] = w_ref[...].astype(BF16)
            for cp in sends:
                cp.start()
            writeback(0).start()
            wq2_st[W_SO] = wso_ref[...].astype(BF16)
            wq2_st[W_O] = wo_ref[...].astype(BF16)
            for cp in own2:
                cp.start()

        for jj in range(1, NSEG):
            @pl.when((j == jj) & (i == 0))
            def _(jj=jj):
                for k, block in arrivals.get(jj, []):
                    copy(k, block, me).wait_recv()
                if jj == 2:
                    relay.start()
                    passed[0].start()
                    passed[1].start()
                    writeback(3).start()
                    wq_st[0] = wco_ref[...].astype(BF16)
                    ag_start()
                if jj == 6:
                    passed[2].start()
                if jj != 3:
                    writeback(jj).start()
                if jj == NSEG - 1:
                    ag_relay()

        p_ref[...] = jnp.dot(h_ref[...], wbuf[order_ref[j]], preferred_element_type=F32).astype(BF16)

        @pl.when((j == NSEG - 1) & (i == nb - 1))
        def _():
            for cp in sends + [relay] + passed:
                cp.wait_send()
            for jj in range(NSEG):
                writeback(jj).wait()
            for cp in own2:
                cp.wait()
            ag_finish()

    vm = pl.BlockSpec(memory_space=VMEM)
    hbm = pl.BlockSpec(memory_space=ANY)
    return pl.pallas_call(
        body, name="in_proj",
        out_shape=(jax.ShapeDtypeStruct((T, NSEG * D), BF16), jax.ShapeDtypeStruct((NSEG, D, D), BF16),
                   jax.ShapeDtypeStruct((1, NDEV, 128, D), BF16), jax.ShapeDtypeStruct((NDEV, KW, 128), F32),
                   jax.ShapeDtypeStruct((2, 128, D), BF16), jax.ShapeDtypeStruct((2, NDEV, 128, D), BF16)),
        grid_spec=pltpu.PrefetchScalarGridSpec(
            num_scalar_prefetch=1, grid=(NSEG, nb),
            in_specs=[pl.BlockSpec((tmp, D), lambda j, i, o: (i, 0)), vm, vm, vm, vm, vm],
            out_specs=(pl.BlockSpec((tmp, D), lambda j, i, o: (i, o[j])), hbm, hbm, hbm, hbm, hbm),
            scratch_shapes=[VMEM((NSEG, D, D), BF16), VMEM((1, 128, D), BF16), VMEM((2, 128, D), BF16),
                            pltpu.SemaphoreType.DMA((7,)), pltpu.SemaphoreType.DMA((7,)), pltpu.SemaphoreType.DMA((NSEG,)),
                            pltpu.SemaphoreType.DMA((2, 7)), pltpu.SemaphoreType.DMA((2, 7)), pltpu.SemaphoreType.DMA((2,)),
                            pltpu.SemaphoreType.DMA((2,))]),
        compiler_params=_cp(("arbitrary", "arbitrary")),
    )(order, h, w_in, w_co, w_so, w_o, conv_w)


def _wq_start(own2, land2):
    def body(src_ref, land_ref, send_sems, recv_sems, src_thru, land_thru, token):
        barrier = pltpu.get_barrier_semaphore()
        for k in range(1, NDEV):
            pl.semaphore_signal(barrier, inc=1, device_id=_xor_peer(k), device_id_type=MESH)
        pl.semaphore_wait(barrier, NDEV - 1)
        x, y, c = _me()
        for k in range(1, NDEV):
            pltpu.make_async_remote_copy(src_ref=src_ref, dst_ref=land_ref.at[:, _slot(x, y, c)], send_sem=send_sems.at[k - 1],
                                         recv_sem=recv_sems.at[k - 1], device_id=_xor_peer(k), device_id_type=MESH).start()
        token[...] = jnp.zeros_like(token)

    hbm = pl.BlockSpec(memory_space=pltpu.HBM)
    sem = pl.BlockSpec(memory_space=pltpu.SEMAPHORE)
    return pl.pallas_call(
        body, name="wq_start",
        out_shape=(pltpu.SemaphoreType.DMA((NDEV - 1,)), pltpu.SemaphoreType.DMA((NDEV - 1,)),
                   pltpu.HBM(own2.shape, own2.dtype), pltpu.HBM(land2.shape, land2.dtype), jax.ShapeDtypeStruct((8, 128), F32)),
        in_specs=(hbm, hbm), out_specs=(sem, sem, hbm, hbm, pl.BlockSpec(memory_space=VMEM)),
        input_output_aliases={0: 2, 1: 3},
        compiler_params=pltpu.CompilerParams(has_side_effects=pltpu.SideEffectType.DATAFLOW_SIDE_EFFECTING, collective_id=2),
    )(pltpu.with_memory_space_constraint(own2, pltpu.HBM), pltpu.with_memory_space_constraint(land2, pltpu.HBM))


def _wq_wait(send_sems, recv_sems, src_thru, land_thru, after):
    def body(src_ref, land_ref, send_sems, recv_sems, after_ref, src_dead, got_ref):
        del after_ref, src_dead, got_ref
        for k in range(1, NDEV):
            px, py, pc = _xor_peer(k)
            cp = pltpu.make_async_remote_copy(src_ref=src_ref, dst_ref=land_ref.at[:, _slot(px, py, pc)],
                                              send_sem=send_sems.at[k - 1], recv_sem=recv_sems.at[k - 1],
                                              device_id=(px, py, pc), device_id_type=MESH)
            cp.wait_send()
            cp.wait_recv()

    hbm = pl.BlockSpec(memory_space=pltpu.HBM)
    sem = pl.BlockSpec(memory_space=pltpu.SEMAPHORE)
    return pl.pallas_call(
        body, name="wq_wait",
        out_shape=(pltpu.HBM(src_thru.shape, src_thru.dtype), pltpu.HBM(land_thru.shape, land_thru.dtype)),
        in_specs=(hbm, hbm, sem, sem, pl.BlockSpec(memory_space=ANY)), out_specs=(hbm, hbm),
        input_output_aliases={0: 0, 1: 1},
        compiler_params=pltpu.CompilerParams(has_side_effects=pltpu.SideEffectType.DATAFLOW_SIDE_EFFECTING),
    )(src_thru, land_thru, send_sems, recv_sems, after)[1]


def _fill_a_ext(i, s_ref, val_ref, glu_ref, hval_ref, hglu_ref):
    ah = hval_ref[...].astype(F32) * _sig(hglu_ref[...].astype(F32))
    s_ref[0, 0:HALO, :] = jnp.where(i > 0, ah, 0.0)

    def fill(rb, carry):
        rows = _rows(rb)
        s_ref[0, pl.ds(HALO + rb * RB, RB), :] = _ld(val_ref, rows) * _sig(_ld(glu_ref, rows))
        return carry
    _blocks(TM // RB, fill)


def _halo_prev(seg):
    hb = TM // HALO
    return pl.BlockSpec((HALO, D), lambda i: (jnp.maximum(i * hb - 1, 0), seg))


def _branch_a_fwd(p, cw, conv_b, lg, lb, wq, token):
    T = p.shape[0]

    def body(val_ref, glu_ref, z_ref, hval_ref, hglu_ref, cw_ref, cb_ref, lg_ref, lb_ref, w_ref, token_ref,
             yc_ref, ya_ref, s_ref, yp_ref):
        del token_ref
        i = pl.program_id(0)
        _fill_a_ext(i, s_ref, val_ref, glu_ref, hval_ref, hglu_ref)
        _shift_copies(s_ref)

        def conv(rb, carry):
            r0 = rb * RBC
            for lanes in LANE_HALVES:
                accs = [jnp.broadcast_to(cb_ref[:, lanes], (8, LANE_W))] * (RBC // 8)
                for k in range(KW):
                    o = 2 + k
                    w = cw_ref[k, :, lanes]
                    accs = [acc + w * s_ref[o % 8, pl.ds(r0 + 8 * (o // 8 + g), 8), lanes]
                            for g, acc in enumerate(accs)]
                for g, acc in enumerate(accs):
                    yc_ref[pl.ds(r0 + 8 * g, 8), lanes] = acc
            return carry
        for rb in range(TM // RBC):
            conv(rb, 0)

        def post(rb, carry):
            rows = _rows(rb)
            n, _ = _ln_stats(yc_ref[rows, :])
            l = n * lg_ref[...] + lb_ref[...]
            z = _ld(z_ref, rows)
            yp_ref[rows, :] = ((l * _sig(l)) * (z * _sig(z))).astype(BF16)
            return carry
        _blocks(TM // RB, post)
        ya_ref[...] = jnp.dot(yp_ref[...], w_ref[...], preferred_element_type=F32)

    tile = lambda seg: pl.BlockSpec((TM, D), lambda i: (i, seg))
    row = pl.BlockSpec((1, D), lambda i: (0, 0))
    return pl.pallas_call(
        body, name="branch_a_fwd", grid=(T // TM,),
        out_shape=(jax.ShapeDtypeStruct((T, D), F32), jax.ShapeDtypeStruct((T, D), F32)),
        in_specs=[tile(0), tile(1), tile(2), _halo_prev(0), _halo_prev(1),
                  pl.BlockSpec((KW, 8, D), lambda i: (0, 0, 0)), row, row, row,
                  pl.BlockSpec((None, D, D), lambda i: (W_CO, 0, 0)), pl.BlockSpec((8, 128), lambda i: (0, 0))],
        out_specs=(pl.BlockSpec((TM, D), lambda i: (i, 0)), pl.BlockSpec((TM, D), lambda i: (i, 0))),
        scratch_shapes=[VMEM((8, EXT, D), F32), VMEM((TM, D), BF16)],
        compiler_params=_cp(("parallel",)),
    )(p, p, p, p, p, cw, conv_b, lg, lb, wq, token)


def _masked_ws(ws_ref, wt_ref):
    tri = lax.broadcasted_iota(jnp.int32, (CHUNK, CHUNK), 0) >= lax.broadcasted_iota(jnp.int32, (CHUNK, CHUNK), 1)
    for h in range(HEADS):
        wt_ref[h] = jnp.where(tri, ws_ref[h], 0.0).astype(BF16)


def _sgu_mix(wt_ref, vl_ref, bst_ref, s_ref):
    for h in range(HEADS):
        cs = slice(h * HD, (h + 1) * HD)
        s = jnp.dot(wt_ref[h], _chunks_on_lanes(vl_ref, cs), preferred_element_type=F32) + bst_ref[:, h:h + 1]
        for ck in range(TM // CHUNK):
            s_ref[ck * CHUNK:(ck + 1) * CHUNK, cs] = s[:, ck * HD:(ck + 1) * HD]


def _chunks_on_lanes(ref, cs):
    return jnp.concatenate([ref[ck * CHUNK:(ck + 1) * CHUNK, cs] for ck in range(TM // CHUNK)], axis=1)


def _branch_b_fwd(p, ws, bst, sg, sb, wq):
    T = p.shape[0]

    def body(pu_ref, pv_ref, pz_ref, ws_ref, bst_ref, sg_ref, sb_ref, w_ref, yb_ref,
             wt_ref, vl_ref, t_ref, s_ref, yp_ref):
        _masked_ws(ws_ref, wt_ref)

        def pre(rb, carry):
            rows = _rows(rb)
            vg, _ = _gelu_parts(_ld(pv_ref, rows))
            vn, _ = _ln_stats(vg)
            vl_ref[rows, :] = (vn * sg_ref[...] + sb_ref[...]).astype(BF16)
            u, _ = _gelu_parts(_ld(pu_ref, rows))
            z = _ld(pz_ref, rows)
            t_ref[rows, :] = u * (z * _sig(z))
            return carry
        _blocks(TM // RB, pre)
        _sgu_mix(wt_ref, vl_ref, bst_ref, s_ref)

        def post(rb, carry):
            rows = _rows(rb)
            yp_ref[rows, :] = (t_ref[rows, :] * s_ref[rows, :]).astype(BF16)
            return carry
        _blocks(TM // RB, post)
        yb_ref[...] = jnp.dot(yp_ref[...], w_ref[...], preferred_element_type=F32)

    tile = lambda seg: pl.BlockSpec((TM, D), lambda i: (i, seg))
    row = pl.BlockSpec((1, D), lambda i: (0, 0))
    return pl.pallas_call(
        body, name="branch_b_fwd", grid=(T // TM,),
        out_shape=jax.ShapeDtypeStruct((T, D), F32),
        in_specs=[tile(3), tile(4), tile(5), pl.BlockSpec((HEADS, CHUNK, CHUNK), lambda i: (0, 0, 0)),
                  pl.BlockSpec((CHUNK, HEADS), lambda i: (0, 0)), row, row,
                  pl.BlockSpec((None, D, D), lambda i: (W_SO, 0, 0))],
        out_specs=pl.BlockSpec((TM, D), lambda i: (i, 0)),
        scratch_shapes=[VMEM((HEADS, CHUNK, CHUNK), BF16), VMEM((TM, D), BF16), VMEM((TM, D), F32),
                        VMEM((TM, D), F32), VMEM((TM, D), BF16)],
        compiler_params=_cp(("parallel",)),
    )(p, p, p, ws, bst, sg, sb, wq)


def _merge_loss(p, ya, yb, x, tgt, mod, g_final, wq):
    T = x.shape[0]
    nh = 2 if T % (2 * TM) == 0 else 1
    tm2 = nh * TM
    nt = T // tm2
    nbh = TM // RB

    def body(ga_ref, gb_ref, ya_ref, yb_ref, x_ref, t_ref, mod_ref, gf_ref, w_ref,
             dx2_ref, dya_ref, dyb_ref, dp_ref, gwo_ref, st_ref, *scratch):
        mrg, out, dout, dm = (scratch[k * nh:(k + 1) * nh] for k in range(4))
        gw_acc = scratch[4 * nh]
        i = pl.program_id(0)

        @pl.when(i == 0)
        def _():
            gw_acc[...] = jnp.zeros_like(gw_acc)
            st_ref[...] = jnp.zeros_like(st_ref)

        def merge(h):
            def f(rb, carry):
                rows, loc = _rows(h * nbh + rb), _rows(rb)
                mrg[h][loc, :] = (_sig(_ld(ga_ref, rows)) * ya_ref[rows, :]
                                  + _sig(_ld(gb_ref, rows)) * yb_ref[rows, :]).astype(BF16)
                return carry
            _blocks(nbh, f)

        def head(h, carry):
            def f(rb, carry):
                loss, gg, dg = carry
                rows, loc = _rows(h * nbh + rb), _rows(rb)
                gate = mod_ref[2:3, :]
                gf = gf_ref[...]
                o = out[h][loc, :]
                x2 = x_ref[rows, :] + gate * o
                r2 = lax.rsqrt(jnp.mean(x2 * x2, axis=-1, keepdims=True) + EPS)
                x2n = x2 * r2
                diff = x2n * gf - t_ref[rows, :]
                dy = diff * (1.0 / D)
                dx2n = dy * gf
                dx2 = r2 * (dx2n - x2n * jnp.mean(dx2n * x2n, axis=-1, keepdims=True))
                dx2_ref[rows, :] = dx2
                dout[h][loc, :] = (dx2 * gate).astype(BF16)
                return loss + _fold8(diff * diff), gg + _fold8(dy * x2n), dg + _fold8(dx2 * o)
            return _blocks(nbh, f, carry)

        def split(h):
            def f(rb, carry):
                rows, loc = _rows(h * nbh + rb), _rows(rb)
                d = dm[h][loc, :]
                sa = _sig(_ld(ga_ref, rows))
                sb = _sig(_ld(gb_ref, rows))
                dya_ref[rows, :] = (d * sa).astype(BF16)
                dyb_ref[rows, :] = (d * sb).astype(BF16)
                dp_ref[0, rows, :] = (d * ya_ref[rows, :] * (sa * (1.0 - sa))).astype(BF16)
                dp_ref[1, rows, :] = (d * yb_ref[rows, :] * (sb * (1.0 - sb))).astype(BF16)
                return carry
            _blocks(nbh, f)

        for h in range(nh):
            merge(h)
        for h in range(nh):
            out[h][...] = jnp.dot(mrg[h][...], w_ref[...], preferred_element_type=F32)
        zero = jnp.zeros((8, D), F32)
        carry = (zero, zero, zero)
        for h in range(nh):
            carry = head(h, carry)
        loss, gg, dg = carry
        st_ref[0] = st_ref[0] + loss * (0.5 / D)
        st_ref[1] = st_ref[1] + gg
        st_ref[2] = st_ref[2] + dg
        for h in range(nh):
            dm[h][...] = lax.dot_general(dout[h][...], w_ref[...], NT, preferred_element_type=F32)
            gw_acc[...] += lax.dot_general(mrg[h][...], dout[h][...], TN, preferred_element_type=F32)
        for h in range(nh):
            split(h)

        @pl.when(i == nt - 1)
        def _():
            gwo_ref[...] = gw_acc[...].astype(BF16)

    tile = pl.BlockSpec((tm2, D), lambda i: (i, 0))
    halves = lambda dt: [VMEM((TM, D), dt)] * nh
    return pl.pallas_call(
        body, name="merge_loss", grid=(nt,),
        out_shape=(jax.ShapeDtypeStruct((T, D), F32), jax.ShapeDtypeStruct((T, D), BF16), jax.ShapeDtypeStruct((T, D), BF16),
                   jax.ShapeDtypeStruct((NSEG, T, D), BF16), jax.ShapeDtypeStruct((D, D), BF16),
                   jax.ShapeDtypeStruct((3, 8, D), F32)),
        in_specs=[pl.BlockSpec((tm2, D), lambda i: (i, 6)), pl.BlockSpec((tm2, D), lambda i: (i, 7)), tile, tile, tile, tile,
                  pl.BlockSpec((3, D), lambda i: (0, 0)), pl.BlockSpec((1, D), lambda i: (0, 0)),
                  pl.BlockSpec((None, D, D), lambda i: (W_O, 0, 0))],
        out_specs=(tile, tile, tile, pl.BlockSpec((2, tm2, D), lambda i: (3, i, 0)),
                   pl.BlockSpec((D, D), lambda i: (0, 0)), pl.BlockSpec((3, 8, D), lambda i: (0, 0, 0))),
        scratch_shapes=halves(BF16) + halves(F32) + halves(BF16) + halves(F32) + [VMEM((D, D), F32)],
        compiler_params=_cp(("arbitrary",)),
    )(p, p, ya, yb, x, tgt, mod, g_final, wq)


def _branch_b_bwd(dp, dyb, p, ws, bst, sg, sb, wq):
    T = p.shape[0]
    nt = T // TM

    def body(dp_in, dyb_ref, pu_ref, pv_ref, pz_ref, ws_ref, bst_ref, sg_ref, sb_ref, w_ref,
             dp_ref, gw_ref, gws_ref, gbs_ref, st_ref,
             wt_ref, d_s, vl_s, vn_s, gpv_s, rs_s, s_s, ds_s, ds32_s, yp_s, dvl_s, gw_acc):
        del dp_in
        i = pl.program_id(0)

        @pl.when(i == 0)
        def _():
            gw_acc[...] = jnp.zeros_like(gw_acc)
            gws_ref[...] = jnp.zeros_like(gws_ref)
            gbs_ref[...] = jnp.zeros_like(gbs_ref)
            st_ref[...] = jnp.zeros_like(st_ref)
            _masked_ws(ws_ref, wt_ref)

        d_s[...] = lax.dot_general(dyb_ref[...], w_ref[...], NT, preferred_element_type=F32)

        def pre(rb, carry):
            rows = _rows(rb)
            vg, gpv = _gelu_parts(_ld(pv_ref, rows))
            vn, rs = _ln_stats(vg)
            vl_s[rows, :] = (vn * sg_ref[...] + sb_ref[...]).astype(BF16)
            vn_s[rows, :] = vn
            gpv_s[rows, :] = gpv
            rs_s[rows, :] = rs
            return carry
        _blocks(TM // RB, pre)
        _sgu_mix(wt_ref, vl_s, bst_ref, s_s)

        def mid(rb, carry):
            rows = _rows(rb)
            u, gpu = _gelu_parts(_ld(pu_ref, rows))
            z = _ld(pz_ref, rows)
            sz = _sig(z)
            siluz = z * sz
            d = d_s[rows, :]
            s = s_s[rows, :]
            t = u * siluz
            yp_s[rows, :] = (t * s).astype(BF16)
            dp_ref[0, rows, :] = (d * s * siluz * gpu).astype(BF16)
            ds = d * t
            ds32_s[rows, :] = ds
            ds_s[rows, :] = ds.astype(BF16)
            dp_ref[2, rows, :] = (d * u * s * (sz * (1.0 + z * (1.0 - sz)))).astype(BF16)
            return carry
        _blocks(TM // RB, mid)

        for h in range(HEADS):
            cs = slice(h * HD, (h + 1) * HD)
            dsh = _chunks_on_lanes(ds_s, cs)
            dvl = lax.dot_general(wt_ref[h], dsh, TN, preferred_element_type=F32)
            gbs = gbs_ref[h]
            for ck in range(TM // CHUNK):
                r = slice(ck * CHUNK, (ck + 1) * CHUNK)
                dvl_s[r, cs] = dvl[:, ck * HD:(ck + 1) * HD]
                gbs = gbs + ds32_s[r, cs]
            gbs_ref[h] = gbs
            gws_ref[h] += lax.dot_general(dsh, _chunks_on_lanes(vl_s, cs), NT, preferred_element_type=F32)

        def post(rb, carry):
            g_sg, g_sb = carry
            rows = _rows(rb)
            dvl = dvl_s[rows, :]
            vn = vn_s[rows, :]
            dvg = _ln_bwd(dvl * sg_ref[...], vn, rs_s[rows, :])
            dp_ref[1, rows, :] = (dvg * gpv_s[rows, :]).astype(BF16)
            return g_sg + _fold8(dvl * vn), g_sb + _fold8(dvl)
        zero = jnp.zeros((8, D), F32)
        g_sg, g_sb = _blocks(TM // RB, post, (zero, zero))
        st_ref[0] = st_ref[0] + g_sg
        st_ref[1] = st_ref[1] + g_sb

        gw_acc[...] += lax.dot_general(yp_s[...], dyb_ref[...], TN, preferred_element_type=F32)

        @pl.when(i == nt - 1)
        def _():
            gw_ref[...] = gw_acc[...].astype(BF16)

    tile = lambda seg: pl.BlockSpec((TM, D), lambda i: (i, seg))
    row = pl.BlockSpec((1, D), lambda i: (0, 0))
    hh = pl.BlockSpec((HEADS, CHUNK, CHUNK), lambda i: (0, 0, 0))
    return pl.pallas_call(
        body, name="branch_b_bwd", grid=(nt,),
        out_shape=(jax.ShapeDtypeStruct((NSEG, T, D), BF16), jax.ShapeDtypeStruct((D, D), BF16),
                   jax.ShapeDtypeStruct((HEADS, CHUNK, CHUNK), F32), jax.ShapeDtypeStruct((HEADS, CHUNK, HD), F32),
                   jax.ShapeDtypeStruct((2, 8, D), F32)),
        in_specs=[pl.BlockSpec(memory_space=ANY), pl.BlockSpec((TM, D), lambda i: (i, 0)), tile(3), tile(4), tile(5),
                  hh, pl.BlockSpec((CHUNK, HEADS), lambda i: (0, 0)), row, row,
                  pl.BlockSpec((None, D, D), lambda i: (W_SO, 0, 0))],
        out_specs=(pl.BlockSpec((3, TM, D), lambda i: (1, i, 0)), pl.BlockSpec((D, D), lambda i: (0, 0)), hh, hh,
                   pl.BlockSpec((2, 8, D), lambda i: (0, 0, 0))),
        scratch_shapes=[VMEM((HEADS, CHUNK, CHUNK), BF16), VMEM((TM, D), F32), VMEM((TM, D), BF16), VMEM((TM, D), F32),
                        VMEM((TM, D), F32), VMEM((TM, 1), F32), VMEM((TM, D), F32), VMEM((TM, D), BF16), VMEM((TM, D), F32),
                        VMEM((TM, D), BF16), VMEM((TM, D), F32), VMEM((D, D), F32)],
        input_output_aliases={0: 0},
        compiler_params=_cp(("arbitrary",)),
    )(dp, dyb, p, p, p, ws, bst, sg, sb, wq)


def _branch_a_bwd1(dp, dya, yc, p, lg, lb, wq):
    T = p.shape[0]
    nh = 2 if T % (2 * TM) == 0 else 1
    tm2 = nh * TM
    nt = T // tm2
    nbh = TM // RB

    def body(dp_in, dya_ref, yc_ref, z_ref, lg_ref, lb_ref, w_ref, dp_ref, dyc_ref, gw_ref, st_ref, *scratch):
        del dp_in
        d_s, yp_s = scratch[0:nh], scratch[nh:2 * nh]
        gw_acc = scratch[2 * nh]
        i = pl.program_id(0)
        half = lambda h: slice(h * TM, (h + 1) * TM)

        @pl.when(i == 0)
        def _():
            gw_acc[...] = jnp.zeros_like(gw_acc)
            st_ref[...] = jnp.zeros_like(st_ref)

        for h in range(nh):
            d_s[h][...] = lax.dot_general(dya_ref[half(h), :], w_ref[...], NT, preferred_element_type=F32)

        def blk(h):
            def f(rb, carry):
                g_lg, g_lb, g_cb = carry
                rows, loc = _rows(h * nbh + rb), _rows(rb)
                n, rs = _ln_stats(yc_ref[rows, :])
                l = n * lg_ref[...] + lb_ref[...]
                sgl = _sig(l)
                sl = l * sgl
                z = _ld(z_ref, rows)
                sz = _sig(z)
                siluz = z * sz
                d = d_s[h][loc, :]
                yp_s[h][loc, :] = (sl * siluz).astype(BF16)
                dp_ref[rows, :] = (d * sl * (sz * (1.0 + z * (1.0 - sz)))).astype(BF16)
                dl = (d * siluz) * (sgl * (1.0 + l * (1.0 - sgl)))
                dyc = _ln_bwd(dl * lg_ref[...], n, rs)
                dyc_ref[rows, :] = dyc
                return g_lg + _fold8(dl * n), g_lb + _fold8(dl), g_cb + _fold8(dyc)
            return f
        zero = jnp.zeros((8, D), F32)
        carry = (zero, zero, zero)
        for h in range(nh):
            carry = _blocks(nbh, blk(h), carry)
        g_lg, g_lb, g_cb = carry
        st_ref[0] = st_ref[0] + g_lg
        st_ref[1] = st_ref[1] + g_lb
        st_ref[2] = st_ref[2] + g_cb

        for h in range(nh):
            gw_acc[...] += lax.dot_general(yp_s[h][...], dya_ref[half(h), :], TN, preferred_element_type=F32)

        @pl.when(i == nt - 1)
        def _():
            gw_ref[...] = gw_acc[...].astype(BF16)

    tile = pl.BlockSpec((tm2, D), lambda i: (i, 0))
    row = pl.BlockSpec((1, D), lambda i: (0, 0))
    return pl.pallas_call(
        body, name="branch_a_bwd1", grid=(nt,),
        out_shape=(jax.ShapeDtypeStruct((NSEG, T, D), BF16), jax.ShapeDtypeStruct((T, D), F32),
                   jax.ShapeDtypeStruct((D, D), BF16), jax.ShapeDtypeStruct((3, 8, D), F32)),
        in_specs=[pl.BlockSpec(memory_space=ANY), tile, tile, pl.BlockSpec((tm2, D), lambda i: (i, 2)), row, row,
                  pl.BlockSpec((None, D, D), lambda i: (0, 0, 0))],
        out_specs=(pl.BlockSpec((None, tm2, D), lambda i: (2, i, 0)), tile, pl.BlockSpec((D, D), lambda i: (0, 0)),
                   pl.BlockSpec((3, 8, D), lambda i: (0, 0, 0))),
        scratch_shapes=[VMEM((TM, D), F32)] * nh + [VMEM((TM, D), BF16)] * nh + [VMEM((D, D), F32)],
        input_output_aliases={0: 0},
        compiler_params=_cp(("arbitrary",)),
    )(dp, dya, yc, p, lg, lb, wq)


def _branch_a_bwd2(dp, dyc, p, cw, gw_co, gw_so, gw_o, early):
    T = p.shape[0]
    nt = T // TM
    hb = TM // HALO

    def body(dp_in, dyc_ref, hdyc_ref, val_ref, glu_ref, cw_ref, gco_ref, gso_ref, go_ref, e_ref,
             dp_ref, gcw_ref, rq_ref, eg_ref, a_s, sd_ref, da_s, x_send, x_recv, x_loc, ag_send, ag_recv, ag_loc):
        del dp_in
        i = pl.program_id(0)
        start_exchange, wait_exchange = _owner_exchange((gco_ref, gso_ref, go_ref), rq_ref, x_send, x_recv, x_loc)
        ag_start, ag_relay, ag_finish = _allgather_phases([e_ref], lambda a, slot: eg_ref.at[slot],
                                                          ag_send, ag_recv, ag_loc)

        @pl.when(i == 0)
        def _():
            gcw_ref[...] = jnp.zeros_like(gcw_ref)
            ag_start()
            start_exchange()

        @pl.when(i == nt // 2)
        def _():
            ag_relay()

        def fill(rb, carry):
            rows = _rows(rb)
            a_s[rows, :] = _ld(val_ref, rows) * _sig(_ld(glu_ref, rows))
            return carry
        _blocks(TM // RB, fill)
        sd_ref[0, 0:TM, :] = dyc_ref[...]
        sd_ref[0, TM:EXT, :] = jnp.where(i < nt - 1, hdyc_ref[...], 0.0)
        _shift_copies(sd_ref)

        def conv_t(rb, carry):
            r0 = rb * RBC
            for lanes in LANE_HALVES:
                accs = [jnp.zeros((8, LANE_W), F32)] * (RBC // 8)
                for k in range(KW):
                    o = KW - 1 - k
                    w = cw_ref[k, :, lanes]
                    accs = [acc + w * sd_ref[o % 8, pl.ds(r0 + 8 * (o // 8 + g), 8), lanes]
                            for g, acc in enumerate(accs)]
                for g, acc in enumerate(accs):
                    da_s[pl.ds(r0 + 8 * g, 8), lanes] = acc
            return carry
        for rb in range(TM // RBC):
            conv_t(rb, 0)

        for k0 in range(0, KW, TAPG):
            taps = list(range(k0, min(k0 + TAPG, KW)))

            def tap_group(rb, accs, lanes, taps=taps):
                for u in range(2):
                    r0 = rb * 32 + 16 * u
                    a = a_s[pl.ds(r0, 16), lanes]
                    out = []
                    for k, acc in zip(taps, accs):
                        o = KW - 1 - k
                        d = sd_ref[o % 8, pl.ds(r0 + 8 * (o // 8), 16), lanes]
                        out.append(acc + _fold8(a * d))
                    accs = tuple(out)
                return accs
            for lanes in LANE_HALVES:
                sums = tuple(jnp.zeros((8, LANE_W), F32) for _ in taps)
                for rb in range(TM // 32):
                    sums = tap_group(rb, sums, lanes)
                for k, s in zip(taps, sums):
                    gcw_ref[k, :, lanes] = gcw_ref[k, :, lanes] + s

        def glu_b(rb, carry):
            rows = _rows(rb)
            da = da_s[rows, :]
            sg = _sig(_ld(glu_ref, rows))
            dp_ref[0, rows, :] = (da * sg).astype(BF16)
            dp_ref[1, rows, :] = (da * _ld(val_ref, rows) * (sg * (1.0 - sg))).astype(BF16)
            return carry
        _blocks(TM // RB, glu_b)

        @pl.when(i == nt - 1)
        def _():
            wait_exchange()
            ag_finish()

    tile = lambda seg: pl.BlockSpec((TM, D), lambda i: (i, seg))
    hbm = pl.BlockSpec(memory_space=ANY)
    return pl.pallas_call(
        body, name="branch_a_bwd2", grid=(nt,),
        out_shape=(jax.ShapeDtypeStruct((NSEG, T, D), BF16), jax.ShapeDtypeStruct((32, 8, D), F32),
                   jax.ShapeDtypeStruct((NDEV, 3, 128, D), BF16), jax.ShapeDtypeStruct((NDEV, NEARLY, D), F32)),
        in_specs=[hbm, pl.BlockSpec((TM, D), lambda i: (i, 0)),
                  pl.BlockSpec((HALO, D), lambda i: (jnp.minimum((i + 1) * hb, nt * hb - 1), 0)),
                  tile(0), tile(1), pl.BlockSpec((KW, 8, D), lambda i: (0, 0, 0)),
                  hbm, hbm, hbm, hbm],
        out_specs=(pl.BlockSpec((2, TM, D), lambda i: (0, i, 0)), pl.BlockSpec((32, 8, D), lambda i: (0, 0, 0)), hbm, hbm),
        scratch_shapes=[VMEM((TM, D), F32), VMEM((8, EXT, D), F32), VMEM((TM, D), F32),
                        pltpu.SemaphoreType.DMA((3, 7)), pltpu.SemaphoreType.DMA((3, 7)), pltpu.SemaphoreType.DMA((3,)),
                        pltpu.SemaphoreType.DMA((1, 7)), pltpu.SemaphoreType.DMA((1, 7)), pltpu.SemaphoreType.DMA((1,))],
        input_output_aliases={0: 0},
        compiler_params=_cp(("arbitrary",)),
    )(dp, dyc, dyc, p, p, cw, gw_co, gw_so, gw_o, early)


def _in_bwd_dx(dp, wing, x, dx2, mod, g_pre, token):
    T = x.shape[0]
    nt = T // TM

    def body(dp_ref, w_ref, x_ref, dx2_ref, mod_ref, g_ref, token_ref, gx_ref, st_ref, dh_s):
        del token_ref
        i = pl.program_id(0)

        @pl.when(i == 0)
        def _():
            st_ref[...] = jnp.zeros_like(st_ref)

        dh_s[...] = lax.dot_general(dp_ref[0], w_ref[0], NT, preferred_element_type=F32)
        for j in range(1, NSEG):
            dh_s[...] += lax.dot_general(dp_ref[j], w_ref[j], NT, preferred_element_type=F32)

        def blk(rb, carry):
            d_sh, d_sc, g_g = carry
            rows = _rows(rb)
            xv = x_ref[rows, :]
            r = lax.rsqrt(jnp.mean(xv * xv, axis=-1, keepdims=True) + EPS)
            xn = xv * r
            g = g_ref[...]
            hpre = xn * g
            dh = dh_s[rows, :]
            dhp = dh * (1.0 + mod_ref[1:2, :])
            dxn = dhp * g
            gx_ref[rows, :] = dx2_ref[rows, :] + r * (dxn - xn * jnp.mean(dxn * xn, axis=-1, keepdims=True))
            return d_sh + _fold8(dh), d_sc + _fold8(dh * hpre), g_g + _fold8(dhp * xn)
        zero = jnp.zeros((8, D), F32)
        d_sh, d_sc, g_g = _blocks(TM // RB, blk, (zero, zero, zero))
        st_ref[0] = st_ref[0] + d_sh
        st_ref[1] = st_ref[1] + d_sc
        st_ref[2] = st_ref[2] + g_g

    tile = pl.BlockSpec((TM, D), lambda i: (i, 0))
    return pl.pallas_call(
        body, name="in_bwd_dx", grid=(nt,),
        out_shape=(jax.ShapeDtypeStruct((T, D), F32), jax.ShapeDtypeStruct((3, 8, D), F32)),
        in_specs=[pl.BlockSpec((NSEG, TM, D), lambda i: (0, i, 0)),
                  pl.BlockSpec((NSEG, D, D), lambda i: (0, 0, 0), pipeline_mode=pl.Buffered(1)),
                  tile, tile, pl.BlockSpec((3, D), lambda i: (0, 0)), pl.BlockSpec((1, D), lambda i: (0, 0)),
                  pl.BlockSpec((8, 128), lambda i: (0, 0))],
        out_specs=(tile, pl.BlockSpec((3, 8, D), lambda i: (0, 0, 0))),
        scratch_shapes=[VMEM((TM, D), F32)],
        compiler_params=_cp(("arbitrary",), vmem_mb=56),
    )(dp, wing, x, dx2, mod, g_pre, token)


RS_ORDER = (5, 4, 3, 2, 7, 6, 1, 0)
RS_HALF = D // 2


def _in_bwd_dw(ht, dp, order):
    T = ht.shape[1]
    tmw = min(4 * TMB, T)
    nb = T // tmw
    nu = 2 * NSEG

    def body(order_ref, ht_ref, dp_ref, csum_ref, g_ref, acc, sbuf, rsib, send_sems, recv_sems, out_sems):
        j = pl.program_id(0)
        i = pl.program_id(1)
        x, y, c = _me()
        sibling = (x, y, 1 - c)

        def to_sibling(s):
            return pltpu.make_async_remote_copy(src_ref=sbuf.at[s], dst_ref=rsib.at[s], send_sem=send_sems.at[s],
                                                recv_sem=recv_sems.at[s], device_id=sibling, device_id_type=MESH)

        def to_csum(t):
            cols = pl.ds((t % 2) * RS_HALF, RS_HALF)
            return pltpu.make_async_copy(sbuf.at[8 + t], csum_ref.at[t // 2, :, cols], out_sems.at[t])

        @pl.when(i == 0)
        def _():
            acc[...] = jnp.zeros_like(acc)
        acc[...] += jnp.dot(ht_ref[...], dp_ref[...], preferred_element_type=F32)

        for jj in range(nu):
            chip, half, mine = jj // 4, (jj % 4) // 2, jj % 2
            s = 2 * chip + half

            @pl.when((j == jj) & (i == nb - 1))
            def _(chip=chip, half=half, mine=mine, s=s):
                if not mine:
                    sbuf[s] = acc[...].astype(BF16)
                    to_sibling(s).start()
                elif chip < 3:
                    to_sibling(s).wait_recv()
                    sbuf[8 + s] = (acc[...] + rsib[s].astype(F32)).astype(BF16)
                    to_csum(s).start()
                else:
                    to_sibling(s).wait_recv()
                    g_ref[:, half * RS_HALF:(half + 1) * RS_HALF] = acc[...] + rsib[s].astype(F32)

        @pl.when((j == nu - 1) & (i == nb - 1))
        def _():
            for s in range(8):
                to_sibling(s).wait_send()
            for t in range(6):
                to_csum(t).wait()

    hbm = pl.BlockSpec(memory_space=ANY)
    return pl.pallas_call(
        body, name="in_bwd_dw",
        out_shape=(jax.ShapeDtypeStruct((3, D, D), BF16), jax.ShapeDtypeStruct((D, D), F32)),
        grid_spec=pltpu.PrefetchScalarGridSpec(
            num_scalar_prefetch=1, grid=(nu, nb),
            in_specs=[pl.BlockSpec((D, tmw), lambda j, i, o: (0, i)),
                      pl.BlockSpec((None, tmw, RS_HALF), lambda j, i, o: (o[2 * (j // 4) + j % 2], i, (j % 4) // 2))],
            out_specs=(hbm, pl.BlockSpec((D, D), lambda j, i, o: (0, 0))),
            scratch_shapes=[VMEM((D, RS_HALF), F32), VMEM((14, D, RS_HALF), BF16), VMEM((8, D, RS_HALF), BF16),
                            pltpu.SemaphoreType.DMA((8,)), pltpu.SemaphoreType.DMA((8,)), pltpu.SemaphoreType.DMA((6,))]),
        compiler_params=_cp(("arbitrary", "arbitrary"), vmem_mb=56),
    )(order, ht, dp)


def _gather_late(late):
    def body(sm_ref, smg_ref, ag_send, ag_recv, ag_loc):
        _allgather([sm_ref], lambda a, slot: smg_ref.at[slot], ag_send, ag_recv, ag_loc)

    hbm = pl.BlockSpec(memory_space=ANY)
    return pl.pallas_call(
        body, name="gather_late",
        out_shape=jax.ShapeDtypeStruct((NDEV, NLATE, D), F32),
        in_specs=[hbm], out_specs=hbm,
        scratch_shapes=[pltpu.SemaphoreType.DMA((1, 7)), pltpu.SemaphoreType.DMA((1, 7)), pltpu.SemaphoreType.DMA((1,))],
        compiler_params=_cp(vmem_mb=32),
    )(late)


RS_CHIPS = ((1, 0), (0, 1), (1, 1))


def _rs_peer(q):
    x, y, c = _me()
    fx, fy = RS_CHIPS[q]
    return (1 - x if fx else x, 1 - y if fy else y, c)


def _rs_start(csum):
    def body(csum_ref, land_ref, send_sems, recv_sems, csum_thru, land_thru, token):
        barrier = pltpu.get_barrier_semaphore()
        for q in range(3):
            pl.semaphore_signal(barrier, inc=1, device_id=_rs_peer(q), device_id_type=MESH)
        pl.semaphore_wait(barrier, 3)
        for q in range(3):
            pltpu.make_async_remote_copy(src_ref=csum_ref.at[q], dst_ref=land_ref.at[q], send_sem=send_sems.at[q],
                                         recv_sem=recv_sems.at[q], device_id=_rs_peer(q), device_id_type=MESH).start()
        token[...] = jnp.zeros_like(token)

    hbm = pl.BlockSpec(memory_space=pltpu.HBM)
    sem = pl.BlockSpec(memory_space=pltpu.SEMAPHORE)
    land = pltpu.with_memory_space_constraint(lax.empty(csum.shape, csum.dtype), pltpu.HBM)
    return pl.pallas_call(
        body, name="rs_start",
        out_shape=(pltpu.SemaphoreType.DMA((3,)), pltpu.SemaphoreType.DMA((3,)), pltpu.HBM(csum.shape, csum.dtype),
                   pltpu.HBM(csum.shape, csum.dtype), jax.ShapeDtypeStruct((8, 128), F32)),
        in_specs=(hbm, hbm), out_specs=(sem, sem, hbm, hbm, pl.BlockSpec(memory_space=VMEM)),
        input_output_aliases={0: 2, 1: 3},
        compiler_params=pltpu.CompilerParams(has_side_effects=pltpu.SideEffectType.DATAFLOW_SIDE_EFFECTING, collective_id=1),
    )(pltpu.with_memory_space_constraint(csum, pltpu.HBM), land)


def _rs_wait(send_sems, recv_sems, csum_thru, land_thru, after):
    def body(csum_ref, land_ref, send_sems, recv_sems, after_ref, csum_dead, got_ref):
        del after_ref, csum_dead, got_ref
        for q in range(3):
            cp = pltpu.make_async_remote_copy(src_ref=csum_ref.at[q], dst_ref=land_ref.at[q], send_sem=send_sems.at[q],
                                              recv_sem=recv_sems.at[q], device_id=_rs_peer(q), device_id_type=MESH)
            cp.wait_send()
            cp.wait_recv()

    hbm = pl.BlockSpec(memory_space=pltpu.HBM)
    sem = pl.BlockSpec(memory_space=pltpu.SEMAPHORE)
    return pl.pallas_call(
        body, name="rs_wait",
        out_shape=(pltpu.HBM(csum_thru.shape, csum_thru.dtype), pltpu.HBM(csum_thru.shape, csum_thru.dtype)),
        in_specs=(hbm, hbm, sem, sem, pl.BlockSpec(memory_space=ANY)), out_specs=(hbm, hbm),
        input_output_aliases={0: 0, 1: 1},
        compiler_params=pltpu.CompilerParams(has_side_effects=pltpu.SideEffectType.DATAFLOW_SIDE_EFFECTING),
    )(csum_thru, land_thru, send_sems, recv_sems, after)[1]


def _pack_early(st_a, st_b, st_m, gws, gbs):
    def body(sa_ref, sb_ref, sm_ref, gws_ref, gbs_ref, out_ref):
        out_ref[...] = jnp.zeros_like(out_ref)
        fold = lambda v: jnp.sum(v, axis=0, keepdims=True)
        out_ref[0:1, :] = fold(sm_ref[2])
        out_ref[1:2, :] = fold(sa_ref[2])
        out_ref[2:3, :] = fold(sa_ref[0])
        out_ref[3:4, :] = fold(sa_ref[1])
        out_ref[4:5, :] = fold(sb_ref[0])
        out_ref[5:6, :] = fold(sb_ref[1])
        out_ref[7:8, :] = fold(sm_ref[1])
        out_ref[8:9, :] = fold(sm_ref[0])
        tri = lax.broadcasted_iota(jnp.int32, (CHUNK, CHUNK), 0) >= lax.broadcasted_iota(jnp.int32, (CHUNK, CHUNK), 1)
        for h in range(HEADS):
            out_ref[16:16 + CHUNK, h * CHUNK:(h + 1) * CHUNK] = jnp.where(tri, gws_ref[h], 0.0)
            out_ref[6:7, h * CHUNK:(h + 1) * CHUNK] = fold(gbs_ref[h].T)

    vm = pl.BlockSpec(memory_space=VMEM)
    return pl.pallas_call(
        body, name="pack_early", out_shape=jax.ShapeDtypeStruct((NEARLY, D), F32),
        in_specs=[vm] * 5, out_specs=vm, compiler_params=_cp(vmem_mb=32),
    )(st_a, st_b, st_m, gws, gbs)


def _pack_late(st_x, gcw):
    def body(sx_ref, gcw_ref, out_ref):
        out_ref[...] = jnp.zeros_like(out_ref)
        fold = lambda v: jnp.sum(v, axis=0, keepdims=True)
        for r in range(3):
            out_ref[r:r + 1, :] = fold(sx_ref[r])
        for k in range(KW):
            out_ref[8 + k:9 + k, :] = fold(gcw_ref[k])

    vm = pl.BlockSpec(memory_space=VMEM)
    return pl.pallas_call(
        body, name="pack_late", out_shape=jax.ShapeDtypeStruct((NLATE, D), F32),
        in_specs=[vm] * 2, out_specs=vm, compiler_params=_cp(vmem_mb=32),
    )(st_x, gcw)


def _pack_params(name, b_ada, g_pre, conv_b, lg, lb, sg, sb, b_sgu, g_final, w_sgu_t):
    def body(ba_ref, *refs):
        rows, ws_ref, out_ref = refs[:8], refs[8], refs[9]
        out_ref[...] = jnp.zeros_like(out_ref)
        for r in range(3):
            out_ref[r:r + 1, :] = ba_ref[:, r * D:(r + 1) * D]
        for r, ref in enumerate(rows):
            out_ref[3 + r:4 + r, :] = ref[...]
        out_ref[ROW_WS:ROW_WS + CHUNK, :] = ws_ref[...]

    vm = pl.BlockSpec(memory_space=VMEM)
    return pl.pallas_call(
        body, name=name, out_shape=jax.ShapeDtypeStruct((NSMALL, D), F32),
        in_specs=[vm] * 10, out_specs=vm, compiler_params=_cp(vmem_mb=32),
    )(b_ada, g_pre, conv_b, lg, lb, sg, sb, b_sgu, g_final, w_sgu_t)


def _adam(w, g, m, v):
    m2 = ADAM_B1 * m + (1.0 - ADAM_B1) * g
    v2 = ADAM_B2 * v + (1.0 - ADAM_B2) * (g * g)
    m_hat = m2 / (1.0 - ADAM_B1 ** ADAM_STEP)
    v_hat = v2 / (1.0 - ADAM_B2 ** ADAM_STEP)
    delta = -ADAM_LR * (m_hat / (jnp.sqrt(v_hat) + ADAM_EPS) + ADAM_WD * w)
    return delta, m2, v2


def _small_finish(eg, lg, w, m, v):
    def body(eg_ref, lg_ref, w_ref, m_ref, v_ref, loss_ref, g_ref, d_ref, m2_ref, v2_ref):
        e = eg_ref[0]
        l = lg_ref[0]
        for s in range(1, NDEV):
            e = e + eg_ref[s]
            l = l + lg_ref[s]
        g_ref[...] = jnp.zeros_like(g_ref)
        g_ref[0:2, :] = l[0:2, :]
        g_ref[2:3, :] = e[0:1, :]
        g_ref[3:4, :] = l[2:3, :]
        g_ref[4:12, :] = e[1:9, :]
        g_ref[ROW_CW:ROW_CW + KW, :] = l[8:8 + KW, :]
        g_ref[ROW_WS:ROW_WS + CHUNK, :] = e[16:16 + CHUNK, :]
        loss_ref[...] = jnp.sum(e[8:9, :], axis=1, keepdims=True)
        d_ref[...], m2_ref[...], v2_ref[...] = _adam(w_ref[...], g_ref[...], m_ref[...], v_ref[...])

    vm = pl.BlockSpec(memory_space=VMEM)
    sd = jax.ShapeDtypeStruct((NSMALL, D), F32)
    return pl.pallas_call(
        body, name="small_finish", out_shape=(jax.ShapeDtypeStruct((1, 1), F32), sd, sd, sd, sd),
        in_specs=[vm] * 5, out_specs=(vm,) * 5, compiler_params=_cp(vmem_mb=32),
    )(eg, lg, w, m, v)


def _ada_grad_adam(ct, dm, w, m, v):
    def body(ct_ref, dm_ref, w_ref, m_ref, v_ref, g_ref, d_ref, m2_ref, v2_ref):
        g = ct_ref[:, 0:1] * dm_ref[0:1, :]
        for b in range(1, NDEV):
            g = g + ct_ref[:, b:b + 1] * dm_ref[b:b + 1, :]
        g_ref[...] = g
        d_ref[...], m2_ref[...], v2_ref[...] = _adam(w_ref[...], g, m_ref[...], v_ref[...])

    vm = pl.BlockSpec(memory_space=VMEM)
    sd = jax.ShapeDtypeStruct(w.shape, F32)
    return pl.pallas_call(
        body, name="ada_grad_adam", out_shape=(sd, sd, sd, sd),
        in_specs=[vm] * 5, out_specs=(vm,) * 4, compiler_params=_cp(vmem_mb=32),
    )(ct, dm, w, m, v)


def _adam_f32(name, g, w, m, v, rows=None):
    R, C = w.shape
    rows = R if rows is None else rows

    def body(g_ref, w_ref, m_ref, v_ref, d_ref, m2_ref, v2_ref):
        d_ref[...], m2_ref[...], v2_ref[...] = _adam(w_ref[...], g_ref[...], m_ref[...], v_ref[...])

    tile = pl.BlockSpec((rows, C), lambda i: (i, 0))
    sd = jax.ShapeDtypeStruct(w.shape, F32)
    return pl.pallas_call(
        body, name=name, grid=(R // rows,), out_shape=(sd, sd, sd), in_specs=[tile] * 4, out_specs=(tile,) * 3,
        compiler_params=_cp(("parallel",), vmem_mb=32),
    )(g, w, m, v)


def _adam_w_in(gown, got, w, m, v, rows=256):
    R, C = w.shape

    def body(go_ref, got_ref, w_ref, m_ref, v_ref, g_ref, d_ref, m2_ref, v2_ref):
        g = go_ref[...]
        for q in range(3):
            g = g + got_ref[q].astype(F32)
        g_ref[...] = g
        d_ref[...], m2_ref[...], v2_ref[...] = _adam(w_ref[...], g, m_ref[...], v_ref[...])

    tile = pl.BlockSpec((rows, C), lambda i: (i, 0))
    sd = jax.ShapeDtypeStruct((R, C), F32)
    return pl.pallas_call(
        body, name="adam_w_in", grid=(R // rows,), out_shape=(sd, sd, sd, sd),
        in_specs=[tile, pl.BlockSpec((3, rows, C), lambda i: (0, i, 0)), tile, tile, tile], out_specs=(tile,) * 4,
        compiler_params=_cp(("parallel",), vmem_mb=32),
    )(gown, got, w, m, v)


def _adam_reduce(name, recv, w, m, v, rows, recv_spec):
    R, C = w.shape

    def body(r_ref, w_ref, m_ref, v_ref, g_ref, d_ref, m2_ref, v2_ref):
        g = r_ref[0].astype(F32)
        for s in range(1, NDEV):
            g = g + r_ref[s].astype(F32)
        g_ref[...] = g
        d_ref[...], m2_ref[...], v2_ref[...] = _adam(w_ref[...], g, m_ref[...], v_ref[...])

    tile = pl.BlockSpec((rows, C), lambda i: (i, 0))
    sd = jax.ShapeDtypeStruct((R, C), F32)
    return pl.pallas_call(
        body, name=name, grid=(R // rows,), out_shape=(sd, sd, sd, sd),
        in_specs=[recv_spec, tile, tile, tile], out_specs=(tile,) * 4,
        compiler_params=_cp(("parallel",)),
    )(recv, w, m, v)


def kernel(x, c, w_ada, b_ada, g_pre, w_in, conv_w, conv_b, conv_ln_g, conv_ln_b, w_conv_out, sgu_ln_g, sgu_ln_b, w_sgu, b_sgu, w_sgu_out, w_o, g_final, loss_target, m_w_ada, m_b_ada, m_g_pre, m_w_in, m_conv_w, m_conv_b, m_conv_ln_g, m_conv_ln_b, m_w_conv_out, m_sgu_ln_g, m_sgu_ln_b, m_w_sgu, m_b_sgu, m_w_sgu_out, m_w_o, m_g_final, v_w_ada, v_b_ada, v_g_pre, v_w_in, v_conv_w, v_conv_b, v_conv_ln_g, v_conv_ln_b, v_w_conv_out, v_sgu_ln_g, v_sgu_ln_b, v_w_sgu, v_b_sgu, v_w_sgu_out, v_w_o, v_g_final):
    T = x.shape[1]
    assert T % TMB == 0 and x.shape[2] == D
    xs = x[0]
    tgt = loss_target[0]
    my = 4 * lax.axis_index("x") + 2 * lax.axis_index("y") + lax.axis_index("c")

    mod, cg = _ada_fwd(c, w_ada[0], b_ada)
    gf = g_final.reshape(1, D)
    bst = b_sgu[0].T

    h, ht = _prep_h(xs, mod, g_pre)
    ag_order = jnp.stack([_slot(*b) for b in _ag_blocks(lax.axis_index("x"), lax.axis_index("y"), lax.axis_index("c"))])
    p, wing, wcog, cwg, own2, land2 = _in_proj(h, w_in[0], ag_order.astype(jnp.int32),
                                               w_conv_out[0], w_sgu_out[0], w_o[0], conv_w[0])
    wco = wcog.reshape(1, D, D)
    cw = jnp.broadcast_to(jnp.transpose(cwg, (1, 0, 2)).reshape(KW, 1, D), (KW, 8, D))
    wq_send, wq_recv, own2_thru, land2_thru, wq_token = _wq_start(own2, land2)
    yc, ya = _branch_a_fwd(p, cw, conv_b, conv_ln_g, conv_ln_b, wco, wq_token)
    wq2 = _wq_wait(wq_send, wq_recv, own2_thru, land2_thru, yc).reshape(2, D, D)
    yb = _branch_b_fwd(p, w_sgu[0], bst, sgu_ln_g, sgu_ln_b, wq2)
    dx2, dya, dyb, dp, gw_o, st_m = _merge_loss(p, ya, yb, xs, tgt, mod, gf, wq2)

    dp, gw_so, gws, gbs, st_b = _branch_b_bwd(dp, dyb, p, w_sgu[0], bst, sgu_ln_g, sgu_ln_b, wq2)
    dp, dyc, gw_co, st_a = _branch_a_bwd1(dp, dya, yc, p, conv_ln_g, conv_ln_b, wco)
    early = _pack_early(st_a, st_b, st_m, gws, gbs)
    dp, gcw, rq, eg = _branch_a_bwd2(dp, dyc, p, cw, gw_co, gw_so, gw_o, early)
    csum, gown = _in_bwd_dw(ht, dp, my ^ jnp.array(RS_ORDER, jnp.int32))
    rs_send, rs_recv, csum_thru, land_thru, token = _rs_start(csum)
    grad_x, st_x = _in_bwd_dx(dp, wing, xs, dx2, mod, g_pre, token)
    got = _rs_wait(rs_send, rs_recv, csum_thru, land_thru, st_x)
    lg = _gather_late(_pack_late(st_x, gcw))

    def pack(name, b_ada_, g_pre_, conv_b_, lg_, lb_, sg_, sb_, b_sgu_, gfin_, w_sgu_):
        return _pack_params(name, b_ada_, g_pre_, conv_b_, lg_, lb_, sg_, sb_, b_sgu_.reshape(1, D), gfin_.reshape(1, D),
                            jnp.transpose(w_sgu_[0], (1, 0, 2)).reshape(CHUNK, D))

    pw = pack("pack_w", b_ada, g_pre, conv_b, conv_ln_g, conv_ln_b, sgu_ln_g, sgu_ln_b, b_sgu, g_final, w_sgu)
    pm = pack("pack_m", m_b_ada, m_g_pre, m_conv_b, m_conv_ln_g, m_conv_ln_b, m_sgu_ln_g, m_sgu_ln_b, m_b_sgu, m_g_final,
              m_w_sgu)
    pv = pack("pack_v", v_b_ada, v_g_pre, v_conv_b, v_conv_ln_g, v_conv_ln_b, v_sgu_ln_g, v_sgu_ln_b, v_b_sgu, v_g_final,
              v_w_sgu)
    loss11, sg_, sd_, sm_, sv_ = _small_finish(eg, lg, pw, pm, pv)

    def unpack(a):
        return dict(
            b_ada=a[0:3].reshape(1, 3 * D), g_pre=a[3:4], conv_b=a[4:5], conv_ln_g=a[5:6], conv_ln_b=a[6:7],
            sgu_ln_g=a[7:8], sgu_ln_b=a[8:9], b_sgu=a[9:10].reshape(1, HEADS, CHUNK), g_final=a[10],
            w_sgu=jnp.transpose(a[ROW_WS:ROW_WS + CHUNK].reshape(CHUNK, HEADS, CHUNK), (1, 0, 2))[None])
    small = [unpack(a) for a in (sg_, sd_, sm_, sv_)]

    g_cw = lax.dynamic_slice_in_dim(sg_[ROW_CW:ROW_CW + KW], my * 128, 128, axis=1)
    d_cw, m_cw, v_cw = _adam_f32("adam_conv_w", g_cw, conv_w[0], m_conv_w[0], v_conv_w[0])

    wcols = w_ada.shape[2]
    dm_all = jnp.concatenate([lg[:, 0], lg[:, 1], eg[:, 0]], axis=1)
    dm_mine = lax.dynamic_slice_in_dim(dm_all, my * wcols, wcols, axis=1)
    g_ada, d_ada, m_ada, v_ada = _ada_grad_adam(cg.T, dm_mine, w_ada[0], m_w_ada[0], v_w_ada[0])

    g_in, d_in, m_in, v_in = _adam_w_in(gown, got, w_in[0], m_w_in[0], v_w_in[0])
    big = {}
    for j, (nm, w_, m_, v_) in enumerate((("w_conv_out", w_conv_out, m_w_conv_out, v_w_conv_out),
                                          ("w_sgu_out", w_sgu_out, m_w_sgu_out, v_w_sgu_out),
                                          ("w_o", w_o, m_w_o, v_w_o))):
        big[nm] = _adam_reduce("adam_" + nm, rq, w_[0], m_[0], v_[0], 128,
                               pl.BlockSpec((NDEV, None, 128, D), lambda i, j=j: (0, j, 0, 0)))

    per = {
        "w_ada": tuple(a[None] for a in (g_ada, d_ada, m_ada, v_ada)),
        "w_in": tuple(a[None] for a in (g_in, d_in, m_in, v_in)),
        "conv_w": tuple(a[None] for a in (g_cw, d_cw, m_cw, v_cw)),
    }
    for nm in ("w_conv_out", "w_sgu_out", "w_o"):
        per[nm] = tuple(a[None] for a in big[nm])
    for nm in ("b_ada", "g_pre", "conv_b", "conv_ln_g", "conv_ln_b", "sgu_ln_g", "sgu_ln_b", "w_sgu", "b_sgu", "g_final"):
        per[nm] = tuple(s[nm] for s in small)

    order = ["w_ada", "b_ada", "g_pre", "w_in", "conv_w", "conv_b", "conv_ln_g", "conv_ln_b", "w_conv_out",
             "sgu_ln_g", "sgu_ln_b", "w_sgu", "b_sgu", "w_sgu_out", "w_o", "g_final"]
    outs = [loss11.reshape(()), grad_x[None]]
    for part in range(4):
        outs += [per[nm][part] for nm in order]
    return tuple(outs)
```
